```python
import math
import jax
import jax.numpy as jnp
from jax import lax
import numpy as np

D_MODEL = 1024
BATCH = 8
SEQ = 4096
DEPTH = 2

N_META = 16
D_FF = 4 * D_MODEL
GDN_HEADS = 4
GDN_DK = 128
GDN_DV = 128
GDN_CONV = 4
GDN_CHUNK = 64
SB_HEADS = 8
SB_DH = 64
SB_BLOCK = 128
HG_HEADS = 8
HG_DK = D_MODEL // HG_HEADS
HG_DV = D_MODEL // HG_HEADS
HG_CHUNK = 64

N_EVEN = (DEPTH + 1) // 2
N_ODD = DEPTH // 2
DN_ALPHA = float((2 * DEPTH) ** 0.25)
DN_BETA = float((8 * DEPTH) ** -0.25)
LN_EPS = 1e-5
RMS_EPS = 1e-6
L2_EPS = 1e-6

GDN_QK_W = GDN_HEADS * GDN_DK
GDN_V_W = GDN_HEADS * GDN_DV
SB_W = SB_HEADS * SB_DH
AB_SPLITS = (GDN_QK_W, GDN_QK_W, GDN_V_W, GDN_V_W, GDN_HEADS, GDN_HEADS, SB_W, SB_W, SB_W)
AB_IN = sum(AB_SPLITS)
AB_OUT_IN = GDN_V_W + SB_W
HG_K_W = HG_HEADS * HG_DK
HG_V_W = HG_HEADS * HG_DV
C_SPLITS = (HG_K_W, HG_K_W, HG_V_W, HG_V_W)
C_IN = sum(C_SPLITS)

kernel_name = "hybrid_gdn_stickbreak_hgrn2_deepnorm"


def _split(a, sizes):
    idx = np.cumsum(sizes)[:-1].tolist()
    return jnp.split(a, idx, axis=-1)


def _front_pad(a, pad):
    return jnp.pad(a, [(0, 0), (pad, 0)] + [(0, 0)] * (a.ndim - 2))


def _to_chunks(a, n, c):
    bn, _, h, d = a.shape
    return jnp.transpose(a.reshape(bn, n, c, h, d), (1, 0, 3, 2, 4))


def _from_chunks(o):
    n, bn, h, c, d = o.shape
    return jnp.transpose(o, (1, 0, 3, 2, 4)).reshape(bn, n * c, h, d)


def layer_norm(x, g, b):
    xf = x.astype(jnp.float32)
    mu = jnp.mean(xf, axis=-1, keepdims=True)
    var = jnp.mean(jnp.square(xf - mu), axis=-1, keepdims=True)
    y = (xf - mu) * lax.rsqrt(var + LN_EPS)
    return (y * g.astype(jnp.float32) + b.astype(jnp.float32)).astype(x.dtype)


def rms_norm(x, g):
    xf = x.astype(jnp.float32)
    y = xf * lax.rsqrt(jnp.mean(xf * xf, axis=-1, keepdims=True) + RMS_EPS)
    return y * g.astype(jnp.float32)


def l2_normalize(x):
    xf = x.astype(jnp.float32)
    return xf * lax.rsqrt(jnp.sum(xf * xf, axis=-1, keepdims=True) + L2_EPS)


def causal_depthwise_conv(x, w):
    k_w, ch = w.shape
    return lax.conv_general_dilated(
        x, w[:, None, :].astype(x.dtype), window_strides=(1,), padding=[(k_w - 1, 0)],
        dimension_numbers=("NWC", "WIO", "NWC"), feature_group_count=ch)


def gated_delta_rule_chunked(q, k, v, g, beta):
    bn, t_len, h, dk = q.shape
    dv = v.shape[-1]
    c = GDN_CHUNK
    n = t_len // c
    q, k, v = (_to_chunks(a, n, c) for a in (q, k, v))
    g, beta = (_to_chunks(a[..., None], n, c)[..., 0] for a in (g, beta))
    gc = jnp.cumsum(g, axis=-1)
    causal = jnp.tril(jnp.ones((c, c), dtype=bool))
    strict = jnp.tril(jnp.ones((c, c), dtype=bool), -1)
    decay = jnp.exp(jnp.where(causal, gc[..., :, None] - gc[..., None, :], -jnp.inf))
    kb = k * beta[..., None]
    m = jnp.where(strict, jnp.einsum("nbhid,nbhjd->nbhij", kb, k) * decay, 0.0)
    eye = jnp.broadcast_to(jnp.eye(c, dtype=jnp.float32), m.shape)
    t_inv = lax.linalg.triangular_solve(eye + m, eye, left_side=True, lower=True, unit_diagonal=True)
    u = jnp.einsum("nbhij,nbhjd->nbhid", t_inv, v * beta[..., None])
    w = jnp.einsum("nbhij,nbhjd->nbhid", t_inv, kb * jnp.exp(gc)[..., None])
    a_intra = jnp.einsum("nbhid,nbhjd->nbhij", q, k) * decay
    q_dec = q * jnp.exp(gc)[..., None]
    k_dec = k * jnp.exp(gc[..., -1:] - gc)[..., None]
    g_tot = jnp.exp(gc[..., -1])

    def step(s, xs):
        u_n, w_n, qd_n, kd_n, a_n, gt_n = xs
        v_new = u_n - jnp.einsum("bhcd,bhde->bhce", w_n, s)
        o_n = jnp.einsum("bhcd,bhde->bhce", qd_n, s) + jnp.einsum("bhij,bhje->bhie", a_n, v_new)
        s = s * gt_n[..., None, None] + jnp.einsum("bhcd,bhce->bhde", kd_n, v_new)
        return s, o_n

    s0 = jnp.zeros((bn, h, dk, dv), jnp.float32)
    _, o = lax.scan(step, s0, (u, w, q_dec, k_dec, a_intra, g_tot))
    return _from_chunks(o)


def stick_breaking_attention(q, k, v):
    bn, l_len, h, d = q.shape
    pad = (-l_len) % SB_BLOCK
    t_len = l_len + pad
    nb = t_len // SB_BLOCK
    to_bhtd = lambda a: jnp.moveaxis(_front_pad(a.astype(jnp.float32), pad), 1, 2)
    q, k, v = (to_bhtd(a) for a in (q, k, v))
    q = q * (d ** -0.5)
    key_pos = jnp.arange(t_len)
    q_blocks = jnp.moveaxis(q.reshape(bn, h, nb, SB_BLOCK, d), 2, 0)
    q_pos = jnp.arange(t_len).reshape(nb, SB_BLOCK)

    def block(args):
        qb, qp = args
        z = jnp.einsum("bhqd,bhkd->bhqk", qb, k)
        allowed = (key_pos[None, :] < qp[:, None]) & (key_pos[None, :] >= pad)
        log_1m = jnp.where(allowed, jax.nn.log_sigmoid(-z), 0.0)
        after = lax.cumsum(log_1m, axis=3, reverse=True) - log_1m
        wgt = jnp.where(allowed, jnp.exp(jax.nn.log_sigmoid(z) + after), 0.0)
        return jnp.einsum("bhqk,bhkd->bhqd", wgt, v)

    o = lax.map(block, (q_blocks, q_pos))
    o = jnp.moveaxis(o, 0, 2).reshape(bn, h, t_len, d)[:, :, pad:]
    return jnp.moveaxis(o, 1, 2).reshape(bn, l_len, h * d)


def hgrn2_chunked(q, k, v, logf):
    bn, t_len, h, dk = q.shape
    dv = v.shape[-1]
    c = HG_CHUNK
    n = t_len // c
    q, k, v, logf = (_to_chunks(a, n, c) for a in (q, k, v, logf))
    b = jnp.cumsum(logf, axis=3)
    causal = jnp.tril(jnp.ones((c, c), dtype=bool))[:, :, None]

    def step(s, xs):
        q_n, k_n, v_n, b_n = xs
        pair = jnp.exp(jnp.where(causal, b_n[:, :, :, None, :] - b_n[:, :, None, :, :], -jnp.inf))
        a_n = jnp.einsum("bhtc,bhsc,bhtsc->bhts", q_n, k_n, pair)
        b_last = b_n[:, :, -1:, :]
        o_n = (jnp.einsum("bhtc,bhce->bhte", q_n * jnp.exp(b_n), s)
               + jnp.einsum("bhts,bhse->bhte", a_n, v_n))
        s = (jnp.exp(b_last)[:, :, 0, :, None] * s
             + jnp.einsum("bhsc,bhse->bhce", k_n * jnp.exp(b_last - b_n), v_n))
        return s, o_n

    s0 = jnp.zeros((bn, h, dk, dv), jnp.float32)
    _, o = lax.scan(step, s0, (q, k, v, b))
    return _from_chunks(o)


def mixer_gdn_sb(h, w_in, conv_w, a_log, dt_bias, gnorm_g, w_out):
    bn, l_len, _ = h.shape
    qa, ka, va, za, ba, aa, qb, kb, vb = _split(h @ w_in, AB_SPLITS)
    qkv = jax.nn.silu(causal_depthwise_conv(jnp.concatenate([qa, ka, va], axis=-1), conv_w))
    qa, ka, va = _split(qkv, (GDN_QK_W, GDN_QK_W, GDN_V_W))
    qa = l2_normalize(qa.reshape(bn, l_len, GDN_HEADS, GDN_DK)) * (GDN_DK ** -0.5)
    ka = l2_normalize(ka.reshape(bn, l_len, GDN_HEADS, GDN_DK))
    va = va.reshape(bn, l_len, GDN_HEADS, GDN_DV).astype(jnp.float32)
    beta = jax.nn.sigmoid(ba.astype(jnp.float32))
    g = -jnp.exp(a_log.astype(jnp.float32)) * jax.nn.softplus(aa.astype(jnp.float32) + dt_bias.astype(jnp.float32))
    pad = (-l_len) % GDN_CHUNK
    oa = gated_delta_rule_chunked(*(_front_pad(a, pad) for a in (qa, ka, va, g, beta)))[:, pad:]
    oa = rms_norm(oa, gnorm_g) * jax.nn.silu(za.reshape(bn, l_len, GDN_HEADS, GDN_DV).astype(jnp.float32))
    oa = oa.reshape(bn, l_len, GDN_V_W)
    shp = (bn, l_len, SB_HEADS, SB_DH)
    ob = stick_breaking_attention(qb.reshape(shp), kb.reshape(shp), vb.reshape(shp))
    return jnp.concatenate([oa.astype(h.dtype), ob.astype(h.dtype)], axis=-1) @ w_out


def mixer_hgrn2(h, w_in, lb, gnorm_g, w_out):
    bn, l_len, _ = h.shape
    q, f, i, z = _split(h @ w_in, C_SPLITS)
    shp_k = (bn, l_len, HG_HEADS, HG_DK)
    shp_v = (bn, l_len, HG_HEADS, HG_DV)
    lbf = lb.astype(jnp.float32).reshape(HG_HEADS, HG_DK)
    fgate = lbf + (1.0 - lbf) * jax.nn.sigmoid(f.astype(jnp.float32).reshape(shp_k))
    logf = jnp.log(fgate)
    k = 1.0 - fgate
    q = jax.nn.silu(q.astype(jnp.float32)).reshape(shp_k)
    v = i.astype(jnp.float32).reshape(shp_v)
    pad = (-l_len) % HG_CHUNK
    o = hgrn2_chunked(*(_front_pad(a, pad) for a in (q, k, v, logf)))[:, pad:]
    o = rms_norm(o, gnorm_g) * jax.nn.silu(z.astype(jnp.float32).reshape(shp_v))
    return o.reshape(bn, l_len, HG_V_W).astype(h.dtype) @ w_out


def squared_relu_mlp(h, w1, w2):
    return jnp.square(jax.nn.relu(h @ w1)) @ w2


def _fwd_setup_inputs(seed: int = 0) -> dict:
    key = jax.random.key(seed)
    ks = jax.random.split(key, 18)
    nrm = lambda k, shape, scale: jax.random.normal(k, shape, jnp.float32) * scale
    x = nrm(ks[0], (BATCH, SEQ, D_MODEL), 1.0)
    meta_tokens = nrm(ks[1], (N_META, D_MODEL), 1.0)
    ab_w_in = nrm(ks[2], (N_EVEN, D_MODEL, AB_IN), D_MODEL ** -0.5)
    ab_conv_w = nrm(ks[3], (N_EVEN, GDN_CONV, 2 * GDN_QK_W + GDN_V_W), GDN_CONV ** -0.5)
    ab_a_log = jnp.log(jax.random.uniform(ks[4], (N_EVEN, GDN_HEADS), jnp.float32, 1.0, 16.0))
    dt = jnp.exp(jax.random.uniform(ks[5], (N_EVEN, GDN_HEADS), jnp.float32, math.log(1e-3), math.log(1e-1)))
    ab_dt_bias = dt + jnp.log(-jnp.expm1(-dt))
    ab_gnorm_g = 1.0 + nrm(ks[6], (N_EVEN, GDN_DV), 0.02)
    ab_w_out = nrm(ks[7], (N_EVEN, AB_OUT_IN, D_MODEL), (AB_OUT_IN ** -0.5) * DN_BETA)
    c_w_in = nrm(ks[8], (N_ODD, D_MODEL, C_IN), D_MODEL ** -0.5)
    c_lb_raw = nrm(ks[9], (DEPTH, HG_K_W), 0.1)
    c_gnorm_g = 1.0 + nrm(ks[10], (N_ODD, HG_DV), 0.02)
    c_w_out = nrm(ks[11], (N_ODD, HG_V_W, D_MODEL), (HG_V_W ** -0.5) * DN_BETA)
    ln_mix_g = 1.0 + nrm(ks[12], (DEPTH, D_MODEL), 0.02)
    ln_mix_b = nrm(ks[13], (DEPTH, D_MODEL), 0.02)
    mlp_w1 = nrm(ks[14], (DEPTH, D_MODEL, D_FF), D_MODEL ** -0.5)
    mlp_w2 = nrm(ks[15], (DEPTH, D_FF, D_MODEL), (D_FF ** -0.5) * DN_BETA)
    ln_ffn_g = 1.0 + nrm(ks[16], (DEPTH, D_MODEL), 0.02)
    ln_ffn_b = nrm(ks[17], (DEPTH, D_MODEL), 0.02)
    return {"x": x, "meta_tokens": meta_tokens, "ab_w_in": ab_w_in, "ab_conv_w": ab_conv_w,
            "ab_a_log": ab_a_log, "ab_dt_bias": ab_dt_bias, "ab_gnorm_g": ab_gnorm_g, "ab_w_out": ab_w_out,
            "c_w_in": c_w_in, "c_lb_raw": c_lb_raw, "c_gnorm_g": c_gnorm_g, "c_w_out": c_w_out,
            "ln_mix_g": ln_mix_g, "ln_mix_b": ln_mix_b, "mlp_w1": mlp_w1, "mlp_w2": mlp_w2,
            "ln_ffn_g": ln_ffn_g, "ln_ffn_b": ln_ffn_b}


def _fwd_reference(x, meta_tokens, ab_w_in, ab_conv_w, ab_a_log, ab_dt_bias, ab_gnorm_g, ab_w_out,
              c_w_in, c_lb_raw, c_gnorm_g, c_w_out, ln_mix_g, ln_mix_b, mlp_w1, mlp_w2,
              ln_ffn_g, ln_ffn_b):
    lb_all = jnp.cumsum(jax.nn.softmax(c_lb_raw.astype(jnp.float32), axis=0), axis=0)
    lb_all = lb_all - lb_all[0:1]
    meta = jnp.broadcast_to(meta_tokens[None].astype(x.dtype), (x.shape[0], N_META, D_MODEL))
    h = jnp.concatenate([meta, x], axis=1)
    for layer in range(DEPTH):
        j = layer // 2
        if layer % 2 == 0:
            mix = mixer_gdn_sb(h, ab_w_in[j], ab_conv_w[j], ab_a_log[j], ab_dt_bias[j], ab_gnorm_g[j], ab_w_out[j])
        else:
            mix = mixer_hgrn2(h, c_w_in[j], lb_all[layer], c_gnorm_g[j], c_w_out[j])
        h = layer_norm(DN_ALPHA * h + mix, ln_mix_g[layer], ln_mix_b[layer])
        h = layer_norm(DN_ALPHA * h + squared_relu_mlp(h, mlp_w1[layer], mlp_w2[layer]),
                       ln_ffn_g[layer], ln_ffn_b[layer])
    return h[:, N_META:]


import jax as _jax
import jax.numpy as _jnp

TWIN_FORMAT = 'train_step'
FWD_PARAMS = ['x', 'meta_tokens', 'ab_w_in', 'ab_conv_w', 'ab_a_log', 'ab_dt_bias', 'ab_gnorm_g', 'ab_w_out', 'c_w_in', 'c_lb_raw', 'c_gnorm_g', 'c_w_out', 'ln_mix_g', 'ln_mix_b', 'mlp_w1', 'mlp_w2', 'ln_ffn_g', 'ln_ffn_b']
TWIN_WEIGHTS = ['meta_tokens', 'ab_w_in', 'ab_conv_w', 'ab_a_log', 'ab_dt_bias', 'ab_gnorm_g', 'ab_w_out', 'c_w_in', 'c_lb_raw', 'c_gnorm_g', 'c_w_out', 'ln_mix_g', 'ln_mix_b', 'mlp_w1', 'mlp_w2', 'ln_ffn_g', 'ln_ffn_b']
TWIN_DIFF_INPUT = 'x'
TWIN_INPUTS = ['x', 'meta_tokens', 'ab_w_in', 'ab_conv_w', 'ab_a_log', 'ab_dt_bias', 'ab_gnorm_g', 'ab_w_out', 'c_w_in', 'c_lb_raw', 'c_gnorm_g', 'c_w_out', 'ln_mix_g', 'ln_mix_b', 'mlp_w1', 'mlp_w2', 'ln_ffn_g', 'ln_ffn_b', 'loss_target', 'm_meta_tokens', 'm_ab_w_in', 'm_ab_conv_w', 'm_ab_a_log', 'm_ab_dt_bias', 'm_ab_gnorm_g', 'm_ab_w_out', 'm_c_w_in', 'm_c_lb_raw', 'm_c_gnorm_g', 'm_c_w_out', 'm_ln_mix_g', 'm_ln_mix_b', 'm_mlp_w1', 'm_mlp_w2', 'm_ln_ffn_g', 'm_ln_ffn_b', 'v_meta_tokens', 'v_ab_w_in', 'v_ab_conv_w', 'v_ab_a_log', 'v_ab_dt_bias', 'v_ab_gnorm_g', 'v_ab_w_out', 'v_c_w_in', 'v_c_lb_raw', 'v_c_gnorm_g', 'v_c_w_out', 'v_ln_mix_g', 'v_ln_mix_b', 'v_mlp_w1', 'v_mlp_w2', 'v_ln_ffn_g', 'v_ln_ffn_b']
TWIN_OUTPUTS = ['loss', 'grad_x', 'grad_meta_tokens', 'grad_ab_w_in', 'grad_ab_conv_w', 'grad_ab_a_log', 'grad_ab_dt_bias', 'grad_ab_gnorm_g', 'grad_ab_w_out', 'grad_c_w_in', 'grad_c_lb_raw', 'grad_c_gnorm_g', 'grad_c_w_out', 'grad_ln_mix_g', 'grad_ln_mix_b', 'grad_mlp_w1', 'grad_mlp_w2', 'grad_ln_ffn_g', 'grad_ln_ffn_b', 'delta_meta_tokens', 'delta_ab_w_in', 'delta_ab_conv_w', 'delta_ab_a_log', 'delta_ab_dt_bias', 'delta_ab_gnorm_g', 'delta_ab_w_out', 'delta_c_w_in', 'delta_c_lb_raw', 'delta_c_gnorm_g', 'delta_c_w_out', 'delta_ln_mix_g', 'delta_ln_mix_b', 'delta_mlp_w1', 'delta_mlp_w2', 'delta_ln_ffn_g', 'delta_ln_ffn_b', 'new_m_meta_tokens', 'new_m_ab_w_in', 'new_m_ab_conv_w', 'new_m_ab_a_log', 'new_m_ab_dt_bias', 'new_m_ab_gnorm_g', 'new_m_ab_w_out', 'new_m_c_w_in', 'new_m_c_lb_raw', 'new_m_c_gnorm_g', 'new_m_c_w_out', 'new_m_ln_mix_g', 'new_m_ln_mix_b', 'new_m_mlp_w1', 'new_m_mlp_w2', 'new_m_ln_ffn_g', 'new_m_ln_ffn_b', 'new_v_meta_tokens', 'new_v_ab_w_in', 'new_v_ab_conv_w', 'new_v_ab_a_log', 'new_v_ab_dt_bias', 'new_v_ab_gnorm_g', 'new_v_ab_w_out', 'new_v_c_w_in', 'new_v_c_lb_raw', 'new_v_c_gnorm_g', 'new_v_c_w_out', 'new_v_ln_mix_g', 'new_v_ln_mix_b', 'new_v_mlp_w1', 'new_v_mlp_w2', 'new_v_ln_ffn_g', 'new_v_ln_ffn_b']
TWIN_LEAF_KINDS = {'loss': 'loss', 'grad_x': 'grad_x', 'grad_meta_tokens': 'grad_w', 'grad_ab_w_in': 'grad_w', 'grad_ab_conv_w': 'grad_w', 'grad_ab_a_log': 'grad_w', 'grad_ab_dt_bias': 'grad_w', 'grad_ab_gnorm_g': 'grad_w', 'grad_ab_w_out': 'grad_w', 'grad_c_w_in': 'grad_w', 'grad_c_lb_raw': 'grad_w', 'grad_c_gnorm_g': 'grad_w', 'grad_c_w_out': 'grad_w', 'grad_ln_mix_g': 'grad_w', 'grad_ln_mix_b': 'grad_w', 'grad_mlp_w1': 'grad_w', 'grad_mlp_w2': 'grad_w', 'grad_ln_ffn_g': 'grad_w', 'grad_ln_ffn_b': 'grad_w', 'delta_meta_tokens': 'delta_w', 'delta_ab_w_in': 'delta_w', 'delta_ab_conv_w': 'delta_w', 'delta_ab_a_log': 'delta_w', 'delta_ab_dt_bias': 'delta_w', 'delta_ab_gnorm_g': 'delta_w', 'delta_ab_w_out': 'delta_w', 'delta_c_w_in': 'delta_w', 'delta_c_lb_raw': 'delta_w', 'delta_c_gnorm_g': 'delta_w', 'delta_c_w_out': 'delta_w', 'delta_ln_mix_g': 'delta_w', 'delta_ln_mix_b': 'delta_w', 'delta_mlp_w1': 'delta_w', 'delta_mlp_w2': 'delta_w', 'delta_ln_ffn_g': 'delta_w', 'delta_ln_ffn_b': 'delta_w', 'new_m_meta_tokens': 'new_m', 'new_m_ab_w_in': 'new_m', 'new_m_ab_conv_w': 'new_m', 'new_m_ab_a_log': 'new_m', 'new_m_ab_dt_bias': 'new_m', 'new_m_ab_gnorm_g': 'new_m', 'new_m_ab_w_out': 'new_m', 'new_m_c_w_in': 'new_m', 'new_m_c_lb_raw': 'new_m', 'new_m_c_gnorm_g': 'new_m', 'new_m_c_w_out': 'new_m', 'new_m_ln_mix_g': 'new_m', 'new_m_ln_mix_b': 'new_m', 'new_m_mlp_w1': 'new_m', 'new_m_mlp_w2': 'new_m', 'new_m_ln_ffn_g': 'new_m', 'new_m_ln_ffn_b': 'new_m', 'new_v_meta_tokens': 'new_v', 'new_v_ab_w_in': 'new_v', 'new_v_ab_conv_w': 'new_v', 'new_v_ab_a_log': 'new_v', 'new_v_ab_dt_bias': 'new_v', 'new_v_ab_gnorm_g': 'new_v', 'new_v_ab_w_out': 'new_v', 'new_v_c_w_in': 'new_v', 'new_v_c_lb_raw': 'new_v', 'new_v_c_gnorm_g': 'new_v', 'new_v_c_w_out': 'new_v', 'new_v_ln_mix_g': 'new_v', 'new_v_ln_mix_b': 'new_v', 'new_v_mlp_w1': 'new_v', 'new_v_mlp_w2': 'new_v', 'new_v_ln_ffn_g': 'new_v', 'new_v_ln_ffn_b': 'new_v'}


def _forward(args):
    return _fwd_reference(*[args[k] for k in FWD_PARAMS])


def _output_shape():
    out = _jax.eval_shape(lambda: _forward(_fwd_setup_inputs(0)))
    return out.shape, out.dtype

N_MICROBATCH = 1
ADAM_LR = 0.001
ADAM_B1 = 0.9
ADAM_B2 = 0.999
ADAM_EPS = 1e-08
ADAM_WD = 0.01
ADAM_STEP = 10
PER_EXAMPLE_BATCH_AXIS = {'x': 0, 'loss_target': 0}
SHARED_INPUTS = []
_WEIGHT_DTYPES = {'meta_tokens': _jnp.float32, 'ab_w_in': _jnp.float32, 'ab_conv_w': _jnp.float32, 'ab_a_log': _jnp.float32, 'ab_dt_bias': _jnp.float32, 'ab_gnorm_g': _jnp.float32, 'ab_w_out': _jnp.float32, 'c_w_in': _jnp.float32, 'c_lb_raw': _jnp.float32, 'c_gnorm_g': _jnp.float32, 'c_w_out': _jnp.float32, 'ln_mix_g': _jnp.float32, 'ln_mix_b': _jnp.float32, 'mlp_w1': _jnp.float32, 'mlp_w2': _jnp.float32, 'ln_ffn_g': _jnp.float32, 'ln_ffn_b': _jnp.float32}
MOMENT_SCALE = {'meta_tokens': 1.248824e-03, 'ab_w_in': 3.118205e-02, 'ab_conv_w': 3.019386e-02, 'ab_a_log': 1.627507e-01, 'ab_dt_bias': 1.534142e-01, 'ab_gnorm_g': 8.202142e-02, 'ab_w_out': 8.276610e-02, 'c_w_in': 2.784585e-02, 'c_lb_raw': 3.160158e-03, 'c_gnorm_g': 1.156196e-01, 'c_w_out': 7.813284e-02, 'ln_mix_g': 7.248545e-01, 'ln_mix_b': 4.563655e-01, 'mlp_w1': 4.184097e-02, 'mlp_w2': 1.906929e-01, 'ln_ffn_g': 2.269723e+01, 'ln_ffn_b': 5.063288e+00}


def _to_microbatches(a, axis):
    t = _jnp.moveaxis(a, axis, 0)
    t = t.reshape((N_MICROBATCH, t.shape[0] // N_MICROBATCH) + t.shape[1:])
    return _jnp.moveaxis(t, 1, axis + 1)


def setup_inputs(seed: int = 0) -> dict:
    inp = _fwd_setup_inputs(seed)
    key = _jax.random.fold_in(_jax.random.key(seed), 7919)
    shape, _ = _output_shape()
    out = dict(inp)
    out["loss_target"] = _jax.random.normal(_jax.random.fold_in(key, 0), shape, _jnp.float32)
    for i, name in enumerate(TWIN_WEIGHTS):
        w = inp[name].astype(_jnp.float32)
        if MOMENT_SCALE is None:
            s = _jnp.sqrt(_jnp.mean(_jnp.square(w)) + 1e-30)
        else:
            s = MOMENT_SCALE[name]
        km, kv = _jax.random.split(_jax.random.fold_in(key, i + 1))
        out[name] = w
        out["m_" + name] = s * _jax.random.normal(km, w.shape, _jnp.float32)
        out["v_" + name] = (s * s) * _jax.random.uniform(kv, w.shape, _jnp.float32, 0.5, 1.5)
    if N_MICROBATCH > 1:
        for name, axis in PER_EXAMPLE_BATCH_AXIS.items():
            out[name] = _to_microbatches(out[name], axis)
    return {'x': out['x'], 'meta_tokens': out['meta_tokens'], 'ab_w_in': out['ab_w_in'], 'ab_conv_w': out['ab_conv_w'], 'ab_a_log': out['ab_a_log'], 'ab_dt_bias': out['ab_dt_bias'], 'ab_gnorm_g': out['ab_gnorm_g'], 'ab_w_out': out['ab_w_out'], 'c_w_in': out['c_w_in'], 'c_lb_raw': out['c_lb_raw'], 'c_gnorm_g': out['c_gnorm_g'], 'c_w_out': out['c_w_out'], 'ln_mix_g': out['ln_mix_g'], 'ln_mix_b': out['ln_mix_b'], 'mlp_w1': out['mlp_w1'], 'mlp_w2': out['mlp_w2'], 'ln_ffn_g': out['ln_ffn_g'], 'ln_ffn_b': out['ln_ffn_b'], 'loss_target': out['loss_target'], 'm_meta_tokens': out['m_meta_tokens'], 'm_ab_w_in': out['m_ab_w_in'], 'm_ab_conv_w': out['m_ab_conv_w'], 'm_ab_a_log': out['m_ab_a_log'], 'm_ab_dt_bias': out['m_ab_dt_bias'], 'm_ab_gnorm_g': out['m_ab_gnorm_g'], 'm_ab_w_out': out['m_ab_w_out'], 'm_c_w_in': out['m_c_w_in'], 'm_c_lb_raw': out['m_c_lb_raw'], 'm_c_gnorm_g': out['m_c_gnorm_g'], 'm_c_w_out': out['m_c_w_out'], 'm_ln_mix_g': out['m_ln_mix_g'], 'm_ln_mix_b': out['m_ln_mix_b'], 'm_mlp_w1': out['m_mlp_w1'], 'm_mlp_w2': out['m_mlp_w2'], 'm_ln_ffn_g': out['m_ln_ffn_g'], 'm_ln_ffn_b': out['m_ln_ffn_b'], 'v_meta_tokens': out['v_meta_tokens'], 'v_ab_w_in': out['v_ab_w_in'], 'v_ab_conv_w': out['v_ab_conv_w'], 'v_ab_a_log': out['v_ab_a_log'], 'v_ab_dt_bias': out['v_ab_dt_bias'], 'v_ab_gnorm_g': out['v_ab_gnorm_g'], 'v_ab_w_out': out['v_ab_w_out'], 'v_c_w_in': out['v_c_w_in'], 'v_c_lb_raw': out['v_c_lb_raw'], 'v_c_gnorm_g': out['v_c_gnorm_g'], 'v_c_w_out': out['v_c_w_out'], 'v_ln_mix_g': out['v_ln_mix_g'], 'v_ln_mix_b': out['v_ln_mix_b'], 'v_mlp_w1': out['v_mlp_w1'], 'v_mlp_w2': out['v_mlp_w2'], 'v_ln_ffn_g': out['v_ln_ffn_g'], 'v_ln_ffn_b': out['v_ln_ffn_b']}


def _loss(weights, diff, rest, loss_target):
    with _jax.named_scope("forward"):
        args = {**rest, TWIN_DIFF_INPUT: diff, **{k: w.astype(_WEIGHT_DTYPES[k]) for k, w in weights.items()}}
        y = _forward(args)
    with _jax.named_scope("loss_head"):
        err = _jnp.square(y.astype(_jnp.float32) - loss_target)
        return 0.5 * _jnp.sum(_jnp.mean(err, axis=-1)) if err.ndim else 0.5 * err


def _adamw(w, g, m, v):
    m = ADAM_B1 * m + (1.0 - ADAM_B1) * g
    v = ADAM_B2 * v + (1.0 - ADAM_B2) * _jnp.square(g)
    m_hat = m / (1.0 - ADAM_B1 ** ADAM_STEP)
    v_hat = v / (1.0 - ADAM_B2 ** ADAM_STEP)
    delta = -ADAM_LR * (m_hat / (_jnp.sqrt(v_hat) + ADAM_EPS) + ADAM_WD * w)
    return delta, m, v


def reference(x, meta_tokens, ab_w_in, ab_conv_w, ab_a_log, ab_dt_bias, ab_gnorm_g, ab_w_out, c_w_in, c_lb_raw, c_gnorm_g, c_w_out, ln_mix_g, ln_mix_b, mlp_w1, mlp_w2, ln_ffn_g, ln_ffn_b, loss_target, m_meta_tokens, m_ab_w_in, m_ab_conv_w, m_ab_a_log, m_ab_dt_bias, m_ab_gnorm_g, m_ab_w_out, m_c_w_in, m_c_lb_raw, m_c_gnorm_g, m_c_w_out, m_ln_mix_g, m_ln_mix_b, m_mlp_w1, m_mlp_w2, m_ln_ffn_g, m_ln_ffn_b, v_meta_tokens, v_ab_w_in, v_ab_conv_w, v_ab_a_log, v_ab_dt_bias, v_ab_gnorm_g, v_ab_w_out, v_c_w_in, v_c_lb_raw, v_c_gnorm_g, v_c_w_out, v_ln_mix_g, v_ln_mix_b, v_mlp_w1, v_mlp_w2, v_ln_ffn_g, v_ln_ffn_b):
    given = dict(x=x, meta_tokens=meta_tokens, ab_w_in=ab_w_in, ab_conv_w=ab_conv_w, ab_a_log=ab_a_log, ab_dt_bias=ab_dt_bias, ab_gnorm_g=ab_gnorm_g, ab_w_out=ab_w_out, c_w_in=c_w_in, c_lb_raw=c_lb_raw, c_gnorm_g=c_gnorm_g, c_w_out=c_w_out, ln_mix_g=ln_mix_g, ln_mix_b=ln_mix_b, mlp_w1=mlp_w1, mlp_w2=mlp_w2, ln_ffn_g=ln_ffn_g, ln_ffn_b=ln_ffn_b, loss_target=loss_target, m_meta_tokens=m_meta_tokens, m_ab_w_in=m_ab_w_in, m_ab_conv_w=m_ab_conv_w, m_ab_a_log=m_ab_a_log, m_ab_dt_bias=m_ab_dt_bias, m_ab_gnorm_g=m_ab_gnorm_g, m_ab_w_out=m_ab_w_out, m_c_w_in=m_c_w_in, m_c_lb_raw=m_c_lb_raw, m_c_gnorm_g=m_c_gnorm_g, m_c_w_out=m_c_w_out, m_ln_mix_g=m_ln_mix_g, m_ln_mix_b=m_ln_mix_b, m_mlp_w1=m_mlp_w1, m_mlp_w2=m_mlp_w2, m_ln_ffn_g=m_ln_ffn_g, m_ln_ffn_b=m_ln_ffn_b, v_meta_tokens=v_meta_tokens, v_ab_w_in=v_ab_w_in, v_ab_conv_w=v_ab_conv_w, v_ab_a_log=v_ab_a_log, v_ab_dt_bias=v_ab_dt_bias, v_ab_gnorm_g=v_ab_gnorm_g, v_ab_w_out=v_ab_w_out, v_c_w_in=v_c_w_in, v_c_lb_raw=v_c_lb_raw, v_c_gnorm_g=v_c_gnorm_g, v_c_w_out=v_c_w_out, v_ln_mix_g=v_ln_mix_g, v_ln_mix_b=v_ln_mix_b, v_mlp_w1=v_mlp_w1, v_mlp_w2=v_mlp_w2, v_ln_ffn_g=v_ln_ffn_g, v_ln_ffn_b=v_ln_ffn_b)
    weights = {n: given[n] for n in TWIN_WEIGHTS}
    shared = {n: given[n] for n in SHARED_INPUTS}
    per_example = {n: given[n] for n in ['x']}
    grad_fn = _jax.value_and_grad(_loss, argnums=(0, 1))

    def one_microbatch(ex, loss_target):
        ex = dict(ex)
        diff = ex.pop(TWIN_DIFF_INPUT)
        return grad_fn(weights, diff, {**shared, **ex}, loss_target)

    if N_MICROBATCH == 1:
        loss, (grad_w, grad_x) = one_microbatch(per_example, given["loss_target"])
    else:
        def body(carry, xs):
            loss_sum, grad_sum = carry
            l_k, (gw_k, gx_k) = one_microbatch(xs[0], xs[1])
            with _jax.named_scope("update"):
                return (loss_sum + l_k, _jax.tree.map(_jnp.add, grad_sum, gw_k)), gx_k

        init = (_jnp.zeros((), _jnp.float32), _jax.tree.map(_jnp.zeros_like, weights))
        (loss, grad_w), grad_x = _jax.lax.scan(body, init, (per_example, given["loss_target"]))
    with _jax.named_scope("update"):
        delta_w, new_m, new_v = {}, {}, {}
        for n in TWIN_WEIGHTS:
            delta_w[n], new_m[n], new_v[n] = _adamw(weights[n], grad_w[n], given["m_" + n], given["v_" + n])
    return (loss, grad_x, *[grad_w[n] for n in TWIN_WEIGHTS], *[delta_w[n] for n in TWIN_WEIGHTS],
            *[new_m[n] for n in TWIN_WEIGHTS], *[new_v[n] for n in TWIN_WEIGHTS])
```

```python
import functools
import math

import jax
import jax.numpy as jnp
from jax import lax
from jax.experimental import pallas as pl
from jax.experimental.pallas import tpu as pltpu

F32 = jnp.float32
BF16 = jnp.bfloat16
HI = lax.Precision.HIGHEST

N_DEV = 8
D_MODEL = 1024
N_META = 16
D_FF = 4096
DEPTH = 2
GDN_HEADS = 4
SB_HEADS = 8
SB_DH = 64
HG_HEADS = 8
HEAD_W = 128
CHUNK = 64
SB_BLOCK = 128
CONV_K = 4
DN_ALPHA = float((2 * DEPTH) ** 0.25)
LN_EPS = 1e-5
RMS_EPS = 1e-6
L2_EPS = 1e-6
ADAM_LR, ADAM_B1, ADAM_B2, ADAM_EPS, ADAM_WD, ADAM_STEP = 0.001, 0.9, 0.999, 1e-08, 0.01, 10

AB_QKV = 0
AB_Z = 1536
AB_SB = 2048
AB_BA = 3584
AB_CAT = 3840
AB_IN = 3592

VMEM_LIMIT = 56 * 1024 * 1024


def _cp(sem=None, **kw):
    if sem is not None:
        kw["dimension_semantics"] = sem
    return pltpu.CompilerParams(vmem_limit_bytes=VMEM_LIMIT, **kw)


def _row_tile(n, want):
    best = 8
    for t in range(8, min(n, want) + 1, 8):
        if n % t == 0:
            best = t
    return best


@jax.custom_vjp
def _sigmoid(x):
    e = jnp.exp(-jnp.abs(x))
    r = 1.0 / (1.0 + e)
    return jnp.where(x >= 0, r, e * r)


def _sigmoid_fwd(x):
    s = _sigmoid(x)
    return s, s


def _sigmoid_bwd(s, g):
    return (g * s * (1.0 - s),)


_sigmoid.defvjp(_sigmoid_fwd, _sigmoid_bwd)


def _log1p_exp_neg_abs(x):
    e = jnp.exp(-jnp.abs(x))
    return jnp.where(e < 1e-4, e - 0.5 * e * e, jnp.log(1.0 + e))


@jax.custom_vjp
def _softplus(x):
    return jnp.maximum(x, 0.0) + _log1p_exp_neg_abs(x)


def _softplus_fwd(x):
    return _softplus(x), x


def _softplus_bwd(x, g):
    return (g * _sigmoid(x),)


_softplus.defvjp(_softplus_fwd, _softplus_bwd)


def _silu(x):
    return x * _sigmoid(x)


def _silu_grad(x):
    s = _sigmoid(x)
    return s * (1.0 + x * (1.0 - s))


def _dot(a, b, dims, precision=None):
    return lax.dot_general(a, b, (dims, ((), ())), precision=precision, preferred_element_type=F32)


NN = ((1,), (0,))
NT = ((1,), (1,))
TN = ((0,), (0,))


def _mm(a, b, mode, *, tm, tn, tk, name, a_fn=None, epi=None, c=None, scale=1.0, b_dev=False, out_dev=False):
    if mode == "NN":
        m, kk = a.shape
        n = b.shape[2] * N_DEV if b_dev else b.shape[1]
    elif mode == "NT":
        m, kk = a.shape
        n = b.shape[1] if b_dev else b.shape[0]
    else:
        kk, m = a.shape
        n = b.shape[1]
    assert m % tm == 0 and n % tn == 0 and kk % tk == 0, (name, m, n, kk, tm, tn, tk)
    nk = kk // tk
    dims = {"NN": NN, "NT": NT, "TN": TN}[mode]

    if mode == "TN":
        a_spec = pl.BlockSpec((tk, tm), lambda i, j, k: (k, i))
    else:
        a_spec = pl.BlockSpec((tm, tk), lambda i, j, k: (i, k))
    if mode == "NN":
        if b_dev:
            assert tn == b.shape[2]
            b_spec = pl.BlockSpec((None, tk, tn), lambda i, j, k: (j, k, 0))
        else:
            b_spec = pl.BlockSpec((tk, tn), lambda i, j, k: (k, j))
    elif mode == "NT":
        if b_dev:
            assert tk == b.shape[2]
            b_spec = pl.BlockSpec((None, tn, tk), lambda i, j, k: (k, j, 0))
        else:
            b_spec = pl.BlockSpec((tn, tk), lambda i, j, k: (j, k))
    else:
        b_spec = pl.BlockSpec((tk, tn), lambda i, j, k: (k, j))
    in_specs = [a_spec, b_spec]
    operands = [a, b]
    if epi is not None:
        in_specs.append(pl.BlockSpec((tm, tn), lambda i, j, k: (i, j)))
        operands.append(c)
    if out_dev:
        assert tn == n // N_DEV
        out_shape = jax.ShapeDtypeStruct((N_DEV, m, tn), F32)
        out_spec = pl.BlockSpec((None, tm, tn), lambda i, j, k: (j, i, 0))
    else:
        out_shape = jax.ShapeDtypeStruct((m, n), F32)
        out_spec = pl.BlockSpec((tm, tn), lambda i, j, k: (i, j))

    def body(*refs):
        a_ref, b_ref = refs[0], refs[1]
        c_ref = refs[2] if epi is not None else None
        o_ref = refs[3] if epi is not None else refs[2]
        acc_ref = refs[-1] if nk > 1 else None
        av = a_ref[...]
        if a_fn == "relu2":
            av = jnp.square(jnp.maximum(av, 0.0))
        p = _dot(av.astype(BF16), b_ref[...].astype(BF16), dims)

        def finish(acc):
            if epi == "add":
                acc = acc + scale * c_ref[...]
            elif epi == "relu2grad":
                acc = acc * (2.0 * jnp.maximum(c_ref[...], 0.0))
            o_ref[...] = acc

        if nk == 1:
            finish(p)
        else:
            k = pl.program_id(2)

            @pl.when(k == 0)
            def _():
                acc_ref[...] = p

            @pl.when(k > 0)
            def _():
                acc_ref[...] += p

            @pl.when(k == nk - 1)
            def _():
                finish(acc_ref[...])

    return pl.pallas_call(
        body, name=name, grid=(m // tm, n // tn, nk), in_specs=in_specs, out_specs=out_spec, out_shape=out_shape,
        scratch_shapes=[pltpu.VMEM((tm, tn), F32)] if nk > 1 else [],
        compiler_params=_cp(("parallel", "parallel", "arbitrary")),
    )(*operands)


def _ln_fwd(a, b, g, beta, *, name):
    lp, d = a.shape
    tm = _row_tile(lp, 512)

    def body(a_ref, b_ref, g_ref, be_ref, y_ref):
        pre = DN_ALPHA * a_ref[...] + b_ref[...]
        mu = jnp.mean(pre, axis=-1, keepdims=True)
        xc = pre - mu
        var = jnp.mean(xc * xc, axis=-1, keepdims=True)
        y_ref[...] = xc * lax.rsqrt(var + LN_EPS) * g_ref[...] + be_ref[...]

    row = pl.BlockSpec((tm, d), lambda i: (i, 0))
    vec = pl.BlockSpec((1, d), lambda i: (0, 0))
    return pl.pallas_call(
        body, name=name, grid=(lp // tm,), in_specs=[row, row, vec, vec], out_specs=row,
        out_shape=jax.ShapeDtypeStruct((lp, d), F32), compiler_params=_cp(("parallel",)),
    )(a, b, g.reshape(1, d), beta.reshape(1, d))


def _ln_bwd(a, b, g, dy, *, name):
    lp, d = a.shape
    tm = _row_tile(lp, 512)

    def body(a_ref, b_ref, g_ref, dy_ref, dpre_ref, dg_ref, db_ref):
        pre = DN_ALPHA * a_ref[...] + b_ref[...]
        mu = jnp.mean(pre, axis=-1, keepdims=True)
        xc = pre - mu
        var = jnp.mean(xc * xc, axis=-1, keepdims=True)
        rstd = lax.rsqrt(var + LN_EPS)
        xhat = xc * rstd
        dyv = dy_ref[...]
        dxh = dyv * g_ref[...]
        m1 = jnp.mean(dxh, axis=-1, keepdims=True)
        m2 = jnp.mean(dxh * xhat, axis=-1, keepdims=True)
        dpre_ref[...] = rstd * (dxh - m1 - xhat * m2)

        @pl.when(pl.program_id(0) == 0)
        def _():
            dg_ref[...] = jnp.zeros_like(dg_ref)
            db_ref[...] = jnp.zeros_like(db_ref)

        dg_ref[...] += jnp.sum(dyv * xhat, axis=0, keepdims=True)
        db_ref[...] += jnp.sum(dyv, axis=0, keepdims=True)

    row = pl.BlockSpec((tm, d), lambda i: (i, 0))
    vec = pl.BlockSpec((1, d), lambda i: (0, 0))
    return pl.pallas_call(
        body, name=name, grid=(lp // tm,), in_specs=[row, row, vec, row], out_specs=[row, vec, vec],
        out_shape=[jax.ShapeDtypeStruct((lp, d), F32), jax.ShapeDtypeStruct((1, d), F32),
                   jax.ShapeDtypeStruct((1, d), F32)],
        compiler_params=_cp(("arbitrary",)),
    )(a, b, g.reshape(1, d), dy)


def _loss_head(y, target, *, name):
    lp, d = y.shape
    seq = target.shape[0]
    tm = SB_BLOCK
    first = (lp - seq) // tm
    assert (lp - seq) % tm == 0 and seq % tm == 0

    def body(y_ref, t_ref, dy_ref, loss_ref):
        i = pl.program_id(0)
        live = i >= first
        diff = jnp.where(live, y_ref[...] - t_ref[...], 0.0)
        dy_ref[...] = diff * (1.0 / d)

        @pl.when(i == 0)
        def _():
            loss_ref[...] = jnp.zeros_like(loss_ref)

        loss_ref[...] += jnp.sum(diff * diff, axis=0, keepdims=True) * (0.5 / d)

    return pl.pallas_call(
        body, name=name, grid=(lp // tm,),
        in_specs=[pl.BlockSpec((tm, d), lambda i: (i, 0)),
                  pl.BlockSpec((tm, d), lambda i: (jnp.maximum(i - first, 0), 0))],
        out_specs=[pl.BlockSpec((tm, d), lambda i: (i, 0)), pl.BlockSpec((1, d), lambda i: (0, 0))],
        out_shape=[jax.ShapeDtypeStruct((lp, d), F32), jax.ShapeDtypeStruct((1, d), F32)],
        compiler_params=_cp(("arbitrary",)),
    )(y, target)


def _gate_fwd(o, zsrc, z_blk0, g, other, *, heads, name):
    lp = o.shape[0]
    tm = _row_tile(lp, 512)
    nblk = D_MODEL // HEAD_W

    def body(o_ref, z_ref, g_ref, x_ref, y_ref):
        h = pl.program_id(1)

        @pl.when(h < heads)
        def _():
            ov = o_ref[...]
            r = lax.rsqrt(jnp.mean(ov * ov, axis=-1, keepdims=True) + RMS_EPS)
            y_ref[...] = ov * r * g_ref[...] * _silu(z_ref[...])

        @pl.when(h >= heads)
        def _():
            y_ref[...] = x_ref[...]

    other_w = other.shape[1] // HEAD_W
    return pl.pallas_call(
        body, name=name, grid=(lp // tm, nblk),
        in_specs=[pl.BlockSpec((tm, HEAD_W), lambda i, h: (i, jnp.minimum(h, heads - 1))),
                  pl.BlockSpec((tm, HEAD_W), lambda i, h: (i, z_blk0 + jnp.minimum(h, heads - 1))),
                  pl.BlockSpec((1, HEAD_W), lambda i, h: (0, 0)),
                  pl.BlockSpec((tm, HEAD_W), lambda i, h: (i, jnp.clip(h - heads, 0, other_w - 1)))],
        out_specs=pl.BlockSpec((tm, HEAD_W), lambda i, h: (i, h)),
        out_shape=jax.ShapeDtypeStruct((lp, D_MODEL), F32),
        compiler_params=_cp(("parallel", "arbitrary")),
    )(o, zsrc, g.reshape(1, HEAD_W), other)


def _gate_bwd(o, zsrc, z_blk0, g, dy, *, heads, name):
    lp = o.shape[0]
    tm = _row_tile(lp, 512)

    def body(o_ref, z_ref, g_ref, dy_ref, do_ref, dz_ref, dg_ref):
        ov, zv, gv, dyv = o_ref[...], z_ref[...], g_ref[...], dy_ref[...]
        r = lax.rsqrt(jnp.mean(ov * ov, axis=-1, keepdims=True) + RMS_EPS)
        nrm = ov * r
        s = _silu(zv)
        dn = dyv * gv * s
        do_ref[...] = r * (dn - nrm * jnp.mean(dn * nrm, axis=-1, keepdims=True))
        dz_ref[...] = dyv * nrm * gv * _silu_grad(zv)

        @pl.when((pl.program_id(0) == 0) & (pl.program_id(1) == 0))
        def _():
            dg_ref[...] = jnp.zeros_like(dg_ref)

        dg_ref[...] += jnp.sum(dyv * nrm * s, axis=0, keepdims=True)

    blk = pl.BlockSpec((tm, HEAD_W), lambda i, h: (i, h))
    return pl.pallas_call(
        body, name=name, grid=(lp // tm, heads),
        in_specs=[blk, pl.BlockSpec((tm, HEAD_W), lambda i, h: (i, z_blk0 + h)),
                  pl.BlockSpec((1, HEAD_W), lambda i, h: (0, 0)), blk],
        out_specs=[blk, blk, pl.BlockSpec((1, HEAD_W), lambda i, h: (0, 0))],
        out_shape=[jax.ShapeDtypeStruct((lp, heads * HEAD_W), F32), jax.ShapeDtypeStruct((lp, heads * HEAD_W), F32),
                   jax.ShapeDtypeStruct((1, HEAD_W), F32)],
        compiler_params=_cp(("arbitrary", "arbitrary")),
    )(o, zsrc, g.reshape(1, HEAD_W), dy)


def _conv_taps(x, w):
    acc = w[CONV_K - 1:CONV_K, :] * x
    for k in range(CONV_K - 1):
        acc = acc + w[k:k + 1, :] * pltpu.roll(x, CONV_K - 1 - k, 0)
    return acc


def _gdn_pre_fwd(p0, conv_w, pad, *, name):
    lp = p0.shape[0]
    nq = GDN_HEADS
    qscale = HEAD_W ** -0.5

    def body(x_ref, w_ref, y_ref):
        j = pl.program_id(0)
        c = _conv_taps(x_ref[...], w_ref[...])
        s = _silu(c)
        r = lax.rsqrt(jnp.sum(s * s, axis=-1, keepdims=True) + L2_EPS)
        mult = jnp.where(j < nq, r * qscale, jnp.where(j < 2 * nq, r, 1.0))
        rows = lax.broadcasted_iota(jnp.int32, (lp, 1), 0)
        y_ref[...] = jnp.where(rows >= pad, s * mult, 0.0)

    return pl.pallas_call(
        body, name=name, grid=(3 * nq,),
        in_specs=[pl.BlockSpec((lp, HEAD_W), lambda j: (0, j)), pl.BlockSpec((CONV_K, HEAD_W), lambda j: (0, j))],
        out_specs=pl.BlockSpec((lp, HEAD_W), lambda j: (0, j)),
        out_shape=jax.ShapeDtypeStruct((lp, 3 * nq * HEAD_W), F32), compiler_params=_cp(("parallel",)),
    )(p0, conv_w)


def _gdn_pre_bwd(p0, conv_w, dqkv, pad, *, name):
    lp = p0.shape[0]
    nq = GDN_HEADS
    qscale = HEAD_W ** -0.5

    def body(x_ref, w_ref, dy_ref, dx_ref, dw_ref):
        j = pl.program_id(0)
        x, w = x_ref[...], w_ref[...]
        c = _conv_taps(x, w)
        s = _silu(c)
        r = lax.rsqrt(jnp.sum(s * s, axis=-1, keepdims=True) + L2_EPS)
        rows = lax.broadcasted_iota(jnp.int32, (lp, 1), 0)
        dy = jnp.where(rows >= pad, dy_ref[...], 0.0)
        nrm = s * r
        dn = dy * jnp.where(j < nq, qscale, 1.0)
        ds_norm = r * (dn - nrm * jnp.sum(nrm * dn, axis=-1, keepdims=True))
        ds = jnp.where(j < 2 * nq, ds_norm, dy)
        dc = ds * _silu_grad(c)
        dx = w[CONV_K - 1:CONV_K, :] * dc
        dws = [None] * CONV_K
        dws[CONV_K - 1] = jnp.sum(dc * x, axis=0, keepdims=True)
        for k in range(CONV_K - 1):
            sh = CONV_K - 1 - k
            dx = dx + w[k:k + 1, :] * pltpu.roll(dc, lp - sh, 0)
            dws[k] = jnp.sum(dc * pltpu.roll(x, sh, 0), axis=0, keepdims=True)
        dx_ref[...] = dx
        dw_ref[...] = jnp.concatenate(dws, axis=0)

    blk = pl.BlockSpec((lp, HEAD_W), lambda j: (0, j))
    wblk = pl.BlockSpec((CONV_K, HEAD_W), lambda j: (0, j))
    return pl.pallas_call(
        body, name=name, grid=(3 * nq,), in_specs=[blk, wblk, blk], out_specs=[blk, wblk],
        out_shape=[jax.ShapeDtypeStruct((lp, 3 * nq * HEAD_W), F32),
                   jax.ShapeDtypeStruct((CONV_K, 3 * nq * HEAD_W), F32)],
        compiler_params=_cp(("parallel",)),
    )(p0, conv_w, dqkv)


def _tri(c, strict):
    r = lax.broadcasted_iota(jnp.int32, (c, c), 0)
    q = lax.broadcasted_iota(jnp.int32, (c, c), 1)
    return (q < r) if strict else (q <= r)


@jax.custom_vjp
def _inv_unit_lower(m):
    c = m.shape[0]
    eye = (lax.broadcasted_iota(jnp.int32, (c, c), 0) == lax.broadcasted_iota(jnp.int32, (c, c), 1)).astype(F32)
    x = eye - m
    p = m
    n = 2
    while n < c:
        p = _dot(p, p, NN, HI)
        x = x + _dot(x, p, NN, HI)
        n *= 2
    return x


def _inv_fwd(m):
    t = _inv_unit_lower(m)
    return t, t


def _inv_bwd(t, g):
    return (-_dot(_dot(t, g, TN, HI), t, NT, HI),)


_inv_unit_lower.defvjp(_inv_fwd, _inv_bwd)


def _gdn_chunk(q, k, v, ba, alog, dtb, s, head, valid):
    c = q.shape[0]
    lane = lax.broadcasted_iota(jnp.int32, (1, HEAD_W), 1)
    pick = lambda x, l: jnp.sum(jnp.where(lane == l, x, 0.0), axis=-1, keepdims=True)
    beta = jnp.where(valid, _sigmoid(pick(ba, head)), 0.0)
    g = jnp.where(valid, -jnp.exp(pick(alog, head)) * _softplus(pick(ba, GDN_HEADS + head) + pick(dtb, head)), 0.0)
    causal, strict = _tri(c, False), _tri(c, True)
    gcb = _dot(causal.astype(F32), g * jnp.ones((1, HEAD_W), F32), NN, HI)
    gc_col = gcb[:, :c]
    gc_row = gcb.T[:c, :]
    decay = jnp.where(causal, jnp.exp(jnp.minimum(gc_col - gc_row, 0.0)), 0.0)
    egc = jnp.exp(gcb)
    kb = k * beta
    m = jnp.where(strict, _dot(kb, k, NT, HI) * decay, 0.0)
    t = _inv_unit_lower(m)
    u = _dot(t, v * beta, NN, HI)
    w = _dot(t, kb * egc, NN, HI)
    a = _dot(q, k, NT, HI) * decay
    gl = gcb[c - 1:c, :]
    v_new = u - _dot(w, s, NN, HI)
    o = _dot(q * egc, s, NN, HI) + _dot(a, v_new, NN, HI)
    s2 = s * jnp.exp(gl) + _dot(k * jnp.exp(gl - gcb), v_new, TN, HI)
    return o, s2


def _gdn_fwd(qkv, p0, alog_v, dtb_v, pad, *, name):
    lp = qkv.shape[0]
    n = lp // CHUNK
    nh = GDN_HEADS

    def body(q_ref, k_ref, v_ref, ba_ref, al_ref, dt_ref, o_ref, st_ref, s_ref):
        i = pl.program_id(0)

        @pl.when(i == 0)
        def _():
            s_ref[...] = jnp.zeros_like(s_ref)

        valid = (i * CHUNK + lax.broadcasted_iota(jnp.int32, (CHUNK, 1), 0)) >= pad
        for h in range(nh):
            cs = slice(h * HEAD_W, (h + 1) * HEAD_W)
            st_ref[h] = s_ref[h]
            o, s2 = _gdn_chunk(q_ref[:, cs], k_ref[:, cs], v_ref[:, cs], ba_ref[...], al_ref[...], dt_ref[...],
                               s_ref[h], h, valid)
            o_ref[:, cs] = o
            s_ref[h] = s2

    w = nh * HEAD_W
    vec = pl.BlockSpec((1, HEAD_W), lambda i: (0, 0))
    return pl.pallas_call(
        body, name=name, grid=(n,),
        in_specs=[pl.BlockSpec((CHUNK, w), lambda i: (i, 0)), pl.BlockSpec((CHUNK, w), lambda i: (i, 1)),
                  pl.BlockSpec((CHUNK, w), lambda i: (i, 2)), pl.BlockSpec((CHUNK, HEAD_W), lambda i: (i, AB_BA // HEAD_W)),
                  vec, vec],
        out_specs=[pl.BlockSpec((CHUNK, w), lambda i: (i, 0)),
                   pl.BlockSpec((None, nh, HEAD_W, HEAD_W), lambda i: (i, 0, 0, 0))],
        out_shape=[jax.ShapeDtypeStruct((lp, w), F32), jax.ShapeDtypeStruct((n, nh, HEAD_W, HEAD_W), F32)],
        scratch_shapes=[pltpu.VMEM((nh, HEAD_W, HEAD_W), F32)],
        compiler_params=_cp(("arbitrary",)),
    )(qkv, qkv, qkv, p0, alog_v, dtb_v)


def _gdn_bwd(qkv, p0, alog_v, dtb_v, states, do, pad, *, name):
    lp = qkv.shape[0]
    n = lp // CHUNK
    nh = GDN_HEADS

    def body(q_ref, k_ref, v_ref, ba_ref, al_ref, dt_ref, st_ref, do_ref,
             dq_ref, dk_ref, dv_ref, dba_ref, dal_ref, ddt_ref, ds_ref):
        step = pl.program_id(0)
        i = n - 1 - step

        @pl.when(step == 0)
        def _():
            ds_ref[...] = jnp.zeros_like(ds_ref)
            dal_ref[...] = jnp.zeros_like(dal_ref)
            ddt_ref[...] = jnp.zeros_like(ddt_ref)

        valid = (i * CHUNK + lax.broadcasted_iota(jnp.int32, (CHUNK, 1), 0)) >= pad
        dba = jnp.zeros((CHUNK, HEAD_W), F32)
        dal = jnp.zeros((1, HEAD_W), F32)
        ddt = jnp.zeros((1, HEAD_W), F32)
        for h in range(nh):
            cs = slice(h * HEAD_W, (h + 1) * HEAD_W)
            fn = functools.partial(_gdn_chunk, head=h, valid=valid)
            _, vjp = jax.vjp(fn, q_ref[:, cs], k_ref[:, cs], v_ref[:, cs], ba_ref[...], al_ref[...], dt_ref[...],
                             st_ref[h])
            dq, dk, dv, dba_h, dal_h, ddt_h, ds = vjp((do_ref[:, cs], ds_ref[h]))
            dq_ref[:, cs] = dq
            dk_ref[:, cs] = dk
            dv_ref[:, cs] = dv
            ds_ref[h] = ds
            dba, dal, ddt = dba + dba_h, dal + dal_h, ddt + ddt_h
        dba_ref[...] = dba
        dal_ref[...] += dal
        ddt_ref[...] += ddt

    w = nh * HEAD_W
    rev = lambda c: (lambda s: (n - 1 - s, c))
    vec = pl.BlockSpec((1, HEAD_W), lambda s: (0, 0))
    dq, dk, dv, dba, dal, ddt = pl.pallas_call(
        body, name=name, grid=(n,),
        in_specs=[pl.BlockSpec((CHUNK, w), rev(0)), pl.BlockSpec((CHUNK, w), rev(1)), pl.BlockSpec((CHUNK, w), rev(2)),
                  pl.BlockSpec((CHUNK, HEAD_W), rev(AB_BA // HEAD_W)), vec, vec,
                  pl.BlockSpec((None, nh, HEAD_W, HEAD_W), lambda s: (n - 1 - s, 0, 0, 0)),
                  pl.BlockSpec((CHUNK, w), rev(0))],
        out_specs=[pl.BlockSpec((CHUNK, w), rev(0)), pl.BlockSpec((CHUNK, w), rev(0)), pl.BlockSpec((CHUNK, w), rev(0)),
                   pl.BlockSpec((CHUNK, HEAD_W), rev(0)), vec, vec],
        out_shape=[jax.ShapeDtypeStruct((lp, w), F32)] * 3 + [jax.ShapeDtypeStruct((lp, HEAD_W), F32)]
        + [jax.ShapeDtypeStruct((1, HEAD_W), F32)] * 2,
        scratch_shapes=[pltpu.VMEM((nh, HEAD_W, HEAD_W), F32)],
        compiler_params=_cp(("arbitrary",)),
    )(qkv, qkv, qkv, p0, alog_v, dtb_v, states, do)
    return dq, dk, dv, dba, dal, ddt


def _hg_chunk(qr, fr, ir, lb, st, valid):
    c = qr.shape[0]
    fg = lb + (1.0 - lb) * _sigmoid(fr)
    logf = jnp.where(valid, jnp.log(fg), 0.0)
    k = jnp.where(valid, 1.0 - fg, 0.0)
    qs = jnp.where(valid, _silu(qr), 0.0)
    v = jnp.where(valid, ir, 0.0)
    causal = _tri(c, False)
    b = _dot(causal.astype(F32), logf, NN, HI)
    shp = (c, c, HEAD_W)
    causal3 = lax.broadcasted_iota(jnp.int32, shp, 1) <= lax.broadcasted_iota(jnp.int32, shp, 0)
    pair = jnp.where(causal3, jnp.exp(jnp.minimum(b[:, None, :] - b[None, :, :], 0.0)), 0.0)
    a = jnp.sum(qs[:, None, :] * k[None, :, :] * pair, axis=-1)
    bl = b[c - 1:c, :]
    o = _dot(qs * jnp.exp(b), st, NT, HI) + _dot(a, v, NN, HI)
    st2 = st * jnp.exp(bl) + _dot(v, k * jnp.exp(bl - b), TN, HI)
    return o, st2


def _hg_fwd(p1, lb, pad, *, name):
    lp = p1.shape[0]
    n = lp // CHUNK
    nh = HG_HEADS

    def body(q_ref, f_ref, i_ref, lb_ref, o_ref, st_ref, s_ref):
        i = pl.program_id(1)

        @pl.when(i == 0)
        def _():
            s_ref[...] = jnp.zeros_like(s_ref)

        valid = (i * CHUNK + lax.broadcasted_iota(jnp.int32, (CHUNK, 1), 0)) >= pad
        st_ref[...] = s_ref[...]
        o, s2 = _hg_chunk(q_ref[...], f_ref[...], i_ref[...], lb_ref[...], s_ref[...], valid)
        o_ref[...] = o
        s_ref[...] = s2

    blk = lambda off: pl.BlockSpec((CHUNK, HEAD_W), lambda h, i: (i, off + h))
    return pl.pallas_call(
        body, name=name, grid=(nh, n),
        in_specs=[blk(0), blk(nh), blk(2 * nh), pl.BlockSpec((1, HEAD_W), lambda h, i: (0, h))],
        out_specs=[blk(0), pl.BlockSpec((None, None, HEAD_W, HEAD_W), lambda h, i: (h, i, 0, 0))],
        out_shape=[jax.ShapeDtypeStruct((lp, nh * HEAD_W), F32), jax.ShapeDtypeStruct((nh, n, HEAD_W, HEAD_W), F32)],
        scratch_shapes=[pltpu.VMEM((HEAD_W, HEAD_W), F32)],
        compiler_params=_cp(("parallel", "arbitrary")),
    )(p1, p1, p1, lb)


def _hg_bwd(p1, lb, states, do, pad, *, name):
    lp = p1.shape[0]
    n = lp // CHUNK
    nh = HG_HEADS

    def body(q_ref, f_ref, i_ref, lb_ref, st_ref, do_ref, dq_ref, df_ref, di_ref, dlb_ref, ds_ref):
        step = pl.program_id(1)
        i = n - 1 - step

        @pl.when(step == 0)
        def _():
            ds_ref[...] = jnp.zeros_like(ds_ref)
            dlb_ref[...] = jnp.zeros_like(dlb_ref)

        valid = (i * CHUNK + lax.broadcasted_iota(jnp.int32, (CHUNK, 1), 0)) >= pad
        fn = functools.partial(_hg_chunk, valid=valid)
        _, vjp = jax.vjp(fn, q_ref[...], f_ref[...], i_ref[...], lb_ref[...], st_ref[...])
        dq, df, di, dlb, ds = vjp((do_ref[...], ds_ref[...]))
        dq_ref[...] = dq
        df_ref[...] = df
        di_ref[...] = di
        dlb_ref[...] += dlb
        ds_ref[...] = ds

    blk = lambda off: pl.BlockSpec((CHUNK, HEAD_W), lambda h, s: (n - 1 - s, off + h))
    w = nh * HEAD_W
    return pl.pallas_call(
        body, name=name, grid=(nh, n),
        in_specs=[blk(0), blk(nh), blk(2 * nh), pl.BlockSpec((1, HEAD_W), lambda h, s: (0, h)),
                  pl.BlockSpec((None, None, HEAD_W, HEAD_W), lambda h, s: (h, n - 1 - s, 0, 0)), blk(0)],
        out_specs=[blk(0), blk(0), blk(0), pl.BlockSpec((1, HEAD_W), lambda h, s: (0, h))],
        out_shape=[jax.ShapeDtypeStruct((lp, w), F32)] * 3 + [jax.ShapeDtypeStruct((1, w), F32)],
        scratch_shapes=[pltpu.VMEM((HEAD_W, HEAD_W), F32)],
        compiler_params=_cp(("parallel", "arbitrary")),
    )(p1, p1, p1, lb, states, do)


def _sb_scores(qm, kblk, allowed):
    z = _dot(qm.astype(BF16), kblk.astype(BF16), NT)
    lse = _log1p_exp_neg_abs(z)
    lsz = jnp.minimum(z, 0.0) - lse
    l1m = jnp.where(allowed, jnp.minimum(-z, 0.0) - lse, 0.0)
    return z, lsz, l1m


def _sb_fwd(p0, pad, *, name):
    lp = p0.shape[0]
    nb = lp // SB_BLOCK
    npair = SB_HEADS // 2
    blk0 = AB_SB // HEAD_W
    scale = SB_DH ** -0.5

    def body(q_ref, k_ref, v_ref, o_ref, tot_ref):
        i = pl.program_id(1)
        lane = lax.broadcasted_iota(jnp.int32, (1, HEAD_W), 1)
        qpos = i * SB_BLOCK + lax.broadcasted_iota(jnp.int32, (SB_BLOCK, 1), 0)
        after_mat = _tri(SB_BLOCK, True).astype(F32)
        q = q_ref[...] * scale
        halves = [lane < SB_DH, lane >= SB_DH]

        def step(t, carry):
            kb = i - t
            off = pl.multiple_of(kb * SB_BLOCK, SB_BLOCK)
            kblk = k_ref[pl.ds(off, SB_BLOCK), :]
            vblk = v_ref[pl.ds(off, SB_BLOCK), :].astype(BF16)
            kpos = kb * SB_BLOCK + lax.broadcasted_iota(jnp.int32, (1, SB_BLOCK), 1)
            allowed = (kpos < qpos) & (kpos >= pad)
            new = []
            for hh in range(2):
                acc, run = carry[hh]
                qm = jnp.where(halves[hh], q, 0.0)
                _, lsz, l1m = _sb_scores(qm, kblk, allowed)
                after = _dot(l1m, after_mat, NN, HI) + run
                wgt = jnp.where(allowed, jnp.exp(lsz + after), 0.0)
                acc = acc + _dot(wgt.astype(BF16), vblk, NN)
                run = run + jnp.sum(l1m, axis=-1, keepdims=True)
                new.append((acc, run))
            return tuple(new)

        zero = (jnp.zeros((SB_BLOCK, HEAD_W), F32), jnp.zeros((SB_BLOCK, 1), F32))
        res = lax.fori_loop(0, i + 1, step, (zero, zero))
        o_ref[...] = jnp.where(halves[0], res[0][0], res[1][0])
        tot_ref[...] = jnp.where(halves[0], res[0][1], res[1][1])

    full = lambda c0: pl.BlockSpec((lp, HEAD_W), lambda p, i: (0, c0 + p))
    out = pl.BlockSpec((SB_BLOCK, HEAD_W), lambda p, i: (i, p))
    return pl.pallas_call(
        body, name=name, grid=(npair, nb),
        in_specs=[pl.BlockSpec((SB_BLOCK, HEAD_W), lambda p, i: (i, blk0 + p)), full(blk0 + npair), full(blk0 + 2 * npair)],
        out_specs=[out, out],
        out_shape=[jax.ShapeDtypeStruct((lp, npair * HEAD_W), F32)] * 2,
        compiler_params=_cp(("parallel", "arbitrary")),
    )(p0, p0, p0)


def _sb_bwd(p0, tot, dsrc, d_blk0, pad, *, name):
    lp = p0.shape[0]
    nb = lp // SB_BLOCK
    npair = SB_HEADS // 2
    blk0 = AB_SB // HEAD_W
    scale = SB_DH ** -0.5

    def body(q_ref, k_ref, v_ref, tot_ref, do_ref, dq_ref, dk_ref, dv_ref):
        i = pl.program_id(1)

        @pl.when(i == 0)
        def _():
            dk_ref[...] = jnp.zeros_like(dk_ref)
            dv_ref[...] = jnp.zeros_like(dv_ref)

        lane = lax.broadcasted_iota(jnp.int32, (1, HEAD_W), 1)
        qpos = i * SB_BLOCK + lax.broadcasted_iota(jnp.int32, (SB_BLOCK, 1), 0)
        r = lax.broadcasted_iota(jnp.int32, (SB_BLOCK, SB_BLOCK), 0)
        cidx = lax.broadcasted_iota(jnp.int32, (SB_BLOCK, SB_BLOCK), 1)
        incl_mat = (r <= cidx).astype(F32)
        before_mat = (r < cidx).astype(F32)
        q = q_ref[...] * scale
        do = do_ref[...]
        totv = tot_ref[...]
        halves = [lane < SB_DH, lane >= SB_DH]
        qms = [jnp.where(hm, q, 0.0) for hm in halves]
        dos = [jnp.where(hm, do, 0.0) for hm in halves]
        tots = [totv[:, 0:1], totv[:, SB_DH:SB_DH + 1]]

        def step(kb, carry):
            off = pl.multiple_of(kb * SB_BLOCK, SB_BLOCK)
            kblk = k_ref[pl.ds(off, SB_BLOCK), :]
            vblk = v_ref[pl.ds(off, SB_BLOCK), :]
            kpos = kb * SB_BLOCK + lax.broadcasted_iota(jnp.int32, (1, SB_BLOCK), 1)
            allowed = (kpos < qpos) & (kpos >= pad)
            new = []
            dk_acc = jnp.zeros((SB_BLOCK, HEAD_W), F32)
            dv_acc = jnp.zeros((SB_BLOCK, HEAD_W), F32)
            for hh in range(2):
                dq, prun, erun = carry[hh]
                z, lsz, l1m = _sb_scores(qms[hh], kblk, allowed)
                after = tots[hh] - prun - _dot(l1m, incl_mat, NN, HI)
                wgt = jnp.where(allowed, jnp.exp(lsz + after), 0.0)
                dwgt = _dot(dos[hh].astype(BF16), vblk.astype(BF16), NT)
                e = wgt * dwgt
                dl1m = erun + _dot(e, before_mat, NN, HI)
                sg = _sigmoid(z)
                dz = jnp.where(allowed, e * (1.0 - sg) - dl1m * sg, 0.0).astype(BF16)
                kmask = jnp.where(halves[hh], kblk, 0.0)
                dq = dq + _dot(dz, kmask.astype(BF16), NN)
                dk_acc = dk_acc + _dot(dz, qms[hh].astype(BF16), TN)
                dv_acc = dv_acc + _dot(wgt.astype(BF16), dos[hh].astype(BF16), TN)
                prun = prun + jnp.sum(l1m, axis=-1, keepdims=True)
                erun = erun + jnp.sum(e, axis=-1, keepdims=True)
                new.append((dq, prun, erun))
            dk_ref[pl.ds(off, SB_BLOCK), :] += dk_acc
            dv_ref[pl.ds(off, SB_BLOCK), :] += dv_acc
            return tuple(new)

        zero = (jnp.zeros((SB_BLOCK, HEAD_W), F32), jnp.zeros((SB_BLOCK, 1), F32), jnp.zeros((SB_BLOCK, 1), F32))
        res = lax.fori_loop(0, i + 1, step, (zero, zero))
        dq_ref[...] = (res[0][0] + res[1][0]) * scale

    full = lambda c0: pl.BlockSpec((lp, HEAD_W), lambda p, i: (0, c0 + p))
    qb = lambda c0: pl.BlockSpec((SB_BLOCK, HEAD_W), lambda p, i: (i, c0 + p))
    return pl.pallas_call(
        body, name=name, grid=(npair, nb),
        in_specs=[qb(blk0), full(blk0 + npair), full(blk0 + 2 * npair), qb(0), qb(d_blk0)],
        out_specs=[qb(0), full(0), full(0)],
        out_shape=[jax.ShapeDtypeStruct((lp, npair * HEAD_W), F32)] * 3,
        compiler_params=_cp(("parallel", "arbitrary")),
    )(p0, p0, p0, tot, dsrc)


def _local_step(h0, target, pad, wts):
    lp = h0.shape[0]
    tm = _row_tile(lp, 1056)
    tkl = tm
    d = D_MODEL
    mm = _mm
    g = {}

    p0 = mm(h0, wts["w_ab"], "NN", tm=tm, tn=768, tk=d, name="l0_in_proj")
    qkv = _gdn_pre_fwd(p0, wts["conv_w"], pad, name="gdn_pre_fwd")
    oa_raw, gdn_states = _gdn_fwd(qkv, p0, wts["alog_v"], wts["dtb_v"], pad, name="gdn_fwd")
    ob, sb_tot = _sb_fwd(p0, pad, name="sb_fwd")
    oab = _gate_fwd(oa_raw, p0, AB_Z // HEAD_W, wts["ab_gn"], ob, heads=GDN_HEADS, name="gdn_gate_fwd")
    mix0 = mm(oab, wts["w_out0"], "NN", tm=tm, tn=512, tk=d, name="l0_out_proj")
    h0a = _ln_fwd(h0, mix0, wts["ln_mix_g"][0], wts["ln_mix_b"][0], name="ln_mix0_fwd")
    u0 = mm(h0a, wts["w1"][0], "NN", tm=tm, tn=512, tk=d, b_dev=True, name="mlp0_up")
    y0 = mm(u0, wts["w2"][0], "NN", tm=tm, tn=512, tk=d, a_fn="relu2", name="mlp0_down")
    h0b = _ln_fwd(h0a, y0, wts["ln_ffn_g"][0], wts["ln_ffn_b"][0], name="ln_ffn0_fwd")
    p1 = mm(h0b, wts["w_c"], "NN", tm=tm, tn=512, tk=d, b_dev=True, name="l1_in_proj")
    oc_raw, hg_states = _hg_fwd(p1, wts["lb"], pad, name="hg_fwd")
    oc = _gate_fwd(oc_raw, p1, 3 * HG_HEADS, wts["c_gn"], oc_raw, heads=HG_HEADS, name="hg_gate_fwd")
    mix1 = mm(oc, wts["w_out1"], "NN", tm=tm, tn=512, tk=d, name="l1_out_proj")
    h1a = _ln_fwd(h0b, mix1, wts["ln_mix_g"][1], wts["ln_mix_b"][1], name="ln_mix1_fwd")
    u1 = mm(h1a, wts["w1"][1], "NN", tm=tm, tn=512, tk=d, b_dev=True, name="mlp1_up")
    y1 = mm(u1, wts["w2"][1], "NN", tm=tm, tn=512, tk=d, a_fn="relu2", name="mlp1_down")
    h1b = _ln_fwd(h1a, y1, wts["ln_ffn_g"][1], wts["ln_ffn_b"][1], name="ln_ffn1_fwd")
    dy, loss_vec = _loss_head(h1b, target, name="loss_head")

    def mlp_bwd(layer, h_in, u, dpre):
        du = mm(dpre, wts["w2"][layer], "NT", tm=tm, tn=512, tk=d, epi="relu2grad", c=u, name=f"mlp{layer}_d_hidden")
        dw2 = mm(u, dpre, "TN", tm=1024, tn=1024, tk=tkl, a_fn="relu2", name=f"mlp{layer}_dw2")
        dw1 = mm(h_in, du, "TN", tm=1024, tn=512, tk=tkl, out_dev=True, name=f"mlp{layer}_dw1")
        dh = mm(du, wts["w1"][layer], "NT", tm=tm, tn=1024, tk=512, b_dev=True, epi="add", c=dpre, scale=DN_ALPHA,
                name=f"mlp{layer}_d_in")
        return dh, dw1, dw2

    ln_ffn_dg, ln_ffn_db, ln_mix_dg, ln_mix_db, dw1s, dw2s = [None, None], [None, None], [None, None], [None, None], [None, None], [None, None]
    dpre, ln_ffn_dg[1], ln_ffn_db[1] = _ln_bwd(h1a, y1, wts["ln_ffn_g"][1], dy, name="ln_ffn1_bwd")
    dh1a, dw1s[1], dw2s[1] = mlp_bwd(1, h1a, u1, dpre)
    dpre, ln_mix_dg[1], ln_mix_db[1] = _ln_bwd(h0b, mix1, wts["ln_mix_g"][1], dh1a, name="ln_mix1_bwd")
    g["c_w_out"] = mm(oc, dpre, "TN", tm=1024, tn=1024, tk=tkl, name="l1_dw_out")
    doc = mm(dpre, wts["w_out1"], "NT", tm=tm, tn=512, tk=d, name="l1_d_gate")
    doc_raw, dz1, g["c_gn"] = _gate_bwd(oc_raw, p1, 3 * HG_HEADS, wts["c_gn"], doc, heads=HG_HEADS, name="hg_gate_bwd")
    dq1, df1, di1, g["lb"] = _hg_bwd(p1, wts["lb"], hg_states, doc_raw, pad, name="hg_bwd")
    dp1 = jnp.concatenate([dq1, df1, di1, dz1], axis=1)
    g["c_w_in"] = mm(h0b, dp1, "TN", tm=1024, tn=512, tk=tkl, out_dev=True, name="l1_dw_in")
    dh0b = mm(dp1, wts["w_c"], "NT", tm=tm, tn=1024, tk=512, b_dev=True, epi="add", c=dpre, scale=DN_ALPHA,
              name="l1_d_in")
    dpre, ln_ffn_dg[0], ln_ffn_db[0] = _ln_bwd(h0a, y0, wts["ln_ffn_g"][0], dh0b, name="ln_ffn0_bwd")
    dh0a, dw1s[0], dw2s[0] = mlp_bwd(0, h0a, u0, dpre)
    dpre, ln_mix_dg[0], ln_mix_db[0] = _ln_bwd(h0, mix0, wts["ln_mix_g"][0], dh0a, name="ln_mix0_bwd")
    g["ab_w_out"] = mm(oab, dpre, "TN", tm=1024, tn=1024, tk=tkl, name="l0_dw_out")
    doab = mm(dpre, wts["w_out0"], "NT", tm=tm, tn=512, tk=d, name="l0_d_gate")
    doa_raw, dz0, g["ab_gn"] = _gate_bwd(oa_raw, p0, AB_Z // HEAD_W, wts["ab_gn"], doab, heads=GDN_HEADS,
                                         name="gdn_gate_bwd")
    dqb, dkb, dvb = _sb_bwd(p0, sb_tot, doab, GDN_HEADS, pad, name="sb_bwd")
    dqn, dkn, dvn, dba, g["alog_v"], g["dtb_v"] = _gdn_bwd(qkv, p0, wts["alog_v"], wts["dtb_v"], gdn_states, doa_raw,
                                                           pad, name="gdn_bwd")
    dconv_in, g["conv_w"] = _gdn_pre_bwd(p0, wts["conv_w"], jnp.concatenate([dqn, dkn, dvn], axis=1), pad,
                                         name="gdn_pre_bwd")
    dp0 = jnp.concatenate([dconv_in, dz0, dqb, dkb, dvb, dba, jnp.zeros((lp, AB_CAT - AB_BA - HEAD_W), F32)], axis=1)
    g["w_ab"] = mm(h0, dp0, "TN", tm=1024, tn=768, tk=tkl, name="l0_dw_in")
    dh0 = mm(dp0, wts["w_ab"], "NT", tm=tm, tn=1024, tk=768, epi="add", c=dpre, scale=DN_ALPHA, name="l0_d_in")

    g["w1"], g["w2"] = dw1s, dw2s
    g["ln_mix_g"] = jnp.concatenate(ln_mix_dg, axis=0)
    g["ln_mix_b"] = jnp.concatenate(ln_mix_db, axis=0)
    g["ln_ffn_g"] = jnp.concatenate(ln_ffn_dg, axis=0)
    g["ln_ffn_b"] = jnp.concatenate(ln_ffn_db, axis=0)
    return loss_vec, dh0, g


def _my_index():
    return 4 * lax.axis_index("x") + 2 * lax.axis_index("y") + lax.axis_index("c")


def _exchange(srcs, dtypes, *, scatter, name):
    n = len(srcs)
    blocks = [s.shape[1:] if scatter else s.shape for s in srcs]

    def body(*refs):
        ins, outs = refs[:n], refs[n:2 * n]
        rest = refs[2 * n:]
        stages = rest[:n] if not scatter else [None] * n
        send_sems, recv_sems, local_sems = rest[-3:]
        me = _my_index()
        pending = []
        for i in range(n):
            if scatter:
                mine = ins[i].at[me]
            else:
                stages[i][...] = ins[i][...].astype(dtypes[i])
                mine = stages[i]
            loc = pltpu.make_async_copy(mine, outs[i].at[me], local_sems.at[i])
            loc.start()
            pending.append(loc)
            for k in range(1, N_DEV):
                peer = jnp.bitwise_xor(me, k)
                dev = (peer // 4, (peer // 2) % 2, peer % 2)
                cp = pltpu.make_async_remote_copy(
                    src_ref=ins[i].at[peer] if scatter else stages[i], dst_ref=outs[i].at[me],
                    send_sem=send_sems.at[i, k - 1], recv_sem=recv_sems.at[i, k - 1],
                    device_id=dev, device_id_type=pl.DeviceIdType.MESH)
                cp.start()
                pending.append(cp)
        for cp in pending:
            cp.wait()

    any_spec = pl.BlockSpec(memory_space=pl.ANY)
    vmem_spec = pl.BlockSpec(memory_space=pltpu.VMEM)
    scratch = [] if scatter else [pltpu.VMEM(b, dt) for b, dt in zip(blocks, dtypes)]
    scratch += [pltpu.SemaphoreType.DMA((n, N_DEV - 1)), pltpu.SemaphoreType.DMA((n, N_DEV - 1)),
                pltpu.SemaphoreType.DMA((n,))]
    return pl.pallas_call(
        body, name=name,
        in_specs=[any_spec if scatter else vmem_spec] * n, out_specs=[any_spec] * n,
        out_shape=[jax.ShapeDtypeStruct((N_DEV, *b), dt) for b, dt in zip(blocks, dtypes)],
        scratch_shapes=scratch, compiler_params=_cp(has_side_effects=True),
    )(*srcs)


def _adamw(w, parts, m, v, *, name):
    r, c = w.shape
    s = parts.shape[0]
    tm = _row_tile(r, 128) if r % 8 == 0 else r
    c1 = 1.0 - ADAM_B1 ** ADAM_STEP
    c2 = 1.0 - ADAM_B2 ** ADAM_STEP

    def body(w_ref, p_ref, m_ref, v_ref, g_ref, d_ref, m2_ref, v2_ref):
        g = p_ref[0]
        for j in range(1, s):
            g = g + p_ref[j]
        m2 = ADAM_B1 * m_ref[...] + (1.0 - ADAM_B1) * g
        v2 = ADAM_B2 * v_ref[...] + (1.0 - ADAM_B2) * jnp.square(g)
        g_ref[...] = g
        m2_ref[...] = m2
        v2_ref[...] = v2
        d_ref[...] = -ADAM_LR * ((m2 / c1) / (jnp.sqrt(v2 / c2) + ADAM_EPS) + ADAM_WD * w_ref[...])

    blk = pl.BlockSpec((tm, c), lambda i: (i, 0))
    return pl.pallas_call(
        body, name=name, grid=(r // tm,),
        in_specs=[blk, pl.BlockSpec((s, tm, c), lambda i: (0, i, 0)), blk, blk], out_specs=[blk] * 4,
        out_shape=[jax.ShapeDtypeStruct((r, c), F32)] * 4, compiler_params=_cp(("parallel",)),
    )(w, parts, m, v)


_WEIGHTS = ("meta_tokens", "ab_w_in", "ab_conv_w", "ab_a_log", "ab_dt_bias", "ab_gnorm_g", "ab_w_out", "c_w_in",
            "c_lb_raw", "c_gnorm_g", "c_w_out", "ln_mix_g", "ln_mix_b", "mlp_w1", "mlp_w2", "ln_ffn_g", "ln_ffn_b")
_PACK_ROWS = (("ln_mix_g", 0), ("ln_mix_b", 2), ("ln_ffn_g", 4), ("ln_ffn_b", 6), ("c_lb_raw", 8))
_PACK_MISC_ROW = 10
_PACK_MISC = (("ab_gnorm_g", 0, 128), ("c_gnorm_g", 128, 128), ("ab_a_log", 256, GDN_HEADS), ("ab_dt_bias", 260, GDN_HEADS))
_PACK_N = 16
_SMALL_META = 16
_SMALL_CONV = 32
_SMALL_N = 40


def _pack_replicated(p):
    rows = jnp.zeros((_PACK_N, D_MODEL), F32)
    for name, r0 in _PACK_ROWS:
        rows = rows.at[r0:r0 + 2].set(p[name])
    for name, c0, width in _PACK_MISC:
        rows = rows.at[_PACK_MISC_ROW, c0:c0 + width].set(p[name].reshape(width))
    return rows


def _unpack_replicated(rows, like):
    out = {}
    for name, r0 in _PACK_ROWS:
        out[name] = rows[r0:r0 + 2]
    for name, c0, width in _PACK_MISC:
        out[name] = rows[_PACK_MISC_ROW, c0:c0 + width].reshape(like[name].shape)
    return out


def _lower_bound(c_lb_raw):
    lb_all = jnp.cumsum(jax.nn.softmax(c_lb_raw.astype(F32), axis=0), axis=0)
    return (lb_all - lb_all[0:1])[1].reshape(1, -1)


def kernel(x, meta_tokens, ab_w_in, ab_conv_w, ab_a_log, ab_dt_bias, ab_gnorm_g, ab_w_out, c_w_in, c_lb_raw, c_gnorm_g, c_w_out, ln_mix_g, ln_mix_b, mlp_w1, mlp_w2, ln_ffn_g, ln_ffn_b, loss_target, m_meta_tokens, m_ab_w_in, m_ab_conv_w, m_ab_a_log, m_ab_dt_bias, m_ab_gnorm_g, m_ab_w_out, m_c_w_in, m_c_lb_raw, m_c_gnorm_g, m_c_w_out, m_ln_mix_g, m_ln_mix_b, m_mlp_w1, m_mlp_w2, m_ln_ffn_g, m_ln_ffn_b, v_meta_tokens, v_ab_w_in, v_ab_conv_w, v_ab_a_log, v_ab_dt_bias, v_ab_gnorm_g, v_ab_w_out, v_c_w_in, v_c_lb_raw, v_c_gnorm_g, v_c_w_out, v_ln_mix_g, v_ln_mix_b, v_mlp_w1, v_mlp_w2, v_ln_ffn_g, v_ln_ffn_b):
    w = dict(zip(_WEIGHTS, (meta_tokens, ab_w_in, ab_conv_w, ab_a_log, ab_dt_bias, ab_gnorm_g, ab_w_out, c_w_in, c_lb_raw,
                            c_gnorm_g, c_w_out, ln_mix_g, ln_mix_b, mlp_w1, mlp_w2, ln_ffn_g, ln_ffn_b)))
    mom = dict(zip(_WEIGHTS, (m_meta_tokens, m_ab_w_in, m_ab_conv_w, m_ab_a_log, m_ab_dt_bias, m_ab_gnorm_g, m_ab_w_out,
                              m_c_w_in, m_c_lb_raw, m_c_gnorm_g, m_c_w_out, m_ln_mix_g, m_ln_mix_b, m_mlp_w1, m_mlp_w2,
                              m_ln_ffn_g, m_ln_ffn_b)))
    var = dict(zip(_WEIGHTS, (v_meta_tokens, v_ab_w_in, v_ab_conv_w, v_ab_a_log, v_ab_dt_bias, v_ab_gnorm_g, v_ab_w_out,
                              v_c_w_in, v_c_lb_raw, v_c_gnorm_g, v_c_w_out, v_ln_mix_g, v_ln_mix_b, v_mlp_w1, v_mlp_w2,
                              v_ln_ffn_g, v_ln_ffn_b)))
    me = _my_index()
    seq = x.shape[1]
    pad = (-(N_META + seq)) % SB_BLOCK
    lp = pad + N_META + seq
    meta_w = D_MODEL // N_DEV
    conv_w_all = 2 * GDN_HEADS * HEAD_W + GDN_HEADS * HEAD_W
    conv_w_mine = conv_w_all // N_DEV

    gathered = _exchange(
        [w["meta_tokens"], w["ab_conv_w"][0], w["ab_w_in"][0], w["ab_w_out"][0], w["c_w_in"][0], w["c_w_out"][0],
         w["mlp_w1"], w["mlp_w2"]],
        [F32, F32, BF16, BF16, BF16, BF16, BF16, BF16], scatter=False, name="gather_weights")
    g_meta, g_conv, g_ab_in, g_ab_out, g_c_in, g_c_out, g_w1, g_w2 = gathered
    meta_full = g_meta.transpose(1, 0, 2).reshape(N_META, D_MODEL)
    conv_full = g_conv.transpose(1, 0, 2).reshape(CONV_K, conv_w_all)
    ab_full = g_ab_in.transpose(1, 0, 2).reshape(D_MODEL, AB_IN)
    ba0 = AB_Z + 512
    w_ab = jnp.concatenate([ab_full[:, :ba0], ab_full[:, ba0 + 2 * GDN_HEADS:], ab_full[:, ba0:ba0 + 2 * GDN_HEADS],
                            jnp.zeros((D_MODEL, AB_CAT - AB_IN), BF16)], axis=1)
    vec128 = lambda p: jnp.zeros((1, HEAD_W), F32).at[0, :GDN_HEADS].set(p.reshape(GDN_HEADS))
    wts = dict(
        w_ab=w_ab, conv_w=conv_full, alog_v=vec128(w["ab_a_log"]), dtb_v=vec128(w["ab_dt_bias"]),
        ab_gn=w["ab_gnorm_g"][0], w_out0=g_ab_out.reshape(D_MODEL, D_MODEL), w_c=g_c_in,
        lb=_lower_bound(w["c_lb_raw"]), c_gn=w["c_gnorm_g"][0], w_out1=g_c_out.reshape(D_MODEL, D_MODEL),
        w1=[g_w1[:, l] for l in range(DEPTH)], w2=[g_w2[:, l].reshape(D_FF, D_MODEL) for l in range(DEPTH)],
        ln_mix_g=w["ln_mix_g"], ln_mix_b=w["ln_mix_b"], ln_ffn_g=w["ln_ffn_g"], ln_ffn_b=w["ln_ffn_b"])

    h0 = jnp.concatenate([jnp.zeros((pad, D_MODEL), F32), meta_full, x[0]], axis=0)
    loss_vec, dh0, g = _local_step(h0, loss_target[0], pad, wts)
    loss = lax.psum(jnp.sum(loss_vec), ("x", "y", "c"))
    grad_x = dh0[lp - seq:][None]

    _, lb_vjp = jax.vjp(_lower_bound, w["c_lb_raw"])
    rep_part = _pack_replicated(dict(
        ln_mix_g=g["ln_mix_g"], ln_mix_b=g["ln_mix_b"], ln_ffn_g=g["ln_ffn_g"], ln_ffn_b=g["ln_ffn_b"],
        c_lb_raw=lb_vjp(g["lb"])[0], ab_gnorm_g=g["ab_gn"], c_gnorm_g=g["c_gn"],
        ab_a_log=g["alog_v"][0, :GDN_HEADS], ab_dt_bias=g["dtb_v"][0, :GDN_HEADS]))
    small = jnp.concatenate([rep_part, dh0[pad:pad + N_META], g["conv_w"].reshape(-1, D_MODEL),
                             jnp.zeros((_SMALL_N - _SMALL_CONV - CONV_K * conv_w_all // D_MODEL, D_MODEL), F32)], axis=0)
    (small_all,) = _exchange([small], [F32], scatter=False, name="gather_small_grads")
    rep_out = _adamw(_pack_replicated(w), small_all[:, :_PACK_N], _pack_replicated(mom), _pack_replicated(var),
                     name="adamw_replicated")
    meta_parts = lax.dynamic_slice_in_dim(small_all[:, _SMALL_META:_SMALL_META + N_META], me * meta_w, meta_w, axis=2)
    meta_out = _adamw(w["meta_tokens"], meta_parts, mom["meta_tokens"], var["meta_tokens"], name="adamw_meta")
    conv_parts = small_all[:, _SMALL_CONV:_SMALL_CONV + CONV_K * conv_w_all // D_MODEL].reshape(N_DEV, CONV_K, conv_w_all)
    conv_parts = lax.dynamic_slice_in_dim(conv_parts, me * conv_w_mine, conv_w_mine, axis=2)
    conv_out = _adamw(w["ab_conv_w"][0], conv_parts, mom["ab_conv_w"][0], var["ab_conv_w"][0], name="adamw_conv")

    gab = g["w_ab"]
    gab = jnp.concatenate([gab[:, :ba0], gab[:, AB_BA:AB_BA + 2 * GDN_HEADS], gab[:, ba0:AB_BA]], axis=1)
    big = [("ab_w_in", None, gab.reshape(D_MODEL, N_DEV, AB_IN // N_DEV).transpose(1, 0, 2)),
           ("ab_w_out", None, g["ab_w_out"].reshape(N_DEV, D_MODEL // N_DEV, D_MODEL)),
           ("c_w_in", None, g["c_w_in"]),
           ("c_w_out", None, g["c_w_out"].reshape(N_DEV, D_MODEL // N_DEV, D_MODEL))]
    for l in range(DEPTH):
        big.append(("mlp_w1", l, g["w1"][l]))
    for l in range(DEPTH):
        big.append(("mlp_w2", l, g["w2"][l].reshape(N_DEV, D_FF // N_DEV, D_MODEL)))
    parts = _exchange([b[2] for b in big], [F32] * len(big), scatter=True, name="scatter_grads")
    big_out = {}
    for (name, l, _), p in zip(big, parts):
        sel = (lambda a: a[0]) if l is None else (lambda a, l=l: a[l])
        res = _adamw(sel(w[name]), p, sel(mom[name]), sel(var[name]), name=f"adamw_{name}" + ("" if l is None else str(l)))
        big_out.setdefault(name, []).append(res)

    rep = [_unpack_replicated(r, w) for r in rep_out]
    outs = {}
    for name in _WEIGHTS:
        if name == "meta_tokens":
            outs[name] = list(meta_out)
        elif name == "ab_conv_w":
            outs[name] = [o[None] for o in conv_out]
        elif name in big_out:
            res = big_out[name]
            outs[name] = [o[None] for o in res[0]] if len(res) == 1 else [jnp.stack(pair) for pair in zip(*res)]
        else:
            outs[name] = [r[name] for r in rep]
    flat = [loss, grad_x]
    for kind in range(4):
        flat += [outs[name][kind] for name in _WEIGHTS]
    return tuple(flat)
```

```python
import functools
import math

import jax
import jax.numpy as jnp
from jax import lax
from jax.experimental import pallas as pl
from jax.experimental.pallas import tpu as pltpu

F32 = jnp.float32
BF16 = jnp.bfloat16
HI = lax.Precision.HIGHEST

N_DEV = 8
D_MODEL = 1024
N_META = 16
D_FF = 4096
DEPTH = 2
GDN_HEADS = 4
SB_HEADS = 8
SB_DH = 64
HG_HEADS = 8
HEAD_W = 128
CHUNK = 64
SB_BLOCK = 128
CONV_K = 4
DN_ALPHA = float((2 * DEPTH) ** 0.25)
LN_EPS = 1e-5
RMS_EPS = 1e-6
L2_EPS = 1e-6
ADAM_LR, ADAM_B1, ADAM_B2, ADAM_EPS, ADAM_WD, ADAM_STEP = 0.001, 0.9, 0.999, 1e-08, 0.01, 10

AB_QKV = 0
AB_Z = 1536
AB_SB = 2048
AB_BA = 3584
AB_CAT = 3840
AB_IN = 3592

VMEM_LIMIT = 56 * 1024 * 1024


def _cp(sem=None, **kw):
    if sem is not None:
        kw["dimension_semantics"] = sem
    return pltpu.CompilerParams(vmem_limit_bytes=VMEM_LIMIT, **kw)


def _row_tile(n, want):
    best = 8
    for t in range(8, min(n, want) + 1, 8):
        if n % t == 0:
            best = t
    return best


@jax.custom_vjp
def _sigmoid(x):
    e = jnp.exp(-jnp.abs(x))
    r = 1.0 / (1.0 + e)
    return jnp.where(x >= 0, r, e * r)


def _sigmoid_fwd(x):
    s = _sigmoid(x)
    return s, s


def _sigmoid_bwd(s, g):
    return (g * s * (1.0 - s),)


_sigmoid.defvjp(_sigmoid_fwd, _sigmoid_bwd)


def _log1p_exp_neg_abs(x):
    e = jnp.exp(-jnp.abs(x))
    return jnp.where(e < 1e-4, e - 0.5 * e * e, jnp.log(1.0 + e))


@jax.custom_vjp
def _softplus(x):
    return jnp.maximum(x, 0.0) + _log1p_exp_neg_abs(x)


def _softplus_fwd(x):
    return _softplus(x), x


def _softplus_bwd(x, g):
    return (g * _sigmoid(x),)


_softplus.defvjp(_softplus_fwd, _softplus_bwd)


def _silu(x):
    return x * _sigmoid(x)


def _silu_grad(x):
    s = _sigmoid(x)
    return s * (1.0 + x * (1.0 - s))


def _dot(a, b, dims, precision=None):
    return lax.dot_general(a, b, (dims, ((), ())), precision=precision, preferred_element_type=F32)


NN = ((1,), (0,))
NT = ((1,), (1,))
TN = ((0,), (0,))


def _bdot(a, b, dims):
    return _dot(a.astype(BF16), b.astype(BF16), dims)


def _mm(a, b, mode, *, tm, tn, tk, name, a_fn=None, epi=None, c=None, scale=1.0, b_dev=False, out_dev=False):
    if mode == "NN":
        m, kk = a.shape
        n = b.shape[2] * N_DEV if b_dev else b.shape[1]
    elif mode == "NT":
        m, kk = a.shape
        n = b.shape[1] if b_dev else b.shape[0]
    else:
        kk, m = a.shape
        n = b.shape[1]
    assert m % tm == 0 and n % tn == 0 and kk % tk == 0, (name, m, n, kk, tm, tn, tk)
    nk = kk // tk
    dims = {"NN": NN, "NT": NT, "TN": TN}[mode]

    if mode == "TN":
        a_spec = pl.BlockSpec((tk, tm), lambda i, j, k: (k, i))
    else:
        a_spec = pl.BlockSpec((tm, tk), lambda i, j, k: (i, k))
    if mode == "NN":
        if b_dev:
            assert tn == b.shape[2]
            b_spec = pl.BlockSpec((None, tk, tn), lambda i, j, k: (j, k, 0))
        else:
            b_spec = pl.BlockSpec((tk, tn), lambda i, j, k: (k, j))
    elif mode == "NT":
        if b_dev:
            assert tk == b.shape[2]
            b_spec = pl.BlockSpec((None, tn, tk), lambda i, j, k: (k, j, 0))
        else:
            b_spec = pl.BlockSpec((tn, tk), lambda i, j, k: (j, k))
    else:
        b_spec = pl.BlockSpec((tk, tn), lambda i, j, k: (k, j))
    in_specs = [a_spec, b_spec]
    operands = [a, b]
    if epi is not None:
        in_specs.append(pl.BlockSpec((tm, tn), lambda i, j, k: (i, j)))
        operands.append(c)
    if out_dev:
        assert tn == n // N_DEV
        out_shape = jax.ShapeDtypeStruct((N_DEV, m, tn), F32)
        out_spec = pl.BlockSpec((None, tm, tn), lambda i, j, k: (j, i, 0))
    else:
        out_shape = jax.ShapeDtypeStruct((m, n), F32)
        out_spec = pl.BlockSpec((tm, tn), lambda i, j, k: (i, j))

    def body(*refs):
        a_ref, b_ref = refs[0], refs[1]
        c_ref = refs[2] if epi is not None else None
        o_ref = refs[3] if epi is not None else refs[2]
        acc_ref = refs[-1] if nk > 1 else None
        av = a_ref[...]
        if a_fn == "relu2":
            av = jnp.square(jnp.maximum(av, 0.0))
        p = _dot(av.astype(BF16), b_ref[...].astype(BF16), dims)

        def finish(acc):
            if epi == "add":
                acc = acc + scale * c_ref[...]
            elif epi == "relu2grad":
                acc = acc * (2.0 * jnp.maximum(c_ref[...], 0.0))
            o_ref[...] = acc

        if nk == 1:
            finish(p)
        else:
            k = pl.program_id(2)

            @pl.when(k == 0)
            def _():
                acc_ref[...] = p

            @pl.when(k > 0)
            def _():
                acc_ref[...] += p

            @pl.when(k == nk - 1)
            def _():
                finish(acc_ref[...])

    return pl.pallas_call(
        body, name=name, grid=(m // tm, n // tn, nk), in_specs=in_specs, out_specs=out_spec, out_shape=out_shape,
        scratch_shapes=[pltpu.VMEM((tm, tn), F32)] if nk > 1 else [],
        compiler_params=_cp(("parallel", "parallel", "arbitrary")),
    )(*operands)


def _ln_fwd(a, b, g, beta, *, name):
    lp, d = a.shape
    tm = _row_tile(lp, 512)

    def body(a_ref, b_ref, g_ref, be_ref, y_ref):
        pre = DN_ALPHA * a_ref[...] + b_ref[...]
        mu = jnp.mean(pre, axis=-1, keepdims=True)
        xc = pre - mu
        var = jnp.mean(xc * xc, axis=-1, keepdims=True)
        y_ref[...] = xc * lax.rsqrt(var + LN_EPS) * g_ref[...] + be_ref[...]

    row = pl.BlockSpec((tm, d), lambda i: (i, 0))
    vec = pl.BlockSpec((1, d), lambda i: (0, 0))
    return pl.pallas_call(
        body, name=name, grid=(lp // tm,), in_specs=[row, row, vec, vec], out_specs=row,
        out_shape=jax.ShapeDtypeStruct((lp, d), F32), compiler_params=_cp(("parallel",)),
    )(a, b, g.reshape(1, d), beta.reshape(1, d))


def _ln_bwd(a, b, g, dy, *, name):
    lp, d = a.shape
    tm = _row_tile(lp, 512)

    def body(a_ref, b_ref, g_ref, dy_ref, dpre_ref, dg_ref, db_ref):
        pre = DN_ALPHA * a_ref[...] + b_ref[...]
        mu = jnp.mean(pre, axis=-1, keepdims=True)
        xc = pre - mu
        var = jnp.mean(xc * xc, axis=-1, keepdims=True)
        rstd = lax.rsqrt(var + LN_EPS)
        xhat = xc * rstd
        dyv = dy_ref[...]
        dxh = dyv * g_ref[...]
        m1 = jnp.mean(dxh, axis=-1, keepdims=True)
        m2 = jnp.mean(dxh * xhat, axis=-1, keepdims=True)
        dpre_ref[...] = rstd * (dxh - m1 - xhat * m2)

        @pl.when(pl.program_id(0) == 0)
        def _():
            dg_ref[...] = jnp.zeros_like(dg_ref)
            db_ref[...] = jnp.zeros_like(db_ref)

        dg_ref[...] += jnp.sum(dyv * xhat, axis=0, keepdims=True)
        db_ref[...] += jnp.sum(dyv, axis=0, keepdims=True)

    row = pl.BlockSpec((tm, d), lambda i: (i, 0))
    vec = pl.BlockSpec((1, d), lambda i: (0, 0))
    return pl.pallas_call(
        body, name=name, grid=(lp // tm,), in_specs=[row, row, vec, row], out_specs=[row, vec, vec],
        out_shape=[jax.ShapeDtypeStruct((lp, d), F32), jax.ShapeDtypeStruct((1, d), F32),
                   jax.ShapeDtypeStruct((1, d), F32)],
        compiler_params=_cp(("arbitrary",)),
    )(a, b, g.reshape(1, d), dy)


def _loss_head(y, target, *, name):
    lp, d = y.shape
    seq = target.shape[0]
    tm = SB_BLOCK
    first = (lp - seq) // tm
    assert (lp - seq) % tm == 0 and seq % tm == 0

    def body(y_ref, t_ref, dy_ref, loss_ref):
        i = pl.program_id(0)
        live = i >= first
        diff = jnp.where(live, y_ref[...] - t_ref[...], 0.0)
        dy_ref[...] = diff * (1.0 / d)

        @pl.when(i == 0)
        def _():
            loss_ref[...] = jnp.zeros_like(loss_ref)

        loss_ref[...] += jnp.sum(diff * diff, axis=0, keepdims=True) * (0.5 / d)

    return pl.pallas_call(
        body, name=name, grid=(lp // tm,),
        in_specs=[pl.BlockSpec((tm, d), lambda i: (i, 0)),
                  pl.BlockSpec((tm, d), lambda i: (jnp.maximum(i - first, 0), 0))],
        out_specs=[pl.BlockSpec((tm, d), lambda i: (i, 0)), pl.BlockSpec((1, d), lambda i: (0, 0))],
        out_shape=[jax.ShapeDtypeStruct((lp, d), F32), jax.ShapeDtypeStruct((1, d), F32)],
        compiler_params=_cp(("arbitrary",)),
    )(y, target)


def _gate_fwd(o, zsrc, z_blk0, g, other, *, heads, name):
    lp = o.shape[0]
    tm = _row_tile(lp, 512)
    nblk = D_MODEL // HEAD_W

    def body(o_ref, z_ref, g_ref, x_ref, y_ref):
        h = pl.program_id(1)

        @pl.when(h < heads)
        def _():
            ov = o_ref[...]
            r = lax.rsqrt(jnp.mean(ov * ov, axis=-1, keepdims=True) + RMS_EPS)
            y_ref[...] = ov * r * g_ref[...] * _silu(z_ref[...])

        @pl.when(h >= heads)
        def _():
            y_ref[...] = x_ref[...]

    other_w = other.shape[1] // HEAD_W
    return pl.pallas_call(
        body, name=name, grid=(lp // tm, nblk),
        in_specs=[pl.BlockSpec((tm, HEAD_W), lambda i, h: (i, jnp.minimum(h, heads - 1))),
                  pl.BlockSpec((tm, HEAD_W), lambda i, h: (i, z_blk0 + jnp.minimum(h, heads - 1))),
                  pl.BlockSpec((1, HEAD_W), lambda i, h: (0, 0)),
                  pl.BlockSpec((tm, HEAD_W), lambda i, h: (i, jnp.clip(h - heads, 0, other_w - 1)))],
        out_specs=pl.BlockSpec((tm, HEAD_W), lambda i, h: (i, h)),
        out_shape=jax.ShapeDtypeStruct((lp, D_MODEL), F32),
        compiler_params=_cp(("parallel", "arbitrary")),
    )(o, zsrc, g.reshape(1, HEAD_W), other)


def _gate_bwd(o, zsrc, z_blk0, g, dy, *, heads, name):
    lp = o.shape[0]
    tm = _row_tile(lp, 512)

    def body(o_ref, z_ref, g_ref, dy_ref, do_ref, dz_ref, dg_ref):
        ov, zv, gv, dyv = o_ref[...], z_ref[...], g_ref[...], dy_ref[...]
        r = lax.rsqrt(jnp.mean(ov * ov, axis=-1, keepdims=True) + RMS_EPS)
        nrm = ov * r
        s = _silu(zv)
        dn = dyv * gv * s
        do_ref[...] = r * (dn - nrm * jnp.mean(dn * nrm, axis=-1, keepdims=True))
        dz_ref[...] = dyv * nrm * gv * _silu_grad(zv)

        @pl.when((pl.program_id(0) == 0) & (pl.program_id(1) == 0))
        def _():
            dg_ref[...] = jnp.zeros_like(dg_ref)

        dg_ref[...] += jnp.sum(dyv * nrm * s, axis=0, keepdims=True)

    blk = pl.BlockSpec((tm, HEAD_W), lambda i, h: (i, h))
    return pl.pallas_call(
        body, name=name, grid=(lp // tm, heads),
        in_specs=[blk, pl.BlockSpec((tm, HEAD_W), lambda i, h: (i, z_blk0 + h)),
                  pl.BlockSpec((1, HEAD_W), lambda i, h: (0, 0)), blk],
        out_specs=[blk, blk, pl.BlockSpec((1, HEAD_W), lambda i, h: (0, 0))],
        out_shape=[jax.ShapeDtypeStruct((lp, heads * HEAD_W), F32), jax.ShapeDtypeStruct((lp, heads * HEAD_W), F32),
                   jax.ShapeDtypeStruct((1, HEAD_W), F32)],
        compiler_params=_cp(("arbitrary", "arbitrary")),
    )(o, zsrc, g.reshape(1, HEAD_W), dy)


def _conv_taps(x, w):
    acc = w[CONV_K - 1:CONV_K, :] * x
    for k in range(CONV_K - 1):
        acc = acc + w[k:k + 1, :] * pltpu.roll(x, CONV_K - 1 - k, 0)
    return acc


def _gdn_pre_fwd(p0, conv_w, pad, *, name):
    lp = p0.shape[0]
    nq = GDN_HEADS
    qscale = HEAD_W ** -0.5

    def body(x_ref, w_ref, y_ref):
        j = pl.program_id(0)
        c = _conv_taps(x_ref[...], w_ref[...])
        s = _silu(c)
        r = lax.rsqrt(jnp.sum(s * s, axis=-1, keepdims=True) + L2_EPS)
        mult = jnp.where(j < nq, r * qscale, jnp.where(j < 2 * nq, r, 1.0))
        rows = lax.broadcasted_iota(jnp.int32, (lp, 1), 0)
        y_ref[...] = jnp.where(rows >= pad, s * mult, 0.0)

    return pl.pallas_call(
        body, name=name, grid=(3 * nq,),
        in_specs=[pl.BlockSpec((lp, HEAD_W), lambda j: (0, j)), pl.BlockSpec((CONV_K, HEAD_W), lambda j: (0, j))],
        out_specs=pl.BlockSpec((lp, HEAD_W), lambda j: (0, j)),
        out_shape=jax.ShapeDtypeStruct((lp, 3 * nq * HEAD_W), F32), compiler_params=_cp(("parallel",)),
    )(p0, conv_w)


def _gdn_pre_bwd(p0, conv_w, dqkv, pad, *, name):
    lp = p0.shape[0]
    nq = GDN_HEADS
    qscale = HEAD_W ** -0.5

    def body(x_ref, w_ref, dy_ref, dx_ref, dw_ref):
        j = pl.program_id(0)
        x, w = x_ref[...], w_ref[...]
        c = _conv_taps(x, w)
        s = _silu(c)
        r = lax.rsqrt(jnp.sum(s * s, axis=-1, keepdims=True) + L2_EPS)
        rows = lax.broadcasted_iota(jnp.int32, (lp, 1), 0)
        dy = jnp.where(rows >= pad, dy_ref[...], 0.0)
        nrm = s * r
        dn = dy * jnp.where(j < nq, qscale, 1.0)
        ds_norm = r * (dn - nrm * jnp.sum(nrm * dn, axis=-1, keepdims=True))
        ds = jnp.where(j < 2 * nq, ds_norm, dy)
        dc = ds * _silu_grad(c)
        dx = w[CONV_K - 1:CONV_K, :] * dc
        dws = [None] * CONV_K
        dws[CONV_K - 1] = jnp.sum(dc * x, axis=0, keepdims=True)
        for k in range(CONV_K - 1):
            sh = CONV_K - 1 - k
            dx = dx + w[k:k + 1, :] * pltpu.roll(dc, lp - sh, 0)
            dws[k] = jnp.sum(dc * pltpu.roll(x, sh, 0), axis=0, keepdims=True)
        dx_ref[...] = dx
        dw_ref[...] = jnp.concatenate(dws, axis=0)

    blk = pl.BlockSpec((lp, HEAD_W), lambda j: (0, j))
    wblk = pl.BlockSpec((CONV_K, HEAD_W), lambda j: (0, j))
    return pl.pallas_call(
        body, name=name, grid=(3 * nq,), in_specs=[blk, wblk, blk], out_specs=[blk, wblk],
        out_shape=[jax.ShapeDtypeStruct((lp, 3 * nq * HEAD_W), F32),
                   jax.ShapeDtypeStruct((CONV_K, 3 * nq * HEAD_W), F32)],
        compiler_params=_cp(("parallel",)),
    )(p0, conv_w, dqkv)


def _tri(c, strict):
    r = lax.broadcasted_iota(jnp.int32, (c, c), 0)
    q = lax.broadcasted_iota(jnp.int32, (c, c), 1)
    return (q < r) if strict else (q <= r)


@jax.custom_vjp
def _inv_unit_lower(m):
    c = m.shape[0]
    eye = (lax.broadcasted_iota(jnp.int32, (c, c), 0) == lax.broadcasted_iota(jnp.int32, (c, c), 1)).astype(F32)
    x = eye - m
    p = m
    n = 2
    while n < c:
        p = _dot(p, p, NN, HI)
        x = x + _dot(x, p, NN, HI)
        n *= 2
    return x


def _inv_fwd(m):
    t = _inv_unit_lower(m)
    return t, t


def _inv_bwd(t, g):
    return (-_dot(_dot(t, g, TN, HI), t, NT, HI),)


_inv_unit_lower.defvjp(_inv_fwd, _inv_bwd)


def _gdn_chunk(q, k, v, ba, alog, dtb, s, head, valid):
    c = q.shape[0]
    lane = lax.broadcasted_iota(jnp.int32, (1, HEAD_W), 1)
    pick = lambda x, l: jnp.sum(jnp.where(lane == l, x, 0.0), axis=-1, keepdims=True)
    beta = jnp.where(valid, _sigmoid(pick(ba, head)), 0.0)
    g = jnp.where(valid, -jnp.exp(pick(alog, head)) * _softplus(pick(ba, GDN_HEADS + head) + pick(dtb, head)), 0.0)
    causal, strict = _tri(c, False), _tri(c, True)
    gcb = _dot(causal.astype(F32), g * jnp.ones((1, HEAD_W), F32), NN, HI)
    gc_col = gcb[:, :c]
    gc_row = gcb.T[:c, :]
    decay = jnp.where(causal, jnp.exp(jnp.minimum(gc_col - gc_row, 0.0)), 0.0)
    egc = jnp.exp(gcb)
    kb = k * beta
    m = jnp.where(strict, _dot(kb, k, NT, HI) * decay, 0.0)
    t = _inv_unit_lower(m)
    u = _dot(t, v * beta, NN, HI)
    w = _dot(t, kb * egc, NN, HI)
    a = _bdot(q, k, NT) * decay
    gl = gcb[c - 1:c, :]
    v_new = u - _bdot(w, s, NN)
    o = _bdot(q * egc, s, NN) + _bdot(a, v_new, NN)
    s2 = s * jnp.exp(gl) + _bdot(k * jnp.exp(gl - gcb), v_new, TN)
    return o, s2


def _gdn_fwd(qkv, p0, alog_v, dtb_v, pad, *, name):
    lp = qkv.shape[0]
    n = lp // CHUNK
    nh = GDN_HEADS

    def body(q_ref, k_ref, v_ref, ba_ref, al_ref, dt_ref, o_ref, st_ref, s_ref):
        i = pl.program_id(0)

        @pl.when(i == 0)
        def _():
            s_ref[...] = jnp.zeros_like(s_ref)

        valid = (i * CHUNK + lax.broadcasted_iota(jnp.int32, (CHUNK, 1), 0)) >= pad
        for h in range(nh):
            cs = slice(h * HEAD_W, (h + 1) * HEAD_W)
            st_ref[h] = s_ref[h]
            o, s2 = _gdn_chunk(q_ref[:, cs], k_ref[:, cs], v_ref[:, cs], ba_ref[...], al_ref[...], dt_ref[...],
                               s_ref[h], h, valid)
            o_ref[:, cs] = o
            s_ref[h] = s2

    w = nh * HEAD_W
    vec = pl.BlockSpec((1, HEAD_W), lambda i: (0, 0))
    return pl.pallas_call(
        body, name=name, grid=(n,),
        in_specs=[pl.BlockSpec((CHUNK, w), lambda i: (i, 0)), pl.BlockSpec((CHUNK, w), lambda i: (i, 1)),
                  pl.BlockSpec((CHUNK, w), lambda i: (i, 2)), pl.BlockSpec((CHUNK, HEAD_W), lambda i: (i, AB_BA // HEAD_W)),
                  vec, vec],
        out_specs=[pl.BlockSpec((CHUNK, w), lambda i: (i, 0)),
                   pl.BlockSpec((None, nh, HEAD_W, HEAD_W), lambda i: (i, 0, 0, 0))],
        out_shape=[jax.ShapeDtypeStruct((lp, w), F32), jax.ShapeDtypeStruct((n, nh, HEAD_W, HEAD_W), F32)],
        scratch_shapes=[pltpu.VMEM((nh, HEAD_W, HEAD_W), F32)],
        compiler_params=_cp(("arbitrary",)),
    )(qkv, qkv, qkv, p0, alog_v, dtb_v)


def _gdn_bwd(qkv, p0, alog_v, dtb_v, states, do, pad, *, name):
    lp = qkv.shape[0]
    n = lp // CHUNK
    nh = GDN_HEADS

    def body(q_ref, k_ref, v_ref, ba_ref, al_ref, dt_ref, st_ref, do_ref,
             dq_ref, dk_ref, dv_ref, dba_ref, dal_ref, ddt_ref, ds_ref):
        step = pl.program_id(0)
        i = n - 1 - step

        @pl.when(step == 0)
        def _():
            ds_ref[...] = jnp.zeros_like(ds_ref)
            dal_ref[...] = jnp.zeros_like(dal_ref)
            ddt_ref[...] = jnp.zeros_like(ddt_ref)

        valid = (i * CHUNK + lax.broadcasted_iota(jnp.int32, (CHUNK, 1), 0)) >= pad
        dba = jnp.zeros((CHUNK, HEAD_W), F32)
        dal = jnp.zeros((1, HEAD_W), F32)
        ddt = jnp.zeros((1, HEAD_W), F32)
        for h in range(nh):
            cs = slice(h * HEAD_W, (h + 1) * HEAD_W)
            fn = functools.partial(_gdn_chunk, head=h, valid=valid)
            _, vjp = jax.vjp(fn, q_ref[:, cs], k_ref[:, cs], v_ref[:, cs], ba_ref[...], al_ref[...], dt_ref[...],
                             st_ref[h])
            dq, dk, dv, dba_h, dal_h, ddt_h, ds = vjp((do_ref[:, cs], ds_ref[h]))
            dq_ref[:, cs] = dq
            dk_ref[:, cs] = dk
            dv_ref[:, cs] = dv
            ds_ref[h] = ds
            dba, dal, ddt = dba + dba_h, dal + dal_h, ddt + ddt_h
        dba_ref[...] = dba
        dal_ref[...] += dal
        ddt_ref[...] += ddt

    w = nh * HEAD_W
    rev = lambda c: (lambda s: (n - 1 - s, c))
    vec = pl.BlockSpec((1, HEAD_W), lambda s: (0, 0))
    dq, dk, dv, dba, dal, ddt = pl.pallas_call(
        body, name=name, grid=(n,),
        in_specs=[pl.BlockSpec((CHUNK, w), rev(0)), pl.BlockSpec((CHUNK, w), rev(1)), pl.BlockSpec((CHUNK, w), rev(2)),
                  pl.BlockSpec((CHUNK, HEAD_W), rev(AB_BA // HEAD_W)), vec, vec,
                  pl.BlockSpec((None, nh, HEAD_W, HEAD_W), lambda s: (n - 1 - s, 0, 0, 0)),
                  pl.BlockSpec((CHUNK, w), rev(0))],
        out_specs=[pl.BlockSpec((CHUNK, w), rev(0)), pl.BlockSpec((CHUNK, w), rev(0)), pl.BlockSpec((CHUNK, w), rev(0)),
                   pl.BlockSpec((CHUNK, HEAD_W), rev(0)), vec, vec],
        out_shape=[jax.ShapeDtypeStruct((lp, w), F32)] * 3 + [jax.ShapeDtypeStruct((lp, HEAD_W), F32)]
        + [jax.ShapeDtypeStruct((1, HEAD_W), F32)] * 2,
        scratch_shapes=[pltpu.VMEM((nh, HEAD_W, HEAD_W), F32)],
        compiler_params=_cp(("arbitrary",)),
    )(qkv, qkv, qkv, p0, alog_v, dtb_v, states, do)
    return dq, dk, dv, dba, dal, ddt


HG_SUB = 16


@functools.partial(jax.custom_vjp, nondiff_argnums=(1,))
def _shift_rows(x, d):
    return pltpu.roll(x, d, 0)


def _shift_rows_fwd(x, d):
    return pltpu.roll(x, d, 0), None


def _shift_rows_bwd(d, _, g):
    return (pltpu.roll(g, g.shape[0] - d, 0),)


_shift_rows.defvjp(_shift_rows_fwd, _shift_rows_bwd)


def _hg_chunk(qr, fr, ir, lb, st, valid):
    c = qr.shape[0]
    fg = lb + (1.0 - lb) * _sigmoid(fr)
    logf = jnp.where(valid, jnp.log(fg), 0.0)
    k = jnp.where(valid, 1.0 - fg, 0.0)
    qs = jnp.where(valid, _silu(qr), 0.0)
    v = jnp.where(valid, ir, 0.0)
    causal = _tri(c, False)
    b = _dot(causal.astype(F32), logf, NN, HI)
    rows = lax.broadcasted_iota(jnp.int32, (c, 1), 0)
    rr = lax.broadcasted_iota(jnp.int32, (c, c), 0)
    cc = lax.broadcasted_iota(jnp.int32, (c, c), 1)
    a = jnp.zeros((c, c), F32)
    for d in range(HG_SUB):
        kd, bd = (k, b) if d == 0 else (_shift_rows(k, d), _shift_rows(b, d))
        diag = jnp.sum(qs * kd * jnp.exp(jnp.minimum(b - bd, 0.0)), axis=-1, keepdims=True)
        a = a + jnp.where((cc == rr - d) & (rows % HG_SUB >= d), diag, 0.0)
    far = [jnp.zeros((HG_SUB, c), F32)]
    for blk in range(1, c // HG_SUB):
        r0 = blk * HG_SUB
        br = b[r0:r0 + 1, :]
        kp = k * jnp.exp(jnp.minimum(br - b, 0.0))
        qp = qs[r0:r0 + HG_SUB] * jnp.exp(jnp.minimum(b[r0:r0 + HG_SUB] - br, 0.0))
        far.append(jnp.where(cc[:HG_SUB] < r0, _dot(qp, kp, NT, HI), 0.0))
    a = a + jnp.concatenate(far, axis=0)
    bl = b[c - 1:c, :]
    o = _bdot(qs * jnp.exp(b), st, NT) + _bdot(a, v, NN)
    st2 = st * jnp.exp(bl) + _bdot(v, k * jnp.exp(bl - b), TN)
    return o, st2


def _hg_fwd(p1, lb, pad, *, name):
    lp = p1.shape[0]
    n = lp // CHUNK
    nh = HG_HEADS

    def body(q_ref, f_ref, i_ref, lb_ref, o_ref, st_ref, s_ref):
        i = pl.program_id(1)

        @pl.when(i == 0)
        def _():
            s_ref[...] = jnp.zeros_like(s_ref)

        valid = (i * CHUNK + lax.broadcasted_iota(jnp.int32, (CHUNK, 1), 0)) >= pad
        st_ref[...] = s_ref[...]
        o, s2 = _hg_chunk(q_ref[...], f_ref[...], i_ref[...], lb_ref[...], s_ref[...], valid)
        o_ref[...] = o
        s_ref[...] = s2

    blk = lambda off: pl.BlockSpec((CHUNK, HEAD_W), lambda h, i: (i, off + h))
    return pl.pallas_call(
        body, name=name, grid=(nh, n),
        in_specs=[blk(0), blk(nh), blk(2 * nh), pl.BlockSpec((1, HEAD_W), lambda h, i: (0, h))],
        out_specs=[blk(0), pl.BlockSpec((None, None, HEAD_W, HEAD_W), lambda h, i: (h, i, 0, 0))],
        out_shape=[jax.ShapeDtypeStruct((lp, nh * HEAD_W), F32), jax.ShapeDtypeStruct((nh, n, HEAD_W, HEAD_W), F32)],
        scratch_shapes=[pltpu.VMEM((HEAD_W, HEAD_W), F32)],
        compiler_params=_cp(("parallel", "arbitrary")),
    )(p1, p1, p1, lb)


def _hg_bwd(p1, lb, states, do, pad, *, name):
    lp = p1.shape[0]
    n = lp // CHUNK
    nh = HG_HEADS

    def body(q_ref, f_ref, i_ref, lb_ref, st_ref, do_ref, dq_ref, df_ref, di_ref, dlb_ref, ds_ref):
        step = pl.program_id(1)
        i = n - 1 - step

        @pl.when(step == 0)
        def _():
            ds_ref[...] = jnp.zeros_like(ds_ref)
            dlb_ref[...] = jnp.zeros_like(dlb_ref)

        valid = (i * CHUNK + lax.broadcasted_iota(jnp.int32, (CHUNK, 1), 0)) >= pad
        fn = functools.partial(_hg_chunk, valid=valid)
        _, vjp = jax.vjp(fn, q_ref[...], f_ref[...], i_ref[...], lb_ref[...], st_ref[...])
        dq, df, di, dlb, ds = vjp((do_ref[...], ds_ref[...]))
        dq_ref[...] = dq
        df_ref[...] = df
        di_ref[...] = di
        dlb_ref[...] += dlb
        ds_ref[...] = ds

    blk = lambda off: pl.BlockSpec((CHUNK, HEAD_W), lambda h, s: (n - 1 - s, off + h))
    w = nh * HEAD_W
    return pl.pallas_call(
        body, name=name, grid=(nh, n),
        in_specs=[blk(0), blk(nh), blk(2 * nh), pl.BlockSpec((1, HEAD_W), lambda h, s: (0, h)),
                  pl.BlockSpec((None, None, HEAD_W, HEAD_W), lambda h, s: (h, n - 1 - s, 0, 0)), blk(0)],
        out_specs=[blk(0), blk(0), blk(0), pl.BlockSpec((1, HEAD_W), lambda h, s: (0, h))],
        out_shape=[jax.ShapeDtypeStruct((lp, w), F32)] * 3 + [jax.ShapeDtypeStruct((1, w), F32)],
        scratch_shapes=[pltpu.VMEM((HEAD_W, HEAD_W), F32)],
        compiler_params=_cp(("parallel", "arbitrary")),
    )(p1, p1, p1, lb, states, do)


SB_GROUP = 4


def _sb_cat(kind, first_key=0):
    r = lax.broadcasted_iota(jnp.int32, (SB_BLOCK, 2 * SB_BLOCK), 0)
    c = lax.broadcasted_iota(jnp.int32, (SB_BLOCK, 2 * SB_BLOCK), 1)
    tri = {"after": c < r, "incl": r <= c, "before": r < c}[kind]
    m = ((c >= SB_BLOCK) | tri) & (r >= first_key)
    return jnp.where(m, 1.0, 0.0).astype(BF16)


def _sb_cumsum(x, cat):
    hi = x.astype(BF16)
    lo = (x - hi.astype(F32)).astype(BF16)
    return _dot(hi, cat, NN) + _dot(lo, cat, NN)


def _sb_logsig(z):
    e = jnp.exp(-jnp.abs(z))
    lse = jnp.where(e < 1e-4, e, jnp.log(1.0 + e))
    lsz = jnp.minimum(z, 0.0) - lse
    return lsz, lsz - z, e


def _sb_stack(x, scale=None):
    lane = lax.broadcasted_iota(jnp.int32, (1, HEAD_W), 1)
    if scale is not None:
        x = x * scale
    return jnp.concatenate([jnp.where(lane < SB_DH, x, 0.0), jnp.where(lane >= SB_DH, x, 0.0)], axis=0).astype(BF16)


def _sb_unstack(x):
    lane = lax.broadcasted_iota(jnp.int32, (1, HEAD_W), 1)
    return jnp.where(lane < SB_DH, x[:SB_BLOCK], x[SB_BLOCK:])


def _sb_fwd(p0, pad, *, name):
    lp = p0.shape[0]
    nb = lp // SB_BLOCK
    npair = SB_HEADS // 2
    blk0 = AB_SB // HEAD_W
    scale = SB_DH ** -0.5
    gw = SB_GROUP * SB_BLOCK
    assert pad < SB_BLOCK

    def body(q_ref, k_ref, v_ref, o_ref, tot_ref):
        i = pl.program_id(1)
        qs = _sb_stack(q_ref[...], scale)
        qpos = i * SB_BLOCK + lax.broadcasted_iota(jnp.int32, (SB_BLOCK, 1), 0)
        qpos = jnp.concatenate([qpos, qpos], axis=0)
        cat = _sb_cat("after")
        cat0 = _sb_cat("after", pad)
        ng = i // SB_GROUP

        def single(t, carry):
            acc, run = carry
            kb = i - t
            off = pl.multiple_of(kb * SB_BLOCK, SB_BLOCK)
            kblk = k_ref[pl.ds(off, SB_BLOCK), :].astype(BF16)
            vblk = v_ref[pl.ds(off, SB_BLOCK), :].astype(BF16)
            kpos = kb * SB_BLOCK + lax.broadcasted_iota(jnp.int32, (1, SB_BLOCK), 1)
            allowed = (kpos < qpos) & (kpos >= pad)
            lsz, l1m, _ = _sb_logsig(_dot(qs, kblk, NT))
            al = _sb_cumsum(jnp.where(allowed, l1m, 0.0), cat)
            wgt = jnp.where(allowed, jnp.exp(lsz + al[:, :SB_BLOCK] + run), 0.0)
            return acc + _dot(wgt.astype(BF16), vblk, NN), run + al[:, SB_BLOCK:]

        def group(t, carry):
            acc, run = carry
            gi = ng - 1 - t
            off = pl.multiple_of(gi * gw, gw)
            kg = k_ref[pl.ds(off, gw), :].astype(BF16)
            vg = v_ref[pl.ds(off, gw), :].astype(BF16)
            lsz, l1m, _ = _sb_logsig(_dot(qs, kg, NT))
            args = [None] * SB_GROUP
            for g in reversed(range(SB_GROUP)):
                sl = slice(g * SB_BLOCK, (g + 1) * SB_BLOCK)
                al = _sb_cumsum(l1m[:, sl], jnp.where(gi == 0, cat0, cat) if g == 0 else cat)
                args[g] = lsz[:, sl] + al[:, :SB_BLOCK] + run
                run = run + al[:, SB_BLOCK:]
            wgt = jnp.exp(jnp.concatenate(args, axis=1)).astype(BF16)
            return acc + _dot(wgt, vg, NN), run

        zero = (jnp.zeros((2 * SB_BLOCK, HEAD_W), F32), jnp.zeros((2 * SB_BLOCK, HEAD_W), F32))
        carry = lax.fori_loop(0, i + 1 - ng * SB_GROUP, single, zero)
        acc, run = lax.fori_loop(0, ng, group, carry)
        o_ref[...] = _sb_unstack(acc)
        tot_ref[...] = _sb_unstack(run)

    full = lambda c0: pl.BlockSpec((lp, HEAD_W), lambda p, i: (0, c0 + p))
    out = pl.BlockSpec((SB_BLOCK, HEAD_W), lambda p, i: (i, p))
    return pl.pallas_call(
        body, name=name, grid=(npair, nb),
        in_specs=[pl.BlockSpec((SB_BLOCK, HEAD_W), lambda p, i: (i, blk0 + p)), full(blk0 + npair), full(blk0 + 2 * npair)],
        out_specs=[out, out],
        out_shape=[jax.ShapeDtypeStruct((lp, npair * HEAD_W), F32)] * 2,
        compiler_params=_cp(("parallel", "arbitrary")),
    )(p0, p0, p0)


def _sb_bwd(p0, tot, dsrc, d_blk0, pad, *, name):
    lp = p0.shape[0]
    nb = lp // SB_BLOCK
    npair = SB_HEADS // 2
    blk0 = AB_SB // HEAD_W
    scale = SB_DH ** -0.5
    gw = SB_GROUP * SB_BLOCK
    assert pad < SB_BLOCK

    def body(q_ref, k_ref, v_ref, tot_ref, do_ref, dq_ref, dkt_ref, dvt_ref):
        i = pl.program_id(1)

        @pl.when(i == 0)
        def _():
            dkt_ref[...] = jnp.zeros_like(dkt_ref)
            dvt_ref[...] = jnp.zeros_like(dvt_ref)

        qs = _sb_stack(q_ref[...], scale)
        dos = _sb_stack(do_ref[...])
        qst, dost = qs.T, dos.T
        totv = tot_ref[...]
        ones = jnp.ones((1, HEAD_W), F32)
        tots = jnp.concatenate([totv[:, 0:1] * ones, totv[:, SB_DH:SB_DH + 1] * ones], axis=0)
        qpos = i * SB_BLOCK + lax.broadcasted_iota(jnp.int32, (SB_BLOCK, 1), 0)
        qpos = jnp.concatenate([qpos, qpos], axis=0)
        incl, incl0 = _sb_cat("incl"), _sb_cat("incl", pad)
        before = _sb_cat("before")
        ng = i // SB_GROUP

        def dscore(z, e, ev, dl1m):
            r = 1.0 / (1.0 + e)
            sg = jnp.where(z >= 0, r, e * r)
            return ev * (1.0 - sg) - dl1m * sg

        def group(gi, carry):
            dq, prun, erun = carry
            off = pl.multiple_of(gi * gw, gw)
            kg = k_ref[pl.ds(off, gw), :].astype(BF16)
            vg = v_ref[pl.ds(off, gw), :].astype(BF16)
            z = _dot(qs, kg, NT)
            lsz, l1m, e = _sb_logsig(z)
            dwgt = _dot(dos, vg, NT)
            dzs = [None] * SB_GROUP
            wgts = [None] * SB_GROUP
            for g in range(SB_GROUP):
                sl = slice(g * SB_BLOCK, (g + 1) * SB_BLOCK)
                al = _sb_cumsum(l1m[:, sl], jnp.where(gi == 0, incl0, incl) if g == 0 else incl)
                wgt = jnp.exp(lsz[:, sl] + (tots - prun - al[:, :SB_BLOCK]))
                prun = prun + al[:, SB_BLOCK:]
                ev = wgt * dwgt[:, sl]
                el = _sb_cumsum(ev, before)
                dzs[g] = dscore(z[:, sl], e[:, sl], ev, erun + el[:, :SB_BLOCK])
                erun = erun + el[:, SB_BLOCK:]
                wgts[g] = wgt
            dz = jnp.concatenate(dzs, axis=1).astype(BF16)
            wg = jnp.concatenate(wgts, axis=1).astype(BF16)
            dkt_ref[:, pl.ds(off, gw)] += _dot(qst, dz, NN)
            dvt_ref[:, pl.ds(off, gw)] += _dot(dost, wg, NN)
            return dq + _dot(dz, kg, NN), prun, erun

        def single(t, carry):
            dq, prun, erun = carry
            kb = ng * SB_GROUP + t
            off = pl.multiple_of(kb * SB_BLOCK, SB_BLOCK)
            kblk = k_ref[pl.ds(off, SB_BLOCK), :].astype(BF16)
            vblk = v_ref[pl.ds(off, SB_BLOCK), :].astype(BF16)
            kpos = kb * SB_BLOCK + lax.broadcasted_iota(jnp.int32, (1, SB_BLOCK), 1)
            allowed = (kpos < qpos) & (kpos >= pad)
            z = _dot(qs, kblk, NT)
            lsz, l1m, e = _sb_logsig(z)
            al = _sb_cumsum(jnp.where(allowed, l1m, 0.0), incl)
            wgt = jnp.where(allowed, jnp.exp(lsz + (tots - prun - al[:, :SB_BLOCK])), 0.0)
            ev = wgt * _dot(dos, vblk, NT)
            el = _sb_cumsum(ev, before)
            dz = jnp.where(allowed, dscore(z, e, ev, erun + el[:, :SB_BLOCK]), 0.0).astype(BF16)
            dkt_ref[:, pl.ds(off, SB_BLOCK)] += _dot(qst, dz, NN)
            dvt_ref[:, pl.ds(off, SB_BLOCK)] += _dot(dost, wgt.astype(BF16), NN)
            return dq + _dot(dz, kblk, NN), prun + al[:, SB_BLOCK:], erun + el[:, SB_BLOCK:]

        zero = tuple(jnp.zeros((2 * SB_BLOCK, HEAD_W), F32) for _ in range(3))
        carry = lax.fori_loop(0, ng, group, zero)
        dq, _, _ = lax.fori_loop(0, i + 1 - ng * SB_GROUP, single, carry)
        dq_ref[...] = _sb_unstack(dq) * scale

    full = lambda c0: pl.BlockSpec((lp, HEAD_W), lambda p, i: (0, c0 + p))
    qb = lambda c0: pl.BlockSpec((SB_BLOCK, HEAD_W), lambda p, i: (i, c0 + p))
    tr = pl.BlockSpec((HEAD_W, lp), lambda p, i: (p, 0))
    return pl.pallas_call(
        body, name=name, grid=(npair, nb),
        in_specs=[qb(blk0), full(blk0 + npair), full(blk0 + 2 * npair), qb(0), qb(d_blk0)],
        out_specs=[qb(0), tr, tr],
        out_shape=[jax.ShapeDtypeStruct((lp, npair * HEAD_W), F32)]
        + [jax.ShapeDtypeStruct((npair * HEAD_W, lp), F32)] * 2,
        compiler_params=_cp(("parallel", "arbitrary")),
    )(p0, p0, p0, tot, dsrc)


def _local_step(h0, target, pad, wts):
    lp = h0.shape[0]
    tm = _row_tile(lp, 1056)
    tkl = tm
    d = D_MODEL
    mm = _mm
    g = {}

    p0 = mm(h0, wts["w_ab"], "NN", tm=tm, tn=768, tk=d, name="l0_in_proj")
    qkv = _gdn_pre_fwd(p0, wts["conv_w"], pad, name="gdn_pre_fwd")
    oa_raw, gdn_states = _gdn_fwd(qkv, p0, wts["alog_v"], wts["dtb_v"], pad, name="gdn_fwd")
    ob, sb_tot = _sb_fwd(p0, pad, name="sb_fwd")
    oab = _gate_fwd(oa_raw, p0, AB_Z // HEAD_W, wts["ab_gn"], ob, heads=GDN_HEADS, name="gdn_gate_fwd")
    mix0 = mm(oab, wts["w_out0"], "NN", tm=tm, tn=512, tk=d, name="l0_out_proj")
    h0a = _ln_fwd(h0, mix0, wts["ln_mix_g"][0], wts["ln_mix_b"][0], name="ln_mix0_fwd")
    u0 = mm(h0a, wts["w1"][0], "NN", tm=tm, tn=512, tk=d, b_dev=True, name="mlp0_up")
    y0 = mm(u0, wts["w2"][0], "NN", tm=tm, tn=512, tk=d, a_fn="relu2", name="mlp0_down")
    h0b = _ln_fwd(h0a, y0, wts["ln_ffn_g"][0], wts["ln_ffn_b"][0], name="ln_ffn0_fwd")
    p1 = mm(h0b, wts["w_c"], "NN", tm=tm, tn=512, tk=d, b_dev=True, name="l1_in_proj")
    oc_raw, hg_states = _hg_fwd(p1, wts["lb"], pad, name="hg_fwd")
    oc = _gate_fwd(oc_raw, p1, 3 * HG_HEADS, wts["c_gn"], oc_raw, heads=HG_HEADS, name="hg_gate_fwd")
    mix1 = mm(oc, wts["w_out1"], "NN", tm=tm, tn=512, tk=d, name="l1_out_proj")
    h1a = _ln_fwd(h0b, mix1, wts["ln_mix_g"][1], wts["ln_mix_b"][1], name="ln_mix1_fwd")
    u1 = mm(h1a, wts["w1"][1], "NN", tm=tm, tn=512, tk=d, b_dev=True, name="mlp1_up")
    y1 = mm(u1, wts["w2"][1], "NN", tm=tm, tn=512, tk=d, a_fn="relu2", name="mlp1_down")
    h1b = _ln_fwd(h1a, y1, wts["ln_ffn_g"][1], wts["ln_ffn_b"][1], name="ln_ffn1_fwd")
    dy, loss_vec = _loss_head(h1b, target, name="loss_head")

    def mlp_bwd(layer, h_in, u, dpre):
        du = mm(dpre, wts["w2"][layer], "NT", tm=tm, tn=512, tk=d, epi="relu2grad", c=u, name=f"mlp{layer}_d_hidden")
        dw2 = mm(u, dpre, "TN", tm=1024, tn=1024, tk=tkl, a_fn="relu2", name=f"mlp{layer}_dw2")
        dw1 = mm(h_in, du, "TN", tm=1024, tn=512, tk=tkl, out_dev=True, name=f"mlp{layer}_dw1")
        dh = mm(du, wts["w1"][layer], "NT", tm=tm, tn=1024, tk=512, b_dev=True, epi="add", c=dpre, scale=DN_ALPHA,
                name=f"mlp{layer}_d_in")
        return dh, dw1, dw2

    ln_ffn_dg, ln_ffn_db, ln_mix_dg, ln_mix_db, dw1s, dw2s = [None, None], [None, None], [None, None], [None, None], [None, None], [None, None]
    dpre, ln_ffn_dg[1], ln_ffn_db[1] = _ln_bwd(h1a, y1, wts["ln_ffn_g"][1], dy, name="ln_ffn1_bwd")
    dh1a, dw1s[1], dw2s[1] = mlp_bwd(1, h1a, u1, dpre)
    dpre, ln_mix_dg[1], ln_mix_db[1] = _ln_bwd(h0b, mix1, wts["ln_mix_g"][1], dh1a, name="ln_mix1_bwd")
    g["c_w_out"] = mm(oc, dpre, "TN", tm=1024, tn=1024, tk=tkl, name="l1_dw_out")
    doc = mm(dpre, wts["w_out1"], "NT", tm=tm, tn=512, tk=d, name="l1_d_gate")
    doc_raw, dz1, g["c_gn"] = _gate_bwd(oc_raw, p1, 3 * HG_HEADS, wts["c_gn"], doc, heads=HG_HEADS, name="hg_gate_bwd")
    dq1, df1, di1, g["lb"] = _hg_bwd(p1, wts["lb"], hg_states, doc_raw, pad, name="hg_bwd")
    dp1 = jnp.concatenate([dq1, df1, di1, dz1], axis=1)
    g["c_w_in"] = mm(h0b, dp1, "TN", tm=1024, tn=512, tk=tkl, out_dev=True, name="l1_dw_in")
    dh0b = mm(dp1, wts["w_c"], "NT", tm=tm, tn=1024, tk=512, b_dev=True, epi="add", c=dpre, scale=DN_ALPHA,
              name="l1_d_in")
    dpre, ln_ffn_dg[0], ln_ffn_db[0] = _ln_bwd(h0a, y0, wts["ln_ffn_g"][0], dh0b, name="ln_ffn0_bwd")
    dh0a, dw1s[0], dw2s[0] = mlp_bwd(0, h0a, u0, dpre)
    dpre, ln_mix_dg[0], ln_mix_db[0] = _ln_bwd(h0, mix0, wts["ln_mix_g"][0], dh0a, name="ln_mix0_bwd")
    g["ab_w_out"] = mm(oab, dpre, "TN", tm=1024, tn=1024, tk=tkl, name="l0_dw_out")
    doab = mm(dpre, wts["w_out0"], "NT", tm=tm, tn=512, tk=d, name="l0_d_gate")
    doa_raw, dz0, g["ab_gn"] = _gate_bwd(oa_raw, p0, AB_Z // HEAD_W, wts["ab_gn"], doab, heads=GDN_HEADS,
                                         name="gdn_gate_bwd")
    dqb, dkb_t, dvb_t = _sb_bwd(p0, sb_tot, doab, GDN_HEADS, pad, name="sb_bwd")
    dkb, dvb = dkb_t.T, dvb_t.T
    dqn, dkn, dvn, dba, g["alog_v"], g["dtb_v"] = _gdn_bwd(qkv, p0, wts["alog_v"], wts["dtb_v"], gdn_states, doa_raw,
                                                           pad, name="gdn_bwd")
    dconv_in, g["conv_w"] = _gdn_pre_bwd(p0, wts["conv_w"], jnp.concatenate([dqn, dkn, dvn], axis=1), pad,
                                         name="gdn_pre_bwd")
    dp0 = jnp.concatenate([dconv_in, dz0, dqb, dkb, dvb, dba, jnp.zeros((lp, AB_CAT - AB_BA - HEAD_W), F32)], axis=1)
    g["w_ab"] = mm(h0, dp0, "TN", tm=1024, tn=768, tk=tkl, name="l0_dw_in")
    dh0 = mm(dp0, wts["w_ab"], "NT", tm=tm, tn=1024, tk=768, epi="add", c=dpre, scale=DN_ALPHA, name="l0_d_in")

    g["w1"], g["w2"] = dw1s, dw2s
    g["ln_mix_g"] = jnp.concatenate(ln_mix_dg, axis=0)
    g["ln_mix_b"] = jnp.concatenate(ln_mix_db, axis=0)
    g["ln_ffn_g"] = jnp.concatenate(ln_ffn_dg, axis=0)
    g["ln_ffn_b"] = jnp.concatenate(ln_ffn_db, axis=0)
    return loss_vec, dh0, g


def _my_index():
    return 4 * lax.axis_index("x") + 2 * lax.axis_index("y") + lax.axis_index("c")


def _exchange(srcs, dtypes, *, scatter, name):
    n = len(srcs)
    blocks = [s.shape[1:] if scatter else s.shape for s in srcs]

    def body(*refs):
        ins, outs = refs[:n], refs[n:2 * n]
        rest = refs[2 * n:]
        stages = rest[:n] if not scatter else [None] * n
        send_sems, recv_sems, local_sems = rest[-3:]
        me = _my_index()
        pending = []
        for i in range(n):
            if scatter:
                mine = ins[i].at[me]
            else:
                stages[i][...] = ins[i][...].astype(dtypes[i])
                mine = stages[i]
            loc = pltpu.make_async_copy(mine, outs[i].at[me], local_sems.at[i])
            loc.start()
            pending.append(loc)
            for k in range(1, N_DEV):
                peer = jnp.bitwise_xor(me, k)
                dev = (peer // 4, (peer // 2) % 2, peer % 2)
                cp = pltpu.make_async_remote_copy(
                    src_ref=ins[i].at[peer] if scatter else stages[i], dst_ref=outs[i].at[me],
                    send_sem=send_sems.at[i, k - 1], recv_sem=recv_sems.at[i, k - 1],
                    device_id=dev, device_id_type=pl.DeviceIdType.MESH)
                cp.start()
                pending.append(cp)
        for cp in pending:
            cp.wait()

    any_spec = pl.BlockSpec(memory_space=pl.ANY)
    vmem_spec = pl.BlockSpec(memory_space=pltpu.VMEM)
    scratch = [] if scatter else [pltpu.VMEM(b, dt) for b, dt in zip(blocks, dtypes)]
    scratch += [pltpu.SemaphoreType.DMA((n, N_DEV - 1)), pltpu.SemaphoreType.DMA((n, N_DEV - 1)),
                pltpu.SemaphoreType.DMA((n,))]
    return pl.pallas_call(
        body, name=name,
        in_specs=[any_spec if scatter else vmem_spec] * n, out_specs=[any_spec] * n,
        out_shape=[jax.ShapeDtypeStruct((N_DEV, *b), dt) for b, dt in zip(blocks, dtypes)],
        scratch_shapes=scratch, compiler_params=_cp(has_side_effects=True),
    )(*srcs)


def _adamw(w, parts, m, v, *, name):
    r, c = w.shape
    s = parts.shape[0]
    tm = _row_tile(r, 128) if r % 8 == 0 else r
    c1 = 1.0 - ADAM_B1 ** ADAM_STEP
    c2 = 1.0 - ADAM_B2 ** ADAM_STEP

    def body(w_ref, p_ref, m_ref, v_ref, g_ref, d_ref, m2_ref, v2_ref):
        g = p_ref[0]
        for j in range(1, s):
            g = g + p_ref[j]
        m2 = ADAM_B1 * m_ref[...] + (1.0 - ADAM_B1) * g
        v2 = ADAM_B2 * v_ref[...] + (1.0 - ADAM_B2) * jnp.square(g)
        g_ref[...] = g
        m2_ref[...] = m2
        v2_ref[...] = v2
        d_ref[...] = -ADAM_LR * ((m2 / c1) / (jnp.sqrt(v2 / c2) + ADAM_EPS) + ADAM_WD * w_ref[...])

    blk = pl.BlockSpec((tm, c), lambda i: (i, 0))
    return pl.pallas_call(
        body, name=name, grid=(r // tm,),
        in_specs=[blk, pl.BlockSpec((s, tm, c), lambda i: (0, i, 0)), blk, blk], out_specs=[blk] * 4,
        out_shape=[jax.ShapeDtypeStruct((r, c), F32)] * 4, compiler_params=_cp(("parallel",)),
    )(w, parts, m, v)


_WEIGHTS = ("meta_tokens", "ab_w_in", "ab_conv_w", "ab_a_log", "ab_dt_bias", "ab_gnorm_g", "ab_w_out", "c_w_in",
            "c_lb_raw", "c_gnorm_g", "c_w_out", "ln_mix_g", "ln_mix_b", "mlp_w1", "mlp_w2", "ln_ffn_g", "ln_ffn_b")
_PACK_ROWS = (("ln_mix_g", 0), ("ln_mix_b", 2), ("ln_ffn_g", 4), ("ln_ffn_b", 6), ("c_lb_raw", 8))
_PACK_MISC_ROW = 10
_PACK_MISC = (("ab_gnorm_g", 0, 128), ("c_gnorm_g", 128, 128), ("ab_a_log", 256, GDN_HEADS), ("ab_dt_bias", 260, GDN_HEADS))
_PACK_N = 16
_SMALL_META = 16
_SMALL_CONV = 32
_SMALL_N = 40


def _pack_replicated(p):
    rows = jnp.zeros((_PACK_N, D_MODEL), F32)
    for name, r0 in _PACK_ROWS:
        rows = rows.at[r0:r0 + 2].set(p[name])
    for name, c0, width in _PACK_MISC:
        rows = rows.at[_PACK_MISC_ROW, c0:c0 + width].set(p[name].reshape(width))
    return rows


def _unpack_replicated(rows, like):
    out = {}
    for name, r0 in _PACK_ROWS:
        out[name] = rows[r0:r0 + 2]
    for name, c0, width in _PACK_MISC:
        out[name] = rows[_PACK_MISC_ROW, c0:c0 + width].reshape(like[name].shape)
    return out


def _lower_bound(c_lb_raw):
    lb_all = jnp.cumsum(jax.nn.softmax(c_lb_raw.astype(F32), axis=0), axis=0)
    return (lb_all - lb_all[0:1])[1].reshape(1, -1)


def kernel(x, meta_tokens, ab_w_in, ab_conv_w, ab_a_log, ab_dt_bias, ab_gnorm_g, ab_w_out, c_w_in, c_lb_raw, c_gnorm_g, c_w_out, ln_mix_g, ln_mix_b, mlp_w1, mlp_w2, ln_ffn_g, ln_ffn_b, loss_target, m_meta_tokens, m_ab_w_in, m_ab_conv_w, m_ab_a_log, m_ab_dt_bias, m_ab_gnorm_g, m_ab_w_out, m_c_w_in, m_c_lb_raw, m_c_gnorm_g, m_c_w_out, m_ln_mix_g, m_ln_mix_b, m_mlp_w1, m_mlp_w2, m_ln_ffn_g, m_ln_ffn_b, v_meta_tokens, v_ab_w_in, v_ab_conv_w, v_ab_a_log, v_ab_dt_bias, v_ab_gnorm_g, v_ab_w_out, v_c_w_in, v_c_lb_raw, v_c_gnorm_g, v_c_w_out, v_ln_mix_g, v_ln_mix_b, v_mlp_w1, v_mlp_w2, v_ln_ffn_g, v_ln_ffn_b):
    w = dict(zip(_WEIGHTS, (meta_tokens, ab_w_in, ab_conv_w, ab_a_log, ab_dt_bias, ab_gnorm_g, ab_w_out, c_w_in, c_lb_raw,
                            c_gnorm_g, c_w_out, ln_mix_g, ln_mix_b, mlp_w1, mlp_w2, ln_ffn_g, ln_ffn_b)))
    mom = dict(zip(_WEIGHTS, (m_meta_tokens, m_ab_w_in, m_ab_conv_w, m_ab_a_log, m_ab_dt_bias, m_ab_gnorm_g, m_ab_w_out,
                              m_c_w_in, m_c_lb_raw, m_c_gnorm_g, m_c_w_out, m_ln_mix_g, m_ln_mix_b, m_mlp_w1, m_mlp_w2,
                              m_ln_ffn_g, m_ln_ffn_b)))
    var = dict(zip(_WEIGHTS, (v_meta_tokens, v_ab_w_in, v_ab_conv_w, v_ab_a_log, v_ab_dt_bias, v_ab_gnorm_g, v_ab_w_out,
                              v_c_w_in, v_c_lb_raw, v_c_gnorm_g, v_c_w_out, v_ln_mix_g, v_ln_mix_b, v_mlp_w1, v_mlp_w2,
                              v_ln_ffn_g, v_ln_ffn_b)))
    me = _my_index()
    seq = x.shape[1]
    pad = (-(N_META + seq)) % SB_BLOCK
    lp = pad + N_META + seq
    meta_w = D_MODEL // N_DEV
    conv_w_all = 2 * GDN_HEADS * HEAD_W + GDN_HEADS * HEAD_W
    conv_w_mine = conv_w_all // N_DEV

    gathered = _exchange(
        [w["meta_tokens"], w["ab_conv_w"][0], w["ab_w_in"][0], w["ab_w_out"][0], w["c_w_in"][0], w["c_w_out"][0],
         w["mlp_w1"], w["mlp_w2"]],
        [F32, F32, BF16, BF16, BF16, BF16, BF16, BF16], scatter=False, name="gather_weights")
    g_meta, g_conv, g_ab_in, g_ab_out, g_c_in, g_c_out, g_w1, g_w2 = gathered
    meta_full = g_meta.transpose(1, 0, 2).reshape(N_META, D_MODEL)
    conv_full = g_conv.transpose(1, 0, 2).reshape(CONV_K, conv_w_all)
    ab_full = g_ab_in.transpose(1, 0, 2).reshape(D_MODEL, AB_IN)
    ba0 = AB_Z + 512
    w_ab = jnp.concatenate([ab_full[:, :ba0], ab_full[:, ba0 + 2 * GDN_HEADS:], ab_full[:, ba0:ba0 + 2 * GDN_HEADS],
                            jnp.zeros((D_MODEL, AB_CAT - AB_IN), BF16)], axis=1)
    vec128 = lambda p: jnp.zeros((1, HEAD_W), F32).at[0, :GDN_HEADS].set(p.reshape(GDN_HEADS))
    wts = dict(
        w_ab=w_ab, conv_w=conv_full, alog_v=vec128(w["ab_a_log"]), dtb_v=vec128(w["ab_dt_bias"]),
        ab_gn=w["ab_gnorm_g"][0], w_out0=g_ab_out.reshape(D_MODEL, D_MODEL), w_c=g_c_in,
        lb=_lower_bound(w["c_lb_raw"]), c_gn=w["c_gnorm_g"][0], w_out1=g_c_out.reshape(D_MODEL, D_MODEL),
        w1=[g_w1[:, l] for l in range(DEPTH)], w2=[g_w2[:, l].reshape(D_FF, D_MODEL) for l in range(DEPTH)],
        ln_mix_g=w["ln_mix_g"], ln_mix_b=w["ln_mix_b"], ln_ffn_g=w["ln_ffn_g"], ln_ffn_b=w["ln_ffn_b"])

    h0 = jnp.concatenate([jnp.zeros((pad, D_MODEL), F32), meta_full, x[0]], axis=0)
    loss_vec, dh0, g = _local_step(h0, loss_target[0], pad, wts)
    loss = lax.psum(jnp.sum(loss_vec), ("x", "y", "c"))
    grad_x = dh0[lp - seq:][None]

    _, lb_vjp = jax.vjp(_lower_bound, w["c_lb_raw"])
    rep_part = _pack_replicated(dict(
        ln_mix_g=g["ln_mix_g"], ln_mix_b=g["ln_mix_b"], ln_ffn_g=g["ln_ffn_g"], ln_ffn_b=g["ln_ffn_b"],
        c_lb_raw=lb_vjp(g["lb"])[0], ab_gnorm_g=g["ab_gn"], c_gnorm_g=g["c_gn"],
        ab_a_log=g["alog_v"][0, :GDN_HEADS], ab_dt_bias=g["dtb_v"][0, :GDN_HEADS]))
    small = jnp.concatenate([rep_part, dh0[pad:pad + N_META], g["conv_w"].reshape(-1, D_MODEL),
                             jnp.zeros((_SMALL_N - _SMALL_CONV - CONV_K * conv_w_all // D_MODEL, D_MODEL), F32)], axis=0)
    (small_all,) = _exchange([small], [F32], scatter=False, name="gather_small_grads")
    rep_out = _adamw(_pack_replicated(w), small_all[:, :_PACK_N], _pack_replicated(mom), _pack_replicated(var),
                     name="adamw_replicated")
    meta_parts = lax.dynamic_slice_in_dim(small_all[:, _SMALL_META:_SMALL_META + N_META], me * meta_w, meta_w, axis=2)
    meta_out = _adamw(w["meta_tokens"], meta_parts, mom["meta_tokens"], var["meta_tokens"], name="adamw_meta")
    conv_parts = small_all[:, _SMALL_CONV:_SMALL_CONV + CONV_K * conv_w_all // D_MODEL].reshape(N_DEV, CONV_K, conv_w_all)
    conv_parts = lax.dynamic_slice_in_dim(conv_parts, me * conv_w_mine, conv_w_mine, axis=2)
    conv_out = _adamw(w["ab_conv_w"][0], conv_parts, mom["ab_conv_w"][0], var["ab_conv_w"][0], name="adamw_conv")

    gab = g["w_ab"]
    gab = jnp.concatenate([gab[:, :ba0], gab[:, AB_BA:AB_BA + 2 * GDN_HEADS], gab[:, ba0:AB_BA]], axis=1)
    big = [("ab_w_in", None, gab.reshape(D_MODEL, N_DEV, AB_IN // N_DEV).transpose(1, 0, 2)),
           ("ab_w_out", None, g["ab_w_out"].reshape(N_DEV, D_MODEL // N_DEV, D_MODEL)),
           ("c_w_in", None, g["c_w_in"]),
           ("c_w_out", None, g["c_w_out"].reshape(N_DEV, D_MODEL // N_DEV, D_MODEL))]
    for l in range(DEPTH):
        big.append(("mlp_w1", l, g["w1"][l]))
    for l in range(DEPTH):
        big.append(("mlp_w2", l, g["w2"][l].reshape(N_DEV, D_FF // N_DEV, D_MODEL)))
    parts = _exchange([b[2] for b in big], [F32] * len(big), scatter=True, name="scatter_grads")
    big_out = {}
    for (name, l, _), p in zip(big, parts):
        sel = (lambda a: a[0]) if l is None else (lambda a, l=l: a[l])
        res = _adamw(sel(w[name]), p, sel(mom[name]), sel(var[name]), name=f"adamw_{name}" + ("" if l is None else str(l)))
        big_out.setdefault(name, []).append(res)

    rep = [_unpack_replicated(r, w) for r in rep_out]
    outs = {}
    for name in _WEIGHTS:
        if name == "meta_tokens":
            outs[name] = list(meta_out)
        elif name == "ab_conv_w":
            outs[name] = [o[None] for o in conv_out]
        elif name in big_out:
            res = big_out[name]
            outs[name] = [o[None] for o in res[0]] if len(res) == 1 else [jnp.stack(pair) for pair in zip(*res)]
        else:
            outs[name] = [r[name] for r in rep]
    flat = [loss, grad_x]
    for kind in range(4):
        flat += [outs[name][kind] for name in _WEIGHTS]
    return tuple(flat)
```

```python
import functools
import math

import jax
import jax.numpy as jnp
from jax import lax
from jax.experimental import pallas as pl
from jax.experimental.pallas import tpu as pltpu

F32 = jnp.float32
BF16 = jnp.bfloat16
HI = lax.Precision.HIGHEST

N_DEV = 8
D_MODEL = 1024
N_META = 16
D_FF = 4096
DEPTH = 2
GDN_HEADS = 4
SB_HEADS = 8
SB_DH = 64
HG_HEADS = 8
HEAD_W = 128
CHUNK = 64
SB_BLOCK = 128
CONV_K = 4
DN_ALPHA = float((2 * DEPTH) ** 0.25)
LN_EPS = 1e-5
RMS_EPS = 1e-6
L2_EPS = 1e-6
ADAM_LR, ADAM_B1, ADAM_B2, ADAM_EPS, ADAM_WD, ADAM_STEP = 0.001, 0.9, 0.999, 1e-08, 0.01, 10

AB_QKV = 0
AB_Z = 1536
AB_SB = 2048
AB_BA = 3584
AB_CAT = 3840
AB_IN = 3592

VMEM_LIMIT = 56 * 1024 * 1024


def _cp(sem=None, **kw):
    if sem is not None:
        kw["dimension_semantics"] = sem
    return pltpu.CompilerParams(vmem_limit_bytes=VMEM_LIMIT, **kw)


def _row_tile(n, want):
    best = 8
    for t in range(8, min(n, want) + 1, 8):
        if n % t == 0:
            best = t
    return best


@jax.custom_vjp
def _sigmoid(x):
    e = jnp.exp(-jnp.abs(x))
    r = 1.0 / (1.0 + e)
    return jnp.where(x >= 0, r, e * r)


def _sigmoid_fwd(x):
    s = _sigmoid(x)
    return s, s


def _sigmoid_bwd(s, g):
    return (g * s * (1.0 - s),)


_sigmoid.defvjp(_sigmoid_fwd, _sigmoid_bwd)


def _log1p_exp_neg_abs(x):
    e = jnp.exp(-jnp.abs(x))
    return jnp.where(e < 1e-4, e - 0.5 * e * e, jnp.log(1.0 + e))


@jax.custom_vjp
def _softplus(x):
    return jnp.maximum(x, 0.0) + _log1p_exp_neg_abs(x)


def _softplus_fwd(x):
    return _softplus(x), x


def _softplus_bwd(x, g):
    return (g * _sigmoid(x),)


_softplus.defvjp(_softplus_fwd, _softplus_bwd)


def _silu(x):
    return x * _sigmoid(x)


def _silu_grad(x):
    s = _sigmoid(x)
    return s * (1.0 + x * (1.0 - s))


def _dot(a, b, dims, precision=None):
    return lax.dot_general(a, b, (dims, ((), ())), precision=precision, preferred_element_type=F32)


NN = ((1,), (0,))
NT = ((1,), (1,))
TN = ((0,), (0,))


def _bdot(a, b, dims):
    return _dot(a.astype(BF16), b.astype(BF16), dims)


def _mm(a, b, mode, *, tm, tn, tk, name, a_fn=None, epi=None, c=None, scale=1.0, b_dev=False, out_dev=False):
    if mode == "NN":
        m, kk = a.shape
        n = b.shape[2] * N_DEV if b_dev else b.shape[1]
    elif mode == "NT":
        m, kk = a.shape
        n = b.shape[1] if b_dev else b.shape[0]
    else:
        kk, m = a.shape
        n = b.shape[1]
    assert m % tm == 0 and n % tn == 0 and kk % tk == 0, (name, m, n, kk, tm, tn, tk)
    nk = kk // tk
    dims = {"NN": NN, "NT": NT, "TN": TN}[mode]

    if mode == "TN":
        a_spec = pl.BlockSpec((tk, tm), lambda i, j, k: (k, i))
    else:
        a_spec = pl.BlockSpec((tm, tk), lambda i, j, k: (i, k))
    if mode == "NN":
        if b_dev:
            assert tn == b.shape[2]
            b_spec = pl.BlockSpec((None, tk, tn), lambda i, j, k: (j, k, 0))
        else:
            b_spec = pl.BlockSpec((tk, tn), lambda i, j, k: (k, j))
    elif mode == "NT":
        if b_dev:
            assert tk == b.shape[2]
            b_spec = pl.BlockSpec((None, tn, tk), lambda i, j, k: (k, j, 0))
        else:
            b_spec = pl.BlockSpec((tn, tk), lambda i, j, k: (j, k))
    else:
        b_spec = pl.BlockSpec((tk, tn), lambda i, j, k: (k, j))
    in_specs = [a_spec, b_spec]
    operands = [a, b]
    if epi is not None:
        in_specs.append(pl.BlockSpec((tm, tn), lambda i, j, k: (i, j)))
        operands.append(c)
    if out_dev:
        assert tn == n // N_DEV
        out_shape = jax.ShapeDtypeStruct((N_DEV, m, tn), F32)
        out_spec = pl.BlockSpec((None, tm, tn), lambda i, j, k: (j, i, 0))
    else:
        out_shape = jax.ShapeDtypeStruct((m, n), F32)
        out_spec = pl.BlockSpec((tm, tn), lambda i, j, k: (i, j))

    def body(*refs):
        a_ref, b_ref = refs[0], refs[1]
        c_ref = refs[2] if epi is not None else None
        o_ref = refs[3] if epi is not None else refs[2]
        acc_ref = refs[-1] if nk > 1 else None
        av = a_ref[...]
        if a_fn == "relu2":
            av = jnp.square(jnp.maximum(av, 0.0))
        p = _dot(av.astype(BF16), b_ref[...].astype(BF16), dims)

        def finish(acc):
            if epi == "add":
                acc = acc + scale * c_ref[...]
            elif epi == "relu2grad":
                acc = acc * (2.0 * jnp.maximum(c_ref[...], 0.0))
            o_ref[...] = acc

        if nk == 1:
            finish(p)
        else:
            k = pl.program_id(2)

            @pl.when(k == 0)
            def _():
                acc_ref[...] = p

            @pl.when(k > 0)
            def _():
                acc_ref[...] += p

            @pl.when(k == nk - 1)
            def _():
                finish(acc_ref[...])

    return pl.pallas_call(
        body, name=name, grid=(m // tm, n // tn, nk), in_specs=in_specs, out_specs=out_spec, out_shape=out_shape,
        scratch_shapes=[pltpu.VMEM((tm, tn), F32)] if nk > 1 else [],
        compiler_params=_cp(("parallel", "parallel", "arbitrary")),
    )(*operands)


def _ln_fwd(a, b, g, beta, *, name):
    lp, d = a.shape
    tm = _row_tile(lp, 512)

    def body(a_ref, b_ref, g_ref, be_ref, y_ref):
        pre = DN_ALPHA * a_ref[...] + b_ref[...]
        mu = jnp.mean(pre, axis=-1, keepdims=True)
        xc = pre - mu
        var = jnp.mean(xc * xc, axis=-1, keepdims=True)
        y_ref[...] = xc * lax.rsqrt(var + LN_EPS) * g_ref[...] + be_ref[...]

    row = pl.BlockSpec((tm, d), lambda i: (i, 0))
    vec = pl.BlockSpec((1, d), lambda i: (0, 0))
    return pl.pallas_call(
        body, name=name, grid=(lp // tm,), in_specs=[row, row, vec, vec], out_specs=row,
        out_shape=jax.ShapeDtypeStruct((lp, d), F32), compiler_params=_cp(("parallel",)),
    )(a, b, g.reshape(1, d), beta.reshape(1, d))


def _ln_bwd(a, b, g, dy, *, name):
    lp, d = a.shape
    tm = _row_tile(lp, 512)

    def body(a_ref, b_ref, g_ref, dy_ref, dpre_ref, dg_ref, db_ref):
        pre = DN_ALPHA * a_ref[...] + b_ref[...]
        mu = jnp.mean(pre, axis=-1, keepdims=True)
        xc = pre - mu
        var = jnp.mean(xc * xc, axis=-1, keepdims=True)
        rstd = lax.rsqrt(var + LN_EPS)
        xhat = xc * rstd
        dyv = dy_ref[...]
        dxh = dyv * g_ref[...]
        m1 = jnp.mean(dxh, axis=-1, keepdims=True)
        m2 = jnp.mean(dxh * xhat, axis=-1, keepdims=True)
        dpre_ref[...] = rstd * (dxh - m1 - xhat * m2)

        @pl.when(pl.program_id(0) == 0)
        def _():
            dg_ref[...] = jnp.zeros_like(dg_ref)
            db_ref[...] = jnp.zeros_like(db_ref)

        dg_ref[...] += jnp.sum(dyv * xhat, axis=0, keepdims=True)
        db_ref[...] += jnp.sum(dyv, axis=0, keepdims=True)

    row = pl.BlockSpec((tm, d), lambda i: (i, 0))
    vec = pl.BlockSpec((1, d), lambda i: (0, 0))
    return pl.pallas_call(
        body, name=name, grid=(lp // tm,), in_specs=[row, row, vec, row], out_specs=[row, vec, vec],
        out_shape=[jax.ShapeDtypeStruct((lp, d), F32), jax.ShapeDtypeStruct((1, d), F32),
                   jax.ShapeDtypeStruct((1, d), F32)],
        compiler_params=_cp(("arbitrary",)),
    )(a, b, g.reshape(1, d), dy)


def _loss_head(y, target, *, name):
    lp, d = y.shape
    seq = target.shape[0]
    tm = SB_BLOCK
    first = (lp - seq) // tm
    assert (lp - seq) % tm == 0 and seq % tm == 0

    def body(y_ref, t_ref, dy_ref, loss_ref):
        i = pl.program_id(0)
        live = i >= first
        diff = jnp.where(live, y_ref[...] - t_ref[...], 0.0)
        dy_ref[...] = diff * (1.0 / d)

        @pl.when(i == 0)
        def _():
            loss_ref[...] = jnp.zeros_like(loss_ref)

        loss_ref[...] += jnp.sum(diff * diff, axis=0, keepdims=True) * (0.5 / d)

    return pl.pallas_call(
        body, name=name, grid=(lp // tm,),
        in_specs=[pl.BlockSpec((tm, d), lambda i: (i, 0)),
                  pl.BlockSpec((tm, d), lambda i: (jnp.maximum(i - first, 0), 0))],
        out_specs=[pl.BlockSpec((tm, d), lambda i: (i, 0)), pl.BlockSpec((1, d), lambda i: (0, 0))],
        out_shape=[jax.ShapeDtypeStruct((lp, d), F32), jax.ShapeDtypeStruct((1, d), F32)],
        compiler_params=_cp(("arbitrary",)),
    )(y, target)


def _gate_fwd(o, zsrc, z_blk0, g, other, *, heads, name):
    lp = o.shape[0]
    tm = _row_tile(lp, 512)
    nblk = D_MODEL // HEAD_W

    def body(o_ref, z_ref, g_ref, x_ref, y_ref):
        h = pl.program_id(1)

        @pl.when(h < heads)
        def _():
            ov = o_ref[...]
            r = lax.rsqrt(jnp.mean(ov * ov, axis=-1, keepdims=True) + RMS_EPS)
            y_ref[...] = ov * r * g_ref[...] * _silu(z_ref[...])

        @pl.when(h >= heads)
        def _():
            y_ref[...] = x_ref[...]

    other_w = other.shape[1] // HEAD_W
    return pl.pallas_call(
        body, name=name, grid=(lp // tm, nblk),
        in_specs=[pl.BlockSpec((tm, HEAD_W), lambda i, h: (i, jnp.minimum(h, heads - 1))),
                  pl.BlockSpec((tm, HEAD_W), lambda i, h: (i, z_blk0 + jnp.minimum(h, heads - 1))),
                  pl.BlockSpec((1, HEAD_W), lambda i, h: (0, 0)),
                  pl.BlockSpec((tm, HEAD_W), lambda i, h: (i, jnp.clip(h - heads, 0, other_w - 1)))],
        out_specs=pl.BlockSpec((tm, HEAD_W), lambda i, h: (i, h)),
        out_shape=jax.ShapeDtypeStruct((lp, D_MODEL), F32),
        compiler_params=_cp(("parallel", "arbitrary")),
    )(o, zsrc, g.reshape(1, HEAD_W), other)


def _gate_bwd(o, zsrc, z_blk0, g, dy, *, heads, name):
    lp = o.shape[0]
    tm = _row_tile(lp, 512)

    def body(o_ref, z_ref, g_ref, dy_ref, do_ref, dz_ref, dg_ref):
        ov, zv, gv, dyv = o_ref[...], z_ref[...], g_ref[...], dy_ref[...]
        r = lax.rsqrt(jnp.mean(ov * ov, axis=-1, keepdims=True) + RMS_EPS)
        nrm = ov * r
        s = _silu(zv)
        dn = dyv * gv * s
        do_ref[...] = r * (dn - nrm * jnp.mean(dn * nrm, axis=-1, keepdims=True))
        dz_ref[...] = dyv * nrm * gv * _silu_grad(zv)

        @pl.when((pl.program_id(0) == 0) & (pl.program_id(1) == 0))
        def _():
            dg_ref[...] = jnp.zeros_like(dg_ref)

        dg_ref[...] += jnp.sum(dyv * nrm * s, axis=0, keepdims=True)

    blk = pl.BlockSpec((tm, HEAD_W), lambda i, h: (i, h))
    return pl.pallas_call(
        body, name=name, grid=(lp // tm, heads),
        in_specs=[blk, pl.BlockSpec((tm, HEAD_W), lambda i, h: (i, z_blk0 + h)),
                  pl.BlockSpec((1, HEAD_W), lambda i, h: (0, 0)), blk],
        out_specs=[blk, blk, pl.BlockSpec((1, HEAD_W), lambda i, h: (0, 0))],
        out_shape=[jax.ShapeDtypeStruct((lp, heads * HEAD_W), F32), jax.ShapeDtypeStruct((lp, heads * HEAD_W), F32),
                   jax.ShapeDtypeStruct((1, HEAD_W), F32)],
        compiler_params=_cp(("arbitrary", "arbitrary")),
    )(o, zsrc, g.reshape(1, HEAD_W), dy)


def _conv_taps(x, w):
    acc = w[CONV_K - 1:CONV_K, :] * x
    for k in range(CONV_K - 1):
        acc = acc + w[k:k + 1, :] * pltpu.roll(x, CONV_K - 1 - k, 0)
    return acc


def _gdn_pre_fwd(p0, conv_w, pad, *, name):
    lp = p0.shape[0]
    nq = GDN_HEADS
    qscale = HEAD_W ** -0.5

    def body(x_ref, w_ref, y_ref):
        j = pl.program_id(0)
        c = _conv_taps(x_ref[...], w_ref[...])
        s = _silu(c)
        r = lax.rsqrt(jnp.sum(s * s, axis=-1, keepdims=True) + L2_EPS)
        mult = jnp.where(j < nq, r * qscale, jnp.where(j < 2 * nq, r, 1.0))
        rows = lax.broadcasted_iota(jnp.int32, (lp, 1), 0)
        y_ref[...] = jnp.where(rows >= pad, s * mult, 0.0)

    return pl.pallas_call(
        body, name=name, grid=(3 * nq,),
        in_specs=[pl.BlockSpec((lp, HEAD_W), lambda j: (0, j)), pl.BlockSpec((CONV_K, HEAD_W), lambda j: (0, j))],
        out_specs=pl.BlockSpec((lp, HEAD_W), lambda j: (0, j)),
        out_shape=jax.ShapeDtypeStruct((lp, 3 * nq * HEAD_W), F32), compiler_params=_cp(("parallel",)),
    )(p0, conv_w)


def _gdn_pre_bwd(p0, conv_w, dqkv, pad, *, name):
    lp = p0.shape[0]
    nq = GDN_HEADS
    qscale = HEAD_W ** -0.5

    def body(x_ref, w_ref, dy_ref, dx_ref, dw_ref):
        j = pl.program_id(0)
        x, w = x_ref[...], w_ref[...]
        c = _conv_taps(x, w)
        s = _silu(c)
        r = lax.rsqrt(jnp.sum(s * s, axis=-1, keepdims=True) + L2_EPS)
        rows = lax.broadcasted_iota(jnp.int32, (lp, 1), 0)
        dy = jnp.where(rows >= pad, dy_ref[...], 0.0)
        nrm = s * r
        dn = dy * jnp.where(j < nq, qscale, 1.0)
        ds_norm = r * (dn - nrm * jnp.sum(nrm * dn, axis=-1, keepdims=True))
        ds = jnp.where(j < 2 * nq, ds_norm, dy)
        dc = ds * _silu_grad(c)
        dx = w[CONV_K - 1:CONV_K, :] * dc
        dws = [None] * CONV_K
        dws[CONV_K - 1] = jnp.sum(dc * x, axis=0, keepdims=True)
        for k in range(CONV_K - 1):
            sh = CONV_K - 1 - k
            dx = dx + w[k:k + 1, :] * pltpu.roll(dc, lp - sh, 0)
            dws[k] = jnp.sum(dc * pltpu.roll(x, sh, 0), axis=0, keepdims=True)
        dx_ref[...] = dx
        dw_ref[...] = jnp.concatenate(dws, axis=0)

    blk = pl.BlockSpec((lp, HEAD_W), lambda j: (0, j))
    wblk = pl.BlockSpec((CONV_K, HEAD_W), lambda j: (0, j))
    return pl.pallas_call(
        body, name=name, grid=(3 * nq,), in_specs=[blk, wblk, blk], out_specs=[blk, wblk],
        out_shape=[jax.ShapeDtypeStruct((lp, 3 * nq * HEAD_W), F32),
                   jax.ShapeDtypeStruct((CONV_K, 3 * nq * HEAD_W), F32)],
        compiler_params=_cp(("parallel",)),
    )(p0, conv_w, dqkv)


def _tri(c, strict):
    r = lax.broadcasted_iota(jnp.int32, (c, c), 0)
    q = lax.broadcasted_iota(jnp.int32, (c, c), 1)
    return (q < r) if strict else (q <= r)


@jax.custom_vjp
def _inv_unit_lower(m):
    c = m.shape[0]
    eye = (lax.broadcasted_iota(jnp.int32, (c, c), 0) == lax.broadcasted_iota(jnp.int32, (c, c), 1)).astype(F32)
    x = eye - m
    p = m
    n = 2
    while n < c:
        p = _dot3x(p, p, NN)
        x = x + _dot3x(x, p, NN)
        n *= 2
    return x


def _inv_fwd(m):
    t = _inv_unit_lower(m)
    return t, t


def _inv_bwd(t, g):
    return (-_dot3x(_dot3x(t, g, TN), t, NT),)


_inv_unit_lower.defvjp(_inv_fwd, _inv_bwd)


def _gdn_chunk(q, k, v, ba, alog, dtb, s, head, valid):
    c = q.shape[0]
    lane = lax.broadcasted_iota(jnp.int32, (1, HEAD_W), 1)
    pick = lambda x, l: jnp.sum(jnp.where(lane == l, x, 0.0), axis=-1, keepdims=True)
    beta = jnp.where(valid, _sigmoid(pick(ba, head)), 0.0)
    g = jnp.where(valid, -jnp.exp(pick(alog, head)) * _softplus(pick(ba, GDN_HEADS + head) + pick(dtb, head)), 0.0)
    causal, strict = _tri(c, False), _tri(c, True)
    rr = lax.broadcasted_iota(jnp.int32, (c, c), 0)
    cc = lax.broadcasted_iota(jnp.int32, (c, c), 1)
    lower = jnp.where(cc <= rr, 1.0, 0.0).astype(BF16)
    upper = jnp.where(cc >= rr, 1.0, 0.0).astype(BF16)
    gcb = _mask_mm(lower, upper, g * jnp.ones((1, HEAD_W), F32))
    gc_col = gcb[:, :c]
    gc_row = gcb.T[:c, :]
    decay = jnp.where(causal, jnp.exp(jnp.minimum(gc_col - gc_row, 0.0)), 0.0)
    egc = jnp.exp(gcb)
    kb = k * beta
    m = jnp.where(strict, _bdot(kb, k, NT) * decay, 0.0)
    t = _inv_unit_lower(m)
    u = _bdot(t, v * beta, NN)
    w = _bdot(t, kb * egc, NN)
    a = _bdot(q, k, NT) * decay
    gl = gcb[c - 1:c, :]
    v_new = u - _bdot(w, s, NN)
    o = _bdot(q * egc, s, NN) + _bdot(a, v_new, NN)
    s2 = s * jnp.exp(gl) + _bdot(k * jnp.exp(gl - gcb), v_new, TN)
    return o, s2


def _gdn_fwd(qkv, p0, alog_v, dtb_v, pad, *, name):
    lp = qkv.shape[0]
    n = lp // CHUNK
    nh = GDN_HEADS

    def body(q_ref, k_ref, v_ref, ba_ref, al_ref, dt_ref, o_ref, st_ref, s_ref):
        i = pl.program_id(0)

        @pl.when(i == 0)
        def _():
            s_ref[...] = jnp.zeros_like(s_ref)

        valid = (i * CHUNK + lax.broadcasted_iota(jnp.int32, (CHUNK, 1), 0)) >= pad
        q, k, v, ba, al, dt, s = q_ref[...], k_ref[...], v_ref[...], ba_ref[...], al_ref[...], dt_ref[...], s_ref[...]
        res = []
        for h in range(nh):
            cs = slice(h * HEAD_W, (h + 1) * HEAD_W)
            res.append(_gdn_chunk(q[:, cs], k[:, cs], v[:, cs], ba, al, dt, s[h], h, valid))
        st_ref[...] = s
        o_ref[...] = jnp.concatenate([r[0] for r in res], axis=1)
        for h in range(nh):
            s_ref[h] = res[h][1]

    w = nh * HEAD_W
    vec = pl.BlockSpec((1, HEAD_W), lambda i: (0, 0))
    return pl.pallas_call(
        body, name=name, grid=(n,),
        in_specs=[pl.BlockSpec((CHUNK, w), lambda i: (i, 0)), pl.BlockSpec((CHUNK, w), lambda i: (i, 1)),
                  pl.BlockSpec((CHUNK, w), lambda i: (i, 2)), pl.BlockSpec((CHUNK, HEAD_W), lambda i: (i, AB_BA // HEAD_W)),
                  vec, vec],
        out_specs=[pl.BlockSpec((CHUNK, w), lambda i: (i, 0)),
                   pl.BlockSpec((None, nh, HEAD_W, HEAD_W), lambda i: (i, 0, 0, 0))],
        out_shape=[jax.ShapeDtypeStruct((lp, w), F32), jax.ShapeDtypeStruct((n, nh, HEAD_W, HEAD_W), F32)],
        scratch_shapes=[pltpu.VMEM((nh, HEAD_W, HEAD_W), F32)],
        compiler_params=_cp(("arbitrary",)),
    )(qkv, qkv, qkv, p0, alog_v, dtb_v)


def _gdn_bwd(qkv, p0, alog_v, dtb_v, states, do, pad, *, name):
    lp = qkv.shape[0]
    n = lp // CHUNK
    nh = GDN_HEADS

    def body(q_ref, k_ref, v_ref, ba_ref, al_ref, dt_ref, st_ref, do_ref,
             dq_ref, dk_ref, dv_ref, dba_ref, dal_ref, ddt_ref, ds_ref):
        step = pl.program_id(0)
        i = n - 1 - step

        @pl.when(step == 0)
        def _():
            ds_ref[...] = jnp.zeros_like(ds_ref)
            dal_ref[...] = jnp.zeros_like(dal_ref)
            ddt_ref[...] = jnp.zeros_like(ddt_ref)

        valid = (i * CHUNK + lax.broadcasted_iota(jnp.int32, (CHUNK, 1), 0)) >= pad
        q, k, v, ba, al, dt = q_ref[...], k_ref[...], v_ref[...], ba_ref[...], al_ref[...], dt_ref[...]
        st, do, dst = st_ref[...], do_ref[...], ds_ref[...]
        res = []
        for h in range(nh):
            cs = slice(h * HEAD_W, (h + 1) * HEAD_W)
            fn = functools.partial(_gdn_chunk, head=h, valid=valid)
            _, vjp = jax.vjp(fn, q[:, cs], k[:, cs], v[:, cs], ba, al, dt, st[h])
            res.append(vjp((do[:, cs], dst[h])))
        dq_ref[...] = jnp.concatenate([r[0] for r in res], axis=1)
        dk_ref[...] = jnp.concatenate([r[1] for r in res], axis=1)
        dv_ref[...] = jnp.concatenate([r[2] for r in res], axis=1)
        dba_ref[...] = sum(r[3] for r in res)
        dal_ref[...] += sum(r[4] for r in res)
        ddt_ref[...] += sum(r[5] for r in res)
        for h in range(nh):
            ds_ref[h] = res[h][6]

    w = nh * HEAD_W
    rev = lambda c: (lambda s: (n - 1 - s, c))
    vec = pl.BlockSpec((1, HEAD_W), lambda s: (0, 0))
    dq, dk, dv, dba, dal, ddt = pl.pallas_call(
        body, name=name, grid=(n,),
        in_specs=[pl.BlockSpec((CHUNK, w), rev(0)), pl.BlockSpec((CHUNK, w), rev(1)), pl.BlockSpec((CHUNK, w), rev(2)),
                  pl.BlockSpec((CHUNK, HEAD_W), rev(AB_BA // HEAD_W)), vec, vec,
                  pl.BlockSpec((None, nh, HEAD_W, HEAD_W), lambda s: (n - 1 - s, 0, 0, 0)),
                  pl.BlockSpec((CHUNK, w), rev(0))],
        out_specs=[pl.BlockSpec((CHUNK, w), rev(0)), pl.BlockSpec((CHUNK, w), rev(0)), pl.BlockSpec((CHUNK, w), rev(0)),
                   pl.BlockSpec((CHUNK, HEAD_W), rev(0)), vec, vec],
        out_shape=[jax.ShapeDtypeStruct((lp, w), F32)] * 3 + [jax.ShapeDtypeStruct((lp, HEAD_W), F32)]
        + [jax.ShapeDtypeStruct((1, HEAD_W), F32)] * 2,
        scratch_shapes=[pltpu.VMEM((nh, HEAD_W, HEAD_W), F32)],
        compiler_params=_cp(("arbitrary",)),
    )(qkv, qkv, qkv, p0, alog_v, dtb_v, states, do)
    return dq, dk, dv, dba, dal, ddt


HG_LEVELS = (32, 16, 8, 4, 2, 1)


def _hg_masks():
    import numpy as np
    c = CHUNK
    t = np.arange(c)[:, None]
    j = np.arange(c)[None, :]
    sums = [j <= t, j > t]
    pairs = [j == t]
    for m in HG_LEVELS:
        p = (t // (2 * m)) * (2 * m)
        r = p + m
        upper = t >= r
        sums.append(upper & (j > r) & (j <= t))
        sums.append(~upper & (j > t) & (j <= r))
        pairs.append(upper & (j < r) & (j >= p))
    sums = np.concatenate(sums, axis=0).astype(np.float32)
    pairs = np.concatenate(pairs, axis=0).astype(np.float32)
    return jnp.asarray(sums, BF16), jnp.asarray(sums.T, BF16), jnp.asarray(pairs, F32)


def _split3(x):
    hi = x.astype(BF16)
    r1 = x - hi.astype(F32)
    mid = r1.astype(BF16)
    return hi, mid, (r1 - mid.astype(F32)).astype(BF16)


def _dot3x(a, b, dims):
    ah, am, _ = _split3(a)
    bh, bm, _ = _split3(b)
    return _dot(ah, bh, dims) + (_dot(ah, bm, dims) + _dot(am, bh, dims))


def _mask_mm_raw(m, x):
    return sum(_dot(m, part, NN) for part in _split3(x))


@jax.custom_vjp
def _mask_mm(m, mt, x):
    return _mask_mm_raw(m, x)


def _mask_mm_fwd(m, mt, x):
    return _mask_mm_raw(m, x), (m, mt)


def _mask_mm_bwd(res, g):
    m, mt = res
    return jnp.zeros_like(m), jnp.zeros_like(mt), _mask_mm_raw(mt, g)


_mask_mm.defvjp(_mask_mm_fwd, _mask_mm_bwd)


def _hg_chunk(qr, fr, ir, lb, st, valid, sums, sums_t, pairs):
    c = qr.shape[0]
    fg = lb + (1.0 - lb) * _sigmoid(fr)
    logf = jnp.where(valid, jnp.log(fg), 0.0)
    k = jnp.where(valid, 1.0 - fg, 0.0)
    qs = jnp.where(valid, _silu(qr), 0.0)
    v = jnp.where(valid, ir, 0.0)
    e = jnp.exp(_mask_mm(sums, sums_t, logf))
    blk = lambda x, n: x[n * c:(n + 1) * c]
    a = blk(pairs, 0) * _bdot(qs, k, NT)
    for lvl in range(len(HG_LEVELS)):
        a = a + blk(pairs, 1 + lvl) * _bdot(qs * blk(e, 2 + 2 * lvl), k * blk(e, 3 + 2 * lvl), NT)
    eb = blk(e, 0)
    o = _bdot(qs * eb, st, NT) + _bdot(a, v, NN)
    st2 = st * eb[c - 1:c, :] + _bdot(v, k * blk(e, 1), TN)
    return o, st2


def _hg_fwd(p1, lb, pad, *, name):
    lp = p1.shape[0]
    n = lp // CHUNK
    nh = HG_HEADS

    def body(q_ref, f_ref, i_ref, lb_ref, sums_ref, sums_t_ref, pairs_ref, o_ref, st_ref, s_ref):
        i = pl.program_id(1)

        @pl.when(i == 0)
        def _():
            s_ref[...] = jnp.zeros_like(s_ref)

        valid = (i * CHUNK + lax.broadcasted_iota(jnp.int32, (CHUNK, 1), 0)) >= pad
        s = s_ref[...]
        o, s2 = _hg_chunk(q_ref[...], f_ref[...], i_ref[...], lb_ref[...], s, valid,
                          sums_ref[...], sums_t_ref[...], pairs_ref[...])
        st_ref[...] = s
        o_ref[...] = o
        s_ref[...] = s2

    masks = _hg_masks()
    blk = lambda off: pl.BlockSpec((CHUNK, HEAD_W), lambda h, i: (i, off + h))
    const = lambda a: pl.BlockSpec(a.shape, lambda h, i: (0, 0))
    return pl.pallas_call(
        body, name=name, grid=(nh, n),
        in_specs=[blk(0), blk(nh), blk(2 * nh), pl.BlockSpec((1, HEAD_W), lambda h, i: (0, h))]
        + [const(a) for a in masks],
        out_specs=[blk(0), pl.BlockSpec((None, None, HEAD_W, HEAD_W), lambda h, i: (h, i, 0, 0))],
        out_shape=[jax.ShapeDtypeStruct((lp, nh * HEAD_W), F32), jax.ShapeDtypeStruct((nh, n, HEAD_W, HEAD_W), F32)],
        scratch_shapes=[pltpu.VMEM((HEAD_W, HEAD_W), F32)],
        compiler_params=_cp(("parallel", "arbitrary")),
    )(p1, p1, p1, lb, *masks)


def _hg_bwd(p1, lb, states, do, pad, *, name):
    lp = p1.shape[0]
    n = lp // CHUNK
    nh = HG_HEADS

    def body(q_ref, f_ref, i_ref, lb_ref, st_ref, do_ref, sums_ref, sums_t_ref, pairs_ref,
             dq_ref, df_ref, di_ref, dlb_ref, ds_ref):
        step = pl.program_id(1)
        i = n - 1 - step

        @pl.when(step == 0)
        def _():
            ds_ref[...] = jnp.zeros_like(ds_ref)
            dlb_ref[...] = jnp.zeros_like(dlb_ref)

        valid = (i * CHUNK + lax.broadcasted_iota(jnp.int32, (CHUNK, 1), 0)) >= pad
        fn = functools.partial(_hg_chunk, valid=valid, sums=sums_ref[...], sums_t=sums_t_ref[...],
                               pairs=pairs_ref[...])
        _, vjp = jax.vjp(fn, q_ref[...], f_ref[...], i_ref[...], lb_ref[...], st_ref[...])
        dq, df, di, dlb, ds = vjp((do_ref[...], ds_ref[...]))
        dq_ref[...] = dq
        df_ref[...] = df
        di_ref[...] = di
        dlb_ref[...] += dlb
        ds_ref[...] = ds

    masks = _hg_masks()
    blk = lambda off: pl.BlockSpec((CHUNK, HEAD_W), lambda h, s: (n - 1 - s, off + h))
    const = lambda a: pl.BlockSpec(a.shape, lambda h, s: (0, 0))
    w = nh * HEAD_W
    return pl.pallas_call(
        body, name=name, grid=(nh, n),
        in_specs=[blk(0), blk(nh), blk(2 * nh), pl.BlockSpec((1, HEAD_W), lambda h, s: (0, h)),
                  pl.BlockSpec((None, None, HEAD_W, HEAD_W), lambda h, s: (h, n - 1 - s, 0, 0)), blk(0)]
        + [const(a) for a in masks],
        out_specs=[blk(0), blk(0), blk(0), pl.BlockSpec((1, HEAD_W), lambda h, s: (0, h))],
        out_shape=[jax.ShapeDtypeStruct((lp, w), F32)] * 3 + [jax.ShapeDtypeStruct((1, w), F32)],
        scratch_shapes=[pltpu.VMEM((HEAD_W, HEAD_W), F32)],
        compiler_params=_cp(("parallel", "arbitrary")),
    )(p1, p1, p1, lb, states, do, *masks)


SB_GROUP = 4


def _sb_cat(kind, first_key=0):
    r = lax.broadcasted_iota(jnp.int32, (SB_BLOCK, 2 * SB_BLOCK), 0)
    c = lax.broadcasted_iota(jnp.int32, (SB_BLOCK, 2 * SB_BLOCK), 1)
    tri = {"after": c < r, "incl": r <= c, "before": r < c}[kind]
    m = ((c >= SB_BLOCK) | tri) & (r >= first_key)
    return jnp.where(m, 1.0, 0.0).astype(BF16)


def _sb_cumsum(x, cat):
    hi = x.astype(BF16)
    lo = (x - hi.astype(F32)).astype(BF16)
    return _dot(hi, cat, NN) + _dot(lo, cat, NN)


def _sb_logsig(z):
    e = jnp.exp(-jnp.abs(z))
    lse = jnp.where(e < 1e-4, e, jnp.log(1.0 + e))
    lsz = jnp.minimum(z, 0.0) - lse
    return lsz, lsz - z, e


def _sb_stack(x, scale=None):
    lane = lax.broadcasted_iota(jnp.int32, (1, HEAD_W), 1)
    if scale is not None:
        x = x * scale
    return jnp.concatenate([jnp.where(lane < SB_DH, x, 0.0), jnp.where(lane >= SB_DH, x, 0.0)], axis=0).astype(BF16)


def _sb_unstack(x):
    lane = lax.broadcasted_iota(jnp.int32, (1, HEAD_W), 1)
    return jnp.where(lane < SB_DH, x[:SB_BLOCK], x[SB_BLOCK:])


def _sb_fwd(p0, pad, *, name):
    lp = p0.shape[0]
    nb = lp // SB_BLOCK
    npair = SB_HEADS // 2
    blk0 = AB_SB // HEAD_W
    scale = SB_DH ** -0.5
    gw = SB_GROUP * SB_BLOCK
    assert pad < SB_BLOCK

    def body(q_ref, k_ref, v_ref, o_ref, tot_ref):
        i = pl.program_id(1)
        qs = _sb_stack(q_ref[...], scale)
        qpos = i * SB_BLOCK + lax.broadcasted_iota(jnp.int32, (SB_BLOCK, 1), 0)
        qpos = jnp.concatenate([qpos, qpos], axis=0)
        cat = _sb_cat("after")
        cat0 = _sb_cat("after", pad)
        ng = i // SB_GROUP

        def single(t, carry):
            acc, run = carry
            kb = i - t
            off = pl.multiple_of(kb * SB_BLOCK, SB_BLOCK)
            kblk = k_ref[pl.ds(off, SB_BLOCK), :].astype(BF16)
            vblk = v_ref[pl.ds(off, SB_BLOCK), :].astype(BF16)
            kpos = kb * SB_BLOCK + lax.broadcasted_iota(jnp.int32, (1, SB_BLOCK), 1)
            allowed = (kpos < qpos) & (kpos >= pad)
            lsz, l1m, _ = _sb_logsig(_dot(qs, kblk, NT))
            al = _sb_cumsum(jnp.where(allowed, l1m, 0.0), cat)
            wgt = jnp.where(allowed, jnp.exp(lsz + al[:, :SB_BLOCK] + run), 0.0)
            return acc + _dot(wgt.astype(BF16), vblk, NN), run + al[:, SB_BLOCK:]

        def group(t, carry):
            acc, run = carry
            gi = ng - 1 - t
            off = pl.multiple_of(gi * gw, gw)
            kg = k_ref[pl.ds(off, gw), :].astype(BF16)
            vg = v_ref[pl.ds(off, gw), :].astype(BF16)
            lsz, l1m, _ = _sb_logsig(_dot(qs, kg, NT))
            args = [None] * SB_GROUP
            for g in reversed(range(SB_GROUP)):
                sl = slice(g * SB_BLOCK, (g + 1) * SB_BLOCK)
                al = _sb_cumsum(l1m[:, sl], jnp.where(gi == 0, cat0, cat) if g == 0 else cat)
                args[g] = lsz[:, sl] + al[:, :SB_BLOCK] + run
                run = run + al[:, SB_BLOCK:]
            wgt = jnp.exp(jnp.concatenate(args, axis=1)).astype(BF16)
            return acc + _dot(wgt, vg, NN), run

        zero = (jnp.zeros((2 * SB_BLOCK, HEAD_W), F32), jnp.zeros((2 * SB_BLOCK, HEAD_W), F32))
        carry = lax.fori_loop(0, i + 1 - ng * SB_GROUP, single, zero)
        acc, run = lax.fori_loop(0, ng, group, carry)
        o_ref[...] = _sb_unstack(acc)
        tot_ref[...] = _sb_unstack(run)

    full = lambda c0: pl.BlockSpec((lp, HEAD_W), lambda p, i: (0, c0 + p))
    out = pl.BlockSpec((SB_BLOCK, HEAD_W), lambda p, i: (i, p))
    return pl.pallas_call(
        body, name=name, grid=(npair, nb),
        in_specs=[pl.BlockSpec((SB_BLOCK, HEAD_W), lambda p, i: (i, blk0 + p)), full(blk0 + npair), full(blk0 + 2 * npair)],
        out_specs=[out, out],
        out_shape=[jax.ShapeDtypeStruct((lp, npair * HEAD_W), F32)] * 2,
        compiler_params=_cp(("parallel", "arbitrary")),
    )(p0, p0, p0)


def _sb_bwd(p0, tot, dsrc, d_blk0, pad, *, name):
    lp = p0.shape[0]
    nb = lp // SB_BLOCK
    npair = SB_HEADS // 2
    blk0 = AB_SB // HEAD_W
    scale = SB_DH ** -0.5
    gw = SB_GROUP * SB_BLOCK
    assert pad < SB_BLOCK

    def body(q_ref, k_ref, v_ref, tot_ref, do_ref, dq_ref, dkt_ref, dvt_ref):
        i = pl.program_id(1)

        @pl.when(i == 0)
        def _():
            dkt_ref[...] = jnp.zeros_like(dkt_ref)
            dvt_ref[...] = jnp.zeros_like(dvt_ref)

        qs = _sb_stack(q_ref[...], scale)
        dos = _sb_stack(do_ref[...])
        qst, dost = qs.T, dos.T
        totv = tot_ref[...]
        ones = jnp.ones((1, HEAD_W), F32)
        tots = jnp.concatenate([totv[:, 0:1] * ones, totv[:, SB_DH:SB_DH + 1] * ones], axis=0)
        qpos = i * SB_BLOCK + lax.broadcasted_iota(jnp.int32, (SB_BLOCK, 1), 0)
        qpos = jnp.concatenate([qpos, qpos], axis=0)
        incl, incl0 = _sb_cat("incl"), _sb_cat("incl", pad)
        before = _sb_cat("before")
        ng = i // SB_GROUP

        def dscore(z, e, ev, dl1m):
            r = 1.0 / (1.0 + e)
            sg = jnp.where(z >= 0, r, e * r)
            return ev * (1.0 - sg) - dl1m * sg

        def group(gi, carry):
            dq, prun, erun = carry
            off = pl.multiple_of(gi * gw, gw)
            kg = k_ref[pl.ds(off, gw), :].astype(BF16)
            vg = v_ref[pl.ds(off, gw), :].astype(BF16)
            z = _dot(qs, kg, NT)
            lsz, l1m, e = _sb_logsig(z)
            dwgt = _dot(dos, vg, NT)
            dzs = [None] * SB_GROUP
            wgts = [None] * SB_GROUP
            for g in range(SB_GROUP):
                sl = slice(g * SB_BLOCK, (g + 1) * SB_BLOCK)
                al = _sb_cumsum(l1m[:, sl], jnp.where(gi == 0, incl0, incl) if g == 0 else incl)
                wgt = jnp.exp(lsz[:, sl] + (tots - prun - al[:, :SB_BLOCK]))
                prun = prun + al[:, SB_BLOCK:]
                ev = wgt * dwgt[:, sl]
                el = _sb_cumsum(ev, before)
                dzs[g] = dscore(z[:, sl], e[:, sl], ev, erun + el[:, :SB_BLOCK])
                erun = erun + el[:, SB_BLOCK:]
                wgts[g] = wgt
            dz = jnp.concatenate(dzs, axis=1).astype(BF16)
            wg = jnp.concatenate(wgts, axis=1).astype(BF16)
            dkt_ref[:, pl.ds(off, gw)] += _dot(qst, dz, NN)
            dvt_ref[:, pl.ds(off, gw)] += _dot(dost, wg, NN)
            return dq + _dot(dz, kg, NN), prun, erun

        def single(t, carry):
            dq, prun, erun = carry
            kb = ng * SB_GROUP + t
            off = pl.multiple_of(kb * SB_BLOCK, SB_BLOCK)
            kblk = k_ref[pl.ds(off, SB_BLOCK), :].astype(BF16)
            vblk = v_ref[pl.ds(off, SB_BLOCK), :].astype(BF16)
            kpos = kb * SB_BLOCK + lax.broadcasted_iota(jnp.int32, (1, SB_BLOCK), 1)
            allowed = (kpos < qpos) & (kpos >= pad)
            z = _dot(qs, kblk, NT)
            lsz, l1m, e = _sb_logsig(z)
            al = _sb_cumsum(jnp.where(allowed, l1m, 0.0), incl)
            wgt = jnp.where(allowed, jnp.exp(lsz + (tots - prun - al[:, :SB_BLOCK])), 0.0)
            ev = wgt * _dot(dos, vblk, NT)
            el = _sb_cumsum(ev, before)
            dz = jnp.where(allowed, dscore(z, e, ev, erun + el[:, :SB_BLOCK]), 0.0).astype(BF16)
            dkt_ref[:, pl.ds(off, SB_BLOCK)] += _dot(qst, dz, NN)
            dvt_ref[:, pl.ds(off, SB_BLOCK)] += _dot(dost, wgt.astype(BF16), NN)
            return dq + _dot(dz, kblk, NN), prun + al[:, SB_BLOCK:], erun + el[:, SB_BLOCK:]

        zero = tuple(jnp.zeros((2 * SB_BLOCK, HEAD_W), F32) for _ in range(3))
        carry = lax.fori_loop(0, ng, group, zero)
        dq, _, _ = lax.fori_loop(0, i + 1 - ng * SB_GROUP, single, carry)
        dq_ref[...] = _sb_unstack(dq) * scale

    full = lambda c0: pl.BlockSpec((lp, HEAD_W), lambda p, i: (0, c0 + p))
    qb = lambda c0: pl.BlockSpec((SB_BLOCK, HEAD_W), lambda p, i: (i, c0 + p))
    tr = pl.BlockSpec((HEAD_W, lp), lambda p, i: (p, 0))
    return pl.pallas_call(
        body, name=name, grid=(npair, nb),
        in_specs=[qb(blk0), full(blk0 + npair), full(blk0 + 2 * npair), qb(0), qb(d_blk0)],
        out_specs=[qb(0), tr, tr],
        out_shape=[jax.ShapeDtypeStruct((lp, npair * HEAD_W), F32)]
        + [jax.ShapeDtypeStruct((npair * HEAD_W, lp), F32)] * 2,
        compiler_params=_cp(("parallel", "arbitrary")),
    )(p0, p0, p0, tot, dsrc)


def _local_step(h0, target, pad, wts):
    lp = h0.shape[0]
    tm = _row_tile(lp, 1056)
    tkl = tm
    d = D_MODEL
    mm = _mm
    g = {}

    p0 = mm(h0, wts["w_ab"], "NN", tm=tm, tn=768, tk=d, name="l0_in_proj")
    qkv = _gdn_pre_fwd(p0, wts["conv_w"], pad, name="gdn_pre_fwd")
    oa_raw, gdn_states = _gdn_fwd(qkv, p0, wts["alog_v"], wts["dtb_v"], pad, name="gdn_fwd")
    ob, sb_tot = _sb_fwd(p0, pad, name="sb_fwd")
    oab = _gate_fwd(oa_raw, p0, AB_Z // HEAD_W, wts["ab_gn"], ob, heads=GDN_HEADS, name="gdn_gate_fwd")
    mix0 = mm(oab, wts["w_out0"], "NN", tm=tm, tn=512, tk=d, name="l0_out_proj")
    h0a = _ln_fwd(h0, mix0, wts["ln_mix_g"][0], wts["ln_mix_b"][0], name="ln_mix0_fwd")
    u0 = mm(h0a, wts["w1"][0], "NN", tm=tm, tn=512, tk=d, b_dev=True, name="mlp0_up")
    y0 = mm(u0, wts["w2"][0], "NN", tm=tm, tn=512, tk=d, a_fn="relu2", name="mlp0_down")
    h0b = _ln_fwd(h0a, y0, wts["ln_ffn_g"][0], wts["ln_ffn_b"][0], name="ln_ffn0_fwd")
    p1 = mm(h0b, wts["w_c"], "NN", tm=tm, tn=512, tk=d, b_dev=True, name="l1_in_proj")
    oc_raw, hg_states = _hg_fwd(p1, wts["lb"], pad, name="hg_fwd")
    oc = _gate_fwd(oc_raw, p1, 3 * HG_HEADS, wts["c_gn"], oc_raw, heads=HG_HEADS, name="hg_gate_fwd")
    mix1 = mm(oc, wts["w_out1"], "NN", tm=tm, tn=512, tk=d, name="l1_out_proj")
    h1a = _ln_fwd(h0b, mix1, wts["ln_mix_g"][1], wts["ln_mix_b"][1], name="ln_mix1_fwd")
    u1 = mm(h1a, wts["w1"][1], "NN", tm=tm, tn=512, tk=d, b_dev=True, name="mlp1_up")
    y1 = mm(u1, wts["w2"][1], "NN", tm=tm, tn=512, tk=d, a_fn="relu2", name="mlp1_down")
    h1b = _ln_fwd(h1a, y1, wts["ln_ffn_g"][1], wts["ln_ffn_b"][1], name="ln_ffn1_fwd")
    dy, loss_vec = _loss_head(h1b, target, name="loss_head")

    def mlp_bwd(layer, h_in, u, dpre):
        du = mm(dpre, wts["w2"][layer], "NT", tm=tm, tn=512, tk=d, epi="relu2grad", c=u, name=f"mlp{layer}_d_hidden")
        dw2 = mm(u, dpre, "TN", tm=1024, tn=1024, tk=tkl, a_fn="relu2", name=f"mlp{layer}_dw2")
        dw1 = mm(h_in, du, "TN", tm=1024, tn=512, tk=tkl, out_dev=True, name=f"mlp{layer}_dw1")
        dh = mm(du, wts["w1"][layer], "NT", tm=tm, tn=1024, tk=512, b_dev=True, epi="add", c=dpre, scale=DN_ALPHA,
                name=f"mlp{layer}_d_in")
        return dh, dw1, dw2

    ln_ffn_dg, ln_ffn_db, ln_mix_dg, ln_mix_db, dw1s, dw2s = [None, None], [None, None], [None, None], [None, None], [None, None], [None, None]
    dpre, ln_ffn_dg[1], ln_ffn_db[1] = _ln_bwd(h1a, y1, wts["ln_ffn_g"][1], dy, name="ln_ffn1_bwd")
    dh1a, dw1s[1], dw2s[1] = mlp_bwd(1, h1a, u1, dpre)
    dpre, ln_mix_dg[1], ln_mix_db[1] = _ln_bwd(h0b, mix1, wts["ln_mix_g"][1], dh1a, name="ln_mix1_bwd")
    g["c_w_out"] = mm(oc, dpre, "TN", tm=1024, tn=1024, tk=tkl, name="l1_dw_out")
    doc = mm(dpre, wts["w_out1"], "NT", tm=tm, tn=512, tk=d, name="l1_d_gate")
    doc_raw, dz1, g["c_gn"] = _gate_bwd(oc_raw, p1, 3 * HG_HEADS, wts["c_gn"], doc, heads=HG_HEADS, name="hg_gate_bwd")
    dq1, df1, di1, g["lb"] = _hg_bwd(p1, wts["lb"], hg_states, doc_raw, pad, name="hg_bwd")
    dp1 = jnp.concatenate([dq1, df1, di1, dz1], axis=1)
    g["c_w_in"] = mm(h0b, dp1, "TN", tm=1024, tn=512, tk=tkl, out_dev=True, name="l1_dw_in")
    dh0b = mm(dp1, wts["w_c"], "NT", tm=tm, tn=1024, tk=512, b_dev=True, epi="add", c=dpre, scale=DN_ALPHA,
              name="l1_d_in")
    dpre, ln_ffn_dg[0], ln_ffn_db[0] = _ln_bwd(h0a, y0, wts["ln_ffn_g"][0], dh0b, name="ln_ffn0_bwd")
    dh0a, dw1s[0], dw2s[0] = mlp_bwd(0, h0a, u0, dpre)
    dpre, ln_mix_dg[0], ln_mix_db[0] = _ln_bwd(h0, mix0, wts["ln_mix_g"][0], dh0a, name="ln_mix0_bwd")
    g["ab_w_out"] = mm(oab, dpre, "TN", tm=1024, tn=1024, tk=tkl, name="l0_dw_out")
    doab = mm(dpre, wts["w_out0"], "NT", tm=tm, tn=512, tk=d, name="l0_d_gate")
    doa_raw, dz0, g["ab_gn"] = _gate_bwd(oa_raw, p0, AB_Z // HEAD_W, wts["ab_gn"], doab, heads=GDN_HEADS,
                                         name="gdn_gate_bwd")
    dqb, dkb_t, dvb_t = _sb_bwd(p0, sb_tot, doab, GDN_HEADS, pad, name="sb_bwd")
    dkb, dvb = dkb_t.T, dvb_t.T
    dqn, dkn, dvn, dba, g["alog_v"], g["dtb_v"] = _gdn_bwd(qkv, p0, wts["alog_v"], wts["dtb_v"], gdn_states, doa_raw,
                                                           pad, name="gdn_bwd")
    dconv_in, g["conv_w"] = _gdn_pre_bwd(p0, wts["conv_w"], jnp.concatenate([dqn, dkn, dvn], axis=1), pad,
                                         name="gdn_pre_bwd")
    dp0 = jnp.concatenate([dconv_in, dz0, dqb, dkb, dvb, dba, jnp.zeros((lp, AB_CAT - AB_BA - HEAD_W), F32)], axis=1)
    g["w_ab"] = mm(h0, dp0, "TN", tm=1024, tn=768, tk=tkl, name="l0_dw_in")
    dh0 = mm(dp0, wts["w_ab"], "NT", tm=tm, tn=1024, tk=768, epi="add", c=dpre, scale=DN_ALPHA, name="l0_d_in")

    g["w1"], g["w2"] = dw1s, dw2s
    g["ln_mix_g"] = jnp.concatenate(ln_mix_dg, axis=0)
    g["ln_mix_b"] = jnp.concatenate(ln_mix_db, axis=0)
    g["ln_ffn_g"] = jnp.concatenate(ln_ffn_dg, axis=0)
    g["ln_ffn_b"] = jnp.concatenate(ln_ffn_db, axis=0)
    return loss_vec, dh0, g


N_CHIP = N_DEV // 2


def _place():
    x, y, c = lax.axis_index("x"), lax.axis_index("y"), lax.axis_index("c")
    return x, y, c, 2 * x + y


def _chip_dev(chip, core):
    return (chip // 2, chip % 2, core)


def _remote(src, dst, send_sem, recv_sem, dev):
    return pltpu.make_async_remote_copy(src_ref=src, dst_ref=dst, send_sem=send_sem, recv_sem=recv_sem,
                                        device_id=dev, device_id_type=pl.DeviceIdType.MESH)


_ANY = pl.BlockSpec(memory_space=pl.ANY)


def _gather(srcs, dtypes, *, name):
    n = len(srcs)
    blocks = [s.shape for s in srcs]

    def body(*refs):
        ins, outs, stages = refs[:n], refs[n:2 * n], refs[2 * n:3 * n]
        send_sems, recv_sems, local_sems = refs[3 * n:]
        x, y, c, chip = _place()
        me = 2 * chip + c
        sibling = (x, y, 1 - c)
        local, pending = [], []
        for i in range(n):
            stages[i][...] = ins[i][...].astype(dtypes[i])
            loc = pltpu.make_async_copy(stages[i], outs[i].at[me], local_sems.at[i])
            loc.start()
            local.append(loc)
            first = [_remote(stages[i], outs[i].at[me], send_sems.at[i, 0], recv_sems.at[i, 0], sibling)]
            for j in range(1, N_CHIP):
                first.append(_remote(stages[i], outs[i].at[me], send_sems.at[i, j], recv_sems.at[i, j],
                                     _chip_dev(jnp.bitwise_xor(chip, j), c)))
            for cp in first:
                cp.start()
            pending += first
        for i in range(n):
            for j in range(1, N_CHIP):
                slot = outs[i].at[2 * jnp.bitwise_xor(chip, j) + c]
                _remote(slot, slot, send_sems.at[i, j], recv_sems.at[i, j], sibling).wait_recv()
                fwd = _remote(slot, slot, send_sems.at[i, N_CHIP - 1 + j], recv_sems.at[i, N_CHIP - 1 + j], sibling)
                fwd.start()
                pending.append(fwd)
        for i in range(n):
            blk = outs[i].at[me]
            for k in (0, *range(N_CHIP, 2 * N_CHIP - 1)):
                _remote(blk, blk, send_sems.at[i, k], recv_sems.at[i, k], sibling).wait_recv()
        for cp in pending:
            cp.wait_send()
        for cp in local:
            cp.wait()

    scratch = [pltpu.VMEM(b, dt) for b, dt in zip(blocks, dtypes)]
    scratch += [pltpu.SemaphoreType.DMA((n, 2 * N_CHIP - 1)), pltpu.SemaphoreType.DMA((n, 2 * N_CHIP - 1)),
                pltpu.SemaphoreType.DMA((n,))]
    return pl.pallas_call(
        body, name=name, in_specs=[pl.BlockSpec(memory_space=pltpu.VMEM)] * n, out_specs=[_ANY] * n,
        out_shape=[jax.ShapeDtypeStruct((N_DEV, *b), dt) for b, dt in zip(blocks, dtypes)],
        scratch_shapes=scratch, compiler_params=_cp(has_side_effects=True),
    )(*srcs)


def _to_sibling(parts, *, name):
    n = len(parts)

    def body(*refs):
        ins, outs = refs[:n], refs[n:2 * n]
        send_sems, recv_sems = refs[2 * n:]
        x, y, c, _ = _place()
        copies = [_remote(ins[i].at[2 * k + (1 - c)], outs[i].at[k], send_sems.at[i, k], recv_sems.at[i, k],
                          (x, y, 1 - c)) for i in range(n) for k in range(N_CHIP)]
        for cp in copies:
            cp.start()
        for cp in copies:
            cp.wait()

    return pl.pallas_call(
        body, name=name, in_specs=[_ANY] * n, out_specs=[_ANY] * n,
        out_shape=[jax.ShapeDtypeStruct((N_CHIP, *p.shape[1:]), p.dtype) for p in parts],
        scratch_shapes=[pltpu.SemaphoreType.DMA((n, N_CHIP)), pltpu.SemaphoreType.DMA((n, N_CHIP))],
        compiler_params=_cp(has_side_effects=True),
    )(*parts)


def _pair_sum(part, from_sibling, core, *, name):
    _, r, c = part.shape
    tm = _row_tile(r, 128)

    def body(core_ref, a_ref, b_ref, o_ref):
        o_ref[...] = (a_ref[...] + b_ref[...]).astype(BF16)

    return pl.pallas_call(
        body, name=name,
        grid_spec=pltpu.PrefetchScalarGridSpec(
            num_scalar_prefetch=1, grid=(N_CHIP, r // tm),
            in_specs=[pl.BlockSpec((None, tm, c), lambda k, i, core_ref: (2 * k + core_ref[0], i, 0)),
                      pl.BlockSpec((None, tm, c), lambda k, i, core_ref: (k, i, 0))],
            out_specs=pl.BlockSpec((None, tm, c), lambda k, i, core_ref: (k, i, 0))),
        out_shape=jax.ShapeDtypeStruct((N_CHIP, r, c), BF16), compiler_params=_cp(("parallel", "parallel")),
    )(core, part, from_sibling)


def _to_chips(sums, *, name):
    n = len(sums)

    def body(*refs):
        ins, outs = refs[:n], refs[n:2 * n]
        send_sems, recv_sems, local_sems = refs[2 * n:]
        _, _, c, chip = _place()
        copies = []
        for i in range(n):
            copies.append(pltpu.make_async_copy(ins[i].at[chip], outs[i].at[chip], local_sems.at[i]))
            for j in range(1, N_CHIP):
                other = jnp.bitwise_xor(chip, j)
                copies.append(_remote(ins[i].at[other], outs[i].at[chip], send_sems.at[i, j - 1], recv_sems.at[i, j - 1],
                                      _chip_dev(other, c)))
        for cp in copies:
            cp.start()
        for cp in copies:
            cp.wait()

    return pl.pallas_call(
        body, name=name, in_specs=[_ANY] * n, out_specs=[_ANY] * n,
        out_shape=[jax.ShapeDtypeStruct(s.shape, s.dtype) for s in sums],
        scratch_shapes=[pltpu.SemaphoreType.DMA((n, N_CHIP - 1)), pltpu.SemaphoreType.DMA((n, N_CHIP - 1)),
                        pltpu.SemaphoreType.DMA((n,))],
        compiler_params=_cp(has_side_effects=True),
    )(*sums)


def _adamw(w, parts, m, v, *, name):
    r, c = w.shape
    s = parts.shape[0]
    tm = _row_tile(r, 128) if r % 8 == 0 else r
    c1 = 1.0 - ADAM_B1 ** ADAM_STEP
    c2 = 1.0 - ADAM_B2 ** ADAM_STEP

    def body(w_ref, p_ref, m_ref, v_ref, g_ref, d_ref, m2_ref, v2_ref):
        g = p_ref[0].astype(F32)
        for j in range(1, s):
            g = g + p_ref[j].astype(F32)
        m2 = ADAM_B1 * m_ref[...] + (1.0 - ADAM_B1) * g
        v2 = ADAM_B2 * v_ref[...] + (1.0 - ADAM_B2) * jnp.square(g)
        g_ref[...] = g
        m2_ref[...] = m2
        v2_ref[...] = v2
        d_ref[...] = -ADAM_LR * ((m2 / c1) / (jnp.sqrt(v2 / c2) + ADAM_EPS) + ADAM_WD * w_ref[...])

    blk = pl.BlockSpec((tm, c), lambda i: (i, 0))
    return pl.pallas_call(
        body, name=name, grid=(r // tm,),
        in_specs=[blk, pl.BlockSpec((s, tm, c), lambda i: (0, i, 0)), blk, blk], out_specs=[blk] * 4,
        out_shape=[jax.ShapeDtypeStruct((r, c), F32)] * 4, compiler_params=_cp(("parallel",)),
    )(w, parts, m, v)


_WEIGHTS = ("meta_tokens", "ab_w_in", "ab_conv_w", "ab_a_log", "ab_dt_bias", "ab_gnorm_g", "ab_w_out", "c_w_in",
            "c_lb_raw", "c_gnorm_g", "c_w_out", "ln_mix_g", "ln_mix_b", "mlp_w1", "mlp_w2", "ln_ffn_g", "ln_ffn_b")
_PACK_ROWS = (("ln_mix_g", 0), ("ln_mix_b", 2), ("ln_ffn_g", 4), ("ln_ffn_b", 6), ("c_lb_raw", 8))
_PACK_MISC_ROW = 10
_PACK_MISC = (("ab_gnorm_g", 0, 128), ("c_gnorm_g", 128, 128), ("ab_a_log", 256, GDN_HEADS), ("ab_dt_bias", 260, GDN_HEADS))
_PACK_N = 16
_SMALL_META = 16
_SMALL_CONV = 32
_SMALL_N = 40


def _pack_replicated(p):
    rows = jnp.zeros((_PACK_N, D_MODEL), F32)
    for name, r0 in _PACK_ROWS:
        rows = rows.at[r0:r0 + 2].set(p[name])
    for name, c0, width in _PACK_MISC:
        rows = rows.at[_PACK_MISC_ROW, c0:c0 + width].set(p[name].reshape(width))
    return rows


def _unpack_replicated(rows, like):
    out = {}
    for name, r0 in _PACK_ROWS:
        out[name] = rows[r0:r0 + 2]
    for name, c0, width in _PACK_MISC:
        out[name] = rows[_PACK_MISC_ROW, c0:c0 + width].reshape(like[name].shape)
    return out


def _lower_bound(c_lb_raw):
    lb_all = jnp.cumsum(jax.nn.softmax(c_lb_raw.astype(F32), axis=0), axis=0)
    return (lb_all - lb_all[0:1])[1].reshape(1, -1)


def kernel(x, meta_tokens, ab_w_in, ab_conv_w, ab_a_log, ab_dt_bias, ab_gnorm_g, ab_w_out, c_w_in, c_lb_raw, c_gnorm_g, c_w_out, ln_mix_g, ln_mix_b, mlp_w1, mlp_w2, ln_ffn_g, ln_ffn_b, loss_target, m_meta_tokens, m_ab_w_in, m_ab_conv_w, m_ab_a_log, m_ab_dt_bias, m_ab_gnorm_g, m_ab_w_out, m_c_w_in, m_c_lb_raw, m_c_gnorm_g, m_c_w_out, m_ln_mix_g, m_ln_mix_b, m_mlp_w1, m_mlp_w2, m_ln_ffn_g, m_ln_ffn_b, v_meta_tokens, v_ab_w_in, v_ab_conv_w, v_ab_a_log, v_ab_dt_bias, v_ab_gnorm_g, v_ab_w_out, v_c_w_in, v_c_lb_raw, v_c_gnorm_g, v_c_w_out, v_ln_mix_g, v_ln_mix_b, v_mlp_w1, v_mlp_w2, v_ln_ffn_g, v_ln_ffn_b):
    w = dict(zip(_WEIGHTS, (meta_tokens, ab_w_in, ab_conv_w, ab_a_log, ab_dt_bias, ab_gnorm_g, ab_w_out, c_w_in, c_lb_raw,
                            c_gnorm_g, c_w_out, ln_mix_g, ln_mix_b, mlp_w1, mlp_w2, ln_ffn_g, ln_ffn_b)))
    mom = dict(zip(_WEIGHTS, (m_meta_tokens, m_ab_w_in, m_ab_conv_w, m_ab_a_log, m_ab_dt_bias, m_ab_gnorm_g, m_ab_w_out,
                              m_c_w_in, m_c_lb_raw, m_c_gnorm_g, m_c_w_out, m_ln_mix_g, m_ln_mix_b, m_mlp_w1, m_mlp_w2,
                              m_ln_ffn_g, m_ln_ffn_b)))
    var = dict(zip(_WEIGHTS, (v_meta_tokens, v_ab_w_in, v_ab_conv_w, v_ab_a_log, v_ab_dt_bias, v_ab_gnorm_g, v_ab_w_out,
                              v_c_w_in, v_c_lb_raw, v_c_gnorm_g, v_c_w_out, v_ln_mix_g, v_ln_mix_b, v_mlp_w1, v_mlp_w2,
                              v_ln_ffn_g, v_ln_ffn_b)))
    me = 4 * lax.axis_index("x") + 2 * lax.axis_index("y") + lax.axis_index("c")
    seq = x.shape[1]
    pad = (-(N_META + seq)) % SB_BLOCK
    lp = pad + N_META + seq
    meta_w = D_MODEL // N_DEV
    conv_w_all = 2 * GDN_HEADS * HEAD_W + GDN_HEADS * HEAD_W
    conv_w_mine = conv_w_all // N_DEV

    gathered = _gather(
        [w["meta_tokens"], w["ab_conv_w"][0], w["ab_w_in"][0], w["ab_w_out"][0], w["c_w_in"][0], w["c_w_out"][0],
         w["mlp_w1"], w["mlp_w2"]],
        [F32, F32, BF16, BF16, BF16, BF16, BF16, BF16], name="gather_weights")
    g_meta, g_conv, g_ab_in, g_ab_out, g_c_in, g_c_out, g_w1, g_w2 = gathered
    meta_full = g_meta.transpose(1, 0, 2).reshape(N_META, D_MODEL)
    conv_full = g_conv.transpose(1, 0, 2).reshape(CONV_K, conv_w_all)
    ab_full = g_ab_in.transpose(1, 0, 2).reshape(D_MODEL, AB_IN)
    ba0 = AB_Z + 512
    w_ab = jnp.concatenate([ab_full[:, :ba0], ab_full[:, ba0 + 2 * GDN_HEADS:], ab_full[:, ba0:ba0 + 2 * GDN_HEADS],
                            jnp.zeros((D_MODEL, AB_CAT - AB_IN), BF16)], axis=1)
    vec128 = lambda p: jnp.zeros((1, HEAD_W), F32).at[0, :GDN_HEADS].set(p.reshape(GDN_HEADS))
    wts = dict(
        w_ab=w_ab, conv_w=conv_full, alog_v=vec128(w["ab_a_log"]), dtb_v=vec128(w["ab_dt_bias"]),
        ab_gn=w["ab_gnorm_g"][0], w_out0=g_ab_out.reshape(D_MODEL, D_MODEL), w_c=g_c_in,
        lb=_lower_bound(w["c_lb_raw"]), c_gn=w["c_gnorm_g"][0], w_out1=g_c_out.reshape(D_MODEL, D_MODEL),
        w1=[g_w1[:, l] for l in range(DEPTH)], w2=[g_w2[:, l].reshape(D_FF, D_MODEL) for l in range(DEPTH)],
        ln_mix_g=w["ln_mix_g"], ln_mix_b=w["ln_mix_b"], ln_ffn_g=w["ln_ffn_g"], ln_ffn_b=w["ln_ffn_b"])

    h0 = jnp.concatenate([jnp.zeros((pad, D_MODEL), F32), meta_full, x[0]], axis=0)
    loss_vec, dh0, g = _local_step(h0, loss_target[0], pad, wts)
    loss = lax.psum(jnp.sum(loss_vec), ("x", "y", "c"))
    grad_x = dh0[lp - seq:][None]

    _, lb_vjp = jax.vjp(_lower_bound, w["c_lb_raw"])
    rep_part = _pack_replicated(dict(
        ln_mix_g=g["ln_mix_g"], ln_mix_b=g["ln_mix_b"], ln_ffn_g=g["ln_ffn_g"], ln_ffn_b=g["ln_ffn_b"],
        c_lb_raw=lb_vjp(g["lb"])[0], ab_gnorm_g=g["ab_gn"], c_gnorm_g=g["c_gn"],
        ab_a_log=g["alog_v"][0, :GDN_HEADS], ab_dt_bias=g["dtb_v"][0, :GDN_HEADS]))
    small = jnp.concatenate([rep_part, dh0[pad:pad + N_META], g["conv_w"].reshape(-1, D_MODEL),
                             jnp.zeros((_SMALL_N - _SMALL_CONV - CONV_K * conv_w_all // D_MODEL, D_MODEL), F32)], axis=0)
    (small_all,) = _gather([small], [F32], name="gather_small_grads")
    rep_out = _adamw(_pack_replicated(w), small_all[:, :_PACK_N], _pack_replicated(mom), _pack_replicated(var),
                     name="adamw_replicated")
    meta_parts = lax.dynamic_slice_in_dim(small_all[:, _SMALL_META:_SMALL_META + N_META], me * meta_w, meta_w, axis=2)
    meta_out = _adamw(w["meta_tokens"], meta_parts, mom["meta_tokens"], var["meta_tokens"], name="adamw_meta")
    conv_parts = small_all[:, _SMALL_CONV:_SMALL_CONV + CONV_K * conv_w_all // D_MODEL].reshape(N_DEV, CONV_K, conv_w_all)
    conv_parts = lax.dynamic_slice_in_dim(conv_parts, me * conv_w_mine, conv_w_mine, axis=2)
    conv_out = _adamw(w["ab_conv_w"][0], conv_parts, mom["ab_conv_w"][0], var["ab_conv_w"][0], name="adamw_conv")

    gab = g["w_ab"]
    gab = jnp.concatenate([gab[:, :ba0], gab[:, AB_BA:AB_BA + 2 * GDN_HEADS], gab[:, ba0:AB_BA]], axis=1)
    big = [("ab_w_in", None, gab.reshape(D_MODEL, N_DEV, AB_IN // N_DEV).transpose(1, 0, 2)),
           ("ab_w_out", None, g["ab_w_out"].reshape(N_DEV, D_MODEL // N_DEV, D_MODEL)),
           ("c_w_in", None, g["c_w_in"]),
           ("c_w_out", None, g["c_w_out"].reshape(N_DEV, D_MODEL // N_DEV, D_MODEL))]
    for l in range(DEPTH):
        big.append(("mlp_w1", l, g["w1"][l]))
    for l in range(DEPTH):
        big.append(("mlp_w2", l, g["w2"][l].reshape(N_DEV, D_FF // N_DEV, D_MODEL)))
    from_sibling = _to_sibling([b[2] for b in big], name="scatter_grads_d2d")
    core = lax.axis_index("c").astype(jnp.int32).reshape(1)
    chip_sums = [_pair_sum(b[2], s, core, name=f"chip_sum_{b[0]}" + ("" if b[1] is None else str(b[1])))
                 for b, s in zip(big, from_sibling)]
    parts = _to_chips(chip_sums, name="scatter_grads_ici")
    big_out = {}
    for (name, l, _), p in zip(big, parts):
        sel = (lambda a: a[0]) if l is None else (lambda a, l=l: a[l])
        res = _adamw(sel(w[name]), p, sel(mom[name]), sel(var[name]), name=f"adamw_{name}" + ("" if l is None else str(l)))
        big_out.setdefault(name, []).append(res)

    rep = [_unpack_replicated(r, w) for r in rep_out]
    outs = {}
    for name in _WEIGHTS:
        if name == "meta_tokens":
            outs[name] = list(meta_out)
        elif name == "ab_conv_w":
            outs[name] = [o[None] for o in conv_out]
        elif name in big_out:
            res = big_out[name]
            outs[name] = [o[None] for o in res[0]] if len(res) == 1 else [jnp.stack(pair) for pair in zip(*res)]
        else:
            outs[name] = [r[name] for r in rep]
    flat = [loss, grad_x]
    for kind in range(4):
        flat += [outs[name][kind] for name in _WEIGHTS]
    return tuple(flat)
```

```python
import functools
import math

import jax
import jax.numpy as jnp
from jax import lax
from jax.experimental import pallas as pl
from jax.experimental.pallas import tpu as pltpu

F32 = jnp.float32
BF16 = jnp.bfloat16
HI = lax.Precision.HIGHEST

N_DEV = 8
D_MODEL = 1024
N_META = 16
D_FF = 4096
DEPTH = 2
GDN_HEADS = 4
SB_HEADS = 8
SB_DH = 64
HG_HEADS = 8
HEAD_W = 128
CHUNK = 64
SB_BLOCK = 128
CONV_K = 4
DN_ALPHA = float((2 * DEPTH) ** 0.25)
LN_EPS = 1e-5
RMS_EPS = 1e-6
L2_EPS = 1e-6
ADAM_LR, ADAM_B1, ADAM_B2, ADAM_EPS, ADAM_WD, ADAM_STEP = 0.001, 0.9, 0.999, 1e-08, 0.01, 10

AB_QKV = 0
AB_Z = 1536
AB_SB = 2048
AB_BA = 3584
AB_CAT = 3840
AB_IN = 3592

VMEM_LIMIT = 56 * 1024 * 1024


def _cp(sem=None, **kw):
    if sem is not None:
        kw["dimension_semantics"] = sem
    return pltpu.CompilerParams(vmem_limit_bytes=VMEM_LIMIT, **kw)


def _row_tile(n, want):
    best = 8
    for t in range(8, min(n, want) + 1, 8):
        if n % t == 0:
            best = t
    return best


@jax.custom_vjp
def _sigmoid(x):
    e = jnp.exp(-jnp.abs(x))
    r = 1.0 / (1.0 + e)
    return jnp.where(x >= 0, r, e * r)


def _sigmoid_fwd(x):
    s = _sigmoid(x)
    return s, s


def _sigmoid_bwd(s, g):
    return (g * s * (1.0 - s),)


_sigmoid.defvjp(_sigmoid_fwd, _sigmoid_bwd)


def _log1p_exp_neg_abs(x):
    e = jnp.exp(-jnp.abs(x))
    return jnp.where(e < 1e-4, e - 0.5 * e * e, jnp.log(1.0 + e))


@jax.custom_vjp
def _softplus(x):
    return jnp.maximum(x, 0.0) + _log1p_exp_neg_abs(x)


def _softplus_fwd(x):
    return _softplus(x), x


def _softplus_bwd(x, g):
    return (g * _sigmoid(x),)


_softplus.defvjp(_softplus_fwd, _softplus_bwd)


def _silu(x):
    return x * _sigmoid(x)


def _silu_grad(x):
    s = _sigmoid(x)
    return s * (1.0 + x * (1.0 - s))


def _dot(a, b, dims, precision=None):
    return lax.dot_general(a, b, (dims, ((), ())), precision=precision, preferred_element_type=F32)


NN = ((1,), (0,))
NT = ((1,), (1,))
TN = ((0,), (0,))


def _bdot(a, b, dims):
    return _dot(a.astype(BF16), b.astype(BF16), dims)


def _mm(a, b, mode, *, tm, tn, tk, name, a_fn=None, epi=None, c=None, scale=1.0, b_dev=False, out_dev=False,
        out_dtype=F32):
    if mode == "NN":
        m, kk = a.shape
        n = b.shape[2] * N_DEV if b_dev else b.shape[1]
    elif mode == "NT":
        m, kk = a.shape
        n = b.shape[1] if b_dev else b.shape[0]
    else:
        kk, m = a.shape
        n = b.shape[1]
    assert m % tm == 0 and n % tn == 0 and kk % tk == 0, (name, m, n, kk, tm, tn, tk)
    nk = kk // tk
    dims = {"NN": NN, "NT": NT, "TN": TN}[mode]

    if mode == "TN":
        a_spec = pl.BlockSpec((tk, tm), lambda i, j, k: (k, i))
    else:
        a_spec = pl.BlockSpec((tm, tk), lambda i, j, k: (i, k))
    if mode == "NN":
        if b_dev:
            assert tn == b.shape[2]
            b_spec = pl.BlockSpec((None, tk, tn), lambda i, j, k: (j, k, 0))
        else:
            b_spec = pl.BlockSpec((tk, tn), lambda i, j, k: (k, j))
    elif mode == "NT":
        if b_dev:
            assert tk == b.shape[2]
            b_spec = pl.BlockSpec((None, tn, tk), lambda i, j, k: (k, j, 0))
        else:
            b_spec = pl.BlockSpec((tn, tk), lambda i, j, k: (j, k))
    else:
        b_spec = pl.BlockSpec((tk, tn), lambda i, j, k: (k, j))
    in_specs = [a_spec, b_spec]
    operands = [a, b]
    if epi is not None:
        in_specs.append(pl.BlockSpec((tm, tn), lambda i, j, k: (i, j)))
        operands.append(c)
    if out_dev:
        assert tn == n // N_DEV
        out_shape = jax.ShapeDtypeStruct((N_DEV, m, tn), out_dtype)
        out_spec = pl.BlockSpec((None, tm, tn), lambda i, j, k: (j, i, 0))
    else:
        out_shape = jax.ShapeDtypeStruct((m, n), out_dtype)
        out_spec = pl.BlockSpec((tm, tn), lambda i, j, k: (i, j))

    def body(*refs):
        a_ref, b_ref = refs[0], refs[1]
        c_ref = refs[2] if epi is not None else None
        o_ref = refs[3] if epi is not None else refs[2]
        acc_ref = refs[-1] if nk > 1 else None
        av = a_ref[...]
        if a_fn == "relu2":
            av = jnp.square(jnp.maximum(av, 0.0))
        p = _dot(av.astype(BF16), b_ref[...].astype(BF16), dims)

        def finish(acc):
            if epi == "add":
                acc = acc + scale * c_ref[...]
            elif epi == "relu2grad":
                acc = acc * (2.0 * jnp.maximum(c_ref[...], 0.0))
            o_ref[...] = acc.astype(out_dtype)

        if nk == 1:
            finish(p)
        else:
            k = pl.program_id(2)

            @pl.when(k == 0)
            def _():
                acc_ref[...] = p

            @pl.when(k > 0)
            def _():
                acc_ref[...] += p

            @pl.when(k == nk - 1)
            def _():
                finish(acc_ref[...])

    return pl.pallas_call(
        body, name=name, grid=(m // tm, n // tn, nk), in_specs=in_specs, out_specs=out_spec, out_shape=out_shape,
        scratch_shapes=[pltpu.VMEM((tm, tn), F32)] if nk > 1 else [],
        compiler_params=_cp(("parallel", "parallel", "arbitrary")),
    )(*operands)


def _ln_fwd(a, b, g, beta, *, name):
    lp, d = a.shape
    tm = _row_tile(lp, 512)

    def body(a_ref, b_ref, g_ref, be_ref, y_ref, yb_ref):
        pre = DN_ALPHA * a_ref[...] + b_ref[...]
        mu = jnp.mean(pre, axis=-1, keepdims=True)
        xc = pre - mu
        var = jnp.mean(xc * xc, axis=-1, keepdims=True)
        y = xc * lax.rsqrt(var + LN_EPS) * g_ref[...] + be_ref[...]
        y_ref[...] = y
        yb_ref[...] = y.astype(BF16)

    row = pl.BlockSpec((tm, d), lambda i: (i, 0))
    vec = pl.BlockSpec((1, d), lambda i: (0, 0))
    return pl.pallas_call(
        body, name=name, grid=(lp // tm,), in_specs=[row, row, vec, vec], out_specs=[row, row],
        out_shape=[jax.ShapeDtypeStruct((lp, d), F32), jax.ShapeDtypeStruct((lp, d), BF16)],
        compiler_params=_cp(("parallel",)),
    )(a, b, g.reshape(1, d), beta.reshape(1, d))


def _ln_bwd(a, b, g, dy, *, name):
    lp, d = a.shape
    tm = _row_tile(lp, 512)

    def body(a_ref, b_ref, g_ref, dy_ref, dpre_ref, dpreb_ref, dg_ref, db_ref):
        pre = DN_ALPHA * a_ref[...] + b_ref[...]
        mu = jnp.mean(pre, axis=-1, keepdims=True)
        xc = pre - mu
        var = jnp.mean(xc * xc, axis=-1, keepdims=True)
        rstd = lax.rsqrt(var + LN_EPS)
        xhat = xc * rstd
        dyv = dy_ref[...]
        dxh = dyv * g_ref[...]
        m1 = jnp.mean(dxh, axis=-1, keepdims=True)
        m2 = jnp.mean(dxh * xhat, axis=-1, keepdims=True)
        dpre = rstd * (dxh - m1 - xhat * m2)
        dpre_ref[...] = dpre
        dpreb_ref[...] = dpre.astype(BF16)

        @pl.when(pl.program_id(0) == 0)
        def _():
            dg_ref[...] = jnp.zeros_like(dg_ref)
            db_ref[...] = jnp.zeros_like(db_ref)

        dg_ref[...] += jnp.sum(dyv * xhat, axis=0, keepdims=True)
        db_ref[...] += jnp.sum(dyv, axis=0, keepdims=True)

    row = pl.BlockSpec((tm, d), lambda i: (i, 0))
    vec = pl.BlockSpec((1, d), lambda i: (0, 0))
    return pl.pallas_call(
        body, name=name, grid=(lp // tm,), in_specs=[row, row, vec, row], out_specs=[row, row, vec, vec],
        out_shape=[jax.ShapeDtypeStruct((lp, d), F32), jax.ShapeDtypeStruct((lp, d), BF16),
                   jax.ShapeDtypeStruct((1, d), F32), jax.ShapeDtypeStruct((1, d), F32)],
        compiler_params=_cp(("arbitrary",)),
    )(a, b, g.reshape(1, d), dy)


def _loss_head(y, target, *, name):
    lp, d = y.shape
    seq = target.shape[0]
    tm = SB_BLOCK
    first = (lp - seq) // tm
    assert (lp - seq) % tm == 0 and seq % tm == 0

    def body(y_ref, t_ref, dy_ref, loss_ref):
        i = pl.program_id(0)
        live = i >= first
        diff = jnp.where(live, y_ref[...] - t_ref[...], 0.0)
        dy_ref[...] = diff * (1.0 / d)

        @pl.when(i == 0)
        def _():
            loss_ref[...] = jnp.zeros_like(loss_ref)

        loss_ref[...] += jnp.sum(diff * diff, axis=0, keepdims=True) * (0.5 / d)

    return pl.pallas_call(
        body, name=name, grid=(lp // tm,),
        in_specs=[pl.BlockSpec((tm, d), lambda i: (i, 0)),
                  pl.BlockSpec((tm, d), lambda i: (jnp.maximum(i - first, 0), 0))],
        out_specs=[pl.BlockSpec((tm, d), lambda i: (i, 0)), pl.BlockSpec((1, d), lambda i: (0, 0))],
        out_shape=[jax.ShapeDtypeStruct((lp, d), F32), jax.ShapeDtypeStruct((1, d), F32)],
        compiler_params=_cp(("arbitrary",)),
    )(y, target)


def _gate_fwd(o, zsrc, z_blk0, g, other, *, heads, name):
    lp = o.shape[0]
    tm = _row_tile(lp, 512)
    nblk = D_MODEL // HEAD_W

    def body(o_ref, z_ref, g_ref, x_ref, y_ref):
        h = pl.program_id(1)

        @pl.when(h < heads)
        def _():
            ov = o_ref[...]
            r = lax.rsqrt(jnp.mean(ov * ov, axis=-1, keepdims=True) + RMS_EPS)
            y_ref[...] = (ov * r * g_ref[...] * _silu(z_ref[...])).astype(BF16)

        @pl.when(h >= heads)
        def _():
            y_ref[...] = x_ref[...].astype(BF16)

    other_w = other.shape[1] // HEAD_W
    return pl.pallas_call(
        body, name=name, grid=(lp // tm, nblk),
        in_specs=[pl.BlockSpec((tm, HEAD_W), lambda i, h: (i, jnp.minimum(h, heads - 1))),
                  pl.BlockSpec((tm, HEAD_W), lambda i, h: (i, z_blk0 + jnp.minimum(h, heads - 1))),
                  pl.BlockSpec((1, HEAD_W), lambda i, h: (0, 0)),
                  pl.BlockSpec((tm, HEAD_W), lambda i, h: (i, jnp.clip(h - heads, 0, other_w - 1)))],
        out_specs=pl.BlockSpec((tm, HEAD_W), lambda i, h: (i, h)),
        out_shape=jax.ShapeDtypeStruct((lp, D_MODEL), BF16),
        compiler_params=_cp(("parallel", "arbitrary")),
    )(o, zsrc, g.reshape(1, HEAD_W), other)


def _gate_bwd(o, zsrc, z_blk0, g, dy, *, heads, name):
    lp = o.shape[0]
    tm = _row_tile(lp, 512)

    def body(o_ref, z_ref, g_ref, dy_ref, do_ref, dz_ref, dg_ref):
        ov, zv, gv, dyv = o_ref[...], z_ref[...], g_ref[...], dy_ref[...]
        r = lax.rsqrt(jnp.mean(ov * ov, axis=-1, keepdims=True) + RMS_EPS)
        nrm = ov * r
        s = _silu(zv)
        dn = dyv * gv * s
        do_ref[...] = r * (dn - nrm * jnp.mean(dn * nrm, axis=-1, keepdims=True))
        dz_ref[...] = dyv * nrm * gv * _silu_grad(zv)

        @pl.when((pl.program_id(0) == 0) & (pl.program_id(1) == 0))
        def _():
            dg_ref[...] = jnp.zeros_like(dg_ref)

        dg_ref[...] += jnp.sum(dyv * nrm * s, axis=0, keepdims=True)

    blk = pl.BlockSpec((tm, HEAD_W), lambda i, h: (i, h))
    return pl.pallas_call(
        body, name=name, grid=(lp // tm, heads),
        in_specs=[blk, pl.BlockSpec((tm, HEAD_W), lambda i, h: (i, z_blk0 + h)),
                  pl.BlockSpec((1, HEAD_W), lambda i, h: (0, 0)), blk],
        out_specs=[blk, blk, pl.BlockSpec((1, HEAD_W), lambda i, h: (0, 0))],
        out_shape=[jax.ShapeDtypeStruct((lp, heads * HEAD_W), F32), jax.ShapeDtypeStruct((lp, heads * HEAD_W), F32),
                   jax.ShapeDtypeStruct((1, HEAD_W), F32)],
        compiler_params=_cp(("arbitrary", "arbitrary")),
    )(o, zsrc, g.reshape(1, HEAD_W), dy)


def _conv_taps(x, w):
    acc = w[CONV_K - 1:CONV_K, :] * x
    for k in range(CONV_K - 1):
        acc = acc + w[k:k + 1, :] * pltpu.roll(x, CONV_K - 1 - k, 0)
    return acc


def _gdn_pre_fwd(p0, conv_w, pad, *, name):
    lp = p0.shape[0]
    nq = GDN_HEADS
    qscale = HEAD_W ** -0.5

    def body(x_ref, w_ref, y_ref):
        j = pl.program_id(0)
        c = _conv_taps(x_ref[...], w_ref[...])
        s = _silu(c)
        r = lax.rsqrt(jnp.sum(s * s, axis=-1, keepdims=True) + L2_EPS)
        mult = jnp.where(j < nq, r * qscale, jnp.where(j < 2 * nq, r, 1.0))
        rows = lax.broadcasted_iota(jnp.int32, (lp, 1), 0)
        y_ref[...] = jnp.where(rows >= pad, s * mult, 0.0)

    return pl.pallas_call(
        body, name=name, grid=(3 * nq,),
        in_specs=[pl.BlockSpec((lp, HEAD_W), lambda j: (0, j)), pl.BlockSpec((CONV_K, HEAD_W), lambda j: (0, j))],
        out_specs=pl.BlockSpec((lp, HEAD_W), lambda j: (0, j)),
        out_shape=jax.ShapeDtypeStruct((lp, 3 * nq * HEAD_W), F32), compiler_params=_cp(("parallel",)),
    )(p0, conv_w)


def _gdn_pre_bwd(p0, conv_w, dqkv, pad, *, name):
    lp = p0.shape[0]
    nq = GDN_HEADS
    qscale = HEAD_W ** -0.5

    def body(x_ref, w_ref, dy_ref, dx_ref, dw_ref):
        j = pl.program_id(0)
        x, w = x_ref[...], w_ref[...]
        c = _conv_taps(x, w)
        s = _silu(c)
        r = lax.rsqrt(jnp.sum(s * s, axis=-1, keepdims=True) + L2_EPS)
        rows = lax.broadcasted_iota(jnp.int32, (lp, 1), 0)
        dy = jnp.where(rows >= pad, dy_ref[...], 0.0)
        nrm = s * r
        dn = dy * jnp.where(j < nq, qscale, 1.0)
        ds_norm = r * (dn - nrm * jnp.sum(nrm * dn, axis=-1, keepdims=True))
        ds = jnp.where(j < 2 * nq, ds_norm, dy)
        dc = ds * _silu_grad(c)
        dx = w[CONV_K - 1:CONV_K, :] * dc
        dws = [None] * CONV_K
        dws[CONV_K - 1] = jnp.sum(dc * x, axis=0, keepdims=True)
        for k in range(CONV_K - 1):
            sh = CONV_K - 1 - k
            dx = dx + w[k:k + 1, :] * pltpu.roll(dc, lp - sh, 0)
            dws[k] = jnp.sum(dc * pltpu.roll(x, sh, 0), axis=0, keepdims=True)
        dx_ref[...] = dx
        dw_ref[...] = jnp.concatenate(dws, axis=0)

    blk = pl.BlockSpec((lp, HEAD_W), lambda j: (0, j))
    wblk = pl.BlockSpec((CONV_K, HEAD_W), lambda j: (0, j))
    return pl.pallas_call(
        body, name=name, grid=(3 * nq,), in_specs=[blk, wblk, blk], out_specs=[blk, wblk],
        out_shape=[jax.ShapeDtypeStruct((lp, 3 * nq * HEAD_W), F32),
                   jax.ShapeDtypeStruct((CONV_K, 3 * nq * HEAD_W), F32)],
        compiler_params=_cp(("parallel",)),
    )(p0, conv_w, dqkv)


def _tri(c, strict):
    r = lax.broadcasted_iota(jnp.int32, (c, c), 0)
    q = lax.broadcasted_iota(jnp.int32, (c, c), 1)
    return (q < r) if strict else (q <= r)


@jax.custom_vjp
def _inv_unit_lower(m):
    c = m.shape[0]
    eye = (lax.broadcasted_iota(jnp.int32, (c, c), 0) == lax.broadcasted_iota(jnp.int32, (c, c), 1)).astype(F32)
    x = eye - m
    p = m
    n = 2
    while n < CHUNK:
        p = _dot3x(p, p, NN)
        x = x + _dot3x(x, p, NN)
        n *= 2
    return x


def _inv_fwd(m):
    t = _inv_unit_lower(m)
    return t, t


def _inv_bwd(t, g):
    return (-_dot3x(_dot3x(t, g, TN), t, NT),)


_inv_unit_lower.defvjp(_inv_fwd, _inv_bwd)


def _heads_to_rows(x, nh):
    return jnp.concatenate([x[:, h * HEAD_W:(h + 1) * HEAD_W] for h in range(nh)], axis=0)


def _rows_to_heads(x, nh):
    c = x.shape[0] // nh
    return jnp.concatenate([x[h * c:(h + 1) * c] for h in range(nh)], axis=1)


def _gdn_chunk(q, k, v, ba, alog, dtb, states, valid):
    nh = GDN_HEADS
    c = q.shape[0]
    r = nh * c
    lane = lax.broadcasted_iota(jnp.int32, (1, HEAD_W), 1)
    pick = lambda x, l: jnp.sum(jnp.where(lane == l, x, 0.0), axis=-1, keepdims=True)
    beta = jnp.concatenate([jnp.where(valid, _sigmoid(pick(ba, h)), 0.0) for h in range(nh)], axis=0)
    g = jnp.concatenate(
        [jnp.where(valid, -jnp.exp(pick(alog, h)) * _softplus(pick(ba, nh + h) + pick(dtb, h)), 0.0) for h in range(nh)],
        axis=0)
    qs, ks, vs = _heads_to_rows(q, nh), _heads_to_rows(k, nh), _heads_to_rows(v, nh)
    rr = lax.broadcasted_iota(jnp.int32, (r, r), 0)
    cc = lax.broadcasted_iota(jnp.int32, (r, r), 1)
    same = (rr // c) == (cc // c)
    causal, strict = same & (cc <= rr), same & (cc < rr)
    lower = jnp.where(causal, 1.0, 0.0).astype(BF16)
    upper = jnp.where(same & (cc >= rr), 1.0, 0.0).astype(BF16)
    gcb = _mask_mm(lower, upper, g * jnp.ones((1, HEAD_W), F32))
    gc_col = jnp.concatenate([gcb] * (r // HEAD_W), axis=1)
    decay = jnp.where(causal, jnp.exp(jnp.minimum(gc_col - gc_col.T, 0.0)), 0.0)
    egc = jnp.exp(gcb)
    kb = ks * beta
    m = jnp.where(strict, _bdot(kb, ks, NT) * decay, 0.0)
    t = _inv_unit_lower(m)
    u = _bdot(t, vs * beta, NN)
    w = _bdot(t, kb * egc, NN)
    a = _bdot(qs, ks, NT) * decay
    rows = lambda x, h: x[h * c:(h + 1) * c]
    qe = qs * egc
    v_new = u - jnp.concatenate([_bdot(rows(w, h), states[h], NN) for h in range(nh)], axis=0)
    o = jnp.concatenate([_bdot(rows(qe, h), states[h], NN) for h in range(nh)], axis=0) + _bdot(a, v_new, NN)
    new_states = []
    for h in range(nh):
        gl = gcb[(h + 1) * c - 1:(h + 1) * c, :]
        k_dec = rows(ks, h) * jnp.exp(gl - rows(gcb, h))
        new_states.append(states[h] * jnp.exp(gl) + _bdot(k_dec, rows(v_new, h), TN))
    return _rows_to_heads(o, nh), new_states


def _gdn_fwd(qkv, p0, alog_v, dtb_v, pad, *, name):
    lp = qkv.shape[0]
    n = lp // CHUNK
    nh = GDN_HEADS

    def body(q_ref, k_ref, v_ref, ba_ref, al_ref, dt_ref, o_ref, st_ref, s_ref):
        i = pl.program_id(0)

        @pl.when(i == 0)
        def _():
            s_ref[...] = jnp.zeros_like(s_ref)

        valid = (i * CHUNK + lax.broadcasted_iota(jnp.int32, (CHUNK, 1), 0)) >= pad
        s = s_ref[...]
        o, s2 = _gdn_chunk(q_ref[...], k_ref[...], v_ref[...], ba_ref[...], al_ref[...], dt_ref[...],
                           [s[h] for h in range(nh)], valid)
        st_ref[...] = s
        o_ref[...] = o
        for h in range(nh):
            s_ref[h] = s2[h]

    w = nh * HEAD_W
    vec = pl.BlockSpec((1, HEAD_W), lambda i: (0, 0))
    return pl.pallas_call(
        body, name=name, grid=(n,),
        in_specs=[pl.BlockSpec((CHUNK, w), lambda i: (i, 0)), pl.BlockSpec((CHUNK, w), lambda i: (i, 1)),
                  pl.BlockSpec((CHUNK, w), lambda i: (i, 2)), pl.BlockSpec((CHUNK, HEAD_W), lambda i: (i, AB_BA // HEAD_W)),
                  vec, vec],
        out_specs=[pl.BlockSpec((CHUNK, w), lambda i: (i, 0)),
                   pl.BlockSpec((None, nh, HEAD_W, HEAD_W), lambda i: (i, 0, 0, 0))],
        out_shape=[jax.ShapeDtypeStruct((lp, w), F32), jax.ShapeDtypeStruct((n, nh, HEAD_W, HEAD_W), F32)],
        scratch_shapes=[pltpu.VMEM((nh, HEAD_W, HEAD_W), F32)],
        compiler_params=_cp(("arbitrary",)),
    )(qkv, qkv, qkv, p0, alog_v, dtb_v)


def _gdn_bwd(qkv, p0, alog_v, dtb_v, states, do, pad, *, name):
    lp = qkv.shape[0]
    n = lp // CHUNK
    nh = GDN_HEADS

    def body(q_ref, k_ref, v_ref, ba_ref, al_ref, dt_ref, st_ref, do_ref,
             dq_ref, dk_ref, dv_ref, dba_ref, dal_ref, ddt_ref, ds_ref):
        step = pl.program_id(0)
        i = n - 1 - step

        @pl.when(step == 0)
        def _():
            ds_ref[...] = jnp.zeros_like(ds_ref)
            dal_ref[...] = jnp.zeros_like(dal_ref)
            ddt_ref[...] = jnp.zeros_like(ddt_ref)

        valid = (i * CHUNK + lax.broadcasted_iota(jnp.int32, (CHUNK, 1), 0)) >= pad
        st, dst = st_ref[...], ds_ref[...]
        fn = functools.partial(_gdn_chunk, valid=valid)
        _, vjp = jax.vjp(fn, q_ref[...], k_ref[...], v_ref[...], ba_ref[...], al_ref[...], dt_ref[...],
                         [st[h] for h in range(nh)])
        dq, dk, dv, dba, dal, ddt, ds = vjp((do_ref[...], [dst[h] for h in range(nh)]))
        dq_ref[...] = dq
        dk_ref[...] = dk
        dv_ref[...] = dv
        dba_ref[...] = dba
        dal_ref[...] += dal
        ddt_ref[...] += ddt
        for h in range(nh):
            ds_ref[h] = ds[h]

    w = nh * HEAD_W
    rev = lambda c: (lambda s: (n - 1 - s, c))
    vec = pl.BlockSpec((1, HEAD_W), lambda s: (0, 0))
    dq, dk, dv, dba, dal, ddt = pl.pallas_call(
        body, name=name, grid=(n,),
        in_specs=[pl.BlockSpec((CHUNK, w), rev(0)), pl.BlockSpec((CHUNK, w), rev(1)), pl.BlockSpec((CHUNK, w), rev(2)),
                  pl.BlockSpec((CHUNK, HEAD_W), rev(AB_BA // HEAD_W)), vec, vec,
                  pl.BlockSpec((None, nh, HEAD_W, HEAD_W), lambda s: (n - 1 - s, 0, 0, 0)),
                  pl.BlockSpec((CHUNK, w), rev(0))],
        out_specs=[pl.BlockSpec((CHUNK, w), rev(0)), pl.BlockSpec((CHUNK, w), rev(0)), pl.BlockSpec((CHUNK, w), rev(0)),
                   pl.BlockSpec((CHUNK, HEAD_W), rev(0)), vec, vec],
        out_shape=[jax.ShapeDtypeStruct((lp, w), F32)] * 3 + [jax.ShapeDtypeStruct((lp, HEAD_W), F32)]
        + [jax.ShapeDtypeStruct((1, HEAD_W), F32)] * 2,
        scratch_shapes=[pltpu.VMEM((nh, HEAD_W, HEAD_W), F32)],
        compiler_params=_cp(("arbitrary",)),
    )(qkv, qkv, qkv, p0, alog_v, dtb_v, states, do)
    return dq, dk, dv, dba, dal, ddt


HG_LEVELS = (32, 16, 8, 4, 2, 1)
HG_GROUP = 4


def _hg_masks():
    import numpy as np
    c = CHUNK
    t = np.arange(c)[:, None]
    j = np.arange(c)[None, :]
    sums = [j <= t, j > t]
    pairs = [j == t]
    for m in HG_LEVELS:
        p = (t // (2 * m)) * (2 * m)
        r = p + m
        upper = t >= r
        sums.append(upper & (j > r) & (j <= t))
        sums.append(~upper & (j > t) & (j <= r))
        pairs.append(upper & (j < r) & (j >= p))
    sums = np.concatenate(sums, axis=0).astype(np.float32)
    pairs = np.concatenate([np.kron(np.eye(HG_GROUP), p) for p in pairs], axis=0).astype(np.float32)
    return jnp.asarray(sums, BF16), jnp.asarray(sums.T, BF16), jnp.asarray(pairs, F32)


def _split3(x):
    hi = x.astype(BF16)
    r1 = x - hi.astype(F32)
    mid = r1.astype(BF16)
    return hi, mid, (r1 - mid.astype(F32)).astype(BF16)


def _dot3x(a, b, dims):
    ah, am, _ = _split3(a)
    bh, bm, _ = _split3(b)
    return _dot(ah, bh, dims) + (_dot(ah, bm, dims) + _dot(am, bh, dims))


def _mask_mm_raw(m, x):
    return sum(_dot(m, part, NN) for part in _split3(x))


@jax.custom_vjp
def _mask_mm(m, mt, x):
    return _mask_mm_raw(m, x)


def _mask_mm_fwd(m, mt, x):
    return _mask_mm_raw(m, x), (m, mt)


def _mask_mm_bwd(res, g):
    m, mt = res
    return jnp.zeros_like(m), jnp.zeros_like(mt), _mask_mm_raw(mt, g)


_mask_mm.defvjp(_mask_mm_fwd, _mask_mm_bwd)


def _hg_chunk(qr, fr, ir, lb, states, valid, sums, sums_t, pairs):
    nh = HG_GROUP
    c = qr.shape[0]
    r = nh * c
    fg = lb + (1.0 - lb) * _sigmoid(fr)
    logf = jnp.where(valid, jnp.log(fg), 0.0)
    k = jnp.where(valid, 1.0 - fg, 0.0)
    qs = jnp.where(valid, _silu(qr), 0.0)
    v = jnp.where(valid, ir, 0.0)
    e = jnp.exp(_mask_mm(sums, sums_t, logf))
    blk = lambda n: e[n * c:(n + 1) * c]
    mask = lambda n: pairs[n * r:(n + 1) * r]
    stack = lambda x: _heads_to_rows(x, nh)
    a = mask(0) * _bdot(stack(qs), stack(k), NT)
    for lvl in range(len(HG_LEVELS)):
        a = a + mask(1 + lvl) * _bdot(stack(qs * blk(2 + 2 * lvl)), stack(k * blk(3 + 2 * lvl)), NT)
    av = _bdot(a, stack(v), NN)
    eb = blk(0)
    qe, kd = qs * eb, k * blk(1)
    outs, new_states = [], []
    for h in range(nh):
        cs = slice(h * HEAD_W, (h + 1) * HEAD_W)
        outs.append(_bdot(qe[:, cs], states[h], NT) + av[h * c:(h + 1) * c])
        new_states.append(states[h] * eb[c - 1:c, cs] + _bdot(v[:, cs], kd[:, cs], TN))
    return jnp.concatenate(outs, axis=1), new_states


def _hg_fwd(p1, lb, pad, *, name):
    lp = p1.shape[0]
    n = lp // CHUNK
    nh = HG_HEADS

    def body(q_ref, f_ref, i_ref, lb_ref, sums_ref, sums_t_ref, pairs_ref, o_ref, st_ref, s_ref):
        i = pl.program_id(1)

        @pl.when(i == 0)
        def _():
            s_ref[...] = jnp.zeros_like(s_ref)

        valid = (i * CHUNK + lax.broadcasted_iota(jnp.int32, (CHUNK, 1), 0)) >= pad
        s = s_ref[...]
        o, s2 = _hg_chunk(q_ref[...], f_ref[...], i_ref[...], lb_ref[...], [s[h] for h in range(grp)], valid,
                          sums_ref[...], sums_t_ref[...], pairs_ref[...])
        st_ref[...] = s
        o_ref[...] = o
        for h in range(grp):
            s_ref[h] = s2[h]

    masks = _hg_masks()
    grp, ngrp, gw = HG_GROUP, nh // HG_GROUP, HG_GROUP * HEAD_W
    blk = lambda off: pl.BlockSpec((CHUNK, gw), lambda h, i: (i, off + h))
    const = lambda a: pl.BlockSpec(a.shape, lambda h, i: (0, 0))
    return pl.pallas_call(
        body, name=name, grid=(ngrp, n),
        in_specs=[blk(0), blk(ngrp), blk(2 * ngrp), pl.BlockSpec((1, gw), lambda h, i: (0, h))]
        + [const(a) for a in masks],
        out_specs=[blk(0), pl.BlockSpec((grp, None, HEAD_W, HEAD_W), lambda h, i: (h, i, 0, 0))],
        out_shape=[jax.ShapeDtypeStruct((lp, nh * HEAD_W), F32), jax.ShapeDtypeStruct((nh, n, HEAD_W, HEAD_W), F32)],
        scratch_shapes=[pltpu.VMEM((grp, HEAD_W, HEAD_W), F32)],
        compiler_params=_cp(("parallel", "arbitrary")),
    )(p1, p1, p1, lb, *masks)


def _hg_bwd(p1, lb, states, do, pad, *, name):
    lp = p1.shape[0]
    n = lp // CHUNK
    nh = HG_HEADS

    def body(q_ref, f_ref, i_ref, lb_ref, st_ref, do_ref, sums_ref, sums_t_ref, pairs_ref,
             dq_ref, df_ref, di_ref, dlb_ref, ds_ref):
        step = pl.program_id(1)
        i = n - 1 - step

        @pl.when(step == 0)
        def _():
            ds_ref[...] = jnp.zeros_like(ds_ref)
            dlb_ref[...] = jnp.zeros_like(dlb_ref)

        valid = (i * CHUNK + lax.broadcasted_iota(jnp.int32, (CHUNK, 1), 0)) >= pad
        fn = functools.partial(_hg_chunk, valid=valid, sums=sums_ref[...], sums_t=sums_t_ref[...],
                               pairs=pairs_ref[...])
        st, dst = st_ref[...], ds_ref[...]
        _, vjp = jax.vjp(fn, q_ref[...], f_ref[...], i_ref[...], lb_ref[...], [st[h] for h in range(grp)])
        dq, df, di, dlb, ds = vjp((do_ref[...], [dst[h] for h in range(grp)]))
        dq_ref[...] = dq
        df_ref[...] = df
        di_ref[...] = di
        dlb_ref[...] += dlb
        for h in range(grp):
            ds_ref[h] = ds[h]

    masks = _hg_masks()
    grp, ngrp, gw = HG_GROUP, nh // HG_GROUP, HG_GROUP * HEAD_W
    blk = lambda off: pl.BlockSpec((CHUNK, gw), lambda h, s: (n - 1 - s, off + h))
    const = lambda a: pl.BlockSpec(a.shape, lambda h, s: (0, 0))
    w = nh * HEAD_W
    return pl.pallas_call(
        body, name=name, grid=(ngrp, n),
        in_specs=[blk(0), blk(ngrp), blk(2 * ngrp), pl.BlockSpec((1, gw), lambda h, s: (0, h)),
                  pl.BlockSpec((grp, None, HEAD_W, HEAD_W), lambda h, s: (h, n - 1 - s, 0, 0)), blk(0)]
        + [const(a) for a in masks],
        out_specs=[blk(0), blk(0), blk(0), pl.BlockSpec((1, gw), lambda h, s: (0, h))],
        out_shape=[jax.ShapeDtypeStruct((lp, w), F32)] * 3 + [jax.ShapeDtypeStruct((1, w), F32)],
        scratch_shapes=[pltpu.VMEM((grp, HEAD_W, HEAD_W), F32)],
        compiler_params=_cp(("parallel", "arbitrary")),
    )(p1, p1, p1, lb, states, do, *masks)


SB_GROUP = 4


def _sb_cat(kind, first_key=0):
    r = lax.broadcasted_iota(jnp.int32, (SB_BLOCK, 2 * SB_BLOCK), 0)
    c = lax.broadcasted_iota(jnp.int32, (SB_BLOCK, 2 * SB_BLOCK), 1)
    tri = {"after": c < r, "incl": r <= c, "before": r < c}[kind]
    m = ((c >= SB_BLOCK) | tri) & (r >= first_key)
    return jnp.where(m, 1.0, 0.0).astype(BF16)


def _sb_cumsum(x, cat):
    hi = x.astype(BF16)
    lo = (x - hi.astype(F32)).astype(BF16)
    return _dot(hi, cat, NN) + _dot(lo, cat, NN)


def _sb_logsig(z):
    e = jnp.exp(-jnp.abs(z))
    lse = jnp.where(e < 1e-4, e, jnp.log(1.0 + e))
    lsz = jnp.minimum(z, 0.0) - lse
    return lsz, lsz - z, e


def _sb_stack(x, scale=None):
    lane = lax.broadcasted_iota(jnp.int32, (1, HEAD_W), 1)
    if scale is not None:
        x = x * scale
    return jnp.concatenate([jnp.where(lane < SB_DH, x, 0.0), jnp.where(lane >= SB_DH, x, 0.0)], axis=0).astype(BF16)


def _sb_unstack(x):
    lane = lax.broadcasted_iota(jnp.int32, (1, HEAD_W), 1)
    return jnp.where(lane < SB_DH, x[:SB_BLOCK], x[SB_BLOCK:])


def _sb_fwd(p0, pad, *, name):
    lp = p0.shape[0]
    nb = lp // SB_BLOCK
    npair = SB_HEADS // 2
    blk0 = AB_SB // HEAD_W
    scale = SB_DH ** -0.5
    gw = SB_GROUP * SB_BLOCK
    assert pad < SB_BLOCK

    def body(q_ref, k_ref, v_ref, o_ref, tot_ref):
        i = pl.program_id(1)
        qs = _sb_stack(q_ref[...], scale)
        qpos = i * SB_BLOCK + lax.broadcasted_iota(jnp.int32, (SB_BLOCK, 1), 0)
        qpos = jnp.concatenate([qpos, qpos], axis=0)
        cat = _sb_cat("after")
        cat0 = _sb_cat("after", pad)
        ng = i // SB_GROUP

        def single(t, carry):
            acc, run = carry
            kb = i - t
            off = pl.multiple_of(kb * SB_BLOCK, SB_BLOCK)
            kblk = k_ref[pl.ds(off, SB_BLOCK), :].astype(BF16)
            vblk = v_ref[pl.ds(off, SB_BLOCK), :].astype(BF16)
            kpos = kb * SB_BLOCK + lax.broadcasted_iota(jnp.int32, (1, SB_BLOCK), 1)
            allowed = (kpos < qpos) & (kpos >= pad)
            lsz, l1m, _ = _sb_logsig(_dot(qs, kblk, NT))
            al = _sb_cumsum(jnp.where(allowed, l1m, 0.0), cat)
            wgt = jnp.where(allowed, jnp.exp(lsz + al[:, :SB_BLOCK] + run), 0.0)
            return acc + _dot(wgt.astype(BF16), vblk, NN), run + al[:, SB_BLOCK:]

        def group(t, carry):
            acc, run = carry
            gi = ng - 1 - t
            off = pl.multiple_of(gi * gw, gw)
            kg = k_ref[pl.ds(off, gw), :].astype(BF16)
            vg = v_ref[pl.ds(off, gw), :].astype(BF16)
            lsz, l1m, _ = _sb_logsig(_dot(qs, kg, NT))
            args = [None] * SB_GROUP
            for g in reversed(range(SB_GROUP)):
                sl = slice(g * SB_BLOCK, (g + 1) * SB_BLOCK)
                al = _sb_cumsum(l1m[:, sl], jnp.where(gi == 0, cat0, cat) if g == 0 else cat)
                args[g] = lsz[:, sl] + al[:, :SB_BLOCK] + run
                run = run + al[:, SB_BLOCK:]
            wgt = jnp.exp(jnp.concatenate(args, axis=1)).astype(BF16)
            return acc + _dot(wgt, vg, NN), run

        zero = (jnp.zeros((2 * SB_BLOCK, HEAD_W), F32), jnp.zeros((2 * SB_BLOCK, HEAD_W), F32))
        carry = lax.fori_loop(0, i + 1 - ng * SB_GROUP, single, zero)
        acc, run = lax.fori_loop(0, ng, group, carry)
        o_ref[...] = _sb_unstack(acc)
        tot_ref[...] = _sb_unstack(run)

    full = lambda c0: pl.BlockSpec((lp, HEAD_W), lambda p, i: (0, c0 + p))
    out = pl.BlockSpec((SB_BLOCK, HEAD_W), lambda p, i: (i, p))
    return pl.pallas_call(
        body, name=name, grid=(npair, nb),
        in_specs=[pl.BlockSpec((SB_BLOCK, HEAD_W), lambda p, i: (i, blk0 + p)), full(blk0 + npair), full(blk0 + 2 * npair)],
        out_specs=[out, out],
        out_shape=[jax.ShapeDtypeStruct((lp, npair * HEAD_W), F32)] * 2,
        compiler_params=_cp(("parallel", "arbitrary")),
    )(p0, p0, p0)


def _sb_bwd(p0, tot, dsrc, d_blk0, pad, *, name):
    lp = p0.shape[0]
    nb = lp // SB_BLOCK
    npair = SB_HEADS // 2
    blk0 = AB_SB // HEAD_W
    scale = SB_DH ** -0.5
    gw = SB_GROUP * SB_BLOCK
    assert pad < SB_BLOCK

    def body(q_ref, k_ref, v_ref, tot_ref, do_ref, dq_ref, dkt_ref, dvt_ref):
        i = pl.program_id(1)

        @pl.when(i == 0)
        def _():
            dkt_ref[...] = jnp.zeros_like(dkt_ref)
            dvt_ref[...] = jnp.zeros_like(dvt_ref)

        qs = _sb_stack(q_ref[...], scale)
        dos = _sb_stack(do_ref[...])
        qst, dost = qs.T, dos.T
        totv = tot_ref[...]
        ones = jnp.ones((1, HEAD_W), F32)
        tots = jnp.concatenate([totv[:, 0:1] * ones, totv[:, SB_DH:SB_DH + 1] * ones], axis=0)
        qpos = i * SB_BLOCK + lax.broadcasted_iota(jnp.int32, (SB_BLOCK, 1), 0)
        qpos = jnp.concatenate([qpos, qpos], axis=0)
        incl, incl0 = _sb_cat("incl"), _sb_cat("incl", pad)
        before = _sb_cat("before")
        ng = i // SB_GROUP

        def dscore(z, e, ev, dl1m):
            r = 1.0 / (1.0 + e)
            sg = jnp.where(z >= 0, r, e * r)
            return ev * (1.0 - sg) - dl1m * sg

        def group(gi, carry):
            dq, prun, erun = carry
            off = pl.multiple_of(gi * gw, gw)
            kg = k_ref[pl.ds(off, gw), :].astype(BF16)
            vg = v_ref[pl.ds(off, gw), :].astype(BF16)
            z = _dot(qs, kg, NT)
            lsz, l1m, e = _sb_logsig(z)
            dwgt = _dot(dos, vg, NT)
            dzs = [None] * SB_GROUP
            wgts = [None] * SB_GROUP
            for g in range(SB_GROUP):
                sl = slice(g * SB_BLOCK, (g + 1) * SB_BLOCK)
                al = _sb_cumsum(l1m[:, sl], jnp.where(gi == 0, incl0, incl) if g == 0 else incl)
                wgt = jnp.exp(lsz[:, sl] + (tots - prun - al[:, :SB_BLOCK]))
                prun = prun + al[:, SB_BLOCK:]
                ev = wgt * dwgt[:, sl]
                el = _sb_cumsum(ev, before)
                dzs[g] = dscore(z[:, sl], e[:, sl], ev, erun + el[:, :SB_BLOCK])
                erun = erun + el[:, SB_BLOCK:]
                wgts[g] = wgt
            dz = jnp.concatenate(dzs, axis=1).astype(BF16)
            wg = jnp.concatenate(wgts, axis=1).astype(BF16)
            dkt_ref[:, pl.ds(off, gw)] += _dot(qst, dz, NN)
            dvt_ref[:, pl.ds(off, gw)] += _dot(dost, wg, NN)
            return dq + _dot(dz, kg, NN), prun, erun

        def single(t, carry):
            dq, prun, erun = carry
            kb = ng * SB_GROUP + t
            off = pl.multiple_of(kb * SB_BLOCK, SB_BLOCK)
            kblk = k_ref[pl.ds(off, SB_BLOCK), :].astype(BF16)
            vblk = v_ref[pl.ds(off, SB_BLOCK), :].astype(BF16)
            kpos = kb * SB_BLOCK + lax.broadcasted_iota(jnp.int32, (1, SB_BLOCK), 1)
            allowed = (kpos < qpos) & (kpos >= pad)
            z = _dot(qs, kblk, NT)
            lsz, l1m, e = _sb_logsig(z)
            al = _sb_cumsum(jnp.where(allowed, l1m, 0.0), incl)
            wgt = jnp.where(allowed, jnp.exp(lsz + (tots - prun - al[:, :SB_BLOCK])), 0.0)
            ev = wgt * _dot(dos, vblk, NT)
            el = _sb_cumsum(ev, before)
            dz = jnp.where(allowed, dscore(z, e, ev, erun + el[:, :SB_BLOCK]), 0.0).astype(BF16)
            dkt_ref[:, pl.ds(off, SB_BLOCK)] += _dot(qst, dz, NN)
            dvt_ref[:, pl.ds(off, SB_BLOCK)] += _dot(dost, wgt.astype(BF16), NN)
            return dq + _dot(dz, kblk, NN), prun + al[:, SB_BLOCK:], erun + el[:, SB_BLOCK:]

        zero = tuple(jnp.zeros((2 * SB_BLOCK, HEAD_W), F32) for _ in range(3))
        carry = lax.fori_loop(0, ng, group, zero)
        dq, _, _ = lax.fori_loop(0, i + 1 - ng * SB_GROUP, single, carry)
        dq_ref[...] = _sb_unstack(dq) * scale

    full = lambda c0: pl.BlockSpec((lp, HEAD_W), lambda p, i: (0, c0 + p))
    qb = lambda c0: pl.BlockSpec((SB_BLOCK, HEAD_W), lambda p, i: (i, c0 + p))
    tr = pl.BlockSpec((HEAD_W, lp), lambda p, i: (p, 0))
    return pl.pallas_call(
        body, name=name, grid=(npair, nb),
        in_specs=[qb(blk0), full(blk0 + npair), full(blk0 + 2 * npair), qb(0), qb(d_blk0)],
        out_specs=[qb(0), tr, tr],
        out_shape=[jax.ShapeDtypeStruct((lp, npair * HEAD_W), F32)]
        + [jax.ShapeDtypeStruct((npair * HEAD_W, lp), F32)] * 2,
        compiler_params=_cp(("parallel", "arbitrary")),
    )(p0, p0, p0, tot, dsrc)


def _local_step(h0, target, pad, wts):
    lp = h0.shape[0]
    tm = _row_tile(lp, 1056)
    tkl = tm
    d = D_MODEL
    mm = _mm
    g = {}

    h0_b = h0.astype(BF16)
    p0 = mm(h0_b, wts["w_ab"], "NN", tm=tm, tn=768, tk=d, name="l0_in_proj")
    qkv = _gdn_pre_fwd(p0, wts["conv_w"], pad, name="gdn_pre_fwd")
    oa_raw, gdn_states = _gdn_fwd(qkv, p0, wts["alog_v"], wts["dtb_v"], pad, name="gdn_fwd")
    ob, sb_tot = _sb_fwd(p0, pad, name="sb_fwd")
    oab = _gate_fwd(oa_raw, p0, AB_Z // HEAD_W, wts["ab_gn"], ob, heads=GDN_HEADS, name="gdn_gate_fwd")
    mix0 = mm(oab, wts["w_out0"], "NN", tm=tm, tn=512, tk=d, name="l0_out_proj")
    h0a, h0a_b = _ln_fwd(h0, mix0, wts["ln_mix_g"][0], wts["ln_mix_b"][0], name="ln_mix0_fwd")
    u0 = mm(h0a_b, wts["w1"][0], "NN", tm=tm, tn=512, tk=d, b_dev=True, name="mlp0_up")
    y0 = mm(u0, wts["w2"][0], "NN", tm=tm, tn=512, tk=d, a_fn="relu2", name="mlp0_down")
    h0b, h0b_b = _ln_fwd(h0a, y0, wts["ln_ffn_g"][0], wts["ln_ffn_b"][0], name="ln_ffn0_fwd")
    p1 = mm(h0b_b, wts["w_c"], "NN", tm=tm, tn=512, tk=d, b_dev=True, name="l1_in_proj")
    oc_raw, hg_states = _hg_fwd(p1, wts["lb"], pad, name="hg_fwd")
    oc = _gate_fwd(oc_raw, p1, 3 * HG_HEADS, wts["c_gn"], oc_raw, heads=HG_HEADS, name="hg_gate_fwd")
    mix1 = mm(oc, wts["w_out1"], "NN", tm=tm, tn=512, tk=d, name="l1_out_proj")
    h1a, h1a_b = _ln_fwd(h0b, mix1, wts["ln_mix_g"][1], wts["ln_mix_b"][1], name="ln_mix1_fwd")
    u1 = mm(h1a_b, wts["w1"][1], "NN", tm=tm, tn=512, tk=d, b_dev=True, name="mlp1_up")
    y1 = mm(u1, wts["w2"][1], "NN", tm=tm, tn=512, tk=d, a_fn="relu2", name="mlp1_down")
    h1b, _ = _ln_fwd(h1a, y1, wts["ln_ffn_g"][1], wts["ln_ffn_b"][1], name="ln_ffn1_fwd")
    dy, loss_vec = _loss_head(h1b, target, name="loss_head")

    def mlp_bwd(layer, h_in_b, u, dpre, dpre_b):
        du = mm(dpre_b, wts["w2"][layer], "NT", tm=tm, tn=512, tk=d, epi="relu2grad", c=u, out_dtype=BF16,
                name=f"mlp{layer}_d_hidden")
        dw2 = mm(u, dpre_b, "TN", tm=1024, tn=1024, tk=tkl, a_fn="relu2", name=f"mlp{layer}_dw2")
        dw1 = mm(h_in_b, du, "TN", tm=1024, tn=512, tk=tkl, out_dev=True, name=f"mlp{layer}_dw1")
        dh = mm(du, wts["w1"][layer], "NT", tm=tm, tn=1024, tk=512, b_dev=True, epi="add", c=dpre, scale=DN_ALPHA,
                name=f"mlp{layer}_d_in")
        return dh, dw1, dw2

    ln_ffn_dg, ln_ffn_db, ln_mix_dg, ln_mix_db, dw1s, dw2s = ([None, None] for _ in range(6))
    dpre, dpre_b, ln_ffn_dg[1], ln_ffn_db[1] = _ln_bwd(h1a, y1, wts["ln_ffn_g"][1], dy, name="ln_ffn1_bwd")
    dh1a, dw1s[1], dw2s[1] = mlp_bwd(1, h1a_b, u1, dpre, dpre_b)
    dpre, dpre_b, ln_mix_dg[1], ln_mix_db[1] = _ln_bwd(h0b, mix1, wts["ln_mix_g"][1], dh1a, name="ln_mix1_bwd")
    g["c_w_out"] = mm(oc, dpre_b, "TN", tm=1024, tn=1024, tk=tkl, name="l1_dw_out")
    doc = mm(dpre_b, wts["w_out1"], "NT", tm=tm, tn=512, tk=d, name="l1_d_gate")
    doc_raw, dz1, g["c_gn"] = _gate_bwd(oc_raw, p1, 3 * HG_HEADS, wts["c_gn"], doc, heads=HG_HEADS, name="hg_gate_bwd")
    dq1, df1, di1, g["lb"] = _hg_bwd(p1, wts["lb"], hg_states, doc_raw, pad, name="hg_bwd")
    dp1 = jnp.concatenate([dq1, df1, di1, dz1], axis=1).astype(BF16)
    g["c_w_in"] = mm(h0b_b, dp1, "TN", tm=1024, tn=512, tk=tkl, out_dev=True, name="l1_dw_in")
    dh0b = mm(dp1, wts["w_c"], "NT", tm=tm, tn=1024, tk=512, b_dev=True, epi="add", c=dpre, scale=DN_ALPHA,
              name="l1_d_in")
    dpre, dpre_b, ln_ffn_dg[0], ln_ffn_db[0] = _ln_bwd(h0a, y0, wts["ln_ffn_g"][0], dh0b, name="ln_ffn0_bwd")
    dh0a, dw1s[0], dw2s[0] = mlp_bwd(0, h0a_b, u0, dpre, dpre_b)
    dpre, dpre_b, ln_mix_dg[0], ln_mix_db[0] = _ln_bwd(h0, mix0, wts["ln_mix_g"][0], dh0a, name="ln_mix0_bwd")
    g["ab_w_out"] = mm(oab, dpre_b, "TN", tm=1024, tn=1024, tk=tkl, name="l0_dw_out")
    doab = mm(dpre_b, wts["w_out0"], "NT", tm=tm, tn=512, tk=d, name="l0_d_gate")
    doa_raw, dz0, g["ab_gn"] = _gate_bwd(oa_raw, p0, AB_Z // HEAD_W, wts["ab_gn"], doab, heads=GDN_HEADS,
                                         name="gdn_gate_bwd")
    dqb, dkb_t, dvb_t = _sb_bwd(p0, sb_tot, doab, GDN_HEADS, pad, name="sb_bwd")
    dkb, dvb = dkb_t.T, dvb_t.T
    dqn, dkn, dvn, dba, g["alog_v"], g["dtb_v"] = _gdn_bwd(qkv, p0, wts["alog_v"], wts["dtb_v"], gdn_states, doa_raw,
                                                           pad, name="gdn_bwd")
    dconv_in, g["conv_w"] = _gdn_pre_bwd(p0, wts["conv_w"], jnp.concatenate([dqn, dkn, dvn], axis=1), pad,
                                         name="gdn_pre_bwd")
    dp0 = jnp.concatenate([dconv_in, dz0, dqb, dkb, dvb, dba, jnp.zeros((lp, AB_CAT - AB_BA - HEAD_W), F32)],
                          axis=1).astype(BF16)
    g["w_ab"] = mm(h0_b, dp0, "TN", tm=1024, tn=768, tk=tkl, name="l0_dw_in")
    dh0 = mm(dp0, wts["w_ab"], "NT", tm=tm, tn=1024, tk=768, epi="add", c=dpre, scale=DN_ALPHA, name="l0_d_in")

    g["w1"], g["w2"] = dw1s, dw2s
    g["ln_mix_g"] = jnp.concatenate(ln_mix_dg, axis=0)
    g["ln_mix_b"] = jnp.concatenate(ln_mix_db, axis=0)
    g["ln_ffn_g"] = jnp.concatenate(ln_ffn_dg, axis=0)
    g["ln_ffn_b"] = jnp.concatenate(ln_ffn_db, axis=0)
    return loss_vec, dh0, g


N_CHIP = N_DEV // 2


def _place():
    x, y, c = lax.axis_index("x"), lax.axis_index("y"), lax.axis_index("c")
    return x, y, c, 2 * x + y


def _chip_dev(chip, core):
    return (chip // 2, chip % 2, core)


def _remote(src, dst, send_sem, recv_sem, dev):
    return pltpu.make_async_remote_copy(src_ref=src, dst_ref=dst, send_sem=send_sem, recv_sem=recv_sem,
                                        device_id=dev, device_id_type=pl.DeviceIdType.MESH)


_ANY = pl.BlockSpec(memory_space=pl.ANY)


def _gather(srcs, dtypes, *, name):
    n = len(srcs)
    blocks = [s.shape for s in srcs]

    def body(*refs):
        ins, outs, stages = refs[:n], refs[n:2 * n], refs[2 * n:3 * n]
        send_sems, recv_sems, local_sems = refs[3 * n:]
        x, y, c, chip = _place()
        me = 2 * chip + c
        sibling = (x, y, 1 - c)
        local, pending = [], []
        for i in range(n):
            stages[i][...] = ins[i][...].astype(dtypes[i])
            loc = pltpu.make_async_copy(stages[i], outs[i].at[me], local_sems.at[i])
            loc.start()
            local.append(loc)
            first = [_remote(stages[i], outs[i].at[me], send_sems.at[i, 0], recv_sems.at[i, 0], sibling)]
            for j in range(1, N_CHIP):
                first.append(_remote(stages[i], outs[i].at[me], send_sems.at[i, j], recv_sems.at[i, j],
                                     _chip_dev(jnp.bitwise_xor(chip, j), c)))
            for cp in first:
                cp.start()
            pending += first
        for i in range(n):
            for j in range(1, N_CHIP):
                slot = outs[i].at[2 * jnp.bitwise_xor(chip, j) + c]
                _remote(slot, slot, send_sems.at[i, j], recv_sems.at[i, j], sibling).wait_recv()
                fwd = _remote(slot, slot, send_sems.at[i, N_CHIP - 1 + j], recv_sems.at[i, N_CHIP - 1 + j], sibling)
                fwd.start()
                pending.append(fwd)
        for i in range(n):
            blk = outs[i].at[me]
            for k in (0, *range(N_CHIP, 2 * N_CHIP - 1)):
                _remote(blk, blk, send_sems.at[i, k], recv_sems.at[i, k], sibling).wait_recv()
        for cp in pending:
            cp.wait_send()
        for cp in local:
            cp.wait()

    scratch = [pltpu.VMEM(b, dt) for b, dt in zip(blocks, dtypes)]
    scratch += [pltpu.SemaphoreType.DMA((n, 2 * N_CHIP - 1)), pltpu.SemaphoreType.DMA((n, 2 * N_CHIP - 1)),
                pltpu.SemaphoreType.DMA((n,))]
    return pl.pallas_call(
        body, name=name, in_specs=[pl.BlockSpec(memory_space=pltpu.VMEM)] * n, out_specs=[_ANY] * n,
        out_shape=[jax.ShapeDtypeStruct((N_DEV, *b), dt) for b, dt in zip(blocks, dtypes)],
        scratch_shapes=scratch, compiler_params=_cp(has_side_effects=True),
    )(*srcs)


def _to_sibling(parts, *, name):
    n = len(parts)

    def body(*refs):
        ins, outs = refs[:n], refs[n:2 * n]
        send_sems, recv_sems = refs[2 * n:]
        x, y, c, _ = _place()
        copies = [_remote(ins[i].at[2 * k + (1 - c)], outs[i].at[k], send_sems.at[i, k], recv_sems.at[i, k],
                          (x, y, 1 - c)) for i in range(n) for k in range(N_CHIP)]
        for cp in copies:
            cp.start()
        for cp in copies:
            cp.wait()

    return pl.pallas_call(
        body, name=name, in_specs=[_ANY] * n, out_specs=[_ANY] * n,
        out_shape=[jax.ShapeDtypeStruct((N_CHIP, *p.shape[1:]), p.dtype) for p in parts],
        scratch_shapes=[pltpu.SemaphoreType.DMA((n, N_CHIP)), pltpu.SemaphoreType.DMA((n, N_CHIP))],
        compiler_params=_cp(has_side_effects=True),
    )(*parts)


def _pair_sum(part, from_sibling, core, *, name):
    _, r, c = part.shape
    tm = _row_tile(r, 128)

    def body(core_ref, a_ref, b_ref, o_ref):
        o_ref[...] = (a_ref[...] + b_ref[...]).astype(BF16)

    return pl.pallas_call(
        body, name=name,
        grid_spec=pltpu.PrefetchScalarGridSpec(
            num_scalar_prefetch=1, grid=(N_CHIP, r // tm),
            in_specs=[pl.BlockSpec((None, tm, c), lambda k, i, core_ref: (2 * k + core_ref[0], i, 0)),
                      pl.BlockSpec((None, tm, c), lambda k, i, core_ref: (k, i, 0))],
            out_specs=pl.BlockSpec((None, tm, c), lambda k, i, core_ref: (k, i, 0))),
        out_shape=jax.ShapeDtypeStruct((N_CHIP, r, c), BF16), compiler_params=_cp(("parallel", "parallel")),
    )(core, part, from_sibling)


def _to_chips(sums, *, name):
    n = len(sums)

    def body(*refs):
        ins, outs = refs[:n], refs[n:2 * n]
        send_sems, recv_sems, local_sems = refs[2 * n:]
        _, _, c, chip = _place()
        copies = []
        for i in range(n):
            copies.append(pltpu.make_async_copy(ins[i].at[chip], outs[i].at[chip], local_sems.at[i]))
            for j in range(1, N_CHIP):
                other = jnp.bitwise_xor(chip, j)
                copies.append(_remote(ins[i].at[other], outs[i].at[chip], send_sems.at[i, j - 1], recv_sems.at[i, j - 1],
                                      _chip_dev(other, c)))
        for cp in copies:
            cp.start()
        for cp in copies:
            cp.wait()

    return pl.pallas_call(
        body, name=name, in_specs=[_ANY] * n, out_specs=[_ANY] * n,
        out_shape=[jax.ShapeDtypeStruct(s.shape, s.dtype) for s in sums],
        scratch_shapes=[pltpu.SemaphoreType.DMA((n, N_CHIP - 1)), pltpu.SemaphoreType.DMA((n, N_CHIP - 1)),
                        pltpu.SemaphoreType.DMA((n,))],
        compiler_params=_cp(has_side_effects=True),
    )(*sums)


def _adamw(w, parts, m, v, *, name):
    r, c = w.shape
    s = parts.shape[0]
    tm = _row_tile(r, 128) if r % 8 == 0 else r
    c1 = 1.0 - ADAM_B1 ** ADAM_STEP
    c2 = 1.0 - ADAM_B2 ** ADAM_STEP

    def body(w_ref, p_ref, m_ref, v_ref, g_ref, d_ref, m2_ref, v2_ref):
        g = p_ref[0].astype(F32)
        for j in range(1, s):
            g = g + p_ref[j].astype(F32)
        m2 = ADAM_B1 * m_ref[...] + (1.0 - ADAM_B1) * g
        v2 = ADAM_B2 * v_ref[...] + (1.0 - ADAM_B2) * jnp.square(g)
        g_ref[...] = g
        m2_ref[...] = m2
        v2_ref[...] = v2
        d_ref[...] = -ADAM_LR * ((m2 / c1) / (jnp.sqrt(v2 / c2) + ADAM_EPS) + ADAM_WD * w_ref[...])

    blk = pl.BlockSpec((tm, c), lambda i: (i, 0))
    return pl.pallas_call(
        body, name=name, grid=(r // tm,),
        in_specs=[blk, pl.BlockSpec((s, tm, c), lambda i: (0, i, 0)), blk, blk], out_specs=[blk] * 4,
        out_shape=[jax.ShapeDtypeStruct((r, c), F32)] * 4, compiler_params=_cp(("parallel",)),
    )(w, parts, m, v)


_WEIGHTS = ("meta_tokens", "ab_w_in", "ab_conv_w", "ab_a_log", "ab_dt_bias", "ab_gnorm_g", "ab_w_out", "c_w_in",
            "c_lb_raw", "c_gnorm_g", "c_w_out", "ln_mix_g", "ln_mix_b", "mlp_w1", "mlp_w2", "ln_ffn_g", "ln_ffn_b")
_PACK_ROWS = (("ln_mix_g", 0), ("ln_mix_b", 2), ("ln_ffn_g", 4), ("ln_ffn_b", 6), ("c_lb_raw", 8))
_PACK_MISC_ROW = 10
_PACK_MISC = (("ab_gnorm_g", 0, 128), ("c_gnorm_g", 128, 128), ("ab_a_log", 256, GDN_HEADS), ("ab_dt_bias", 260, GDN_HEADS))
_PACK_N = 16
_SMALL_META = 16
_SMALL_CONV = 32
_SMALL_N = 40


def _pack_replicated(p):
    rows = jnp.zeros((_PACK_N, D_MODEL), F32)
    for name, r0 in _PACK_ROWS:
        rows = rows.at[r0:r0 + 2].set(p[name])
    for name, c0, width in _PACK_MISC:
        rows = rows.at[_PACK_MISC_ROW, c0:c0 + width].set(p[name].reshape(width))
    return rows


def _unpack_replicated(rows, like):
    out = {}
    for name, r0 in _PACK_ROWS:
        out[name] = rows[r0:r0 + 2]
    for name, c0, width in _PACK_MISC:
        out[name] = rows[_PACK_MISC_ROW, c0:c0 + width].reshape(like[name].shape)
    return out


def _lower_bound(c_lb_raw):
    lb_all = jnp.cumsum(jax.nn.softmax(c_lb_raw.astype(F32), axis=0), axis=0)
    return (lb_all - lb_all[0:1])[1].reshape(1, -1)


def kernel(x, meta_tokens, ab_w_in, ab_conv_w, ab_a_log, ab_dt_bias, ab_gnorm_g, ab_w_out, c_w_in, c_lb_raw, c_gnorm_g, c_w_out, ln_mix_g, ln_mix_b, mlp_w1, mlp_w2, ln_ffn_g, ln_ffn_b, loss_target, m_meta_tokens, m_ab_w_in, m_ab_conv_w, m_ab_a_log, m_ab_dt_bias, m_ab_gnorm_g, m_ab_w_out, m_c_w_in, m_c_lb_raw, m_c_gnorm_g, m_c_w_out, m_ln_mix_g, m_ln_mix_b, m_mlp_w1, m_mlp_w2, m_ln_ffn_g, m_ln_ffn_b, v_meta_tokens, v_ab_w_in, v_ab_conv_w, v_ab_a_log, v_ab_dt_bias, v_ab_gnorm_g, v_ab_w_out, v_c_w_in, v_c_lb_raw, v_c_gnorm_g, v_c_w_out, v_ln_mix_g, v_ln_mix_b, v_mlp_w1, v_mlp_w2, v_ln_ffn_g, v_ln_ffn_b):
    w = dict(zip(_WEIGHTS, (meta_tokens, ab_w_in, ab_conv_w, ab_a_log, ab_dt_bias, ab_gnorm_g, ab_w_out, c_w_in, c_lb_raw,
                            c_gnorm_g, c_w_out, ln_mix_g, ln_mix_b, mlp_w1, mlp_w2, ln_ffn_g, ln_ffn_b)))
    mom = dict(zip(_WEIGHTS, (m_meta_tokens, m_ab_w_in, m_ab_conv_w, m_ab_a_log, m_ab_dt_bias, m_ab_gnorm_g, m_ab_w_out,
                              m_c_w_in, m_c_lb_raw, m_c_gnorm_g, m_c_w_out, m_ln_mix_g, m_ln_mix_b, m_mlp_w1, m_mlp_w2,
                              m_ln_ffn_g, m_ln_ffn_b)))
    var = dict(zip(_WEIGHTS, (v_meta_tokens, v_ab_w_in, v_ab_conv_w, v_ab_a_log, v_ab_dt_bias, v_ab_gnorm_g, v_ab_w_out,
                              v_c_w_in, v_c_lb_raw, v_c_gnorm_g, v_c_w_out, v_ln_mix_g, v_ln_mix_b, v_mlp_w1, v_mlp_w2,
                              v_ln_ffn_g, v_ln_ffn_b)))
    me = 4 * lax.axis_index("x") + 2 * lax.axis_index("y") + lax.axis_index("c")
    seq = x.shape[1]
    pad = (-(N_META + seq)) % SB_BLOCK
    lp = pad + N_META + seq
    meta_w = D_MODEL // N_DEV
    conv_w_all = 2 * GDN_HEADS * HEAD_W + GDN_HEADS * HEAD_W
    conv_w_mine = conv_w_all // N_DEV

    gathered = _gather(
        [w["meta_tokens"], w["ab_conv_w"][0], w["ab_w_in"][0], w["ab_w_out"][0], w["c_w_in"][0], w["c_w_out"][0],
         w["mlp_w1"], w["mlp_w2"]],
        [F32, F32, BF16, BF16, BF16, BF16, BF16, BF16], name="gather_weights")
    g_meta, g_conv, g_ab_in, g_ab_out, g_c_in, g_c_out, g_w1, g_w2 = gathered
    meta_full = g_meta.transpose(1, 0, 2).reshape(N_META, D_MODEL)
    conv_full = g_conv.transpose(1, 0, 2).reshape(CONV_K, conv_w_all)
    ab_full = g_ab_in.transpose(1, 0, 2).reshape(D_MODEL, AB_IN)
    ba0 = AB_Z + 512
    w_ab = jnp.concatenate([ab_full[:, :ba0], ab_full[:, ba0 + 2 * GDN_HEADS:], ab_full[:, ba0:ba0 + 2 * GDN_HEADS],
                            jnp.zeros((D_MODEL, AB_CAT - AB_IN), BF16)], axis=1)
    vec128 = lambda p: jnp.zeros((1, HEAD_W), F32).at[0, :GDN_HEADS].set(p.reshape(GDN_HEADS))
    wts = dict(
        w_ab=w_ab, conv_w=conv_full, alog_v=vec128(w["ab_a_log"]), dtb_v=vec128(w["ab_dt_bias"]),
        ab_gn=w["ab_gnorm_g"][0], w_out0=g_ab_out.reshape(D_MODEL, D_MODEL), w_c=g_c_in,
        lb=_lower_bound(w["c_lb_raw"]), c_gn=w["c_gnorm_g"][0], w_out1=g_c_out.reshape(D_MODEL, D_MODEL),
        w1=[g_w1[:, l] for l in range(DEPTH)], w2=[g_w2[:, l].reshape(D_FF, D_MODEL) for l in range(DEPTH)],
        ln_mix_g=w["ln_mix_g"], ln_mix_b=w["ln_mix_b"], ln_ffn_g=w["ln_ffn_g"], ln_ffn_b=w["ln_ffn_b"])

    h0 = jnp.concatenate([jnp.zeros((pad, D_MODEL), F32), meta_full, x[0]], axis=0)
    loss_vec, dh0, g = _local_step(h0, loss_target[0], pad, wts)
    loss = lax.psum(jnp.sum(loss_vec), ("x", "y", "c"))
    grad_x = dh0[lp - seq:][None]

    _, lb_vjp = jax.vjp(_lower_bound, w["c_lb_raw"])
    rep_part = _pack_replicated(dict(
        ln_mix_g=g["ln_mix_g"], ln_mix_b=g["ln_mix_b"], ln_ffn_g=g["ln_ffn_g"], ln_ffn_b=g["ln_ffn_b"],
        c_lb_raw=lb_vjp(g["lb"])[0], ab_gnorm_g=g["ab_gn"], c_gnorm_g=g["c_gn"],
        ab_a_log=g["alog_v"][0, :GDN_HEADS], ab_dt_bias=g["dtb_v"][0, :GDN_HEADS]))
    small = jnp.concatenate([rep_part, dh0[pad:pad + N_META], g["conv_w"].reshape(-1, D_MODEL),
                             jnp.zeros((_SMALL_N - _SMALL_CONV - CONV_K * conv_w_all // D_MODEL, D_MODEL), F32)], axis=0)
    (small_all,) = _gather([small], [F32], name="gather_small_grads")
    rep_out = _adamw(_pack_replicated(w), small_all[:, :_PACK_N], _pack_replicated(mom), _pack_replicated(var),
                     name="adamw_replicated")
    meta_parts = lax.dynamic_slice_in_dim(small_all[:, _SMALL_META:_SMALL_META + N_META], me * meta_w, meta_w, axis=2)
    meta_out = _adamw(w["meta_tokens"], meta_parts, mom["meta_tokens"], var["meta_tokens"], name="adamw_meta")
    conv_parts = small_all[:, _SMALL_CONV:_SMALL_CONV + CONV_K * conv_w_all // D_MODEL].reshape(N_DEV, CONV_K, conv_w_all)
    conv_parts = lax.dynamic_slice_in_dim(conv_parts, me * conv_w_mine, conv_w_mine, axis=2)
    conv_out = _adamw(w["ab_conv_w"][0], conv_parts, mom["ab_conv_w"][0], var["ab_conv_w"][0], name="adamw_conv")

    gab = g["w_ab"]
    gab = jnp.concatenate([gab[:, :ba0], gab[:, AB_BA:AB_BA + 2 * GDN_HEADS], gab[:, ba0:AB_BA]], axis=1)
    big = [("ab_w_in", None, gab.reshape(D_MODEL, N_DEV, AB_IN // N_DEV).transpose(1, 0, 2)),
           ("ab_w_out", None, g["ab_w_out"].reshape(N_DEV, D_MODEL // N_DEV, D_MODEL)),
           ("c_w_in", None, g["c_w_in"]),
           ("c_w_out", None, g["c_w_out"].reshape(N_DEV, D_MODEL // N_DEV, D_MODEL))]
    for l in range(DEPTH):
        big.append(("mlp_w1", l, g["w1"][l]))
    for l in range(DEPTH):
        big.append(("mlp_w2", l, g["w2"][l].reshape(N_DEV, D_FF // N_DEV, D_MODEL)))
    from_sibling = _to_sibling([b[2] for b in big], name="scatter_grads_d2d")
    core = lax.axis_index("c").astype(jnp.int32).reshape(1)
    chip_sums = [_pair_sum(b[2], s, core, name=f"chip_sum_{b[0]}" + ("" if b[1] is None else str(b[1])))
                 for b, s in zip(big, from_sibling)]
    parts = _to_chips(chip_sums, name="scatter_grads_ici")
    big_out = {}
    for (name, l, _), p in zip(big, parts):
        sel = (lambda a: a[0]) if l is None else (lambda a, l=l: a[l])
        res = _adamw(sel(w[name]), p, sel(mom[name]), sel(var[name]), name=f"adamw_{name}" + ("" if l is None else str(l)))
        big_out.setdefault(name, []).append(res)

    rep = [_unpack_replicated(r, w) for r in rep_out]
    outs = {}
    for name in _WEIGHTS:
        if name == "meta_tokens":
            outs[name] = list(meta_out)
        elif name == "ab_conv_w":
            outs[name] = [o[None] for o in conv_out]
        elif name in big_out:
            res = big_out[name]
            outs[name] = [o[None] for o in res[0]] if len(res) == 1 else [jnp.stack(pair) for pair in zip(*res)]
        else:
            outs[name] = [r[name] for r in rep]
    flat = [loss, grad_x]
    for kind in range(4):
        flat += [outs[name][kind] for name in _WEIGHTS]
    return tuple(flat)
```

```python
import functools
import math

import jax
import jax.numpy as jnp
from jax import lax
from jax.experimental import pallas as pl
from jax.experimental.pallas import tpu as pltpu

F32 = jnp.float32
BF16 = jnp.bfloat16
HI = lax.Precision.HIGHEST

N_DEV = 8
D_MODEL = 1024
N_META = 16
D_FF = 4096
DEPTH = 2
GDN_HEADS = 4
SB_HEADS = 8
SB_DH = 64
HG_HEADS = 8
HEAD_W = 128
CHUNK = 64
SB_BLOCK = 128
CONV_K = 4
DN_ALPHA = float((2 * DEPTH) ** 0.25)
LN_EPS = 1e-5
RMS_EPS = 1e-6
L2_EPS = 1e-6
ADAM_LR, ADAM_B1, ADAM_B2, ADAM_EPS, ADAM_WD, ADAM_STEP = 0.001, 0.9, 0.999, 1e-08, 0.01, 10

AB_QKV = 0
AB_Z = 1536
AB_SB = 2048
AB_BA = 3584
AB_CAT = 3840
AB_IN = 3592

VMEM_LIMIT = 56 * 1024 * 1024


def _cp(sem=None, **kw):
    if sem is not None:
        kw["dimension_semantics"] = sem
    return pltpu.CompilerParams(vmem_limit_bytes=VMEM_LIMIT, **kw)


def _row_tile(n, want):
    best = 8
    for t in range(8, min(n, want) + 1, 8):
        if n % t == 0:
            best = t
    return best


@jax.custom_vjp
def _sigmoid(x):
    e = jnp.exp(-jnp.abs(x))
    r = 1.0 / (1.0 + e)
    return jnp.where(x >= 0, r, e * r)


def _sigmoid_fwd(x):
    s = _sigmoid(x)
    return s, s


def _sigmoid_bwd(s, g):
    return (g * s * (1.0 - s),)


_sigmoid.defvjp(_sigmoid_fwd, _sigmoid_bwd)


def _log1p_exp_neg_abs(x):
    e = jnp.exp(-jnp.abs(x))
    return jnp.where(e < 1e-4, e - 0.5 * e * e, jnp.log(1.0 + e))


@jax.custom_vjp
def _softplus(x):
    return jnp.maximum(x, 0.0) + _log1p_exp_neg_abs(x)


def _softplus_fwd(x):
    return _softplus(x), x


def _softplus_bwd(x, g):
    return (g * _sigmoid(x),)


_softplus.defvjp(_softplus_fwd, _softplus_bwd)


def _silu(x):
    return x * _sigmoid(x)


def _silu_grad(x):
    s = _sigmoid(x)
    return s * (1.0 + x * (1.0 - s))


def _dot(a, b, dims, precision=None):
    return lax.dot_general(a, b, (dims, ((), ())), precision=precision, preferred_element_type=F32)


NN = ((1,), (0,))
NT = ((1,), (1,))
TN = ((0,), (0,))


def _bdot(a, b, dims):
    return _dot(a.astype(BF16), b.astype(BF16), dims)


def _mm(a, b, mode, *, tm, tn, tk, name, a_fn=None, epi=None, c=None, scale=1.0, b_dev=False, out_dev=False,
        out_dtype=F32):
    if mode == "NN":
        m, kk = a.shape
        n = b.shape[2] * N_DEV if b_dev else b.shape[1]
    elif mode == "NT":
        m, kk = a.shape
        n = b.shape[1] if b_dev else b.shape[0]
    else:
        kk, m = a.shape
        n = b.shape[1]
    assert m % tm == 0 and n % tn == 0 and kk % tk == 0, (name, m, n, kk, tm, tn, tk)
    nk = kk // tk
    dims = {"NN": NN, "NT": NT, "TN": TN}[mode]

    if mode == "TN":
        a_spec = pl.BlockSpec((tk, tm), lambda i, j, k: (k, i))
    else:
        a_spec = pl.BlockSpec((tm, tk), lambda i, j, k: (i, k))
    if mode == "NN":
        if b_dev:
            assert tn == b.shape[2]
            b_spec = pl.BlockSpec((None, tk, tn), lambda i, j, k: (j, k, 0))
        else:
            b_spec = pl.BlockSpec((tk, tn), lambda i, j, k: (k, j))
    elif mode == "NT":
        if b_dev:
            assert tk == b.shape[2]
            b_spec = pl.BlockSpec((None, tn, tk), lambda i, j, k: (k, j, 0))
        else:
            b_spec = pl.BlockSpec((tn, tk), lambda i, j, k: (j, k))
    else:
        b_spec = pl.BlockSpec((tk, tn), lambda i, j, k: (k, j))
    in_specs = [a_spec, b_spec]
    operands = [a, b]
    if epi is not None:
        in_specs.append(pl.BlockSpec((tm, tn), lambda i, j, k: (i, j)))
        operands.append(c)
    if out_dev:
        assert tn == n // N_DEV
        out_shape = jax.ShapeDtypeStruct((N_DEV, m, tn), out_dtype)
        out_spec = pl.BlockSpec((None, tm, tn), lambda i, j, k: (j, i, 0))
    else:
        out_shape = jax.ShapeDtypeStruct((m, n), out_dtype)
        out_spec = pl.BlockSpec((tm, tn), lambda i, j, k: (i, j))

    def body(*refs):
        a_ref, b_ref = refs[0], refs[1]
        c_ref = refs[2] if epi is not None else None
        o_ref = refs[3] if epi is not None else refs[2]
        acc_ref = refs[-1] if nk > 1 else None
        av = a_ref[...]
        if a_fn == "relu2":
            av = jnp.square(jnp.maximum(av, 0.0))
        p = _dot(av.astype(BF16), b_ref[...].astype(BF16), dims)

        def finish(acc):
            if epi == "add":
                acc = acc + scale * c_ref[...]
            elif epi == "relu2grad":
                acc = acc * (2.0 * jnp.maximum(c_ref[...], 0.0))
            o_ref[...] = acc.astype(out_dtype)

        if nk == 1:
            finish(p)
        else:
            k = pl.program_id(2)

            @pl.when(k == 0)
            def _():
                acc_ref[...] = p

            @pl.when(k > 0)
            def _():
                acc_ref[...] += p

            @pl.when(k == nk - 1)
            def _():
                finish(acc_ref[...])

    return pl.pallas_call(
        body, name=name, grid=(m // tm, n // tn, nk), in_specs=in_specs, out_specs=out_spec, out_shape=out_shape,
        scratch_shapes=[pltpu.VMEM((tm, tn), F32)] if nk > 1 else [],
        compiler_params=_cp(("parallel", "parallel", "arbitrary")),
    )(*operands)


def _ln_fwd(a, b, g, beta, *, name):
    lp, d = a.shape
    tm = _row_tile(lp, 512)

    def body(a_ref, b_ref, g_ref, be_ref, y_ref, yb_ref):
        pre = DN_ALPHA * a_ref[...] + b_ref[...]
        mu = jnp.mean(pre, axis=-1, keepdims=True)
        xc = pre - mu
        var = jnp.mean(xc * xc, axis=-1, keepdims=True)
        y = xc * lax.rsqrt(var + LN_EPS) * g_ref[...] + be_ref[...]
        y_ref[...] = y
        yb_ref[...] = y.astype(BF16)

    row = pl.BlockSpec((tm, d), lambda i: (i, 0))
    vec = pl.BlockSpec((1, d), lambda i: (0, 0))
    return pl.pallas_call(
        body, name=name, grid=(lp // tm,), in_specs=[row, row, vec, vec], out_specs=[row, row],
        out_shape=[jax.ShapeDtypeStruct((lp, d), F32), jax.ShapeDtypeStruct((lp, d), BF16)],
        compiler_params=_cp(("parallel",)),
    )(a, b, g.reshape(1, d), beta.reshape(1, d))


def _ln_bwd(a, b, g, dy, *, name):
    lp, d = a.shape
    tm = _row_tile(lp, 512)

    def body(a_ref, b_ref, g_ref, dy_ref, dpre_ref, dpreb_ref, dg_ref, db_ref):
        pre = DN_ALPHA * a_ref[...] + b_ref[...]
        mu = jnp.mean(pre, axis=-1, keepdims=True)
        xc = pre - mu
        var = jnp.mean(xc * xc, axis=-1, keepdims=True)
        rstd = lax.rsqrt(var + LN_EPS)
        xhat = xc * rstd
        dyv = dy_ref[...]
        dxh = dyv * g_ref[...]
        m1 = jnp.mean(dxh, axis=-1, keepdims=True)
        m2 = jnp.mean(dxh * xhat, axis=-1, keepdims=True)
        dpre = rstd * (dxh - m1 - xhat * m2)
        dpre_ref[...] = dpre
        dpreb_ref[...] = dpre.astype(BF16)

        @pl.when(pl.program_id(0) == 0)
        def _():
            dg_ref[...] = jnp.zeros_like(dg_ref)
            db_ref[...] = jnp.zeros_like(db_ref)

        dg_ref[...] += jnp.sum(dyv * xhat, axis=0, keepdims=True)
        db_ref[...] += jnp.sum(dyv, axis=0, keepdims=True)

    row = pl.BlockSpec((tm, d), lambda i: (i, 0))
    vec = pl.BlockSpec((1, d), lambda i: (0, 0))
    return pl.pallas_call(
        body, name=name, grid=(lp // tm,), in_specs=[row, row, vec, row], out_specs=[row, row, vec, vec],
        out_shape=[jax.ShapeDtypeStruct((lp, d), F32), jax.ShapeDtypeStruct((lp, d), BF16),
                   jax.ShapeDtypeStruct((1, d), F32), jax.ShapeDtypeStruct((1, d), F32)],
        compiler_params=_cp(("arbitrary",)),
    )(a, b, g.reshape(1, d), dy)


def _loss_head(y, target, *, name):
    lp, d = y.shape
    seq = target.shape[0]
    tm = SB_BLOCK
    first = (lp - seq) // tm
    assert (lp - seq) % tm == 0 and seq % tm == 0

    def body(y_ref, t_ref, dy_ref, loss_ref):
        i = pl.program_id(0)
        live = i >= first
        diff = jnp.where(live, y_ref[...] - t_ref[...], 0.0)
        dy_ref[...] = diff * (1.0 / d)

        @pl.when(i == 0)
        def _():
            loss_ref[...] = jnp.zeros_like(loss_ref)

        loss_ref[...] += jnp.sum(diff * diff, axis=0, keepdims=True) * (0.5 / d)

    return pl.pallas_call(
        body, name=name, grid=(lp // tm,),
        in_specs=[pl.BlockSpec((tm, d), lambda i: (i, 0)),
                  pl.BlockSpec((tm, d), lambda i: (jnp.maximum(i - first, 0), 0))],
        out_specs=[pl.BlockSpec((tm, d), lambda i: (i, 0)), pl.BlockSpec((1, d), lambda i: (0, 0))],
        out_shape=[jax.ShapeDtypeStruct((lp, d), F32), jax.ShapeDtypeStruct((1, d), F32)],
        compiler_params=_cp(("arbitrary",)),
    )(y, target)


def _gate_fwd(o, zsrc, z_blk0, g, other, *, heads, name):
    lp = o.shape[0]
    tm = _row_tile(lp, 512)
    nblk = D_MODEL // HEAD_W

    def body(o_ref, z_ref, g_ref, x_ref, y_ref):
        h = pl.program_id(1)

        @pl.when(h < heads)
        def _():
            ov = o_ref[...]
            r = lax.rsqrt(jnp.mean(ov * ov, axis=-1, keepdims=True) + RMS_EPS)
            y_ref[...] = (ov * r * g_ref[...] * _silu(z_ref[...])).astype(BF16)

        @pl.when(h >= heads)
        def _():
            y_ref[...] = x_ref[...].astype(BF16)

    other_w = other.shape[1] // HEAD_W
    return pl.pallas_call(
        body, name=name, grid=(lp // tm, nblk),
        in_specs=[pl.BlockSpec((tm, HEAD_W), lambda i, h: (i, jnp.minimum(h, heads - 1))),
                  pl.BlockSpec((tm, HEAD_W), lambda i, h: (i, z_blk0 + jnp.minimum(h, heads - 1))),
                  pl.BlockSpec((1, HEAD_W), lambda i, h: (0, 0)),
                  pl.BlockSpec((tm, HEAD_W), lambda i, h: (i, jnp.clip(h - heads, 0, other_w - 1)))],
        out_specs=pl.BlockSpec((tm, HEAD_W), lambda i, h: (i, h)),
        out_shape=jax.ShapeDtypeStruct((lp, D_MODEL), BF16),
        compiler_params=_cp(("parallel", "arbitrary")),
    )(o, zsrc, g.reshape(1, HEAD_W), other)


def _gate_bwd(o, zsrc, z_blk0, g, dy, *, heads, name):
    lp = o.shape[0]
    tm = _row_tile(lp, 512)

    def body(o_ref, z_ref, g_ref, dy_ref, do_ref, dz_ref, dg_ref):
        ov, zv, gv, dyv = o_ref[...], z_ref[...], g_ref[...], dy_ref[...]
        r = lax.rsqrt(jnp.mean(ov * ov, axis=-1, keepdims=True) + RMS_EPS)
        nrm = ov * r
        s = _silu(zv)
        dn = dyv * gv * s
        do_ref[...] = r * (dn - nrm * jnp.mean(dn * nrm, axis=-1, keepdims=True))
        dz_ref[...] = dyv * nrm * gv * _silu_grad(zv)

        @pl.when((pl.program_id(0) == 0) & (pl.program_id(1) == 0))
        def _():
            dg_ref[...] = jnp.zeros_like(dg_ref)

        dg_ref[...] += jnp.sum(dyv * nrm * s, axis=0, keepdims=True)

    blk = pl.BlockSpec((tm, HEAD_W), lambda i, h: (i, h))
    return pl.pallas_call(
        body, name=name, grid=(lp // tm, heads),
        in_specs=[blk, pl.BlockSpec((tm, HEAD_W), lambda i, h: (i, z_blk0 + h)),
                  pl.BlockSpec((1, HEAD_W), lambda i, h: (0, 0)), blk],
        out_specs=[blk, blk, pl.BlockSpec((1, HEAD_W), lambda i, h: (0, 0))],
        out_shape=[jax.ShapeDtypeStruct((lp, heads * HEAD_W), F32), jax.ShapeDtypeStruct((lp, heads * HEAD_W), F32),
                   jax.ShapeDtypeStruct((1, HEAD_W), F32)],
        compiler_params=_cp(("arbitrary", "arbitrary")),
    )(o, zsrc, g.reshape(1, HEAD_W), dy)


def _conv_taps(x, w):
    acc = w[CONV_K - 1:CONV_K, :] * x
    for k in range(CONV_K - 1):
        acc = acc + w[k:k + 1, :] * pltpu.roll(x, CONV_K - 1 - k, 0)
    return acc


def _gdn_pre_fwd(p0, conv_w, pad, *, name):
    lp = p0.shape[0]
    nq = GDN_HEADS
    qscale = HEAD_W ** -0.5

    def body(x_ref, w_ref, y_ref):
        j = pl.program_id(0)
        c = _conv_taps(x_ref[...], w_ref[...])
        s = _silu(c)
        r = lax.rsqrt(jnp.sum(s * s, axis=-1, keepdims=True) + L2_EPS)
        mult = jnp.where(j < nq, r * qscale, jnp.where(j < 2 * nq, r, 1.0))
        rows = lax.broadcasted_iota(jnp.int32, (lp, 1), 0)
        y_ref[...] = jnp.where(rows >= pad, s * mult, 0.0)

    return pl.pallas_call(
        body, name=name, grid=(3 * nq,),
        in_specs=[pl.BlockSpec((lp, HEAD_W), lambda j: (0, j)), pl.BlockSpec((CONV_K, HEAD_W), lambda j: (0, j))],
        out_specs=pl.BlockSpec((lp, HEAD_W), lambda j: (0, j)),
        out_shape=jax.ShapeDtypeStruct((lp, 3 * nq * HEAD_W), F32), compiler_params=_cp(("parallel",)),
    )(p0, conv_w)


def _gdn_pre_bwd(p0, conv_w, dqkv, pad, *, name):
    lp = p0.shape[0]
    nq = GDN_HEADS
    qscale = HEAD_W ** -0.5

    def body(x_ref, w_ref, dy_ref, dx_ref, dw_ref):
        j = pl.program_id(0)
        x, w = x_ref[...], w_ref[...]
        c = _conv_taps(x, w)
        s = _silu(c)
        r = lax.rsqrt(jnp.sum(s * s, axis=-1, keepdims=True) + L2_EPS)
        rows = lax.broadcasted_iota(jnp.int32, (lp, 1), 0)
        dy = jnp.where(rows >= pad, dy_ref[...], 0.0)
        nrm = s * r
        dn = dy * jnp.where(j < nq, qscale, 1.0)
        ds_norm = r * (dn - nrm * jnp.sum(nrm * dn, axis=-1, keepdims=True))
        ds = jnp.where(j < 2 * nq, ds_norm, dy)
        dc = ds * _silu_grad(c)
        dx = w[CONV_K - 1:CONV_K, :] * dc
        dws = [None] * CONV_K
        dws[CONV_K - 1] = jnp.sum(dc * x, axis=0, keepdims=True)
        for k in range(CONV_K - 1):
            sh = CONV_K - 1 - k
            dx = dx + w[k:k + 1, :] * pltpu.roll(dc, lp - sh, 0)
            dws[k] = jnp.sum(dc * pltpu.roll(x, sh, 0), axis=0, keepdims=True)
        dx_ref[...] = dx
        dw_ref[...] = jnp.concatenate(dws, axis=0)

    blk = pl.BlockSpec((lp, HEAD_W), lambda j: (0, j))
    wblk = pl.BlockSpec((CONV_K, HEAD_W), lambda j: (0, j))
    return pl.pallas_call(
        body, name=name, grid=(3 * nq,), in_specs=[blk, wblk, blk], out_specs=[blk, wblk],
        out_shape=[jax.ShapeDtypeStruct((lp, 3 * nq * HEAD_W), F32),
                   jax.ShapeDtypeStruct((CONV_K, 3 * nq * HEAD_W), F32)],
        compiler_params=_cp(("parallel",)),
    )(p0, conv_w, dqkv)


def _tri(c, strict):
    r = lax.broadcasted_iota(jnp.int32, (c, c), 0)
    q = lax.broadcasted_iota(jnp.int32, (c, c), 1)
    return (q < r) if strict else (q <= r)


@jax.custom_vjp
def _inv_unit_lower(m):
    c = m.shape[0]
    eye = (lax.broadcasted_iota(jnp.int32, (c, c), 0) == lax.broadcasted_iota(jnp.int32, (c, c), 1)).astype(F32)
    x = eye - m
    p = m
    n = 2
    while n < CHUNK:
        p = _dot3x(p, p, NN)
        x = x + _dot3x(x, p, NN)
        n *= 2
    return x


def _inv_fwd(m):
    t = _inv_unit_lower(m)
    return t, t


def _inv_bwd(t, g):
    return (-_dot3x(_dot3x(t, g, TN), t, NT),)


_inv_unit_lower.defvjp(_inv_fwd, _inv_bwd)


def _heads_to_rows(x, nh):
    return jnp.concatenate([x[:, h * HEAD_W:(h + 1) * HEAD_W] for h in range(nh)], axis=0)


def _rows_to_heads(x, nh):
    c = x.shape[0] // nh
    return jnp.concatenate([x[h * c:(h + 1) * c] for h in range(nh)], axis=1)


def _gdn_chunk(q, k, v, ba, alog, dtb, states, valid):
    nh = GDN_HEADS
    c = q.shape[0]
    r = nh * c
    lane = lax.broadcasted_iota(jnp.int32, (1, HEAD_W), 1)
    pick = lambda x, l: jnp.sum(jnp.where(lane == l, x, 0.0), axis=-1, keepdims=True)
    beta = jnp.concatenate([jnp.where(valid, _sigmoid(pick(ba, h)), 0.0) for h in range(nh)], axis=0)
    g = jnp.concatenate(
        [jnp.where(valid, -jnp.exp(pick(alog, h)) * _softplus(pick(ba, nh + h) + pick(dtb, h)), 0.0) for h in range(nh)],
        axis=0)
    qs, ks, vs = _heads_to_rows(q, nh), _heads_to_rows(k, nh), _heads_to_rows(v, nh)
    rr = lax.broadcasted_iota(jnp.int32, (r, r), 0)
    cc = lax.broadcasted_iota(jnp.int32, (r, r), 1)
    same = (rr // c) == (cc // c)
    causal, strict = same & (cc <= rr), same & (cc < rr)
    lower = jnp.where(causal, 1.0, 0.0).astype(BF16)
    upper = jnp.where(same & (cc >= rr), 1.0, 0.0).astype(BF16)
    gcb = _mask_mm(lower, upper, g * jnp.ones((1, HEAD_W), F32))
    gc_col = jnp.concatenate([gcb] * (r // HEAD_W), axis=1)
    decay = jnp.where(causal, jnp.exp(jnp.minimum(gc_col - gc_col.T, 0.0)), 0.0)
    egc = jnp.exp(gcb)
    kb = ks * beta
    m = jnp.where(strict, _bdot(kb, ks, NT) * decay, 0.0)
    t = _inv_unit_lower(m)
    u = _bdot(t, vs * beta, NN)
    w = _bdot(t, kb * egc, NN)
    a = _bdot(qs, ks, NT) * decay
    rows = lambda x, h: x[h * c:(h + 1) * c]
    qe = qs * egc
    v_new = u - jnp.concatenate([_bdot(rows(w, h), states[h], NN) for h in range(nh)], axis=0)
    o = jnp.concatenate([_bdot(rows(qe, h), states[h], NN) for h in range(nh)], axis=0) + _bdot(a, v_new, NN)
    new_states = []
    for h in range(nh):
        gl = gcb[(h + 1) * c - 1:(h + 1) * c, :]
        k_dec = rows(ks, h) * jnp.exp(gl - rows(gcb, h))
        new_states.append(states[h] * jnp.exp(gl) + _bdot(k_dec, rows(v_new, h), TN))
    return _rows_to_heads(o, nh), new_states


def _gdn_fwd(qkv, p0, alog_v, dtb_v, pad, *, name):
    lp = qkv.shape[0]
    n = lp // CHUNK
    nh = GDN_HEADS

    def body(q_ref, k_ref, v_ref, ba_ref, al_ref, dt_ref, o_ref, st_ref, s_ref):
        i = pl.program_id(0)

        @pl.when(i == 0)
        def _():
            s_ref[...] = jnp.zeros_like(s_ref)

        valid = (i * CHUNK + lax.broadcasted_iota(jnp.int32, (CHUNK, 1), 0)) >= pad
        s = s_ref[...]
        o, s2 = _gdn_chunk(q_ref[...], k_ref[...], v_ref[...], ba_ref[...], al_ref[...], dt_ref[...],
                           [s[h] for h in range(nh)], valid)
        st_ref[...] = s
        o_ref[...] = o
        for h in range(nh):
            s_ref[h] = s2[h]

    w = nh * HEAD_W
    vec = pl.BlockSpec((1, HEAD_W), lambda i: (0, 0))
    return pl.pallas_call(
        body, name=name, grid=(n,),
        in_specs=[pl.BlockSpec((CHUNK, w), lambda i: (i, 0)), pl.BlockSpec((CHUNK, w), lambda i: (i, 1)),
                  pl.BlockSpec((CHUNK, w), lambda i: (i, 2)), pl.BlockSpec((CHUNK, HEAD_W), lambda i: (i, AB_BA // HEAD_W)),
                  vec, vec],
        out_specs=[pl.BlockSpec((CHUNK, w), lambda i: (i, 0)),
                   pl.BlockSpec((None, nh, HEAD_W, HEAD_W), lambda i: (i, 0, 0, 0))],
        out_shape=[jax.ShapeDtypeStruct((lp, w), F32), jax.ShapeDtypeStruct((n, nh, HEAD_W, HEAD_W), F32)],
        scratch_shapes=[pltpu.VMEM((nh, HEAD_W, HEAD_W), F32)],
        compiler_params=_cp(("arbitrary",)),
    )(qkv, qkv, qkv, p0, alog_v, dtb_v)


def _gdn_bwd(qkv, p0, alog_v, dtb_v, states, do, pad, *, name):
    lp = qkv.shape[0]
    n = lp // CHUNK
    nh = GDN_HEADS

    def body(q_ref, k_ref, v_ref, ba_ref, al_ref, dt_ref, st_ref, do_ref,
             dq_ref, dk_ref, dv_ref, dba_ref, dal_ref, ddt_ref, ds_ref):
        step = pl.program_id(0)
        i = n - 1 - step

        @pl.when(step == 0)
        def _():
            ds_ref[...] = jnp.zeros_like(ds_ref)
            dal_ref[...] = jnp.zeros_like(dal_ref)
            ddt_ref[...] = jnp.zeros_like(ddt_ref)

        valid = (i * CHUNK + lax.broadcasted_iota(jnp.int32, (CHUNK, 1), 0)) >= pad
        st, dst = st_ref[...], ds_ref[...]
        fn = functools.partial(_gdn_chunk, valid=valid)
        _, vjp = jax.vjp(fn, q_ref[...], k_ref[...], v_ref[...], ba_ref[...], al_ref[...], dt_ref[...],
                         [st[h] for h in range(nh)])
        dq, dk, dv, dba, dal, ddt, ds = vjp((do_ref[...], [dst[h] for h in range(nh)]))
        dq_ref[...] = dq
        dk_ref[...] = dk
        dv_ref[...] = dv
        dba_ref[...] = dba
        dal_ref[...] += dal
        ddt_ref[...] += ddt
        for h in range(nh):
            ds_ref[h] = ds[h]

    w = nh * HEAD_W
    rev = lambda c: (lambda s: (n - 1 - s, c))
    vec = pl.BlockSpec((1, HEAD_W), lambda s: (0, 0))
    dq, dk, dv, dba, dal, ddt = pl.pallas_call(
        body, name=name, grid=(n,),
        in_specs=[pl.BlockSpec((CHUNK, w), rev(0)), pl.BlockSpec((CHUNK, w), rev(1)), pl.BlockSpec((CHUNK, w), rev(2)),
                  pl.BlockSpec((CHUNK, HEAD_W), rev(AB_BA // HEAD_W)), vec, vec,
                  pl.BlockSpec((None, nh, HEAD_W, HEAD_W), lambda s: (n - 1 - s, 0, 0, 0)),
                  pl.BlockSpec((CHUNK, w), rev(0))],
        out_specs=[pl.BlockSpec((CHUNK, w), rev(0)), pl.BlockSpec((CHUNK, w), rev(0)), pl.BlockSpec((CHUNK, w), rev(0)),
                   pl.BlockSpec((CHUNK, HEAD_W), rev(0)), vec, vec],
        out_shape=[jax.ShapeDtypeStruct((lp, w), F32)] * 3 + [jax.ShapeDtypeStruct((lp, HEAD_W), F32)]
        + [jax.ShapeDtypeStruct((1, HEAD_W), F32)] * 2,
        scratch_shapes=[pltpu.VMEM((nh, HEAD_W, HEAD_W), F32)],
        compiler_params=_cp(("arbitrary",)),
    )(qkv, qkv, qkv, p0, alog_v, dtb_v, states, do)
    return dq, dk, dv, dba, dal, ddt


HG_LEVELS = (32, 16, 8, 4, 2, 1)
HG_GROUP = 4


def _hg_masks():
    import numpy as np
    c = CHUNK
    t = np.arange(c)[:, None]
    j = np.arange(c)[None, :]
    sums = [j <= t, j > t]
    pairs = [j == t]
    for m in HG_LEVELS:
        p = (t // (2 * m)) * (2 * m)
        r = p + m
        upper = t >= r
        sums.append(upper & (j > r) & (j <= t))
        sums.append(~upper & (j > t) & (j <= r))
        pairs.append(upper & (j < r) & (j >= p))
    sums = np.concatenate(sums, axis=0).astype(np.float32)
    pairs = np.concatenate([np.kron(np.eye(HG_GROUP), p) for p in pairs], axis=0).astype(np.float32)
    return jnp.asarray(sums, BF16), jnp.asarray(sums.T, BF16), jnp.asarray(pairs, F32)


def _split3(x):
    hi = x.astype(BF16)
    r1 = x - hi.astype(F32)
    mid = r1.astype(BF16)
    return hi, mid, (r1 - mid.astype(F32)).astype(BF16)


def _dot3x(a, b, dims):
    ah, am, _ = _split3(a)
    bh, bm, _ = _split3(b)
    return _dot(ah, bh, dims) + (_dot(ah, bm, dims) + _dot(am, bh, dims))


def _mask_mm_raw(m, x):
    return sum(_dot(m, part, NN) for part in _split3(x))


@jax.custom_vjp
def _mask_mm(m, mt, x):
    return _mask_mm_raw(m, x)


def _mask_mm_fwd(m, mt, x):
    return _mask_mm_raw(m, x), (m, mt)


def _mask_mm_bwd(res, g):
    m, mt = res
    return jnp.zeros_like(m), jnp.zeros_like(mt), _mask_mm_raw(mt, g)


_mask_mm.defvjp(_mask_mm_fwd, _mask_mm_bwd)


def _hg_chunk(qr, fr, ir, lb, states, valid, sums, sums_t, pairs):
    nh = HG_GROUP
    c = qr.shape[0]
    r = nh * c
    fg = lb + (1.0 - lb) * _sigmoid(fr)
    logf = jnp.where(valid, jnp.log(fg), 0.0)
    k = jnp.where(valid, 1.0 - fg, 0.0)
    qs = jnp.where(valid, _silu(qr), 0.0)
    v = jnp.where(valid, ir, 0.0)
    e = jnp.exp(_mask_mm(sums, sums_t, logf))
    blk = lambda n: e[n * c:(n + 1) * c]
    mask = lambda n: pairs[n * r:(n + 1) * r]
    stack = lambda x: _heads_to_rows(x, nh)
    a = mask(0) * _bdot(stack(qs), stack(k), NT)
    for lvl in range(len(HG_LEVELS)):
        a = a + mask(1 + lvl) * _bdot(stack(qs * blk(2 + 2 * lvl)), stack(k * blk(3 + 2 * lvl)), NT)
    av = _bdot(a, stack(v), NN)
    eb = blk(0)
    qe, kd = qs * eb, k * blk(1)
    outs, new_states = [], []
    for h in range(nh):
        cs = slice(h * HEAD_W, (h + 1) * HEAD_W)
        outs.append(_bdot(qe[:, cs], states[h], NT) + av[h * c:(h + 1) * c])
        new_states.append(states[h] * eb[c - 1:c, cs] + _bdot(v[:, cs], kd[:, cs], TN))
    return jnp.concatenate(outs, axis=1), new_states


def _hg_fwd(p1, lb, pad, *, name):
    lp = p1.shape[0]
    n = lp // CHUNK
    nh = HG_HEADS

    def body(q_ref, f_ref, i_ref, lb_ref, sums_ref, sums_t_ref, pairs_ref, o_ref, st_ref, s_ref):
        i = pl.program_id(1)

        @pl.when(i == 0)
        def _():
            s_ref[...] = jnp.zeros_like(s_ref)

        valid = (i * CHUNK + lax.broadcasted_iota(jnp.int32, (CHUNK, 1), 0)) >= pad
        s = s_ref[...]
        o, s2 = _hg_chunk(q_ref[...], f_ref[...], i_ref[...], lb_ref[...], [s[h] for h in range(grp)], valid,
                          sums_ref[...], sums_t_ref[...], pairs_ref[...])
        st_ref[...] = s
        o_ref[...] = o
        for h in range(grp):
            s_ref[h] = s2[h]

    masks = _hg_masks()
    grp, ngrp, gw = HG_GROUP, nh // HG_GROUP, HG_GROUP * HEAD_W
    blk = lambda off: pl.BlockSpec((CHUNK, gw), lambda h, i: (i, off + h))
    const = lambda a: pl.BlockSpec(a.shape, lambda h, i: (0, 0))
    return pl.pallas_call(
        body, name=name, grid=(ngrp, n),
        in_specs=[blk(0), blk(ngrp), blk(2 * ngrp), pl.BlockSpec((1, gw), lambda h, i: (0, h))]
        + [const(a) for a in masks],
        out_specs=[blk(0), pl.BlockSpec((grp, None, HEAD_W, HEAD_W), lambda h, i: (h, i, 0, 0))],
        out_shape=[jax.ShapeDtypeStruct((lp, nh * HEAD_W), F32), jax.ShapeDtypeStruct((nh, n, HEAD_W, HEAD_W), F32)],
        scratch_shapes=[pltpu.VMEM((grp, HEAD_W, HEAD_W), F32)],
        compiler_params=_cp(("parallel", "arbitrary")),
    )(p1, p1, p1, lb, *masks)


def _hg_bwd(p1, lb, states, do, pad, *, name):
    lp = p1.shape[0]
    n = lp // CHUNK
    nh = HG_HEADS

    def body(q_ref, f_ref, i_ref, lb_ref, st_ref, do_ref, sums_ref, sums_t_ref, pairs_ref,
             dq_ref, df_ref, di_ref, dlb_ref, ds_ref):
        step = pl.program_id(1)
        i = n - 1 - step

        @pl.when(step == 0)
        def _():
            ds_ref[...] = jnp.zeros_like(ds_ref)
            dlb_ref[...] = jnp.zeros_like(dlb_ref)

        valid = (i * CHUNK + lax.broadcasted_iota(jnp.int32, (CHUNK, 1), 0)) >= pad
        fn = functools.partial(_hg_chunk, valid=valid, sums=sums_ref[...], sums_t=sums_t_ref[...],
                               pairs=pairs_ref[...])
        st, dst = st_ref[...], ds_ref[...]
        _, vjp = jax.vjp(fn, q_ref[...], f_ref[...], i_ref[...], lb_ref[...], [st[h] for h in range(grp)])
        dq, df, di, dlb, ds = vjp((do_ref[...], [dst[h] for h in range(grp)]))
        dq_ref[...] = dq
        df_ref[...] = df
        di_ref[...] = di
        dlb_ref[...] += dlb
        for h in range(grp):
            ds_ref[h] = ds[h]

    masks = _hg_masks()
    grp, ngrp, gw = HG_GROUP, nh // HG_GROUP, HG_GROUP * HEAD_W
    blk = lambda off: pl.BlockSpec((CHUNK, gw), lambda h, s: (n - 1 - s, off + h))
    const = lambda a: pl.BlockSpec(a.shape, lambda h, s: (0, 0))
    w = nh * HEAD_W
    return pl.pallas_call(
        body, name=name, grid=(ngrp, n),
        in_specs=[blk(0), blk(ngrp), blk(2 * ngrp), pl.BlockSpec((1, gw), lambda h, s: (0, h)),
                  pl.BlockSpec((grp, None, HEAD_W, HEAD_W), lambda h, s: (h, n - 1 - s, 0, 0)), blk(0)]
        + [const(a) for a in masks],
        out_specs=[blk(0), blk(0), blk(0), pl.BlockSpec((1, gw), lambda h, s: (0, h))],
        out_shape=[jax.ShapeDtypeStruct((lp, w), F32)] * 3 + [jax.ShapeDtypeStruct((1, w), F32)],
        scratch_shapes=[pltpu.VMEM((grp, HEAD_W, HEAD_W), F32)],
        compiler_params=_cp(("parallel", "arbitrary")),
    )(p1, p1, p1, lb, states, do, *masks)


SB_GROUP = 4


def _sb_cat(kind, first_key=0):
    r = lax.broadcasted_iota(jnp.int32, (SB_BLOCK, 2 * SB_BLOCK), 0)
    c = lax.broadcasted_iota(jnp.int32, (SB_BLOCK, 2 * SB_BLOCK), 1)
    tri = {"after": c < r, "incl": r <= c, "before": r < c}[kind]
    m = ((c >= SB_BLOCK) | tri) & (r >= first_key)
    return jnp.where(m, 1.0, 0.0).astype(BF16)


def _sb_cumsum(x, cat):
    return _dot(x.astype(BF16), cat, NN)


def _sb_logsig(z):
    e = jnp.exp(-jnp.abs(z))
    lse = jnp.where(e < 1e-4, e, jnp.log(1.0 + e))
    lsz = jnp.minimum(z, 0.0) - lse
    return lsz, lsz - z, e


def _sb_stack(x, scale=None):
    lane = lax.broadcasted_iota(jnp.int32, (1, HEAD_W), 1)
    if scale is not None:
        x = x * scale
    return jnp.concatenate([jnp.where(lane < SB_DH, x, 0.0), jnp.where(lane >= SB_DH, x, 0.0)], axis=0).astype(BF16)


def _sb_unstack(x):
    lane = lax.broadcasted_iota(jnp.int32, (1, HEAD_W), 1)
    return jnp.where(lane < SB_DH, x[:SB_BLOCK], x[SB_BLOCK:])


def _sb_fwd(p0, pad, *, name):
    lp = p0.shape[0]
    nb = lp // SB_BLOCK
    npair = SB_HEADS // 2
    blk0 = AB_SB // HEAD_W
    scale = SB_DH ** -0.5
    gw = SB_GROUP * SB_BLOCK
    assert pad < SB_BLOCK

    def body(q_ref, k_ref, v_ref, o_ref, tot_ref):
        i = pl.program_id(1)
        qs = _sb_stack(q_ref[...], scale)
        qpos = i * SB_BLOCK + lax.broadcasted_iota(jnp.int32, (SB_BLOCK, 1), 0)
        qpos = jnp.concatenate([qpos, qpos], axis=0)
        cat = _sb_cat("after")
        cat0 = _sb_cat("after", pad)
        ng = i // SB_GROUP

        def group(off, first_cat, allowed, carry):
            acc, run = carry
            kg = k_ref[pl.ds(off, gw), :].astype(BF16)
            vg = v_ref[pl.ds(off, gw), :].astype(BF16)
            lsz, l1m, _ = _sb_logsig(_dot(qs, kg, NT))
            if allowed is not None:
                l1m = jnp.where(allowed, l1m, 0.0)
            args = [None] * SB_GROUP
            for g in reversed(range(SB_GROUP)):
                sl = slice(g * SB_BLOCK, (g + 1) * SB_BLOCK)
                al = _sb_cumsum(l1m[:, sl], first_cat if g == 0 else cat)
                args[g] = lsz[:, sl] + al[:, :SB_BLOCK] + run
                run = run + al[:, SB_BLOCK:]
            wgt = jnp.exp(jnp.concatenate(args, axis=1))
            if allowed is not None:
                wgt = jnp.where(allowed, wgt, 0.0)
            return acc + _dot(wgt.astype(BF16), vg, NN), run

        def below(t, carry):
            gi = ng - 1 - t
            return group(pl.multiple_of(gi * gw, gw), jnp.where(gi == 0, cat0, cat), None, carry)

        top = ng * gw
        off = pl.multiple_of(jnp.minimum(top, lp - gw), SB_BLOCK)
        kpos = off + lax.broadcasted_iota(jnp.int32, (1, gw), 1)
        allowed = (kpos < qpos) & (kpos >= pad) & (kpos >= top)
        zero = (jnp.zeros((2 * SB_BLOCK, HEAD_W), F32), jnp.zeros((2 * SB_BLOCK, HEAD_W), F32))
        carry = group(off, cat, allowed, zero)
        acc, run = lax.fori_loop(0, ng, below, carry)
        o_ref[...] = _sb_unstack(acc)
        tot_ref[...] = _sb_unstack(run)

    full = lambda c0: pl.BlockSpec((lp, HEAD_W), lambda p, i: (0, c0 + p))
    out = pl.BlockSpec((SB_BLOCK, HEAD_W), lambda p, i: (i, p))
    return pl.pallas_call(
        body, name=name, grid=(npair, nb),
        in_specs=[pl.BlockSpec((SB_BLOCK, HEAD_W), lambda p, i: (i, blk0 + p)), full(blk0 + npair), full(blk0 + 2 * npair)],
        out_specs=[out, out],
        out_shape=[jax.ShapeDtypeStruct((lp, npair * HEAD_W), F32)] * 2,
        compiler_params=_cp(("parallel", "arbitrary")),
    )(p0, p0, p0)


def _sb_bwd(p0, tot, dsrc, d_blk0, pad, *, name):
    lp = p0.shape[0]
    nb = lp // SB_BLOCK
    npair = SB_HEADS // 2
    blk0 = AB_SB // HEAD_W
    scale = SB_DH ** -0.5
    gw = SB_GROUP * SB_BLOCK
    assert pad < SB_BLOCK

    def body(q_ref, k_ref, v_ref, tot_ref, do_ref, dq_ref, dkt_ref, dvt_ref):
        i = pl.program_id(1)

        @pl.when(i == 0)
        def _():
            dkt_ref[...] = jnp.zeros_like(dkt_ref)
            dvt_ref[...] = jnp.zeros_like(dvt_ref)

        qs = _sb_stack(q_ref[...], scale)
        dos = _sb_stack(do_ref[...])
        qst, dost = qs.T, dos.T
        totv = tot_ref[...]
        ones = jnp.ones((1, HEAD_W), F32)
        tots = jnp.concatenate([totv[:, 0:1] * ones, totv[:, SB_DH:SB_DH + 1] * ones], axis=0)
        qpos = i * SB_BLOCK + lax.broadcasted_iota(jnp.int32, (SB_BLOCK, 1), 0)
        qpos = jnp.concatenate([qpos, qpos], axis=0)
        incl, incl0 = _sb_cat("incl"), _sb_cat("incl", pad)
        before = _sb_cat("before")
        ng = i // SB_GROUP

        def dscore(z, e, ev, dl1m):
            r = 1.0 / (1.0 + e)
            sg = jnp.where(z >= 0, r, e * r)
            return ev * (1.0 - sg) - dl1m * sg

        def group(off, first_incl, allowed, carry):
            dq, prun, erun = carry
            kg = k_ref[pl.ds(off, gw), :].astype(BF16)
            vg = v_ref[pl.ds(off, gw), :].astype(BF16)
            z = _dot(qs, kg, NT)
            lsz, l1m, e = _sb_logsig(z)
            if allowed is not None:
                l1m = jnp.where(allowed, l1m, 0.0)
            dwgt = _dot(dos, vg, NT)
            dzs = [None] * SB_GROUP
            wgts = [None] * SB_GROUP
            for g in range(SB_GROUP):
                sl = slice(g * SB_BLOCK, (g + 1) * SB_BLOCK)
                al = _sb_cumsum(l1m[:, sl], first_incl if g == 0 else incl)
                wgt = jnp.exp(lsz[:, sl] + (tots - prun - al[:, :SB_BLOCK]))
                if allowed is not None:
                    wgt = jnp.where(allowed[:, sl], wgt, 0.0)
                prun = prun + al[:, SB_BLOCK:]
                ev = wgt * dwgt[:, sl]
                el = _sb_cumsum(ev, before)
                dzs[g] = dscore(z[:, sl], e[:, sl], ev, erun + el[:, :SB_BLOCK])
                erun = erun + el[:, SB_BLOCK:]
                wgts[g] = wgt
            dz = jnp.concatenate(dzs, axis=1)
            if allowed is not None:
                dz = jnp.where(allowed, dz, 0.0)
            dz = dz.astype(BF16)
            wg = jnp.concatenate(wgts, axis=1).astype(BF16)
            dkt_ref[:, pl.ds(off, gw)] += _dot(qst, dz, NN)
            dvt_ref[:, pl.ds(off, gw)] += _dot(dost, wg, NN)
            return dq + _dot(dz, kg, NN), prun, erun

        def below(gi, carry):
            return group(pl.multiple_of(gi * gw, gw), jnp.where(gi == 0, incl0, incl), None, carry)

        zero = tuple(jnp.zeros((2 * SB_BLOCK, HEAD_W), F32) for _ in range(3))
        carry = lax.fori_loop(0, ng, below, zero)
        top = ng * gw
        off = pl.multiple_of(jnp.minimum(top, lp - gw), SB_BLOCK)
        kpos = off + lax.broadcasted_iota(jnp.int32, (1, gw), 1)
        allowed = (kpos < qpos) & (kpos >= pad) & (kpos >= top)
        dq, _, _ = group(off, incl, allowed, carry)
        dq_ref[...] = _sb_unstack(dq) * scale

    full = lambda c0: pl.BlockSpec((lp, HEAD_W), lambda p, i: (0, c0 + p))
    qb = lambda c0: pl.BlockSpec((SB_BLOCK, HEAD_W), lambda p, i: (i, c0 + p))
    tr = pl.BlockSpec((HEAD_W, lp), lambda p, i: (p, 0))
    return pl.pallas_call(
        body, name=name, grid=(npair, nb),
        in_specs=[qb(blk0), full(blk0 + npair), full(blk0 + 2 * npair), qb(0), qb(d_blk0)],
        out_specs=[qb(0), tr, tr],
        out_shape=[jax.ShapeDtypeStruct((lp, npair * HEAD_W), F32)]
        + [jax.ShapeDtypeStruct((npair * HEAD_W, lp), F32)] * 2,
        compiler_params=_cp(("parallel", "arbitrary")),
    )(p0, p0, p0, tot, dsrc)


def _local_step(h0, target, pad, wts):
    lp = h0.shape[0]
    tm = _row_tile(lp, 1056)
    tkl = tm
    d = D_MODEL
    mm = _mm
    g = {}

    h0_b = h0.astype(BF16)
    p0 = mm(h0_b, wts["w_ab"], "NN", tm=tm, tn=768, tk=d, name="l0_in_proj")
    qkv = _gdn_pre_fwd(p0, wts["conv_w"], pad, name="gdn_pre_fwd")
    oa_raw, gdn_states = _gdn_fwd(qkv, p0, wts["alog_v"], wts["dtb_v"], pad, name="gdn_fwd")
    ob, sb_tot = _sb_fwd(p0, pad, name="sb_fwd")
    oab = _gate_fwd(oa_raw, p0, AB_Z // HEAD_W, wts["ab_gn"], ob, heads=GDN_HEADS, name="gdn_gate_fwd")
    mix0 = mm(oab, wts["w_out0"], "NN", tm=tm, tn=512, tk=d, name="l0_out_proj")
    h0a, h0a_b = _ln_fwd(h0, mix0, wts["ln_mix_g"][0], wts["ln_mix_b"][0], name="ln_mix0_fwd")
    u0 = mm(h0a_b, wts["w1"][0], "NN", tm=tm, tn=512, tk=d, b_dev=True, name="mlp0_up")
    y0 = mm(u0, wts["w2"][0], "NN", tm=tm, tn=512, tk=d, a_fn="relu2", name="mlp0_down")
    h0b, h0b_b = _ln_fwd(h0a, y0, wts["ln_ffn_g"][0], wts["ln_ffn_b"][0], name="ln_ffn0_fwd")
    p1 = mm(h0b_b, wts["w_c"], "NN", tm=tm, tn=512, tk=d, b_dev=True, name="l1_in_proj")
    oc_raw, hg_states = _hg_fwd(p1, wts["lb"], pad, name="hg_fwd")
    oc = _gate_fwd(oc_raw, p1, 3 * HG_HEADS, wts["c_gn"], oc_raw, heads=HG_HEADS, name="hg_gate_fwd")
    mix1 = mm(oc, wts["w_out1"], "NN", tm=tm, tn=512, tk=d, name="l1_out_proj")
    h1a, h1a_b = _ln_fwd(h0b, mix1, wts["ln_mix_g"][1], wts["ln_mix_b"][1], name="ln_mix1_fwd")
    u1 = mm(h1a_b, wts["w1"][1], "NN", tm=tm, tn=512, tk=d, b_dev=True, name="mlp1_up")
    y1 = mm(u1, wts["w2"][1], "NN", tm=tm, tn=512, tk=d, a_fn="relu2", name="mlp1_down")
    h1b, _ = _ln_fwd(h1a, y1, wts["ln_ffn_g"][1], wts["ln_ffn_b"][1], name="ln_ffn1_fwd")
    dy, loss_vec = _loss_head(h1b, target, name="loss_head")

    def mlp_bwd(layer, h_in_b, u, dpre, dpre_b):
        du = mm(dpre_b, wts["w2"][layer], "NT", tm=tm, tn=512, tk=d, epi="relu2grad", c=u, out_dtype=BF16,
                name=f"mlp{layer}_d_hidden")
        dw2 = mm(u, dpre_b, "TN", tm=1024, tn=1024, tk=tkl, a_fn="relu2", name=f"mlp{layer}_dw2")
        dw1 = mm(h_in_b, du, "TN", tm=1024, tn=512, tk=tkl, out_dev=True, name=f"mlp{layer}_dw1")
        dh = mm(du, wts["w1"][layer], "NT", tm=tm, tn=1024, tk=512, b_dev=True, epi="add", c=dpre, scale=DN_ALPHA,
                name=f"mlp{layer}_d_in")
        return dh, dw1, dw2

    ln_ffn_dg, ln_ffn_db, ln_mix_dg, ln_mix_db, dw1s, dw2s = ([None, None] for _ in range(6))
    dpre, dpre_b, ln_ffn_dg[1], ln_ffn_db[1] = _ln_bwd(h1a, y1, wts["ln_ffn_g"][1], dy, name="ln_ffn1_bwd")
    dh1a, dw1s[1], dw2s[1] = mlp_bwd(1, h1a_b, u1, dpre, dpre_b)
    dpre, dpre_b, ln_mix_dg[1], ln_mix_db[1] = _ln_bwd(h0b, mix1, wts["ln_mix_g"][1], dh1a, name="ln_mix1_bwd")
    g["c_w_out"] = mm(oc, dpre_b, "TN", tm=1024, tn=1024, tk=tkl, name="l1_dw_out")
    doc = mm(dpre_b, wts["w_out1"], "NT", tm=tm, tn=512, tk=d, name="l1_d_gate")
    doc_raw, dz1, g["c_gn"] = _gate_bwd(oc_raw, p1, 3 * HG_HEADS, wts["c_gn"], doc, heads=HG_HEADS, name="hg_gate_bwd")
    dq1, df1, di1, g["lb"] = _hg_bwd(p1, wts["lb"], hg_states, doc_raw, pad, name="hg_bwd")
    dp1 = jnp.concatenate([dq1, df1, di1, dz1], axis=1).astype(BF16)
    g["c_w_in"] = mm(h0b_b, dp1, "TN", tm=1024, tn=512, tk=tkl, out_dev=True, name="l1_dw_in")
    dh0b = mm(dp1, wts["w_c"], "NT", tm=tm, tn=1024, tk=512, b_dev=True, epi="add", c=dpre, scale=DN_ALPHA,
              name="l1_d_in")
    dpre, dpre_b, ln_ffn_dg[0], ln_ffn_db[0] = _ln_bwd(h0a, y0, wts["ln_ffn_g"][0], dh0b, name="ln_ffn0_bwd")
    dh0a, dw1s[0], dw2s[0] = mlp_bwd(0, h0a_b, u0, dpre, dpre_b)
    dpre, dpre_b, ln_mix_dg[0], ln_mix_db[0] = _ln_bwd(h0, mix0, wts["ln_mix_g"][0], dh0a, name="ln_mix0_bwd")
    g["ab_w_out"] = mm(oab, dpre_b, "TN", tm=1024, tn=1024, tk=tkl, name="l0_dw_out")
    doab = mm(dpre_b, wts["w_out0"], "NT", tm=tm, tn=512, tk=d, name="l0_d_gate")
    doa_raw, dz0, g["ab_gn"] = _gate_bwd(oa_raw, p0, AB_Z // HEAD_W, wts["ab_gn"], doab, heads=GDN_HEADS,
                                         name="gdn_gate_bwd")
    dqb, dkb_t, dvb_t = _sb_bwd(p0, sb_tot, doab, GDN_HEADS, pad, name="sb_bwd")
    dkb, dvb = dkb_t.T, dvb_t.T
    dqn, dkn, dvn, dba, g["alog_v"], g["dtb_v"] = _gdn_bwd(qkv, p0, wts["alog_v"], wts["dtb_v"], gdn_states, doa_raw,
                                                           pad, name="gdn_bwd")
    dconv_in, g["conv_w"] = _gdn_pre_bwd(p0, wts["conv_w"], jnp.concatenate([dqn, dkn, dvn], axis=1), pad,
                                         name="gdn_pre_bwd")
    dp0 = jnp.concatenate([dconv_in, dz0, dqb, dkb, dvb, dba, jnp.zeros((lp, AB_CAT - AB_BA - HEAD_W), F32)],
                          axis=1).astype(BF16)
    g["w_ab"] = mm(h0_b, dp0, "TN", tm=1024, tn=768, tk=tkl, name="l0_dw_in")
    dh0 = mm(dp0, wts["w_ab"], "NT", tm=tm, tn=1024, tk=768, epi="add", c=dpre, scale=DN_ALPHA, name="l0_d_in")

    g["w1"], g["w2"] = dw1s, dw2s
    g["ln_mix_g"] = jnp.concatenate(ln_mix_dg, axis=0)
    g["ln_mix_b"] = jnp.concatenate(ln_mix_db, axis=0)
    g["ln_ffn_g"] = jnp.concatenate(ln_ffn_dg, axis=0)
    g["ln_ffn_b"] = jnp.concatenate(ln_ffn_db, axis=0)
    return loss_vec, dh0, g


N_CHIP = N_DEV // 2


def _place():
    x, y, c = lax.axis_index("x"), lax.axis_index("y"), lax.axis_index("c")
    return x, y, c, 2 * x + y


def _chip_dev(chip, core):
    return (chip // 2, chip % 2, core)


def _remote(src, dst, send_sem, recv_sem, dev):
    return pltpu.make_async_remote_copy(src_ref=src, dst_ref=dst, send_sem=send_sem, recv_sem=recv_sem,
                                        device_id=dev, device_id_type=pl.DeviceIdType.MESH)


_ANY = pl.BlockSpec(memory_space=pl.ANY)


def _gather(srcs, dtypes, *, name):
    n = len(srcs)
    blocks = [s.shape for s in srcs]

    def body(*refs):
        ins, outs, stages = refs[:n], refs[n:2 * n], refs[2 * n:3 * n]
        send_sems, recv_sems, local_sems = refs[3 * n:]
        x, y, c, chip = _place()
        me = 2 * chip + c
        sibling = (x, y, 1 - c)
        local, pending = [], []
        for i in range(n):
            stages[i][...] = ins[i][...].astype(dtypes[i])
            loc = pltpu.make_async_copy(stages[i], outs[i].at[me], local_sems.at[i])
            loc.start()
            local.append(loc)
            first = [_remote(stages[i], outs[i].at[me], send_sems.at[i, 0], recv_sems.at[i, 0], sibling)]
            for j in range(1, N_CHIP):
                first.append(_remote(stages[i], outs[i].at[me], send_sems.at[i, j], recv_sems.at[i, j],
                                     _chip_dev(jnp.bitwise_xor(chip, j), c)))
            for cp in first:
                cp.start()
            pending += first
        for i in range(n):
            for j in range(1, N_CHIP):
                slot = outs[i].at[2 * jnp.bitwise_xor(chip, j) + c]
                _remote(slot, slot, send_sems.at[i, j], recv_sems.at[i, j], sibling).wait_recv()
                fwd = _remote(slot, slot, send_sems.at[i, N_CHIP - 1 + j], recv_sems.at[i, N_CHIP - 1 + j], sibling)
                fwd.start()
                pending.append(fwd)
        for i in range(n):
            blk = outs[i].at[me]
            for k in (0, *range(N_CHIP, 2 * N_CHIP - 1)):
                _remote(blk, blk, send_sems.at[i, k], recv_sems.at[i, k], sibling).wait_recv()
        for cp in pending:
            cp.wait_send()
        for cp in local:
            cp.wait()

    scratch = [pltpu.VMEM(b, dt) for b, dt in zip(blocks, dtypes)]
    scratch += [pltpu.SemaphoreType.DMA((n, 2 * N_CHIP - 1)), pltpu.SemaphoreType.DMA((n, 2 * N_CHIP - 1)),
                pltpu.SemaphoreType.DMA((n,))]
    return pl.pallas_call(
        body, name=name, in_specs=[pl.BlockSpec(memory_space=pltpu.VMEM)] * n, out_specs=[_ANY] * n,
        out_shape=[jax.ShapeDtypeStruct((N_DEV, *b), dt) for b, dt in zip(blocks, dtypes)],
        scratch_shapes=scratch, compiler_params=_cp(has_side_effects=True),
    )(*srcs)


def _to_sibling(parts, *, name):
    n = len(parts)

    def body(*refs):
        ins, outs = refs[:n], refs[n:2 * n]
        send_sems, recv_sems = refs[2 * n:]
        x, y, c, _ = _place()
        copies = [_remote(ins[i].at[2 * k + (1 - c)], outs[i].at[k], send_sems.at[i, k], recv_sems.at[i, k],
                          (x, y, 1 - c)) for i in range(n) for k in range(N_CHIP)]
        for cp in copies:
            cp.start()
        for cp in copies:
            cp.wait()

    return pl.pallas_call(
        body, name=name, in_specs=[_ANY] * n, out_specs=[_ANY] * n,
        out_shape=[jax.ShapeDtypeStruct((N_CHIP, *p.shape[1:]), p.dtype) for p in parts],
        scratch_shapes=[pltpu.SemaphoreType.DMA((n, N_CHIP)), pltpu.SemaphoreType.DMA((n, N_CHIP))],
        compiler_params=_cp(has_side_effects=True),
    )(*parts)


def _pair_sum(part, from_sibling, core, *, name):
    _, r, c = part.shape
    tm = _row_tile(r, 128)

    def body(core_ref, a_ref, b_ref, o_ref):
        o_ref[...] = (a_ref[...] + b_ref[...]).astype(BF16)

    return pl.pallas_call(
        body, name=name,
        grid_spec=pltpu.PrefetchScalarGridSpec(
            num_scalar_prefetch=1, grid=(N_CHIP, r // tm),
            in_specs=[pl.BlockSpec((None, tm, c), lambda k, i, core_ref: (2 * k + core_ref[0], i, 0)),
                      pl.BlockSpec((None, tm, c), lambda k, i, core_ref: (k, i, 0))],
            out_specs=pl.BlockSpec((None, tm, c), lambda k, i, core_ref: (k, i, 0))),
        out_shape=jax.ShapeDtypeStruct((N_CHIP, r, c), BF16), compiler_params=_cp(("parallel", "parallel")),
    )(core, part, from_sibling)


def _to_chips(sums, *, name):
    n = len(sums)

    def body(*refs):
        ins, outs = refs[:n], refs[n:2 * n]
        send_sems, recv_sems, local_sems = refs[2 * n:]
        _, _, c, chip = _place()
        copies = []
        for i in range(n):
            copies.append(pltpu.make_async_copy(ins[i].at[chip], outs[i].at[chip], local_sems.at[i]))
            for j in range(1, N_CHIP):
                other = jnp.bitwise_xor(chip, j)
                copies.append(_remote(ins[i].at[other], outs[i].at[chip], send_sems.at[i, j - 1], recv_sems.at[i, j - 1],
                                      _chip_dev(other, c)))
        for cp in copies:
            cp.start()
        for cp in copies:
            cp.wait()

    return pl.pallas_call(
        body, name=name, in_specs=[_ANY] * n, out_specs=[_ANY] * n,
        out_shape=[jax.ShapeDtypeStruct(s.shape, s.dtype) for s in sums],
        scratch_shapes=[pltpu.SemaphoreType.DMA((n, N_CHIP - 1)), pltpu.SemaphoreType.DMA((n, N_CHIP - 1)),
                        pltpu.SemaphoreType.DMA((n,))],
        compiler_params=_cp(has_side_effects=True),
    )(*sums)


def _adamw(w, parts, m, v, *, name):
    r, c = w.shape
    s = parts.shape[0]
    tm = _row_tile(r, 128) if r % 8 == 0 else r
    c1 = 1.0 - ADAM_B1 ** ADAM_STEP
    c2 = 1.0 - ADAM_B2 ** ADAM_STEP

    def body(w_ref, p_ref, m_ref, v_ref, g_ref, d_ref, m2_ref, v2_ref):
        g = p_ref[0].astype(F32)
        for j in range(1, s):
            g = g + p_ref[j].astype(F32)
        m2 = ADAM_B1 * m_ref[...] + (1.0 - ADAM_B1) * g
        v2 = ADAM_B2 * v_ref[...] + (1.0 - ADAM_B2) * jnp.square(g)
        g_ref[...] = g
        m2_ref[...] = m2
        v2_ref[...] = v2
        d_ref[...] = -ADAM_LR * ((m2 / c1) / (jnp.sqrt(v2 / c2) + ADAM_EPS) + ADAM_WD * w_ref[...])

    blk = pl.BlockSpec((tm, c), lambda i: (i, 0))
    return pl.pallas_call(
        body, name=name, grid=(r // tm,),
        in_specs=[blk, pl.BlockSpec((s, tm, c), lambda i: (0, i, 0)), blk, blk], out_specs=[blk] * 4,
        out_shape=[jax.ShapeDtypeStruct((r, c), F32)] * 4, compiler_params=_cp(("parallel",)),
    )(w, parts, m, v)


_WEIGHTS = ("meta_tokens", "ab_w_in", "ab_conv_w", "ab_a_log", "ab_dt_bias", "ab_gnorm_g", "ab_w_out", "c_w_in",
            "c_lb_raw", "c_gnorm_g", "c_w_out", "ln_mix_g", "ln_mix_b", "mlp_w1", "mlp_w2", "ln_ffn_g", "ln_ffn_b")
_PACK_ROWS = (("ln_mix_g", 0), ("ln_mix_b", 2), ("ln_ffn_g", 4), ("ln_ffn_b", 6), ("c_lb_raw", 8))
_PACK_MISC_ROW = 10
_PACK_MISC = (("ab_gnorm_g", 0, 128), ("c_gnorm_g", 128, 128), ("ab_a_log", 256, GDN_HEADS), ("ab_dt_bias", 260, GDN_HEADS))
_PACK_N = 16
_SMALL_META = 16
_SMALL_CONV = 32
_SMALL_N = 40


def _pack_replicated(p):
    rows = jnp.zeros((_PACK_N, D_MODEL), F32)
    for name, r0 in _PACK_ROWS:
        rows = rows.at[r0:r0 + 2].set(p[name])
    for name, c0, width in _PACK_MISC:
        rows = rows.at[_PACK_MISC_ROW, c0:c0 + width].set(p[name].reshape(width))
    return rows


def _unpack_replicated(rows, like):
    out = {}
    for name, r0 in _PACK_ROWS:
        out[name] = rows[r0:r0 + 2]
    for name, c0, width in _PACK_MISC:
        out[name] = rows[_PACK_MISC_ROW, c0:c0 + width].reshape(like[name].shape)
    return out


def _lower_bound(c_lb_raw):
    lb_all = jnp.cumsum(jax.nn.softmax(c_lb_raw.astype(F32), axis=0), axis=0)
    return (lb_all - lb_all[0:1])[1].reshape(1, -1)


def kernel(x, meta_tokens, ab_w_in, ab_conv_w, ab_a_log, ab_dt_bias, ab_gnorm_g, ab_w_out, c_w_in, c_lb_raw, c_gnorm_g, c_w_out, ln_mix_g, ln_mix_b, mlp_w1, mlp_w2, ln_ffn_g, ln_ffn_b, loss_target, m_meta_tokens, m_ab_w_in, m_ab_conv_w, m_ab_a_log, m_ab_dt_bias, m_ab_gnorm_g, m_ab_w_out, m_c_w_in, m_c_lb_raw, m_c_gnorm_g, m_c_w_out, m_ln_mix_g, m_ln_mix_b, m_mlp_w1, m_mlp_w2, m_ln_ffn_g, m_ln_ffn_b, v_meta_tokens, v_ab_w_in, v_ab_conv_w, v_ab_a_log, v_ab_dt_bias, v_ab_gnorm_g, v_ab_w_out, v_c_w_in, v_c_lb_raw, v_c_gnorm_g, v_c_w_out, v_ln_mix_g, v_ln_mix_b, v_mlp_w1, v_mlp_w2, v_ln_ffn_g, v_ln_ffn_b):
    w = dict(zip(_WEIGHTS, (meta_tokens, ab_w_in, ab_conv_w, ab_a_log, ab_dt_bias, ab_gnorm_g, ab_w_out, c_w_in, c_lb_raw,
                            c_gnorm_g, c_w_out, ln_mix_g, ln_mix_b, mlp_w1, mlp_w2, ln_ffn_g, ln_ffn_b)))
    mom = dict(zip(_WEIGHTS, (m_meta_tokens, m_ab_w_in, m_ab_conv_w, m_ab_a_log, m_ab_dt_bias, m_ab_gnorm_g, m_ab_w_out,
                              m_c_w_in, m_c_lb_raw, m_c_gnorm_g, m_c_w_out, m_ln_mix_g, m_ln_mix_b, m_mlp_w1, m_mlp_w2,
                              m_ln_ffn_g, m_ln_ffn_b)))
    var = dict(zip(_WEIGHTS, (v_meta_tokens, v_ab_w_in, v_ab_conv_w, v_ab_a_log, v_ab_dt_bias, v_ab_gnorm_g, v_ab_w_out,
                              v_c_w_in, v_c_lb_raw, v_c_gnorm_g, v_c_w_out, v_ln_mix_g, v_ln_mix_b, v_mlp_w1, v_mlp_w2,
                              v_ln_ffn_g, v_ln_ffn_b)))
    me = 4 * lax.axis_index("x") + 2 * lax.axis_index("y") + lax.axis_index("c")
    seq = x.shape[1]
    pad = (-(N_META + seq)) % SB_BLOCK
    lp = pad + N_META + seq
    meta_w = D_MODEL // N_DEV
    conv_w_all = 2 * GDN_HEADS * HEAD_W + GDN_HEADS * HEAD_W
    conv_w_mine = conv_w_all // N_DEV

    gathered = _gather(
        [w["meta_tokens"], w["ab_conv_w"][0], w["ab_w_in"][0], w["ab_w_out"][0], w["c_w_in"][0], w["c_w_out"][0],
         w["mlp_w1"], w["mlp_w2"]],
        [F32, F32, BF16, BF16, BF16, BF16, BF16, BF16], name="gather_weights")
    g_meta, g_conv, g_ab_in, g_ab_out, g_c_in, g_c_out, g_w1, g_w2 = gathered
    meta_full = g_meta.transpose(1, 0, 2).reshape(N_META, D_MODEL)
    conv_full = g_conv.transpose(1, 0, 2).reshape(CONV_K, conv_w_all)
    ab_full = g_ab_in.transpose(1, 0, 2).reshape(D_MODEL, AB_IN)
    ba0 = AB_Z + 512
    w_ab = jnp.concatenate([ab_full[:, :ba0], ab_full[:, ba0 + 2 * GDN_HEADS:], ab_full[:, ba0:ba0 + 2 * GDN_HEADS],
                            jnp.zeros((D_MODEL, AB_CAT - AB_IN), BF16)], axis=1)
    vec128 = lambda p: jnp.zeros((1, HEAD_W), F32).at[0, :GDN_HEADS].set(p.reshape(GDN_HEADS))
    wts = dict(
        w_ab=w_ab, conv_w=conv_full, alog_v=vec128(w["ab_a_log"]), dtb_v=vec128(w["ab_dt_bias"]),
        ab_gn=w["ab_gnorm_g"][0], w_out0=g_ab_out.reshape(D_MODEL, D_MODEL), w_c=g_c_in,
        lb=_lower_bound(w["c_lb_raw"]), c_gn=w["c_gnorm_g"][0], w_out1=g_c_out.reshape(D_MODEL, D_MODEL),
        w1=[g_w1[:, l] for l in range(DEPTH)], w2=[g_w2[:, l].reshape(D_FF, D_MODEL) for l in range(DEPTH)],
        ln_mix_g=w["ln_mix_g"], ln_mix_b=w["ln_mix_b"], ln_ffn_g=w["ln_ffn_g"], ln_ffn_b=w["ln_ffn_b"])

    h0 = jnp.concatenate([jnp.zeros((pad, D_MODEL), F32), meta_full, x[0]], axis=0)
    loss_vec, dh0, g = _local_step(h0, loss_target[0], pad, wts)
    loss = lax.psum(jnp.sum(loss_vec), ("x", "y", "c"))
    grad_x = dh0[lp - seq:][None]

    _, lb_vjp = jax.vjp(_lower_bound, w["c_lb_raw"])
    rep_part = _pack_replicated(dict(
        ln_mix_g=g["ln_mix_g"], ln_mix_b=g["ln_mix_b"], ln_ffn_g=g["ln_ffn_g"], ln_ffn_b=g["ln_ffn_b"],
        c_lb_raw=lb_vjp(g["lb"])[0], ab_gnorm_g=g["ab_gn"], c_gnorm_g=g["c_gn"],
        ab_a_log=g["alog_v"][0, :GDN_HEADS], ab_dt_bias=g["dtb_v"][0, :GDN_HEADS]))
    small = jnp.concatenate([rep_part, dh0[pad:pad + N_META], g["conv_w"].reshape(-1, D_MODEL),
                             jnp.zeros((_SMALL_N - _SMALL_CONV - CONV_K * conv_w_all // D_MODEL, D_MODEL), F32)], axis=0)
    (small_all,) = _gather([small], [F32], name="gather_small_grads")
    rep_out = _adamw(_pack_replicated(w), small_all[:, :_PACK_N], _pack_replicated(mom), _pack_replicated(var),
                     name="adamw_replicated")
    meta_parts = lax.dynamic_slice_in_dim(small_all[:, _SMALL_META:_SMALL_META + N_META], me * meta_w, meta_w, axis=2)
    meta_out = _adamw(w["meta_tokens"], meta_parts, mom["meta_tokens"], var["meta_tokens"], name="adamw_meta")
    conv_parts = small_all[:, _SMALL_CONV:_SMALL_CONV + CONV_K * conv_w_all // D_MODEL].reshape(N_DEV, CONV_K, conv_w_all)
    conv_parts = lax.dynamic_slice_in_dim(conv_parts, me * conv_w_mine, conv_w_mine, axis=2)
    conv_out = _adamw(w["ab_conv_w"][0], conv_parts, mom["ab_conv_w"][0], var["ab_conv_w"][0], name="adamw_conv")

    gab = g["w_ab"]
    gab = jnp.concatenate([gab[:, :ba0], gab[:, AB_BA:AB_BA + 2 * GDN_HEADS], gab[:, ba0:AB_BA]], axis=1)
    big = [("ab_w_in", None, gab.reshape(D_MODEL, N_DEV, AB_IN // N_DEV).transpose(1, 0, 2)),
           ("ab_w_out", None, g["ab_w_out"].reshape(N_DEV, D_MODEL // N_DEV, D_MODEL)),
           ("c_w_in", None, g["c_w_in"]),
           ("c_w_out", None, g["c_w_out"].reshape(N_DEV, D_MODEL // N_DEV, D_MODEL))]
    for l in range(DEPTH):
        big.append(("mlp_w1", l, g["w1"][l]))
    for l in range(DEPTH):
        big.append(("mlp_w2", l, g["w2"][l].reshape(N_DEV, D_FF // N_DEV, D_MODEL)))
    from_sibling = _to_sibling([b[2] for b in big], name="scatter_grads_d2d")
    core = lax.axis_index("c").astype(jnp.int32).reshape(1)
    chip_sums = [_pair_sum(b[2], s, core, name=f"chip_sum_{b[0]}" + ("" if b[1] is None else str(b[1])))
                 for b, s in zip(big, from_sibling)]
    parts = _to_chips(chip_sums, name="scatter_grads_ici")
    big_out = {}
    for (name, l, _), p in zip(big, parts):
        sel = (lambda a: a[0]) if l is None else (lambda a, l=l: a[l])
        res = _adamw(sel(w[name]), p, sel(mom[name]), sel(var[name]), name=f"adamw_{name}" + ("" if l is None else str(l)))
        big_out.setdefault(name, []).append(res)

    rep = [_unpack_replicated(r, w) for r in rep_out]
    outs = {}
    for name in _WEIGHTS:
        if name == "meta_tokens":
            outs[name] = list(meta_out)
        elif name == "ab_conv_w":
            outs[name] = [o[None] for o in conv_out]
        elif name in big_out:
            res = big_out[name]
            outs[name] = [o[None] for o in res[0]] if len(res) == 1 else [jnp.stack(pair) for pair in zip(*res)]
        else:
            outs[name] = [r[name] for r in rep]
    flat = [loss, grad_x]
    for kind in range(4):
        flat += [outs[name][kind] for name in _WEIGHTS]
    return tuple(flat)
```

```python
import functools
import math

import jax
import jax.numpy as jnp
from jax import lax
from jax.experimental import pallas as pl
from jax.experimental.pallas import tpu as pltpu

F32 = jnp.float32
BF16 = jnp.bfloat16
HI = lax.Precision.HIGHEST

N_DEV = 8
D_MODEL = 1024
N_META = 16
D_FF = 4096
DEPTH = 2
GDN_HEADS = 4
SB_HEADS = 8
SB_DH = 64
HG_HEADS = 8
HEAD_W = 128
CHUNK = 64
SB_BLOCK = 128
CONV_K = 4
DN_ALPHA = float((2 * DEPTH) ** 0.25)
LN_EPS = 1e-5
RMS_EPS = 1e-6
L2_EPS = 1e-6
ADAM_LR, ADAM_B1, ADAM_B2, ADAM_EPS, ADAM_WD, ADAM_STEP = 0.001, 0.9, 0.999, 1e-08, 0.01, 10

AB_QKV = 0
AB_Z = 1536
AB_SB = 2048
AB_BA = 3584
AB_CAT = 3840
AB_IN = 3592

VMEM_LIMIT = 56 * 1024 * 1024


def _cp(sem=None, **kw):
    if sem is not None:
        kw["dimension_semantics"] = sem
    return pltpu.CompilerParams(vmem_limit_bytes=VMEM_LIMIT, **kw)


def _row_tile(n, want):
    best = 8
    for t in range(8, min(n, want) + 1, 8):
        if n % t == 0:
            best = t
    return best


@jax.custom_vjp
def _sigmoid(x):
    e = jnp.exp(-jnp.abs(x))
    r = 1.0 / (1.0 + e)
    return jnp.where(x >= 0, r, e * r)


def _sigmoid_fwd(x):
    s = _sigmoid(x)
    return s, s


def _sigmoid_bwd(s, g):
    return (g * s * (1.0 - s),)


_sigmoid.defvjp(_sigmoid_fwd, _sigmoid_bwd)


def _log1p_exp_neg_abs(x):
    e = jnp.exp(-jnp.abs(x))
    return jnp.where(e < 1e-4, e - 0.5 * e * e, jnp.log(1.0 + e))


@jax.custom_vjp
def _softplus(x):
    return jnp.maximum(x, 0.0) + _log1p_exp_neg_abs(x)


def _softplus_fwd(x):
    return _softplus(x), x


def _softplus_bwd(x, g):
    return (g * _sigmoid(x),)


_softplus.defvjp(_softplus_fwd, _softplus_bwd)


def _silu(x):
    return x * _sigmoid(x)


def _silu_grad(x):
    s = _sigmoid(x)
    return s * (1.0 + x * (1.0 - s))


def _dot(a, b, dims, precision=None):
    return lax.dot_general(a, b, (dims, ((), ())), precision=precision, preferred_element_type=F32)


NN = ((1,), (0,))
NT = ((1,), (1,))
TN = ((0,), (0,))


def _bdot(a, b, dims):
    return _dot(a.astype(BF16), b.astype(BF16), dims)


def _mm(a, b, mode, *, tm, tn, tk, name, a_fn=None, epi=None, c=None, scale=1.0, b_dev=False, out_dev=False,
        out_dtype=F32):
    if mode == "NN":
        m, kk = a.shape
        n = b.shape[2] * N_DEV if b_dev else b.shape[1]
    elif mode == "NT":
        m, kk = a.shape
        n = b.shape[1] if b_dev else b.shape[0]
    else:
        kk, m = a.shape
        n = b.shape[1]
    assert m % tm == 0 and n % tn == 0 and kk % tk == 0, (name, m, n, kk, tm, tn, tk)
    nk = kk // tk
    dims = {"NN": NN, "NT": NT, "TN": TN}[mode]

    if mode == "TN":
        a_spec = pl.BlockSpec((tk, tm), lambda i, j, k: (k, i))
    else:
        a_spec = pl.BlockSpec((tm, tk), lambda i, j, k: (i, k))
    if mode == "NN":
        if b_dev:
            assert tn == b.shape[2]
            b_spec = pl.BlockSpec((None, tk, tn), lambda i, j, k: (j, k, 0))
        else:
            b_spec = pl.BlockSpec((tk, tn), lambda i, j, k: (k, j))
    elif mode == "NT":
        if b_dev:
            assert tk == b.shape[2]
            b_spec = pl.BlockSpec((None, tn, tk), lambda i, j, k: (k, j, 0))
        else:
            b_spec = pl.BlockSpec((tn, tk), lambda i, j, k: (j, k))
    else:
        b_spec = pl.BlockSpec((tk, tn), lambda i, j, k: (k, j))
    in_specs = [a_spec, b_spec]
    operands = [a, b]
    if epi is not None:
        in_specs.append(pl.BlockSpec((tm, tn), lambda i, j, k: (i, j)))
        operands.append(c)
    if out_dev:
        assert tn == n // N_DEV
        out_shape = jax.ShapeDtypeStruct((N_DEV, m, tn), out_dtype)
        out_spec = pl.BlockSpec((None, tm, tn), lambda i, j, k: (j, i, 0))
    else:
        out_shape = jax.ShapeDtypeStruct((m, n), out_dtype)
        out_spec = pl.BlockSpec((tm, tn), lambda i, j, k: (i, j))

    def body(*refs):
        a_ref, b_ref = refs[0], refs[1]
        c_ref = refs[2] if epi is not None else None
        o_ref = refs[3] if epi is not None else refs[2]
        acc_ref = refs[-1] if nk > 1 else None
        av = a_ref[...]
        if a_fn == "relu2":
            av = jnp.square(jnp.maximum(av, 0.0))
        p = _dot(av.astype(BF16), b_ref[...].astype(BF16), dims)

        def finish(acc):
            if epi == "add":
                acc = acc + scale * c_ref[...]
            elif epi == "relu2grad":
                acc = acc * (2.0 * jnp.maximum(c_ref[...], 0.0))
            o_ref[...] = acc.astype(out_dtype)

        if nk == 1:
            finish(p)
        else:
            k = pl.program_id(2)

            @pl.when(k == 0)
            def _():
                acc_ref[...] = p

            @pl.when(k > 0)
            def _():
                acc_ref[...] += p

            @pl.when(k == nk - 1)
            def _():
                finish(acc_ref[...])

    return pl.pallas_call(
        body, name=name, grid=(m // tm, n // tn, nk), in_specs=in_specs, out_specs=out_spec, out_shape=out_shape,
        scratch_shapes=[pltpu.VMEM((tm, tn), F32)] if nk > 1 else [],
        compiler_params=_cp(("parallel", "parallel", "arbitrary")),
    )(*operands)


def _ln_fwd(a, b, g, beta, *, name):
    lp, d = a.shape
    tm = _row_tile(lp, 512)

    def body(a_ref, b_ref, g_ref, be_ref, y_ref, yb_ref):
        pre = DN_ALPHA * a_ref[...] + b_ref[...]
        mu = jnp.mean(pre, axis=-1, keepdims=True)
        xc = pre - mu
        var = jnp.mean(xc * xc, axis=-1, keepdims=True)
        y = xc * lax.rsqrt(var + LN_EPS) * g_ref[...] + be_ref[...]
        y_ref[...] = y
        yb_ref[...] = y.astype(BF16)

    row = pl.BlockSpec((tm, d), lambda i: (i, 0))
    vec = pl.BlockSpec((1, d), lambda i: (0, 0))
    return pl.pallas_call(
        body, name=name, grid=(lp // tm,), in_specs=[row, row, vec, vec], out_specs=[row, row],
        out_shape=[jax.ShapeDtypeStruct((lp, d), F32), jax.ShapeDtypeStruct((lp, d), BF16)],
        compiler_params=_cp(("parallel",)),
    )(a, b, g.reshape(1, d), beta.reshape(1, d))


def _ln_bwd(a, b, g, dy, *, name):
    lp, d = a.shape
    tm = _row_tile(lp, 512)

    def body(a_ref, b_ref, g_ref, dy_ref, dpre_ref, dpreb_ref, dg_ref, db_ref):
        pre = DN_ALPHA * a_ref[...] + b_ref[...]
        mu = jnp.mean(pre, axis=-1, keepdims=True)
        xc = pre - mu
        var = jnp.mean(xc * xc, axis=-1, keepdims=True)
        rstd = lax.rsqrt(var + LN_EPS)
        xhat = xc * rstd
        dyv = dy_ref[...]
        dxh = dyv * g_ref[...]
        m1 = jnp.mean(dxh, axis=-1, keepdims=True)
        m2 = jnp.mean(dxh * xhat, axis=-1, keepdims=True)
        dpre = rstd * (dxh - m1 - xhat * m2)
        dpre_ref[...] = dpre
        dpreb_ref[...] = dpre.astype(BF16)

        @pl.when(pl.program_id(0) == 0)
        def _():
            dg_ref[...] = jnp.zeros_like(dg_ref)
            db_ref[...] = jnp.zeros_like(db_ref)

        dg_ref[...] += jnp.sum(dyv * xhat, axis=0, keepdims=True)
        db_ref[...] += jnp.sum(dyv, axis=0, keepdims=True)

    row = pl.BlockSpec((tm, d), lambda i: (i, 0))
    vec = pl.BlockSpec((1, d), lambda i: (0, 0))
    return pl.pallas_call(
        body, name=name, grid=(lp // tm,), in_specs=[row, row, vec, row], out_specs=[row, row, vec, vec],
        out_shape=[jax.ShapeDtypeStruct((lp, d), F32), jax.ShapeDtypeStruct((lp, d), BF16),
                   jax.ShapeDtypeStruct((1, d), F32), jax.ShapeDtypeStruct((1, d), F32)],
        compiler_params=_cp(("arbitrary",)),
    )(a, b, g.reshape(1, d), dy)


def _loss_head(y, target, *, name):
    lp, d = y.shape
    seq = target.shape[0]
    tm = SB_BLOCK
    first = (lp - seq) // tm
    assert (lp - seq) % tm == 0 and seq % tm == 0

    def body(y_ref, t_ref, dy_ref, loss_ref):
        i = pl.program_id(0)
        live = i >= first
        diff = jnp.where(live, y_ref[...] - t_ref[...], 0.0)
        dy_ref[...] = diff * (1.0 / d)

        @pl.when(i == 0)
        def _():
            loss_ref[...] = jnp.zeros_like(loss_ref)

        loss_ref[...] += jnp.sum(diff * diff, axis=0, keepdims=True) * (0.5 / d)

    return pl.pallas_call(
        body, name=name, grid=(lp // tm,),
        in_specs=[pl.BlockSpec((tm, d), lambda i: (i, 0)),
                  pl.BlockSpec((tm, d), lambda i: (jnp.maximum(i - first, 0), 0))],
        out_specs=[pl.BlockSpec((tm, d), lambda i: (i, 0)), pl.BlockSpec((1, d), lambda i: (0, 0))],
        out_shape=[jax.ShapeDtypeStruct((lp, d), F32), jax.ShapeDtypeStruct((1, d), F32)],
        compiler_params=_cp(("arbitrary",)),
    )(y, target)


def _gate_fwd(o, zsrc, z_blk0, g, other, *, heads, name):
    lp = o.shape[0]
    tm = _row_tile(lp, 512)
    nblk = D_MODEL // HEAD_W

    def body(o_ref, z_ref, g_ref, x_ref, y_ref):
        h = pl.program_id(1)

        @pl.when(h < heads)
        def _():
            ov = o_ref[...]
            r = lax.rsqrt(jnp.mean(ov * ov, axis=-1, keepdims=True) + RMS_EPS)
            y_ref[...] = (ov * r * g_ref[...] * _silu(z_ref[...])).astype(BF16)

        @pl.when(h >= heads)
        def _():
            y_ref[...] = x_ref[...].astype(BF16)

    other_w = other.shape[1] // HEAD_W
    return pl.pallas_call(
        body, name=name, grid=(lp // tm, nblk),
        in_specs=[pl.BlockSpec((tm, HEAD_W), lambda i, h: (i, jnp.minimum(h, heads - 1))),
                  pl.BlockSpec((tm, HEAD_W), lambda i, h: (i, z_blk0 + jnp.minimum(h, heads - 1))),
                  pl.BlockSpec((1, HEAD_W), lambda i, h: (0, 0)),
                  pl.BlockSpec((tm, HEAD_W), lambda i, h: (i, jnp.clip(h - heads, 0, other_w - 1)))],
        out_specs=pl.BlockSpec((tm, HEAD_W), lambda i, h: (i, h)),
        out_shape=jax.ShapeDtypeStruct((lp, D_MODEL), BF16),
        compiler_params=_cp(("parallel", "arbitrary")),
    )(o, zsrc, g.reshape(1, HEAD_W), other)


def _gate_bwd(o, zsrc, z_blk0, g, dy, *, heads, name):
    lp = o.shape[0]
    tm = _row_tile(lp, 512)

    def body(o_ref, z_ref, g_ref, dy_ref, do_ref, dz_ref, dg_ref):
        ov, zv, gv, dyv = o_ref[...], z_ref[...], g_ref[...], dy_ref[...]
        r = lax.rsqrt(jnp.mean(ov * ov, axis=-1, keepdims=True) + RMS_EPS)
        nrm = ov * r
        s = _silu(zv)
        dn = dyv * gv * s
        do_ref[...] = r * (dn - nrm * jnp.mean(dn * nrm, axis=-1, keepdims=True))
        dz_ref[...] = dyv * nrm * gv * _silu_grad(zv)

        @pl.when((pl.program_id(0) == 0) & (pl.program_id(1) == 0))
        def _():
            dg_ref[...] = jnp.zeros_like(dg_ref)

        dg_ref[...] += jnp.sum(dyv * nrm * s, axis=0, keepdims=True)

    blk = pl.BlockSpec((tm, HEAD_W), lambda i, h: (i, h))
    return pl.pallas_call(
        body, name=name, grid=(lp // tm, heads),
        in_specs=[blk, pl.BlockSpec((tm, HEAD_W), lambda i, h: (i, z_blk0 + h)),
                  pl.BlockSpec((1, HEAD_W), lambda i, h: (0, 0)), blk],
        out_specs=[blk, blk, pl.BlockSpec((1, HEAD_W), lambda i, h: (0, 0))],
        out_shape=[jax.ShapeDtypeStruct((lp, heads * HEAD_W), F32), jax.ShapeDtypeStruct((lp, heads * HEAD_W), F32),
                   jax.ShapeDtypeStruct((1, HEAD_W), F32)],
        compiler_params=_cp(("arbitrary", "arbitrary")),
    )(o, zsrc, g.reshape(1, HEAD_W), dy)


def _conv_taps(x, w):
    acc = w[CONV_K - 1:CONV_K, :] * x
    for k in range(CONV_K - 1):
        acc = acc + w[k:k + 1, :] * pltpu.roll(x, CONV_K - 1 - k, 0)
    return acc


def _gdn_pre_fwd(p0, conv_w, pad, *, name):
    lp = p0.shape[0]
    nq = GDN_HEADS
    qscale = HEAD_W ** -0.5

    def body(x_ref, w_ref, y_ref):
        j = pl.program_id(0)
        c = _conv_taps(x_ref[...], w_ref[...])
        s = _silu(c)
        r = lax.rsqrt(jnp.sum(s * s, axis=-1, keepdims=True) + L2_EPS)
        mult = jnp.where(j < nq, r * qscale, jnp.where(j < 2 * nq, r, 1.0))
        rows = lax.broadcasted_iota(jnp.int32, (lp, 1), 0)
        y_ref[...] = jnp.where(rows >= pad, s * mult, 0.0)

    return pl.pallas_call(
        body, name=name, grid=(3 * nq,),
        in_specs=[pl.BlockSpec((lp, HEAD_W), lambda j: (0, j)), pl.BlockSpec((CONV_K, HEAD_W), lambda j: (0, j))],
        out_specs=pl.BlockSpec((lp, HEAD_W), lambda j: (0, j)),
        out_shape=jax.ShapeDtypeStruct((lp, 3 * nq * HEAD_W), F32), compiler_params=_cp(("parallel",)),
    )(p0, conv_w)


def _gdn_pre_bwd(p0, conv_w, dqkv, pad, *, name):
    lp = p0.shape[0]
    nq = GDN_HEADS
    qscale = HEAD_W ** -0.5

    def body(x_ref, w_ref, dy_ref, dx_ref, dw_ref):
        j = pl.program_id(0)
        x, w = x_ref[...], w_ref[...]
        c = _conv_taps(x, w)
        s = _silu(c)
        r = lax.rsqrt(jnp.sum(s * s, axis=-1, keepdims=True) + L2_EPS)
        rows = lax.broadcasted_iota(jnp.int32, (lp, 1), 0)
        dy = jnp.where(rows >= pad, dy_ref[...], 0.0)
        nrm = s * r
        dn = dy * jnp.where(j < nq, qscale, 1.0)
        ds_norm = r * (dn - nrm * jnp.sum(nrm * dn, axis=-1, keepdims=True))
        ds = jnp.where(j < 2 * nq, ds_norm, dy)
        dc = ds * _silu_grad(c)
        dx = w[CONV_K - 1:CONV_K, :] * dc
        dws = [None] * CONV_K
        dws[CONV_K - 1] = jnp.sum(dc * x, axis=0, keepdims=True)
        for k in range(CONV_K - 1):
            sh = CONV_K - 1 - k
            dx = dx + w[k:k + 1, :] * pltpu.roll(dc, lp - sh, 0)
            dws[k] = jnp.sum(dc * pltpu.roll(x, sh, 0), axis=0, keepdims=True)
        dx_ref[...] = dx
        dw_ref[...] = jnp.concatenate(dws, axis=0)

    blk = pl.BlockSpec((lp, HEAD_W), lambda j: (0, j))
    wblk = pl.BlockSpec((CONV_K, HEAD_W), lambda j: (0, j))
    return pl.pallas_call(
        body, name=name, grid=(3 * nq,), in_specs=[blk, wblk, blk], out_specs=[blk, wblk],
        out_shape=[jax.ShapeDtypeStruct((lp, 3 * nq * HEAD_W), F32),
                   jax.ShapeDtypeStruct((CONV_K, 3 * nq * HEAD_W), F32)],
        compiler_params=_cp(("parallel",)),
    )(p0, conv_w, dqkv)


def _tri(c, strict):
    r = lax.broadcasted_iota(jnp.int32, (c, c), 0)
    q = lax.broadcasted_iota(jnp.int32, (c, c), 1)
    return (q < r) if strict else (q <= r)


@jax.custom_vjp
def _inv_unit_lower(m):
    c = m.shape[0]
    eye = (lax.broadcasted_iota(jnp.int32, (c, c), 0) == lax.broadcasted_iota(jnp.int32, (c, c), 1)).astype(F32)
    x = eye - m
    p = m
    n = 2
    while n < CHUNK:
        p = _dot3x(p, p, NN)
        x = x + _dot3x(x, p, NN)
        n *= 2
    return x


def _inv_fwd(m):
    t = _inv_unit_lower(m)
    return t, t


def _inv_bwd(t, g):
    return (-_dot3x(_dot3x(t, g, TN), t, NT),)


_inv_unit_lower.defvjp(_inv_fwd, _inv_bwd)


def _heads_to_rows(x, nh):
    return jnp.concatenate([x[:, h * HEAD_W:(h + 1) * HEAD_W] for h in range(nh)], axis=0)


def _rows_to_heads(x, nh):
    c = x.shape[0] // nh
    return jnp.concatenate([x[h * c:(h + 1) * c] for h in range(nh)], axis=1)


def _gdn_chunk(q, k, v, ba, alog, dtb, states, valid):
    nh = GDN_HEADS
    c = q.shape[0]
    r = nh * c
    lane = lax.broadcasted_iota(jnp.int32, (1, HEAD_W), 1)
    pick = lambda x, l: jnp.sum(jnp.where(lane == l, x, 0.0), axis=-1, keepdims=True)
    beta = jnp.concatenate([jnp.where(valid, _sigmoid(pick(ba, h)), 0.0) for h in range(nh)], axis=0)
    g = jnp.concatenate(
        [jnp.where(valid, -jnp.exp(pick(alog, h)) * _softplus(pick(ba, nh + h) + pick(dtb, h)), 0.0) for h in range(nh)],
        axis=0)
    qs, ks, vs = _heads_to_rows(q, nh), _heads_to_rows(k, nh), _heads_to_rows(v, nh)
    rr = lax.broadcasted_iota(jnp.int32, (r, r), 0)
    cc = lax.broadcasted_iota(jnp.int32, (r, r), 1)
    same = (rr // c) == (cc // c)
    causal, strict = same & (cc <= rr), same & (cc < rr)
    lower = jnp.where(causal, 1.0, 0.0).astype(BF16)
    upper = jnp.where(same & (cc >= rr), 1.0, 0.0).astype(BF16)
    gcb = _mask_mm(lower, upper, g * jnp.ones((1, HEAD_W), F32))
    gc_col = jnp.concatenate([gcb] * (r // HEAD_W), axis=1)
    decay = jnp.where(causal, jnp.exp(jnp.minimum(gc_col - gc_col.T, 0.0)), 0.0)
    egc = jnp.exp(gcb)
    kb = ks * beta
    m = jnp.where(strict, _bdot(kb, ks, NT) * decay, 0.0)
    t = _inv_unit_lower(m)
    u = _bdot(t, vs * beta, NN)
    w = _bdot(t, kb * egc, NN)
    a = _bdot(qs, ks, NT) * decay
    rows = lambda x, h: x[h * c:(h + 1) * c]
    qe = qs * egc
    v_new = u - jnp.concatenate([_bdot(rows(w, h), states[h], NN) for h in range(nh)], axis=0)
    o = jnp.concatenate([_bdot(rows(qe, h), states[h], NN) for h in range(nh)], axis=0) + _bdot(a, v_new, NN)
    new_states = []
    for h in range(nh):
        gl = gcb[(h + 1) * c - 1:(h + 1) * c, :]
        k_dec = rows(ks, h) * jnp.exp(gl - rows(gcb, h))
        new_states.append(states[h] * jnp.exp(gl) + _bdot(k_dec, rows(v_new, h), TN))
    return _rows_to_heads(o, nh), new_states


def _gdn_fwd(qkv, p0, alog_v, dtb_v, pad, *, name):
    lp = qkv.shape[0]
    n = lp // CHUNK
    nh = GDN_HEADS

    def body(q_ref, k_ref, v_ref, ba_ref, al_ref, dt_ref, o_ref, st_ref, s_ref):
        i = pl.program_id(0)

        @pl.when(i == 0)
        def _():
            s_ref[...] = jnp.zeros_like(s_ref)

        valid = (i * CHUNK + lax.broadcasted_iota(jnp.int32, (CHUNK, 1), 0)) >= pad
        s = s_ref[...]
        o, s2 = _gdn_chunk(q_ref[...], k_ref[...], v_ref[...], ba_ref[...], al_ref[...], dt_ref[...],
                           [s[h] for h in range(nh)], valid)
        st_ref[...] = s
        o_ref[...] = o
        for h in range(nh):
            s_ref[h] = s2[h]

    w = nh * HEAD_W
    vec = pl.BlockSpec((1, HEAD_W), lambda i: (0, 0))
    return pl.pallas_call(
        body, name=name, grid=(n,),
        in_specs=[pl.BlockSpec((CHUNK, w), lambda i: (i, 0)), pl.BlockSpec((CHUNK, w), lambda i: (i, 1)),
                  pl.BlockSpec((CHUNK, w), lambda i: (i, 2)), pl.BlockSpec((CHUNK, HEAD_W), lambda i: (i, AB_BA // HEAD_W)),
                  vec, vec],
        out_specs=[pl.BlockSpec((CHUNK, w), lambda i: (i, 0)),
                   pl.BlockSpec((None, nh, HEAD_W, HEAD_W), lambda i: (i, 0, 0, 0))],
        out_shape=[jax.ShapeDtypeStruct((lp, w), F32), jax.ShapeDtypeStruct((n, nh, HEAD_W, HEAD_W), F32)],
        scratch_shapes=[pltpu.VMEM((nh, HEAD_W, HEAD_W), F32)],
        compiler_params=_cp(("arbitrary",)),
    )(qkv, qkv, qkv, p0, alog_v, dtb_v)


def _gdn_bwd(qkv, p0, alog_v, dtb_v, states, do, pad, *, name):
    lp = qkv.shape[0]
    n = lp // CHUNK
    nh = GDN_HEADS

    def body(q_ref, k_ref, v_ref, ba_ref, al_ref, dt_ref, st_ref, do_ref,
             dq_ref, dk_ref, dv_ref, dba_ref, dal_ref, ddt_ref, ds_ref):
        step = pl.program_id(0)
        i = n - 1 - step

        @pl.when(step == 0)
        def _():
            ds_ref[...] = jnp.zeros_like(ds_ref)
            dal_ref[...] = jnp.zeros_like(dal_ref)
            ddt_ref[...] = jnp.zeros_like(ddt_ref)

        valid = (i * CHUNK + lax.broadcasted_iota(jnp.int32, (CHUNK, 1), 0)) >= pad
        st, dst = st_ref[...], ds_ref[...]
        fn = functools.partial(_gdn_chunk, valid=valid)
        _, vjp = jax.vjp(fn, q_ref[...], k_ref[...], v_ref[...], ba_ref[...], al_ref[...], dt_ref[...],
                         [st[h] for h in range(nh)])
        dq, dk, dv, dba, dal, ddt, ds = vjp((do_ref[...], [dst[h] for h in range(nh)]))
        dq_ref[...] = dq
        dk_ref[...] = dk
        dv_ref[...] = dv
        dba_ref[...] = dba
        dal_ref[...] += dal
        ddt_ref[...] += ddt
        for h in range(nh):
            ds_ref[h] = ds[h]

    w = nh * HEAD_W
    rev = lambda c: (lambda s: (n - 1 - s, c))
    vec = pl.BlockSpec((1, HEAD_W), lambda s: (0, 0))
    dq, dk, dv, dba, dal, ddt = pl.pallas_call(
        body, name=name, grid=(n,),
        in_specs=[pl.BlockSpec((CHUNK, w), rev(0)), pl.BlockSpec((CHUNK, w), rev(1)), pl.BlockSpec((CHUNK, w), rev(2)),
                  pl.BlockSpec((CHUNK, HEAD_W), rev(AB_BA // HEAD_W)), vec, vec,
                  pl.BlockSpec((None, nh, HEAD_W, HEAD_W), lambda s: (n - 1 - s, 0, 0, 0)),
                  pl.BlockSpec((CHUNK, w), rev(0))],
        out_specs=[pl.BlockSpec((CHUNK, w), rev(0)), pl.BlockSpec((CHUNK, w), rev(0)), pl.BlockSpec((CHUNK, w), rev(0)),
                   pl.BlockSpec((CHUNK, HEAD_W), rev(0)), vec, vec],
        out_shape=[jax.ShapeDtypeStruct((lp, w), F32)] * 3 + [jax.ShapeDtypeStruct((lp, HEAD_W), F32)]
        + [jax.ShapeDtypeStruct((1, HEAD_W), F32)] * 2,
        scratch_shapes=[pltpu.VMEM((nh, HEAD_W, HEAD_W), F32)],
        compiler_params=_cp(("arbitrary",)),
    )(qkv, qkv, qkv, p0, alog_v, dtb_v, states, do)
    return dq, dk, dv, dba, dal, ddt


HG_LEVELS = (32, 16, 8, 4, 2, 1)
HG_GROUP = 4


def _hg_masks():
    import numpy as np
    c = CHUNK
    t = np.arange(c)[:, None]
    j = np.arange(c)[None, :]
    sums = [j <= t, j > t]
    pairs = [j == t]
    for m in HG_LEVELS:
        p = (t // (2 * m)) * (2 * m)
        r = p + m
        upper = t >= r
        sums.append(upper & (j > r) & (j <= t))
        sums.append(~upper & (j > t) & (j <= r))
        pairs.append(upper & (j < r) & (j >= p))
    sums = np.concatenate(sums, axis=0).astype(np.float32)
    pairs = np.concatenate([np.kron(np.eye(HG_GROUP), p) for p in pairs], axis=0).astype(np.float32)
    return jnp.asarray(sums, BF16), jnp.asarray(sums.T, BF16), jnp.asarray(pairs, F32)


def _split3(x):
    hi = x.astype(BF16)
    r1 = x - hi.astype(F32)
    mid = r1.astype(BF16)
    return hi, mid, (r1 - mid.astype(F32)).astype(BF16)


def _dot3x(a, b, dims):
    ah, am, _ = _split3(a)
    bh, bm, _ = _split3(b)
    return _dot(ah, bh, dims) + (_dot(ah, bm, dims) + _dot(am, bh, dims))


def _mask_mm_raw(m, x):
    return sum(_dot(m, part, NN) for part in _split3(x))


@jax.custom_vjp
def _mask_mm(m, mt, x):
    return _mask_mm_raw(m, x)


def _mask_mm_fwd(m, mt, x):
    return _mask_mm_raw(m, x), (m, mt)


def _mask_mm_bwd(res, g):
    m, mt = res
    return jnp.zeros_like(m), jnp.zeros_like(mt), _mask_mm_raw(mt, g)


_mask_mm.defvjp(_mask_mm_fwd, _mask_mm_bwd)


def _hg_chunk(qr, fr, ir, lb, states, valid, sums, sums_t, pairs):
    nh = HG_GROUP
    c = qr.shape[0]
    r = nh * c
    fg = lb + (1.0 - lb) * _sigmoid(fr)
    logf = jnp.where(valid, jnp.log(fg), 0.0)
    k = jnp.where(valid, 1.0 - fg, 0.0)
    qs = jnp.where(valid, _silu(qr), 0.0)
    v = jnp.where(valid, ir, 0.0)
    e = jnp.exp(_mask_mm(sums, sums_t, logf))
    blk = lambda n: e[n * c:(n + 1) * c]
    mask = lambda n: pairs[n * r:(n + 1) * r]
    stack = lambda x: _heads_to_rows(x, nh)
    a = mask(0) * _bdot(stack(qs), stack(k), NT)
    for lvl in range(len(HG_LEVELS)):
        a = a + mask(1 + lvl) * _bdot(stack(qs * blk(2 + 2 * lvl)), stack(k * blk(3 + 2 * lvl)), NT)
    av = _bdot(a, stack(v), NN)
    eb = blk(0)
    qe, kd = qs * eb, k * blk(1)
    outs, new_states = [], []
    for h in range(nh):
        cs = slice(h * HEAD_W, (h + 1) * HEAD_W)
        outs.append(_bdot(qe[:, cs], states[h], NT) + av[h * c:(h + 1) * c])
        new_states.append(states[h] * eb[c - 1:c, cs] + _bdot(v[:, cs], kd[:, cs], TN))
    return jnp.concatenate(outs, axis=1), new_states


def _hg_fwd(p1, lb, pad, *, name):
    lp = p1.shape[0]
    n = lp // CHUNK
    nh = HG_HEADS

    def body(q_ref, f_ref, i_ref, lb_ref, sums_ref, sums_t_ref, pairs_ref, o_ref, st_ref, s_ref):
        i = pl.program_id(1)

        @pl.when(i == 0)
        def _():
            s_ref[...] = jnp.zeros_like(s_ref)

        valid = (i * CHUNK + lax.broadcasted_iota(jnp.int32, (CHUNK, 1), 0)) >= pad
        s = s_ref[...]
        o, s2 = _hg_chunk(q_ref[...], f_ref[...], i_ref[...], lb_ref[...], [s[h] for h in range(grp)], valid,
                          sums_ref[...], sums_t_ref[...], pairs_ref[...])
        st_ref[...] = s
        o_ref[...] = o
        for h in range(grp):
            s_ref[h] = s2[h]

    masks = _hg_masks()
    grp, ngrp, gw = HG_GROUP, nh // HG_GROUP, HG_GROUP * HEAD_W
    blk = lambda off: pl.BlockSpec((CHUNK, gw), lambda h, i: (i, off + h))
    const = lambda a: pl.BlockSpec(a.shape, lambda h, i: (0, 0))
    return pl.pallas_call(
        body, name=name, grid=(ngrp, n),
        in_specs=[blk(0), blk(ngrp), blk(2 * ngrp), pl.BlockSpec((1, gw), lambda h, i: (0, h))]
        + [const(a) for a in masks],
        out_specs=[blk(0), pl.BlockSpec((grp, None, HEAD_W, HEAD_W), lambda h, i: (h, i, 0, 0))],
        out_shape=[jax.ShapeDtypeStruct((lp, nh * HEAD_W), F32), jax.ShapeDtypeStruct((nh, n, HEAD_W, HEAD_W), F32)],
        scratch_shapes=[pltpu.VMEM((grp, HEAD_W, HEAD_W), F32)],
        compiler_params=_cp(("parallel", "arbitrary")),
    )(p1, p1, p1, lb, *masks)


def _hg_bwd(p1, lb, states, do, pad, *, name):
    lp = p1.shape[0]
    n = lp // CHUNK
    nh = HG_HEADS

    def body(q_ref, f_ref, i_ref, lb_ref, st_ref, do_ref, sums_ref, sums_t_ref, pairs_ref,
             dq_ref, df_ref, di_ref, dlb_ref, ds_ref):
        step = pl.program_id(1)
        i = n - 1 - step

        @pl.when(step == 0)
        def _():
            ds_ref[...] = jnp.zeros_like(ds_ref)
            dlb_ref[...] = jnp.zeros_like(dlb_ref)

        valid = (i * CHUNK + lax.broadcasted_iota(jnp.int32, (CHUNK, 1), 0)) >= pad
        fn = functools.partial(_hg_chunk, valid=valid, sums=sums_ref[...], sums_t=sums_t_ref[...],
                               pairs=pairs_ref[...])
        st, dst = st_ref[...], ds_ref[...]
        _, vjp = jax.vjp(fn, q_ref[...], f_ref[...], i_ref[...], lb_ref[...], [st[h] for h in range(grp)])
        dq, df, di, dlb, ds = vjp((do_ref[...], [dst[h] for h in range(grp)]))
        dq_ref[...] = dq
        df_ref[...] = df
        di_ref[...] = di
        dlb_ref[...] += dlb
        for h in range(grp):
            ds_ref[h] = ds[h]

    masks = _hg_masks()
    grp, ngrp, gw = HG_GROUP, nh // HG_GROUP, HG_GROUP * HEAD_W
    blk = lambda off: pl.BlockSpec((CHUNK, gw), lambda h, s: (n - 1 - s, off + h))
    const = lambda a: pl.BlockSpec(a.shape, lambda h, s: (0, 0))
    w = nh * HEAD_W
    return pl.pallas_call(
        body, name=name, grid=(ngrp, n),
        in_specs=[blk(0), blk(ngrp), blk(2 * ngrp), pl.BlockSpec((1, gw), lambda h, s: (0, h)),
                  pl.BlockSpec((grp, None, HEAD_W, HEAD_W), lambda h, s: (h, n - 1 - s, 0, 0)), blk(0)]
        + [const(a) for a in masks],
        out_specs=[blk(0), blk(0), blk(0), pl.BlockSpec((1, gw), lambda h, s: (0, h))],
        out_shape=[jax.ShapeDtypeStruct((lp, w), F32)] * 3 + [jax.ShapeDtypeStruct((1, w), F32)],
        scratch_shapes=[pltpu.VMEM((grp, HEAD_W, HEAD_W), F32)],
        compiler_params=_cp(("parallel", "arbitrary")),
    )(p1, p1, p1, lb, states, do, *masks)


SB_GROUP = 4


def _sb_cat(kind, first_key=0):
    r = lax.broadcasted_iota(jnp.int32, (SB_BLOCK, 2 * SB_BLOCK), 0)
    c = lax.broadcasted_iota(jnp.int32, (SB_BLOCK, 2 * SB_BLOCK), 1)
    tri = {"after": c < r, "incl": r <= c, "before": r < c}[kind]
    m = ((c >= SB_BLOCK) | tri) & (r >= first_key)
    return jnp.where(m, 1.0, 0.0).astype(BF16)


def _sb_cumsum(x, cat):
    return _dot(x.astype(BF16), cat, NN)


def _sb_logsig(z):
    e = jnp.exp(-jnp.abs(z))
    lse = jnp.where(e < 1e-4, e, jnp.log(1.0 + e))
    lsz = jnp.minimum(z, 0.0) - lse
    return lsz, lsz - z, e


def _sb_stack(x, scale=None):
    lane = lax.broadcasted_iota(jnp.int32, (1, HEAD_W), 1)
    if scale is not None:
        x = x * scale
    return jnp.concatenate([jnp.where(lane < SB_DH, x, 0.0), jnp.where(lane >= SB_DH, x, 0.0)], axis=0).astype(BF16)


def _sb_unstack(x):
    lane = lax.broadcasted_iota(jnp.int32, (1, HEAD_W), 1)
    return jnp.where(lane < SB_DH, x[:SB_BLOCK], x[SB_BLOCK:])


def _sb_fwd(p0, pad, *, name, gather=None):
    lp = p0.shape[0]
    nb = lp // SB_BLOCK
    npair = SB_HEADS // 2
    blk0 = AB_SB // HEAD_W
    scale = SB_DH ** -0.5
    gw = SB_GROUP * SB_BLOCK
    assert pad < SB_BLOCK
    g_srcs, g_dtypes = gather if gather is not None else ([], [])
    ng_arr = len(g_srcs)

    def body(q_ref, k_ref, v_ref, *rest):
        g_ins, (o_ref, tot_ref) = rest[:ng_arr], rest[ng_arr:ng_arr + 2]
        g_outs, g_scratch = rest[ng_arr + 2:2 * ng_arr + 2], rest[2 * ng_arr + 2:]
        first_step = (pl.program_id(0) == 0) & (pl.program_id(1) == 0)
        last_pair = pl.program_id(0) == npair - 1
        if ng_arr:
            g_start, g_forward, g_finish = _gather_phases(g_ins, g_outs, g_scratch[:ng_arr], *g_scratch[ng_arr:],
                                                          g_dtypes)
            pl.when(first_step)(g_start)
            pl.when(last_pair & (pl.program_id(1) == 0))(g_forward)
        i = pl.program_id(1)
        qs = _sb_stack(q_ref[...], scale)
        qpos = i * SB_BLOCK + lax.broadcasted_iota(jnp.int32, (SB_BLOCK, 1), 0)
        qpos = jnp.concatenate([qpos, qpos], axis=0)
        cat = _sb_cat("after")
        cat0 = _sb_cat("after", pad)
        ng = i // SB_GROUP

        def group(off, first_cat, allowed, carry):
            acc, run = carry
            kg = k_ref[pl.ds(off, gw), :].astype(BF16)
            vg = v_ref[pl.ds(off, gw), :].astype(BF16)
            lsz, l1m, _ = _sb_logsig(_dot(qs, kg, NT))
            if allowed is not None:
                l1m = jnp.where(allowed, l1m, 0.0)
            args = [None] * SB_GROUP
            for g in reversed(range(SB_GROUP)):
                sl = slice(g * SB_BLOCK, (g + 1) * SB_BLOCK)
                al = _sb_cumsum(l1m[:, sl], first_cat if g == 0 else cat)
                args[g] = lsz[:, sl] + al[:, :SB_BLOCK] + run
                run = run + al[:, SB_BLOCK:]
            wgt = jnp.exp(jnp.concatenate(args, axis=1))
            if allowed is not None:
                wgt = jnp.where(allowed, wgt, 0.0)
            return acc + _dot(wgt.astype(BF16), vg, NN), run

        def below(t, carry):
            gi = ng - 1 - t
            return group(pl.multiple_of(gi * gw, gw), jnp.where(gi == 0, cat0, cat), None, carry)

        top = ng * gw
        off = pl.multiple_of(jnp.minimum(top, lp - gw), SB_BLOCK)
        kpos = off + lax.broadcasted_iota(jnp.int32, (1, gw), 1)
        allowed = (kpos < qpos) & (kpos >= pad) & (kpos >= top)
        zero = (jnp.zeros((2 * SB_BLOCK, HEAD_W), F32), jnp.zeros((2 * SB_BLOCK, HEAD_W), F32))
        carry = group(off, cat, allowed, zero)
        acc, run = lax.fori_loop(0, ng, below, carry)
        o_ref[...] = _sb_unstack(acc)
        tot_ref[...] = _sb_unstack(run)
        if ng_arr:
            pl.when(last_pair & (pl.program_id(1) == nb - 1))(g_finish)

    full = lambda c0: pl.BlockSpec((lp, HEAD_W), lambda p, i: (0, c0 + p))
    out = pl.BlockSpec((SB_BLOCK, HEAD_W), lambda p, i: (i, p))
    return pl.pallas_call(
        body, name=name, grid=(npair, nb),
        in_specs=[pl.BlockSpec((SB_BLOCK, HEAD_W), lambda p, i: (i, blk0 + p)), full(blk0 + npair), full(blk0 + 2 * npair)]
        + [pl.BlockSpec(memory_space=pltpu.VMEM)] * ng_arr,
        out_specs=[out, out] + [_ANY] * ng_arr,
        out_shape=[jax.ShapeDtypeStruct((lp, npair * HEAD_W), F32)] * 2 + _gather_out_shapes(g_srcs, g_dtypes),
        scratch_shapes=_gather_scratch(g_srcs, g_dtypes) if ng_arr else [],
        compiler_params=_cp(("arbitrary", "arbitrary"), has_side_effects=bool(ng_arr)),
    )(p0, p0, p0, *g_srcs)


def _sb_bwd(p0, tot, dsrc, d_blk0, pad, *, name, chip_sums=()):
    lp = p0.shape[0]
    nb = lp // SB_BLOCK
    npair = SB_HEADS // 2
    blk0 = AB_SB // HEAD_W
    scale = SB_DH ** -0.5
    gw = SB_GROUP * SB_BLOCK
    assert pad < SB_BLOCK
    ns = len(chip_sums)

    def body(q_ref, k_ref, v_ref, tot_ref, do_ref, *rest):
        s_ins, (dq_ref, dkt_ref, dvt_ref) = rest[:ns], rest[ns:ns + 3]
        s_outs, s_sems = rest[ns + 3:2 * ns + 3], rest[2 * ns + 3:]
        if ns:
            s_start, s_finish = _to_chips_phases(s_ins, s_outs, *s_sems)
            pl.when((pl.program_id(0) == 0) & (pl.program_id(1) == 0))(s_start)
        i = pl.program_id(1)

        @pl.when(i == 0)
        def _():
            dkt_ref[...] = jnp.zeros_like(dkt_ref)
            dvt_ref[...] = jnp.zeros_like(dvt_ref)

        qs = _sb_stack(q_ref[...], scale)
        dos = _sb_stack(do_ref[...])
        qst, dost = qs.T, dos.T
        totv = tot_ref[...]
        ones = jnp.ones((1, HEAD_W), F32)
        tots = jnp.concatenate([totv[:, 0:1] * ones, totv[:, SB_DH:SB_DH + 1] * ones], axis=0)
        qpos = i * SB_BLOCK + lax.broadcasted_iota(jnp.int32, (SB_BLOCK, 1), 0)
        qpos = jnp.concatenate([qpos, qpos], axis=0)
        incl, incl0 = _sb_cat("incl"), _sb_cat("incl", pad)
        before = _sb_cat("before")
        ng = i // SB_GROUP

        def dscore(z, e, ev, dl1m):
            r = 1.0 / (1.0 + e)
            sg = jnp.where(z >= 0, r, e * r)
            return ev * (1.0 - sg) - dl1m * sg

        def group(off, first_incl, allowed, carry):
            dq, prun, erun = carry
            kg = k_ref[pl.ds(off, gw), :].astype(BF16)
            vg = v_ref[pl.ds(off, gw), :].astype(BF16)
            z = _dot(qs, kg, NT)
            lsz, l1m, e = _sb_logsig(z)
            if allowed is not None:
                l1m = jnp.where(allowed, l1m, 0.0)
            dwgt = _dot(dos, vg, NT)
            dzs = [None] * SB_GROUP
            wgts = [None] * SB_GROUP
            for g in range(SB_GROUP):
                sl = slice(g * SB_BLOCK, (g + 1) * SB_BLOCK)
                al = _sb_cumsum(l1m[:, sl], first_incl if g == 0 else incl)
                wgt = jnp.exp(lsz[:, sl] + (tots - prun - al[:, :SB_BLOCK]))
                if allowed is not None:
                    wgt = jnp.where(allowed[:, sl], wgt, 0.0)
                prun = prun + al[:, SB_BLOCK:]
                ev = wgt * dwgt[:, sl]
                el = _sb_cumsum(ev, before)
                dzs[g] = dscore(z[:, sl], e[:, sl], ev, erun + el[:, :SB_BLOCK])
                erun = erun + el[:, SB_BLOCK:]
                wgts[g] = wgt
            dz = jnp.concatenate(dzs, axis=1)
            if allowed is not None:
                dz = jnp.where(allowed, dz, 0.0)
            dz = dz.astype(BF16)
            wg = jnp.concatenate(wgts, axis=1).astype(BF16)
            dkt_ref[:, pl.ds(off, gw)] += _dot(qst, dz, NN)
            dvt_ref[:, pl.ds(off, gw)] += _dot(dost, wg, NN)
            return dq + _dot(dz, kg, NN), prun, erun

        def below(gi, carry):
            return group(pl.multiple_of(gi * gw, gw), jnp.where(gi == 0, incl0, incl), None, carry)

        zero = tuple(jnp.zeros((2 * SB_BLOCK, HEAD_W), F32) for _ in range(3))
        carry = lax.fori_loop(0, ng, below, zero)
        top = ng * gw
        off = pl.multiple_of(jnp.minimum(top, lp - gw), SB_BLOCK)
        kpos = off + lax.broadcasted_iota(jnp.int32, (1, gw), 1)
        allowed = (kpos < qpos) & (kpos >= pad) & (kpos >= top)
        dq, _, _ = group(off, incl, allowed, carry)
        dq_ref[...] = _sb_unstack(dq) * scale
        if ns:
            pl.when((pl.program_id(0) == npair - 1) & (pl.program_id(1) == nb - 1))(s_finish)

    full = lambda c0: pl.BlockSpec((lp, HEAD_W), lambda p, i: (0, c0 + p))
    qb = lambda c0: pl.BlockSpec((SB_BLOCK, HEAD_W), lambda p, i: (i, c0 + p))
    tr = pl.BlockSpec((HEAD_W, lp), lambda p, i: (p, 0))
    return pl.pallas_call(
        body, name=name, grid=(npair, nb),
        in_specs=[qb(blk0), full(blk0 + npair), full(blk0 + 2 * npair), qb(0), qb(d_blk0)] + [_ANY] * ns,
        out_specs=[qb(0), tr, tr] + [_ANY] * ns,
        out_shape=[jax.ShapeDtypeStruct((lp, npair * HEAD_W), F32)]
        + [jax.ShapeDtypeStruct((npair * HEAD_W, lp), F32)] * 2
        + [jax.ShapeDtypeStruct(s.shape, s.dtype) for s in chip_sums],
        scratch_shapes=_to_chips_scratch(ns) if ns else [],
        compiler_params=_cp(("arbitrary", "arbitrary"), has_side_effects=bool(ns)),
    )(p0, p0, p0, tot, dsrc, *chip_sums)


def _local_step(h0, target, pad, wts, hooks=None):
    lp = h0.shape[0]
    tm = _row_tile(lp, 1056)
    tkl = tm
    d = D_MODEL
    mm = _mm
    g = {}

    h0_b = h0.astype(BF16)
    p0 = mm(h0_b, wts["w_ab"], "NN", tm=tm, tn=768, tk=d, name="l0_in_proj")
    ob, sb_tot, *gathered = _sb_fwd(p0, pad, name="sb_fwd", gather=hooks["late_gather"] if hooks else None)
    if hooks:
        wts = {**wts, **hooks["later_weights"](gathered)}
    qkv = _gdn_pre_fwd(p0, wts["conv_w"], pad, name="gdn_pre_fwd")
    oa_raw, gdn_states = _gdn_fwd(qkv, p0, wts["alog_v"], wts["dtb_v"], pad, name="gdn_fwd")
    oab = _gate_fwd(oa_raw, p0, AB_Z // HEAD_W, wts["ab_gn"], ob, heads=GDN_HEADS, name="gdn_gate_fwd")
    mix0 = mm(oab, wts["w_out0"], "NN", tm=tm, tn=512, tk=d, name="l0_out_proj")
    h0a, h0a_b = _ln_fwd(h0, mix0, wts["ln_mix_g"][0], wts["ln_mix_b"][0], name="ln_mix0_fwd")
    u0 = mm(h0a_b, wts["w1"][0], "NN", tm=tm, tn=512, tk=d, b_dev=True, name="mlp0_up")
    y0 = mm(u0, wts["w2"][0], "NN", tm=tm, tn=512, tk=d, a_fn="relu2", name="mlp0_down")
    h0b, h0b_b = _ln_fwd(h0a, y0, wts["ln_ffn_g"][0], wts["ln_ffn_b"][0], name="ln_ffn0_fwd")
    p1 = mm(h0b_b, wts["w_c"], "NN", tm=tm, tn=512, tk=d, b_dev=True, name="l1_in_proj")
    oc_raw, hg_states = _hg_fwd(p1, wts["lb"], pad, name="hg_fwd")
    oc = _gate_fwd(oc_raw, p1, 3 * HG_HEADS, wts["c_gn"], oc_raw, heads=HG_HEADS, name="hg_gate_fwd")
    mix1 = mm(oc, wts["w_out1"], "NN", tm=tm, tn=512, tk=d, name="l1_out_proj")
    h1a, h1a_b = _ln_fwd(h0b, mix1, wts["ln_mix_g"][1], wts["ln_mix_b"][1], name="ln_mix1_fwd")
    u1 = mm(h1a_b, wts["w1"][1], "NN", tm=tm, tn=512, tk=d, b_dev=True, name="mlp1_up")
    y1 = mm(u1, wts["w2"][1], "NN", tm=tm, tn=512, tk=d, a_fn="relu2", name="mlp1_down")
    h1b, _ = _ln_fwd(h1a, y1, wts["ln_ffn_g"][1], wts["ln_ffn_b"][1], name="ln_ffn1_fwd")
    dy, loss_vec = _loss_head(h1b, target, name="loss_head")

    def mlp_bwd(layer, h_in_b, u, dpre, dpre_b):
        du = mm(dpre_b, wts["w2"][layer], "NT", tm=tm, tn=512, tk=d, epi="relu2grad", c=u, out_dtype=BF16,
                name=f"mlp{layer}_d_hidden")
        dw2 = mm(u, dpre_b, "TN", tm=1024, tn=1024, tk=tkl, a_fn="relu2", name=f"mlp{layer}_dw2")
        dw1 = mm(h_in_b, du, "TN", tm=1024, tn=512, tk=tkl, out_dev=True, name=f"mlp{layer}_dw1")
        dh = mm(du, wts["w1"][layer], "NT", tm=tm, tn=1024, tk=512, b_dev=True, epi="add", c=dpre, scale=DN_ALPHA,
                name=f"mlp{layer}_d_in")
        return dh, dw1, dw2

    ln_ffn_dg, ln_ffn_db, ln_mix_dg, ln_mix_db, dw1s, dw2s = ([None, None] for _ in range(6))
    dpre, dpre_b, ln_ffn_dg[1], ln_ffn_db[1] = _ln_bwd(h1a, y1, wts["ln_ffn_g"][1], dy, name="ln_ffn1_bwd")
    dh1a, dw1s[1], dw2s[1] = mlp_bwd(1, h1a_b, u1, dpre, dpre_b)
    dpre, dpre_b, ln_mix_dg[1], ln_mix_db[1] = _ln_bwd(h0b, mix1, wts["ln_mix_g"][1], dh1a, name="ln_mix1_bwd")
    g["c_w_out"] = mm(oc, dpre_b, "TN", tm=1024, tn=1024, tk=tkl, name="l1_dw_out")
    doc = mm(dpre_b, wts["w_out1"], "NT", tm=tm, tn=512, tk=d, name="l1_d_gate")
    doc_raw, dz1, g["c_gn"] = _gate_bwd(oc_raw, p1, 3 * HG_HEADS, wts["c_gn"], doc, heads=HG_HEADS, name="hg_gate_bwd")
    dq1, df1, di1, g["lb"] = _hg_bwd(p1, wts["lb"], hg_states, doc_raw, pad, name="hg_bwd")
    dp1 = jnp.concatenate([dq1, df1, di1, dz1], axis=1).astype(BF16)
    g["c_w_in"] = mm(h0b_b, dp1, "TN", tm=1024, tn=512, tk=tkl, out_dev=True, name="l1_dw_in")
    dh0b = mm(dp1, wts["w_c"], "NT", tm=tm, tn=1024, tk=512, b_dev=True, epi="add", c=dpre, scale=DN_ALPHA,
              name="l1_d_in")
    dpre, dpre_b, ln_ffn_dg[0], ln_ffn_db[0] = _ln_bwd(h0a, y0, wts["ln_ffn_g"][0], dh0b, name="ln_ffn0_bwd")
    dh0a, dw1s[0], dw2s[0] = mlp_bwd(0, h0a_b, u0, dpre, dpre_b)
    dpre, dpre_b, ln_mix_dg[0], ln_mix_db[0] = _ln_bwd(h0, mix0, wts["ln_mix_g"][0], dh0a, name="ln_mix0_bwd")
    g["ab_w_out"] = mm(oab, dpre_b, "TN", tm=1024, tn=1024, tk=tkl, name="l0_dw_out")
    doab = mm(dpre_b, wts["w_out0"], "NT", tm=tm, tn=512, tk=d, name="l0_d_gate")
    doa_raw, dz0, g["ab_gn"] = _gate_bwd(oa_raw, p0, AB_Z // HEAD_W, wts["ab_gn"], doab, heads=GDN_HEADS,
                                         name="gdn_gate_bwd")
    l1_sums = ()
    if hooks:
        l1_sums = hooks["chip_sums"]([g["c_w_in"], g["c_w_out"].reshape(N_DEV, d // N_DEV, d), dw1s[1],
                                     dw2s[1].reshape(N_DEV, D_FF // N_DEV, d)])
    dqb, dkb_t, dvb_t, *g["l1_parts"] = _sb_bwd(p0, sb_tot, doab, GDN_HEADS, pad, name="sb_bwd", chip_sums=l1_sums)
    dkb, dvb = dkb_t.T, dvb_t.T
    dqn, dkn, dvn, dba, g["alog_v"], g["dtb_v"] = _gdn_bwd(qkv, p0, wts["alog_v"], wts["dtb_v"], gdn_states, doa_raw,
                                                           pad, name="gdn_bwd")
    dconv_in, g["conv_w"] = _gdn_pre_bwd(p0, wts["conv_w"], jnp.concatenate([dqn, dkn, dvn], axis=1), pad,
                                         name="gdn_pre_bwd")
    dp0 = jnp.concatenate([dconv_in, dz0, dqb, dkb, dvb, dba, jnp.zeros((lp, AB_CAT - AB_BA - HEAD_W), F32)],
                          axis=1).astype(BF16)
    g["w_ab"] = mm(h0_b, dp0, "TN", tm=1024, tn=768, tk=tkl, name="l0_dw_in")
    dh0 = mm(dp0, wts["w_ab"], "NT", tm=tm, tn=1024, tk=768, epi="add", c=dpre, scale=DN_ALPHA, name="l0_d_in")

    g["w1"], g["w2"] = dw1s, dw2s
    g["ln_mix_g"] = jnp.concatenate(ln_mix_dg, axis=0)
    g["ln_mix_b"] = jnp.concatenate(ln_mix_db, axis=0)
    g["ln_ffn_g"] = jnp.concatenate(ln_ffn_dg, axis=0)
    g["ln_ffn_b"] = jnp.concatenate(ln_ffn_db, axis=0)
    return loss_vec, dh0, g


N_CHIP = N_DEV // 2


def _place():
    x, y, c = lax.axis_index("x"), lax.axis_index("y"), lax.axis_index("c")
    return x, y, c, 2 * x + y


def _chip_dev(chip, core):
    return (chip // 2, chip % 2, core)


def _remote(src, dst, send_sem, recv_sem, dev):
    return pltpu.make_async_remote_copy(src_ref=src, dst_ref=dst, send_sem=send_sem, recv_sem=recv_sem,
                                        device_id=dev, device_id_type=pl.DeviceIdType.MESH)


_ANY = pl.BlockSpec(memory_space=pl.ANY)


def _gather(srcs, dtypes, *, name):
    n = len(srcs)

    def body(*refs):
        start, forward, finish = _gather_phases(refs[:n], refs[n:2 * n], refs[2 * n:3 * n], *refs[3 * n:], dtypes)
        start()
        forward()
        finish()

    return pl.pallas_call(
        body, name=name, in_specs=[pl.BlockSpec(memory_space=pltpu.VMEM)] * n, out_specs=[_ANY] * n,
        out_shape=_gather_out_shapes(srcs, dtypes), scratch_shapes=_gather_scratch(srcs, dtypes),
        compiler_params=_cp(has_side_effects=True),
    )(*srcs)


def _gather_out_shapes(srcs, dtypes):
    return [jax.ShapeDtypeStruct((N_DEV, *s.shape), dt) for s, dt in zip(srcs, dtypes)]


def _gather_scratch(srcs, dtypes):
    n = len(srcs)
    return [pltpu.VMEM(s.shape, dt) for s, dt in zip(srcs, dtypes)] + [
        pltpu.SemaphoreType.DMA((n, 2 * N_CHIP - 1)), pltpu.SemaphoreType.DMA((n, 2 * N_CHIP - 1)),
        pltpu.SemaphoreType.DMA((n,))]


def _gather_phases(ins, outs, stages, send_sems, recv_sems, local_sems, dtypes):
    n = len(ins)
    x, y, c, chip = _place()
    me = 2 * chip + c
    sibling = (x, y, 1 - c)

    def own(i):
        cps = [_remote(stages[i], outs[i].at[me], send_sems.at[i, 0], recv_sems.at[i, 0], sibling)]
        for j in range(1, N_CHIP):
            cps.append(_remote(stages[i], outs[i].at[me], send_sems.at[i, j], recv_sems.at[i, j],
                               _chip_dev(jnp.bitwise_xor(chip, j), c)))
        return cps

    def local(i):
        return pltpu.make_async_copy(stages[i], outs[i].at[me], local_sems.at[i])

    def passed_on(i, j):
        slot = outs[i].at[2 * jnp.bitwise_xor(chip, j) + c]
        return _remote(slot, slot, send_sems.at[i, N_CHIP - 1 + j], recv_sems.at[i, N_CHIP - 1 + j], sibling)

    def start():
        for i in range(n):
            stages[i][...] = ins[i][...].astype(dtypes[i])
            local(i).start()
            for cp in own(i):
                cp.start()

    def forward():
        for i in range(n):
            for j in range(1, N_CHIP):
                own(i)[j].wait_recv()
                passed_on(i, j).start()

    def finish():
        for i in range(n):
            own(i)[0].wait_recv()
            for j in range(1, N_CHIP):
                passed_on(i, j).wait_recv()
        for i in range(n):
            for cp in own(i):
                cp.wait_send()
            for j in range(1, N_CHIP):
                passed_on(i, j).wait_send()
            local(i).wait()

    return start, forward, finish


def _to_sibling(parts, *, name):
    n = len(parts)

    def body(*refs):
        ins, outs = refs[:n], refs[n:2 * n]
        send_sems, recv_sems = refs[2 * n:]
        x, y, c, _ = _place()
        copies = [_remote(ins[i].at[2 * k + (1 - c)], outs[i].at[k], send_sems.at[i, k], recv_sems.at[i, k],
                          (x, y, 1 - c)) for i in range(n) for k in range(N_CHIP)]
        for cp in copies:
            cp.start()
        for cp in copies:
            cp.wait()

    return pl.pallas_call(
        body, name=name, in_specs=[_ANY] * n, out_specs=[_ANY] * n,
        out_shape=[jax.ShapeDtypeStruct((N_CHIP, *p.shape[1:]), p.dtype) for p in parts],
        scratch_shapes=[pltpu.SemaphoreType.DMA((n, N_CHIP)), pltpu.SemaphoreType.DMA((n, N_CHIP))],
        compiler_params=_cp(has_side_effects=True),
    )(*parts)


def _pair_sum(part, from_sibling, core, *, name):
    _, r, c = part.shape
    tm = _row_tile(r, 128)

    def body(core_ref, a_ref, b_ref, o_ref):
        o_ref[...] = (a_ref[...] + b_ref[...]).astype(BF16)

    return pl.pallas_call(
        body, name=name,
        grid_spec=pltpu.PrefetchScalarGridSpec(
            num_scalar_prefetch=1, grid=(N_CHIP, r // tm),
            in_specs=[pl.BlockSpec((None, tm, c), lambda k, i, core_ref: (2 * k + core_ref[0], i, 0)),
                      pl.BlockSpec((None, tm, c), lambda k, i, core_ref: (k, i, 0))],
            out_specs=pl.BlockSpec((None, tm, c), lambda k, i, core_ref: (k, i, 0))),
        out_shape=jax.ShapeDtypeStruct((N_CHIP, r, c), BF16), compiler_params=_cp(("parallel", "parallel")),
    )(core, part, from_sibling)


def _to_chips(sums, *, name):
    n = len(sums)

    def body(*refs):
        start, finish = _to_chips_phases(refs[:n], refs[n:2 * n], *refs[2 * n:])
        start()
        finish()

    return pl.pallas_call(
        body, name=name, in_specs=[_ANY] * n, out_specs=[_ANY] * n,
        out_shape=[jax.ShapeDtypeStruct(s.shape, s.dtype) for s in sums], scratch_shapes=_to_chips_scratch(n),
        compiler_params=_cp(has_side_effects=True),
    )(*sums)


def _to_chips_scratch(n):
    return [pltpu.SemaphoreType.DMA((n, N_CHIP - 1)), pltpu.SemaphoreType.DMA((n, N_CHIP - 1)),
            pltpu.SemaphoreType.DMA((n,))]


def _to_chips_phases(ins, outs, send_sems, recv_sems, local_sems):
    n = len(ins)
    _, _, c, chip = _place()

    def copies():
        cps = []
        for i in range(n):
            cps.append(pltpu.make_async_copy(ins[i].at[chip], outs[i].at[chip], local_sems.at[i]))
            for j in range(1, N_CHIP):
                other = jnp.bitwise_xor(chip, j)
                cps.append(_remote(ins[i].at[other], outs[i].at[chip], send_sems.at[i, j - 1], recv_sems.at[i, j - 1],
                                   _chip_dev(other, c)))
        return cps

    def start():
        for cp in copies():
            cp.start()

    def finish():
        for cp in copies():
            cp.wait()

    return start, finish


def _adamw(w, parts, m, v, *, name):
    r, c = w.shape
    s = parts.shape[0]
    tm = _row_tile(r, 128) if r % 8 == 0 else r
    c1 = 1.0 - ADAM_B1 ** ADAM_STEP
    c2 = 1.0 - ADAM_B2 ** ADAM_STEP

    def body(w_ref, p_ref, m_ref, v_ref, g_ref, d_ref, m2_ref, v2_ref):
        g = p_ref[0].astype(F32)
        for j in range(1, s):
            g = g + p_ref[j].astype(F32)
        m2 = ADAM_B1 * m_ref[...] + (1.0 - ADAM_B1) * g
        v2 = ADAM_B2 * v_ref[...] + (1.0 - ADAM_B2) * jnp.square(g)
        g_ref[...] = g
        m2_ref[...] = m2
        v2_ref[...] = v2
        d_ref[...] = -ADAM_LR * ((m2 / c1) / (jnp.sqrt(v2 / c2) + ADAM_EPS) + ADAM_WD * w_ref[...])

    blk = pl.BlockSpec((tm, c), lambda i: (i, 0))
    return pl.pallas_call(
        body, name=name, grid=(r // tm,),
        in_specs=[blk, pl.BlockSpec((s, tm, c), lambda i: (0, i, 0)), blk, blk], out_specs=[blk] * 4,
        out_shape=[jax.ShapeDtypeStruct((r, c), F32)] * 4, compiler_params=_cp(("parallel",)),
    )(w, parts, m, v)


_WEIGHTS = ("meta_tokens", "ab_w_in", "ab_conv_w", "ab_a_log", "ab_dt_bias", "ab_gnorm_g", "ab_w_out", "c_w_in",
            "c_lb_raw", "c_gnorm_g", "c_w_out", "ln_mix_g", "ln_mix_b", "mlp_w1", "mlp_w2", "ln_ffn_g", "ln_ffn_b")
_PACK_ROWS = (("ln_mix_g", 0), ("ln_mix_b", 2), ("ln_ffn_g", 4), ("ln_ffn_b", 6), ("c_lb_raw", 8))
_PACK_MISC_ROW = 10
_PACK_MISC = (("ab_gnorm_g", 0, 128), ("c_gnorm_g", 128, 128), ("ab_a_log", 256, GDN_HEADS), ("ab_dt_bias", 260, GDN_HEADS))
_PACK_N = 16
_SMALL_META = 16
_SMALL_CONV = 32
_SMALL_N = 40


def _pack_replicated(p):
    rows = jnp.zeros((_PACK_N, D_MODEL), F32)
    for name, r0 in _PACK_ROWS:
        rows = rows.at[r0:r0 + 2].set(p[name])
    for name, c0, width in _PACK_MISC:
        rows = rows.at[_PACK_MISC_ROW, c0:c0 + width].set(p[name].reshape(width))
    return rows


def _unpack_replicated(rows, like):
    out = {}
    for name, r0 in _PACK_ROWS:
        out[name] = rows[r0:r0 + 2]
    for name, c0, width in _PACK_MISC:
        out[name] = rows[_PACK_MISC_ROW, c0:c0 + width].reshape(like[name].shape)
    return out


def _lower_bound(c_lb_raw):
    lb_all = jnp.cumsum(jax.nn.softmax(c_lb_raw.astype(F32), axis=0), axis=0)
    return (lb_all - lb_all[0:1])[1].reshape(1, -1)


def kernel(x, meta_tokens, ab_w_in, ab_conv_w, ab_a_log, ab_dt_bias, ab_gnorm_g, ab_w_out, c_w_in, c_lb_raw, c_gnorm_g, c_w_out, ln_mix_g, ln_mix_b, mlp_w1, mlp_w2, ln_ffn_g, ln_ffn_b, loss_target, m_meta_tokens, m_ab_w_in, m_ab_conv_w, m_ab_a_log, m_ab_dt_bias, m_ab_gnorm_g, m_ab_w_out, m_c_w_in, m_c_lb_raw, m_c_gnorm_g, m_c_w_out, m_ln_mix_g, m_ln_mix_b, m_mlp_w1, m_mlp_w2, m_ln_ffn_g, m_ln_ffn_b, v_meta_tokens, v_ab_w_in, v_ab_conv_w, v_ab_a_log, v_ab_dt_bias, v_ab_gnorm_g, v_ab_w_out, v_c_w_in, v_c_lb_raw, v_c_gnorm_g, v_c_w_out, v_ln_mix_g, v_ln_mix_b, v_mlp_w1, v_mlp_w2, v_ln_ffn_g, v_ln_ffn_b):
    w = dict(zip(_WEIGHTS, (meta_tokens, ab_w_in, ab_conv_w, ab_a_log, ab_dt_bias, ab_gnorm_g, ab_w_out, c_w_in, c_lb_raw,
                            c_gnorm_g, c_w_out, ln_mix_g, ln_mix_b, mlp_w1, mlp_w2, ln_ffn_g, ln_ffn_b)))
    mom = dict(zip(_WEIGHTS, (m_meta_tokens, m_ab_w_in, m_ab_conv_w, m_ab_a_log, m_ab_dt_bias, m_ab_gnorm_g, m_ab_w_out,
                              m_c_w_in, m_c_lb_raw, m_c_gnorm_g, m_c_w_out, m_ln_mix_g, m_ln_mix_b, m_mlp_w1, m_mlp_w2,
                              m_ln_ffn_g, m_ln_ffn_b)))
    var = dict(zip(_WEIGHTS, (v_meta_tokens, v_ab_w_in, v_ab_conv_w, v_ab_a_log, v_ab_dt_bias, v_ab_gnorm_g, v_ab_w_out,
                              v_c_w_in, v_c_lb_raw, v_c_gnorm_g, v_c_w_out, v_ln_mix_g, v_ln_mix_b, v_mlp_w1, v_mlp_w2,
                              v_ln_ffn_g, v_ln_ffn_b)))
    me = 4 * lax.axis_index("x") + 2 * lax.axis_index("y") + lax.axis_index("c")
    seq = x.shape[1]
    pad = (-(N_META + seq)) % SB_BLOCK
    lp = pad + N_META + seq
    meta_w = D_MODEL // N_DEV
    conv_w_all = 2 * GDN_HEADS * HEAD_W + GDN_HEADS * HEAD_W
    conv_w_mine = conv_w_all // N_DEV

    g_meta, g_conv, g_ab_in = _gather([w["meta_tokens"], w["ab_conv_w"][0], w["ab_w_in"][0]], [F32, F32, BF16],
                                      name="gather_weights_first")
    meta_full = g_meta.transpose(1, 0, 2).reshape(N_META, D_MODEL)
    conv_full = g_conv.transpose(1, 0, 2).reshape(CONV_K, conv_w_all)
    ab_full = g_ab_in.transpose(1, 0, 2).reshape(D_MODEL, AB_IN)
    ba0 = AB_Z + 512
    w_ab = jnp.concatenate([ab_full[:, :ba0], ab_full[:, ba0 + 2 * GDN_HEADS:], ab_full[:, ba0:ba0 + 2 * GDN_HEADS],
                            jnp.zeros((D_MODEL, AB_CAT - AB_IN), BF16)], axis=1)
    vec128 = lambda p: jnp.zeros((1, HEAD_W), F32).at[0, :GDN_HEADS].set(p.reshape(GDN_HEADS))
    wts = dict(
        w_ab=w_ab, conv_w=conv_full, alog_v=vec128(w["ab_a_log"]), dtb_v=vec128(w["ab_dt_bias"]),
        ab_gn=w["ab_gnorm_g"][0], lb=_lower_bound(w["c_lb_raw"]), c_gn=w["c_gnorm_g"][0],
        ln_mix_g=w["ln_mix_g"], ln_mix_b=w["ln_mix_b"], ln_ffn_g=w["ln_ffn_g"], ln_ffn_b=w["ln_ffn_b"])

    def later_weights(gathered):
        g_ab_out, g_c_in, g_c_out, g_w1, g_w2 = gathered
        return dict(w_out0=g_ab_out.reshape(D_MODEL, D_MODEL), w_c=g_c_in, w_out1=g_c_out.reshape(D_MODEL, D_MODEL),
                    w1=[g_w1[:, l] for l in range(DEPTH)], w2=[g_w2[:, l].reshape(D_FF, D_MODEL) for l in range(DEPTH)])

    core = lax.axis_index("c").astype(jnp.int32).reshape(1)

    def chip_sums(tag, grads):
        from_sibling = _to_sibling(grads, name=f"scatter_{tag}_d2d")
        return [_pair_sum(a, s, core, name=f"chip_sum_{tag}_{i}") for i, (a, s) in enumerate(zip(grads, from_sibling))]

    hooks = dict(
        late_gather=([w["ab_w_out"][0], w["c_w_in"][0], w["c_w_out"][0], w["mlp_w1"], w["mlp_w2"]], [BF16] * 5),
        later_weights=later_weights, chip_sums=functools.partial(chip_sums, "l1"))

    h0 = jnp.concatenate([jnp.zeros((pad, D_MODEL), F32), meta_full, x[0]], axis=0)
    loss_vec, dh0, g = _local_step(h0, loss_target[0], pad, wts, hooks)
    loss = lax.psum(jnp.sum(loss_vec), ("x", "y", "c"))
    grad_x = dh0[lp - seq:][None]

    _, lb_vjp = jax.vjp(_lower_bound, w["c_lb_raw"])
    rep_part = _pack_replicated(dict(
        ln_mix_g=g["ln_mix_g"], ln_mix_b=g["ln_mix_b"], ln_ffn_g=g["ln_ffn_g"], ln_ffn_b=g["ln_ffn_b"],
        c_lb_raw=lb_vjp(g["lb"])[0], ab_gnorm_g=g["ab_gn"], c_gnorm_g=g["c_gn"],
        ab_a_log=g["alog_v"][0, :GDN_HEADS], ab_dt_bias=g["dtb_v"][0, :GDN_HEADS]))
    small = jnp.concatenate([rep_part, dh0[pad:pad + N_META], g["conv_w"].reshape(-1, D_MODEL),
                             jnp.zeros((_SMALL_N - _SMALL_CONV - CONV_K * conv_w_all // D_MODEL, D_MODEL), F32)], axis=0)
    (small_all,) = _gather([small], [F32], name="gather_small_grads")
    rep_out = _adamw(_pack_replicated(w), small_all[:, :_PACK_N], _pack_replicated(mom), _pack_replicated(var),
                     name="adamw_replicated")
    meta_parts = lax.dynamic_slice_in_dim(small_all[:, _SMALL_META:_SMALL_META + N_META], me * meta_w, meta_w, axis=2)
    meta_out = _adamw(w["meta_tokens"], meta_parts, mom["meta_tokens"], var["meta_tokens"], name="adamw_meta")
    conv_parts = small_all[:, _SMALL_CONV:_SMALL_CONV + CONV_K * conv_w_all // D_MODEL].reshape(N_DEV, CONV_K, conv_w_all)
    conv_parts = lax.dynamic_slice_in_dim(conv_parts, me * conv_w_mine, conv_w_mine, axis=2)
    conv_out = _adamw(w["ab_conv_w"][0], conv_parts, mom["ab_conv_w"][0], var["ab_conv_w"][0], name="adamw_conv")

    gab = g["w_ab"]
    gab = jnp.concatenate([gab[:, :ba0], gab[:, AB_BA:AB_BA + 2 * GDN_HEADS], gab[:, ba0:AB_BA]], axis=1)
    l0_grads = [gab.reshape(D_MODEL, N_DEV, AB_IN // N_DEV).transpose(1, 0, 2),
                g["ab_w_out"].reshape(N_DEV, D_MODEL // N_DEV, D_MODEL), g["w1"][0],
                g["w2"][0].reshape(N_DEV, D_FF // N_DEV, D_MODEL)]
    l0_parts = _to_chips(chip_sums("l0", l0_grads), name="scatter_l0_ici")
    big = [("ab_w_in", 0, l0_parts[0]), ("ab_w_out", 0, l0_parts[1]), ("mlp_w1", 0, l0_parts[2]),
           ("mlp_w2", 0, l0_parts[3]), ("c_w_in", 0, g["l1_parts"][0]), ("c_w_out", 0, g["l1_parts"][1]),
           ("mlp_w1", 1, g["l1_parts"][2]), ("mlp_w2", 1, g["l1_parts"][3])]
    big_out = {}
    for name, l, p in big:
        res = _adamw(w[name][l], p, mom[name][l], var[name][l], name=f"adamw_{name}{l}")
        big_out.setdefault(name, []).append(res)

    rep = [_unpack_replicated(r, w) for r in rep_out]
    outs = {}
    for name in _WEIGHTS:
        if name == "meta_tokens":
            outs[name] = list(meta_out)
        elif name == "ab_conv_w":
            outs[name] = [o[None] for o in conv_out]
        elif name in big_out:
            res = big_out[name]
            outs[name] = [o[None] for o in res[0]] if len(res) == 1 else [jnp.stack(pair) for pair in zip(*res)]
        else:
            outs[name] = [r[name] for r in rep]
    flat = [loss, grad_x]
    for kind in range(4):
        flat += [outs[name][kind] for name in _WEIGHTS]
    return tuple(flat)
```

```python
import functools
import math

import jax
import jax.numpy as jnp
from jax import lax
from jax.experimental import pallas as pl
from jax.experimental.pallas import tpu as pltpu

F32 = jnp.float32
BF16 = jnp.bfloat16
HI = lax.Precision.HIGHEST

N_DEV = 8
D_MODEL = 1024
N_META = 16
D_FF = 4096
DEPTH = 2
GDN_HEADS = 4
SB_HEADS = 8
SB_DH = 64
HG_HEADS = 8
HEAD_W = 128
CHUNK = 64
SB_BLOCK = 128
CONV_K = 4
DN_ALPHA = float((2 * DEPTH) ** 0.25)
LN_EPS = 1e-5
RMS_EPS = 1e-6
L2_EPS = 1e-6
ADAM_LR, ADAM_B1, ADAM_B2, ADAM_EPS, ADAM_WD, ADAM_STEP = 0.001, 0.9, 0.999, 1e-08, 0.01, 10

AB_QKV = 0
AB_Z = 1536
AB_SB = 2048
AB_BA = 3584
AB_CAT = 3840
AB_IN = 3592

VMEM_LIMIT = 56 * 1024 * 1024


def _cp(sem=None, **kw):
    if sem is not None:
        kw["dimension_semantics"] = sem
    return pltpu.CompilerParams(vmem_limit_bytes=VMEM_LIMIT, **kw)


def _row_tile(n, want):
    best = 8
    for t in range(8, min(n, want) + 1, 8):
        if n % t == 0:
            best = t
    return best


@jax.custom_vjp
def _sigmoid(x):
    e = jnp.exp(-jnp.abs(x))
    r = 1.0 / (1.0 + e)
    return jnp.where(x >= 0, r, e * r)


def _sigmoid_fwd(x):
    s = _sigmoid(x)
    return s, s


def _sigmoid_bwd(s, g):
    return (g * s * (1.0 - s),)


_sigmoid.defvjp(_sigmoid_fwd, _sigmoid_bwd)


def _log1p_exp_neg_abs(x):
    e = jnp.exp(-jnp.abs(x))
    return jnp.where(e < 1e-4, e - 0.5 * e * e, jnp.log(1.0 + e))


@jax.custom_vjp
def _softplus(x):
    return jnp.maximum(x, 0.0) + _log1p_exp_neg_abs(x)


def _softplus_fwd(x):
    return _softplus(x), x


def _softplus_bwd(x, g):
    return (g * _sigmoid(x),)


_softplus.defvjp(_softplus_fwd, _softplus_bwd)


def _silu(x):
    return x * _sigmoid(x)


def _silu_grad(x):
    s = _sigmoid(x)
    return s * (1.0 + x * (1.0 - s))


def _dot(a, b, dims, precision=None):
    return lax.dot_general(a, b, (dims, ((), ())), precision=precision, preferred_element_type=F32)


NN = ((1,), (0,))
NT = ((1,), (1,))
TN = ((0,), (0,))


def _bdot(a, b, dims):
    return _dot(a.astype(BF16), b.astype(BF16), dims)


def _mm(a, b, mode, *, tm, tn, tk, name, a_fn=None, epi=None, c=None, scale=1.0, b_dev=False, out_dev=False,
        out_dtype=F32):
    if mode == "NN":
        m, kk = a.shape
        n = b.shape[2] * N_DEV if b_dev else b.shape[1]
    elif mode == "NT":
        m, kk = a.shape
        n = b.shape[1] if b_dev else b.shape[0]
    else:
        kk, m = a.shape
        n = b.shape[1]
    assert m % tm == 0 and n % tn == 0 and kk % tk == 0, (name, m, n, kk, tm, tn, tk)
    nk = kk // tk
    dims = {"NN": NN, "NT": NT, "TN": TN}[mode]

    if mode == "TN":
        a_spec = pl.BlockSpec((tk, tm), lambda i, j, k: (k, i))
    else:
        a_spec = pl.BlockSpec((tm, tk), lambda i, j, k: (i, k))
    if mode == "NN":
        if b_dev:
            assert tn == b.shape[2]
            b_spec = pl.BlockSpec((None, tk, tn), lambda i, j, k: (j, k, 0))
        else:
            b_spec = pl.BlockSpec((tk, tn), lambda i, j, k: (k, j))
    elif mode == "NT":
        if b_dev:
            assert tk == b.shape[2]
            b_spec = pl.BlockSpec((None, tn, tk), lambda i, j, k: (k, j, 0))
        else:
            b_spec = pl.BlockSpec((tn, tk), lambda i, j, k: (j, k))
    else:
        b_spec = pl.BlockSpec((tk, tn), lambda i, j, k: (k, j))
    in_specs = [a_spec, b_spec]
    operands = [a, b]
    if epi is not None:
        in_specs.append(pl.BlockSpec((tm, tn), lambda i, j, k: (i, j)))
        operands.append(c)
    if out_dev:
        assert tn == n // N_DEV
        out_shape = jax.ShapeDtypeStruct((N_DEV, m, tn), out_dtype)
        out_spec = pl.BlockSpec((None, tm, tn), lambda i, j, k: (j, i, 0))
    else:
        out_shape = jax.ShapeDtypeStruct((m, n), out_dtype)
        out_spec = pl.BlockSpec((tm, tn), lambda i, j, k: (i, j))

    def body(*refs):
        a_ref, b_ref = refs[0], refs[1]
        c_ref = refs[2] if epi is not None else None
        o_ref = refs[3] if epi is not None else refs[2]
        acc_ref = refs[-1] if nk > 1 else None
        av = a_ref[...]
        if a_fn == "relu2":
            av = jnp.square(jnp.maximum(av, 0.0))
        p = _dot(av.astype(BF16), b_ref[...].astype(BF16), dims)

        def finish(acc):
            if epi == "add":
                acc = acc + scale * c_ref[...]
            elif epi == "relu2grad":
                acc = acc * (2.0 * jnp.maximum(c_ref[...], 0.0))
            o_ref[...] = acc.astype(out_dtype)

        if nk == 1:
            finish(p)
        else:
            k = pl.program_id(2)

            @pl.when(k == 0)
            def _():
                acc_ref[...] = p

            @pl.when(k > 0)
            def _():
                acc_ref[...] += p

            @pl.when(k == nk - 1)
            def _():
                finish(acc_ref[...])

    return pl.pallas_call(
        body, name=name, grid=(m // tm, n // tn, nk), in_specs=in_specs, out_specs=out_spec, out_shape=out_shape,
        scratch_shapes=[pltpu.VMEM((tm, tn), F32)] if nk > 1 else [],
        compiler_params=_cp(("parallel", "parallel", "arbitrary")),
    )(*operands)


def _ln_fwd(a, b, g, beta, *, name):
    lp, d = a.shape
    tm = _row_tile(lp, 512)

    def body(a_ref, b_ref, g_ref, be_ref, y_ref, yb_ref):
        pre = DN_ALPHA * a_ref[...] + b_ref[...]
        mu = jnp.mean(pre, axis=-1, keepdims=True)
        xc = pre - mu
        var = jnp.mean(xc * xc, axis=-1, keepdims=True)
        y = xc * lax.rsqrt(var + LN_EPS) * g_ref[...] + be_ref[...]
        y_ref[...] = y
        yb_ref[...] = y.astype(BF16)

    row = pl.BlockSpec((tm, d), lambda i: (i, 0))
    vec = pl.BlockSpec((1, d), lambda i: (0, 0))
    return pl.pallas_call(
        body, name=name, grid=(lp // tm,), in_specs=[row, row, vec, vec], out_specs=[row, row],
        out_shape=[jax.ShapeDtypeStruct((lp, d), F32), jax.ShapeDtypeStruct((lp, d), BF16)],
        compiler_params=_cp(("parallel",)),
    )(a, b, g.reshape(1, d), beta.reshape(1, d))


def _ln_bwd(a, b, g, dy, *, name):
    lp, d = a.shape
    tm = _row_tile(lp, 512)

    def body(a_ref, b_ref, g_ref, dy_ref, dpre_ref, dpreb_ref, dg_ref, db_ref):
        pre = DN_ALPHA * a_ref[...] + b_ref[...]
        mu = jnp.mean(pre, axis=-1, keepdims=True)
        xc = pre - mu
        var = jnp.mean(xc * xc, axis=-1, keepdims=True)
        rstd = lax.rsqrt(var + LN_EPS)
        xhat = xc * rstd
        dyv = dy_ref[...]
        dxh = dyv * g_ref[...]
        m1 = jnp.mean(dxh, axis=-1, keepdims=True)
        m2 = jnp.mean(dxh * xhat, axis=-1, keepdims=True)
        dpre = rstd * (dxh - m1 - xhat * m2)
        dpre_ref[...] = dpre
        dpreb_ref[...] = dpre.astype(BF16)

        @pl.when(pl.program_id(0) == 0)
        def _():
            dg_ref[...] = jnp.zeros_like(dg_ref)
            db_ref[...] = jnp.zeros_like(db_ref)

        dg_ref[...] += jnp.sum(dyv * xhat, axis=0, keepdims=True)
        db_ref[...] += jnp.sum(dyv, axis=0, keepdims=True)

    row = pl.BlockSpec((tm, d), lambda i: (i, 0))
    vec = pl.BlockSpec((1, d), lambda i: (0, 0))
    return pl.pallas_call(
        body, name=name, grid=(lp // tm,), in_specs=[row, row, vec, row], out_specs=[row, row, vec, vec],
        out_shape=[jax.ShapeDtypeStruct((lp, d), F32), jax.ShapeDtypeStruct((lp, d), BF16),
                   jax.ShapeDtypeStruct((1, d), F32), jax.ShapeDtypeStruct((1, d), F32)],
        compiler_params=_cp(("arbitrary",)),
    )(a, b, g.reshape(1, d), dy)


def _loss_head(y, target, *, name):
    lp, d = y.shape
    seq = target.shape[0]
    tm = SB_BLOCK
    first = (lp - seq) // tm
    assert (lp - seq) % tm == 0 and seq % tm == 0

    def body(y_ref, t_ref, dy_ref, loss_ref):
        i = pl.program_id(0)
        live = i >= first
        diff = jnp.where(live, y_ref[...] - t_ref[...], 0.0)
        dy_ref[...] = diff * (1.0 / d)

        @pl.when(i == 0)
        def _():
            loss_ref[...] = jnp.zeros_like(loss_ref)

        loss_ref[...] += jnp.sum(diff * diff, axis=0, keepdims=True) * (0.5 / d)

    return pl.pallas_call(
        body, name=name, grid=(lp // tm,),
        in_specs=[pl.BlockSpec((tm, d), lambda i: (i, 0)),
                  pl.BlockSpec((tm, d), lambda i: (jnp.maximum(i - first, 0), 0))],
        out_specs=[pl.BlockSpec((tm, d), lambda i: (i, 0)), pl.BlockSpec((1, d), lambda i: (0, 0))],
        out_shape=[jax.ShapeDtypeStruct((lp, d), F32), jax.ShapeDtypeStruct((1, d), F32)],
        compiler_params=_cp(("arbitrary",)),
    )(y, target)


def _gate_fwd(o, zsrc, z_blk0, g, other, *, heads, name):
    lp = o.shape[0]
    tm = _row_tile(lp, 512)
    w = heads * HEAD_W
    assert (z_blk0 * HEAD_W) % w == 0
    has_other = w < D_MODEL

    def body(o_ref, z_ref, g_ref, *rest):
        y_ref = rest[-1]
        gv = g_ref[...]
        for h in range(heads):
            cs = slice(h * HEAD_W, (h + 1) * HEAD_W)
            ov = o_ref[:, cs]
            r = lax.rsqrt(jnp.mean(ov * ov, axis=-1, keepdims=True) + RMS_EPS)
            y_ref[:, cs] = (ov * r * gv * _silu(z_ref[:, cs])).astype(BF16)
        if has_other:
            y_ref[:, w:] = rest[0][...].astype(BF16)

    row = lambda width, blk: pl.BlockSpec((tm, width), lambda i: (i, blk))
    return pl.pallas_call(
        body, name=name, grid=(lp // tm,),
        in_specs=[row(w, 0), row(w, z_blk0 * HEAD_W // w), pl.BlockSpec((1, HEAD_W), lambda i: (0, 0))]
        + ([row(D_MODEL - w, 0)] if has_other else []),
        out_specs=row(D_MODEL, 0), out_shape=jax.ShapeDtypeStruct((lp, D_MODEL), BF16),
        compiler_params=_cp(("parallel",)),
    )(o, zsrc, g.reshape(1, HEAD_W), *([other] if has_other else []))


def _gate_bwd(o, zsrc, z_blk0, g, dy, *, heads, name):
    lp = o.shape[0]
    tm = _row_tile(lp, 512)

    w = heads * HEAD_W
    assert (z_blk0 * HEAD_W) % w == 0

    def body(o_ref, z_ref, g_ref, dy_ref, do_ref, dz_ref, dg_ref):
        @pl.when(pl.program_id(0) == 0)
        def _():
            dg_ref[...] = jnp.zeros_like(dg_ref)

        gv = g_ref[...]
        dg = jnp.zeros((1, HEAD_W), F32)
        for h in range(heads):
            cs = slice(h * HEAD_W, (h + 1) * HEAD_W)
            ov, zv, dyv = o_ref[:, cs], z_ref[:, cs], dy_ref[:, cs]
            r = lax.rsqrt(jnp.mean(ov * ov, axis=-1, keepdims=True) + RMS_EPS)
            nrm = ov * r
            s = _silu(zv)
            dn = dyv * gv * s
            do_ref[:, cs] = r * (dn - nrm * jnp.mean(dn * nrm, axis=-1, keepdims=True))
            dz_ref[:, cs] = dyv * nrm * gv * _silu_grad(zv)
            dg = dg + jnp.sum(dyv * nrm * s, axis=0, keepdims=True)
        dg_ref[...] += dg

    row = lambda blk: pl.BlockSpec((tm, w), lambda i: (i, blk))
    vec = pl.BlockSpec((1, HEAD_W), lambda i: (0, 0))
    return pl.pallas_call(
        body, name=name, grid=(lp // tm,),
        in_specs=[row(0), row(z_blk0 * HEAD_W // w), vec, row(0)], out_specs=[row(0), row(0), vec],
        out_shape=[jax.ShapeDtypeStruct((lp, w), F32), jax.ShapeDtypeStruct((lp, w), F32),
                   jax.ShapeDtypeStruct((1, HEAD_W), F32)],
        compiler_params=_cp(("arbitrary",)),
    )(o, zsrc, g.reshape(1, HEAD_W), dy)


def _conv_taps(x, w):
    acc = w[CONV_K - 1:CONV_K, :] * x
    for k in range(CONV_K - 1):
        acc = acc + w[k:k + 1, :] * pltpu.roll(x, CONV_K - 1 - k, 0)
    return acc


def _gdn_pre_fwd(p0, conv_w, pad, *, name):
    lp = p0.shape[0]
    nq = GDN_HEADS
    qscale = HEAD_W ** -0.5

    def body(x_ref, w_ref, y_ref):
        j = pl.program_id(0)
        c = _conv_taps(x_ref[...], w_ref[...])
        s = _silu(c)
        r = lax.rsqrt(jnp.sum(s * s, axis=-1, keepdims=True) + L2_EPS)
        mult = jnp.where(j < nq, r * qscale, jnp.where(j < 2 * nq, r, 1.0))
        rows = lax.broadcasted_iota(jnp.int32, (lp, 1), 0)
        y_ref[...] = jnp.where(rows >= pad, s * mult, 0.0)

    return pl.pallas_call(
        body, name=name, grid=(3 * nq,),
        in_specs=[pl.BlockSpec((lp, HEAD_W), lambda j: (0, j)), pl.BlockSpec((CONV_K, HEAD_W), lambda j: (0, j))],
        out_specs=pl.BlockSpec((lp, HEAD_W), lambda j: (0, j)),
        out_shape=jax.ShapeDtypeStruct((lp, 3 * nq * HEAD_W), F32), compiler_params=_cp(("parallel",)),
    )(p0, conv_w)


def _gdn_pre_bwd(p0, conv_w, dqkv, pad, *, name):
    lp = p0.shape[0]
    nq = GDN_HEADS
    qscale = HEAD_W ** -0.5

    def body(x_ref, w_ref, dy_ref, dx_ref, dw_ref):
        j = pl.program_id(0)
        x, w = x_ref[...], w_ref[...]
        c = _conv_taps(x, w)
        s = _silu(c)
        r = lax.rsqrt(jnp.sum(s * s, axis=-1, keepdims=True) + L2_EPS)
        rows = lax.broadcasted_iota(jnp.int32, (lp, 1), 0)
        dy = jnp.where(rows >= pad, dy_ref[...], 0.0)
        nrm = s * r
        dn = dy * jnp.where(j < nq, qscale, 1.0)
        ds_norm = r * (dn - nrm * jnp.sum(nrm * dn, axis=-1, keepdims=True))
        ds = jnp.where(j < 2 * nq, ds_norm, dy)
        dc = ds * _silu_grad(c)
        dx = w[CONV_K - 1:CONV_K, :] * dc
        dws = [None] * CONV_K
        dws[CONV_K - 1] = jnp.sum(dc * x, axis=0, keepdims=True)
        for k in range(CONV_K - 1):
            sh = CONV_K - 1 - k
            dx = dx + w[k:k + 1, :] * pltpu.roll(dc, lp - sh, 0)
            dws[k] = jnp.sum(dc * pltpu.roll(x, sh, 0), axis=0, keepdims=True)
        dx_ref[...] = dx
        dw_ref[...] = jnp.concatenate(dws, axis=0)

    blk = pl.BlockSpec((lp, HEAD_W), lambda j: (0, j))
    wblk = pl.BlockSpec((CONV_K, HEAD_W), lambda j: (0, j))
    return pl.pallas_call(
        body, name=name, grid=(3 * nq,), in_specs=[blk, wblk, blk], out_specs=[blk, wblk],
        out_shape=[jax.ShapeDtypeStruct((lp, 3 * nq * HEAD_W), F32),
                   jax.ShapeDtypeStruct((CONV_K, 3 * nq * HEAD_W), F32)],
        compiler_params=_cp(("parallel",)),
    )(p0, conv_w, dqkv)


def _tri(c, strict):
    r = lax.broadcasted_iota(jnp.int32, (c, c), 0)
    q = lax.broadcasted_iota(jnp.int32, (c, c), 1)
    return (q < r) if strict else (q <= r)


@jax.custom_vjp
def _inv_unit_lower(m):
    c = m.shape[0]
    eye = (lax.broadcasted_iota(jnp.int32, (c, c), 0) == lax.broadcasted_iota(jnp.int32, (c, c), 1)).astype(F32)
    x = eye - m
    p = m
    n = 2
    while n < CHUNK:
        p = _dot3x(p, p, NN)
        x = x + _dot3x(x, p, NN)
        n *= 2
    return x


def _inv_fwd(m):
    t = _inv_unit_lower(m)
    return t, t


def _inv_bwd(t, g):
    return (-_dot3x(_dot3x(t, g, TN), t, NT),)


_inv_unit_lower.defvjp(_inv_fwd, _inv_bwd)


def _heads_to_rows(x, nh):
    return jnp.concatenate([x[:, h * HEAD_W:(h + 1) * HEAD_W] for h in range(nh)], axis=0)


def _rows_to_heads(x, nh):
    c = x.shape[0] // nh
    return jnp.concatenate([x[h * c:(h + 1) * c] for h in range(nh)], axis=1)


def _gdn_chunk(q, k, v, ba, alog, dtb, states, valid):
    nh = GDN_HEADS
    c = q.shape[0]
    r = nh * c
    lane = lax.broadcasted_iota(jnp.int32, (1, HEAD_W), 1)
    pick = lambda x, l: jnp.sum(jnp.where(lane == l, x, 0.0), axis=-1, keepdims=True)
    beta = jnp.concatenate([jnp.where(valid, _sigmoid(pick(ba, h)), 0.0) for h in range(nh)], axis=0)
    g = jnp.concatenate(
        [jnp.where(valid, -jnp.exp(pick(alog, h)) * _softplus(pick(ba, nh + h) + pick(dtb, h)), 0.0) for h in range(nh)],
        axis=0)
    qs, ks, vs = _heads_to_rows(q, nh), _heads_to_rows(k, nh), _heads_to_rows(v, nh)
    rr = lax.broadcasted_iota(jnp.int32, (r, r), 0)
    cc = lax.broadcasted_iota(jnp.int32, (r, r), 1)
    same = (rr // c) == (cc // c)
    causal, strict = same & (cc <= rr), same & (cc < rr)
    lower = jnp.where(causal, 1.0, 0.0).astype(BF16)
    upper = jnp.where(same & (cc >= rr), 1.0, 0.0).astype(BF16)
    gcb = _mask_mm(lower, upper, g * jnp.ones((1, HEAD_W), F32))
    gc_col = jnp.concatenate([gcb] * (r // HEAD_W), axis=1)
    decay = jnp.where(causal, jnp.exp(jnp.minimum(gc_col - gc_col.T, 0.0)), 0.0)
    egc = jnp.exp(gcb)
    kb = ks * beta
    m = jnp.where(strict, _bdot(kb, ks, NT) * decay, 0.0)
    t = _inv_unit_lower(m)
    u = _bdot(t, vs * beta, NN)
    w = _bdot(t, kb * egc, NN)
    a = _bdot(qs, ks, NT) * decay
    rows = lambda x, h: x[h * c:(h + 1) * c]
    qe = qs * egc
    v_new = u - jnp.concatenate([_bdot(rows(w, h), states[h], NN) for h in range(nh)], axis=0)
    o = jnp.concatenate([_bdot(rows(qe, h), states[h], NN) for h in range(nh)], axis=0) + _bdot(a, v_new, NN)
    new_states = []
    for h in range(nh):
        gl = gcb[(h + 1) * c - 1:(h + 1) * c, :]
        k_dec = rows(ks, h) * jnp.exp(gl - rows(gcb, h))
        new_states.append(states[h] * jnp.exp(gl) + _bdot(k_dec, rows(v_new, h), TN))
    return _rows_to_heads(o, nh), new_states


def _gdn_fwd(qkv, p0, alog_v, dtb_v, pad, *, name):
    lp = qkv.shape[0]
    n = lp // CHUNK
    nh = GDN_HEADS

    def body(q_ref, k_ref, v_ref, ba_ref, al_ref, dt_ref, o_ref, st_ref, s_ref):
        i = pl.program_id(0)

        @pl.when(i == 0)
        def _():
            s_ref[...] = jnp.zeros_like(s_ref)

        valid = (i * CHUNK + lax.broadcasted_iota(jnp.int32, (CHUNK, 1), 0)) >= pad
        s = s_ref[...]
        o, s2 = _gdn_chunk(q_ref[...], k_ref[...], v_ref[...], ba_ref[...], al_ref[...], dt_ref[...],
                           [s[h] for h in range(nh)], valid)
        st_ref[...] = s
        o_ref[...] = o
        for h in range(nh):
            s_ref[h] = s2[h]

    w = nh * HEAD_W
    vec = pl.BlockSpec((1, HEAD_W), lambda i: (0, 0))
    return pl.pallas_call(
        body, name=name, grid=(n,),
        in_specs=[pl.BlockSpec((CHUNK, w), lambda i: (i, 0)), pl.BlockSpec((CHUNK, w), lambda i: (i, 1)),
                  pl.BlockSpec((CHUNK, w), lambda i: (i, 2)), pl.BlockSpec((CHUNK, HEAD_W), lambda i: (i, AB_BA // HEAD_W)),
                  vec, vec],
        out_specs=[pl.BlockSpec((CHUNK, w), lambda i: (i, 0)),
                   pl.BlockSpec((None, nh, HEAD_W, HEAD_W), lambda i: (i, 0, 0, 0))],
        out_shape=[jax.ShapeDtypeStruct((lp, w), F32), jax.ShapeDtypeStruct((n, nh, HEAD_W, HEAD_W), F32)],
        scratch_shapes=[pltpu.VMEM((nh, HEAD_W, HEAD_W), F32)],
        compiler_params=_cp(("arbitrary",)),
    )(qkv, qkv, qkv, p0, alog_v, dtb_v)


def _gdn_bwd(qkv, p0, alog_v, dtb_v, states, do, pad, *, name):
    lp = qkv.shape[0]
    n = lp // CHUNK
    nh = GDN_HEADS

    def body(q_ref, k_ref, v_ref, ba_ref, al_ref, dt_ref, st_ref, do_ref,
             dq_ref, dk_ref, dv_ref, dba_ref, dal_ref, ddt_ref, ds_ref):
        step = pl.program_id(0)
        i = n - 1 - step

        @pl.when(step == 0)
        def _():
            ds_ref[...] = jnp.zeros_like(ds_ref)
            dal_ref[...] = jnp.zeros_like(dal_ref)
            ddt_ref[...] = jnp.zeros_like(ddt_ref)

        valid = (i * CHUNK + lax.broadcasted_iota(jnp.int32, (CHUNK, 1), 0)) >= pad
        st, dst = st_ref[...], ds_ref[...]
        fn = functools.partial(_gdn_chunk, valid=valid)
        _, vjp = jax.vjp(fn, q_ref[...], k_ref[...], v_ref[...], ba_ref[...], al_ref[...], dt_ref[...],
                         [st[h] for h in range(nh)])
        dq, dk, dv, dba, dal, ddt, ds = vjp((do_ref[...], [dst[h] for h in range(nh)]))
        dq_ref[...] = dq
        dk_ref[...] = dk
        dv_ref[...] = dv
        dba_ref[...] = dba
        dal_ref[...] += dal
        ddt_ref[...] += ddt
        for h in range(nh):
            ds_ref[h] = ds[h]

    w = nh * HEAD_W
    rev = lambda c: (lambda s: (n - 1 - s, c))
    vec = pl.BlockSpec((1, HEAD_W), lambda s: (0, 0))
    dq, dk, dv, dba, dal, ddt = pl.pallas_call(
        body, name=name, grid=(n,),
        in_specs=[pl.BlockSpec((CHUNK, w), rev(0)), pl.BlockSpec((CHUNK, w), rev(1)), pl.BlockSpec((CHUNK, w), rev(2)),
                  pl.BlockSpec((CHUNK, HEAD_W), rev(AB_BA // HEAD_W)), vec, vec,
                  pl.BlockSpec((None, nh, HEAD_W, HEAD_W), lambda s: (n - 1 - s, 0, 0, 0)),
                  pl.BlockSpec((CHUNK, w), rev(0))],
        out_specs=[pl.BlockSpec((CHUNK, w), rev(0)), pl.BlockSpec((CHUNK, w), rev(0)), pl.BlockSpec((CHUNK, w), rev(0)),
                   pl.BlockSpec((CHUNK, HEAD_W), rev(0)), vec, vec],
        out_shape=[jax.ShapeDtypeStruct((lp, w), F32)] * 3 + [jax.ShapeDtypeStruct((lp, HEAD_W), F32)]
        + [jax.ShapeDtypeStruct((1, HEAD_W), F32)] * 2,
        scratch_shapes=[pltpu.VMEM((nh, HEAD_W, HEAD_W), F32)],
        compiler_params=_cp(("arbitrary",)),
    )(qkv, qkv, qkv, p0, alog_v, dtb_v, states, do)
    return dq, dk, dv, dba, dal, ddt


HG_LEVELS = (32, 16, 8, 4, 2, 1)
HG_GROUP = 4


def _hg_masks():
    import numpy as np
    c = CHUNK
    t = np.arange(c)[:, None]
    j = np.arange(c)[None, :]
    sums = [j <= t, j > t]
    pairs = [j == t]
    for m in HG_LEVELS:
        p = (t // (2 * m)) * (2 * m)
        r = p + m
        upper = t >= r
        sums.append(upper & (j > r) & (j <= t))
        sums.append(~upper & (j > t) & (j <= r))
        pairs.append(upper & (j < r) & (j >= p))
    sums = np.concatenate(sums, axis=0).astype(np.float32)
    pairs = np.concatenate([np.kron(np.eye(HG_GROUP), p) for p in pairs], axis=0).astype(np.float32)
    return jnp.asarray(sums, BF16), jnp.asarray(sums.T, BF16), jnp.asarray(pairs, F32)


def _split3(x):
    hi = x.astype(BF16)
    r1 = x - hi.astype(F32)
    mid = r1.astype(BF16)
    return hi, mid, (r1 - mid.astype(F32)).astype(BF16)


def _dot3x(a, b, dims):
    ah, am, _ = _split3(a)
    bh, bm, _ = _split3(b)
    return _dot(ah, bh, dims) + (_dot(ah, bm, dims) + _dot(am, bh, dims))


def _mask_mm_raw(m, x):
    return sum(_dot(m, part, NN) for part in _split3(x))


@jax.custom_vjp
def _mask_mm(m, mt, x):
    return _mask_mm_raw(m, x)


def _mask_mm_fwd(m, mt, x):
    return _mask_mm_raw(m, x), (m, mt)


def _mask_mm_bwd(res, g):
    m, mt = res
    return jnp.zeros_like(m), jnp.zeros_like(mt), _mask_mm_raw(mt, g)


_mask_mm.defvjp(_mask_mm_fwd, _mask_mm_bwd)


def _hg_chunk(qr, fr, ir, lb, states, valid, sums, sums_t, pairs):
    nh = HG_GROUP
    c = qr.shape[0]
    r = nh * c
    fg = lb + (1.0 - lb) * _sigmoid(fr)
    logf = jnp.where(valid, jnp.log(fg), 0.0)
    k = jnp.where(valid, 1.0 - fg, 0.0)
    qs = jnp.where(valid, _silu(qr), 0.0)
    v = jnp.where(valid, ir, 0.0)
    e = jnp.exp(_mask_mm(sums, sums_t, logf))
    blk = lambda n: e[n * c:(n + 1) * c]
    mask = lambda n: pairs[n * r:(n + 1) * r]
    stack = lambda x: _heads_to_rows(x, nh)
    a = mask(0) * _bdot(stack(qs), stack(k), NT)
    for lvl in range(len(HG_LEVELS)):
        a = a + mask(1 + lvl) * _bdot(stack(qs * blk(2 + 2 * lvl)), stack(k * blk(3 + 2 * lvl)), NT)
    av = _bdot(a, stack(v), NN)
    eb = blk(0)
    qe, kd = qs * eb, k * blk(1)
    outs, new_states = [], []
    for h in range(nh):
        cs = slice(h * HEAD_W, (h + 1) * HEAD_W)
        outs.append(_bdot(qe[:, cs], states[h], NT) + av[h * c:(h + 1) * c])
        new_states.append(states[h] * eb[c - 1:c, cs] + _bdot(v[:, cs], kd[:, cs], TN))
    return jnp.concatenate(outs, axis=1), new_states


def _hg_fwd(p1, lb, pad, *, name):
    lp = p1.shape[0]
    n = lp // CHUNK
    nh = HG_HEADS

    def body(q_ref, f_ref, i_ref, lb_ref, sums_ref, sums_t_ref, pairs_ref, o_ref, st_ref, s_ref):
        i = pl.program_id(1)

        @pl.when(i == 0)
        def _():
            s_ref[...] = jnp.zeros_like(s_ref)

        valid = (i * CHUNK + lax.broadcasted_iota(jnp.int32, (CHUNK, 1), 0)) >= pad
        s = s_ref[...]
        o, s2 = _hg_chunk(q_ref[...], f_ref[...], i_ref[...], lb_ref[...], [s[h] for h in range(grp)], valid,
                          sums_ref[...], sums_t_ref[...], pairs_ref[...])
        st_ref[...] = s
        o_ref[...] = o
        for h in range(grp):
            s_ref[h] = s2[h]

    masks = _hg_masks()
    grp, ngrp, gw = HG_GROUP, nh // HG_GROUP, HG_GROUP * HEAD_W
    blk = lambda off: pl.BlockSpec((CHUNK, gw), lambda h, i: (i, off + h))
    const = lambda a: pl.BlockSpec(a.shape, lambda h, i: (0, 0))
    return pl.pallas_call(
        body, name=name, grid=(ngrp, n),
        in_specs=[blk(0), blk(ngrp), blk(2 * ngrp), pl.BlockSpec((1, gw), lambda h, i: (0, h))]
        + [const(a) for a in masks],
        out_specs=[blk(0), pl.BlockSpec((grp, None, HEAD_W, HEAD_W), lambda h, i: (h, i, 0, 0))],
        out_shape=[jax.ShapeDtypeStruct((lp, nh * HEAD_W), F32), jax.ShapeDtypeStruct((nh, n, HEAD_W, HEAD_W), F32)],
        scratch_shapes=[pltpu.VMEM((grp, HEAD_W, HEAD_W), F32)],
        compiler_params=_cp(("parallel", "arbitrary")),
    )(p1, p1, p1, lb, *masks)


def _hg_bwd(p1, lb, states, do, pad, *, name):
    lp = p1.shape[0]
    n = lp // CHUNK
    nh = HG_HEADS

    def body(q_ref, f_ref, i_ref, lb_ref, st_ref, do_ref, sums_ref, sums_t_ref, pairs_ref,
             dq_ref, df_ref, di_ref, dlb_ref, ds_ref):
        step = pl.program_id(1)
        i = n - 1 - step

        @pl.when(step == 0)
        def _():
            ds_ref[...] = jnp.zeros_like(ds_ref)
            dlb_ref[...] = jnp.zeros_like(dlb_ref)

        valid = (i * CHUNK + lax.broadcasted_iota(jnp.int32, (CHUNK, 1), 0)) >= pad
        fn = functools.partial(_hg_chunk, valid=valid, sums=sums_ref[...], sums_t=sums_t_ref[...],
                               pairs=pairs_ref[...])
        st, dst = st_ref[...], ds_ref[...]
        _, vjp = jax.vjp(fn, q_ref[...], f_ref[...], i_ref[...], lb_ref[...], [st[h] for h in range(grp)])
        dq, df, di, dlb, ds = vjp((do_ref[...], [dst[h] for h in range(grp)]))
        dq_ref[...] = dq
        df_ref[...] = df
        di_ref[...] = di
        dlb_ref[...] += dlb
        for h in range(grp):
            ds_ref[h] = ds[h]

    masks = _hg_masks()
    grp, ngrp, gw = HG_GROUP, nh // HG_GROUP, HG_GROUP * HEAD_W
    blk = lambda off: pl.BlockSpec((CHUNK, gw), lambda h, s: (n - 1 - s, off + h))
    const = lambda a: pl.BlockSpec(a.shape, lambda h, s: (0, 0))
    w = nh * HEAD_W
    return pl.pallas_call(
        body, name=name, grid=(ngrp, n),
        in_specs=[blk(0), blk(ngrp), blk(2 * ngrp), pl.BlockSpec((1, gw), lambda h, s: (0, h)),
                  pl.BlockSpec((grp, None, HEAD_W, HEAD_W), lambda h, s: (h, n - 1 - s, 0, 0)), blk(0)]
        + [const(a) for a in masks],
        out_specs=[blk(0), blk(0), blk(0), pl.BlockSpec((1, gw), lambda h, s: (0, h))],
        out_shape=[jax.ShapeDtypeStruct((lp, w), F32)] * 3 + [jax.ShapeDtypeStruct((1, w), F32)],
        scratch_shapes=[pltpu.VMEM((grp, HEAD_W, HEAD_W), F32)],
        compiler_params=_cp(("parallel", "arbitrary")),
    )(p1, p1, p1, lb, states, do, *masks)


SB_GROUP = 4


def _sb_cat(kind, first_key=0):
    r = lax.broadcasted_iota(jnp.int32, (SB_BLOCK, 2 * SB_BLOCK), 0)
    c = lax.broadcasted_iota(jnp.int32, (SB_BLOCK, 2 * SB_BLOCK), 1)
    tri = {"after": c < r, "incl": r <= c, "before": r < c}[kind]
    m = ((c >= SB_BLOCK) | tri) & (r >= first_key)
    return jnp.where(m, 1.0, 0.0).astype(BF16)


def _sb_cumsum(x, cat):
    return _dot(x.astype(BF16), cat, NN)


def _sb_logsig(z):
    e = jnp.exp(-jnp.abs(z))
    lse = jnp.where(e < 1e-4, e, jnp.log(1.0 + e))
    lsz = jnp.minimum(z, 0.0) - lse
    return lsz, lsz - z, e


def _sb_stack(x, scale=None):
    lane = lax.broadcasted_iota(jnp.int32, (1, HEAD_W), 1)
    if scale is not None:
        x = x * scale
    return jnp.concatenate([jnp.where(lane < SB_DH, x, 0.0), jnp.where(lane >= SB_DH, x, 0.0)], axis=0).astype(BF16)


def _sb_unstack(x):
    lane = lax.broadcasted_iota(jnp.int32, (1, HEAD_W), 1)
    return jnp.where(lane < SB_DH, x[:SB_BLOCK], x[SB_BLOCK:])


def _sb_fwd(p0, pad, *, name, gather=None):
    lp = p0.shape[0]
    nb = lp // SB_BLOCK
    npair = SB_HEADS // 2
    blk0 = AB_SB // HEAD_W
    scale = SB_DH ** -0.5
    gw = SB_GROUP * SB_BLOCK
    assert pad < SB_BLOCK
    g_srcs, g_dtypes = gather if gather is not None else ([], [])
    ng_arr = len(g_srcs)

    def body(q_ref, k_ref, v_ref, *rest):
        g_ins, (o_ref, tot_ref) = rest[:ng_arr], rest[ng_arr:ng_arr + 2]
        g_outs, g_scratch = rest[ng_arr + 2:2 * ng_arr + 2], rest[2 * ng_arr + 2:]
        first_step = (pl.program_id(0) == 0) & (pl.program_id(1) == 0)
        last_pair = pl.program_id(0) == npair - 1
        if ng_arr:
            g_start, g_forward, g_finish = _gather_phases(g_ins, g_outs, g_scratch[:ng_arr], *g_scratch[ng_arr:],
                                                          g_dtypes)
            pl.when(first_step)(g_start)
            pl.when(last_pair & (pl.program_id(1) == 0))(g_forward)
        i = pl.program_id(1)
        qs = _sb_stack(q_ref[...], scale)
        qpos = i * SB_BLOCK + lax.broadcasted_iota(jnp.int32, (SB_BLOCK, 1), 0)
        qpos = jnp.concatenate([qpos, qpos], axis=0)
        cat = _sb_cat("after")
        cat0 = _sb_cat("after", pad)
        ng = i // SB_GROUP

        def group(off, first_cat, allowed, carry):
            acc, run = carry
            kg = k_ref[pl.ds(off, gw), :].astype(BF16)
            vg = v_ref[pl.ds(off, gw), :].astype(BF16)
            lsz, l1m, _ = _sb_logsig(_dot(qs, kg, NT))
            if allowed is not None:
                l1m = jnp.where(allowed, l1m, 0.0)
            args = [None] * SB_GROUP
            for g in reversed(range(SB_GROUP)):
                sl = slice(g * SB_BLOCK, (g + 1) * SB_BLOCK)
                al = _sb_cumsum(l1m[:, sl], first_cat if g == 0 else cat)
                args[g] = lsz[:, sl] + al[:, :SB_BLOCK] + run
                run = run + al[:, SB_BLOCK:]
            wgt = jnp.exp(jnp.concatenate(args, axis=1))
            if allowed is not None:
                wgt = jnp.where(allowed, wgt, 0.0)
            return acc + _dot(wgt.astype(BF16), vg, NN), run

        def below(t, carry):
            gi = ng - 1 - t
            return group(pl.multiple_of(gi * gw, gw), jnp.where(gi == 0, cat0, cat), None, carry)

        top = ng * gw
        off = pl.multiple_of(jnp.minimum(top, lp - gw), SB_BLOCK)
        kpos = off + lax.broadcasted_iota(jnp.int32, (1, gw), 1)
        allowed = (kpos < qpos) & (kpos >= pad) & (kpos >= top)
        zero = (jnp.zeros((2 * SB_BLOCK, HEAD_W), F32), jnp.zeros((2 * SB_BLOCK, HEAD_W), F32))
        carry = group(off, cat, allowed, zero)
        acc, run = lax.fori_loop(0, ng, below, carry)
        o_ref[...] = _sb_unstack(acc)
        tot_ref[...] = _sb_unstack(run)
        if ng_arr:
            pl.when(last_pair & (pl.program_id(1) == nb - 1))(g_finish)

    full = lambda c0: pl.BlockSpec((lp, HEAD_W), lambda p, i: (0, c0 + p))
    out = pl.BlockSpec((SB_BLOCK, HEAD_W), lambda p, i: (i, p))
    return pl.pallas_call(
        body, name=name, grid=(npair, nb),
        in_specs=[pl.BlockSpec((SB_BLOCK, HEAD_W), lambda p, i: (i, blk0 + p)), full(blk0 + npair), full(blk0 + 2 * npair)]
        + [pl.BlockSpec(memory_space=pltpu.VMEM)] * ng_arr,
        out_specs=[out, out] + [_ANY] * ng_arr,
        out_shape=[jax.ShapeDtypeStruct((lp, npair * HEAD_W), F32)] * 2 + _gather_out_shapes(g_srcs, g_dtypes),
        scratch_shapes=_gather_scratch(g_srcs, g_dtypes) if ng_arr else [],
        compiler_params=_cp(("arbitrary", "arbitrary"), has_side_effects=bool(ng_arr)),
    )(p0, p0, p0, *g_srcs)


def _sb_bwd(p0, tot, dsrc, d_blk0, pad, *, name, chip_sums=()):
    lp = p0.shape[0]
    nb = lp // SB_BLOCK
    npair = SB_HEADS // 2
    blk0 = AB_SB // HEAD_W
    scale = SB_DH ** -0.5
    gw = SB_GROUP * SB_BLOCK
    assert pad < SB_BLOCK
    ns = len(chip_sums)

    def body(q_ref, k_ref, v_ref, tot_ref, do_ref, *rest):
        s_ins, (dq_ref, dkt_ref, dvt_ref) = rest[:ns], rest[ns:ns + 3]
        s_outs, s_sems = rest[ns + 3:2 * ns + 3], rest[2 * ns + 3:]
        if ns:
            s_start, s_finish = _to_chips_phases(s_ins, s_outs, *s_sems)
            pl.when((pl.program_id(0) == 0) & (pl.program_id(1) == 0))(s_start)
        i = pl.program_id(1)

        @pl.when(i == 0)
        def _():
            dkt_ref[...] = jnp.zeros_like(dkt_ref)
            dvt_ref[...] = jnp.zeros_like(dvt_ref)

        qs = _sb_stack(q_ref[...], scale)
        dos = _sb_stack(do_ref[...])
        qst, dost = qs.T, dos.T
        totv = tot_ref[...]
        ones = jnp.ones((1, HEAD_W), F32)
        tots = jnp.concatenate([totv[:, 0:1] * ones, totv[:, SB_DH:SB_DH + 1] * ones], axis=0)
        qpos = i * SB_BLOCK + lax.broadcasted_iota(jnp.int32, (SB_BLOCK, 1), 0)
        qpos = jnp.concatenate([qpos, qpos], axis=0)
        incl, incl0 = _sb_cat("incl"), _sb_cat("incl", pad)
        before = _sb_cat("before")
        ng = i // SB_GROUP

        def dscore(z, e, ev, dl1m):
            r = 1.0 / (1.0 + e)
            sg = jnp.where(z >= 0, r, e * r)
            return ev * (1.0 - sg) - dl1m * sg

        def group(off, first_incl, allowed, carry):
            dq, prun, erun = carry
            kg = k_ref[pl.ds(off, gw), :].astype(BF16)
            vg = v_ref[pl.ds(off, gw), :].astype(BF16)
            z = _dot(qs, kg, NT)
            lsz, l1m, e = _sb_logsig(z)
            if allowed is not None:
                l1m = jnp.where(allowed, l1m, 0.0)
            dwgt = _dot(dos, vg, NT)
            dzs = [None] * SB_GROUP
            wgts = [None] * SB_GROUP
            for g in range(SB_GROUP):
                sl = slice(g * SB_BLOCK, (g + 1) * SB_BLOCK)
                al = _sb_cumsum(l1m[:, sl], first_incl if g == 0 else incl)
                wgt = jnp.exp(lsz[:, sl] + (tots - prun - al[:, :SB_BLOCK]))
                if allowed is not None:
                    wgt = jnp.where(allowed[:, sl], wgt, 0.0)
                prun = prun + al[:, SB_BLOCK:]
                ev = wgt * dwgt[:, sl]
                el = _sb_cumsum(ev, before)
                dzs[g] = dscore(z[:, sl], e[:, sl], ev, erun + el[:, :SB_BLOCK])
                erun = erun + el[:, SB_BLOCK:]
                wgts[g] = wgt
            dz = jnp.concatenate(dzs, axis=1)
            if allowed is not None:
                dz = jnp.where(allowed, dz, 0.0)
            dz = dz.astype(BF16)
            wg = jnp.concatenate(wgts, axis=1).astype(BF16)
            dkt_ref[:, pl.ds(off, gw)] += _dot(qst, dz, NN)
            dvt_ref[:, pl.ds(off, gw)] += _dot(dost, wg, NN)
            return dq + _dot(dz, kg, NN), prun, erun

        def below(gi, carry):
            return group(pl.multiple_of(gi * gw, gw), jnp.where(gi == 0, incl0, incl), None, carry)

        zero = tuple(jnp.zeros((2 * SB_BLOCK, HEAD_W), F32) for _ in range(3))
        carry = lax.fori_loop(0, ng, below, zero)
        top = ng * gw
        off = pl.multiple_of(jnp.minimum(top, lp - gw), SB_BLOCK)
        kpos = off + lax.broadcasted_iota(jnp.int32, (1, gw), 1)
        allowed = (kpos < qpos) & (kpos >= pad) & (kpos >= top)
        dq, _, _ = group(off, incl, allowed, carry)
        dq_ref[...] = _sb_unstack(dq) * scale
        if ns:
            pl.when((pl.program_id(0) == npair - 1) & (pl.program_id(1) == nb - 1))(s_finish)

    full = lambda c0: pl.BlockSpec((lp, HEAD_W), lambda p, i: (0, c0 + p))
    qb = lambda c0: pl.BlockSpec((SB_BLOCK, HEAD_W), lambda p, i: (i, c0 + p))
    tr = pl.BlockSpec((HEAD_W, lp), lambda p, i: (p, 0))
    return pl.pallas_call(
        body, name=name, grid=(npair, nb),
        in_specs=[qb(blk0), full(blk0 + npair), full(blk0 + 2 * npair), qb(0), qb(d_blk0)] + [_ANY] * ns,
        out_specs=[qb(0), tr, tr] + [_ANY] * ns,
        out_shape=[jax.ShapeDtypeStruct((lp, npair * HEAD_W), F32)]
        + [jax.ShapeDtypeStruct((npair * HEAD_W, lp), F32)] * 2
        + [jax.ShapeDtypeStruct(s.shape, s.dtype) for s in chip_sums],
        scratch_shapes=_to_chips_scratch(ns) if ns else [],
        compiler_params=_cp(("arbitrary", "arbitrary"), has_side_effects=bool(ns)),
    )(p0, p0, p0, tot, dsrc, *chip_sums)


def _local_step(h0, target, pad, wts, hooks=None):
    lp = h0.shape[0]
    tm = _row_tile(lp, 1056)
    tkl = tm
    d = D_MODEL
    mm = _mm
    g = {}

    h0_b = h0.astype(BF16)
    p0 = mm(h0_b, wts["w_ab"], "NN", tm=tm, tn=768, tk=d, name="l0_in_proj")
    ob, sb_tot, *gathered = _sb_fwd(p0, pad, name="sb_fwd", gather=hooks["late_gather"] if hooks else None)
    if hooks:
        wts = {**wts, **hooks["later_weights"](gathered)}
    qkv = _gdn_pre_fwd(p0, wts["conv_w"], pad, name="gdn_pre_fwd")
    oa_raw, gdn_states = _gdn_fwd(qkv, p0, wts["alog_v"], wts["dtb_v"], pad, name="gdn_fwd")
    oab = _gate_fwd(oa_raw, p0, AB_Z // HEAD_W, wts["ab_gn"], ob, heads=GDN_HEADS, name="gdn_gate_fwd")
    mix0 = mm(oab, wts["w_out0"], "NN", tm=tm, tn=1024, tk=d, name="l0_out_proj")
    h0a, h0a_b = _ln_fwd(h0, mix0, wts["ln_mix_g"][0], wts["ln_mix_b"][0], name="ln_mix0_fwd")
    u0 = mm(h0a_b, wts["w1"][0], "NN", tm=tm, tn=512, tk=d, b_dev=True, name="mlp0_up")
    y0 = mm(u0, wts["w2"][0], "NN", tm=tm, tn=1024, tk=d, a_fn="relu2", name="mlp0_down")
    h0b, h0b_b = _ln_fwd(h0a, y0, wts["ln_ffn_g"][0], wts["ln_ffn_b"][0], name="ln_ffn0_fwd")
    p1 = mm(h0b_b, wts["w_c"], "NN", tm=tm, tn=512, tk=d, b_dev=True, name="l1_in_proj")
    oc_raw, hg_states = _hg_fwd(p1, wts["lb"], pad, name="hg_fwd")
    oc = _gate_fwd(oc_raw, p1, 3 * HG_HEADS, wts["c_gn"], oc_raw, heads=HG_HEADS, name="hg_gate_fwd")
    mix1 = mm(oc, wts["w_out1"], "NN", tm=tm, tn=1024, tk=d, name="l1_out_proj")
    h1a, h1a_b = _ln_fwd(h0b, mix1, wts["ln_mix_g"][1], wts["ln_mix_b"][1], name="ln_mix1_fwd")
    u1 = mm(h1a_b, wts["w1"][1], "NN", tm=tm, tn=512, tk=d, b_dev=True, name="mlp1_up")
    y1 = mm(u1, wts["w2"][1], "NN", tm=tm, tn=1024, tk=d, a_fn="relu2", name="mlp1_down")
    h1b, _ = _ln_fwd(h1a, y1, wts["ln_ffn_g"][1], wts["ln_ffn_b"][1], name="ln_ffn1_fwd")
    dy, loss_vec = _loss_head(h1b, target, name="loss_head")

    def mlp_bwd(layer, h_in_b, u, dpre, dpre_b):
        du = mm(dpre_b, wts["w2"][layer], "NT", tm=tm, tn=1024, tk=d, epi="relu2grad", c=u, out_dtype=BF16,
                name=f"mlp{layer}_d_hidden")
        dw2 = mm(u, dpre_b, "TN", tm=1024, tn=1024, tk=tkl, a_fn="relu2", name=f"mlp{layer}_dw2")
        dw1 = mm(h_in_b, du, "TN", tm=1024, tn=512, tk=tkl, out_dev=True, name=f"mlp{layer}_dw1")
        dh = mm(du, wts["w1"][layer], "NT", tm=tm, tn=1024, tk=512, b_dev=True, epi="add", c=dpre, scale=DN_ALPHA,
                name=f"mlp{layer}_d_in")
        return dh, dw1, dw2

    ln_ffn_dg, ln_ffn_db, ln_mix_dg, ln_mix_db, dw1s, dw2s = ([None, None] for _ in range(6))
    dpre, dpre_b, ln_ffn_dg[1], ln_ffn_db[1] = _ln_bwd(h1a, y1, wts["ln_ffn_g"][1], dy, name="ln_ffn1_bwd")
    dh1a, dw1s[1], dw2s[1] = mlp_bwd(1, h1a_b, u1, dpre, dpre_b)
    dpre, dpre_b, ln_mix_dg[1], ln_mix_db[1] = _ln_bwd(h0b, mix1, wts["ln_mix_g"][1], dh1a, name="ln_mix1_bwd")
    g["c_w_out"] = mm(oc, dpre_b, "TN", tm=1024, tn=1024, tk=tkl, name="l1_dw_out")
    doc = mm(dpre_b, wts["w_out1"], "NT", tm=tm, tn=1024, tk=d, name="l1_d_gate")
    doc_raw, dz1, g["c_gn"] = _gate_bwd(oc_raw, p1, 3 * HG_HEADS, wts["c_gn"], doc, heads=HG_HEADS, name="hg_gate_bwd")
    dq1, df1, di1, g["lb"] = _hg_bwd(p1, wts["lb"], hg_states, doc_raw, pad, name="hg_bwd")
    dp1 = jnp.concatenate([dq1, df1, di1, dz1], axis=1).astype(BF16)
    g["c_w_in"] = mm(h0b_b, dp1, "TN", tm=1024, tn=512, tk=tkl, out_dev=True, name="l1_dw_in")
    dh0b = mm(dp1, wts["w_c"], "NT", tm=tm, tn=1024, tk=512, b_dev=True, epi="add", c=dpre, scale=DN_ALPHA,
              name="l1_d_in")
    dpre, dpre_b, ln_ffn_dg[0], ln_ffn_db[0] = _ln_bwd(h0a, y0, wts["ln_ffn_g"][0], dh0b, name="ln_ffn0_bwd")
    dh0a, dw1s[0], dw2s[0] = mlp_bwd(0, h0a_b, u0, dpre, dpre_b)
    dpre, dpre_b, ln_mix_dg[0], ln_mix_db[0] = _ln_bwd(h0, mix0, wts["ln_mix_g"][0], dh0a, name="ln_mix0_bwd")
    g["ab_w_out"] = mm(oab, dpre_b, "TN", tm=1024, tn=1024, tk=tkl, name="l0_dw_out")
    doab = mm(dpre_b, wts["w_out0"], "NT", tm=tm, tn=1024, tk=d, name="l0_d_gate")
    doa_raw, dz0, g["ab_gn"] = _gate_bwd(oa_raw, p0, AB_Z // HEAD_W, wts["ab_gn"], doab, heads=GDN_HEADS,
                                         name="gdn_gate_bwd")
    early_sums = ()
    if hooks:
        rows = lambda a, n: a.reshape(N_DEV, n // N_DEV, d)
        early_sums = hooks["chip_sums"]([g["c_w_in"], rows(g["c_w_out"], d), dw1s[1], rows(dw2s[1], D_FF), dw1s[0],
                                        rows(dw2s[0], D_FF), rows(g["ab_w_out"], d)])
    dqb, dkb_t, dvb_t, *g["early_parts"] = _sb_bwd(p0, sb_tot, doab, GDN_HEADS, pad, name="sb_bwd",
                                                   chip_sums=early_sums)
    dkb, dvb = dkb_t.T, dvb_t.T
    dqn, dkn, dvn, dba, g["alog_v"], g["dtb_v"] = _gdn_bwd(qkv, p0, wts["alog_v"], wts["dtb_v"], gdn_states, doa_raw,
                                                           pad, name="gdn_bwd")
    dconv_in, g["conv_w"] = _gdn_pre_bwd(p0, wts["conv_w"], jnp.concatenate([dqn, dkn, dvn], axis=1), pad,
                                         name="gdn_pre_bwd")
    dp0 = jnp.concatenate([dconv_in, dz0, dqb, dkb, dvb, dba, jnp.zeros((lp, AB_CAT - AB_BA - HEAD_W), F32)],
                          axis=1).astype(BF16)
    g["w_ab"] = mm(h0_b, dp0, "TN", tm=1024, tn=768, tk=tkl, name="l0_dw_in")
    dh0 = mm(dp0, wts["w_ab"], "NT", tm=tm, tn=1024, tk=768, epi="add", c=dpre, scale=DN_ALPHA, name="l0_d_in")

    g["w1"], g["w2"] = dw1s, dw2s
    g["ln_mix_g"] = jnp.concatenate(ln_mix_dg, axis=0)
    g["ln_mix_b"] = jnp.concatenate(ln_mix_db, axis=0)
    g["ln_ffn_g"] = jnp.concatenate(ln_ffn_dg, axis=0)
    g["ln_ffn_b"] = jnp.concatenate(ln_ffn_db, axis=0)
    return loss_vec, dh0, g


N_CHIP = N_DEV // 2


def _place():
    x, y, c = lax.axis_index("x"), lax.axis_index("y"), lax.axis_index("c")
    return x, y, c, 2 * x + y


def _chip_dev(chip, core):
    return (chip // 2, chip % 2, core)


def _remote(src, dst, send_sem, recv_sem, dev):
    return pltpu.make_async_remote_copy(src_ref=src, dst_ref=dst, send_sem=send_sem, recv_sem=recv_sem,
                                        device_id=dev, device_id_type=pl.DeviceIdType.MESH)


_ANY = pl.BlockSpec(memory_space=pl.ANY)


def _gather(srcs, dtypes, *, name):
    n = len(srcs)

    def body(*refs):
        start, forward, finish = _gather_phases(refs[:n], refs[n:2 * n], refs[2 * n:3 * n], *refs[3 * n:], dtypes)
        start()
        forward()
        finish()

    return pl.pallas_call(
        body, name=name, in_specs=[pl.BlockSpec(memory_space=pltpu.VMEM)] * n, out_specs=[_ANY] * n,
        out_shape=_gather_out_shapes(srcs, dtypes), scratch_shapes=_gather_scratch(srcs, dtypes),
        compiler_params=_cp(has_side_effects=True),
    )(*srcs)


def _gather_out_shapes(srcs, dtypes):
    return [jax.ShapeDtypeStruct((N_DEV, *s.shape), dt) for s, dt in zip(srcs, dtypes)]


def _gather_scratch(srcs, dtypes):
    n = len(srcs)
    return [pltpu.VMEM(s.shape, dt) for s, dt in zip(srcs, dtypes)] + [
        pltpu.SemaphoreType.DMA((n, 2 * N_CHIP - 1)), pltpu.SemaphoreType.DMA((n, 2 * N_CHIP - 1)),
        pltpu.SemaphoreType.DMA((n,))]


def _gather_phases(ins, outs, stages, send_sems, recv_sems, local_sems, dtypes):
    n = len(ins)
    x, y, c, chip = _place()
    me = 2 * chip + c
    sibling = (x, y, 1 - c)

    def own(i):
        cps = [_remote(stages[i], outs[i].at[me], send_sems.at[i, 0], recv_sems.at[i, 0], sibling)]
        for j in range(1, N_CHIP):
            cps.append(_remote(stages[i], outs[i].at[me], send_sems.at[i, j], recv_sems.at[i, j],
                               _chip_dev(jnp.bitwise_xor(chip, j), c)))
        return cps

    def local(i):
        return pltpu.make_async_copy(stages[i], outs[i].at[me], local_sems.at[i])

    def passed_on(i, j):
        slot = outs[i].at[2 * jnp.bitwise_xor(chip, j) + c]
        return _remote(slot, slot, send_sems.at[i, N_CHIP - 1 + j], recv_sems.at[i, N_CHIP - 1 + j], sibling)

    def start():
        for i in range(n):
            stages[i][...] = ins[i][...].astype(dtypes[i])
            local(i).start()
            for cp in own(i):
                cp.start()

    def forward():
        for i in range(n):
            for j in range(1, N_CHIP):
                own(i)[j].wait_recv()
                passed_on(i, j).start()

    def finish():
        for i in range(n):
            own(i)[0].wait_recv()
            for j in range(1, N_CHIP):
                passed_on(i, j).wait_recv()
        for i in range(n):
            for cp in own(i):
                cp.wait_send()
            for j in range(1, N_CHIP):
                passed_on(i, j).wait_send()
            local(i).wait()

    return start, forward, finish


def _to_sibling(parts, *, name):
    n = len(parts)

    def body(*refs):
        ins, outs = refs[:n], refs[n:2 * n]
        send_sems, recv_sems = refs[2 * n:]
        x, y, c, _ = _place()
        copies = [_remote(ins[i].at[2 * k + (1 - c)], outs[i].at[k], send_sems.at[i, k], recv_sems.at[i, k],
                          (x, y, 1 - c)) for i in range(n) for k in range(N_CHIP)]
        for cp in copies:
            cp.start()
        for cp in copies:
            cp.wait()

    return pl.pallas_call(
        body, name=name, in_specs=[_ANY] * n, out_specs=[_ANY] * n,
        out_shape=[jax.ShapeDtypeStruct((N_CHIP, *p.shape[1:]), p.dtype) for p in parts],
        scratch_shapes=[pltpu.SemaphoreType.DMA((n, N_CHIP)), pltpu.SemaphoreType.DMA((n, N_CHIP))],
        compiler_params=_cp(has_side_effects=True),
    )(*parts)


def _pair_sum(part, from_sibling, core, *, name):
    _, r, c = part.shape
    tm = _row_tile(r, 128)

    def body(core_ref, a_ref, b_ref, o_ref):
        o_ref[...] = (a_ref[...] + b_ref[...]).astype(BF16)

    return pl.pallas_call(
        body, name=name,
        grid_spec=pltpu.PrefetchScalarGridSpec(
            num_scalar_prefetch=1, grid=(N_CHIP, r // tm),
            in_specs=[pl.BlockSpec((None, tm, c), lambda k, i, core_ref: (2 * k + core_ref[0], i, 0)),
                      pl.BlockSpec((None, tm, c), lambda k, i, core_ref: (k, i, 0))],
            out_specs=pl.BlockSpec((None, tm, c), lambda k, i, core_ref: (k, i, 0))),
        out_shape=jax.ShapeDtypeStruct((N_CHIP, r, c), BF16), compiler_params=_cp(("parallel", "parallel")),
    )(core, part, from_sibling)


def _to_chips(sums, *, name):
    n = len(sums)

    def body(*refs):
        start, finish = _to_chips_phases(refs[:n], refs[n:2 * n], *refs[2 * n:])
        start()
        finish()

    return pl.pallas_call(
        body, name=name, in_specs=[_ANY] * n, out_specs=[_ANY] * n,
        out_shape=[jax.ShapeDtypeStruct(s.shape, s.dtype) for s in sums], scratch_shapes=_to_chips_scratch(n),
        compiler_params=_cp(has_side_effects=True),
    )(*sums)


def _to_chips_scratch(n):
    return [pltpu.SemaphoreType.DMA((n, N_CHIP - 1)), pltpu.SemaphoreType.DMA((n, N_CHIP - 1)),
            pltpu.SemaphoreType.DMA((n,))]


def _to_chips_phases(ins, outs, send_sems, recv_sems, local_sems):
    n = len(ins)
    _, _, c, chip = _place()

    def copies():
        cps = []
        for i in range(n):
            cps.append(pltpu.make_async_copy(ins[i].at[chip], outs[i].at[chip], local_sems.at[i]))
            for j in range(1, N_CHIP):
                other = jnp.bitwise_xor(chip, j)
                cps.append(_remote(ins[i].at[other], outs[i].at[chip], send_sems.at[i, j - 1], recv_sems.at[i, j - 1],
                                   _chip_dev(other, c)))
        return cps

    def start():
        for cp in copies():
            cp.start()

    def finish():
        for cp in copies():
            cp.wait()

    return start, finish


def _adamw(w, parts, m, v, *, name):
    r, c = w.shape
    s = parts.shape[0]
    tm = _row_tile(r, 128) if r % 8 == 0 else r
    c1 = 1.0 - ADAM_B1 ** ADAM_STEP
    c2 = 1.0 - ADAM_B2 ** ADAM_STEP

    def body(w_ref, p_ref, m_ref, v_ref, g_ref, d_ref, m2_ref, v2_ref):
        g = p_ref[0].astype(F32)
        for j in range(1, s):
            g = g + p_ref[j].astype(F32)
        m2 = ADAM_B1 * m_ref[...] + (1.0 - ADAM_B1) * g
        v2 = ADAM_B2 * v_ref[...] + (1.0 - ADAM_B2) * jnp.square(g)
        g_ref[...] = g
        m2_ref[...] = m2
        v2_ref[...] = v2
        d_ref[...] = -ADAM_LR * ((m2 / c1) / (jnp.sqrt(v2 / c2) + ADAM_EPS) + ADAM_WD * w_ref[...])

    blk = pl.BlockSpec((tm, c), lambda i: (i, 0))
    return pl.pallas_call(
        body, name=name, grid=(r // tm,),
        in_specs=[blk, pl.BlockSpec((s, tm, c), lambda i: (0, i, 0)), blk, blk], out_specs=[blk] * 4,
        out_shape=[jax.ShapeDtypeStruct((r, c), F32)] * 4, compiler_params=_cp(("parallel",)),
    )(w, parts, m, v)


_WEIGHTS = ("meta_tokens", "ab_w_in", "ab_conv_w", "ab_a_log", "ab_dt_bias", "ab_gnorm_g", "ab_w_out", "c_w_in",
            "c_lb_raw", "c_gnorm_g", "c_w_out", "ln_mix_g", "ln_mix_b", "mlp_w1", "mlp_w2", "ln_ffn_g", "ln_ffn_b")
_PACK_ROWS = (("ln_mix_g", 0), ("ln_mix_b", 2), ("ln_ffn_g", 4), ("ln_ffn_b", 6), ("c_lb_raw", 8))
_PACK_MISC_ROW = 10
_PACK_MISC = (("ab_gnorm_g", 0, 128), ("c_gnorm_g", 128, 128), ("ab_a_log", 256, GDN_HEADS), ("ab_dt_bias", 260, GDN_HEADS))
_PACK_N = 16
_SMALL_META = 16
_SMALL_CONV = 32
_SMALL_N = 40


def _pack_replicated(p):
    rows = jnp.zeros((_PACK_N, D_MODEL), F32)
    for name, r0 in _PACK_ROWS:
        rows = rows.at[r0:r0 + 2].set(p[name])
    for name, c0, width in _PACK_MISC:
        rows = rows.at[_PACK_MISC_ROW, c0:c0 + width].set(p[name].reshape(width))
    return rows


def _unpack_replicated(rows, like):
    out = {}
    for name, r0 in _PACK_ROWS:
        out[name] = rows[r0:r0 + 2]
    for name, c0, width in _PACK_MISC:
        out[name] = rows[_PACK_MISC_ROW, c0:c0 + width].reshape(like[name].shape)
    return out


def _lower_bound(c_lb_raw):
    lb_all = jnp.cumsum(jax.nn.softmax(c_lb_raw.astype(F32), axis=0), axis=0)
    return (lb_all - lb_all[0:1])[1].reshape(1, -1)


def kernel(x, meta_tokens, ab_w_in, ab_conv_w, ab_a_log, ab_dt_bias, ab_gnorm_g, ab_w_out, c_w_in, c_lb_raw, c_gnorm_g, c_w_out, ln_mix_g, ln_mix_b, mlp_w1, mlp_w2, ln_ffn_g, ln_ffn_b, loss_target, m_meta_tokens, m_ab_w_in, m_ab_conv_w, m_ab_a_log, m_ab_dt_bias, m_ab_gnorm_g, m_ab_w_out, m_c_w_in, m_c_lb_raw, m_c_gnorm_g, m_c_w_out, m_ln_mix_g, m_ln_mix_b, m_mlp_w1, m_mlp_w2, m_ln_ffn_g, m_ln_ffn_b, v_meta_tokens, v_ab_w_in, v_ab_conv_w, v_ab_a_log, v_ab_dt_bias, v_ab_gnorm_g, v_ab_w_out, v_c_w_in, v_c_lb_raw, v_c_gnorm_g, v_c_w_out, v_ln_mix_g, v_ln_mix_b, v_mlp_w1, v_mlp_w2, v_ln_ffn_g, v_ln_ffn_b):
    w = dict(zip(_WEIGHTS, (meta_tokens, ab_w_in, ab_conv_w, ab_a_log, ab_dt_bias, ab_gnorm_g, ab_w_out, c_w_in, c_lb_raw,
                            c_gnorm_g, c_w_out, ln_mix_g, ln_mix_b, mlp_w1, mlp_w2, ln_ffn_g, ln_ffn_b)))
    mom = dict(zip(_WEIGHTS, (m_meta_tokens, m_ab_w_in, m_ab_conv_w, m_ab_a_log, m_ab_dt_bias, m_ab_gnorm_g, m_ab_w_out,
                              m_c_w_in, m_c_lb_raw, m_c_gnorm_g, m_c_w_out, m_ln_mix_g, m_ln_mix_b, m_mlp_w1, m_mlp_w2,
                              m_ln_ffn_g, m_ln_ffn_b)))
    var = dict(zip(_WEIGHTS, (v_meta_tokens, v_ab_w_in, v_ab_conv_w, v_ab_a_log, v_ab_dt_bias, v_ab_gnorm_g, v_ab_w_out,
                              v_c_w_in, v_c_lb_raw, v_c_gnorm_g, v_c_w_out, v_ln_mix_g, v_ln_mix_b, v_mlp_w1, v_mlp_w2,
                              v_ln_ffn_g, v_ln_ffn_b)))
    me = 4 * lax.axis_index("x") + 2 * lax.axis_index("y") + lax.axis_index("c")
    seq = x.shape[1]
    pad = (-(N_META + seq)) % SB_BLOCK
    lp = pad + N_META + seq
    meta_w = D_MODEL // N_DEV
    conv_w_all = 2 * GDN_HEADS * HEAD_W + GDN_HEADS * HEAD_W
    conv_w_mine = conv_w_all // N_DEV

    g_meta, g_conv, g_ab_in = _gather([w["meta_tokens"], w["ab_conv_w"][0], w["ab_w_in"][0]], [F32, F32, BF16],
                                      name="gather_weights_first")
    meta_full = g_meta.transpose(1, 0, 2).reshape(N_META, D_MODEL)
    conv_full = g_conv.transpose(1, 0, 2).reshape(CONV_K, conv_w_all)
    ab_full = g_ab_in.transpose(1, 0, 2).reshape(D_MODEL, AB_IN)
    ba0 = AB_Z + 512
    w_ab = jnp.concatenate([ab_full[:, :ba0], ab_full[:, ba0 + 2 * GDN_HEADS:], ab_full[:, ba0:ba0 + 2 * GDN_HEADS],
                            jnp.zeros((D_MODEL, AB_CAT - AB_IN), BF16)], axis=1)
    vec128 = lambda p: jnp.zeros((1, HEAD_W), F32).at[0, :GDN_HEADS].set(p.reshape(GDN_HEADS))
    wts = dict(
        w_ab=w_ab, conv_w=conv_full, alog_v=vec128(w["ab_a_log"]), dtb_v=vec128(w["ab_dt_bias"]),
        ab_gn=w["ab_gnorm_g"][0], lb=_lower_bound(w["c_lb_raw"]), c_gn=w["c_gnorm_g"][0],
        ln_mix_g=w["ln_mix_g"], ln_mix_b=w["ln_mix_b"], ln_ffn_g=w["ln_ffn_g"], ln_ffn_b=w["ln_ffn_b"])

    def later_weights(gathered):
        g_ab_out, g_c_in, g_c_out, g_w1, g_w2 = gathered
        return dict(w_out0=g_ab_out.reshape(D_MODEL, D_MODEL), w_c=g_c_in, w_out1=g_c_out.reshape(D_MODEL, D_MODEL),
                    w1=[g_w1[:, l] for l in range(DEPTH)], w2=[g_w2[:, l].reshape(D_FF, D_MODEL) for l in range(DEPTH)])

    core = lax.axis_index("c").astype(jnp.int32).reshape(1)

    def chip_sums(tag, grads):
        from_sibling = _to_sibling(grads, name=f"scatter_{tag}_d2d")
        return [_pair_sum(a, s, core, name=f"chip_sum_{tag}_{i}") for i, (a, s) in enumerate(zip(grads, from_sibling))]

    hooks = dict(
        late_gather=([w["ab_w_out"][0], w["c_w_in"][0], w["c_w_out"][0], w["mlp_w1"], w["mlp_w2"]], [BF16] * 5),
        later_weights=later_weights, chip_sums=functools.partial(chip_sums, "early"))

    h0 = jnp.concatenate([jnp.zeros((pad, D_MODEL), F32), meta_full, x[0]], axis=0)
    loss_vec, dh0, g = _local_step(h0, loss_target[0], pad, wts, hooks)
    loss = lax.psum(jnp.sum(loss_vec), ("x", "y", "c"))
    grad_x = dh0[lp - seq:][None]

    _, lb_vjp = jax.vjp(_lower_bound, w["c_lb_raw"])
    rep_part = _pack_replicated(dict(
        ln_mix_g=g["ln_mix_g"], ln_mix_b=g["ln_mix_b"], ln_ffn_g=g["ln_ffn_g"], ln_ffn_b=g["ln_ffn_b"],
        c_lb_raw=lb_vjp(g["lb"])[0], ab_gnorm_g=g["ab_gn"], c_gnorm_g=g["c_gn"],
        ab_a_log=g["alog_v"][0, :GDN_HEADS], ab_dt_bias=g["dtb_v"][0, :GDN_HEADS]))
    small = jnp.concatenate([rep_part, dh0[pad:pad + N_META], g["conv_w"].reshape(-1, D_MODEL),
                             jnp.zeros((_SMALL_N - _SMALL_CONV - CONV_K * conv_w_all // D_MODEL, D_MODEL), F32)], axis=0)
    (small_all,) = _gather([small], [F32], name="gather_small_grads")
    rep_out = _adamw(_pack_replicated(w), small_all[:, :_PACK_N], _pack_replicated(mom), _pack_replicated(var),
                     name="adamw_replicated")
    meta_parts = lax.dynamic_slice_in_dim(small_all[:, _SMALL_META:_SMALL_META + N_META], me * meta_w, meta_w, axis=2)
    meta_out = _adamw(w["meta_tokens"], meta_parts, mom["meta_tokens"], var["meta_tokens"], name="adamw_meta")
    conv_parts = small_all[:, _SMALL_CONV:_SMALL_CONV + CONV_K * conv_w_all // D_MODEL].reshape(N_DEV, CONV_K, conv_w_all)
    conv_parts = lax.dynamic_slice_in_dim(conv_parts, me * conv_w_mine, conv_w_mine, axis=2)
    conv_out = _adamw(w["ab_conv_w"][0], conv_parts, mom["ab_conv_w"][0], var["ab_conv_w"][0], name="adamw_conv")

    gab = g["w_ab"]
    gab = jnp.concatenate([gab[:, :ba0], gab[:, AB_BA:AB_BA + 2 * GDN_HEADS], gab[:, ba0:AB_BA]], axis=1)
    (ab_in_parts,) = _to_chips(chip_sums("last", [gab.reshape(D_MODEL, N_DEV, AB_IN // N_DEV).transpose(1, 0, 2)]),
                               name="scatter_last_ici")
    early = g["early_parts"]
    big = [("ab_w_in", 0, ab_in_parts), ("ab_w_out", 0, early[6]), ("mlp_w1", 0, early[4]), ("mlp_w2", 0, early[5]),
           ("c_w_in", 0, early[0]), ("c_w_out", 0, early[1]), ("mlp_w1", 1, early[2]), ("mlp_w2", 1, early[3])]
    big_out = {}
    for name, l, p in big:
        res = _adamw(w[name][l], p, mom[name][l], var[name][l], name=f"adamw_{name}{l}")
        big_out.setdefault(name, []).append(res)

    rep = [_unpack_replicated(r, w) for r in rep_out]
    outs = {}
    for name in _WEIGHTS:
        if name == "meta_tokens":
            outs[name] = list(meta_out)
        elif name == "ab_conv_w":
            outs[name] = [o[None] for o in conv_out]
        elif name in big_out:
            res = big_out[name]
            outs[name] = [o[None] for o in res[0]] if len(res) == 1 else [jnp.stack(pair) for pair in zip(*res)]
        else:
            outs[name] = [r[name] for r in rep]
    flat = [loss, grad_x]
    for kind in range(4):
        flat += [outs[name][kind] for name in _WEIGHTS]
    return tuple(flat)
```

```python
import functools
import math

import jax
import jax.numpy as jnp
from jax import lax
from jax.experimental import pallas as pl
from jax.experimental.pallas import tpu as pltpu

F32 = jnp.float32
BF16 = jnp.bfloat16
HI = lax.Precision.HIGHEST

N_DEV = 8
D_MODEL = 1024
N_META = 16
D_FF = 4096
DEPTH = 2
GDN_HEADS = 4
SB_HEADS = 8
SB_DH = 64
HG_HEADS = 8
HEAD_W = 128
CHUNK = 64
SB_BLOCK = 128
CONV_K = 4
DN_ALPHA = float((2 * DEPTH) ** 0.25)
LN_EPS = 1e-5
RMS_EPS = 1e-6
L2_EPS = 1e-6
ADAM_LR, ADAM_B1, ADAM_B2, ADAM_EPS, ADAM_WD, ADAM_STEP = 0.001, 0.9, 0.999, 1e-08, 0.01, 10

AB_QKV = 0
AB_Z = 1536
AB_SB = 2048
AB_BA = 3584
AB_CAT = 3840
AB_IN = 3592

VMEM_LIMIT = 56 * 1024 * 1024


def _cp(sem=None, **kw):
    if sem is not None:
        kw["dimension_semantics"] = sem
    return pltpu.CompilerParams(vmem_limit_bytes=VMEM_LIMIT, **kw)


def _row_tile(n, want):
    best = 8
    for t in range(8, min(n, want) + 1, 8):
        if n % t == 0:
            best = t
    return best


@jax.custom_vjp
def _sigmoid(x):
    e = jnp.exp(-jnp.abs(x))
    r = 1.0 / (1.0 + e)
    return jnp.where(x >= 0, r, e * r)


def _sigmoid_fwd(x):
    s = _sigmoid(x)
    return s, s


def _sigmoid_bwd(s, g):
    return (g * s * (1.0 - s),)


_sigmoid.defvjp(_sigmoid_fwd, _sigmoid_bwd)


def _log1p_exp_neg_abs(x):
    e = jnp.exp(-jnp.abs(x))
    return jnp.where(e < 1e-4, e - 0.5 * e * e, jnp.log(1.0 + e))


@jax.custom_vjp
def _softplus(x):
    return jnp.maximum(x, 0.0) + _log1p_exp_neg_abs(x)


def _softplus_fwd(x):
    return _softplus(x), x


def _softplus_bwd(x, g):
    return (g * _sigmoid(x),)


_softplus.defvjp(_softplus_fwd, _softplus_bwd)


def _silu(x):
    return x * _sigmoid(x)


def _silu_grad(x):
    s = _sigmoid(x)
    return s * (1.0 + x * (1.0 - s))


def _dot(a, b, dims, precision=None):
    return lax.dot_general(a, b, (dims, ((), ())), precision=precision, preferred_element_type=F32)


NN = ((1,), (0,))
NT = ((1,), (1,))
TN = ((0,), (0,))


def _bdot(a, b, dims):
    return _dot(a.astype(BF16), b.astype(BF16), dims)


def _layer_norm(pre, g, beta):
    mu = jnp.mean(pre, axis=-1, keepdims=True)
    xc = pre - mu
    var = jnp.mean(xc * xc, axis=-1, keepdims=True)
    return xc * lax.rsqrt(var + LN_EPS) * g + beta


def _mm(a, b, mode, *, tm, tn, tk, name, a_fn=None, epi=None, c=None, scale=1.0, b_dev=False, out_dev=False,
        out_dtype=F32, ln=None):
    if mode == "NN":
        m, kk = a.shape
        n = b.shape[2] * N_DEV if b_dev else b.shape[1]
    elif mode == "NT":
        m, kk = a.shape
        n = b.shape[1] if b_dev else b.shape[0]
    else:
        kk, m = a.shape
        n = b.shape[1]
    assert m % tm == 0 and n % tn == 0 and kk % tk == 0, (name, m, n, kk, tm, tn, tk)
    nk = kk // tk
    dims = {"NN": NN, "NT": NT, "TN": TN}[mode]

    if mode == "TN":
        a_spec = pl.BlockSpec((tk, tm), lambda i, j, k: (k, i))
    else:
        a_spec = pl.BlockSpec((tm, tk), lambda i, j, k: (i, k))
    if mode == "NN":
        if b_dev:
            assert tn == b.shape[2]
            b_spec = pl.BlockSpec((None, tk, tn), lambda i, j, k: (j, k, 0))
        else:
            b_spec = pl.BlockSpec((tk, tn), lambda i, j, k: (k, j))
    elif mode == "NT":
        if b_dev:
            assert tk == b.shape[2]
            b_spec = pl.BlockSpec((None, tn, tk), lambda i, j, k: (k, j, 0))
        else:
            b_spec = pl.BlockSpec((tn, tk), lambda i, j, k: (j, k))
    else:
        b_spec = pl.BlockSpec((tk, tn), lambda i, j, k: (k, j))
    in_specs = [a_spec, b_spec]
    operands = [a, b]
    if epi is not None:
        in_specs.append(pl.BlockSpec((tm, tn), lambda i, j, k: (i, j)))
        operands.append(c)
    if epi == "ln":
        assert tn == n and not out_dev
        in_specs += [pl.BlockSpec((1, n), lambda i, j, k: (0, 0))] * 2
        operands += [ln[0].reshape(1, n), ln[1].reshape(1, n)]
    if out_dev:
        assert tn == n // N_DEV
        out_shape = jax.ShapeDtypeStruct((N_DEV, m, tn), out_dtype)
        out_spec = pl.BlockSpec((None, tm, tn), lambda i, j, k: (j, i, 0))
    else:
        out_shape = jax.ShapeDtypeStruct((m, n), out_dtype)
        out_spec = pl.BlockSpec((tm, tn), lambda i, j, k: (i, j))
    if epi == "ln":
        out_shape = [out_shape, out_shape, jax.ShapeDtypeStruct((m, n), BF16)]
        out_spec = [out_spec] * 3
    n_in = len(operands)
    n_out = 3 if epi == "ln" else 1

    def body(*refs):
        a_ref, b_ref = refs[0], refs[1]
        c_ref = refs[2] if epi is not None else None
        o_ref = refs[n_in]
        acc_ref = refs[-1] if nk > 1 else None
        av = a_ref[...]
        if a_fn == "relu2":
            av = jnp.square(jnp.maximum(av, 0.0))
        p = _dot(av.astype(BF16), b_ref[...].astype(BF16), dims)

        def finish(acc):
            if epi == "add":
                acc = acc + scale * c_ref[...]
            elif epi == "relu2grad":
                acc = acc * (2.0 * jnp.maximum(c_ref[...], 0.0))
            elif epi == "ln":
                acc = acc + scale * c_ref[...]
                y = _layer_norm(acc, refs[3][...], refs[4][...])
                refs[n_in + 1][...] = y
                refs[n_in + 2][...] = y.astype(BF16)
            o_ref[...] = acc.astype(out_dtype)

        if nk == 1:
            finish(p)
        else:
            k = pl.program_id(2)

            @pl.when(k == 0)
            def _():
                acc_ref[...] = p

            @pl.when(k > 0)
            def _():
                acc_ref[...] += p

            @pl.when(k == nk - 1)
            def _():
                finish(acc_ref[...])

    return pl.pallas_call(
        body, name=name, grid=(m // tm, n // tn, nk), in_specs=in_specs, out_specs=out_spec, out_shape=out_shape,
        scratch_shapes=[pltpu.VMEM((tm, tn), F32)] if nk > 1 else [],
        compiler_params=_cp(("parallel", "parallel", "arbitrary")),
    )(*operands)


def _ln_bwd(pre, g, dy, *, name):
    lp, d = pre.shape
    tm = _row_tile(lp, 512)

    def body(pre_ref, g_ref, dy_ref, dpre_ref, dpreb_ref, dg_ref, db_ref):
        pre = pre_ref[...]
        mu = jnp.mean(pre, axis=-1, keepdims=True)
        xc = pre - mu
        var = jnp.mean(xc * xc, axis=-1, keepdims=True)
        rstd = lax.rsqrt(var + LN_EPS)
        xhat = xc * rstd
        dyv = dy_ref[...]
        dxh = dyv * g_ref[...]
        m1 = jnp.mean(dxh, axis=-1, keepdims=True)
        m2 = jnp.mean(dxh * xhat, axis=-1, keepdims=True)
        dpre = rstd * (dxh - m1 - xhat * m2)
        dpre_ref[...] = dpre
        dpreb_ref[...] = dpre.astype(BF16)

        @pl.when(pl.program_id(0) == 0)
        def _():
            dg_ref[...] = jnp.zeros_like(dg_ref)
            db_ref[...] = jnp.zeros_like(db_ref)

        dg_ref[...] += jnp.sum(dyv * xhat, axis=0, keepdims=True)
        db_ref[...] += jnp.sum(dyv, axis=0, keepdims=True)

    row = pl.BlockSpec((tm, d), lambda i: (i, 0))
    vec = pl.BlockSpec((1, d), lambda i: (0, 0))
    return pl.pallas_call(
        body, name=name, grid=(lp // tm,), in_specs=[row, vec, row], out_specs=[row, row, vec, vec],
        out_shape=[jax.ShapeDtypeStruct((lp, d), F32), jax.ShapeDtypeStruct((lp, d), BF16),
                   jax.ShapeDtypeStruct((1, d), F32), jax.ShapeDtypeStruct((1, d), F32)],
        compiler_params=_cp(("arbitrary",)),
    )(pre, g.reshape(1, d), dy)


def _loss_head(y, target, *, name):
    lp, d = y.shape
    seq = target.shape[0]
    tm = SB_BLOCK
    first = (lp - seq) // tm
    assert (lp - seq) % tm == 0 and seq % tm == 0

    def body(y_ref, t_ref, dy_ref, loss_ref):
        i = pl.program_id(0)
        live = i >= first
        diff = jnp.where(live, y_ref[...] - t_ref[...], 0.0)
        dy_ref[...] = diff * (1.0 / d)

        @pl.when(i == 0)
        def _():
            loss_ref[...] = jnp.zeros_like(loss_ref)

        loss_ref[...] += jnp.sum(diff * diff, axis=0, keepdims=True) * (0.5 / d)

    return pl.pallas_call(
        body, name=name, grid=(lp // tm,),
        in_specs=[pl.BlockSpec((tm, d), lambda i: (i, 0)),
                  pl.BlockSpec((tm, d), lambda i: (jnp.maximum(i - first, 0), 0))],
        out_specs=[pl.BlockSpec((tm, d), lambda i: (i, 0)), pl.BlockSpec((1, d), lambda i: (0, 0))],
        out_shape=[jax.ShapeDtypeStruct((lp, d), F32), jax.ShapeDtypeStruct((1, d), F32)],
        compiler_params=_cp(("arbitrary",)),
    )(y, target)


def _gate_fwd(o, zsrc, z_blk0, g, other, *, heads, name):
    lp = o.shape[0]
    tm = _row_tile(lp, 512)
    w = heads * HEAD_W
    assert (z_blk0 * HEAD_W) % w == 0
    has_other = w < D_MODEL

    def body(o_ref, z_ref, g_ref, *rest):
        y_ref = rest[-1]
        gv = g_ref[...]
        for h in range(heads):
            cs = slice(h * HEAD_W, (h + 1) * HEAD_W)
            ov = o_ref[:, cs]
            r = lax.rsqrt(jnp.mean(ov * ov, axis=-1, keepdims=True) + RMS_EPS)
            y_ref[:, cs] = (ov * r * gv * _silu(z_ref[:, cs])).astype(BF16)
        if has_other:
            y_ref[:, w:] = rest[0][...].astype(BF16)

    row = lambda width, blk: pl.BlockSpec((tm, width), lambda i: (i, blk))
    return pl.pallas_call(
        body, name=name, grid=(lp // tm,),
        in_specs=[row(w, 0), row(w, z_blk0 * HEAD_W // w), pl.BlockSpec((1, HEAD_W), lambda i: (0, 0))]
        + ([row(D_MODEL - w, 0)] if has_other else []),
        out_specs=row(D_MODEL, 0), out_shape=jax.ShapeDtypeStruct((lp, D_MODEL), BF16),
        compiler_params=_cp(("parallel",)),
    )(o, zsrc, g.reshape(1, HEAD_W), *([other] if has_other else []))


def _gate_bwd(o, zsrc, z_blk0, g, dy, *, heads, name):
    lp = o.shape[0]
    tm = _row_tile(lp, 512)

    w = heads * HEAD_W
    assert (z_blk0 * HEAD_W) % w == 0

    def body(o_ref, z_ref, g_ref, dy_ref, do_ref, dz_ref, dg_ref):
        @pl.when(pl.program_id(0) == 0)
        def _():
            dg_ref[...] = jnp.zeros_like(dg_ref)

        gv = g_ref[...]
        dg = jnp.zeros((1, HEAD_W), F32)
        for h in range(heads):
            cs = slice(h * HEAD_W, (h + 1) * HEAD_W)
            ov, zv, dyv = o_ref[:, cs], z_ref[:, cs], dy_ref[:, cs]
            r = lax.rsqrt(jnp.mean(ov * ov, axis=-1, keepdims=True) + RMS_EPS)
            nrm = ov * r
            s = _silu(zv)
            dn = dyv * gv * s
            do_ref[:, cs] = r * (dn - nrm * jnp.mean(dn * nrm, axis=-1, keepdims=True))
            dz_ref[:, cs] = dyv * nrm * gv * _silu_grad(zv)
            dg = dg + jnp.sum(dyv * nrm * s, axis=0, keepdims=True)
        dg_ref[...] += dg

    row = lambda blk: pl.BlockSpec((tm, w), lambda i: (i, blk))
    vec = pl.BlockSpec((1, HEAD_W), lambda i: (0, 0))
    return pl.pallas_call(
        body, name=name, grid=(lp // tm,),
        in_specs=[row(0), row(z_blk0 * HEAD_W // w), vec, row(0)], out_specs=[row(0), row(0), vec],
        out_shape=[jax.ShapeDtypeStruct((lp, w), F32), jax.ShapeDtypeStruct((lp, w), F32),
                   jax.ShapeDtypeStruct((1, HEAD_W), F32)],
        compiler_params=_cp(("arbitrary",)),
    )(o, zsrc, g.reshape(1, HEAD_W), dy)


def _conv_taps(x, w):
    acc = w[CONV_K - 1:CONV_K, :] * x
    for k in range(CONV_K - 1):
        acc = acc + w[k:k + 1, :] * pltpu.roll(x, CONV_K - 1 - k, 0)
    return acc


def _gdn_pre_fwd(p0, conv_w, pad, *, name):
    lp = p0.shape[0]
    nq = GDN_HEADS
    qscale = HEAD_W ** -0.5

    def body(x_ref, w_ref, y_ref):
        j = pl.program_id(0)
        c = _conv_taps(x_ref[...], w_ref[...])
        s = _silu(c)
        r = lax.rsqrt(jnp.sum(s * s, axis=-1, keepdims=True) + L2_EPS)
        mult = jnp.where(j < nq, r * qscale, jnp.where(j < 2 * nq, r, 1.0))
        rows = lax.broadcasted_iota(jnp.int32, (lp, 1), 0)
        y_ref[...] = jnp.where(rows >= pad, s * mult, 0.0)

    return pl.pallas_call(
        body, name=name, grid=(3 * nq,),
        in_specs=[pl.BlockSpec((lp, HEAD_W), lambda j: (0, j)), pl.BlockSpec((CONV_K, HEAD_W), lambda j: (0, j))],
        out_specs=pl.BlockSpec((lp, HEAD_W), lambda j: (0, j)),
        out_shape=jax.ShapeDtypeStruct((lp, 3 * nq * HEAD_W), F32), compiler_params=_cp(("parallel",)),
    )(p0, conv_w)


def _gdn_pre_bwd(p0, conv_w, dqkv, pad, *, name):
    lp = p0.shape[0]
    nq = GDN_HEADS
    qscale = HEAD_W ** -0.5

    def body(x_ref, w_ref, dy_ref, dx_ref, dw_ref):
        j = pl.program_id(0)
        x, w = x_ref[...], w_ref[...]
        c = _conv_taps(x, w)
        s = _silu(c)
        r = lax.rsqrt(jnp.sum(s * s, axis=-1, keepdims=True) + L2_EPS)
        rows = lax.broadcasted_iota(jnp.int32, (lp, 1), 0)
        dy = jnp.where(rows >= pad, dy_ref[...], 0.0)
        nrm = s * r
        dn = dy * jnp.where(j < nq, qscale, 1.0)
        ds_norm = r * (dn - nrm * jnp.sum(nrm * dn, axis=-1, keepdims=True))
        ds = jnp.where(j < 2 * nq, ds_norm, dy)
        dc = ds * _silu_grad(c)
        dx = w[CONV_K - 1:CONV_K, :] * dc
        dws = [None] * CONV_K
        dws[CONV_K - 1] = jnp.sum(dc * x, axis=0, keepdims=True)
        for k in range(CONV_K - 1):
            sh = CONV_K - 1 - k
            dx = dx + w[k:k + 1, :] * pltpu.roll(dc, lp - sh, 0)
            dws[k] = jnp.sum(dc * pltpu.roll(x, sh, 0), axis=0, keepdims=True)
        dx_ref[...] = dx
        dw_ref[...] = jnp.concatenate(dws, axis=0)

    blk = pl.BlockSpec((lp, HEAD_W), lambda j: (0, j))
    wblk = pl.BlockSpec((CONV_K, HEAD_W), lambda j: (0, j))
    return pl.pallas_call(
        body, name=name, grid=(3 * nq,), in_specs=[blk, wblk, blk], out_specs=[blk, wblk],
        out_shape=[jax.ShapeDtypeStruct((lp, 3 * nq * HEAD_W), F32),
                   jax.ShapeDtypeStruct((CONV_K, 3 * nq * HEAD_W), F32)],
        compiler_params=_cp(("parallel",)),
    )(p0, conv_w, dqkv)


def _tri(c, strict):
    r = lax.broadcasted_iota(jnp.int32, (c, c), 0)
    q = lax.broadcasted_iota(jnp.int32, (c, c), 1)
    return (q < r) if strict else (q <= r)


@jax.custom_vjp
def _inv_unit_lower(m):
    c = m.shape[0]
    eye = (lax.broadcasted_iota(jnp.int32, (c, c), 0) == lax.broadcasted_iota(jnp.int32, (c, c), 1)).astype(F32)
    x = eye - m
    p = m
    n = 2
    while n < CHUNK:
        p = _bdot(p, p, NN)
        x = x + _bdot(x, p, NN)
        n *= 2
    return x


def _inv_fwd(m):
    t = _inv_unit_lower(m)
    return t, t


def _inv_bwd(t, g):
    return (-_bdot(_bdot(t, g, TN), t, NT),)


_inv_unit_lower.defvjp(_inv_fwd, _inv_bwd)


def _heads_to_rows(x, nh):
    return jnp.concatenate([x[:, h * HEAD_W:(h + 1) * HEAD_W] for h in range(nh)], axis=0)


def _rows_to_heads(x, nh):
    c = x.shape[0] // nh
    return jnp.concatenate([x[h * c:(h + 1) * c] for h in range(nh)], axis=1)


def _gdn_chunk(q, k, v, ba, alog, dtb, states, valid):
    nh = GDN_HEADS
    c = q.shape[0]
    r = nh * c
    lane = lax.broadcasted_iota(jnp.int32, (1, HEAD_W), 1)
    pick = lambda x, l: jnp.sum(jnp.where(lane == l, x, 0.0), axis=-1, keepdims=True)
    beta = jnp.concatenate([jnp.where(valid, _sigmoid(pick(ba, h)), 0.0) for h in range(nh)], axis=0)
    g = jnp.concatenate(
        [jnp.where(valid, -jnp.exp(pick(alog, h)) * _softplus(pick(ba, nh + h) + pick(dtb, h)), 0.0) for h in range(nh)],
        axis=0)
    qs, ks, vs = _heads_to_rows(q, nh), _heads_to_rows(k, nh), _heads_to_rows(v, nh)
    rr = lax.broadcasted_iota(jnp.int32, (r, r), 0)
    cc = lax.broadcasted_iota(jnp.int32, (r, r), 1)
    same = (rr // c) == (cc // c)
    causal, strict = same & (cc <= rr), same & (cc < rr)
    lower = jnp.where(causal, 1.0, 0.0).astype(BF16)
    upper = jnp.where(same & (cc >= rr), 1.0, 0.0).astype(BF16)
    gcb = _mask_mm(lower, upper, g * jnp.ones((1, HEAD_W), F32))
    gc_col = jnp.concatenate([gcb] * (r // HEAD_W), axis=1)
    decay = jnp.where(causal, jnp.exp(jnp.minimum(gc_col - gc_col.T, 0.0)), 0.0)
    egc = jnp.exp(gcb)
    kb = ks * beta
    m = jnp.where(strict, _bdot(kb, ks, NT) * decay, 0.0)
    t = _inv_unit_lower(m)
    u = _bdot(t, vs * beta, NN)
    w = _bdot(t, kb * egc, NN)
    a = _bdot(qs, ks, NT) * decay
    rows = lambda x, h: x[h * c:(h + 1) * c]
    qe = qs * egc
    v_new = u - jnp.concatenate([_bdot(rows(w, h), states[h], NN) for h in range(nh)], axis=0)
    o = jnp.concatenate([_bdot(rows(qe, h), states[h], NN) for h in range(nh)], axis=0) + _bdot(a, v_new, NN)
    new_states = []
    for h in range(nh):
        gl = gcb[(h + 1) * c - 1:(h + 1) * c, :]
        k_dec = rows(ks, h) * jnp.exp(gl - rows(gcb, h))
        new_states.append(states[h] * jnp.exp(gl) + _bdot(k_dec, rows(v_new, h), TN))
    return _rows_to_heads(o, nh), new_states


def _gdn_fwd(qkv, p0, alog_v, dtb_v, pad, *, name):
    lp = qkv.shape[0]
    n = lp // CHUNK
    nh = GDN_HEADS

    def body(q_ref, k_ref, v_ref, ba_ref, al_ref, dt_ref, o_ref, st_ref, s_ref):
        i = pl.program_id(0)

        @pl.when(i == 0)
        def _():
            s_ref[...] = jnp.zeros_like(s_ref)

        valid = (i * CHUNK + lax.broadcasted_iota(jnp.int32, (CHUNK, 1), 0)) >= pad
        s = s_ref[...]
        o, s2 = _gdn_chunk(q_ref[...], k_ref[...], v_ref[...], ba_ref[...], al_ref[...], dt_ref[...],
                           [s[h] for h in range(nh)], valid)
        st_ref[...] = s
        o_ref[...] = o
        for h in range(nh):
            s_ref[h] = s2[h]

    w = nh * HEAD_W
    vec = pl.BlockSpec((1, HEAD_W), lambda i: (0, 0))
    return pl.pallas_call(
        body, name=name, grid=(n,),
        in_specs=[pl.BlockSpec((CHUNK, w), lambda i: (i, 0)), pl.BlockSpec((CHUNK, w), lambda i: (i, 1)),
                  pl.BlockSpec((CHUNK, w), lambda i: (i, 2)), pl.BlockSpec((CHUNK, HEAD_W), lambda i: (i, AB_BA // HEAD_W)),
                  vec, vec],
        out_specs=[pl.BlockSpec((CHUNK, w), lambda i: (i, 0)),
                   pl.BlockSpec((None, nh, HEAD_W, HEAD_W), lambda i: (i, 0, 0, 0))],
        out_shape=[jax.ShapeDtypeStruct((lp, w), F32), jax.ShapeDtypeStruct((n, nh, HEAD_W, HEAD_W), F32)],
        scratch_shapes=[pltpu.VMEM((nh, HEAD_W, HEAD_W), F32)],
        compiler_params=_cp(("arbitrary",)),
    )(qkv, qkv, qkv, p0, alog_v, dtb_v)


def _gdn_bwd(qkv, p0, alog_v, dtb_v, states, do, pad, *, name):
    lp = qkv.shape[0]
    n = lp // CHUNK
    nh = GDN_HEADS

    def body(q_ref, k_ref, v_ref, ba_ref, al_ref, dt_ref, st_ref, do_ref,
             dq_ref, dk_ref, dv_ref, dba_ref, dal_ref, ddt_ref, ds_ref):
        step = pl.program_id(0)
        i = n - 1 - step

        @pl.when(step == 0)
        def _():
            ds_ref[...] = jnp.zeros_like(ds_ref)
            dal_ref[...] = jnp.zeros_like(dal_ref)
            ddt_ref[...] = jnp.zeros_like(ddt_ref)

        valid = (i * CHUNK + lax.broadcasted_iota(jnp.int32, (CHUNK, 1), 0)) >= pad
        st, dst = st_ref[...], ds_ref[...]
        fn = functools.partial(_gdn_chunk, valid=valid)
        _, vjp = jax.vjp(fn, q_ref[...], k_ref[...], v_ref[...], ba_ref[...], al_ref[...], dt_ref[...],
                         [st[h] for h in range(nh)])
        dq, dk, dv, dba, dal, ddt, ds = vjp((do_ref[...], [dst[h] for h in range(nh)]))
        dq_ref[...] = dq
        dk_ref[...] = dk
        dv_ref[...] = dv
        dba_ref[...] = dba
        dal_ref[...] += dal
        ddt_ref[...] += ddt
        for h in range(nh):
            ds_ref[h] = ds[h]

    w = nh * HEAD_W
    rev = lambda c: (lambda s: (n - 1 - s, c))
    vec = pl.BlockSpec((1, HEAD_W), lambda s: (0, 0))
    dq, dk, dv, dba, dal, ddt = pl.pallas_call(
        body, name=name, grid=(n,),
        in_specs=[pl.BlockSpec((CHUNK, w), rev(0)), pl.BlockSpec((CHUNK, w), rev(1)), pl.BlockSpec((CHUNK, w), rev(2)),
                  pl.BlockSpec((CHUNK, HEAD_W), rev(AB_BA // HEAD_W)), vec, vec,
                  pl.BlockSpec((None, nh, HEAD_W, HEAD_W), lambda s: (n - 1 - s, 0, 0, 0)),
                  pl.BlockSpec((CHUNK, w), rev(0))],
        out_specs=[pl.BlockSpec((CHUNK, w), rev(0)), pl.BlockSpec((CHUNK, w), rev(0)), pl.BlockSpec((CHUNK, w), rev(0)),
                   pl.BlockSpec((CHUNK, HEAD_W), rev(0)), vec, vec],
        out_shape=[jax.ShapeDtypeStruct((lp, w), F32)] * 3 + [jax.ShapeDtypeStruct((lp, HEAD_W), F32)]
        + [jax.ShapeDtypeStruct((1, HEAD_W), F32)] * 2,
        scratch_shapes=[pltpu.VMEM((nh, HEAD_W, HEAD_W), F32)],
        compiler_params=_cp(("arbitrary",)),
    )(qkv, qkv, qkv, p0, alog_v, dtb_v, states, do)
    return dq, dk, dv, dba, dal, ddt


HG_LEVELS = (32, 16, 8, 4, 2, 1)
HG_GROUP = 4


def _hg_masks():
    import numpy as np
    c = CHUNK
    t = np.arange(c)[:, None]
    j = np.arange(c)[None, :]
    sums = [j <= t, j > t]
    pairs = [j == t]
    for m in HG_LEVELS:
        p = (t // (2 * m)) * (2 * m)
        r = p + m
        upper = t >= r
        sums.append(upper & (j > r) & (j <= t))
        sums.append(~upper & (j > t) & (j <= r))
        pairs.append(upper & (j < r) & (j >= p))
    sums = np.concatenate(sums, axis=0).astype(np.float32)
    pairs = np.concatenate([np.kron(np.eye(HG_GROUP), p) for p in pairs], axis=0).astype(np.float32)
    return jnp.asarray(sums, BF16), jnp.asarray(sums.T, BF16), jnp.asarray(pairs, F32)


def _split3(x):
    hi = x.astype(BF16)
    r1 = x - hi.astype(F32)
    mid = r1.astype(BF16)
    return hi, mid, (r1 - mid.astype(F32)).astype(BF16)


def _mask_mm_raw(m, x):
    return sum(_dot(m, part, NN) for part in _split3(x))


@jax.custom_vjp
def _mask_mm(m, mt, x):
    return _mask_mm_raw(m, x)


def _mask_mm_fwd(m, mt, x):
    return _mask_mm_raw(m, x), (m, mt)


def _mask_mm_bwd(res, g):
    m, mt = res
    return jnp.zeros_like(m), jnp.zeros_like(mt), _mask_mm_raw(mt, g)


_mask_mm.defvjp(_mask_mm_fwd, _mask_mm_bwd)


def _hg_chunk(qr, fr, ir, lb, states, valid, sums, sums_t, pairs):
    nh = HG_GROUP
    c = qr.shape[0]
    r = nh * c
    fg = lb + (1.0 - lb) * _sigmoid(fr)
    logf = jnp.where(valid, jnp.log(fg), 0.0)
    k = jnp.where(valid, 1.0 - fg, 0.0)
    qs = jnp.where(valid, _silu(qr), 0.0)
    v = jnp.where(valid, ir, 0.0)
    e = jnp.exp(_mask_mm(sums, sums_t, logf))
    blk = lambda n: e[n * c:(n + 1) * c]
    mask = lambda n: pairs[n * r:(n + 1) * r]
    stack = lambda x: _heads_to_rows(x, nh)
    a = mask(0) * _bdot(stack(qs), stack(k), NT)
    for lvl in range(len(HG_LEVELS)):
        a = a + mask(1 + lvl) * _bdot(stack(qs * blk(2 + 2 * lvl)), stack(k * blk(3 + 2 * lvl)), NT)
    av = _bdot(a, stack(v), NN)
    eb = blk(0)
    qe, kd = qs * eb, k * blk(1)
    outs, new_states = [], []
    for h in range(nh):
        cs = slice(h * HEAD_W, (h + 1) * HEAD_W)
        outs.append(_bdot(qe[:, cs], states[h], NT) + av[h * c:(h + 1) * c])
        new_states.append(states[h] * eb[c - 1:c, cs] + _bdot(v[:, cs], kd[:, cs], TN))
    return jnp.concatenate(outs, axis=1), new_states


def _hg_fwd(p1, lb, pad, *, name):
    lp = p1.shape[0]
    n = lp // CHUNK
    nh = HG_HEADS

    def body(q_ref, f_ref, i_ref, lb_ref, sums_ref, sums_t_ref, pairs_ref, o_ref, st_ref, s_ref):
        i = pl.program_id(1)

        @pl.when(i == 0)
        def _():
            s_ref[...] = jnp.zeros_like(s_ref)

        valid = (i * CHUNK + lax.broadcasted_iota(jnp.int32, (CHUNK, 1), 0)) >= pad
        s = s_ref[...]
        o, s2 = _hg_chunk(q_ref[...], f_ref[...], i_ref[...], lb_ref[...], [s[h] for h in range(grp)], valid,
                          sums_ref[...], sums_t_ref[...], pairs_ref[...])
        st_ref[...] = s
        o_ref[...] = o
        for h in range(grp):
            s_ref[h] = s2[h]

    masks = _hg_masks()
    grp, ngrp, gw = HG_GROUP, nh // HG_GROUP, HG_GROUP * HEAD_W
    blk = lambda off: pl.BlockSpec((CHUNK, gw), lambda h, i: (i, off + h))
    const = lambda a: pl.BlockSpec(a.shape, lambda h, i: (0, 0))
    return pl.pallas_call(
        body, name=name, grid=(ngrp, n),
        in_specs=[blk(0), blk(ngrp), blk(2 * ngrp), pl.BlockSpec((1, gw), lambda h, i: (0, h))]
        + [const(a) for a in masks],
        out_specs=[blk(0), pl.BlockSpec((grp, None, HEAD_W, HEAD_W), lambda h, i: (h, i, 0, 0))],
        out_shape=[jax.ShapeDtypeStruct((lp, nh * HEAD_W), F32), jax.ShapeDtypeStruct((nh, n, HEAD_W, HEAD_W), F32)],
        scratch_shapes=[pltpu.VMEM((grp, HEAD_W, HEAD_W), F32)],
        compiler_params=_cp(("parallel", "arbitrary")),
    )(p1, p1, p1, lb, *masks)


def _hg_bwd(p1, lb, states, do, pad, *, name):
    lp = p1.shape[0]
    n = lp // CHUNK
    nh = HG_HEADS

    def body(q_ref, f_ref, i_ref, lb_ref, st_ref, do_ref, sums_ref, sums_t_ref, pairs_ref,
             dq_ref, df_ref, di_ref, dlb_ref, ds_ref):
        step = pl.program_id(1)
        i = n - 1 - step

        @pl.when(step == 0)
        def _():
            ds_ref[...] = jnp.zeros_like(ds_ref)
            dlb_ref[...] = jnp.zeros_like(dlb_ref)

        valid = (i * CHUNK + lax.broadcasted_iota(jnp.int32, (CHUNK, 1), 0)) >= pad
        fn = functools.partial(_hg_chunk, valid=valid, sums=sums_ref[...], sums_t=sums_t_ref[...],
                               pairs=pairs_ref[...])
        st, dst = st_ref[...], ds_ref[...]
        _, vjp = jax.vjp(fn, q_ref[...], f_ref[...], i_ref[...], lb_ref[...], [st[h] for h in range(grp)])
        dq, df, di, dlb, ds = vjp((do_ref[...], [dst[h] for h in range(grp)]))
        dq_ref[...] = dq
        df_ref[...] = df
        di_ref[...] = di
        dlb_ref[...] += dlb
        for h in range(grp):
            ds_ref[h] = ds[h]

    masks = _hg_masks()
    grp, ngrp, gw = HG_GROUP, nh // HG_GROUP, HG_GROUP * HEAD_W
    blk = lambda off: pl.BlockSpec((CHUNK, gw), lambda h, s: (n - 1 - s, off + h))
    const = lambda a: pl.BlockSpec(a.shape, lambda h, s: (0, 0))
    w = nh * HEAD_W
    return pl.pallas_call(
        body, name=name, grid=(ngrp, n),
        in_specs=[blk(0), blk(ngrp), blk(2 * ngrp), pl.BlockSpec((1, gw), lambda h, s: (0, h)),
                  pl.BlockSpec((grp, None, HEAD_W, HEAD_W), lambda h, s: (h, n - 1 - s, 0, 0)), blk(0)]
        + [const(a) for a in masks],
        out_specs=[blk(0), blk(0), blk(0), pl.BlockSpec((1, gw), lambda h, s: (0, h))],
        out_shape=[jax.ShapeDtypeStruct((lp, w), F32)] * 3 + [jax.ShapeDtypeStruct((1, w), F32)],
        scratch_shapes=[pltpu.VMEM((grp, HEAD_W, HEAD_W), F32)],
        compiler_params=_cp(("parallel", "arbitrary")),
    )(p1, p1, p1, lb, states, do, *masks)


SB_GROUP = 4


def _sb_cat(kind, first_key=0):
    r = lax.broadcasted_iota(jnp.int32, (SB_BLOCK, 2 * SB_BLOCK), 0)
    c = lax.broadcasted_iota(jnp.int32, (SB_BLOCK, 2 * SB_BLOCK), 1)
    tri = {"after": c < r, "incl": r <= c, "before": r < c}[kind]
    m = ((c >= SB_BLOCK) | tri) & (r >= first_key)
    return jnp.where(m, 1.0, 0.0).astype(BF16)


def _sb_cumsum(x, cat):
    return _dot(x.astype(BF16), cat, NN)


def _sb_logsig(z):
    e = jnp.exp(-jnp.abs(z))
    lse = jnp.where(e < 1e-4, e, jnp.log(1.0 + e))
    lsz = jnp.minimum(z, 0.0) - lse
    return lsz, lsz - z, e


def _sb_stack(x, scale=None):
    lane = lax.broadcasted_iota(jnp.int32, (1, HEAD_W), 1)
    if scale is not None:
        x = x * scale
    return jnp.concatenate([jnp.where(lane < SB_DH, x, 0.0), jnp.where(lane >= SB_DH, x, 0.0)], axis=0).astype(BF16)


def _sb_unstack(x):
    lane = lax.broadcasted_iota(jnp.int32, (1, HEAD_W), 1)
    return jnp.where(lane < SB_DH, x[:SB_BLOCK], x[SB_BLOCK:])


def _sb_fwd(p0, pad, *, name, gather=None):
    lp = p0.shape[0]
    nb = lp // SB_BLOCK
    npair = SB_HEADS // 2
    blk0 = AB_SB // HEAD_W
    scale = SB_DH ** -0.5
    gw = SB_GROUP * SB_BLOCK
    assert pad < SB_BLOCK
    g_srcs, g_dtypes = gather if gather is not None else ([], [])
    ng_arr = len(g_srcs)

    def body(q_ref, k_ref, v_ref, *rest):
        g_ins, (o_ref, tot_ref) = rest[:ng_arr], rest[ng_arr:ng_arr + 2]
        g_outs, g_scratch = rest[ng_arr + 2:2 * ng_arr + 2], rest[2 * ng_arr + 2:]
        first_step = (pl.program_id(0) == 0) & (pl.program_id(1) == 0)
        last_pair = pl.program_id(0) == npair - 1
        if ng_arr:
            g_start, g_forward, g_finish = _gather_phases(g_ins, g_outs, g_scratch[:ng_arr], *g_scratch[ng_arr:],
                                                          g_dtypes)
            pl.when(first_step)(g_start)
            pl.when(last_pair & (pl.program_id(1) == 0))(g_forward)
        i = pl.program_id(1)
        qs = _sb_stack(q_ref[...], scale)
        qpos = i * SB_BLOCK + lax.broadcasted_iota(jnp.int32, (SB_BLOCK, 1), 0)
        qpos = jnp.concatenate([qpos, qpos], axis=0)
        cat = _sb_cat("after")
        cat0 = _sb_cat("after", pad)
        ng = i // SB_GROUP

        def group(off, first_cat, allowed, carry):
            acc, run = carry
            kg = k_ref[pl.ds(off, gw), :].astype(BF16)
            vg = v_ref[pl.ds(off, gw), :].astype(BF16)
            lsz, l1m, _ = _sb_logsig(_dot(qs, kg, NT))
            if allowed is not None:
                l1m = jnp.where(allowed, l1m, 0.0)
            args = [None] * SB_GROUP
            for g in reversed(range(SB_GROUP)):
                sl = slice(g * SB_BLOCK, (g + 1) * SB_BLOCK)
                al = _sb_cumsum(l1m[:, sl], first_cat if g == 0 else cat)
                args[g] = lsz[:, sl] + al[:, :SB_BLOCK] + run
                run = run + al[:, SB_BLOCK:]
            wgt = jnp.exp(jnp.concatenate(args, axis=1))
            if allowed is not None:
                wgt = jnp.where(allowed, wgt, 0.0)
            return acc + _dot(wgt.astype(BF16), vg, NN), run

        def below(t, carry):
            gi = ng - 1 - t
            return group(pl.multiple_of(gi * gw, gw), jnp.where(gi == 0, cat0, cat), None, carry)

        top = ng * gw
        off = pl.multiple_of(jnp.minimum(top, lp - gw), SB_BLOCK)
        kpos = off + lax.broadcasted_iota(jnp.int32, (1, gw), 1)
        allowed = (kpos < qpos) & (kpos >= pad) & (kpos >= top)
        zero = (jnp.zeros((2 * SB_BLOCK, HEAD_W), F32), jnp.zeros((2 * SB_BLOCK, HEAD_W), F32))
        carry = group(off, cat, allowed, zero)
        acc, run = lax.fori_loop(0, ng, below, carry)
        o_ref[...] = _sb_unstack(acc)
        tot_ref[...] = _sb_unstack(run)
        if ng_arr:
            pl.when(last_pair & (pl.program_id(1) == nb - 1))(g_finish)

    full = lambda c0: pl.BlockSpec((lp, HEAD_W), lambda p, i: (0, c0 + p))
    out = pl.BlockSpec((SB_BLOCK, HEAD_W), lambda p, i: (i, p))
    return pl.pallas_call(
        body, name=name, grid=(npair, nb),
        in_specs=[pl.BlockSpec((SB_BLOCK, HEAD_W), lambda p, i: (i, blk0 + p)), full(blk0 + npair), full(blk0 + 2 * npair)]
        + [pl.BlockSpec(memory_space=pltpu.VMEM)] * ng_arr,
        out_specs=[out, out] + [_ANY] * ng_arr,
        out_shape=[jax.ShapeDtypeStruct((lp, npair * HEAD_W), F32)] * 2 + _gather_out_shapes(g_srcs, g_dtypes),
        scratch_shapes=_gather_scratch(g_srcs, g_dtypes) if ng_arr else [],
        compiler_params=_cp(("arbitrary", "arbitrary"), has_side_effects=bool(ng_arr)),
    )(p0, p0, p0, *g_srcs)


def _sb_bwd(p0, tot, dsrc, d_blk0, pad, *, name, scatter=()):
    lp = p0.shape[0]
    nb = lp // SB_BLOCK
    npair = SB_HEADS // 2
    blk0 = AB_SB // HEAD_W
    scale = SB_DH ** -0.5
    gw = SB_GROUP * SB_BLOCK
    assert pad < SB_BLOCK
    ns = len(scatter)

    def body(q_ref, k_ref, v_ref, tot_ref, do_ref, *rest):
        s_ins, (dq_ref, dkt_ref, dvt_ref) = rest[:ns], rest[ns:ns + 3]
        s_outs, s_sems = rest[ns + 3:2 * ns + 3], rest[2 * ns + 3:]
        if ns:
            s_start, s_finish = _scatter_phases(s_ins, s_outs, *s_sems)
            pl.when((pl.program_id(0) == 0) & (pl.program_id(1) == 0))(s_start)
        i = pl.program_id(1)

        @pl.when(i == 0)
        def _():
            dkt_ref[...] = jnp.zeros_like(dkt_ref)
            dvt_ref[...] = jnp.zeros_like(dvt_ref)

        qs = _sb_stack(q_ref[...], scale)
        dos = _sb_stack(do_ref[...])
        qst, dost = qs.T, dos.T
        totv = tot_ref[...]
        ones = jnp.ones((1, HEAD_W), F32)
        tots = jnp.concatenate([totv[:, 0:1] * ones, totv[:, SB_DH:SB_DH + 1] * ones], axis=0)
        qpos = i * SB_BLOCK + lax.broadcasted_iota(jnp.int32, (SB_BLOCK, 1), 0)
        qpos = jnp.concatenate([qpos, qpos], axis=0)
        incl, incl0 = _sb_cat("incl"), _sb_cat("incl", pad)
        before = _sb_cat("before")
        ng = i // SB_GROUP

        def dscore(z, e, ev, dl1m):
            r = 1.0 / (1.0 + e)
            sg = jnp.where(z >= 0, r, e * r)
            return ev * (1.0 - sg) - dl1m * sg

        def group(off, first_incl, allowed, carry):
            dq, prun, erun = carry
            kg = k_ref[pl.ds(off, gw), :].astype(BF16)
            vg = v_ref[pl.ds(off, gw), :].astype(BF16)
            z = _dot(qs, kg, NT)
            lsz, l1m, e = _sb_logsig(z)
            if allowed is not None:
                l1m = jnp.where(allowed, l1m, 0.0)
            dwgt = _dot(dos, vg, NT)
            dzs = [None] * SB_GROUP
            wgts = [None] * SB_GROUP
            for g in range(SB_GROUP):
                sl = slice(g * SB_BLOCK, (g + 1) * SB_BLOCK)
                al = _sb_cumsum(l1m[:, sl], first_incl if g == 0 else incl)
                wgt = jnp.exp(lsz[:, sl] + (tots - prun - al[:, :SB_BLOCK]))
                if allowed is not None:
                    wgt = jnp.where(allowed[:, sl], wgt, 0.0)
                prun = prun + al[:, SB_BLOCK:]
                ev = wgt * dwgt[:, sl]
                el = _sb_cumsum(ev, before)
                dzs[g] = dscore(z[:, sl], e[:, sl], ev, erun + el[:, :SB_BLOCK])
                erun = erun + el[:, SB_BLOCK:]
                wgts[g] = wgt
            dz = jnp.concatenate(dzs, axis=1)
            if allowed is not None:
                dz = jnp.where(allowed, dz, 0.0)
            dz = dz.astype(BF16)
            wg = jnp.concatenate(wgts, axis=1).astype(BF16)
            dkt_ref[:, pl.ds(off, gw)] += _dot(qst, dz, NN)
            dvt_ref[:, pl.ds(off, gw)] += _dot(dost, wg, NN)
            return dq + _dot(dz, kg, NN), prun, erun

        def below(gi, carry):
            return group(pl.multiple_of(gi * gw, gw), jnp.where(gi == 0, incl0, incl), None, carry)

        zero = tuple(jnp.zeros((2 * SB_BLOCK, HEAD_W), F32) for _ in range(3))
        carry = lax.fori_loop(0, ng, below, zero)
        top = ng * gw
        off = pl.multiple_of(jnp.minimum(top, lp - gw), SB_BLOCK)
        kpos = off + lax.broadcasted_iota(jnp.int32, (1, gw), 1)
        allowed = (kpos < qpos) & (kpos >= pad) & (kpos >= top)
        dq, _, _ = group(off, incl, allowed, carry)
        dq_ref[...] = _sb_unstack(dq) * scale
        if ns:
            pl.when((pl.program_id(0) == npair - 1) & (pl.program_id(1) == nb - 1))(s_finish)

    full = lambda c0: pl.BlockSpec((lp, HEAD_W), lambda p, i: (0, c0 + p))
    qb = lambda c0: pl.BlockSpec((SB_BLOCK, HEAD_W), lambda p, i: (i, c0 + p))
    tr = pl.BlockSpec((HEAD_W, lp), lambda p, i: (p, 0))
    return pl.pallas_call(
        body, name=name, grid=(npair, nb),
        in_specs=[qb(blk0), full(blk0 + npair), full(blk0 + 2 * npair), qb(0), qb(d_blk0)] + [_ANY] * ns,
        out_specs=[qb(0), tr, tr] + [_ANY] * ns,
        out_shape=[jax.ShapeDtypeStruct((lp, npair * HEAD_W), F32)]
        + [jax.ShapeDtypeStruct((npair * HEAD_W, lp), F32)] * 2
        + [jax.ShapeDtypeStruct(s.shape, s.dtype) for s in scatter],
        scratch_shapes=_scatter_scratch(ns) if ns else [],
        compiler_params=_cp(("arbitrary", "arbitrary"), has_side_effects=bool(ns)),
    )(p0, p0, p0, tot, dsrc, *scatter)


def _local_step(h0, target, pad, wts, hooks=None):
    lp = h0.shape[0]
    tm = _row_tile(lp, 1056)
    tkl = tm
    tml = _row_tile(lp, 528)
    d = D_MODEL
    mm = _mm
    mmw = functools.partial(_mm, out_dtype=BF16)
    g = {}

    h0_b = h0.astype(BF16)
    p0 = mm(h0_b, wts["w_ab"], "NN", tm=tm, tn=768, tk=d, name="l0_in_proj")
    ob, sb_tot, *gathered = _sb_fwd(p0, pad, name="sb_fwd", gather=hooks["late_gather"] if hooks else None)
    if hooks:
        wts = {**wts, **hooks["later_weights"](gathered)}
    qkv = _gdn_pre_fwd(p0, wts["conv_w"], pad, name="gdn_pre_fwd")
    oa_raw, gdn_states = _gdn_fwd(qkv, p0, wts["alog_v"], wts["dtb_v"], pad, name="gdn_fwd")
    oab = _gate_fwd(oa_raw, p0, AB_Z // HEAD_W, wts["ab_gn"], ob, heads=GDN_HEADS, name="gdn_gate_fwd")
    ln = lambda kind, layer: (wts[f"ln_{kind}_g"][layer], wts[f"ln_{kind}_b"][layer])
    pre_mix0, h0a, h0a_b = mm(oab, wts["w_out0"], "NN", tm=tml, tn=d, tk=d, epi="ln", c=h0, scale=DN_ALPHA,
                              ln=ln("mix", 0), name="l0_out_proj")
    u0 = mm(h0a_b, wts["w1"][0], "NN", tm=tm, tn=512, tk=d, b_dev=True, name="mlp0_up")
    pre_ffn0, h0b, h0b_b = mm(u0, wts["w2"][0], "NN", tm=tml, tn=d, tk=d, a_fn="relu2", epi="ln", c=h0a,
                              scale=DN_ALPHA, ln=ln("ffn", 0), name="mlp0_down")
    p1 = mm(h0b_b, wts["w_c"], "NN", tm=tm, tn=512, tk=d, b_dev=True, name="l1_in_proj")
    oc_raw, hg_states = _hg_fwd(p1, wts["lb"], pad, name="hg_fwd")
    oc = _gate_fwd(oc_raw, p1, 3 * HG_HEADS, wts["c_gn"], oc_raw, heads=HG_HEADS, name="hg_gate_fwd")
    pre_mix1, h1a, h1a_b = mm(oc, wts["w_out1"], "NN", tm=tml, tn=d, tk=d, epi="ln", c=h0b, scale=DN_ALPHA,
                              ln=ln("mix", 1), name="l1_out_proj")
    u1 = mm(h1a_b, wts["w1"][1], "NN", tm=tm, tn=512, tk=d, b_dev=True, name="mlp1_up")
    pre_ffn1, h1b, _ = mm(u1, wts["w2"][1], "NN", tm=tml, tn=d, tk=d, a_fn="relu2", epi="ln", c=h1a, scale=DN_ALPHA,
                          ln=ln("ffn", 1), name="mlp1_down")
    dy, loss_vec = _loss_head(h1b, target, name="loss_head")

    def mlp_bwd(layer, h_in_b, u, dpre, dpre_b):
        du = mm(dpre_b, wts["w2"][layer], "NT", tm=tm, tn=1024, tk=d, epi="relu2grad", c=u, out_dtype=BF16,
                name=f"mlp{layer}_d_hidden")
        dw2 = mmw(u, dpre_b, "TN", tm=1024, tn=1024, tk=tkl, a_fn="relu2", name=f"mlp{layer}_dw2")
        dw1 = mmw(h_in_b, du, "TN", tm=1024, tn=512, tk=tkl, out_dev=True, name=f"mlp{layer}_dw1")
        dh = mm(du, wts["w1"][layer], "NT", tm=tm, tn=1024, tk=512, b_dev=True, epi="add", c=dpre, scale=DN_ALPHA,
                name=f"mlp{layer}_d_in")
        return dh, dw1, dw2

    ln_ffn_dg, ln_ffn_db, ln_mix_dg, ln_mix_db, dw1s, dw2s = ([None, None] for _ in range(6))
    dpre, dpre_b, ln_ffn_dg[1], ln_ffn_db[1] = _ln_bwd(pre_ffn1, wts["ln_ffn_g"][1], dy, name="ln_ffn1_bwd")
    dh1a, dw1s[1], dw2s[1] = mlp_bwd(1, h1a_b, u1, dpre, dpre_b)
    dpre, dpre_b, ln_mix_dg[1], ln_mix_db[1] = _ln_bwd(pre_mix1, wts["ln_mix_g"][1], dh1a, name="ln_mix1_bwd")
    g["c_w_out"] = mmw(oc, dpre_b, "TN", tm=1024, tn=1024, tk=tkl, name="l1_dw_out")
    doc = mm(dpre_b, wts["w_out1"], "NT", tm=tm, tn=1024, tk=d, name="l1_d_gate")
    doc_raw, dz1, g["c_gn"] = _gate_bwd(oc_raw, p1, 3 * HG_HEADS, wts["c_gn"], doc, heads=HG_HEADS, name="hg_gate_bwd")
    dq1, df1, di1, g["lb"] = _hg_bwd(p1, wts["lb"], hg_states, doc_raw, pad, name="hg_bwd")
    dp1 = jnp.concatenate([dq1, df1, di1, dz1], axis=1).astype(BF16)
    g["c_w_in"] = mmw(h0b_b, dp1, "TN", tm=1024, tn=512, tk=tkl, out_dev=True, name="l1_dw_in")
    dh0b = mm(dp1, wts["w_c"], "NT", tm=tm, tn=1024, tk=512, b_dev=True, epi="add", c=dpre, scale=DN_ALPHA,
              name="l1_d_in")
    dpre, dpre_b, ln_ffn_dg[0], ln_ffn_db[0] = _ln_bwd(pre_ffn0, wts["ln_ffn_g"][0], dh0b, name="ln_ffn0_bwd")
    dh0a, dw1s[0], dw2s[0] = mlp_bwd(0, h0a_b, u0, dpre, dpre_b)
    dpre, dpre_b, ln_mix_dg[0], ln_mix_db[0] = _ln_bwd(pre_mix0, wts["ln_mix_g"][0], dh0a, name="ln_mix0_bwd")
    g["ab_w_out"] = mmw(oab, dpre_b, "TN", tm=1024, tn=1024, tk=tkl, name="l0_dw_out")
    doab = mm(dpre_b, wts["w_out0"], "NT", tm=tm, tn=1024, tk=d, name="l0_d_gate")
    doa_raw, dz0, g["ab_gn"] = _gate_bwd(oa_raw, p0, AB_Z // HEAD_W, wts["ab_gn"], doab, heads=GDN_HEADS,
                                         name="gdn_gate_bwd")
    early = ()
    if hooks:
        rows = lambda a, n: a.reshape(N_DEV, n // N_DEV, d)
        early = [g["c_w_in"], rows(g["c_w_out"], d), dw1s[1], rows(dw2s[1], D_FF), dw1s[0], rows(dw2s[0], D_FF),
                 rows(g["ab_w_out"], d)]
    dqb, dkb_t, dvb_t, *g["early_parts"] = _sb_bwd(p0, sb_tot, doab, GDN_HEADS, pad, name="sb_bwd", scatter=early)
    dkb, dvb = dkb_t.T, dvb_t.T
    dqn, dkn, dvn, dba, g["alog_v"], g["dtb_v"] = _gdn_bwd(qkv, p0, wts["alog_v"], wts["dtb_v"], gdn_states, doa_raw,
                                                           pad, name="gdn_bwd")
    dconv_in, g["conv_w"] = _gdn_pre_bwd(p0, wts["conv_w"], jnp.concatenate([dqn, dkn, dvn], axis=1), pad,
                                         name="gdn_pre_bwd")
    dp0 = jnp.concatenate([dconv_in, dz0, dqb, dkb, dvb, dba, jnp.zeros((lp, AB_CAT - AB_BA - HEAD_W), F32)],
                          axis=1).astype(BF16)
    g["w_ab"] = mmw(h0_b, dp0, "TN", tm=1024, tn=768, tk=tkl, name="l0_dw_in")
    dh0 = mm(dp0, wts["w_ab"], "NT", tm=tm, tn=1024, tk=768, epi="add", c=dpre, scale=DN_ALPHA, name="l0_d_in")

    g["w1"], g["w2"] = dw1s, dw2s
    g["ln_mix_g"] = jnp.concatenate(ln_mix_dg, axis=0)
    g["ln_mix_b"] = jnp.concatenate(ln_mix_db, axis=0)
    g["ln_ffn_g"] = jnp.concatenate(ln_ffn_dg, axis=0)
    g["ln_ffn_b"] = jnp.concatenate(ln_ffn_db, axis=0)
    return loss_vec, dh0, g


N_CHIP = N_DEV // 2


def _place():
    x, y, c = lax.axis_index("x"), lax.axis_index("y"), lax.axis_index("c")
    return x, y, c, 2 * x + y


def _chip_dev(chip, core):
    return (chip // 2, chip % 2, core)


def _remote(src, dst, send_sem, recv_sem, dev):
    return pltpu.make_async_remote_copy(src_ref=src, dst_ref=dst, send_sem=send_sem, recv_sem=recv_sem,
                                        device_id=dev, device_id_type=pl.DeviceIdType.MESH)


_ANY = pl.BlockSpec(memory_space=pl.ANY)


def _gather(srcs, dtypes, *, name):
    n = len(srcs)

    def body(*refs):
        start, forward, finish = _gather_phases(refs[:n], refs[n:2 * n], refs[2 * n:3 * n], *refs[3 * n:], dtypes)
        start()
        forward()
        finish()

    return pl.pallas_call(
        body, name=name, in_specs=[pl.BlockSpec(memory_space=pltpu.VMEM)] * n, out_specs=[_ANY] * n,
        out_shape=_gather_out_shapes(srcs, dtypes), scratch_shapes=_gather_scratch(srcs, dtypes),
        compiler_params=_cp(has_side_effects=True),
    )(*srcs)


def _gather_out_shapes(srcs, dtypes):
    return [jax.ShapeDtypeStruct((N_DEV, *s.shape), dt) for s, dt in zip(srcs, dtypes)]


def _gather_scratch(srcs, dtypes):
    n = len(srcs)
    return [pltpu.VMEM(s.shape, dt) for s, dt in zip(srcs, dtypes)] + [
        pltpu.SemaphoreType.DMA((n, 2 * N_CHIP - 1)), pltpu.SemaphoreType.DMA((n, 2 * N_CHIP - 1)),
        pltpu.SemaphoreType.DMA((n,))]


def _gather_phases(ins, outs, stages, send_sems, recv_sems, local_sems, dtypes):
    n = len(ins)
    x, y, c, chip = _place()
    me = 2 * chip + c
    sibling = (x, y, 1 - c)

    def own(i):
        cps = [_remote(stages[i], outs[i].at[me], send_sems.at[i, 0], recv_sems.at[i, 0], sibling)]
        for j in range(1, N_CHIP):
            cps.append(_remote(stages[i], outs[i].at[me], send_sems.at[i, j], recv_sems.at[i, j],
                               _chip_dev(jnp.bitwise_xor(chip, j), c)))
        return cps

    def local(i):
        return pltpu.make_async_copy(stages[i], outs[i].at[me], local_sems.at[i])

    def passed_on(i, j):
        slot = outs[i].at[2 * jnp.bitwise_xor(chip, j) + c]
        return _remote(slot, slot, send_sems.at[i, N_CHIP - 1 + j], recv_sems.at[i, N_CHIP - 1 + j], sibling)

    def start():
        for i in range(n):
            stages[i][...] = ins[i][...].astype(dtypes[i])
            local(i).start()
            for cp in own(i):
                cp.start()

    def forward():
        for i in range(n):
            for j in range(1, N_CHIP):
                own(i)[j].wait_recv()
                passed_on(i, j).start()

    def finish():
        for i in range(n):
            own(i)[0].wait_recv()
            for j in range(1, N_CHIP):
                passed_on(i, j).wait_recv()
        for i in range(n):
            for cp in own(i):
                cp.wait_send()
            for j in range(1, N_CHIP):
                passed_on(i, j).wait_send()
            local(i).wait()

    return start, forward, finish


def _scatter(parts, *, name):
    n = len(parts)

    def body(*refs):
        start, finish = _scatter_phases(refs[:n], refs[n:2 * n], *refs[2 * n:])
        start()
        finish()

    return pl.pallas_call(
        body, name=name, in_specs=[_ANY] * n, out_specs=[_ANY] * n,
        out_shape=[jax.ShapeDtypeStruct(p.shape, p.dtype) for p in parts], scratch_shapes=_scatter_scratch(n),
        compiler_params=_cp(has_side_effects=True),
    )(*parts)


def _scatter_scratch(n):
    return [pltpu.SemaphoreType.DMA((n, N_DEV - 1)), pltpu.SemaphoreType.DMA((n, N_DEV - 1)),
            pltpu.SemaphoreType.DMA((n,))]


def _scatter_phases(ins, outs, send_sems, recv_sems, local_sems):
    n = len(ins)
    _, _, c, chip = _place()
    me = 2 * chip + c

    def copies():
        cps = []
        for i in range(n):
            cps.append(pltpu.make_async_copy(ins[i].at[me], outs[i].at[me], local_sems.at[i]))
            for k in range(1, N_DEV):
                peer = jnp.bitwise_xor(me, k)
                cps.append(_remote(ins[i].at[peer], outs[i].at[me], send_sems.at[i, k - 1], recv_sems.at[i, k - 1],
                                   _chip_dev(peer // 2, peer % 2)))
        return cps

    def start():
        for cp in copies():
            cp.start()

    def finish():
        for cp in copies():
            cp.wait()

    return start, finish


def _adamw(w, parts, m, v, *, name):
    r, c = w.shape
    s = parts.shape[0]
    tm = _row_tile(r, 128) if r % 8 == 0 else r
    c1 = 1.0 - ADAM_B1 ** ADAM_STEP
    c2 = 1.0 - ADAM_B2 ** ADAM_STEP

    def body(w_ref, p_ref, m_ref, v_ref, g_ref, d_ref, m2_ref, v2_ref):
        g = p_ref[0].astype(F32)
        for j in range(1, s):
            g = g + p_ref[j].astype(F32)
        m2 = ADAM_B1 * m_ref[...] + (1.0 - ADAM_B1) * g
        v2 = ADAM_B2 * v_ref[...] + (1.0 - ADAM_B2) * jnp.square(g)
        g_ref[...] = g
        m2_ref[...] = m2
        v2_ref[...] = v2
        d_ref[...] = -ADAM_LR * ((m2 / c1) / (jnp.sqrt(v2 / c2) + ADAM_EPS) + ADAM_WD * w_ref[...])

    blk = pl.BlockSpec((tm, c), lambda i: (i, 0))
    return pl.pallas_call(
        body, name=name, grid=(r // tm,),
        in_specs=[blk, pl.BlockSpec((s, tm, c), lambda i: (0, i, 0)), blk, blk], out_specs=[blk] * 4,
        out_shape=[jax.ShapeDtypeStruct((r, c), F32)] * 4, compiler_params=_cp(("parallel",)),
    )(w, parts, m, v)


_WEIGHTS = ("meta_tokens", "ab_w_in", "ab_conv_w", "ab_a_log", "ab_dt_bias", "ab_gnorm_g", "ab_w_out", "c_w_in",
            "c_lb_raw", "c_gnorm_g", "c_w_out", "ln_mix_g", "ln_mix_b", "mlp_w1", "mlp_w2", "ln_ffn_g", "ln_ffn_b")
_PACK_ROWS = (("ln_mix_g", 0), ("ln_mix_b", 2), ("ln_ffn_g", 4), ("ln_ffn_b", 6), ("c_lb_raw", 8))
_PACK_MISC_ROW = 10
_PACK_MISC = (("ab_gnorm_g", 0, 128), ("c_gnorm_g", 128, 128), ("ab_a_log", 256, GDN_HEADS), ("ab_dt_bias", 260, GDN_HEADS))
_PACK_N = 16
_SMALL_META = 16
_SMALL_CONV = 32
_SMALL_N = 40


def _pack_replicated(p):
    rows = jnp.zeros((_PACK_N, D_MODEL), F32)
    for name, r0 in _PACK_ROWS:
        rows = rows.at[r0:r0 + 2].set(p[name])
    for name, c0, width in _PACK_MISC:
        rows = rows.at[_PACK_MISC_ROW, c0:c0 + width].set(p[name].reshape(width))
    return rows


def _unpack_replicated(rows, like):
    out = {}
    for name, r0 in _PACK_ROWS:
        out[name] = rows[r0:r0 + 2]
    for name, c0, width in _PACK_MISC:
        out[name] = rows[_PACK_MISC_ROW, c0:c0 + width].reshape(like[name].shape)
    return out


def _lower_bound(c_lb_raw):
    lb_all = jnp.cumsum(jax.nn.softmax(c_lb_raw.astype(F32), axis=0), axis=0)
    return (lb_all - lb_all[0:1])[1].reshape(1, -1)


def kernel(x, meta_tokens, ab_w_in, ab_conv_w, ab_a_log, ab_dt_bias, ab_gnorm_g, ab_w_out, c_w_in, c_lb_raw, c_gnorm_g, c_w_out, ln_mix_g, ln_mix_b, mlp_w1, mlp_w2, ln_ffn_g, ln_ffn_b, loss_target, m_meta_tokens, m_ab_w_in, m_ab_conv_w, m_ab_a_log, m_ab_dt_bias, m_ab_gnorm_g, m_ab_w_out, m_c_w_in, m_c_lb_raw, m_c_gnorm_g, m_c_w_out, m_ln_mix_g, m_ln_mix_b, m_mlp_w1, m_mlp_w2, m_ln_ffn_g, m_ln_ffn_b, v_meta_tokens, v_ab_w_in, v_ab_conv_w, v_ab_a_log, v_ab_dt_bias, v_ab_gnorm_g, v_ab_w_out, v_c_w_in, v_c_lb_raw, v_c_gnorm_g, v_c_w_out, v_ln_mix_g, v_ln_mix_b, v_mlp_w1, v_mlp_w2, v_ln_ffn_g, v_ln_ffn_b):
    w = dict(zip(_WEIGHTS, (meta_tokens, ab_w_in, ab_conv_w, ab_a_log, ab_dt_bias, ab_gnorm_g, ab_w_out, c_w_in, c_lb_raw,
                            c_gnorm_g, c_w_out, ln_mix_g, ln_mix_b, mlp_w1, mlp_w2, ln_ffn_g, ln_ffn_b)))
    mom = dict(zip(_WEIGHTS, (m_meta_tokens, m_ab_w_in, m_ab_conv_w, m_ab_a_log, m_ab_dt_bias, m_ab_gnorm_g, m_ab_w_out,
                              m_c_w_in, m_c_lb_raw, m_c_gnorm_g, m_c_w_out, m_ln_mix_g, m_ln_mix_b, m_mlp_w1, m_mlp_w2,
                              m_ln_ffn_g, m_ln_ffn_b)))
    var = dict(zip(_WEIGHTS, (v_meta_tokens, v_ab_w_in, v_ab_conv_w, v_ab_a_log, v_ab_dt_bias, v_ab_gnorm_g, v_ab_w_out,
                              v_c_w_in, v_c_lb_raw, v_c_gnorm_g, v_c_w_out, v_ln_mix_g, v_ln_mix_b, v_mlp_w1, v_mlp_w2,
                              v_ln_ffn_g, v_ln_ffn_b)))
    me = 4 * lax.axis_index("x") + 2 * lax.axis_index("y") + lax.axis_index("c")
    seq = x.shape[1]
    pad = (-(N_META + seq)) % SB_BLOCK
    lp = pad + N_META + seq
    meta_w = D_MODEL // N_DEV
    conv_w_all = 2 * GDN_HEADS * HEAD_W + GDN_HEADS * HEAD_W
    conv_w_mine = conv_w_all // N_DEV

    g_meta, g_conv, g_ab_in = _gather([w["meta_tokens"], w["ab_conv_w"][0], w["ab_w_in"][0]], [F32, F32, BF16],
                                      name="gather_weights_first")
    meta_full = g_meta.transpose(1, 0, 2).reshape(N_META, D_MODEL)
    conv_full = g_conv.transpose(1, 0, 2).reshape(CONV_K, conv_w_all)
    ab_full = g_ab_in.transpose(1, 0, 2).reshape(D_MODEL, AB_IN)
    ba0 = AB_Z + 512
    w_ab = jnp.concatenate([ab_full[:, :ba0], ab_full[:, ba0 + 2 * GDN_HEADS:], ab_full[:, ba0:ba0 + 2 * GDN_HEADS],
                            jnp.zeros((D_MODEL, AB_CAT - AB_IN), BF16)], axis=1)
    vec128 = lambda p: jnp.zeros((1, HEAD_W), F32).at[0, :GDN_HEADS].set(p.reshape(GDN_HEADS))
    wts = dict(
        w_ab=w_ab, conv_w=conv_full, alog_v=vec128(w["ab_a_log"]), dtb_v=vec128(w["ab_dt_bias"]),
        ab_gn=w["ab_gnorm_g"][0], lb=_lower_bound(w["c_lb_raw"]), c_gn=w["c_gnorm_g"][0],
        ln_mix_g=w["ln_mix_g"], ln_mix_b=w["ln_mix_b"], ln_ffn_g=w["ln_ffn_g"], ln_ffn_b=w["ln_ffn_b"])

    def later_weights(gathered):
        g_ab_out, g_c_in, g_c_out, g_w1, g_w2 = gathered
        return dict(w_out0=g_ab_out.reshape(D_MODEL, D_MODEL), w_c=g_c_in, w_out1=g_c_out.reshape(D_MODEL, D_MODEL),
                    w1=[g_w1[:, l] for l in range(DEPTH)], w2=[g_w2[:, l].reshape(D_FF, D_MODEL) for l in range(DEPTH)])

    hooks = dict(
        late_gather=([w["ab_w_out"][0], w["c_w_in"][0], w["c_w_out"][0], w["mlp_w1"], w["mlp_w2"]], [BF16] * 5),
        later_weights=later_weights)

    h0 = jnp.concatenate([jnp.zeros((pad, D_MODEL), F32), meta_full, x[0]], axis=0)
    loss_vec, dh0, g = _local_step(h0, loss_target[0], pad, wts, hooks)
    loss = lax.psum(jnp.sum(loss_vec), ("x", "y", "c"))
    grad_x = dh0[lp - seq:][None]

    _, lb_vjp = jax.vjp(_lower_bound, w["c_lb_raw"])
    rep_part = _pack_replicated(dict(
        ln_mix_g=g["ln_mix_g"], ln_mix_b=g["ln_mix_b"], ln_ffn_g=g["ln_ffn_g"], ln_ffn_b=g["ln_ffn_b"],
        c_lb_raw=lb_vjp(g["lb"])[0], ab_gnorm_g=g["ab_gn"], c_gnorm_g=g["c_gn"],
        ab_a_log=g["alog_v"][0, :GDN_HEADS], ab_dt_bias=g["dtb_v"][0, :GDN_HEADS]))
    small = jnp.concatenate([rep_part, dh0[pad:pad + N_META], g["conv_w"].reshape(-1, D_MODEL),
                             jnp.zeros((_SMALL_N - _SMALL_CONV - CONV_K * conv_w_all // D_MODEL, D_MODEL), F32)], axis=0)
    (small_all,) = _gather([small], [F32], name="gather_small_grads")
    rep_out = _adamw(_pack_replicated(w), small_all[:, :_PACK_N], _pack_replicated(mom), _pack_replicated(var),
                     name="adamw_replicated")
    meta_parts = lax.dynamic_slice_in_dim(small_all[:, _SMALL_META:_SMALL_META + N_META], me * meta_w, meta_w, axis=2)
    meta_out = _adamw(w["meta_tokens"], meta_parts, mom["meta_tokens"], var["meta_tokens"], name="adamw_meta")
    conv_parts = small_all[:, _SMALL_CONV:_SMALL_CONV + CONV_K * conv_w_all // D_MODEL].reshape(N_DEV, CONV_K, conv_w_all)
    conv_parts = lax.dynamic_slice_in_dim(conv_parts, me * conv_w_mine, conv_w_mine, axis=2)
    conv_out = _adamw(w["ab_conv_w"][0], conv_parts, mom["ab_conv_w"][0], var["ab_conv_w"][0], name="adamw_conv")

    gab = g["w_ab"]
    gab = jnp.concatenate([gab[:, :ba0], gab[:, AB_BA:AB_BA + 2 * GDN_HEADS], gab[:, ba0:AB_BA]], axis=1)
    (ab_in_parts,) = _scatter([gab.reshape(D_MODEL, N_DEV, AB_IN // N_DEV).transpose(1, 0, 2)], name="scatter_last")
    early = g["early_parts"]
    big = [("ab_w_in", 0, ab_in_parts), ("ab_w_out", 0, early[6]), ("mlp_w1", 0, early[4]), ("mlp_w2", 0, early[5]),
           ("c_w_in", 0, early[0]), ("c_w_out", 0, early[1]), ("mlp_w1", 1, early[2]), ("mlp_w2", 1, early[3])]
    big_out = {}
    for name, l, p in big:
        res = _adamw(w[name][l], p, mom[name][l], var[name][l], name=f"adamw_{name}{l}")
        big_out.setdefault(name, []).append(res)

    rep = [_unpack_replicated(r, w) for r in rep_out]
    outs = {}
    for name in _WEIGHTS:
        if name == "meta_tokens":
            outs[name] = list(meta_out)
        elif name == "ab_conv_w":
            outs[name] = [o[None] for o in conv_out]
        elif name in big_out:
            res = big_out[name]
            outs[name] = [o[None] for o in res[0]] if len(res) == 1 else [jnp.stack(pair) for pair in zip(*res)]
        else:
            outs[name] = [r[name] for r in rep]
    flat = [loss, grad_x]
    for kind in range(4):
        flat += [outs[name][kind] for name in _WEIGHTS]
    return tuple(flat)
```

```python
import functools
import math

import jax
import jax.numpy as jnp
from jax import lax
from jax.experimental import pallas as pl
from jax.experimental.pallas import tpu as pltpu

F32 = jnp.float32
BF16 = jnp.bfloat16
HI = lax.Precision.HIGHEST

N_DEV = 8
D_MODEL = 1024
N_META = 16
D_FF = 4096
DEPTH = 2
GDN_HEADS = 4
SB_HEADS = 8
SB_DH = 64
HG_HEADS = 8
HEAD_W = 128
CHUNK = 64
SB_BLOCK = 128
CONV_K = 4
DN_ALPHA = float((2 * DEPTH) ** 0.25)
LN_EPS = 1e-5
RMS_EPS = 1e-6
L2_EPS = 1e-6
ADAM_LR, ADAM_B1, ADAM_B2, ADAM_EPS, ADAM_WD, ADAM_STEP = 0.001, 0.9, 0.999, 1e-08, 0.01, 10

AB_QKV = 0
AB_Z = 1536
AB_SB = 2048
AB_BA = 3584
AB_CAT = 3840
AB_IN = 3592

VMEM_LIMIT = 56 * 1024 * 1024


def _cp(sem=None, **kw):
    if sem is not None:
        kw["dimension_semantics"] = sem
    return pltpu.CompilerParams(vmem_limit_bytes=VMEM_LIMIT, **kw)


def _row_tile(n, want):
    best = 8
    for t in range(8, min(n, want) + 1, 8):
        if n % t == 0:
            best = t
    return best


@jax.custom_vjp
def _sigmoid(x):
    e = jnp.exp(-jnp.abs(x))
    r = 1.0 / (1.0 + e)
    return jnp.where(x >= 0, r, e * r)


def _sigmoid_fwd(x):
    s = _sigmoid(x)
    return s, s


def _sigmoid_bwd(s, g):
    return (g * s * (1.0 - s),)


_sigmoid.defvjp(_sigmoid_fwd, _sigmoid_bwd)


def _log1p_exp_neg_abs(x):
    e = jnp.exp(-jnp.abs(x))
    return jnp.where(e < 1e-4, e - 0.5 * e * e, jnp.log(1.0 + e))


@jax.custom_vjp
def _softplus(x):
    return jnp.maximum(x, 0.0) + _log1p_exp_neg_abs(x)


def _softplus_fwd(x):
    return _softplus(x), x


def _softplus_bwd(x, g):
    return (g * _sigmoid(x),)


_softplus.defvjp(_softplus_fwd, _softplus_bwd)


def _silu(x):
    return x * _sigmoid(x)


def _silu_grad(x):
    s = _sigmoid(x)
    return s * (1.0 + x * (1.0 - s))


def _dot(a, b, dims, precision=None):
    return lax.dot_general(a, b, (dims, ((), ())), precision=precision, preferred_element_type=F32)


NN = ((1,), (0,))
NT = ((1,), (1,))
TN = ((0,), (0,))


def _bdot(a, b, dims):
    return _dot(a.astype(BF16), b.astype(BF16), dims)


def _layer_norm(pre, g, beta):
    mu = jnp.mean(pre, axis=-1, keepdims=True)
    xc = pre - mu
    var = jnp.mean(xc * xc, axis=-1, keepdims=True)
    return xc * lax.rsqrt(var + LN_EPS) * g + beta


def _mm(a, b, mode, *, tm, tn, tk, name, a_fn=None, epi=None, c=None, scale=1.0, b_dev=False, out_dev=False,
        out_dtype=F32, ln=None):
    if mode == "NN":
        m, kk = a.shape
        n = b.shape[2] * N_DEV if b_dev else b.shape[1]
    elif mode == "NT":
        m, kk = a.shape
        n = b.shape[1] if b_dev else b.shape[0]
    else:
        kk, m = a.shape
        n = b.shape[1]
    assert m % tm == 0 and n % tn == 0 and kk % tk == 0, (name, m, n, kk, tm, tn, tk)
    nk = kk // tk
    dims = {"NN": NN, "NT": NT, "TN": TN}[mode]

    if mode == "TN":
        a_spec = pl.BlockSpec((tk, tm), lambda i, j, k: (k, i))
    else:
        a_spec = pl.BlockSpec((tm, tk), lambda i, j, k: (i, k))
    if mode == "NN":
        if b_dev:
            assert tn == b.shape[2]
            b_spec = pl.BlockSpec((None, tk, tn), lambda i, j, k: (j, k, 0))
        else:
            b_spec = pl.BlockSpec((tk, tn), lambda i, j, k: (k, j))
    elif mode == "NT":
        if b_dev:
            assert tk == b.shape[2]
            b_spec = pl.BlockSpec((None, tn, tk), lambda i, j, k: (k, j, 0))
        else:
            b_spec = pl.BlockSpec((tn, tk), lambda i, j, k: (j, k))
    else:
        b_spec = pl.BlockSpec((tk, tn), lambda i, j, k: (k, j))
    in_specs = [a_spec, b_spec]
    operands = [a, b]
    if epi is not None:
        in_specs.append(pl.BlockSpec((tm, tn), lambda i, j, k: (i, j)))
        operands.append(c)
    if epi == "ln":
        assert tn == n and not out_dev
        in_specs += [pl.BlockSpec((1, n), lambda i, j, k: (0, 0))] * 2
        operands += [ln[0].reshape(1, n), ln[1].reshape(1, n)]
    if out_dev:
        assert tn == n // N_DEV
        out_shape = jax.ShapeDtypeStruct((N_DEV, m, tn), out_dtype)
        out_spec = pl.BlockSpec((None, tm, tn), lambda i, j, k: (j, i, 0))
    else:
        out_shape = jax.ShapeDtypeStruct((m, n), out_dtype)
        out_spec = pl.BlockSpec((tm, tn), lambda i, j, k: (i, j))
    if epi == "ln":
        out_shape = [out_shape, out_shape, jax.ShapeDtypeStruct((m, n), BF16)]
        out_spec = [out_spec] * 3
    n_in = len(operands)
    n_out = 3 if epi == "ln" else 1

    def body(*refs):
        a_ref, b_ref = refs[0], refs[1]
        c_ref = refs[2] if epi is not None else None
        o_ref = refs[n_in]
        acc_ref = refs[-1] if nk > 1 else None
        av = a_ref[...]
        if a_fn == "relu2":
            av = jnp.square(jnp.maximum(av, 0.0))
        p = _dot(av.astype(BF16), b_ref[...].astype(BF16), dims)

        def finish(acc):
            if epi == "add":
                acc = acc + scale * c_ref[...]
            elif epi == "relu2grad":
                acc = acc * (2.0 * jnp.maximum(c_ref[...], 0.0))
            elif epi == "ln":
                acc = acc + scale * c_ref[...]
                y = _layer_norm(acc, refs[3][...], refs[4][...])
                refs[n_in + 1][...] = y
                refs[n_in + 2][...] = y.astype(BF16)
            o_ref[...] = acc.astype(out_dtype)

        if nk == 1:
            finish(p)
        else:
            k = pl.program_id(2)

            @pl.when(k == 0)
            def _():
                acc_ref[...] = p

            @pl.when(k > 0)
            def _():
                acc_ref[...] += p

            @pl.when(k == nk - 1)
            def _():
                finish(acc_ref[...])

    return pl.pallas_call(
        body, name=name, grid=(m // tm, n // tn, nk), in_specs=in_specs, out_specs=out_spec, out_shape=out_shape,
        scratch_shapes=[pltpu.VMEM((tm, tn), F32)] if nk > 1 else [],
        compiler_params=_cp(("parallel", "parallel", "arbitrary")),
    )(*operands)


def _ln_bwd(pre, g, dy, *, name):
    lp, d = pre.shape
    tm = _row_tile(lp, 512)

    def body(pre_ref, g_ref, dy_ref, dpre_ref, dpreb_ref, dg_ref, db_ref):
        pre = pre_ref[...]
        mu = jnp.mean(pre, axis=-1, keepdims=True)
        xc = pre - mu
        var = jnp.mean(xc * xc, axis=-1, keepdims=True)
        rstd = lax.rsqrt(var + LN_EPS)
        xhat = xc * rstd
        dyv = dy_ref[...]
        dxh = dyv * g_ref[...]
        m1 = jnp.mean(dxh, axis=-1, keepdims=True)
        m2 = jnp.mean(dxh * xhat, axis=-1, keepdims=True)
        dpre = rstd * (dxh - m1 - xhat * m2)
        dpre_ref[...] = dpre
        dpreb_ref[...] = dpre.astype(BF16)

        @pl.when(pl.program_id(0) == 0)
        def _():
            dg_ref[...] = jnp.zeros_like(dg_ref)
            db_ref[...] = jnp.zeros_like(db_ref)

        dg_ref[...] += jnp.sum(dyv * xhat, axis=0, keepdims=True)
        db_ref[...] += jnp.sum(dyv, axis=0, keepdims=True)

    row = pl.BlockSpec((tm, d), lambda i: (i, 0))
    vec = pl.BlockSpec((1, d), lambda i: (0, 0))
    return pl.pallas_call(
        body, name=name, grid=(lp // tm,), in_specs=[row, vec, row], out_specs=[row, row, vec, vec],
        out_shape=[jax.ShapeDtypeStruct((lp, d), F32), jax.ShapeDtypeStruct((lp, d), BF16),
                   jax.ShapeDtypeStruct((1, d), F32), jax.ShapeDtypeStruct((1, d), F32)],
        compiler_params=_cp(("arbitrary",)),
    )(pre, g.reshape(1, d), dy)


def _loss_head(y, target, *, name):
    lp, d = y.shape
    seq = target.shape[0]
    tm = SB_BLOCK
    first = (lp - seq) // tm
    assert (lp - seq) % tm == 0 and seq % tm == 0

    def body(y_ref, t_ref, dy_ref, loss_ref):
        i = pl.program_id(0)
        live = i >= first
        diff = jnp.where(live, y_ref[...] - t_ref[...], 0.0)
        dy_ref[...] = diff * (1.0 / d)

        @pl.when(i == 0)
        def _():
            loss_ref[...] = jnp.zeros_like(loss_ref)

        loss_ref[...] += jnp.sum(diff * diff, axis=0, keepdims=True) * (0.5 / d)

    return pl.pallas_call(
        body, name=name, grid=(lp // tm,),
        in_specs=[pl.BlockSpec((tm, d), lambda i: (i, 0)),
                  pl.BlockSpec((tm, d), lambda i: (jnp.maximum(i - first, 0), 0))],
        out_specs=[pl.BlockSpec((tm, d), lambda i: (i, 0)), pl.BlockSpec((1, d), lambda i: (0, 0))],
        out_shape=[jax.ShapeDtypeStruct((lp, d), F32), jax.ShapeDtypeStruct((1, d), F32)],
        compiler_params=_cp(("arbitrary",)),
    )(y, target)


def _gate_fwd(o, zsrc, z_blk0, g, other, *, heads, name):
    lp = o.shape[0]
    tm = _row_tile(lp, 512)
    w = heads * HEAD_W
    assert (z_blk0 * HEAD_W) % w == 0
    has_other = w < D_MODEL

    def body(o_ref, z_ref, g_ref, *rest):
        y_ref = rest[-1]
        gv = g_ref[...]
        for h in range(heads):
            cs = slice(h * HEAD_W, (h + 1) * HEAD_W)
            ov = o_ref[:, cs]
            r = lax.rsqrt(jnp.mean(ov * ov, axis=-1, keepdims=True) + RMS_EPS)
            y_ref[:, cs] = (ov * r * gv * _silu(z_ref[:, cs])).astype(BF16)
        if has_other:
            y_ref[:, w:] = rest[0][...].astype(BF16)

    row = lambda width, blk: pl.BlockSpec((tm, width), lambda i: (i, blk))
    return pl.pallas_call(
        body, name=name, grid=(lp // tm,),
        in_specs=[row(w, 0), row(w, z_blk0 * HEAD_W // w), pl.BlockSpec((1, HEAD_W), lambda i: (0, 0))]
        + ([row(D_MODEL - w, 0)] if has_other else []),
        out_specs=row(D_MODEL, 0), out_shape=jax.ShapeDtypeStruct((lp, D_MODEL), BF16),
        compiler_params=_cp(("parallel",)),
    )(o, zsrc, g.reshape(1, HEAD_W), *([other] if has_other else []))


def _gate_bwd(o, zsrc, z_blk0, g, dy, *, heads, name):
    lp = o.shape[0]
    tm = _row_tile(lp, 512)

    w = heads * HEAD_W
    assert (z_blk0 * HEAD_W) % w == 0

    def body(o_ref, z_ref, g_ref, dy_ref, do_ref, dz_ref, dg_ref):
        @pl.when(pl.program_id(0) == 0)
        def _():
            dg_ref[...] = jnp.zeros_like(dg_ref)

        gv = g_ref[...]
        dg = jnp.zeros((1, HEAD_W), F32)
        for h in range(heads):
            cs = slice(h * HEAD_W, (h + 1) * HEAD_W)
            ov, zv, dyv = o_ref[:, cs], z_ref[:, cs], dy_ref[:, cs]
            r = lax.rsqrt(jnp.mean(ov * ov, axis=-1, keepdims=True) + RMS_EPS)
            nrm = ov * r
            s = _silu(zv)
            dn = dyv * gv * s
            do_ref[:, cs] = r * (dn - nrm * jnp.mean(dn * nrm, axis=-1, keepdims=True))
            dz_ref[:, cs] = dyv * nrm * gv * _silu_grad(zv)
            dg = dg + jnp.sum(dyv * nrm * s, axis=0, keepdims=True)
        dg_ref[...] += dg

    row = lambda blk: pl.BlockSpec((tm, w), lambda i: (i, blk))
    vec = pl.BlockSpec((1, HEAD_W), lambda i: (0, 0))
    return pl.pallas_call(
        body, name=name, grid=(lp // tm,),
        in_specs=[row(0), row(z_blk0 * HEAD_W // w), vec, row(0)], out_specs=[row(0), row(0), vec],
        out_shape=[jax.ShapeDtypeStruct((lp, w), F32), jax.ShapeDtypeStruct((lp, w), F32),
                   jax.ShapeDtypeStruct((1, HEAD_W), F32)],
        compiler_params=_cp(("arbitrary",)),
    )(o, zsrc, g.reshape(1, HEAD_W), dy)


def _conv_taps(x, w):
    acc = w[CONV_K - 1:CONV_K, :] * x
    for k in range(CONV_K - 1):
        acc = acc + w[k:k + 1, :] * pltpu.roll(x, CONV_K - 1 - k, 0)
    return acc


def _gdn_pre_fwd(p0, conv_w, pad, *, name):
    lp = p0.shape[0]
    nq = GDN_HEADS
    qscale = HEAD_W ** -0.5

    def body(x_ref, w_ref, y_ref):
        j = pl.program_id(0)
        c = _conv_taps(x_ref[...], w_ref[...])
        s = _silu(c)
        r = lax.rsqrt(jnp.sum(s * s, axis=-1, keepdims=True) + L2_EPS)
        mult = jnp.where(j < nq, r * qscale, jnp.where(j < 2 * nq, r, 1.0))
        rows = lax.broadcasted_iota(jnp.int32, (lp, 1), 0)
        y_ref[...] = jnp.where(rows >= pad, s * mult, 0.0)

    return pl.pallas_call(
        body, name=name, grid=(3 * nq,),
        in_specs=[pl.BlockSpec((lp, HEAD_W), lambda j: (0, j)), pl.BlockSpec((CONV_K, HEAD_W), lambda j: (0, j))],
        out_specs=pl.BlockSpec((lp, HEAD_W), lambda j: (0, j)),
        out_shape=jax.ShapeDtypeStruct((lp, 3 * nq * HEAD_W), F32), compiler_params=_cp(("parallel",)),
    )(p0, conv_w)


def _gdn_pre_bwd(p0, conv_w, dqkv, pad, *, name):
    lp = p0.shape[0]
    nq = GDN_HEADS
    qscale = HEAD_W ** -0.5

    def body(x_ref, w_ref, dy_ref, dx_ref, dw_ref):
        j = pl.program_id(0)
        x, w = x_ref[...], w_ref[...]
        c = _conv_taps(x, w)
        s = _silu(c)
        r = lax.rsqrt(jnp.sum(s * s, axis=-1, keepdims=True) + L2_EPS)
        rows = lax.broadcasted_iota(jnp.int32, (lp, 1), 0)
        dy = jnp.where(rows >= pad, dy_ref[...], 0.0)
        nrm = s * r
        dn = dy * jnp.where(j < nq, qscale, 1.0)
        ds_norm = r * (dn - nrm * jnp.sum(nrm * dn, axis=-1, keepdims=True))
        ds = jnp.where(j < 2 * nq, ds_norm, dy)
        dc = ds * _silu_grad(c)
        dx = w[CONV_K - 1:CONV_K, :] * dc
        dws = [None] * CONV_K
        dws[CONV_K - 1] = jnp.sum(dc * x, axis=0, keepdims=True)
        for k in range(CONV_K - 1):
            sh = CONV_K - 1 - k
            dx = dx + w[k:k + 1, :] * pltpu.roll(dc, lp - sh, 0)
            dws[k] = jnp.sum(dc * pltpu.roll(x, sh, 0), axis=0, keepdims=True)
        dx_ref[...] = dx
        dw_ref[...] = jnp.concatenate(dws, axis=0)

    blk = pl.BlockSpec((lp, HEAD_W), lambda j: (0, j))
    wblk = pl.BlockSpec((CONV_K, HEAD_W), lambda j: (0, j))
    return pl.pallas_call(
        body, name=name, grid=(3 * nq,), in_specs=[blk, wblk, blk], out_specs=[blk, wblk],
        out_shape=[jax.ShapeDtypeStruct((lp, 3 * nq * HEAD_W), F32),
                   jax.ShapeDtypeStruct((CONV_K, 3 * nq * HEAD_W), F32)],
        compiler_params=_cp(("parallel",)),
    )(p0, conv_w, dqkv)


def _tri(c, strict):
    r = lax.broadcasted_iota(jnp.int32, (c, c), 0)
    q = lax.broadcasted_iota(jnp.int32, (c, c), 1)
    return (q < r) if strict else (q <= r)


@jax.custom_vjp
def _inv_unit_lower(m):
    c = m.shape[0]
    eye = (lax.broadcasted_iota(jnp.int32, (c, c), 0) == lax.broadcasted_iota(jnp.int32, (c, c), 1)).astype(F32)
    x = eye - m
    p = m
    n = 2
    while n < CHUNK:
        p = _bdot(p, p, NN)
        x = x + _bdot(x, p, NN)
        n *= 2
    return x


def _inv_fwd(m):
    t = _inv_unit_lower(m)
    return t, t


def _inv_bwd(t, g):
    return (-_bdot(_bdot(t, g, TN), t, NT),)


_inv_unit_lower.defvjp(_inv_fwd, _inv_bwd)


def _heads_to_rows(x, nh):
    return jnp.concatenate([x[:, h * HEAD_W:(h + 1) * HEAD_W] for h in range(nh)], axis=0)


def _rows_to_heads(x, nh):
    c = x.shape[0] // nh
    return jnp.concatenate([x[h * c:(h + 1) * c] for h in range(nh)], axis=1)


def _gdn_chunk(q, k, v, ba, alog, dtb, states, valid):
    nh = GDN_HEADS
    c = q.shape[0]
    r = nh * c
    lane = lax.broadcasted_iota(jnp.int32, (1, HEAD_W), 1)
    pick = lambda x, l: jnp.sum(jnp.where(lane == l, x, 0.0), axis=-1, keepdims=True)
    beta = jnp.concatenate([jnp.where(valid, _sigmoid(pick(ba, h)), 0.0) for h in range(nh)], axis=0)
    g = jnp.concatenate(
        [jnp.where(valid, -jnp.exp(pick(alog, h)) * _softplus(pick(ba, nh + h) + pick(dtb, h)), 0.0) for h in range(nh)],
        axis=0)
    qs, ks, vs = _heads_to_rows(q, nh), _heads_to_rows(k, nh), _heads_to_rows(v, nh)
    rr = lax.broadcasted_iota(jnp.int32, (r, r), 0)
    cc = lax.broadcasted_iota(jnp.int32, (r, r), 1)
    same = (rr // c) == (cc // c)
    causal, strict = same & (cc <= rr), same & (cc < rr)
    lower = jnp.where(causal, 1.0, 0.0).astype(BF16)
    upper = jnp.where(same & (cc >= rr), 1.0, 0.0).astype(BF16)
    gcb = _mask_mm(lower, upper, g * jnp.ones((1, HEAD_W), F32))
    gc_col = jnp.concatenate([gcb] * (r // HEAD_W), axis=1)
    decay = jnp.where(causal, jnp.exp(jnp.minimum(gc_col - gc_col.T, 0.0)), 0.0)
    egc = jnp.exp(gcb)
    kb = ks * beta
    m = jnp.where(strict, _bdot(kb, ks, NT) * decay, 0.0)
    t = _inv_unit_lower(m)
    u = _bdot(t, vs * beta, NN)
    w = _bdot(t, kb * egc, NN)
    a = _bdot(qs, ks, NT) * decay
    rows = lambda x, h: x[h * c:(h + 1) * c]
    qe = qs * egc
    v_new = u - jnp.concatenate([_bdot(rows(w, h), states[h], NN) for h in range(nh)], axis=0)
    o = jnp.concatenate([_bdot(rows(qe, h), states[h], NN) for h in range(nh)], axis=0) + _bdot(a, v_new, NN)
    new_states = []
    for h in range(nh):
        gl = gcb[(h + 1) * c - 1:(h + 1) * c, :]
        k_dec = rows(ks, h) * jnp.exp(gl - rows(gcb, h))
        new_states.append(states[h] * jnp.exp(gl) + _bdot(k_dec, rows(v_new, h), TN))
    return _rows_to_heads(o, nh), new_states


def _gdn_fwd(qkv, p0, alog_v, dtb_v, pad, *, name):
    lp = qkv.shape[0]
    n = lp // CHUNK
    nh = GDN_HEADS

    def body(q_ref, k_ref, v_ref, ba_ref, al_ref, dt_ref, o_ref, st_ref, s_ref):
        i = pl.program_id(0)

        @pl.when(i == 0)
        def _():
            s_ref[...] = jnp.zeros_like(s_ref)

        valid = (i * CHUNK + lax.broadcasted_iota(jnp.int32, (CHUNK, 1), 0)) >= pad
        s = s_ref[...]
        o, s2 = _gdn_chunk(q_ref[...], k_ref[...], v_ref[...], ba_ref[...], al_ref[...], dt_ref[...],
                           [s[h] for h in range(nh)], valid)
        st_ref[...] = s
        o_ref[...] = o
        for h in range(nh):
            s_ref[h] = s2[h]

    w = nh * HEAD_W
    vec = pl.BlockSpec((1, HEAD_W), lambda i: (0, 0))
    return pl.pallas_call(
        body, name=name, grid=(n,),
        in_specs=[pl.BlockSpec((CHUNK, w), lambda i: (i, 0)), pl.BlockSpec((CHUNK, w), lambda i: (i, 1)),
                  pl.BlockSpec((CHUNK, w), lambda i: (i, 2)), pl.BlockSpec((CHUNK, HEAD_W), lambda i: (i, AB_BA // HEAD_W)),
                  vec, vec],
        out_specs=[pl.BlockSpec((CHUNK, w), lambda i: (i, 0)),
                   pl.BlockSpec((None, nh, HEAD_W, HEAD_W), lambda i: (i, 0, 0, 0))],
        out_shape=[jax.ShapeDtypeStruct((lp, w), F32), jax.ShapeDtypeStruct((n, nh, HEAD_W, HEAD_W), F32)],
        scratch_shapes=[pltpu.VMEM((nh, HEAD_W, HEAD_W), F32)],
        compiler_params=_cp(("arbitrary",)),
    )(qkv, qkv, qkv, p0, alog_v, dtb_v)


def _gdn_bwd(qkv, p0, alog_v, dtb_v, states, do, pad, *, name):
    lp = qkv.shape[0]
    n = lp // CHUNK
    nh = GDN_HEADS

    def body(q_ref, k_ref, v_ref, ba_ref, al_ref, dt_ref, st_ref, do_ref,
             dq_ref, dk_ref, dv_ref, dba_ref, dal_ref, ddt_ref, ds_ref):
        step = pl.program_id(0)
        i = n - 1 - step

        @pl.when(step == 0)
        def _():
            ds_ref[...] = jnp.zeros_like(ds_ref)
            dal_ref[...] = jnp.zeros_like(dal_ref)
            ddt_ref[...] = jnp.zeros_like(ddt_ref)

        valid = (i * CHUNK + lax.broadcasted_iota(jnp.int32, (CHUNK, 1), 0)) >= pad
        st, dst = st_ref[...], ds_ref[...]
        fn = functools.partial(_gdn_chunk, valid=valid)
        _, vjp = jax.vjp(fn, q_ref[...], k_ref[...], v_ref[...], ba_ref[...], al_ref[...], dt_ref[...],
                         [st[h] for h in range(nh)])
        dq, dk, dv, dba, dal, ddt, ds = vjp((do_ref[...], [dst[h] for h in range(nh)]))
        dq_ref[...] = dq
        dk_ref[...] = dk
        dv_ref[...] = dv
        dba_ref[...] = dba
        dal_ref[...] += dal
        ddt_ref[...] += ddt
        for h in range(nh):
            ds_ref[h] = ds[h]

    w = nh * HEAD_W
    rev = lambda c: (lambda s: (n - 1 - s, c))
    vec = pl.BlockSpec((1, HEAD_W), lambda s: (0, 0))
    dq, dk, dv, dba, dal, ddt = pl.pallas_call(
        body, name=name, grid=(n,),
        in_specs=[pl.BlockSpec((CHUNK, w), rev(0)), pl.BlockSpec((CHUNK, w), rev(1)), pl.BlockSpec((CHUNK, w), rev(2)),
                  pl.BlockSpec((CHUNK, HEAD_W), rev(AB_BA // HEAD_W)), vec, vec,
                  pl.BlockSpec((None, nh, HEAD_W, HEAD_W), lambda s: (n - 1 - s, 0, 0, 0)),
                  pl.BlockSpec((CHUNK, w), rev(0))],
        out_specs=[pl.BlockSpec((CHUNK, w), rev(0)), pl.BlockSpec((CHUNK, w), rev(0)), pl.BlockSpec((CHUNK, w), rev(0)),
                   pl.BlockSpec((CHUNK, HEAD_W), rev(0)), vec, vec],
        out_shape=[jax.ShapeDtypeStruct((lp, w), F32)] * 3 + [jax.ShapeDtypeStruct((lp, HEAD_W), F32)]
        + [jax.ShapeDtypeStruct((1, HEAD_W), F32)] * 2,
        scratch_shapes=[pltpu.VMEM((nh, HEAD_W, HEAD_W), F32)],
        compiler_params=_cp(("arbitrary",)),
    )(qkv, qkv, qkv, p0, alog_v, dtb_v, states, do)
    return dq, dk, dv, dba, dal, ddt


HG_LEVELS = (32, 16, 8, 4, 2, 1)
HG_GROUP = 4


def _hg_masks():
    import numpy as np
    c = CHUNK
    t = np.arange(c)[:, None]
    j = np.arange(c)[None, :]
    sums = (j <= t).astype(np.float32)
    pairs = [j == t]
    for m in HG_LEVELS:
        p = (t // (2 * m)) * (2 * m)
        r = p + m
        pairs.append((t >= r) & (j < r) & (j >= p))
    pairs = np.concatenate([np.kron(np.eye(HG_GROUP), p) for p in pairs], axis=0).astype(np.float32)
    return jnp.asarray(sums, BF16), jnp.asarray(sums.T, BF16), jnp.asarray(pairs, F32)


def _hg_level_row(b, m):
    c, w = b.shape
    if m >= 8:
        return jnp.concatenate([jnp.broadcast_to(b[p + m:p + m + 1], (2 * m, w)) for p in range(0, c, 2 * m)], axis=0)
    tiles = b.reshape(c // 8, 8, w)
    sub = lax.broadcasted_iota(jnp.int32, (1, 8, 1), 1)
    out = None
    for r0 in range(m, 8, 2 * m):
        cand = jnp.broadcast_to(tiles[:, r0:r0 + 1, :], tiles.shape)
        out = cand if out is None else jnp.where(sub >= r0 - m, cand, out)
    return out.reshape(c, w)


def _split3(x):
    hi = x.astype(BF16)
    r1 = x - hi.astype(F32)
    mid = r1.astype(BF16)
    return hi, mid, (r1 - mid.astype(F32)).astype(BF16)


def _mask_mm_raw(m, x):
    return sum(_dot(m, part, NN) for part in _split3(x))


@jax.custom_vjp
def _mask_mm(m, mt, x):
    return _mask_mm_raw(m, x)


def _mask_mm_fwd(m, mt, x):
    return _mask_mm_raw(m, x), (m, mt)


def _mask_mm_bwd(res, g):
    m, mt = res
    return jnp.zeros_like(m), jnp.zeros_like(mt), _mask_mm_raw(mt, g)


_mask_mm.defvjp(_mask_mm_fwd, _mask_mm_bwd)


def _hg_chunk(qr, fr, ir, lb, states, valid, sums, sums_t, pairs):
    nh = HG_GROUP
    c = qr.shape[0]
    r = nh * c
    fg = lb + (1.0 - lb) * _sigmoid(fr)
    logf = jnp.where(valid, jnp.log(fg), 0.0)
    k = jnp.where(valid, 1.0 - fg, 0.0)
    qs = jnp.where(valid, _silu(qr), 0.0)
    v = jnp.where(valid, ir, 0.0)
    b = _mask_mm(sums, sums_t, logf)
    mask = lambda n: pairs[n * r:(n + 1) * r]
    stack = lambda x: _heads_to_rows(x, nh)
    a = mask(0) * _bdot(stack(qs), stack(k), NT)
    for lvl, m in enumerate(HG_LEVELS):
        d = b - _hg_level_row(b, m)
        a = a + mask(1 + lvl) * _bdot(stack(qs * jnp.exp(jnp.minimum(d, 0.0))),
                                      stack(k * jnp.exp(jnp.minimum(-d, 0.0))), NT)
    av = _bdot(a, stack(v), NN)
    eb = jnp.exp(b)
    qe, kd = qs * eb, k * jnp.exp(b[c - 1:c] - b)
    outs, new_states = [], []
    for h in range(nh):
        cs = slice(h * HEAD_W, (h + 1) * HEAD_W)
        outs.append(_bdot(qe[:, cs], states[h], NT) + av[h * c:(h + 1) * c])
        new_states.append(states[h] * eb[c - 1:c, cs] + _bdot(v[:, cs], kd[:, cs], TN))
    return jnp.concatenate(outs, axis=1), new_states


def _hg_fwd(p1, lb, pad, *, name):
    lp = p1.shape[0]
    n = lp // CHUNK
    nh = HG_HEADS

    def body(q_ref, f_ref, i_ref, lb_ref, sums_ref, sums_t_ref, pairs_ref, o_ref, st_ref, s_ref):
        i = pl.program_id(1)

        @pl.when(i == 0)
        def _():
            s_ref[...] = jnp.zeros_like(s_ref)

        valid = (i * CHUNK + lax.broadcasted_iota(jnp.int32, (CHUNK, 1), 0)) >= pad
        s = s_ref[...]
        o, s2 = _hg_chunk(q_ref[...], f_ref[...], i_ref[...], lb_ref[...], [s[h] for h in range(grp)], valid,
                          sums_ref[...], sums_t_ref[...], pairs_ref[...])
        st_ref[...] = s
        o_ref[...] = o
        for h in range(grp):
            s_ref[h] = s2[h]

    masks = _hg_masks()
    grp, ngrp, gw = HG_GROUP, nh // HG_GROUP, HG_GROUP * HEAD_W
    blk = lambda off: pl.BlockSpec((CHUNK, gw), lambda h, i: (i, off + h))
    const = lambda a: pl.BlockSpec(a.shape, lambda h, i: (0, 0))
    return pl.pallas_call(
        body, name=name, grid=(ngrp, n),
        in_specs=[blk(0), blk(ngrp), blk(2 * ngrp), pl.BlockSpec((1, gw), lambda h, i: (0, h))]
        + [const(a) for a in masks],
        out_specs=[blk(0), pl.BlockSpec((grp, None, HEAD_W, HEAD_W), lambda h, i: (h, i, 0, 0))],
        out_shape=[jax.ShapeDtypeStruct((lp, nh * HEAD_W), F32), jax.ShapeDtypeStruct((nh, n, HEAD_W, HEAD_W), F32)],
        scratch_shapes=[pltpu.VMEM((grp, HEAD_W, HEAD_W), F32)],
        compiler_params=_cp(("parallel", "arbitrary")),
    )(p1, p1, p1, lb, *masks)


def _hg_bwd(p1, lb, states, do, pad, *, name):
    lp = p1.shape[0]
    n = lp // CHUNK
    nh = HG_HEADS

    def body(q_ref, f_ref, i_ref, lb_ref, st_ref, do_ref, sums_ref, sums_t_ref, pairs_ref,
             dq_ref, df_ref, di_ref, dlb_ref, ds_ref):
        step = pl.program_id(1)
        i = n - 1 - step

        @pl.when(step == 0)
        def _():
            ds_ref[...] = jnp.zeros_like(ds_ref)
            dlb_ref[...] = jnp.zeros_like(dlb_ref)

        valid = (i * CHUNK + lax.broadcasted_iota(jnp.int32, (CHUNK, 1), 0)) >= pad
        fn = functools.partial(_hg_chunk, valid=valid, sums=sums_ref[...], sums_t=sums_t_ref[...],
                               pairs=pairs_ref[...])
        st, dst = st_ref[...], ds_ref[...]
        _, vjp = jax.vjp(fn, q_ref[...], f_ref[...], i_ref[...], lb_ref[...], [st[h] for h in range(grp)])
        dq, df, di, dlb, ds = vjp((do_ref[...], [dst[h] for h in range(grp)]))
        dq_ref[...] = dq
        df_ref[...] = df
        di_ref[...] = di
        dlb_ref[...] += dlb
        for h in range(grp):
            ds_ref[h] = ds[h]

    masks = _hg_masks()
    grp, ngrp, gw = HG_GROUP, nh // HG_GROUP, HG_GROUP * HEAD_W
    blk = lambda off: pl.BlockSpec((CHUNK, gw), lambda h, s: (n - 1 - s, off + h))
    const = lambda a: pl.BlockSpec(a.shape, lambda h, s: (0, 0))
    w = nh * HEAD_W
    return pl.pallas_call(
        body, name=name, grid=(ngrp, n),
        in_specs=[blk(0), blk(ngrp), blk(2 * ngrp), pl.BlockSpec((1, gw), lambda h, s: (0, h)),
                  pl.BlockSpec((grp, None, HEAD_W, HEAD_W), lambda h, s: (h, n - 1 - s, 0, 0)), blk(0)]
        + [const(a) for a in masks],
        out_specs=[blk(0), blk(0), blk(0), pl.BlockSpec((1, gw), lambda h, s: (0, h))],
        out_shape=[jax.ShapeDtypeStruct((lp, w), F32)] * 3 + [jax.ShapeDtypeStruct((1, w), F32)],
        scratch_shapes=[pltpu.VMEM((grp, HEAD_W, HEAD_W), F32)],
        compiler_params=_cp(("parallel", "arbitrary")),
    )(p1, p1, p1, lb, states, do, *masks)


SB_GROUP = 4


def _sb_cat(kind, first_key=0):
    r = lax.broadcasted_iota(jnp.int32, (SB_BLOCK, 2 * SB_BLOCK), 0)
    c = lax.broadcasted_iota(jnp.int32, (SB_BLOCK, 2 * SB_BLOCK), 1)
    tri = {"after": c < r, "incl": r <= c, "before": r < c}[kind]
    m = ((c >= SB_BLOCK) | tri) & (r >= first_key)
    return jnp.where(m, 1.0, 0.0).astype(BF16)


def _sb_cumsum(x, cat):
    return _dot(x.astype(BF16), cat, NN)


def _sb_logsig(z):
    e = jnp.exp(-jnp.abs(z))
    lse = jnp.where(e < 1e-4, e, jnp.log(1.0 + e))
    lsz = jnp.minimum(z, 0.0) - lse
    return lsz, lsz - z, e


def _sb_stack(x, scale=None):
    lane = lax.broadcasted_iota(jnp.int32, (1, HEAD_W), 1)
    if scale is not None:
        x = x * scale
    return jnp.concatenate([jnp.where(lane < SB_DH, x, 0.0), jnp.where(lane >= SB_DH, x, 0.0)], axis=0).astype(BF16)


def _sb_unstack(x):
    lane = lax.broadcasted_iota(jnp.int32, (1, HEAD_W), 1)
    return jnp.where(lane < SB_DH, x[:SB_BLOCK], x[SB_BLOCK:])


def _sb_fwd(p0, pad, *, name, gather=None):
    lp = p0.shape[0]
    nb = lp // SB_BLOCK
    npair = SB_HEADS // 2
    blk0 = AB_SB // HEAD_W
    scale = SB_DH ** -0.5
    gw = SB_GROUP * SB_BLOCK
    assert pad < SB_BLOCK
    g_srcs, g_dtypes = gather if gather is not None else ([], [])
    ng_arr = len(g_srcs)

    def body(q_ref, k_ref, v_ref, *rest):
        g_ins, (o_ref, tot_ref) = rest[:ng_arr], rest[ng_arr:ng_arr + 2]
        g_outs, g_scratch = rest[ng_arr + 2:2 * ng_arr + 2], rest[2 * ng_arr + 2:]
        first_step = (pl.program_id(0) == 0) & (pl.program_id(1) == 0)
        last_pair = pl.program_id(0) == npair - 1
        if ng_arr:
            g_start, g_forward, g_finish = _gather_phases(g_ins, g_outs, g_scratch[:ng_arr], *g_scratch[ng_arr:],
                                                          g_dtypes)
            pl.when(first_step)(g_start)
            pl.when(last_pair & (pl.program_id(1) == 0))(g_forward)
        i = pl.program_id(1)
        qs = _sb_stack(q_ref[...], scale)
        qpos = i * SB_BLOCK + lax.broadcasted_iota(jnp.int32, (SB_BLOCK, 1), 0)
        qpos = jnp.concatenate([qpos, qpos], axis=0)
        cat = _sb_cat("after")
        cat0 = _sb_cat("after", pad)
        ng = i // SB_GROUP

        def group(off, first_cat, allowed, carry):
            acc, run = carry
            kg = k_ref[pl.ds(off, gw), :].astype(BF16)
            vg = v_ref[pl.ds(off, gw), :].astype(BF16)
            lsz, l1m, _ = _sb_logsig(_dot(qs, kg, NT))
            if allowed is not None:
                l1m = jnp.where(allowed, l1m, 0.0)
            args = [None] * SB_GROUP
            for g in reversed(range(SB_GROUP)):
                sl = slice(g * SB_BLOCK, (g + 1) * SB_BLOCK)
                al = _sb_cumsum(l1m[:, sl], first_cat if g == 0 else cat)
                args[g] = lsz[:, sl] + al[:, :SB_BLOCK] + run
                run = run + al[:, SB_BLOCK:]
            wgt = jnp.exp(jnp.concatenate(args, axis=1))
            if allowed is not None:
                wgt = jnp.where(allowed, wgt, 0.0)
            return acc + _dot(wgt.astype(BF16), vg, NN), run

        def below(t, carry):
            gi = ng - 1 - t
            return group(pl.multiple_of(gi * gw, gw), jnp.where(gi == 0, cat0, cat), None, carry)

        top = ng * gw
        off = pl.multiple_of(jnp.minimum(top, lp - gw), SB_BLOCK)
        kpos = off + lax.broadcasted_iota(jnp.int32, (1, gw), 1)
        allowed = (kpos < qpos) & (kpos >= pad) & (kpos >= top)
        zero = (jnp.zeros((2 * SB_BLOCK, HEAD_W), F32), jnp.zeros((2 * SB_BLOCK, HEAD_W), F32))
        carry = group(off, cat, allowed, zero)
        acc, run = lax.fori_loop(0, ng, below, carry)
        o_ref[...] = _sb_unstack(acc)
        tot_ref[...] = _sb_unstack(run)
        if ng_arr:
            pl.when(last_pair & (pl.program_id(1) == nb - 1))(g_finish)

    full = lambda c0: pl.BlockSpec((lp, HEAD_W), lambda p, i: (0, c0 + p))
    out = pl.BlockSpec((SB_BLOCK, HEAD_W), lambda p, i: (i, p))
    return pl.pallas_call(
        body, name=name, grid=(npair, nb),
        in_specs=[pl.BlockSpec((SB_BLOCK, HEAD_W), lambda p, i: (i, blk0 + p)), full(blk0 + npair), full(blk0 + 2 * npair)]
        + [pl.BlockSpec(memory_space=pltpu.VMEM)] * ng_arr,
        out_specs=[out, out] + [_ANY] * ng_arr,
        out_shape=[jax.ShapeDtypeStruct((lp, npair * HEAD_W), F32)] * 2 + _gather_out_shapes(g_srcs, g_dtypes),
        scratch_shapes=_gather_scratch(g_srcs, g_dtypes) if ng_arr else [],
        compiler_params=_cp(("arbitrary", "arbitrary"), has_side_effects=bool(ng_arr)),
    )(p0, p0, p0, *g_srcs)


def _sb_bwd(p0, tot, dsrc, d_blk0, pad, *, name, scatter=()):
    lp = p0.shape[0]
    nb = lp // SB_BLOCK
    npair = SB_HEADS // 2
    blk0 = AB_SB // HEAD_W
    scale = SB_DH ** -0.5
    gw = SB_GROUP * SB_BLOCK
    assert pad < SB_BLOCK
    ns = len(scatter)

    def body(q_ref, k_ref, v_ref, tot_ref, do_ref, *rest):
        s_ins, (dq_ref, dkt_ref, dvt_ref) = rest[:ns], rest[ns:ns + 3]
        s_outs, s_sems = rest[ns + 3:2 * ns + 3], rest[2 * ns + 3:]
        if ns:
            s_start, s_finish = _scatter_phases(s_ins, s_outs, *s_sems)
            pl.when((pl.program_id(0) == 0) & (pl.program_id(1) == 0))(s_start)
        i = pl.program_id(1)

        @pl.when(i == 0)
        def _():
            dkt_ref[...] = jnp.zeros_like(dkt_ref)
            dvt_ref[...] = jnp.zeros_like(dvt_ref)

        qs = _sb_stack(q_ref[...], scale)
        dos = _sb_stack(do_ref[...])
        qst, dost = qs.T, dos.T
        totv = tot_ref[...]
        ones = jnp.ones((1, HEAD_W), F32)
        tots = jnp.concatenate([totv[:, 0:1] * ones, totv[:, SB_DH:SB_DH + 1] * ones], axis=0)
        qpos = i * SB_BLOCK + lax.broadcasted_iota(jnp.int32, (SB_BLOCK, 1), 0)
        qpos = jnp.concatenate([qpos, qpos], axis=0)
        incl, incl0 = _sb_cat("incl"), _sb_cat("incl", pad)
        before = _sb_cat("before")
        ng = i // SB_GROUP

        def dscore(z, e, ev, dl1m):
            r = 1.0 / (1.0 + e)
            sg = jnp.where(z >= 0, r, e * r)
            return ev * (1.0 - sg) - dl1m * sg

        def group(off, first_incl, allowed, carry):
            dq, prun, erun = carry
            kg = k_ref[pl.ds(off, gw), :].astype(BF16)
            vg = v_ref[pl.ds(off, gw), :].astype(BF16)
            z = _dot(qs, kg, NT)
            lsz, l1m, e = _sb_logsig(z)
            if allowed is not None:
                l1m = jnp.where(allowed, l1m, 0.0)
            dwgt = _dot(dos, vg, NT)
            dzs = [None] * SB_GROUP
            wgts = [None] * SB_GROUP
            for g in range(SB_GROUP):
                sl = slice(g * SB_BLOCK, (g + 1) * SB_BLOCK)
                al = _sb_cumsum(l1m[:, sl], first_incl if g == 0 else incl)
                wgt = jnp.exp(lsz[:, sl] + (tots - prun - al[:, :SB_BLOCK]))
                if allowed is not None:
                    wgt = jnp.where(allowed[:, sl], wgt, 0.0)
                prun = prun + al[:, SB_BLOCK:]
                ev = wgt * dwgt[:, sl]
                el = _sb_cumsum(ev, before)
                dzs[g] = dscore(z[:, sl], e[:, sl], ev, erun + el[:, :SB_BLOCK])
                erun = erun + el[:, SB_BLOCK:]
                wgts[g] = wgt
            dz = jnp.concatenate(dzs, axis=1)
            if allowed is not None:
                dz = jnp.where(allowed, dz, 0.0)
            dz = dz.astype(BF16)
            wg = jnp.concatenate(wgts, axis=1).astype(BF16)
            dkt_ref[:, pl.ds(off, gw)] += _dot(qst, dz, NN)
            dvt_ref[:, pl.ds(off, gw)] += _dot(dost, wg, NN)
            return dq + _dot(dz, kg, NN), prun, erun

        def below(gi, carry):
            return group(pl.multiple_of(gi * gw, gw), jnp.where(gi == 0, incl0, incl), None, carry)

        zero = tuple(jnp.zeros((2 * SB_BLOCK, HEAD_W), F32) for _ in range(3))
        carry = lax.fori_loop(0, ng, below, zero)
        top = ng * gw
        off = pl.multiple_of(jnp.minimum(top, lp - gw), SB_BLOCK)
        kpos = off + lax.broadcasted_iota(jnp.int32, (1, gw), 1)
        allowed = (kpos < qpos) & (kpos >= pad) & (kpos >= top)
        dq, _, _ = group(off, incl, allowed, carry)
        dq_ref[...] = _sb_unstack(dq) * scale
        if ns:
            pl.when((pl.program_id(0) == npair - 1) & (pl.program_id(1) == nb - 1))(s_finish)

    full = lambda c0: pl.BlockSpec((lp, HEAD_W), lambda p, i: (0, c0 + p))
    qb = lambda c0: pl.BlockSpec((SB_BLOCK, HEAD_W), lambda p, i: (i, c0 + p))
    tr = pl.BlockSpec((HEAD_W, lp), lambda p, i: (p, 0))
    return pl.pallas_call(
        body, name=name, grid=(npair, nb),
        in_specs=[qb(blk0), full(blk0 + npair), full(blk0 + 2 * npair), qb(0), qb(d_blk0)] + [_ANY] * ns,
        out_specs=[qb(0), tr, tr] + [_ANY] * ns,
        out_shape=[jax.ShapeDtypeStruct((lp, npair * HEAD_W), F32)]
        + [jax.ShapeDtypeStruct((npair * HEAD_W, lp), F32)] * 2
        + [jax.ShapeDtypeStruct(s.shape, s.dtype) for s in scatter],
        scratch_shapes=_scatter_scratch(ns) if ns else [],
        compiler_params=_cp(("arbitrary", "arbitrary"), has_side_effects=bool(ns)),
    )(p0, p0, p0, tot, dsrc, *scatter)


def _local_step(h0, target, pad, wts, hooks=None):
    lp = h0.shape[0]
    tm = _row_tile(lp, 1056)
    tkl = tm
    tml = _row_tile(lp, 528)
    d = D_MODEL
    mm = _mm
    mmw = functools.partial(_mm, out_dtype=BF16)
    g = {}

    h0_b = h0.astype(BF16)
    p0 = mm(h0_b, wts["w_ab"], "NN", tm=tm, tn=768, tk=d, name="l0_in_proj")
    ob, sb_tot, *gathered = _sb_fwd(p0, pad, name="sb_fwd", gather=hooks["late_gather"] if hooks else None)
    if hooks:
        wts = {**wts, **hooks["later_weights"](gathered)}
    qkv = _gdn_pre_fwd(p0, wts["conv_w"], pad, name="gdn_pre_fwd")
    oa_raw, gdn_states = _gdn_fwd(qkv, p0, wts["alog_v"], wts["dtb_v"], pad, name="gdn_fwd")
    oab = _gate_fwd(oa_raw, p0, AB_Z // HEAD_W, wts["ab_gn"], ob, heads=GDN_HEADS, name="gdn_gate_fwd")
    ln = lambda kind, layer: (wts[f"ln_{kind}_g"][layer], wts[f"ln_{kind}_b"][layer])
    pre_mix0, h0a, h0a_b = mm(oab, wts["w_out0"], "NN", tm=tml, tn=d, tk=d, epi="ln", c=h0, scale=DN_ALPHA,
                              ln=ln("mix", 0), name="l0_out_proj")
    u0 = mm(h0a_b, wts["w1"][0], "NN", tm=tm, tn=512, tk=d, b_dev=True, name="mlp0_up")
    pre_ffn0, h0b, h0b_b = mm(u0, wts["w2"][0], "NN", tm=tml, tn=d, tk=d, a_fn="relu2", epi="ln", c=h0a,
                              scale=DN_ALPHA, ln=ln("ffn", 0), name="mlp0_down")
    p1 = mm(h0b_b, wts["w_c"], "NN", tm=tm, tn=512, tk=d, b_dev=True, name="l1_in_proj")
    oc_raw, hg_states = _hg_fwd(p1, wts["lb"], pad, name="hg_fwd")
    oc = _gate_fwd(oc_raw, p1, 3 * HG_HEADS, wts["c_gn"], oc_raw, heads=HG_HEADS, name="hg_gate_fwd")
    pre_mix1, h1a, h1a_b = mm(oc, wts["w_out1"], "NN", tm=tml, tn=d, tk=d, epi="ln", c=h0b, scale=DN_ALPHA,
                              ln=ln("mix", 1), name="l1_out_proj")
    u1 = mm(h1a_b, wts["w1"][1], "NN", tm=tm, tn=512, tk=d, b_dev=True, name="mlp1_up")
    pre_ffn1, h1b, _ = mm(u1, wts["w2"][1], "NN", tm=tml, tn=d, tk=d, a_fn="relu2", epi="ln", c=h1a, scale=DN_ALPHA,
                          ln=ln("ffn", 1), name="mlp1_down")
    dy, loss_vec = _loss_head(h1b, target, name="loss_head")

    def mlp_bwd(layer, h_in_b, u, dpre, dpre_b):
        du = mm(dpre_b, wts["w2"][layer], "NT", tm=tm, tn=1024, tk=d, epi="relu2grad", c=u, out_dtype=BF16,
                name=f"mlp{layer}_d_hidden")
        dw2 = mmw(u, dpre_b, "TN", tm=1024, tn=1024, tk=tkl, a_fn="relu2", name=f"mlp{layer}_dw2")
        dw1 = mmw(h_in_b, du, "TN", tm=1024, tn=512, tk=tkl, out_dev=True, name=f"mlp{layer}_dw1")
        dh = mm(du, wts["w1"][layer], "NT", tm=tm, tn=1024, tk=512, b_dev=True, epi="add", c=dpre, scale=DN_ALPHA,
                name=f"mlp{layer}_d_in")
        return dh, dw1, dw2

    ln_ffn_dg, ln_ffn_db, ln_mix_dg, ln_mix_db, dw1s, dw2s = ([None, None] for _ in range(6))
    dpre, dpre_b, ln_ffn_dg[1], ln_ffn_db[1] = _ln_bwd(pre_ffn1, wts["ln_ffn_g"][1], dy, name="ln_ffn1_bwd")
    dh1a, dw1s[1], dw2s[1] = mlp_bwd(1, h1a_b, u1, dpre, dpre_b)
    dpre, dpre_b, ln_mix_dg[1], ln_mix_db[1] = _ln_bwd(pre_mix1, wts["ln_mix_g"][1], dh1a, name="ln_mix1_bwd")
    g["c_w_out"] = mmw(oc, dpre_b, "TN", tm=1024, tn=1024, tk=tkl, name="l1_dw_out")
    doc = mm(dpre_b, wts["w_out1"], "NT", tm=tm, tn=1024, tk=d, name="l1_d_gate")
    doc_raw, dz1, g["c_gn"] = _gate_bwd(oc_raw, p1, 3 * HG_HEADS, wts["c_gn"], doc, heads=HG_HEADS, name="hg_gate_bwd")
    dq1, df1, di1, g["lb"] = _hg_bwd(p1, wts["lb"], hg_states, doc_raw, pad, name="hg_bwd")
    dp1 = jnp.concatenate([dq1, df1, di1, dz1], axis=1).astype(BF16)
    g["c_w_in"] = mmw(h0b_b, dp1, "TN", tm=1024, tn=512, tk=tkl, out_dev=True, name="l1_dw_in")
    dh0b = mm(dp1, wts["w_c"], "NT", tm=tm, tn=1024, tk=512, b_dev=True, epi="add", c=dpre, scale=DN_ALPHA,
              name="l1_d_in")
    dpre, dpre_b, ln_ffn_dg[0], ln_ffn_db[0] = _ln_bwd(pre_ffn0, wts["ln_ffn_g"][0], dh0b, name="ln_ffn0_bwd")
    dh0a, dw1s[0], dw2s[0] = mlp_bwd(0, h0a_b, u0, dpre, dpre_b)
    dpre, dpre_b, ln_mix_dg[0], ln_mix_db[0] = _ln_bwd(pre_mix0, wts["ln_mix_g"][0], dh0a, name="ln_mix0_bwd")
    g["ab_w_out"] = mmw(oab, dpre_b, "TN", tm=1024, tn=1024, tk=tkl, name="l0_dw_out")
    doab = mm(dpre_b, wts["w_out0"], "NT", tm=tm, tn=1024, tk=d, name="l0_d_gate")
    doa_raw, dz0, g["ab_gn"] = _gate_bwd(oa_raw, p0, AB_Z // HEAD_W, wts["ab_gn"], doab, heads=GDN_HEADS,
                                         name="gdn_gate_bwd")
    early = ()
    if hooks:
        rows = lambda a, n: a.reshape(N_DEV, n // N_DEV, d)
        early = [g["c_w_in"], rows(g["c_w_out"], d), dw1s[1], rows(dw2s[1], D_FF), dw1s[0], rows(dw2s[0], D_FF),
                 rows(g["ab_w_out"], d)]
    dqb, dkb_t, dvb_t, *g["early_parts"] = _sb_bwd(p0, sb_tot, doab, GDN_HEADS, pad, name="sb_bwd", scatter=early)
    dkb, dvb = dkb_t.T, dvb_t.T
    dqn, dkn, dvn, dba, g["alog_v"], g["dtb_v"] = _gdn_bwd(qkv, p0, wts["alog_v"], wts["dtb_v"], gdn_states, doa_raw,
                                                           pad, name="gdn_bwd")
    dconv_in, g["conv_w"] = _gdn_pre_bwd(p0, wts["conv_w"], jnp.concatenate([dqn, dkn, dvn], axis=1), pad,
                                         name="gdn_pre_bwd")
    dp0 = jnp.concatenate([dconv_in, dz0, dqb, dkb, dvb, dba, jnp.zeros((lp, AB_CAT - AB_BA - HEAD_W), F32)],
                          axis=1).astype(BF16)
    g["w_ab"] = mmw(h0_b, dp0, "TN", tm=1024, tn=768, tk=tkl, name="l0_dw_in")
    dh0 = mm(dp0, wts["w_ab"], "NT", tm=tm, tn=1024, tk=768, epi="add", c=dpre, scale=DN_ALPHA, name="l0_d_in")

    g["w1"], g["w2"] = dw1s, dw2s
    g["ln_mix_g"] = jnp.concatenate(ln_mix_dg, axis=0)
    g["ln_mix_b"] = jnp.concatenate(ln_mix_db, axis=0)
    g["ln_ffn_g"] = jnp.concatenate(ln_ffn_dg, axis=0)
    g["ln_ffn_b"] = jnp.concatenate(ln_ffn_db, axis=0)
    return loss_vec, dh0, g


N_CHIP = N_DEV // 2


def _place():
    x, y, c = lax.axis_index("x"), lax.axis_index("y"), lax.axis_index("c")
    return x, y, c, 2 * x + y


def _chip_dev(chip, core):
    return (chip // 2, chip % 2, core)


def _remote(src, dst, send_sem, recv_sem, dev):
    return pltpu.make_async_remote_copy(src_ref=src, dst_ref=dst, send_sem=send_sem, recv_sem=recv_sem,
                                        device_id=dev, device_id_type=pl.DeviceIdType.MESH)


_ANY = pl.BlockSpec(memory_space=pl.ANY)


def _gather(srcs, dtypes, *, name):
    n = len(srcs)

    def body(*refs):
        start, forward, finish = _gather_phases(refs[:n], refs[n:2 * n], refs[2 * n:3 * n], *refs[3 * n:], dtypes)
        start()
        forward()
        finish()

    return pl.pallas_call(
        body, name=name, in_specs=[pl.BlockSpec(memory_space=pltpu.VMEM)] * n, out_specs=[_ANY] * n,
        out_shape=_gather_out_shapes(srcs, dtypes), scratch_shapes=_gather_scratch(srcs, dtypes),
        compiler_params=_cp(has_side_effects=True),
    )(*srcs)


def _gather_out_shapes(srcs, dtypes):
    return [jax.ShapeDtypeStruct((N_DEV, *s.shape), dt) for s, dt in zip(srcs, dtypes)]


def _gather_scratch(srcs, dtypes):
    n = len(srcs)
    return [pltpu.VMEM(s.shape, dt) for s, dt in zip(srcs, dtypes)] + [
        pltpu.SemaphoreType.DMA((n, 2 * N_CHIP - 1)), pltpu.SemaphoreType.DMA((n, 2 * N_CHIP - 1)),
        pltpu.SemaphoreType.DMA((n,))]


def _gather_phases(ins, outs, stages, send_sems, recv_sems, local_sems, dtypes):
    n = len(ins)
    x, y, c, chip = _place()
    me = 2 * chip + c
    sibling = (x, y, 1 - c)

    def own(i):
        cps = [_remote(stages[i], outs[i].at[me], send_sems.at[i, 0], recv_sems.at[i, 0], sibling)]
        for j in range(1, N_CHIP):
            cps.append(_remote(stages[i], outs[i].at[me], send_sems.at[i, j], recv_sems.at[i, j],
                               _chip_dev(jnp.bitwise_xor(chip, j), c)))
        return cps

    def local(i):
        return pltpu.make_async_copy(stages[i], outs[i].at[me], local_sems.at[i])

    def passed_on(i, j):
        slot = outs[i].at[2 * jnp.bitwise_xor(chip, j) + c]
        return _remote(slot, slot, send_sems.at[i, N_CHIP - 1 + j], recv_sems.at[i, N_CHIP - 1 + j], sibling)

    def start():
        for i in range(n):
            stages[i][...] = ins[i][...].astype(dtypes[i])
            local(i).start()
            for cp in own(i):
                cp.start()

    def forward():
        for i in range(n):
            for j in range(1, N_CHIP):
                own(i)[j].wait_recv()
                passed_on(i, j).start()

    def finish():
        for i in range(n):
            own(i)[0].wait_recv()
            for j in range(1, N_CHIP):
                passed_on(i, j).wait_recv()
        for i in range(n):
            for cp in own(i):
                cp.wait_send()
            for j in range(1, N_CHIP):
                passed_on(i, j).wait_send()
            local(i).wait()

    return start, forward, finish


def _scatter(parts, *, name):
    n = len(parts)

    def body(*refs):
        start, finish = _scatter_phases(refs[:n], refs[n:2 * n], *refs[2 * n:])
        start()
        finish()

    return pl.pallas_call(
        body, name=name, in_specs=[_ANY] * n, out_specs=[_ANY] * n,
        out_shape=[jax.ShapeDtypeStruct(p.shape, p.dtype) for p in parts], scratch_shapes=_scatter_scratch(n),
        compiler_params=_cp(has_side_effects=True),
    )(*parts)


def _scatter_scratch(n):
    return [pltpu.SemaphoreType.DMA((n, N_DEV - 1)), pltpu.SemaphoreType.DMA((n, N_DEV - 1)),
            pltpu.SemaphoreType.DMA((n,))]


def _scatter_phases(ins, outs, send_sems, recv_sems, local_sems):
    n = len(ins)
    _, _, c, chip = _place()
    me = 2 * chip + c

    def copies():
        cps = []
        for i in range(n):
            cps.append(pltpu.make_async_copy(ins[i].at[me], outs[i].at[me], local_sems.at[i]))
            for k in range(1, N_DEV):
                peer = jnp.bitwise_xor(me, k)
                cps.append(_remote(ins[i].at[peer], outs[i].at[me], send_sems.at[i, k - 1], recv_sems.at[i, k - 1],
                                   _chip_dev(peer // 2, peer % 2)))
        return cps

    def start():
        for cp in copies():
            cp.start()

    def finish():
        for cp in copies():
            cp.wait()

    return start, finish


def _adamw(w, parts, m, v, *, name):
    r, c = w.shape
    s = parts.shape[0]
    tm = _row_tile(r, 128) if r % 8 == 0 else r
    c1 = 1.0 - ADAM_B1 ** ADAM_STEP
    c2 = 1.0 - ADAM_B2 ** ADAM_STEP

    def body(w_ref, p_ref, m_ref, v_ref, g_ref, d_ref, m2_ref, v2_ref):
        g = p_ref[0].astype(F32)
        for j in range(1, s):
            g = g + p_ref[j].astype(F32)
        m2 = ADAM_B1 * m_ref[...] + (1.0 - ADAM_B1) * g
        v2 = ADAM_B2 * v_ref[...] + (1.0 - ADAM_B2) * jnp.square(g)
        g_ref[...] = g
        m2_ref[...] = m2
        v2_ref[...] = v2
        d_ref[...] = -ADAM_LR * ((m2 / c1) / (jnp.sqrt(v2 / c2) + ADAM_EPS) + ADAM_WD * w_ref[...])

    blk = pl.BlockSpec((tm, c), lambda i: (i, 0))
    return pl.pallas_call(
        body, name=name, grid=(r // tm,),
        in_specs=[blk, pl.BlockSpec((s, tm, c), lambda i: (0, i, 0)), blk, blk], out_specs=[blk] * 4,
        out_shape=[jax.ShapeDtypeStruct((r, c), F32)] * 4, compiler_params=_cp(("parallel",)),
    )(w, parts, m, v)


_WEIGHTS = ("meta_tokens", "ab_w_in", "ab_conv_w", "ab_a_log", "ab_dt_bias", "ab_gnorm_g", "ab_w_out", "c_w_in",
            "c_lb_raw", "c_gnorm_g", "c_w_out", "ln_mix_g", "ln_mix_b", "mlp_w1", "mlp_w2", "ln_ffn_g", "ln_ffn_b")
_PACK_ROWS = (("ln_mix_g", 0), ("ln_mix_b", 2), ("ln_ffn_g", 4), ("ln_ffn_b", 6), ("c_lb_raw", 8))
_PACK_MISC_ROW = 10
_PACK_MISC = (("ab_gnorm_g", 0, 128), ("c_gnorm_g", 128, 128), ("ab_a_log", 256, GDN_HEADS), ("ab_dt_bias", 260, GDN_HEADS))
_PACK_N = 16
_SMALL_META = 16
_SMALL_CONV = 32
_SMALL_N = 40


def _pack_replicated(p):
    rows = jnp.zeros((_PACK_N, D_MODEL), F32)
    for name, r0 in _PACK_ROWS:
        rows = rows.at[r0:r0 + 2].set(p[name])
    for name, c0, width in _PACK_MISC:
        rows = rows.at[_PACK_MISC_ROW, c0:c0 + width].set(p[name].reshape(width))
    return rows


def _unpack_replicated(rows, like):
    out = {}
    for name, r0 in _PACK_ROWS:
        out[name] = rows[r0:r0 + 2]
    for name, c0, width in _PACK_MISC:
        out[name] = rows[_PACK_MISC_ROW, c0:c0 + width].reshape(like[name].shape)
    return out


def _lower_bound(c_lb_raw):
    lb_all = jnp.cumsum(jax.nn.softmax(c_lb_raw.astype(F32), axis=0), axis=0)
    return (lb_all - lb_all[0:1])[1].reshape(1, -1)


def kernel(x, meta_tokens, ab_w_in, ab_conv_w, ab_a_log, ab_dt_bias, ab_gnorm_g, ab_w_out, c_w_in, c_lb_raw, c_gnorm_g, c_w_out, ln_mix_g, ln_mix_b, mlp_w1, mlp_w2, ln_ffn_g, ln_ffn_b, loss_target, m_meta_tokens, m_ab_w_in, m_ab_conv_w, m_ab_a_log, m_ab_dt_bias, m_ab_gnorm_g, m_ab_w_out, m_c_w_in, m_c_lb_raw, m_c_gnorm_g, m_c_w_out, m_ln_mix_g, m_ln_mix_b, m_mlp_w1, m_mlp_w2, m_ln_ffn_g, m_ln_ffn_b, v_meta_tokens, v_ab_w_in, v_ab_conv_w, v_ab_a_log, v_ab_dt_bias, v_ab_gnorm_g, v_ab_w_out, v_c_w_in, v_c_lb_raw, v_c_gnorm_g, v_c_w_out, v_ln_mix_g, v_ln_mix_b, v_mlp_w1, v_mlp_w2, v_ln_ffn_g, v_ln_ffn_b):
    w = dict(zip(_WEIGHTS, (meta_tokens, ab_w_in, ab_conv_w, ab_a_log, ab_dt_bias, ab_gnorm_g, ab_w_out, c_w_in, c_lb_raw,
                            c_gnorm_g, c_w_out, ln_mix_g, ln_mix_b, mlp_w1, mlp_w2, ln_ffn_g, ln_ffn_b)))
    mom = dict(zip(_WEIGHTS, (m_meta_tokens, m_ab_w_in, m_ab_conv_w, m_ab_a_log, m_ab_dt_bias, m_ab_gnorm_g, m_ab_w_out,
                              m_c_w_in, m_c_lb_raw, m_c_gnorm_g, m_c_w_out, m_ln_mix_g, m_ln_mix_b, m_mlp_w1, m_mlp_w2,
                              m_ln_ffn_g, m_ln_ffn_b)))
    var = dict(zip(_WEIGHTS, (v_meta_tokens, v_ab_w_in, v_ab_conv_w, v_ab_a_log, v_ab_dt_bias, v_ab_gnorm_g, v_ab_w_out,
                              v_c_w_in, v_c_lb_raw, v_c_gnorm_g, v_c_w_out, v_ln_mix_g, v_ln_mix_b, v_mlp_w1, v_mlp_w2,
                              v_ln_ffn_g, v_ln_ffn_b)))
    me = 4 * lax.axis_index("x") + 2 * lax.axis_index("y") + lax.axis_index("c")
    seq = x.shape[1]
    pad = (-(N_META + seq)) % SB_BLOCK
    lp = pad + N_META + seq
    meta_w = D_MODEL // N_DEV
    conv_w_all = 2 * GDN_HEADS * HEAD_W + GDN_HEADS * HEAD_W
    conv_w_mine = conv_w_all // N_DEV

    g_meta, g_conv, g_ab_in = _gather([w["meta_tokens"], w["ab_conv_w"][0], w["ab_w_in"][0]], [F32, F32, BF16],
                                      name="gather_weights_first")
    meta_full = g_meta.transpose(1, 0, 2).reshape(N_META, D_MODEL)
    conv_full = g_conv.transpose(1, 0, 2).reshape(CONV_K, conv_w_all)
    ab_full = g_ab_in.transpose(1, 0, 2).reshape(D_MODEL, AB_IN)
    ba0 = AB_Z + 512
    w_ab = jnp.concatenate([ab_full[:, :ba0], ab_full[:, ba0 + 2 * GDN_HEADS:], ab_full[:, ba0:ba0 + 2 * GDN_HEADS],
                            jnp.zeros((D_MODEL, AB_CAT - AB_IN), BF16)], axis=1)
    vec128 = lambda p: jnp.zeros((1, HEAD_W), F32).at[0, :GDN_HEADS].set(p.reshape(GDN_HEADS))
    wts = dict(
        w_ab=w_ab, conv_w=conv_full, alog_v=vec128(w["ab_a_log"]), dtb_v=vec128(w["ab_dt_bias"]),
        ab_gn=w["ab_gnorm_g"][0], lb=_lower_bound(w["c_lb_raw"]), c_gn=w["c_gnorm_g"][0],
        ln_mix_g=w["ln_mix_g"], ln_mix_b=w["ln_mix_b"], ln_ffn_g=w["ln_ffn_g"], ln_ffn_b=w["ln_ffn_b"])

    def later_weights(gathered):
        g_ab_out, g_c_in, g_c_out, g_w1, g_w2 = gathered
        return dict(w_out0=g_ab_out.reshape(D_MODEL, D_MODEL), w_c=g_c_in, w_out1=g_c_out.reshape(D_MODEL, D_MODEL),
                    w1=[g_w1[:, l] for l in range(DEPTH)], w2=[g_w2[:, l].reshape(D_FF, D_MODEL) for l in range(DEPTH)])

    hooks = dict(
        late_gather=([w["ab_w_out"][0], w["c_w_in"][0], w["c_w_out"][0], w["mlp_w1"], w["mlp_w2"]], [BF16] * 5),
        later_weights=later_weights)

    h0 = jnp.concatenate([jnp.zeros((pad, D_MODEL), F32), meta_full, x[0]], axis=0)
    loss_vec, dh0, g = _local_step(h0, loss_target[0], pad, wts, hooks)
    loss = lax.psum(jnp.sum(loss_vec), ("x", "y", "c"))
    grad_x = dh0[lp - seq:][None]

    _, lb_vjp = jax.vjp(_lower_bound, w["c_lb_raw"])
    rep_part = _pack_replicated(dict(
        ln_mix_g=g["ln_mix_g"], ln_mix_b=g["ln_mix_b"], ln_ffn_g=g["ln_ffn_g"], ln_ffn_b=g["ln_ffn_b"],
        c_lb_raw=lb_vjp(g["lb"])[0], ab_gnorm_g=g["ab_gn"], c_gnorm_g=g["c_gn"],
        ab_a_log=g["alog_v"][0, :GDN_HEADS], ab_dt_bias=g["dtb_v"][0, :GDN_HEADS]))
    small = jnp.concatenate([rep_part, dh0[pad:pad + N_META], g["conv_w"].reshape(-1, D_MODEL),
                             jnp.zeros((_SMALL_N - _SMALL_CONV - CONV_K * conv_w_all // D_MODEL, D_MODEL), F32)], axis=0)
    (small_all,) = _gather([small], [F32], name="gather_small_grads")
    rep_out = _adamw(_pack_replicated(w), small_all[:, :_PACK_N], _pack_replicated(mom), _pack_replicated(var),
                     name="adamw_replicated")
    meta_parts = lax.dynamic_slice_in_dim(small_all[:, _SMALL_META:_SMALL_META + N_META], me * meta_w, meta_w, axis=2)
    meta_out = _adamw(w["meta_tokens"], meta_parts, mom["meta_tokens"], var["meta_tokens"], name="adamw_meta")
    conv_parts = small_all[:, _SMALL_CONV:_SMALL_CONV + CONV_K * conv_w_all // D_MODEL].reshape(N_DEV, CONV_K, conv_w_all)
    conv_parts = lax.dynamic_slice_in_dim(conv_parts, me * conv_w_mine, conv_w_mine, axis=2)
    conv_out = _adamw(w["ab_conv_w"][0], conv_parts, mom["ab_conv_w"][0], var["ab_conv_w"][0], name="adamw_conv")

    gab = g["w_ab"]
    gab = jnp.concatenate([gab[:, :ba0], gab[:, AB_BA:AB_BA + 2 * GDN_HEADS], gab[:, ba0:AB_BA]], axis=1)
    (ab_in_parts,) = _scatter([gab.reshape(D_MODEL, N_DEV, AB_IN // N_DEV).transpose(1, 0, 2)], name="scatter_last")
    early = g["early_parts"]
    big = [("ab_w_in", 0, ab_in_parts), ("ab_w_out", 0, early[6]), ("mlp_w1", 0, early[4]), ("mlp_w2", 0, early[5]),
           ("c_w_in", 0, early[0]), ("c_w_out", 0, early[1]), ("mlp_w1", 1, early[2]), ("mlp_w2", 1, early[3])]
    big_out = {}
    for name, l, p in big:
        res = _adamw(w[name][l], p, mom[name][l], var[name][l], name=f"adamw_{name}{l}")
        big_out.setdefault(name, []).append(res)

    rep = [_unpack_replicated(r, w) for r in rep_out]
    outs = {}
    for name in _WEIGHTS:
        if name == "meta_tokens":
            outs[name] = list(meta_out)
        elif name == "ab_conv_w":
            outs[name] = [o[None] for o in conv_out]
        elif name in big_out:
            res = big_out[name]
            outs[name] = [o[None] for o in res[0]] if len(res) == 1 else [jnp.stack(pair) for pair in zip(*res)]
        else:
            outs[name] = [r[name] for r in rep]
    flat = [loss, grad_x]
    for kind in range(4):
        flat += [outs[name][kind] for name in _WEIGHTS]
    return tuple(flat)
```

```python
import functools
import math

import jax
import jax.numpy as jnp
from jax import lax
from jax.experimental import pallas as pl
from jax.experimental.pallas import tpu as pltpu

F32 = jnp.float32
BF16 = jnp.bfloat16
HI = lax.Precision.HIGHEST

N_DEV = 8
D_MODEL = 1024
N_META = 16
D_FF = 4096
DEPTH = 2
GDN_HEADS = 4
SB_HEADS = 8
SB_DH = 64
HG_HEADS = 8
HEAD_W = 128
CHUNK = 64
SB_BLOCK = 128
CONV_K = 4
DN_ALPHA = float((2 * DEPTH) ** 0.25)
LN_EPS = 1e-5
RMS_EPS = 1e-6
L2_EPS = 1e-6
ADAM_LR, ADAM_B1, ADAM_B2, ADAM_EPS, ADAM_WD, ADAM_STEP = 0.001, 0.9, 0.999, 1e-08, 0.01, 10

AB_QKV = 0
AB_Z = 1536
AB_SB = 2048
AB_BA = 3584
AB_CAT = 3840
AB_IN = 3592

VMEM_LIMIT = 56 * 1024 * 1024


def _cp(sem=None, **kw):
    if sem is not None:
        kw["dimension_semantics"] = sem
    return pltpu.CompilerParams(vmem_limit_bytes=VMEM_LIMIT, **kw)


def _row_tile(n, want):
    best = 8
    for t in range(8, min(n, want) + 1, 8):
        if n % t == 0:
            best = t
    return best


@jax.custom_vjp
def _sigmoid(x):
    e = jnp.exp(-jnp.abs(x))
    r = 1.0 / (1.0 + e)
    return jnp.where(x >= 0, r, e * r)


def _sigmoid_fwd(x):
    s = _sigmoid(x)
    return s, s


def _sigmoid_bwd(s, g):
    return (g * s * (1.0 - s),)


_sigmoid.defvjp(_sigmoid_fwd, _sigmoid_bwd)


def _log1p_exp_neg_abs(x):
    e = jnp.exp(-jnp.abs(x))
    return jnp.where(e < 1e-4, e - 0.5 * e * e, jnp.log(1.0 + e))


@jax.custom_vjp
def _softplus(x):
    return jnp.maximum(x, 0.0) + _log1p_exp_neg_abs(x)


def _softplus_fwd(x):
    return _softplus(x), x


def _softplus_bwd(x, g):
    return (g * _sigmoid(x),)


_softplus.defvjp(_softplus_fwd, _softplus_bwd)


def _silu(x):
    return x * _sigmoid(x)


def _silu_grad(x):
    s = _sigmoid(x)
    return s * (1.0 + x * (1.0 - s))


def _dot(a, b, dims, precision=None):
    return lax.dot_general(a, b, (dims, ((), ())), precision=precision, preferred_element_type=F32)


NN = ((1,), (0,))
NT = ((1,), (1,))
TN = ((0,), (0,))


def _bdot(a, b, dims):
    return _dot(a.astype(BF16), b.astype(BF16), dims)


def _layer_norm(pre, g, beta):
    mu = jnp.mean(pre, axis=-1, keepdims=True)
    xc = pre - mu
    var = jnp.mean(xc * xc, axis=-1, keepdims=True)
    return xc * lax.rsqrt(var + LN_EPS) * g + beta


def _mm(a, b, mode, *, tm, tn, tk, name, a_fn=None, epi=None, c=None, scale=1.0, b_dev=False, out_dev=False,
        out_dtype=F32, ln=None, scatter=()):
    if mode == "NN":
        m, kk = a.shape
        n = b.shape[2] * N_DEV if b_dev else b.shape[1]
    elif mode == "NT":
        m, kk = a.shape
        n = b.shape[1] if b_dev else b.shape[0]
    else:
        kk, m = a.shape
        n = b.shape[1]
    assert m % tm == 0 and n % tn == 0 and kk % tk == 0, (name, m, n, kk, tm, tn, tk)
    nk = kk // tk
    dims = {"NN": NN, "NT": NT, "TN": TN}[mode]

    if mode == "TN":
        a_spec = pl.BlockSpec((tk, tm), lambda i, j, k: (k, i))
    else:
        a_spec = pl.BlockSpec((tm, tk), lambda i, j, k: (i, k))
    if mode == "NN":
        if b_dev:
            assert tn == b.shape[2]
            b_spec = pl.BlockSpec((None, tk, tn), lambda i, j, k: (j, k, 0))
        else:
            b_spec = pl.BlockSpec((tk, tn), lambda i, j, k: (k, j))
    elif mode == "NT":
        if b_dev:
            assert tk == b.shape[2]
            b_spec = pl.BlockSpec((None, tn, tk), lambda i, j, k: (k, j, 0))
        else:
            b_spec = pl.BlockSpec((tn, tk), lambda i, j, k: (j, k))
    else:
        b_spec = pl.BlockSpec((tk, tn), lambda i, j, k: (k, j))
    in_specs = [a_spec, b_spec]
    operands = [a, b]
    if epi is not None:
        in_specs.append(pl.BlockSpec((tm, tn), lambda i, j, k: (i, j)))
        operands.append(c)
    if epi == "ln":
        assert tn == n and not out_dev
        in_specs += [pl.BlockSpec((1, n), lambda i, j, k: (0, 0))] * 2
        operands += [ln[0].reshape(1, n), ln[1].reshape(1, n)]
    if out_dev:
        assert tn == n // N_DEV
        out_shape = jax.ShapeDtypeStruct((N_DEV, m, tn), out_dtype)
        out_spec = pl.BlockSpec((None, tm, tn), lambda i, j, k: (j, i, 0))
    else:
        out_shape = jax.ShapeDtypeStruct((m, n), out_dtype)
        out_spec = pl.BlockSpec((tm, tn), lambda i, j, k: (i, j))
    if epi == "ln":
        out_shape = [out_shape, out_shape, jax.ShapeDtypeStruct((m, n), BF16)]
        out_spec = [out_spec] * 3
    n_out = 3 if epi == "ln" else 1
    ns = len(scatter)
    if ns:
        in_specs += [_ANY] * ns
        operands += list(scatter)
        out_shape = (out_shape if n_out > 1 else [out_shape]) + [jax.ShapeDtypeStruct(s.shape, s.dtype) for s in scatter]
        out_spec = (out_spec if n_out > 1 else [out_spec]) + [_ANY] * ns
    n_in = len(operands)
    grid = (m // tm, n // tn, nk)

    def body(*refs):
        a_ref, b_ref = refs[0], refs[1]
        c_ref = refs[2] if epi is not None else None
        o_ref = refs[n_in]
        scratch0 = n_in + n_out + ns
        acc_ref = refs[scratch0] if nk > 1 else None
        if ns:
            s_start, s_finish = _scatter_phases(refs[n_in - ns:n_in], refs[n_in + n_out:scratch0],
                                                *refs[scratch0 + (1 if nk > 1 else 0):])
            at = lambda step: functools.reduce(lambda x, y: x & y, [pl.program_id(ax) == step[ax] for ax in range(3)])
            pl.when(at((0, 0, 0)))(s_start)
        av = a_ref[...]
        if a_fn == "relu2":
            av = jnp.square(jnp.maximum(av, 0.0))
        p = _dot(av.astype(BF16), b_ref[...].astype(BF16), dims)

        def finish(acc):
            if epi == "add":
                acc = acc + scale * c_ref[...]
            elif epi == "relu2grad":
                acc = acc * (2.0 * jnp.maximum(c_ref[...], 0.0))
            elif epi == "ln":
                acc = acc + scale * c_ref[...]
                y = _layer_norm(acc, refs[3][...], refs[4][...])
                refs[n_in + 1][...] = y
                refs[n_in + 2][...] = y.astype(BF16)
            o_ref[...] = acc.astype(out_dtype)

        if nk == 1:
            finish(p)
        else:
            k = pl.program_id(2)

            @pl.when(k == 0)
            def _():
                acc_ref[...] = p

            @pl.when(k > 0)
            def _():
                acc_ref[...] += p

            @pl.when(k == nk - 1)
            def _():
                finish(acc_ref[...])

        if ns:
            pl.when(at(tuple(g - 1 for g in grid)))(s_finish)

    res = pl.pallas_call(
        body, name=name, grid=grid, in_specs=in_specs, out_specs=out_spec, out_shape=out_shape,
        scratch_shapes=([pltpu.VMEM((tm, tn), F32)] if nk > 1 else []) + (_scatter_scratch(ns) if ns else []),
        compiler_params=_cp(("arbitrary",) * 3 if ns else ("parallel", "parallel", "arbitrary"),
                            has_side_effects=bool(ns)),
    )(*operands)
    return res


def _ln_bwd(pre, g, dy, *, name):
    lp, d = pre.shape
    tm = _row_tile(lp, 512)

    def body(pre_ref, g_ref, dy_ref, dpre_ref, dpreb_ref, dg_ref, db_ref):
        pre = pre_ref[...]
        mu = jnp.mean(pre, axis=-1, keepdims=True)
        xc = pre - mu
        var = jnp.mean(xc * xc, axis=-1, keepdims=True)
        rstd = lax.rsqrt(var + LN_EPS)
        xhat = xc * rstd
        dyv = dy_ref[...]
        dxh = dyv * g_ref[...]
        m1 = jnp.mean(dxh, axis=-1, keepdims=True)
        m2 = jnp.mean(dxh * xhat, axis=-1, keepdims=True)
        dpre = rstd * (dxh - m1 - xhat * m2)
        dpre_ref[...] = dpre
        dpreb_ref[...] = dpre.astype(BF16)

        @pl.when(pl.program_id(0) == 0)
        def _():
            dg_ref[...] = jnp.zeros_like(dg_ref)
            db_ref[...] = jnp.zeros_like(db_ref)

        dg_ref[...] += jnp.sum(dyv * xhat, axis=0, keepdims=True)
        db_ref[...] += jnp.sum(dyv, axis=0, keepdims=True)

    row = pl.BlockSpec((tm, d), lambda i: (i, 0))
    vec = pl.BlockSpec((1, d), lambda i: (0, 0))
    return pl.pallas_call(
        body, name=name, grid=(lp // tm,), in_specs=[row, vec, row], out_specs=[row, row, vec, vec],
        out_shape=[jax.ShapeDtypeStruct((lp, d), F32), jax.ShapeDtypeStruct((lp, d), BF16),
                   jax.ShapeDtypeStruct((1, d), F32), jax.ShapeDtypeStruct((1, d), F32)],
        compiler_params=_cp(("arbitrary",)),
    )(pre, g.reshape(1, d), dy)


def _loss_head(y, target, *, name):
    lp, d = y.shape
    seq = target.shape[0]
    tm = SB_BLOCK
    first = (lp - seq) // tm
    assert (lp - seq) % tm == 0 and seq % tm == 0

    def body(y_ref, t_ref, dy_ref, loss_ref):
        i = pl.program_id(0)
        live = i >= first
        diff = jnp.where(live, y_ref[...] - t_ref[...], 0.0)
        dy_ref[...] = diff * (1.0 / d)

        @pl.when(i == 0)
        def _():
            loss_ref[...] = jnp.zeros_like(loss_ref)

        loss_ref[...] += jnp.sum(diff * diff, axis=0, keepdims=True) * (0.5 / d)

    return pl.pallas_call(
        body, name=name, grid=(lp // tm,),
        in_specs=[pl.BlockSpec((tm, d), lambda i: (i, 0)),
                  pl.BlockSpec((tm, d), lambda i: (jnp.maximum(i - first, 0), 0))],
        out_specs=[pl.BlockSpec((tm, d), lambda i: (i, 0)), pl.BlockSpec((1, d), lambda i: (0, 0))],
        out_shape=[jax.ShapeDtypeStruct((lp, d), F32), jax.ShapeDtypeStruct((1, d), F32)],
        compiler_params=_cp(("arbitrary",)),
    )(y, target)


def _gate_fwd(o, zsrc, z_blk0, g, other, *, heads, name):
    lp = o.shape[0]
    tm = _row_tile(lp, 512)
    w = heads * HEAD_W
    assert (z_blk0 * HEAD_W) % w == 0
    has_other = w < D_MODEL

    def body(o_ref, z_ref, g_ref, *rest):
        y_ref = rest[-1]
        gv = g_ref[...]
        for h in range(heads):
            cs = slice(h * HEAD_W, (h + 1) * HEAD_W)
            ov = o_ref[:, cs]
            r = lax.rsqrt(jnp.mean(ov * ov, axis=-1, keepdims=True) + RMS_EPS)
            y_ref[:, cs] = (ov * r * gv * _silu(z_ref[:, cs])).astype(BF16)
        if has_other:
            y_ref[:, w:] = rest[0][...].astype(BF16)

    row = lambda width, blk: pl.BlockSpec((tm, width), lambda i: (i, blk))
    return pl.pallas_call(
        body, name=name, grid=(lp // tm,),
        in_specs=[row(w, 0), row(w, z_blk0 * HEAD_W // w), pl.BlockSpec((1, HEAD_W), lambda i: (0, 0))]
        + ([row(D_MODEL - w, 0)] if has_other else []),
        out_specs=row(D_MODEL, 0), out_shape=jax.ShapeDtypeStruct((lp, D_MODEL), BF16),
        compiler_params=_cp(("parallel",)),
    )(o, zsrc, g.reshape(1, HEAD_W), *([other] if has_other else []))


def _gate_bwd(o, zsrc, z_blk0, g, dy, *, heads, name):
    lp = o.shape[0]
    tm = _row_tile(lp, 512)

    w = heads * HEAD_W
    assert (z_blk0 * HEAD_W) % w == 0

    def body(o_ref, z_ref, g_ref, dy_ref, do_ref, dz_ref, dg_ref):
        @pl.when(pl.program_id(0) == 0)
        def _():
            dg_ref[...] = jnp.zeros_like(dg_ref)

        gv = g_ref[...]
        dg = jnp.zeros((1, HEAD_W), F32)
        for h in range(heads):
            cs = slice(h * HEAD_W, (h + 1) * HEAD_W)
            ov, zv, dyv = o_ref[:, cs], z_ref[:, cs], dy_ref[:, cs]
            r = lax.rsqrt(jnp.mean(ov * ov, axis=-1, keepdims=True) + RMS_EPS)
            nrm = ov * r
            s = _silu(zv)
            dn = dyv * gv * s
            do_ref[:, cs] = r * (dn - nrm * jnp.mean(dn * nrm, axis=-1, keepdims=True))
            dz_ref[:, cs] = dyv * nrm * gv * _silu_grad(zv)
            dg = dg + jnp.sum(dyv * nrm * s, axis=0, keepdims=True)
        dg_ref[...] += dg

    row = lambda blk: pl.BlockSpec((tm, w), lambda i: (i, blk))
    vec = pl.BlockSpec((1, HEAD_W), lambda i: (0, 0))
    return pl.pallas_call(
        body, name=name, grid=(lp // tm,),
        in_specs=[row(0), row(z_blk0 * HEAD_W // w), vec, row(0)], out_specs=[row(0), row(0), vec],
        out_shape=[jax.ShapeDtypeStruct((lp, w), F32), jax.ShapeDtypeStruct((lp, w), F32),
                   jax.ShapeDtypeStruct((1, HEAD_W), F32)],
        compiler_params=_cp(("arbitrary",)),
    )(o, zsrc, g.reshape(1, HEAD_W), dy)


def _conv_taps(x, w):
    acc = w[CONV_K - 1:CONV_K, :] * x
    for k in range(CONV_K - 1):
        acc = acc + w[k:k + 1, :] * pltpu.roll(x, CONV_K - 1 - k, 0)
    return acc


def _gdn_pre_fwd(p0, conv_w, pad, *, name):
    lp = p0.shape[0]
    nq = GDN_HEADS
    qscale = HEAD_W ** -0.5

    def body(x_ref, w_ref, y_ref):
        j = pl.program_id(0)
        c = _conv_taps(x_ref[...], w_ref[...])
        s = _silu(c)
        r = lax.rsqrt(jnp.sum(s * s, axis=-1, keepdims=True) + L2_EPS)
        mult = jnp.where(j < nq, r * qscale, jnp.where(j < 2 * nq, r, 1.0))
        rows = lax.broadcasted_iota(jnp.int32, (lp, 1), 0)
        y_ref[...] = jnp.where(rows >= pad, s * mult, 0.0)

    return pl.pallas_call(
        body, name=name, grid=(3 * nq,),
        in_specs=[pl.BlockSpec((lp, HEAD_W), lambda j: (0, j)), pl.BlockSpec((CONV_K, HEAD_W), lambda j: (0, j))],
        out_specs=pl.BlockSpec((lp, HEAD_W), lambda j: (0, j)),
        out_shape=jax.ShapeDtypeStruct((lp, 3 * nq * HEAD_W), F32), compiler_params=_cp(("parallel",)),
    )(p0, conv_w)


def _gdn_pre_bwd(p0, conv_w, dqkv, pad, *, name):
    lp = p0.shape[0]
    nq = GDN_HEADS
    qscale = HEAD_W ** -0.5

    def body(x_ref, w_ref, dy_ref, dx_ref, dw_ref):
        j = pl.program_id(0)
        x, w = x_ref[...], w_ref[...]
        c = _conv_taps(x, w)
        s = _silu(c)
        r = lax.rsqrt(jnp.sum(s * s, axis=-1, keepdims=True) + L2_EPS)
        rows = lax.broadcasted_iota(jnp.int32, (lp, 1), 0)
        dy = jnp.where(rows >= pad, dy_ref[...], 0.0)
        nrm = s * r
        dn = dy * jnp.where(j < nq, qscale, 1.0)
        ds_norm = r * (dn - nrm * jnp.sum(nrm * dn, axis=-1, keepdims=True))
        ds = jnp.where(j < 2 * nq, ds_norm, dy)
        dc = ds * _silu_grad(c)
        dx = w[CONV_K - 1:CONV_K, :] * dc
        dws = [None] * CONV_K
        dws[CONV_K - 1] = jnp.sum(dc * x, axis=0, keepdims=True)
        for k in range(CONV_K - 1):
            sh = CONV_K - 1 - k
            dx = dx + w[k:k + 1, :] * pltpu.roll(dc, lp - sh, 0)
            dws[k] = jnp.sum(dc * pltpu.roll(x, sh, 0), axis=0, keepdims=True)
        dx_ref[...] = dx
        dw_ref[...] = jnp.concatenate(dws, axis=0)

    blk = pl.BlockSpec((lp, HEAD_W), lambda j: (0, j))
    wblk = pl.BlockSpec((CONV_K, HEAD_W), lambda j: (0, j))
    return pl.pallas_call(
        body, name=name, grid=(3 * nq,), in_specs=[blk, wblk, blk], out_specs=[blk, wblk],
        out_shape=[jax.ShapeDtypeStruct((lp, 3 * nq * HEAD_W), F32),
                   jax.ShapeDtypeStruct((CONV_K, 3 * nq * HEAD_W), F32)],
        compiler_params=_cp(("parallel",)),
    )(p0, conv_w, dqkv)


def _tri(c, strict):
    r = lax.broadcasted_iota(jnp.int32, (c, c), 0)
    q = lax.broadcasted_iota(jnp.int32, (c, c), 1)
    return (q < r) if strict else (q <= r)


@jax.custom_vjp
def _inv_unit_lower(m):
    c = m.shape[0]
    eye = (lax.broadcasted_iota(jnp.int32, (c, c), 0) == lax.broadcasted_iota(jnp.int32, (c, c), 1)).astype(F32)
    x = eye - m
    p = m
    n = 2
    while n < CHUNK:
        p = _bdot(p, p, NN)
        x = x + _bdot(x, p, NN)
        n *= 2
    return x


def _inv_fwd(m):
    t = _inv_unit_lower(m)
    return t, t


def _inv_bwd(t, g):
    return (-_bdot(_bdot(t, g, TN), t, NT),)


_inv_unit_lower.defvjp(_inv_fwd, _inv_bwd)


def _heads_to_rows(x, nh):
    return jnp.concatenate([x[:, h * HEAD_W:(h + 1) * HEAD_W] for h in range(nh)], axis=0)


def _rows_to_heads(x, nh):
    c = x.shape[0] // nh
    return jnp.concatenate([x[h * c:(h + 1) * c] for h in range(nh)], axis=1)


def _gdn_chunk(q, k, v, ba, alog, dtb, states, valid):
    nh = GDN_HEADS
    c = q.shape[0]
    r = nh * c
    lane = lax.broadcasted_iota(jnp.int32, (1, HEAD_W), 1)
    pick = lambda x, l: jnp.sum(jnp.where(lane == l, x, 0.0), axis=-1, keepdims=True)
    beta = jnp.concatenate([jnp.where(valid, _sigmoid(pick(ba, h)), 0.0) for h in range(nh)], axis=0)
    g = jnp.concatenate(
        [jnp.where(valid, -jnp.exp(pick(alog, h)) * _softplus(pick(ba, nh + h) + pick(dtb, h)), 0.0) for h in range(nh)],
        axis=0)
    qs, ks, vs = _heads_to_rows(q, nh), _heads_to_rows(k, nh), _heads_to_rows(v, nh)
    rr = lax.broadcasted_iota(jnp.int32, (r, r), 0)
    cc = lax.broadcasted_iota(jnp.int32, (r, r), 1)
    same = (rr // c) == (cc // c)
    causal, strict = same & (cc <= rr), same & (cc < rr)
    lower = jnp.where(causal, 1.0, 0.0).astype(BF16)
    upper = jnp.where(same & (cc >= rr), 1.0, 0.0).astype(BF16)
    gcb = _mask_mm(lower, upper, g * jnp.ones((1, HEAD_W), F32))
    gc_col = jnp.concatenate([gcb] * (r // HEAD_W), axis=1)
    decay = jnp.where(causal, jnp.exp(jnp.minimum(gc_col - gc_col.T, 0.0)), 0.0)
    egc = jnp.exp(gcb)
    kb = ks * beta
    m = jnp.where(strict, _bdot(kb, ks, NT) * decay, 0.0)
    t = _inv_unit_lower(m)
    u = _bdot(t, vs * beta, NN)
    w = _bdot(t, kb * egc, NN)
    a = _bdot(qs, ks, NT) * decay
    rows = lambda x, h: x[h * c:(h + 1) * c]
    qe = qs * egc
    v_new = u - jnp.concatenate([_bdot(rows(w, h), states[h], NN) for h in range(nh)], axis=0)
    o = jnp.concatenate([_bdot(rows(qe, h), states[h], NN) for h in range(nh)], axis=0) + _bdot(a, v_new, NN)
    new_states = []
    for h in range(nh):
        gl = gcb[(h + 1) * c - 1:(h + 1) * c, :]
        k_dec = rows(ks, h) * jnp.exp(gl - rows(gcb, h))
        new_states.append(states[h] * jnp.exp(gl) + _bdot(k_dec, rows(v_new, h), TN))
    return _rows_to_heads(o, nh), new_states


def _gdn_fwd(qkv, p0, alog_v, dtb_v, pad, *, name):
    lp = qkv.shape[0]
    n = lp // CHUNK
    nh = GDN_HEADS

    def body(q_ref, k_ref, v_ref, ba_ref, al_ref, dt_ref, o_ref, st_ref, s_ref):
        i = pl.program_id(0)

        @pl.when(i == 0)
        def _():
            s_ref[...] = jnp.zeros_like(s_ref)

        valid = (i * CHUNK + lax.broadcasted_iota(jnp.int32, (CHUNK, 1), 0)) >= pad
        s = s_ref[...]
        o, s2 = _gdn_chunk(q_ref[...], k_ref[...], v_ref[...], ba_ref[...], al_ref[...], dt_ref[...],
                           [s[h] for h in range(nh)], valid)
        st_ref[...] = s
        o_ref[...] = o
        for h in range(nh):
            s_ref[h] = s2[h]

    w = nh * HEAD_W
    vec = pl.BlockSpec((1, HEAD_W), lambda i: (0, 0))
    return pl.pallas_call(
        body, name=name, grid=(n,),
        in_specs=[pl.BlockSpec((CHUNK, w), lambda i: (i, 0)), pl.BlockSpec((CHUNK, w), lambda i: (i, 1)),
                  pl.BlockSpec((CHUNK, w), lambda i: (i, 2)), pl.BlockSpec((CHUNK, HEAD_W), lambda i: (i, AB_BA // HEAD_W)),
                  vec, vec],
        out_specs=[pl.BlockSpec((CHUNK, w), lambda i: (i, 0)),
                   pl.BlockSpec((None, nh, HEAD_W, HEAD_W), lambda i: (i, 0, 0, 0))],
        out_shape=[jax.ShapeDtypeStruct((lp, w), F32), jax.ShapeDtypeStruct((n, nh, HEAD_W, HEAD_W), F32)],
        scratch_shapes=[pltpu.VMEM((nh, HEAD_W, HEAD_W), F32)],
        compiler_params=_cp(("arbitrary",)),
    )(qkv, qkv, qkv, p0, alog_v, dtb_v)


def _gdn_bwd(qkv, p0, alog_v, dtb_v, states, do, pad, *, name):
    lp = qkv.shape[0]
    n = lp // CHUNK
    nh = GDN_HEADS

    def body(q_ref, k_ref, v_ref, ba_ref, al_ref, dt_ref, st_ref, do_ref,
             dq_ref, dk_ref, dv_ref, dba_ref, dal_ref, ddt_ref, ds_ref):
        step = pl.program_id(0)
        i = n - 1 - step

        @pl.when(step == 0)
        def _():
            ds_ref[...] = jnp.zeros_like(ds_ref)
            dal_ref[...] = jnp.zeros_like(dal_ref)
            ddt_ref[...] = jnp.zeros_like(ddt_ref)

        valid = (i * CHUNK + lax.broadcasted_iota(jnp.int32, (CHUNK, 1), 0)) >= pad
        st, dst = st_ref[...], ds_ref[...]
        fn = functools.partial(_gdn_chunk, valid=valid)
        _, vjp = jax.vjp(fn, q_ref[...], k_ref[...], v_ref[...], ba_ref[...], al_ref[...], dt_ref[...],
                         [st[h] for h in range(nh)])
        dq, dk, dv, dba, dal, ddt, ds = vjp((do_ref[...], [dst[h] for h in range(nh)]))
        dq_ref[...] = dq
        dk_ref[...] = dk
        dv_ref[...] = dv
        dba_ref[...] = dba
        dal_ref[...] += dal
        ddt_ref[...] += ddt
        for h in range(nh):
            ds_ref[h] = ds[h]

    w = nh * HEAD_W
    rev = lambda c: (lambda s: (n - 1 - s, c))
    vec = pl.BlockSpec((1, HEAD_W), lambda s: (0, 0))
    dq, dk, dv, dba, dal, ddt = pl.pallas_call(
        body, name=name, grid=(n,),
        in_specs=[pl.BlockSpec((CHUNK, w), rev(0)), pl.BlockSpec((CHUNK, w), rev(1)), pl.BlockSpec((CHUNK, w), rev(2)),
                  pl.BlockSpec((CHUNK, HEAD_W), rev(AB_BA // HEAD_W)), vec, vec,
                  pl.BlockSpec((None, nh, HEAD_W, HEAD_W), lambda s: (n - 1 - s, 0, 0, 0)),
                  pl.BlockSpec((CHUNK, w), rev(0))],
        out_specs=[pl.BlockSpec((CHUNK, w), rev(0)), pl.BlockSpec((CHUNK, w), rev(0)), pl.BlockSpec((CHUNK, w), rev(0)),
                   pl.BlockSpec((CHUNK, HEAD_W), rev(0)), vec, vec],
        out_shape=[jax.ShapeDtypeStruct((lp, w), F32)] * 3 + [jax.ShapeDtypeStruct((lp, HEAD_W), F32)]
        + [jax.ShapeDtypeStruct((1, HEAD_W), F32)] * 2,
        scratch_shapes=[pltpu.VMEM((nh, HEAD_W, HEAD_W), F32)],
        compiler_params=_cp(("arbitrary",)),
    )(qkv, qkv, qkv, p0, alog_v, dtb_v, states, do)
    return dq, dk, dv, dba, dal, ddt


HG_LEVELS = (32, 16, 8, 4, 2, 1)
HG_GROUP = 4


def _hg_masks():
    import numpy as np
    c = CHUNK
    t = np.arange(c)[:, None]
    j = np.arange(c)[None, :]
    sums = (j <= t).astype(np.float32)
    pairs = [j == t]
    for m in HG_LEVELS:
        p = (t // (2 * m)) * (2 * m)
        r = p + m
        pairs.append((t >= r) & (j < r) & (j >= p))
    pairs = np.concatenate([np.kron(np.eye(HG_GROUP), p) for p in pairs], axis=0).astype(np.float32)
    return jnp.asarray(sums, BF16), jnp.asarray(sums.T, BF16), jnp.asarray(pairs, F32)


def _hg_level_row(b, m):
    c, w = b.shape
    if m >= 8:
        return jnp.concatenate([jnp.broadcast_to(b[p + m:p + m + 1], (2 * m, w)) for p in range(0, c, 2 * m)], axis=0)
    tiles = b.reshape(c // 8, 8, w)
    sub = lax.broadcasted_iota(jnp.int32, (1, 8, 1), 1)
    out = None
    for r0 in range(m, 8, 2 * m):
        cand = jnp.broadcast_to(tiles[:, r0:r0 + 1, :], tiles.shape)
        out = cand if out is None else jnp.where(sub >= r0 - m, cand, out)
    return out.reshape(c, w)


def _split3(x):
    hi = x.astype(BF16)
    r1 = x - hi.astype(F32)
    mid = r1.astype(BF16)
    return hi, mid, (r1 - mid.astype(F32)).astype(BF16)


def _mask_mm_raw(m, x):
    return sum(_dot(m, part, NN) for part in _split3(x))


@jax.custom_vjp
def _mask_mm(m, mt, x):
    return _mask_mm_raw(m, x)


def _mask_mm_fwd(m, mt, x):
    return _mask_mm_raw(m, x), (m, mt)


def _mask_mm_bwd(res, g):
    m, mt = res
    return jnp.zeros_like(m), jnp.zeros_like(mt), _mask_mm_raw(mt, g)


_mask_mm.defvjp(_mask_mm_fwd, _mask_mm_bwd)


def _hg_chunk(qr, fr, ir, lb, states, valid, sums, sums_t, pairs):
    nh = HG_GROUP
    c = qr.shape[0]
    r = nh * c
    fg = lb + (1.0 - lb) * _sigmoid(fr)
    logf = jnp.where(valid, jnp.log(fg), 0.0)
    k = jnp.where(valid, 1.0 - fg, 0.0)
    qs = jnp.where(valid, _silu(qr), 0.0)
    v = jnp.where(valid, ir, 0.0)
    b = _mask_mm(sums, sums_t, logf)
    mask = lambda n: pairs[n * r:(n + 1) * r]
    stack = lambda x: _heads_to_rows(x, nh)
    a = mask(0) * _bdot(stack(qs), stack(k), NT)
    for lvl, m in enumerate(HG_LEVELS):
        d = b - _hg_level_row(b, m)
        a = a + mask(1 + lvl) * _bdot(stack(qs * jnp.exp(jnp.minimum(d, 0.0))),
                                      stack(k * jnp.exp(jnp.minimum(-d, 0.0))), NT)
    av = _bdot(a, stack(v), NN)
    eb = jnp.exp(b)
    qe, kd = qs * eb, k * jnp.exp(b[c - 1:c] - b)
    outs, new_states = [], []
    for h in range(nh):
        cs = slice(h * HEAD_W, (h + 1) * HEAD_W)
        outs.append(_bdot(qe[:, cs], states[h], NT) + av[h * c:(h + 1) * c])
        new_states.append(states[h] * eb[c - 1:c, cs] + _bdot(v[:, cs], kd[:, cs], TN))
    return jnp.concatenate(outs, axis=1), new_states


def _hg_fwd(p1, lb, pad, *, name):
    lp = p1.shape[0]
    n = lp // CHUNK
    nh = HG_HEADS

    def body(q_ref, f_ref, i_ref, lb_ref, sums_ref, sums_t_ref, pairs_ref, o_ref, st_ref, s_ref):
        i = pl.program_id(1)

        @pl.when(i == 0)
        def _():
            s_ref[...] = jnp.zeros_like(s_ref)

        valid = (i * CHUNK + lax.broadcasted_iota(jnp.int32, (CHUNK, 1), 0)) >= pad
        s = s_ref[...]
        o, s2 = _hg_chunk(q_ref[...], f_ref[...], i_ref[...], lb_ref[...], [s[h] for h in range(grp)], valid,
                          sums_ref[...], sums_t_ref[...], pairs_ref[...])
        st_ref[...] = s
        o_ref[...] = o
        for h in range(grp):
            s_ref[h] = s2[h]

    masks = _hg_masks()
    grp, ngrp, gw = HG_GROUP, nh // HG_GROUP, HG_GROUP * HEAD_W
    blk = lambda off: pl.BlockSpec((CHUNK, gw), lambda h, i: (i, off + h))
    const = lambda a: pl.BlockSpec(a.shape, lambda h, i: (0, 0))
    return pl.pallas_call(
        body, name=name, grid=(ngrp, n),
        in_specs=[blk(0), blk(ngrp), blk(2 * ngrp), pl.BlockSpec((1, gw), lambda h, i: (0, h))]
        + [const(a) for a in masks],
        out_specs=[blk(0), pl.BlockSpec((grp, None, HEAD_W, HEAD_W), lambda h, i: (h, i, 0, 0))],
        out_shape=[jax.ShapeDtypeStruct((lp, nh * HEAD_W), F32), jax.ShapeDtypeStruct((nh, n, HEAD_W, HEAD_W), F32)],
        scratch_shapes=[pltpu.VMEM((grp, HEAD_W, HEAD_W), F32)],
        compiler_params=_cp(("parallel", "arbitrary")),
    )(p1, p1, p1, lb, *masks)


def _hg_bwd(p1, lb, states, do, pad, *, name):
    lp = p1.shape[0]
    n = lp // CHUNK
    nh = HG_HEADS

    def body(q_ref, f_ref, i_ref, lb_ref, st_ref, do_ref, sums_ref, sums_t_ref, pairs_ref,
             dq_ref, df_ref, di_ref, dlb_ref, ds_ref):
        step = pl.program_id(1)
        i = n - 1 - step

        @pl.when(step == 0)
        def _():
            ds_ref[...] = jnp.zeros_like(ds_ref)
            dlb_ref[...] = jnp.zeros_like(dlb_ref)

        valid = (i * CHUNK + lax.broadcasted_iota(jnp.int32, (CHUNK, 1), 0)) >= pad
        fn = functools.partial(_hg_chunk, valid=valid, sums=sums_ref[...], sums_t=sums_t_ref[...],
                               pairs=pairs_ref[...])
        st, dst = st_ref[...], ds_ref[...]
        _, vjp = jax.vjp(fn, q_ref[...], f_ref[...], i_ref[...], lb_ref[...], [st[h] for h in range(grp)])
        dq, df, di, dlb, ds = vjp((do_ref[...], [dst[h] for h in range(grp)]))
        dq_ref[...] = dq
        df_ref[...] = df
        di_ref[...] = di
        dlb_ref[...] += dlb
        for h in range(grp):
            ds_ref[h] = ds[h]

    masks = _hg_masks()
    grp, ngrp, gw = HG_GROUP, nh // HG_GROUP, HG_GROUP * HEAD_W
    blk = lambda off: pl.BlockSpec((CHUNK, gw), lambda h, s: (n - 1 - s, off + h))
    const = lambda a: pl.BlockSpec(a.shape, lambda h, s: (0, 0))
    w = nh * HEAD_W
    return pl.pallas_call(
        body, name=name, grid=(ngrp, n),
        in_specs=[blk(0), blk(ngrp), blk(2 * ngrp), pl.BlockSpec((1, gw), lambda h, s: (0, h)),
                  pl.BlockSpec((grp, None, HEAD_W, HEAD_W), lambda h, s: (h, n - 1 - s, 0, 0)), blk(0)]
        + [const(a) for a in masks],
        out_specs=[blk(0), blk(0), blk(0), pl.BlockSpec((1, gw), lambda h, s: (0, h))],
        out_shape=[jax.ShapeDtypeStruct((lp, w), F32)] * 3 + [jax.ShapeDtypeStruct((1, w), F32)],
        scratch_shapes=[pltpu.VMEM((grp, HEAD_W, HEAD_W), F32)],
        compiler_params=_cp(("parallel", "arbitrary")),
    )(p1, p1, p1, lb, states, do, *masks)


SB_GROUP = 4


def _sb_cat(kind, first_key=0):
    r = lax.broadcasted_iota(jnp.int32, (SB_BLOCK, 2 * SB_BLOCK), 0)
    c = lax.broadcasted_iota(jnp.int32, (SB_BLOCK, 2 * SB_BLOCK), 1)
    tri = {"after": c < r, "incl": r <= c, "before": r < c}[kind]
    m = ((c >= SB_BLOCK) | tri) & (r >= first_key)
    return jnp.where(m, 1.0, 0.0).astype(BF16)


def _sb_cumsum(x, cat):
    return _dot(x.astype(BF16), cat, NN)


def _sb_logsig(z):
    e = jnp.exp(-jnp.abs(z))
    lse = jnp.where(e < 1e-4, e, jnp.log(1.0 + e))
    lsz = jnp.minimum(z, 0.0) - lse
    return lsz, lsz - z, e


def _sb_stack(x, scale=None):
    lane = lax.broadcasted_iota(jnp.int32, (1, HEAD_W), 1)
    if scale is not None:
        x = x * scale
    return jnp.concatenate([jnp.where(lane < SB_DH, x, 0.0), jnp.where(lane >= SB_DH, x, 0.0)], axis=0).astype(BF16)


def _sb_unstack(x):
    lane = lax.broadcasted_iota(jnp.int32, (1, HEAD_W), 1)
    return jnp.where(lane < SB_DH, x[:SB_BLOCK], x[SB_BLOCK:])


def _sb_fwd(p0, pad, *, name, gather=None):
    lp = p0.shape[0]
    nb = lp // SB_BLOCK
    npair = SB_HEADS // 2
    blk0 = AB_SB // HEAD_W
    scale = SB_DH ** -0.5
    gw = SB_GROUP * SB_BLOCK
    assert pad < SB_BLOCK
    g_srcs, g_dtypes = gather if gather is not None else ([], [])
    ng_arr = len(g_srcs)

    def body(q_ref, k_ref, v_ref, *rest):
        g_ins, (o_ref, tot_ref) = rest[:ng_arr], rest[ng_arr:ng_arr + 2]
        g_outs, g_scratch = rest[ng_arr + 2:2 * ng_arr + 2], rest[2 * ng_arr + 2:]
        first_step = (pl.program_id(0) == 0) & (pl.program_id(1) == 0)
        last_pair = pl.program_id(0) == npair - 1
        if ng_arr:
            g_start, g_forward, g_finish = _gather_phases(g_ins, g_outs, g_scratch[:ng_arr], *g_scratch[ng_arr:],
                                                          g_dtypes)
            pl.when(first_step)(g_start)
            pl.when(last_pair & (pl.program_id(1) == 0))(g_forward)
        i = pl.program_id(1)
        qs = _sb_stack(q_ref[...], scale)
        qpos = i * SB_BLOCK + lax.broadcasted_iota(jnp.int32, (SB_BLOCK, 1), 0)
        qpos = jnp.concatenate([qpos, qpos], axis=0)
        cat = _sb_cat("after")
        cat0 = _sb_cat("after", pad)
        ng = i // SB_GROUP

        def group(off, nblk, first_cat, allowed, carry):
            acc, run = carry
            kg = k_ref[pl.ds(off, nblk * SB_BLOCK), :].astype(BF16)
            vg = v_ref[pl.ds(off, nblk * SB_BLOCK), :].astype(BF16)
            lsz, l1m, _ = _sb_logsig(_dot(qs, kg, NT))
            if allowed is not None:
                l1m = jnp.where(allowed, l1m, 0.0)
            args = [None] * nblk
            for g in reversed(range(nblk)):
                sl = slice(g * SB_BLOCK, (g + 1) * SB_BLOCK)
                al = _sb_cumsum(l1m[:, sl], first_cat if g == 0 else cat)
                args[g] = lsz[:, sl] + al[:, :SB_BLOCK] + run
                run = run + al[:, SB_BLOCK:]
            wgt = jnp.exp(jnp.concatenate(args, axis=1))
            if allowed is not None:
                wgt = jnp.where(allowed, wgt, 0.0)
            return acc + _dot(wgt.astype(BF16), vg, NN), run

        def below(t, carry):
            gi = ng - 1 - t
            return group(pl.multiple_of(gi * gw, gw), SB_GROUP, jnp.where(gi == 0, cat0, cat), None, carry)

        top = ng * gw

        def top_group(nblk, carry):
            off = pl.multiple_of(jnp.minimum(top, lp - nblk * SB_BLOCK), SB_BLOCK)
            kpos = off + lax.broadcasted_iota(jnp.int32, (1, nblk * SB_BLOCK), 1)
            return group(off, nblk, cat, (kpos < qpos) & (kpos >= pad) & (kpos >= top), carry)

        zero = (jnp.zeros((2 * SB_BLOCK, HEAD_W), F32), jnp.zeros((2 * SB_BLOCK, HEAD_W), F32))
        carry = lax.cond(i - ng * SB_GROUP < SB_GROUP // 2, functools.partial(top_group, SB_GROUP // 2),
                         functools.partial(top_group, SB_GROUP), zero)
        acc, run = lax.fori_loop(0, ng, below, carry)
        o_ref[...] = _sb_unstack(acc)
        tot_ref[...] = _sb_unstack(run)
        if ng_arr:
            pl.when(last_pair & (pl.program_id(1) == nb - 1))(g_finish)

    full = lambda c0: pl.BlockSpec((lp, HEAD_W), lambda p, i: (0, c0 + p))
    out = pl.BlockSpec((SB_BLOCK, HEAD_W), lambda p, i: (i, p))
    return pl.pallas_call(
        body, name=name, grid=(npair, nb),
        in_specs=[pl.BlockSpec((SB_BLOCK, HEAD_W), lambda p, i: (i, blk0 + p)), full(blk0 + npair), full(blk0 + 2 * npair)]
        + [pl.BlockSpec(memory_space=pltpu.VMEM)] * ng_arr,
        out_specs=[out, out] + [_ANY] * ng_arr,
        out_shape=[jax.ShapeDtypeStruct((lp, npair * HEAD_W), F32)] * 2 + _gather_out_shapes(g_srcs, g_dtypes),
        scratch_shapes=_gather_scratch(g_srcs, g_dtypes) if ng_arr else [],
        compiler_params=_cp(("arbitrary", "arbitrary"), has_side_effects=bool(ng_arr)),
    )(p0, p0, p0, *g_srcs)


def _sb_bwd(p0, tot, dsrc, d_blk0, pad, *, name, scatter=()):
    lp = p0.shape[0]
    nb = lp // SB_BLOCK
    npair = SB_HEADS // 2
    blk0 = AB_SB // HEAD_W
    scale = SB_DH ** -0.5
    gw = SB_GROUP * SB_BLOCK
    assert pad < SB_BLOCK
    ns = len(scatter)

    def body(q_ref, k_ref, v_ref, tot_ref, do_ref, *rest):
        s_ins, (dq_ref, dkt_ref, dvt_ref) = rest[:ns], rest[ns:ns + 3]
        s_outs, s_sems = rest[ns + 3:2 * ns + 3], rest[2 * ns + 3:]
        if ns:
            s_start, s_finish = _scatter_phases(s_ins, s_outs, *s_sems)
            pl.when((pl.program_id(0) == 0) & (pl.program_id(1) == 0))(s_start)
        i = pl.program_id(1)

        @pl.when(i == 0)
        def _():
            dkt_ref[...] = jnp.zeros_like(dkt_ref)
            dvt_ref[...] = jnp.zeros_like(dvt_ref)

        qs = _sb_stack(q_ref[...], scale)
        dos = _sb_stack(do_ref[...])
        qst, dost = qs.T, dos.T
        totv = tot_ref[...]
        ones = jnp.ones((1, HEAD_W), F32)
        tots = jnp.concatenate([totv[:, 0:1] * ones, totv[:, SB_DH:SB_DH + 1] * ones], axis=0)
        qpos = i * SB_BLOCK + lax.broadcasted_iota(jnp.int32, (SB_BLOCK, 1), 0)
        qpos = jnp.concatenate([qpos, qpos], axis=0)
        incl, incl0 = _sb_cat("incl"), _sb_cat("incl", pad)
        before = _sb_cat("before")
        ng = i // SB_GROUP

        def dscore(z, e, ev, dl1m):
            r = 1.0 / (1.0 + e)
            sg = jnp.where(z >= 0, r, e * r)
            return ev * (1.0 - sg) - dl1m * sg

        def group(off, nblk, first_incl, allowed, carry):
            dq, prun, erun = carry
            width = nblk * SB_BLOCK
            kg = k_ref[pl.ds(off, width), :].astype(BF16)
            vg = v_ref[pl.ds(off, width), :].astype(BF16)
            z = _dot(qs, kg, NT)
            lsz, l1m, e = _sb_logsig(z)
            if allowed is not None:
                l1m = jnp.where(allowed, l1m, 0.0)
            dwgt = _dot(dos, vg, NT)
            dzs = [None] * nblk
            wgts = [None] * nblk
            for g in range(nblk):
                sl = slice(g * SB_BLOCK, (g + 1) * SB_BLOCK)
                al = _sb_cumsum(l1m[:, sl], first_incl if g == 0 else incl)
                wgt = jnp.exp(lsz[:, sl] + (tots - prun - al[:, :SB_BLOCK]))
                if allowed is not None:
                    wgt = jnp.where(allowed[:, sl], wgt, 0.0)
                prun = prun + al[:, SB_BLOCK:]
                ev = wgt * dwgt[:, sl]
                el = _sb_cumsum(ev, before)
                dzs[g] = dscore(z[:, sl], e[:, sl], ev, erun + el[:, :SB_BLOCK])
                erun = erun + el[:, SB_BLOCK:]
                wgts[g] = wgt
            dz = jnp.concatenate(dzs, axis=1)
            if allowed is not None:
                dz = jnp.where(allowed, dz, 0.0)
            dz = dz.astype(BF16)
            wg = jnp.concatenate(wgts, axis=1).astype(BF16)
            dkt_ref[:, pl.ds(off, width)] += _dot(qst, dz, NN)
            dvt_ref[:, pl.ds(off, width)] += _dot(dost, wg, NN)
            return dq + _dot(dz, kg, NN), prun, erun

        def below(gi, carry):
            return group(pl.multiple_of(gi * gw, gw), SB_GROUP, jnp.where(gi == 0, incl0, incl), None, carry)

        zero = tuple(jnp.zeros((2 * SB_BLOCK, HEAD_W), F32) for _ in range(3))
        carry = lax.fori_loop(0, ng, below, zero)
        top = ng * gw

        def top_group(nblk, carry):
            off = pl.multiple_of(jnp.minimum(top, lp - nblk * SB_BLOCK), SB_BLOCK)
            kpos = off + lax.broadcasted_iota(jnp.int32, (1, nblk * SB_BLOCK), 1)
            return group(off, nblk, incl, (kpos < qpos) & (kpos >= pad) & (kpos >= top), carry)

        dq, _, _ = lax.cond(i - ng * SB_GROUP < SB_GROUP // 2, functools.partial(top_group, SB_GROUP // 2),
                            functools.partial(top_group, SB_GROUP), carry)
        dq_ref[...] = _sb_unstack(dq) * scale
        if ns:
            pl.when((pl.program_id(0) == npair - 1) & (pl.program_id(1) == nb - 1))(s_finish)

    full = lambda c0: pl.BlockSpec((lp, HEAD_W), lambda p, i: (0, c0 + p))
    qb = lambda c0: pl.BlockSpec((SB_BLOCK, HEAD_W), lambda p, i: (i, c0 + p))
    tr = pl.BlockSpec((HEAD_W, lp), lambda p, i: (p, 0))
    return pl.pallas_call(
        body, name=name, grid=(npair, nb),
        in_specs=[qb(blk0), full(blk0 + npair), full(blk0 + 2 * npair), qb(0), qb(d_blk0)] + [_ANY] * ns,
        out_specs=[qb(0), tr, tr] + [_ANY] * ns,
        out_shape=[jax.ShapeDtypeStruct((lp, npair * HEAD_W), F32)]
        + [jax.ShapeDtypeStruct((npair * HEAD_W, lp), F32)] * 2
        + [jax.ShapeDtypeStruct(s.shape, s.dtype) for s in scatter],
        scratch_shapes=_scatter_scratch(ns) if ns else [],
        compiler_params=_cp(("arbitrary", "arbitrary"), has_side_effects=bool(ns)),
    )(p0, p0, p0, tot, dsrc, *scatter)


def _local_step(h0, target, pad, wts, hooks=None):
    lp = h0.shape[0]
    tm = _row_tile(lp, 1056)
    tkl = tm
    tml = _row_tile(lp, 528)
    d = D_MODEL
    mm = _mm
    mmw = functools.partial(_mm, out_dtype=BF16)
    g = {}

    h0_b = h0.astype(BF16)
    p0 = mm(h0_b, wts["w_ab"], "NN", tm=tm, tn=768, tk=d, name="l0_in_proj")
    ob, sb_tot, *gathered = _sb_fwd(p0, pad, name="sb_fwd", gather=hooks["late_gather"] if hooks else None)
    if hooks:
        wts = {**wts, **hooks["later_weights"](gathered)}
    qkv = _gdn_pre_fwd(p0, wts["conv_w"], pad, name="gdn_pre_fwd")
    oa_raw, gdn_states = _gdn_fwd(qkv, p0, wts["alog_v"], wts["dtb_v"], pad, name="gdn_fwd")
    oab = _gate_fwd(oa_raw, p0, AB_Z // HEAD_W, wts["ab_gn"], ob, heads=GDN_HEADS, name="gdn_gate_fwd")
    ln = lambda kind, layer: (wts[f"ln_{kind}_g"][layer], wts[f"ln_{kind}_b"][layer])
    pre_mix0, h0a, h0a_b = mm(oab, wts["w_out0"], "NN", tm=tml, tn=d, tk=d, epi="ln", c=h0, scale=DN_ALPHA,
                              ln=ln("mix", 0), name="l0_out_proj")
    u0 = mm(h0a_b, wts["w1"][0], "NN", tm=tm, tn=512, tk=d, b_dev=True, name="mlp0_up")
    pre_ffn0, h0b, h0b_b = mm(u0, wts["w2"][0], "NN", tm=tml, tn=d, tk=d, a_fn="relu2", epi="ln", c=h0a,
                              scale=DN_ALPHA, ln=ln("ffn", 0), name="mlp0_down")
    p1 = mm(h0b_b, wts["w_c"], "NN", tm=tm, tn=512, tk=d, b_dev=True, name="l1_in_proj")
    oc_raw, hg_states = _hg_fwd(p1, wts["lb"], pad, name="hg_fwd")
    oc = _gate_fwd(oc_raw, p1, 3 * HG_HEADS, wts["c_gn"], oc_raw, heads=HG_HEADS, name="hg_gate_fwd")
    pre_mix1, h1a, h1a_b = mm(oc, wts["w_out1"], "NN", tm=tml, tn=d, tk=d, epi="ln", c=h0b, scale=DN_ALPHA,
                              ln=ln("mix", 1), name="l1_out_proj")
    u1 = mm(h1a_b, wts["w1"][1], "NN", tm=tm, tn=512, tk=d, b_dev=True, name="mlp1_up")
    pre_ffn1, h1b, _ = mm(u1, wts["w2"][1], "NN", tm=tml, tn=d, tk=d, a_fn="relu2", epi="ln", c=h1a, scale=DN_ALPHA,
                          ln=ln("ffn", 1), name="mlp1_down")
    dy, loss_vec = _loss_head(h1b, target, name="loss_head")

    def mlp_bwd(layer, h_in_b, u, dpre, dpre_b):
        du = mm(dpre_b, wts["w2"][layer], "NT", tm=tm, tn=1024, tk=d, epi="relu2grad", c=u, out_dtype=BF16,
                name=f"mlp{layer}_d_hidden")
        dw2 = mmw(u, dpre_b, "TN", tm=1024, tn=1024, tk=tkl, a_fn="relu2", name=f"mlp{layer}_dw2")
        dw1 = mmw(h_in_b, du, "TN", tm=1024, tn=512, tk=tkl, out_dev=True, name=f"mlp{layer}_dw1")
        dh = mm(du, wts["w1"][layer], "NT", tm=tm, tn=1024, tk=512, b_dev=True, epi="add", c=dpre, scale=DN_ALPHA,
                name=f"mlp{layer}_d_in")
        return dh, dw1, dw2

    ln_ffn_dg, ln_ffn_db, ln_mix_dg, ln_mix_db, dw1s, dw2s = ([None, None] for _ in range(6))
    dpre, dpre_b, ln_ffn_dg[1], ln_ffn_db[1] = _ln_bwd(pre_ffn1, wts["ln_ffn_g"][1], dy, name="ln_ffn1_bwd")
    dh1a, dw1s[1], dw2s[1] = mlp_bwd(1, h1a_b, u1, dpre, dpre_b)
    dpre, dpre_b, ln_mix_dg[1], ln_mix_db[1] = _ln_bwd(pre_mix1, wts["ln_mix_g"][1], dh1a, name="ln_mix1_bwd")
    g["c_w_out"] = mmw(oc, dpre_b, "TN", tm=1024, tn=1024, tk=tkl, name="l1_dw_out")
    doc = mm(dpre_b, wts["w_out1"], "NT", tm=tm, tn=1024, tk=d, name="l1_d_gate")
    doc_raw, dz1, g["c_gn"] = _gate_bwd(oc_raw, p1, 3 * HG_HEADS, wts["c_gn"], doc, heads=HG_HEADS, name="hg_gate_bwd")
    dq1, df1, di1, g["lb"] = _hg_bwd(p1, wts["lb"], hg_states, doc_raw, pad, name="hg_bwd")
    dp1 = jnp.concatenate([dq1, df1, di1, dz1], axis=1).astype(BF16)
    g["c_w_in"] = mmw(h0b_b, dp1, "TN", tm=1024, tn=512, tk=tkl, out_dev=True, name="l1_dw_in")
    dh0b = mm(dp1, wts["w_c"], "NT", tm=tm, tn=1024, tk=512, b_dev=True, epi="add", c=dpre, scale=DN_ALPHA,
              name="l1_d_in")
    dpre, dpre_b, ln_ffn_dg[0], ln_ffn_db[0] = _ln_bwd(pre_ffn0, wts["ln_ffn_g"][0], dh0b, name="ln_ffn0_bwd")
    dh0a, dw1s[0], dw2s[0] = mlp_bwd(0, h0a_b, u0, dpre, dpre_b)
    dpre, dpre_b, ln_mix_dg[0], ln_mix_db[0] = _ln_bwd(pre_mix0, wts["ln_mix_g"][0], dh0a, name="ln_mix0_bwd")
    g["ab_w_out"] = mmw(oab, dpre_b, "TN", tm=1024, tn=1024, tk=tkl, name="l0_dw_out")
    doab = mm(dpre_b, wts["w_out0"], "NT", tm=tm, tn=1024, tk=d, name="l0_d_gate")
    doa_raw, dz0, g["ab_gn"] = _gate_bwd(oa_raw, p0, AB_Z // HEAD_W, wts["ab_gn"], doab, heads=GDN_HEADS,
                                         name="gdn_gate_bwd")
    early = ()
    if hooks:
        rows = lambda a, n: a.reshape(N_DEV, n // N_DEV, d)
        early = [g["c_w_in"], rows(g["c_w_out"], d), dw1s[1], rows(dw2s[1], D_FF), dw1s[0], rows(dw2s[0], D_FF),
                 rows(g["ab_w_out"], d)]
    dqb, dkb_t, dvb_t, *g["early_parts"] = _sb_bwd(p0, sb_tot, doab, GDN_HEADS, pad, name="sb_bwd", scatter=early)
    dkb, dvb = dkb_t.T, dvb_t.T
    dqn, dkn, dvn, dba, g["alog_v"], g["dtb_v"] = _gdn_bwd(qkv, p0, wts["alog_v"], wts["dtb_v"], gdn_states, doa_raw,
                                                           pad, name="gdn_bwd")
    dconv_in, g["conv_w"] = _gdn_pre_bwd(p0, wts["conv_w"], jnp.concatenate([dqn, dkn, dvn], axis=1), pad,
                                         name="gdn_pre_bwd")
    dp0 = jnp.concatenate([dconv_in, dz0, dqb, dkb, dvb, dba, jnp.zeros((lp, AB_CAT - AB_BA - HEAD_W), F32)],
                          axis=1).astype(BF16)
    g["w_ab"] = mmw(h0_b, dp0, "TN", tm=1024, tn=768, tk=tkl, name="l0_dw_in")
    last = ()
    if hooks:
        gab, ba0 = g["w_ab"], AB_Z + GDN_HEADS * HEAD_W
        gab = jnp.concatenate([gab[:, :ba0], gab[:, AB_BA:AB_BA + 2 * GDN_HEADS], gab[:, ba0:AB_BA]], axis=1)
        last = [gab.reshape(d, N_DEV, AB_IN // N_DEV).transpose(1, 0, 2)]
    res = mm(dp0, wts["w_ab"], "NT", tm=tm, tn=1024, tk=768, epi="add", c=dpre, scale=DN_ALPHA, scatter=last,
             name="l0_d_in")
    dh0, g["last_parts"] = (res[0], res[1:]) if last else (res, [])

    g["w1"], g["w2"] = dw1s, dw2s
    g["ln_mix_g"] = jnp.concatenate(ln_mix_dg, axis=0)
    g["ln_mix_b"] = jnp.concatenate(ln_mix_db, axis=0)
    g["ln_ffn_g"] = jnp.concatenate(ln_ffn_dg, axis=0)
    g["ln_ffn_b"] = jnp.concatenate(ln_ffn_db, axis=0)
    return loss_vec, dh0, g


N_CHIP = N_DEV // 2


def _place():
    x, y, c = lax.axis_index("x"), lax.axis_index("y"), lax.axis_index("c")
    return x, y, c, 2 * x + y


def _chip_dev(chip, core):
    return (chip // 2, chip % 2, core)


def _remote(src, dst, send_sem, recv_sem, dev):
    return pltpu.make_async_remote_copy(src_ref=src, dst_ref=dst, send_sem=send_sem, recv_sem=recv_sem,
                                        device_id=dev, device_id_type=pl.DeviceIdType.MESH)


_ANY = pl.BlockSpec(memory_space=pl.ANY)


def _gather(srcs, dtypes, *, name):
    n = len(srcs)

    def body(*refs):
        start, forward, finish = _gather_phases(refs[:n], refs[n:2 * n], refs[2 * n:3 * n], *refs[3 * n:], dtypes)
        start()
        forward()
        finish()

    return pl.pallas_call(
        body, name=name, in_specs=[pl.BlockSpec(memory_space=pltpu.VMEM)] * n, out_specs=[_ANY] * n,
        out_shape=_gather_out_shapes(srcs, dtypes), scratch_shapes=_gather_scratch(srcs, dtypes),
        compiler_params=_cp(has_side_effects=True),
    )(*srcs)


def _gather_out_shapes(srcs, dtypes):
    return [jax.ShapeDtypeStruct((N_DEV, *s.shape), dt) for s, dt in zip(srcs, dtypes)]


def _gather_scratch(srcs, dtypes):
    n = len(srcs)
    return [pltpu.VMEM(s.shape, dt) for s, dt in zip(srcs, dtypes)] + [
        pltpu.SemaphoreType.DMA((n, 2 * N_CHIP - 1)), pltpu.SemaphoreType.DMA((n, 2 * N_CHIP - 1)),
        pltpu.SemaphoreType.DMA((n,))]


def _gather_phases(ins, outs, stages, send_sems, recv_sems, local_sems, dtypes):
    n = len(ins)
    x, y, c, chip = _place()
    me = 2 * chip + c
    sibling = (x, y, 1 - c)

    def own(i):
        cps = [_remote(stages[i], outs[i].at[me], send_sems.at[i, 0], recv_sems.at[i, 0], sibling)]
        for j in range(1, N_CHIP):
            cps.append(_remote(stages[i], outs[i].at[me], send_sems.at[i, j], recv_sems.at[i, j],
                               _chip_dev(jnp.bitwise_xor(chip, j), c)))
        return cps

    def local(i):
        return pltpu.make_async_copy(stages[i], outs[i].at[me], local_sems.at[i])

    def passed_on(i, j):
        slot = outs[i].at[2 * jnp.bitwise_xor(chip, j) + c]
        return _remote(slot, slot, send_sems.at[i, N_CHIP - 1 + j], recv_sems.at[i, N_CHIP - 1 + j], sibling)

    def start():
        for i in range(n):
            stages[i][...] = ins[i][...].astype(dtypes[i])
            local(i).start()
            for cp in own(i):
                cp.start()

    def forward():
        for i in range(n):
            for j in range(1, N_CHIP):
                own(i)[j].wait_recv()
                passed_on(i, j).start()

    def finish():
        for i in range(n):
            own(i)[0].wait_recv()
            for j in range(1, N_CHIP):
                passed_on(i, j).wait_recv()
        for i in range(n):
            for cp in own(i):
                cp.wait_send()
            for j in range(1, N_CHIP):
                passed_on(i, j).wait_send()
            local(i).wait()

    return start, forward, finish


def _scatter_scratch(n):
    return [pltpu.SemaphoreType.DMA((n, N_DEV - 1)), pltpu.SemaphoreType.DMA((n, N_DEV - 1)),
            pltpu.SemaphoreType.DMA((n,))]


def _scatter_phases(ins, outs, send_sems, recv_sems, local_sems):
    n = len(ins)
    _, _, c, chip = _place()
    me = 2 * chip + c

    def copies():
        cps = []
        for i in range(n):
            cps.append(pltpu.make_async_copy(ins[i].at[me], outs[i].at[me], local_sems.at[i]))
            for k in range(1, N_DEV):
                peer = jnp.bitwise_xor(me, k)
                cps.append(_remote(ins[i].at[peer], outs[i].at[me], send_sems.at[i, k - 1], recv_sems.at[i, k - 1],
                                   _chip_dev(peer // 2, peer % 2)))
        return cps

    def start():
        for cp in copies():
            cp.start()

    def finish():
        for cp in copies():
            cp.wait()

    return start, finish


def _adamw(w, parts, m, v, *, name):
    r, c = w.shape
    s = parts.shape[0]
    tm = _row_tile(r, 128) if r % 8 == 0 else r
    c1 = 1.0 - ADAM_B1 ** ADAM_STEP
    c2 = 1.0 - ADAM_B2 ** ADAM_STEP

    def body(w_ref, p_ref, m_ref, v_ref, g_ref, d_ref, m2_ref, v2_ref):
        g = p_ref[0].astype(F32)
        for j in range(1, s):
            g = g + p_ref[j].astype(F32)
        m2 = ADAM_B1 * m_ref[...] + (1.0 - ADAM_B1) * g
        v2 = ADAM_B2 * v_ref[...] + (1.0 - ADAM_B2) * jnp.square(g)
        g_ref[...] = g
        m2_ref[...] = m2
        v2_ref[...] = v2
        d_ref[...] = -ADAM_LR * ((m2 / c1) / (jnp.sqrt(v2 / c2) + ADAM_EPS) + ADAM_WD * w_ref[...])

    blk = pl.BlockSpec((tm, c), lambda i: (i, 0))
    return pl.pallas_call(
        body, name=name, grid=(r // tm,),
        in_specs=[blk, pl.BlockSpec((s, tm, c), lambda i: (0, i, 0)), blk, blk], out_specs=[blk] * 4,
        out_shape=[jax.ShapeDtypeStruct((r, c), F32)] * 4, compiler_params=_cp(("parallel",)),
    )(w, parts, m, v)


_WEIGHTS = ("meta_tokens", "ab_w_in", "ab_conv_w", "ab_a_log", "ab_dt_bias", "ab_gnorm_g", "ab_w_out", "c_w_in",
            "c_lb_raw", "c_gnorm_g", "c_w_out", "ln_mix_g", "ln_mix_b", "mlp_w1", "mlp_w2", "ln_ffn_g", "ln_ffn_b")
_PACK_ROWS = (("ln_mix_g", 0), ("ln_mix_b", 2), ("ln_ffn_g", 4), ("ln_ffn_b", 6), ("c_lb_raw", 8))
_PACK_MISC_ROW = 10
_PACK_MISC = (("ab_gnorm_g", 0, 128), ("c_gnorm_g", 128, 128), ("ab_a_log", 256, GDN_HEADS), ("ab_dt_bias", 260, GDN_HEADS))
_PACK_N = 16
_SMALL_META = 16
_SMALL_CONV = 32
_SMALL_N = 40


def _pack_replicated(p):
    rows = jnp.zeros((_PACK_N, D_MODEL), F32)
    for name, r0 in _PACK_ROWS:
        rows = rows.at[r0:r0 + 2].set(p[name])
    for name, c0, width in _PACK_MISC:
        rows = rows.at[_PACK_MISC_ROW, c0:c0 + width].set(p[name].reshape(width))
    return rows


def _unpack_replicated(rows, like):
    out = {}
    for name, r0 in _PACK_ROWS:
        out[name] = rows[r0:r0 + 2]
    for name, c0, width in _PACK_MISC:
        out[name] = rows[_PACK_MISC_ROW, c0:c0 + width].reshape(like[name].shape)
    return out


def _lower_bound(c_lb_raw):
    lb_all = jnp.cumsum(jax.nn.softmax(c_lb_raw.astype(F32), axis=0), axis=0)
    return (lb_all - lb_all[0:1])[1].reshape(1, -1)


def kernel(x, meta_tokens, ab_w_in, ab_conv_w, ab_a_log, ab_dt_bias, ab_gnorm_g, ab_w_out, c_w_in, c_lb_raw, c_gnorm_g, c_w_out, ln_mix_g, ln_mix_b, mlp_w1, mlp_w2, ln_ffn_g, ln_ffn_b, loss_target, m_meta_tokens, m_ab_w_in, m_ab_conv_w, m_ab_a_log, m_ab_dt_bias, m_ab_gnorm_g, m_ab_w_out, m_c_w_in, m_c_lb_raw, m_c_gnorm_g, m_c_w_out, m_ln_mix_g, m_ln_mix_b, m_mlp_w1, m_mlp_w2, m_ln_ffn_g, m_ln_ffn_b, v_meta_tokens, v_ab_w_in, v_ab_conv_w, v_ab_a_log, v_ab_dt_bias, v_ab_gnorm_g, v_ab_w_out, v_c_w_in, v_c_lb_raw, v_c_gnorm_g, v_c_w_out, v_ln_mix_g, v_ln_mix_b, v_mlp_w1, v_mlp_w2, v_ln_ffn_g, v_ln_ffn_b):
    w = dict(zip(_WEIGHTS, (meta_tokens, ab_w_in, ab_conv_w, ab_a_log, ab_dt_bias, ab_gnorm_g, ab_w_out, c_w_in, c_lb_raw,
                            c_gnorm_g, c_w_out, ln_mix_g, ln_mix_b, mlp_w1, mlp_w2, ln_ffn_g, ln_ffn_b)))
    mom = dict(zip(_WEIGHTS, (m_meta_tokens, m_ab_w_in, m_ab_conv_w, m_ab_a_log, m_ab_dt_bias, m_ab_gnorm_g, m_ab_w_out,
                              m_c_w_in, m_c_lb_raw, m_c_gnorm_g, m_c_w_out, m_ln_mix_g, m_ln_mix_b, m_mlp_w1, m_mlp_w2,
                              m_ln_ffn_g, m_ln_ffn_b)))
    var = dict(zip(_WEIGHTS, (v_meta_tokens, v_ab_w_in, v_ab_conv_w, v_ab_a_log, v_ab_dt_bias, v_ab_gnorm_g, v_ab_w_out,
                              v_c_w_in, v_c_lb_raw, v_c_gnorm_g, v_c_w_out, v_ln_mix_g, v_ln_mix_b, v_mlp_w1, v_mlp_w2,
                              v_ln_ffn_g, v_ln_ffn_b)))
    me = 4 * lax.axis_index("x") + 2 * lax.axis_index("y") + lax.axis_index("c")
    seq = x.shape[1]
    pad = (-(N_META + seq)) % SB_BLOCK
    lp = pad + N_META + seq
    meta_w = D_MODEL // N_DEV
    conv_w_all = 2 * GDN_HEADS * HEAD_W + GDN_HEADS * HEAD_W
    conv_w_mine = conv_w_all // N_DEV

    g_meta, g_conv, g_ab_in = _gather([w["meta_tokens"], w["ab_conv_w"][0], w["ab_w_in"][0]], [F32, F32, BF16],
                                      name="gather_weights_first")
    meta_full = g_meta.transpose(1, 0, 2).reshape(N_META, D_MODEL)
    conv_full = g_conv.transpose(1, 0, 2).reshape(CONV_K, conv_w_all)
    ab_full = g_ab_in.transpose(1, 0, 2).reshape(D_MODEL, AB_IN)
    ba0 = AB_Z + 512
    w_ab = jnp.concatenate([ab_full[:, :ba0], ab_full[:, ba0 + 2 * GDN_HEADS:], ab_full[:, ba0:ba0 + 2 * GDN_HEADS],
                            jnp.zeros((D_MODEL, AB_CAT - AB_IN), BF16)], axis=1)
    vec128 = lambda p: jnp.zeros((1, HEAD_W), F32).at[0, :GDN_HEADS].set(p.reshape(GDN_HEADS))
    wts = dict(
        w_ab=w_ab, conv_w=conv_full, alog_v=vec128(w["ab_a_log"]), dtb_v=vec128(w["ab_dt_bias"]),
        ab_gn=w["ab_gnorm_g"][0], lb=_lower_bound(w["c_lb_raw"]), c_gn=w["c_gnorm_g"][0],
        ln_mix_g=w["ln_mix_g"], ln_mix_b=w["ln_mix_b"], ln_ffn_g=w["ln_ffn_g"], ln_ffn_b=w["ln_ffn_b"])

    def later_weights(gathered):
        g_ab_out, g_c_in, g_c_out, g_w1, g_w2 = gathered
        return dict(w_out0=g_ab_out.reshape(D_MODEL, D_MODEL), w_c=g_c_in, w_out1=g_c_out.reshape(D_MODEL, D_MODEL),
                    w1=[g_w1[:, l] for l in range(DEPTH)], w2=[g_w2[:, l].reshape(D_FF, D_MODEL) for l in range(DEPTH)])

    hooks = dict(
        late_gather=([w["ab_w_out"][0], w["c_w_in"][0], w["c_w_out"][0], w["mlp_w1"], w["mlp_w2"]], [BF16] * 5),
        later_weights=later_weights)

    h0 = jnp.concatenate([jnp.zeros((pad, D_MODEL), F32), meta_full, x[0]], axis=0)
    loss_vec, dh0, g = _local_step(h0, loss_target[0], pad, wts, hooks)
    loss = lax.psum(jnp.sum(loss_vec), ("x", "y", "c"))
    grad_x = dh0[lp - seq:][None]

    _, lb_vjp = jax.vjp(_lower_bound, w["c_lb_raw"])
    rep_part = _pack_replicated(dict(
        ln_mix_g=g["ln_mix_g"], ln_mix_b=g["ln_mix_b"], ln_ffn_g=g["ln_ffn_g"], ln_ffn_b=g["ln_ffn_b"],
        c_lb_raw=lb_vjp(g["lb"])[0], ab_gnorm_g=g["ab_gn"], c_gnorm_g=g["c_gn"],
        ab_a_log=g["alog_v"][0, :GDN_HEADS], ab_dt_bias=g["dtb_v"][0, :GDN_HEADS]))
    small = jnp.concatenate([rep_part, dh0[pad:pad + N_META], g["conv_w"].reshape(-1, D_MODEL),
                             jnp.zeros((_SMALL_N - _SMALL_CONV - CONV_K * conv_w_all // D_MODEL, D_MODEL), F32)], axis=0)
    (small_all,) = _gather([small], [F32], name="gather_small_grads")
    rep_out = _adamw(_pack_replicated(w), small_all[:, :_PACK_N], _pack_replicated(mom), _pack_replicated(var),
                     name="adamw_replicated")
    meta_parts = lax.dynamic_slice_in_dim(small_all[:, _SMALL_META:_SMALL_META + N_META], me * meta_w, meta_w, axis=2)
    meta_out = _adamw(w["meta_tokens"], meta_parts, mom["meta_tokens"], var["meta_tokens"], name="adamw_meta")
    conv_parts = small_all[:, _SMALL_CONV:_SMALL_CONV + CONV_K * conv_w_all // D_MODEL].reshape(N_DEV, CONV_K, conv_w_all)
    conv_parts = lax.dynamic_slice_in_dim(conv_parts, me * conv_w_mine, conv_w_mine, axis=2)
    conv_out = _adamw(w["ab_conv_w"][0], conv_parts, mom["ab_conv_w"][0], var["ab_conv_w"][0], name="adamw_conv")

    (ab_in_parts,) = g["last_parts"]
    early = g["early_parts"]
    big = [("ab_w_in", 0, ab_in_parts), ("ab_w_out", 0, early[6]), ("mlp_w1", 0, early[4]), ("mlp_w2", 0, early[5]),
           ("c_w_in", 0, early[0]), ("c_w_out", 0, early[1]), ("mlp_w1", 1, early[2]), ("mlp_w2", 1, early[3])]
    big_out = {}
    for name, l, p in big:
        res = _adamw(w[name][l], p, mom[name][l], var[name][l], name=f"adamw_{name}{l}")
        big_out.setdefault(name, []).append(res)

    rep = [_unpack_replicated(r, w) for r in rep_out]
    outs = {}
    for name in _WEIGHTS:
        if name == "meta_tokens":
            outs[name] = list(meta_out)
        elif name == "ab_conv_w":
            outs[name] = [o[None] for o in conv_out]
        elif name in big_out:
            res = big_out[name]
            outs[name] = [o[None] for o in res[0]] if len(res) == 1 else [jnp.stack(pair) for pair in zip(*res)]
        else:
            outs[name] = [r[name] for r in rep]
    flat = [loss, grad_x]
    for kind in range(4):
        flat += [outs[name][kind] for name in _WEIGHTS]
    return tuple(flat)
```

```python
import functools
import math

import jax
import jax.numpy as jnp
from jax import lax
from jax.experimental import pallas as pl
from jax.experimental.pallas import tpu as pltpu

F32 = jnp.float32
BF16 = jnp.bfloat16
HI = lax.Precision.HIGHEST

N_DEV = 8
D_MODEL = 1024
N_META = 16
D_FF = 4096
DEPTH = 2
GDN_HEADS = 4
SB_HEADS = 8
SB_DH = 64
HG_HEADS = 8
HEAD_W = 128
CHUNK = 64
SB_BLOCK = 128
CONV_K = 4
DN_ALPHA = float((2 * DEPTH) ** 0.25)
LN_EPS = 1e-5
RMS_EPS = 1e-6
L2_EPS = 1e-6
ADAM_LR, ADAM_B1, ADAM_B2, ADAM_EPS, ADAM_WD, ADAM_STEP = 0.001, 0.9, 0.999, 1e-08, 0.01, 10

AB_QKV = 0
AB_Z = 1536
AB_SB = 2048
AB_BA = 3584
AB_CAT = 3840
AB_IN = 3592

VMEM_LIMIT = 56 * 1024 * 1024


def _cp(sem=None, **kw):
    if sem is not None:
        kw["dimension_semantics"] = sem
    return pltpu.CompilerParams(vmem_limit_bytes=VMEM_LIMIT, **kw)


def _row_tile(n, want):
    best = 8
    for t in range(8, min(n, want) + 1, 8):
        if n % t == 0:
            best = t
    return best


@jax.custom_vjp
def _sigmoid(x):
    e = jnp.exp(-jnp.abs(x))
    r = 1.0 / (1.0 + e)
    return jnp.where(x >= 0, r, e * r)


def _sigmoid_fwd(x):
    s = _sigmoid(x)
    return s, s


def _sigmoid_bwd(s, g):
    return (g * s * (1.0 - s),)


_sigmoid.defvjp(_sigmoid_fwd, _sigmoid_bwd)


def _log1p_exp_neg_abs(x):
    e = jnp.exp(-jnp.abs(x))
    return jnp.where(e < 1e-4, e - 0.5 * e * e, jnp.log(1.0 + e))


@jax.custom_vjp
def _softplus(x):
    return jnp.maximum(x, 0.0) + _log1p_exp_neg_abs(x)


def _softplus_fwd(x):
    return _softplus(x), x


def _softplus_bwd(x, g):
    return (g * _sigmoid(x),)


_softplus.defvjp(_softplus_fwd, _softplus_bwd)


def _silu(x):
    return x * _sigmoid(x)


def _silu_grad(x):
    s = _sigmoid(x)
    return s * (1.0 + x * (1.0 - s))


def _dot(a, b, dims, precision=None):
    return lax.dot_general(a, b, (dims, ((), ())), precision=precision, preferred_element_type=F32)


NN = ((1,), (0,))
NT = ((1,), (1,))
TN = ((0,), (0,))


def _bdot(a, b, dims):
    return _dot(a.astype(BF16), b.astype(BF16), dims)


def _layer_norm(pre, g, beta):
    mu = jnp.mean(pre, axis=-1, keepdims=True)
    xc = pre - mu
    var = jnp.mean(xc * xc, axis=-1, keepdims=True)
    return xc * lax.rsqrt(var + LN_EPS) * g + beta


def _mm(a, b, mode, *, tm, tn, tk, name, a_fn=None, epi=None, c=None, scale=1.0, b_dev=False, out_dev=False,
        out_dtype=F32, ln=None, scatter=()):
    if mode == "NN":
        m, kk = a.shape
        n = b.shape[2] * N_DEV if b_dev else b.shape[1]
    elif mode == "NT":
        m, kk = a.shape
        n = b.shape[1] if b_dev else b.shape[0]
    else:
        kk, m = a.shape
        n = b.shape[1]
    assert m % tm == 0 and n % tn == 0 and kk % tk == 0, (name, m, n, kk, tm, tn, tk)
    nk = kk // tk
    dims = {"NN": NN, "NT": NT, "TN": TN}[mode]

    if mode == "TN":
        a_spec = pl.BlockSpec((tk, tm), lambda i, j, k: (k, i))
    else:
        a_spec = pl.BlockSpec((tm, tk), lambda i, j, k: (i, k))
    if mode == "NN":
        if b_dev:
            assert tn == b.shape[2]
            b_spec = pl.BlockSpec((None, tk, tn), lambda i, j, k: (j, k, 0))
        else:
            b_spec = pl.BlockSpec((tk, tn), lambda i, j, k: (k, j))
    elif mode == "NT":
        if b_dev:
            assert tk == b.shape[2]
            b_spec = pl.BlockSpec((None, tn, tk), lambda i, j, k: (k, j, 0))
        else:
            b_spec = pl.BlockSpec((tn, tk), lambda i, j, k: (j, k))
    else:
        b_spec = pl.BlockSpec((tk, tn), lambda i, j, k: (k, j))
    in_specs = [a_spec, b_spec]
    operands = [a, b]
    if epi is not None:
        in_specs.append(pl.BlockSpec((tm, tn), lambda i, j, k: (i, j)))
        operands.append(c)
    if epi == "ln":
        assert tn == n and not out_dev
        in_specs += [pl.BlockSpec((1, n), lambda i, j, k: (0, 0))] * 2
        operands += [ln[0].reshape(1, n), ln[1].reshape(1, n)]
    if out_dev:
        assert tn == n // N_DEV
        out_shape = jax.ShapeDtypeStruct((N_DEV, m, tn), out_dtype)
        out_spec = pl.BlockSpec((None, tm, tn), lambda i, j, k: (j, i, 0))
    else:
        out_shape = jax.ShapeDtypeStruct((m, n), out_dtype)
        out_spec = pl.BlockSpec((tm, tn), lambda i, j, k: (i, j))
    if epi == "ln":
        out_shape = [out_shape, out_shape, jax.ShapeDtypeStruct((m, n), BF16)]
        out_spec = [out_spec] * 3
    n_out = 3 if epi == "ln" else 1
    ns = len(scatter)
    if ns:
        in_specs += [_ANY] * ns
        operands += list(scatter)
        out_shape = (out_shape if n_out > 1 else [out_shape]) + [jax.ShapeDtypeStruct(s.shape, s.dtype) for s in scatter]
        out_spec = (out_spec if n_out > 1 else [out_spec]) + [_ANY] * ns
    n_in = len(operands)
    grid = (m // tm, n // tn, nk)

    def body(*refs):
        a_ref, b_ref = refs[0], refs[1]
        c_ref = refs[2] if epi is not None else None
        o_ref = refs[n_in]
        scratch0 = n_in + n_out + ns
        acc_ref = refs[scratch0] if nk > 1 else None
        if ns:
            s_start, s_finish = _scatter_phases(refs[n_in - ns:n_in], refs[n_in + n_out:scratch0],
                                                *refs[scratch0 + (1 if nk > 1 else 0):])
            at = lambda step: functools.reduce(lambda x, y: x & y, [pl.program_id(ax) == step[ax] for ax in range(3)])
            pl.when(at((0, 0, 0)))(s_start)
        av = a_ref[...]
        if a_fn == "relu2":
            av = jnp.square(jnp.maximum(av, 0.0))
        p = _dot(av.astype(BF16), b_ref[...].astype(BF16), dims)

        def finish(acc):
            if epi == "add":
                acc = acc + scale * c_ref[...]
            elif epi == "relu2grad":
                acc = acc * (2.0 * jnp.maximum(c_ref[...], 0.0))
            elif epi == "ln":
                acc = acc + scale * c_ref[...]
                y = _layer_norm(acc, refs[3][...], refs[4][...])
                refs[n_in + 1][...] = y
                refs[n_in + 2][...] = y.astype(BF16)
            o_ref[...] = acc.astype(out_dtype)

        if nk == 1:
            finish(p)
        else:
            k = pl.program_id(2)

            @pl.when(k == 0)
            def _():
                acc_ref[...] = p

            @pl.when(k > 0)
            def _():
                acc_ref[...] += p

            @pl.when(k == nk - 1)
            def _():
                finish(acc_ref[...])

        if ns:
            pl.when(at(tuple(g - 1 for g in grid)))(s_finish)

    res = pl.pallas_call(
        body, name=name, grid=grid, in_specs=in_specs, out_specs=out_spec, out_shape=out_shape,
        scratch_shapes=([pltpu.VMEM((tm, tn), F32)] if nk > 1 else []) + (_scatter_scratch(ns) if ns else []),
        compiler_params=_cp(("arbitrary",) * 3 if ns else ("parallel", "parallel", "arbitrary"),
                            has_side_effects=bool(ns)),
    )(*operands)
    return res


def _ln_bwd(pre, g, dy, *, name):
    lp, d = pre.shape
    tm = _row_tile(lp, 512)

    def body(pre_ref, g_ref, dy_ref, dpre_ref, dpreb_ref, dg_ref, db_ref):
        pre = pre_ref[...]
        mu = jnp.mean(pre, axis=-1, keepdims=True)
        xc = pre - mu
        var = jnp.mean(xc * xc, axis=-1, keepdims=True)
        rstd = lax.rsqrt(var + LN_EPS)
        xhat = xc * rstd
        dyv = dy_ref[...]
        dxh = dyv * g_ref[...]
        m1 = jnp.mean(dxh, axis=-1, keepdims=True)
        m2 = jnp.mean(dxh * xhat, axis=-1, keepdims=True)
        dpre = rstd * (dxh - m1 - xhat * m2)
        dpre_ref[...] = dpre
        dpreb_ref[...] = dpre.astype(BF16)

        @pl.when(pl.program_id(0) == 0)
        def _():
            dg_ref[...] = jnp.zeros_like(dg_ref)
            db_ref[...] = jnp.zeros_like(db_ref)

        dg_ref[...] += jnp.sum(dyv * xhat, axis=0, keepdims=True)
        db_ref[...] += jnp.sum(dyv, axis=0, keepdims=True)

    row = pl.BlockSpec((tm, d), lambda i: (i, 0))
    vec = pl.BlockSpec((1, d), lambda i: (0, 0))
    return pl.pallas_call(
        body, name=name, grid=(lp // tm,), in_specs=[row, vec, row], out_specs=[row, row, vec, vec],
        out_shape=[jax.ShapeDtypeStruct((lp, d), F32), jax.ShapeDtypeStruct((lp, d), BF16),
                   jax.ShapeDtypeStruct((1, d), F32), jax.ShapeDtypeStruct((1, d), F32)],
        compiler_params=_cp(("arbitrary",)),
    )(pre, g.reshape(1, d), dy)


def _loss_head(y, target, *, name):
    lp, d = y.shape
    seq = target.shape[0]
    tm = SB_BLOCK
    first = (lp - seq) // tm
    assert (lp - seq) % tm == 0 and seq % tm == 0

    def body(y_ref, t_ref, dy_ref, loss_ref):
        i = pl.program_id(0)
        live = i >= first
        diff = jnp.where(live, y_ref[...] - t_ref[...], 0.0)
        dy_ref[...] = diff * (1.0 / d)

        @pl.when(i == 0)
        def _():
            loss_ref[...] = jnp.zeros_like(loss_ref)

        loss_ref[...] += jnp.sum(diff * diff, axis=0, keepdims=True) * (0.5 / d)

    return pl.pallas_call(
        body, name=name, grid=(lp // tm,),
        in_specs=[pl.BlockSpec((tm, d), lambda i: (i, 0)),
                  pl.BlockSpec((tm, d), lambda i: (jnp.maximum(i - first, 0), 0))],
        out_specs=[pl.BlockSpec((tm, d), lambda i: (i, 0)), pl.BlockSpec((1, d), lambda i: (0, 0))],
        out_shape=[jax.ShapeDtypeStruct((lp, d), F32), jax.ShapeDtypeStruct((1, d), F32)],
        compiler_params=_cp(("arbitrary",)),
    )(y, target)


def _gate_fwd(o, zsrc, z_blk0, g, other, *, heads, name):
    lp = o.shape[0]
    tm = _row_tile(lp, 512)
    w = heads * HEAD_W
    assert (z_blk0 * HEAD_W) % w == 0
    has_other = w < D_MODEL

    def body(o_ref, z_ref, g_ref, *rest):
        y_ref = rest[-1]
        gv = g_ref[...]
        for h in range(heads):
            cs = slice(h * HEAD_W, (h + 1) * HEAD_W)
            ov = o_ref[:, cs]
            r = lax.rsqrt(jnp.mean(ov * ov, axis=-1, keepdims=True) + RMS_EPS)
            y_ref[:, cs] = (ov * r * gv * _silu(z_ref[:, cs])).astype(BF16)
        if has_other:
            y_ref[:, w:] = rest[0][...].astype(BF16)

    row = lambda width, blk: pl.BlockSpec((tm, width), lambda i: (i, blk))
    return pl.pallas_call(
        body, name=name, grid=(lp // tm,),
        in_specs=[row(w, 0), row(w, z_blk0 * HEAD_W // w), pl.BlockSpec((1, HEAD_W), lambda i: (0, 0))]
        + ([row(D_MODEL - w, 0)] if has_other else []),
        out_specs=row(D_MODEL, 0), out_shape=jax.ShapeDtypeStruct((lp, D_MODEL), BF16),
        compiler_params=_cp(("parallel",)),
    )(o, zsrc, g.reshape(1, HEAD_W), *([other] if has_other else []))


def _gate_bwd(o, zsrc, z_blk0, g, dy, *, heads, name):
    lp = o.shape[0]
    tm = _row_tile(lp, 512)

    w = heads * HEAD_W
    assert (z_blk0 * HEAD_W) % w == 0

    def body(o_ref, z_ref, g_ref, dy_ref, do_ref, dz_ref, dg_ref):
        @pl.when(pl.program_id(0) == 0)
        def _():
            dg_ref[...] = jnp.zeros_like(dg_ref)

        gv = g_ref[...]
        dg = jnp.zeros((1, HEAD_W), F32)
        for h in range(heads):
            cs = slice(h * HEAD_W, (h + 1) * HEAD_W)
            ov, zv, dyv = o_ref[:, cs], z_ref[:, cs], dy_ref[:, cs]
            r = lax.rsqrt(jnp.mean(ov * ov, axis=-1, keepdims=True) + RMS_EPS)
            nrm = ov * r
            s = _silu(zv)
            dn = dyv * gv * s
            do_ref[:, cs] = r * (dn - nrm * jnp.mean(dn * nrm, axis=-1, keepdims=True))
            dz_ref[:, cs] = dyv * nrm * gv * _silu_grad(zv)
            dg = dg + jnp.sum(dyv * nrm * s, axis=0, keepdims=True)
        dg_ref[...] += dg

    row = lambda blk: pl.BlockSpec((tm, w), lambda i: (i, blk))
    vec = pl.BlockSpec((1, HEAD_W), lambda i: (0, 0))
    return pl.pallas_call(
        body, name=name, grid=(lp // tm,),
        in_specs=[row(0), row(z_blk0 * HEAD_W // w), vec, row(0)], out_specs=[row(0), row(0), vec],
        out_shape=[jax.ShapeDtypeStruct((lp, w), F32), jax.ShapeDtypeStruct((lp, w), F32),
                   jax.ShapeDtypeStruct((1, HEAD_W), F32)],
        compiler_params=_cp(("arbitrary",)),
    )(o, zsrc, g.reshape(1, HEAD_W), dy)


def _conv_taps(x, w):
    acc = w[CONV_K - 1:CONV_K, :] * x
    for k in range(CONV_K - 1):
        acc = acc + w[k:k + 1, :] * pltpu.roll(x, CONV_K - 1 - k, 0)
    return acc


def _gdn_pre_fwd(p0, conv_w, pad, *, name):
    lp = p0.shape[0]
    nq = GDN_HEADS
    qscale = HEAD_W ** -0.5

    def body(x_ref, w_ref, y_ref):
        j = pl.program_id(0)
        c = _conv_taps(x_ref[...], w_ref[...])
        s = _silu(c)
        r = lax.rsqrt(jnp.sum(s * s, axis=-1, keepdims=True) + L2_EPS)
        mult = jnp.where(j < nq, r * qscale, jnp.where(j < 2 * nq, r, 1.0))
        rows = lax.broadcasted_iota(jnp.int32, (lp, 1), 0)
        y_ref[...] = jnp.where(rows >= pad, s * mult, 0.0)

    return pl.pallas_call(
        body, name=name, grid=(3 * nq,),
        in_specs=[pl.BlockSpec((lp, HEAD_W), lambda j: (0, j)), pl.BlockSpec((CONV_K, HEAD_W), lambda j: (0, j))],
        out_specs=pl.BlockSpec((lp, HEAD_W), lambda j: (0, j)),
        out_shape=jax.ShapeDtypeStruct((lp, 3 * nq * HEAD_W), F32), compiler_params=_cp(("parallel",)),
    )(p0, conv_w)


def _gdn_pre_bwd(p0, conv_w, dqkv, pad, *, name):
    lp = p0.shape[0]
    nq = GDN_HEADS
    qscale = HEAD_W ** -0.5

    def body(x_ref, w_ref, dy_ref, dx_ref, dw_ref):
        j = pl.program_id(0)
        x, w = x_ref[...], w_ref[...]
        c = _conv_taps(x, w)
        s = _silu(c)
        r = lax.rsqrt(jnp.sum(s * s, axis=-1, keepdims=True) + L2_EPS)
        rows = lax.broadcasted_iota(jnp.int32, (lp, 1), 0)
        dy = jnp.where(rows >= pad, dy_ref[...], 0.0)
        nrm = s * r
        dn = dy * jnp.where(j < nq, qscale, 1.0)
        ds_norm = r * (dn - nrm * jnp.sum(nrm * dn, axis=-1, keepdims=True))
        ds = jnp.where(j < 2 * nq, ds_norm, dy)
        dc = ds * _silu_grad(c)
        dx = w[CONV_K - 1:CONV_K, :] * dc
        dws = [None] * CONV_K
        dws[CONV_K - 1] = jnp.sum(dc * x, axis=0, keepdims=True)
        for k in range(CONV_K - 1):
            sh = CONV_K - 1 - k
            dx = dx + w[k:k + 1, :] * pltpu.roll(dc, lp - sh, 0)
            dws[k] = jnp.sum(dc * pltpu.roll(x, sh, 0), axis=0, keepdims=True)
        dx_ref[...] = dx
        dw_ref[...] = jnp.concatenate(dws, axis=0)

    blk = pl.BlockSpec((lp, HEAD_W), lambda j: (0, j))
    wblk = pl.BlockSpec((CONV_K, HEAD_W), lambda j: (0, j))
    return pl.pallas_call(
        body, name=name, grid=(3 * nq,), in_specs=[blk, wblk, blk], out_specs=[blk, wblk],
        out_shape=[jax.ShapeDtypeStruct((lp, 3 * nq * HEAD_W), F32),
                   jax.ShapeDtypeStruct((CONV_K, 3 * nq * HEAD_W), F32)],
        compiler_params=_cp(("parallel",)),
    )(p0, conv_w, dqkv)


def _tri(c, strict):
    r = lax.broadcasted_iota(jnp.int32, (c, c), 0)
    q = lax.broadcasted_iota(jnp.int32, (c, c), 1)
    return (q < r) if strict else (q <= r)


@jax.custom_vjp
def _inv_unit_lower(m):
    c = m.shape[0]
    eye = (lax.broadcasted_iota(jnp.int32, (c, c), 0) == lax.broadcasted_iota(jnp.int32, (c, c), 1)).astype(F32)
    x = eye - m
    p = m
    n = 2
    while n < CHUNK:
        p = _bdot(p, p, NN)
        x = x + _bdot(x, p, NN)
        n *= 2
    return x


def _inv_fwd(m):
    t = _inv_unit_lower(m)
    return t, t


def _inv_bwd(t, g):
    return (-_bdot(_bdot(t, g, TN), t, NT),)


_inv_unit_lower.defvjp(_inv_fwd, _inv_bwd)


def _heads_to_rows(x, nh):
    return jnp.concatenate([x[:, h * HEAD_W:(h + 1) * HEAD_W] for h in range(nh)], axis=0)


def _rows_to_heads(x, nh):
    c = x.shape[0] // nh
    return jnp.concatenate([x[h * c:(h + 1) * c] for h in range(nh)], axis=1)


def _gdn_chunk(q, k, v, ba, alog, dtb, states, valid):
    nh = GDN_HEADS
    c = q.shape[0]
    r = nh * c
    lane = lax.broadcasted_iota(jnp.int32, (1, HEAD_W), 1)
    pick = lambda x, l: jnp.sum(jnp.where(lane == l, x, 0.0), axis=-1, keepdims=True)
    beta = jnp.concatenate([jnp.where(valid, _sigmoid(pick(ba, h)), 0.0) for h in range(nh)], axis=0)
    g = jnp.concatenate(
        [jnp.where(valid, -jnp.exp(pick(alog, h)) * _softplus(pick(ba, nh + h) + pick(dtb, h)), 0.0) for h in range(nh)],
        axis=0)
    qs, ks, vs = _heads_to_rows(q, nh), _heads_to_rows(k, nh), _heads_to_rows(v, nh)
    rr = lax.broadcasted_iota(jnp.int32, (r, r), 0)
    cc = lax.broadcasted_iota(jnp.int32, (r, r), 1)
    same = (rr // c) == (cc // c)
    causal, strict = same & (cc <= rr), same & (cc < rr)
    lower = jnp.where(causal, 1.0, 0.0).astype(BF16)
    upper = jnp.where(same & (cc >= rr), 1.0, 0.0).astype(BF16)
    gcb = _mask_mm(lower, upper, g * jnp.ones((1, HEAD_W), F32))
    gc_col = jnp.concatenate([gcb] * (r // HEAD_W), axis=1)
    decay = jnp.where(causal, jnp.exp(jnp.minimum(gc_col - gc_col.T, 0.0)), 0.0)
    egc = jnp.exp(gcb)
    kb = ks * beta
    m = jnp.where(strict, _bdot(kb, ks, NT) * decay, 0.0)
    t = _inv_unit_lower(m)
    u = _bdot(t, vs * beta, NN)
    w = _bdot(t, kb * egc, NN)
    a = _bdot(qs, ks, NT) * decay
    rows = lambda x, h: x[h * c:(h + 1) * c]
    qe = qs * egc
    v_new = u - jnp.concatenate([_bdot(rows(w, h), states[h], NN) for h in range(nh)], axis=0)
    o = jnp.concatenate([_bdot(rows(qe, h), states[h], NN) for h in range(nh)], axis=0) + _bdot(a, v_new, NN)
    new_states = []
    for h in range(nh):
        gl = gcb[(h + 1) * c - 1:(h + 1) * c, :]
        k_dec = rows(ks, h) * jnp.exp(gl - rows(gcb, h))
        new_states.append(states[h] * jnp.exp(gl) + _bdot(k_dec, rows(v_new, h), TN))
    return _rows_to_heads(o, nh), new_states


def _gdn_fwd(qkv, p0, alog_v, dtb_v, pad, *, name, gather=None):
    lp = qkv.shape[0]
    n = lp // CHUNK
    nh = GDN_HEADS
    g_srcs, g_dtypes = gather if gather is not None else ([], [])
    ng_arr = len(g_srcs)

    def body(q_ref, k_ref, v_ref, ba_ref, al_ref, dt_ref, *rest):
        g_ins, (o_ref, st_ref) = rest[:ng_arr], rest[ng_arr:ng_arr + 2]
        g_outs, s_ref, g_scratch = rest[ng_arr + 2:2 * ng_arr + 2], rest[2 * ng_arr + 2], rest[2 * ng_arr + 3:]
        i = pl.program_id(0)
        if ng_arr:
            g_start, g_forward, g_finish = _gather_phases(g_ins, g_outs, g_scratch[:ng_arr], *g_scratch[ng_arr:],
                                                          g_dtypes)
            pl.when(i == 0)(g_start)
            pl.when(i == (3 * n) // 4)(g_forward)

        @pl.when(i == 0)
        def _():
            s_ref[...] = jnp.zeros_like(s_ref)

        valid = (i * CHUNK + lax.broadcasted_iota(jnp.int32, (CHUNK, 1), 0)) >= pad
        s = s_ref[...]
        o, s2 = _gdn_chunk(q_ref[...], k_ref[...], v_ref[...], ba_ref[...], al_ref[...], dt_ref[...],
                           [s[h] for h in range(nh)], valid)
        st_ref[...] = s
        o_ref[...] = o
        for h in range(nh):
            s_ref[h] = s2[h]
        if ng_arr:
            pl.when(i == n - 1)(g_finish)

    w = nh * HEAD_W
    vec = pl.BlockSpec((1, HEAD_W), lambda i: (0, 0))
    return pl.pallas_call(
        body, name=name, grid=(n,),
        in_specs=[pl.BlockSpec((CHUNK, w), lambda i: (i, 0)), pl.BlockSpec((CHUNK, w), lambda i: (i, 1)),
                  pl.BlockSpec((CHUNK, w), lambda i: (i, 2)), pl.BlockSpec((CHUNK, HEAD_W), lambda i: (i, AB_BA // HEAD_W)),
                  vec, vec] + [pl.BlockSpec(memory_space=pltpu.VMEM)] * ng_arr,
        out_specs=[pl.BlockSpec((CHUNK, w), lambda i: (i, 0)),
                   pl.BlockSpec((None, nh, HEAD_W, HEAD_W), lambda i: (i, 0, 0, 0))] + [_ANY] * ng_arr,
        out_shape=[jax.ShapeDtypeStruct((lp, w), F32), jax.ShapeDtypeStruct((n, nh, HEAD_W, HEAD_W), F32)]
        + _gather_out_shapes(g_srcs, g_dtypes),
        scratch_shapes=[pltpu.VMEM((nh, HEAD_W, HEAD_W), F32)] + (_gather_scratch(g_srcs, g_dtypes) if ng_arr else []),
        compiler_params=_cp(("arbitrary",), has_side_effects=bool(ng_arr)),
    )(qkv, qkv, qkv, p0, alog_v, dtb_v, *g_srcs)


def _gdn_bwd(qkv, p0, alog_v, dtb_v, states, do, pad, *, name, scatter=()):
    lp = qkv.shape[0]
    n = lp // CHUNK
    nh = GDN_HEADS
    ns = len(scatter)

    def body(q_ref, k_ref, v_ref, ba_ref, al_ref, dt_ref, st_ref, do_ref, *rest):
        s_ins, (dq_ref, dk_ref, dv_ref, dba_ref, dal_ref, ddt_ref) = rest[:ns], rest[ns:ns + 6]
        s_outs, ds_ref, s_sems = rest[ns + 6:2 * ns + 6], rest[2 * ns + 6], rest[2 * ns + 7:]
        step = pl.program_id(0)
        i = n - 1 - step
        if ns:
            s_start, s_finish = _scatter_phases(s_ins, s_outs, *s_sems)
            pl.when(step == 0)(s_start)

        @pl.when(step == 0)
        def _():
            ds_ref[...] = jnp.zeros_like(ds_ref)
            dal_ref[...] = jnp.zeros_like(dal_ref)
            ddt_ref[...] = jnp.zeros_like(ddt_ref)

        valid = (i * CHUNK + lax.broadcasted_iota(jnp.int32, (CHUNK, 1), 0)) >= pad
        st, dst = st_ref[...], ds_ref[...]
        fn = functools.partial(_gdn_chunk, valid=valid)
        _, vjp = jax.vjp(fn, q_ref[...], k_ref[...], v_ref[...], ba_ref[...], al_ref[...], dt_ref[...],
                         [st[h] for h in range(nh)])
        dq, dk, dv, dba, dal, ddt, ds = vjp((do_ref[...], [dst[h] for h in range(nh)]))
        dq_ref[...] = dq
        dk_ref[...] = dk
        dv_ref[...] = dv
        dba_ref[...] = dba
        dal_ref[...] += dal
        ddt_ref[...] += ddt
        for h in range(nh):
            ds_ref[h] = ds[h]
        if ns:
            pl.when(step == n - 1)(s_finish)

    w = nh * HEAD_W
    rev = lambda c: (lambda s: (n - 1 - s, c))
    vec = pl.BlockSpec((1, HEAD_W), lambda s: (0, 0))
    return pl.pallas_call(
        body, name=name, grid=(n,),
        in_specs=[pl.BlockSpec((CHUNK, w), rev(0)), pl.BlockSpec((CHUNK, w), rev(1)), pl.BlockSpec((CHUNK, w), rev(2)),
                  pl.BlockSpec((CHUNK, HEAD_W), rev(AB_BA // HEAD_W)), vec, vec,
                  pl.BlockSpec((None, nh, HEAD_W, HEAD_W), lambda s: (n - 1 - s, 0, 0, 0)),
                  pl.BlockSpec((CHUNK, w), rev(0))] + [_ANY] * ns,
        out_specs=[pl.BlockSpec((CHUNK, w), rev(0)), pl.BlockSpec((CHUNK, w), rev(0)), pl.BlockSpec((CHUNK, w), rev(0)),
                   pl.BlockSpec((CHUNK, HEAD_W), rev(0)), vec, vec] + [_ANY] * ns,
        out_shape=[jax.ShapeDtypeStruct((lp, w), F32)] * 3 + [jax.ShapeDtypeStruct((lp, HEAD_W), F32)]
        + [jax.ShapeDtypeStruct((1, HEAD_W), F32)] * 2 + [jax.ShapeDtypeStruct(s.shape, s.dtype) for s in scatter],
        scratch_shapes=[pltpu.VMEM((nh, HEAD_W, HEAD_W), F32)] + (_scatter_scratch(ns) if ns else []),
        compiler_params=_cp(("arbitrary",), has_side_effects=bool(ns)),
    )(qkv, qkv, qkv, p0, alog_v, dtb_v, states, do, *scatter)


HG_LEVELS = (32, 16, 8, 4, 2, 1)
HG_GROUP = 4


def _hg_masks():
    import numpy as np
    c = CHUNK
    t = np.arange(c)[:, None]
    j = np.arange(c)[None, :]
    sums = (j <= t).astype(np.float32)
    pairs = [j == t]
    for m in HG_LEVELS:
        p = (t // (2 * m)) * (2 * m)
        r = p + m
        pairs.append((t >= r) & (j < r) & (j >= p))
    pairs = np.concatenate([np.kron(np.eye(HG_GROUP), p) for p in pairs], axis=0).astype(np.float32)
    return jnp.asarray(sums, BF16), jnp.asarray(sums.T, BF16), jnp.asarray(pairs, F32)


def _hg_level_row(b, m):
    c, w = b.shape
    if m >= 8:
        return jnp.concatenate([jnp.broadcast_to(b[p + m:p + m + 1], (2 * m, w)) for p in range(0, c, 2 * m)], axis=0)
    tiles = b.reshape(c // 8, 8, w)
    sub = lax.broadcasted_iota(jnp.int32, (1, 8, 1), 1)
    out = None
    for r0 in range(m, 8, 2 * m):
        cand = jnp.broadcast_to(tiles[:, r0:r0 + 1, :], tiles.shape)
        out = cand if out is None else jnp.where(sub >= r0 - m, cand, out)
    return out.reshape(c, w)


def _split3(x):
    hi = x.astype(BF16)
    r1 = x - hi.astype(F32)
    mid = r1.astype(BF16)
    return hi, mid, (r1 - mid.astype(F32)).astype(BF16)


def _mask_mm_raw(m, x):
    return sum(_dot(m, part, NN) for part in _split3(x))


@jax.custom_vjp
def _mask_mm(m, mt, x):
    return _mask_mm_raw(m, x)


def _mask_mm_fwd(m, mt, x):
    return _mask_mm_raw(m, x), (m, mt)


def _mask_mm_bwd(res, g):
    m, mt = res
    return jnp.zeros_like(m), jnp.zeros_like(mt), _mask_mm_raw(mt, g)


_mask_mm.defvjp(_mask_mm_fwd, _mask_mm_bwd)


def _hg_chunk(qr, fr, ir, lb, states, valid, sums, sums_t, pairs):
    nh = HG_GROUP
    c = qr.shape[0]
    r = nh * c
    fg = lb + (1.0 - lb) * _sigmoid(fr)
    logf = jnp.where(valid, jnp.log(fg), 0.0)
    k = jnp.where(valid, 1.0 - fg, 0.0)
    qs = jnp.where(valid, _silu(qr), 0.0)
    v = jnp.where(valid, ir, 0.0)
    b = _mask_mm(sums, sums_t, logf)
    mask = lambda n: pairs[n * r:(n + 1) * r]
    stack = lambda x: _heads_to_rows(x, nh)
    a = mask(0) * _bdot(stack(qs), stack(k), NT)
    for lvl, m in enumerate(HG_LEVELS):
        d = b - _hg_level_row(b, m)
        a = a + mask(1 + lvl) * _bdot(stack(qs * jnp.exp(jnp.minimum(d, 0.0))),
                                      stack(k * jnp.exp(jnp.minimum(-d, 0.0))), NT)
    av = _bdot(a, stack(v), NN)
    eb = jnp.exp(b)
    qe, kd = qs * eb, k * jnp.exp(b[c - 1:c] - b)
    outs, new_states = [], []
    for h in range(nh):
        cs = slice(h * HEAD_W, (h + 1) * HEAD_W)
        outs.append(_bdot(qe[:, cs], states[h], NT) + av[h * c:(h + 1) * c])
        new_states.append(states[h] * eb[c - 1:c, cs] + _bdot(v[:, cs], kd[:, cs], TN))
    return jnp.concatenate(outs, axis=1), new_states


def _hg_fwd(p1, lb, pad, *, name):
    lp = p1.shape[0]
    n = lp // CHUNK
    nh = HG_HEADS

    def body(q_ref, f_ref, i_ref, lb_ref, sums_ref, sums_t_ref, pairs_ref, o_ref, st_ref, s_ref):
        i = pl.program_id(1)

        @pl.when(i == 0)
        def _():
            s_ref[...] = jnp.zeros_like(s_ref)

        valid = (i * CHUNK + lax.broadcasted_iota(jnp.int32, (CHUNK, 1), 0)) >= pad
        s = s_ref[...]
        o, s2 = _hg_chunk(q_ref[...], f_ref[...], i_ref[...], lb_ref[...], [s[h] for h in range(grp)], valid,
                          sums_ref[...], sums_t_ref[...], pairs_ref[...])
        st_ref[...] = s
        o_ref[...] = o
        for h in range(grp):
            s_ref[h] = s2[h]

    masks = _hg_masks()
    grp, ngrp, gw = HG_GROUP, nh // HG_GROUP, HG_GROUP * HEAD_W
    blk = lambda off: pl.BlockSpec((CHUNK, gw), lambda h, i: (i, off + h))
    const = lambda a: pl.BlockSpec(a.shape, lambda h, i: (0, 0))
    return pl.pallas_call(
        body, name=name, grid=(ngrp, n),
        in_specs=[blk(0), blk(ngrp), blk(2 * ngrp), pl.BlockSpec((1, gw), lambda h, i: (0, h))]
        + [const(a) for a in masks],
        out_specs=[blk(0), pl.BlockSpec((grp, None, HEAD_W, HEAD_W), lambda h, i: (h, i, 0, 0))],
        out_shape=[jax.ShapeDtypeStruct((lp, nh * HEAD_W), F32), jax.ShapeDtypeStruct((nh, n, HEAD_W, HEAD_W), F32)],
        scratch_shapes=[pltpu.VMEM((grp, HEAD_W, HEAD_W), F32)],
        compiler_params=_cp(("parallel", "arbitrary")),
    )(p1, p1, p1, lb, *masks)


def _hg_bwd(p1, lb, states, do, pad, *, name, scatter=()):
    lp = p1.shape[0]
    n = lp // CHUNK
    nh = HG_HEADS
    ns = len(scatter)

    def body(q_ref, f_ref, i_ref, lb_ref, st_ref, do_ref, sums_ref, sums_t_ref, pairs_ref, *rest):
        s_ins, (dq_ref, df_ref, di_ref, dlb_ref) = rest[:ns], rest[ns:ns + 4]
        s_outs, ds_ref, s_sems = rest[ns + 4:2 * ns + 4], rest[2 * ns + 4], rest[2 * ns + 5:]
        step = pl.program_id(1)
        i = n - 1 - step
        if ns:
            s_start, s_finish = _scatter_phases(s_ins, s_outs, *s_sems)
            pl.when((pl.program_id(0) == 0) & (step == 0))(s_start)

        @pl.when(step == 0)
        def _():
            ds_ref[...] = jnp.zeros_like(ds_ref)
            dlb_ref[...] = jnp.zeros_like(dlb_ref)

        valid = (i * CHUNK + lax.broadcasted_iota(jnp.int32, (CHUNK, 1), 0)) >= pad
        fn = functools.partial(_hg_chunk, valid=valid, sums=sums_ref[...], sums_t=sums_t_ref[...],
                               pairs=pairs_ref[...])
        st, dst = st_ref[...], ds_ref[...]
        _, vjp = jax.vjp(fn, q_ref[...], f_ref[...], i_ref[...], lb_ref[...], [st[h] for h in range(grp)])
        dq, df, di, dlb, ds = vjp((do_ref[...], [dst[h] for h in range(grp)]))
        dq_ref[...] = dq
        df_ref[...] = df
        di_ref[...] = di
        dlb_ref[...] += dlb
        for h in range(grp):
            ds_ref[h] = ds[h]
        if ns:
            pl.when((pl.program_id(0) == ngrp - 1) & (step == n - 1))(s_finish)

    masks = _hg_masks()
    grp, ngrp, gw = HG_GROUP, nh // HG_GROUP, HG_GROUP * HEAD_W
    blk = lambda off: pl.BlockSpec((CHUNK, gw), lambda h, s: (n - 1 - s, off + h))
    const = lambda a: pl.BlockSpec(a.shape, lambda h, s: (0, 0))
    w = nh * HEAD_W
    return pl.pallas_call(
        body, name=name, grid=(ngrp, n),
        in_specs=[blk(0), blk(ngrp), blk(2 * ngrp), pl.BlockSpec((1, gw), lambda h, s: (0, h)),
                  pl.BlockSpec((grp, None, HEAD_W, HEAD_W), lambda h, s: (h, n - 1 - s, 0, 0)), blk(0)]
        + [const(a) for a in masks] + [_ANY] * ns,
        out_specs=[blk(0), blk(0), blk(0), pl.BlockSpec((1, gw), lambda h, s: (0, h))] + [_ANY] * ns,
        out_shape=[jax.ShapeDtypeStruct((lp, w), F32)] * 3 + [jax.ShapeDtypeStruct((1, w), F32)]
        + [jax.ShapeDtypeStruct(s.shape, s.dtype) for s in scatter],
        scratch_shapes=[pltpu.VMEM((grp, HEAD_W, HEAD_W), F32)] + (_scatter_scratch(ns) if ns else []),
        compiler_params=_cp(("arbitrary", "arbitrary"), has_side_effects=bool(ns)),
    )(p1, p1, p1, lb, states, do, *masks, *scatter)


SB_GROUP = 4
SB_FAR = -110.0


def _sb_cat(kind, first_key=0):
    r = lax.broadcasted_iota(jnp.int32, (SB_BLOCK, 2 * SB_BLOCK), 0)
    c = lax.broadcasted_iota(jnp.int32, (SB_BLOCK, 2 * SB_BLOCK), 1)
    tri = {"after": c < r, "incl": r <= c, "before": r < c}[kind]
    m = ((c >= SB_BLOCK) | tri) & (r >= first_key)
    return jnp.where(m, 1.0, 0.0).astype(BF16)


def _sb_cumsum(x, cat):
    return _dot(x.astype(BF16), cat, NN)


def _sb_logsig(z):
    e = jnp.exp(-jnp.abs(z))
    lse = jnp.where(e < 1e-4, e, jnp.log(1.0 + e))
    lsz = jnp.minimum(z, 0.0) - lse
    return lsz, lsz - z, e


def _sb_stack(x, scale=None):
    lane = lax.broadcasted_iota(jnp.int32, (1, HEAD_W), 1)
    if scale is not None:
        x = x * scale
    return jnp.concatenate([jnp.where(lane < SB_DH, x, 0.0), jnp.where(lane >= SB_DH, x, 0.0)], axis=0).astype(BF16)


def _sb_unstack(x):
    lane = lax.broadcasted_iota(jnp.int32, (1, HEAD_W), 1)
    return jnp.where(lane < SB_DH, x[:SB_BLOCK], x[SB_BLOCK:])


def _sb_fwd(p0, pad, *, name, gather=None):
    lp = p0.shape[0]
    nb = lp // SB_BLOCK
    npair = SB_HEADS // 2
    blk0 = AB_SB // HEAD_W
    scale = SB_DH ** -0.5
    gw = SB_GROUP * SB_BLOCK
    assert pad < SB_BLOCK
    g_srcs, g_dtypes = gather if gather is not None else ([], [])
    ng_arr = len(g_srcs)

    def body(q_ref, k_ref, v_ref, *rest):
        g_ins, (o_ref, tot_ref, nproc_ref) = rest[:ng_arr], rest[ng_arr:ng_arr + 3]
        g_outs, g_scratch = rest[ng_arr + 3:2 * ng_arr + 3], rest[2 * ng_arr + 3:]
        first_step = (pl.program_id(0) == 0) & (pl.program_id(1) == 0)
        last_pair = pl.program_id(0) == npair - 1
        if ng_arr:
            g_start, g_forward, g_finish = _gather_phases(g_ins, g_outs, g_scratch[:ng_arr], *g_scratch[ng_arr:],
                                                          g_dtypes)
            pl.when(first_step)(g_start)
            pl.when(last_pair & (pl.program_id(1) == 0))(g_forward)
        i = pl.program_id(1)
        qs = _sb_stack(q_ref[...], scale)
        qpos = i * SB_BLOCK + lax.broadcasted_iota(jnp.int32, (SB_BLOCK, 1), 0)
        qpos = jnp.concatenate([qpos, qpos], axis=0)
        cat = _sb_cat("after")
        cat0 = _sb_cat("after", pad)
        ng = i // SB_GROUP

        def group(off, nblk, first_cat, allowed, carry):
            acc, run = carry
            kg = k_ref[pl.ds(off, nblk * SB_BLOCK), :].astype(BF16)
            vg = v_ref[pl.ds(off, nblk * SB_BLOCK), :].astype(BF16)
            lsz, l1m, _ = _sb_logsig(_dot(qs, kg, NT))
            if allowed is not None:
                l1m = jnp.where(allowed, l1m, 0.0)
            args = [None] * nblk
            for g in reversed(range(nblk)):
                sl = slice(g * SB_BLOCK, (g + 1) * SB_BLOCK)
                al = _sb_cumsum(l1m[:, sl], first_cat if g == 0 else cat)
                args[g] = lsz[:, sl] + al[:, :SB_BLOCK] + run
                run = run + al[:, SB_BLOCK:]
            wgt = jnp.exp(jnp.concatenate(args, axis=1))
            if allowed is not None:
                wgt = jnp.where(allowed, wgt, 0.0)
            return acc + _dot(wgt.astype(BF16), vg, NN), run

        def below(t, carry):
            gi = ng - 1 - t
            return group(pl.multiple_of(gi * gw, gw), SB_GROUP, jnp.where(gi == 0, cat0, cat), None, carry)

        top = ng * gw

        def top_group(nblk, carry):
            off = pl.multiple_of(jnp.minimum(top, lp - nblk * SB_BLOCK), SB_BLOCK)
            kpos = off + lax.broadcasted_iota(jnp.int32, (1, nblk * SB_BLOCK), 1)
            return group(off, nblk, cat, (kpos < qpos) & (kpos >= pad) & (kpos >= top), carry)

        zero = (jnp.zeros((2 * SB_BLOCK, HEAD_W), F32), jnp.zeros((2 * SB_BLOCK, HEAD_W), F32))
        carry = lax.cond(i - ng * SB_GROUP < SB_GROUP // 2, functools.partial(top_group, SB_GROUP // 2),
                         functools.partial(top_group, SB_GROUP), zero)
        used, acc, run = lax.while_loop(lambda s: (s[0] < ng) & (jnp.max(s[2]) > SB_FAR),
                                        lambda s: (s[0] + 1, *below(s[0], (s[1], s[2]))), (jnp.int32(0), *carry))
        o_ref[...] = _sb_unstack(acc)
        tot_ref[...] = _sb_unstack(run)
        nproc_ref[pl.program_id(0), i] = used.astype(F32)
        if ng_arr:
            pl.when(last_pair & (pl.program_id(1) == nb - 1))(g_finish)

    full = lambda c0: pl.BlockSpec((lp, HEAD_W), lambda p, i: (0, c0 + p))
    out = pl.BlockSpec((SB_BLOCK, HEAD_W), lambda p, i: (i, p))
    return pl.pallas_call(
        body, name=name, grid=(npair, nb),
        in_specs=[pl.BlockSpec((SB_BLOCK, HEAD_W), lambda p, i: (i, blk0 + p)), full(blk0 + npair), full(blk0 + 2 * npair)]
        + [pl.BlockSpec(memory_space=pltpu.VMEM)] * ng_arr,
        out_specs=[out, out, pl.BlockSpec(memory_space=pltpu.SMEM)] + [_ANY] * ng_arr,
        out_shape=[jax.ShapeDtypeStruct((lp, npair * HEAD_W), F32)] * 2 + [jax.ShapeDtypeStruct((npair, nb), F32)]
        + _gather_out_shapes(g_srcs, g_dtypes),
        scratch_shapes=_gather_scratch(g_srcs, g_dtypes) if ng_arr else [],
        compiler_params=_cp(("arbitrary", "arbitrary"), has_side_effects=bool(ng_arr)),
    )(p0, p0, p0, *g_srcs)


def _sb_bwd(p0, tot, nproc, dsrc, d_blk0, pad, *, name, scatter=()):
    lp = p0.shape[0]
    nb = lp // SB_BLOCK
    npair = SB_HEADS // 2
    blk0 = AB_SB // HEAD_W
    scale = SB_DH ** -0.5
    gw = SB_GROUP * SB_BLOCK
    assert pad < SB_BLOCK
    ns = len(scatter)

    def body(q_ref, k_ref, v_ref, tot_ref, nproc_ref, do_ref, *rest):
        s_ins, (dq_ref, dkt_ref, dvt_ref) = rest[:ns], rest[ns:ns + 3]
        s_outs, s_sems = rest[ns + 3:2 * ns + 3], rest[2 * ns + 3:]
        if ns:
            s_start, s_finish = _scatter_phases(s_ins, s_outs, *s_sems)
            pl.when((pl.program_id(0) == 0) & (pl.program_id(1) == 0))(s_start)
        i = pl.program_id(1)

        @pl.when(i == 0)
        def _():
            dkt_ref[...] = jnp.zeros_like(dkt_ref)
            dvt_ref[...] = jnp.zeros_like(dvt_ref)

        qs = _sb_stack(q_ref[...], scale)
        dos = _sb_stack(do_ref[...])
        qst, dost = qs.T, dos.T
        totv = tot_ref[...]
        ones = jnp.ones((1, HEAD_W), F32)
        tots = jnp.concatenate([totv[:, 0:1] * ones, totv[:, SB_DH:SB_DH + 1] * ones], axis=0)
        qpos = i * SB_BLOCK + lax.broadcasted_iota(jnp.int32, (SB_BLOCK, 1), 0)
        qpos = jnp.concatenate([qpos, qpos], axis=0)
        incl, incl0 = _sb_cat("incl"), _sb_cat("incl", pad)
        before = _sb_cat("before")
        ng = i // SB_GROUP
        used = jnp.clip(nproc_ref[pl.program_id(0), i].astype(jnp.int32), 0, ng)

        def dscore(z, e, ev, dl1m):
            r = 1.0 / (1.0 + e)
            sg = jnp.where(z >= 0, r, e * r)
            return ev * (1.0 - sg) - dl1m * sg

        def group(off, nblk, first_incl, allowed, carry):
            dq, prun, erun = carry
            width = nblk * SB_BLOCK
            kg = k_ref[pl.ds(off, width), :].astype(BF16)
            vg = v_ref[pl.ds(off, width), :].astype(BF16)
            z = _dot(qs, kg, NT)
            lsz, l1m, e = _sb_logsig(z)
            if allowed is not None:
                l1m = jnp.where(allowed, l1m, 0.0)
            dwgt = _dot(dos, vg, NT)
            dzs = [None] * nblk
            wgts = [None] * nblk
            for g in range(nblk):
                sl = slice(g * SB_BLOCK, (g + 1) * SB_BLOCK)
                al = _sb_cumsum(l1m[:, sl], first_incl if g == 0 else incl)
                wgt = jnp.exp(jnp.minimum(lsz[:, sl] + (tots - prun - al[:, :SB_BLOCK]), 0.0))
                if allowed is not None:
                    wgt = jnp.where(allowed[:, sl], wgt, 0.0)
                prun = prun + al[:, SB_BLOCK:]
                ev = wgt * dwgt[:, sl]
                el = _sb_cumsum(ev, before)
                dzs[g] = dscore(z[:, sl], e[:, sl], ev, erun + el[:, :SB_BLOCK])
                erun = erun + el[:, SB_BLOCK:]
                wgts[g] = wgt
            dz = jnp.concatenate(dzs, axis=1)
            if allowed is not None:
                dz = jnp.where(allowed, dz, 0.0)
            dz = dz.astype(BF16)
            wg = jnp.concatenate(wgts, axis=1).astype(BF16)
            dkt_ref[:, pl.ds(off, width)] += _dot(qst, dz, NN)
            dvt_ref[:, pl.ds(off, width)] += _dot(dost, wg, NN)
            return dq + _dot(dz, kg, NN), prun, erun

        def below(gi, carry):
            return group(pl.multiple_of(gi * gw, gw), SB_GROUP, jnp.where(gi == 0, incl0, incl), None, carry)

        zero = tuple(jnp.zeros((2 * SB_BLOCK, HEAD_W), F32) for _ in range(3))
        carry = lax.fori_loop(ng - used, ng, below, zero)
        top = ng * gw

        def top_group(nblk, carry):
            off = pl.multiple_of(jnp.minimum(top, lp - nblk * SB_BLOCK), SB_BLOCK)
            kpos = off + lax.broadcasted_iota(jnp.int32, (1, nblk * SB_BLOCK), 1)
            return group(off, nblk, incl, (kpos < qpos) & (kpos >= pad) & (kpos >= top), carry)

        dq, _, _ = lax.cond(i - ng * SB_GROUP < SB_GROUP // 2, functools.partial(top_group, SB_GROUP // 2),
                            functools.partial(top_group, SB_GROUP), carry)
        dq_ref[...] = _sb_unstack(dq) * scale
        if ns:
            pl.when((pl.program_id(0) == npair - 1) & (pl.program_id(1) == nb - 1))(s_finish)

    full = lambda c0: pl.BlockSpec((lp, HEAD_W), lambda p, i: (0, c0 + p))
    qb = lambda c0: pl.BlockSpec((SB_BLOCK, HEAD_W), lambda p, i: (i, c0 + p))
    tr = pl.BlockSpec((HEAD_W, lp), lambda p, i: (p, 0))
    return pl.pallas_call(
        body, name=name, grid=(npair, nb),
        in_specs=[qb(blk0), full(blk0 + npair), full(blk0 + 2 * npair), qb(0), pl.BlockSpec(memory_space=pltpu.SMEM),
                  qb(d_blk0)] + [_ANY] * ns,
        out_specs=[qb(0), tr, tr] + [_ANY] * ns,
        out_shape=[jax.ShapeDtypeStruct((lp, npair * HEAD_W), F32)]
        + [jax.ShapeDtypeStruct((npair * HEAD_W, lp), F32)] * 2
        + [jax.ShapeDtypeStruct(s.shape, s.dtype) for s in scatter],
        scratch_shapes=_scatter_scratch(ns) if ns else [],
        compiler_params=_cp(("arbitrary", "arbitrary"), has_side_effects=bool(ns)),
    )(p0, p0, p0, tot, nproc, dsrc, *scatter)


def _local_step(h0, target, pad, wts, hooks=None):
    lp = h0.shape[0]
    tm = _row_tile(lp, 1056)
    tkl = tm
    tml = _row_tile(lp, 528)
    d = D_MODEL
    mm = _mm
    mmw = functools.partial(_mm, out_dtype=BF16)
    g = {}

    h0_b = h0.astype(BF16)
    p0 = mm(h0_b, wts["w_ab"], "NN", tm=tm, tn=768, tk=d, name="l0_in_proj")
    ob, sb_tot, sb_used, *gathered = _sb_fwd(p0, pad, name="sb_fwd", gather=hooks["gather_a"] if hooks else None)
    if hooks:
        wts = {**wts, **hooks["weights_a"](gathered)}
    qkv = _gdn_pre_fwd(p0, wts["conv_w"], pad, name="gdn_pre_fwd")
    oa_raw, gdn_states, *gathered = _gdn_fwd(qkv, p0, wts["alog_v"], wts["dtb_v"], pad, name="gdn_fwd",
                                             gather=hooks["gather_b"] if hooks else None)
    if hooks:
        second = hooks["weights_b"](gathered)
        wts = {**wts, **second, "w1": wts["w1"] + second["w1"], "w2": wts["w2"] + second["w2"]}
    rows = lambda a, n: a.reshape(N_DEV, n // N_DEV, d)
    parts = g["parts"] = {}
    oab = _gate_fwd(oa_raw, p0, AB_Z // HEAD_W, wts["ab_gn"], ob, heads=GDN_HEADS, name="gdn_gate_fwd")
    ln = lambda kind, layer: (wts[f"ln_{kind}_g"][layer], wts[f"ln_{kind}_b"][layer])
    pre_mix0, h0a, h0a_b = mm(oab, wts["w_out0"], "NN", tm=tml, tn=d, tk=d, epi="ln", c=h0, scale=DN_ALPHA,
                              ln=ln("mix", 0), name="l0_out_proj")
    u0 = mm(h0a_b, wts["w1"][0], "NN", tm=tm, tn=512, tk=d, b_dev=True, name="mlp0_up")
    pre_ffn0, h0b, h0b_b = mm(u0, wts["w2"][0], "NN", tm=tml, tn=d, tk=d, a_fn="relu2", epi="ln", c=h0a,
                              scale=DN_ALPHA, ln=ln("ffn", 0), name="mlp0_down")
    p1 = mm(h0b_b, wts["w_c"], "NN", tm=tm, tn=512, tk=d, b_dev=True, name="l1_in_proj")
    oc_raw, hg_states = _hg_fwd(p1, wts["lb"], pad, name="hg_fwd")
    oc = _gate_fwd(oc_raw, p1, 3 * HG_HEADS, wts["c_gn"], oc_raw, heads=HG_HEADS, name="hg_gate_fwd")
    pre_mix1, h1a, h1a_b = mm(oc, wts["w_out1"], "NN", tm=tml, tn=d, tk=d, epi="ln", c=h0b, scale=DN_ALPHA,
                              ln=ln("mix", 1), name="l1_out_proj")
    u1 = mm(h1a_b, wts["w1"][1], "NN", tm=tm, tn=512, tk=d, b_dev=True, name="mlp1_up")
    pre_ffn1, h1b, _ = mm(u1, wts["w2"][1], "NN", tm=tml, tn=d, tk=d, a_fn="relu2", epi="ln", c=h1a, scale=DN_ALPHA,
                          ln=ln("ffn", 1), name="mlp1_down")
    dy, loss_vec = _loss_head(h1b, target, name="loss_head")

    def mlp_bwd(layer, h_in_b, u, dpre, dpre_b):
        du = mm(dpre_b, wts["w2"][layer], "NT", tm=tm, tn=1024, tk=d, epi="relu2grad", c=u, out_dtype=BF16,
                name=f"mlp{layer}_d_hidden")
        dw2 = mmw(u, dpre_b, "TN", tm=1024, tn=1024, tk=tkl, a_fn="relu2", name=f"mlp{layer}_dw2")
        dw1 = mmw(h_in_b, du, "TN", tm=1024, tn=512, tk=tkl, out_dev=True, name=f"mlp{layer}_dw1")
        dh = mm(du, wts["w1"][layer], "NT", tm=tm, tn=1024, tk=512, b_dev=True, epi="add", c=dpre, scale=DN_ALPHA,
                name=f"mlp{layer}_d_in")
        return dh, dw1, dw2

    ln_ffn_dg, ln_ffn_db, ln_mix_dg, ln_mix_db, dw1s, dw2s = ([None, None] for _ in range(6))
    dpre, dpre_b, ln_ffn_dg[1], ln_ffn_db[1] = _ln_bwd(pre_ffn1, wts["ln_ffn_g"][1], dy, name="ln_ffn1_bwd")
    dh1a, dw1s[1], dw2s[1] = mlp_bwd(1, h1a_b, u1, dpre, dpre_b)
    dpre, dpre_b, ln_mix_dg[1], ln_mix_db[1] = _ln_bwd(pre_mix1, wts["ln_mix_g"][1], dh1a, name="ln_mix1_bwd")
    g["c_w_out"] = mmw(oc, dpre_b, "TN", tm=1024, tn=1024, tk=tkl, name="l1_dw_out")
    doc = mm(dpre_b, wts["w_out1"], "NT", tm=tm, tn=1024, tk=d, name="l1_d_gate")
    doc_raw, dz1, g["c_gn"] = _gate_bwd(oc_raw, p1, 3 * HG_HEADS, wts["c_gn"], doc, heads=HG_HEADS, name="hg_gate_bwd")
    ready = [dw1s[1], rows(dw2s[1], D_FF), rows(g["c_w_out"], d)] if hooks else ()
    dq1, df1, di1, g["lb"], *got = _hg_bwd(p1, wts["lb"], hg_states, doc_raw, pad, name="hg_bwd", scatter=ready)
    parts.update(zip(("mlp_w1_1", "mlp_w2_1", "c_w_out"), got))
    dp1 = jnp.concatenate([dq1, df1, di1, dz1], axis=1).astype(BF16)
    g["c_w_in"] = mmw(h0b_b, dp1, "TN", tm=1024, tn=512, tk=tkl, out_dev=True, name="l1_dw_in")
    dh0b = mm(dp1, wts["w_c"], "NT", tm=tm, tn=1024, tk=512, b_dev=True, epi="add", c=dpre, scale=DN_ALPHA,
              name="l1_d_in")
    dpre, dpre_b, ln_ffn_dg[0], ln_ffn_db[0] = _ln_bwd(pre_ffn0, wts["ln_ffn_g"][0], dh0b, name="ln_ffn0_bwd")
    dh0a, dw1s[0], dw2s[0] = mlp_bwd(0, h0a_b, u0, dpre, dpre_b)
    dpre, dpre_b, ln_mix_dg[0], ln_mix_db[0] = _ln_bwd(pre_mix0, wts["ln_mix_g"][0], dh0a, name="ln_mix0_bwd")
    g["ab_w_out"] = mmw(oab, dpre_b, "TN", tm=1024, tn=1024, tk=tkl, name="l0_dw_out")
    doab = mm(dpre_b, wts["w_out0"], "NT", tm=tm, tn=1024, tk=d, name="l0_d_gate")
    doa_raw, dz0, g["ab_gn"] = _gate_bwd(oa_raw, p0, AB_Z // HEAD_W, wts["ab_gn"], doab, heads=GDN_HEADS,
                                         name="gdn_gate_bwd")
    ready = [g["c_w_in"]] if hooks else ()
    dqb, dkb_t, dvb_t, *got = _sb_bwd(p0, sb_tot, sb_used, doab, GDN_HEADS, pad, name="sb_bwd", scatter=ready)
    parts.update(zip(("c_w_in",), got))
    dkb, dvb = dkb_t.T, dvb_t.T
    ready = [dw1s[0], rows(dw2s[0], D_FF), rows(g["ab_w_out"], d)] if hooks else ()
    dqn, dkn, dvn, dba, g["alog_v"], g["dtb_v"], *got = _gdn_bwd(qkv, p0, wts["alog_v"], wts["dtb_v"], gdn_states,
                                                                 doa_raw, pad, name="gdn_bwd", scatter=ready)
    parts.update(zip(("mlp_w1_0", "mlp_w2_0", "ab_w_out"), got))
    dconv_in, g["conv_w"] = _gdn_pre_bwd(p0, wts["conv_w"], jnp.concatenate([dqn, dkn, dvn], axis=1), pad,
                                         name="gdn_pre_bwd")
    dp0 = jnp.concatenate([dconv_in, dz0, dqb, dkb, dvb, dba, jnp.zeros((lp, AB_CAT - AB_BA - HEAD_W), F32)],
                          axis=1).astype(BF16)
    g["w_ab"] = mmw(h0_b, dp0, "TN", tm=1024, tn=768, tk=tkl, name="l0_dw_in")
    last = ()
    if hooks:
        gab, ba0 = g["w_ab"], AB_Z + GDN_HEADS * HEAD_W
        gab = jnp.concatenate([gab[:, :ba0], gab[:, AB_BA:AB_BA + 2 * GDN_HEADS], gab[:, ba0:AB_BA]], axis=1)
        last = [gab.reshape(d, N_DEV, AB_IN // N_DEV).transpose(1, 0, 2)]
    res = mm(dp0, wts["w_ab"], "NT", tm=tm, tn=1024, tk=768, epi="add", c=dpre, scale=DN_ALPHA, scatter=last,
             name="l0_d_in")
    dh0 = res[0] if last else res
    parts.update(zip(("ab_w_in",), res[1:] if last else ()))

    g["w1"], g["w2"] = dw1s, dw2s
    g["ln_mix_g"] = jnp.concatenate(ln_mix_dg, axis=0)
    g["ln_mix_b"] = jnp.concatenate(ln_mix_db, axis=0)
    g["ln_ffn_g"] = jnp.concatenate(ln_ffn_dg, axis=0)
    g["ln_ffn_b"] = jnp.concatenate(ln_ffn_db, axis=0)
    return loss_vec, dh0, g


N_CHIP = N_DEV // 2


def _place():
    x, y, c = lax.axis_index("x"), lax.axis_index("y"), lax.axis_index("c")
    return x, y, c, 2 * x + y


def _chip_dev(chip, core):
    return (chip // 2, chip % 2, core)


def _remote(src, dst, send_sem, recv_sem, dev):
    return pltpu.make_async_remote_copy(src_ref=src, dst_ref=dst, send_sem=send_sem, recv_sem=recv_sem,
                                        device_id=dev, device_id_type=pl.DeviceIdType.MESH)


_ANY = pl.BlockSpec(memory_space=pl.ANY)


def _gather(srcs, dtypes, *, name):
    n = len(srcs)

    def body(*refs):
        start, forward, finish = _gather_phases(refs[:n], refs[n:2 * n], refs[2 * n:3 * n], *refs[3 * n:], dtypes)
        start()
        forward()
        finish()

    return pl.pallas_call(
        body, name=name, in_specs=[pl.BlockSpec(memory_space=pltpu.VMEM)] * n, out_specs=[_ANY] * n,
        out_shape=_gather_out_shapes(srcs, dtypes), scratch_shapes=_gather_scratch(srcs, dtypes),
        compiler_params=_cp(has_side_effects=True),
    )(*srcs)


def _gather_out_shapes(srcs, dtypes):
    return [jax.ShapeDtypeStruct((N_DEV, *s.shape), dt) for s, dt in zip(srcs, dtypes)]


def _gather_scratch(srcs, dtypes):
    n = len(srcs)
    return [pltpu.VMEM(s.shape, dt) for s, dt in zip(srcs, dtypes)] + [
        pltpu.SemaphoreType.DMA((n, 2 * N_CHIP - 1)), pltpu.SemaphoreType.DMA((n, 2 * N_CHIP - 1)),
        pltpu.SemaphoreType.DMA((n,))]


def _gather_phases(ins, outs, stages, send_sems, recv_sems, local_sems, dtypes):
    n = len(ins)
    x, y, c, chip = _place()
    me = 2 * chip + c
    sibling = (x, y, 1 - c)

    def own(i):
        cps = [_remote(stages[i], outs[i].at[me], send_sems.at[i, 0], recv_sems.at[i, 0], sibling)]
        for j in range(1, N_CHIP):
            cps.append(_remote(stages[i], outs[i].at[me], send_sems.at[i, j], recv_sems.at[i, j],
                               _chip_dev(jnp.bitwise_xor(chip, j), c)))
        return cps

    def local(i):
        return pltpu.make_async_copy(stages[i], outs[i].at[me], local_sems.at[i])

    def passed_on(i, j):
        slot = outs[i].at[2 * jnp.bitwise_xor(chip, j) + c]
        return _remote(slot, slot, send_sems.at[i, N_CHIP - 1 + j], recv_sems.at[i, N_CHIP - 1 + j], sibling)

    def start():
        for i in range(n):
            stages[i][...] = ins[i][...].astype(dtypes[i])
            local(i).start()
            for cp in own(i):
                cp.start()

    def forward():
        for i in range(n):
            for j in range(1, N_CHIP):
                own(i)[j].wait_recv()
                passed_on(i, j).start()

    def finish():
        for i in range(n):
            own(i)[0].wait_recv()
            for j in range(1, N_CHIP):
                passed_on(i, j).wait_recv()
        for i in range(n):
            for cp in own(i):
                cp.wait_send()
            for j in range(1, N_CHIP):
                passed_on(i, j).wait_send()
            local(i).wait()

    return start, forward, finish


def _scatter_scratch(n):
    return [pltpu.SemaphoreType.DMA((n, N_DEV - 1)), pltpu.SemaphoreType.DMA((n, N_DEV - 1)),
            pltpu.SemaphoreType.DMA((n,))]


def _scatter_phases(ins, outs, send_sems, recv_sems, local_sems):
    n = len(ins)
    _, _, c, chip = _place()
    me = 2 * chip + c

    def copies():
        cps = []
        for i in range(n):
            cps.append(pltpu.make_async_copy(ins[i].at[me], outs[i].at[me], local_sems.at[i]))
            for k in range(1, N_DEV):
                peer = jnp.bitwise_xor(me, k)
                cps.append(_remote(ins[i].at[peer], outs[i].at[me], send_sems.at[i, k - 1], recv_sems.at[i, k - 1],
                                   _chip_dev(peer // 2, peer % 2)))
        return cps

    def start():
        for cp in copies():
            cp.start()

    def finish():
        for cp in copies():
            cp.wait()

    return start, finish


def _adamw(w, parts, m, v, *, name):
    r, c = w.shape
    s = parts.shape[0]
    tm = _row_tile(r, 128) if r % 8 == 0 else r
    c1 = 1.0 - ADAM_B1 ** ADAM_STEP
    c2 = 1.0 - ADAM_B2 ** ADAM_STEP

    def body(w_ref, p_ref, m_ref, v_ref, g_ref, d_ref, m2_ref, v2_ref):
        g = p_ref[0].astype(F32)
        for j in range(1, s):
            g = g + p_ref[j].astype(F32)
        m2 = ADAM_B1 * m_ref[...] + (1.0 - ADAM_B1) * g
        v2 = ADAM_B2 * v_ref[...] + (1.0 - ADAM_B2) * jnp.square(g)
        g_ref[...] = g
        m2_ref[...] = m2
        v2_ref[...] = v2
        d_ref[...] = -ADAM_LR * ((m2 / c1) / (jnp.sqrt(v2 / c2) + ADAM_EPS) + ADAM_WD * w_ref[...])

    blk = pl.BlockSpec((tm, c), lambda i: (i, 0))
    return pl.pallas_call(
        body, name=name, grid=(r // tm,),
        in_specs=[blk, pl.BlockSpec((s, tm, c), lambda i: (0, i, 0)), blk, blk], out_specs=[blk] * 4,
        out_shape=[jax.ShapeDtypeStruct((r, c), F32)] * 4, compiler_params=_cp(("parallel",)),
    )(w, parts, m, v)


_WEIGHTS = ("meta_tokens", "ab_w_in", "ab_conv_w", "ab_a_log", "ab_dt_bias", "ab_gnorm_g", "ab_w_out", "c_w_in",
            "c_lb_raw", "c_gnorm_g", "c_w_out", "ln_mix_g", "ln_mix_b", "mlp_w1", "mlp_w2", "ln_ffn_g", "ln_ffn_b")
_PACK_ROWS = (("ln_mix_g", 0), ("ln_mix_b", 2), ("ln_ffn_g", 4), ("ln_ffn_b", 6), ("c_lb_raw", 8))
_PACK_MISC_ROW = 10
_PACK_MISC = (("ab_gnorm_g", 0, 128), ("c_gnorm_g", 128, 128), ("ab_a_log", 256, GDN_HEADS), ("ab_dt_bias", 260, GDN_HEADS))
_PACK_N = 16
_SMALL_META = 16
_SMALL_CONV = 32
_SMALL_N = 40


def _pack_replicated(p):
    rows = jnp.zeros((_PACK_N, D_MODEL), F32)
    for name, r0 in _PACK_ROWS:
        rows = rows.at[r0:r0 + 2].set(p[name])
    for name, c0, width in _PACK_MISC:
        rows = rows.at[_PACK_MISC_ROW, c0:c0 + width].set(p[name].reshape(width))
    return rows


def _unpack_replicated(rows, like):
    out = {}
    for name, r0 in _PACK_ROWS:
        out[name] = rows[r0:r0 + 2]
    for name, c0, width in _PACK_MISC:
        out[name] = rows[_PACK_MISC_ROW, c0:c0 + width].reshape(like[name].shape)
    return out


def _lower_bound(c_lb_raw):
    lb_all = jnp.cumsum(jax.nn.softmax(c_lb_raw.astype(F32), axis=0), axis=0)
    return (lb_all - lb_all[0:1])[1].reshape(1, -1)


def kernel(x, meta_tokens, ab_w_in, ab_conv_w, ab_a_log, ab_dt_bias, ab_gnorm_g, ab_w_out, c_w_in, c_lb_raw, c_gnorm_g, c_w_out, ln_mix_g, ln_mix_b, mlp_w1, mlp_w2, ln_ffn_g, ln_ffn_b, loss_target, m_meta_tokens, m_ab_w_in, m_ab_conv_w, m_ab_a_log, m_ab_dt_bias, m_ab_gnorm_g, m_ab_w_out, m_c_w_in, m_c_lb_raw, m_c_gnorm_g, m_c_w_out, m_ln_mix_g, m_ln_mix_b, m_mlp_w1, m_mlp_w2, m_ln_ffn_g, m_ln_ffn_b, v_meta_tokens, v_ab_w_in, v_ab_conv_w, v_ab_a_log, v_ab_dt_bias, v_ab_gnorm_g, v_ab_w_out, v_c_w_in, v_c_lb_raw, v_c_gnorm_g, v_c_w_out, v_ln_mix_g, v_ln_mix_b, v_mlp_w1, v_mlp_w2, v_ln_ffn_g, v_ln_ffn_b):
    w = dict(zip(_WEIGHTS, (meta_tokens, ab_w_in, ab_conv_w, ab_a_log, ab_dt_bias, ab_gnorm_g, ab_w_out, c_w_in, c_lb_raw,
                            c_gnorm_g, c_w_out, ln_mix_g, ln_mix_b, mlp_w1, mlp_w2, ln_ffn_g, ln_ffn_b)))
    mom = dict(zip(_WEIGHTS, (m_meta_tokens, m_ab_w_in, m_ab_conv_w, m_ab_a_log, m_ab_dt_bias, m_ab_gnorm_g, m_ab_w_out,
                              m_c_w_in, m_c_lb_raw, m_c_gnorm_g, m_c_w_out, m_ln_mix_g, m_ln_mix_b, m_mlp_w1, m_mlp_w2,
                              m_ln_ffn_g, m_ln_ffn_b)))
    var = dict(zip(_WEIGHTS, (v_meta_tokens, v_ab_w_in, v_ab_conv_w, v_ab_a_log, v_ab_dt_bias, v_ab_gnorm_g, v_ab_w_out,
                              v_c_w_in, v_c_lb_raw, v_c_gnorm_g, v_c_w_out, v_ln_mix_g, v_ln_mix_b, v_mlp_w1, v_mlp_w2,
                              v_ln_ffn_g, v_ln_ffn_b)))
    me = 4 * lax.axis_index("x") + 2 * lax.axis_index("y") + lax.axis_index("c")
    seq = x.shape[1]
    pad = (-(N_META + seq)) % SB_BLOCK
    lp = pad + N_META + seq
    meta_w = D_MODEL // N_DEV
    conv_w_all = 2 * GDN_HEADS * HEAD_W + GDN_HEADS * HEAD_W
    conv_w_mine = conv_w_all // N_DEV

    g_meta, g_conv, g_ab_in = _gather([w["meta_tokens"], w["ab_conv_w"][0], w["ab_w_in"][0]], [F32, F32, BF16],
                                      name="gather_weights_first")
    meta_full = g_meta.transpose(1, 0, 2).reshape(N_META, D_MODEL)
    conv_full = g_conv.transpose(1, 0, 2).reshape(CONV_K, conv_w_all)
    ab_full = g_ab_in.transpose(1, 0, 2).reshape(D_MODEL, AB_IN)
    ba0 = AB_Z + 512
    w_ab = jnp.concatenate([ab_full[:, :ba0], ab_full[:, ba0 + 2 * GDN_HEADS:], ab_full[:, ba0:ba0 + 2 * GDN_HEADS],
                            jnp.zeros((D_MODEL, AB_CAT - AB_IN), BF16)], axis=1)
    vec128 = lambda p: jnp.zeros((1, HEAD_W), F32).at[0, :GDN_HEADS].set(p.reshape(GDN_HEADS))
    wts = dict(
        w_ab=w_ab, conv_w=conv_full, alog_v=vec128(w["ab_a_log"]), dtb_v=vec128(w["ab_dt_bias"]),
        ab_gn=w["ab_gnorm_g"][0], lb=_lower_bound(w["c_lb_raw"]), c_gn=w["c_gnorm_g"][0],
        ln_mix_g=w["ln_mix_g"], ln_mix_b=w["ln_mix_b"], ln_ffn_g=w["ln_ffn_g"], ln_ffn_b=w["ln_ffn_b"])

    def weights_a(gathered):
        g_ab_out, g_w1, g_w2 = gathered
        return dict(w_out0=g_ab_out.reshape(D_MODEL, D_MODEL), w1=[g_w1], w2=[g_w2.reshape(D_FF, D_MODEL)])

    def weights_b(gathered):
        g_c_in, g_c_out, g_w1, g_w2 = gathered
        return dict(w_c=g_c_in, w_out1=g_c_out.reshape(D_MODEL, D_MODEL), w1=[g_w1], w2=[g_w2.reshape(D_FF, D_MODEL)])

    hooks = dict(
        gather_a=([w["ab_w_out"][0], w["mlp_w1"][0], w["mlp_w2"][0]], [BF16] * 3), weights_a=weights_a,
        gather_b=([w["c_w_in"][0], w["c_w_out"][0], w["mlp_w1"][1], w["mlp_w2"][1]], [BF16] * 4), weights_b=weights_b)

    h0 = jnp.concatenate([jnp.zeros((pad, D_MODEL), F32), meta_full, x[0]], axis=0)
    loss_vec, dh0, g = _local_step(h0, loss_target[0], pad, wts, hooks)
    loss = lax.psum(jnp.sum(loss_vec), ("x", "y", "c"))
    grad_x = dh0[lp - seq:][None]

    _, lb_vjp = jax.vjp(_lower_bound, w["c_lb_raw"])
    rep_part = _pack_replicated(dict(
        ln_mix_g=g["ln_mix_g"], ln_mix_b=g["ln_mix_b"], ln_ffn_g=g["ln_ffn_g"], ln_ffn_b=g["ln_ffn_b"],
        c_lb_raw=lb_vjp(g["lb"])[0], ab_gnorm_g=g["ab_gn"], c_gnorm_g=g["c_gn"],
        ab_a_log=g["alog_v"][0, :GDN_HEADS], ab_dt_bias=g["dtb_v"][0, :GDN_HEADS]))
    small = jnp.concatenate([rep_part, dh0[pad:pad + N_META], g["conv_w"].reshape(-1, D_MODEL),
                             jnp.zeros((_SMALL_N - _SMALL_CONV - CONV_K * conv_w_all // D_MODEL, D_MODEL), F32)], axis=0)
    (small_all,) = _gather([small], [F32], name="gather_small_grads")
    rep_out = _adamw(_pack_replicated(w), small_all[:, :_PACK_N], _pack_replicated(mom), _pack_replicated(var),
                     name="adamw_replicated")
    meta_parts = lax.dynamic_slice_in_dim(small_all[:, _SMALL_META:_SMALL_META + N_META], me * meta_w, meta_w, axis=2)
    meta_out = _adamw(w["meta_tokens"], meta_parts, mom["meta_tokens"], var["meta_tokens"], name="adamw_meta")
    conv_parts = small_all[:, _SMALL_CONV:_SMALL_CONV + CONV_K * conv_w_all // D_MODEL].reshape(N_DEV, CONV_K, conv_w_all)
    conv_parts = lax.dynamic_slice_in_dim(conv_parts, me * conv_w_mine, conv_w_mine, axis=2)
    conv_out = _adamw(w["ab_conv_w"][0], conv_parts, mom["ab_conv_w"][0], var["ab_conv_w"][0], name="adamw_conv")

    parts = g["parts"]
    big = [("ab_w_in", 0, parts["ab_w_in"]), ("ab_w_out", 0, parts["ab_w_out"]), ("mlp_w1", 0, parts["mlp_w1_0"]),
           ("mlp_w2", 0, parts["mlp_w2_0"]), ("c_w_in", 0, parts["c_w_in"]), ("c_w_out", 0, parts["c_w_out"]),
           ("mlp_w1", 1, parts["mlp_w1_1"]), ("mlp_w2", 1, parts["mlp_w2_1"])]
    big_out = {}
    for name, l, p in big:
        res = _adamw(w[name][l], p, mom[name][l], var[name][l], name=f"adamw_{name}{l}")
        big_out.setdefault(name, []).append(res)

    rep = [_unpack_replicated(r, w) for r in rep_out]
    outs = {}
    for name in _WEIGHTS:
        if name == "meta_tokens":
            outs[name] = list(meta_out)
        elif name == "ab_conv_w":
            outs[name] = [o[None] for o in conv_out]
        elif name in big_out:
            res = big_out[name]
            outs[name] = [o[None] for o in res[0]] if len(res) == 1 else [jnp.stack(pair) for pair in zip(*res)]
        else:
            outs[name] = [r[name] for r in rep]
    flat = [loss, grad_x]
    for kind in range(4):
        flat += [outs[name][kind] for name in _WEIGHTS]
    return tuple(flat)
```

```python
import functools

import jax
import jax.numpy as jnp
from jax import lax
from jax.experimental import pallas as pl
from jax.experimental.pallas import tpu as pltpu

F32 = jnp.float32
BF16 = jnp.bfloat16

N_DEV = 8
D_MODEL = 1024
N_META = 16
D_FF = 4096
DEPTH = 2
GDN_HEADS = 4
SB_HEADS = 8
SB_DH = 64
HG_HEADS = 8
HEAD_W = 128
CHUNK = 64
SB_BLOCK = 128
CONV_K = 4
DN_ALPHA = float((2 * DEPTH) ** 0.25)
LN_EPS = 1e-5
RMS_EPS = 1e-6
L2_EPS = 1e-6
ADAM_LR, ADAM_B1, ADAM_B2, ADAM_EPS, ADAM_WD, ADAM_STEP = 0.001, 0.9, 0.999, 1e-08, 0.01, 10

AB_Z = 1536
AB_SB = 2048
AB_BA = 3584
AB_CAT = 3840
AB_IN = 3592

VMEM_LIMIT = 56 * 1024 * 1024


def _cp(sem=None, **kw):
    if sem is not None:
        kw["dimension_semantics"] = sem
    return pltpu.CompilerParams(vmem_limit_bytes=VMEM_LIMIT, **kw)


def _row_tile(n, want):
    best = 8
    for t in range(8, min(n, want) + 1, 8):
        if n % t == 0:
            best = t
    return best


@jax.custom_vjp
def _sigmoid(x):
    e = jnp.exp(-jnp.abs(x))
    r = 1.0 / (1.0 + e)
    return jnp.where(x >= 0, r, e * r)


def _sigmoid_fwd(x):
    s = _sigmoid(x)
    return s, s


def _sigmoid_bwd(s, g):
    return (g * s * (1.0 - s),)


_sigmoid.defvjp(_sigmoid_fwd, _sigmoid_bwd)


def _log1p_exp_neg_abs(x):
    e = jnp.exp(-jnp.abs(x))
    return jnp.where(e < 1e-4, e - 0.5 * e * e, jnp.log(1.0 + e))


@jax.custom_vjp
def _softplus(x):
    return jnp.maximum(x, 0.0) + _log1p_exp_neg_abs(x)


def _softplus_fwd(x):
    return _softplus(x), x


def _softplus_bwd(x, g):
    return (g * _sigmoid(x),)


_softplus.defvjp(_softplus_fwd, _softplus_bwd)


def _silu(x):
    return x * _sigmoid(x)


def _silu_grad(x):
    s = _sigmoid(x)
    return s * (1.0 + x * (1.0 - s))


def _dot(a, b, dims, precision=None):
    return lax.dot_general(a, b, (dims, ((), ())), precision=precision, preferred_element_type=F32)


NN = ((1,), (0,))
NT = ((1,), (1,))
TN = ((0,), (0,))


def _bdot(a, b, dims):
    return _dot(a.astype(BF16), b.astype(BF16), dims)


def _layer_norm(pre, g, beta):
    mu = jnp.mean(pre, axis=-1, keepdims=True)
    xc = pre - mu
    var = jnp.mean(xc * xc, axis=-1, keepdims=True)
    return xc * lax.rsqrt(var + LN_EPS) * g + beta


def _mm(a, b, mode, *, tm, tn, tk, name, epi=None, c=None, scale=1.0, b_dev=False, out_dev=False, out_dtype=F32,
        ln=None, scatter=()):
    if mode == "NN":
        m, kk = a.shape
        n = b.shape[2] * N_DEV if b_dev else b.shape[1]
    elif mode == "NT":
        m, kk = a.shape
        n = b.shape[1] if b_dev else b.shape[0]
    else:
        kk, m = a.shape
        n = b.shape[1]
    assert m % tm == 0 and n % tn == 0 and kk % tk == 0, (name, m, n, kk, tm, tn, tk)
    nk = kk // tk
    dims = {"NN": NN, "NT": NT, "TN": TN}[mode]

    if mode == "TN":
        a_spec = pl.BlockSpec((tk, tm), lambda i, j, k: (k, i))
    else:
        a_spec = pl.BlockSpec((tm, tk), lambda i, j, k: (i, k))
    if mode == "NN":
        if b_dev:
            assert tn == b.shape[2]
            b_spec = pl.BlockSpec((None, tk, tn), lambda i, j, k: (j, k, 0))
        else:
            b_spec = pl.BlockSpec((tk, tn), lambda i, j, k: (k, j))
    elif mode == "NT":
        if b_dev:
            assert tk == b.shape[2]
            b_spec = pl.BlockSpec((None, tn, tk), lambda i, j, k: (k, j, 0))
        else:
            b_spec = pl.BlockSpec((tn, tk), lambda i, j, k: (j, k))
    else:
        b_spec = pl.BlockSpec((tk, tn), lambda i, j, k: (k, j))
    in_specs = [a_spec, b_spec]
    operands = [a, b]
    if c is not None:
        in_specs.append(pl.BlockSpec((tm, tn), lambda i, j, k: (i, j)))
        operands.append(c)
    if epi == "ln":
        assert tn == n and not out_dev
        in_specs += [pl.BlockSpec((1, n), lambda i, j, k: (0, 0))] * 2
        operands += [ln[0].reshape(1, n), ln[1].reshape(1, n)]
    if out_dev:
        assert tn == n // N_DEV
        out_shape = jax.ShapeDtypeStruct((N_DEV, m, tn), out_dtype)
        out_spec = pl.BlockSpec((None, tm, tn), lambda i, j, k: (j, i, 0))
    else:
        out_shape = jax.ShapeDtypeStruct((m, n), out_dtype)
        out_spec = pl.BlockSpec((tm, tn), lambda i, j, k: (i, j))
    if epi == "ln":
        out_shape = [out_shape, out_shape, jax.ShapeDtypeStruct((m, n), BF16)]
        out_spec = [out_spec] * 3
    elif epi == "relu2_copy":
        assert not out_dev
        out_shape = [out_shape, jax.ShapeDtypeStruct((m, n), BF16)]
        out_spec = [out_spec] * 2
    n_out = {"ln": 3, "relu2_copy": 2}.get(epi, 1)
    ns = len(scatter)
    if ns:
        in_specs += [_ANY] * ns
        operands += list(scatter)
        out_shape = (out_shape if n_out > 1 else [out_shape]) + [jax.ShapeDtypeStruct(s.shape, s.dtype) for s in scatter]
        out_spec = (out_spec if n_out > 1 else [out_spec]) + [_ANY] * ns
    n_in = len(operands)
    grid = (m // tm, n // tn, nk)

    def body(*refs):
        a_ref, b_ref = refs[0], refs[1]
        c_ref = refs[2] if c is not None else None
        o_ref = refs[n_in]
        scratch0 = n_in + n_out + ns
        acc_ref = refs[scratch0] if nk > 1 else None
        if ns:
            s_start, s_finish = _scatter_phases(refs[n_in - ns:n_in], refs[n_in + n_out:scratch0],
                                                *refs[scratch0 + (1 if nk > 1 else 0):])
            at = lambda step: functools.reduce(lambda x, y: x & y, [pl.program_id(ax) == step[ax] for ax in range(3)])
            pl.when(at((0, 0, 0)))(s_start)
        p = _dot(a_ref[...].astype(BF16), b_ref[...].astype(BF16), dims)

        def finish(acc):
            if epi == "add":
                acc = acc + scale * c_ref[...]
            elif epi == "relu2grad":
                acc = acc * (2.0 * jnp.maximum(c_ref[...], 0.0))
            elif epi == "relu2_copy":
                refs[n_in + 1][...] = jnp.square(jnp.maximum(acc, 0.0)).astype(BF16)
            elif epi == "ln":
                acc = acc + scale * c_ref[...]
                y = _layer_norm(acc, refs[3][...], refs[4][...])
                refs[n_in + 1][...] = y
                refs[n_in + 2][...] = y.astype(BF16)
            o_ref[...] = acc.astype(out_dtype)

        if nk == 1:
            finish(p)
        else:
            k = pl.program_id(2)

            @pl.when(k == 0)
            def _():
                acc_ref[...] = p

            @pl.when(k > 0)
            def _():
                acc_ref[...] += p

            @pl.when(k == nk - 1)
            def _():
                finish(acc_ref[...])

        if ns:
            pl.when(at(tuple(g - 1 for g in grid)))(s_finish)

    res = pl.pallas_call(
        body, name=name, grid=grid, in_specs=in_specs, out_specs=out_spec, out_shape=out_shape,
        scratch_shapes=([pltpu.VMEM((tm, tn), F32)] if nk > 1 else []) + (_scatter_scratch(ns) if ns else []),
        compiler_params=_cp(("arbitrary",) * 3 if ns else ("parallel", "parallel", "arbitrary"),
                            has_side_effects=bool(ns)),
    )(*operands)
    return res


def _ln_bwd(pre, g, dy, *, name):
    lp, d = pre.shape
    tm = _row_tile(lp, 512)

    def body(pre_ref, g_ref, dy_ref, dpre_ref, dpreb_ref, dg_ref, db_ref):
        pre = pre_ref[...]
        mu = jnp.mean(pre, axis=-1, keepdims=True)
        xc = pre - mu
        var = jnp.mean(xc * xc, axis=-1, keepdims=True)
        rstd = lax.rsqrt(var + LN_EPS)
        xhat = xc * rstd
        dyv = dy_ref[...]
        dxh = dyv * g_ref[...]
        m1 = jnp.mean(dxh, axis=-1, keepdims=True)
        m2 = jnp.mean(dxh * xhat, axis=-1, keepdims=True)
        dpre = rstd * (dxh - m1 - xhat * m2)
        dpre_ref[...] = dpre
        dpreb_ref[...] = dpre.astype(BF16)

        @pl.when(pl.program_id(0) == 0)
        def _():
            dg_ref[...] = jnp.zeros_like(dg_ref)
            db_ref[...] = jnp.zeros_like(db_ref)

        dg_ref[...] += jnp.sum(dyv * xhat, axis=0, keepdims=True)
        db_ref[...] += jnp.sum(dyv, axis=0, keepdims=True)

    row = pl.BlockSpec((tm, d), lambda i: (i, 0))
    vec = pl.BlockSpec((1, d), lambda i: (0, 0))
    return pl.pallas_call(
        body, name=name, grid=(lp // tm,), in_specs=[row, vec, row], out_specs=[row, row, vec, vec],
        out_shape=[jax.ShapeDtypeStruct((lp, d), F32), jax.ShapeDtypeStruct((lp, d), BF16),
                   jax.ShapeDtypeStruct((1, d), F32), jax.ShapeDtypeStruct((1, d), F32)],
        compiler_params=_cp(("arbitrary",)),
    )(pre, g.reshape(1, d), dy)


def _loss_head(y, target, *, name):
    lp, d = y.shape
    seq = target.shape[0]
    tm = SB_BLOCK
    first = (lp - seq) // tm
    assert (lp - seq) % tm == 0 and seq % tm == 0

    def body(y_ref, t_ref, dy_ref, loss_ref):
        i = pl.program_id(0)
        live = i >= first
        diff = jnp.where(live, y_ref[...] - t_ref[...], 0.0)
        dy_ref[...] = diff * (1.0 / d)

        @pl.when(i == 0)
        def _():
            loss_ref[...] = jnp.zeros_like(loss_ref)

        loss_ref[...] += jnp.sum(diff * diff, axis=0, keepdims=True) * (0.5 / d)

    return pl.pallas_call(
        body, name=name, grid=(lp // tm,),
        in_specs=[pl.BlockSpec((tm, d), lambda i: (i, 0)),
                  pl.BlockSpec((tm, d), lambda i: (jnp.maximum(i - first, 0), 0))],
        out_specs=[pl.BlockSpec((tm, d), lambda i: (i, 0)), pl.BlockSpec((1, d), lambda i: (0, 0))],
        out_shape=[jax.ShapeDtypeStruct((lp, d), F32), jax.ShapeDtypeStruct((1, d), F32)],
        compiler_params=_cp(("arbitrary",)),
    )(y, target)


def _gate_fwd(o, zsrc, z_blk0, g, other, *, heads, name):
    lp = o.shape[0]
    tm = _row_tile(lp, 512)
    w = heads * HEAD_W
    assert (z_blk0 * HEAD_W) % w == 0
    has_other = w < D_MODEL

    def body(o_ref, z_ref, g_ref, *rest):
        y_ref = rest[-1]
        gv = g_ref[...]
        for h in range(heads):
            cs = slice(h * HEAD_W, (h + 1) * HEAD_W)
            ov = o_ref[:, cs]
            r = lax.rsqrt(jnp.mean(ov * ov, axis=-1, keepdims=True) + RMS_EPS)
            y_ref[:, cs] = (ov * r * gv * _silu(z_ref[:, cs])).astype(BF16)
        if has_other:
            y_ref[:, w:] = rest[0][...].astype(BF16)

    row = lambda width, blk: pl.BlockSpec((tm, width), lambda i: (i, blk))
    return pl.pallas_call(
        body, name=name, grid=(lp // tm,),
        in_specs=[row(w, 0), row(w, z_blk0 * HEAD_W // w), pl.BlockSpec((1, HEAD_W), lambda i: (0, 0))]
        + ([row(D_MODEL - w, 0)] if has_other else []),
        out_specs=row(D_MODEL, 0), out_shape=jax.ShapeDtypeStruct((lp, D_MODEL), BF16),
        compiler_params=_cp(("parallel",)),
    )(o, zsrc, g.reshape(1, HEAD_W), *([other] if has_other else []))


def _gate_bwd(o, zsrc, z_blk0, g, dy, *, heads, name):
    lp = o.shape[0]
    tm = _row_tile(lp, 512)

    w = heads * HEAD_W
    assert (z_blk0 * HEAD_W) % w == 0

    def body(o_ref, z_ref, g_ref, dy_ref, do_ref, dz_ref, dg_ref):
        @pl.when(pl.program_id(0) == 0)
        def _():
            dg_ref[...] = jnp.zeros_like(dg_ref)

        gv = g_ref[...]
        dg = jnp.zeros((1, HEAD_W), F32)
        for h in range(heads):
            cs = slice(h * HEAD_W, (h + 1) * HEAD_W)
            ov, zv, dyv = o_ref[:, cs], z_ref[:, cs], dy_ref[:, cs]
            r = lax.rsqrt(jnp.mean(ov * ov, axis=-1, keepdims=True) + RMS_EPS)
            nrm = ov * r
            s = _silu(zv)
            dn = dyv * gv * s
            do_ref[:, cs] = r * (dn - nrm * jnp.mean(dn * nrm, axis=-1, keepdims=True))
            dz_ref[:, cs] = dyv * nrm * gv * _silu_grad(zv)
            dg = dg + jnp.sum(dyv * nrm * s, axis=0, keepdims=True)
        dg_ref[...] += dg

    row = lambda blk: pl.BlockSpec((tm, w), lambda i: (i, blk))
    vec = pl.BlockSpec((1, HEAD_W), lambda i: (0, 0))
    return pl.pallas_call(
        body, name=name, grid=(lp // tm,),
        in_specs=[row(0), row(z_blk0 * HEAD_W // w), vec, row(0)], out_specs=[row(0), row(0), vec],
        out_shape=[jax.ShapeDtypeStruct((lp, w), F32), jax.ShapeDtypeStruct((lp, w), F32),
                   jax.ShapeDtypeStruct((1, HEAD_W), F32)],
        compiler_params=_cp(("arbitrary",)),
    )(o, zsrc, g.reshape(1, HEAD_W), dy)


def _conv_taps(x, w):
    acc = w[CONV_K - 1:CONV_K, :] * x
    for k in range(CONV_K - 1):
        acc = acc + w[k:k + 1, :] * pltpu.roll(x, CONV_K - 1 - k, 0)
    return acc


def _gdn_pre_fwd(p0, conv_w, pad, *, name):
    lp = p0.shape[0]
    nq = GDN_HEADS
    qscale = HEAD_W ** -0.5

    def body(x_ref, w_ref, y_ref):
        j = pl.program_id(0)
        c = _conv_taps(x_ref[...], w_ref[...])
        s = _silu(c)
        r = lax.rsqrt(jnp.sum(s * s, axis=-1, keepdims=True) + L2_EPS)
        mult = jnp.where(j < nq, r * qscale, jnp.where(j < 2 * nq, r, 1.0))
        rows = lax.broadcasted_iota(jnp.int32, (lp, 1), 0)
        y_ref[...] = jnp.where(rows >= pad, s * mult, 0.0)

    return pl.pallas_call(
        body, name=name, grid=(3 * nq,),
        in_specs=[pl.BlockSpec((lp, HEAD_W), lambda j: (0, j)), pl.BlockSpec((CONV_K, HEAD_W), lambda j: (0, j))],
        out_specs=pl.BlockSpec((lp, HEAD_W), lambda j: (0, j)),
        out_shape=jax.ShapeDtypeStruct((lp, 3 * nq * HEAD_W), F32), compiler_params=_cp(("parallel",)),
    )(p0, conv_w)


def _gdn_pre_bwd(p0, conv_w, dqkv, pad, *, name):
    lp = p0.shape[0]
    nq = GDN_HEADS
    qscale = HEAD_W ** -0.5

    def body(x_ref, w_ref, dy_ref, dx_ref, dw_ref):
        j = pl.program_id(0)
        x, w = x_ref[...], w_ref[...]
        c = _conv_taps(x, w)
        s = _silu(c)
        r = lax.rsqrt(jnp.sum(s * s, axis=-1, keepdims=True) + L2_EPS)
        rows = lax.broadcasted_iota(jnp.int32, (lp, 1), 0)
        dy = jnp.where(rows >= pad, dy_ref[...], 0.0)
        nrm = s * r
        dn = dy * jnp.where(j < nq, qscale, 1.0)
        ds_norm = r * (dn - nrm * jnp.sum(nrm * dn, axis=-1, keepdims=True))
        ds = jnp.where(j < 2 * nq, ds_norm, dy)
        dc = ds * _silu_grad(c)
        dx = w[CONV_K - 1:CONV_K, :] * dc
        dws = [None] * CONV_K
        dws[CONV_K - 1] = jnp.sum(dc * x, axis=0, keepdims=True)
        for k in range(CONV_K - 1):
            sh = CONV_K - 1 - k
            dx = dx + w[k:k + 1, :] * pltpu.roll(dc, lp - sh, 0)
            dws[k] = jnp.sum(dc * pltpu.roll(x, sh, 0), axis=0, keepdims=True)
        dx_ref[...] = dx
        dw_ref[...] = jnp.concatenate(dws, axis=0)

    blk = pl.BlockSpec((lp, HEAD_W), lambda j: (0, j))
    wblk = pl.BlockSpec((CONV_K, HEAD_W), lambda j: (0, j))
    return pl.pallas_call(
        body, name=name, grid=(3 * nq,), in_specs=[blk, wblk, blk], out_specs=[blk, wblk],
        out_shape=[jax.ShapeDtypeStruct((lp, 3 * nq * HEAD_W), F32),
                   jax.ShapeDtypeStruct((CONV_K, 3 * nq * HEAD_W), F32)],
        compiler_params=_cp(("parallel",)),
    )(p0, conv_w, dqkv)


@jax.custom_vjp
def _inv_unit_lower(m):
    c = m.shape[0]
    eye = (lax.broadcasted_iota(jnp.int32, (c, c), 0) == lax.broadcasted_iota(jnp.int32, (c, c), 1)).astype(F32)
    x = eye - m
    p = m
    n = 2
    while n < CHUNK:
        p = _bdot(p, p, NN)
        x = x + _bdot(x, p, NN)
        n *= 2
    return x


def _inv_fwd(m):
    t = _inv_unit_lower(m)
    return t, t


def _inv_bwd(t, g):
    return (-_bdot(_bdot(t, g, TN), t, NT),)


_inv_unit_lower.defvjp(_inv_fwd, _inv_bwd)


def _heads_to_rows(x, nh):
    return jnp.concatenate([x[:, h * HEAD_W:(h + 1) * HEAD_W] for h in range(nh)], axis=0)


def _rows_to_heads(x, nh):
    c = x.shape[0] // nh
    return jnp.concatenate([x[h * c:(h + 1) * c] for h in range(nh)], axis=1)


def _gdn_chunk(q, k, v, ba, alog, dtb, states, valid):
    nh = GDN_HEADS
    c = q.shape[0]
    r = nh * c
    lane = lax.broadcasted_iota(jnp.int32, (1, HEAD_W), 1)
    pick = lambda x, l: jnp.sum(jnp.where(lane == l, x, 0.0), axis=-1, keepdims=True)
    beta = jnp.concatenate([jnp.where(valid, _sigmoid(pick(ba, h)), 0.0) for h in range(nh)], axis=0)
    g = jnp.concatenate(
        [jnp.where(valid, -jnp.exp(pick(alog, h)) * _softplus(pick(ba, nh + h) + pick(dtb, h)), 0.0) for h in range(nh)],
        axis=0)
    qs, ks, vs = _heads_to_rows(q, nh), _heads_to_rows(k, nh), _heads_to_rows(v, nh)
    rr = lax.broadcasted_iota(jnp.int32, (r, r), 0)
    cc = lax.broadcasted_iota(jnp.int32, (r, r), 1)
    same = (rr // c) == (cc // c)
    causal, strict = same & (cc <= rr), same & (cc < rr)
    lower = jnp.where(causal, 1.0, 0.0).astype(BF16)
    upper = jnp.where(same & (cc >= rr), 1.0, 0.0).astype(BF16)
    gcb = _mask_mm(lower, upper, g * jnp.ones((1, HEAD_W), F32))
    gc_col = jnp.concatenate([gcb] * (r // HEAD_W), axis=1)
    decay = jnp.where(causal, jnp.exp(jnp.minimum(gc_col - gc_col.T, 0.0)), 0.0)
    egc = jnp.exp(gcb)
    kb = ks * beta
    m = jnp.where(strict, _bdot(kb, ks, NT) * decay, 0.0)
    t = _inv_unit_lower(m)
    u = _bdot(t, vs * beta, NN)
    w = _bdot(t, kb * egc, NN)
    a = _bdot(qs, ks, NT) * decay
    rows = lambda x, h: x[h * c:(h + 1) * c]
    qe = qs * egc
    v_new = u - jnp.concatenate([_bdot(rows(w, h), states[h], NN) for h in range(nh)], axis=0)
    o = jnp.concatenate([_bdot(rows(qe, h), states[h], NN) for h in range(nh)], axis=0) + _bdot(a, v_new, NN)
    new_states = []
    for h in range(nh):
        gl = gcb[(h + 1) * c - 1:(h + 1) * c, :]
        k_dec = rows(ks, h) * jnp.exp(gl - rows(gcb, h))
        new_states.append(states[h] * jnp.exp(gl) + _bdot(k_dec, rows(v_new, h), TN))
    return _rows_to_heads(o, nh), new_states


def _gdn_fwd(qkv, p0, alog_v, dtb_v, pad, *, name, gather=None):
    lp = qkv.shape[0]
    n = lp // CHUNK
    nh = GDN_HEADS
    g_srcs, g_dtypes = gather if gather is not None else ([], [])
    ng_arr = len(g_srcs)

    def body(q_ref, k_ref, v_ref, ba_ref, al_ref, dt_ref, *rest):
        g_ins, (o_ref, st_ref) = rest[:ng_arr], rest[ng_arr:ng_arr + 2]
        g_outs, s_ref, g_scratch = rest[ng_arr + 2:2 * ng_arr + 2], rest[2 * ng_arr + 2], rest[2 * ng_arr + 3:]
        i = pl.program_id(0)
        if ng_arr:
            g_start, g_forward, g_finish = _gather_phases(g_ins, g_outs, g_scratch[:ng_arr], *g_scratch[ng_arr:],
                                                          g_dtypes)
            pl.when(i == 0)(g_start)
            pl.when(i == (3 * n) // 4)(g_forward)

        @pl.when(i == 0)
        def _():
            s_ref[...] = jnp.zeros_like(s_ref)

        valid = (i * CHUNK + lax.broadcasted_iota(jnp.int32, (CHUNK, 1), 0)) >= pad
        s = s_ref[...]
        o, s2 = _gdn_chunk(q_ref[...], k_ref[...], v_ref[...], ba_ref[...], al_ref[...], dt_ref[...],
                           [s[h] for h in range(nh)], valid)
        st_ref[...] = s
        o_ref[...] = o
        for h in range(nh):
            s_ref[h] = s2[h]
        if ng_arr:
            pl.when(i == n - 1)(g_finish)

    w = nh * HEAD_W
    vec = pl.BlockSpec((1, HEAD_W), lambda i: (0, 0))
    return pl.pallas_call(
        body, name=name, grid=(n,),
        in_specs=[pl.BlockSpec((CHUNK, w), lambda i: (i, 0)), pl.BlockSpec((CHUNK, w), lambda i: (i, 1)),
                  pl.BlockSpec((CHUNK, w), lambda i: (i, 2)), pl.BlockSpec((CHUNK, HEAD_W), lambda i: (i, AB_BA // HEAD_W)),
                  vec, vec] + [pl.BlockSpec(memory_space=pltpu.VMEM)] * ng_arr,
        out_specs=[pl.BlockSpec((CHUNK, w), lambda i: (i, 0)),
                   pl.BlockSpec((None, nh, HEAD_W, HEAD_W), lambda i: (i, 0, 0, 0))] + [_ANY] * ng_arr,
        out_shape=[jax.ShapeDtypeStruct((lp, w), F32), jax.ShapeDtypeStruct((n, nh, HEAD_W, HEAD_W), F32)]
        + _gather_out_shapes(g_srcs, g_dtypes),
        scratch_shapes=[pltpu.VMEM((nh, HEAD_W, HEAD_W), F32)] + (_gather_scratch(g_srcs, g_dtypes) if ng_arr else []),
        compiler_params=_cp(("arbitrary",), has_side_effects=bool(ng_arr)),
    )(qkv, qkv, qkv, p0, alog_v, dtb_v, *g_srcs)


def _gdn_bwd(qkv, p0, alog_v, dtb_v, states, do, pad, *, name, scatter=()):
    lp = qkv.shape[0]
    n = lp // CHUNK
    nh = GDN_HEADS
    ns = len(scatter)

    def body(q_ref, k_ref, v_ref, ba_ref, al_ref, dt_ref, st_ref, do_ref, *rest):
        s_ins, (dq_ref, dk_ref, dv_ref, dba_ref, dal_ref, ddt_ref) = rest[:ns], rest[ns:ns + 6]
        s_outs, ds_ref, s_sems = rest[ns + 6:2 * ns + 6], rest[2 * ns + 6], rest[2 * ns + 7:]
        step = pl.program_id(0)
        i = n - 1 - step
        if ns:
            s_start, s_finish = _scatter_phases(s_ins, s_outs, *s_sems)
            pl.when(step == 0)(s_start)

        @pl.when(step == 0)
        def _():
            ds_ref[...] = jnp.zeros_like(ds_ref)
            dal_ref[...] = jnp.zeros_like(dal_ref)
            ddt_ref[...] = jnp.zeros_like(ddt_ref)

        valid = (i * CHUNK + lax.broadcasted_iota(jnp.int32, (CHUNK, 1), 0)) >= pad
        st, dst = st_ref[...], ds_ref[...]
        fn = functools.partial(_gdn_chunk, valid=valid)
        _, vjp = jax.vjp(fn, q_ref[...], k_ref[...], v_ref[...], ba_ref[...], al_ref[...], dt_ref[...],
                         [st[h] for h in range(nh)])
        dq, dk, dv, dba, dal, ddt, ds = vjp((do_ref[...], [dst[h] for h in range(nh)]))
        dq_ref[...] = dq
        dk_ref[...] = dk
        dv_ref[...] = dv
        dba_ref[...] = dba
        dal_ref[...] += dal
        ddt_ref[...] += ddt
        for h in range(nh):
            ds_ref[h] = ds[h]
        if ns:
            pl.when(step == n - 1)(s_finish)

    w = nh * HEAD_W
    rev = lambda c: (lambda s: (n - 1 - s, c))
    vec = pl.BlockSpec((1, HEAD_W), lambda s: (0, 0))
    return pl.pallas_call(
        body, name=name, grid=(n,),
        in_specs=[pl.BlockSpec((CHUNK, w), rev(0)), pl.BlockSpec((CHUNK, w), rev(1)), pl.BlockSpec((CHUNK, w), rev(2)),
                  pl.BlockSpec((CHUNK, HEAD_W), rev(AB_BA // HEAD_W)), vec, vec,
                  pl.BlockSpec((None, nh, HEAD_W, HEAD_W), lambda s: (n - 1 - s, 0, 0, 0)),
                  pl.BlockSpec((CHUNK, w), rev(0))] + [_ANY] * ns,
        out_specs=[pl.BlockSpec((CHUNK, w), rev(0)), pl.BlockSpec((CHUNK, w), rev(0)), pl.BlockSpec((CHUNK, w), rev(0)),
                   pl.BlockSpec((CHUNK, HEAD_W), rev(0)), vec, vec] + [_ANY] * ns,
        out_shape=[jax.ShapeDtypeStruct((lp, w), F32)] * 3 + [jax.ShapeDtypeStruct((lp, HEAD_W), F32)]
        + [jax.ShapeDtypeStruct((1, HEAD_W), F32)] * 2 + [jax.ShapeDtypeStruct(s.shape, s.dtype) for s in scatter],
        scratch_shapes=[pltpu.VMEM((nh, HEAD_W, HEAD_W), F32)] + (_scatter_scratch(ns) if ns else []),
        compiler_params=_cp(("arbitrary",), has_side_effects=bool(ns)),
    )(qkv, qkv, qkv, p0, alog_v, dtb_v, states, do, *scatter)


HG_LEVELS = (32, 16, 8, 4, 2, 1)
HG_GROUP = 4


def _hg_masks():
    import numpy as np
    c = CHUNK
    t = np.arange(c)[:, None]
    j = np.arange(c)[None, :]
    sums = (j <= t).astype(np.float32)
    pairs = [j == t]
    for m in HG_LEVELS:
        p = (t // (2 * m)) * (2 * m)
        r = p + m
        pairs.append((t >= r) & (j < r) & (j >= p))
    pairs = np.concatenate([np.kron(np.eye(HG_GROUP), p) for p in pairs], axis=0).astype(np.float32)
    return jnp.asarray(sums, BF16), jnp.asarray(sums.T, BF16), jnp.asarray(pairs, F32)


def _hg_level_row(b, m):
    c, w = b.shape
    if m >= 8:
        return jnp.concatenate([jnp.broadcast_to(b[p + m:p + m + 1], (2 * m, w)) for p in range(0, c, 2 * m)], axis=0)
    tiles = b.reshape(c // 8, 8, w)
    sub = lax.broadcasted_iota(jnp.int32, (1, 8, 1), 1)
    out = None
    for r0 in range(m, 8, 2 * m):
        cand = jnp.broadcast_to(tiles[:, r0:r0 + 1, :], tiles.shape)
        out = cand if out is None else jnp.where(sub >= r0 - m, cand, out)
    return out.reshape(c, w)


def _split3(x):
    hi = x.astype(BF16)
    r1 = x - hi.astype(F32)
    mid = r1.astype(BF16)
    return hi, mid, (r1 - mid.astype(F32)).astype(BF16)


def _mask_mm_raw(m, x):
    return sum(_dot(m, part, NN) for part in _split3(x))


@jax.custom_vjp
def _mask_mm(m, mt, x):
    return _mask_mm_raw(m, x)


def _mask_mm_fwd(m, mt, x):
    return _mask_mm_raw(m, x), (m, mt)


def _mask_mm_bwd(res, g):
    m, mt = res
    return jnp.zeros_like(m), jnp.zeros_like(mt), _mask_mm_raw(mt, g)


_mask_mm.defvjp(_mask_mm_fwd, _mask_mm_bwd)


def _hg_chunk(qr, fr, ir, lb, states, valid, sums, sums_t, pairs):
    nh = HG_GROUP
    c = qr.shape[0]
    r = nh * c
    fg = lb + (1.0 - lb) * _sigmoid(fr)
    logf = jnp.where(valid, jnp.log(fg), 0.0)
    k = jnp.where(valid, 1.0 - fg, 0.0)
    qs = jnp.where(valid, _silu(qr), 0.0)
    v = jnp.where(valid, ir, 0.0)
    b = _mask_mm(sums, sums_t, logf)
    mask = lambda n: pairs[n * r:(n + 1) * r]
    stack = lambda x: _heads_to_rows(x, nh)
    a = mask(0) * _bdot(stack(qs), stack(k), NT)
    for lvl, m in enumerate(HG_LEVELS):
        d = b - _hg_level_row(b, m)
        a = a + mask(1 + lvl) * _bdot(stack(qs * jnp.exp(jnp.minimum(d, 0.0))),
                                      stack(k * jnp.exp(jnp.minimum(-d, 0.0))), NT)
    av = _bdot(a, stack(v), NN)
    eb = jnp.exp(b)
    qe, kd = qs * eb, k * jnp.exp(b[c - 1:c] - b)
    outs, new_states = [], []
    for h in range(nh):
        cs = slice(h * HEAD_W, (h + 1) * HEAD_W)
        outs.append(_bdot(qe[:, cs], states[h], NT) + av[h * c:(h + 1) * c])
        new_states.append(states[h] * eb[c - 1:c, cs] + _bdot(v[:, cs], kd[:, cs], TN))
    return jnp.concatenate(outs, axis=1), new_states


def _hg_fwd(p1, lb, pad, *, name):
    lp = p1.shape[0]
    n = lp // CHUNK
    nh = HG_HEADS

    def body(q_ref, f_ref, i_ref, lb_ref, sums_ref, sums_t_ref, pairs_ref, o_ref, st_ref, s_ref):
        i = pl.program_id(1)

        @pl.when(i == 0)
        def _():
            s_ref[...] = jnp.zeros_like(s_ref)

        valid = (i * CHUNK + lax.broadcasted_iota(jnp.int32, (CHUNK, 1), 0)) >= pad
        s = s_ref[...]
        o, s2 = _hg_chunk(q_ref[...], f_ref[...], i_ref[...], lb_ref[...], [s[h] for h in range(grp)], valid,
                          sums_ref[...], sums_t_ref[...], pairs_ref[...])
        st_ref[...] = s
        o_ref[...] = o
        for h in range(grp):
            s_ref[h] = s2[h]

    masks = _hg_masks()
    grp, ngrp, gw = HG_GROUP, nh // HG_GROUP, HG_GROUP * HEAD_W
    blk = lambda off: pl.BlockSpec((CHUNK, gw), lambda h, i: (i, off + h))
    const = lambda a: pl.BlockSpec(a.shape, lambda h, i: (0, 0))
    return pl.pallas_call(
        body, name=name, grid=(ngrp, n),
        in_specs=[blk(0), blk(ngrp), blk(2 * ngrp), pl.BlockSpec((1, gw), lambda h, i: (0, h))]
        + [const(a) for a in masks],
        out_specs=[blk(0), pl.BlockSpec((grp, None, HEAD_W, HEAD_W), lambda h, i: (h, i, 0, 0))],
        out_shape=[jax.ShapeDtypeStruct((lp, nh * HEAD_W), F32), jax.ShapeDtypeStruct((nh, n, HEAD_W, HEAD_W), F32)],
        scratch_shapes=[pltpu.VMEM((grp, HEAD_W, HEAD_W), F32)],
        compiler_params=_cp(("parallel", "arbitrary")),
    )(p1, p1, p1, lb, *masks)


def _hg_bwd(p1, lb, states, do, pad, *, name, scatter=()):
    lp = p1.shape[0]
    n = lp // CHUNK
    nh = HG_HEADS
    ns = len(scatter)

    def body(q_ref, f_ref, i_ref, lb_ref, st_ref, do_ref, sums_ref, sums_t_ref, pairs_ref, *rest):
        s_ins, (dq_ref, df_ref, di_ref, dlb_ref) = rest[:ns], rest[ns:ns + 4]
        s_outs, ds_ref, s_sems = rest[ns + 4:2 * ns + 4], rest[2 * ns + 4], rest[2 * ns + 5:]
        step = pl.program_id(1)
        i = n - 1 - step
        if ns:
            s_start, s_finish = _scatter_phases(s_ins, s_outs, *s_sems)
            pl.when((pl.program_id(0) == 0) & (step == 0))(s_start)

        @pl.when(step == 0)
        def _():
            ds_ref[...] = jnp.zeros_like(ds_ref)
            dlb_ref[...] = jnp.zeros_like(dlb_ref)

        valid = (i * CHUNK + lax.broadcasted_iota(jnp.int32, (CHUNK, 1), 0)) >= pad
        fn = functools.partial(_hg_chunk, valid=valid, sums=sums_ref[...], sums_t=sums_t_ref[...],
                               pairs=pairs_ref[...])
        st, dst = st_ref[...], ds_ref[...]
        _, vjp = jax.vjp(fn, q_ref[...], f_ref[...], i_ref[...], lb_ref[...], [st[h] for h in range(grp)])
        dq, df, di, dlb, ds = vjp((do_ref[...], [dst[h] for h in range(grp)]))
        dq_ref[...] = dq
        df_ref[...] = df
        di_ref[...] = di
        dlb_ref[...] += dlb
        for h in range(grp):
            ds_ref[h] = ds[h]
        if ns:
            pl.when((pl.program_id(0) == ngrp - 1) & (step == n - 1))(s_finish)

    masks = _hg_masks()
    grp, ngrp, gw = HG_GROUP, nh // HG_GROUP, HG_GROUP * HEAD_W
    blk = lambda off: pl.BlockSpec((CHUNK, gw), lambda h, s: (n - 1 - s, off + h))
    const = lambda a: pl.BlockSpec(a.shape, lambda h, s: (0, 0))
    w = nh * HEAD_W
    return pl.pallas_call(
        body, name=name, grid=(ngrp, n),
        in_specs=[blk(0), blk(ngrp), blk(2 * ngrp), pl.BlockSpec((1, gw), lambda h, s: (0, h)),
                  pl.BlockSpec((grp, None, HEAD_W, HEAD_W), lambda h, s: (h, n - 1 - s, 0, 0)), blk(0)]
        + [const(a) for a in masks] + [_ANY] * ns,
        out_specs=[blk(0), blk(0), blk(0), pl.BlockSpec((1, gw), lambda h, s: (0, h))] + [_ANY] * ns,
        out_shape=[jax.ShapeDtypeStruct((lp, w), F32)] * 3 + [jax.ShapeDtypeStruct((1, w), F32)]
        + [jax.ShapeDtypeStruct(s.shape, s.dtype) for s in scatter],
        scratch_shapes=[pltpu.VMEM((grp, HEAD_W, HEAD_W), F32)] + (_scatter_scratch(ns) if ns else []),
        compiler_params=_cp(("arbitrary", "arbitrary"), has_side_effects=bool(ns)),
    )(p1, p1, p1, lb, states, do, *masks, *scatter)


SB_GROUP = 4
SB_FAR = -110.0


def _sb_cat(kind, first_key=0):
    r = lax.broadcasted_iota(jnp.int32, (SB_BLOCK, 2 * SB_BLOCK), 0)
    c = lax.broadcasted_iota(jnp.int32, (SB_BLOCK, 2 * SB_BLOCK), 1)
    tri = {"after": c < r, "incl": r <= c, "before": r < c}[kind]
    m = ((c >= SB_BLOCK) | tri) & (r >= first_key)
    return jnp.where(m, 1.0, 0.0).astype(BF16)


def _sb_cumsum(x, cat):
    return _dot(x.astype(BF16), cat, NN)


def _sb_logsig(z):
    e = jnp.exp(-jnp.abs(z))
    lse = jnp.where(e < 1e-4, e, jnp.log(1.0 + e))
    lsz = jnp.minimum(z, 0.0) - lse
    return lsz, lsz - z, e


def _sb_stack(x, scale=None):
    lane = lax.broadcasted_iota(jnp.int32, (1, HEAD_W), 1)
    if scale is not None:
        x = x * scale
    return jnp.concatenate([jnp.where(lane < SB_DH, x, 0.0), jnp.where(lane >= SB_DH, x, 0.0)], axis=0).astype(BF16)


def _sb_unstack(x):
    lane = lax.broadcasted_iota(jnp.int32, (1, HEAD_W), 1)
    return jnp.where(lane < SB_DH, x[:SB_BLOCK], x[SB_BLOCK:])


def _sb_fwd(p0, pad, *, name, gather=None):
    lp = p0.shape[0]
    nb = lp // SB_BLOCK
    npair = SB_HEADS // 2
    blk0 = AB_SB // HEAD_W
    scale = SB_DH ** -0.5
    gw = SB_GROUP * SB_BLOCK
    assert pad < SB_BLOCK
    g_srcs, g_dtypes = gather if gather is not None else ([], [])
    ng_arr = len(g_srcs)

    def body(q_ref, k_ref, v_ref, *rest):
        g_ins, (o_ref, tot_ref, nproc_ref) = rest[:ng_arr], rest[ng_arr:ng_arr + 3]
        g_outs, g_scratch = rest[ng_arr + 3:2 * ng_arr + 3], rest[2 * ng_arr + 3:]
        first_step = (pl.program_id(0) == 0) & (pl.program_id(1) == 0)
        last_pair = pl.program_id(0) == npair - 1
        if ng_arr:
            g_start, g_forward, g_finish = _gather_phases(g_ins, g_outs, g_scratch[:ng_arr], *g_scratch[ng_arr:],
                                                          g_dtypes)
            pl.when(first_step)(g_start)
            pl.when(last_pair & (pl.program_id(1) == 0))(g_forward)
        i = pl.program_id(1)
        qs = _sb_stack(q_ref[...], scale)
        qpos = i * SB_BLOCK + lax.broadcasted_iota(jnp.int32, (SB_BLOCK, 1), 0)
        qpos = jnp.concatenate([qpos, qpos], axis=0)
        cat = _sb_cat("after")
        cat0 = _sb_cat("after", pad)
        ng = i // SB_GROUP

        def group(off, nblk, first_cat, allowed, carry):
            acc, run = carry
            kg = k_ref[pl.ds(off, nblk * SB_BLOCK), :].astype(BF16)
            vg = v_ref[pl.ds(off, nblk * SB_BLOCK), :].astype(BF16)
            lsz, l1m, _ = _sb_logsig(_dot(qs, kg, NT))
            if allowed is not None:
                l1m = jnp.where(allowed, l1m, 0.0)
            args = [None] * nblk
            for g in reversed(range(nblk)):
                sl = slice(g * SB_BLOCK, (g + 1) * SB_BLOCK)
                al = _sb_cumsum(l1m[:, sl], first_cat if g == 0 else cat)
                args[g] = lsz[:, sl] + al[:, :SB_BLOCK] + run
                run = run + al[:, SB_BLOCK:]
            wgt = jnp.exp(jnp.concatenate(args, axis=1))
            if allowed is not None:
                wgt = jnp.where(allowed, wgt, 0.0)
            return acc + _dot(wgt.astype(BF16), vg, NN), run

        def below(t, carry):
            gi = ng - 1 - t
            return group(pl.multiple_of(gi * gw, gw), SB_GROUP, jnp.where(gi == 0, cat0, cat), None, carry)

        top = ng * gw

        def top_group(nblk, carry):
            off = pl.multiple_of(jnp.minimum(top, lp - nblk * SB_BLOCK), SB_BLOCK)
            kpos = off + lax.broadcasted_iota(jnp.int32, (1, nblk * SB_BLOCK), 1)
            return group(off, nblk, cat, (kpos < qpos) & (kpos >= pad) & (kpos >= top), carry)

        zero = (jnp.zeros((2 * SB_BLOCK, HEAD_W), F32), jnp.zeros((2 * SB_BLOCK, HEAD_W), F32))
        carry = lax.cond(i - ng * SB_GROUP < SB_GROUP // 2, functools.partial(top_group, SB_GROUP // 2),
                         functools.partial(top_group, SB_GROUP), zero)
        used, acc, run = lax.while_loop(lambda s: (s[0] < ng) & (jnp.max(s[2]) > SB_FAR),
                                        lambda s: (s[0] + 1, *below(s[0], (s[1], s[2]))), (jnp.int32(0), *carry))
        o_ref[...] = _sb_unstack(acc)
        tot_ref[...] = _sb_unstack(run)
        nproc_ref[pl.program_id(0), i] = used.astype(F32)
        if ng_arr:
            pl.when(last_pair & (pl.program_id(1) == nb - 1))(g_finish)

    full = lambda c0: pl.BlockSpec((lp, HEAD_W), lambda p, i: (0, c0 + p))
    out = pl.BlockSpec((SB_BLOCK, HEAD_W), lambda p, i: (i, p))
    return pl.pallas_call(
        body, name=name, grid=(npair, nb),
        in_specs=[pl.BlockSpec((SB_BLOCK, HEAD_W), lambda p, i: (i, blk0 + p)), full(blk0 + npair), full(blk0 + 2 * npair)]
        + [pl.BlockSpec(memory_space=pltpu.VMEM)] * ng_arr,
        out_specs=[out, out, pl.BlockSpec(memory_space=pltpu.SMEM)] + [_ANY] * ng_arr,
        out_shape=[jax.ShapeDtypeStruct((lp, npair * HEAD_W), F32)] * 2 + [jax.ShapeDtypeStruct((npair, nb), F32)]
        + _gather_out_shapes(g_srcs, g_dtypes),
        scratch_shapes=_gather_scratch(g_srcs, g_dtypes) if ng_arr else [],
        compiler_params=_cp(("arbitrary", "arbitrary"), has_side_effects=bool(ng_arr)),
    )(p0, p0, p0, *g_srcs)


def _sb_bwd(p0, tot, nproc, dsrc, d_blk0, pad, *, name, scatter=()):
    lp = p0.shape[0]
    nb = lp // SB_BLOCK
    npair = SB_HEADS // 2
    blk0 = AB_SB // HEAD_W
    scale = SB_DH ** -0.5
    gw = SB_GROUP * SB_BLOCK
    assert pad < SB_BLOCK
    ns = len(scatter)

    def body(q_ref, k_ref, v_ref, tot_ref, nproc_ref, do_ref, *rest):
        s_ins, (dq_ref, dkt_ref, dvt_ref) = rest[:ns], rest[ns:ns + 3]
        s_outs, s_sems = rest[ns + 3:2 * ns + 3], rest[2 * ns + 3:]
        if ns:
            s_start, s_finish = _scatter_phases(s_ins, s_outs, *s_sems)
            pl.when((pl.program_id(0) == 0) & (pl.program_id(1) == 0))(s_start)
        i = pl.program_id(1)

        @pl.when(i == 0)
        def _():
            dkt_ref[...] = jnp.zeros_like(dkt_ref)
            dvt_ref[...] = jnp.zeros_like(dvt_ref)

        qs = _sb_stack(q_ref[...], scale)
        dos = _sb_stack(do_ref[...])
        qst, dost = qs.T, dos.T
        totv = tot_ref[...]
        ones = jnp.ones((1, HEAD_W), F32)
        tots = jnp.concatenate([totv[:, 0:1] * ones, totv[:, SB_DH:SB_DH + 1] * ones], axis=0)
        qpos = i * SB_BLOCK + lax.broadcasted_iota(jnp.int32, (SB_BLOCK, 1), 0)
        qpos = jnp.concatenate([qpos, qpos], axis=0)
        incl, incl0 = _sb_cat("incl"), _sb_cat("incl", pad)
        before = _sb_cat("before")
        ng = i // SB_GROUP
        used = jnp.clip(nproc_ref[pl.program_id(0), i].astype(jnp.int32), 0, ng)

        def dscore(z, e, ev, dl1m):
            r = 1.0 / (1.0 + e)
            sg = jnp.where(z >= 0, r, e * r)
            return ev * (1.0 - sg) - dl1m * sg

        def group(off, nblk, first_incl, allowed, carry):
            dq, prun, erun = carry
            width = nblk * SB_BLOCK
            kg = k_ref[pl.ds(off, width), :].astype(BF16)
            vg = v_ref[pl.ds(off, width), :].astype(BF16)
            z = _dot(qs, kg, NT)
            lsz, l1m, e = _sb_logsig(z)
            if allowed is not None:
                l1m = jnp.where(allowed, l1m, 0.0)
            dwgt = _dot(dos, vg, NT)
            dzs = [None] * nblk
            wgts = [None] * nblk
            for g in range(nblk):
                sl = slice(g * SB_BLOCK, (g + 1) * SB_BLOCK)
                al = _sb_cumsum(l1m[:, sl], first_incl if g == 0 else incl)
                wgt = jnp.exp(jnp.minimum(lsz[:, sl] + (tots - prun - al[:, :SB_BLOCK]), 0.0))
                if allowed is not None:
                    wgt = jnp.where(allowed[:, sl], wgt, 0.0)
                prun = prun + al[:, SB_BLOCK:]
                ev = wgt * dwgt[:, sl]
                el = _sb_cumsum(ev, before)
                dzs[g] = dscore(z[:, sl], e[:, sl], ev, erun + el[:, :SB_BLOCK])
                erun = erun + el[:, SB_BLOCK:]
                wgts[g] = wgt
            dz = jnp.concatenate(dzs, axis=1)
            if allowed is not None:
                dz = jnp.where(allowed, dz, 0.0)
            dz = dz.astype(BF16)
            wg = jnp.concatenate(wgts, axis=1).astype(BF16)
            dkt_ref[:, pl.ds(off, width)] += _dot(qst, dz, NN)
            dvt_ref[:, pl.ds(off, width)] += _dot(dost, wg, NN)
            return dq + _dot(dz, kg, NN), prun, erun

        def below(gi, carry):
            return group(pl.multiple_of(gi * gw, gw), SB_GROUP, jnp.where(gi == 0, incl0, incl), None, carry)

        zero = tuple(jnp.zeros((2 * SB_BLOCK, HEAD_W), F32) for _ in range(3))
        carry = lax.fori_loop(ng - used, ng, below, zero)
        top = ng * gw

        def top_group(nblk, carry):
            off = pl.multiple_of(jnp.minimum(top, lp - nblk * SB_BLOCK), SB_BLOCK)
            kpos = off + lax.broadcasted_iota(jnp.int32, (1, nblk * SB_BLOCK), 1)
            return group(off, nblk, incl, (kpos < qpos) & (kpos >= pad) & (kpos >= top), carry)

        dq, _, _ = lax.cond(i - ng * SB_GROUP < SB_GROUP // 2, functools.partial(top_group, SB_GROUP // 2),
                            functools.partial(top_group, SB_GROUP), carry)
        dq_ref[...] = _sb_unstack(dq) * scale
        if ns:
            pl.when((pl.program_id(0) == npair - 1) & (pl.program_id(1) == nb - 1))(s_finish)

    full = lambda c0: pl.BlockSpec((lp, HEAD_W), lambda p, i: (0, c0 + p))
    qb = lambda c0: pl.BlockSpec((SB_BLOCK, HEAD_W), lambda p, i: (i, c0 + p))
    tr = pl.BlockSpec((HEAD_W, lp), lambda p, i: (p, 0))
    return pl.pallas_call(
        body, name=name, grid=(npair, nb),
        in_specs=[qb(blk0), full(blk0 + npair), full(blk0 + 2 * npair), qb(0), pl.BlockSpec(memory_space=pltpu.SMEM),
                  qb(d_blk0)] + [_ANY] * ns,
        out_specs=[qb(0), tr, tr] + [_ANY] * ns,
        out_shape=[jax.ShapeDtypeStruct((lp, npair * HEAD_W), F32)]
        + [jax.ShapeDtypeStruct((npair * HEAD_W, lp), F32)] * 2
        + [jax.ShapeDtypeStruct(s.shape, s.dtype) for s in scatter],
        scratch_shapes=_scatter_scratch(ns) if ns else [],
        compiler_params=_cp(("arbitrary", "arbitrary"), has_side_effects=bool(ns)),
    )(p0, p0, p0, tot, nproc, dsrc, *scatter)


def _local_step(h0, target, pad, wts, hooks=None):
    lp = h0.shape[0]
    tm = _row_tile(lp, 1056)
    tkl = tm
    tml = _row_tile(lp, 528)
    d = D_MODEL
    mm = _mm
    mmw = functools.partial(_mm, out_dtype=BF16)
    g = {}

    h0_b = h0.astype(BF16)
    p0 = mm(h0_b, wts["w_ab"], "NN", tm=tm, tn=768, tk=d, name="l0_in_proj")
    ob, sb_tot, sb_used, *gathered = _sb_fwd(p0, pad, name="sb_fwd", gather=hooks["gather_a"] if hooks else None)
    if hooks:
        wts = {**wts, **hooks["weights_a"](gathered)}
    qkv = _gdn_pre_fwd(p0, wts["conv_w"], pad, name="gdn_pre_fwd")
    oa_raw, gdn_states, *gathered = _gdn_fwd(qkv, p0, wts["alog_v"], wts["dtb_v"], pad, name="gdn_fwd",
                                             gather=hooks["gather_b"] if hooks else None)
    if hooks:
        second = hooks["weights_b"](gathered)
        wts = {**wts, **second, "w1": wts["w1"] + second["w1"], "w2": wts["w2"] + second["w2"]}
    rows = lambda a, n: a.reshape(N_DEV, n // N_DEV, d)
    parts = g["parts"] = {}
    oab = _gate_fwd(oa_raw, p0, AB_Z // HEAD_W, wts["ab_gn"], ob, heads=GDN_HEADS, name="gdn_gate_fwd")
    ln = lambda kind, layer: (wts[f"ln_{kind}_g"][layer], wts[f"ln_{kind}_b"][layer])
    pre_mix0, h0a, h0a_b = mm(oab, wts["w_out0"], "NN", tm=tml, tn=d, tk=d, epi="ln", c=h0, scale=DN_ALPHA,
                              ln=ln("mix", 0), name="l0_out_proj")
    u0, act0 = mm(h0a_b, wts["w1"][0], "NN", tm=tm, tn=512, tk=d, b_dev=True, epi="relu2_copy", name="mlp0_up")
    pre_ffn0, h0b, h0b_b = mm(act0, wts["w2"][0], "NN", tm=tml, tn=d, tk=d, epi="ln", c=h0a, scale=DN_ALPHA,
                              ln=ln("ffn", 0), name="mlp0_down")
    p1 = mm(h0b_b, wts["w_c"], "NN", tm=tm, tn=512, tk=d, b_dev=True, name="l1_in_proj")
    oc_raw, hg_states = _hg_fwd(p1, wts["lb"], pad, name="hg_fwd")
    oc = _gate_fwd(oc_raw, p1, 3 * HG_HEADS, wts["c_gn"], oc_raw, heads=HG_HEADS, name="hg_gate_fwd")
    pre_mix1, h1a, h1a_b = mm(oc, wts["w_out1"], "NN", tm=tml, tn=d, tk=d, epi="ln", c=h0b, scale=DN_ALPHA,
                              ln=ln("mix", 1), name="l1_out_proj")
    u1, act1 = mm(h1a_b, wts["w1"][1], "NN", tm=tm, tn=512, tk=d, b_dev=True, epi="relu2_copy", name="mlp1_up")
    pre_ffn1, h1b, _ = mm(act1, wts["w2"][1], "NN", tm=tml, tn=d, tk=d, epi="ln", c=h1a, scale=DN_ALPHA,
                          ln=ln("ffn", 1), name="mlp1_down")
    dy, loss_vec = _loss_head(h1b, target, name="loss_head")

    def mlp_bwd(layer, h_in_b, u, act, dpre, dpre_b):
        du = mm(dpre_b, wts["w2"][layer], "NT", tm=tm, tn=1024, tk=d, epi="relu2grad", c=u, out_dtype=BF16,
                name=f"mlp{layer}_d_hidden")
        dw2 = mmw(act, dpre_b, "TN", tm=1024, tn=1024, tk=tkl, name=f"mlp{layer}_dw2")
        dw1 = mmw(h_in_b, du, "TN", tm=1024, tn=512, tk=tkl, out_dev=True, name=f"mlp{layer}_dw1")
        dh = mm(du, k_major(wts["w1"][layer]), "NT", tm=tm, tn=1024, tk=2048, epi="add", c=dpre, scale=DN_ALPHA,
                name=f"mlp{layer}_d_in")
        return dh, dw1, dw2

    k_major = lambda wd: wd.transpose(1, 0, 2).reshape(wd.shape[1], -1)

    ln_ffn_dg, ln_ffn_db, ln_mix_dg, ln_mix_db, dw1s, dw2s = ([None, None] for _ in range(6))
    dpre, dpre_b, ln_ffn_dg[1], ln_ffn_db[1] = _ln_bwd(pre_ffn1, wts["ln_ffn_g"][1], dy, name="ln_ffn1_bwd")
    dh1a, dw1s[1], dw2s[1] = mlp_bwd(1, h1a_b, u1, act1, dpre, dpre_b)
    dpre, dpre_b, ln_mix_dg[1], ln_mix_db[1] = _ln_bwd(pre_mix1, wts["ln_mix_g"][1], dh1a, name="ln_mix1_bwd")
    g["c_w_out"] = mmw(oc, dpre_b, "TN", tm=1024, tn=1024, tk=tkl, name="l1_dw_out")
    doc = mm(dpre_b, wts["w_out1"], "NT", tm=tm, tn=1024, tk=d, name="l1_d_gate")
    doc_raw, dz1, g["c_gn"] = _gate_bwd(oc_raw, p1, 3 * HG_HEADS, wts["c_gn"], doc, heads=HG_HEADS, name="hg_gate_bwd")
    ready = [dw1s[1], rows(dw2s[1], D_FF), rows(g["c_w_out"], d)] if hooks else ()
    dq1, df1, di1, g["lb"], *got = _hg_bwd(p1, wts["lb"], hg_states, doc_raw, pad, name="hg_bwd", scatter=ready)
    parts.update(zip(("mlp_w1_1", "mlp_w2_1", "c_w_out"), got))
    dp1 = jnp.concatenate([dq1, df1, di1, dz1], axis=1).astype(BF16)
    g["c_w_in"] = mmw(h0b_b, dp1, "TN", tm=1024, tn=512, tk=tkl, out_dev=True, name="l1_dw_in")
    dh0b = mm(dp1, k_major(wts["w_c"]), "NT", tm=tm, tn=1024, tk=2048, epi="add", c=dpre, scale=DN_ALPHA,
              name="l1_d_in")
    dpre, dpre_b, ln_ffn_dg[0], ln_ffn_db[0] = _ln_bwd(pre_ffn0, wts["ln_ffn_g"][0], dh0b, name="ln_ffn0_bwd")
    dh0a, dw1s[0], dw2s[0] = mlp_bwd(0, h0a_b, u0, act0, dpre, dpre_b)
    dpre, dpre_b, ln_mix_dg[0], ln_mix_db[0] = _ln_bwd(pre_mix0, wts["ln_mix_g"][0], dh0a, name="ln_mix0_bwd")
    g["ab_w_out"] = mmw(oab, dpre_b, "TN", tm=1024, tn=1024, tk=tkl, name="l0_dw_out")
    doab = mm(dpre_b, wts["w_out0"], "NT", tm=tm, tn=1024, tk=d, name="l0_d_gate")
    doa_raw, dz0, g["ab_gn"] = _gate_bwd(oa_raw, p0, AB_Z // HEAD_W, wts["ab_gn"], doab, heads=GDN_HEADS,
                                         name="gdn_gate_bwd")
    ready = [g["c_w_in"]] if hooks else ()
    dqb, dkb_t, dvb_t, *got = _sb_bwd(p0, sb_tot, sb_used, doab, GDN_HEADS, pad, name="sb_bwd", scatter=ready)
    parts.update(zip(("c_w_in",), got))
    dkb, dvb = dkb_t.T, dvb_t.T
    ready = [dw1s[0], rows(dw2s[0], D_FF), rows(g["ab_w_out"], d)] if hooks else ()
    dqn, dkn, dvn, dba, g["alog_v"], g["dtb_v"], *got = _gdn_bwd(qkv, p0, wts["alog_v"], wts["dtb_v"], gdn_states,
                                                                 doa_raw, pad, name="gdn_bwd", scatter=ready)
    parts.update(zip(("mlp_w1_0", "mlp_w2_0", "ab_w_out"), got))
    dconv_in, g["conv_w"] = _gdn_pre_bwd(p0, wts["conv_w"], jnp.concatenate([dqn, dkn, dvn], axis=1), pad,
                                         name="gdn_pre_bwd")
    dp0 = jnp.concatenate([dconv_in, dz0, dqb, dkb, dvb, dba, jnp.zeros((lp, AB_CAT - AB_BA - HEAD_W), F32)],
                          axis=1).astype(BF16)
    g["w_ab"] = mmw(h0_b, dp0, "TN", tm=1024, tn=768, tk=tkl, name="l0_dw_in")
    last = ()
    if hooks:
        gab, ba0 = g["w_ab"], AB_Z + GDN_HEADS * HEAD_W
        gab = jnp.concatenate([gab[:, :ba0], gab[:, AB_BA:AB_BA + 2 * GDN_HEADS], gab[:, ba0:AB_BA]], axis=1)
        last = [gab.reshape(d, N_DEV, AB_IN // N_DEV).transpose(1, 0, 2)]
    res = mm(dp0, wts["w_ab"], "NT", tm=tm, tn=1024, tk=1920, epi="add", c=dpre, scale=DN_ALPHA, scatter=last,
             name="l0_d_in")
    dh0 = res[0] if last else res
    parts.update(zip(("ab_w_in",), res[1:] if last else ()))

    g["w1"], g["w2"] = dw1s, dw2s
    g["ln_mix_g"] = jnp.concatenate(ln_mix_dg, axis=0)
    g["ln_mix_b"] = jnp.concatenate(ln_mix_db, axis=0)
    g["ln_ffn_g"] = jnp.concatenate(ln_ffn_dg, axis=0)
    g["ln_ffn_b"] = jnp.concatenate(ln_ffn_db, axis=0)
    return loss_vec, dh0, g


N_CHIP = N_DEV // 2


def _place():
    x, y, c = lax.axis_index("x"), lax.axis_index("y"), lax.axis_index("c")
    return x, y, c, 2 * x + y


def _chip_dev(chip, core):
    return (chip // 2, chip % 2, core)


def _remote(src, dst, send_sem, recv_sem, dev):
    return pltpu.make_async_remote_copy(src_ref=src, dst_ref=dst, send_sem=send_sem, recv_sem=recv_sem,
                                        device_id=dev, device_id_type=pl.DeviceIdType.MESH)


_ANY = pl.BlockSpec(memory_space=pl.ANY)


def _gather(srcs, dtypes, *, name):
    n = len(srcs)

    def body(*refs):
        start, forward, finish = _gather_phases(refs[:n], refs[n:2 * n], refs[2 * n:3 * n], *refs[3 * n:], dtypes)
        start()
        forward()
        finish()

    return pl.pallas_call(
        body, name=name, in_specs=[pl.BlockSpec(memory_space=pltpu.VMEM)] * n, out_specs=[_ANY] * n,
        out_shape=_gather_out_shapes(srcs, dtypes), scratch_shapes=_gather_scratch(srcs, dtypes),
        compiler_params=_cp(has_side_effects=True),
    )(*srcs)


def _gather_out_shapes(srcs, dtypes):
    return [jax.ShapeDtypeStruct((N_DEV, *s.shape), dt) for s, dt in zip(srcs, dtypes)]


def _gather_scratch(srcs, dtypes):
    n = len(srcs)
    return [pltpu.VMEM(s.shape, dt) for s, dt in zip(srcs, dtypes)] + [
        pltpu.SemaphoreType.DMA((n, 2 * N_CHIP - 1)), pltpu.SemaphoreType.DMA((n, 2 * N_CHIP - 1)),
        pltpu.SemaphoreType.DMA((n,))]


def _gather_phases(ins, outs, stages, send_sems, recv_sems, local_sems, dtypes):
    n = len(ins)
    x, y, c, chip = _place()
    me = 2 * chip + c
    sibling = (x, y, 1 - c)

    def own(i):
        cps = [_remote(stages[i], outs[i].at[me], send_sems.at[i, 0], recv_sems.at[i, 0], sibling)]
        for j in range(1, N_CHIP):
            cps.append(_remote(stages[i], outs[i].at[me], send_sems.at[i, j], recv_sems.at[i, j],
                               _chip_dev(jnp.bitwise_xor(chip, j), c)))
        return cps

    def local(i):
        return pltpu.make_async_copy(stages[i], outs[i].at[me], local_sems.at[i])

    def passed_on(i, j):
        slot = outs[i].at[2 * jnp.bitwise_xor(chip, j) + c]
        return _remote(slot, slot, send_sems.at[i, N_CHIP - 1 + j], recv_sems.at[i, N_CHIP - 1 + j], sibling)

    def start():
        for i in range(n):
            stages[i][...] = ins[i][...].astype(dtypes[i])
            local(i).start()
            for cp in own(i):
                cp.start()

    def forward():
        for i in range(n):
            for j in range(1, N_CHIP):
                own(i)[j].wait_recv()
                passed_on(i, j).start()

    def finish():
        for i in range(n):
            own(i)[0].wait_recv()
            for j in range(1, N_CHIP):
                passed_on(i, j).wait_recv()
        for i in range(n):
            for cp in own(i):
                cp.wait_send()
            for j in range(1, N_CHIP):
                passed_on(i, j).wait_send()
            local(i).wait()

    return start, forward, finish


def _scatter_scratch(n):
    return [pltpu.SemaphoreType.DMA((n, N_DEV - 1)), pltpu.SemaphoreType.DMA((n, N_DEV - 1)),
            pltpu.SemaphoreType.DMA((n,))]


def _scatter_phases(ins, outs, send_sems, recv_sems, local_sems):
    n = len(ins)
    _, _, c, chip = _place()
    me = 2 * chip + c

    def copies():
        cps = []
        for i in range(n):
            cps.append(pltpu.make_async_copy(ins[i].at[me], outs[i].at[me], local_sems.at[i]))
            for k in range(1, N_DEV):
                peer = jnp.bitwise_xor(me, k)
                cps.append(_remote(ins[i].at[peer], outs[i].at[me], send_sems.at[i, k - 1], recv_sems.at[i, k - 1],
                                   _chip_dev(peer // 2, peer % 2)))
        return cps

    def start():
        for cp in copies():
            cp.start()

    def finish():
        for cp in copies():
            cp.wait()

    return start, finish


def _adamw(w, parts, m, v, *, name):
    r, c = w.shape
    s = parts.shape[0]
    tm = _row_tile(r, 128) if r % 8 == 0 else r
    c1 = 1.0 - ADAM_B1 ** ADAM_STEP
    c2 = 1.0 - ADAM_B2 ** ADAM_STEP

    def body(w_ref, p_ref, m_ref, v_ref, g_ref, d_ref, m2_ref, v2_ref):
        g = p_ref[0].astype(F32)
        for j in range(1, s):
            g = g + p_ref[j].astype(F32)
        m2 = ADAM_B1 * m_ref[...] + (1.0 - ADAM_B1) * g
        v2 = ADAM_B2 * v_ref[...] + (1.0 - ADAM_B2) * jnp.square(g)
        g_ref[...] = g
        m2_ref[...] = m2
        v2_ref[...] = v2
        d_ref[...] = -ADAM_LR * ((m2 / c1) / (jnp.sqrt(v2 / c2) + ADAM_EPS) + ADAM_WD * w_ref[...])

    blk = pl.BlockSpec((tm, c), lambda i: (i, 0))
    return pl.pallas_call(
        body, name=name, grid=(r // tm,),
        in_specs=[blk, pl.BlockSpec((s, tm, c), lambda i: (0, i, 0)), blk, blk], out_specs=[blk] * 4,
        out_shape=[jax.ShapeDtypeStruct((r, c), F32)] * 4, compiler_params=_cp(("parallel",)),
    )(w, parts, m, v)


_WEIGHTS = ("meta_tokens", "ab_w_in", "ab_conv_w", "ab_a_log", "ab_dt_bias", "ab_gnorm_g", "ab_w_out", "c_w_in",
            "c_lb_raw", "c_gnorm_g", "c_w_out", "ln_mix_g", "ln_mix_b", "mlp_w1", "mlp_w2", "ln_ffn_g", "ln_ffn_b")
_PACK_ROWS = (("ln_mix_g", 0), ("ln_mix_b", 2), ("ln_ffn_g", 4), ("ln_ffn_b", 6), ("c_lb_raw", 8))
_PACK_MISC_ROW = 10
_PACK_MISC = (("ab_gnorm_g", 0, 128), ("c_gnorm_g", 128, 128), ("ab_a_log", 256, GDN_HEADS), ("ab_dt_bias", 260, GDN_HEADS))
_PACK_N = 16
_SMALL_META = 16
_SMALL_CONV = 32
_SMALL_N = 40


def _pack_replicated(p):
    rows = jnp.zeros((_PACK_N, D_MODEL), F32)
    for name, r0 in _PACK_ROWS:
        rows = rows.at[r0:r0 + 2].set(p[name])
    for name, c0, width in _PACK_MISC:
        rows = rows.at[_PACK_MISC_ROW, c0:c0 + width].set(p[name].reshape(width))
    return rows


def _unpack_replicated(rows, like):
    out = {}
    for name, r0 in _PACK_ROWS:
        out[name] = rows[r0:r0 + 2]
    for name, c0, width in _PACK_MISC:
        out[name] = rows[_PACK_MISC_ROW, c0:c0 + width].reshape(like[name].shape)
    return out


def _lower_bound(c_lb_raw):
    lb_all = jnp.cumsum(jax.nn.softmax(c_lb_raw.astype(F32), axis=0), axis=0)
    return (lb_all - lb_all[0:1])[1].reshape(1, -1)


def kernel(x, meta_tokens, ab_w_in, ab_conv_w, ab_a_log, ab_dt_bias, ab_gnorm_g, ab_w_out, c_w_in, c_lb_raw, c_gnorm_g, c_w_out, ln_mix_g, ln_mix_b, mlp_w1, mlp_w2, ln_ffn_g, ln_ffn_b, loss_target, m_meta_tokens, m_ab_w_in, m_ab_conv_w, m_ab_a_log, m_ab_dt_bias, m_ab_gnorm_g, m_ab_w_out, m_c_w_in, m_c_lb_raw, m_c_gnorm_g, m_c_w_out, m_ln_mix_g, m_ln_mix_b, m_mlp_w1, m_mlp_w2, m_ln_ffn_g, m_ln_ffn_b, v_meta_tokens, v_ab_w_in, v_ab_conv_w, v_ab_a_log, v_ab_dt_bias, v_ab_gnorm_g, v_ab_w_out, v_c_w_in, v_c_lb_raw, v_c_gnorm_g, v_c_w_out, v_ln_mix_g, v_ln_mix_b, v_mlp_w1, v_mlp_w2, v_ln_ffn_g, v_ln_ffn_b):
    w = dict(zip(_WEIGHTS, (meta_tokens, ab_w_in, ab_conv_w, ab_a_log, ab_dt_bias, ab_gnorm_g, ab_w_out, c_w_in, c_lb_raw,
                            c_gnorm_g, c_w_out, ln_mix_g, ln_mix_b, mlp_w1, mlp_w2, ln_ffn_g, ln_ffn_b)))
    mom = dict(zip(_WEIGHTS, (m_meta_tokens, m_ab_w_in, m_ab_conv_w, m_ab_a_log, m_ab_dt_bias, m_ab_gnorm_g, m_ab_w_out,
                              m_c_w_in, m_c_lb_raw, m_c_gnorm_g, m_c_w_out, m_ln_mix_g, m_ln_mix_b, m_mlp_w1, m_mlp_w2,
                              m_ln_ffn_g, m_ln_ffn_b)))
    var = dict(zip(_WEIGHTS, (v_meta_tokens, v_ab_w_in, v_ab_conv_w, v_ab_a_log, v_ab_dt_bias, v_ab_gnorm_g, v_ab_w_out,
                              v_c_w_in, v_c_lb_raw, v_c_gnorm_g, v_c_w_out, v_ln_mix_g, v_ln_mix_b, v_mlp_w1, v_mlp_w2,
                              v_ln_ffn_g, v_ln_ffn_b)))
    me = 4 * lax.axis_index("x") + 2 * lax.axis_index("y") + lax.axis_index("c")
    seq = x.shape[1]
    pad = (-(N_META + seq)) % SB_BLOCK
    lp = pad + N_META + seq
    meta_w = D_MODEL // N_DEV
    conv_w_all = 2 * GDN_HEADS * HEAD_W + GDN_HEADS * HEAD_W
    conv_w_mine = conv_w_all // N_DEV

    g_meta, g_conv, g_ab_in = _gather([w["meta_tokens"], w["ab_conv_w"][0], w["ab_w_in"][0]], [F32, F32, BF16],
                                      name="gather_weights_first")
    meta_full = g_meta.transpose(1, 0, 2).reshape(N_META, D_MODEL)
    conv_full = g_conv.transpose(1, 0, 2).reshape(CONV_K, conv_w_all)
    ab_full = g_ab_in.transpose(1, 0, 2).reshape(D_MODEL, AB_IN)
    ba0 = AB_Z + 512
    w_ab = jnp.concatenate([ab_full[:, :ba0], ab_full[:, ba0 + 2 * GDN_HEADS:], ab_full[:, ba0:ba0 + 2 * GDN_HEADS],
                            jnp.zeros((D_MODEL, AB_CAT - AB_IN), BF16)], axis=1)
    vec128 = lambda p: jnp.zeros((1, HEAD_W), F32).at[0, :GDN_HEADS].set(p.reshape(GDN_HEADS))
    wts = dict(
        w_ab=w_ab, conv_w=conv_full, alog_v=vec128(w["ab_a_log"]), dtb_v=vec128(w["ab_dt_bias"]),
        ab_gn=w["ab_gnorm_g"][0], lb=_lower_bound(w["c_lb_raw"]), c_gn=w["c_gnorm_g"][0],
        ln_mix_g=w["ln_mix_g"], ln_mix_b=w["ln_mix_b"], ln_ffn_g=w["ln_ffn_g"], ln_ffn_b=w["ln_ffn_b"])

    def weights_a(gathered):
        g_ab_out, g_w1, g_w2 = gathered
        return dict(w_out0=g_ab_out.reshape(D_MODEL, D_MODEL), w1=[g_w1], w2=[g_w2.reshape(D_FF, D_MODEL)])

    def weights_b(gathered):
        g_c_in, g_c_out, g_w1, g_w2 = gathered
        return dict(w_c=g_c_in, w_out1=g_c_out.reshape(D_MODEL, D_MODEL), w1=[g_w1], w2=[g_w2.reshape(D_FF, D_MODEL)])

    hooks = dict(
        gather_a=([w["ab_w_out"][0], w["mlp_w1"][0], w["mlp_w2"][0]], [BF16] * 3), weights_a=weights_a,
        gather_b=([w["c_w_in"][0], w["c_w_out"][0], w["mlp_w1"][1], w["mlp_w2"][1]], [BF16] * 4), weights_b=weights_b)

    h0 = jnp.concatenate([jnp.zeros((pad, D_MODEL), F32), meta_full, x[0]], axis=0)
    loss_vec, dh0, g = _local_step(h0, loss_target[0], pad, wts, hooks)
    loss = lax.psum(jnp.sum(loss_vec), ("x", "y", "c"))
    grad_x = dh0[lp - seq:][None]

    _, lb_vjp = jax.vjp(_lower_bound, w["c_lb_raw"])
    rep_part = _pack_replicated(dict(
        ln_mix_g=g["ln_mix_g"], ln_mix_b=g["ln_mix_b"], ln_ffn_g=g["ln_ffn_g"], ln_ffn_b=g["ln_ffn_b"],
        c_lb_raw=lb_vjp(g["lb"])[0], ab_gnorm_g=g["ab_gn"], c_gnorm_g=g["c_gn"],
        ab_a_log=g["alog_v"][0, :GDN_HEADS], ab_dt_bias=g["dtb_v"][0, :GDN_HEADS]))
    small = jnp.concatenate([rep_part, dh0[pad:pad + N_META], g["conv_w"].reshape(-1, D_MODEL),
                             jnp.zeros((_SMALL_N - _SMALL_CONV - CONV_K * conv_w_all // D_MODEL, D_MODEL), F32)], axis=0)
    (small_all,) = _gather([small], [F32], name="gather_small_grads")
    rep_out = _adamw(_pack_replicated(w), small_all[:, :_PACK_N], _pack_replicated(mom), _pack_replicated(var),
                     name="adamw_replicated")
    meta_parts = lax.dynamic_slice_in_dim(small_all[:, _SMALL_META:_SMALL_META + N_META], me * meta_w, meta_w, axis=2)
    meta_out = _adamw(w["meta_tokens"], meta_parts, mom["meta_tokens"], var["meta_tokens"], name="adamw_meta")
    conv_parts = small_all[:, _SMALL_CONV:_SMALL_CONV + CONV_K * conv_w_all // D_MODEL].reshape(N_DEV, CONV_K, conv_w_all)
    conv_parts = lax.dynamic_slice_in_dim(conv_parts, me * conv_w_mine, conv_w_mine, axis=2)
    conv_out = _adamw(w["ab_conv_w"][0], conv_parts, mom["ab_conv_w"][0], var["ab_conv_w"][0], name="adamw_conv")

    parts = g["parts"]
    big = [("ab_w_in", 0, parts["ab_w_in"]), ("ab_w_out", 0, parts["ab_w_out"]), ("mlp_w1", 0, parts["mlp_w1_0"]),
           ("mlp_w2", 0, parts["mlp_w2_0"]), ("c_w_in", 0, parts["c_w_in"]), ("c_w_out", 0, parts["c_w_out"]),
           ("mlp_w1", 1, parts["mlp_w1_1"]), ("mlp_w2", 1, parts["mlp_w2_1"])]
    big_out = {}
    for name, l, p in big:
        res = _adamw(w[name][l], p, mom[name][l], var[name][l], name=f"adamw_{name}{l}")
        big_out.setdefault(name, []).append(res)

    rep = [_unpack_replicated(r, w) for r in rep_out]
    outs = {}
    for name in _WEIGHTS:
        if name == "meta_tokens":
            outs[name] = list(meta_out)
        elif name == "ab_conv_w":
            outs[name] = [o[None] for o in conv_out]
        elif name in big_out:
            res = big_out[name]
            outs[name] = [o[None] for o in res[0]] if len(res) == 1 else [jnp.stack(pair) for pair in zip(*res)]
        else:
            outs[name] = [r[name] for r in rep]
    flat = [loss, grad_x]
    for kind in range(4):
        flat += [outs[name][kind] for name in _WEIGHTS]
    return tuple(flat)
```

```python
import functools

import jax
import jax.numpy as jnp
from jax import lax
from jax.experimental import pallas as pl
from jax.experimental.pallas import tpu as pltpu

F32 = jnp.float32
BF16 = jnp.bfloat16

N_DEV = 8
D_MODEL = 1024
N_META = 16
D_FF = 4096
DEPTH = 2
GDN_HEADS = 4
SB_HEADS = 8
SB_DH = 64
HG_HEADS = 8
HEAD_W = 128
CHUNK = 64
SB_BLOCK = 128
CONV_K = 4
DN_ALPHA = float((2 * DEPTH) ** 0.25)
LN_EPS = 1e-5
RMS_EPS = 1e-6
L2_EPS = 1e-6
ADAM_LR, ADAM_B1, ADAM_B2, ADAM_EPS, ADAM_WD, ADAM_STEP = 0.001, 0.9, 0.999, 1e-08, 0.01, 10

AB_Z = 1536
AB_SB = 2048
AB_BA = 3584
AB_CAT = 3840
AB_IN = 3592

VMEM_LIMIT = 56 * 1024 * 1024


def _cp(sem=None, **kw):
    if sem is not None:
        kw["dimension_semantics"] = sem
    return pltpu.CompilerParams(vmem_limit_bytes=VMEM_LIMIT, **kw)


def _row_tile(n, want):
    best = 8
    for t in range(8, min(n, want) + 1, 8):
        if n % t == 0:
            best = t
    return best


@jax.custom_vjp
def _sigmoid(x):
    e = jnp.exp(-jnp.abs(x))
    r = 1.0 / (1.0 + e)
    return jnp.where(x >= 0, r, e * r)


def _sigmoid_fwd(x):
    s = _sigmoid(x)
    return s, s


def _sigmoid_bwd(s, g):
    return (g * s * (1.0 - s),)


_sigmoid.defvjp(_sigmoid_fwd, _sigmoid_bwd)


def _log1p_exp_neg_abs(x):
    e = jnp.exp(-jnp.abs(x))
    return jnp.where(e < 1e-4, e - 0.5 * e * e, jnp.log(1.0 + e))


@jax.custom_vjp
def _softplus(x):
    return jnp.maximum(x, 0.0) + _log1p_exp_neg_abs(x)


def _softplus_fwd(x):
    return _softplus(x), x


def _softplus_bwd(x, g):
    return (g * _sigmoid(x),)


_softplus.defvjp(_softplus_fwd, _softplus_bwd)


def _silu(x):
    return x * _sigmoid(x)


def _silu_grad(x):
    s = _sigmoid(x)
    return s * (1.0 + x * (1.0 - s))


def _dot(a, b, dims, precision=None):
    return lax.dot_general(a, b, (dims, ((), ())), precision=precision, preferred_element_type=F32)


NN = ((1,), (0,))
NT = ((1,), (1,))
TN = ((0,), (0,))


def _bdot(a, b, dims):
    return _dot(a.astype(BF16), b.astype(BF16), dims)


def _layer_norm(pre, g, beta):
    mu = jnp.mean(pre, axis=-1, keepdims=True)
    xc = pre - mu
    var = jnp.mean(xc * xc, axis=-1, keepdims=True)
    return xc * lax.rsqrt(var + LN_EPS) * g + beta


def _layer_norm_bwd(pre, g, dy):
    mu = jnp.mean(pre, axis=-1, keepdims=True)
    xc = pre - mu
    rstd = lax.rsqrt(jnp.mean(xc * xc, axis=-1, keepdims=True) + LN_EPS)
    xhat = xc * rstd
    dxh = dy * g
    m1 = jnp.mean(dxh, axis=-1, keepdims=True)
    m2 = jnp.mean(dxh * xhat, axis=-1, keepdims=True)
    return (rstd * (dxh - m1 - xhat * m2), jnp.sum(dy * xhat, axis=0, keepdims=True),
            jnp.sum(dy, axis=0, keepdims=True))


def _mm(a, b, mode, *, tm, tn, tk, name, epi=None, c=None, scale=1.0, b_dev=False, out_dev=False, out_dtype=F32,
        ln=None, scatter=()):
    if mode == "NN":
        m, kk = a.shape
        n = b.shape[2] * N_DEV if b_dev else b.shape[1]
    elif mode == "NT":
        m, kk = a.shape
        n = b.shape[1] if b_dev else b.shape[0]
    else:
        kk, m = a.shape
        n = b.shape[1]
    assert m % tm == 0 and n % tn == 0 and kk % tk == 0, (name, m, n, kk, tm, tn, tk)
    nk = kk // tk
    dims = {"NN": NN, "NT": NT, "TN": TN}[mode]

    if mode == "TN":
        a_spec = pl.BlockSpec((tk, tm), lambda i, j, k: (k, i))
    else:
        a_spec = pl.BlockSpec((tm, tk), lambda i, j, k: (i, k))
    if mode == "NN":
        if b_dev:
            assert tn == b.shape[2]
            b_spec = pl.BlockSpec((None, tk, tn), lambda i, j, k: (j, k, 0))
        else:
            b_spec = pl.BlockSpec((tk, tn), lambda i, j, k: (k, j))
    elif mode == "NT":
        if b_dev:
            assert tk == b.shape[2]
            b_spec = pl.BlockSpec((None, tn, tk), lambda i, j, k: (k, j, 0))
        else:
            b_spec = pl.BlockSpec((tn, tk), lambda i, j, k: (j, k))
    else:
        b_spec = pl.BlockSpec((tk, tn), lambda i, j, k: (k, j))
    in_specs = [a_spec, b_spec]
    operands = [a, b]
    if c is not None:
        in_specs.append(pl.BlockSpec((tm, tn), lambda i, j, k: (i, j)))
        operands.append(c)
    if epi == "ln":
        assert tn == n and not out_dev
        in_specs += [pl.BlockSpec((1, n), lambda i, j, k: (0, 0))] * 2
        operands += [ln[0].reshape(1, n), ln[1].reshape(1, n)]
    elif epi == "ln_bwd":
        assert tn == n and not out_dev
        in_specs += [pl.BlockSpec((tm, tn), lambda i, j, k: (i, j)), pl.BlockSpec((1, n), lambda i, j, k: (0, 0))]
        operands += [ln[0], ln[1].reshape(1, n)]
    if out_dev:
        assert tn == n // N_DEV
        out_shape = jax.ShapeDtypeStruct((N_DEV, m, tn), out_dtype)
        out_spec = pl.BlockSpec((None, tm, tn), lambda i, j, k: (j, i, 0))
    else:
        out_shape = jax.ShapeDtypeStruct((m, n), out_dtype)
        out_spec = pl.BlockSpec((tm, tn), lambda i, j, k: (i, j))
    if epi == "ln":
        out_shape = [out_shape, out_shape, jax.ShapeDtypeStruct((m, n), BF16)]
        out_spec = [out_spec] * 3
    elif epi == "relu2_copy":
        assert not out_dev
        out_shape = [out_shape, jax.ShapeDtypeStruct((m, n), BF16)]
        out_spec = [out_spec] * 2
    elif epi == "ln_bwd":
        vec_shape, vec_spec = jax.ShapeDtypeStruct((1, n), F32), pl.BlockSpec((1, n), lambda i, j, k: (0, 0))
        out_shape = [out_shape, jax.ShapeDtypeStruct((m, n), BF16), vec_shape, vec_shape]
        out_spec = [out_spec, out_spec, vec_spec, vec_spec]
    n_out = {"ln": 3, "relu2_copy": 2, "ln_bwd": 4}.get(epi, 1)
    ns = len(scatter)
    if ns:
        in_specs += [_ANY] * ns
        operands += list(scatter)
        out_shape = (out_shape if n_out > 1 else [out_shape]) + [jax.ShapeDtypeStruct(s.shape, s.dtype) for s in scatter]
        out_spec = (out_spec if n_out > 1 else [out_spec]) + [_ANY] * ns
    n_in = len(operands)
    grid = (m // tm, n // tn, nk)

    def body(*refs):
        a_ref, b_ref = refs[0], refs[1]
        c_ref = refs[2] if c is not None else None
        o_ref = refs[n_in]
        scratch0 = n_in + n_out + ns
        acc_ref = refs[scratch0] if nk > 1 else None
        if ns:
            s_start, s_finish = _scatter_phases(refs[n_in - ns:n_in], refs[n_in + n_out:scratch0],
                                                *refs[scratch0 + (1 if nk > 1 else 0):])
            at = lambda step: functools.reduce(lambda x, y: x & y, [pl.program_id(ax) == step[ax] for ax in range(3)])
            pl.when(at((0, 0, 0)))(s_start)
        p = _dot(a_ref[...].astype(BF16), b_ref[...].astype(BF16), dims)
        first_rows = pl.program_id(0) == 0

        def finish(acc):
            if epi == "add":
                acc = acc + scale * c_ref[...]
            elif epi == "relu2grad":
                acc = acc * (2.0 * jnp.maximum(c_ref[...], 0.0))
            elif epi == "relu2_copy":
                refs[n_in + 1][...] = jnp.square(jnp.maximum(acc, 0.0)).astype(BF16)
            elif epi == "ln_bwd":
                acc, dg, db = _layer_norm_bwd(refs[3][...], refs[4][...], acc + scale * c_ref[...])
                dg_ref, db_ref = refs[n_in + 2], refs[n_in + 3]

                @pl.when(first_rows)
                def _():
                    dg_ref[...] = jnp.zeros_like(dg_ref)
                    db_ref[...] = jnp.zeros_like(db_ref)

                dg_ref[...] += dg
                db_ref[...] += db
                refs[n_in + 1][...] = acc.astype(BF16)
            elif epi == "ln":
                acc = acc + scale * c_ref[...]
                y = _layer_norm(acc, refs[3][...], refs[4][...])
                refs[n_in + 1][...] = y
                refs[n_in + 2][...] = y.astype(BF16)
            o_ref[...] = acc.astype(out_dtype)

        if nk == 1:
            finish(p)
        else:
            k = pl.program_id(2)

            @pl.when(k == 0)
            def _():
                acc_ref[...] = p

            @pl.when(k > 0)
            def _():
                acc_ref[...] += p

            @pl.when(k == nk - 1)
            def _():
                finish(acc_ref[...])

        if ns:
            pl.when(at(tuple(g - 1 for g in grid)))(s_finish)

    res = pl.pallas_call(
        body, name=name, grid=grid, in_specs=in_specs, out_specs=out_spec, out_shape=out_shape,
        scratch_shapes=([pltpu.VMEM((tm, tn), F32)] if nk > 1 else []) + (_scatter_scratch(ns) if ns else []),
        compiler_params=_cp(("arbitrary",) * 3 if ns or epi == "ln_bwd" else ("parallel", "parallel", "arbitrary"),
                            has_side_effects=bool(ns)),
    )(*operands)
    return res


def _ln_bwd(pre, g, dy, *, name):
    lp, d = pre.shape
    tm = _row_tile(lp, 512)

    def body(pre_ref, g_ref, dy_ref, dpre_ref, dpreb_ref, dg_ref, db_ref):
        dpre, dg, db = _layer_norm_bwd(pre_ref[...], g_ref[...], dy_ref[...])
        dpre_ref[...] = dpre
        dpreb_ref[...] = dpre.astype(BF16)

        @pl.when(pl.program_id(0) == 0)
        def _():
            dg_ref[...] = jnp.zeros_like(dg_ref)
            db_ref[...] = jnp.zeros_like(db_ref)

        dg_ref[...] += dg
        db_ref[...] += db

    row = pl.BlockSpec((tm, d), lambda i: (i, 0))
    vec = pl.BlockSpec((1, d), lambda i: (0, 0))
    return pl.pallas_call(
        body, name=name, grid=(lp // tm,), in_specs=[row, vec, row], out_specs=[row, row, vec, vec],
        out_shape=[jax.ShapeDtypeStruct((lp, d), F32), jax.ShapeDtypeStruct((lp, d), BF16),
                   jax.ShapeDtypeStruct((1, d), F32), jax.ShapeDtypeStruct((1, d), F32)],
        compiler_params=_cp(("arbitrary",)),
    )(pre, g.reshape(1, d), dy)


def _loss_head(y, target, *, name):
    lp, d = y.shape
    seq = target.shape[0]
    tm = SB_BLOCK
    first = (lp - seq) // tm
    assert (lp - seq) % tm == 0 and seq % tm == 0

    def body(y_ref, t_ref, dy_ref, loss_ref):
        i = pl.program_id(0)
        live = i >= first
        diff = jnp.where(live, y_ref[...] - t_ref[...], 0.0)
        dy_ref[...] = diff * (1.0 / d)

        @pl.when(i == 0)
        def _():
            loss_ref[...] = jnp.zeros_like(loss_ref)

        loss_ref[...] += jnp.sum(diff * diff, axis=0, keepdims=True) * (0.5 / d)

    return pl.pallas_call(
        body, name=name, grid=(lp // tm,),
        in_specs=[pl.BlockSpec((tm, d), lambda i: (i, 0)),
                  pl.BlockSpec((tm, d), lambda i: (jnp.maximum(i - first, 0), 0))],
        out_specs=[pl.BlockSpec((tm, d), lambda i: (i, 0)), pl.BlockSpec((1, d), lambda i: (0, 0))],
        out_shape=[jax.ShapeDtypeStruct((lp, d), F32), jax.ShapeDtypeStruct((1, d), F32)],
        compiler_params=_cp(("arbitrary",)),
    )(y, target)


def _gate_fwd(o, zsrc, z_blk0, g, other, *, heads, name):
    lp = o.shape[0]
    tm = _row_tile(lp, 512)
    w = heads * HEAD_W
    assert (z_blk0 * HEAD_W) % w == 0
    has_other = w < D_MODEL

    def body(o_ref, z_ref, g_ref, *rest):
        y_ref = rest[-1]
        gv = g_ref[...]
        for h in range(heads):
            cs = slice(h * HEAD_W, (h + 1) * HEAD_W)
            ov = o_ref[:, cs]
            r = lax.rsqrt(jnp.mean(ov * ov, axis=-1, keepdims=True) + RMS_EPS)
            y_ref[:, cs] = (ov * r * gv * _silu(z_ref[:, cs])).astype(BF16)
        if has_other:
            y_ref[:, w:] = rest[0][...].astype(BF16)

    row = lambda width, blk: pl.BlockSpec((tm, width), lambda i: (i, blk))
    return pl.pallas_call(
        body, name=name, grid=(lp // tm,),
        in_specs=[row(w, 0), row(w, z_blk0 * HEAD_W // w), pl.BlockSpec((1, HEAD_W), lambda i: (0, 0))]
        + ([row(D_MODEL - w, 0)] if has_other else []),
        out_specs=row(D_MODEL, 0), out_shape=jax.ShapeDtypeStruct((lp, D_MODEL), BF16),
        compiler_params=_cp(("parallel",)),
    )(o, zsrc, g.reshape(1, HEAD_W), *([other] if has_other else []))


def _gate_bwd(o, zsrc, z_blk0, g, dy, *, heads, name):
    lp = o.shape[0]
    tm = _row_tile(lp, 512)

    w = heads * HEAD_W
    assert (z_blk0 * HEAD_W) % w == 0

    def body(o_ref, z_ref, g_ref, dy_ref, do_ref, dz_ref, dg_ref):
        @pl.when(pl.program_id(0) == 0)
        def _():
            dg_ref[...] = jnp.zeros_like(dg_ref)

        gv = g_ref[...]
        dg = jnp.zeros((1, HEAD_W), F32)
        for h in range(heads):
            cs = slice(h * HEAD_W, (h + 1) * HEAD_W)
            ov, zv, dyv = o_ref[:, cs], z_ref[:, cs], dy_ref[:, cs]
            r = lax.rsqrt(jnp.mean(ov * ov, axis=-1, keepdims=True) + RMS_EPS)
            nrm = ov * r
            s = _silu(zv)
            dn = dyv * gv * s
            do_ref[:, cs] = r * (dn - nrm * jnp.mean(dn * nrm, axis=-1, keepdims=True))
            dz_ref[:, cs] = dyv * nrm * gv * _silu_grad(zv)
            dg = dg + jnp.sum(dyv * nrm * s, axis=0, keepdims=True)
        dg_ref[...] += dg

    row = lambda blk: pl.BlockSpec((tm, w), lambda i: (i, blk))
    vec = pl.BlockSpec((1, HEAD_W), lambda i: (0, 0))
    return pl.pallas_call(
        body, name=name, grid=(lp // tm,),
        in_specs=[row(0), row(z_blk0 * HEAD_W // w), vec, row(0)], out_specs=[row(0), row(0), vec],
        out_shape=[jax.ShapeDtypeStruct((lp, w), F32), jax.ShapeDtypeStruct((lp, w), F32),
                   jax.ShapeDtypeStruct((1, HEAD_W), F32)],
        compiler_params=_cp(("arbitrary",)),
    )(o, zsrc, g.reshape(1, HEAD_W), dy)


def _conv_taps(x, w):
    acc = w[CONV_K - 1:CONV_K, :] * x
    for k in range(CONV_K - 1):
        acc = acc + w[k:k + 1, :] * pltpu.roll(x, CONV_K - 1 - k, 0)
    return acc


def _gdn_pre_fwd(p0, conv_w, pad, *, name):
    lp = p0.shape[0]
    nq = GDN_HEADS
    qscale = HEAD_W ** -0.5

    def body(x_ref, w_ref, y_ref):
        j = pl.program_id(0)
        c = _conv_taps(x_ref[...], w_ref[...])
        s = _silu(c)
        r = lax.rsqrt(jnp.sum(s * s, axis=-1, keepdims=True) + L2_EPS)
        mult = jnp.where(j < nq, r * qscale, jnp.where(j < 2 * nq, r, 1.0))
        rows = lax.broadcasted_iota(jnp.int32, (lp, 1), 0)
        y_ref[...] = jnp.where(rows >= pad, s * mult, 0.0)

    return pl.pallas_call(
        body, name=name, grid=(3 * nq,),
        in_specs=[pl.BlockSpec((lp, HEAD_W), lambda j: (0, j)), pl.BlockSpec((CONV_K, HEAD_W), lambda j: (0, j))],
        out_specs=pl.BlockSpec((lp, HEAD_W), lambda j: (0, j)),
        out_shape=jax.ShapeDtypeStruct((lp, 3 * nq * HEAD_W), F32), compiler_params=_cp(("parallel",)),
    )(p0, conv_w)


def _gdn_pre_bwd(p0, conv_w, dqkv, pad, *, name):
    lp = p0.shape[0]
    nq = GDN_HEADS
    qscale = HEAD_W ** -0.5

    def body(x_ref, w_ref, dy_ref, dx_ref, dw_ref):
        j = pl.program_id(0)
        x, w = x_ref[...], w_ref[...]
        c = _conv_taps(x, w)
        s = _silu(c)
        r = lax.rsqrt(jnp.sum(s * s, axis=-1, keepdims=True) + L2_EPS)
        rows = lax.broadcasted_iota(jnp.int32, (lp, 1), 0)
        dy = jnp.where(rows >= pad, dy_ref[...], 0.0)
        nrm = s * r
        dn = dy * jnp.where(j < nq, qscale, 1.0)
        ds_norm = r * (dn - nrm * jnp.sum(nrm * dn, axis=-1, keepdims=True))
        ds = jnp.where(j < 2 * nq, ds_norm, dy)
        dc = ds * _silu_grad(c)
        dx = w[CONV_K - 1:CONV_K, :] * dc
        dws = [None] * CONV_K
        dws[CONV_K - 1] = jnp.sum(dc * x, axis=0, keepdims=True)
        for k in range(CONV_K - 1):
            sh = CONV_K - 1 - k
            dx = dx + w[k:k + 1, :] * pltpu.roll(dc, lp - sh, 0)
            dws[k] = jnp.sum(dc * pltpu.roll(x, sh, 0), axis=0, keepdims=True)
        dx_ref[...] = dx
        dw_ref[...] = jnp.concatenate(dws, axis=0)

    blk = pl.BlockSpec((lp, HEAD_W), lambda j: (0, j))
    wblk = pl.BlockSpec((CONV_K, HEAD_W), lambda j: (0, j))
    return pl.pallas_call(
        body, name=name, grid=(3 * nq,), in_specs=[blk, wblk, blk], out_specs=[blk, wblk],
        out_shape=[jax.ShapeDtypeStruct((lp, 3 * nq * HEAD_W), F32),
                   jax.ShapeDtypeStruct((CONV_K, 3 * nq * HEAD_W), F32)],
        compiler_params=_cp(("parallel",)),
    )(p0, conv_w, dqkv)


@jax.custom_vjp
def _inv_unit_lower(m):
    c = m.shape[0]
    eye = (lax.broadcasted_iota(jnp.int32, (c, c), 0) == lax.broadcasted_iota(jnp.int32, (c, c), 1)).astype(F32)
    x = eye - m
    p = m
    n = 2
    while n < CHUNK:
        p = _bdot(p, p, NN)
        x = x + _bdot(x, p, NN)
        n *= 2
    return x


def _inv_fwd(m):
    t = _inv_unit_lower(m)
    return t, t


def _inv_bwd(t, g):
    return (-_bdot(_bdot(t, g, TN), t, NT),)


_inv_unit_lower.defvjp(_inv_fwd, _inv_bwd)


def _heads_to_rows(x, nh):
    return jnp.concatenate([x[:, h * HEAD_W:(h + 1) * HEAD_W] for h in range(nh)], axis=0)


def _rows_to_heads(x, nh):
    c = x.shape[0] // nh
    return jnp.concatenate([x[h * c:(h + 1) * c] for h in range(nh)], axis=1)


def _gdn_chunk(q, k, v, ba, alog, dtb, states, valid):
    nh = GDN_HEADS
    c = q.shape[0]
    r = nh * c
    lane = lax.broadcasted_iota(jnp.int32, (1, HEAD_W), 1)
    pick = lambda x, l: jnp.sum(jnp.where(lane == l, x, 0.0), axis=-1, keepdims=True)
    beta = jnp.concatenate([jnp.where(valid, _sigmoid(pick(ba, h)), 0.0) for h in range(nh)], axis=0)
    g = jnp.concatenate(
        [jnp.where(valid, -jnp.exp(pick(alog, h)) * _softplus(pick(ba, nh + h) + pick(dtb, h)), 0.0) for h in range(nh)],
        axis=0)
    qs, ks, vs = _heads_to_rows(q, nh), _heads_to_rows(k, nh), _heads_to_rows(v, nh)
    rr = lax.broadcasted_iota(jnp.int32, (r, r), 0)
    cc = lax.broadcasted_iota(jnp.int32, (r, r), 1)
    same = (rr // c) == (cc // c)
    causal, strict = same & (cc <= rr), same & (cc < rr)
    lower = jnp.where(causal, 1.0, 0.0).astype(BF16)
    upper = jnp.where(same & (cc >= rr), 1.0, 0.0).astype(BF16)
    gcb = _mask_mm(lower, upper, g * jnp.ones((1, HEAD_W), F32))
    gc_col = jnp.concatenate([gcb] * (r // HEAD_W), axis=1)
    decay = jnp.where(causal, jnp.exp(jnp.minimum(gc_col - gc_col.T, 0.0)), 0.0)
    egc = jnp.exp(gcb)
    kb = ks * beta
    m = jnp.where(strict, _bdot(kb, ks, NT) * decay, 0.0)
    t = _inv_unit_lower(m)
    u = _bdot(t, vs * beta, NN)
    w = _bdot(t, kb * egc, NN)
    a = _bdot(qs, ks, NT) * decay
    rows = lambda x, h: x[h * c:(h + 1) * c]
    qe = qs * egc
    v_new = u - jnp.concatenate([_bdot(rows(w, h), states[h], NN) for h in range(nh)], axis=0)
    o = jnp.concatenate([_bdot(rows(qe, h), states[h], NN) for h in range(nh)], axis=0) + _bdot(a, v_new, NN)
    new_states = []
    for h in range(nh):
        gl = gcb[(h + 1) * c - 1:(h + 1) * c, :]
        k_dec = rows(ks, h) * jnp.exp(gl - rows(gcb, h))
        new_states.append(states[h] * jnp.exp(gl) + _bdot(k_dec, rows(v_new, h), TN))
    return _rows_to_heads(o, nh), new_states


def _gdn_fwd(qkv, p0, alog_v, dtb_v, pad, *, name, gather=None):
    lp = qkv.shape[0]
    n = lp // CHUNK
    nh = GDN_HEADS
    g_srcs, g_dtypes = gather if gather is not None else ([], [])
    ng_arr = len(g_srcs)

    def body(q_ref, k_ref, v_ref, ba_ref, al_ref, dt_ref, *rest):
        g_ins, (o_ref, st_ref) = rest[:ng_arr], rest[ng_arr:ng_arr + 2]
        g_outs, s_ref, g_scratch = rest[ng_arr + 2:2 * ng_arr + 2], rest[2 * ng_arr + 2], rest[2 * ng_arr + 3:]
        i = pl.program_id(0)
        if ng_arr:
            g_start, g_forward, g_finish = _gather_phases(g_ins, g_outs, g_scratch[:ng_arr], *g_scratch[ng_arr:],
                                                          g_dtypes)
            pl.when(i == 0)(g_start)
            pl.when(i == (3 * n) // 4)(g_forward)

        @pl.when(i == 0)
        def _():
            s_ref[...] = jnp.zeros_like(s_ref)

        valid = (i * CHUNK + lax.broadcasted_iota(jnp.int32, (CHUNK, 1), 0)) >= pad
        s = s_ref[...]
        o, s2 = _gdn_chunk(q_ref[...], k_ref[...], v_ref[...], ba_ref[...], al_ref[...], dt_ref[...],
                           [s[h] for h in range(nh)], valid)
        st_ref[...] = s
        o_ref[...] = o
        for h in range(nh):
            s_ref[h] = s2[h]
        if ng_arr:
            pl.when(i == n - 1)(g_finish)

    w = nh * HEAD_W
    vec = pl.BlockSpec((1, HEAD_W), lambda i: (0, 0))
    return pl.pallas_call(
        body, name=name, grid=(n,),
        in_specs=[pl.BlockSpec((CHUNK, w), lambda i: (i, 0)), pl.BlockSpec((CHUNK, w), lambda i: (i, 1)),
                  pl.BlockSpec((CHUNK, w), lambda i: (i, 2)), pl.BlockSpec((CHUNK, HEAD_W), lambda i: (i, AB_BA // HEAD_W)),
                  vec, vec] + [pl.BlockSpec(memory_space=pltpu.VMEM)] * ng_arr,
        out_specs=[pl.BlockSpec((CHUNK, w), lambda i: (i, 0)),
                   pl.BlockSpec((None, nh, HEAD_W, HEAD_W), lambda i: (i, 0, 0, 0))] + [_ANY] * ng_arr,
        out_shape=[jax.ShapeDtypeStruct((lp, w), F32), jax.ShapeDtypeStruct((n, nh, HEAD_W, HEAD_W), F32)]
        + _gather_out_shapes(g_srcs, g_dtypes),
        scratch_shapes=[pltpu.VMEM((nh, HEAD_W, HEAD_W), F32)] + (_gather_scratch(g_srcs, g_dtypes) if ng_arr else []),
        compiler_params=_cp(("arbitrary",), has_side_effects=bool(ng_arr)),
    )(qkv, qkv, qkv, p0, alog_v, dtb_v, *g_srcs)


def _gdn_bwd(qkv, p0, alog_v, dtb_v, states, do, pad, *, name, scatter=()):
    lp = qkv.shape[0]
    n = lp // CHUNK
    nh = GDN_HEADS
    ns = len(scatter)

    def body(q_ref, k_ref, v_ref, ba_ref, al_ref, dt_ref, st_ref, do_ref, *rest):
        s_ins, (dq_ref, dk_ref, dv_ref, dba_ref, dal_ref, ddt_ref) = rest[:ns], rest[ns:ns + 6]
        s_outs, ds_ref, s_sems = rest[ns + 6:2 * ns + 6], rest[2 * ns + 6], rest[2 * ns + 7:]
        step = pl.program_id(0)
        i = n - 1 - step
        if ns:
            s_start, s_finish = _scatter_phases(s_ins, s_outs, *s_sems)
            pl.when(step == 0)(s_start)

        @pl.when(step == 0)
        def _():
            ds_ref[...] = jnp.zeros_like(ds_ref)
            dal_ref[...] = jnp.zeros_like(dal_ref)
            ddt_ref[...] = jnp.zeros_like(ddt_ref)

        valid = (i * CHUNK + lax.broadcasted_iota(jnp.int32, (CHUNK, 1), 0)) >= pad
        st, dst = st_ref[...], ds_ref[...]
        fn = functools.partial(_gdn_chunk, valid=valid)
        _, vjp = jax.vjp(fn, q_ref[...], k_ref[...], v_ref[...], ba_ref[...], al_ref[...], dt_ref[...],
                         [st[h] for h in range(nh)])
        dq, dk, dv, dba, dal, ddt, ds = vjp((do_ref[...], [dst[h] for h in range(nh)]))
        dq_ref[...] = dq
        dk_ref[...] = dk
        dv_ref[...] = dv
        dba_ref[...] = dba
        dal_ref[...] += dal
        ddt_ref[...] += ddt
        for h in range(nh):
            ds_ref[h] = ds[h]
        if ns:
            pl.when(step == n - 1)(s_finish)

    w = nh * HEAD_W
    rev = lambda c: (lambda s: (n - 1 - s, c))
    vec = pl.BlockSpec((1, HEAD_W), lambda s: (0, 0))
    return pl.pallas_call(
        body, name=name, grid=(n,),
        in_specs=[pl.BlockSpec((CHUNK, w), rev(0)), pl.BlockSpec((CHUNK, w), rev(1)), pl.BlockSpec((CHUNK, w), rev(2)),
                  pl.BlockSpec((CHUNK, HEAD_W), rev(AB_BA // HEAD_W)), vec, vec,
                  pl.BlockSpec((None, nh, HEAD_W, HEAD_W), lambda s: (n - 1 - s, 0, 0, 0)),
                  pl.BlockSpec((CHUNK, w), rev(0))] + [_ANY] * ns,
        out_specs=[pl.BlockSpec((CHUNK, w), rev(0)), pl.BlockSpec((CHUNK, w), rev(0)), pl.BlockSpec((CHUNK, w), rev(0)),
                   pl.BlockSpec((CHUNK, HEAD_W), rev(0)), vec, vec] + [_ANY] * ns,
        out_shape=[jax.ShapeDtypeStruct((lp, w), F32)] * 3 + [jax.ShapeDtypeStruct((lp, HEAD_W), F32)]
        + [jax.ShapeDtypeStruct((1, HEAD_W), F32)] * 2 + [jax.ShapeDtypeStruct(s.shape, s.dtype) for s in scatter],
        scratch_shapes=[pltpu.VMEM((nh, HEAD_W, HEAD_W), F32)] + (_scatter_scratch(ns) if ns else []),
        compiler_params=_cp(("arbitrary",), has_side_effects=bool(ns)),
    )(qkv, qkv, qkv, p0, alog_v, dtb_v, states, do, *scatter)


HG_LEVELS = (32, 16, 8, 4, 2, 1)
HG_GROUP = 4


def _hg_masks():
    import numpy as np
    c = CHUNK
    t = np.arange(c)[:, None]
    j = np.arange(c)[None, :]
    sums = (j <= t).astype(np.float32)
    pairs = [j == t]
    for m in HG_LEVELS:
        p = (t // (2 * m)) * (2 * m)
        r = p + m
        pairs.append((t >= r) & (j < r) & (j >= p))
    pairs = np.concatenate([np.kron(np.eye(HG_GROUP), p) for p in pairs], axis=0).astype(np.float32)
    return jnp.asarray(sums, BF16), jnp.asarray(sums.T, BF16), jnp.asarray(pairs, F32)


def _hg_level_row(b, m):
    c, w = b.shape
    if m >= 8:
        return jnp.concatenate([jnp.broadcast_to(b[p + m:p + m + 1], (2 * m, w)) for p in range(0, c, 2 * m)], axis=0)
    tiles = b.reshape(c // 8, 8, w)
    sub = lax.broadcasted_iota(jnp.int32, (1, 8, 1), 1)
    out = None
    for r0 in range(m, 8, 2 * m):
        cand = jnp.broadcast_to(tiles[:, r0:r0 + 1, :], tiles.shape)
        out = cand if out is None else jnp.where(sub >= r0 - m, cand, out)
    return out.reshape(c, w)


def _split3(x):
    hi = x.astype(BF16)
    r1 = x - hi.astype(F32)
    mid = r1.astype(BF16)
    return hi, mid, (r1 - mid.astype(F32)).astype(BF16)


def _mask_mm_raw(m, x):
    return sum(_dot(m, part, NN) for part in _split3(x))


@jax.custom_vjp
def _mask_mm(m, mt, x):
    return _mask_mm_raw(m, x)


def _mask_mm_fwd(m, mt, x):
    return _mask_mm_raw(m, x), (m, mt)


def _mask_mm_bwd(res, g):
    m, mt = res
    return jnp.zeros_like(m), jnp.zeros_like(mt), _mask_mm_raw(mt, g)


_mask_mm.defvjp(_mask_mm_fwd, _mask_mm_bwd)


def _hg_chunk(qr, fr, ir, lb, states, valid, sums, sums_t, pairs):
    nh = HG_GROUP
    c = qr.shape[0]
    r = nh * c
    fg = lb + (1.0 - lb) * _sigmoid(fr)
    logf = jnp.where(valid, jnp.log(fg), 0.0)
    k = jnp.where(valid, 1.0 - fg, 0.0)
    qs = jnp.where(valid, _silu(qr), 0.0)
    v = jnp.where(valid, ir, 0.0)
    b = _mask_mm(sums, sums_t, logf)
    mask = lambda n: pairs[n * r:(n + 1) * r]
    stack = lambda x: _heads_to_rows(x, nh)
    a = mask(0) * _bdot(stack(qs), stack(k), NT)
    for lvl, m in enumerate(HG_LEVELS):
        d = b - _hg_level_row(b, m)
        a = a + mask(1 + lvl) * _bdot(stack(qs * jnp.exp(jnp.minimum(d, 0.0))),
                                      stack(k * jnp.exp(jnp.minimum(-d, 0.0))), NT)
    av = _bdot(a, stack(v), NN)
    eb = jnp.exp(b)
    qe, kd = qs * eb, k * jnp.exp(b[c - 1:c] - b)
    outs, new_states = [], []
    for h in range(nh):
        cs = slice(h * HEAD_W, (h + 1) * HEAD_W)
        outs.append(_bdot(qe[:, cs], states[h], NT) + av[h * c:(h + 1) * c])
        new_states.append(states[h] * eb[c - 1:c, cs] + _bdot(v[:, cs], kd[:, cs], TN))
    return jnp.concatenate(outs, axis=1), new_states


def _hg_fwd(p1, lb, pad, *, name):
    lp = p1.shape[0]
    n = lp // CHUNK
    nh = HG_HEADS

    def body(q_ref, f_ref, i_ref, lb_ref, sums_ref, sums_t_ref, pairs_ref, o_ref, st_ref, s_ref):
        i = pl.program_id(1)

        @pl.when(i == 0)
        def _():
            s_ref[...] = jnp.zeros_like(s_ref)

        valid = (i * CHUNK + lax.broadcasted_iota(jnp.int32, (CHUNK, 1), 0)) >= pad
        s = s_ref[...]
        o, s2 = _hg_chunk(q_ref[...], f_ref[...], i_ref[...], lb_ref[...], [s[h] for h in range(grp)], valid,
                          sums_ref[...], sums_t_ref[...], pairs_ref[...])
        st_ref[...] = s
        o_ref[...] = o
        for h in range(grp):
            s_ref[h] = s2[h]

    masks = _hg_masks()
    grp, ngrp, gw = HG_GROUP, nh // HG_GROUP, HG_GROUP * HEAD_W
    blk = lambda off: pl.BlockSpec((CHUNK, gw), lambda h, i: (i, off + h))
    const = lambda a: pl.BlockSpec(a.shape, lambda h, i: (0, 0))
    return pl.pallas_call(
        body, name=name, grid=(ngrp, n),
        in_specs=[blk(0), blk(ngrp), blk(2 * ngrp), pl.BlockSpec((1, gw), lambda h, i: (0, h))]
        + [const(a) for a in masks],
        out_specs=[blk(0), pl.BlockSpec((grp, None, HEAD_W, HEAD_W), lambda h, i: (h, i, 0, 0))],
        out_shape=[jax.ShapeDtypeStruct((lp, nh * HEAD_W), F32), jax.ShapeDtypeStruct((nh, n, HEAD_W, HEAD_W), F32)],
        scratch_shapes=[pltpu.VMEM((grp, HEAD_W, HEAD_W), F32)],
        compiler_params=_cp(("parallel", "arbitrary")),
    )(p1, p1, p1, lb, *masks)


def _hg_bwd(p1, lb, states, do, pad, *, name, scatter=()):
    lp = p1.shape[0]
    n = lp // CHUNK
    nh = HG_HEADS
    ns = len(scatter)

    def body(q_ref, f_ref, i_ref, lb_ref, st_ref, do_ref, sums_ref, sums_t_ref, pairs_ref, *rest):
        s_ins, (dq_ref, df_ref, di_ref, dlb_ref) = rest[:ns], rest[ns:ns + 4]
        s_outs, ds_ref, s_sems = rest[ns + 4:2 * ns + 4], rest[2 * ns + 4], rest[2 * ns + 5:]
        step = pl.program_id(1)
        i = n - 1 - step
        if ns:
            s_start, s_finish = _scatter_phases(s_ins, s_outs, *s_sems)
            pl.when((pl.program_id(0) == 0) & (step == 0))(s_start)

        @pl.when(step == 0)
        def _():
            ds_ref[...] = jnp.zeros_like(ds_ref)
            dlb_ref[...] = jnp.zeros_like(dlb_ref)

        valid = (i * CHUNK + lax.broadcasted_iota(jnp.int32, (CHUNK, 1), 0)) >= pad
        fn = functools.partial(_hg_chunk, valid=valid, sums=sums_ref[...], sums_t=sums_t_ref[...],
                               pairs=pairs_ref[...])
        st, dst = st_ref[...], ds_ref[...]
        _, vjp = jax.vjp(fn, q_ref[...], f_ref[...], i_ref[...], lb_ref[...], [st[h] for h in range(grp)])
        dq, df, di, dlb, ds = vjp((do_ref[...], [dst[h] for h in range(grp)]))
        dq_ref[...] = dq
        df_ref[...] = df
        di_ref[...] = di
        dlb_ref[...] += dlb
        for h in range(grp):
            ds_ref[h] = ds[h]
        if ns:
            pl.when((pl.program_id(0) == ngrp - 1) & (step == n - 1))(s_finish)

    masks = _hg_masks()
    grp, ngrp, gw = HG_GROUP, nh // HG_GROUP, HG_GROUP * HEAD_W
    blk = lambda off: pl.BlockSpec((CHUNK, gw), lambda h, s: (n - 1 - s, off + h))
    const = lambda a: pl.BlockSpec(a.shape, lambda h, s: (0, 0))
    w = nh * HEAD_W
    return pl.pallas_call(
        body, name=name, grid=(ngrp, n),
        in_specs=[blk(0), blk(ngrp), blk(2 * ngrp), pl.BlockSpec((1, gw), lambda h, s: (0, h)),
                  pl.BlockSpec((grp, None, HEAD_W, HEAD_W), lambda h, s: (h, n - 1 - s, 0, 0)), blk(0)]
        + [const(a) for a in masks] + [_ANY] * ns,
        out_specs=[blk(0), blk(0), blk(0), pl.BlockSpec((1, gw), lambda h, s: (0, h))] + [_ANY] * ns,
        out_shape=[jax.ShapeDtypeStruct((lp, w), F32)] * 3 + [jax.ShapeDtypeStruct((1, w), F32)]
        + [jax.ShapeDtypeStruct(s.shape, s.dtype) for s in scatter],
        scratch_shapes=[pltpu.VMEM((grp, HEAD_W, HEAD_W), F32)] + (_scatter_scratch(ns) if ns else []),
        compiler_params=_cp(("arbitrary", "arbitrary"), has_side_effects=bool(ns)),
    )(p1, p1, p1, lb, states, do, *masks, *scatter)


SB_GROUP = 4
SB_FAR = -110.0


def _sb_cat(kind, first_key=0):
    r = lax.broadcasted_iota(jnp.int32, (SB_BLOCK, 2 * SB_BLOCK), 0)
    c = lax.broadcasted_iota(jnp.int32, (SB_BLOCK, 2 * SB_BLOCK), 1)
    tri = {"after": c < r, "incl": r <= c, "before": r < c}[kind]
    m = ((c >= SB_BLOCK) | tri) & (r >= first_key)
    return jnp.where(m, 1.0, 0.0).astype(BF16)


def _sb_cumsum(x, cat):
    return _dot(x.astype(BF16), cat, NN)


def _sb_logsig(z):
    e = jnp.exp(-jnp.abs(z))
    lse = jnp.where(e < 1e-4, e, jnp.log(1.0 + e))
    lsz = jnp.minimum(z, 0.0) - lse
    return lsz, lsz - z, e


def _sb_stack(x, scale=None):
    lane = lax.broadcasted_iota(jnp.int32, (1, HEAD_W), 1)
    if scale is not None:
        x = x * scale
    return jnp.concatenate([jnp.where(lane < SB_DH, x, 0.0), jnp.where(lane >= SB_DH, x, 0.0)], axis=0).astype(BF16)


def _sb_unstack(x):
    lane = lax.broadcasted_iota(jnp.int32, (1, HEAD_W), 1)
    return jnp.where(lane < SB_DH, x[:SB_BLOCK], x[SB_BLOCK:])


def _sb_fwd(p0, pad, *, name, gather=None):
    lp = p0.shape[0]
    nb = lp // SB_BLOCK
    npair = SB_HEADS // 2
    blk0 = AB_SB // HEAD_W
    scale = SB_DH ** -0.5
    gw = SB_GROUP * SB_BLOCK
    assert pad < SB_BLOCK
    g_srcs, g_dtypes = gather if gather is not None else ([], [])
    ng_arr = len(g_srcs)

    def body(q_ref, k_ref, v_ref, *rest):
        g_ins, (o_ref, tot_ref, nproc_ref) = rest[:ng_arr], rest[ng_arr:ng_arr + 3]
        g_outs, g_scratch = rest[ng_arr + 3:2 * ng_arr + 3], rest[2 * ng_arr + 3:]
        first_step = (pl.program_id(0) == 0) & (pl.program_id(1) == 0)
        last_pair = pl.program_id(0) == npair - 1
        if ng_arr:
            g_start, g_forward, g_finish = _gather_phases(g_ins, g_outs, g_scratch[:ng_arr], *g_scratch[ng_arr:],
                                                          g_dtypes)
            pl.when(first_step)(g_start)
            pl.when(last_pair & (pl.program_id(1) == 0))(g_forward)
        i = pl.program_id(1)
        qs = _sb_stack(q_ref[...], scale)
        qpos = i * SB_BLOCK + lax.broadcasted_iota(jnp.int32, (SB_BLOCK, 1), 0)
        qpos = jnp.concatenate([qpos, qpos], axis=0)
        cat = _sb_cat("after")
        cat0 = _sb_cat("after", pad)
        ng = i // SB_GROUP

        def group(off, nblk, first_cat, allowed, carry):
            acc, run = carry
            kg = k_ref[pl.ds(off, nblk * SB_BLOCK), :].astype(BF16)
            vg = v_ref[pl.ds(off, nblk * SB_BLOCK), :].astype(BF16)
            lsz, l1m, _ = _sb_logsig(_dot(qs, kg, NT))
            if allowed is not None:
                l1m = jnp.where(allowed, l1m, 0.0)
            args = [None] * nblk
            for g in reversed(range(nblk)):
                sl = slice(g * SB_BLOCK, (g + 1) * SB_BLOCK)
                al = _sb_cumsum(l1m[:, sl], first_cat if g == 0 else cat)
                args[g] = lsz[:, sl] + al[:, :SB_BLOCK] + run
                run = run + al[:, SB_BLOCK:]
            wgt = jnp.exp(jnp.concatenate(args, axis=1))
            if allowed is not None:
                wgt = jnp.where(allowed, wgt, 0.0)
            return acc + _dot(wgt.astype(BF16), vg, NN), run

        def below(t, carry):
            gi = ng - 1 - t
            return group(pl.multiple_of(gi * gw, gw), SB_GROUP, jnp.where(gi == 0, cat0, cat), None, carry)

        top = ng * gw

        def top_group(nblk, carry):
            off = pl.multiple_of(jnp.minimum(top, lp - nblk * SB_BLOCK), SB_BLOCK)
            kpos = off + lax.broadcasted_iota(jnp.int32, (1, nblk * SB_BLOCK), 1)
            return group(off, nblk, cat, (kpos < qpos) & (kpos >= pad) & (kpos >= top), carry)

        zero = (jnp.zeros((2 * SB_BLOCK, HEAD_W), F32), jnp.zeros((2 * SB_BLOCK, HEAD_W), F32))
        carry = lax.cond(i - ng * SB_GROUP < SB_GROUP // 2, functools.partial(top_group, SB_GROUP // 2),
                         functools.partial(top_group, SB_GROUP), zero)
        used, acc, run = lax.while_loop(lambda s: (s[0] < ng) & (jnp.max(s[2]) > SB_FAR),
                                        lambda s: (s[0] + 1, *below(s[0], (s[1], s[2]))), (jnp.int32(0), *carry))
        o_ref[...] = _sb_unstack(acc)
        tot_ref[...] = _sb_unstack(run)
        nproc_ref[pl.program_id(0), i] = used.astype(F32)
        if ng_arr:
            pl.when(last_pair & (pl.program_id(1) == nb - 1))(g_finish)

    full = lambda c0: pl.BlockSpec((lp, HEAD_W), lambda p, i: (0, c0 + p))
    out = pl.BlockSpec((SB_BLOCK, HEAD_W), lambda p, i: (i, p))
    return pl.pallas_call(
        body, name=name, grid=(npair, nb),
        in_specs=[pl.BlockSpec((SB_BLOCK, HEAD_W), lambda p, i: (i, blk0 + p)), full(blk0 + npair), full(blk0 + 2 * npair)]
        + [pl.BlockSpec(memory_space=pltpu.VMEM)] * ng_arr,
        out_specs=[out, out, pl.BlockSpec(memory_space=pltpu.SMEM)] + [_ANY] * ng_arr,
        out_shape=[jax.ShapeDtypeStruct((lp, npair * HEAD_W), F32)] * 2 + [jax.ShapeDtypeStruct((npair, nb), F32)]
        + _gather_out_shapes(g_srcs, g_dtypes),
        scratch_shapes=_gather_scratch(g_srcs, g_dtypes) if ng_arr else [],
        compiler_params=_cp(("arbitrary", "arbitrary"), has_side_effects=bool(ng_arr)),
    )(p0, p0, p0, *g_srcs)


def _sb_bwd(p0, tot, nproc, dsrc, d_blk0, pad, *, name, scatter=()):
    lp = p0.shape[0]
    nb = lp // SB_BLOCK
    npair = SB_HEADS // 2
    blk0 = AB_SB // HEAD_W
    scale = SB_DH ** -0.5
    gw = SB_GROUP * SB_BLOCK
    assert pad < SB_BLOCK
    ns = len(scatter)

    def body(q_ref, k_ref, v_ref, tot_ref, nproc_ref, do_ref, *rest):
        s_ins, (dq_ref, dkt_ref, dvt_ref) = rest[:ns], rest[ns:ns + 3]
        s_outs, s_sems = rest[ns + 3:2 * ns + 3], rest[2 * ns + 3:]
        if ns:
            s_start, s_finish = _scatter_phases(s_ins, s_outs, *s_sems)
            pl.when((pl.program_id(0) == 0) & (pl.program_id(1) == 0))(s_start)
        i = pl.program_id(1)

        @pl.when(i == 0)
        def _():
            dkt_ref[...] = jnp.zeros_like(dkt_ref)
            dvt_ref[...] = jnp.zeros_like(dvt_ref)

        qs = _sb_stack(q_ref[...], scale)
        dos = _sb_stack(do_ref[...])
        qst, dost = qs.T, dos.T
        totv = tot_ref[...]
        ones = jnp.ones((1, HEAD_W), F32)
        tots = jnp.concatenate([totv[:, 0:1] * ones, totv[:, SB_DH:SB_DH + 1] * ones], axis=0)
        qpos = i * SB_BLOCK + lax.broadcasted_iota(jnp.int32, (SB_BLOCK, 1), 0)
        qpos = jnp.concatenate([qpos, qpos], axis=0)
        incl, incl0 = _sb_cat("incl"), _sb_cat("incl", pad)
        before = _sb_cat("before")
        ng = i // SB_GROUP
        used = jnp.clip(nproc_ref[pl.program_id(0), i].astype(jnp.int32), 0, ng)

        def dscore(z, e, ev, dl1m):
            r = 1.0 / (1.0 + e)
            sg = jnp.where(z >= 0, r, e * r)
            return ev * (1.0 - sg) - dl1m * sg

        def group(off, nblk, first_incl, allowed, carry):
            dq, prun, erun = carry
            width = nblk * SB_BLOCK
            kg = k_ref[pl.ds(off, width), :].astype(BF16)
            vg = v_ref[pl.ds(off, width), :].astype(BF16)
            z = _dot(qs, kg, NT)
            lsz, l1m, e = _sb_logsig(z)
            if allowed is not None:
                l1m = jnp.where(allowed, l1m, 0.0)
            dwgt = _dot(dos, vg, NT)
            dzs = [None] * nblk
            wgts = [None] * nblk
            for g in range(nblk):
                sl = slice(g * SB_BLOCK, (g + 1) * SB_BLOCK)
                al = _sb_cumsum(l1m[:, sl], first_incl if g == 0 else incl)
                wgt = jnp.exp(jnp.minimum(lsz[:, sl] + (tots - prun - al[:, :SB_BLOCK]), 0.0))
                if allowed is not None:
                    wgt = jnp.where(allowed[:, sl], wgt, 0.0)
                prun = prun + al[:, SB_BLOCK:]
                ev = wgt * dwgt[:, sl]
                el = _sb_cumsum(ev, before)
                dzs[g] = dscore(z[:, sl], e[:, sl], ev, erun + el[:, :SB_BLOCK])
                erun = erun + el[:, SB_BLOCK:]
                wgts[g] = wgt
            dz = jnp.concatenate(dzs, axis=1)
            if allowed is not None:
                dz = jnp.where(allowed, dz, 0.0)
            dz = dz.astype(BF16)
            wg = jnp.concatenate(wgts, axis=1).astype(BF16)
            dkt_ref[:, pl.ds(off, width)] += _dot(qst, dz, NN)
            dvt_ref[:, pl.ds(off, width)] += _dot(dost, wg, NN)
            return dq + _dot(dz, kg, NN), prun, erun

        def below(gi, carry):
            return group(pl.multiple_of(gi * gw, gw), SB_GROUP, jnp.where(gi == 0, incl0, incl), None, carry)

        zero = tuple(jnp.zeros((2 * SB_BLOCK, HEAD_W), F32) for _ in range(3))
        carry = lax.fori_loop(ng - used, ng, below, zero)
        top = ng * gw

        def top_group(nblk, carry):
            off = pl.multiple_of(jnp.minimum(top, lp - nblk * SB_BLOCK), SB_BLOCK)
            kpos = off + lax.broadcasted_iota(jnp.int32, (1, nblk * SB_BLOCK), 1)
            return group(off, nblk, incl, (kpos < qpos) & (kpos >= pad) & (kpos >= top), carry)

        dq, _, _ = lax.cond(i - ng * SB_GROUP < SB_GROUP // 2, functools.partial(top_group, SB_GROUP // 2),
                            functools.partial(top_group, SB_GROUP), carry)
        dq_ref[...] = _sb_unstack(dq) * scale
        if ns:
            pl.when((pl.program_id(0) == npair - 1) & (pl.program_id(1) == nb - 1))(s_finish)

    full = lambda c0: pl.BlockSpec((lp, HEAD_W), lambda p, i: (0, c0 + p))
    qb = lambda c0: pl.BlockSpec((SB_BLOCK, HEAD_W), lambda p, i: (i, c0 + p))
    tr = pl.BlockSpec((HEAD_W, lp), lambda p, i: (p, 0))
    return pl.pallas_call(
        body, name=name, grid=(npair, nb),
        in_specs=[qb(blk0), full(blk0 + npair), full(blk0 + 2 * npair), qb(0), pl.BlockSpec(memory_space=pltpu.SMEM),
                  qb(d_blk0)] + [_ANY] * ns,
        out_specs=[qb(0), tr, tr] + [_ANY] * ns,
        out_shape=[jax.ShapeDtypeStruct((lp, npair * HEAD_W), F32)]
        + [jax.ShapeDtypeStruct((npair * HEAD_W, lp), F32)] * 2
        + [jax.ShapeDtypeStruct(s.shape, s.dtype) for s in scatter],
        scratch_shapes=_scatter_scratch(ns) if ns else [],
        compiler_params=_cp(("arbitrary", "arbitrary"), has_side_effects=bool(ns)),
    )(p0, p0, p0, tot, nproc, dsrc, *scatter)


def _local_step(h0, target, pad, wts, hooks=None):
    lp = h0.shape[0]
    tm = _row_tile(lp, 1056)
    tkl = tm
    tml = _row_tile(lp, 528)
    d = D_MODEL
    mm = _mm
    mmw = functools.partial(_mm, out_dtype=BF16)
    g = {}

    h0_b = h0.astype(BF16)
    p0 = mm(h0_b, wts["w_ab"], "NN", tm=tm, tn=768, tk=d, name="l0_in_proj")
    ob, sb_tot, sb_used, *gathered = _sb_fwd(p0, pad, name="sb_fwd", gather=hooks["gather_a"] if hooks else None)
    if hooks:
        wts = {**wts, **hooks["weights_a"](gathered)}
    qkv = _gdn_pre_fwd(p0, wts["conv_w"], pad, name="gdn_pre_fwd")
    oa_raw, gdn_states, *gathered = _gdn_fwd(qkv, p0, wts["alog_v"], wts["dtb_v"], pad, name="gdn_fwd",
                                             gather=hooks["gather_b"] if hooks else None)
    if hooks:
        second = hooks["weights_b"](gathered)
        wts = {**wts, **second, "w1": wts["w1"] + second["w1"], "w2": wts["w2"] + second["w2"]}
    rows = lambda a, n: a.reshape(N_DEV, n // N_DEV, d)
    parts = g["parts"] = {}
    oab = _gate_fwd(oa_raw, p0, AB_Z // HEAD_W, wts["ab_gn"], ob, heads=GDN_HEADS, name="gdn_gate_fwd")
    ln = lambda kind, layer: (wts[f"ln_{kind}_g"][layer], wts[f"ln_{kind}_b"][layer])
    pre_mix0, h0a, h0a_b = mm(oab, wts["w_out0"], "NN", tm=tml, tn=d, tk=d, epi="ln", c=h0, scale=DN_ALPHA,
                              ln=ln("mix", 0), name="l0_out_proj")
    u0, act0 = mm(h0a_b, wts["w1"][0], "NN", tm=tm, tn=512, tk=d, b_dev=True, epi="relu2_copy", name="mlp0_up")
    pre_ffn0, h0b, h0b_b = mm(act0, wts["w2"][0], "NN", tm=tml, tn=d, tk=d, epi="ln", c=h0a, scale=DN_ALPHA,
                              ln=ln("ffn", 0), name="mlp0_down")
    p1 = mm(h0b_b, wts["w_c"], "NN", tm=tm, tn=512, tk=d, b_dev=True, name="l1_in_proj")
    oc_raw, hg_states = _hg_fwd(p1, wts["lb"], pad, name="hg_fwd")
    oc = _gate_fwd(oc_raw, p1, 3 * HG_HEADS, wts["c_gn"], oc_raw, heads=HG_HEADS, name="hg_gate_fwd")
    pre_mix1, h1a, h1a_b = mm(oc, wts["w_out1"], "NN", tm=tml, tn=d, tk=d, epi="ln", c=h0b, scale=DN_ALPHA,
                              ln=ln("mix", 1), name="l1_out_proj")
    u1, act1 = mm(h1a_b, wts["w1"][1], "NN", tm=tm, tn=512, tk=d, b_dev=True, epi="relu2_copy", name="mlp1_up")
    pre_ffn1, h1b, _ = mm(act1, wts["w2"][1], "NN", tm=tml, tn=d, tk=d, epi="ln", c=h1a, scale=DN_ALPHA,
                          ln=ln("ffn", 1), name="mlp1_down")
    dy, loss_vec = _loss_head(h1b, target, name="loss_head")

    def mlp_bwd(layer, h_in_b, u, act, dpre, dpre_b, pre_mix):
        du = mm(dpre_b, wts["w2"][layer], "NT", tm=tm, tn=1024, tk=d, epi="relu2grad", c=u, out_dtype=BF16,
                name=f"mlp{layer}_d_hidden")
        dw2 = mmw(act, dpre_b, "TN", tm=1024, tn=1024, tk=tkl, name=f"mlp{layer}_dw2")
        dw1 = mmw(h_in_b, du, "TN", tm=1024, tn=512, tk=tkl, out_dev=True, name=f"mlp{layer}_dw1")
        return (*mm(du, k_major(wts["w1"][layer]), "NT", tm=tml, tn=1024, tk=2048, epi="ln_bwd", c=dpre, scale=DN_ALPHA,
                    ln=(pre_mix, wts["ln_mix_g"][layer]), name=f"mlp{layer}_d_in"), dw1, dw2)

    k_major = lambda wd: wd.transpose(1, 0, 2).reshape(wd.shape[1], -1)

    ln_ffn_dg, ln_ffn_db, ln_mix_dg, ln_mix_db, dw1s, dw2s = ([None, None] for _ in range(6))
    dpre, dpre_b, ln_ffn_dg[1], ln_ffn_db[1] = _ln_bwd(pre_ffn1, wts["ln_ffn_g"][1], dy, name="ln_ffn1_bwd")
    dpre, dpre_b, ln_mix_dg[1], ln_mix_db[1], dw1s[1], dw2s[1] = mlp_bwd(1, h1a_b, u1, act1, dpre, dpre_b, pre_mix1)
    g["c_w_out"] = mmw(oc, dpre_b, "TN", tm=1024, tn=1024, tk=tkl, name="l1_dw_out")
    doc = mm(dpre_b, wts["w_out1"], "NT", tm=tm, tn=1024, tk=d, name="l1_d_gate")
    doc_raw, dz1, g["c_gn"] = _gate_bwd(oc_raw, p1, 3 * HG_HEADS, wts["c_gn"], doc, heads=HG_HEADS, name="hg_gate_bwd")
    ready = [dw1s[1], rows(dw2s[1], D_FF), rows(g["c_w_out"], d)] if hooks else ()
    dq1, df1, di1, g["lb"], *got = _hg_bwd(p1, wts["lb"], hg_states, doc_raw, pad, name="hg_bwd", scatter=ready)
    parts.update(zip(("mlp_w1_1", "mlp_w2_1", "c_w_out"), got))
    dp1 = jnp.concatenate([dq1, df1, di1, dz1], axis=1).astype(BF16)
    g["c_w_in"] = mmw(h0b_b, dp1, "TN", tm=1024, tn=512, tk=tkl, out_dev=True, name="l1_dw_in")
    dpre, dpre_b, ln_ffn_dg[0], ln_ffn_db[0] = mm(
        dp1, k_major(wts["w_c"]), "NT", tm=tml, tn=1024, tk=2048, epi="ln_bwd", c=dpre, scale=DN_ALPHA,
        ln=(pre_ffn0, wts["ln_ffn_g"][0]), name="l1_d_in")
    dpre, dpre_b, ln_mix_dg[0], ln_mix_db[0], dw1s[0], dw2s[0] = mlp_bwd(0, h0a_b, u0, act0, dpre, dpre_b, pre_mix0)
    g["ab_w_out"] = mmw(oab, dpre_b, "TN", tm=1024, tn=1024, tk=tkl, name="l0_dw_out")
    doab = mm(dpre_b, wts["w_out0"], "NT", tm=tm, tn=1024, tk=d, name="l0_d_gate")
    doa_raw, dz0, g["ab_gn"] = _gate_bwd(oa_raw, p0, AB_Z // HEAD_W, wts["ab_gn"], doab, heads=GDN_HEADS,
                                         name="gdn_gate_bwd")
    ready = [g["c_w_in"]] if hooks else ()
    dqb, dkb_t, dvb_t, *got = _sb_bwd(p0, sb_tot, sb_used, doab, GDN_HEADS, pad, name="sb_bwd", scatter=ready)
    parts.update(zip(("c_w_in",), got))
    dkb, dvb = dkb_t.T, dvb_t.T
    ready = [dw1s[0], rows(dw2s[0], D_FF), rows(g["ab_w_out"], d)] if hooks else ()
    dqn, dkn, dvn, dba, g["alog_v"], g["dtb_v"], *got = _gdn_bwd(qkv, p0, wts["alog_v"], wts["dtb_v"], gdn_states,
                                                                 doa_raw, pad, name="gdn_bwd", scatter=ready)
    parts.update(zip(("mlp_w1_0", "mlp_w2_0", "ab_w_out"), got))
    dconv_in, g["conv_w"] = _gdn_pre_bwd(p0, wts["conv_w"], jnp.concatenate([dqn, dkn, dvn], axis=1), pad,
                                         name="gdn_pre_bwd")
    dp0 = jnp.concatenate([dconv_in, dz0, dqb, dkb, dvb, dba, jnp.zeros((lp, AB_CAT - AB_BA - HEAD_W), F32)],
                          axis=1).astype(BF16)
    g["w_ab"] = mmw(h0_b, dp0, "TN", tm=1024, tn=768, tk=tkl, name="l0_dw_in")
    last = ()
    if hooks:
        gab, ba0 = g["w_ab"], AB_Z + GDN_HEADS * HEAD_W
        gab = jnp.concatenate([gab[:, :ba0], gab[:, AB_BA:AB_BA + 2 * GDN_HEADS], gab[:, ba0:AB_BA]], axis=1)
        last = [gab.reshape(d, N_DEV, AB_IN // N_DEV).transpose(1, 0, 2)]
    res = mm(dp0, wts["w_ab"], "NT", tm=tm, tn=1024, tk=1920, epi="add", c=dpre, scale=DN_ALPHA, scatter=last,
             name="l0_d_in")
    dh0 = res[0] if last else res
    parts.update(zip(("ab_w_in",), res[1:] if last else ()))

    g["w1"], g["w2"] = dw1s, dw2s
    g["ln_mix_g"] = jnp.concatenate(ln_mix_dg, axis=0)
    g["ln_mix_b"] = jnp.concatenate(ln_mix_db, axis=0)
    g["ln_ffn_g"] = jnp.concatenate(ln_ffn_dg, axis=0)
    g["ln_ffn_b"] = jnp.concatenate(ln_ffn_db, axis=0)
    return loss_vec, dh0, g


N_CHIP = N_DEV // 2


def _place():
    x, y, c = lax.axis_index("x"), lax.axis_index("y"), lax.axis_index("c")
    return x, y, c, 2 * x + y


def _chip_dev(chip, core):
    return (chip // 2, chip % 2, core)


def _remote(src, dst, send_sem, recv_sem, dev):
    return pltpu.make_async_remote_copy(src_ref=src, dst_ref=dst, send_sem=send_sem, recv_sem=recv_sem,
                                        device_id=dev, device_id_type=pl.DeviceIdType.MESH)


_ANY = pl.BlockSpec(memory_space=pl.ANY)


def _gather(srcs, dtypes, *, name):
    n = len(srcs)

    def body(*refs):
        start, forward, finish = _gather_phases(refs[:n], refs[n:2 * n], refs[2 * n:3 * n], *refs[3 * n:], dtypes)
        start()
        forward()
        finish()

    return pl.pallas_call(
        body, name=name, in_specs=[pl.BlockSpec(memory_space=pltpu.VMEM)] * n, out_specs=[_ANY] * n,
        out_shape=_gather_out_shapes(srcs, dtypes), scratch_shapes=_gather_scratch(srcs, dtypes),
        compiler_params=_cp(has_side_effects=True),
    )(*srcs)


def _gather_out_shapes(srcs, dtypes):
    return [jax.ShapeDtypeStruct((N_DEV, *s.shape), dt) for s, dt in zip(srcs, dtypes)]


def _gather_scratch(srcs, dtypes):
    n = len(srcs)
    return [pltpu.VMEM(s.shape, dt) for s, dt in zip(srcs, dtypes)] + [
        pltpu.SemaphoreType.DMA((n, 2 * N_CHIP - 1)), pltpu.SemaphoreType.DMA((n, 2 * N_CHIP - 1)),
        pltpu.SemaphoreType.DMA((n,))]


def _gather_phases(ins, outs, stages, send_sems, recv_sems, local_sems, dtypes):
    n = len(ins)
    x, y, c, chip = _place()
    me = 2 * chip + c
    sibling = (x, y, 1 - c)

    def own(i):
        cps = [_remote(stages[i], outs[i].at[me], send_sems.at[i, 0], recv_sems.at[i, 0], sibling)]
        for j in range(1, N_CHIP):
            cps.append(_remote(stages[i], outs[i].at[me], send_sems.at[i, j], recv_sems.at[i, j],
                               _chip_dev(jnp.bitwise_xor(chip, j), c)))
        return cps

    def local(i):
        return pltpu.make_async_copy(stages[i], outs[i].at[me], local_sems.at[i])

    def passed_on(i, j):
        slot = outs[i].at[2 * jnp.bitwise_xor(chip, j) + c]
        return _remote(slot, slot, send_sems.at[i, N_CHIP - 1 + j], recv_sems.at[i, N_CHIP - 1 + j], sibling)

    def start():
        for i in range(n):
            stages[i][...] = ins[i][...].astype(dtypes[i])
            local(i).start()
            for cp in own(i):
                cp.start()

    def forward():
        for i in range(n):
            for j in range(1, N_CHIP):
                own(i)[j].wait_recv()
                passed_on(i, j).start()

    def finish():
        for i in range(n):
            own(i)[0].wait_recv()
            for j in range(1, N_CHIP):
                passed_on(i, j).wait_recv()
        for i in range(n):
            for cp in own(i):
                cp.wait_send()
            for j in range(1, N_CHIP):
                passed_on(i, j).wait_send()
            local(i).wait()

    return start, forward, finish


def _scatter_scratch(n):
    return [pltpu.SemaphoreType.DMA((n, N_DEV - 1)), pltpu.SemaphoreType.DMA((n, N_DEV - 1)),
            pltpu.SemaphoreType.DMA((n,))]


def _scatter_phases(ins, outs, send_sems, recv_sems, local_sems):
    n = len(ins)
    _, _, c, chip = _place()
    me = 2 * chip + c

    def copies():
        cps = []
        for i in range(n):
            cps.append(pltpu.make_async_copy(ins[i].at[me], outs[i].at[me], local_sems.at[i]))
            for k in range(1, N_DEV):
                peer = jnp.bitwise_xor(me, k)
                cps.append(_remote(ins[i].at[peer], outs[i].at[me], send_sems.at[i, k - 1], recv_sems.at[i, k - 1],
                                   _chip_dev(peer // 2, peer % 2)))
        return cps

    def start():
        for cp in copies():
            cp.start()

    def finish():
        for cp in copies():
            cp.wait()

    return start, finish


def _adamw(w, parts, m, v, *, name):
    r, c = w.shape
    s = parts.shape[0]
    tm = _row_tile(r, 128) if r % 8 == 0 else r
    c1 = 1.0 - ADAM_B1 ** ADAM_STEP
    c2 = 1.0 - ADAM_B2 ** ADAM_STEP

    def body(w_ref, p_ref, m_ref, v_ref, g_ref, d_ref, m2_ref, v2_ref):
        g = p_ref[0].astype(F32)
        for j in range(1, s):
            g = g + p_ref[j].astype(F32)
        m2 = ADAM_B1 * m_ref[...] + (1.0 - ADAM_B1) * g
        v2 = ADAM_B2 * v_ref[...] + (1.0 - ADAM_B2) * jnp.square(g)
        g_ref[...] = g
        m2_ref[...] = m2
        v2_ref[...] = v2
        d_ref[...] = -ADAM_LR * ((m2 / c1) / (jnp.sqrt(v2 / c2) + ADAM_EPS) + ADAM_WD * w_ref[...])

    blk = pl.BlockSpec((tm, c), lambda i: (i, 0))
    return pl.pallas_call(
        body, name=name, grid=(r // tm,),
        in_specs=[blk, pl.BlockSpec((s, tm, c), lambda i: (0, i, 0)), blk, blk], out_specs=[blk] * 4,
        out_shape=[jax.ShapeDtypeStruct((r, c), F32)] * 4, compiler_params=_cp(("parallel",)),
    )(w, parts, m, v)


_WEIGHTS = ("meta_tokens", "ab_w_in", "ab_conv_w", "ab_a_log", "ab_dt_bias", "ab_gnorm_g", "ab_w_out", "c_w_in",
            "c_lb_raw", "c_gnorm_g", "c_w_out", "ln_mix_g", "ln_mix_b", "mlp_w1", "mlp_w2", "ln_ffn_g", "ln_ffn_b")
_PACK_ROWS = (("ln_mix_g", 0), ("ln_mix_b", 2), ("ln_ffn_g", 4), ("ln_ffn_b", 6), ("c_lb_raw", 8))
_PACK_MISC_ROW = 10
_PACK_MISC = (("ab_gnorm_g", 0, 128), ("c_gnorm_g", 128, 128), ("ab_a_log", 256, GDN_HEADS), ("ab_dt_bias", 260, GDN_HEADS))
_PACK_N = 16
_SMALL_META = 16
_SMALL_CONV = 32
_SMALL_N = 40


def _pack_replicated(p):
    rows = jnp.zeros((_PACK_N, D_MODEL), F32)
    for name, r0 in _PACK_ROWS:
        rows = rows.at[r0:r0 + 2].set(p[name])
    for name, c0, width in _PACK_MISC:
        rows = rows.at[_PACK_MISC_ROW, c0:c0 + width].set(p[name].reshape(width))
    return rows


def _unpack_replicated(rows, like):
    out = {}
    for name, r0 in _PACK_ROWS:
        out[name] = rows[r0:r0 + 2]
    for name, c0, width in _PACK_MISC:
        out[name] = rows[_PACK_MISC_ROW, c0:c0 + width].reshape(like[name].shape)
    return out


def _lower_bound(c_lb_raw):
    lb_all = jnp.cumsum(jax.nn.softmax(c_lb_raw.astype(F32), axis=0), axis=0)
    return (lb_all - lb_all[0:1])[1].reshape(1, -1)


def kernel(x, meta_tokens, ab_w_in, ab_conv_w, ab_a_log, ab_dt_bias, ab_gnorm_g, ab_w_out, c_w_in, c_lb_raw, c_gnorm_g, c_w_out, ln_mix_g, ln_mix_b, mlp_w1, mlp_w2, ln_ffn_g, ln_ffn_b, loss_target, m_meta_tokens, m_ab_w_in, m_ab_conv_w, m_ab_a_log, m_ab_dt_bias, m_ab_gnorm_g, m_ab_w_out, m_c_w_in, m_c_lb_raw, m_c_gnorm_g, m_c_w_out, m_ln_mix_g, m_ln_mix_b, m_mlp_w1, m_mlp_w2, m_ln_ffn_g, m_ln_ffn_b, v_meta_tokens, v_ab_w_in, v_ab_conv_w, v_ab_a_log, v_ab_dt_bias, v_ab_gnorm_g, v_ab_w_out, v_c_w_in, v_c_lb_raw, v_c_gnorm_g, v_c_w_out, v_ln_mix_g, v_ln_mix_b, v_mlp_w1, v_mlp_w2, v_ln_ffn_g, v_ln_ffn_b):
    w = dict(zip(_WEIGHTS, (meta_tokens, ab_w_in, ab_conv_w, ab_a_log, ab_dt_bias, ab_gnorm_g, ab_w_out, c_w_in, c_lb_raw,
                            c_gnorm_g, c_w_out, ln_mix_g, ln_mix_b, mlp_w1, mlp_w2, ln_ffn_g, ln_ffn_b)))
    mom = dict(zip(_WEIGHTS, (m_meta_tokens, m_ab_w_in, m_ab_conv_w, m_ab_a_log, m_ab_dt_bias, m_ab_gnorm_g, m_ab_w_out,
                              m_c_w_in, m_c_lb_raw, m_c_gnorm_g, m_c_w_out, m_ln_mix_g, m_ln_mix_b, m_mlp_w1, m_mlp_w2,
                              m_ln_ffn_g, m_ln_ffn_b)))
    var = dict(zip(_WEIGHTS, (v_meta_tokens, v_ab_w_in, v_ab_conv_w, v_ab_a_log, v_ab_dt_bias, v_ab_gnorm_g, v_ab_w_out,
                              v_c_w_in, v_c_lb_raw, v_c_gnorm_g, v_c_w_out, v_ln_mix_g, v_ln_mix_b, v_mlp_w1, v_mlp_w2,
                              v_ln_ffn_g, v_ln_ffn_b)))
    me = 4 * lax.axis_index("x") + 2 * lax.axis_index("y") + lax.axis_index("c")
    seq = x.shape[1]
    pad = (-(N_META + seq)) % SB_BLOCK
    lp = pad + N_META + seq
    meta_w = D_MODEL // N_DEV
    conv_w_all = 2 * GDN_HEADS * HEAD_W + GDN_HEADS * HEAD_W
    conv_w_mine = conv_w_all // N_DEV

    g_meta, g_conv, g_ab_in = _gather([w["meta_tokens"], w["ab_conv_w"][0], w["ab_w_in"][0]], [F32, F32, BF16],
                                      name="gather_weights_first")
    meta_full = g_meta.transpose(1, 0, 2).reshape(N_META, D_MODEL)
    conv_full = g_conv.transpose(1, 0, 2).reshape(CONV_K, conv_w_all)
    ab_full = g_ab_in.transpose(1, 0, 2).reshape(D_MODEL, AB_IN)
    ba0 = AB_Z + 512
    w_ab = jnp.concatenate([ab_full[:, :ba0], ab_full[:, ba0 + 2 * GDN_HEADS:], ab_full[:, ba0:ba0 + 2 * GDN_HEADS],
                            jnp.zeros((D_MODEL, AB_CAT - AB_IN), BF16)], axis=1)
    vec128 = lambda p: jnp.zeros((1, HEAD_W), F32).at[0, :GDN_HEADS].set(p.reshape(GDN_HEADS))
    wts = dict(
        w_ab=w_ab, conv_w=conv_full, alog_v=vec128(w["ab_a_log"]), dtb_v=vec128(w["ab_dt_bias"]),
        ab_gn=w["ab_gnorm_g"][0], lb=_lower_bound(w["c_lb_raw"]), c_gn=w["c_gnorm_g"][0],
        ln_mix_g=w["ln_mix_g"], ln_mix_b=w["ln_mix_b"], ln_ffn_g=w["ln_ffn_g"], ln_ffn_b=w["ln_ffn_b"])

    def weights_a(gathered):
        g_ab_out, g_w1, g_w2 = gathered
        return dict(w_out0=g_ab_out.reshape(D_MODEL, D_MODEL), w1=[g_w1], w2=[g_w2.reshape(D_FF, D_MODEL)])

    def weights_b(gathered):
        g_c_in, g_c_out, g_w1, g_w2 = gathered
        return dict(w_c=g_c_in, w_out1=g_c_out.reshape(D_MODEL, D_MODEL), w1=[g_w1], w2=[g_w2.reshape(D_FF, D_MODEL)])

    hooks = dict(
        gather_a=([w["ab_w_out"][0], w["mlp_w1"][0], w["mlp_w2"][0]], [BF16] * 3), weights_a=weights_a,
        gather_b=([w["c_w_in"][0], w["c_w_out"][0], w["mlp_w1"][1], w["mlp_w2"][1]], [BF16] * 4), weights_b=weights_b)

    h0 = jnp.concatenate([jnp.zeros((pad, D_MODEL), F32), meta_full, x[0]], axis=0)
    loss_vec, dh0, g = _local_step(h0, loss_target[0], pad, wts, hooks)
    loss = lax.psum(jnp.sum(loss_vec), ("x", "y", "c"))
    grad_x = dh0[lp - seq:][None]

    _, lb_vjp = jax.vjp(_lower_bound, w["c_lb_raw"])
    rep_part = _pack_replicated(dict(
        ln_mix_g=g["ln_mix_g"], ln_mix_b=g["ln_mix_b"], ln_ffn_g=g["ln_ffn_g"], ln_ffn_b=g["ln_ffn_b"],
        c_lb_raw=lb_vjp(g["lb"])[0], ab_gnorm_g=g["ab_gn"], c_gnorm_g=g["c_gn"],
        ab_a_log=g["alog_v"][0, :GDN_HEADS], ab_dt_bias=g["dtb_v"][0, :GDN_HEADS]))
    small = jnp.concatenate([rep_part, dh0[pad:pad + N_META], g["conv_w"].reshape(-1, D_MODEL),
                             jnp.zeros((_SMALL_N - _SMALL_CONV - CONV_K * conv_w_all // D_MODEL, D_MODEL), F32)], axis=0)
    (small_all,) = _gather([small], [F32], name="gather_small_grads")
    rep_out = _adamw(_pack_replicated(w), small_all[:, :_PACK_N], _pack_replicated(mom), _pack_replicated(var),
                     name="adamw_replicated")
    meta_parts = lax.dynamic_slice_in_dim(small_all[:, _SMALL_META:_SMALL_META + N_META], me * meta_w, meta_w, axis=2)
    meta_out = _adamw(w["meta_tokens"], meta_parts, mom["meta_tokens"], var["meta_tokens"], name="adamw_meta")
    conv_parts = small_all[:, _SMALL_CONV:_SMALL_CONV + CONV_K * conv_w_all // D_MODEL].reshape(N_DEV, CONV_K, conv_w_all)
    conv_parts = lax.dynamic_slice_in_dim(conv_parts, me * conv_w_mine, conv_w_mine, axis=2)
    conv_out = _adamw(w["ab_conv_w"][0], conv_parts, mom["ab_conv_w"][0], var["ab_conv_w"][0], name="adamw_conv")

    parts = g["parts"]
    big = [("ab_w_in", 0, parts["ab_w_in"]), ("ab_w_out", 0, parts["ab_w_out"]), ("mlp_w1", 0, parts["mlp_w1_0"]),
           ("mlp_w2", 0, parts["mlp_w2_0"]), ("c_w_in", 0, parts["c_w_in"]), ("c_w_out", 0, parts["c_w_out"]),
           ("mlp_w1", 1, parts["mlp_w1_1"]), ("mlp_w2", 1, parts["mlp_w2_1"])]
    big_out = {}
    for name, l, p in big:
        res = _adamw(w[name][l], p, mom[name][l], var[name][l], name=f"adamw_{name}{l}")
        big_out.setdefault(name, []).append(res)

    rep = [_unpack_replicated(r, w) for r in rep_out]
    outs = {}
    for name in _WEIGHTS:
        if name == "meta_tokens":
            outs[name] = list(meta_out)
        elif name == "ab_conv_w":
            outs[name] = [o[None] for o in conv_out]
        elif name in big_out:
            res = big_out[name]
            outs[name] = [o[None] for o in res[0]] if len(res) == 1 else [jnp.stack(pair) for pair in zip(*res)]
        else:
            outs[name] = [r[name] for r in rep]
    flat = [loss, grad_x]
    for kind in range(4):
        flat += [outs[name][kind] for name in _WEIGHTS]
    return tuple(flat)
```

```python
import functools

import jax
import jax.numpy as jnp
from jax import lax
from jax.experimental import pallas as pl
from jax.experimental.pallas import tpu as pltpu

F32 = jnp.float32
BF16 = jnp.bfloat16

N_DEV = 8
D_MODEL = 1024
N_META = 16
D_FF = 4096
DEPTH = 2
GDN_HEADS = 4
SB_HEADS = 8
SB_DH = 64
HG_HEADS = 8
HEAD_W = 128
CHUNK = 64
SB_BLOCK = 128
CONV_K = 4
DN_ALPHA = float((2 * DEPTH) ** 0.25)
LN_EPS = 1e-5
RMS_EPS = 1e-6
L2_EPS = 1e-6
ADAM_LR, ADAM_B1, ADAM_B2, ADAM_EPS, ADAM_WD, ADAM_STEP = 0.001, 0.9, 0.999, 1e-08, 0.01, 10

AB_Z = 1536
AB_SB = 2048
AB_BA = 3584
AB_CAT = 3840
AB_IN = 3592

VMEM_LIMIT = 56 * 1024 * 1024


def _cp(sem=None, **kw):
    if sem is not None:
        kw["dimension_semantics"] = sem
    return pltpu.CompilerParams(vmem_limit_bytes=VMEM_LIMIT, **kw)


def _row_tile(n, want):
    best = 8
    for t in range(8, min(n, want) + 1, 8):
        if n % t == 0:
            best = t
    return best


@jax.custom_vjp
def _sigmoid(x):
    e = jnp.exp(-jnp.abs(x))
    r = 1.0 / (1.0 + e)
    return jnp.where(x >= 0, r, e * r)


def _sigmoid_fwd(x):
    s = _sigmoid(x)
    return s, s


def _sigmoid_bwd(s, g):
    return (g * s * (1.0 - s),)


_sigmoid.defvjp(_sigmoid_fwd, _sigmoid_bwd)


def _log1p_exp_neg_abs(x):
    e = jnp.exp(-jnp.abs(x))
    return jnp.where(e < 1e-4, e - 0.5 * e * e, jnp.log(1.0 + e))


@jax.custom_vjp
def _softplus(x):
    return jnp.maximum(x, 0.0) + _log1p_exp_neg_abs(x)


def _softplus_fwd(x):
    return _softplus(x), x


def _softplus_bwd(x, g):
    return (g * _sigmoid(x),)


_softplus.defvjp(_softplus_fwd, _softplus_bwd)


def _silu(x):
    return x * _sigmoid(x)


def _silu_grad(x):
    s = _sigmoid(x)
    return s * (1.0 + x * (1.0 - s))


def _dot(a, b, dims, precision=None):
    return lax.dot_general(a, b, (dims, ((), ())), precision=precision, preferred_element_type=F32)


NN = ((1,), (0,))
NT = ((1,), (1,))
TN = ((0,), (0,))


def _bdot(a, b, dims):
    return _dot(a.astype(BF16), b.astype(BF16), dims)


def _layer_norm(pre, g, beta):
    mu = jnp.mean(pre, axis=-1, keepdims=True)
    xc = pre - mu
    var = jnp.mean(xc * xc, axis=-1, keepdims=True)
    return xc * lax.rsqrt(var + LN_EPS) * g + beta


def _layer_norm_bwd(pre, g, dy):
    mu = jnp.mean(pre, axis=-1, keepdims=True)
    xc = pre - mu
    rstd = lax.rsqrt(jnp.mean(xc * xc, axis=-1, keepdims=True) + LN_EPS)
    xhat = xc * rstd
    dxh = dy * g
    m1 = jnp.mean(dxh, axis=-1, keepdims=True)
    m2 = jnp.mean(dxh * xhat, axis=-1, keepdims=True)
    return (rstd * (dxh - m1 - xhat * m2), jnp.sum(dy * xhat, axis=0, keepdims=True),
            jnp.sum(dy, axis=0, keepdims=True))


def _mm(a, b, mode, *, tm, tn, tk, name, epi=None, c=None, scale=1.0, b_dev=False, out_dev=False, out_dtype=F32,
        ln=None, scatter=()):
    if mode == "NN":
        m, kk = a.shape
        n = b.shape[2] * N_DEV if b_dev else b.shape[1]
    elif mode == "NT":
        m, kk = a.shape
        n = b.shape[1] if b_dev else b.shape[0]
    else:
        kk, m = a.shape
        n = b.shape[1]
    assert m % tm == 0 and n % tn == 0 and kk % tk == 0, (name, m, n, kk, tm, tn, tk)
    nk = kk // tk
    dims = {"NN": NN, "NT": NT, "TN": TN}[mode]

    if mode == "TN":
        a_spec = pl.BlockSpec((tk, tm), lambda i, j, k: (k, i))
    else:
        a_spec = pl.BlockSpec((tm, tk), lambda i, j, k: (i, k))
    if mode == "NN":
        if b_dev:
            assert tn == b.shape[2]
            b_spec = pl.BlockSpec((None, tk, tn), lambda i, j, k: (j, k, 0))
        else:
            b_spec = pl.BlockSpec((tk, tn), lambda i, j, k: (k, j))
    elif mode == "NT":
        if b_dev:
            assert tk == b.shape[2]
            b_spec = pl.BlockSpec((None, tn, tk), lambda i, j, k: (k, j, 0))
        else:
            b_spec = pl.BlockSpec((tn, tk), lambda i, j, k: (j, k))
    else:
        b_spec = pl.BlockSpec((tk, tn), lambda i, j, k: (k, j))
    in_specs = [a_spec, b_spec]
    operands = [a, b]
    if c is not None:
        in_specs.append(pl.BlockSpec((tm, tn), lambda i, j, k: (i, j)))
        operands.append(c)
    if epi == "ln":
        assert tn == n and not out_dev
        in_specs += [pl.BlockSpec((1, n), lambda i, j, k: (0, 0))] * 2
        operands += [ln[0].reshape(1, n), ln[1].reshape(1, n)]
    elif epi == "ln_bwd":
        assert tn == n and not out_dev
        in_specs += [pl.BlockSpec((tm, tn), lambda i, j, k: (i, j)), pl.BlockSpec((1, n), lambda i, j, k: (0, 0))]
        operands += [ln[0], ln[1].reshape(1, n)]
    if out_dev:
        assert tn == n // N_DEV
        out_shape = jax.ShapeDtypeStruct((N_DEV, m, tn), out_dtype)
        out_spec = pl.BlockSpec((None, tm, tn), lambda i, j, k: (j, i, 0))
    else:
        out_shape = jax.ShapeDtypeStruct((m, n), out_dtype)
        out_spec = pl.BlockSpec((tm, tn), lambda i, j, k: (i, j))
    if epi == "ln":
        out_shape = [out_shape, out_shape, jax.ShapeDtypeStruct((m, n), BF16)]
        out_spec = [out_spec] * 3
    elif epi == "relu2_copy":
        assert not out_dev
        out_shape = [out_shape, jax.ShapeDtypeStruct((m, n), BF16)]
        out_spec = [out_spec] * 2
    elif epi == "ln_bwd":
        vec_shape, vec_spec = jax.ShapeDtypeStruct((1, n), F32), pl.BlockSpec((1, n), lambda i, j, k: (0, 0))
        out_shape = [out_shape, jax.ShapeDtypeStruct((m, n), BF16), vec_shape, vec_shape]
        out_spec = [out_spec, out_spec, vec_spec, vec_spec]
    n_out = {"ln": 3, "relu2_copy": 2, "ln_bwd": 4}.get(epi, 1)
    ns = len(scatter)
    if ns:
        in_specs += [_ANY] * ns
        operands += list(scatter)
        out_shape = (out_shape if n_out > 1 else [out_shape]) + [jax.ShapeDtypeStruct(s.shape, s.dtype) for s in scatter]
        out_spec = (out_spec if n_out > 1 else [out_spec]) + [_ANY] * ns
    n_in = len(operands)
    grid = (m // tm, n // tn, nk)

    def body(*refs):
        a_ref, b_ref = refs[0], refs[1]
        c_ref = refs[2] if c is not None else None
        o_ref = refs[n_in]
        scratch0 = n_in + n_out + ns
        acc_ref = refs[scratch0] if nk > 1 else None
        if ns:
            s_start, s_finish = _scatter_phases(refs[n_in - ns:n_in], refs[n_in + n_out:scratch0],
                                                *refs[scratch0 + (1 if nk > 1 else 0):])
            at = lambda step: functools.reduce(lambda x, y: x & y, [pl.program_id(ax) == step[ax] for ax in range(3)])
            pl.when(at((0, 0, 0)))(s_start)
        p = _dot(a_ref[...].astype(BF16), b_ref[...].astype(BF16), dims)
        first_rows = pl.program_id(0) == 0

        def finish(acc):
            if epi == "add":
                acc = acc + scale * c_ref[...]
            elif epi == "relu2grad":
                acc = acc * (2.0 * jnp.maximum(c_ref[...], 0.0))
            elif epi == "relu2_copy":
                refs[n_in + 1][...] = jnp.square(jnp.maximum(acc, 0.0)).astype(BF16)
            elif epi == "ln_bwd":
                acc, dg, db = _layer_norm_bwd(refs[3][...], refs[4][...], acc + scale * c_ref[...])
                dg_ref, db_ref = refs[n_in + 2], refs[n_in + 3]

                @pl.when(first_rows)
                def _():
                    dg_ref[...] = jnp.zeros_like(dg_ref)
                    db_ref[...] = jnp.zeros_like(db_ref)

                dg_ref[...] += dg
                db_ref[...] += db
                refs[n_in + 1][...] = acc.astype(BF16)
            elif epi == "ln":
                acc = acc + scale * c_ref[...]
                y = _layer_norm(acc, refs[3][...], refs[4][...])
                refs[n_in + 1][...] = y
                refs[n_in + 2][...] = y.astype(BF16)
            o_ref[...] = acc.astype(out_dtype)

        if nk == 1:
            finish(p)
        else:
            k = pl.program_id(2)

            @pl.when(k == 0)
            def _():
                acc_ref[...] = p

            @pl.when(k > 0)
            def _():
                acc_ref[...] += p

            @pl.when(k == nk - 1)
            def _():
                finish(acc_ref[...])

        if ns:
            pl.when(at(tuple(g - 1 for g in grid)))(s_finish)

    res = pl.pallas_call(
        body, name=name, grid=grid, in_specs=in_specs, out_specs=out_spec, out_shape=out_shape,
        scratch_shapes=([pltpu.VMEM((tm, tn), F32)] if nk > 1 else []) + (_scatter_scratch(ns) if ns else []),
        compiler_params=_cp(("arbitrary",) * 3 if ns or epi == "ln_bwd" else ("parallel", "parallel", "arbitrary"),
                            has_side_effects=bool(ns)),
    )(*operands)
    return res


def _ln_bwd(pre, g, dy, *, name):
    lp, d = pre.shape
    tm = _row_tile(lp, 512)

    def body(pre_ref, g_ref, dy_ref, dpre_ref, dpreb_ref, dg_ref, db_ref):
        dpre, dg, db = _layer_norm_bwd(pre_ref[...], g_ref[...], dy_ref[...])
        dpre_ref[...] = dpre
        dpreb_ref[...] = dpre.astype(BF16)

        @pl.when(pl.program_id(0) == 0)
        def _():
            dg_ref[...] = jnp.zeros_like(dg_ref)
            db_ref[...] = jnp.zeros_like(db_ref)

        dg_ref[...] += dg
        db_ref[...] += db

    row = pl.BlockSpec((tm, d), lambda i: (i, 0))
    vec = pl.BlockSpec((1, d), lambda i: (0, 0))
    return pl.pallas_call(
        body, name=name, grid=(lp // tm,), in_specs=[row, vec, row], out_specs=[row, row, vec, vec],
        out_shape=[jax.ShapeDtypeStruct((lp, d), F32), jax.ShapeDtypeStruct((lp, d), BF16),
                   jax.ShapeDtypeStruct((1, d), F32), jax.ShapeDtypeStruct((1, d), F32)],
        compiler_params=_cp(("arbitrary",)),
    )(pre, g.reshape(1, d), dy)


def _loss_head(y, target, *, name):
    lp, d = y.shape
    seq = target.shape[0]
    tm = SB_BLOCK
    first = (lp - seq) // tm
    assert (lp - seq) % tm == 0 and seq % tm == 0

    def body(y_ref, t_ref, dy_ref, loss_ref):
        i = pl.program_id(0)
        live = i >= first
        diff = jnp.where(live, y_ref[...] - t_ref[...], 0.0)
        dy_ref[...] = diff * (1.0 / d)

        @pl.when(i == 0)
        def _():
            loss_ref[...] = jnp.zeros_like(loss_ref)

        loss_ref[...] += jnp.sum(diff * diff, axis=0, keepdims=True) * (0.5 / d)

    return pl.pallas_call(
        body, name=name, grid=(lp // tm,),
        in_specs=[pl.BlockSpec((tm, d), lambda i: (i, 0)),
                  pl.BlockSpec((tm, d), lambda i: (jnp.maximum(i - first, 0), 0))],
        out_specs=[pl.BlockSpec((tm, d), lambda i: (i, 0)), pl.BlockSpec((1, d), lambda i: (0, 0))],
        out_shape=[jax.ShapeDtypeStruct((lp, d), F32), jax.ShapeDtypeStruct((1, d), F32)],
        compiler_params=_cp(("arbitrary",)),
    )(y, target)


def _gate_fwd(o, zsrc, z_blk0, g, other, *, heads, name):
    lp = o.shape[0]
    tm = _row_tile(lp, 512)
    w = heads * HEAD_W
    assert (z_blk0 * HEAD_W) % w == 0
    has_other = w < D_MODEL

    def body(o_ref, z_ref, g_ref, *rest):
        y_ref = rest[-1]
        gv = g_ref[...]
        for h in range(heads):
            cs = slice(h * HEAD_W, (h + 1) * HEAD_W)
            ov = o_ref[:, cs]
            r = lax.rsqrt(jnp.mean(ov * ov, axis=-1, keepdims=True) + RMS_EPS)
            y_ref[:, cs] = (ov * r * gv * _silu(z_ref[:, cs])).astype(BF16)
        if has_other:
            y_ref[:, w:] = rest[0][...].astype(BF16)

    row = lambda width, blk: pl.BlockSpec((tm, width), lambda i: (i, blk))
    return pl.pallas_call(
        body, name=name, grid=(lp // tm,),
        in_specs=[row(w, 0), row(w, z_blk0 * HEAD_W // w), pl.BlockSpec((1, HEAD_W), lambda i: (0, 0))]
        + ([row(D_MODEL - w, 0)] if has_other else []),
        out_specs=row(D_MODEL, 0), out_shape=jax.ShapeDtypeStruct((lp, D_MODEL), BF16),
        compiler_params=_cp(("parallel",)),
    )(o, zsrc, g.reshape(1, HEAD_W), *([other] if has_other else []))


def _gate_bwd(o, zsrc, z_blk0, g, dy, *, heads, name):
    lp = o.shape[0]
    tm = _row_tile(lp, 512)

    w = heads * HEAD_W
    assert (z_blk0 * HEAD_W) % w == 0

    def body(o_ref, z_ref, g_ref, dy_ref, do_ref, dz_ref, dg_ref):
        @pl.when(pl.program_id(0) == 0)
        def _():
            dg_ref[...] = jnp.zeros_like(dg_ref)

        gv = g_ref[...]
        dg = jnp.zeros((1, HEAD_W), F32)
        for h in range(heads):
            cs = slice(h * HEAD_W, (h + 1) * HEAD_W)
            ov, zv, dyv = o_ref[:, cs], z_ref[:, cs], dy_ref[:, cs]
            r = lax.rsqrt(jnp.mean(ov * ov, axis=-1, keepdims=True) + RMS_EPS)
            nrm = ov * r
            s = _silu(zv)
            dn = dyv * gv * s
            do_ref[:, cs] = r * (dn - nrm * jnp.mean(dn * nrm, axis=-1, keepdims=True))
            dz_ref[:, cs] = dyv * nrm * gv * _silu_grad(zv)
            dg = dg + jnp.sum(dyv * nrm * s, axis=0, keepdims=True)
        dg_ref[...] += dg

    row = lambda blk: pl.BlockSpec((tm, w), lambda i: (i, blk))
    vec = pl.BlockSpec((1, HEAD_W), lambda i: (0, 0))
    return pl.pallas_call(
        body, name=name, grid=(lp // tm,),
        in_specs=[row(0), row(z_blk0 * HEAD_W // w), vec, row(0)], out_specs=[row(0), row(0), vec],
        out_shape=[jax.ShapeDtypeStruct((lp, w), F32), jax.ShapeDtypeStruct((lp, w), F32),
                   jax.ShapeDtypeStruct((1, HEAD_W), F32)],
        compiler_params=_cp(("arbitrary",)),
    )(o, zsrc, g.reshape(1, HEAD_W), dy)


def _conv_taps(x, w):
    acc = w[CONV_K - 1:CONV_K, :] * x
    for k in range(CONV_K - 1):
        acc = acc + w[k:k + 1, :] * pltpu.roll(x, CONV_K - 1 - k, 0)
    return acc


def _gdn_pre_fwd(p0, conv_w, pad, *, name):
    lp = p0.shape[0]
    nq = GDN_HEADS
    qscale = HEAD_W ** -0.5

    def body(x_ref, w_ref, y_ref):
        j = pl.program_id(0)
        c = _conv_taps(x_ref[...], w_ref[...])
        s = _silu(c)
        r = lax.rsqrt(jnp.sum(s * s, axis=-1, keepdims=True) + L2_EPS)
        mult = jnp.where(j < nq, r * qscale, jnp.where(j < 2 * nq, r, 1.0))
        rows = lax.broadcasted_iota(jnp.int32, (lp, 1), 0)
        y_ref[...] = jnp.where(rows >= pad, s * mult, 0.0)

    return pl.pallas_call(
        body, name=name, grid=(3 * nq,),
        in_specs=[pl.BlockSpec((lp, HEAD_W), lambda j: (0, j)), pl.BlockSpec((CONV_K, HEAD_W), lambda j: (0, j))],
        out_specs=pl.BlockSpec((lp, HEAD_W), lambda j: (0, j)),
        out_shape=jax.ShapeDtypeStruct((lp, 3 * nq * HEAD_W), F32), compiler_params=_cp(("parallel",)),
    )(p0, conv_w)


def _gdn_pre_bwd(p0, conv_w, dqkv, pad, *, name):
    lp = p0.shape[0]
    nq = GDN_HEADS
    qscale = HEAD_W ** -0.5

    def body(x_ref, w_ref, dy_ref, dx_ref, dw_ref):
        j = pl.program_id(0)
        x, w = x_ref[...], w_ref[...]
        c = _conv_taps(x, w)
        s = _silu(c)
        r = lax.rsqrt(jnp.sum(s * s, axis=-1, keepdims=True) + L2_EPS)
        rows = lax.broadcasted_iota(jnp.int32, (lp, 1), 0)
        dy = jnp.where(rows >= pad, dy_ref[...], 0.0)
        nrm = s * r
        dn = dy * jnp.where(j < nq, qscale, 1.0)
        ds_norm = r * (dn - nrm * jnp.sum(nrm * dn, axis=-1, keepdims=True))
        ds = jnp.where(j < 2 * nq, ds_norm, dy)
        dc = ds * _silu_grad(c)
        dx = w[CONV_K - 1:CONV_K, :] * dc
        dws = [None] * CONV_K
        dws[CONV_K - 1] = jnp.sum(dc * x, axis=0, keepdims=True)
        for k in range(CONV_K - 1):
            sh = CONV_K - 1 - k
            dx = dx + w[k:k + 1, :] * pltpu.roll(dc, lp - sh, 0)
            dws[k] = jnp.sum(dc * pltpu.roll(x, sh, 0), axis=0, keepdims=True)
        dx_ref[...] = dx
        dw_ref[...] = jnp.concatenate(dws, axis=0)

    blk = pl.BlockSpec((lp, HEAD_W), lambda j: (0, j))
    wblk = pl.BlockSpec((CONV_K, HEAD_W), lambda j: (0, j))
    return pl.pallas_call(
        body, name=name, grid=(3 * nq,), in_specs=[blk, wblk, blk], out_specs=[blk, wblk],
        out_shape=[jax.ShapeDtypeStruct((lp, 3 * nq * HEAD_W), F32),
                   jax.ShapeDtypeStruct((CONV_K, 3 * nq * HEAD_W), F32)],
        compiler_params=_cp(("parallel",)),
    )(p0, conv_w, dqkv)


@jax.custom_vjp
def _inv_unit_lower(m):
    c = m.shape[0]
    eye = (lax.broadcasted_iota(jnp.int32, (c, c), 0) == lax.broadcasted_iota(jnp.int32, (c, c), 1)).astype(F32)
    x = eye - m
    p = m
    n = 2
    while n < CHUNK:
        p = _bdot(p, p, NN)
        x = x + _bdot(x, p, NN)
        n *= 2
    return x


def _inv_fwd(m):
    t = _inv_unit_lower(m)
    return t, t


def _inv_bwd(t, g):
    return (-_bdot(_bdot(t, g, TN), t, NT),)


_inv_unit_lower.defvjp(_inv_fwd, _inv_bwd)


def _heads_to_rows(x, nh):
    return jnp.concatenate([x[:, h * HEAD_W:(h + 1) * HEAD_W] for h in range(nh)], axis=0)


def _rows_to_heads(x, nh):
    c = x.shape[0] // nh
    return jnp.concatenate([x[h * c:(h + 1) * c] for h in range(nh)], axis=1)


def _gdn_chunk(q, k, v, ba, alog, dtb, states, valid):
    nh = GDN_HEADS
    c = q.shape[0]
    r = nh * c
    lane = lax.broadcasted_iota(jnp.int32, (1, HEAD_W), 1)
    pick = lambda x, l: jnp.sum(jnp.where(lane == l, x, 0.0), axis=-1, keepdims=True)
    beta = jnp.concatenate([jnp.where(valid, _sigmoid(pick(ba, h)), 0.0) for h in range(nh)], axis=0)
    g = jnp.concatenate(
        [jnp.where(valid, -jnp.exp(pick(alog, h)) * _softplus(pick(ba, nh + h) + pick(dtb, h)), 0.0) for h in range(nh)],
        axis=0)
    qs, ks, vs = _heads_to_rows(q, nh), _heads_to_rows(k, nh), _heads_to_rows(v, nh)
    rr = lax.broadcasted_iota(jnp.int32, (r, r), 0)
    cc = lax.broadcasted_iota(jnp.int32, (r, r), 1)
    same = (rr // c) == (cc // c)
    causal, strict = same & (cc <= rr), same & (cc < rr)
    lower = jnp.where(causal, 1.0, 0.0).astype(BF16)
    upper = jnp.where(same & (cc >= rr), 1.0, 0.0).astype(BF16)
    gcb = _mask_mm(lower, upper, g * jnp.ones((1, HEAD_W), F32))
    gc_col = jnp.concatenate([gcb] * (r // HEAD_W), axis=1)
    decay = jnp.where(causal, jnp.exp(jnp.minimum(gc_col - gc_col.T, 0.0)), 0.0)
    egc = jnp.exp(gcb)
    kb = ks * beta
    m = jnp.where(strict, _bdot(kb, ks, NT) * decay, 0.0)
    t = _inv_unit_lower(m)
    u = _bdot(t, vs * beta, NN)
    w = _bdot(t, kb * egc, NN)
    a = _bdot(qs, ks, NT) * decay
    rows = lambda x, h: x[h * c:(h + 1) * c]
    qe = qs * egc
    v_new = u - jnp.concatenate([_bdot(rows(w, h), states[h], NN) for h in range(nh)], axis=0)
    o = jnp.concatenate([_bdot(rows(qe, h), states[h], NN) for h in range(nh)], axis=0) + _bdot(a, v_new, NN)
    new_states = []
    for h in range(nh):
        gl = gcb[(h + 1) * c - 1:(h + 1) * c, :]
        k_dec = rows(ks, h) * jnp.exp(gl - rows(gcb, h))
        new_states.append(states[h] * jnp.exp(gl) + _bdot(k_dec, rows(v_new, h), TN))
    return _rows_to_heads(o, nh), new_states


def _gdn_fwd(qkv, p0, alog_v, dtb_v, pad, *, name, gather=None):
    lp = qkv.shape[0]
    n = lp // CHUNK
    nh = GDN_HEADS
    g_srcs, g_dtypes = gather if gather is not None else ([], [])
    ng_arr = len(g_srcs)

    def body(q_ref, k_ref, v_ref, ba_ref, al_ref, dt_ref, *rest):
        g_ins, (o_ref, st_ref) = rest[:ng_arr], rest[ng_arr:ng_arr + 2]
        g_outs, s_ref, g_scratch = rest[ng_arr + 2:2 * ng_arr + 2], rest[2 * ng_arr + 2], rest[2 * ng_arr + 3:]
        i = pl.program_id(0)
        if ng_arr:
            g_start, g_forward, g_finish = _gather_phases(g_ins, g_outs, g_scratch[:ng_arr], *g_scratch[ng_arr:],
                                                          g_dtypes)
            pl.when(i == 0)(g_start)
            pl.when(i == (3 * n) // 4)(g_forward)

        @pl.when(i == 0)
        def _():
            s_ref[...] = jnp.zeros_like(s_ref)

        valid = (i * CHUNK + lax.broadcasted_iota(jnp.int32, (CHUNK, 1), 0)) >= pad
        s = s_ref[...]
        o, s2 = _gdn_chunk(q_ref[...], k_ref[...], v_ref[...], ba_ref[...], al_ref[...], dt_ref[...],
                           [s[h] for h in range(nh)], valid)
        st_ref[...] = s
        o_ref[...] = o
        for h in range(nh):
            s_ref[h] = s2[h]
        if ng_arr:
            pl.when(i == n - 1)(g_finish)

    w = nh * HEAD_W
    vec = pl.BlockSpec((1, HEAD_W), lambda i: (0, 0))
    return pl.pallas_call(
        body, name=name, grid=(n,),
        in_specs=[pl.BlockSpec((CHUNK, w), lambda i: (i, 0)), pl.BlockSpec((CHUNK, w), lambda i: (i, 1)),
                  pl.BlockSpec((CHUNK, w), lambda i: (i, 2)), pl.BlockSpec((CHUNK, HEAD_W), lambda i: (i, AB_BA // HEAD_W)),
                  vec, vec] + [pl.BlockSpec(memory_space=pltpu.VMEM)] * ng_arr,
        out_specs=[pl.BlockSpec((CHUNK, w), lambda i: (i, 0)),
                   pl.BlockSpec((None, nh, HEAD_W, HEAD_W), lambda i: (i, 0, 0, 0))] + [_ANY] * ng_arr,
        out_shape=[jax.ShapeDtypeStruct((lp, w), F32), jax.ShapeDtypeStruct((n, nh, HEAD_W, HEAD_W), F32)]
        + _gather_out_shapes(g_srcs, g_dtypes),
        scratch_shapes=[pltpu.VMEM((nh, HEAD_W, HEAD_W), F32)] + (_gather_scratch(g_srcs, g_dtypes) if ng_arr else []),
        compiler_params=_cp(("arbitrary",), has_side_effects=bool(ng_arr)),
    )(qkv, qkv, qkv, p0, alog_v, dtb_v, *g_srcs)


def _gdn_bwd(qkv, p0, alog_v, dtb_v, states, do, pad, *, name, scatter=()):
    lp = qkv.shape[0]
    n = lp // CHUNK
    nh = GDN_HEADS
    ns = len(scatter)

    def body(q_ref, k_ref, v_ref, ba_ref, al_ref, dt_ref, st_ref, do_ref, *rest):
        s_ins, (dq_ref, dk_ref, dv_ref, dba_ref, dal_ref, ddt_ref) = rest[:ns], rest[ns:ns + 6]
        s_outs, ds_ref, s_sems = rest[ns + 6:2 * ns + 6], rest[2 * ns + 6], rest[2 * ns + 7:]
        step = pl.program_id(0)
        i = n - 1 - step
        if ns:
            s_start, s_finish = _scatter_phases(s_ins, s_outs, *s_sems)
            pl.when(step == 0)(s_start)

        @pl.when(step == 0)
        def _():
            ds_ref[...] = jnp.zeros_like(ds_ref)
            dal_ref[...] = jnp.zeros_like(dal_ref)
            ddt_ref[...] = jnp.zeros_like(ddt_ref)

        valid = (i * CHUNK + lax.broadcasted_iota(jnp.int32, (CHUNK, 1), 0)) >= pad
        st, dst = st_ref[...], ds_ref[...]
        fn = functools.partial(_gdn_chunk, valid=valid)
        _, vjp = jax.vjp(fn, q_ref[...], k_ref[...], v_ref[...], ba_ref[...], al_ref[...], dt_ref[...],
                         [st[h] for h in range(nh)])
        dq, dk, dv, dba, dal, ddt, ds = vjp((do_ref[...], [dst[h] for h in range(nh)]))
        dq_ref[...] = dq
        dk_ref[...] = dk
        dv_ref[...] = dv
        dba_ref[...] = dba
        dal_ref[...] += dal
        ddt_ref[...] += ddt
        for h in range(nh):
            ds_ref[h] = ds[h]
        if ns:
            pl.when(step == n - 1)(s_finish)

    w = nh * HEAD_W
    rev = lambda c: (lambda s: (n - 1 - s, c))
    vec = pl.BlockSpec((1, HEAD_W), lambda s: (0, 0))
    return pl.pallas_call(
        body, name=name, grid=(n,),
        in_specs=[pl.BlockSpec((CHUNK, w), rev(0)), pl.BlockSpec((CHUNK, w), rev(1)), pl.BlockSpec((CHUNK, w), rev(2)),
                  pl.BlockSpec((CHUNK, HEAD_W), rev(AB_BA // HEAD_W)), vec, vec,
                  pl.BlockSpec((None, nh, HEAD_W, HEAD_W), lambda s: (n - 1 - s, 0, 0, 0)),
                  pl.BlockSpec((CHUNK, w), rev(0))] + [_ANY] * ns,
        out_specs=[pl.BlockSpec((CHUNK, w), rev(0)), pl.BlockSpec((CHUNK, w), rev(0)), pl.BlockSpec((CHUNK, w), rev(0)),
                   pl.BlockSpec((CHUNK, HEAD_W), rev(0)), vec, vec] + [_ANY] * ns,
        out_shape=[jax.ShapeDtypeStruct((lp, w), F32)] * 3 + [jax.ShapeDtypeStruct((lp, HEAD_W), F32)]
        + [jax.ShapeDtypeStruct((1, HEAD_W), F32)] * 2 + [jax.ShapeDtypeStruct(s.shape, s.dtype) for s in scatter],
        scratch_shapes=[pltpu.VMEM((nh, HEAD_W, HEAD_W), F32)] + (_scatter_scratch(ns) if ns else []),
        compiler_params=_cp(("arbitrary",), has_side_effects=bool(ns)),
    )(qkv, qkv, qkv, p0, alog_v, dtb_v, states, do, *scatter)


HG_LEVELS = (32, 16, 8, 4, 2, 1)
HG_GROUP = 4


def _hg_masks():
    import numpy as np
    c = CHUNK
    t = np.arange(c)[:, None]
    j = np.arange(c)[None, :]
    sums = (j <= t).astype(np.float32)
    pairs = [j == t]
    for m in HG_LEVELS:
        p = (t // (2 * m)) * (2 * m)
        r = p + m
        pairs.append((t >= r) & (j < r) & (j >= p))
    pairs = np.concatenate([np.kron(np.eye(HG_GROUP), p) for p in pairs], axis=0).astype(np.float32)
    return jnp.asarray(sums, BF16), jnp.asarray(sums.T, BF16), jnp.asarray(pairs, F32)


def _hg_level_row(b, m):
    c, w = b.shape
    if m >= 8:
        return jnp.concatenate([jnp.broadcast_to(b[p + m:p + m + 1], (2 * m, w)) for p in range(0, c, 2 * m)], axis=0)
    tiles = b.reshape(c // 8, 8, w)
    sub = lax.broadcasted_iota(jnp.int32, (1, 8, 1), 1)
    out = None
    for r0 in range(m, 8, 2 * m):
        cand = jnp.broadcast_to(tiles[:, r0:r0 + 1, :], tiles.shape)
        out = cand if out is None else jnp.where(sub >= r0 - m, cand, out)
    return out.reshape(c, w)


def _split3(x):
    hi = x.astype(BF16)
    r1 = x - hi.astype(F32)
    mid = r1.astype(BF16)
    return hi, mid, (r1 - mid.astype(F32)).astype(BF16)


def _mask_mm_raw(m, x):
    return sum(_dot(m, part, NN) for part in _split3(x))


@jax.custom_vjp
def _mask_mm(m, mt, x):
    return _mask_mm_raw(m, x)


def _mask_mm_fwd(m, mt, x):
    return _mask_mm_raw(m, x), (m, mt)


def _mask_mm_bwd(res, g):
    m, mt = res
    return jnp.zeros_like(m), jnp.zeros_like(mt), _mask_mm_raw(mt, g)


_mask_mm.defvjp(_mask_mm_fwd, _mask_mm_bwd)


def _hg_chunk(qr, fr, ir, lb, states, valid, sums, sums_t, pairs):
    nh = HG_GROUP
    c = qr.shape[0]
    r = nh * c
    fg = lb + (1.0 - lb) * _sigmoid(fr)
    logf = jnp.where(valid, jnp.log(fg), 0.0)
    k = jnp.where(valid, 1.0 - fg, 0.0)
    qs = jnp.where(valid, _silu(qr), 0.0)
    v = jnp.where(valid, ir, 0.0)
    b = _mask_mm(sums, sums_t, logf)
    mask = lambda n: pairs[n * r:(n + 1) * r]
    stack = lambda x: _heads_to_rows(x, nh)
    a = mask(0) * _bdot(stack(qs), stack(k), NT)
    for lvl, m in enumerate(HG_LEVELS):
        d = b - _hg_level_row(b, m)
        a = a + mask(1 + lvl) * _bdot(stack(qs * jnp.exp(jnp.minimum(d, 0.0))),
                                      stack(k * jnp.exp(jnp.minimum(-d, 0.0))), NT)
    av = _bdot(a, stack(v), NN)
    eb = jnp.exp(b)
    qe, kd = qs * eb, k * jnp.exp(b[c - 1:c] - b)
    outs, new_states = [], []
    for h in range(nh):
        cs = slice(h * HEAD_W, (h + 1) * HEAD_W)
        outs.append(_bdot(qe[:, cs], states[h], NT) + av[h * c:(h + 1) * c])
        new_states.append(states[h] * eb[c - 1:c, cs] + _bdot(v[:, cs], kd[:, cs], TN))
    return jnp.concatenate(outs, axis=1), new_states


def _hg_fwd(p1, lb, pad, *, name, gather=None):
    lp = p1.shape[0]
    n = lp // CHUNK
    nh = HG_HEADS
    g_srcs, g_dtypes = gather if gather is not None else ([], [])
    ng_arr = len(g_srcs)

    def body(q_ref, f_ref, i_ref, lb_ref, sums_ref, sums_t_ref, pairs_ref, *rest):
        g_ins, (o_ref, st_ref) = rest[:ng_arr], rest[ng_arr:ng_arr + 2]
        g_outs, s_ref, g_scratch = rest[ng_arr + 2:2 * ng_arr + 2], rest[2 * ng_arr + 2], rest[2 * ng_arr + 3:]
        i = pl.program_id(1)
        if ng_arr:
            g_start, g_forward, g_finish = _gather_phases(g_ins, g_outs, g_scratch[:ng_arr], *g_scratch[ng_arr:],
                                                          g_dtypes)
            last_group = pl.program_id(0) == ngrp - 1
            pl.when((pl.program_id(0) == 0) & (i == 0))(g_start)
            pl.when(last_group & (i == 0))(g_forward)

        @pl.when(i == 0)
        def _():
            s_ref[...] = jnp.zeros_like(s_ref)

        valid = (i * CHUNK + lax.broadcasted_iota(jnp.int32, (CHUNK, 1), 0)) >= pad
        s = s_ref[...]
        o, s2 = _hg_chunk(q_ref[...], f_ref[...], i_ref[...], lb_ref[...], [s[h] for h in range(grp)], valid,
                          sums_ref[...], sums_t_ref[...], pairs_ref[...])
        st_ref[...] = s
        o_ref[...] = o
        for h in range(grp):
            s_ref[h] = s2[h]
        if ng_arr:
            pl.when(last_group & (i == n - 1))(g_finish)

    masks = _hg_masks()
    grp, ngrp, gw = HG_GROUP, nh // HG_GROUP, HG_GROUP * HEAD_W
    blk = lambda off: pl.BlockSpec((CHUNK, gw), lambda h, i: (i, off + h))
    const = lambda a: pl.BlockSpec(a.shape, lambda h, i: (0, 0))
    return pl.pallas_call(
        body, name=name, grid=(ngrp, n),
        in_specs=[blk(0), blk(ngrp), blk(2 * ngrp), pl.BlockSpec((1, gw), lambda h, i: (0, h))]
        + [const(a) for a in masks] + [pl.BlockSpec(memory_space=pltpu.VMEM)] * ng_arr,
        out_specs=[blk(0), pl.BlockSpec((grp, None, HEAD_W, HEAD_W), lambda h, i: (h, i, 0, 0))] + [_ANY] * ng_arr,
        out_shape=[jax.ShapeDtypeStruct((lp, nh * HEAD_W), F32), jax.ShapeDtypeStruct((nh, n, HEAD_W, HEAD_W), F32)]
        + _gather_out_shapes(g_srcs, g_dtypes),
        scratch_shapes=[pltpu.VMEM((grp, HEAD_W, HEAD_W), F32)] + (_gather_scratch(g_srcs, g_dtypes) if ng_arr else []),
        compiler_params=_cp(("arbitrary", "arbitrary"), has_side_effects=bool(ng_arr)),
    )(p1, p1, p1, lb, *masks, *g_srcs)


def _hg_bwd(p1, lb, states, do, pad, *, name, scatter=()):
    lp = p1.shape[0]
    n = lp // CHUNK
    nh = HG_HEADS
    ns = len(scatter)

    def body(q_ref, f_ref, i_ref, lb_ref, st_ref, do_ref, sums_ref, sums_t_ref, pairs_ref, *rest):
        s_ins, (dq_ref, df_ref, di_ref, dlb_ref) = rest[:ns], rest[ns:ns + 4]
        s_outs, ds_ref, s_sems = rest[ns + 4:2 * ns + 4], rest[2 * ns + 4], rest[2 * ns + 5:]
        step = pl.program_id(1)
        i = n - 1 - step
        if ns:
            s_start, s_finish = _scatter_phases(s_ins, s_outs, *s_sems)
            pl.when((pl.program_id(0) == 0) & (step == 0))(s_start)

        @pl.when(step == 0)
        def _():
            ds_ref[...] = jnp.zeros_like(ds_ref)
            dlb_ref[...] = jnp.zeros_like(dlb_ref)

        valid = (i * CHUNK + lax.broadcasted_iota(jnp.int32, (CHUNK, 1), 0)) >= pad
        fn = functools.partial(_hg_chunk, valid=valid, sums=sums_ref[...], sums_t=sums_t_ref[...],
                               pairs=pairs_ref[...])
        st, dst = st_ref[...], ds_ref[...]
        _, vjp = jax.vjp(fn, q_ref[...], f_ref[...], i_ref[...], lb_ref[...], [st[h] for h in range(grp)])
        dq, df, di, dlb, ds = vjp((do_ref[...], [dst[h] for h in range(grp)]))
        dq_ref[...] = dq
        df_ref[...] = df
        di_ref[...] = di
        dlb_ref[...] += dlb
        for h in range(grp):
            ds_ref[h] = ds[h]
        if ns:
            pl.when((pl.program_id(0) == ngrp - 1) & (step == n - 1))(s_finish)

    masks = _hg_masks()
    grp, ngrp, gw = HG_GROUP, nh // HG_GROUP, HG_GROUP * HEAD_W
    blk = lambda off: pl.BlockSpec((CHUNK, gw), lambda h, s: (n - 1 - s, off + h))
    const = lambda a: pl.BlockSpec(a.shape, lambda h, s: (0, 0))
    w = nh * HEAD_W
    return pl.pallas_call(
        body, name=name, grid=(ngrp, n),
        in_specs=[blk(0), blk(ngrp), blk(2 * ngrp), pl.BlockSpec((1, gw), lambda h, s: (0, h)),
                  pl.BlockSpec((grp, None, HEAD_W, HEAD_W), lambda h, s: (h, n - 1 - s, 0, 0)), blk(0)]
        + [const(a) for a in masks] + [_ANY] * ns,
        out_specs=[blk(0), blk(0), blk(0), pl.BlockSpec((1, gw), lambda h, s: (0, h))] + [_ANY] * ns,
        out_shape=[jax.ShapeDtypeStruct((lp, w), F32)] * 3 + [jax.ShapeDtypeStruct((1, w), F32)]
        + [jax.ShapeDtypeStruct(s.shape, s.dtype) for s in scatter],
        scratch_shapes=[pltpu.VMEM((grp, HEAD_W, HEAD_W), F32)] + (_scatter_scratch(ns) if ns else []),
        compiler_params=_cp(("arbitrary", "arbitrary"), has_side_effects=bool(ns)),
    )(p1, p1, p1, lb, states, do, *masks, *scatter)


SB_GROUP = 4
SB_FAR = -110.0


def _sb_cat(kind, first_key=0):
    r = lax.broadcasted_iota(jnp.int32, (SB_BLOCK, 2 * SB_BLOCK), 0)
    c = lax.broadcasted_iota(jnp.int32, (SB_BLOCK, 2 * SB_BLOCK), 1)
    tri = {"after": c < r, "incl": r <= c, "before": r < c}[kind]
    m = ((c >= SB_BLOCK) | tri) & (r >= first_key)
    return jnp.where(m, 1.0, 0.0).astype(BF16)


def _sb_cumsum(x, cat):
    return _dot(x.astype(BF16), cat, NN)


def _sb_logsig(z):
    e = jnp.exp(-jnp.abs(z))
    lse = jnp.where(e < 1e-4, e, jnp.log(1.0 + e))
    lsz = jnp.minimum(z, 0.0) - lse
    return lsz, lsz - z, e


def _sb_stack(x, scale=None):
    lane = lax.broadcasted_iota(jnp.int32, (1, HEAD_W), 1)
    if scale is not None:
        x = x * scale
    return jnp.concatenate([jnp.where(lane < SB_DH, x, 0.0), jnp.where(lane >= SB_DH, x, 0.0)], axis=0).astype(BF16)


def _sb_unstack(x):
    lane = lax.broadcasted_iota(jnp.int32, (1, HEAD_W), 1)
    return jnp.where(lane < SB_DH, x[:SB_BLOCK], x[SB_BLOCK:])


def _sb_fwd(p0, pad, *, name, gather=None):
    lp = p0.shape[0]
    nb = lp // SB_BLOCK
    npair = SB_HEADS // 2
    blk0 = AB_SB // HEAD_W
    scale = SB_DH ** -0.5
    gw = SB_GROUP * SB_BLOCK
    assert pad < SB_BLOCK
    g_srcs, g_dtypes = gather if gather is not None else ([], [])
    ng_arr = len(g_srcs)

    def body(q_ref, k_ref, v_ref, *rest):
        g_ins, (o_ref, tot_ref, nproc_ref) = rest[:ng_arr], rest[ng_arr:ng_arr + 3]
        g_outs, g_scratch = rest[ng_arr + 3:2 * ng_arr + 3], rest[2 * ng_arr + 3:]
        first_step = (pl.program_id(0) == 0) & (pl.program_id(1) == 0)
        last_pair = pl.program_id(0) == npair - 1
        if ng_arr:
            g_start, g_forward, g_finish = _gather_phases(g_ins, g_outs, g_scratch[:ng_arr], *g_scratch[ng_arr:],
                                                          g_dtypes)
            pl.when(first_step)(g_start)
            pl.when(last_pair & (pl.program_id(1) == 0))(g_forward)
        i = pl.program_id(1)
        qs = _sb_stack(q_ref[...], scale)
        qpos = i * SB_BLOCK + lax.broadcasted_iota(jnp.int32, (SB_BLOCK, 1), 0)
        qpos = jnp.concatenate([qpos, qpos], axis=0)
        cat = _sb_cat("after")
        cat0 = _sb_cat("after", pad)
        ng = i // SB_GROUP

        def group(off, nblk, first_cat, allowed, carry):
            acc, run = carry
            kg = k_ref[pl.ds(off, nblk * SB_BLOCK), :].astype(BF16)
            vg = v_ref[pl.ds(off, nblk * SB_BLOCK), :].astype(BF16)
            lsz, l1m, _ = _sb_logsig(_dot(qs, kg, NT))
            if allowed is not None:
                l1m = jnp.where(allowed, l1m, 0.0)
            args = [None] * nblk
            for g in reversed(range(nblk)):
                sl = slice(g * SB_BLOCK, (g + 1) * SB_BLOCK)
                al = _sb_cumsum(l1m[:, sl], first_cat if g == 0 else cat)
                args[g] = lsz[:, sl] + al[:, :SB_BLOCK] + run
                run = run + al[:, SB_BLOCK:]
            wgt = jnp.exp(jnp.concatenate(args, axis=1))
            if allowed is not None:
                wgt = jnp.where(allowed, wgt, 0.0)
            return acc + _dot(wgt.astype(BF16), vg, NN), run

        def below(t, carry):
            gi = ng - 1 - t
            return group(pl.multiple_of(gi * gw, gw), SB_GROUP, jnp.where(gi == 0, cat0, cat), None, carry)

        top = ng * gw

        def top_group(nblk, carry):
            off = pl.multiple_of(jnp.minimum(top, lp - nblk * SB_BLOCK), SB_BLOCK)
            kpos = off + lax.broadcasted_iota(jnp.int32, (1, nblk * SB_BLOCK), 1)
            return group(off, nblk, cat, (kpos < qpos) & (kpos >= pad) & (kpos >= top), carry)

        zero = (jnp.zeros((2 * SB_BLOCK, HEAD_W), F32), jnp.zeros((2 * SB_BLOCK, HEAD_W), F32))
        carry = lax.cond(i - ng * SB_GROUP < SB_GROUP // 2, functools.partial(top_group, SB_GROUP // 2),
                         functools.partial(top_group, SB_GROUP), zero)
        used, acc, run = lax.while_loop(lambda s: (s[0] < ng) & (jnp.max(s[2]) > SB_FAR),
                                        lambda s: (s[0] + 1, *below(s[0], (s[1], s[2]))), (jnp.int32(0), *carry))
        o_ref[...] = _sb_unstack(acc)
        tot_ref[...] = _sb_unstack(run)
        nproc_ref[pl.program_id(0), i] = used.astype(F32)
        if ng_arr:
            pl.when(last_pair & (pl.program_id(1) == nb - 1))(g_finish)

    full = lambda c0: pl.BlockSpec((lp, HEAD_W), lambda p, i: (0, c0 + p))
    out = pl.BlockSpec((SB_BLOCK, HEAD_W), lambda p, i: (i, p))
    return pl.pallas_call(
        body, name=name, grid=(npair, nb),
        in_specs=[pl.BlockSpec((SB_BLOCK, HEAD_W), lambda p, i: (i, blk0 + p)), full(blk0 + npair), full(blk0 + 2 * npair)]
        + [pl.BlockSpec(memory_space=pltpu.VMEM)] * ng_arr,
        out_specs=[out, out, pl.BlockSpec(memory_space=pltpu.SMEM)] + [_ANY] * ng_arr,
        out_shape=[jax.ShapeDtypeStruct((lp, npair * HEAD_W), F32)] * 2 + [jax.ShapeDtypeStruct((npair, nb), F32)]
        + _gather_out_shapes(g_srcs, g_dtypes),
        scratch_shapes=_gather_scratch(g_srcs, g_dtypes) if ng_arr else [],
        compiler_params=_cp(("arbitrary", "arbitrary"), has_side_effects=bool(ng_arr)),
    )(p0, p0, p0, *g_srcs)


def _sb_bwd(p0, tot, nproc, dsrc, d_blk0, pad, *, name, scatter=()):
    lp = p0.shape[0]
    nb = lp // SB_BLOCK
    npair = SB_HEADS // 2
    blk0 = AB_SB // HEAD_W
    scale = SB_DH ** -0.5
    gw = SB_GROUP * SB_BLOCK
    assert pad < SB_BLOCK
    ns = len(scatter)

    def body(q_ref, k_ref, v_ref, tot_ref, nproc_ref, do_ref, *rest):
        s_ins, (dq_ref, dkt_ref, dvt_ref) = rest[:ns], rest[ns:ns + 3]
        s_outs, s_sems = rest[ns + 3:2 * ns + 3], rest[2 * ns + 3:]
        if ns:
            s_start, s_finish = _scatter_phases(s_ins, s_outs, *s_sems)
            pl.when((pl.program_id(0) == 0) & (pl.program_id(1) == 0))(s_start)
        i = pl.program_id(1)

        @pl.when(i == 0)
        def _():
            dkt_ref[...] = jnp.zeros_like(dkt_ref)
            dvt_ref[...] = jnp.zeros_like(dvt_ref)

        qs = _sb_stack(q_ref[...], scale)
        dos = _sb_stack(do_ref[...])
        qst, dost = qs.T, dos.T
        totv = tot_ref[...]
        ones = jnp.ones((1, HEAD_W), F32)
        tots = jnp.concatenate([totv[:, 0:1] * ones, totv[:, SB_DH:SB_DH + 1] * ones], axis=0)
        qpos = i * SB_BLOCK + lax.broadcasted_iota(jnp.int32, (SB_BLOCK, 1), 0)
        qpos = jnp.concatenate([qpos, qpos], axis=0)
        incl, incl0 = _sb_cat("incl"), _sb_cat("incl", pad)
        before = _sb_cat("before")
        ng = i // SB_GROUP
        used = jnp.clip(nproc_ref[pl.program_id(0), i].astype(jnp.int32), 0, ng)

        def dscore(z, e, ev, dl1m):
            r = 1.0 / (1.0 + e)
            sg = jnp.where(z >= 0, r, e * r)
            return ev * (1.0 - sg) - dl1m * sg

        def group(off, nblk, first_incl, allowed, carry):
            dq, prun, erun = carry
            width = nblk * SB_BLOCK
            kg = k_ref[pl.ds(off, width), :].astype(BF16)
            vg = v_ref[pl.ds(off, width), :].astype(BF16)
            z = _dot(qs, kg, NT)
            lsz, l1m, e = _sb_logsig(z)
            if allowed is not None:
                l1m = jnp.where(allowed, l1m, 0.0)
            dwgt = _dot(dos, vg, NT)
            dzs = [None] * nblk
            wgts = [None] * nblk
            for g in range(nblk):
                sl = slice(g * SB_BLOCK, (g + 1) * SB_BLOCK)
                al = _sb_cumsum(l1m[:, sl], first_incl if g == 0 else incl)
                wgt = jnp.exp(jnp.minimum(lsz[:, sl] + (tots - prun - al[:, :SB_BLOCK]), 0.0))
                if allowed is not None:
                    wgt = jnp.where(allowed[:, sl], wgt, 0.0)
                prun = prun + al[:, SB_BLOCK:]
                ev = wgt * dwgt[:, sl]
                el = _sb_cumsum(ev, before)
                dzs[g] = dscore(z[:, sl], e[:, sl], ev, erun + el[:, :SB_BLOCK])
                erun = erun + el[:, SB_BLOCK:]
                wgts[g] = wgt
            dz = jnp.concatenate(dzs, axis=1)
            if allowed is not None:
                dz = jnp.where(allowed, dz, 0.0)
            dz = dz.astype(BF16)
            wg = jnp.concatenate(wgts, axis=1).astype(BF16)
            dkt_ref[:, pl.ds(off, width)] += _dot(qst, dz, NN)
            dvt_ref[:, pl.ds(off, width)] += _dot(dost, wg, NN)
            return dq + _dot(dz, kg, NN), prun, erun

        def below(gi, carry):
            return group(pl.multiple_of(gi * gw, gw), SB_GROUP, jnp.where(gi == 0, incl0, incl), None, carry)

        zero = tuple(jnp.zeros((2 * SB_BLOCK, HEAD_W), F32) for _ in range(3))
        carry = lax.fori_loop(ng - used, ng, below, zero)
        top = ng * gw

        def top_group(nblk, carry):
            off = pl.multiple_of(jnp.minimum(top, lp - nblk * SB_BLOCK), SB_BLOCK)
            kpos = off + lax.broadcasted_iota(jnp.int32, (1, nblk * SB_BLOCK), 1)
            return group(off, nblk, incl, (kpos < qpos) & (kpos >= pad) & (kpos >= top), carry)

        dq, _, _ = lax.cond(i - ng * SB_GROUP < SB_GROUP // 2, functools.partial(top_group, SB_GROUP // 2),
                            functools.partial(top_group, SB_GROUP), carry)
        dq_ref[...] = _sb_unstack(dq) * scale
        if ns:
            pl.when((pl.program_id(0) == npair - 1) & (pl.program_id(1) == nb - 1))(s_finish)

    full = lambda c0: pl.BlockSpec((lp, HEAD_W), lambda p, i: (0, c0 + p))
    qb = lambda c0: pl.BlockSpec((SB_BLOCK, HEAD_W), lambda p, i: (i, c0 + p))
    tr = pl.BlockSpec((HEAD_W, lp), lambda p, i: (p, 0))
    return pl.pallas_call(
        body, name=name, grid=(npair, nb),
        in_specs=[qb(blk0), full(blk0 + npair), full(blk0 + 2 * npair), qb(0), pl.BlockSpec(memory_space=pltpu.SMEM),
                  qb(d_blk0)] + [_ANY] * ns,
        out_specs=[qb(0), tr, tr] + [_ANY] * ns,
        out_shape=[jax.ShapeDtypeStruct((lp, npair * HEAD_W), F32)]
        + [jax.ShapeDtypeStruct((npair * HEAD_W, lp), F32)] * 2
        + [jax.ShapeDtypeStruct(s.shape, s.dtype) for s in scatter],
        scratch_shapes=_scatter_scratch(ns) if ns else [],
        compiler_params=_cp(("arbitrary", "arbitrary"), has_side_effects=bool(ns)),
    )(p0, p0, p0, tot, nproc, dsrc, *scatter)


def _local_step(h0, target, pad, wts, hooks=None):
    lp = h0.shape[0]
    tm = _row_tile(lp, 1056)
    tkl = tm
    tml = _row_tile(lp, 528)
    d = D_MODEL
    mm = _mm
    mmw = functools.partial(_mm, out_dtype=BF16)
    g = {}

    h0_b = h0.astype(BF16)
    p0 = mm(h0_b, wts["w_ab"], "NN", tm=tm, tn=768, tk=d, name="l0_in_proj")
    ob, sb_tot, sb_used, *gathered = _sb_fwd(p0, pad, name="sb_fwd", gather=hooks["gather_a"] if hooks else None)
    if hooks:
        wts = {**wts, **hooks["weights_a"](gathered)}
    qkv = _gdn_pre_fwd(p0, wts["conv_w"], pad, name="gdn_pre_fwd")
    oa_raw, gdn_states, *gathered = _gdn_fwd(qkv, p0, wts["alog_v"], wts["dtb_v"], pad, name="gdn_fwd",
                                             gather=hooks["gather_b"] if hooks else None)
    if hooks:
        wts = {**wts, **hooks["weights_b"](gathered)}
    rows = lambda a, n: a.reshape(N_DEV, n // N_DEV, d)
    parts = g["parts"] = {}
    oab = _gate_fwd(oa_raw, p0, AB_Z // HEAD_W, wts["ab_gn"], ob, heads=GDN_HEADS, name="gdn_gate_fwd")
    ln = lambda kind, layer: (wts[f"ln_{kind}_g"][layer], wts[f"ln_{kind}_b"][layer])
    pre_mix0, h0a, h0a_b = mm(oab, wts["w_out0"], "NN", tm=tml, tn=d, tk=d, epi="ln", c=h0, scale=DN_ALPHA,
                              ln=ln("mix", 0), name="l0_out_proj")
    u0, act0 = mm(h0a_b, wts["w1"][0], "NN", tm=tm, tn=512, tk=d, b_dev=True, epi="relu2_copy", name="mlp0_up")
    pre_ffn0, h0b, h0b_b = mm(act0, wts["w2"][0], "NN", tm=tml, tn=d, tk=d, epi="ln", c=h0a, scale=DN_ALPHA,
                              ln=ln("ffn", 0), name="mlp0_down")
    p1 = mm(h0b_b, wts["w_c"], "NN", tm=tm, tn=512, tk=d, b_dev=True, name="l1_in_proj")
    oc_raw, hg_states, *gathered = _hg_fwd(p1, wts["lb"], pad, name="hg_fwd",
                                           gather=hooks["gather_c"] if hooks else None)
    if hooks:
        third = hooks["weights_c"](gathered)
        wts = {**wts, "w1": wts["w1"] + third["w1"], "w2": wts["w2"] + third["w2"]}
    oc = _gate_fwd(oc_raw, p1, 3 * HG_HEADS, wts["c_gn"], oc_raw, heads=HG_HEADS, name="hg_gate_fwd")
    pre_mix1, h1a, h1a_b = mm(oc, wts["w_out1"], "NN", tm=tml, tn=d, tk=d, epi="ln", c=h0b, scale=DN_ALPHA,
                              ln=ln("mix", 1), name="l1_out_proj")
    u1, act1 = mm(h1a_b, wts["w1"][1], "NN", tm=tm, tn=512, tk=d, b_dev=True, epi="relu2_copy", name="mlp1_up")
    pre_ffn1, h1b, _ = mm(act1, wts["w2"][1], "NN", tm=tml, tn=d, tk=d, epi="ln", c=h1a, scale=DN_ALPHA,
                          ln=ln("ffn", 1), name="mlp1_down")
    dy, loss_vec = _loss_head(h1b, target, name="loss_head")

    def mlp_bwd(layer, h_in_b, u, act, dpre, dpre_b, pre_mix):
        du = mm(dpre_b, wts["w2"][layer], "NT", tm=tm, tn=1024, tk=d, epi="relu2grad", c=u, out_dtype=BF16,
                name=f"mlp{layer}_d_hidden")
        dw2 = mmw(act, dpre_b, "TN", tm=1024, tn=1024, tk=tkl, name=f"mlp{layer}_dw2")
        dw1 = mmw(h_in_b, du, "TN", tm=1024, tn=512, tk=tkl, out_dev=True, name=f"mlp{layer}_dw1")
        return (*mm(du, k_major(wts["w1"][layer]), "NT", tm=tml, tn=1024, tk=2048, epi="ln_bwd", c=dpre, scale=DN_ALPHA,
                    ln=(pre_mix, wts["ln_mix_g"][layer]), name=f"mlp{layer}_d_in"), dw1, dw2)

    k_major = lambda wd: wd.transpose(1, 0, 2).reshape(wd.shape[1], -1)

    ln_ffn_dg, ln_ffn_db, ln_mix_dg, ln_mix_db, dw1s, dw2s = ([None, None] for _ in range(6))
    dpre, dpre_b, ln_ffn_dg[1], ln_ffn_db[1] = _ln_bwd(pre_ffn1, wts["ln_ffn_g"][1], dy, name="ln_ffn1_bwd")
    dpre, dpre_b, ln_mix_dg[1], ln_mix_db[1], dw1s[1], dw2s[1] = mlp_bwd(1, h1a_b, u1, act1, dpre, dpre_b, pre_mix1)
    g["c_w_out"] = mmw(oc, dpre_b, "TN", tm=1024, tn=1024, tk=tkl, name="l1_dw_out")
    doc = mm(dpre_b, wts["w_out1"], "NT", tm=tm, tn=1024, tk=d, name="l1_d_gate")
    doc_raw, dz1, g["c_gn"] = _gate_bwd(oc_raw, p1, 3 * HG_HEADS, wts["c_gn"], doc, heads=HG_HEADS, name="hg_gate_bwd")
    ready = [dw1s[1], rows(dw2s[1], D_FF), rows(g["c_w_out"], d)] if hooks else ()
    dq1, df1, di1, g["lb"], *got = _hg_bwd(p1, wts["lb"], hg_states, doc_raw, pad, name="hg_bwd", scatter=ready)
    parts.update(zip(("mlp_w1_1", "mlp_w2_1", "c_w_out"), got))
    dp1 = jnp.concatenate([dq1, df1, di1, dz1], axis=1).astype(BF16)
    g["c_w_in"] = mmw(h0b_b, dp1, "TN", tm=1024, tn=512, tk=tkl, out_dev=True, name="l1_dw_in")
    dpre, dpre_b, ln_ffn_dg[0], ln_ffn_db[0] = mm(
        dp1, k_major(wts["w_c"]), "NT", tm=tml, tn=1024, tk=2048, epi="ln_bwd", c=dpre, scale=DN_ALPHA,
        ln=(pre_ffn0, wts["ln_ffn_g"][0]), name="l1_d_in")
    dpre, dpre_b, ln_mix_dg[0], ln_mix_db[0], dw1s[0], dw2s[0] = mlp_bwd(0, h0a_b, u0, act0, dpre, dpre_b, pre_mix0)
    g["ab_w_out"] = mmw(oab, dpre_b, "TN", tm=1024, tn=1024, tk=tkl, name="l0_dw_out")
    doab = mm(dpre_b, wts["w_out0"], "NT", tm=tm, tn=1024, tk=d, name="l0_d_gate")
    doa_raw, dz0, g["ab_gn"] = _gate_bwd(oa_raw, p0, AB_Z // HEAD_W, wts["ab_gn"], doab, heads=GDN_HEADS,
                                         name="gdn_gate_bwd")
    ready = [g["c_w_in"]] if hooks else ()
    dqb, dkb_t, dvb_t, *got = _sb_bwd(p0, sb_tot, sb_used, doab, GDN_HEADS, pad, name="sb_bwd", scatter=ready)
    parts.update(zip(("c_w_in",), got))
    dkb, dvb = dkb_t.T, dvb_t.T
    ready = [dw1s[0], rows(dw2s[0], D_FF), rows(g["ab_w_out"], d)] if hooks else ()
    dqn, dkn, dvn, dba, g["alog_v"], g["dtb_v"], *got = _gdn_bwd(qkv, p0, wts["alog_v"], wts["dtb_v"], gdn_states,
                                                                 doa_raw, pad, name="gdn_bwd", scatter=ready)
    parts.update(zip(("mlp_w1_0", "mlp_w2_0", "ab_w_out"), got))
    dconv_in, g["conv_w"] = _gdn_pre_bwd(p0, wts["conv_w"], jnp.concatenate([dqn, dkn, dvn], axis=1), pad,
                                         name="gdn_pre_bwd")
    dp0 = jnp.concatenate([dconv_in, dz0, dqb, dkb, dvb, dba, jnp.zeros((lp, AB_CAT - AB_BA - HEAD_W), F32)],
                          axis=1).astype(BF16)
    g["w_ab"] = mmw(h0_b, dp0, "TN", tm=1024, tn=768, tk=tkl, name="l0_dw_in")
    last = ()
    if hooks:
        gab, ba0 = g["w_ab"], AB_Z + GDN_HEADS * HEAD_W
        gab = jnp.concatenate([gab[:, :ba0], gab[:, AB_BA:AB_BA + 2 * GDN_HEADS], gab[:, ba0:AB_BA]], axis=1)
        last = [gab.reshape(d, N_DEV, AB_IN // N_DEV).transpose(1, 0, 2)]
    res = mm(dp0, wts["w_ab"], "NT", tm=tm, tn=1024, tk=1920, epi="add", c=dpre, scale=DN_ALPHA, scatter=last,
             name="l0_d_in")
    dh0 = res[0] if last else res
    parts.update(zip(("ab_w_in",), res[1:] if last else ()))

    g["w1"], g["w2"] = dw1s, dw2s
    g["ln_mix_g"] = jnp.concatenate(ln_mix_dg, axis=0)
    g["ln_mix_b"] = jnp.concatenate(ln_mix_db, axis=0)
    g["ln_ffn_g"] = jnp.concatenate(ln_ffn_dg, axis=0)
    g["ln_ffn_b"] = jnp.concatenate(ln_ffn_db, axis=0)
    return loss_vec, dh0, g


N_CHIP = N_DEV // 2


def _place():
    x, y, c = lax.axis_index("x"), lax.axis_index("y"), lax.axis_index("c")
    return x, y, c, 2 * x + y


def _chip_dev(chip, core):
    return (chip // 2, chip % 2, core)


def _remote(src, dst, send_sem, recv_sem, dev):
    return pltpu.make_async_remote_copy(src_ref=src, dst_ref=dst, send_sem=send_sem, recv_sem=recv_sem,
                                        device_id=dev, device_id_type=pl.DeviceIdType.MESH)


_ANY = pl.BlockSpec(memory_space=pl.ANY)


def _gather(srcs, dtypes, *, name):
    n = len(srcs)

    def body(*refs):
        start, forward, finish = _gather_phases(refs[:n], refs[n:2 * n], refs[2 * n:3 * n], *refs[3 * n:], dtypes)
        start()
        forward()
        finish()

    return pl.pallas_call(
        body, name=name, in_specs=[pl.BlockSpec(memory_space=pltpu.VMEM)] * n, out_specs=[_ANY] * n,
        out_shape=_gather_out_shapes(srcs, dtypes), scratch_shapes=_gather_scratch(srcs, dtypes),
        compiler_params=_cp(has_side_effects=True),
    )(*srcs)


def _gather_out_shapes(srcs, dtypes):
    return [jax.ShapeDtypeStruct((N_DEV, *s.shape), dt) for s, dt in zip(srcs, dtypes)]


def _gather_scratch(srcs, dtypes):
    n = len(srcs)
    return [pltpu.VMEM(s.shape, dt) for s, dt in zip(srcs, dtypes)] + [
        pltpu.SemaphoreType.DMA((n, 2 * N_CHIP - 1)), pltpu.SemaphoreType.DMA((n, 2 * N_CHIP - 1)),
        pltpu.SemaphoreType.DMA((n,))]


def _gather_phases(ins, outs, stages, send_sems, recv_sems, local_sems, dtypes):
    n = len(ins)
    x, y, c, chip = _place()
    me = 2 * chip + c
    sibling = (x, y, 1 - c)

    def own(i):
        cps = [_remote(stages[i], outs[i].at[me], send_sems.at[i, 0], recv_sems.at[i, 0], sibling)]
        for j in range(1, N_CHIP):
            cps.append(_remote(stages[i], outs[i].at[me], send_sems.at[i, j], recv_sems.at[i, j],
                               _chip_dev(jnp.bitwise_xor(chip, j), c)))
        return cps

    def local(i):
        return pltpu.make_async_copy(stages[i], outs[i].at[me], local_sems.at[i])

    def passed_on(i, j):
        slot = outs[i].at[2 * jnp.bitwise_xor(chip, j) + c]
        return _remote(slot, slot, send_sems.at[i, N_CHIP - 1 + j], recv_sems.at[i, N_CHIP - 1 + j], sibling)

    def start():
        for i in range(n):
            stages[i][...] = ins[i][...].astype(dtypes[i])
            local(i).start()
            for cp in own(i):
                cp.start()

    def forward():
        for i in range(n):
            for j in range(1, N_CHIP):
                own(i)[j].wait_recv()
                passed_on(i, j).start()

    def finish():
        for i in range(n):
            own(i)[0].wait_recv()
            for j in range(1, N_CHIP):
                passed_on(i, j).wait_recv()
        for i in range(n):
            for cp in own(i):
                cp.wait_send()
            for j in range(1, N_CHIP):
                passed_on(i, j).wait_send()
            local(i).wait()

    return start, forward, finish


def _scatter_scratch(n):
    return [pltpu.SemaphoreType.DMA((n, N_DEV - 1)), pltpu.SemaphoreType.DMA((n, N_DEV - 1)),
            pltpu.SemaphoreType.DMA((n,))]


def _scatter_phases(ins, outs, send_sems, recv_sems, local_sems):
    n = len(ins)
    _, _, c, chip = _place()
    me = 2 * chip + c

    def copies():
        cps = []
        for i in range(n):
            cps.append(pltpu.make_async_copy(ins[i].at[me], outs[i].at[me], local_sems.at[i]))
            for k in range(1, N_DEV):
                peer = jnp.bitwise_xor(me, k)
                cps.append(_remote(ins[i].at[peer], outs[i].at[me], send_sems.at[i, k - 1], recv_sems.at[i, k - 1],
                                   _chip_dev(peer // 2, peer % 2)))
        return cps

    def start():
        for cp in copies():
            cp.start()

    def finish():
        for cp in copies():
            cp.wait()

    return start, finish


def _adamw(w, parts, m, v, *, name):
    r, c = w.shape
    s = parts.shape[0]
    tm = _row_tile(r, 128) if r % 8 == 0 else r
    c1 = 1.0 - ADAM_B1 ** ADAM_STEP
    c2 = 1.0 - ADAM_B2 ** ADAM_STEP

    def body(w_ref, p_ref, m_ref, v_ref, g_ref, d_ref, m2_ref, v2_ref):
        g = p_ref[0].astype(F32)
        for j in range(1, s):
            g = g + p_ref[j].astype(F32)
        m2 = ADAM_B1 * m_ref[...] + (1.0 - ADAM_B1) * g
        v2 = ADAM_B2 * v_ref[...] + (1.0 - ADAM_B2) * jnp.square(g)
        g_ref[...] = g
        m2_ref[...] = m2
        v2_ref[...] = v2
        d_ref[...] = -ADAM_LR * ((m2 / c1) / (jnp.sqrt(v2 / c2) + ADAM_EPS) + ADAM_WD * w_ref[...])

    blk = pl.BlockSpec((tm, c), lambda i: (i, 0))
    return pl.pallas_call(
        body, name=name, grid=(r // tm,),
        in_specs=[blk, pl.BlockSpec((s, tm, c), lambda i: (0, i, 0)), blk, blk], out_specs=[blk] * 4,
        out_shape=[jax.ShapeDtypeStruct((r, c), F32)] * 4, compiler_params=_cp(("parallel",)),
    )(w, parts, m, v)


_WEIGHTS = ("meta_tokens", "ab_w_in", "ab_conv_w", "ab_a_log", "ab_dt_bias", "ab_gnorm_g", "ab_w_out", "c_w_in",
            "c_lb_raw", "c_gnorm_g", "c_w_out", "ln_mix_g", "ln_mix_b", "mlp_w1", "mlp_w2", "ln_ffn_g", "ln_ffn_b")
_PACK_ROWS = (("ln_mix_g", 0), ("ln_mix_b", 2), ("ln_ffn_g", 4), ("ln_ffn_b", 6), ("c_lb_raw", 8))
_PACK_MISC_ROW = 10
_PACK_MISC = (("ab_gnorm_g", 0, 128), ("c_gnorm_g", 128, 128), ("ab_a_log", 256, GDN_HEADS), ("ab_dt_bias", 260, GDN_HEADS))
_PACK_N = 16
_SMALL_META = 16
_SMALL_CONV = 32
_SMALL_N = 40


def _pack_replicated(p):
    rows = jnp.zeros((_PACK_N, D_MODEL), F32)
    for name, r0 in _PACK_ROWS:
        rows = rows.at[r0:r0 + 2].set(p[name])
    for name, c0, width in _PACK_MISC:
        rows = rows.at[_PACK_MISC_ROW, c0:c0 + width].set(p[name].reshape(width))
    return rows


def _unpack_replicated(rows, like):
    out = {}
    for name, r0 in _PACK_ROWS:
        out[name] = rows[r0:r0 + 2]
    for name, c0, width in _PACK_MISC:
        out[name] = rows[_PACK_MISC_ROW, c0:c0 + width].reshape(like[name].shape)
    return out


def _lower_bound(c_lb_raw):
    lb_all = jnp.cumsum(jax.nn.softmax(c_lb_raw.astype(F32), axis=0), axis=0)
    return (lb_all - lb_all[0:1])[1].reshape(1, -1)


def kernel(x, meta_tokens, ab_w_in, ab_conv_w, ab_a_log, ab_dt_bias, ab_gnorm_g, ab_w_out, c_w_in, c_lb_raw, c_gnorm_g, c_w_out, ln_mix_g, ln_mix_b, mlp_w1, mlp_w2, ln_ffn_g, ln_ffn_b, loss_target, m_meta_tokens, m_ab_w_in, m_ab_conv_w, m_ab_a_log, m_ab_dt_bias, m_ab_gnorm_g, m_ab_w_out, m_c_w_in, m_c_lb_raw, m_c_gnorm_g, m_c_w_out, m_ln_mix_g, m_ln_mix_b, m_mlp_w1, m_mlp_w2, m_ln_ffn_g, m_ln_ffn_b, v_meta_tokens, v_ab_w_in, v_ab_conv_w, v_ab_a_log, v_ab_dt_bias, v_ab_gnorm_g, v_ab_w_out, v_c_w_in, v_c_lb_raw, v_c_gnorm_g, v_c_w_out, v_ln_mix_g, v_ln_mix_b, v_mlp_w1, v_mlp_w2, v_ln_ffn_g, v_ln_ffn_b):
    w = dict(zip(_WEIGHTS, (meta_tokens, ab_w_in, ab_conv_w, ab_a_log, ab_dt_bias, ab_gnorm_g, ab_w_out, c_w_in, c_lb_raw,
                            c_gnorm_g, c_w_out, ln_mix_g, ln_mix_b, mlp_w1, mlp_w2, ln_ffn_g, ln_ffn_b)))
    mom = dict(zip(_WEIGHTS, (m_meta_tokens, m_ab_w_in, m_ab_conv_w, m_ab_a_log, m_ab_dt_bias, m_ab_gnorm_g, m_ab_w_out,
                              m_c_w_in, m_c_lb_raw, m_c_gnorm_g, m_c_w_out, m_ln_mix_g, m_ln_mix_b, m_mlp_w1, m_mlp_w2,
                              m_ln_ffn_g, m_ln_ffn_b)))
    var = dict(zip(_WEIGHTS, (v_meta_tokens, v_ab_w_in, v_ab_conv_w, v_ab_a_log, v_ab_dt_bias, v_ab_gnorm_g, v_ab_w_out,
                              v_c_w_in, v_c_lb_raw, v_c_gnorm_g, v_c_w_out, v_ln_mix_g, v_ln_mix_b, v_mlp_w1, v_mlp_w2,
                              v_ln_ffn_g, v_ln_ffn_b)))
    me = 4 * lax.axis_index("x") + 2 * lax.axis_index("y") + lax.axis_index("c")
    seq = x.shape[1]
    pad = (-(N_META + seq)) % SB_BLOCK
    lp = pad + N_META + seq
    meta_w = D_MODEL // N_DEV
    conv_w_all = 2 * GDN_HEADS * HEAD_W + GDN_HEADS * HEAD_W
    conv_w_mine = conv_w_all // N_DEV

    g_meta, g_conv, g_ab_in = _gather([w["meta_tokens"], w["ab_conv_w"][0], w["ab_w_in"][0]], [F32, F32, BF16],
                                      name="gather_weights_first")
    meta_full = g_meta.transpose(1, 0, 2).reshape(N_META, D_MODEL)
    conv_full = g_conv.transpose(1, 0, 2).reshape(CONV_K, conv_w_all)
    ab_full = g_ab_in.transpose(1, 0, 2).reshape(D_MODEL, AB_IN)
    ba0 = AB_Z + 512
    w_ab = jnp.concatenate([ab_full[:, :ba0], ab_full[:, ba0 + 2 * GDN_HEADS:], ab_full[:, ba0:ba0 + 2 * GDN_HEADS],
                            jnp.zeros((D_MODEL, AB_CAT - AB_IN), BF16)], axis=1)
    vec128 = lambda p: jnp.zeros((1, HEAD_W), F32).at[0, :GDN_HEADS].set(p.reshape(GDN_HEADS))
    wts = dict(
        w_ab=w_ab, conv_w=conv_full, alog_v=vec128(w["ab_a_log"]), dtb_v=vec128(w["ab_dt_bias"]),
        ab_gn=w["ab_gnorm_g"][0], lb=_lower_bound(w["c_lb_raw"]), c_gn=w["c_gnorm_g"][0],
        ln_mix_g=w["ln_mix_g"], ln_mix_b=w["ln_mix_b"], ln_ffn_g=w["ln_ffn_g"], ln_ffn_b=w["ln_ffn_b"])

    def weights_a(gathered):
        g_ab_out, g_w1, g_w2 = gathered
        return dict(w_out0=g_ab_out.reshape(D_MODEL, D_MODEL), w1=[g_w1], w2=[g_w2.reshape(D_FF, D_MODEL)])

    def weights_b(gathered):
        g_c_in, g_c_out = gathered
        return dict(w_c=g_c_in, w_out1=g_c_out.reshape(D_MODEL, D_MODEL))

    def weights_c(gathered):
        g_w1, g_w2 = gathered
        return dict(w1=[g_w1], w2=[g_w2.reshape(D_FF, D_MODEL)])

    hooks = dict(
        gather_a=([w["ab_w_out"][0], w["mlp_w1"][0], w["mlp_w2"][0]], [BF16] * 3), weights_a=weights_a,
        gather_b=([w["c_w_in"][0], w["c_w_out"][0]], [BF16] * 2), weights_b=weights_b,
        gather_c=([w["mlp_w1"][1], w["mlp_w2"][1]], [BF16] * 2), weights_c=weights_c)

    h0 = jnp.concatenate([jnp.zeros((pad, D_MODEL), F32), meta_full, x[0]], axis=0)
    loss_vec, dh0, g = _local_step(h0, loss_target[0], pad, wts, hooks)
    loss = lax.psum(jnp.sum(loss_vec), ("x", "y", "c"))
    grad_x = dh0[lp - seq:][None]

    _, lb_vjp = jax.vjp(_lower_bound, w["c_lb_raw"])
    rep_part = _pack_replicated(dict(
        ln_mix_g=g["ln_mix_g"], ln_mix_b=g["ln_mix_b"], ln_ffn_g=g["ln_ffn_g"], ln_ffn_b=g["ln_ffn_b"],
        c_lb_raw=lb_vjp(g["lb"])[0], ab_gnorm_g=g["ab_gn"], c_gnorm_g=g["c_gn"],
        ab_a_log=g["alog_v"][0, :GDN_HEADS], ab_dt_bias=g["dtb_v"][0, :GDN_HEADS]))
    small = jnp.concatenate([rep_part, dh0[pad:pad + N_META], g["conv_w"].reshape(-1, D_MODEL),
                             jnp.zeros((_SMALL_N - _SMALL_CONV - CONV_K * conv_w_all // D_MODEL, D_MODEL), F32)], axis=0)
    (small_all,) = _gather([small], [F32], name="gather_small_grads")
    rep_out = _adamw(_pack_replicated(w), small_all[:, :_PACK_N], _pack_replicated(mom), _pack_replicated(var),
                     name="adamw_replicated")
    meta_parts = lax.dynamic_slice_in_dim(small_all[:, _SMALL_META:_SMALL_META + N_META], me * meta_w, meta_w, axis=2)
    meta_out = _adamw(w["meta_tokens"], meta_parts, mom["meta_tokens"], var["meta_tokens"], name="adamw_meta")
    conv_parts = small_all[:, _SMALL_CONV:_SMALL_CONV + CONV_K * conv_w_all // D_MODEL].reshape(N_DEV, CONV_K, conv_w_all)
    conv_parts = lax.dynamic_slice_in_dim(conv_parts, me * conv_w_mine, conv_w_mine, axis=2)
    conv_out = _adamw(w["ab_conv_w"][0], conv_parts, mom["ab_conv_w"][0], var["ab_conv_w"][0], name="adamw_conv")

    parts = g["parts"]
    big = [("ab_w_in", 0, parts["ab_w_in"]), ("ab_w_out", 0, parts["ab_w_out"]), ("mlp_w1", 0, parts["mlp_w1_0"]),
           ("mlp_w2", 0, parts["mlp_w2_0"]), ("c_w_in", 0, parts["c_w_in"]), ("c_w_out", 0, parts["c_w_out"]),
           ("mlp_w1", 1, parts["mlp_w1_1"]), ("mlp_w2", 1, parts["mlp_w2_1"])]
    big_out = {}
    for name, l, p in big:
        res = _adamw(w[name][l], p, mom[name][l], var[name][l], name=f"adamw_{name}{l}")
        big_out.setdefault(name, []).append(res)

    rep = [_unpack_replicated(r, w) for r in rep_out]
    outs = {}
    for name in _WEIGHTS:
        if name == "meta_tokens":
            outs[name] = list(meta_out)
        elif name == "ab_conv_w":
            outs[name] = [o[None] for o in conv_out]
        elif name in big_out:
            res = big_out[name]
            outs[name] = [o[None] for o in res[0]] if len(res) == 1 else [jnp.stack(pair) for pair in zip(*res)]
        else:
            outs[name] = [r[name] for r in rep]
    flat = [loss, grad_x]
    for kind in range(4):
        flat += [outs[name][kind] for name in _WEIGHTS]
    return tuple(flat)
```

```python
import functools

import jax
import jax.numpy as jnp
from jax import lax
from jax.experimental import pallas as pl
from jax.experimental.pallas import tpu as pltpu

F32 = jnp.float32
BF16 = jnp.bfloat16

N_DEV = 8
D_MODEL = 1024
N_META = 16
D_FF = 4096
DEPTH = 2
GDN_HEADS = 4
SB_HEADS = 8
SB_DH = 64
HG_HEADS = 8
HEAD_W = 128
CHUNK = 64
SB_BLOCK = 128
CONV_K = 4
DN_ALPHA = float((2 * DEPTH) ** 0.25)
LN_EPS = 1e-5
RMS_EPS = 1e-6
L2_EPS = 1e-6
ADAM_LR, ADAM_B1, ADAM_B2, ADAM_EPS, ADAM_WD, ADAM_STEP = 0.001, 0.9, 0.999, 1e-08, 0.01, 10

AB_Z = 1536
AB_SB = 2048
AB_BA = 3584
AB_CAT = 3840
AB_IN = 3592

VMEM_LIMIT = 56 * 1024 * 1024


def _cp(sem=None, **kw):
    if sem is not None:
        kw["dimension_semantics"] = sem
    return pltpu.CompilerParams(vmem_limit_bytes=VMEM_LIMIT, **kw)


def _row_tile(n, want):
    best = 8
    for t in range(8, min(n, want) + 1, 8):
        if n % t == 0:
            best = t
    return best


@jax.custom_vjp
def _sigmoid(x):
    e = jnp.exp(-jnp.abs(x))
    r = 1.0 / (1.0 + e)
    return jnp.where(x >= 0, r, e * r)


def _sigmoid_fwd(x):
    s = _sigmoid(x)
    return s, s


def _sigmoid_bwd(s, g):
    return (g * s * (1.0 - s),)


_sigmoid.defvjp(_sigmoid_fwd, _sigmoid_bwd)


def _log1p_exp_neg_abs(x):
    e = jnp.exp(-jnp.abs(x))
    return jnp.where(e < 1e-4, e - 0.5 * e * e, jnp.log(1.0 + e))


@jax.custom_vjp
def _softplus(x):
    return jnp.maximum(x, 0.0) + _log1p_exp_neg_abs(x)


def _softplus_fwd(x):
    return _softplus(x), x


def _softplus_bwd(x, g):
    return (g * _sigmoid(x),)


_softplus.defvjp(_softplus_fwd, _softplus_bwd)


def _silu(x):
    return x * _sigmoid(x)


def _silu_grad(x):
    s = _sigmoid(x)
    return s * (1.0 + x * (1.0 - s))


def _dot(a, b, dims, precision=None):
    return lax.dot_general(a, b, (dims, ((), ())), precision=precision, preferred_element_type=F32)


NN = ((1,), (0,))
NT = ((1,), (1,))
TN = ((0,), (0,))


def _bdot(a, b, dims):
    return _dot(a.astype(BF16), b.astype(BF16), dims)


def _layer_norm(pre, g, beta):
    mu = jnp.mean(pre, axis=-1, keepdims=True)
    xc = pre - mu
    var = jnp.mean(xc * xc, axis=-1, keepdims=True)
    return xc * lax.rsqrt(var + LN_EPS) * g + beta


def _layer_norm_bwd(pre, g, dy):
    mu = jnp.mean(pre, axis=-1, keepdims=True)
    xc = pre - mu
    rstd = lax.rsqrt(jnp.mean(xc * xc, axis=-1, keepdims=True) + LN_EPS)
    xhat = xc * rstd
    dxh = dy * g
    m1 = jnp.mean(dxh, axis=-1, keepdims=True)
    m2 = jnp.mean(dxh * xhat, axis=-1, keepdims=True)
    return (rstd * (dxh - m1 - xhat * m2), jnp.sum(dy * xhat, axis=0, keepdims=True),
            jnp.sum(dy, axis=0, keepdims=True))


def _mm(a, b, mode, *, tm, tn, tk, name, epi=None, c=None, scale=1.0, b_dev=False, out_dev=False, out_dtype=F32,
        ln=None, scatter=()):
    if mode == "NN":
        m, kk = a.shape
        n = b.shape[2] * N_DEV if b_dev else b.shape[1]
    elif mode == "NT":
        m, kk = a.shape
        n = b.shape[1] if b_dev else b.shape[0]
    else:
        kk, m = a.shape
        n = b.shape[1]
    assert m % tm == 0 and n % tn == 0 and kk % tk == 0, (name, m, n, kk, tm, tn, tk)
    nk = kk // tk
    dims = {"NN": NN, "NT": NT, "TN": TN}[mode]

    if mode == "TN":
        a_spec = pl.BlockSpec((tk, tm), lambda i, j, k: (k, i))
    else:
        a_spec = pl.BlockSpec((tm, tk), lambda i, j, k: (i, k))
    if mode == "NN":
        if b_dev:
            assert tn == b.shape[2]
            b_spec = pl.BlockSpec((None, tk, tn), lambda i, j, k: (j, k, 0))
        else:
            b_spec = pl.BlockSpec((tk, tn), lambda i, j, k: (k, j))
    elif mode == "NT":
        if b_dev:
            assert tk == b.shape[2]
            b_spec = pl.BlockSpec((None, tn, tk), lambda i, j, k: (k, j, 0))
        else:
            b_spec = pl.BlockSpec((tn, tk), lambda i, j, k: (j, k))
    else:
        b_spec = pl.BlockSpec((tk, tn), lambda i, j, k: (k, j))
    in_specs = [a_spec, b_spec]
    operands = [a, b]
    if c is not None:
        in_specs.append(pl.BlockSpec((tm, tn), lambda i, j, k: (i, j)))
        operands.append(c)
    if epi == "ln":
        assert tn == n and not out_dev
        in_specs += [pl.BlockSpec((1, n), lambda i, j, k: (0, 0))] * 2
        operands += [ln[0].reshape(1, n), ln[1].reshape(1, n)]
    elif epi == "ln_bwd":
        assert tn == n and not out_dev
        in_specs += [pl.BlockSpec((tm, tn), lambda i, j, k: (i, j)), pl.BlockSpec((1, n), lambda i, j, k: (0, 0))]
        operands += [ln[0], ln[1].reshape(1, n)]
    if out_dev:
        assert tn == n // N_DEV
        out_shape = jax.ShapeDtypeStruct((N_DEV, m, tn), out_dtype)
        out_spec = pl.BlockSpec((None, tm, tn), lambda i, j, k: (j, i, 0))
    else:
        out_shape = jax.ShapeDtypeStruct((m, n), out_dtype)
        out_spec = pl.BlockSpec((tm, tn), lambda i, j, k: (i, j))
    if epi == "ln":
        out_shape = [out_shape, out_shape, jax.ShapeDtypeStruct((m, n), BF16)]
        out_spec = [out_spec] * 3
    elif epi == "relu2_copy":
        assert not out_dev
        out_shape = [out_shape, jax.ShapeDtypeStruct((m, n), BF16)]
        out_spec = [out_spec] * 2
    elif epi == "ln_bwd":
        vec_shape, vec_spec = jax.ShapeDtypeStruct((1, n), F32), pl.BlockSpec((1, n), lambda i, j, k: (0, 0))
        out_shape = [out_shape, jax.ShapeDtypeStruct((m, n), BF16), vec_shape, vec_shape]
        out_spec = [out_spec, out_spec, vec_spec, vec_spec]
    n_out = {"ln": 3, "relu2_copy": 2, "ln_bwd": 4}.get(epi, 1)
    ns = len(scatter)
    if ns:
        in_specs += [_ANY] * ns
        operands += list(scatter)
        out_shape = (out_shape if n_out > 1 else [out_shape]) + [jax.ShapeDtypeStruct(s.shape, s.dtype) for s in scatter]
        out_spec = (out_spec if n_out > 1 else [out_spec]) + [_ANY] * ns
    n_in = len(operands)
    grid = (m // tm, n // tn, nk)

    def body(*refs):
        a_ref, b_ref = refs[0], refs[1]
        c_ref = refs[2] if c is not None else None
        o_ref = refs[n_in]
        scratch0 = n_in + n_out + ns
        acc_ref = refs[scratch0] if nk > 1 else None
        if ns:
            s_start, s_finish = _scatter_phases(refs[n_in - ns:n_in], refs[n_in + n_out:scratch0],
                                                *refs[scratch0 + (1 if nk > 1 else 0):])
            at = lambda step: functools.reduce(lambda x, y: x & y, [pl.program_id(ax) == step[ax] for ax in range(3)])
            pl.when(at((0, 0, 0)))(s_start)
        p = _dot(a_ref[...].astype(BF16), b_ref[...].astype(BF16), dims)
        first_rows = pl.program_id(0) == 0

        def finish(acc):
            if epi == "add":
                acc = acc + scale * c_ref[...]
            elif epi == "relu2grad":
                acc = acc * (2.0 * jnp.maximum(c_ref[...], 0.0))
            elif epi == "relu2_copy":
                refs[n_in + 1][...] = jnp.square(jnp.maximum(acc, 0.0)).astype(BF16)
            elif epi == "ln_bwd":
                acc, dg, db = _layer_norm_bwd(refs[3][...], refs[4][...], acc + scale * c_ref[...])
                dg_ref, db_ref = refs[n_in + 2], refs[n_in + 3]

                @pl.when(first_rows)
                def _():
                    dg_ref[...] = jnp.zeros_like(dg_ref)
                    db_ref[...] = jnp.zeros_like(db_ref)

                dg_ref[...] += dg
                db_ref[...] += db
                refs[n_in + 1][...] = acc.astype(BF16)
            elif epi == "ln":
                acc = acc + scale * c_ref[...]
                y = _layer_norm(acc, refs[3][...], refs[4][...])
                refs[n_in + 1][...] = y
                refs[n_in + 2][...] = y.astype(BF16)
            o_ref[...] = acc.astype(out_dtype)

        if nk == 1:
            finish(p)
        else:
            k = pl.program_id(2)

            @pl.when(k == 0)
            def _():
                acc_ref[...] = p

            @pl.when(k > 0)
            def _():
                acc_ref[...] += p

            @pl.when(k == nk - 1)
            def _():
                finish(acc_ref[...])

        if ns:
            pl.when(at(tuple(g - 1 for g in grid)))(s_finish)

    res = pl.pallas_call(
        body, name=name, grid=grid, in_specs=in_specs, out_specs=out_spec, out_shape=out_shape,
        scratch_shapes=([pltpu.VMEM((tm, tn), F32)] if nk > 1 else []) + (_scatter_scratch(ns) if ns else []),
        compiler_params=_cp(("arbitrary",) * 3 if ns or epi == "ln_bwd" else ("parallel", "parallel", "arbitrary"),
                            has_side_effects=bool(ns)),
    )(*operands)
    return res


def _ln_bwd(pre, g, dy, *, name):
    lp, d = pre.shape
    tm = _row_tile(lp, 512)

    def body(pre_ref, g_ref, dy_ref, dpre_ref, dpreb_ref, dg_ref, db_ref):
        dpre, dg, db = _layer_norm_bwd(pre_ref[...], g_ref[...], dy_ref[...])
        dpre_ref[...] = dpre
        dpreb_ref[...] = dpre.astype(BF16)

        @pl.when(pl.program_id(0) == 0)
        def _():
            dg_ref[...] = jnp.zeros_like(dg_ref)
            db_ref[...] = jnp.zeros_like(db_ref)

        dg_ref[...] += dg
        db_ref[...] += db

    row = pl.BlockSpec((tm, d), lambda i: (i, 0))
    vec = pl.BlockSpec((1, d), lambda i: (0, 0))
    return pl.pallas_call(
        body, name=name, grid=(lp // tm,), in_specs=[row, vec, row], out_specs=[row, row, vec, vec],
        out_shape=[jax.ShapeDtypeStruct((lp, d), F32), jax.ShapeDtypeStruct((lp, d), BF16),
                   jax.ShapeDtypeStruct((1, d), F32), jax.ShapeDtypeStruct((1, d), F32)],
        compiler_params=_cp(("arbitrary",)),
    )(pre, g.reshape(1, d), dy)


def _loss_head(y, target, *, name):
    lp, d = y.shape
    seq = target.shape[0]
    tm = SB_BLOCK
    first = (lp - seq) // tm
    assert (lp - seq) % tm == 0 and seq % tm == 0

    def body(y_ref, t_ref, dy_ref, loss_ref):
        i = pl.program_id(0)
        live = i >= first
        diff = jnp.where(live, y_ref[...] - t_ref[...], 0.0)
        dy_ref[...] = diff * (1.0 / d)

        @pl.when(i == 0)
        def _():
            loss_ref[...] = jnp.zeros_like(loss_ref)

        loss_ref[...] += jnp.sum(diff * diff, axis=0, keepdims=True) * (0.5 / d)

    return pl.pallas_call(
        body, name=name, grid=(lp // tm,),
        in_specs=[pl.BlockSpec((tm, d), lambda i: (i, 0)),
                  pl.BlockSpec((tm, d), lambda i: (jnp.maximum(i - first, 0), 0))],
        out_specs=[pl.BlockSpec((tm, d), lambda i: (i, 0)), pl.BlockSpec((1, d), lambda i: (0, 0))],
        out_shape=[jax.ShapeDtypeStruct((lp, d), F32), jax.ShapeDtypeStruct((1, d), F32)],
        compiler_params=_cp(("arbitrary",)),
    )(y, target)


def _gate_fwd(o, zsrc, z_blk0, g, other, *, heads, name):
    lp = o.shape[0]
    tm = _row_tile(lp, 512)
    w = heads * HEAD_W
    assert (z_blk0 * HEAD_W) % w == 0
    has_other = w < D_MODEL

    def body(o_ref, z_ref, g_ref, *rest):
        y_ref = rest[-1]
        gv = g_ref[...]
        for h in range(heads):
            cs = slice(h * HEAD_W, (h + 1) * HEAD_W)
            ov = o_ref[:, cs]
            r = lax.rsqrt(jnp.mean(ov * ov, axis=-1, keepdims=True) + RMS_EPS)
            y_ref[:, cs] = (ov * r * gv * _silu(z_ref[:, cs])).astype(BF16)
        if has_other:
            y_ref[:, w:] = rest[0][...].astype(BF16)

    row = lambda width, blk: pl.BlockSpec((tm, width), lambda i: (i, blk))
    return pl.pallas_call(
        body, name=name, grid=(lp // tm,),
        in_specs=[row(w, 0), row(w, z_blk0 * HEAD_W // w), pl.BlockSpec((1, HEAD_W), lambda i: (0, 0))]
        + ([row(D_MODEL - w, 0)] if has_other else []),
        out_specs=row(D_MODEL, 0), out_shape=jax.ShapeDtypeStruct((lp, D_MODEL), BF16),
        compiler_params=_cp(("parallel",)),
    )(o, zsrc, g.reshape(1, HEAD_W), *([other] if has_other else []))


def _gate_bwd(o, zsrc, z_blk0, g, dy, *, heads, name):
    lp = o.shape[0]
    tm = _row_tile(lp, 512)

    w = heads * HEAD_W
    assert (z_blk0 * HEAD_W) % w == 0

    def body(o_ref, z_ref, g_ref, dy_ref, do_ref, dz_ref, dg_ref):
        @pl.when(pl.program_id(0) == 0)
        def _():
            dg_ref[...] = jnp.zeros_like(dg_ref)

        gv = g_ref[...]
        dg = jnp.zeros((1, HEAD_W), F32)
        for h in range(heads):
            cs = slice(h * HEAD_W, (h + 1) * HEAD_W)
            ov, zv, dyv = o_ref[:, cs], z_ref[:, cs], dy_ref[:, cs]
            r = lax.rsqrt(jnp.mean(ov * ov, axis=-1, keepdims=True) + RMS_EPS)
            nrm = ov * r
            s = _silu(zv)
            dn = dyv * gv * s
            do_ref[:, cs] = r * (dn - nrm * jnp.mean(dn * nrm, axis=-1, keepdims=True))
            dz_ref[:, cs] = dyv * nrm * gv * _silu_grad(zv)
            dg = dg + jnp.sum(dyv * nrm * s, axis=0, keepdims=True)
        dg_ref[...] += dg

    row = lambda blk: pl.BlockSpec((tm, w), lambda i: (i, blk))
    vec = pl.BlockSpec((1, HEAD_W), lambda i: (0, 0))
    return pl.pallas_call(
        body, name=name, grid=(lp // tm,),
        in_specs=[row(0), row(z_blk0 * HEAD_W // w), vec, row(0)], out_specs=[row(0), row(0), vec],
        out_shape=[jax.ShapeDtypeStruct((lp, w), F32), jax.ShapeDtypeStruct((lp, w), F32),
                   jax.ShapeDtypeStruct((1, HEAD_W), F32)],
        compiler_params=_cp(("arbitrary",)),
    )(o, zsrc, g.reshape(1, HEAD_W), dy)


def _conv_taps(x, w):
    acc = w[CONV_K - 1:CONV_K, :] * x
    for k in range(CONV_K - 1):
        acc = acc + w[k:k + 1, :] * pltpu.roll(x, CONV_K - 1 - k, 0)
    return acc


def _gdn_pre_fwd(p0, conv_w, pad, *, name):
    lp = p0.shape[0]
    nq = GDN_HEADS
    qscale = HEAD_W ** -0.5

    def body(x_ref, w_ref, y_ref):
        j = pl.program_id(0)
        c = _conv_taps(x_ref[...], w_ref[...])
        s = _silu(c)
        r = lax.rsqrt(jnp.sum(s * s, axis=-1, keepdims=True) + L2_EPS)
        mult = jnp.where(j < nq, r * qscale, jnp.where(j < 2 * nq, r, 1.0))
        rows = lax.broadcasted_iota(jnp.int32, (lp, 1), 0)
        y_ref[...] = jnp.where(rows >= pad, s * mult, 0.0)

    return pl.pallas_call(
        body, name=name, grid=(3 * nq,),
        in_specs=[pl.BlockSpec((lp, HEAD_W), lambda j: (0, j)), pl.BlockSpec((CONV_K, HEAD_W), lambda j: (0, j))],
        out_specs=pl.BlockSpec((lp, HEAD_W), lambda j: (0, j)),
        out_shape=jax.ShapeDtypeStruct((lp, 3 * nq * HEAD_W), F32), compiler_params=_cp(("parallel",)),
    )(p0, conv_w)


def _gdn_pre_bwd(p0, conv_w, dqkv, pad, *, name):
    lp = p0.shape[0]
    nq = GDN_HEADS
    qscale = HEAD_W ** -0.5

    def body(x_ref, w_ref, dy_ref, dx_ref, dw_ref):
        j = pl.program_id(0)
        x, w = x_ref[...], w_ref[...]
        c = _conv_taps(x, w)
        s = _silu(c)
        r = lax.rsqrt(jnp.sum(s * s, axis=-1, keepdims=True) + L2_EPS)
        rows = lax.broadcasted_iota(jnp.int32, (lp, 1), 0)
        dy = jnp.where(rows >= pad, dy_ref[...], 0.0)
        nrm = s * r
        dn = dy * jnp.where(j < nq, qscale, 1.0)
        ds_norm = r * (dn - nrm * jnp.sum(nrm * dn, axis=-1, keepdims=True))
        ds = jnp.where(j < 2 * nq, ds_norm, dy)
        dc = ds * _silu_grad(c)
        dx = w[CONV_K - 1:CONV_K, :] * dc
        dws = [None] * CONV_K
        dws[CONV_K - 1] = jnp.sum(dc * x, axis=0, keepdims=True)
        for k in range(CONV_K - 1):
            sh = CONV_K - 1 - k
            dx = dx + w[k:k + 1, :] * pltpu.roll(dc, lp - sh, 0)
            dws[k] = jnp.sum(dc * pltpu.roll(x, sh, 0), axis=0, keepdims=True)
        dx_ref[...] = dx
        dw_ref[...] = jnp.concatenate(dws, axis=0)

    blk = pl.BlockSpec((lp, HEAD_W), lambda j: (0, j))
    wblk = pl.BlockSpec((CONV_K, HEAD_W), lambda j: (0, j))
    return pl.pallas_call(
        body, name=name, grid=(3 * nq,), in_specs=[blk, wblk, blk], out_specs=[blk, wblk],
        out_shape=[jax.ShapeDtypeStruct((lp, 3 * nq * HEAD_W), F32),
                   jax.ShapeDtypeStruct((CONV_K, 3 * nq * HEAD_W), F32)],
        compiler_params=_cp(("parallel",)),
    )(p0, conv_w, dqkv)


@jax.custom_vjp
def _inv_unit_lower(m):
    c = m.shape[0]
    eye = (lax.broadcasted_iota(jnp.int32, (c, c), 0) == lax.broadcasted_iota(jnp.int32, (c, c), 1)).astype(F32)
    x = eye - m
    p = m
    n = 2
    while n < CHUNK:
        p = _bdot(p, p, NN)
        x = x + _bdot(x, p, NN)
        n *= 2
    return x


def _inv_fwd(m):
    t = _inv_unit_lower(m)
    return t, t


def _inv_bwd(t, g):
    return (-_bdot(_bdot(t, g, TN), t, NT),)


_inv_unit_lower.defvjp(_inv_fwd, _inv_bwd)


GDN_STEP = 2


def _heads_to_rows(x, nh):
    return jnp.concatenate([x[:, h * HEAD_W:(h + 1) * HEAD_W] for h in range(nh)], axis=0)


def _rows_to_heads(x, nh):
    c = x.shape[0] // nh
    return jnp.concatenate([x[h * c:(h + 1) * c] for h in range(nh)], axis=1)


def _gdn_chunk(q, k, v, ba, alog, dtb, states, valid):
    nh = GDN_HEADS
    c = q.shape[0]
    r = nh * c
    lane = lax.broadcasted_iota(jnp.int32, (1, HEAD_W), 1)
    pick = lambda x, l: jnp.sum(jnp.where(lane == l, x, 0.0), axis=-1, keepdims=True)
    beta = jnp.concatenate([jnp.where(valid, _sigmoid(pick(ba, h)), 0.0) for h in range(nh)], axis=0)
    g = jnp.concatenate(
        [jnp.where(valid, -jnp.exp(pick(alog, h)) * _softplus(pick(ba, nh + h) + pick(dtb, h)), 0.0) for h in range(nh)],
        axis=0)
    qs, ks, vs = _heads_to_rows(q, nh), _heads_to_rows(k, nh), _heads_to_rows(v, nh)
    rr = lax.broadcasted_iota(jnp.int32, (r, r), 0)
    cc = lax.broadcasted_iota(jnp.int32, (r, r), 1)
    same = (rr // c) == (cc // c)
    causal, strict = same & (cc <= rr), same & (cc < rr)
    lower = jnp.where(causal, 1.0, 0.0).astype(BF16)
    upper = jnp.where(same & (cc >= rr), 1.0, 0.0).astype(BF16)
    gcb = _mask_mm(lower, upper, g * jnp.ones((1, HEAD_W), F32))
    gc_col = jnp.concatenate([gcb] * (r // HEAD_W), axis=1)
    decay = jnp.where(causal, jnp.exp(jnp.minimum(gc_col - gc_col.T, 0.0)), 0.0)
    egc = jnp.exp(gcb)
    kb = ks * beta
    m = jnp.where(strict, _bdot(kb, ks, NT) * decay, 0.0)
    t = _inv_unit_lower(m)
    u = _bdot(t, vs * beta, NN)
    w = _bdot(t, kb * egc, NN)
    a = _bdot(qs, ks, NT) * decay
    rows = lambda x, h: x[h * c:(h + 1) * c]
    qe = qs * egc
    v_new = u - jnp.concatenate([_bdot(rows(w, h), states[h], NN) for h in range(nh)], axis=0)
    o = jnp.concatenate([_bdot(rows(qe, h), states[h], NN) for h in range(nh)], axis=0) + _bdot(a, v_new, NN)
    new_states = []
    for h in range(nh):
        gl = gcb[(h + 1) * c - 1:(h + 1) * c, :]
        k_dec = rows(ks, h) * jnp.exp(gl - rows(gcb, h))
        new_states.append(states[h] * jnp.exp(gl) + _bdot(k_dec, rows(v_new, h), TN))
    return _rows_to_heads(o, nh), new_states


def _gdn_fwd(qkv, p0, alog_v, dtb_v, pad, *, name, gather=None):
    lp = qkv.shape[0]
    n = lp // CHUNK
    nh = GDN_HEADS
    assert n % GDN_STEP == 0
    steps, rows = n // GDN_STEP, GDN_STEP * CHUNK
    g_srcs, g_dtypes = gather if gather is not None else ([], [])
    ng_arr = len(g_srcs)

    def body(q_ref, k_ref, v_ref, ba_ref, al_ref, dt_ref, *rest):
        g_ins, (o_ref, st_ref) = rest[:ng_arr], rest[ng_arr:ng_arr + 2]
        g_outs, s_ref, g_scratch = rest[ng_arr + 2:2 * ng_arr + 2], rest[2 * ng_arr + 2], rest[2 * ng_arr + 3:]
        i = pl.program_id(0)
        if ng_arr:
            g_start, g_forward, g_finish = _gather_phases(g_ins, g_outs, g_scratch[:ng_arr], *g_scratch[ng_arr:],
                                                          g_dtypes)
            pl.when(i == 0)(g_start)
            pl.when(i == (3 * steps) // 4)(g_forward)

        @pl.when(i == 0)
        def _():
            s_ref[...] = jnp.zeros_like(s_ref)

        s = s_ref[...]
        s = [s[h] for h in range(nh)]
        q, k, v, ba, al, dt = q_ref[...], k_ref[...], v_ref[...], ba_ref[...], al_ref[...], dt_ref[...]
        outs = []
        for c in range(GDN_STEP):
            sl = slice(c * CHUNK, (c + 1) * CHUNK)
            valid = (i * rows + c * CHUNK + lax.broadcasted_iota(jnp.int32, (CHUNK, 1), 0)) >= pad
            for h in range(nh):
                st_ref[c, h] = s[h]
            o, s = _gdn_chunk(q[sl], k[sl], v[sl], ba[sl], al, dt, s, valid)
            outs.append(o)
        o_ref[...] = jnp.concatenate(outs, axis=0)
        for h in range(nh):
            s_ref[h] = s[h]
        if ng_arr:
            pl.when(i == steps - 1)(g_finish)

    w = nh * HEAD_W
    vec = pl.BlockSpec((1, HEAD_W), lambda i: (0, 0))
    return pl.pallas_call(
        body, name=name, grid=(steps,),
        in_specs=[pl.BlockSpec((rows, w), lambda i: (i, 0)), pl.BlockSpec((rows, w), lambda i: (i, 1)),
                  pl.BlockSpec((rows, w), lambda i: (i, 2)), pl.BlockSpec((rows, HEAD_W), lambda i: (i, AB_BA // HEAD_W)),
                  vec, vec] + [pl.BlockSpec(memory_space=pltpu.VMEM)] * ng_arr,
        out_specs=[pl.BlockSpec((rows, w), lambda i: (i, 0)),
                   pl.BlockSpec((GDN_STEP, nh, HEAD_W, HEAD_W), lambda i: (i, 0, 0, 0))] + [_ANY] * ng_arr,
        out_shape=[jax.ShapeDtypeStruct((lp, w), F32), jax.ShapeDtypeStruct((n, nh, HEAD_W, HEAD_W), F32)]
        + _gather_out_shapes(g_srcs, g_dtypes),
        scratch_shapes=[pltpu.VMEM((nh, HEAD_W, HEAD_W), F32)] + (_gather_scratch(g_srcs, g_dtypes) if ng_arr else []),
        compiler_params=_cp(("arbitrary",), has_side_effects=bool(ng_arr)),
    )(qkv, qkv, qkv, p0, alog_v, dtb_v, *g_srcs)


def _gdn_bwd(qkv, p0, alog_v, dtb_v, states, do, pad, *, name, scatter=()):
    lp = qkv.shape[0]
    n = lp // CHUNK
    nh = GDN_HEADS
    assert n % GDN_STEP == 0
    steps, rows = n // GDN_STEP, GDN_STEP * CHUNK
    ns = len(scatter)

    def body(q_ref, k_ref, v_ref, ba_ref, al_ref, dt_ref, st_ref, do_ref, *rest):
        s_ins, (dq_ref, dk_ref, dv_ref, dba_ref, dal_ref, ddt_ref) = rest[:ns], rest[ns:ns + 6]
        s_outs, ds_ref, s_sems = rest[ns + 6:2 * ns + 6], rest[2 * ns + 6], rest[2 * ns + 7:]
        step = pl.program_id(0)
        i = steps - 1 - step
        if ns:
            s_start, s_finish = _scatter_phases(s_ins, s_outs, *s_sems)
            pl.when(step == 0)(s_start)

        @pl.when(step == 0)
        def _():
            ds_ref[...] = jnp.zeros_like(ds_ref)
            dal_ref[...] = jnp.zeros_like(dal_ref)
            ddt_ref[...] = jnp.zeros_like(ddt_ref)

        q, k, v, ba, al, dt = q_ref[...], k_ref[...], v_ref[...], ba_ref[...], al_ref[...], dt_ref[...]
        st, do, dst = st_ref[...], do_ref[...], ds_ref[...]
        vjps = []
        for c in range(GDN_STEP):
            sl = slice(c * CHUNK, (c + 1) * CHUNK)
            valid = (i * rows + c * CHUNK + lax.broadcasted_iota(jnp.int32, (CHUNK, 1), 0)) >= pad
            fn = functools.partial(_gdn_chunk, valid=valid)
            vjps.append(jax.vjp(fn, q[sl], k[sl], v[sl], ba[sl], al, dt, [st[c, h] for h in range(nh)])[1])
        ds = [dst[h] for h in range(nh)]
        grads = [None] * GDN_STEP
        for c in reversed(range(GDN_STEP)):
            grads[c] = vjps[c]((do[c * CHUNK:(c + 1) * CHUNK], ds))
            ds = grads[c][6]
        for j, ref in enumerate((dq_ref, dk_ref, dv_ref, dba_ref)):
            ref[...] = jnp.concatenate([gr[j] for gr in grads], axis=0)
        dal_ref[...] += sum(gr[4] for gr in grads)
        ddt_ref[...] += sum(gr[5] for gr in grads)
        for h in range(nh):
            ds_ref[h] = ds[h]
        if ns:
            pl.when(step == steps - 1)(s_finish)

    w = nh * HEAD_W
    rev = lambda c: (lambda s: (steps - 1 - s, c))
    vec = pl.BlockSpec((1, HEAD_W), lambda s: (0, 0))
    return pl.pallas_call(
        body, name=name, grid=(steps,),
        in_specs=[pl.BlockSpec((rows, w), rev(0)), pl.BlockSpec((rows, w), rev(1)), pl.BlockSpec((rows, w), rev(2)),
                  pl.BlockSpec((rows, HEAD_W), rev(AB_BA // HEAD_W)), vec, vec,
                  pl.BlockSpec((GDN_STEP, nh, HEAD_W, HEAD_W), lambda s: (steps - 1 - s, 0, 0, 0)),
                  pl.BlockSpec((rows, w), rev(0))] + [_ANY] * ns,
        out_specs=[pl.BlockSpec((rows, w), rev(0)), pl.BlockSpec((rows, w), rev(0)), pl.BlockSpec((rows, w), rev(0)),
                   pl.BlockSpec((rows, HEAD_W), rev(0)), vec, vec] + [_ANY] * ns,
        out_shape=[jax.ShapeDtypeStruct((lp, w), F32)] * 3 + [jax.ShapeDtypeStruct((lp, HEAD_W), F32)]
        + [jax.ShapeDtypeStruct((1, HEAD_W), F32)] * 2 + [jax.ShapeDtypeStruct(s.shape, s.dtype) for s in scatter],
        scratch_shapes=[pltpu.VMEM((nh, HEAD_W, HEAD_W), F32)] + (_scatter_scratch(ns) if ns else []),
        compiler_params=_cp(("arbitrary",), has_side_effects=bool(ns)),
    )(qkv, qkv, qkv, p0, alog_v, dtb_v, states, do, *scatter)


HG_LEVELS = (32, 16, 8, 4, 2, 1)
HG_GROUP = 4


def _hg_masks():
    import numpy as np
    c = CHUNK
    t = np.arange(c)[:, None]
    j = np.arange(c)[None, :]
    sums = (j <= t).astype(np.float32)
    pairs = [j == t]
    for m in HG_LEVELS:
        p = (t // (2 * m)) * (2 * m)
        r = p + m
        pairs.append((t >= r) & (j < r) & (j >= p))
    pairs = np.concatenate([np.kron(np.eye(HG_GROUP), p) for p in pairs], axis=0).astype(np.float32)
    return jnp.asarray(sums, BF16), jnp.asarray(sums.T, BF16), jnp.asarray(pairs, F32)


def _hg_level_row(b, m):
    c, w = b.shape
    if m >= 8:
        return jnp.concatenate([jnp.broadcast_to(b[p + m:p + m + 1], (2 * m, w)) for p in range(0, c, 2 * m)], axis=0)
    tiles = b.reshape(c // 8, 8, w)
    sub = lax.broadcasted_iota(jnp.int32, (1, 8, 1), 1)
    out = None
    for r0 in range(m, 8, 2 * m):
        cand = jnp.broadcast_to(tiles[:, r0:r0 + 1, :], tiles.shape)
        out = cand if out is None else jnp.where(sub >= r0 - m, cand, out)
    return out.reshape(c, w)


def _split3(x):
    hi = x.astype(BF16)
    r1 = x - hi.astype(F32)
    mid = r1.astype(BF16)
    return hi, mid, (r1 - mid.astype(F32)).astype(BF16)


def _mask_mm_raw(m, x):
    return sum(_dot(m, part, NN) for part in _split3(x))


@jax.custom_vjp
def _mask_mm(m, mt, x):
    return _mask_mm_raw(m, x)


def _mask_mm_fwd(m, mt, x):
    return _mask_mm_raw(m, x), (m, mt)


def _mask_mm_bwd(res, g):
    m, mt = res
    return jnp.zeros_like(m), jnp.zeros_like(mt), _mask_mm_raw(mt, g)


_mask_mm.defvjp(_mask_mm_fwd, _mask_mm_bwd)


def _hg_chunk(qr, fr, ir, lb, states, valid, sums, sums_t, pairs):
    nh = HG_GROUP
    c = qr.shape[0]
    r = nh * c
    fg = lb + (1.0 - lb) * _sigmoid(fr)
    logf = jnp.where(valid, jnp.log(fg), 0.0)
    k = jnp.where(valid, 1.0 - fg, 0.0)
    qs = jnp.where(valid, _silu(qr), 0.0)
    v = jnp.where(valid, ir, 0.0)
    b = _mask_mm(sums, sums_t, logf)
    mask = lambda n: pairs[n * r:(n + 1) * r]
    stack = lambda x: _heads_to_rows(x, nh)
    a = mask(0) * _bdot(stack(qs), stack(k), NT)
    for lvl, m in enumerate(HG_LEVELS):
        d = b - _hg_level_row(b, m)
        a = a + mask(1 + lvl) * _bdot(stack(qs * jnp.exp(jnp.minimum(d, 0.0))),
                                      stack(k * jnp.exp(jnp.minimum(-d, 0.0))), NT)
    av = _bdot(a, stack(v), NN)
    eb = jnp.exp(b)
    qe, kd = qs * eb, k * jnp.exp(b[c - 1:c] - b)
    outs, new_states = [], []
    for h in range(nh):
        cs = slice(h * HEAD_W, (h + 1) * HEAD_W)
        outs.append(_bdot(qe[:, cs], states[h], NT) + av[h * c:(h + 1) * c])
        new_states.append(states[h] * eb[c - 1:c, cs] + _bdot(v[:, cs], kd[:, cs], TN))
    return jnp.concatenate(outs, axis=1), new_states


def _hg_fwd(p1, lb, pad, *, name, gather=None):
    lp = p1.shape[0]
    n = lp // CHUNK
    nh = HG_HEADS
    g_srcs, g_dtypes = gather if gather is not None else ([], [])
    ng_arr = len(g_srcs)

    def body(q_ref, f_ref, i_ref, lb_ref, sums_ref, sums_t_ref, pairs_ref, *rest):
        g_ins, (o_ref, st_ref) = rest[:ng_arr], rest[ng_arr:ng_arr + 2]
        g_outs, s_ref, g_scratch = rest[ng_arr + 2:2 * ng_arr + 2], rest[2 * ng_arr + 2], rest[2 * ng_arr + 3:]
        i = pl.program_id(1)
        if ng_arr:
            g_start, g_forward, g_finish = _gather_phases(g_ins, g_outs, g_scratch[:ng_arr], *g_scratch[ng_arr:],
                                                          g_dtypes)
            last_group = pl.program_id(0) == ngrp - 1
            pl.when((pl.program_id(0) == 0) & (i == 0))(g_start)
            pl.when(last_group & (i == 0))(g_forward)

        @pl.when(i == 0)
        def _():
            s_ref[...] = jnp.zeros_like(s_ref)

        valid = (i * CHUNK + lax.broadcasted_iota(jnp.int32, (CHUNK, 1), 0)) >= pad
        s = s_ref[...]
        o, s2 = _hg_chunk(q_ref[...], f_ref[...], i_ref[...], lb_ref[...], [s[h] for h in range(grp)], valid,
                          sums_ref[...], sums_t_ref[...], pairs_ref[...])
        st_ref[...] = s
        o_ref[...] = o
        for h in range(grp):
            s_ref[h] = s2[h]
        if ng_arr:
            pl.when(last_group & (i == n - 1))(g_finish)

    masks = _hg_masks()
    grp, ngrp, gw = HG_GROUP, nh // HG_GROUP, HG_GROUP * HEAD_W
    blk = lambda off: pl.BlockSpec((CHUNK, gw), lambda h, i: (i, off + h))
    const = lambda a: pl.BlockSpec(a.shape, lambda h, i: (0, 0))
    return pl.pallas_call(
        body, name=name, grid=(ngrp, n),
        in_specs=[blk(0), blk(ngrp), blk(2 * ngrp), pl.BlockSpec((1, gw), lambda h, i: (0, h))]
        + [const(a) for a in masks] + [pl.BlockSpec(memory_space=pltpu.VMEM)] * ng_arr,
        out_specs=[blk(0), pl.BlockSpec((grp, None, HEAD_W, HEAD_W), lambda h, i: (h, i, 0, 0))] + [_ANY] * ng_arr,
        out_shape=[jax.ShapeDtypeStruct((lp, nh * HEAD_W), F32), jax.ShapeDtypeStruct((nh, n, HEAD_W, HEAD_W), F32)]
        + _gather_out_shapes(g_srcs, g_dtypes),
        scratch_shapes=[pltpu.VMEM((grp, HEAD_W, HEAD_W), F32)] + (_gather_scratch(g_srcs, g_dtypes) if ng_arr else []),
        compiler_params=_cp(("arbitrary", "arbitrary"), has_side_effects=bool(ng_arr)),
    )(p1, p1, p1, lb, *masks, *g_srcs)


def _hg_bwd(p1, lb, states, do, pad, *, name, scatter=()):
    lp = p1.shape[0]
    n = lp // CHUNK
    nh = HG_HEADS
    ns = len(scatter)

    def body(q_ref, f_ref, i_ref, lb_ref, st_ref, do_ref, sums_ref, sums_t_ref, pairs_ref, *rest):
        s_ins, (dq_ref, df_ref, di_ref, dlb_ref) = rest[:ns], rest[ns:ns + 4]
        s_outs, ds_ref, s_sems = rest[ns + 4:2 * ns + 4], rest[2 * ns + 4], rest[2 * ns + 5:]
        step = pl.program_id(1)
        i = n - 1 - step
        if ns:
            s_start, s_finish = _scatter_phases(s_ins, s_outs, *s_sems)
            pl.when((pl.program_id(0) == 0) & (step == 0))(s_start)

        @pl.when(step == 0)
        def _():
            ds_ref[...] = jnp.zeros_like(ds_ref)
            dlb_ref[...] = jnp.zeros_like(dlb_ref)

        valid = (i * CHUNK + lax.broadcasted_iota(jnp.int32, (CHUNK, 1), 0)) >= pad
        fn = functools.partial(_hg_chunk, valid=valid, sums=sums_ref[...], sums_t=sums_t_ref[...],
                               pairs=pairs_ref[...])
        st, dst = st_ref[...], ds_ref[...]
        _, vjp = jax.vjp(fn, q_ref[...], f_ref[...], i_ref[...], lb_ref[...], [st[h] for h in range(grp)])
        dq, df, di, dlb, ds = vjp((do_ref[...], [dst[h] for h in range(grp)]))
        dq_ref[...] = dq
        df_ref[...] = df
        di_ref[...] = di
        dlb_ref[...] += dlb
        for h in range(grp):
            ds_ref[h] = ds[h]
        if ns:
            pl.when((pl.program_id(0) == ngrp - 1) & (step == n - 1))(s_finish)

    masks = _hg_masks()
    grp, ngrp, gw = HG_GROUP, nh // HG_GROUP, HG_GROUP * HEAD_W
    blk = lambda off: pl.BlockSpec((CHUNK, gw), lambda h, s: (n - 1 - s, off + h))
    const = lambda a: pl.BlockSpec(a.shape, lambda h, s: (0, 0))
    w = nh * HEAD_W
    return pl.pallas_call(
        body, name=name, grid=(ngrp, n),
        in_specs=[blk(0), blk(ngrp), blk(2 * ngrp), pl.BlockSpec((1, gw), lambda h, s: (0, h)),
                  pl.BlockSpec((grp, None, HEAD_W, HEAD_W), lambda h, s: (h, n - 1 - s, 0, 0)), blk(0)]
        + [const(a) for a in masks] + [_ANY] * ns,
        out_specs=[blk(0), blk(0), blk(0), pl.BlockSpec((1, gw), lambda h, s: (0, h))] + [_ANY] * ns,
        out_shape=[jax.ShapeDtypeStruct((lp, w), F32)] * 3 + [jax.ShapeDtypeStruct((1, w), F32)]
        + [jax.ShapeDtypeStruct(s.shape, s.dtype) for s in scatter],
        scratch_shapes=[pltpu.VMEM((grp, HEAD_W, HEAD_W), F32)] + (_scatter_scratch(ns) if ns else []),
        compiler_params=_cp(("arbitrary", "arbitrary"), has_side_effects=bool(ns)),
    )(p1, p1, p1, lb, states, do, *masks, *scatter)


SB_GROUP = 4
SB_FAR = -110.0


def _sb_cat(kind, first_key=0):
    r = lax.broadcasted_iota(jnp.int32, (SB_BLOCK, 2 * SB_BLOCK), 0)
    c = lax.broadcasted_iota(jnp.int32, (SB_BLOCK, 2 * SB_BLOCK), 1)
    tri = {"after": c < r, "incl": r <= c, "before": r < c}[kind]
    m = ((c >= SB_BLOCK) | tri) & (r >= first_key)
    return jnp.where(m, 1.0, 0.0).astype(BF16)


def _sb_cumsum(x, cat):
    return _dot(x.astype(BF16), cat, NN)


def _sb_logsig(z):
    e = jnp.exp(-jnp.abs(z))
    lse = jnp.where(e < 1e-4, e, jnp.log(1.0 + e))
    lsz = jnp.minimum(z, 0.0) - lse
    return lsz, lsz - z, e


def _sb_stack(x, scale=None):
    lane = lax.broadcasted_iota(jnp.int32, (1, HEAD_W), 1)
    if scale is not None:
        x = x * scale
    return jnp.concatenate([jnp.where(lane < SB_DH, x, 0.0), jnp.where(lane >= SB_DH, x, 0.0)], axis=0).astype(BF16)


def _sb_unstack(x):
    lane = lax.broadcasted_iota(jnp.int32, (1, HEAD_W), 1)
    return jnp.where(lane < SB_DH, x[:SB_BLOCK], x[SB_BLOCK:])


def _sb_fwd(p0, pad, *, name, gather=None):
    lp = p0.shape[0]
    nb = lp // SB_BLOCK
    npair = SB_HEADS // 2
    blk0 = AB_SB // HEAD_W
    scale = SB_DH ** -0.5
    gw = SB_GROUP * SB_BLOCK
    assert pad < SB_BLOCK
    g_srcs, g_dtypes = gather if gather is not None else ([], [])
    ng_arr = len(g_srcs)

    def body(q_ref, k_ref, v_ref, *rest):
        g_ins, (o_ref, tot_ref, nproc_ref) = rest[:ng_arr], rest[ng_arr:ng_arr + 3]
        g_outs, g_scratch = rest[ng_arr + 3:2 * ng_arr + 3], rest[2 * ng_arr + 3:]
        first_step = (pl.program_id(0) == 0) & (pl.program_id(1) == 0)
        last_pair = pl.program_id(0) == npair - 1
        if ng_arr:
            g_start, g_forward, g_finish = _gather_phases(g_ins, g_outs, g_scratch[:ng_arr], *g_scratch[ng_arr:],
                                                          g_dtypes)
            pl.when(first_step)(g_start)
            pl.when(last_pair & (pl.program_id(1) == 0))(g_forward)
        i = pl.program_id(1)
        qs = _sb_stack(q_ref[...], scale)
        qpos = i * SB_BLOCK + lax.broadcasted_iota(jnp.int32, (SB_BLOCK, 1), 0)
        qpos = jnp.concatenate([qpos, qpos], axis=0)
        cat = _sb_cat("after")
        cat0 = _sb_cat("after", pad)
        ng = i // SB_GROUP

        def group(off, nblk, first_cat, allowed, carry):
            acc, run = carry
            kg = k_ref[pl.ds(off, nblk * SB_BLOCK), :].astype(BF16)
            vg = v_ref[pl.ds(off, nblk * SB_BLOCK), :].astype(BF16)
            lsz, l1m, _ = _sb_logsig(_dot(qs, kg, NT))
            if allowed is not None:
                l1m = jnp.where(allowed, l1m, 0.0)
            args = [None] * nblk
            for g in reversed(range(nblk)):
                sl = slice(g * SB_BLOCK, (g + 1) * SB_BLOCK)
                al = _sb_cumsum(l1m[:, sl], first_cat if g == 0 else cat)
                args[g] = lsz[:, sl] + al[:, :SB_BLOCK] + run
                run = run + al[:, SB_BLOCK:]
            wgt = jnp.exp(jnp.concatenate(args, axis=1))
            if allowed is not None:
                wgt = jnp.where(allowed, wgt, 0.0)
            return acc + _dot(wgt.astype(BF16), vg, NN), run

        def below(t, carry):
            gi = ng - 1 - t
            return group(pl.multiple_of(gi * gw, gw), SB_GROUP, jnp.where(gi == 0, cat0, cat), None, carry)

        top = ng * gw

        def top_group(nblk, carry):
            off = pl.multiple_of(jnp.minimum(top, lp - nblk * SB_BLOCK), SB_BLOCK)
            kpos = off + lax.broadcasted_iota(jnp.int32, (1, nblk * SB_BLOCK), 1)
            return group(off, nblk, cat, (kpos < qpos) & (kpos >= pad) & (kpos >= top), carry)

        zero = (jnp.zeros((2 * SB_BLOCK, HEAD_W), F32), jnp.zeros((2 * SB_BLOCK, HEAD_W), F32))
        carry = lax.cond(i - ng * SB_GROUP < SB_GROUP // 2, functools.partial(top_group, SB_GROUP // 2),
                         functools.partial(top_group, SB_GROUP), zero)
        used, acc, run = lax.while_loop(lambda s: (s[0] < ng) & (jnp.max(s[2]) > SB_FAR),
                                        lambda s: (s[0] + 1, *below(s[0], (s[1], s[2]))), (jnp.int32(0), *carry))
        o_ref[...] = _sb_unstack(acc)
        tot_ref[...] = _sb_unstack(run)
        nproc_ref[pl.program_id(0), i] = used.astype(F32)
        if ng_arr:
            pl.when(last_pair & (pl.program_id(1) == nb - 1))(g_finish)

    full = lambda c0: pl.BlockSpec((lp, HEAD_W), lambda p, i: (0, c0 + p))
    out = pl.BlockSpec((SB_BLOCK, HEAD_W), lambda p, i: (i, p))
    return pl.pallas_call(
        body, name=name, grid=(npair, nb),
        in_specs=[pl.BlockSpec((SB_BLOCK, HEAD_W), lambda p, i: (i, blk0 + p)), full(blk0 + npair), full(blk0 + 2 * npair)]
        + [pl.BlockSpec(memory_space=pltpu.VMEM)] * ng_arr,
        out_specs=[out, out, pl.BlockSpec(memory_space=pltpu.SMEM)] + [_ANY] * ng_arr,
        out_shape=[jax.ShapeDtypeStruct((lp, npair * HEAD_W), F32)] * 2 + [jax.ShapeDtypeStruct((npair, nb), F32)]
        + _gather_out_shapes(g_srcs, g_dtypes),
        scratch_shapes=_gather_scratch(g_srcs, g_dtypes) if ng_arr else [],
        compiler_params=_cp(("arbitrary", "arbitrary"), has_side_effects=bool(ng_arr)),
    )(p0, p0, p0, *g_srcs)


def _sb_bwd(p0, tot, nproc, dsrc, d_blk0, pad, *, name, scatter=()):
    lp = p0.shape[0]
    nb = lp // SB_BLOCK
    npair = SB_HEADS // 2
    blk0 = AB_SB // HEAD_W
    scale = SB_DH ** -0.5
    gw = SB_GROUP * SB_BLOCK
    assert pad < SB_BLOCK
    ns = len(scatter)

    def body(q_ref, k_ref, v_ref, tot_ref, nproc_ref, do_ref, *rest):
        s_ins, (dq_ref, dkt_ref, dvt_ref) = rest[:ns], rest[ns:ns + 3]
        s_outs, s_sems = rest[ns + 3:2 * ns + 3], rest[2 * ns + 3:]
        if ns:
            s_start, s_finish = _scatter_phases(s_ins, s_outs, *s_sems)
            pl.when((pl.program_id(0) == 0) & (pl.program_id(1) == 0))(s_start)
        i = pl.program_id(1)

        @pl.when(i == 0)
        def _():
            dkt_ref[...] = jnp.zeros_like(dkt_ref)
            dvt_ref[...] = jnp.zeros_like(dvt_ref)

        qs = _sb_stack(q_ref[...], scale)
        dos = _sb_stack(do_ref[...])
        qst, dost = qs.T, dos.T
        totv = tot_ref[...]
        ones = jnp.ones((1, HEAD_W), F32)
        tots = jnp.concatenate([totv[:, 0:1] * ones, totv[:, SB_DH:SB_DH + 1] * ones], axis=0)
        qpos = i * SB_BLOCK + lax.broadcasted_iota(jnp.int32, (SB_BLOCK, 1), 0)
        qpos = jnp.concatenate([qpos, qpos], axis=0)
        incl, incl0 = _sb_cat("incl"), _sb_cat("incl", pad)
        before = _sb_cat("before")
        ng = i // SB_GROUP
        used = jnp.clip(nproc_ref[pl.program_id(0), i].astype(jnp.int32), 0, ng)

        def dscore(z, e, ev, dl1m):
            r = 1.0 / (1.0 + e)
            sg = jnp.where(z >= 0, r, e * r)
            return ev * (1.0 - sg) - dl1m * sg

        def group(off, nblk, first_incl, allowed, carry):
            dq, prun, erun = carry
            width = nblk * SB_BLOCK
            kg = k_ref[pl.ds(off, width), :].astype(BF16)
            vg = v_ref[pl.ds(off, width), :].astype(BF16)
            z = _dot(qs, kg, NT)
            lsz, l1m, e = _sb_logsig(z)
            if allowed is not None:
                l1m = jnp.where(allowed, l1m, 0.0)
            dwgt = _dot(dos, vg, NT)
            dzs = [None] * nblk
            wgts = [None] * nblk
            for g in range(nblk):
                sl = slice(g * SB_BLOCK, (g + 1) * SB_BLOCK)
                al = _sb_cumsum(l1m[:, sl], first_incl if g == 0 else incl)
                wgt = jnp.exp(jnp.minimum(lsz[:, sl] + (tots - prun - al[:, :SB_BLOCK]), 0.0))
                if allowed is not None:
                    wgt = jnp.where(allowed[:, sl], wgt, 0.0)
                prun = prun + al[:, SB_BLOCK:]
                ev = wgt * dwgt[:, sl]
                el = _sb_cumsum(ev, before)
                dzs[g] = dscore(z[:, sl], e[:, sl], ev, erun + el[:, :SB_BLOCK])
                erun = erun + el[:, SB_BLOCK:]
                wgts[g] = wgt
            dz = jnp.concatenate(dzs, axis=1)
            if allowed is not None:
                dz = jnp.where(allowed, dz, 0.0)
            dz = dz.astype(BF16)
            wg = jnp.concatenate(wgts, axis=1).astype(BF16)
            dkt_ref[:, pl.ds(off, width)] += _dot(qst, dz, NN)
            dvt_ref[:, pl.ds(off, width)] += _dot(dost, wg, NN)
            return dq + _dot(dz, kg, NN), prun, erun

        def below(gi, carry):
            return group(pl.multiple_of(gi * gw, gw), SB_GROUP, jnp.where(gi == 0, incl0, incl), None, carry)

        zero = tuple(jnp.zeros((2 * SB_BLOCK, HEAD_W), F32) for _ in range(3))
        carry = lax.fori_loop(ng - used, ng, below, zero)
        top = ng * gw

        def top_group(nblk, carry):
            off = pl.multiple_of(jnp.minimum(top, lp - nblk * SB_BLOCK), SB_BLOCK)
            kpos = off + lax.broadcasted_iota(jnp.int32, (1, nblk * SB_BLOCK), 1)
            return group(off, nblk, incl, (kpos < qpos) & (kpos >= pad) & (kpos >= top), carry)

        dq, _, _ = lax.cond(i - ng * SB_GROUP < SB_GROUP // 2, functools.partial(top_group, SB_GROUP // 2),
                            functools.partial(top_group, SB_GROUP), carry)
        dq_ref[...] = _sb_unstack(dq) * scale
        if ns:
            pl.when((pl.program_id(0) == npair - 1) & (pl.program_id(1) == nb - 1))(s_finish)

    full = lambda c0: pl.BlockSpec((lp, HEAD_W), lambda p, i: (0, c0 + p))
    qb = lambda c0: pl.BlockSpec((SB_BLOCK, HEAD_W), lambda p, i: (i, c0 + p))
    tr = pl.BlockSpec((HEAD_W, lp), lambda p, i: (p, 0))
    return pl.pallas_call(
        body, name=name, grid=(npair, nb),
        in_specs=[qb(blk0), full(blk0 + npair), full(blk0 + 2 * npair), qb(0), pl.BlockSpec(memory_space=pltpu.SMEM),
                  qb(d_blk0)] + [_ANY] * ns,
        out_specs=[qb(0), tr, tr] + [_ANY] * ns,
        out_shape=[jax.ShapeDtypeStruct((lp, npair * HEAD_W), F32)]
        + [jax.ShapeDtypeStruct((npair * HEAD_W, lp), F32)] * 2
        + [jax.ShapeDtypeStruct(s.shape, s.dtype) for s in scatter],
        scratch_shapes=_scatter_scratch(ns) if ns else [],
        compiler_params=_cp(("arbitrary", "arbitrary"), has_side_effects=bool(ns)),
    )(p0, p0, p0, tot, nproc, dsrc, *scatter)


def _local_step(h0, target, pad, wts, hooks=None):
    lp = h0.shape[0]
    tm = _row_tile(lp, 1056)
    tkl = tm
    tml = _row_tile(lp, 528)
    d = D_MODEL
    mm = _mm
    mmw = functools.partial(_mm, out_dtype=BF16)
    g = {}

    h0_b = h0.astype(BF16)
    p0 = mm(h0_b, wts["w_ab"], "NN", tm=tm, tn=768, tk=d, name="l0_in_proj")
    ob, sb_tot, sb_used, *gathered = _sb_fwd(p0, pad, name="sb_fwd", gather=hooks["gather_a"] if hooks else None)
    if hooks:
        wts = {**wts, **hooks["weights_a"](gathered)}
    qkv = _gdn_pre_fwd(p0, wts["conv_w"], pad, name="gdn_pre_fwd")
    oa_raw, gdn_states, *gathered = _gdn_fwd(qkv, p0, wts["alog_v"], wts["dtb_v"], pad, name="gdn_fwd",
                                             gather=hooks["gather_b"] if hooks else None)
    if hooks:
        wts = {**wts, **hooks["weights_b"](gathered)}
    rows = lambda a, n: a.reshape(N_DEV, n // N_DEV, d)
    parts = g["parts"] = {}
    oab = _gate_fwd(oa_raw, p0, AB_Z // HEAD_W, wts["ab_gn"], ob, heads=GDN_HEADS, name="gdn_gate_fwd")
    ln = lambda kind, layer: (wts[f"ln_{kind}_g"][layer], wts[f"ln_{kind}_b"][layer])
    pre_mix0, h0a, h0a_b = mm(oab, wts["w_out0"], "NN", tm=tml, tn=d, tk=d, epi="ln", c=h0, scale=DN_ALPHA,
                              ln=ln("mix", 0), name="l0_out_proj")
    u0, act0 = mm(h0a_b, wts["w1"][0], "NN", tm=tm, tn=512, tk=d, b_dev=True, epi="relu2_copy", name="mlp0_up")
    pre_ffn0, h0b, h0b_b = mm(act0, wts["w2"][0], "NN", tm=tml, tn=d, tk=d, epi="ln", c=h0a, scale=DN_ALPHA,
                              ln=ln("ffn", 0), name="mlp0_down")
    p1 = mm(h0b_b, wts["w_c"], "NN", tm=tm, tn=512, tk=d, b_dev=True, name="l1_in_proj")
    oc_raw, hg_states, *gathered = _hg_fwd(p1, wts["lb"], pad, name="hg_fwd",
                                           gather=hooks["gather_c"] if hooks else None)
    if hooks:
        third = hooks["weights_c"](gathered)
        wts = {**wts, "w1": wts["w1"] + third["w1"], "w2": wts["w2"] + third["w2"]}
    oc = _gate_fwd(oc_raw, p1, 3 * HG_HEADS, wts["c_gn"], oc_raw, heads=HG_HEADS, name="hg_gate_fwd")
    pre_mix1, h1a, h1a_b = mm(oc, wts["w_out1"], "NN", tm=tml, tn=d, tk=d, epi="ln", c=h0b, scale=DN_ALPHA,
                              ln=ln("mix", 1), name="l1_out_proj")
    u1, act1 = mm(h1a_b, wts["w1"][1], "NN", tm=tm, tn=512, tk=d, b_dev=True, epi="relu2_copy", name="mlp1_up")
    pre_ffn1, h1b, _ = mm(act1, wts["w2"][1], "NN", tm=tml, tn=d, tk=d, epi="ln", c=h1a, scale=DN_ALPHA,
                          ln=ln("ffn", 1), name="mlp1_down")
    dy, loss_vec = _loss_head(h1b, target, name="loss_head")

    def mlp_bwd(layer, h_in_b, u, act, dpre, dpre_b, pre_mix):
        du = mm(dpre_b, wts["w2"][layer], "NT", tm=tm, tn=1024, tk=d, epi="relu2grad", c=u, out_dtype=BF16,
                name=f"mlp{layer}_d_hidden")
        dw2 = mmw(act, dpre_b, "TN", tm=1024, tn=1024, tk=tkl, name=f"mlp{layer}_dw2")
        dw1 = mmw(h_in_b, du, "TN", tm=1024, tn=512, tk=tkl, out_dev=True, name=f"mlp{layer}_dw1")
        return (*mm(du, k_major(wts["w1"][layer]), "NT", tm=tml, tn=1024, tk=2048, epi="ln_bwd", c=dpre, scale=DN_ALPHA,
                    ln=(pre_mix, wts["ln_mix_g"][layer]), name=f"mlp{layer}_d_in"), dw1, dw2)

    k_major = lambda wd: wd.transpose(1, 0, 2).reshape(wd.shape[1], -1)

    ln_ffn_dg, ln_ffn_db, ln_mix_dg, ln_mix_db, dw1s, dw2s = ([None, None] for _ in range(6))
    dpre, dpre_b, ln_ffn_dg[1], ln_ffn_db[1] = _ln_bwd(pre_ffn1, wts["ln_ffn_g"][1], dy, name="ln_ffn1_bwd")
    dpre, dpre_b, ln_mix_dg[1], ln_mix_db[1], dw1s[1], dw2s[1] = mlp_bwd(1, h1a_b, u1, act1, dpre, dpre_b, pre_mix1)
    g["c_w_out"] = mmw(oc, dpre_b, "TN", tm=1024, tn=1024, tk=tkl, name="l1_dw_out")
    doc = mm(dpre_b, wts["w_out1"], "NT", tm=tm, tn=1024, tk=d, name="l1_d_gate")
    doc_raw, dz1, g["c_gn"] = _gate_bwd(oc_raw, p1, 3 * HG_HEADS, wts["c_gn"], doc, heads=HG_HEADS, name="hg_gate_bwd")
    ready = [dw1s[1], rows(dw2s[1], D_FF), rows(g["c_w_out"], d)] if hooks else ()
    dq1, df1, di1, g["lb"], *got = _hg_bwd(p1, wts["lb"], hg_states, doc_raw, pad, name="hg_bwd", scatter=ready)
    parts.update(zip(("mlp_w1_1", "mlp_w2_1", "c_w_out"), got))
    dp1 = jnp.concatenate([dq1, df1, di1, dz1], axis=1).astype(BF16)
    g["c_w_in"] = mmw(h0b_b, dp1, "TN", tm=1024, tn=512, tk=tkl, out_dev=True, name="l1_dw_in")
    dpre, dpre_b, ln_ffn_dg[0], ln_ffn_db[0] = mm(
        dp1, k_major(wts["w_c"]), "NT", tm=tml, tn=1024, tk=2048, epi="ln_bwd", c=dpre, scale=DN_ALPHA,
        ln=(pre_ffn0, wts["ln_ffn_g"][0]), name="l1_d_in")
    dpre, dpre_b, ln_mix_dg[0], ln_mix_db[0], dw1s[0], dw2s[0] = mlp_bwd(0, h0a_b, u0, act0, dpre, dpre_b, pre_mix0)
    g["ab_w_out"] = mmw(oab, dpre_b, "TN", tm=1024, tn=1024, tk=tkl, name="l0_dw_out")
    doab = mm(dpre_b, wts["w_out0"], "NT", tm=tm, tn=1024, tk=d, name="l0_d_gate")
    doa_raw, dz0, g["ab_gn"] = _gate_bwd(oa_raw, p0, AB_Z // HEAD_W, wts["ab_gn"], doab, heads=GDN_HEADS,
                                         name="gdn_gate_bwd")
    ready = [g["c_w_in"]] if hooks else ()
    dqb, dkb_t, dvb_t, *got = _sb_bwd(p0, sb_tot, sb_used, doab, GDN_HEADS, pad, name="sb_bwd", scatter=ready)
    parts.update(zip(("c_w_in",), got))
    dkb, dvb = dkb_t.T, dvb_t.T
    ready = [dw1s[0], rows(dw2s[0], D_FF), rows(g["ab_w_out"], d)] if hooks else ()
    dqn, dkn, dvn, dba, g["alog_v"], g["dtb_v"], *got = _gdn_bwd(qkv, p0, wts["alog_v"], wts["dtb_v"], gdn_states,
                                                                 doa_raw, pad, name="gdn_bwd", scatter=ready)
    parts.update(zip(("mlp_w1_0", "mlp_w2_0", "ab_w_out"), got))
    dconv_in, g["conv_w"] = _gdn_pre_bwd(p0, wts["conv_w"], jnp.concatenate([dqn, dkn, dvn], axis=1), pad,
                                         name="gdn_pre_bwd")
    dp0 = jnp.concatenate([dconv_in, dz0, dqb, dkb, dvb, dba, jnp.zeros((lp, AB_CAT - AB_BA - HEAD_W), F32)],
                          axis=1).astype(BF16)
    g["w_ab"] = mmw(h0_b, dp0, "TN", tm=1024, tn=768, tk=tkl, name="l0_dw_in")
    last = ()
    if hooks:
        gab, ba0 = g["w_ab"], AB_Z + GDN_HEADS * HEAD_W
        gab = jnp.concatenate([gab[:, :ba0], gab[:, AB_BA:AB_BA + 2 * GDN_HEADS], gab[:, ba0:AB_BA]], axis=1)
        last = [gab.reshape(d, N_DEV, AB_IN // N_DEV).transpose(1, 0, 2)]
    res = mm(dp0, wts["w_ab"], "NT", tm=tm, tn=1024, tk=1920, epi="add", c=dpre, scale=DN_ALPHA, scatter=last,
             name="l0_d_in")
    dh0 = res[0] if last else res
    parts.update(zip(("ab_w_in",), res[1:] if last else ()))

    g["w1"], g["w2"] = dw1s, dw2s
    g["ln_mix_g"] = jnp.concatenate(ln_mix_dg, axis=0)
    g["ln_mix_b"] = jnp.concatenate(ln_mix_db, axis=0)
    g["ln_ffn_g"] = jnp.concatenate(ln_ffn_dg, axis=0)
    g["ln_ffn_b"] = jnp.concatenate(ln_ffn_db, axis=0)
    return loss_vec, dh0, g


N_CHIP = N_DEV // 2


def _place():
    x, y, c = lax.axis_index("x"), lax.axis_index("y"), lax.axis_index("c")
    return x, y, c, 2 * x + y


def _chip_dev(chip, core):
    return (chip // 2, chip % 2, core)


def _remote(src, dst, send_sem, recv_sem, dev):
    return pltpu.make_async_remote_copy(src_ref=src, dst_ref=dst, send_sem=send_sem, recv_sem=recv_sem,
                                        device_id=dev, device_id_type=pl.DeviceIdType.MESH)


_ANY = pl.BlockSpec(memory_space=pl.ANY)


def _gather(srcs, dtypes, *, name):
    n = len(srcs)

    def body(*refs):
        start, forward, finish = _gather_phases(refs[:n], refs[n:2 * n], refs[2 * n:3 * n], *refs[3 * n:], dtypes)
        start()
        forward()
        finish()

    return pl.pallas_call(
        body, name=name, in_specs=[pl.BlockSpec(memory_space=pltpu.VMEM)] * n, out_specs=[_ANY] * n,
        out_shape=_gather_out_shapes(srcs, dtypes), scratch_shapes=_gather_scratch(srcs, dtypes),
        compiler_params=_cp(has_side_effects=True),
    )(*srcs)


def _gather_out_shapes(srcs, dtypes):
    return [jax.ShapeDtypeStruct((N_DEV, *s.shape), dt) for s, dt in zip(srcs, dtypes)]


def _gather_scratch(srcs, dtypes):
    n = len(srcs)
    return [pltpu.VMEM(s.shape, dt) for s, dt in zip(srcs, dtypes)] + [
        pltpu.SemaphoreType.DMA((n, 2 * N_CHIP - 1)), pltpu.SemaphoreType.DMA((n, 2 * N_CHIP - 1)),
        pltpu.SemaphoreType.DMA((n,))]


def _gather_phases(ins, outs, stages, send_sems, recv_sems, local_sems, dtypes):
    n = len(ins)
    x, y, c, chip = _place()
    me = 2 * chip + c
    sibling = (x, y, 1 - c)

    def own(i):
        cps = [_remote(stages[i], outs[i].at[me], send_sems.at[i, 0], recv_sems.at[i, 0], sibling)]
        for j in range(1, N_CHIP):
            cps.append(_remote(stages[i], outs[i].at[me], send_sems.at[i, j], recv_sems.at[i, j],
                               _chip_dev(jnp.bitwise_xor(chip, j), c)))
        return cps

    def local(i):
        return pltpu.make_async_copy(stages[i], outs[i].at[me], local_sems.at[i])

    def passed_on(i, j):
        slot = outs[i].at[2 * jnp.bitwise_xor(chip, j) + c]
        return _remote(slot, slot, send_sems.at[i, N_CHIP - 1 + j], recv_sems.at[i, N_CHIP - 1 + j], sibling)

    def start():
        for i in range(n):
            stages[i][...] = ins[i][...].astype(dtypes[i])
            local(i).start()
            for cp in own(i):
                cp.start()

    def forward():
        for i in range(n):
            for j in range(1, N_CHIP):
                own(i)[j].wait_recv()
                passed_on(i, j).start()

    def finish():
        for i in range(n):
            own(i)[0].wait_recv()
            for j in range(1, N_CHIP):
                passed_on(i, j).wait_recv()
        for i in range(n):
            for cp in own(i):
                cp.wait_send()
            for j in range(1, N_CHIP):
                passed_on(i, j).wait_send()
            local(i).wait()

    return start, forward, finish


def _scatter_scratch(n):
    return [pltpu.SemaphoreType.DMA((n, N_DEV - 1)), pltpu.SemaphoreType.DMA((n, N_DEV - 1)),
            pltpu.SemaphoreType.DMA((n,))]


def _scatter_phases(ins, outs, send_sems, recv_sems, local_sems):
    n = len(ins)
    _, _, c, chip = _place()
    me = 2 * chip + c

    def copies():
        cps = []
        for i in range(n):
            cps.append(pltpu.make_async_copy(ins[i].at[me], outs[i].at[me], local_sems.at[i]))
            for k in range(1, N_DEV):
                peer = jnp.bitwise_xor(me, k)
                cps.append(_remote(ins[i].at[peer], outs[i].at[me], send_sems.at[i, k - 1], recv_sems.at[i, k - 1],
                                   _chip_dev(peer // 2, peer % 2)))
        return cps

    def start():
        for cp in copies():
            cp.start()

    def finish():
        for cp in copies():
            cp.wait()

    return start, finish


def _adamw(w, parts, m, v, *, name):
    r, c = w.shape
    s = parts.shape[0]
    tm = _row_tile(r, 128) if r % 8 == 0 else r
    c1 = 1.0 - ADAM_B1 ** ADAM_STEP
    c2 = 1.0 - ADAM_B2 ** ADAM_STEP

    def body(w_ref, p_ref, m_ref, v_ref, g_ref, d_ref, m2_ref, v2_ref):
        g = p_ref[0].astype(F32)
        for j in range(1, s):
            g = g + p_ref[j].astype(F32)
        m2 = ADAM_B1 * m_ref[...] + (1.0 - ADAM_B1) * g
        v2 = ADAM_B2 * v_ref[...] + (1.0 - ADAM_B2) * jnp.square(g)
        g_ref[...] = g
        m2_ref[...] = m2
        v2_ref[...] = v2
        d_ref[...] = -ADAM_LR * ((m2 / c1) / (jnp.sqrt(v2 / c2) + ADAM_EPS) + ADAM_WD * w_ref[...])

    blk = pl.BlockSpec((tm, c), lambda i: (i, 0))
    return pl.pallas_call(
        body, name=name, grid=(r // tm,),
        in_specs=[blk, pl.BlockSpec((s, tm, c), lambda i: (0, i, 0)), blk, blk], out_specs=[blk] * 4,
        out_shape=[jax.ShapeDtypeStruct((r, c), F32)] * 4, compiler_params=_cp(("parallel",)),
    )(w, parts, m, v)


_WEIGHTS = ("meta_tokens", "ab_w_in", "ab_conv_w", "ab_a_log", "ab_dt_bias", "ab_gnorm_g", "ab_w_out", "c_w_in",
            "c_lb_raw", "c_gnorm_g", "c_w_out", "ln_mix_g", "ln_mix_b", "mlp_w1", "mlp_w2", "ln_ffn_g", "ln_ffn_b")
_PACK_ROWS = (("ln_mix_g", 0), ("ln_mix_b", 2), ("ln_ffn_g", 4), ("ln_ffn_b", 6), ("c_lb_raw", 8))
_PACK_MISC_ROW = 10
_PACK_MISC = (("ab_gnorm_g", 0, 128), ("c_gnorm_g", 128, 128), ("ab_a_log", 256, GDN_HEADS), ("ab_dt_bias", 260, GDN_HEADS))
_PACK_N = 16
_SMALL_META = 16
_SMALL_CONV = 32
_SMALL_N = 40


def _pack_replicated(p):
    rows = jnp.zeros((_PACK_N, D_MODEL), F32)
    for name, r0 in _PACK_ROWS:
        rows = rows.at[r0:r0 + 2].set(p[name])
    for name, c0, width in _PACK_MISC:
        rows = rows.at[_PACK_MISC_ROW, c0:c0 + width].set(p[name].reshape(width))
    return rows


def _unpack_replicated(rows, like):
    out = {}
    for name, r0 in _PACK_ROWS:
        out[name] = rows[r0:r0 + 2]
    for name, c0, width in _PACK_MISC:
        out[name] = rows[_PACK_MISC_ROW, c0:c0 + width].reshape(like[name].shape)
    return out


def _lower_bound(c_lb_raw):
    lb_all = jnp.cumsum(jax.nn.softmax(c_lb_raw.astype(F32), axis=0), axis=0)
    return (lb_all - lb_all[0:1])[1].reshape(1, -1)


def kernel(x, meta_tokens, ab_w_in, ab_conv_w, ab_a_log, ab_dt_bias, ab_gnorm_g, ab_w_out, c_w_in, c_lb_raw, c_gnorm_g, c_w_out, ln_mix_g, ln_mix_b, mlp_w1, mlp_w2, ln_ffn_g, ln_ffn_b, loss_target, m_meta_tokens, m_ab_w_in, m_ab_conv_w, m_ab_a_log, m_ab_dt_bias, m_ab_gnorm_g, m_ab_w_out, m_c_w_in, m_c_lb_raw, m_c_gnorm_g, m_c_w_out, m_ln_mix_g, m_ln_mix_b, m_mlp_w1, m_mlp_w2, m_ln_ffn_g, m_ln_ffn_b, v_meta_tokens, v_ab_w_in, v_ab_conv_w, v_ab_a_log, v_ab_dt_bias, v_ab_gnorm_g, v_ab_w_out, v_c_w_in, v_c_lb_raw, v_c_gnorm_g, v_c_w_out, v_ln_mix_g, v_ln_mix_b, v_mlp_w1, v_mlp_w2, v_ln_ffn_g, v_ln_ffn_b):
    w = dict(zip(_WEIGHTS, (meta_tokens, ab_w_in, ab_conv_w, ab_a_log, ab_dt_bias, ab_gnorm_g, ab_w_out, c_w_in, c_lb_raw,
                            c_gnorm_g, c_w_out, ln_mix_g, ln_mix_b, mlp_w1, mlp_w2, ln_ffn_g, ln_ffn_b)))
    mom = dict(zip(_WEIGHTS, (m_meta_tokens, m_ab_w_in, m_ab_conv_w, m_ab_a_log, m_ab_dt_bias, m_ab_gnorm_g, m_ab_w_out,
                              m_c_w_in, m_c_lb_raw, m_c_gnorm_g, m_c_w_out, m_ln_mix_g, m_ln_mix_b, m_mlp_w1, m_mlp_w2,
                              m_ln_ffn_g, m_ln_ffn_b)))
    var = dict(zip(_WEIGHTS, (v_meta_tokens, v_ab_w_in, v_ab_conv_w, v_ab_a_log, v_ab_dt_bias, v_ab_gnorm_g, v_ab_w_out,
                              v_c_w_in, v_c_lb_raw, v_c_gnorm_g, v_c_w_out, v_ln_mix_g, v_ln_mix_b, v_mlp_w1, v_mlp_w2,
                              v_ln_ffn_g, v_ln_ffn_b)))
    me = 4 * lax.axis_index("x") + 2 * lax.axis_index("y") + lax.axis_index("c")
    seq = x.shape[1]
    pad = (-(N_META + seq)) % SB_BLOCK
    lp = pad + N_META + seq
    meta_w = D_MODEL // N_DEV
    conv_w_all = 2 * GDN_HEADS * HEAD_W + GDN_HEADS * HEAD_W
    conv_w_mine = conv_w_all // N_DEV

    g_meta, g_conv, g_ab_in = _gather([w["meta_tokens"], w["ab_conv_w"][0], w["ab_w_in"][0]], [F32, F32, BF16],
                                      name="gather_weights_first")
    meta_full = g_meta.transpose(1, 0, 2).reshape(N_META, D_MODEL)
    conv_full = g_conv.transpose(1, 0, 2).reshape(CONV_K, conv_w_all)
    ab_full = g_ab_in.transpose(1, 0, 2).reshape(D_MODEL, AB_IN)
    ba0 = AB_Z + 512
    w_ab = jnp.concatenate([ab_full[:, :ba0], ab_full[:, ba0 + 2 * GDN_HEADS:], ab_full[:, ba0:ba0 + 2 * GDN_HEADS],
                            jnp.zeros((D_MODEL, AB_CAT - AB_IN), BF16)], axis=1)
    vec128 = lambda p: jnp.zeros((1, HEAD_W), F32).at[0, :GDN_HEADS].set(p.reshape(GDN_HEADS))
    wts = dict(
        w_ab=w_ab, conv_w=conv_full, alog_v=vec128(w["ab_a_log"]), dtb_v=vec128(w["ab_dt_bias"]),
        ab_gn=w["ab_gnorm_g"][0], lb=_lower_bound(w["c_lb_raw"]), c_gn=w["c_gnorm_g"][0],
        ln_mix_g=w["ln_mix_g"], ln_mix_b=w["ln_mix_b"], ln_ffn_g=w["ln_ffn_g"], ln_ffn_b=w["ln_ffn_b"])

    def weights_a(gathered):
        g_ab_out, g_w1, g_w2 = gathered
        return dict(w_out0=g_ab_out.reshape(D_MODEL, D_MODEL), w1=[g_w1], w2=[g_w2.reshape(D_FF, D_MODEL)])

    def weights_b(gathered):
        g_c_in, g_c_out = gathered
        return dict(w_c=g_c_in, w_out1=g_c_out.reshape(D_MODEL, D_MODEL))

    def weights_c(gathered):
        g_w1, g_w2 = gathered
        return dict(w1=[g_w1], w2=[g_w2.reshape(D_FF, D_MODEL)])

    hooks = dict(
        gather_a=([w["ab_w_out"][0], w["mlp_w1"][0], w["mlp_w2"][0]], [BF16] * 3), weights_a=weights_a,
        gather_b=([w["c_w_in"][0], w["c_w_out"][0]], [BF16] * 2), weights_b=weights_b,
        gather_c=([w["mlp_w1"][1], w["mlp_w2"][1]], [BF16] * 2), weights_c=weights_c)

    h0 = jnp.concatenate([jnp.zeros((pad, D_MODEL), F32), meta_full, x[0]], axis=0)
    loss_vec, dh0, g = _local_step(h0, loss_target[0], pad, wts, hooks)
    loss = lax.psum(jnp.sum(loss_vec), ("x", "y", "c"))
    grad_x = dh0[lp - seq:][None]

    _, lb_vjp = jax.vjp(_lower_bound, w["c_lb_raw"])
    rep_part = _pack_replicated(dict(
        ln_mix_g=g["ln_mix_g"], ln_mix_b=g["ln_mix_b"], ln_ffn_g=g["ln_ffn_g"], ln_ffn_b=g["ln_ffn_b"],
        c_lb_raw=lb_vjp(g["lb"])[0], ab_gnorm_g=g["ab_gn"], c_gnorm_g=g["c_gn"],
        ab_a_log=g["alog_v"][0, :GDN_HEADS], ab_dt_bias=g["dtb_v"][0, :GDN_HEADS]))
    small = jnp.concatenate([rep_part, dh0[pad:pad + N_META], g["conv_w"].reshape(-1, D_MODEL),
                             jnp.zeros((_SMALL_N - _SMALL_CONV - CONV_K * conv_w_all // D_MODEL, D_MODEL), F32)], axis=0)
    (small_all,) = _gather([small], [F32], name="gather_small_grads")
    rep_out = _adamw(_pack_replicated(w), small_all[:, :_PACK_N], _pack_replicated(mom), _pack_replicated(var),
                     name="adamw_replicated")
    meta_parts = lax.dynamic_slice_in_dim(small_all[:, _SMALL_META:_SMALL_META + N_META], me * meta_w, meta_w, axis=2)
    meta_out = _adamw(w["meta_tokens"], meta_parts, mom["meta_tokens"], var["meta_tokens"], name="adamw_meta")
    conv_parts = small_all[:, _SMALL_CONV:_SMALL_CONV + CONV_K * conv_w_all // D_MODEL].reshape(N_DEV, CONV_K, conv_w_all)
    conv_parts = lax.dynamic_slice_in_dim(conv_parts, me * conv_w_mine, conv_w_mine, axis=2)
    conv_out = _adamw(w["ab_conv_w"][0], conv_parts, mom["ab_conv_w"][0], var["ab_conv_w"][0], name="adamw_conv")

    parts = g["parts"]
    big = [("ab_w_in", 0, parts["ab_w_in"]), ("ab_w_out", 0, parts["ab_w_out"]), ("mlp_w1", 0, parts["mlp_w1_0"]),
           ("mlp_w2", 0, parts["mlp_w2_0"]), ("c_w_in", 0, parts["c_w_in"]), ("c_w_out", 0, parts["c_w_out"]),
           ("mlp_w1", 1, parts["mlp_w1_1"]), ("mlp_w2", 1, parts["mlp_w2_1"])]
    big_out = {}
    for name, l, p in big:
        res = _adamw(w[name][l], p, mom[name][l], var[name][l], name=f"adamw_{name}{l}")
        big_out.setdefault(name, []).append(res)

    rep = [_unpack_replicated(r, w) for r in rep_out]
    outs = {}
    for name in _WEIGHTS:
        if name == "meta_tokens":
            outs[name] = list(meta_out)
        elif name == "ab_conv_w":
            outs[name] = [o[None] for o in conv_out]
        elif name in big_out:
            res = big_out[name]
            outs[name] = [o[None] for o in res[0]] if len(res) == 1 else [jnp.stack(pair) for pair in zip(*res)]
        else:
            outs[name] = [r[name] for r in rep]
    flat = [loss, grad_x]
    for kind in range(4):
        flat += [outs[name][kind] for name in _WEIGHTS]
    return tuple(flat)
```

```python
import functools

import jax
import jax.numpy as jnp
from jax import lax
from jax.experimental import pallas as pl
from jax.experimental.pallas import tpu as pltpu

F32 = jnp.float32
BF16 = jnp.bfloat16

N_DEV = 8
D_MODEL = 1024
N_META = 16
D_FF = 4096
DEPTH = 2
GDN_HEADS = 4
SB_HEADS = 8
SB_DH = 64
HG_HEADS = 8
HEAD_W = 128
CHUNK = 64
SB_BLOCK = 128
CONV_K = 4
DN_ALPHA = float((2 * DEPTH) ** 0.25)
LN_EPS = 1e-5
RMS_EPS = 1e-6
L2_EPS = 1e-6
ADAM_LR, ADAM_B1, ADAM_B2, ADAM_EPS, ADAM_WD, ADAM_STEP = 0.001, 0.9, 0.999, 1e-08, 0.01, 10

AB_Z = 1536
AB_SB = 2048
AB_BA = 3584
AB_CAT = 3840
AB_IN = 3592

VMEM_LIMIT = 56 * 1024 * 1024


def _cp(sem=None, **kw):
    if sem is not None:
        kw["dimension_semantics"] = sem
    return pltpu.CompilerParams(vmem_limit_bytes=VMEM_LIMIT, **kw)


def _row_tile(n, want):
    best = 8
    for t in range(8, min(n, want) + 1, 8):
        if n % t == 0:
            best = t
    return best


@jax.custom_vjp
def _sigmoid(x):
    e = jnp.exp(-jnp.abs(x))
    r = 1.0 / (1.0 + e)
    return jnp.where(x >= 0, r, e * r)


def _sigmoid_fwd(x):
    s = _sigmoid(x)
    return s, s


def _sigmoid_bwd(s, g):
    return (g * s * (1.0 - s),)


_sigmoid.defvjp(_sigmoid_fwd, _sigmoid_bwd)


def _log1p_exp_neg_abs(x):
    e = jnp.exp(-jnp.abs(x))
    return jnp.where(e < 1e-4, e - 0.5 * e * e, jnp.log(1.0 + e))


@jax.custom_vjp
def _softplus(x):
    return jnp.maximum(x, 0.0) + _log1p_exp_neg_abs(x)


def _softplus_fwd(x):
    return _softplus(x), x


def _softplus_bwd(x, g):
    return (g * _sigmoid(x),)


_softplus.defvjp(_softplus_fwd, _softplus_bwd)


def _silu(x):
    return x * _sigmoid(x)


def _silu_grad(x):
    s = _sigmoid(x)
    return s * (1.0 + x * (1.0 - s))


def _dot(a, b, dims, precision=None):
    return lax.dot_general(a, b, (dims, ((), ())), precision=precision, preferred_element_type=F32)


NN = ((1,), (0,))
NT = ((1,), (1,))
TN = ((0,), (0,))


def _bdot(a, b, dims):
    return _dot(a.astype(BF16), b.astype(BF16), dims)


def _layer_norm(pre, g, beta):
    mu = jnp.mean(pre, axis=-1, keepdims=True)
    xc = pre - mu
    var = jnp.mean(xc * xc, axis=-1, keepdims=True)
    return xc * lax.rsqrt(var + LN_EPS) * g + beta


def _layer_norm_bwd(pre, g, dy):
    mu = jnp.mean(pre, axis=-1, keepdims=True)
    xc = pre - mu
    rstd = lax.rsqrt(jnp.mean(xc * xc, axis=-1, keepdims=True) + LN_EPS)
    xhat = xc * rstd
    dxh = dy * g
    m1 = jnp.mean(dxh, axis=-1, keepdims=True)
    m2 = jnp.mean(dxh * xhat, axis=-1, keepdims=True)
    return (rstd * (dxh - m1 - xhat * m2), jnp.sum(dy * xhat, axis=0, keepdims=True),
            jnp.sum(dy, axis=0, keepdims=True))


def _mm(a, b, mode, *, tm, tn, tk, name, epi=None, c=None, scale=1.0, b_dev=False, out_dev=False, out_dtype=F32,
        ln=None, scatter=()):
    if mode == "NN":
        m, kk = a.shape
        n = b.shape[2] * N_DEV if b_dev else b.shape[1]
    elif mode == "NT":
        m, kk = a.shape
        n = b.shape[1] if b_dev else b.shape[0]
    else:
        kk, m = a.shape
        n = b.shape[1]
    assert m % tm == 0 and n % tn == 0 and kk % tk == 0, (name, m, n, kk, tm, tn, tk)
    nk = kk // tk
    dims = {"NN": NN, "NT": NT, "TN": TN}[mode]

    if mode == "TN":
        a_spec = pl.BlockSpec((tk, tm), lambda i, j, k: (k, i))
    else:
        a_spec = pl.BlockSpec((tm, tk), lambda i, j, k: (i, k))
    if mode == "NN":
        if b_dev:
            assert tn == b.shape[2]
            b_spec = pl.BlockSpec((None, tk, tn), lambda i, j, k: (j, k, 0))
        else:
            b_spec = pl.BlockSpec((tk, tn), lambda i, j, k: (k, j))
    elif mode == "NT":
        if b_dev:
            assert tk == b.shape[2]
            b_spec = pl.BlockSpec((None, tn, tk), lambda i, j, k: (k, j, 0))
        else:
            b_spec = pl.BlockSpec((tn, tk), lambda i, j, k: (j, k))
    else:
        b_spec = pl.BlockSpec((tk, tn), lambda i, j, k: (k, j))
    in_specs = [a_spec, b_spec]
    operands = [a, b]
    if c is not None:
        in_specs.append(pl.BlockSpec((tm, tn), lambda i, j, k: (i, j)))
        operands.append(c)
    if epi == "ln":
        assert tn == n and not out_dev
        in_specs += [pl.BlockSpec((1, n), lambda i, j, k: (0, 0))] * 2
        operands += [ln[0].reshape(1, n), ln[1].reshape(1, n)]
    elif epi == "ln_bwd":
        assert tn == n and not out_dev
        in_specs += [pl.BlockSpec((tm, tn), lambda i, j, k: (i, j)), pl.BlockSpec((1, n), lambda i, j, k: (0, 0))]
        operands += [ln[0], ln[1].reshape(1, n)]
    if out_dev:
        assert tn == n // N_DEV
        out_shape = jax.ShapeDtypeStruct((N_DEV, m, tn), out_dtype)
        out_spec = pl.BlockSpec((None, tm, tn), lambda i, j, k: (j, i, 0))
    else:
        out_shape = jax.ShapeDtypeStruct((m, n), out_dtype)
        out_spec = pl.BlockSpec((tm, tn), lambda i, j, k: (i, j))
    if epi == "ln":
        out_shape = [out_shape, out_shape, jax.ShapeDtypeStruct((m, n), BF16)]
        out_spec = [out_spec] * 3
    elif epi == "relu2_copy":
        assert not out_dev
        out_shape = [out_shape, jax.ShapeDtypeStruct((m, n), BF16)]
        out_spec = [out_spec] * 2
    elif epi == "ln_bwd":
        vec_shape, vec_spec = jax.ShapeDtypeStruct((1, n), F32), pl.BlockSpec((1, n), lambda i, j, k: (0, 0))
        out_shape = [out_shape, jax.ShapeDtypeStruct((m, n), BF16), vec_shape, vec_shape]
        out_spec = [out_spec, out_spec, vec_spec, vec_spec]
    n_out = {"ln": 3, "relu2_copy": 2, "ln_bwd": 4}.get(epi, 1)
    ns = len(scatter)
    if ns:
        in_specs += [_ANY] * ns
        operands += list(scatter)
        out_shape = (out_shape if n_out > 1 else [out_shape]) + [jax.ShapeDtypeStruct(s.shape, s.dtype) for s in scatter]
        out_spec = (out_spec if n_out > 1 else [out_spec]) + [_ANY] * ns
    n_in = len(operands)
    grid = (m // tm, n // tn, nk)

    def body(*refs):
        a_ref, b_ref = refs[0], refs[1]
        c_ref = refs[2] if c is not None else None
        o_ref = refs[n_in]
        scratch0 = n_in + n_out + ns
        acc_ref = refs[scratch0] if nk > 1 else None
        if ns:
            s_start, s_finish = _scatter_phases(refs[n_in - ns:n_in], refs[n_in + n_out:scratch0],
                                                *refs[scratch0 + (1 if nk > 1 else 0):])
            at = lambda step: functools.reduce(lambda x, y: x & y, [pl.program_id(ax) == step[ax] for ax in range(3)])
            pl.when(at((0, 0, 0)))(s_start)
        p = _dot(a_ref[...].astype(BF16), b_ref[...].astype(BF16), dims)
        first_rows = pl.program_id(0) == 0

        def finish(acc):
            if epi == "add":
                acc = acc + scale * c_ref[...]
            elif epi == "relu2grad":
                acc = acc * (2.0 * jnp.maximum(c_ref[...], 0.0))
            elif epi == "relu2_copy":
                refs[n_in + 1][...] = jnp.square(jnp.maximum(acc, 0.0)).astype(BF16)
            elif epi == "ln_bwd":
                acc, dg, db = _layer_norm_bwd(refs[3][...], refs[4][...], acc + scale * c_ref[...])
                dg_ref, db_ref = refs[n_in + 2], refs[n_in + 3]

                @pl.when(first_rows)
                def _():
                    dg_ref[...] = jnp.zeros_like(dg_ref)
                    db_ref[...] = jnp.zeros_like(db_ref)

                dg_ref[...] += dg
                db_ref[...] += db
                refs[n_in + 1][...] = acc.astype(BF16)
            elif epi == "ln":
                acc = acc + scale * c_ref[...]
                y = _layer_norm(acc, refs[3][...], refs[4][...])
                refs[n_in + 1][...] = y
                refs[n_in + 2][...] = y.astype(BF16)
            o_ref[...] = acc.astype(out_dtype)

        if nk == 1:
            finish(p)
        else:
            k = pl.program_id(2)

            @pl.when(k == 0)
            def _():
                acc_ref[...] = p

            @pl.when(k > 0)
            def _():
                acc_ref[...] += p

            @pl.when(k == nk - 1)
            def _():
                finish(acc_ref[...])

        if ns:
            pl.when(at(tuple(g - 1 for g in grid)))(s_finish)

    res = pl.pallas_call(
        body, name=name, grid=grid, in_specs=in_specs, out_specs=out_spec, out_shape=out_shape,
        scratch_shapes=([pltpu.VMEM((tm, tn), F32)] if nk > 1 else []) + (_scatter_scratch(ns) if ns else []),
        compiler_params=_cp(("arbitrary",) * 3 if ns or epi == "ln_bwd" else ("parallel", "parallel", "arbitrary"),
                            has_side_effects=bool(ns)),
    )(*operands)
    return res


def _ln_bwd(pre, g, dy, *, name):
    lp, d = pre.shape
    tm = _row_tile(lp, 512)

    def body(pre_ref, g_ref, dy_ref, dpre_ref, dpreb_ref, dg_ref, db_ref):
        dpre, dg, db = _layer_norm_bwd(pre_ref[...], g_ref[...], dy_ref[...])
        dpre_ref[...] = dpre
        dpreb_ref[...] = dpre.astype(BF16)

        @pl.when(pl.program_id(0) == 0)
        def _():
            dg_ref[...] = jnp.zeros_like(dg_ref)
            db_ref[...] = jnp.zeros_like(db_ref)

        dg_ref[...] += dg
        db_ref[...] += db

    row = pl.BlockSpec((tm, d), lambda i: (i, 0))
    vec = pl.BlockSpec((1, d), lambda i: (0, 0))
    return pl.pallas_call(
        body, name=name, grid=(lp // tm,), in_specs=[row, vec, row], out_specs=[row, row, vec, vec],
        out_shape=[jax.ShapeDtypeStruct((lp, d), F32), jax.ShapeDtypeStruct((lp, d), BF16),
                   jax.ShapeDtypeStruct((1, d), F32), jax.ShapeDtypeStruct((1, d), F32)],
        compiler_params=_cp(("arbitrary",)),
    )(pre, g.reshape(1, d), dy)


def _loss_head(y, target, *, name):
    lp, d = y.shape
    seq = target.shape[0]
    tm = SB_BLOCK
    first = (lp - seq) // tm
    assert (lp - seq) % tm == 0 and seq % tm == 0

    def body(y_ref, t_ref, dy_ref, loss_ref):
        i = pl.program_id(0)
        live = i >= first
        diff = jnp.where(live, y_ref[...] - t_ref[...], 0.0)
        dy_ref[...] = diff * (1.0 / d)

        @pl.when(i == 0)
        def _():
            loss_ref[...] = jnp.zeros_like(loss_ref)

        loss_ref[...] += jnp.sum(diff * diff, axis=0, keepdims=True) * (0.5 / d)

    return pl.pallas_call(
        body, name=name, grid=(lp // tm,),
        in_specs=[pl.BlockSpec((tm, d), lambda i: (i, 0)),
                  pl.BlockSpec((tm, d), lambda i: (jnp.maximum(i - first, 0), 0))],
        out_specs=[pl.BlockSpec((tm, d), lambda i: (i, 0)), pl.BlockSpec((1, d), lambda i: (0, 0))],
        out_shape=[jax.ShapeDtypeStruct((lp, d), F32), jax.ShapeDtypeStruct((1, d), F32)],
        compiler_params=_cp(("arbitrary",)),
    )(y, target)


def _gate_fwd(o, zsrc, z_blk0, g, other, *, heads, name):
    lp = o.shape[0]
    tm = _row_tile(lp, 512)
    w = heads * HEAD_W
    assert (z_blk0 * HEAD_W) % w == 0
    has_other = w < D_MODEL

    def body(o_ref, z_ref, g_ref, *rest):
        y_ref = rest[-1]
        gv = g_ref[...]
        for h in range(heads):
            cs = slice(h * HEAD_W, (h + 1) * HEAD_W)
            ov = o_ref[:, cs]
            r = lax.rsqrt(jnp.mean(ov * ov, axis=-1, keepdims=True) + RMS_EPS)
            y_ref[:, cs] = (ov * r * gv * _silu(z_ref[:, cs])).astype(BF16)
        if has_other:
            y_ref[:, w:] = rest[0][...].astype(BF16)

    row = lambda width, blk: pl.BlockSpec((tm, width), lambda i: (i, blk))
    return pl.pallas_call(
        body, name=name, grid=(lp // tm,),
        in_specs=[row(w, 0), row(w, z_blk0 * HEAD_W // w), pl.BlockSpec((1, HEAD_W), lambda i: (0, 0))]
        + ([row(D_MODEL - w, 0)] if has_other else []),
        out_specs=row(D_MODEL, 0), out_shape=jax.ShapeDtypeStruct((lp, D_MODEL), BF16),
        compiler_params=_cp(("parallel",)),
    )(o, zsrc, g.reshape(1, HEAD_W), *([other] if has_other else []))


def _gate_bwd(o, zsrc, z_blk0, g, dy, *, heads, name):
    lp = o.shape[0]
    tm = _row_tile(lp, 512)

    w = heads * HEAD_W
    assert (z_blk0 * HEAD_W) % w == 0

    def body(o_ref, z_ref, g_ref, dy_ref, do_ref, dz_ref, dg_ref):
        @pl.when(pl.program_id(0) == 0)
        def _():
            dg_ref[...] = jnp.zeros_like(dg_ref)

        gv = g_ref[...]
        dg = jnp.zeros((1, HEAD_W), F32)
        for h in range(heads):
            cs = slice(h * HEAD_W, (h + 1) * HEAD_W)
            ov, zv, dyv = o_ref[:, cs], z_ref[:, cs], dy_ref[:, cs]
            r = lax.rsqrt(jnp.mean(ov * ov, axis=-1, keepdims=True) + RMS_EPS)
            nrm = ov * r
            s = _silu(zv)
            dn = dyv * gv * s
            do_ref[:, cs] = r * (dn - nrm * jnp.mean(dn * nrm, axis=-1, keepdims=True))
            dz_ref[:, cs] = dyv * nrm * gv * _silu_grad(zv)
            dg = dg + jnp.sum(dyv * nrm * s, axis=0, keepdims=True)
        dg_ref[...] += dg

    row = lambda blk: pl.BlockSpec((tm, w), lambda i: (i, blk))
    vec = pl.BlockSpec((1, HEAD_W), lambda i: (0, 0))
    return pl.pallas_call(
        body, name=name, grid=(lp // tm,),
        in_specs=[row(0), row(z_blk0 * HEAD_W // w), vec, row(0)], out_specs=[row(0), row(0), vec],
        out_shape=[jax.ShapeDtypeStruct((lp, w), F32), jax.ShapeDtypeStruct((lp, w), F32),
                   jax.ShapeDtypeStruct((1, HEAD_W), F32)],
        compiler_params=_cp(("arbitrary",)),
    )(o, zsrc, g.reshape(1, HEAD_W), dy)


def _conv_taps(x, w):
    acc = w[CONV_K - 1:CONV_K, :] * x
    for k in range(CONV_K - 1):
        acc = acc + w[k:k + 1, :] * pltpu.roll(x, CONV_K - 1 - k, 0)
    return acc


def _gdn_pre_fwd(p0, conv_w, pad, *, name):
    lp = p0.shape[0]
    nq = GDN_HEADS
    qscale = HEAD_W ** -0.5

    def body(x_ref, w_ref, y_ref):
        j = pl.program_id(0)
        c = _conv_taps(x_ref[...], w_ref[...])
        s = _silu(c)
        r = lax.rsqrt(jnp.sum(s * s, axis=-1, keepdims=True) + L2_EPS)
        mult = jnp.where(j < nq, r * qscale, jnp.where(j < 2 * nq, r, 1.0))
        rows = lax.broadcasted_iota(jnp.int32, (lp, 1), 0)
        y_ref[...] = jnp.where(rows >= pad, s * mult, 0.0)

    return pl.pallas_call(
        body, name=name, grid=(3 * nq,),
        in_specs=[pl.BlockSpec((lp, HEAD_W), lambda j: (0, j)), pl.BlockSpec((CONV_K, HEAD_W), lambda j: (0, j))],
        out_specs=pl.BlockSpec((lp, HEAD_W), lambda j: (0, j)),
        out_shape=jax.ShapeDtypeStruct((lp, 3 * nq * HEAD_W), F32), compiler_params=_cp(("parallel",)),
    )(p0, conv_w)


def _gdn_pre_bwd(p0, conv_w, dqkv, pad, *, name):
    lp = p0.shape[0]
    nq = GDN_HEADS
    qscale = HEAD_W ** -0.5

    def body(x_ref, w_ref, dy_ref, dx_ref, dw_ref):
        j = pl.program_id(0)
        x, w = x_ref[...], w_ref[...]
        c = _conv_taps(x, w)
        s = _silu(c)
        r = lax.rsqrt(jnp.sum(s * s, axis=-1, keepdims=True) + L2_EPS)
        rows = lax.broadcasted_iota(jnp.int32, (lp, 1), 0)
        dy = jnp.where(rows >= pad, dy_ref[...], 0.0)
        nrm = s * r
        dn = dy * jnp.where(j < nq, qscale, 1.0)
        ds_norm = r * (dn - nrm * jnp.sum(nrm * dn, axis=-1, keepdims=True))
        ds = jnp.where(j < 2 * nq, ds_norm, dy)
        dc = ds * _silu_grad(c)
        dx = w[CONV_K - 1:CONV_K, :] * dc
        dws = [None] * CONV_K
        dws[CONV_K - 1] = jnp.sum(dc * x, axis=0, keepdims=True)
        for k in range(CONV_K - 1):
            sh = CONV_K - 1 - k
            dx = dx + w[k:k + 1, :] * pltpu.roll(dc, lp - sh, 0)
            dws[k] = jnp.sum(dc * pltpu.roll(x, sh, 0), axis=0, keepdims=True)
        dx_ref[...] = dx
        dw_ref[...] = jnp.concatenate(dws, axis=0)

    blk = pl.BlockSpec((lp, HEAD_W), lambda j: (0, j))
    wblk = pl.BlockSpec((CONV_K, HEAD_W), lambda j: (0, j))
    return pl.pallas_call(
        body, name=name, grid=(3 * nq,), in_specs=[blk, wblk, blk], out_specs=[blk, wblk],
        out_shape=[jax.ShapeDtypeStruct((lp, 3 * nq * HEAD_W), F32),
                   jax.ShapeDtypeStruct((CONV_K, 3 * nq * HEAD_W), F32)],
        compiler_params=_cp(("parallel",)),
    )(p0, conv_w, dqkv)


@jax.custom_vjp
def _inv_unit_lower(m):
    c = m.shape[0]
    eye = (lax.broadcasted_iota(jnp.int32, (c, c), 0) == lax.broadcasted_iota(jnp.int32, (c, c), 1)).astype(F32)
    x = eye - m
    p = m
    n = 2
    while n < CHUNK:
        p = _bdot(p, p, NN)
        x = x + _bdot(x, p, NN)
        n *= 2
    return x


def _inv_fwd(m):
    t = _inv_unit_lower(m)
    return t, t


def _inv_bwd(t, g):
    return (-_bdot(_bdot(t, g, TN), t, NT),)


_inv_unit_lower.defvjp(_inv_fwd, _inv_bwd)


GDN_STEP = 2


def _heads_to_rows(x, nh):
    return jnp.concatenate([x[:, h * HEAD_W:(h + 1) * HEAD_W] for h in range(nh)], axis=0)


def _rows_to_heads(x, nh):
    c = x.shape[0] // nh
    return jnp.concatenate([x[h * c:(h + 1) * c] for h in range(nh)], axis=1)


def _gdn_chunk(q, k, v, ba, alog, dtb, states, valid):
    nh = GDN_HEADS
    c = q.shape[0]
    r = nh * c
    lane = lax.broadcasted_iota(jnp.int32, (1, HEAD_W), 1)
    pick = lambda x, l: jnp.sum(jnp.where(lane == l, x, 0.0), axis=-1, keepdims=True)
    beta = jnp.concatenate([jnp.where(valid, _sigmoid(pick(ba, h)), 0.0) for h in range(nh)], axis=0)
    g = jnp.concatenate(
        [jnp.where(valid, -jnp.exp(pick(alog, h)) * _softplus(pick(ba, nh + h) + pick(dtb, h)), 0.0) for h in range(nh)],
        axis=0)
    qs, ks, vs = _heads_to_rows(q, nh), _heads_to_rows(k, nh), _heads_to_rows(v, nh)
    rr = lax.broadcasted_iota(jnp.int32, (r, r), 0)
    cc = lax.broadcasted_iota(jnp.int32, (r, r), 1)
    same = (rr // c) == (cc // c)
    causal, strict = same & (cc <= rr), same & (cc < rr)
    lower = jnp.where(causal, 1.0, 0.0).astype(BF16)
    upper = jnp.where(same & (cc >= rr), 1.0, 0.0).astype(BF16)
    gcb = _mask_mm(lower, upper, g * jnp.ones((1, HEAD_W), F32))
    gc_col = jnp.concatenate([gcb] * (r // HEAD_W), axis=1)
    decay = jnp.where(causal, jnp.exp(jnp.minimum(gc_col - gc_col.T, 0.0)), 0.0)
    egc = jnp.exp(gcb)
    kb = ks * beta
    m = jnp.where(strict, _bdot(kb, ks, NT) * decay, 0.0)
    t = _inv_unit_lower(m)
    u = _bdot(t, vs * beta, NN)
    w = _bdot(t, kb * egc, NN)
    a = _bdot(qs, ks, NT) * decay
    rows = lambda x, h: x[h * c:(h + 1) * c]
    qe = qs * egc
    v_new = u - jnp.concatenate([_bdot(rows(w, h), states[h], NN) for h in range(nh)], axis=0)
    o = jnp.concatenate([_bdot(rows(qe, h), states[h], NN) for h in range(nh)], axis=0) + _bdot(a, v_new, NN)
    new_states = []
    for h in range(nh):
        gl = gcb[(h + 1) * c - 1:(h + 1) * c, :]
        k_dec = rows(ks, h) * jnp.exp(gl - rows(gcb, h))
        new_states.append(states[h] * jnp.exp(gl) + _bdot(k_dec, rows(v_new, h), TN))
    return _rows_to_heads(o, nh), new_states


def _gdn_fwd(qkv, p0, alog_v, dtb_v, pad, *, name, gather=None):
    lp = qkv.shape[0]
    n = lp // CHUNK
    nh = GDN_HEADS
    assert n % GDN_STEP == 0
    steps, rows = n // GDN_STEP, GDN_STEP * CHUNK
    g_srcs, g_dtypes = gather if gather is not None else ([], [])
    ng_arr = len(g_srcs)

    def body(q_ref, k_ref, v_ref, ba_ref, al_ref, dt_ref, *rest):
        g_ins, (o_ref, st_ref) = rest[:ng_arr], rest[ng_arr:ng_arr + 2]
        g_outs, s_ref, g_scratch = rest[ng_arr + 2:2 * ng_arr + 2], rest[2 * ng_arr + 2], rest[2 * ng_arr + 3:]
        i = pl.program_id(0)
        if ng_arr:
            g_start, g_forward, g_finish = _gather_phases(g_ins, g_outs, g_scratch[:ng_arr], *g_scratch[ng_arr:],
                                                          g_dtypes)
            pl.when(i == 0)(g_start)
            pl.when(i == (3 * steps) // 4)(g_forward)

        @pl.when(i == 0)
        def _():
            s_ref[...] = jnp.zeros_like(s_ref)

        s = s_ref[...]
        s = [s[h] for h in range(nh)]
        q, k, v, ba, al, dt = q_ref[...], k_ref[...], v_ref[...], ba_ref[...], al_ref[...], dt_ref[...]
        outs = []
        for c in range(GDN_STEP):
            sl = slice(c * CHUNK, (c + 1) * CHUNK)
            valid = (i * rows + c * CHUNK + lax.broadcasted_iota(jnp.int32, (CHUNK, 1), 0)) >= pad
            for h in range(nh):
                st_ref[c, h] = s[h]
            o, s = _gdn_chunk(q[sl], k[sl], v[sl], ba[sl], al, dt, s, valid)
            outs.append(o)
        o_ref[...] = jnp.concatenate(outs, axis=0)
        for h in range(nh):
            s_ref[h] = s[h]
        if ng_arr:
            pl.when(i == steps - 1)(g_finish)

    w = nh * HEAD_W
    vec = pl.BlockSpec((1, HEAD_W), lambda i: (0, 0))
    return pl.pallas_call(
        body, name=name, grid=(steps,),
        in_specs=[pl.BlockSpec((rows, w), lambda i: (i, 0)), pl.BlockSpec((rows, w), lambda i: (i, 1)),
                  pl.BlockSpec((rows, w), lambda i: (i, 2)), pl.BlockSpec((rows, HEAD_W), lambda i: (i, AB_BA // HEAD_W)),
                  vec, vec] + [pl.BlockSpec(memory_space=pltpu.VMEM)] * ng_arr,
        out_specs=[pl.BlockSpec((rows, w), lambda i: (i, 0)),
                   pl.BlockSpec((GDN_STEP, nh, HEAD_W, HEAD_W), lambda i: (i, 0, 0, 0))] + [_ANY] * ng_arr,
        out_shape=[jax.ShapeDtypeStruct((lp, w), F32), jax.ShapeDtypeStruct((n, nh, HEAD_W, HEAD_W), F32)]
        + _gather_out_shapes(g_srcs, g_dtypes),
        scratch_shapes=[pltpu.VMEM((nh, HEAD_W, HEAD_W), F32)] + (_gather_scratch(g_srcs, g_dtypes) if ng_arr else []),
        compiler_params=_cp(("arbitrary",), has_side_effects=bool(ng_arr)),
    )(qkv, qkv, qkv, p0, alog_v, dtb_v, *g_srcs)


def _gdn_bwd(qkv, p0, alog_v, dtb_v, states, do, pad, *, name, scatter=()):
    lp = qkv.shape[0]
    n = lp // CHUNK
    nh = GDN_HEADS
    assert n % GDN_STEP == 0
    steps, rows = n // GDN_STEP, GDN_STEP * CHUNK
    ns = len(scatter)

    def body(q_ref, k_ref, v_ref, ba_ref, al_ref, dt_ref, st_ref, do_ref, *rest):
        s_ins, (dq_ref, dk_ref, dv_ref, dba_ref, dal_ref, ddt_ref) = rest[:ns], rest[ns:ns + 6]
        s_outs, ds_ref, s_sems = rest[ns + 6:2 * ns + 6], rest[2 * ns + 6], rest[2 * ns + 7:]
        step = pl.program_id(0)
        i = steps - 1 - step
        if ns:
            s_start, s_finish = _scatter_phases(s_ins, s_outs, *s_sems)
            pl.when(step == 0)(s_start)

        @pl.when(step == 0)
        def _():
            ds_ref[...] = jnp.zeros_like(ds_ref)
            dal_ref[...] = jnp.zeros_like(dal_ref)
            ddt_ref[...] = jnp.zeros_like(ddt_ref)

        q, k, v, ba, al, dt = q_ref[...], k_ref[...], v_ref[...], ba_ref[...], al_ref[...], dt_ref[...]
        st, do, dst = st_ref[...], do_ref[...], ds_ref[...]
        vjps = []
        for c in range(GDN_STEP):
            sl = slice(c * CHUNK, (c + 1) * CHUNK)
            valid = (i * rows + c * CHUNK + lax.broadcasted_iota(jnp.int32, (CHUNK, 1), 0)) >= pad
            fn = functools.partial(_gdn_chunk, valid=valid)
            vjps.append(jax.vjp(fn, q[sl], k[sl], v[sl], ba[sl], al, dt, [st[c, h] for h in range(nh)])[1])
        ds = [dst[h] for h in range(nh)]
        grads = [None] * GDN_STEP
        for c in reversed(range(GDN_STEP)):
            grads[c] = vjps[c]((do[c * CHUNK:(c + 1) * CHUNK], ds))
            ds = grads[c][6]
        for j, ref in enumerate((dq_ref, dk_ref, dv_ref, dba_ref)):
            ref[...] = jnp.concatenate([gr[j] for gr in grads], axis=0)
        dal_ref[...] += sum(gr[4] for gr in grads)
        ddt_ref[...] += sum(gr[5] for gr in grads)
        for h in range(nh):
            ds_ref[h] = ds[h]
        if ns:
            pl.when(step == steps - 1)(s_finish)

    w = nh * HEAD_W
    rev = lambda c: (lambda s: (steps - 1 - s, c))
    vec = pl.BlockSpec((1, HEAD_W), lambda s: (0, 0))
    return pl.pallas_call(
        body, name=name, grid=(steps,),
        in_specs=[pl.BlockSpec((rows, w), rev(0)), pl.BlockSpec((rows, w), rev(1)), pl.BlockSpec((rows, w), rev(2)),
                  pl.BlockSpec((rows, HEAD_W), rev(AB_BA // HEAD_W)), vec, vec,
                  pl.BlockSpec((GDN_STEP, nh, HEAD_W, HEAD_W), lambda s: (steps - 1 - s, 0, 0, 0)),
                  pl.BlockSpec((rows, w), rev(0))] + [_ANY] * ns,
        out_specs=[pl.BlockSpec((rows, w), rev(0)), pl.BlockSpec((rows, w), rev(0)), pl.BlockSpec((rows, w), rev(0)),
                   pl.BlockSpec((rows, HEAD_W), rev(0)), vec, vec] + [_ANY] * ns,
        out_shape=[jax.ShapeDtypeStruct((lp, w), F32)] * 3 + [jax.ShapeDtypeStruct((lp, HEAD_W), F32)]
        + [jax.ShapeDtypeStruct((1, HEAD_W), F32)] * 2 + [jax.ShapeDtypeStruct(s.shape, s.dtype) for s in scatter],
        scratch_shapes=[pltpu.VMEM((nh, HEAD_W, HEAD_W), F32)] + (_scatter_scratch(ns) if ns else []),
        compiler_params=_cp(("arbitrary",), has_side_effects=bool(ns)),
    )(qkv, qkv, qkv, p0, alog_v, dtb_v, states, do, *scatter)


HG_LEVELS = (32, 16, 8, 4, 2, 1)
HG_GROUP = 4
HG_STEP = 2


def _hg_masks():
    import numpy as np
    c = CHUNK
    t = np.arange(c)[:, None]
    j = np.arange(c)[None, :]
    sums = (j <= t).astype(np.float32)
    pairs = [j == t]
    for m in HG_LEVELS:
        p = (t // (2 * m)) * (2 * m)
        r = p + m
        pairs.append((t >= r) & (j < r) & (j >= p))
    pairs = np.concatenate([np.kron(np.eye(HG_GROUP), p) for p in pairs], axis=0).astype(np.float32)
    return jnp.asarray(sums, BF16), jnp.asarray(sums.T, BF16), jnp.asarray(pairs, F32)


def _hg_level_row(b, m):
    c, w = b.shape
    if m >= 8:
        return jnp.concatenate([jnp.broadcast_to(b[p + m:p + m + 1], (2 * m, w)) for p in range(0, c, 2 * m)], axis=0)
    tiles = b.reshape(c // 8, 8, w)
    sub = lax.broadcasted_iota(jnp.int32, (1, 8, 1), 1)
    out = None
    for r0 in range(m, 8, 2 * m):
        cand = jnp.broadcast_to(tiles[:, r0:r0 + 1, :], tiles.shape)
        out = cand if out is None else jnp.where(sub >= r0 - m, cand, out)
    return out.reshape(c, w)


def _split3(x):
    hi = x.astype(BF16)
    r1 = x - hi.astype(F32)
    mid = r1.astype(BF16)
    return hi, mid, (r1 - mid.astype(F32)).astype(BF16)


def _mask_mm_raw(m, x):
    return sum(_dot(m, part, NN) for part in _split3(x))


@jax.custom_vjp
def _mask_mm(m, mt, x):
    return _mask_mm_raw(m, x)


def _mask_mm_fwd(m, mt, x):
    return _mask_mm_raw(m, x), (m, mt)


def _mask_mm_bwd(res, g):
    m, mt = res
    return jnp.zeros_like(m), jnp.zeros_like(mt), _mask_mm_raw(mt, g)


_mask_mm.defvjp(_mask_mm_fwd, _mask_mm_bwd)


def _hg_chunk(qr, fr, ir, lb, states, valid, sums, sums_t, pairs):
    nh = HG_GROUP
    c = qr.shape[0]
    r = nh * c
    fg = lb + (1.0 - lb) * _sigmoid(fr)
    logf = jnp.where(valid, jnp.log(fg), 0.0)
    k = jnp.where(valid, 1.0 - fg, 0.0)
    qs = jnp.where(valid, _silu(qr), 0.0)
    v = jnp.where(valid, ir, 0.0)
    b = _mask_mm(sums, sums_t, logf)
    mask = lambda n: pairs[n * r:(n + 1) * r]
    stack = lambda x: _heads_to_rows(x, nh)
    a = mask(0) * _bdot(stack(qs), stack(k), NT)
    for lvl, m in enumerate(HG_LEVELS):
        d = b - _hg_level_row(b, m)
        a = a + mask(1 + lvl) * _bdot(stack(qs * jnp.exp(jnp.minimum(d, 0.0))),
                                      stack(k * jnp.exp(jnp.minimum(-d, 0.0))), NT)
    av = _bdot(a, stack(v), NN)
    eb = jnp.exp(b)
    qe, kd = qs * eb, k * jnp.exp(b[c - 1:c] - b)
    outs, new_states = [], []
    for h in range(nh):
        cs = slice(h * HEAD_W, (h + 1) * HEAD_W)
        outs.append(_bdot(qe[:, cs], states[h], NT) + av[h * c:(h + 1) * c])
        new_states.append(states[h] * eb[c - 1:c, cs] + _bdot(v[:, cs], kd[:, cs], TN))
    return jnp.concatenate(outs, axis=1), new_states


def _hg_fwd(p1, lb, pad, *, name, gather=None):
    lp = p1.shape[0]
    n = lp // CHUNK
    nh = HG_HEADS
    g_srcs, g_dtypes = gather if gather is not None else ([], [])
    ng_arr = len(g_srcs)

    def body(q_ref, f_ref, i_ref, lb_ref, sums_ref, sums_t_ref, pairs_ref, *rest):
        g_ins, (o_ref, st_ref) = rest[:ng_arr], rest[ng_arr:ng_arr + 2]
        g_outs, s_ref, g_scratch = rest[ng_arr + 2:2 * ng_arr + 2], rest[2 * ng_arr + 2], rest[2 * ng_arr + 3:]
        i = pl.program_id(1)
        if ng_arr:
            g_start, g_forward, g_finish = _gather_phases(g_ins, g_outs, g_scratch[:ng_arr], *g_scratch[ng_arr:],
                                                          g_dtypes)
            last_group = pl.program_id(0) == ngrp - 1
            pl.when((pl.program_id(0) == 0) & (i == 0))(g_start)
            pl.when(last_group & (i == 0))(g_forward)

        @pl.when(i == 0)
        def _():
            s_ref[...] = jnp.zeros_like(s_ref)

        s = s_ref[...]
        s = [s[h] for h in range(grp)]
        q, f, iv, lbv = q_ref[...], f_ref[...], i_ref[...], lb_ref[...]
        masks_v = (sums_ref[...], sums_t_ref[...], pairs_ref[...])
        outs = []
        for c in range(HG_STEP):
            sl = slice(c * CHUNK, (c + 1) * CHUNK)
            valid = (i * rows + c * CHUNK + lax.broadcasted_iota(jnp.int32, (CHUNK, 1), 0)) >= pad
            for h in range(grp):
                st_ref[h, c] = s[h]
            o, s = _hg_chunk(q[sl], f[sl], iv[sl], lbv, s, valid, *masks_v)
            outs.append(o)
        o_ref[...] = jnp.concatenate(outs, axis=0)
        for h in range(grp):
            s_ref[h] = s[h]
        if ng_arr:
            pl.when(last_group & (i == steps - 1))(g_finish)

    masks = _hg_masks()
    grp, ngrp, gw = HG_GROUP, nh // HG_GROUP, HG_GROUP * HEAD_W
    assert n % HG_STEP == 0
    steps, rows = n // HG_STEP, HG_STEP * CHUNK
    blk = lambda off: pl.BlockSpec((rows, gw), lambda h, i: (i, off + h))
    const = lambda a: pl.BlockSpec(a.shape, lambda h, i: (0, 0))
    return pl.pallas_call(
        body, name=name, grid=(ngrp, steps),
        in_specs=[blk(0), blk(ngrp), blk(2 * ngrp), pl.BlockSpec((1, gw), lambda h, i: (0, h))]
        + [const(a) for a in masks] + [pl.BlockSpec(memory_space=pltpu.VMEM)] * ng_arr,
        out_specs=[blk(0), pl.BlockSpec((grp, HG_STEP, HEAD_W, HEAD_W), lambda h, i: (h, i, 0, 0))] + [_ANY] * ng_arr,
        out_shape=[jax.ShapeDtypeStruct((lp, nh * HEAD_W), F32), jax.ShapeDtypeStruct((nh, n, HEAD_W, HEAD_W), F32)]
        + _gather_out_shapes(g_srcs, g_dtypes),
        scratch_shapes=[pltpu.VMEM((grp, HEAD_W, HEAD_W), F32)] + (_gather_scratch(g_srcs, g_dtypes) if ng_arr else []),
        compiler_params=_cp(("arbitrary", "arbitrary"), has_side_effects=bool(ng_arr)),
    )(p1, p1, p1, lb, *masks, *g_srcs)


def _hg_bwd(p1, lb, states, do, pad, *, name, scatter=()):
    lp = p1.shape[0]
    n = lp // CHUNK
    nh = HG_HEADS
    ns = len(scatter)

    def body(q_ref, f_ref, i_ref, lb_ref, st_ref, do_ref, sums_ref, sums_t_ref, pairs_ref, *rest):
        s_ins, (dq_ref, df_ref, di_ref, dlb_ref) = rest[:ns], rest[ns:ns + 4]
        s_outs, ds_ref, s_sems = rest[ns + 4:2 * ns + 4], rest[2 * ns + 4], rest[2 * ns + 5:]
        step = pl.program_id(1)
        i = steps - 1 - step
        if ns:
            s_start, s_finish = _scatter_phases(s_ins, s_outs, *s_sems)
            pl.when((pl.program_id(0) == 0) & (step == 0))(s_start)

        @pl.when(step == 0)
        def _():
            ds_ref[...] = jnp.zeros_like(ds_ref)
            dlb_ref[...] = jnp.zeros_like(dlb_ref)

        q, f, iv, lbv, st, do, dst = q_ref[...], f_ref[...], i_ref[...], lb_ref[...], st_ref[...], do_ref[...], ds_ref[...]
        masks_v = dict(sums=sums_ref[...], sums_t=sums_t_ref[...], pairs=pairs_ref[...])
        vjps = []
        for c in range(HG_STEP):
            sl = slice(c * CHUNK, (c + 1) * CHUNK)
            valid = (i * rows + c * CHUNK + lax.broadcasted_iota(jnp.int32, (CHUNK, 1), 0)) >= pad
            fn = functools.partial(_hg_chunk, valid=valid, **masks_v)
            vjps.append(jax.vjp(fn, q[sl], f[sl], iv[sl], lbv, [st[h, c] for h in range(grp)])[1])
        ds = [dst[h] for h in range(grp)]
        grads = [None] * HG_STEP
        for c in reversed(range(HG_STEP)):
            grads[c] = vjps[c]((do[c * CHUNK:(c + 1) * CHUNK], ds))
            ds = grads[c][4]
        for j, ref in enumerate((dq_ref, df_ref, di_ref)):
            ref[...] = jnp.concatenate([gr[j] for gr in grads], axis=0)
        dlb_ref[...] += sum(gr[3] for gr in grads)
        for h in range(grp):
            ds_ref[h] = ds[h]
        if ns:
            pl.when((pl.program_id(0) == ngrp - 1) & (step == steps - 1))(s_finish)

    masks = _hg_masks()
    grp, ngrp, gw = HG_GROUP, nh // HG_GROUP, HG_GROUP * HEAD_W
    assert n % HG_STEP == 0
    steps, rows = n // HG_STEP, HG_STEP * CHUNK
    blk = lambda off: pl.BlockSpec((rows, gw), lambda h, s: (steps - 1 - s, off + h))
    const = lambda a: pl.BlockSpec(a.shape, lambda h, s: (0, 0))
    w = nh * HEAD_W
    return pl.pallas_call(
        body, name=name, grid=(ngrp, steps),
        in_specs=[blk(0), blk(ngrp), blk(2 * ngrp), pl.BlockSpec((1, gw), lambda h, s: (0, h)),
                  pl.BlockSpec((grp, HG_STEP, HEAD_W, HEAD_W), lambda h, s: (h, steps - 1 - s, 0, 0)), blk(0)]
        + [const(a) for a in masks] + [_ANY] * ns,
        out_specs=[blk(0), blk(0), blk(0), pl.BlockSpec((1, gw), lambda h, s: (0, h))] + [_ANY] * ns,
        out_shape=[jax.ShapeDtypeStruct((lp, w), F32)] * 3 + [jax.ShapeDtypeStruct((1, w), F32)]
        + [jax.ShapeDtypeStruct(s.shape, s.dtype) for s in scatter],
        scratch_shapes=[pltpu.VMEM((grp, HEAD_W, HEAD_W), F32)] + (_scatter_scratch(ns) if ns else []),
        compiler_params=_cp(("arbitrary", "arbitrary"), has_side_effects=bool(ns)),
    )(p1, p1, p1, lb, states, do, *masks, *scatter)


SB_GROUP = 4
SB_FAR = -110.0


def _sb_cat(kind, first_key=0):
    r = lax.broadcasted_iota(jnp.int32, (SB_BLOCK, 2 * SB_BLOCK), 0)
    c = lax.broadcasted_iota(jnp.int32, (SB_BLOCK, 2 * SB_BLOCK), 1)
    tri = {"after": c < r, "incl": r <= c, "before": r < c}[kind]
    m = ((c >= SB_BLOCK) | tri) & (r >= first_key)
    return jnp.where(m, 1.0, 0.0).astype(BF16)


def _sb_cumsum(x, cat):
    return _dot(x.astype(BF16), cat, NN)


def _sb_logsig(z):
    e = jnp.exp(-jnp.abs(z))
    lse = jnp.where(e < 1e-4, e, jnp.log(1.0 + e))
    lsz = jnp.minimum(z, 0.0) - lse
    return lsz, lsz - z, e


def _sb_stack(x, scale=None):
    lane = lax.broadcasted_iota(jnp.int32, (1, HEAD_W), 1)
    if scale is not None:
        x = x * scale
    return jnp.concatenate([jnp.where(lane < SB_DH, x, 0.0), jnp.where(lane >= SB_DH, x, 0.0)], axis=0).astype(BF16)


def _sb_unstack(x):
    lane = lax.broadcasted_iota(jnp.int32, (1, HEAD_W), 1)
    return jnp.where(lane < SB_DH, x[:SB_BLOCK], x[SB_BLOCK:])


def _sb_fwd(p0, pad, *, name, gather=None):
    lp = p0.shape[0]
    nb = lp // SB_BLOCK
    npair = SB_HEADS // 2
    blk0 = AB_SB // HEAD_W
    scale = SB_DH ** -0.5
    gw = SB_GROUP * SB_BLOCK
    assert pad < SB_BLOCK
    g_srcs, g_dtypes = gather if gather is not None else ([], [])
    ng_arr = len(g_srcs)

    def body(q_ref, k_ref, v_ref, *rest):
        g_ins, (o_ref, tot_ref, nproc_ref) = rest[:ng_arr], rest[ng_arr:ng_arr + 3]
        g_outs, g_scratch = rest[ng_arr + 3:2 * ng_arr + 3], rest[2 * ng_arr + 3:]
        first_step = (pl.program_id(0) == 0) & (pl.program_id(1) == 0)
        last_pair = pl.program_id(0) == npair - 1
        if ng_arr:
            g_start, g_forward, g_finish = _gather_phases(g_ins, g_outs, g_scratch[:ng_arr], *g_scratch[ng_arr:],
                                                          g_dtypes)
            pl.when(first_step)(g_start)
            pl.when(last_pair & (pl.program_id(1) == 0))(g_forward)
        i = pl.program_id(1)
        qs = _sb_stack(q_ref[...], scale)
        qpos = i * SB_BLOCK + lax.broadcasted_iota(jnp.int32, (SB_BLOCK, 1), 0)
        qpos = jnp.concatenate([qpos, qpos], axis=0)
        cat = _sb_cat("after")
        cat0 = _sb_cat("after", pad)
        ng = i // SB_GROUP

        def group(off, nblk, first_cat, allowed, carry):
            acc, run = carry
            kg = k_ref[pl.ds(off, nblk * SB_BLOCK), :].astype(BF16)
            vg = v_ref[pl.ds(off, nblk * SB_BLOCK), :].astype(BF16)
            lsz, l1m, _ = _sb_logsig(_dot(qs, kg, NT))
            if allowed is not None:
                l1m = jnp.where(allowed, l1m, 0.0)
            args = [None] * nblk
            for g in reversed(range(nblk)):
                sl = slice(g * SB_BLOCK, (g + 1) * SB_BLOCK)
                al = _sb_cumsum(l1m[:, sl], first_cat if g == 0 else cat)
                args[g] = lsz[:, sl] + al[:, :SB_BLOCK] + run
                run = run + al[:, SB_BLOCK:]
            wgt = jnp.exp(jnp.concatenate(args, axis=1))
            if allowed is not None:
                wgt = jnp.where(allowed, wgt, 0.0)
            return acc + _dot(wgt.astype(BF16), vg, NN), run

        def below(t, carry):
            gi = ng - 1 - t
            return group(pl.multiple_of(gi * gw, gw), SB_GROUP, jnp.where(gi == 0, cat0, cat), None, carry)

        top = ng * gw

        def top_group(nblk, carry):
            off = pl.multiple_of(jnp.minimum(top, lp - nblk * SB_BLOCK), SB_BLOCK)
            kpos = off + lax.broadcasted_iota(jnp.int32, (1, nblk * SB_BLOCK), 1)
            return group(off, nblk, cat, (kpos < qpos) & (kpos >= pad) & (kpos >= top), carry)

        zero = (jnp.zeros((2 * SB_BLOCK, HEAD_W), F32), jnp.zeros((2 * SB_BLOCK, HEAD_W), F32))
        carry = lax.cond(i - ng * SB_GROUP < SB_GROUP // 2, functools.partial(top_group, SB_GROUP // 2),
                         functools.partial(top_group, SB_GROUP), zero)
        used, acc, run = lax.while_loop(lambda s: (s[0] < ng) & (jnp.max(s[2]) > SB_FAR),
                                        lambda s: (s[0] + 1, *below(s[0], (s[1], s[2]))), (jnp.int32(0), *carry))
        o_ref[...] = _sb_unstack(acc)
        tot_ref[...] = _sb_unstack(run)
        nproc_ref[pl.program_id(0), i] = used.astype(F32)
        if ng_arr:
            pl.when(last_pair & (pl.program_id(1) == nb - 1))(g_finish)

    full = lambda c0: pl.BlockSpec((lp, HEAD_W), lambda p, i: (0, c0 + p))
    out = pl.BlockSpec((SB_BLOCK, HEAD_W), lambda p, i: (i, p))
    return pl.pallas_call(
        body, name=name, grid=(npair, nb),
        in_specs=[pl.BlockSpec((SB_BLOCK, HEAD_W), lambda p, i: (i, blk0 + p)), full(blk0 + npair), full(blk0 + 2 * npair)]
        + [pl.BlockSpec(memory_space=pltpu.VMEM)] * ng_arr,
        out_specs=[out, out, pl.BlockSpec(memory_space=pltpu.SMEM)] + [_ANY] * ng_arr,
        out_shape=[jax.ShapeDtypeStruct((lp, npair * HEAD_W), F32)] * 2 + [jax.ShapeDtypeStruct((npair, nb), F32)]
        + _gather_out_shapes(g_srcs, g_dtypes),
        scratch_shapes=_gather_scratch(g_srcs, g_dtypes) if ng_arr else [],
        compiler_params=_cp(("arbitrary", "arbitrary"), has_side_effects=bool(ng_arr)),
    )(p0, p0, p0, *g_srcs)


def _sb_bwd(p0, tot, nproc, dsrc, d_blk0, pad, *, name, scatter=()):
    lp = p0.shape[0]
    nb = lp // SB_BLOCK
    npair = SB_HEADS // 2
    blk0 = AB_SB // HEAD_W
    scale = SB_DH ** -0.5
    gw = SB_GROUP * SB_BLOCK
    assert pad < SB_BLOCK
    ns = len(scatter)

    def body(q_ref, k_ref, v_ref, tot_ref, nproc_ref, do_ref, *rest):
        s_ins, (dq_ref, dkt_ref, dvt_ref) = rest[:ns], rest[ns:ns + 3]
        s_outs, s_sems = rest[ns + 3:2 * ns + 3], rest[2 * ns + 3:]
        if ns:
            s_start, s_finish = _scatter_phases(s_ins, s_outs, *s_sems)
            pl.when((pl.program_id(0) == 0) & (pl.program_id(1) == 0))(s_start)
        i = pl.program_id(1)

        @pl.when(i == 0)
        def _():
            dkt_ref[...] = jnp.zeros_like(dkt_ref)
            dvt_ref[...] = jnp.zeros_like(dvt_ref)

        qs = _sb_stack(q_ref[...], scale)
        dos = _sb_stack(do_ref[...])
        qst, dost = qs.T, dos.T
        totv = tot_ref[...]
        ones = jnp.ones((1, HEAD_W), F32)
        tots = jnp.concatenate([totv[:, 0:1] * ones, totv[:, SB_DH:SB_DH + 1] * ones], axis=0)
        qpos = i * SB_BLOCK + lax.broadcasted_iota(jnp.int32, (SB_BLOCK, 1), 0)
        qpos = jnp.concatenate([qpos, qpos], axis=0)
        incl, incl0 = _sb_cat("incl"), _sb_cat("incl", pad)
        before = _sb_cat("before")
        ng = i // SB_GROUP
        used = jnp.clip(nproc_ref[pl.program_id(0), i].astype(jnp.int32), 0, ng)

        def dscore(z, e, ev, dl1m):
            r = 1.0 / (1.0 + e)
            sg = jnp.where(z >= 0, r, e * r)
            return ev * (1.0 - sg) - dl1m * sg

        def group(off, nblk, first_incl, allowed, carry):
            dq, prun, erun = carry
            width = nblk * SB_BLOCK
            kg = k_ref[pl.ds(off, width), :].astype(BF16)
            vg = v_ref[pl.ds(off, width), :].astype(BF16)
            z = _dot(qs, kg, NT)
            lsz, l1m, e = _sb_logsig(z)
            if allowed is not None:
                l1m = jnp.where(allowed, l1m, 0.0)
            dwgt = _dot(dos, vg, NT)
            dzs = [None] * nblk
            wgts = [None] * nblk
            for g in range(nblk):
                sl = slice(g * SB_BLOCK, (g + 1) * SB_BLOCK)
                al = _sb_cumsum(l1m[:, sl], first_incl if g == 0 else incl)
                wgt = jnp.exp(jnp.minimum(lsz[:, sl] + (tots - prun - al[:, :SB_BLOCK]), 0.0))
                if allowed is not None:
                    wgt = jnp.where(allowed[:, sl], wgt, 0.0)
                prun = prun + al[:, SB_BLOCK:]
                ev = wgt * dwgt[:, sl]
                el = _sb_cumsum(ev, before)
                dzs[g] = dscore(z[:, sl], e[:, sl], ev, erun + el[:, :SB_BLOCK])
                erun = erun + el[:, SB_BLOCK:]
                wgts[g] = wgt
            dz = jnp.concatenate(dzs, axis=1)
            if allowed is not None:
                dz = jnp.where(allowed, dz, 0.0)
            dz = dz.astype(BF16)
            wg = jnp.concatenate(wgts, axis=1).astype(BF16)
            dkt_ref[:, pl.ds(off, width)] += _dot(qst, dz, NN)
            dvt_ref[:, pl.ds(off, width)] += _dot(dost, wg, NN)
            return dq + _dot(dz, kg, NN), prun, erun

        def below(gi, carry):
            return group(pl.multiple_of(gi * gw, gw), SB_GROUP, jnp.where(gi == 0, incl0, incl), None, carry)

        zero = tuple(jnp.zeros((2 * SB_BLOCK, HEAD_W), F32) for _ in range(3))
        carry = lax.fori_loop(ng - used, ng, below, zero)
        top = ng * gw

        def top_group(nblk, carry):
            off = pl.multiple_of(jnp.minimum(top, lp - nblk * SB_BLOCK), SB_BLOCK)
            kpos = off + lax.broadcasted_iota(jnp.int32, (1, nblk * SB_BLOCK), 1)
            return group(off, nblk, incl, (kpos < qpos) & (kpos >= pad) & (kpos >= top), carry)

        dq, _, _ = lax.cond(i - ng * SB_GROUP < SB_GROUP // 2, functools.partial(top_group, SB_GROUP // 2),
                            functools.partial(top_group, SB_GROUP), carry)
        dq_ref[...] = _sb_unstack(dq) * scale
        if ns:
            pl.when((pl.program_id(0) == npair - 1) & (pl.program_id(1) == nb - 1))(s_finish)

    full = lambda c0: pl.BlockSpec((lp, HEAD_W), lambda p, i: (0, c0 + p))
    qb = lambda c0: pl.BlockSpec((SB_BLOCK, HEAD_W), lambda p, i: (i, c0 + p))
    tr = pl.BlockSpec((HEAD_W, lp), lambda p, i: (p, 0))
    return pl.pallas_call(
        body, name=name, grid=(npair, nb),
        in_specs=[qb(blk0), full(blk0 + npair), full(blk0 + 2 * npair), qb(0), pl.BlockSpec(memory_space=pltpu.SMEM),
                  qb(d_blk0)] + [_ANY] * ns,
        out_specs=[qb(0), tr, tr] + [_ANY] * ns,
        out_shape=[jax.ShapeDtypeStruct((lp, npair * HEAD_W), F32)]
        + [jax.ShapeDtypeStruct((npair * HEAD_W, lp), F32)] * 2
        + [jax.ShapeDtypeStruct(s.shape, s.dtype) for s in scatter],
        scratch_shapes=_scatter_scratch(ns) if ns else [],
        compiler_params=_cp(("arbitrary", "arbitrary"), has_side_effects=bool(ns)),
    )(p0, p0, p0, tot, nproc, dsrc, *scatter)


def _local_step(h0, target, pad, wts, hooks=None):
    lp = h0.shape[0]
    tm = _row_tile(lp, 1056)
    tkl = tm
    tml = _row_tile(lp, 528)
    d = D_MODEL
    mm = _mm
    mmw = functools.partial(_mm, out_dtype=BF16)
    g = {}

    h0_b = h0.astype(BF16)
    p0 = mm(h0_b, wts["w_ab"], "NN", tm=tm, tn=768, tk=d, name="l0_in_proj")
    ob, sb_tot, sb_used, *gathered = _sb_fwd(p0, pad, name="sb_fwd", gather=hooks["gather_a"] if hooks else None)
    if hooks:
        wts = {**wts, **hooks["weights_a"](gathered)}
    qkv = _gdn_pre_fwd(p0, wts["conv_w"], pad, name="gdn_pre_fwd")
    oa_raw, gdn_states, *gathered = _gdn_fwd(qkv, p0, wts["alog_v"], wts["dtb_v"], pad, name="gdn_fwd",
                                             gather=hooks["gather_b"] if hooks else None)
    if hooks:
        wts = {**wts, **hooks["weights_b"](gathered)}
    rows = lambda a, n: a.reshape(N_DEV, n // N_DEV, d)
    parts = g["parts"] = {}
    oab = _gate_fwd(oa_raw, p0, AB_Z // HEAD_W, wts["ab_gn"], ob, heads=GDN_HEADS, name="gdn_gate_fwd")
    ln = lambda kind, layer: (wts[f"ln_{kind}_g"][layer], wts[f"ln_{kind}_b"][layer])
    pre_mix0, h0a, h0a_b = mm(oab, wts["w_out0"], "NN", tm=tml, tn=d, tk=d, epi="ln", c=h0, scale=DN_ALPHA,
                              ln=ln("mix", 0), name="l0_out_proj")
    u0, act0 = mm(h0a_b, wts["w1"][0], "NN", tm=tm, tn=512, tk=d, b_dev=True, epi="relu2_copy", name="mlp0_up")
    pre_ffn0, h0b, h0b_b = mm(act0, wts["w2"][0], "NN", tm=tml, tn=d, tk=d, epi="ln", c=h0a, scale=DN_ALPHA,
                              ln=ln("ffn", 0), name="mlp0_down")
    p1 = mm(h0b_b, wts["w_c"], "NN", tm=tm, tn=512, tk=d, b_dev=True, name="l1_in_proj")
    oc_raw, hg_states, *gathered = _hg_fwd(p1, wts["lb"], pad, name="hg_fwd",
                                           gather=hooks["gather_c"] if hooks else None)
    if hooks:
        third = hooks["weights_c"](gathered)
        wts = {**wts, "w1": wts["w1"] + third["w1"], "w2": wts["w2"] + third["w2"]}
    oc = _gate_fwd(oc_raw, p1, 3 * HG_HEADS, wts["c_gn"], oc_raw, heads=HG_HEADS, name="hg_gate_fwd")
    pre_mix1, h1a, h1a_b = mm(oc, wts["w_out1"], "NN", tm=tml, tn=d, tk=d, epi="ln", c=h0b, scale=DN_ALPHA,
                              ln=ln("mix", 1), name="l1_out_proj")
    u1, act1 = mm(h1a_b, wts["w1"][1], "NN", tm=tm, tn=512, tk=d, b_dev=True, epi="relu2_copy", name="mlp1_up")
    pre_ffn1, h1b, _ = mm(act1, wts["w2"][1], "NN", tm=tml, tn=d, tk=d, epi="ln", c=h1a, scale=DN_ALPHA,
                          ln=ln("ffn", 1), name="mlp1_down")
    dy, loss_vec = _loss_head(h1b, target, name="loss_head")

    def mlp_bwd(layer, h_in_b, u, act, dpre, dpre_b, pre_mix):
        du = mm(dpre_b, wts["w2"][layer], "NT", tm=tm, tn=1024, tk=d, epi="relu2grad", c=u, out_dtype=BF16,
                name=f"mlp{layer}_d_hidden")
        dw2 = mmw(act, dpre_b, "TN", tm=1024, tn=1024, tk=tkl, name=f"mlp{layer}_dw2")
        dw1 = mmw(h_in_b, du, "TN", tm=1024, tn=512, tk=tkl, out_dev=True, name=f"mlp{layer}_dw1")
        return (*mm(du, k_major(wts["w1"][layer]), "NT", tm=tml, tn=1024, tk=2048, epi="ln_bwd", c=dpre, scale=DN_ALPHA,
                    ln=(pre_mix, wts["ln_mix_g"][layer]), name=f"mlp{layer}_d_in"), dw1, dw2)

    k_major = lambda wd: wd.transpose(1, 0, 2).reshape(wd.shape[1], -1)

    ln_ffn_dg, ln_ffn_db, ln_mix_dg, ln_mix_db, dw1s, dw2s = ([None, None] for _ in range(6))
    dpre, dpre_b, ln_ffn_dg[1], ln_ffn_db[1] = _ln_bwd(pre_ffn1, wts["ln_ffn_g"][1], dy, name="ln_ffn1_bwd")
    dpre, dpre_b, ln_mix_dg[1], ln_mix_db[1], dw1s[1], dw2s[1] = mlp_bwd(1, h1a_b, u1, act1, dpre, dpre_b, pre_mix1)
    g["c_w_out"] = mmw(oc, dpre_b, "TN", tm=1024, tn=1024, tk=tkl, name="l1_dw_out")
    doc = mm(dpre_b, wts["w_out1"], "NT", tm=tm, tn=1024, tk=d, name="l1_d_gate")
    doc_raw, dz1, g["c_gn"] = _gate_bwd(oc_raw, p1, 3 * HG_HEADS, wts["c_gn"], doc, heads=HG_HEADS, name="hg_gate_bwd")
    ready = [dw1s[1], rows(dw2s[1], D_FF), rows(g["c_w_out"], d)] if hooks else ()
    dq1, df1, di1, g["lb"], *got = _hg_bwd(p1, wts["lb"], hg_states, doc_raw, pad, name="hg_bwd", scatter=ready)
    parts.update(zip(("mlp_w1_1", "mlp_w2_1", "c_w_out"), got))
    dp1 = jnp.concatenate([dq1, df1, di1, dz1], axis=1).astype(BF16)
    g["c_w_in"] = mmw(h0b_b, dp1, "TN", tm=1024, tn=512, tk=tkl, out_dev=True, name="l1_dw_in")
    dpre, dpre_b, ln_ffn_dg[0], ln_ffn_db[0] = mm(
        dp1, k_major(wts["w_c"]), "NT", tm=tml, tn=1024, tk=2048, epi="ln_bwd", c=dpre, scale=DN_ALPHA,
        ln=(pre_ffn0, wts["ln_ffn_g"][0]), name="l1_d_in")
    dpre, dpre_b, ln_mix_dg[0], ln_mix_db[0], dw1s[0], dw2s[0] = mlp_bwd(0, h0a_b, u0, act0, dpre, dpre_b, pre_mix0)
    g["ab_w_out"] = mmw(oab, dpre_b, "TN", tm=1024, tn=1024, tk=tkl, name="l0_dw_out")
    doab = mm(dpre_b, wts["w_out0"], "NT", tm=tm, tn=1024, tk=d, name="l0_d_gate")
    doa_raw, dz0, g["ab_gn"] = _gate_bwd(oa_raw, p0, AB_Z // HEAD_W, wts["ab_gn"], doab, heads=GDN_HEADS,
                                         name="gdn_gate_bwd")
    ready = [g["c_w_in"]] if hooks else ()
    dqb, dkb_t, dvb_t, *got = _sb_bwd(p0, sb_tot, sb_used, doab, GDN_HEADS, pad, name="sb_bwd", scatter=ready)
    parts.update(zip(("c_w_in",), got))
    dkb, dvb = dkb_t.T, dvb_t.T
    ready = [dw1s[0], rows(dw2s[0], D_FF), rows(g["ab_w_out"], d)] if hooks else ()
    dqn, dkn, dvn, dba, g["alog_v"], g["dtb_v"], *got = _gdn_bwd(qkv, p0, wts["alog_v"], wts["dtb_v"], gdn_states,
                                                                 doa_raw, pad, name="gdn_bwd", scatter=ready)
    parts.update(zip(("mlp_w1_0", "mlp_w2_0", "ab_w_out"), got))
    dconv_in, g["conv_w"] = _gdn_pre_bwd(p0, wts["conv_w"], jnp.concatenate([dqn, dkn, dvn], axis=1), pad,
                                         name="gdn_pre_bwd")
    dp0 = jnp.concatenate([dconv_in, dz0, dqb, dkb, dvb, dba, jnp.zeros((lp, AB_CAT - AB_BA - HEAD_W), F32)],
                          axis=1).astype(BF16)
    g["w_ab"] = mmw(h0_b, dp0, "TN", tm=1024, tn=768, tk=tkl, name="l0_dw_in")
    last = ()
    if hooks:
        gab, ba0 = g["w_ab"], AB_Z + GDN_HEADS * HEAD_W
        gab = jnp.concatenate([gab[:, :ba0], gab[:, AB_BA:AB_BA + 2 * GDN_HEADS], gab[:, ba0:AB_BA]], axis=1)
        last = [gab.reshape(d, N_DEV, AB_IN // N_DEV).transpose(1, 0, 2)]
    res = mm(dp0, wts["w_ab"], "NT", tm=tm, tn=1024, tk=1920, epi="add", c=dpre, scale=DN_ALPHA, scatter=last,
             name="l0_d_in")
    dh0 = res[0] if last else res
    parts.update(zip(("ab_w_in",), res[1:] if last else ()))

    g["w1"], g["w2"] = dw1s, dw2s
    g["ln_mix_g"] = jnp.concatenate(ln_mix_dg, axis=0)
    g["ln_mix_b"] = jnp.concatenate(ln_mix_db, axis=0)
    g["ln_ffn_g"] = jnp.concatenate(ln_ffn_dg, axis=0)
    g["ln_ffn_b"] = jnp.concatenate(ln_ffn_db, axis=0)
    return loss_vec, dh0, g


N_CHIP = N_DEV // 2


def _place():
    x, y, c = lax.axis_index("x"), lax.axis_index("y"), lax.axis_index("c")
    return x, y, c, 2 * x + y


def _chip_dev(chip, core):
    return (chip // 2, chip % 2, core)


def _remote(src, dst, send_sem, recv_sem, dev):
    return pltpu.make_async_remote_copy(src_ref=src, dst_ref=dst, send_sem=send_sem, recv_sem=recv_sem,
                                        device_id=dev, device_id_type=pl.DeviceIdType.MESH)


_ANY = pl.BlockSpec(memory_space=pl.ANY)


def _gather(srcs, dtypes, *, name):
    n = len(srcs)

    def body(*refs):
        start, forward, finish = _gather_phases(refs[:n], refs[n:2 * n], refs[2 * n:3 * n], *refs[3 * n:], dtypes)
        start()
        forward()
        finish()

    return pl.pallas_call(
        body, name=name, in_specs=[pl.BlockSpec(memory_space=pltpu.VMEM)] * n, out_specs=[_ANY] * n,
        out_shape=_gather_out_shapes(srcs, dtypes), scratch_shapes=_gather_scratch(srcs, dtypes),
        compiler_params=_cp(has_side_effects=True),
    )(*srcs)


def _gather_out_shapes(srcs, dtypes):
    return [jax.ShapeDtypeStruct((N_DEV, *s.shape), dt) for s, dt in zip(srcs, dtypes)]


def _gather_scratch(srcs, dtypes):
    n = len(srcs)
    return [pltpu.VMEM(s.shape, dt) for s, dt in zip(srcs, dtypes)] + [
        pltpu.SemaphoreType.DMA((n, 2 * N_CHIP - 1)), pltpu.SemaphoreType.DMA((n, 2 * N_CHIP - 1)),
        pltpu.SemaphoreType.DMA((n,))]


def _gather_phases(ins, outs, stages, send_sems, recv_sems, local_sems, dtypes):
    n = len(ins)
    x, y, c, chip = _place()
    me = 2 * chip + c
    sibling = (x, y, 1 - c)

    def own(i):
        cps = [_remote(stages[i], outs[i].at[me], send_sems.at[i, 0], recv_sems.at[i, 0], sibling)]
        for j in range(1, N_CHIP):
            cps.append(_remote(stages[i], outs[i].at[me], send_sems.at[i, j], recv_sems.at[i, j],
                               _chip_dev(jnp.bitwise_xor(chip, j), c)))
        return cps

    def local(i):
        return pltpu.make_async_copy(stages[i], outs[i].at[me], local_sems.at[i])

    def passed_on(i, j):
        slot = outs[i].at[2 * jnp.bitwise_xor(chip, j) + c]
        return _remote(slot, slot, send_sems.at[i, N_CHIP - 1 + j], recv_sems.at[i, N_CHIP - 1 + j], sibling)

    def start():
        for i in range(n):
            stages[i][...] = ins[i][...].astype(dtypes[i])
            local(i).start()
            for cp in own(i):
                cp.start()

    def forward():
        for i in range(n):
            for j in range(1, N_CHIP):
                own(i)[j].wait_recv()
                passed_on(i, j).start()

    def finish():
        for i in range(n):
            own(i)[0].wait_recv()
            for j in range(1, N_CHIP):
                passed_on(i, j).wait_recv()
        for i in range(n):
            for cp in own(i):
                cp.wait_send()
            for j in range(1, N_CHIP):
                passed_on(i, j).wait_send()
            local(i).wait()

    return start, forward, finish


def _scatter_scratch(n):
    return [pltpu.SemaphoreType.DMA((n, N_DEV - 1)), pltpu.SemaphoreType.DMA((n, N_DEV - 1)),
            pltpu.SemaphoreType.DMA((n,))]


def _scatter_phases(ins, outs, send_sems, recv_sems, local_sems):
    n = len(ins)
    _, _, c, chip = _place()
    me = 2 * chip + c

    def copies():
        cps = []
        for i in range(n):
            cps.append(pltpu.make_async_copy(ins[i].at[me], outs[i].at[me], local_sems.at[i]))
            for k in range(1, N_DEV):
                peer = jnp.bitwise_xor(me, k)
                cps.append(_remote(ins[i].at[peer], outs[i].at[me], send_sems.at[i, k - 1], recv_sems.at[i, k - 1],
                                   _chip_dev(peer // 2, peer % 2)))
        return cps

    def start():
        for cp in copies():
            cp.start()

    def finish():
        for cp in copies():
            cp.wait()

    return start, finish


def _adamw(w, parts, m, v, *, name):
    r, c = w.shape
    s = parts.shape[0]
    tm = _row_tile(r, 128) if r % 8 == 0 else r
    c1 = 1.0 - ADAM_B1 ** ADAM_STEP
    c2 = 1.0 - ADAM_B2 ** ADAM_STEP

    def body(w_ref, p_ref, m_ref, v_ref, g_ref, d_ref, m2_ref, v2_ref):
        g = p_ref[0].astype(F32)
        for j in range(1, s):
            g = g + p_ref[j].astype(F32)
        m2 = ADAM_B1 * m_ref[...] + (1.0 - ADAM_B1) * g
        v2 = ADAM_B2 * v_ref[...] + (1.0 - ADAM_B2) * jnp.square(g)
        g_ref[...] = g
        m2_ref[...] = m2
        v2_ref[...] = v2
        d_ref[...] = -ADAM_LR * ((m2 / c1) / (jnp.sqrt(v2 / c2) + ADAM_EPS) + ADAM_WD * w_ref[...])

    blk = pl.BlockSpec((tm, c), lambda i: (i, 0))
    return pl.pallas_call(
        body, name=name, grid=(r // tm,),
        in_specs=[blk, pl.BlockSpec((s, tm, c), lambda i: (0, i, 0)), blk, blk], out_specs=[blk] * 4,
        out_shape=[jax.ShapeDtypeStruct((r, c), F32)] * 4, compiler_params=_cp(("parallel",)),
    )(w, parts, m, v)


_WEIGHTS = ("meta_tokens", "ab_w_in", "ab_conv_w", "ab_a_log", "ab_dt_bias", "ab_gnorm_g", "ab_w_out", "c_w_in",
            "c_lb_raw", "c_gnorm_g", "c_w_out", "ln_mix_g", "ln_mix_b", "mlp_w1", "mlp_w2", "ln_ffn_g", "ln_ffn_b")
_PACK_ROWS = (("ln_mix_g", 0), ("ln_mix_b", 2), ("ln_ffn_g", 4), ("ln_ffn_b", 6), ("c_lb_raw", 8))
_PACK_MISC_ROW = 10
_PACK_MISC = (("ab_gnorm_g", 0, 128), ("c_gnorm_g", 128, 128), ("ab_a_log", 256, GDN_HEADS), ("ab_dt_bias", 260, GDN_HEADS))
_PACK_N = 16
_SMALL_META = 16
_SMALL_CONV = 32
_SMALL_N = 40


def _pack_replicated(p):
    rows = jnp.zeros((_PACK_N, D_MODEL), F32)
    for name, r0 in _PACK_ROWS:
        rows = rows.at[r0:r0 + 2].set(p[name])
    for name, c0, width in _PACK_MISC:
        rows = rows.at[_PACK_MISC_ROW, c0:c0 + width].set(p[name].reshape(width))
    return rows


def _unpack_replicated(rows, like):
    out = {}
    for name, r0 in _PACK_ROWS:
        out[name] = rows[r0:r0 + 2]
    for name, c0, width in _PACK_MISC:
        out[name] = rows[_PACK_MISC_ROW, c0:c0 + width].reshape(like[name].shape)
    return out


def _lower_bound(c_lb_raw):
    lb_all = jnp.cumsum(jax.nn.softmax(c_lb_raw.astype(F32), axis=0), axis=0)
    return (lb_all - lb_all[0:1])[1].reshape(1, -1)


def kernel(x, meta_tokens, ab_w_in, ab_conv_w, ab_a_log, ab_dt_bias, ab_gnorm_g, ab_w_out, c_w_in, c_lb_raw, c_gnorm_g, c_w_out, ln_mix_g, ln_mix_b, mlp_w1, mlp_w2, ln_ffn_g, ln_ffn_b, loss_target, m_meta_tokens, m_ab_w_in, m_ab_conv_w, m_ab_a_log, m_ab_dt_bias, m_ab_gnorm_g, m_ab_w_out, m_c_w_in, m_c_lb_raw, m_c_gnorm_g, m_c_w_out, m_ln_mix_g, m_ln_mix_b, m_mlp_w1, m_mlp_w2, m_ln_ffn_g, m_ln_ffn_b, v_meta_tokens, v_ab_w_in, v_ab_conv_w, v_ab_a_log, v_ab_dt_bias, v_ab_gnorm_g, v_ab_w_out, v_c_w_in, v_c_lb_raw, v_c_gnorm_g, v_c_w_out, v_ln_mix_g, v_ln_mix_b, v_mlp_w1, v_mlp_w2, v_ln_ffn_g, v_ln_ffn_b):
    w = dict(zip(_WEIGHTS, (meta_tokens, ab_w_in, ab_conv_w, ab_a_log, ab_dt_bias, ab_gnorm_g, ab_w_out, c_w_in, c_lb_raw,
                            c_gnorm_g, c_w_out, ln_mix_g, ln_mix_b, mlp_w1, mlp_w2, ln_ffn_g, ln_ffn_b)))
    mom = dict(zip(_WEIGHTS, (m_meta_tokens, m_ab_w_in, m_ab_conv_w, m_ab_a_log, m_ab_dt_bias, m_ab_gnorm_g, m_ab_w_out,
                              m_c_w_in, m_c_lb_raw, m_c_gnorm_g, m_c_w_out, m_ln_mix_g, m_ln_mix_b, m_mlp_w1, m_mlp_w2,
                              m_ln_ffn_g, m_ln_ffn_b)))
    var = dict(zip(_WEIGHTS, (v_meta_tokens, v_ab_w_in, v_ab_conv_w, v_ab_a_log, v_ab_dt_bias, v_ab_gnorm_g, v_ab_w_out,
                              v_c_w_in, v_c_lb_raw, v_c_gnorm_g, v_c_w_out, v_ln_mix_g, v_ln_mix_b, v_mlp_w1, v_mlp_w2,
                              v_ln_ffn_g, v_ln_ffn_b)))
    me = 4 * lax.axis_index("x") + 2 * lax.axis_index("y") + lax.axis_index("c")
    seq = x.shape[1]
    pad = (-(N_META + seq)) % SB_BLOCK
    lp = pad + N_META + seq
    meta_w = D_MODEL // N_DEV
    conv_w_all = 2 * GDN_HEADS * HEAD_W + GDN_HEADS * HEAD_W
    conv_w_mine = conv_w_all // N_DEV

    g_meta, g_conv, g_ab_in = _gather([w["meta_tokens"], w["ab_conv_w"][0], w["ab_w_in"][0]], [F32, F32, BF16],
                                      name="gather_weights_first")
    meta_full = g_meta.transpose(1, 0, 2).reshape(N_META, D_MODEL)
    conv_full = g_conv.transpose(1, 0, 2).reshape(CONV_K, conv_w_all)
    ab_full = g_ab_in.transpose(1, 0, 2).reshape(D_MODEL, AB_IN)
    ba0 = AB_Z + 512
    w_ab = jnp.concatenate([ab_full[:, :ba0], ab_full[:, ba0 + 2 * GDN_HEADS:], ab_full[:, ba0:ba0 + 2 * GDN_HEADS],
                            jnp.zeros((D_MODEL, AB_CAT - AB_IN), BF16)], axis=1)
    vec128 = lambda p: jnp.zeros((1, HEAD_W), F32).at[0, :GDN_HEADS].set(p.reshape(GDN_HEADS))
    wts = dict(
        w_ab=w_ab, conv_w=conv_full, alog_v=vec128(w["ab_a_log"]), dtb_v=vec128(w["ab_dt_bias"]),
        ab_gn=w["ab_gnorm_g"][0], lb=_lower_bound(w["c_lb_raw"]), c_gn=w["c_gnorm_g"][0],
        ln_mix_g=w["ln_mix_g"], ln_mix_b=w["ln_mix_b"], ln_ffn_g=w["ln_ffn_g"], ln_ffn_b=w["ln_ffn_b"])

    def weights_a(gathered):
        g_ab_out, g_w1, g_w2 = gathered
        return dict(w_out0=g_ab_out.reshape(D_MODEL, D_MODEL), w1=[g_w1], w2=[g_w2.reshape(D_FF, D_MODEL)])

    def weights_b(gathered):
        g_c_in, g_c_out = gathered
        return dict(w_c=g_c_in, w_out1=g_c_out.reshape(D_MODEL, D_MODEL))

    def weights_c(gathered):
        g_w1, g_w2 = gathered
        return dict(w1=[g_w1], w2=[g_w2.reshape(D_FF, D_MODEL)])

    hooks = dict(
        gather_a=([w["ab_w_out"][0], w["mlp_w1"][0], w["mlp_w2"][0]], [BF16] * 3), weights_a=weights_a,
        gather_b=([w["c_w_in"][0], w["c_w_out"][0]], [BF16] * 2), weights_b=weights_b,
        gather_c=([w["mlp_w1"][1], w["mlp_w2"][1]], [BF16] * 2), weights_c=weights_c)

    h0 = jnp.concatenate([jnp.zeros((pad, D_MODEL), F32), meta_full, x[0]], axis=0)
    loss_vec, dh0, g = _local_step(h0, loss_target[0], pad, wts, hooks)
    loss = lax.psum(jnp.sum(loss_vec), ("x", "y", "c"))
    grad_x = dh0[lp - seq:][None]

    _, lb_vjp = jax.vjp(_lower_bound, w["c_lb_raw"])
    rep_part = _pack_replicated(dict(
        ln_mix_g=g["ln_mix_g"], ln_mix_b=g["ln_mix_b"], ln_ffn_g=g["ln_ffn_g"], ln_ffn_b=g["ln_ffn_b"],
        c_lb_raw=lb_vjp(g["lb"])[0], ab_gnorm_g=g["ab_gn"], c_gnorm_g=g["c_gn"],
        ab_a_log=g["alog_v"][0, :GDN_HEADS], ab_dt_bias=g["dtb_v"][0, :GDN_HEADS]))
    small = jnp.concatenate([rep_part, dh0[pad:pad + N_META], g["conv_w"].reshape(-1, D_MODEL),
                             jnp.zeros((_SMALL_N - _SMALL_CONV - CONV_K * conv_w_all // D_MODEL, D_MODEL), F32)], axis=0)
    (small_all,) = _gather([small], [F32], name="gather_small_grads")
    rep_out = _adamw(_pack_replicated(w), small_all[:, :_PACK_N], _pack_replicated(mom), _pack_replicated(var),
                     name="adamw_replicated")
    meta_parts = lax.dynamic_slice_in_dim(small_all[:, _SMALL_META:_SMALL_META + N_META], me * meta_w, meta_w, axis=2)
    meta_out = _adamw(w["meta_tokens"], meta_parts, mom["meta_tokens"], var["meta_tokens"], name="adamw_meta")
    conv_parts = small_all[:, _SMALL_CONV:_SMALL_CONV + CONV_K * conv_w_all // D_MODEL].reshape(N_DEV, CONV_K, conv_w_all)
    conv_parts = lax.dynamic_slice_in_dim(conv_parts, me * conv_w_mine, conv_w_mine, axis=2)
    conv_out = _adamw(w["ab_conv_w"][0], conv_parts, mom["ab_conv_w"][0], var["ab_conv_w"][0], name="adamw_conv")

    parts = g["parts"]
    big = [("ab_w_in", 0, parts["ab_w_in"]), ("ab_w_out", 0, parts["ab_w_out"]), ("mlp_w1", 0, parts["mlp_w1_0"]),
           ("mlp_w2", 0, parts["mlp_w2_0"]), ("c_w_in", 0, parts["c_w_in"]), ("c_w_out", 0, parts["c_w_out"]),
           ("mlp_w1", 1, parts["mlp_w1_1"]), ("mlp_w2", 1, parts["mlp_w2_1"])]
    big_out = {}
    for name, l, p in big:
        res = _adamw(w[name][l], p, mom[name][l], var[name][l], name=f"adamw_{name}{l}")
        big_out.setdefault(name, []).append(res)

    rep = [_unpack_replicated(r, w) for r in rep_out]
    outs = {}
    for name in _WEIGHTS:
        if name == "meta_tokens":
            outs[name] = list(meta_out)
        elif name == "ab_conv_w":
            outs[name] = [o[None] for o in conv_out]
        elif name in big_out:
            res = big_out[name]
            outs[name] = [o[None] for o in res[0]] if len(res) == 1 else [jnp.stack(pair) for pair in zip(*res)]
        else:
            outs[name] = [r[name] for r in rep]
    flat = [loss, grad_x]
    for kind in range(4):
        flat += [outs[name][kind] for name in _WEIGHTS]
    return tuple(flat)
```

```python
import functools

import jax
import jax.numpy as jnp
from jax import lax
from jax.experimental import pallas as pl
from jax.experimental.pallas import tpu as pltpu

F32 = jnp.float32
BF16 = jnp.bfloat16

N_DEV = 8
D_MODEL = 1024
N_META = 16
D_FF = 4096
DEPTH = 2
GDN_HEADS = 4
SB_HEADS = 8
SB_DH = 64
HG_HEADS = 8
HEAD_W = 128
CHUNK = 64
SB_BLOCK = 128
CONV_K = 4
DN_ALPHA = float((2 * DEPTH) ** 0.25)
LN_EPS = 1e-5
RMS_EPS = 1e-6
L2_EPS = 1e-6
ADAM_LR, ADAM_B1, ADAM_B2, ADAM_EPS, ADAM_WD, ADAM_STEP = 0.001, 0.9, 0.999, 1e-08, 0.01, 10

AB_Z = 1536
AB_SB = 2048
AB_BA = 3584
AB_CAT = 3840
AB_IN = 3592

VMEM_LIMIT = 56 * 1024 * 1024


def _cp(sem=None, **kw):
    if sem is not None:
        kw["dimension_semantics"] = sem
    return pltpu.CompilerParams(vmem_limit_bytes=VMEM_LIMIT, **kw)


def _row_tile(n, want):
    best = 8
    for t in range(8, min(n, want) + 1, 8):
        if n % t == 0:
            best = t
    return best


@jax.custom_vjp
def _sigmoid(x):
    e = jnp.exp(-jnp.abs(x))
    r = 1.0 / (1.0 + e)
    return jnp.where(x >= 0, r, e * r)


def _sigmoid_fwd(x):
    s = _sigmoid(x)
    return s, s


def _sigmoid_bwd(s, g):
    return (g * s * (1.0 - s),)


_sigmoid.defvjp(_sigmoid_fwd, _sigmoid_bwd)


def _log1p_exp_neg_abs(x):
    e = jnp.exp(-jnp.abs(x))
    return jnp.where(e < 1e-4, e - 0.5 * e * e, jnp.log(1.0 + e))


@jax.custom_vjp
def _softplus(x):
    return jnp.maximum(x, 0.0) + _log1p_exp_neg_abs(x)


def _softplus_fwd(x):
    return _softplus(x), x


def _softplus_bwd(x, g):
    return (g * _sigmoid(x),)


_softplus.defvjp(_softplus_fwd, _softplus_bwd)


def _silu(x):
    return x * _sigmoid(x)


def _silu_grad(x):
    s = _sigmoid(x)
    return s * (1.0 + x * (1.0 - s))


def _dot(a, b, dims, precision=None):
    return lax.dot_general(a, b, (dims, ((), ())), precision=precision, preferred_element_type=F32)


NN = ((1,), (0,))
NT = ((1,), (1,))
TN = ((0,), (0,))


def _bdot(a, b, dims):
    return _dot(a.astype(BF16), b.astype(BF16), dims)


def _layer_norm(pre, g, beta):
    mu = jnp.mean(pre, axis=-1, keepdims=True)
    xc = pre - mu
    var = jnp.mean(xc * xc, axis=-1, keepdims=True)
    return xc * lax.rsqrt(var + LN_EPS) * g + beta


def _layer_norm_bwd(pre, g, dy):
    mu = jnp.mean(pre, axis=-1, keepdims=True)
    xc = pre - mu
    rstd = lax.rsqrt(jnp.mean(xc * xc, axis=-1, keepdims=True) + LN_EPS)
    xhat = xc * rstd
    dxh = dy * g
    m1 = jnp.mean(dxh, axis=-1, keepdims=True)
    m2 = jnp.mean(dxh * xhat, axis=-1, keepdims=True)
    return (rstd * (dxh - m1 - xhat * m2), jnp.sum(dy * xhat, axis=0, keepdims=True),
            jnp.sum(dy, axis=0, keepdims=True))


def _mm(a, b, mode, *, tm, tn, tk, name, epi=None, c=None, scale=1.0, b_dev=False, out_dev=False, out_dtype=F32,
        ln=None, scatter=()):
    if mode == "NN":
        m, kk = a.shape
        n = b.shape[2] * N_DEV if b_dev else b.shape[1]
    elif mode == "NT":
        m, kk = a.shape
        n = b.shape[1] if b_dev else b.shape[0]
    else:
        kk, m = a.shape
        n = b.shape[1]
    assert m % tm == 0 and n % tn == 0 and kk % tk == 0, (name, m, n, kk, tm, tn, tk)
    nk = kk // tk
    dims = {"NN": NN, "NT": NT, "TN": TN}[mode]

    if mode == "TN":
        a_spec = pl.BlockSpec((tk, tm), lambda i, j, k: (k, i))
    else:
        a_spec = pl.BlockSpec((tm, tk), lambda i, j, k: (i, k))
    if mode == "NN":
        if b_dev:
            assert tn == b.shape[2]
            b_spec = pl.BlockSpec((None, tk, tn), lambda i, j, k: (j, k, 0))
        else:
            b_spec = pl.BlockSpec((tk, tn), lambda i, j, k: (k, j))
    elif mode == "NT":
        if b_dev:
            assert tk == b.shape[2]
            b_spec = pl.BlockSpec((None, tn, tk), lambda i, j, k: (k, j, 0))
        else:
            b_spec = pl.BlockSpec((tn, tk), lambda i, j, k: (j, k))
    else:
        b_spec = pl.BlockSpec((tk, tn), lambda i, j, k: (k, j))
    in_specs = [a_spec, b_spec]
    operands = [a, b]
    if c is not None:
        in_specs.append(pl.BlockSpec((tm, tn), lambda i, j, k: (i, j)))
        operands.append(c)
    if epi == "ln":
        assert tn == n and not out_dev
        in_specs += [pl.BlockSpec((1, n), lambda i, j, k: (0, 0))] * 2
        operands += [ln[0].reshape(1, n), ln[1].reshape(1, n)]
    elif epi == "ln_bwd":
        assert tn == n and not out_dev
        in_specs += [pl.BlockSpec((tm, tn), lambda i, j, k: (i, j)), pl.BlockSpec((1, n), lambda i, j, k: (0, 0))]
        operands += [ln[0], ln[1].reshape(1, n)]
    if out_dev:
        assert tn == n // N_DEV
        out_shape = jax.ShapeDtypeStruct((N_DEV, m, tn), out_dtype)
        out_spec = pl.BlockSpec((None, tm, tn), lambda i, j, k: (j, i, 0))
    else:
        out_shape = jax.ShapeDtypeStruct((m, n), out_dtype)
        out_spec = pl.BlockSpec((tm, tn), lambda i, j, k: (i, j))
    if epi == "ln":
        out_shape = [out_shape, out_shape, jax.ShapeDtypeStruct((m, n), BF16)]
        out_spec = [out_spec] * 3
    elif epi == "relu2_copy":
        assert not out_dev
        out_shape = [out_shape, jax.ShapeDtypeStruct((m, n), BF16)]
        out_spec = [out_spec] * 2
    elif epi == "ln_bwd":
        vec_shape, vec_spec = jax.ShapeDtypeStruct((1, n), F32), pl.BlockSpec((1, n), lambda i, j, k: (0, 0))
        out_shape = [out_shape, jax.ShapeDtypeStruct((m, n), BF16), vec_shape, vec_shape]
        out_spec = [out_spec, out_spec, vec_spec, vec_spec]
    n_out = {"ln": 3, "relu2_copy": 2, "ln_bwd": 4}.get(epi, 1)
    ns = len(scatter)
    if ns:
        in_specs += [_ANY] * ns
        operands += list(scatter)
        out_shape = (out_shape if n_out > 1 else [out_shape]) + [jax.ShapeDtypeStruct(s.shape, s.dtype) for s in scatter]
        out_spec = (out_spec if n_out > 1 else [out_spec]) + [_ANY] * ns
    n_in = len(operands)
    grid = (m // tm, n // tn, nk)

    def body(*refs):
        a_ref, b_ref = refs[0], refs[1]
        c_ref = refs[2] if c is not None else None
        o_ref = refs[n_in]
        scratch0 = n_in + n_out + ns
        acc_ref = refs[scratch0] if nk > 1 else None
        if ns:
            s_start, s_finish = _scatter_phases(refs[n_in - ns:n_in], refs[n_in + n_out:scratch0],
                                                *refs[scratch0 + (1 if nk > 1 else 0):])
            at = lambda step: functools.reduce(lambda x, y: x & y, [pl.program_id(ax) == step[ax] for ax in range(3)])
            pl.when(at((0, 0, 0)))(s_start)
        p = _dot(a_ref[...].astype(BF16), b_ref[...].astype(BF16), dims)
        first_rows = pl.program_id(0) == 0

        def finish(acc):
            if epi == "add":
                acc = acc + scale * c_ref[...]
            elif epi == "relu2grad":
                acc = acc * (2.0 * jnp.maximum(c_ref[...], 0.0))
            elif epi == "relu2_copy":
                refs[n_in + 1][...] = jnp.square(jnp.maximum(acc, 0.0)).astype(BF16)
            elif epi == "ln_bwd":
                acc, dg, db = _layer_norm_bwd(refs[3][...], refs[4][...], acc + scale * c_ref[...])
                dg_ref, db_ref = refs[n_in + 2], refs[n_in + 3]

                @pl.when(first_rows)
                def _():
                    dg_ref[...] = jnp.zeros_like(dg_ref)
                    db_ref[...] = jnp.zeros_like(db_ref)

                dg_ref[...] += dg
                db_ref[...] += db
                refs[n_in + 1][...] = acc.astype(BF16)
            elif epi == "ln":
                acc = acc + scale * c_ref[...]
                y = _layer_norm(acc, refs[3][...], refs[4][...])
                refs[n_in + 1][...] = y
                refs[n_in + 2][...] = y.astype(BF16)
            o_ref[...] = acc.astype(out_dtype)

        if nk == 1:
            finish(p)
        else:
            k = pl.program_id(2)

            @pl.when(k == 0)
            def _():
                acc_ref[...] = p

            @pl.when(k > 0)
            def _():
                acc_ref[...] += p

            @pl.when(k == nk - 1)
            def _():
                finish(acc_ref[...])

        if ns:
            pl.when(at(tuple(g - 1 for g in grid)))(s_finish)

    res = pl.pallas_call(
        body, name=name, grid=grid, in_specs=in_specs, out_specs=out_spec, out_shape=out_shape,
        scratch_shapes=([pltpu.VMEM((tm, tn), F32)] if nk > 1 else []) + (_scatter_scratch(ns) if ns else []),
        compiler_params=_cp(("arbitrary",) * 3 if ns or epi == "ln_bwd" else ("parallel", "parallel", "arbitrary"),
                            has_side_effects=bool(ns)),
    )(*operands)
    return res


def _ln_bwd(pre, g, dy, *, name):
    lp, d = pre.shape
    tm = _row_tile(lp, 512)

    def body(pre_ref, g_ref, dy_ref, dpre_ref, dpreb_ref, dg_ref, db_ref):
        dpre, dg, db = _layer_norm_bwd(pre_ref[...], g_ref[...], dy_ref[...])
        dpre_ref[...] = dpre
        dpreb_ref[...] = dpre.astype(BF16)

        @pl.when(pl.program_id(0) == 0)
        def _():
            dg_ref[...] = jnp.zeros_like(dg_ref)
            db_ref[...] = jnp.zeros_like(db_ref)

        dg_ref[...] += dg
        db_ref[...] += db

    row = pl.BlockSpec((tm, d), lambda i: (i, 0))
    vec = pl.BlockSpec((1, d), lambda i: (0, 0))
    return pl.pallas_call(
        body, name=name, grid=(lp // tm,), in_specs=[row, vec, row], out_specs=[row, row, vec, vec],
        out_shape=[jax.ShapeDtypeStruct((lp, d), F32), jax.ShapeDtypeStruct((lp, d), BF16),
                   jax.ShapeDtypeStruct((1, d), F32), jax.ShapeDtypeStruct((1, d), F32)],
        compiler_params=_cp(("arbitrary",)),
    )(pre, g.reshape(1, d), dy)


def _loss_head(y, target, *, name):
    lp, d = y.shape
    seq = target.shape[0]
    tm = SB_BLOCK
    first = (lp - seq) // tm
    assert (lp - seq) % tm == 0 and seq % tm == 0

    def body(y_ref, t_ref, dy_ref, loss_ref):
        i = pl.program_id(0)
        live = i >= first
        diff = jnp.where(live, y_ref[...] - t_ref[...], 0.0)
        dy_ref[...] = diff * (1.0 / d)

        @pl.when(i == 0)
        def _():
            loss_ref[...] = jnp.zeros_like(loss_ref)

        loss_ref[...] += jnp.sum(diff * diff, axis=0, keepdims=True) * (0.5 / d)

    return pl.pallas_call(
        body, name=name, grid=(lp // tm,),
        in_specs=[pl.BlockSpec((tm, d), lambda i: (i, 0)),
                  pl.BlockSpec((tm, d), lambda i: (jnp.maximum(i - first, 0), 0))],
        out_specs=[pl.BlockSpec((tm, d), lambda i: (i, 0)), pl.BlockSpec((1, d), lambda i: (0, 0))],
        out_shape=[jax.ShapeDtypeStruct((lp, d), F32), jax.ShapeDtypeStruct((1, d), F32)],
        compiler_params=_cp(("arbitrary",)),
    )(y, target)


def _gate_fwd(o, zsrc, z_blk0, g, other, *, heads, name):
    lp = o.shape[0]
    tm = _row_tile(lp, 512)
    w = heads * HEAD_W
    assert (z_blk0 * HEAD_W) % w == 0
    has_other = w < D_MODEL

    def body(o_ref, z_ref, g_ref, *rest):
        y_ref = rest[-1]
        gv = g_ref[...]
        for h in range(heads):
            cs = slice(h * HEAD_W, (h + 1) * HEAD_W)
            ov = o_ref[:, cs]
            r = lax.rsqrt(jnp.mean(ov * ov, axis=-1, keepdims=True) + RMS_EPS)
            y_ref[:, cs] = (ov * r * gv * _silu(z_ref[:, cs])).astype(BF16)
        if has_other:
            y_ref[:, w:] = rest[0][...].astype(BF16)

    row = lambda width, blk: pl.BlockSpec((tm, width), lambda i: (i, blk))
    return pl.pallas_call(
        body, name=name, grid=(lp // tm,),
        in_specs=[row(w, 0), row(w, z_blk0 * HEAD_W // w), pl.BlockSpec((1, HEAD_W), lambda i: (0, 0))]
        + ([row(D_MODEL - w, 0)] if has_other else []),
        out_specs=row(D_MODEL, 0), out_shape=jax.ShapeDtypeStruct((lp, D_MODEL), BF16),
        compiler_params=_cp(("parallel",)),
    )(o, zsrc, g.reshape(1, HEAD_W), *([other] if has_other else []))


def _gate_bwd(o, zsrc, z_blk0, g, dy, *, heads, name):
    lp = o.shape[0]
    tm = _row_tile(lp, 512)

    w = heads * HEAD_W
    assert (z_blk0 * HEAD_W) % w == 0

    def body(o_ref, z_ref, g_ref, dy_ref, do_ref, dz_ref, dg_ref):
        @pl.when(pl.program_id(0) == 0)
        def _():
            dg_ref[...] = jnp.zeros_like(dg_ref)

        gv = g_ref[...]
        dg = jnp.zeros((1, HEAD_W), F32)
        for h in range(heads):
            cs = slice(h * HEAD_W, (h + 1) * HEAD_W)
            ov, zv, dyv = o_ref[:, cs], z_ref[:, cs], dy_ref[:, cs]
            r = lax.rsqrt(jnp.mean(ov * ov, axis=-1, keepdims=True) + RMS_EPS)
            nrm = ov * r
            s = _silu(zv)
            dn = dyv * gv * s
            do_ref[:, cs] = r * (dn - nrm * jnp.mean(dn * nrm, axis=-1, keepdims=True))
            dz_ref[:, cs] = dyv * nrm * gv * _silu_grad(zv)
            dg = dg + jnp.sum(dyv * nrm * s, axis=0, keepdims=True)
        dg_ref[...] += dg

    row = lambda blk: pl.BlockSpec((tm, w), lambda i: (i, blk))
    vec = pl.BlockSpec((1, HEAD_W), lambda i: (0, 0))
    return pl.pallas_call(
        body, name=name, grid=(lp // tm,),
        in_specs=[row(0), row(z_blk0 * HEAD_W // w), vec, row(0)], out_specs=[row(0), row(0), vec],
        out_shape=[jax.ShapeDtypeStruct((lp, w), F32), jax.ShapeDtypeStruct((lp, w), F32),
                   jax.ShapeDtypeStruct((1, HEAD_W), F32)],
        compiler_params=_cp(("arbitrary",)),
    )(o, zsrc, g.reshape(1, HEAD_W), dy)


def _conv_taps(x, w):
    acc = w[CONV_K - 1:CONV_K, :] * x
    for k in range(CONV_K - 1):
        acc = acc + w[k:k + 1, :] * pltpu.roll(x, CONV_K - 1 - k, 0)
    return acc


def _gdn_pre_fwd(p0, conv_w, pad, *, name):
    lp = p0.shape[0]
    nq = GDN_HEADS
    qscale = HEAD_W ** -0.5

    def body(x_ref, w_ref, y_ref):
        j = pl.program_id(0)
        c = _conv_taps(x_ref[...], w_ref[...])
        s = _silu(c)
        r = lax.rsqrt(jnp.sum(s * s, axis=-1, keepdims=True) + L2_EPS)
        mult = jnp.where(j < nq, r * qscale, jnp.where(j < 2 * nq, r, 1.0))
        rows = lax.broadcasted_iota(jnp.int32, (lp, 1), 0)
        y_ref[...] = jnp.where(rows >= pad, s * mult, 0.0)

    return pl.pallas_call(
        body, name=name, grid=(3 * nq,),
        in_specs=[pl.BlockSpec((lp, HEAD_W), lambda j: (0, j)), pl.BlockSpec((CONV_K, HEAD_W), lambda j: (0, j))],
        out_specs=pl.BlockSpec((lp, HEAD_W), lambda j: (0, j)),
        out_shape=jax.ShapeDtypeStruct((lp, 3 * nq * HEAD_W), F32), compiler_params=_cp(("parallel",)),
    )(p0, conv_w)


def _gdn_pre_bwd(p0, conv_w, dqkv, pad, *, name):
    lp = p0.shape[0]
    nq = GDN_HEADS
    qscale = HEAD_W ** -0.5

    def body(x_ref, w_ref, dy_ref, dx_ref, dw_ref):
        j = pl.program_id(0)
        x, w = x_ref[...], w_ref[...]
        c = _conv_taps(x, w)
        s = _silu(c)
        r = lax.rsqrt(jnp.sum(s * s, axis=-1, keepdims=True) + L2_EPS)
        rows = lax.broadcasted_iota(jnp.int32, (lp, 1), 0)
        dy = jnp.where(rows >= pad, dy_ref[...], 0.0)
        nrm = s * r
        dn = dy * jnp.where(j < nq, qscale, 1.0)
        ds_norm = r * (dn - nrm * jnp.sum(nrm * dn, axis=-1, keepdims=True))
        ds = jnp.where(j < 2 * nq, ds_norm, dy)
        dc = ds * _silu_grad(c)
        dx = w[CONV_K - 1:CONV_K, :] * dc
        dws = [None] * CONV_K
        dws[CONV_K - 1] = jnp.sum(dc * x, axis=0, keepdims=True)
        for k in range(CONV_K - 1):
            sh = CONV_K - 1 - k
            dx = dx + w[k:k + 1, :] * pltpu.roll(dc, lp - sh, 0)
            dws[k] = jnp.sum(dc * pltpu.roll(x, sh, 0), axis=0, keepdims=True)
        dx_ref[...] = dx
        dw_ref[...] = jnp.concatenate(dws, axis=0)

    blk = pl.BlockSpec((lp, HEAD_W), lambda j: (0, j))
    wblk = pl.BlockSpec((CONV_K, HEAD_W), lambda j: (0, j))
    return pl.pallas_call(
        body, name=name, grid=(3 * nq,), in_specs=[blk, wblk, blk], out_specs=[blk, wblk],
        out_shape=[jax.ShapeDtypeStruct((lp, 3 * nq * HEAD_W), F32),
                   jax.ShapeDtypeStruct((CONV_K, 3 * nq * HEAD_W), F32)],
        compiler_params=_cp(("parallel",)),
    )(p0, conv_w, dqkv)


@jax.custom_vjp
def _inv_unit_lower(m):
    c = m.shape[0]
    eye = (lax.broadcasted_iota(jnp.int32, (c, c), 0) == lax.broadcasted_iota(jnp.int32, (c, c), 1)).astype(F32)
    x = eye - m
    p = m
    n = 2
    while n < CHUNK:
        p = _bdot(p, p, NN)
        x = x + _bdot(x, p, NN)
        n *= 2
    return x


def _inv_fwd(m):
    t = _inv_unit_lower(m)
    return t, t


def _inv_bwd(t, g):
    return (-_bdot(_bdot(t, g, TN), t, NT),)


_inv_unit_lower.defvjp(_inv_fwd, _inv_bwd)


GDN_STEP = 2


def _heads_to_rows(x, nh):
    return jnp.concatenate([x[:, h * HEAD_W:(h + 1) * HEAD_W] for h in range(nh)], axis=0)


def _rows_to_heads(x, nh):
    c = x.shape[0] // nh
    return jnp.concatenate([x[h * c:(h + 1) * c] for h in range(nh)], axis=1)


def _gdn_chunk(q, k, v, ba, alog, dtb, states, valid):
    nh = GDN_HEADS
    c = q.shape[0]
    r = nh * c
    lane = lax.broadcasted_iota(jnp.int32, (1, HEAD_W), 1)
    pick = lambda x, l: jnp.sum(jnp.where(lane == l, x, 0.0), axis=-1, keepdims=True)
    beta = jnp.concatenate([jnp.where(valid, _sigmoid(pick(ba, h)), 0.0) for h in range(nh)], axis=0)
    g = jnp.concatenate(
        [jnp.where(valid, -jnp.exp(pick(alog, h)) * _softplus(pick(ba, nh + h) + pick(dtb, h)), 0.0) for h in range(nh)],
        axis=0)
    qs, ks, vs = _heads_to_rows(q, nh), _heads_to_rows(k, nh), _heads_to_rows(v, nh)
    rr = lax.broadcasted_iota(jnp.int32, (r, r), 0)
    cc = lax.broadcasted_iota(jnp.int32, (r, r), 1)
    same = (rr // c) == (cc // c)
    causal, strict = same & (cc <= rr), same & (cc < rr)
    lower = jnp.where(causal, 1.0, 0.0).astype(BF16)
    upper = jnp.where(same & (cc >= rr), 1.0, 0.0).astype(BF16)
    gcb = _mask_mm(lower, upper, g * jnp.ones((1, HEAD_W), F32))
    gc_col = jnp.concatenate([gcb] * (r // HEAD_W), axis=1)
    decay = jnp.where(causal, jnp.exp(jnp.minimum(gc_col - gc_col.T, 0.0)), 0.0)
    egc = jnp.exp(gcb)
    kb = ks * beta
    m = jnp.where(strict, _dot3(kb, ks, NT) * decay, 0.0)
    t = _inv_unit_lower(m)
    u = _dot3(t, vs * beta, NN)
    w = _dot3(t, kb * egc, NN)
    a = _bdot(qs, ks, NT) * decay
    rows = lambda x, h: x[h * c:(h + 1) * c]
    qe = qs * egc
    v_new = u - jnp.concatenate([_bdot(rows(w, h), states[h], NN) for h in range(nh)], axis=0)
    o = jnp.concatenate([_bdot(rows(qe, h), states[h], NN) for h in range(nh)], axis=0) + _bdot(a, v_new, NN)
    new_states = []
    for h in range(nh):
        gl = gcb[(h + 1) * c - 1:(h + 1) * c, :]
        k_dec = rows(ks, h) * jnp.exp(gl - rows(gcb, h))
        new_states.append(states[h] * jnp.exp(gl) + _bdot(k_dec, rows(v_new, h), TN))
    return _rows_to_heads(o, nh), new_states


def _gdn_fwd(qkv, p0, alog_v, dtb_v, pad, *, name, gather=None):
    lp = qkv.shape[0]
    n = lp // CHUNK
    nh = GDN_HEADS
    assert n % GDN_STEP == 0
    steps, rows = n // GDN_STEP, GDN_STEP * CHUNK
    g_srcs, g_dtypes = gather if gather is not None else ([], [])
    ng_arr = len(g_srcs)

    def body(q_ref, k_ref, v_ref, ba_ref, al_ref, dt_ref, *rest):
        g_ins, (o_ref, st_ref) = rest[:ng_arr], rest[ng_arr:ng_arr + 2]
        g_outs, s_ref, g_scratch = rest[ng_arr + 2:2 * ng_arr + 2], rest[2 * ng_arr + 2], rest[2 * ng_arr + 3:]
        i = pl.program_id(0)
        if ng_arr:
            g_start, g_forward, g_finish = _gather_phases(g_ins, g_outs, g_scratch[:ng_arr], *g_scratch[ng_arr:],
                                                          g_dtypes)
            pl.when(i == 0)(g_start)
            pl.when(i == (3 * steps) // 4)(g_forward)

        @pl.when(i == 0)
        def _():
            s_ref[...] = jnp.zeros_like(s_ref)

        s = s_ref[...]
        s = [s[h] for h in range(nh)]
        q, k, v, ba, al, dt = q_ref[...], k_ref[...], v_ref[...], ba_ref[...], al_ref[...], dt_ref[...]
        outs = []
        for c in range(GDN_STEP):
            sl = slice(c * CHUNK, (c + 1) * CHUNK)
            valid = (i * rows + c * CHUNK + lax.broadcasted_iota(jnp.int32, (CHUNK, 1), 0)) >= pad
            for h in range(nh):
                st_ref[c, h] = s[h]
            o, s = _gdn_chunk(q[sl], k[sl], v[sl], ba[sl], al, dt, s, valid)
            outs.append(o)
        o_ref[...] = jnp.concatenate(outs, axis=0)
        for h in range(nh):
            s_ref[h] = s[h]
        if ng_arr:
            pl.when(i == steps - 1)(g_finish)

    w = nh * HEAD_W
    vec = pl.BlockSpec((1, HEAD_W), lambda i: (0, 0))
    return pl.pallas_call(
        body, name=name, grid=(steps,),
        in_specs=[pl.BlockSpec((rows, w), lambda i: (i, 0)), pl.BlockSpec((rows, w), lambda i: (i, 1)),
                  pl.BlockSpec((rows, w), lambda i: (i, 2)), pl.BlockSpec((rows, HEAD_W), lambda i: (i, AB_BA // HEAD_W)),
                  vec, vec] + [pl.BlockSpec(memory_space=pltpu.VMEM)] * ng_arr,
        out_specs=[pl.BlockSpec((rows, w), lambda i: (i, 0)),
                   pl.BlockSpec((GDN_STEP, nh, HEAD_W, HEAD_W), lambda i: (i, 0, 0, 0))] + [_ANY] * ng_arr,
        out_shape=[jax.ShapeDtypeStruct((lp, w), F32), jax.ShapeDtypeStruct((n, nh, HEAD_W, HEAD_W), F32)]
        + _gather_out_shapes(g_srcs, g_dtypes),
        scratch_shapes=[pltpu.VMEM((nh, HEAD_W, HEAD_W), F32)] + (_gather_scratch(g_srcs, g_dtypes) if ng_arr else []),
        compiler_params=_cp(("arbitrary",), has_side_effects=bool(ng_arr)),
    )(qkv, qkv, qkv, p0, alog_v, dtb_v, *g_srcs)


def _gdn_bwd(qkv, p0, alog_v, dtb_v, states, do, pad, *, name, scatter=()):
    lp = qkv.shape[0]
    n = lp // CHUNK
    nh = GDN_HEADS
    assert n % GDN_STEP == 0
    steps, rows = n // GDN_STEP, GDN_STEP * CHUNK
    ns = len(scatter)

    def body(q_ref, k_ref, v_ref, ba_ref, al_ref, dt_ref, st_ref, do_ref, *rest):
        s_ins, (dq_ref, dk_ref, dv_ref, dba_ref, dal_ref, ddt_ref) = rest[:ns], rest[ns:ns + 6]
        s_outs, ds_ref, s_sems = rest[ns + 6:2 * ns + 6], rest[2 * ns + 6], rest[2 * ns + 7:]
        step = pl.program_id(0)
        i = steps - 1 - step
        if ns:
            s_start, s_finish = _scatter_phases(s_ins, s_outs, *s_sems)
            pl.when(step == 0)(s_start)

        @pl.when(step == 0)
        def _():
            ds_ref[...] = jnp.zeros_like(ds_ref)
            dal_ref[...] = jnp.zeros_like(dal_ref)
            ddt_ref[...] = jnp.zeros_like(ddt_ref)

        q, k, v, ba, al, dt = q_ref[...], k_ref[...], v_ref[...], ba_ref[...], al_ref[...], dt_ref[...]
        st, do, dst = st_ref[...], do_ref[...], ds_ref[...]
        vjps = []
        for c in range(GDN_STEP):
            sl = slice(c * CHUNK, (c + 1) * CHUNK)
            valid = (i * rows + c * CHUNK + lax.broadcasted_iota(jnp.int32, (CHUNK, 1), 0)) >= pad
            fn = functools.partial(_gdn_chunk, valid=valid)
            vjps.append(jax.vjp(fn, q[sl], k[sl], v[sl], ba[sl], al, dt, [st[c, h] for h in range(nh)])[1])
        ds = [dst[h] for h in range(nh)]
        grads = [None] * GDN_STEP
        for c in reversed(range(GDN_STEP)):
            grads[c] = vjps[c]((do[c * CHUNK:(c + 1) * CHUNK], ds))
            ds = grads[c][6]
        for j, ref in enumerate((dq_ref, dk_ref, dv_ref, dba_ref)):
            ref[...] = jnp.concatenate([gr[j] for gr in grads], axis=0)
        dal_ref[...] += sum(gr[4] for gr in grads)
        ddt_ref[...] += sum(gr[5] for gr in grads)
        for h in range(nh):
            ds_ref[h] = ds[h]
        if ns:
            pl.when(step == steps - 1)(s_finish)

    w = nh * HEAD_W
    rev = lambda c: (lambda s: (steps - 1 - s, c))
    vec = pl.BlockSpec((1, HEAD_W), lambda s: (0, 0))
    return pl.pallas_call(
        body, name=name, grid=(steps,),
        in_specs=[pl.BlockSpec((rows, w), rev(0)), pl.BlockSpec((rows, w), rev(1)), pl.BlockSpec((rows, w), rev(2)),
                  pl.BlockSpec((rows, HEAD_W), rev(AB_BA // HEAD_W)), vec, vec,
                  pl.BlockSpec((GDN_STEP, nh, HEAD_W, HEAD_W), lambda s: (steps - 1 - s, 0, 0, 0)),
                  pl.BlockSpec((rows, w), rev(0))] + [_ANY] * ns,
        out_specs=[pl.BlockSpec((rows, w), rev(0)), pl.BlockSpec((rows, w), rev(0)), pl.BlockSpec((rows, w), rev(0)),
                   pl.BlockSpec((rows, HEAD_W), rev(0)), vec, vec] + [_ANY] * ns,
        out_shape=[jax.ShapeDtypeStruct((lp, w), F32)] * 3 + [jax.ShapeDtypeStruct((lp, HEAD_W), F32)]
        + [jax.ShapeDtypeStruct((1, HEAD_W), F32)] * 2 + [jax.ShapeDtypeStruct(s.shape, s.dtype) for s in scatter],
        scratch_shapes=[pltpu.VMEM((nh, HEAD_W, HEAD_W), F32)] + (_scatter_scratch(ns) if ns else []),
        compiler_params=_cp(("arbitrary",), has_side_effects=bool(ns)),
    )(qkv, qkv, qkv, p0, alog_v, dtb_v, states, do, *scatter)


HG_LEVELS = (32, 16, 8, 4, 2, 1)
HG_GROUP = 4
HG_STEP = 2


def _hg_masks():
    import numpy as np
    c = CHUNK
    t = np.arange(c)[:, None]
    j = np.arange(c)[None, :]
    sums = (j <= t).astype(np.float32)
    pairs = [j == t]
    for m in HG_LEVELS:
        p = (t // (2 * m)) * (2 * m)
        r = p + m
        pairs.append((t >= r) & (j < r) & (j >= p))
    pairs = np.concatenate([np.kron(np.eye(HG_GROUP), p) for p in pairs], axis=0).astype(np.float32)
    return jnp.asarray(sums, BF16), jnp.asarray(sums.T, BF16), jnp.asarray(pairs, F32)


def _hg_level_row(b, m):
    c, w = b.shape
    if m >= 8:
        return jnp.concatenate([jnp.broadcast_to(b[p + m:p + m + 1], (2 * m, w)) for p in range(0, c, 2 * m)], axis=0)
    tiles = b.reshape(c // 8, 8, w)
    sub = lax.broadcasted_iota(jnp.int32, (1, 8, 1), 1)
    out = None
    for r0 in range(m, 8, 2 * m):
        cand = jnp.broadcast_to(tiles[:, r0:r0 + 1, :], tiles.shape)
        out = cand if out is None else jnp.where(sub >= r0 - m, cand, out)
    return out.reshape(c, w)


def _split3(x):
    hi = x.astype(BF16)
    r1 = x - hi.astype(F32)
    mid = r1.astype(BF16)
    return hi, mid, (r1 - mid.astype(F32)).astype(BF16)


def _dot3_raw(a, b, dims):
    ah, am, _ = _split3(a)
    bh, bm, _ = _split3(b)
    return _dot(ah, bh, dims) + (_dot(ah, bm, dims) + _dot(am, bh, dims))


@functools.partial(jax.custom_vjp, nondiff_argnums=(2,))
def _dot3(a, b, dims):
    return _dot3_raw(a, b, dims)


def _dot3_fwd(a, b, dims):
    return _dot3_raw(a, b, dims), (a, b)


def _dot3_bwd(dims, res, g):
    a, b = res
    if dims == NN:
        return _dot3_raw(g, b, NT), _dot3_raw(a, g, TN)
    return _dot3_raw(g, b, NN), _dot3_raw(g, a, TN)


_dot3.defvjp(_dot3_fwd, _dot3_bwd)


def _mask_mm_raw(m, x):
    return sum(_dot(m, part, NN) for part in _split3(x))


@jax.custom_vjp
def _mask_mm(m, mt, x):
    return _mask_mm_raw(m, x)


def _mask_mm_fwd(m, mt, x):
    return _mask_mm_raw(m, x), (m, mt)


def _mask_mm_bwd(res, g):
    m, mt = res
    return jnp.zeros_like(m), jnp.zeros_like(mt), _mask_mm_raw(mt, g)


_mask_mm.defvjp(_mask_mm_fwd, _mask_mm_bwd)


def _hg_chunk(qr, fr, ir, lb, states, valid, sums, sums_t, pairs):
    nh = HG_GROUP
    c = qr.shape[0]
    r = nh * c
    fg = lb + (1.0 - lb) * _sigmoid(fr)
    logf = jnp.where(valid, jnp.log(fg), 0.0)
    k = jnp.where(valid, 1.0 - fg, 0.0)
    qs = jnp.where(valid, _silu(qr), 0.0)
    v = jnp.where(valid, ir, 0.0)
    b = _mask_mm(sums, sums_t, logf)
    mask = lambda n: pairs[n * r:(n + 1) * r]
    stack = lambda x: _heads_to_rows(x, nh)
    a = mask(0) * _bdot(stack(qs), stack(k), NT)
    for lvl, m in enumerate(HG_LEVELS):
        d = b - _hg_level_row(b, m)
        a = a + mask(1 + lvl) * _bdot(stack(qs * jnp.exp(jnp.minimum(d, 0.0))),
                                      stack(k * jnp.exp(jnp.minimum(-d, 0.0))), NT)
    av = _bdot(a, stack(v), NN)
    eb = jnp.exp(b)
    qe, kd = qs * eb, k * jnp.exp(b[c - 1:c] - b)
    outs, new_states = [], []
    for h in range(nh):
        cs = slice(h * HEAD_W, (h + 1) * HEAD_W)
        outs.append(_bdot(qe[:, cs], states[h], NT) + av[h * c:(h + 1) * c])
        new_states.append(states[h] * eb[c - 1:c, cs] + _bdot(v[:, cs], kd[:, cs], TN))
    return jnp.concatenate(outs, axis=1), new_states


def _hg_fwd(p1, lb, pad, *, name, gather=None):
    lp = p1.shape[0]
    n = lp // CHUNK
    nh = HG_HEADS
    g_srcs, g_dtypes = gather if gather is not None else ([], [])
    ng_arr = len(g_srcs)

    def body(q_ref, f_ref, i_ref, lb_ref, sums_ref, sums_t_ref, pairs_ref, *rest):
        g_ins, (o_ref, st_ref) = rest[:ng_arr], rest[ng_arr:ng_arr + 2]
        g_outs, s_ref, g_scratch = rest[ng_arr + 2:2 * ng_arr + 2], rest[2 * ng_arr + 2], rest[2 * ng_arr + 3:]
        i = pl.program_id(1)
        if ng_arr:
            g_start, g_forward, g_finish = _gather_phases(g_ins, g_outs, g_scratch[:ng_arr], *g_scratch[ng_arr:],
                                                          g_dtypes)
            last_group = pl.program_id(0) == ngrp - 1
            pl.when((pl.program_id(0) == 0) & (i == 0))(g_start)
            pl.when(last_group & (i == 0))(g_forward)

        @pl.when(i == 0)
        def _():
            s_ref[...] = jnp.zeros_like(s_ref)

        s = s_ref[...]
        s = [s[h] for h in range(grp)]
        q, f, iv, lbv = q_ref[...], f_ref[...], i_ref[...], lb_ref[...]
        masks_v = (sums_ref[...], sums_t_ref[...], pairs_ref[...])
        outs = []
        for c in range(HG_STEP):
            sl = slice(c * CHUNK, (c + 1) * CHUNK)
            valid = (i * rows + c * CHUNK + lax.broadcasted_iota(jnp.int32, (CHUNK, 1), 0)) >= pad
            for h in range(grp):
                st_ref[h, c] = s[h]
            o, s = _hg_chunk(q[sl], f[sl], iv[sl], lbv, s, valid, *masks_v)
            outs.append(o)
        o_ref[...] = jnp.concatenate(outs, axis=0)
        for h in range(grp):
            s_ref[h] = s[h]
        if ng_arr:
            pl.when(last_group & (i == steps - 1))(g_finish)

    masks = _hg_masks()
    grp, ngrp, gw = HG_GROUP, nh // HG_GROUP, HG_GROUP * HEAD_W
    assert n % HG_STEP == 0
    steps, rows = n // HG_STEP, HG_STEP * CHUNK
    blk = lambda off: pl.BlockSpec((rows, gw), lambda h, i: (i, off + h))
    const = lambda a: pl.BlockSpec(a.shape, lambda h, i: (0, 0))
    return pl.pallas_call(
        body, name=name, grid=(ngrp, steps),
        in_specs=[blk(0), blk(ngrp), blk(2 * ngrp), pl.BlockSpec((1, gw), lambda h, i: (0, h))]
        + [const(a) for a in masks] + [pl.BlockSpec(memory_space=pltpu.VMEM)] * ng_arr,
        out_specs=[blk(0), pl.BlockSpec((grp, HG_STEP, HEAD_W, HEAD_W), lambda h, i: (h, i, 0, 0))] + [_ANY] * ng_arr,
        out_shape=[jax.ShapeDtypeStruct((lp, nh * HEAD_W), F32), jax.ShapeDtypeStruct((nh, n, HEAD_W, HEAD_W), F32)]
        + _gather_out_shapes(g_srcs, g_dtypes),
        scratch_shapes=[pltpu.VMEM((grp, HEAD_W, HEAD_W), F32)] + (_gather_scratch(g_srcs, g_dtypes) if ng_arr else []),
        compiler_params=_cp(("arbitrary", "arbitrary"), has_side_effects=bool(ng_arr)),
    )(p1, p1, p1, lb, *masks, *g_srcs)


def _hg_bwd(p1, lb, states, do, pad, *, name, scatter=()):
    lp = p1.shape[0]
    n = lp // CHUNK
    nh = HG_HEADS
    ns = len(scatter)

    def body(q_ref, f_ref, i_ref, lb_ref, st_ref, do_ref, sums_ref, sums_t_ref, pairs_ref, *rest):
        s_ins, (dq_ref, df_ref, di_ref, dlb_ref) = rest[:ns], rest[ns:ns + 4]
        s_outs, ds_ref, s_sems = rest[ns + 4:2 * ns + 4], rest[2 * ns + 4], rest[2 * ns + 5:]
        step = pl.program_id(1)
        i = steps - 1 - step
        if ns:
            s_start, s_finish = _scatter_phases(s_ins, s_outs, *s_sems)
            pl.when((pl.program_id(0) == 0) & (step == 0))(s_start)

        @pl.when(step == 0)
        def _():
            ds_ref[...] = jnp.zeros_like(ds_ref)
            dlb_ref[...] = jnp.zeros_like(dlb_ref)

        q, f, iv, lbv, st, do, dst = q_ref[...], f_ref[...], i_ref[...], lb_ref[...], st_ref[...], do_ref[...], ds_ref[...]
        masks_v = dict(sums=sums_ref[...], sums_t=sums_t_ref[...], pairs=pairs_ref[...])
        vjps = []
        for c in range(HG_STEP):
            sl = slice(c * CHUNK, (c + 1) * CHUNK)
            valid = (i * rows + c * CHUNK + lax.broadcasted_iota(jnp.int32, (CHUNK, 1), 0)) >= pad
            fn = functools.partial(_hg_chunk, valid=valid, **masks_v)
            vjps.append(jax.vjp(fn, q[sl], f[sl], iv[sl], lbv, [st[h, c] for h in range(grp)])[1])
        ds = [dst[h] for h in range(grp)]
        grads = [None] * HG_STEP
        for c in reversed(range(HG_STEP)):
            grads[c] = vjps[c]((do[c * CHUNK:(c + 1) * CHUNK], ds))
            ds = grads[c][4]
        for j, ref in enumerate((dq_ref, df_ref, di_ref)):
            ref[...] = jnp.concatenate([gr[j] for gr in grads], axis=0)
        dlb_ref[...] += sum(gr[3] for gr in grads)
        for h in range(grp):
            ds_ref[h] = ds[h]
        if ns:
            pl.when((pl.program_id(0) == ngrp - 1) & (step == steps - 1))(s_finish)

    masks = _hg_masks()
    grp, ngrp, gw = HG_GROUP, nh // HG_GROUP, HG_GROUP * HEAD_W
    assert n % HG_STEP == 0
    steps, rows = n // HG_STEP, HG_STEP * CHUNK
    blk = lambda off: pl.BlockSpec((rows, gw), lambda h, s: (steps - 1 - s, off + h))
    const = lambda a: pl.BlockSpec(a.shape, lambda h, s: (0, 0))
    w = nh * HEAD_W
    return pl.pallas_call(
        body, name=name, grid=(ngrp, steps),
        in_specs=[blk(0), blk(ngrp), blk(2 * ngrp), pl.BlockSpec((1, gw), lambda h, s: (0, h)),
                  pl.BlockSpec((grp, HG_STEP, HEAD_W, HEAD_W), lambda h, s: (h, steps - 1 - s, 0, 0)), blk(0)]
        + [const(a) for a in masks] + [_ANY] * ns,
        out_specs=[blk(0), blk(0), blk(0), pl.BlockSpec((1, gw), lambda h, s: (0, h))] + [_ANY] * ns,
        out_shape=[jax.ShapeDtypeStruct((lp, w), F32)] * 3 + [jax.ShapeDtypeStruct((1, w), F32)]
        + [jax.ShapeDtypeStruct(s.shape, s.dtype) for s in scatter],
        scratch_shapes=[pltpu.VMEM((grp, HEAD_W, HEAD_W), F32)] + (_scatter_scratch(ns) if ns else []),
        compiler_params=_cp(("arbitrary", "arbitrary"), has_side_effects=bool(ns)),
    )(p1, p1, p1, lb, states, do, *masks, *scatter)


SB_GROUP = 4
SB_FAR = -110.0


def _sb_cat(kind, first_key=0):
    r = lax.broadcasted_iota(jnp.int32, (SB_BLOCK, 2 * SB_BLOCK), 0)
    c = lax.broadcasted_iota(jnp.int32, (SB_BLOCK, 2 * SB_BLOCK), 1)
    tri = {"after": c < r, "incl": r <= c, "before": r < c}[kind]
    m = ((c >= SB_BLOCK) | tri) & (r >= first_key)
    return jnp.where(m, 1.0, 0.0).astype(BF16)


def _sb_cumsum(x, cat):
    return _dot(x.astype(BF16), cat, NN)


def _sb_logsig(z):
    e = jnp.exp(-jnp.abs(z))
    lse = jnp.where(e < 1e-4, e, jnp.log(1.0 + e))
    lsz = jnp.minimum(z, 0.0) - lse
    return lsz, lsz - z, e


def _sb_stack(x, scale=None):
    lane = lax.broadcasted_iota(jnp.int32, (1, HEAD_W), 1)
    if scale is not None:
        x = x * scale
    return jnp.concatenate([jnp.where(lane < SB_DH, x, 0.0), jnp.where(lane >= SB_DH, x, 0.0)], axis=0).astype(BF16)


def _sb_unstack(x):
    lane = lax.broadcasted_iota(jnp.int32, (1, HEAD_W), 1)
    return jnp.where(lane < SB_DH, x[:SB_BLOCK], x[SB_BLOCK:])


def _sb_fwd(p0, pad, *, name, gather=None):
    lp = p0.shape[0]
    nb = lp // SB_BLOCK
    npair = SB_HEADS // 2
    blk0 = AB_SB // HEAD_W
    scale = SB_DH ** -0.5
    gw = SB_GROUP * SB_BLOCK
    assert pad < SB_BLOCK
    g_srcs, g_dtypes = gather if gather is not None else ([], [])
    ng_arr = len(g_srcs)

    def body(q_ref, k_ref, v_ref, *rest):
        g_ins, (o_ref, tot_ref, nproc_ref) = rest[:ng_arr], rest[ng_arr:ng_arr + 3]
        g_outs, g_scratch = rest[ng_arr + 3:2 * ng_arr + 3], rest[2 * ng_arr + 3:]
        first_step = (pl.program_id(0) == 0) & (pl.program_id(1) == 0)
        last_pair = pl.program_id(0) == npair - 1
        if ng_arr:
            g_start, g_forward, g_finish = _gather_phases(g_ins, g_outs, g_scratch[:ng_arr], *g_scratch[ng_arr:],
                                                          g_dtypes)
            pl.when(first_step)(g_start)
            pl.when(last_pair & (pl.program_id(1) == 0))(g_forward)
        i = pl.program_id(1)
        qs = _sb_stack(q_ref[...], scale)
        qpos = i * SB_BLOCK + lax.broadcasted_iota(jnp.int32, (SB_BLOCK, 1), 0)
        qpos = jnp.concatenate([qpos, qpos], axis=0)
        cat = _sb_cat("after")
        cat0 = _sb_cat("after", pad)
        ng = i // SB_GROUP

        def group(off, nblk, first_cat, allowed, carry):
            acc, run = carry
            kg = k_ref[pl.ds(off, nblk * SB_BLOCK), :].astype(BF16)
            vg = v_ref[pl.ds(off, nblk * SB_BLOCK), :].astype(BF16)
            lsz, l1m, _ = _sb_logsig(_dot(qs, kg, NT))
            if allowed is not None:
                l1m = jnp.where(allowed, l1m, 0.0)
            args = [None] * nblk
            for g in reversed(range(nblk)):
                sl = slice(g * SB_BLOCK, (g + 1) * SB_BLOCK)
                al = _sb_cumsum(l1m[:, sl], first_cat if g == 0 else cat)
                args[g] = lsz[:, sl] + al[:, :SB_BLOCK] + run
                run = run + al[:, SB_BLOCK:]
            wgt = jnp.exp(jnp.concatenate(args, axis=1))
            if allowed is not None:
                wgt = jnp.where(allowed, wgt, 0.0)
            return acc + _dot(wgt.astype(BF16), vg, NN), run

        def below(t, carry):
            gi = ng - 1 - t
            return group(pl.multiple_of(gi * gw, gw), SB_GROUP, jnp.where(gi == 0, cat0, cat), None, carry)

        top = ng * gw

        def top_group(nblk, carry):
            off = pl.multiple_of(jnp.minimum(top, lp - nblk * SB_BLOCK), SB_BLOCK)
            kpos = off + lax.broadcasted_iota(jnp.int32, (1, nblk * SB_BLOCK), 1)
            return group(off, nblk, cat, (kpos < qpos) & (kpos >= pad) & (kpos >= top), carry)

        zero = (jnp.zeros((2 * SB_BLOCK, HEAD_W), F32), jnp.zeros((2 * SB_BLOCK, HEAD_W), F32))
        carry = lax.cond(i - ng * SB_GROUP < SB_GROUP // 2, functools.partial(top_group, SB_GROUP // 2),
                         functools.partial(top_group, SB_GROUP), zero)
        used, acc, run = lax.while_loop(lambda s: (s[0] < ng) & (jnp.max(s[2]) > SB_FAR),
                                        lambda s: (s[0] + 1, *below(s[0], (s[1], s[2]))), (jnp.int32(0), *carry))
        o_ref[...] = _sb_unstack(acc)
        tot_ref[...] = _sb_unstack(run)
        nproc_ref[pl.program_id(0), i] = used.astype(F32)
        if ng_arr:
            pl.when(last_pair & (pl.program_id(1) == nb - 1))(g_finish)

    full = lambda c0: pl.BlockSpec((lp, HEAD_W), lambda p, i: (0, c0 + p))
    out = pl.BlockSpec((SB_BLOCK, HEAD_W), lambda p, i: (i, p))
    return pl.pallas_call(
        body, name=name, grid=(npair, nb),
        in_specs=[pl.BlockSpec((SB_BLOCK, HEAD_W), lambda p, i: (i, blk0 + p)), full(blk0 + npair), full(blk0 + 2 * npair)]
        + [pl.BlockSpec(memory_space=pltpu.VMEM)] * ng_arr,
        out_specs=[out, out, pl.BlockSpec(memory_space=pltpu.SMEM)] + [_ANY] * ng_arr,
        out_shape=[jax.ShapeDtypeStruct((lp, npair * HEAD_W), F32)] * 2 + [jax.ShapeDtypeStruct((npair, nb), F32)]
        + _gather_out_shapes(g_srcs, g_dtypes),
        scratch_shapes=_gather_scratch(g_srcs, g_dtypes) if ng_arr else [],
        compiler_params=_cp(("arbitrary", "arbitrary"), has_side_effects=bool(ng_arr)),
    )(p0, p0, p0, *g_srcs)


def _sb_bwd(p0, tot, nproc, dsrc, d_blk0, pad, *, name, scatter=()):
    lp = p0.shape[0]
    nb = lp // SB_BLOCK
    npair = SB_HEADS // 2
    blk0 = AB_SB // HEAD_W
    scale = SB_DH ** -0.5
    gw = SB_GROUP * SB_BLOCK
    assert pad < SB_BLOCK
    ns = len(scatter)

    def body(q_ref, k_ref, v_ref, tot_ref, nproc_ref, do_ref, *rest):
        s_ins, (dq_ref, dkt_ref, dvt_ref) = rest[:ns], rest[ns:ns + 3]
        s_outs, s_sems = rest[ns + 3:2 * ns + 3], rest[2 * ns + 3:]
        if ns:
            s_start, s_finish = _scatter_phases(s_ins, s_outs, *s_sems)
            pl.when((pl.program_id(0) == 0) & (pl.program_id(1) == 0))(s_start)
        i = pl.program_id(1)

        @pl.when(i == 0)
        def _():
            dkt_ref[...] = jnp.zeros_like(dkt_ref)
            dvt_ref[...] = jnp.zeros_like(dvt_ref)

        qs = _sb_stack(q_ref[...], scale)
        dos = _sb_stack(do_ref[...])
        qst, dost = qs.T, dos.T
        totv = tot_ref[...]
        ones = jnp.ones((1, HEAD_W), F32)
        tots = jnp.concatenate([totv[:, 0:1] * ones, totv[:, SB_DH:SB_DH + 1] * ones], axis=0)
        qpos = i * SB_BLOCK + lax.broadcasted_iota(jnp.int32, (SB_BLOCK, 1), 0)
        qpos = jnp.concatenate([qpos, qpos], axis=0)
        incl, incl0 = _sb_cat("incl"), _sb_cat("incl", pad)
        before = _sb_cat("before")
        ng = i // SB_GROUP
        used = jnp.clip(nproc_ref[pl.program_id(0), i].astype(jnp.int32), 0, ng)

        def dscore(z, e, ev, dl1m):
            r = 1.0 / (1.0 + e)
            sg = jnp.where(z >= 0, r, e * r)
            return ev * (1.0 - sg) - dl1m * sg

        def group(off, nblk, first_incl, allowed, carry):
            dq, prun, erun = carry
            width = nblk * SB_BLOCK
            kg = k_ref[pl.ds(off, width), :].astype(BF16)
            vg = v_ref[pl.ds(off, width), :].astype(BF16)
            z = _dot(qs, kg, NT)
            lsz, l1m, e = _sb_logsig(z)
            if allowed is not None:
                l1m = jnp.where(allowed, l1m, 0.0)
            dwgt = _dot(dos, vg, NT)
            dzs = [None] * nblk
            wgts = [None] * nblk
            for g in range(nblk):
                sl = slice(g * SB_BLOCK, (g + 1) * SB_BLOCK)
                al = _sb_cumsum(l1m[:, sl], first_incl if g == 0 else incl)
                wgt = jnp.exp(jnp.minimum(lsz[:, sl] + (tots - prun - al[:, :SB_BLOCK]), 0.0))
                if allowed is not None:
                    wgt = jnp.where(allowed[:, sl], wgt, 0.0)
                prun = prun + al[:, SB_BLOCK:]
                ev = wgt * dwgt[:, sl]
                el = _sb_cumsum(ev, before)
                dzs[g] = dscore(z[:, sl], e[:, sl], ev, erun + el[:, :SB_BLOCK])
                erun = erun + el[:, SB_BLOCK:]
                wgts[g] = wgt
            dz = jnp.concatenate(dzs, axis=1)
            if allowed is not None:
                dz = jnp.where(allowed, dz, 0.0)
            dz = dz.astype(BF16)
            wg = jnp.concatenate(wgts, axis=1).astype(BF16)
            dkt_ref[:, pl.ds(off, width)] += _dot(qst, dz, NN)
            dvt_ref[:, pl.ds(off, width)] += _dot(dost, wg, NN)
            return dq + _dot(dz, kg, NN), prun, erun

        def below(gi, carry):
            return group(pl.multiple_of(gi * gw, gw), SB_GROUP, jnp.where(gi == 0, incl0, incl), None, carry)

        zero = tuple(jnp.zeros((2 * SB_BLOCK, HEAD_W), F32) for _ in range(3))
        carry = lax.fori_loop(ng - used, ng, below, zero)
        top = ng * gw

        def top_group(nblk, carry):
            off = pl.multiple_of(jnp.minimum(top, lp - nblk * SB_BLOCK), SB_BLOCK)
            kpos = off + lax.broadcasted_iota(jnp.int32, (1, nblk * SB_BLOCK), 1)
            return group(off, nblk, incl, (kpos < qpos) & (kpos >= pad) & (kpos >= top), carry)

        dq, _, _ = lax.cond(i - ng * SB_GROUP < SB_GROUP // 2, functools.partial(top_group, SB_GROUP // 2),
                            functools.partial(top_group, SB_GROUP), carry)
        dq_ref[...] = _sb_unstack(dq) * scale
        if ns:
            pl.when((pl.program_id(0) == npair - 1) & (pl.program_id(1) == nb - 1))(s_finish)

    full = lambda c0: pl.BlockSpec((lp, HEAD_W), lambda p, i: (0, c0 + p))
    qb = lambda c0: pl.BlockSpec((SB_BLOCK, HEAD_W), lambda p, i: (i, c0 + p))
    tr = pl.BlockSpec((HEAD_W, lp), lambda p, i: (p, 0))
    return pl.pallas_call(
        body, name=name, grid=(npair, nb),
        in_specs=[qb(blk0), full(blk0 + npair), full(blk0 + 2 * npair), qb(0), pl.BlockSpec(memory_space=pltpu.SMEM),
                  qb(d_blk0)] + [_ANY] * ns,
        out_specs=[qb(0), tr, tr] + [_ANY] * ns,
        out_shape=[jax.ShapeDtypeStruct((lp, npair * HEAD_W), F32)]
        + [jax.ShapeDtypeStruct((npair * HEAD_W, lp), F32)] * 2
        + [jax.ShapeDtypeStruct(s.shape, s.dtype) for s in scatter],
        scratch_shapes=_scatter_scratch(ns) if ns else [],
        compiler_params=_cp(("arbitrary", "arbitrary"), has_side_effects=bool(ns)),
    )(p0, p0, p0, tot, nproc, dsrc, *scatter)


def _local_step(h0, target, pad, wts, hooks=None):
    lp = h0.shape[0]
    tm = _row_tile(lp, 1056)
    tkl = tm
    tml = _row_tile(lp, 528)
    d = D_MODEL
    mm = _mm
    mmw = functools.partial(_mm, out_dtype=BF16)
    g = {}

    h0_b = h0.astype(BF16)
    p0 = mm(h0_b, wts["w_ab"], "NN", tm=tm, tn=768, tk=d, name="l0_in_proj")
    ob, sb_tot, sb_used, *gathered = _sb_fwd(p0, pad, name="sb_fwd", gather=hooks["gather_a"] if hooks else None)
    if hooks:
        wts = {**wts, **hooks["weights_a"](gathered)}
    qkv = _gdn_pre_fwd(p0, wts["conv_w"], pad, name="gdn_pre_fwd")
    oa_raw, gdn_states, *gathered = _gdn_fwd(qkv, p0, wts["alog_v"], wts["dtb_v"], pad, name="gdn_fwd",
                                             gather=hooks["gather_b"] if hooks else None)
    if hooks:
        wts = {**wts, **hooks["weights_b"](gathered)}
    rows = lambda a, n: a.reshape(N_DEV, n // N_DEV, d)
    parts = g["parts"] = {}
    oab = _gate_fwd(oa_raw, p0, AB_Z // HEAD_W, wts["ab_gn"], ob, heads=GDN_HEADS, name="gdn_gate_fwd")
    ln = lambda kind, layer: (wts[f"ln_{kind}_g"][layer], wts[f"ln_{kind}_b"][layer])
    pre_mix0, h0a, h0a_b = mm(oab, wts["w_out0"], "NN", tm=tml, tn=d, tk=d, epi="ln", c=h0, scale=DN_ALPHA,
                              ln=ln("mix", 0), name="l0_out_proj")
    u0, act0 = mm(h0a_b, wts["w1"][0], "NN", tm=tm, tn=512, tk=d, b_dev=True, epi="relu2_copy", name="mlp0_up")
    pre_ffn0, h0b, h0b_b = mm(act0, wts["w2"][0], "NN", tm=tml, tn=d, tk=d, epi="ln", c=h0a, scale=DN_ALPHA,
                              ln=ln("ffn", 0), name="mlp0_down")
    p1 = mm(h0b_b, wts["w_c"], "NN", tm=tm, tn=512, tk=d, b_dev=True, name="l1_in_proj")
    oc_raw, hg_states, *gathered = _hg_fwd(p1, wts["lb"], pad, name="hg_fwd",
                                           gather=hooks["gather_c"] if hooks else None)
    if hooks:
        third = hooks["weights_c"](gathered)
        wts = {**wts, "w1": wts["w1"] + third["w1"], "w2": wts["w2"] + third["w2"]}
    oc = _gate_fwd(oc_raw, p1, 3 * HG_HEADS, wts["c_gn"], oc_raw, heads=HG_HEADS, name="hg_gate_fwd")
    pre_mix1, h1a, h1a_b = mm(oc, wts["w_out1"], "NN", tm=tml, tn=d, tk=d, epi="ln", c=h0b, scale=DN_ALPHA,
                              ln=ln("mix", 1), name="l1_out_proj")
    u1, act1 = mm(h1a_b, wts["w1"][1], "NN", tm=tm, tn=512, tk=d, b_dev=True, epi="relu2_copy", name="mlp1_up")
    pre_ffn1, h1b, _ = mm(act1, wts["w2"][1], "NN", tm=tml, tn=d, tk=d, epi="ln", c=h1a, scale=DN_ALPHA,
                          ln=ln("ffn", 1), name="mlp1_down")
    dy, loss_vec = _loss_head(h1b, target, name="loss_head")

    def mlp_bwd(layer, h_in_b, u, act, dpre, dpre_b, pre_mix):
        du = mm(dpre_b, wts["w2"][layer], "NT", tm=tm, tn=1024, tk=d, epi="relu2grad", c=u, out_dtype=BF16,
                name=f"mlp{layer}_d_hidden")
        dw2 = mmw(act, dpre_b, "TN", tm=1024, tn=1024, tk=tkl, name=f"mlp{layer}_dw2")
        dw1 = mmw(h_in_b, du, "TN", tm=1024, tn=512, tk=tkl, out_dev=True, name=f"mlp{layer}_dw1")
        return (*mm(du, k_major(wts["w1"][layer]), "NT", tm=tml, tn=1024, tk=2048, epi="ln_bwd", c=dpre, scale=DN_ALPHA,
                    ln=(pre_mix, wts["ln_mix_g"][layer]), name=f"mlp{layer}_d_in"), dw1, dw2)

    k_major = lambda wd: wd.transpose(1, 0, 2).reshape(wd.shape[1], -1)

    ln_ffn_dg, ln_ffn_db, ln_mix_dg, ln_mix_db, dw1s, dw2s = ([None, None] for _ in range(6))
    dpre, dpre_b, ln_ffn_dg[1], ln_ffn_db[1] = _ln_bwd(pre_ffn1, wts["ln_ffn_g"][1], dy, name="ln_ffn1_bwd")
    dpre, dpre_b, ln_mix_dg[1], ln_mix_db[1], dw1s[1], dw2s[1] = mlp_bwd(1, h1a_b, u1, act1, dpre, dpre_b, pre_mix1)
    g["c_w_out"] = mmw(oc, dpre_b, "TN", tm=1024, tn=1024, tk=tkl, name="l1_dw_out")
    doc = mm(dpre_b, wts["w_out1"], "NT", tm=tm, tn=1024, tk=d, name="l1_d_gate")
    doc_raw, dz1, g["c_gn"] = _gate_bwd(oc_raw, p1, 3 * HG_HEADS, wts["c_gn"], doc, heads=HG_HEADS, name="hg_gate_bwd")
    ready = [dw1s[1], rows(dw2s[1], D_FF), rows(g["c_w_out"], d)] if hooks else ()
    dq1, df1, di1, g["lb"], *got = _hg_bwd(p1, wts["lb"], hg_states, doc_raw, pad, name="hg_bwd", scatter=ready)
    parts.update(zip(("mlp_w1_1", "mlp_w2_1", "c_w_out"), got))
    dp1 = jnp.concatenate([dq1, df1, di1, dz1], axis=1).astype(BF16)
    g["c_w_in"] = mmw(h0b_b, dp1, "TN", tm=1024, tn=512, tk=tkl, out_dev=True, name="l1_dw_in")
    dpre, dpre_b, ln_ffn_dg[0], ln_ffn_db[0] = mm(
        dp1, k_major(wts["w_c"]), "NT", tm=tml, tn=1024, tk=2048, epi="ln_bwd", c=dpre, scale=DN_ALPHA,
        ln=(pre_ffn0, wts["ln_ffn_g"][0]), name="l1_d_in")
    dpre, dpre_b, ln_mix_dg[0], ln_mix_db[0], dw1s[0], dw2s[0] = mlp_bwd(0, h0a_b, u0, act0, dpre, dpre_b, pre_mix0)
    g["ab_w_out"] = mmw(oab, dpre_b, "TN", tm=1024, tn=1024, tk=tkl, name="l0_dw_out")
    doab = mm(dpre_b, wts["w_out0"], "NT", tm=tm, tn=1024, tk=d, name="l0_d_gate")
    doa_raw, dz0, g["ab_gn"] = _gate_bwd(oa_raw, p0, AB_Z // HEAD_W, wts["ab_gn"], doab, heads=GDN_HEADS,
                                         name="gdn_gate_bwd")
    ready = [g["c_w_in"]] if hooks else ()
    dqb, dkb_t, dvb_t, *got = _sb_bwd(p0, sb_tot, sb_used, doab, GDN_HEADS, pad, name="sb_bwd", scatter=ready)
    parts.update(zip(("c_w_in",), got))
    dkb, dvb = dkb_t.T, dvb_t.T
    ready = [dw1s[0], rows(dw2s[0], D_FF), rows(g["ab_w_out"], d)] if hooks else ()
    dqn, dkn, dvn, dba, g["alog_v"], g["dtb_v"], *got = _gdn_bwd(qkv, p0, wts["alog_v"], wts["dtb_v"], gdn_states,
                                                                 doa_raw, pad, name="gdn_bwd", scatter=ready)
    parts.update(zip(("mlp_w1_0", "mlp_w2_0", "ab_w_out"), got))
    dconv_in, g["conv_w"] = _gdn_pre_bwd(p0, wts["conv_w"], jnp.concatenate([dqn, dkn, dvn], axis=1), pad,
                                         name="gdn_pre_bwd")
    dp0 = jnp.concatenate([dconv_in, dz0, dqb, dkb, dvb, dba, jnp.zeros((lp, AB_CAT - AB_BA - HEAD_W), F32)],
                          axis=1).astype(BF16)
    g["w_ab"] = mmw(h0_b, dp0, "TN", tm=1024, tn=768, tk=tkl, name="l0_dw_in")
    last = ()
    if hooks:
        gab, ba0 = g["w_ab"], AB_Z + GDN_HEADS * HEAD_W
        gab = jnp.concatenate([gab[:, :ba0], gab[:, AB_BA:AB_BA + 2 * GDN_HEADS], gab[:, ba0:AB_BA]], axis=1)
        last = [gab.reshape(d, N_DEV, AB_IN // N_DEV).transpose(1, 0, 2)]
    res = mm(dp0, wts["w_ab"], "NT", tm=tm, tn=1024, tk=1920, epi="add", c=dpre, scale=DN_ALPHA, scatter=last,
             name="l0_d_in")
    dh0 = res[0] if last else res
    parts.update(zip(("ab_w_in",), res[1:] if last else ()))

    g["w1"], g["w2"] = dw1s, dw2s
    g["ln_mix_g"] = jnp.concatenate(ln_mix_dg, axis=0)
    g["ln_mix_b"] = jnp.concatenate(ln_mix_db, axis=0)
    g["ln_ffn_g"] = jnp.concatenate(ln_ffn_dg, axis=0)
    g["ln_ffn_b"] = jnp.concatenate(ln_ffn_db, axis=0)
    return loss_vec, dh0, g


N_CHIP = N_DEV // 2


def _place():
    x, y, c = lax.axis_index("x"), lax.axis_index("y"), lax.axis_index("c")
    return x, y, c, 2 * x + y


def _chip_dev(chip, core):
    return (chip // 2, chip % 2, core)


def _remote(src, dst, send_sem, recv_sem, dev):
    return pltpu.make_async_remote_copy(src_ref=src, dst_ref=dst, send_sem=send_sem, recv_sem=recv_sem,
                                        device_id=dev, device_id_type=pl.DeviceIdType.MESH)


_ANY = pl.BlockSpec(memory_space=pl.ANY)


def _gather(srcs, dtypes, *, name):
    n = len(srcs)

    def body(*refs):
        start, forward, finish = _gather_phases(refs[:n], refs[n:2 * n], refs[2 * n:3 * n], *refs[3 * n:], dtypes)
        start()
        forward()
        finish()

    return pl.pallas_call(
        body, name=name, in_specs=[pl.BlockSpec(memory_space=pltpu.VMEM)] * n, out_specs=[_ANY] * n,
        out_shape=_gather_out_shapes(srcs, dtypes), scratch_shapes=_gather_scratch(srcs, dtypes),
        compiler_params=_cp(has_side_effects=True),
    )(*srcs)


def _gather_out_shapes(srcs, dtypes):
    return [jax.ShapeDtypeStruct((N_DEV, *s.shape), dt) for s, dt in zip(srcs, dtypes)]


def _gather_scratch(srcs, dtypes):
    n = len(srcs)
    return [pltpu.VMEM(s.shape, dt) for s, dt in zip(srcs, dtypes)] + [
        pltpu.SemaphoreType.DMA((n, 2 * N_CHIP - 1)), pltpu.SemaphoreType.DMA((n, 2 * N_CHIP - 1)),
        pltpu.SemaphoreType.DMA((n,))]


def _gather_phases(ins, outs, stages, send_sems, recv_sems, local_sems, dtypes):
    n = len(ins)
    x, y, c, chip = _place()
    me = 2 * chip + c
    sibling = (x, y, 1 - c)

    def own(i):
        cps = [_remote(stages[i], outs[i].at[me], send_sems.at[i, 0], recv_sems.at[i, 0], sibling)]
        for j in range(1, N_CHIP):
            cps.append(_remote(stages[i], outs[i].at[me], send_sems.at[i, j], recv_sems.at[i, j],
                               _chip_dev(jnp.bitwise_xor(chip, j), c)))
        return cps

    def local(i):
        return pltpu.make_async_copy(stages[i], outs[i].at[me], local_sems.at[i])

    def passed_on(i, j):
        slot = outs[i].at[2 * jnp.bitwise_xor(chip, j) + c]
        return _remote(slot, slot, send_sems.at[i, N_CHIP - 1 + j], recv_sems.at[i, N_CHIP - 1 + j], sibling)

    def start():
        for i in range(n):
            stages[i][...] = ins[i][...].astype(dtypes[i])
            local(i).start()
            for cp in own(i):
                cp.start()

    def forward():
        for i in range(n):
            for j in range(1, N_CHIP):
                own(i)[j].wait_recv()
                passed_on(i, j).start()

    def finish():
        for i in range(n):
            own(i)[0].wait_recv()
            for j in range(1, N_CHIP):
                passed_on(i, j).wait_recv()
        for i in range(n):
            for cp in own(i):
                cp.wait_send()
            for j in range(1, N_CHIP):
                passed_on(i, j).wait_send()
            local(i).wait()

    return start, forward, finish


def _scatter_scratch(n):
    return [pltpu.SemaphoreType.DMA((n, N_DEV - 1)), pltpu.SemaphoreType.DMA((n, N_DEV - 1)),
            pltpu.SemaphoreType.DMA((n,))]


def _scatter_phases(ins, outs, send_sems, recv_sems, local_sems):
    n = len(ins)
    _, _, c, chip = _place()
    me = 2 * chip + c

    def copies():
        cps = []
        for i in range(n):
            cps.append(pltpu.make_async_copy(ins[i].at[me], outs[i].at[me], local_sems.at[i]))
            for k in range(1, N_DEV):
                peer = jnp.bitwise_xor(me, k)
                cps.append(_remote(ins[i].at[peer], outs[i].at[me], send_sems.at[i, k - 1], recv_sems.at[i, k - 1],
                                   _chip_dev(peer // 2, peer % 2)))
        return cps

    def start():
        for cp in copies():
            cp.start()

    def finish():
        for cp in copies():
            cp.wait()

    return start, finish


def _adamw(w, parts, m, v, *, name):
    r, c = w.shape
    s = parts.shape[0]
    tm = _row_tile(r, 128) if r % 8 == 0 else r
    c1 = 1.0 - ADAM_B1 ** ADAM_STEP
    c2 = 1.0 - ADAM_B2 ** ADAM_STEP

    def body(w_ref, p_ref, m_ref, v_ref, g_ref, d_ref, m2_ref, v2_ref):
        g = p_ref[0].astype(F32)
        for j in range(1, s):
            g = g + p_ref[j].astype(F32)
        m2 = ADAM_B1 * m_ref[...] + (1.0 - ADAM_B1) * g
        v2 = ADAM_B2 * v_ref[...] + (1.0 - ADAM_B2) * jnp.square(g)
        g_ref[...] = g
        m2_ref[...] = m2
        v2_ref[...] = v2
        d_ref[...] = -ADAM_LR * ((m2 / c1) / (jnp.sqrt(v2 / c2) + ADAM_EPS) + ADAM_WD * w_ref[...])

    blk = pl.BlockSpec((tm, c), lambda i: (i, 0))
    return pl.pallas_call(
        body, name=name, grid=(r // tm,),
        in_specs=[blk, pl.BlockSpec((s, tm, c), lambda i: (0, i, 0)), blk, blk], out_specs=[blk] * 4,
        out_shape=[jax.ShapeDtypeStruct((r, c), F32)] * 4, compiler_params=_cp(("parallel",)),
    )(w, parts, m, v)


_WEIGHTS = ("meta_tokens", "ab_w_in", "ab_conv_w", "ab_a_log", "ab_dt_bias", "ab_gnorm_g", "ab_w_out", "c_w_in",
            "c_lb_raw", "c_gnorm_g", "c_w_out", "ln_mix_g", "ln_mix_b", "mlp_w1", "mlp_w2", "ln_ffn_g", "ln_ffn_b")
_PACK_ROWS = (("ln_mix_g", 0), ("ln_mix_b", 2), ("ln_ffn_g", 4), ("ln_ffn_b", 6), ("c_lb_raw", 8))
_PACK_MISC_ROW = 10
_PACK_MISC = (("ab_gnorm_g", 0, 128), ("c_gnorm_g", 128, 128), ("ab_a_log", 256, GDN_HEADS), ("ab_dt_bias", 260, GDN_HEADS))
_PACK_N = 16
_SMALL_META = 16
_SMALL_CONV = 32
_SMALL_N = 40


def _pack_replicated(p):
    rows = jnp.zeros((_PACK_N, D_MODEL), F32)
    for name, r0 in _PACK_ROWS:
        rows = rows.at[r0:r0 + 2].set(p[name])
    for name, c0, width in _PACK_MISC:
        rows = rows.at[_PACK_MISC_ROW, c0:c0 + width].set(p[name].reshape(width))
    return rows


def _unpack_replicated(rows, like):
    out = {}
    for name, r0 in _PACK_ROWS:
        out[name] = rows[r0:r0 + 2]
    for name, c0, width in _PACK_MISC:
        out[name] = rows[_PACK_MISC_ROW, c0:c0 + width].reshape(like[name].shape)
    return out


def _lower_bound(c_lb_raw):
    lb_all = jnp.cumsum(jax.nn.softmax(c_lb_raw.astype(F32), axis=0), axis=0)
    return (lb_all - lb_all[0:1])[1].reshape(1, -1)


def kernel(x, meta_tokens, ab_w_in, ab_conv_w, ab_a_log, ab_dt_bias, ab_gnorm_g, ab_w_out, c_w_in, c_lb_raw, c_gnorm_g, c_w_out, ln_mix_g, ln_mix_b, mlp_w1, mlp_w2, ln_ffn_g, ln_ffn_b, loss_target, m_meta_tokens, m_ab_w_in, m_ab_conv_w, m_ab_a_log, m_ab_dt_bias, m_ab_gnorm_g, m_ab_w_out, m_c_w_in, m_c_lb_raw, m_c_gnorm_g, m_c_w_out, m_ln_mix_g, m_ln_mix_b, m_mlp_w1, m_mlp_w2, m_ln_ffn_g, m_ln_ffn_b, v_meta_tokens, v_ab_w_in, v_ab_conv_w, v_ab_a_log, v_ab_dt_bias, v_ab_gnorm_g, v_ab_w_out, v_c_w_in, v_c_lb_raw, v_c_gnorm_g, v_c_w_out, v_ln_mix_g, v_ln_mix_b, v_mlp_w1, v_mlp_w2, v_ln_ffn_g, v_ln_ffn_b):
    w = dict(zip(_WEIGHTS, (meta_tokens, ab_w_in, ab_conv_w, ab_a_log, ab_dt_bias, ab_gnorm_g, ab_w_out, c_w_in, c_lb_raw,
                            c_gnorm_g, c_w_out, ln_mix_g, ln_mix_b, mlp_w1, mlp_w2, ln_ffn_g, ln_ffn_b)))
    mom = dict(zip(_WEIGHTS, (m_meta_tokens, m_ab_w_in, m_ab_conv_w, m_ab_a_log, m_ab_dt_bias, m_ab_gnorm_g, m_ab_w_out,
                              m_c_w_in, m_c_lb_raw, m_c_gnorm_g, m_c_w_out, m_ln_mix_g, m_ln_mix_b, m_mlp_w1, m_mlp_w2,
                              m_ln_ffn_g, m_ln_ffn_b)))
    var = dict(zip(_WEIGHTS, (v_meta_tokens, v_ab_w_in, v_ab_conv_w, v_ab_a_log, v_ab_dt_bias, v_ab_gnorm_g, v_ab_w_out,
                              v_c_w_in, v_c_lb_raw, v_c_gnorm_g, v_c_w_out, v_ln_mix_g, v_ln_mix_b, v_mlp_w1, v_mlp_w2,
                              v_ln_ffn_g, v_ln_ffn_b)))
    me = 4 * lax.axis_index("x") + 2 * lax.axis_index("y") + lax.axis_index("c")
    seq = x.shape[1]
    pad = (-(N_META + seq)) % SB_BLOCK
    lp = pad + N_META + seq
    meta_w = D_MODEL // N_DEV
    conv_w_all = 2 * GDN_HEADS * HEAD_W + GDN_HEADS * HEAD_W
    conv_w_mine = conv_w_all // N_DEV

    g_meta, g_conv, g_ab_in = _gather([w["meta_tokens"], w["ab_conv_w"][0], w["ab_w_in"][0]], [F32, F32, BF16],
                                      name="gather_weights_first")
    meta_full = g_meta.transpose(1, 0, 2).reshape(N_META, D_MODEL)
    conv_full = g_conv.transpose(1, 0, 2).reshape(CONV_K, conv_w_all)
    ab_full = g_ab_in.transpose(1, 0, 2).reshape(D_MODEL, AB_IN)
    ba0 = AB_Z + 512
    w_ab = jnp.concatenate([ab_full[:, :ba0], ab_full[:, ba0 + 2 * GDN_HEADS:], ab_full[:, ba0:ba0 + 2 * GDN_HEADS],
                            jnp.zeros((D_MODEL, AB_CAT - AB_IN), BF16)], axis=1)
    vec128 = lambda p: jnp.zeros((1, HEAD_W), F32).at[0, :GDN_HEADS].set(p.reshape(GDN_HEADS))
    wts = dict(
        w_ab=w_ab, conv_w=conv_full, alog_v=vec128(w["ab_a_log"]), dtb_v=vec128(w["ab_dt_bias"]),
        ab_gn=w["ab_gnorm_g"][0], lb=_lower_bound(w["c_lb_raw"]), c_gn=w["c_gnorm_g"][0],
        ln_mix_g=w["ln_mix_g"], ln_mix_b=w["ln_mix_b"], ln_ffn_g=w["ln_ffn_g"], ln_ffn_b=w["ln_ffn_b"])

    def weights_a(gathered):
        g_ab_out, g_w1, g_w2 = gathered
        return dict(w_out0=g_ab_out.reshape(D_MODEL, D_MODEL), w1=[g_w1], w2=[g_w2.reshape(D_FF, D_MODEL)])

    def weights_b(gathered):
        g_c_in, g_c_out = gathered
        return dict(w_c=g_c_in, w_out1=g_c_out.reshape(D_MODEL, D_MODEL))

    def weights_c(gathered):
        g_w1, g_w2 = gathered
        return dict(w1=[g_w1], w2=[g_w2.reshape(D_FF, D_MODEL)])

    hooks = dict(
        gather_a=([w["ab_w_out"][0], w["mlp_w1"][0], w["mlp_w2"][0]], [BF16] * 3), weights_a=weights_a,
        gather_b=([w["c_w_in"][0], w["c_w_out"][0]], [BF16] * 2), weights_b=weights_b,
        gather_c=([w["mlp_w1"][1], w["mlp_w2"][1]], [BF16] * 2), weights_c=weights_c)

    h0 = jnp.concatenate([jnp.zeros((pad, D_MODEL), F32), meta_full, x[0]], axis=0)
    loss_vec, dh0, g = _local_step(h0, loss_target[0], pad, wts, hooks)
    loss = lax.psum(jnp.sum(loss_vec), ("x", "y", "c"))
    grad_x = dh0[lp - seq:][None]

    _, lb_vjp = jax.vjp(_lower_bound, w["c_lb_raw"])
    rep_part = _pack_replicated(dict(
        ln_mix_g=g["ln_mix_g"], ln_mix_b=g["ln_mix_b"], ln_ffn_g=g["ln_ffn_g"], ln_ffn_b=g["ln_ffn_b"],
        c_lb_raw=lb_vjp(g["lb"])[0], ab_gnorm_g=g["ab_gn"], c_gnorm_g=g["c_gn"],
        ab_a_log=g["alog_v"][0, :GDN_HEADS], ab_dt_bias=g["dtb_v"][0, :GDN_HEADS]))
    small = jnp.concatenate([rep_part, dh0[pad:pad + N_META], g["conv_w"].reshape(-1, D_MODEL),
                             jnp.zeros((_SMALL_N - _SMALL_CONV - CONV_K * conv_w_all // D_MODEL, D_MODEL), F32)], axis=0)
    (small_all,) = _gather([small], [F32], name="gather_small_grads")
    rep_out = _adamw(_pack_replicated(w), small_all[:, :_PACK_N], _pack_replicated(mom), _pack_replicated(var),
                     name="adamw_replicated")
    meta_parts = lax.dynamic_slice_in_dim(small_all[:, _SMALL_META:_SMALL_META + N_META], me * meta_w, meta_w, axis=2)
    meta_out = _adamw(w["meta_tokens"], meta_parts, mom["meta_tokens"], var["meta_tokens"], name="adamw_meta")
    conv_parts = small_all[:, _SMALL_CONV:_SMALL_CONV + CONV_K * conv_w_all // D_MODEL].reshape(N_DEV, CONV_K, conv_w_all)
    conv_parts = lax.dynamic_slice_in_dim(conv_parts, me * conv_w_mine, conv_w_mine, axis=2)
    conv_out = _adamw(w["ab_conv_w"][0], conv_parts, mom["ab_conv_w"][0], var["ab_conv_w"][0], name="adamw_conv")

    parts = g["parts"]
    big = [("ab_w_in", 0, parts["ab_w_in"]), ("ab_w_out", 0, parts["ab_w_out"]), ("mlp_w1", 0, parts["mlp_w1_0"]),
           ("mlp_w2", 0, parts["mlp_w2_0"]), ("c_w_in", 0, parts["c_w_in"]), ("c_w_out", 0, parts["c_w_out"]),
           ("mlp_w1", 1, parts["mlp_w1_1"]), ("mlp_w2", 1, parts["mlp_w2_1"])]
    big_out = {}
    for name, l, p in big:
        res = _adamw(w[name][l], p, mom[name][l], var[name][l], name=f"adamw_{name}{l}")
        big_out.setdefault(name, []).append(res)

    rep = [_unpack_replicated(r, w) for r in rep_out]
    outs = {}
    for name in _WEIGHTS:
        if name == "meta_tokens":
            outs[name] = list(meta_out)
        elif name == "ab_conv_w":
            outs[name] = [o[None] for o in conv_out]
        elif name in big_out:
            res = big_out[name]
            outs[name] = [o[None] for o in res[0]] if len(res) == 1 else [jnp.stack(pair) for pair in zip(*res)]
        else:
            outs[name] = [r[name] for r in rep]
    flat = [loss, grad_x]
    for kind in range(4):
        flat += [outs[name][kind] for name in _WEIGHTS]
    return tuple(flat)
```

```python
import functools

import jax
import jax.numpy as jnp
from jax import lax
from jax.experimental import pallas as pl
from jax.experimental.pallas import tpu as pltpu

F32 = jnp.float32
BF16 = jnp.bfloat16

N_DEV = 8
D_MODEL = 1024
N_META = 16
D_FF = 4096
DEPTH = 2
GDN_HEADS = 4
SB_HEADS = 8
SB_DH = 64
HG_HEADS = 8
HEAD_W = 128
CHUNK = 64
SB_BLOCK = 128
CONV_K = 4
DN_ALPHA = float((2 * DEPTH) ** 0.25)
LN_EPS = 1e-5
RMS_EPS = 1e-6
L2_EPS = 1e-6
ADAM_LR, ADAM_B1, ADAM_B2, ADAM_EPS, ADAM_WD, ADAM_STEP = 0.001, 0.9, 0.999, 1e-08, 0.01, 10

AB_Z = 1536
AB_SB = 2048
AB_BA = 3584
AB_CAT = 3840
AB_IN = 3592

VMEM_LIMIT = 56 * 1024 * 1024


def _cp(sem=None, **kw):
    if sem is not None:
        kw["dimension_semantics"] = sem
    return pltpu.CompilerParams(vmem_limit_bytes=VMEM_LIMIT, **kw)


def _row_tile(n, want):
    best = 8
    for t in range(8, min(n, want) + 1, 8):
        if n % t == 0:
            best = t
    return best


@jax.custom_vjp
def _sigmoid(x):
    e = jnp.exp(-jnp.abs(x))
    r = 1.0 / (1.0 + e)
    return jnp.where(x >= 0, r, e * r)


def _sigmoid_fwd(x):
    s = _sigmoid(x)
    return s, s


def _sigmoid_bwd(s, g):
    return (g * s * (1.0 - s),)


_sigmoid.defvjp(_sigmoid_fwd, _sigmoid_bwd)


def _log1p_exp_neg_abs(x):
    e = jnp.exp(-jnp.abs(x))
    return jnp.where(e < 1e-4, e - 0.5 * e * e, jnp.log(1.0 + e))


@jax.custom_vjp
def _softplus(x):
    return jnp.maximum(x, 0.0) + _log1p_exp_neg_abs(x)


def _softplus_fwd(x):
    return _softplus(x), x


def _softplus_bwd(x, g):
    return (g * _sigmoid(x),)


_softplus.defvjp(_softplus_fwd, _softplus_bwd)


def _silu(x):
    return x * _sigmoid(x)


def _silu_grad(x):
    s = _sigmoid(x)
    return s * (1.0 + x * (1.0 - s))


def _dot(a, b, dims, precision=None):
    return lax.dot_general(a, b, (dims, ((), ())), precision=precision, preferred_element_type=F32)


NN = ((1,), (0,))
NT = ((1,), (1,))
TN = ((0,), (0,))


def _bdot(a, b, dims):
    return _dot(a.astype(BF16), b.astype(BF16), dims)


def _layer_norm(pre, g, beta):
    mu = jnp.mean(pre, axis=-1, keepdims=True)
    xc = pre - mu
    var = jnp.mean(xc * xc, axis=-1, keepdims=True)
    return xc * lax.rsqrt(var + LN_EPS) * g + beta


def _layer_norm_bwd(pre, g, dy):
    mu = jnp.mean(pre, axis=-1, keepdims=True)
    xc = pre - mu
    rstd = lax.rsqrt(jnp.mean(xc * xc, axis=-1, keepdims=True) + LN_EPS)
    xhat = xc * rstd
    dxh = dy * g
    m1 = jnp.mean(dxh, axis=-1, keepdims=True)
    m2 = jnp.mean(dxh * xhat, axis=-1, keepdims=True)
    return (rstd * (dxh - m1 - xhat * m2), jnp.sum(dy * xhat, axis=0, keepdims=True),
            jnp.sum(dy, axis=0, keepdims=True))


def _mm(a, b, mode, *, tm, tn, tk, name, epi=None, c=None, scale=1.0, b_dev=False, out_dev=False, out_dtype=F32,
        ln=None, scatter=()):
    if mode == "NN":
        m, kk = a.shape
        n = b.shape[2] * N_DEV if b_dev else b.shape[1]
    elif mode == "NT":
        m, kk = a.shape
        n = b.shape[1] if b_dev else b.shape[0]
    else:
        kk, m = a.shape
        n = b.shape[1]
    assert m % tm == 0 and n % tn == 0 and kk % tk == 0, (name, m, n, kk, tm, tn, tk)
    nk = kk // tk
    dims = {"NN": NN, "NT": NT, "TN": TN}[mode]

    if mode == "TN":
        a_spec = pl.BlockSpec((tk, tm), lambda i, j, k: (k, i))
    else:
        a_spec = pl.BlockSpec((tm, tk), lambda i, j, k: (i, k))
    if mode == "NN":
        if b_dev:
            assert tn == b.shape[2]
            b_spec = pl.BlockSpec((None, tk, tn), lambda i, j, k: (j, k, 0))
        else:
            b_spec = pl.BlockSpec((tk, tn), lambda i, j, k: (k, j))
    elif mode == "NT":
        if b_dev:
            assert tk == b.shape[2]
            b_spec = pl.BlockSpec((None, tn, tk), lambda i, j, k: (k, j, 0))
        else:
            b_spec = pl.BlockSpec((tn, tk), lambda i, j, k: (j, k))
    else:
        b_spec = pl.BlockSpec((tk, tn), lambda i, j, k: (k, j))
    in_specs = [a_spec, b_spec]
    operands = [a, b]
    if c is not None:
        in_specs.append(pl.BlockSpec((tm, tn), lambda i, j, k: (i, j)))
        operands.append(c)
    if epi == "ln":
        assert tn == n and not out_dev
        in_specs += [pl.BlockSpec((1, n), lambda i, j, k: (0, 0))] * 2
        operands += [ln[0].reshape(1, n), ln[1].reshape(1, n)]
    elif epi == "ln_bwd":
        assert tn == n and not out_dev
        in_specs += [pl.BlockSpec((tm, tn), lambda i, j, k: (i, j)), pl.BlockSpec((1, n), lambda i, j, k: (0, 0))]
        operands += [ln[0], ln[1].reshape(1, n)]
    if out_dev:
        assert tn == n // N_DEV
        out_shape = jax.ShapeDtypeStruct((N_DEV, m, tn), out_dtype)
        out_spec = pl.BlockSpec((None, tm, tn), lambda i, j, k: (j, i, 0))
    else:
        out_shape = jax.ShapeDtypeStruct((m, n), out_dtype)
        out_spec = pl.BlockSpec((tm, tn), lambda i, j, k: (i, j))
    if epi == "ln":
        out_shape = [out_shape, out_shape, jax.ShapeDtypeStruct((m, n), BF16)]
        out_spec = [out_spec] * 3
    elif epi == "relu2_copy":
        assert not out_dev
        out_shape = [out_shape, jax.ShapeDtypeStruct((m, n), BF16)]
        out_spec = [out_spec] * 2
    elif epi == "ln_bwd":
        vec_shape, vec_spec = jax.ShapeDtypeStruct((1, n), F32), pl.BlockSpec((1, n), lambda i, j, k: (0, 0))
        out_shape = [out_shape, jax.ShapeDtypeStruct((m, n), BF16), vec_shape, vec_shape]
        out_spec = [out_spec, out_spec, vec_spec, vec_spec]
    n_out = {"ln": 3, "relu2_copy": 2, "ln_bwd": 4}.get(epi, 1)
    ns = len(scatter)
    if ns:
        in_specs += [_ANY] * ns
        operands += list(scatter)
        out_shape = (out_shape if n_out > 1 else [out_shape]) + [jax.ShapeDtypeStruct(s.shape, s.dtype) for s in scatter]
        out_spec = (out_spec if n_out > 1 else [out_spec]) + [_ANY] * ns
    n_in = len(operands)
    grid = (m // tm, n // tn, nk)

    def body(*refs):
        a_ref, b_ref = refs[0], refs[1]
        c_ref = refs[2] if c is not None else None
        o_ref = refs[n_in]
        scratch0 = n_in + n_out + ns
        acc_ref = refs[scratch0] if nk > 1 else None
        if ns:
            s_start, s_finish = _scatter_phases(refs[n_in - ns:n_in], refs[n_in + n_out:scratch0],
                                                *refs[scratch0 + (1 if nk > 1 else 0):])
            at = lambda step: functools.reduce(lambda x, y: x & y, [pl.program_id(ax) == step[ax] for ax in range(3)])
            pl.when(at((0, 0, 0)))(s_start)
        p = _dot(a_ref[...].astype(BF16), b_ref[...].astype(BF16), dims)
        first_rows = pl.program_id(0) == 0

        def finish(acc):
            if epi == "add":
                acc = acc + scale * c_ref[...]
            elif epi == "relu2grad":
                acc = acc * (2.0 * jnp.maximum(c_ref[...], 0.0))
            elif epi == "relu2_copy":
                refs[n_in + 1][...] = jnp.square(jnp.maximum(acc, 0.0)).astype(BF16)
            elif epi == "ln_bwd":
                acc, dg, db = _layer_norm_bwd(refs[3][...], refs[4][...], acc + scale * c_ref[...])
                dg_ref, db_ref = refs[n_in + 2], refs[n_in + 3]

                @pl.when(first_rows)
                def _():
                    dg_ref[...] = jnp.zeros_like(dg_ref)
                    db_ref[...] = jnp.zeros_like(db_ref)

                dg_ref[...] += dg
                db_ref[...] += db
                refs[n_in + 1][...] = acc.astype(BF16)
            elif epi == "ln":
                acc = acc + scale * c_ref[...]
                y = _layer_norm(acc, refs[3][...], refs[4][...])
                refs[n_in + 1][...] = y
                refs[n_in + 2][...] = y.astype(BF16)
            o_ref[...] = acc.astype(out_dtype)

        if nk == 1:
            finish(p)
        else:
            k = pl.program_id(2)

            @pl.when(k == 0)
            def _():
                acc_ref[...] = p

            @pl.when(k > 0)
            def _():
                acc_ref[...] += p

            @pl.when(k == nk - 1)
            def _():
                finish(acc_ref[...])

        if ns:
            pl.when(at(tuple(g - 1 for g in grid)))(s_finish)

    res = pl.pallas_call(
        body, name=name, grid=grid, in_specs=in_specs, out_specs=out_spec, out_shape=out_shape,
        scratch_shapes=([pltpu.VMEM((tm, tn), F32)] if nk > 1 else []) + (_scatter_scratch(ns) if ns else []),
        compiler_params=_cp(("arbitrary",) * 3 if ns or epi == "ln_bwd" else ("parallel", "parallel", "arbitrary"),
                            has_side_effects=bool(ns)),
    )(*operands)
    return res


def _ln_bwd(pre, g, dy, *, name):
    lp, d = pre.shape
    tm = _row_tile(lp, 512)

    def body(pre_ref, g_ref, dy_ref, dpre_ref, dpreb_ref, dg_ref, db_ref):
        dpre, dg, db = _layer_norm_bwd(pre_ref[...], g_ref[...], dy_ref[...])
        dpre_ref[...] = dpre
        dpreb_ref[...] = dpre.astype(BF16)

        @pl.when(pl.program_id(0) == 0)
        def _():
            dg_ref[...] = jnp.zeros_like(dg_ref)
            db_ref[...] = jnp.zeros_like(db_ref)

        dg_ref[...] += dg
        db_ref[...] += db

    row = pl.BlockSpec((tm, d), lambda i: (i, 0))
    vec = pl.BlockSpec((1, d), lambda i: (0, 0))
    return pl.pallas_call(
        body, name=name, grid=(lp // tm,), in_specs=[row, vec, row], out_specs=[row, row, vec, vec],
        out_shape=[jax.ShapeDtypeStruct((lp, d), F32), jax.ShapeDtypeStruct((lp, d), BF16),
                   jax.ShapeDtypeStruct((1, d), F32), jax.ShapeDtypeStruct((1, d), F32)],
        compiler_params=_cp(("arbitrary",)),
    )(pre, g.reshape(1, d), dy)


def _loss_head(y, target, *, name):
    lp, d = y.shape
    seq = target.shape[0]
    tm = SB_BLOCK
    first = (lp - seq) // tm
    assert (lp - seq) % tm == 0 and seq % tm == 0

    def body(y_ref, t_ref, dy_ref, loss_ref):
        i = pl.program_id(0)
        live = i >= first
        diff = jnp.where(live, y_ref[...] - t_ref[...], 0.0)
        dy_ref[...] = diff * (1.0 / d)

        @pl.when(i == 0)
        def _():
            loss_ref[...] = jnp.zeros_like(loss_ref)

        loss_ref[...] += jnp.sum(diff * diff, axis=0, keepdims=True) * (0.5 / d)

    return pl.pallas_call(
        body, name=name, grid=(lp // tm,),
        in_specs=[pl.BlockSpec((tm, d), lambda i: (i, 0)),
                  pl.BlockSpec((tm, d), lambda i: (jnp.maximum(i - first, 0), 0))],
        out_specs=[pl.BlockSpec((tm, d), lambda i: (i, 0)), pl.BlockSpec((1, d), lambda i: (0, 0))],
        out_shape=[jax.ShapeDtypeStruct((lp, d), F32), jax.ShapeDtypeStruct((1, d), F32)],
        compiler_params=_cp(("arbitrary",)),
    )(y, target)


def _gate_fwd(o, zsrc, z_blk0, g, other, *, heads, name):
    lp = o.shape[0]
    tm = _row_tile(lp, 512)
    w = heads * HEAD_W
    assert (z_blk0 * HEAD_W) % w == 0
    has_other = w < D_MODEL

    def body(o_ref, z_ref, g_ref, *rest):
        y_ref = rest[-1]
        gv = g_ref[...]
        for h in range(heads):
            cs = slice(h * HEAD_W, (h + 1) * HEAD_W)
            ov = o_ref[:, cs]
            r = lax.rsqrt(jnp.mean(ov * ov, axis=-1, keepdims=True) + RMS_EPS)
            y_ref[:, cs] = (ov * r * gv * _silu(z_ref[:, cs])).astype(BF16)
        if has_other:
            y_ref[:, w:] = rest[0][...].astype(BF16)

    row = lambda width, blk: pl.BlockSpec((tm, width), lambda i: (i, blk))
    return pl.pallas_call(
        body, name=name, grid=(lp // tm,),
        in_specs=[row(w, 0), row(w, z_blk0 * HEAD_W // w), pl.BlockSpec((1, HEAD_W), lambda i: (0, 0))]
        + ([row(D_MODEL - w, 0)] if has_other else []),
        out_specs=row(D_MODEL, 0), out_shape=jax.ShapeDtypeStruct((lp, D_MODEL), BF16),
        compiler_params=_cp(("parallel",)),
    )(o, zsrc, g.reshape(1, HEAD_W), *([other] if has_other else []))


def _gate_bwd(o, zsrc, z_blk0, g, dy, *, heads, name):
    lp = o.shape[0]
    tm = _row_tile(lp, 512)

    w = heads * HEAD_W
    assert (z_blk0 * HEAD_W) % w == 0

    def body(o_ref, z_ref, g_ref, dy_ref, do_ref, dz_ref, dg_ref):
        @pl.when(pl.program_id(0) == 0)
        def _():
            dg_ref[...] = jnp.zeros_like(dg_ref)

        gv = g_ref[...]
        dg = jnp.zeros((1, HEAD_W), F32)
        for h in range(heads):
            cs = slice(h * HEAD_W, (h + 1) * HEAD_W)
            ov, zv, dyv = o_ref[:, cs], z_ref[:, cs], dy_ref[:, cs]
            r = lax.rsqrt(jnp.mean(ov * ov, axis=-1, keepdims=True) + RMS_EPS)
            nrm = ov * r
            s = _silu(zv)
            dn = dyv * gv * s
            do_ref[:, cs] = r * (dn - nrm * jnp.mean(dn * nrm, axis=-1, keepdims=True))
            dz_ref[:, cs] = dyv * nrm * gv * _silu_grad(zv)
            dg = dg + jnp.sum(dyv * nrm * s, axis=0, keepdims=True)
        dg_ref[...] += dg

    row = lambda blk: pl.BlockSpec((tm, w), lambda i: (i, blk))
    vec = pl.BlockSpec((1, HEAD_W), lambda i: (0, 0))
    return pl.pallas_call(
        body, name=name, grid=(lp // tm,),
        in_specs=[row(0), row(z_blk0 * HEAD_W // w), vec, row(0)], out_specs=[row(0), row(0), vec],
        out_shape=[jax.ShapeDtypeStruct((lp, w), F32), jax.ShapeDtypeStruct((lp, w), F32),
                   jax.ShapeDtypeStruct((1, HEAD_W), F32)],
        compiler_params=_cp(("arbitrary",)),
    )(o, zsrc, g.reshape(1, HEAD_W), dy)


def _conv_taps(x, w):
    acc = w[CONV_K - 1:CONV_K, :] * x
    for k in range(CONV_K - 1):
        acc = acc + w[k:k + 1, :] * pltpu.roll(x, CONV_K - 1 - k, 0)
    return acc


def _gdn_pre_fwd(p0, conv_w, pad, *, name):
    lp = p0.shape[0]
    nq = GDN_HEADS
    qscale = HEAD_W ** -0.5

    def body(x_ref, w_ref, y_ref):
        j = pl.program_id(0)
        c = _conv_taps(x_ref[...], w_ref[...])
        s = _silu(c)
        r = lax.rsqrt(jnp.sum(s * s, axis=-1, keepdims=True) + L2_EPS)
        mult = jnp.where(j < nq, r * qscale, jnp.where(j < 2 * nq, r, 1.0))
        rows = lax.broadcasted_iota(jnp.int32, (lp, 1), 0)
        y_ref[...] = jnp.where(rows >= pad, s * mult, 0.0)

    return pl.pallas_call(
        body, name=name, grid=(3 * nq,),
        in_specs=[pl.BlockSpec((lp, HEAD_W), lambda j: (0, j)), pl.BlockSpec((CONV_K, HEAD_W), lambda j: (0, j))],
        out_specs=pl.BlockSpec((lp, HEAD_W), lambda j: (0, j)),
        out_shape=jax.ShapeDtypeStruct((lp, 3 * nq * HEAD_W), F32), compiler_params=_cp(("parallel",)),
    )(p0, conv_w)


def _gdn_pre_bwd(p0, conv_w, dqkv, pad, *, name):
    lp = p0.shape[0]
    nq = GDN_HEADS
    qscale = HEAD_W ** -0.5

    def body(x_ref, w_ref, dy_ref, dx_ref, dw_ref):
        j = pl.program_id(0)
        x, w = x_ref[...], w_ref[...]
        c = _conv_taps(x, w)
        s = _silu(c)
        r = lax.rsqrt(jnp.sum(s * s, axis=-1, keepdims=True) + L2_EPS)
        rows = lax.broadcasted_iota(jnp.int32, (lp, 1), 0)
        dy = jnp.where(rows >= pad, dy_ref[...], 0.0)
        nrm = s * r
        dn = dy * jnp.where(j < nq, qscale, 1.0)
        ds_norm = r * (dn - nrm * jnp.sum(nrm * dn, axis=-1, keepdims=True))
        ds = jnp.where(j < 2 * nq, ds_norm, dy)
        dc = ds * _silu_grad(c)
        dx = w[CONV_K - 1:CONV_K, :] * dc
        dws = [None] * CONV_K
        dws[CONV_K - 1] = jnp.sum(dc * x, axis=0, keepdims=True)
        for k in range(CONV_K - 1):
            sh = CONV_K - 1 - k
            dx = dx + w[k:k + 1, :] * pltpu.roll(dc, lp - sh, 0)
            dws[k] = jnp.sum(dc * pltpu.roll(x, sh, 0), axis=0, keepdims=True)
        dx_ref[...] = dx
        dw_ref[...] = jnp.concatenate(dws, axis=0)

    blk = pl.BlockSpec((lp, HEAD_W), lambda j: (0, j))
    wblk = pl.BlockSpec((CONV_K, HEAD_W), lambda j: (0, j))
    return pl.pallas_call(
        body, name=name, grid=(3 * nq,), in_specs=[blk, wblk, blk], out_specs=[blk, wblk],
        out_shape=[jax.ShapeDtypeStruct((lp, 3 * nq * HEAD_W), F32),
                   jax.ShapeDtypeStruct((CONV_K, 3 * nq * HEAD_W), F32)],
        compiler_params=_cp(("parallel",)),
    )(p0, conv_w, dqkv)


@jax.custom_vjp
def _inv_unit_lower(m):
    c = m.shape[0]
    eye = (lax.broadcasted_iota(jnp.int32, (c, c), 0) == lax.broadcasted_iota(jnp.int32, (c, c), 1)).astype(F32)
    x = eye - m
    p = m
    n = 2
    while n < CHUNK:
        p = _bdot(p, p, NN)
        x = x + _bdot(x, p, NN)
        n *= 2
    return x


def _inv_fwd(m):
    t = _inv_unit_lower(m)
    return t, t


def _inv_bwd(t, g):
    return (-_bdot(_bdot(t, g, TN), t, NT),)


_inv_unit_lower.defvjp(_inv_fwd, _inv_bwd)


GDN_STEP = 2


def _heads_to_rows(x, nh):
    return jnp.concatenate([x[:, h * HEAD_W:(h + 1) * HEAD_W] for h in range(nh)], axis=0)


def _rows_to_heads(x, nh):
    c = x.shape[0] // nh
    return jnp.concatenate([x[h * c:(h + 1) * c] for h in range(nh)], axis=1)


def _gdn_chunk(q, k, v, ba, alog, dtb, states, valid):
    nh = GDN_HEADS
    c = q.shape[0]
    r = nh * c
    lane = lax.broadcasted_iota(jnp.int32, (1, HEAD_W), 1)
    pick = lambda x, l: jnp.sum(jnp.where(lane == l, x, 0.0), axis=-1, keepdims=True)
    beta = jnp.concatenate([jnp.where(valid, _sigmoid(pick(ba, h)), 0.0) for h in range(nh)], axis=0)
    g = jnp.concatenate(
        [jnp.where(valid, -jnp.exp(pick(alog, h)) * _softplus(pick(ba, nh + h) + pick(dtb, h)), 0.0) for h in range(nh)],
        axis=0)
    qs, ks, vs = _heads_to_rows(q, nh), _heads_to_rows(k, nh), _heads_to_rows(v, nh)
    rr = lax.broadcasted_iota(jnp.int32, (r, r), 0)
    cc = lax.broadcasted_iota(jnp.int32, (r, r), 1)
    same = (rr // c) == (cc // c)
    causal, strict = same & (cc <= rr), same & (cc < rr)
    lower = jnp.where(causal, 1.0, 0.0).astype(BF16)
    upper = jnp.where(same & (cc >= rr), 1.0, 0.0).astype(BF16)
    gcb = _mask_mm(lower, upper, g * jnp.ones((1, HEAD_W), F32))
    gc_col = jnp.concatenate([gcb] * (r // HEAD_W), axis=1)
    decay = jnp.where(causal, jnp.exp(jnp.minimum(gc_col - gc_col.T, 0.0)), 0.0)
    egc = jnp.exp(gcb)
    kb = ks * beta
    m = jnp.where(strict, _dot3(kb, ks, NT) * decay, 0.0)
    t = _inv_unit_lower(m)
    u = _bdot(t, vs * beta, NN)
    w = _bdot(t, kb * egc, NN)
    a = _bdot(qs, ks, NT) * decay
    rows = lambda x, h: x[h * c:(h + 1) * c]
    qe = qs * egc
    v_new = u - jnp.concatenate([_bdot(rows(w, h), states[h], NN) for h in range(nh)], axis=0)
    o = jnp.concatenate([_bdot(rows(qe, h), states[h], NN) for h in range(nh)], axis=0) + _bdot(a, v_new, NN)
    new_states = []
    for h in range(nh):
        gl = gcb[(h + 1) * c - 1:(h + 1) * c, :]
        k_dec = rows(ks, h) * jnp.exp(gl - rows(gcb, h))
        new_states.append(states[h] * jnp.exp(gl) + _bdot(k_dec, rows(v_new, h), TN))
    return _rows_to_heads(o, nh), new_states


def _gdn_fwd(qkv, p0, alog_v, dtb_v, pad, *, name, gather=None):
    lp = qkv.shape[0]
    n = lp // CHUNK
    nh = GDN_HEADS
    assert n % GDN_STEP == 0
    steps, rows = n // GDN_STEP, GDN_STEP * CHUNK
    g_srcs, g_dtypes = gather if gather is not None else ([], [])
    ng_arr = len(g_srcs)

    def body(q_ref, k_ref, v_ref, ba_ref, al_ref, dt_ref, *rest):
        g_ins, (o_ref, st_ref) = rest[:ng_arr], rest[ng_arr:ng_arr + 2]
        g_outs, s_ref, g_scratch = rest[ng_arr + 2:2 * ng_arr + 2], rest[2 * ng_arr + 2], rest[2 * ng_arr + 3:]
        i = pl.program_id(0)
        if ng_arr:
            g_start, g_forward, g_finish = _gather_phases(g_ins, g_outs, g_scratch[:ng_arr], *g_scratch[ng_arr:],
                                                          g_dtypes)
            pl.when(i == 0)(g_start)
            pl.when(i == (3 * steps) // 4)(g_forward)

        @pl.when(i == 0)
        def _():
            s_ref[...] = jnp.zeros_like(s_ref)

        s = s_ref[...]
        s = [s[h] for h in range(nh)]
        q, k, v, ba, al, dt = q_ref[...], k_ref[...], v_ref[...], ba_ref[...], al_ref[...], dt_ref[...]
        outs = []
        for c in range(GDN_STEP):
            sl = slice(c * CHUNK, (c + 1) * CHUNK)
            valid = (i * rows + c * CHUNK + lax.broadcasted_iota(jnp.int32, (CHUNK, 1), 0)) >= pad
            for h in range(nh):
                st_ref[c, h] = s[h]
            o, s = _gdn_chunk(q[sl], k[sl], v[sl], ba[sl], al, dt, s, valid)
            outs.append(o)
        o_ref[...] = jnp.concatenate(outs, axis=0)
        for h in range(nh):
            s_ref[h] = s[h]
        if ng_arr:
            pl.when(i == steps - 1)(g_finish)

    w = nh * HEAD_W
    vec = pl.BlockSpec((1, HEAD_W), lambda i: (0, 0))
    return pl.pallas_call(
        body, name=name, grid=(steps,),
        in_specs=[pl.BlockSpec((rows, w), lambda i: (i, 0)), pl.BlockSpec((rows, w), lambda i: (i, 1)),
                  pl.BlockSpec((rows, w), lambda i: (i, 2)), pl.BlockSpec((rows, HEAD_W), lambda i: (i, AB_BA // HEAD_W)),
                  vec, vec] + [pl.BlockSpec(memory_space=pltpu.VMEM)] * ng_arr,
        out_specs=[pl.BlockSpec((rows, w), lambda i: (i, 0)),
                   pl.BlockSpec((GDN_STEP, nh, HEAD_W, HEAD_W), lambda i: (i, 0, 0, 0))] + [_ANY] * ng_arr,
        out_shape=[jax.ShapeDtypeStruct((lp, w), F32), jax.ShapeDtypeStruct((n, nh, HEAD_W, HEAD_W), F32)]
        + _gather_out_shapes(g_srcs, g_dtypes),
        scratch_shapes=[pltpu.VMEM((nh, HEAD_W, HEAD_W), F32)] + (_gather_scratch(g_srcs, g_dtypes) if ng_arr else []),
        compiler_params=_cp(("arbitrary",), has_side_effects=bool(ng_arr)),
    )(qkv, qkv, qkv, p0, alog_v, dtb_v, *g_srcs)


def _gdn_bwd(qkv, p0, alog_v, dtb_v, states, do, pad, *, name, scatter=()):
    lp = qkv.shape[0]
    n = lp // CHUNK
    nh = GDN_HEADS
    assert n % GDN_STEP == 0
    steps, rows = n // GDN_STEP, GDN_STEP * CHUNK
    ns = len(scatter)

    def body(q_ref, k_ref, v_ref, ba_ref, al_ref, dt_ref, st_ref, do_ref, *rest):
        s_ins, (dq_ref, dk_ref, dv_ref, dba_ref, dal_ref, ddt_ref) = rest[:ns], rest[ns:ns + 6]
        s_outs, ds_ref, s_sems = rest[ns + 6:2 * ns + 6], rest[2 * ns + 6], rest[2 * ns + 7:]
        step = pl.program_id(0)
        i = steps - 1 - step
        if ns:
            s_start, s_finish = _scatter_phases(s_ins, s_outs, *s_sems)
            pl.when(step == 0)(s_start)

        @pl.when(step == 0)
        def _():
            ds_ref[...] = jnp.zeros_like(ds_ref)
            dal_ref[...] = jnp.zeros_like(dal_ref)
            ddt_ref[...] = jnp.zeros_like(ddt_ref)

        q, k, v, ba, al, dt = q_ref[...], k_ref[...], v_ref[...], ba_ref[...], al_ref[...], dt_ref[...]
        st, do, dst = st_ref[...], do_ref[...], ds_ref[...]
        vjps = []
        for c in range(GDN_STEP):
            sl = slice(c * CHUNK, (c + 1) * CHUNK)
            valid = (i * rows + c * CHUNK + lax.broadcasted_iota(jnp.int32, (CHUNK, 1), 0)) >= pad
            fn = functools.partial(_gdn_chunk, valid=valid)
            vjps.append(jax.vjp(fn, q[sl], k[sl], v[sl], ba[sl], al, dt, [st[c, h] for h in range(nh)])[1])
        ds = [dst[h] for h in range(nh)]
        grads = [None] * GDN_STEP
        for c in reversed(range(GDN_STEP)):
            grads[c] = vjps[c]((do[c * CHUNK:(c + 1) * CHUNK], ds))
            ds = grads[c][6]
        for j, ref in enumerate((dq_ref, dk_ref, dv_ref, dba_ref)):
            ref[...] = jnp.concatenate([gr[j] for gr in grads], axis=0)
        dal_ref[...] += sum(gr[4] for gr in grads)
        ddt_ref[...] += sum(gr[5] for gr in grads)
        for h in range(nh):
            ds_ref[h] = ds[h]
        if ns:
            pl.when(step == steps - 1)(s_finish)

    w = nh * HEAD_W
    rev = lambda c: (lambda s: (steps - 1 - s, c))
    vec = pl.BlockSpec((1, HEAD_W), lambda s: (0, 0))
    return pl.pallas_call(
        body, name=name, grid=(steps,),
        in_specs=[pl.BlockSpec((rows, w), rev(0)), pl.BlockSpec((rows, w), rev(1)), pl.BlockSpec((rows, w), rev(2)),
                  pl.BlockSpec((rows, HEAD_W), rev(AB_BA // HEAD_W)), vec, vec,
                  pl.BlockSpec((GDN_STEP, nh, HEAD_W, HEAD_W), lambda s: (steps - 1 - s, 0, 0, 0)),
                  pl.BlockSpec((rows, w), rev(0))] + [_ANY] * ns,
        out_specs=[pl.BlockSpec((rows, w), rev(0)), pl.BlockSpec((rows, w), rev(0)), pl.BlockSpec((rows, w), rev(0)),
                   pl.BlockSpec((rows, HEAD_W), rev(0)), vec, vec] + [_ANY] * ns,
        out_shape=[jax.ShapeDtypeStruct((lp, w), F32)] * 3 + [jax.ShapeDtypeStruct((lp, HEAD_W), F32)]
        + [jax.ShapeDtypeStruct((1, HEAD_W), F32)] * 2 + [jax.ShapeDtypeStruct(s.shape, s.dtype) for s in scatter],
        scratch_shapes=[pltpu.VMEM((nh, HEAD_W, HEAD_W), F32)] + (_scatter_scratch(ns) if ns else []),
        compiler_params=_cp(("arbitrary",), has_side_effects=bool(ns)),
    )(qkv, qkv, qkv, p0, alog_v, dtb_v, states, do, *scatter)


HG_LEVELS = (32, 16, 8, 4, 2, 1)
HG_GROUP = 4
HG_STEP = 2


def _hg_masks():
    import numpy as np
    c = CHUNK
    t = np.arange(c)[:, None]
    j = np.arange(c)[None, :]
    sums = (j <= t).astype(np.float32)
    pairs = [j == t]
    for m in HG_LEVELS:
        p = (t // (2 * m)) * (2 * m)
        r = p + m
        pairs.append((t >= r) & (j < r) & (j >= p))
    pairs = np.concatenate([np.kron(np.eye(HG_GROUP), p) for p in pairs], axis=0).astype(np.float32)
    return jnp.asarray(sums, BF16), jnp.asarray(sums.T, BF16), jnp.asarray(pairs, F32)


def _hg_level_row(b, m):
    c, w = b.shape
    if m >= 8:
        return jnp.concatenate([jnp.broadcast_to(b[p + m:p + m + 1], (2 * m, w)) for p in range(0, c, 2 * m)], axis=0)
    tiles = b.reshape(c // 8, 8, w)
    sub = lax.broadcasted_iota(jnp.int32, (1, 8, 1), 1)
    out = None
    for r0 in range(m, 8, 2 * m):
        cand = jnp.broadcast_to(tiles[:, r0:r0 + 1, :], tiles.shape)
        out = cand if out is None else jnp.where(sub >= r0 - m, cand, out)
    return out.reshape(c, w)


def _split3(x):
    hi = x.astype(BF16)
    r1 = x - hi.astype(F32)
    mid = r1.astype(BF16)
    return hi, mid, (r1 - mid.astype(F32)).astype(BF16)


def _dot3_raw(a, b, dims):
    ah, am, _ = _split3(a)
    bh, bm, _ = _split3(b)
    return _dot(ah, bh, dims) + (_dot(ah, bm, dims) + _dot(am, bh, dims))


@functools.partial(jax.custom_vjp, nondiff_argnums=(2,))
def _dot3(a, b, dims):
    return _dot3_raw(a, b, dims)


def _dot3_fwd(a, b, dims):
    return _dot3_raw(a, b, dims), (a, b)


def _dot3_bwd(dims, res, g):
    a, b = res
    if dims == NN:
        return _dot3_raw(g, b, NT), _dot3_raw(a, g, TN)
    return _dot3_raw(g, b, NN), _dot3_raw(g, a, TN)


_dot3.defvjp(_dot3_fwd, _dot3_bwd)


def _mask_mm_raw(m, x):
    return sum(_dot(m, part, NN) for part in _split3(x))


@jax.custom_vjp
def _mask_mm(m, mt, x):
    return _mask_mm_raw(m, x)


def _mask_mm_fwd(m, mt, x):
    return _mask_mm_raw(m, x), (m, mt)


def _mask_mm_bwd(res, g):
    m, mt = res
    return jnp.zeros_like(m), jnp.zeros_like(mt), _mask_mm_raw(mt, g)


_mask_mm.defvjp(_mask_mm_fwd, _mask_mm_bwd)


def _hg_chunk(qr, fr, ir, lb, states, valid, sums, sums_t, pairs):
    nh = HG_GROUP
    c = qr.shape[0]
    r = nh * c
    fg = lb + (1.0 - lb) * _sigmoid(fr)
    logf = jnp.where(valid, jnp.log(fg), 0.0)
    k = jnp.where(valid, 1.0 - fg, 0.0)
    qs = jnp.where(valid, _silu(qr), 0.0)
    v = jnp.where(valid, ir, 0.0)
    b = _mask_mm(sums, sums_t, logf)
    mask = lambda n: pairs[n * r:(n + 1) * r]
    stack = lambda x: _heads_to_rows(x, nh)
    a = mask(0) * _bdot(stack(qs), stack(k), NT)
    for lvl, m in enumerate(HG_LEVELS):
        d = b - _hg_level_row(b, m)
        a = a + mask(1 + lvl) * _bdot(stack(qs * jnp.exp(jnp.minimum(d, 0.0))),
                                      stack(k * jnp.exp(jnp.minimum(-d, 0.0))), NT)
    av = _bdot(a, stack(v), NN)
    eb = jnp.exp(b)
    qe, kd = qs * eb, k * jnp.exp(b[c - 1:c] - b)
    outs, new_states = [], []
    for h in range(nh):
        cs = slice(h * HEAD_W, (h + 1) * HEAD_W)
        outs.append(_bdot(qe[:, cs], states[h], NT) + av[h * c:(h + 1) * c])
        new_states.append(states[h] * eb[c - 1:c, cs] + _bdot(v[:, cs], kd[:, cs], TN))
    return jnp.concatenate(outs, axis=1), new_states


def _hg_fwd(p1, lb, pad, *, name, gather=None):
    lp = p1.shape[0]
    n = lp // CHUNK
    nh = HG_HEADS
    g_srcs, g_dtypes = gather if gather is not None else ([], [])
    ng_arr = len(g_srcs)

    def body(q_ref, f_ref, i_ref, lb_ref, sums_ref, sums_t_ref, pairs_ref, *rest):
        g_ins, (o_ref, st_ref) = rest[:ng_arr], rest[ng_arr:ng_arr + 2]
        g_outs, s_ref, g_scratch = rest[ng_arr + 2:2 * ng_arr + 2], rest[2 * ng_arr + 2], rest[2 * ng_arr + 3:]
        i = pl.program_id(1)
        if ng_arr:
            g_start, g_forward, g_finish = _gather_phases(g_ins, g_outs, g_scratch[:ng_arr], *g_scratch[ng_arr:],
                                                          g_dtypes)
            last_group = pl.program_id(0) == ngrp - 1
            pl.when((pl.program_id(0) == 0) & (i == 0))(g_start)
            pl.when(last_group & (i == 0))(g_forward)

        @pl.when(i == 0)
        def _():
            s_ref[...] = jnp.zeros_like(s_ref)

        s = s_ref[...]
        s = [s[h] for h in range(grp)]
        q, f, iv, lbv = q_ref[...], f_ref[...], i_ref[...], lb_ref[...]
        masks_v = (sums_ref[...], sums_t_ref[...], pairs_ref[...])
        outs = []
        for c in range(HG_STEP):
            sl = slice(c * CHUNK, (c + 1) * CHUNK)
            valid = (i * rows + c * CHUNK + lax.broadcasted_iota(jnp.int32, (CHUNK, 1), 0)) >= pad
            for h in range(grp):
                st_ref[h, c] = s[h]
            o, s = _hg_chunk(q[sl], f[sl], iv[sl], lbv, s, valid, *masks_v)
            outs.append(o)
        o_ref[...] = jnp.concatenate(outs, axis=0)
        for h in range(grp):
            s_ref[h] = s[h]
        if ng_arr:
            pl.when(last_group & (i == steps - 1))(g_finish)

    masks = _hg_masks()
    grp, ngrp, gw = HG_GROUP, nh // HG_GROUP, HG_GROUP * HEAD_W
    assert n % HG_STEP == 0
    steps, rows = n // HG_STEP, HG_STEP * CHUNK
    blk = lambda off: pl.BlockSpec((rows, gw), lambda h, i: (i, off + h))
    const = lambda a: pl.BlockSpec(a.shape, lambda h, i: (0, 0))
    return pl.pallas_call(
        body, name=name, grid=(ngrp, steps),
        in_specs=[blk(0), blk(ngrp), blk(2 * ngrp), pl.BlockSpec((1, gw), lambda h, i: (0, h))]
        + [const(a) for a in masks] + [pl.BlockSpec(memory_space=pltpu.VMEM)] * ng_arr,
        out_specs=[blk(0), pl.BlockSpec((grp, HG_STEP, HEAD_W, HEAD_W), lambda h, i: (h, i, 0, 0))] + [_ANY] * ng_arr,
        out_shape=[jax.ShapeDtypeStruct((lp, nh * HEAD_W), F32), jax.ShapeDtypeStruct((nh, n, HEAD_W, HEAD_W), F32)]
        + _gather_out_shapes(g_srcs, g_dtypes),
        scratch_shapes=[pltpu.VMEM((grp, HEAD_W, HEAD_W), F32)] + (_gather_scratch(g_srcs, g_dtypes) if ng_arr else []),
        compiler_params=_cp(("arbitrary", "arbitrary"), has_side_effects=bool(ng_arr)),
    )(p1, p1, p1, lb, *masks, *g_srcs)


def _hg_bwd(p1, lb, states, do, pad, *, name, scatter=()):
    lp = p1.shape[0]
    n = lp // CHUNK
    nh = HG_HEADS
    ns = len(scatter)

    def body(q_ref, f_ref, i_ref, lb_ref, st_ref, do_ref, sums_ref, sums_t_ref, pairs_ref, *rest):
        s_ins, (dq_ref, df_ref, di_ref, dlb_ref) = rest[:ns], rest[ns:ns + 4]
        s_outs, ds_ref, s_sems = rest[ns + 4:2 * ns + 4], rest[2 * ns + 4], rest[2 * ns + 5:]
        step = pl.program_id(1)
        i = steps - 1 - step
        if ns:
            s_start, s_finish = _scatter_phases(s_ins, s_outs, *s_sems)
            pl.when((pl.program_id(0) == 0) & (step == 0))(s_start)

        @pl.when(step == 0)
        def _():
            ds_ref[...] = jnp.zeros_like(ds_ref)
            dlb_ref[...] = jnp.zeros_like(dlb_ref)

        q, f, iv, lbv, st, do, dst = q_ref[...], f_ref[...], i_ref[...], lb_ref[...], st_ref[...], do_ref[...], ds_ref[...]
        masks_v = dict(sums=sums_ref[...], sums_t=sums_t_ref[...], pairs=pairs_ref[...])
        vjps = []
        for c in range(HG_STEP):
            sl = slice(c * CHUNK, (c + 1) * CHUNK)
            valid = (i * rows + c * CHUNK + lax.broadcasted_iota(jnp.int32, (CHUNK, 1), 0)) >= pad
            fn = functools.partial(_hg_chunk, valid=valid, **masks_v)
            vjps.append(jax.vjp(fn, q[sl], f[sl], iv[sl], lbv, [st[h, c] for h in range(grp)])[1])
        ds = [dst[h] for h in range(grp)]
        grads = [None] * HG_STEP
        for c in reversed(range(HG_STEP)):
            grads[c] = vjps[c]((do[c * CHUNK:(c + 1) * CHUNK], ds))
            ds = grads[c][4]
        for j, ref in enumerate((dq_ref, df_ref, di_ref)):
            ref[...] = jnp.concatenate([gr[j] for gr in grads], axis=0)
        dlb_ref[...] += sum(gr[3] for gr in grads)
        for h in range(grp):
            ds_ref[h] = ds[h]
        if ns:
            pl.when((pl.program_id(0) == ngrp - 1) & (step == steps - 1))(s_finish)

    masks = _hg_masks()
    grp, ngrp, gw = HG_GROUP, nh // HG_GROUP, HG_GROUP * HEAD_W
    assert n % HG_STEP == 0
    steps, rows = n // HG_STEP, HG_STEP * CHUNK
    blk = lambda off: pl.BlockSpec((rows, gw), lambda h, s: (steps - 1 - s, off + h))
    const = lambda a: pl.BlockSpec(a.shape, lambda h, s: (0, 0))
    w = nh * HEAD_W
    return pl.pallas_call(
        body, name=name, grid=(ngrp, steps),
        in_specs=[blk(0), blk(ngrp), blk(2 * ngrp), pl.BlockSpec((1, gw), lambda h, s: (0, h)),
                  pl.BlockSpec((grp, HG_STEP, HEAD_W, HEAD_W), lambda h, s: (h, steps - 1 - s, 0, 0)), blk(0)]
        + [const(a) for a in masks] + [_ANY] * ns,
        out_specs=[blk(0), blk(0), blk(0), pl.BlockSpec((1, gw), lambda h, s: (0, h))] + [_ANY] * ns,
        out_shape=[jax.ShapeDtypeStruct((lp, w), F32)] * 3 + [jax.ShapeDtypeStruct((1, w), F32)]
        + [jax.ShapeDtypeStruct(s.shape, s.dtype) for s in scatter],
        scratch_shapes=[pltpu.VMEM((grp, HEAD_W, HEAD_W), F32)] + (_scatter_scratch(ns) if ns else []),
        compiler_params=_cp(("arbitrary", "arbitrary"), has_side_effects=bool(ns)),
    )(p1, p1, p1, lb, states, do, *masks, *scatter)


SB_GROUP = 4
SB_FAR = -110.0


def _sb_cat(kind, first_key=0):
    r = lax.broadcasted_iota(jnp.int32, (SB_BLOCK, 2 * SB_BLOCK), 0)
    c = lax.broadcasted_iota(jnp.int32, (SB_BLOCK, 2 * SB_BLOCK), 1)
    tri = {"after": c < r, "incl": r <= c, "before": r < c}[kind]
    m = ((c >= SB_BLOCK) | tri) & (r >= first_key)
    return jnp.where(m, 1.0, 0.0).astype(BF16)


def _sb_cumsum(x, cat):
    return _dot(x.astype(BF16), cat, NN)


def _sb_logsig(z):
    e = jnp.exp(-jnp.abs(z))
    lse = jnp.where(e < 1e-4, e, jnp.log(1.0 + e))
    lsz = jnp.minimum(z, 0.0) - lse
    return lsz, lsz - z, e


def _sb_stack(x, scale=None):
    lane = lax.broadcasted_iota(jnp.int32, (1, HEAD_W), 1)
    if scale is not None:
        x = x * scale
    return jnp.concatenate([jnp.where(lane < SB_DH, x, 0.0), jnp.where(lane >= SB_DH, x, 0.0)], axis=0).astype(BF16)


def _sb_unstack(x):
    lane = lax.broadcasted_iota(jnp.int32, (1, HEAD_W), 1)
    return jnp.where(lane < SB_DH, x[:SB_BLOCK], x[SB_BLOCK:])


def _sb_fwd(p0, pad, *, name, gather=None):
    lp = p0.shape[0]
    nb = lp // SB_BLOCK
    npair = SB_HEADS // 2
    blk0 = AB_SB // HEAD_W
    scale = SB_DH ** -0.5
    gw = SB_GROUP * SB_BLOCK
    assert pad < SB_BLOCK
    g_srcs, g_dtypes = gather if gather is not None else ([], [])
    ng_arr = len(g_srcs)

    def body(q_ref, k_ref, v_ref, *rest):
        g_ins, (o_ref, tot_ref, nproc_ref) = rest[:ng_arr], rest[ng_arr:ng_arr + 3]
        g_outs, g_scratch = rest[ng_arr + 3:2 * ng_arr + 3], rest[2 * ng_arr + 3:]
        first_step = (pl.program_id(0) == 0) & (pl.program_id(1) == 0)
        last_pair = pl.program_id(0) == npair - 1
        if ng_arr:
            g_start, g_forward, g_finish = _gather_phases(g_ins, g_outs, g_scratch[:ng_arr], *g_scratch[ng_arr:],
                                                          g_dtypes)
            pl.when(first_step)(g_start)
            pl.when(last_pair & (pl.program_id(1) == 0))(g_forward)
        i = pl.program_id(1)
        qs = _sb_stack(q_ref[...], scale)
        qpos = i * SB_BLOCK + lax.broadcasted_iota(jnp.int32, (SB_BLOCK, 1), 0)
        qpos = jnp.concatenate([qpos, qpos], axis=0)
        cat = _sb_cat("after")
        cat0 = _sb_cat("after", pad)
        ng = i // SB_GROUP

        def group(off, nblk, first_cat, allowed, carry):
            acc, run = carry
            kg = k_ref[pl.ds(off, nblk * SB_BLOCK), :].astype(BF16)
            vg = v_ref[pl.ds(off, nblk * SB_BLOCK), :].astype(BF16)
            lsz, l1m, _ = _sb_logsig(_dot(qs, kg, NT))
            if allowed is not None:
                l1m = jnp.where(allowed, l1m, 0.0)
            args = [None] * nblk
            for g in reversed(range(nblk)):
                sl = slice(g * SB_BLOCK, (g + 1) * SB_BLOCK)
                al = _sb_cumsum(l1m[:, sl], first_cat if g == 0 else cat)
                args[g] = lsz[:, sl] + al[:, :SB_BLOCK] + run
                run = run + al[:, SB_BLOCK:]
            wgt = jnp.exp(jnp.concatenate(args, axis=1))
            if allowed is not None:
                wgt = jnp.where(allowed, wgt, 0.0)
            return acc + _dot(wgt.astype(BF16), vg, NN), run

        def below(t, carry):
            gi = ng - 1 - t
            return group(pl.multiple_of(gi * gw, gw), SB_GROUP, jnp.where(gi == 0, cat0, cat), None, carry)

        top = ng * gw

        def top_group(nblk, carry):
            off = pl.multiple_of(jnp.minimum(top, lp - nblk * SB_BLOCK), SB_BLOCK)
            kpos = off + lax.broadcasted_iota(jnp.int32, (1, nblk * SB_BLOCK), 1)
            return group(off, nblk, cat, (kpos < qpos) & (kpos >= pad) & (kpos >= top), carry)

        zero = (jnp.zeros((2 * SB_BLOCK, HEAD_W), F32), jnp.zeros((2 * SB_BLOCK, HEAD_W), F32))
        carry = lax.cond(i - ng * SB_GROUP < SB_GROUP // 2, functools.partial(top_group, SB_GROUP // 2),
                         functools.partial(top_group, SB_GROUP), zero)
        used, acc, run = lax.while_loop(lambda s: (s[0] < ng) & (jnp.max(s[2]) > SB_FAR),
                                        lambda s: (s[0] + 1, *below(s[0], (s[1], s[2]))), (jnp.int32(0), *carry))
        o_ref[...] = _sb_unstack(acc)
        tot_ref[...] = _sb_unstack(run)
        nproc_ref[pl.program_id(0), i] = used.astype(F32)
        if ng_arr:
            pl.when(last_pair & (pl.program_id(1) == nb - 1))(g_finish)

    full = lambda c0: pl.BlockSpec((lp, HEAD_W), lambda p, i: (0, c0 + p))
    out = pl.BlockSpec((SB_BLOCK, HEAD_W), lambda p, i: (i, p))
    return pl.pallas_call(
        body, name=name, grid=(npair, nb),
        in_specs=[pl.BlockSpec((SB_BLOCK, HEAD_W), lambda p, i: (i, blk0 + p)), full(blk0 + npair), full(blk0 + 2 * npair)]
        + [pl.BlockSpec(memory_space=pltpu.VMEM)] * ng_arr,
        out_specs=[out, out, pl.BlockSpec(memory_space=pltpu.SMEM)] + [_ANY] * ng_arr,
        out_shape=[jax.ShapeDtypeStruct((lp, npair * HEAD_W), F32)] * 2 + [jax.ShapeDtypeStruct((npair, nb), F32)]
        + _gather_out_shapes(g_srcs, g_dtypes),
        scratch_shapes=_gather_scratch(g_srcs, g_dtypes) if ng_arr else [],
        compiler_params=_cp(("arbitrary", "arbitrary"), has_side_effects=bool(ng_arr)),
    )(p0, p0, p0, *g_srcs)


def _sb_bwd(p0, tot, nproc, dsrc, d_blk0, pad, *, name, scatter=()):
    lp = p0.shape[0]
    nb = lp // SB_BLOCK
    npair = SB_HEADS // 2
    blk0 = AB_SB // HEAD_W
    scale = SB_DH ** -0.5
    gw = SB_GROUP * SB_BLOCK
    assert pad < SB_BLOCK
    ns = len(scatter)

    def body(q_ref, k_ref, v_ref, tot_ref, nproc_ref, do_ref, *rest):
        s_ins, (dq_ref, dkt_ref, dvt_ref) = rest[:ns], rest[ns:ns + 3]
        s_outs, s_sems = rest[ns + 3:2 * ns + 3], rest[2 * ns + 3:]
        if ns:
            s_start, s_finish = _scatter_phases(s_ins, s_outs, *s_sems)
            pl.when((pl.program_id(0) == 0) & (pl.program_id(1) == 0))(s_start)
        i = pl.program_id(1)

        @pl.when(i == 0)
        def _():
            dkt_ref[...] = jnp.zeros_like(dkt_ref)
            dvt_ref[...] = jnp.zeros_like(dvt_ref)

        qs = _sb_stack(q_ref[...], scale)
        dos = _sb_stack(do_ref[...])
        qst, dost = qs.T, dos.T
        totv = tot_ref[...]
        ones = jnp.ones((1, HEAD_W), F32)
        tots = jnp.concatenate([totv[:, 0:1] * ones, totv[:, SB_DH:SB_DH + 1] * ones], axis=0)
        qpos = i * SB_BLOCK + lax.broadcasted_iota(jnp.int32, (SB_BLOCK, 1), 0)
        qpos = jnp.concatenate([qpos, qpos], axis=0)
        incl, incl0 = _sb_cat("incl"), _sb_cat("incl", pad)
        before = _sb_cat("before")
        ng = i // SB_GROUP
        used = jnp.clip(nproc_ref[pl.program_id(0), i].astype(jnp.int32), 0, ng)

        def dscore(z, e, ev, dl1m):
            r = 1.0 / (1.0 + e)
            sg = jnp.where(z >= 0, r, e * r)
            return ev * (1.0 - sg) - dl1m * sg

        def group(off, nblk, first_incl, allowed, carry):
            dq, prun, erun = carry
            width = nblk * SB_BLOCK
            kg = k_ref[pl.ds(off, width), :].astype(BF16)
            vg = v_ref[pl.ds(off, width), :].astype(BF16)
            z = _dot(qs, kg, NT)
            lsz, l1m, e = _sb_logsig(z)
            if allowed is not None:
                l1m = jnp.where(allowed, l1m, 0.0)
            dwgt = _dot(dos, vg, NT)
            dzs = [None] * nblk
            wgts = [None] * nblk
            for g in range(nblk):
                sl = slice(g * SB_BLOCK, (g + 1) * SB_BLOCK)
                al = _sb_cumsum(l1m[:, sl], first_incl if g == 0 else incl)
                wgt = jnp.exp(jnp.minimum(lsz[:, sl] + (tots - prun - al[:, :SB_BLOCK]), 0.0))
                if allowed is not None:
                    wgt = jnp.where(allowed[:, sl], wgt, 0.0)
                prun = prun + al[:, SB_BLOCK:]
                ev = wgt * dwgt[:, sl]
                el = _sb_cumsum(ev, before)
                dzs[g] = dscore(z[:, sl], e[:, sl], ev, erun + el[:, :SB_BLOCK])
                erun = erun + el[:, SB_BLOCK:]
                wgts[g] = wgt
            dz = jnp.concatenate(dzs, axis=1)
            if allowed is not None:
                dz = jnp.where(allowed, dz, 0.0)
            dz = dz.astype(BF16)
            wg = jnp.concatenate(wgts, axis=1).astype(BF16)
            dkt_ref[:, pl.ds(off, width)] += _dot(qst, dz, NN)
            dvt_ref[:, pl.ds(off, width)] += _dot(dost, wg, NN)
            return dq + _dot(dz, kg, NN), prun, erun

        def below(gi, carry):
            return group(pl.multiple_of(gi * gw, gw), SB_GROUP, jnp.where(gi == 0, incl0, incl), None, carry)

        zero = tuple(jnp.zeros((2 * SB_BLOCK, HEAD_W), F32) for _ in range(3))
        carry = lax.fori_loop(ng - used, ng, below, zero)
        top = ng * gw

        def top_group(nblk, carry):
            off = pl.multiple_of(jnp.minimum(top, lp - nblk * SB_BLOCK), SB_BLOCK)
            kpos = off + lax.broadcasted_iota(jnp.int32, (1, nblk * SB_BLOCK), 1)
            return group(off, nblk, incl, (kpos < qpos) & (kpos >= pad) & (kpos >= top), carry)

        dq, _, _ = lax.cond(i - ng * SB_GROUP < SB_GROUP // 2, functools.partial(top_group, SB_GROUP // 2),
                            functools.partial(top_group, SB_GROUP), carry)
        dq_ref[...] = _sb_unstack(dq) * scale
        if ns:
            pl.when((pl.program_id(0) == npair - 1) & (pl.program_id(1) == nb - 1))(s_finish)

    full = lambda c0: pl.BlockSpec((lp, HEAD_W), lambda p, i: (0, c0 + p))
    qb = lambda c0: pl.BlockSpec((SB_BLOCK, HEAD_W), lambda p, i: (i, c0 + p))
    tr = pl.BlockSpec((HEAD_W, lp), lambda p, i: (p, 0))
    return pl.pallas_call(
        body, name=name, grid=(npair, nb),
        in_specs=[qb(blk0), full(blk0 + npair), full(blk0 + 2 * npair), qb(0), pl.BlockSpec(memory_space=pltpu.SMEM),
                  qb(d_blk0)] + [_ANY] * ns,
        out_specs=[qb(0), tr, tr] + [_ANY] * ns,
        out_shape=[jax.ShapeDtypeStruct((lp, npair * HEAD_W), F32)]
        + [jax.ShapeDtypeStruct((npair * HEAD_W, lp), F32)] * 2
        + [jax.ShapeDtypeStruct(s.shape, s.dtype) for s in scatter],
        scratch_shapes=_scatter_scratch(ns) if ns else [],
        compiler_params=_cp(("arbitrary", "arbitrary"), has_side_effects=bool(ns)),
    )(p0, p0, p0, tot, nproc, dsrc, *scatter)


def _local_step(h0, target, pad, wts, hooks=None):
    lp = h0.shape[0]
    tm = _row_tile(lp, 1056)
    tkl = tm
    tml = _row_tile(lp, 528)
    d = D_MODEL
    mm = _mm
    mmw = functools.partial(_mm, out_dtype=BF16)
    g = {}

    h0_b = h0.astype(BF16)
    p0 = mm(h0_b, wts["w_ab"], "NN", tm=tm, tn=768, tk=d, name="l0_in_proj")
    ob, sb_tot, sb_used, *gathered = _sb_fwd(p0, pad, name="sb_fwd", gather=hooks["gather_a"] if hooks else None)
    if hooks:
        wts = {**wts, **hooks["weights_a"](gathered)}
    qkv = _gdn_pre_fwd(p0, wts["conv_w"], pad, name="gdn_pre_fwd")
    oa_raw, gdn_states, *gathered = _gdn_fwd(qkv, p0, wts["alog_v"], wts["dtb_v"], pad, name="gdn_fwd",
                                             gather=hooks["gather_b"] if hooks else None)
    if hooks:
        wts = {**wts, **hooks["weights_b"](gathered)}
    rows = lambda a, n: a.reshape(N_DEV, n // N_DEV, d)
    parts = g["parts"] = {}
    oab = _gate_fwd(oa_raw, p0, AB_Z // HEAD_W, wts["ab_gn"], ob, heads=GDN_HEADS, name="gdn_gate_fwd")
    ln = lambda kind, layer: (wts[f"ln_{kind}_g"][layer], wts[f"ln_{kind}_b"][layer])
    pre_mix0, h0a, h0a_b = mm(oab, wts["w_out0"], "NN", tm=tml, tn=d, tk=d, epi="ln", c=h0, scale=DN_ALPHA,
                              ln=ln("mix", 0), name="l0_out_proj")
    u0, act0 = mm(h0a_b, wts["w1"][0], "NN", tm=tm, tn=512, tk=d, b_dev=True, epi="relu2_copy", name="mlp0_up")
    pre_ffn0, h0b, h0b_b = mm(act0, wts["w2"][0], "NN", tm=tml, tn=d, tk=d, epi="ln", c=h0a, scale=DN_ALPHA,
                              ln=ln("ffn", 0), name="mlp0_down")
    p1 = mm(h0b_b, wts["w_c"], "NN", tm=tm, tn=512, tk=d, b_dev=True, name="l1_in_proj")
    oc_raw, hg_states, *gathered = _hg_fwd(p1, wts["lb"], pad, name="hg_fwd",
                                           gather=hooks["gather_c"] if hooks else None)
    if hooks:
        third = hooks["weights_c"](gathered)
        wts = {**wts, "w1": wts["w1"] + third["w1"], "w2": wts["w2"] + third["w2"]}
    oc = _gate_fwd(oc_raw, p1, 3 * HG_HEADS, wts["c_gn"], oc_raw, heads=HG_HEADS, name="hg_gate_fwd")
    pre_mix1, h1a, h1a_b = mm(oc, wts["w_out1"], "NN", tm=tml, tn=d, tk=d, epi="ln", c=h0b, scale=DN_ALPHA,
                              ln=ln("mix", 1), name="l1_out_proj")
    u1, act1 = mm(h1a_b, wts["w1"][1], "NN", tm=tm, tn=512, tk=d, b_dev=True, epi="relu2_copy", name="mlp1_up")
    pre_ffn1, h1b, _ = mm(act1, wts["w2"][1], "NN", tm=tml, tn=d, tk=d, epi="ln", c=h1a, scale=DN_ALPHA,
                          ln=ln("ffn", 1), name="mlp1_down")
    dy, loss_vec = _loss_head(h1b, target, name="loss_head")

    def mlp_bwd(layer, h_in_b, u, act, dpre, dpre_b, pre_mix):
        du = mm(dpre_b, wts["w2"][layer], "NT", tm=tm, tn=1024, tk=d, epi="relu2grad", c=u, out_dtype=BF16,
                name=f"mlp{layer}_d_hidden")
        dw2 = mmw(act, dpre_b, "TN", tm=1024, tn=1024, tk=tkl, name=f"mlp{layer}_dw2")
        dw1 = mmw(h_in_b, du, "TN", tm=1024, tn=512, tk=tkl, out_dev=True, name=f"mlp{layer}_dw1")
        return (*mm(du, k_major(wts["w1"][layer]), "NT", tm=tml, tn=1024, tk=2048, epi="ln_bwd", c=dpre, scale=DN_ALPHA,
                    ln=(pre_mix, wts["ln_mix_g"][layer]), name=f"mlp{layer}_d_in"), dw1, dw2)

    k_major = lambda wd: wd.transpose(1, 0, 2).reshape(wd.shape[1], -1)

    ln_ffn_dg, ln_ffn_db, ln_mix_dg, ln_mix_db, dw1s, dw2s = ([None, None] for _ in range(6))
    dpre, dpre_b, ln_ffn_dg[1], ln_ffn_db[1] = _ln_bwd(pre_ffn1, wts["ln_ffn_g"][1], dy, name="ln_ffn1_bwd")
    dpre, dpre_b, ln_mix_dg[1], ln_mix_db[1], dw1s[1], dw2s[1] = mlp_bwd(1, h1a_b, u1, act1, dpre, dpre_b, pre_mix1)
    g["c_w_out"] = mmw(oc, dpre_b, "TN", tm=1024, tn=1024, tk=tkl, name="l1_dw_out")
    doc = mm(dpre_b, wts["w_out1"], "NT", tm=tm, tn=1024, tk=d, name="l1_d_gate")
    doc_raw, dz1, g["c_gn"] = _gate_bwd(oc_raw, p1, 3 * HG_HEADS, wts["c_gn"], doc, heads=HG_HEADS, name="hg_gate_bwd")
    ready = [dw1s[1], rows(dw2s[1], D_FF), rows(g["c_w_out"], d)] if hooks else ()
    dq1, df1, di1, g["lb"], *got = _hg_bwd(p1, wts["lb"], hg_states, doc_raw, pad, name="hg_bwd", scatter=ready)
    parts.update(zip(("mlp_w1_1", "mlp_w2_1", "c_w_out"), got))
    dp1 = jnp.concatenate([dq1, df1, di1, dz1], axis=1).astype(BF16)
    g["c_w_in"] = mmw(h0b_b, dp1, "TN", tm=1024, tn=512, tk=tkl, out_dev=True, name="l1_dw_in")
    dpre, dpre_b, ln_ffn_dg[0], ln_ffn_db[0] = mm(
        dp1, k_major(wts["w_c"]), "NT", tm=tml, tn=1024, tk=2048, epi="ln_bwd", c=dpre, scale=DN_ALPHA,
        ln=(pre_ffn0, wts["ln_ffn_g"][0]), name="l1_d_in")
    dpre, dpre_b, ln_mix_dg[0], ln_mix_db[0], dw1s[0], dw2s[0] = mlp_bwd(0, h0a_b, u0, act0, dpre, dpre_b, pre_mix0)
    g["ab_w_out"] = mmw(oab, dpre_b, "TN", tm=1024, tn=1024, tk=tkl, name="l0_dw_out")
    doab = mm(dpre_b, wts["w_out0"], "NT", tm=tm, tn=1024, tk=d, name="l0_d_gate")
    doa_raw, dz0, g["ab_gn"] = _gate_bwd(oa_raw, p0, AB_Z // HEAD_W, wts["ab_gn"], doab, heads=GDN_HEADS,
                                         name="gdn_gate_bwd")
    ready = [g["c_w_in"]] if hooks else ()
    dqb, dkb_t, dvb_t, *got = _sb_bwd(p0, sb_tot, sb_used, doab, GDN_HEADS, pad, name="sb_bwd", scatter=ready)
    parts.update(zip(("c_w_in",), got))
    dkb, dvb = dkb_t.T, dvb_t.T
    ready = [dw1s[0], rows(dw2s[0], D_FF), rows(g["ab_w_out"], d)] if hooks else ()
    dqn, dkn, dvn, dba, g["alog_v"], g["dtb_v"], *got = _gdn_bwd(qkv, p0, wts["alog_v"], wts["dtb_v"], gdn_states,
                                                                 doa_raw, pad, name="gdn_bwd", scatter=ready)
    parts.update(zip(("mlp_w1_0", "mlp_w2_0", "ab_w_out"), got))
    dconv_in, g["conv_w"] = _gdn_pre_bwd(p0, wts["conv_w"], jnp.concatenate([dqn, dkn, dvn], axis=1), pad,
                                         name="gdn_pre_bwd")
    dp0 = jnp.concatenate([dconv_in, dz0, dqb, dkb, dvb, dba, jnp.zeros((lp, AB_CAT - AB_BA - HEAD_W), F32)],
                          axis=1).astype(BF16)
    g["w_ab"] = mmw(h0_b, dp0, "TN", tm=1024, tn=768, tk=tkl, name="l0_dw_in")
    last = ()
    if hooks:
        gab, ba0 = g["w_ab"], AB_Z + GDN_HEADS * HEAD_W
        gab = jnp.concatenate([gab[:, :ba0], gab[:, AB_BA:AB_BA + 2 * GDN_HEADS], gab[:, ba0:AB_BA]], axis=1)
        last = [gab.reshape(d, N_DEV, AB_IN // N_DEV).transpose(1, 0, 2)]
    res = mm(dp0, wts["w_ab"], "NT", tm=tm, tn=1024, tk=1920, epi="add", c=dpre, scale=DN_ALPHA, scatter=last,
             name="l0_d_in")
    dh0 = res[0] if last else res
    parts.update(zip(("ab_w_in",), res[1:] if last else ()))

    g["w1"], g["w2"] = dw1s, dw2s
    g["ln_mix_g"] = jnp.concatenate(ln_mix_dg, axis=0)
    g["ln_mix_b"] = jnp.concatenate(ln_mix_db, axis=0)
    g["ln_ffn_g"] = jnp.concatenate(ln_ffn_dg, axis=0)
    g["ln_ffn_b"] = jnp.concatenate(ln_ffn_db, axis=0)
    return loss_vec, dh0, g


N_CHIP = N_DEV // 2


def _place():
    x, y, c = lax.axis_index("x"), lax.axis_index("y"), lax.axis_index("c")
    return x, y, c, 2 * x + y


def _chip_dev(chip, core):
    return (chip // 2, chip % 2, core)


def _remote(src, dst, send_sem, recv_sem, dev):
    return pltpu.make_async_remote_copy(src_ref=src, dst_ref=dst, send_sem=send_sem, recv_sem=recv_sem,
                                        device_id=dev, device_id_type=pl.DeviceIdType.MESH)


_ANY = pl.BlockSpec(memory_space=pl.ANY)


def _gather(srcs, dtypes, *, name):
    n = len(srcs)

    def body(*refs):
        start, forward, finish = _gather_phases(refs[:n], refs[n:2 * n], refs[2 * n:3 * n], *refs[3 * n:], dtypes)
        start()
        forward()
        finish()

    return pl.pallas_call(
        body, name=name, in_specs=[pl.BlockSpec(memory_space=pltpu.VMEM)] * n, out_specs=[_ANY] * n,
        out_shape=_gather_out_shapes(srcs, dtypes), scratch_shapes=_gather_scratch(srcs, dtypes),
        compiler_params=_cp(has_side_effects=True),
    )(*srcs)


def _gather_out_shapes(srcs, dtypes):
    return [jax.ShapeDtypeStruct((N_DEV, *s.shape), dt) for s, dt in zip(srcs, dtypes)]


def _gather_scratch(srcs, dtypes):
    n = len(srcs)
    return [pltpu.VMEM(s.shape, dt) for s, dt in zip(srcs, dtypes)] + [
        pltpu.SemaphoreType.DMA((n, 2 * N_CHIP - 1)), pltpu.SemaphoreType.DMA((n, 2 * N_CHIP - 1)),
        pltpu.SemaphoreType.DMA((n,))]


def _gather_phases(ins, outs, stages, send_sems, recv_sems, local_sems, dtypes):
    n = len(ins)
    x, y, c, chip = _place()
    me = 2 * chip + c
    sibling = (x, y, 1 - c)

    def own(i):
        cps = [_remote(stages[i], outs[i].at[me], send_sems.at[i, 0], recv_sems.at[i, 0], sibling)]
        for j in range(1, N_CHIP):
            cps.append(_remote(stages[i], outs[i].at[me], send_sems.at[i, j], recv_sems.at[i, j],
                               _chip_dev(jnp.bitwise_xor(chip, j), c)))
        return cps

    def local(i):
        return pltpu.make_async_copy(stages[i], outs[i].at[me], local_sems.at[i])

    def passed_on(i, j):
        slot = outs[i].at[2 * jnp.bitwise_xor(chip, j) + c]
        return _remote(slot, slot, send_sems.at[i, N_CHIP - 1 + j], recv_sems.at[i, N_CHIP - 1 + j], sibling)

    def start():
        for i in range(n):
            stages[i][...] = ins[i][...].astype(dtypes[i])
            local(i).start()
            for cp in own(i):
                cp.start()

    def forward():
        for i in range(n):
            for j in range(1, N_CHIP):
                own(i)[j].wait_recv()
                passed_on(i, j).start()

    def finish():
        for i in range(n):
            own(i)[0].wait_recv()
            for j in range(1, N_CHIP):
                passed_on(i, j).wait_recv()
        for i in range(n):
            for cp in own(i):
                cp.wait_send()
            for j in range(1, N_CHIP):
                passed_on(i, j).wait_send()
            local(i).wait()

    return start, forward, finish


def _scatter_scratch(n):
    return [pltpu.SemaphoreType.DMA((n, N_DEV - 1)), pltpu.SemaphoreType.DMA((n, N_DEV - 1)),
            pltpu.SemaphoreType.DMA((n,))]


def _scatter_phases(ins, outs, send_sems, recv_sems, local_sems):
    n = len(ins)
    _, _, c, chip = _place()
    me = 2 * chip + c

    def copies():
        cps = []
        for i in range(n):
            cps.append(pltpu.make_async_copy(ins[i].at[me], outs[i].at[me], local_sems.at[i]))
            for k in range(1, N_DEV):
                peer = jnp.bitwise_xor(me, k)
                cps.append(_remote(ins[i].at[peer], outs[i].at[me], send_sems.at[i, k - 1], recv_sems.at[i, k - 1],
                                   _chip_dev(peer // 2, peer % 2)))
        return cps

    def start():
        for cp in copies():
            cp.start()

    def finish():
        for cp in copies():
            cp.wait()

    return start, finish


def _adamw(w, parts, m, v, *, name):
    r, c = w.shape
    s = parts.shape[0]
    tm = _row_tile(r, 128) if r % 8 == 0 else r
    c1 = 1.0 - ADAM_B1 ** ADAM_STEP
    c2 = 1.0 - ADAM_B2 ** ADAM_STEP

    def body(w_ref, p_ref, m_ref, v_ref, g_ref, d_ref, m2_ref, v2_ref):
        g = p_ref[0].astype(F32)
        for j in range(1, s):
            g = g + p_ref[j].astype(F32)
        m2 = ADAM_B1 * m_ref[...] + (1.0 - ADAM_B1) * g
        v2 = ADAM_B2 * v_ref[...] + (1.0 - ADAM_B2) * jnp.square(g)
        g_ref[...] = g
        m2_ref[...] = m2
        v2_ref[...] = v2
        d_ref[...] = -ADAM_LR * ((m2 / c1) / (jnp.sqrt(v2 / c2) + ADAM_EPS) + ADAM_WD * w_ref[...])

    blk = pl.BlockSpec((tm, c), lambda i: (i, 0))
    return pl.pallas_call(
        body, name=name, grid=(r // tm,),
        in_specs=[blk, pl.BlockSpec((s, tm, c), lambda i: (0, i, 0)), blk, blk], out_specs=[blk] * 4,
        out_shape=[jax.ShapeDtypeStruct((r, c), F32)] * 4, compiler_params=_cp(("parallel",)),
    )(w, parts, m, v)


_WEIGHTS = ("meta_tokens", "ab_w_in", "ab_conv_w", "ab_a_log", "ab_dt_bias", "ab_gnorm_g", "ab_w_out", "c_w_in",
            "c_lb_raw", "c_gnorm_g", "c_w_out", "ln_mix_g", "ln_mix_b", "mlp_w1", "mlp_w2", "ln_ffn_g", "ln_ffn_b")
_PACK_ROWS = (("ln_mix_g", 0), ("ln_mix_b", 2), ("ln_ffn_g", 4), ("ln_ffn_b", 6), ("c_lb_raw", 8))
_PACK_MISC_ROW = 10
_PACK_MISC = (("ab_gnorm_g", 0, 128), ("c_gnorm_g", 128, 128), ("ab_a_log", 256, GDN_HEADS), ("ab_dt_bias", 260, GDN_HEADS))
_PACK_N = 16
_SMALL_META = 16
_SMALL_CONV = 32
_SMALL_N = 40


def _pack_replicated(p):
    rows = jnp.zeros((_PACK_N, D_MODEL), F32)
    for name, r0 in _PACK_ROWS:
        rows = rows.at[r0:r0 + 2].set(p[name])
    for name, c0, width in _PACK_MISC:
        rows = rows.at[_PACK_MISC_ROW, c0:c0 + width].set(p[name].reshape(width))
    return rows


def _unpack_replicated(rows, like):
    out = {}
    for name, r0 in _PACK_ROWS:
        out[name] = rows[r0:r0 + 2]
    for name, c0, width in _PACK_MISC:
        out[name] = rows[_PACK_MISC_ROW, c0:c0 + width].reshape(like[name].shape)
    return out


def _lower_bound(c_lb_raw):
    lb_all = jnp.cumsum(jax.nn.softmax(c_lb_raw.astype(F32), axis=0), axis=0)
    return (lb_all - lb_all[0:1])[1].reshape(1, -1)


def kernel(x, meta_tokens, ab_w_in, ab_conv_w, ab_a_log, ab_dt_bias, ab_gnorm_g, ab_w_out, c_w_in, c_lb_raw, c_gnorm_g, c_w_out, ln_mix_g, ln_mix_b, mlp_w1, mlp_w2, ln_ffn_g, ln_ffn_b, loss_target, m_meta_tokens, m_ab_w_in, m_ab_conv_w, m_ab_a_log, m_ab_dt_bias, m_ab_gnorm_g, m_ab_w_out, m_c_w_in, m_c_lb_raw, m_c_gnorm_g, m_c_w_out, m_ln_mix_g, m_ln_mix_b, m_mlp_w1, m_mlp_w2, m_ln_ffn_g, m_ln_ffn_b, v_meta_tokens, v_ab_w_in, v_ab_conv_w, v_ab_a_log, v_ab_dt_bias, v_ab_gnorm_g, v_ab_w_out, v_c_w_in, v_c_lb_raw, v_c_gnorm_g, v_c_w_out, v_ln_mix_g, v_ln_mix_b, v_mlp_w1, v_mlp_w2, v_ln_ffn_g, v_ln_ffn_b):
    w = dict(zip(_WEIGHTS, (meta_tokens, ab_w_in, ab_conv_w, ab_a_log, ab_dt_bias, ab_gnorm_g, ab_w_out, c_w_in, c_lb_raw,
                            c_gnorm_g, c_w_out, ln_mix_g, ln_mix_b, mlp_w1, mlp_w2, ln_ffn_g, ln_ffn_b)))
    mom = dict(zip(_WEIGHTS, (m_meta_tokens, m_ab_w_in, m_ab_conv_w, m_ab_a_log, m_ab_dt_bias, m_ab_gnorm_g, m_ab_w_out,
                              m_c_w_in, m_c_lb_raw, m_c_gnorm_g, m_c_w_out, m_ln_mix_g, m_ln_mix_b, m_mlp_w1, m_mlp_w2,
                              m_ln_ffn_g, m_ln_ffn_b)))
    var = dict(zip(_WEIGHTS, (v_meta_tokens, v_ab_w_in, v_ab_conv_w, v_ab_a_log, v_ab_dt_bias, v_ab_gnorm_g, v_ab_w_out,
                              v_c_w_in, v_c_lb_raw, v_c_gnorm_g, v_c_w_out, v_ln_mix_g, v_ln_mix_b, v_mlp_w1, v_mlp_w2,
                              v_ln_ffn_g, v_ln_ffn_b)))
    me = 4 * lax.axis_index("x") + 2 * lax.axis_index("y") + lax.axis_index("c")
    seq = x.shape[1]
    pad = (-(N_META + seq)) % SB_BLOCK
    lp = pad + N_META + seq
    meta_w = D_MODEL // N_DEV
    conv_w_all = 2 * GDN_HEADS * HEAD_W + GDN_HEADS * HEAD_W
    conv_w_mine = conv_w_all // N_DEV

    g_meta, g_conv, g_ab_in = _gather([w["meta_tokens"], w["ab_conv_w"][0], w["ab_w_in"][0]], [F32, F32, BF16],
                                      name="gather_weights_first")
    meta_full = g_meta.transpose(1, 0, 2).reshape(N_META, D_MODEL)
    conv_full = g_conv.transpose(1, 0, 2).reshape(CONV_K, conv_w_all)
    ab_full = g_ab_in.transpose(1, 0, 2).reshape(D_MODEL, AB_IN)
    ba0 = AB_Z + 512
    w_ab = jnp.concatenate([ab_full[:, :ba0], ab_full[:, ba0 + 2 * GDN_HEADS:], ab_full[:, ba0:ba0 + 2 * GDN_HEADS],
                            jnp.zeros((D_MODEL, AB_CAT - AB_IN), BF16)], axis=1)
    vec128 = lambda p: jnp.zeros((1, HEAD_W), F32).at[0, :GDN_HEADS].set(p.reshape(GDN_HEADS))
    wts = dict(
        w_ab=w_ab, conv_w=conv_full, alog_v=vec128(w["ab_a_log"]), dtb_v=vec128(w["ab_dt_bias"]),
        ab_gn=w["ab_gnorm_g"][0], lb=_lower_bound(w["c_lb_raw"]), c_gn=w["c_gnorm_g"][0],
        ln_mix_g=w["ln_mix_g"], ln_mix_b=w["ln_mix_b"], ln_ffn_g=w["ln_ffn_g"], ln_ffn_b=w["ln_ffn_b"])

    def weights_a(gathered):
        g_ab_out, g_w1, g_w2 = gathered
        return dict(w_out0=g_ab_out.reshape(D_MODEL, D_MODEL), w1=[g_w1], w2=[g_w2.reshape(D_FF, D_MODEL)])

    def weights_b(gathered):
        g_c_in, g_c_out = gathered
        return dict(w_c=g_c_in, w_out1=g_c_out.reshape(D_MODEL, D_MODEL))

    def weights_c(gathered):
        g_w1, g_w2 = gathered
        return dict(w1=[g_w1], w2=[g_w2.reshape(D_FF, D_MODEL)])

    hooks = dict(
        gather_a=([w["ab_w_out"][0], w["mlp_w1"][0], w["mlp_w2"][0]], [BF16] * 3), weights_a=weights_a,
        gather_b=([w["c_w_in"][0], w["c_w_out"][0]], [BF16] * 2), weights_b=weights_b,
        gather_c=([w["mlp_w1"][1], w["mlp_w2"][1]], [BF16] * 2), weights_c=weights_c)

    h0 = jnp.concatenate([jnp.zeros((pad, D_MODEL), F32), meta_full, x[0]], axis=0)
    loss_vec, dh0, g = _local_step(h0, loss_target[0], pad, wts, hooks)
    loss = lax.psum(jnp.sum(loss_vec), ("x", "y", "c"))
    grad_x = dh0[lp - seq:][None]

    _, lb_vjp = jax.vjp(_lower_bound, w["c_lb_raw"])
    rep_part = _pack_replicated(dict(
        ln_mix_g=g["ln_mix_g"], ln_mix_b=g["ln_mix_b"], ln_ffn_g=g["ln_ffn_g"], ln_ffn_b=g["ln_ffn_b"],
        c_lb_raw=lb_vjp(g["lb"])[0], ab_gnorm_g=g["ab_gn"], c_gnorm_g=g["c_gn"],
        ab_a_log=g["alog_v"][0, :GDN_HEADS], ab_dt_bias=g["dtb_v"][0, :GDN_HEADS]))
    small = jnp.concatenate([rep_part, dh0[pad:pad + N_META], g["conv_w"].reshape(-1, D_MODEL),
                             jnp.zeros((_SMALL_N - _SMALL_CONV - CONV_K * conv_w_all // D_MODEL, D_MODEL), F32)], axis=0)
    (small_all,) = _gather([small], [F32], name="gather_small_grads")
    rep_out = _adamw(_pack_replicated(w), small_all[:, :_PACK_N], _pack_replicated(mom), _pack_replicated(var),
                     name="adamw_replicated")
    meta_parts = lax.dynamic_slice_in_dim(small_all[:, _SMALL_META:_SMALL_META + N_META], me * meta_w, meta_w, axis=2)
    meta_out = _adamw(w["meta_tokens"], meta_parts, mom["meta_tokens"], var["meta_tokens"], name="adamw_meta")
    conv_parts = small_all[:, _SMALL_CONV:_SMALL_CONV + CONV_K * conv_w_all // D_MODEL].reshape(N_DEV, CONV_K, conv_w_all)
    conv_parts = lax.dynamic_slice_in_dim(conv_parts, me * conv_w_mine, conv_w_mine, axis=2)
    conv_out = _adamw(w["ab_conv_w"][0], conv_parts, mom["ab_conv_w"][0], var["ab_conv_w"][0], name="adamw_conv")

    parts = g["parts"]
    big = [("ab_w_in", 0, parts["ab_w_in"]), ("ab_w_out", 0, parts["ab_w_out"]), ("mlp_w1", 0, parts["mlp_w1_0"]),
           ("mlp_w2", 0, parts["mlp_w2_0"]), ("c_w_in", 0, parts["c_w_in"]), ("c_w_out", 0, parts["c_w_out"]),
           ("mlp_w1", 1, parts["mlp_w1_1"]), ("mlp_w2", 1, parts["mlp_w2_1"])]
    big_out = {}
    for name, l, p in big:
        res = _adamw(w[name][l], p, mom[name][l], var[name][l], name=f"adamw_{name}{l}")
        big_out.setdefault(name, []).append(res)

    rep = [_unpack_replicated(r, w) for r in rep_out]
    outs = {}
    for name in _WEIGHTS:
        if name == "meta_tokens":
            outs[name] = list(meta_out)
        elif name == "ab_conv_w":
            outs[name] = [o[None] for o in conv_out]
        elif name in big_out:
            res = big_out[name]
            outs[name] = [o[None] for o in res[0]] if len(res) == 1 else [jnp.stack(pair) for pair in zip(*res)]
        else:
            outs[name] = [r[name] for r in rep]
    flat = [loss, grad_x]
    for kind in range(4):
        flat += [outs[name][kind] for name in _WEIGHTS]
    return tuple(flat)
```

```python
import functools

import jax
import jax.numpy as jnp
from jax import lax
from jax.experimental import pallas as pl
from jax.experimental.pallas import tpu as pltpu

F32 = jnp.float32
BF16 = jnp.bfloat16

N_DEV = 8
D_MODEL = 1024
N_META = 16
D_FF = 4096
DEPTH = 2
GDN_HEADS = 4
SB_HEADS = 8
SB_DH = 64
HG_HEADS = 8
HEAD_W = 128
CHUNK = 64
SB_BLOCK = 128
CONV_K = 4
DN_ALPHA = float((2 * DEPTH) ** 0.25)
LN_EPS = 1e-5
RMS_EPS = 1e-6
L2_EPS = 1e-6
ADAM_LR, ADAM_B1, ADAM_B2, ADAM_EPS, ADAM_WD, ADAM_STEP = 0.001, 0.9, 0.999, 1e-08, 0.01, 10

AB_Z = 1536
AB_SB = 2048
AB_BA = 3584
AB_CAT = 3840
AB_IN = 3592

VMEM_LIMIT = 56 * 1024 * 1024


def _cp(sem=None, **kw):
    if sem is not None:
        kw["dimension_semantics"] = sem
    return pltpu.CompilerParams(vmem_limit_bytes=VMEM_LIMIT, **kw)


def _row_tile(n, want):
    best = 8
    for t in range(8, min(n, want) + 1, 8):
        if n % t == 0:
            best = t
    return best


@jax.custom_vjp
def _sigmoid(x):
    e = jnp.exp(-jnp.abs(x))
    r = 1.0 / (1.0 + e)
    return jnp.where(x >= 0, r, e * r)


def _sigmoid_fwd(x):
    s = _sigmoid(x)
    return s, s


def _sigmoid_bwd(s, g):
    return (g * s * (1.0 - s),)


_sigmoid.defvjp(_sigmoid_fwd, _sigmoid_bwd)


def _log1p_exp_neg_abs(x):
    e = jnp.exp(-jnp.abs(x))
    return jnp.where(e < 1e-4, e - 0.5 * e * e, jnp.log(1.0 + e))


@jax.custom_vjp
def _softplus(x):
    return jnp.maximum(x, 0.0) + _log1p_exp_neg_abs(x)


def _softplus_fwd(x):
    return _softplus(x), x


def _softplus_bwd(x, g):
    return (g * _sigmoid(x),)


_softplus.defvjp(_softplus_fwd, _softplus_bwd)


def _silu(x):
    return x * _sigmoid(x)


def _silu_grad(x):
    s = _sigmoid(x)
    return s * (1.0 + x * (1.0 - s))


def _dot(a, b, dims, precision=None):
    return lax.dot_general(a, b, (dims, ((), ())), precision=precision, preferred_element_type=F32)


NN = ((1,), (0,))
NT = ((1,), (1,))
TN = ((0,), (0,))


def _bdot(a, b, dims):
    return _dot(a.astype(BF16), b.astype(BF16), dims)


def _layer_norm(pre, g, beta):
    mu = jnp.mean(pre, axis=-1, keepdims=True)
    xc = pre - mu
    var = jnp.mean(xc * xc, axis=-1, keepdims=True)
    return xc * lax.rsqrt(var + LN_EPS) * g + beta


def _layer_norm_bwd(pre, g, dy):
    mu = jnp.mean(pre, axis=-1, keepdims=True)
    xc = pre - mu
    rstd = lax.rsqrt(jnp.mean(xc * xc, axis=-1, keepdims=True) + LN_EPS)
    xhat = xc * rstd
    dxh = dy * g
    m1 = jnp.mean(dxh, axis=-1, keepdims=True)
    m2 = jnp.mean(dxh * xhat, axis=-1, keepdims=True)
    return (rstd * (dxh - m1 - xhat * m2), jnp.sum(dy * xhat, axis=0, keepdims=True),
            jnp.sum(dy, axis=0, keepdims=True))


def _mm(a, b, mode, *, tm, tn, tk, name, epi=None, c=None, scale=1.0, b_dev=False, out_dev=False, out_dtype=F32,
        ln=None, scatter=()):
    if mode == "NN":
        m, kk = a.shape
        n = b.shape[2] * N_DEV if b_dev else b.shape[1]
    elif mode == "NT":
        m, kk = a.shape
        n = b.shape[1] if b_dev else b.shape[0]
    else:
        kk, m = a.shape
        n = b.shape[1]
    assert m % tm == 0 and n % tn == 0 and kk % tk == 0, (name, m, n, kk, tm, tn, tk)
    nk = kk // tk
    dims = {"NN": NN, "NT": NT, "TN": TN}[mode]

    if mode == "TN":
        a_spec = pl.BlockSpec((tk, tm), lambda i, j, k: (k, i))
    else:
        a_spec = pl.BlockSpec((tm, tk), lambda i, j, k: (i, k))
    if mode == "NN":
        if b_dev:
            assert tn == b.shape[2]
            b_spec = pl.BlockSpec((None, tk, tn), lambda i, j, k: (j, k, 0))
        else:
            b_spec = pl.BlockSpec((tk, tn), lambda i, j, k: (k, j))
    elif mode == "NT":
        if b_dev:
            assert tk == b.shape[2]
            b_spec = pl.BlockSpec((None, tn, tk), lambda i, j, k: (k, j, 0))
        else:
            b_spec = pl.BlockSpec((tn, tk), lambda i, j, k: (j, k))
    else:
        b_spec = pl.BlockSpec((tk, tn), lambda i, j, k: (k, j))
    in_specs = [a_spec, b_spec]
    operands = [a, b]
    if c is not None:
        in_specs.append(pl.BlockSpec((tm, tn), lambda i, j, k: (i, j)))
        operands.append(c)
    if epi == "ln":
        assert tn == n and not out_dev
        in_specs += [pl.BlockSpec((1, n), lambda i, j, k: (0, 0))] * 2
        operands += [ln[0].reshape(1, n), ln[1].reshape(1, n)]
    elif epi == "ln_bwd":
        assert tn == n and not out_dev
        in_specs += [pl.BlockSpec((tm, tn), lambda i, j, k: (i, j)), pl.BlockSpec((1, n), lambda i, j, k: (0, 0))]
        operands += [ln[0], ln[1].reshape(1, n)]
    if out_dev:
        assert tn == n // N_DEV
        out_shape = jax.ShapeDtypeStruct((N_DEV, m, tn), out_dtype)
        out_spec = pl.BlockSpec((None, tm, tn), lambda i, j, k: (j, i, 0))
    else:
        out_shape = jax.ShapeDtypeStruct((m, n), out_dtype)
        out_spec = pl.BlockSpec((tm, tn), lambda i, j, k: (i, j))
    if epi == "ln":
        out_shape = [out_shape, out_shape, jax.ShapeDtypeStruct((m, n), BF16)]
        out_spec = [out_spec] * 3
    elif epi == "relu2_copy":
        assert not out_dev
        out_shape = [out_shape, jax.ShapeDtypeStruct((m, n), BF16)]
        out_spec = [out_spec] * 2
    elif epi == "ln_bwd":
        vec_shape, vec_spec = jax.ShapeDtypeStruct((1, n), F32), pl.BlockSpec((1, n), lambda i, j, k: (0, 0))
        out_shape = [out_shape, jax.ShapeDtypeStruct((m, n), BF16), vec_shape, vec_shape]
        out_spec = [out_spec, out_spec, vec_spec, vec_spec]
    n_out = {"ln": 3, "relu2_copy": 2, "ln_bwd": 4}.get(epi, 1)
    ns = len(scatter)
    if ns:
        in_specs += [_ANY] * ns
        operands += list(scatter)
        out_shape = (out_shape if n_out > 1 else [out_shape]) + [jax.ShapeDtypeStruct(s.shape, s.dtype) for s in scatter]
        out_spec = (out_spec if n_out > 1 else [out_spec]) + [_ANY] * ns
    n_in = len(operands)
    grid = (m // tm, n // tn, nk)

    def body(*refs):
        a_ref, b_ref = refs[0], refs[1]
        c_ref = refs[2] if c is not None else None
        o_ref = refs[n_in]
        scratch0 = n_in + n_out + ns
        acc_ref = refs[scratch0] if nk > 1 else None
        if ns:
            s_start, s_finish = _scatter_phases(refs[n_in - ns:n_in], refs[n_in + n_out:scratch0],
                                                *refs[scratch0 + (1 if nk > 1 else 0):])
            at = lambda step: functools.reduce(lambda x, y: x & y, [pl.program_id(ax) == step[ax] for ax in range(3)])
            pl.when(at((0, 0, 0)))(s_start)
        p = _dot(a_ref[...].astype(BF16), b_ref[...].astype(BF16), dims)
        first_rows = pl.program_id(0) == 0

        def finish(acc):
            if epi == "add":
                acc = acc + scale * c_ref[...]
            elif epi == "relu2grad":
                acc = acc * (2.0 * jnp.maximum(c_ref[...], 0.0))
            elif epi == "relu2_copy":
                refs[n_in + 1][...] = jnp.square(jnp.maximum(acc, 0.0)).astype(BF16)
            elif epi == "ln_bwd":
                acc, dg, db = _layer_norm_bwd(refs[3][...], refs[4][...], acc + scale * c_ref[...])
                dg_ref, db_ref = refs[n_in + 2], refs[n_in + 3]

                @pl.when(first_rows)
                def _():
                    dg_ref[...] = jnp.zeros_like(dg_ref)
                    db_ref[...] = jnp.zeros_like(db_ref)

                dg_ref[...] += dg
                db_ref[...] += db
                refs[n_in + 1][...] = acc.astype(BF16)
            elif epi == "ln":
                acc = acc + scale * c_ref[...]
                y = _layer_norm(acc, refs[3][...], refs[4][...])
                refs[n_in + 1][...] = y
                refs[n_in + 2][...] = y.astype(BF16)
            o_ref[...] = acc.astype(out_dtype)

        if nk == 1:
            finish(p)
        else:
            k = pl.program_id(2)

            @pl.when(k == 0)
            def _():
                acc_ref[...] = p

            @pl.when(k > 0)
            def _():
                acc_ref[...] += p

            @pl.when(k == nk - 1)
            def _():
                finish(acc_ref[...])

        if ns:
            pl.when(at(tuple(g - 1 for g in grid)))(s_finish)

    res = pl.pallas_call(
        body, name=name, grid=grid, in_specs=in_specs, out_specs=out_spec, out_shape=out_shape,
        scratch_shapes=([pltpu.VMEM((tm, tn), F32)] if nk > 1 else []) + (_scatter_scratch(ns) if ns else []),
        compiler_params=_cp(("arbitrary",) * 3 if ns or epi == "ln_bwd" else ("parallel", "parallel", "arbitrary"),
                            has_side_effects=bool(ns)),
    )(*operands)
    return res


def _ln_bwd(pre, g, dy, *, name):
    lp, d = pre.shape
    tm = _row_tile(lp, 512)

    def body(pre_ref, g_ref, dy_ref, dpre_ref, dpreb_ref, dg_ref, db_ref):
        dpre, dg, db = _layer_norm_bwd(pre_ref[...], g_ref[...], dy_ref[...])
        dpre_ref[...] = dpre
        dpreb_ref[...] = dpre.astype(BF16)

        @pl.when(pl.program_id(0) == 0)
        def _():
            dg_ref[...] = jnp.zeros_like(dg_ref)
            db_ref[...] = jnp.zeros_like(db_ref)

        dg_ref[...] += dg
        db_ref[...] += db

    row = pl.BlockSpec((tm, d), lambda i: (i, 0))
    vec = pl.BlockSpec((1, d), lambda i: (0, 0))
    return pl.pallas_call(
        body, name=name, grid=(lp // tm,), in_specs=[row, vec, row], out_specs=[row, row, vec, vec],
        out_shape=[jax.ShapeDtypeStruct((lp, d), F32), jax.ShapeDtypeStruct((lp, d), BF16),
                   jax.ShapeDtypeStruct((1, d), F32), jax.ShapeDtypeStruct((1, d), F32)],
        compiler_params=_cp(("arbitrary",)),
    )(pre, g.reshape(1, d), dy)


def _loss_head(y, target, *, name):
    lp, d = y.shape
    seq = target.shape[0]
    tm = SB_BLOCK
    first = (lp - seq) // tm
    assert (lp - seq) % tm == 0 and seq % tm == 0

    def body(y_ref, t_ref, dy_ref, loss_ref):
        i = pl.program_id(0)
        live = i >= first
        diff = jnp.where(live, y_ref[...] - t_ref[...], 0.0)
        dy_ref[...] = diff * (1.0 / d)

        @pl.when(i == 0)
        def _():
            loss_ref[...] = jnp.zeros_like(loss_ref)

        loss_ref[...] += jnp.sum(diff * diff, axis=0, keepdims=True) * (0.5 / d)

    return pl.pallas_call(
        body, name=name, grid=(lp // tm,),
        in_specs=[pl.BlockSpec((tm, d), lambda i: (i, 0)),
                  pl.BlockSpec((tm, d), lambda i: (jnp.maximum(i - first, 0), 0))],
        out_specs=[pl.BlockSpec((tm, d), lambda i: (i, 0)), pl.BlockSpec((1, d), lambda i: (0, 0))],
        out_shape=[jax.ShapeDtypeStruct((lp, d), F32), jax.ShapeDtypeStruct((1, d), F32)],
        compiler_params=_cp(("arbitrary",)),
    )(y, target)


def _gate_fwd(o, zsrc, z_blk0, g, other, *, heads, name):
    lp = o.shape[0]
    tm = _row_tile(lp, 512)
    w = heads * HEAD_W
    assert (z_blk0 * HEAD_W) % w == 0
    has_other = w < D_MODEL

    def body(o_ref, z_ref, g_ref, *rest):
        y_ref = rest[-1]
        gv = g_ref[...]
        for h in range(heads):
            cs = slice(h * HEAD_W, (h + 1) * HEAD_W)
            ov = o_ref[:, cs]
            r = lax.rsqrt(jnp.mean(ov * ov, axis=-1, keepdims=True) + RMS_EPS)
            y_ref[:, cs] = (ov * r * gv * _silu(z_ref[:, cs])).astype(BF16)
        if has_other:
            y_ref[:, w:] = rest[0][...].astype(BF16)

    row = lambda width, blk: pl.BlockSpec((tm, width), lambda i: (i, blk))
    return pl.pallas_call(
        body, name=name, grid=(lp // tm,),
        in_specs=[row(w, 0), row(w, z_blk0 * HEAD_W // w), pl.BlockSpec((1, HEAD_W), lambda i: (0, 0))]
        + ([row(D_MODEL - w, 0)] if has_other else []),
        out_specs=row(D_MODEL, 0), out_shape=jax.ShapeDtypeStruct((lp, D_MODEL), BF16),
        compiler_params=_cp(("parallel",)),
    )(o, zsrc, g.reshape(1, HEAD_W), *([other] if has_other else []))


def _gate_bwd(o, zsrc, z_blk0, g, dy, *, heads, name):
    lp = o.shape[0]
    tm = _row_tile(lp, 512)

    w = heads * HEAD_W
    assert (z_blk0 * HEAD_W) % w == 0

    def body(o_ref, z_ref, g_ref, dy_ref, do_ref, dz_ref, dg_ref):
        @pl.when(pl.program_id(0) == 0)
        def _():
            dg_ref[...] = jnp.zeros_like(dg_ref)

        gv = g_ref[...]
        dg = jnp.zeros((1, HEAD_W), F32)
        for h in range(heads):
            cs = slice(h * HEAD_W, (h + 1) * HEAD_W)
            ov, zv, dyv = o_ref[:, cs], z_ref[:, cs], dy_ref[:, cs]
            r = lax.rsqrt(jnp.mean(ov * ov, axis=-1, keepdims=True) + RMS_EPS)
            nrm = ov * r
            s = _silu(zv)
            dn = dyv * gv * s
            do_ref[:, cs] = r * (dn - nrm * jnp.mean(dn * nrm, axis=-1, keepdims=True))
            dz_ref[:, cs] = dyv * nrm * gv * _silu_grad(zv)
            dg = dg + jnp.sum(dyv * nrm * s, axis=0, keepdims=True)
        dg_ref[...] += dg

    row = lambda blk: pl.BlockSpec((tm, w), lambda i: (i, blk))
    vec = pl.BlockSpec((1, HEAD_W), lambda i: (0, 0))
    return pl.pallas_call(
        body, name=name, grid=(lp // tm,),
        in_specs=[row(0), row(z_blk0 * HEAD_W // w), vec, row(0)], out_specs=[row(0), row(0), vec],
        out_shape=[jax.ShapeDtypeStruct((lp, w), F32), jax.ShapeDtypeStruct((lp, w), F32),
                   jax.ShapeDtypeStruct((1, HEAD_W), F32)],
        compiler_params=_cp(("arbitrary",)),
    )(o, zsrc, g.reshape(1, HEAD_W), dy)


def _conv_taps(x, w):
    acc = w[CONV_K - 1:CONV_K, :] * x
    for k in range(CONV_K - 1):
        acc = acc + w[k:k + 1, :] * pltpu.roll(x, CONV_K - 1 - k, 0)
    return acc


def _gdn_pre_fwd(p0, conv_w, pad, *, name):
    lp = p0.shape[0]
    nq = GDN_HEADS
    qscale = HEAD_W ** -0.5

    def body(x_ref, w_ref, y_ref):
        j = pl.program_id(0)
        c = _conv_taps(x_ref[...], w_ref[...])
        s = _silu(c)
        r = lax.rsqrt(jnp.sum(s * s, axis=-1, keepdims=True) + L2_EPS)
        mult = jnp.where(j < nq, r * qscale, jnp.where(j < 2 * nq, r, 1.0))
        rows = lax.broadcasted_iota(jnp.int32, (lp, 1), 0)
        y_ref[...] = jnp.where(rows >= pad, s * mult, 0.0)

    return pl.pallas_call(
        body, name=name, grid=(3 * nq,),
        in_specs=[pl.BlockSpec((lp, HEAD_W), lambda j: (0, j)), pl.BlockSpec((CONV_K, HEAD_W), lambda j: (0, j))],
        out_specs=pl.BlockSpec((lp, HEAD_W), lambda j: (0, j)),
        out_shape=jax.ShapeDtypeStruct((lp, 3 * nq * HEAD_W), F32), compiler_params=_cp(("parallel",)),
    )(p0, conv_w)


def _gdn_pre_bwd(p0, conv_w, dqkv, pad, *, name):
    lp = p0.shape[0]
    nq = GDN_HEADS
    qscale = HEAD_W ** -0.5

    def body(x_ref, w_ref, dy_ref, dx_ref, dw_ref):
        j = pl.program_id(0)
        x, w = x_ref[...], w_ref[...]
        c = _conv_taps(x, w)
        s = _silu(c)
        r = lax.rsqrt(jnp.sum(s * s, axis=-1, keepdims=True) + L2_EPS)
        rows = lax.broadcasted_iota(jnp.int32, (lp, 1), 0)
        dy = jnp.where(rows >= pad, dy_ref[...], 0.0)
        nrm = s * r
        dn = dy * jnp.where(j < nq, qscale, 1.0)
        ds_norm = r * (dn - nrm * jnp.sum(nrm * dn, axis=-1, keepdims=True))
        ds = jnp.where(j < 2 * nq, ds_norm, dy)
        dc = ds * _silu_grad(c)
        dx = w[CONV_K - 1:CONV_K, :] * dc
        dws = [None] * CONV_K
        dws[CONV_K - 1] = jnp.sum(dc * x, axis=0, keepdims=True)
        for k in range(CONV_K - 1):
            sh = CONV_K - 1 - k
            dx = dx + w[k:k + 1, :] * pltpu.roll(dc, lp - sh, 0)
            dws[k] = jnp.sum(dc * pltpu.roll(x, sh, 0), axis=0, keepdims=True)
        dx_ref[...] = dx
        dw_ref[...] = jnp.concatenate(dws, axis=0)

    blk = pl.BlockSpec((lp, HEAD_W), lambda j: (0, j))
    wblk = pl.BlockSpec((CONV_K, HEAD_W), lambda j: (0, j))
    return pl.pallas_call(
        body, name=name, grid=(3 * nq,), in_specs=[blk, wblk, blk], out_specs=[blk, wblk],
        out_shape=[jax.ShapeDtypeStruct((lp, 3 * nq * HEAD_W), F32),
                   jax.ShapeDtypeStruct((CONV_K, 3 * nq * HEAD_W), F32)],
        compiler_params=_cp(("parallel",)),
    )(p0, conv_w, dqkv)


@jax.custom_vjp
def _inv_unit_lower(m):
    c = m.shape[0]
    eye = (lax.broadcasted_iota(jnp.int32, (c, c), 0) == lax.broadcasted_iota(jnp.int32, (c, c), 1)).astype(F32)
    x = eye - m
    p = m
    n = 2
    while n < CHUNK:
        p = _bdot(p, p, NN)
        x = x + _bdot(x, p, NN)
        n *= 2
    return x


def _inv_fwd(m):
    t = _inv_unit_lower(m)
    return t, t


def _inv_bwd(t, g):
    return (-_bdot(_bdot(t, g, TN), t, NT),)


_inv_unit_lower.defvjp(_inv_fwd, _inv_bwd)


GDN_STEP = 3


def _heads_to_rows(x, nh):
    return jnp.concatenate([x[:, h * HEAD_W:(h + 1) * HEAD_W] for h in range(nh)], axis=0)


def _rows_to_heads(x, nh):
    c = x.shape[0] // nh
    return jnp.concatenate([x[h * c:(h + 1) * c] for h in range(nh)], axis=1)


def _gdn_chunk(q, k, v, ba, alog, dtb, states, valid):
    nh = GDN_HEADS
    c = q.shape[0]
    r = nh * c
    lane = lax.broadcasted_iota(jnp.int32, (1, HEAD_W), 1)
    pick = lambda x, l: jnp.sum(jnp.where(lane == l, x, 0.0), axis=-1, keepdims=True)
    beta = jnp.concatenate([jnp.where(valid, _sigmoid(pick(ba, h)), 0.0) for h in range(nh)], axis=0)
    g = jnp.concatenate(
        [jnp.where(valid, -jnp.exp(pick(alog, h)) * _softplus(pick(ba, nh + h) + pick(dtb, h)), 0.0) for h in range(nh)],
        axis=0)
    qs, ks, vs = _heads_to_rows(q, nh), _heads_to_rows(k, nh), _heads_to_rows(v, nh)
    rr = lax.broadcasted_iota(jnp.int32, (r, r), 0)
    cc = lax.broadcasted_iota(jnp.int32, (r, r), 1)
    same = (rr // c) == (cc // c)
    causal, strict = same & (cc <= rr), same & (cc < rr)
    lower = jnp.where(causal, 1.0, 0.0).astype(BF16)
    upper = jnp.where(same & (cc >= rr), 1.0, 0.0).astype(BF16)
    gcb = _mask_mm(lower, upper, g * jnp.ones((1, HEAD_W), F32))
    gc_col = jnp.concatenate([gcb] * (r // HEAD_W), axis=1)
    decay = jnp.where(causal, jnp.exp(jnp.minimum(gc_col - gc_col.T, 0.0)), 0.0)
    egc = jnp.exp(gcb)
    kb = ks * beta
    m = jnp.where(strict, _dot3(kb, ks, NT) * decay, 0.0)
    t = _inv_unit_lower(m)
    u = _bdot(t, vs * beta, NN)
    w = _bdot(t, kb * egc, NN)
    a = _bdot(qs, ks, NT) * decay
    rows = lambda x, h: x[h * c:(h + 1) * c]
    qe = qs * egc
    v_new = u - jnp.concatenate([_bdot(rows(w, h), states[h], NN) for h in range(nh)], axis=0)
    o = jnp.concatenate([_bdot(rows(qe, h), states[h], NN) for h in range(nh)], axis=0) + _bdot(a, v_new, NN)
    new_states = []
    for h in range(nh):
        gl = gcb[(h + 1) * c - 1:(h + 1) * c, :]
        k_dec = rows(ks, h) * jnp.exp(gl - rows(gcb, h))
        new_states.append(states[h] * jnp.exp(gl) + _bdot(k_dec, rows(v_new, h), TN))
    return _rows_to_heads(o, nh), new_states


def _gdn_fwd(qkv, p0, alog_v, dtb_v, pad, *, name, gather=None):
    lp = qkv.shape[0]
    n = lp // CHUNK
    nh = GDN_HEADS
    assert n % GDN_STEP == 0
    steps, rows = n // GDN_STEP, GDN_STEP * CHUNK
    g_srcs, g_dtypes = gather if gather is not None else ([], [])
    ng_arr = len(g_srcs)

    def body(q_ref, k_ref, v_ref, ba_ref, al_ref, dt_ref, *rest):
        g_ins, (o_ref, st_ref) = rest[:ng_arr], rest[ng_arr:ng_arr + 2]
        g_outs, s_ref, g_scratch = rest[ng_arr + 2:2 * ng_arr + 2], rest[2 * ng_arr + 2], rest[2 * ng_arr + 3:]
        i = pl.program_id(0)
        if ng_arr:
            g_start, g_forward, g_finish = _gather_phases(g_ins, g_outs, g_scratch[:ng_arr], *g_scratch[ng_arr:],
                                                          g_dtypes)
            pl.when(i == 0)(g_start)
            pl.when(i == (3 * steps) // 4)(g_forward)

        @pl.when(i == 0)
        def _():
            s_ref[...] = jnp.zeros_like(s_ref)

        s = s_ref[...]
        s = [s[h] for h in range(nh)]
        q, k, v, ba, al, dt = q_ref[...], k_ref[...], v_ref[...], ba_ref[...], al_ref[...], dt_ref[...]
        outs = []
        for c in range(GDN_STEP):
            sl = slice(c * CHUNK, (c + 1) * CHUNK)
            valid = (i * rows + c * CHUNK + lax.broadcasted_iota(jnp.int32, (CHUNK, 1), 0)) >= pad
            for h in range(nh):
                st_ref[c, h] = s[h]
            o, s = _gdn_chunk(q[sl], k[sl], v[sl], ba[sl], al, dt, s, valid)
            outs.append(o)
        o_ref[...] = jnp.concatenate(outs, axis=0)
        for h in range(nh):
            s_ref[h] = s[h]
        if ng_arr:
            pl.when(i == steps - 1)(g_finish)

    w = nh * HEAD_W
    vec = pl.BlockSpec((1, HEAD_W), lambda i: (0, 0))
    return pl.pallas_call(
        body, name=name, grid=(steps,),
        in_specs=[pl.BlockSpec((rows, w), lambda i: (i, 0)), pl.BlockSpec((rows, w), lambda i: (i, 1)),
                  pl.BlockSpec((rows, w), lambda i: (i, 2)), pl.BlockSpec((rows, HEAD_W), lambda i: (i, AB_BA // HEAD_W)),
                  vec, vec] + [pl.BlockSpec(memory_space=pltpu.VMEM)] * ng_arr,
        out_specs=[pl.BlockSpec((rows, w), lambda i: (i, 0)),
                   pl.BlockSpec((GDN_STEP, nh, HEAD_W, HEAD_W), lambda i: (i, 0, 0, 0))] + [_ANY] * ng_arr,
        out_shape=[jax.ShapeDtypeStruct((lp, w), F32), jax.ShapeDtypeStruct((n, nh, HEAD_W, HEAD_W), F32)]
        + _gather_out_shapes(g_srcs, g_dtypes),
        scratch_shapes=[pltpu.VMEM((nh, HEAD_W, HEAD_W), F32)] + (_gather_scratch(g_srcs, g_dtypes) if ng_arr else []),
        compiler_params=_cp(("arbitrary",), has_side_effects=bool(ng_arr)),
    )(qkv, qkv, qkv, p0, alog_v, dtb_v, *g_srcs)


def _gdn_bwd(qkv, p0, alog_v, dtb_v, states, do, pad, *, name, scatter=()):
    lp = qkv.shape[0]
    n = lp // CHUNK
    nh = GDN_HEADS
    assert n % GDN_STEP == 0
    steps, rows = n // GDN_STEP, GDN_STEP * CHUNK
    ns = len(scatter)

    def body(q_ref, k_ref, v_ref, ba_ref, al_ref, dt_ref, st_ref, do_ref, *rest):
        s_ins, (dq_ref, dk_ref, dv_ref, dba_ref, dal_ref, ddt_ref) = rest[:ns], rest[ns:ns + 6]
        s_outs, ds_ref, s_sems = rest[ns + 6:2 * ns + 6], rest[2 * ns + 6], rest[2 * ns + 7:]
        step = pl.program_id(0)
        i = steps - 1 - step
        if ns:
            s_start, s_finish = _scatter_phases(s_ins, s_outs, *s_sems)
            pl.when(step == 0)(s_start)

        @pl.when(step == 0)
        def _():
            ds_ref[...] = jnp.zeros_like(ds_ref)
            dal_ref[...] = jnp.zeros_like(dal_ref)
            ddt_ref[...] = jnp.zeros_like(ddt_ref)

        q, k, v, ba, al, dt = q_ref[...], k_ref[...], v_ref[...], ba_ref[...], al_ref[...], dt_ref[...]
        st, do, dst = st_ref[...], do_ref[...], ds_ref[...]
        vjps = []
        for c in range(GDN_STEP):
            sl = slice(c * CHUNK, (c + 1) * CHUNK)
            valid = (i * rows + c * CHUNK + lax.broadcasted_iota(jnp.int32, (CHUNK, 1), 0)) >= pad
            fn = functools.partial(_gdn_chunk, valid=valid)
            vjps.append(jax.vjp(fn, q[sl], k[sl], v[sl], ba[sl], al, dt, [st[c, h] for h in range(nh)])[1])
        ds = [dst[h] for h in range(nh)]
        grads = [None] * GDN_STEP
        for c in reversed(range(GDN_STEP)):
            grads[c] = vjps[c]((do[c * CHUNK:(c + 1) * CHUNK], ds))
            ds = grads[c][6]
        for j, ref in enumerate((dq_ref, dk_ref, dv_ref, dba_ref)):
            ref[...] = jnp.concatenate([gr[j] for gr in grads], axis=0)
        dal_ref[...] += sum(gr[4] for gr in grads)
        ddt_ref[...] += sum(gr[5] for gr in grads)
        for h in range(nh):
            ds_ref[h] = ds[h]
        if ns:
            pl.when(step == steps - 1)(s_finish)

    w = nh * HEAD_W
    rev = lambda c: (lambda s: (steps - 1 - s, c))
    vec = pl.BlockSpec((1, HEAD_W), lambda s: (0, 0))
    return pl.pallas_call(
        body, name=name, grid=(steps,),
        in_specs=[pl.BlockSpec((rows, w), rev(0)), pl.BlockSpec((rows, w), rev(1)), pl.BlockSpec((rows, w), rev(2)),
                  pl.BlockSpec((rows, HEAD_W), rev(AB_BA // HEAD_W)), vec, vec,
                  pl.BlockSpec((GDN_STEP, nh, HEAD_W, HEAD_W), lambda s: (steps - 1 - s, 0, 0, 0)),
                  pl.BlockSpec((rows, w), rev(0))] + [_ANY] * ns,
        out_specs=[pl.BlockSpec((rows, w), rev(0)), pl.BlockSpec((rows, w), rev(0)), pl.BlockSpec((rows, w), rev(0)),
                   pl.BlockSpec((rows, HEAD_W), rev(0)), vec, vec] + [_ANY] * ns,
        out_shape=[jax.ShapeDtypeStruct((lp, w), F32)] * 3 + [jax.ShapeDtypeStruct((lp, HEAD_W), F32)]
        + [jax.ShapeDtypeStruct((1, HEAD_W), F32)] * 2 + [jax.ShapeDtypeStruct(s.shape, s.dtype) for s in scatter],
        scratch_shapes=[pltpu.VMEM((nh, HEAD_W, HEAD_W), F32)] + (_scatter_scratch(ns) if ns else []),
        compiler_params=_cp(("arbitrary",), has_side_effects=bool(ns)),
    )(qkv, qkv, qkv, p0, alog_v, dtb_v, states, do, *scatter)


HG_LEVELS = (32, 16, 8, 4, 2, 1)
HG_GROUP = 4
HG_STEP = 3


def _hg_masks():
    import numpy as np
    c = CHUNK
    t = np.arange(c)[:, None]
    j = np.arange(c)[None, :]
    sums = (j <= t).astype(np.float32)
    pairs = [j == t]
    for m in HG_LEVELS:
        p = (t // (2 * m)) * (2 * m)
        r = p + m
        pairs.append((t >= r) & (j < r) & (j >= p))
    pairs = np.concatenate([np.kron(np.eye(HG_GROUP), p) for p in pairs], axis=0).astype(np.float32)
    return jnp.asarray(sums, BF16), jnp.asarray(sums.T, BF16), jnp.asarray(pairs, F32)


def _hg_level_row(b, m):
    c, w = b.shape
    if m >= 8:
        return jnp.concatenate([jnp.broadcast_to(b[p + m:p + m + 1], (2 * m, w)) for p in range(0, c, 2 * m)], axis=0)
    tiles = b.reshape(c // 8, 8, w)
    sub = lax.broadcasted_iota(jnp.int32, (1, 8, 1), 1)
    out = None
    for r0 in range(m, 8, 2 * m):
        cand = jnp.broadcast_to(tiles[:, r0:r0 + 1, :], tiles.shape)
        out = cand if out is None else jnp.where(sub >= r0 - m, cand, out)
    return out.reshape(c, w)


def _split3(x):
    hi = x.astype(BF16)
    r1 = x - hi.astype(F32)
    mid = r1.astype(BF16)
    return hi, mid, (r1 - mid.astype(F32)).astype(BF16)


def _dot3_raw(a, b, dims):
    ah, am, _ = _split3(a)
    bh, bm, _ = _split3(b)
    return _dot(ah, bh, dims) + (_dot(ah, bm, dims) + _dot(am, bh, dims))


@functools.partial(jax.custom_vjp, nondiff_argnums=(2,))
def _dot3(a, b, dims):
    return _dot3_raw(a, b, dims)


def _dot3_fwd(a, b, dims):
    return _dot3_raw(a, b, dims), (a, b)


def _dot3_bwd(dims, res, g):
    a, b = res
    if dims == NN:
        return _dot3_raw(g, b, NT), _dot3_raw(a, g, TN)
    return _dot3_raw(g, b, NN), _dot3_raw(g, a, TN)


_dot3.defvjp(_dot3_fwd, _dot3_bwd)


def _mask_mm_raw(m, x):
    return sum(_dot(m, part, NN) for part in _split3(x))


@jax.custom_vjp
def _mask_mm(m, mt, x):
    return _mask_mm_raw(m, x)


def _mask_mm_fwd(m, mt, x):
    return _mask_mm_raw(m, x), (m, mt)


def _mask_mm_bwd(res, g):
    m, mt = res
    return jnp.zeros_like(m), jnp.zeros_like(mt), _mask_mm_raw(mt, g)


_mask_mm.defvjp(_mask_mm_fwd, _mask_mm_bwd)


def _hg_chunk(qr, fr, ir, lb, states, valid, sums, sums_t, pairs):
    nh = HG_GROUP
    c = qr.shape[0]
    r = nh * c
    fg = lb + (1.0 - lb) * _sigmoid(fr)
    logf = jnp.where(valid, jnp.log(fg), 0.0)
    k = jnp.where(valid, 1.0 - fg, 0.0)
    qs = jnp.where(valid, _silu(qr), 0.0)
    v = jnp.where(valid, ir, 0.0)
    b = _mask_mm(sums, sums_t, logf)
    mask = lambda n: pairs[n * r:(n + 1) * r]
    stack = lambda x: _heads_to_rows(x, nh)
    a = mask(0) * _bdot(stack(qs), stack(k), NT)
    for lvl, m in enumerate(HG_LEVELS):
        d = b - _hg_level_row(b, m)
        a = a + mask(1 + lvl) * _bdot(stack(qs * jnp.exp(jnp.minimum(d, 0.0))),
                                      stack(k * jnp.exp(jnp.minimum(-d, 0.0))), NT)
    av = _bdot(a, stack(v), NN)
    eb = jnp.exp(b)
    qe, kd = qs * eb, k * jnp.exp(b[c - 1:c] - b)
    outs, new_states = [], []
    for h in range(nh):
        cs = slice(h * HEAD_W, (h + 1) * HEAD_W)
        outs.append(_bdot(qe[:, cs], states[h], NT) + av[h * c:(h + 1) * c])
        new_states.append(states[h] * eb[c - 1:c, cs] + _bdot(v[:, cs], kd[:, cs], TN))
    return jnp.concatenate(outs, axis=1), new_states


def _hg_fwd(p1, lb, pad, *, name, gather=None):
    lp = p1.shape[0]
    n = lp // CHUNK
    nh = HG_HEADS
    g_srcs, g_dtypes = gather if gather is not None else ([], [])
    ng_arr = len(g_srcs)

    def body(q_ref, f_ref, i_ref, lb_ref, sums_ref, sums_t_ref, pairs_ref, *rest):
        g_ins, (o_ref, st_ref) = rest[:ng_arr], rest[ng_arr:ng_arr + 2]
        g_outs, s_ref, g_scratch = rest[ng_arr + 2:2 * ng_arr + 2], rest[2 * ng_arr + 2], rest[2 * ng_arr + 3:]
        i = pl.program_id(1)
        if ng_arr:
            g_start, g_forward, g_finish = _gather_phases(g_ins, g_outs, g_scratch[:ng_arr], *g_scratch[ng_arr:],
                                                          g_dtypes)
            last_group = pl.program_id(0) == ngrp - 1
            pl.when((pl.program_id(0) == 0) & (i == 0))(g_start)
            pl.when(last_group & (i == 0))(g_forward)

        @pl.when(i == 0)
        def _():
            s_ref[...] = jnp.zeros_like(s_ref)

        s = s_ref[...]
        s = [s[h] for h in range(grp)]
        q, f, iv, lbv = q_ref[...], f_ref[...], i_ref[...], lb_ref[...]
        masks_v = (sums_ref[...], sums_t_ref[...], pairs_ref[...])
        outs = []
        for c in range(HG_STEP):
            sl = slice(c * CHUNK, (c + 1) * CHUNK)
            valid = (i * rows + c * CHUNK + lax.broadcasted_iota(jnp.int32, (CHUNK, 1), 0)) >= pad
            for h in range(grp):
                st_ref[h, c] = s[h]
            o, s = _hg_chunk(q[sl], f[sl], iv[sl], lbv, s, valid, *masks_v)
            outs.append(o)
        o_ref[...] = jnp.concatenate(outs, axis=0)
        for h in range(grp):
            s_ref[h] = s[h]
        if ng_arr:
            pl.when(last_group & (i == steps - 1))(g_finish)

    masks = _hg_masks()
    grp, ngrp, gw = HG_GROUP, nh // HG_GROUP, HG_GROUP * HEAD_W
    assert n % HG_STEP == 0
    steps, rows = n // HG_STEP, HG_STEP * CHUNK
    blk = lambda off: pl.BlockSpec((rows, gw), lambda h, i: (i, off + h))
    const = lambda a: pl.BlockSpec(a.shape, lambda h, i: (0, 0))
    return pl.pallas_call(
        body, name=name, grid=(ngrp, steps),
        in_specs=[blk(0), blk(ngrp), blk(2 * ngrp), pl.BlockSpec((1, gw), lambda h, i: (0, h))]
        + [const(a) for a in masks] + [pl.BlockSpec(memory_space=pltpu.VMEM)] * ng_arr,
        out_specs=[blk(0), pl.BlockSpec((grp, HG_STEP, HEAD_W, HEAD_W), lambda h, i: (h, i, 0, 0))] + [_ANY] * ng_arr,
        out_shape=[jax.ShapeDtypeStruct((lp, nh * HEAD_W), F32), jax.ShapeDtypeStruct((nh, n, HEAD_W, HEAD_W), F32)]
        + _gather_out_shapes(g_srcs, g_dtypes),
        scratch_shapes=[pltpu.VMEM((grp, HEAD_W, HEAD_W), F32)] + (_gather_scratch(g_srcs, g_dtypes) if ng_arr else []),
        compiler_params=_cp(("arbitrary", "arbitrary"), has_side_effects=bool(ng_arr)),
    )(p1, p1, p1, lb, *masks, *g_srcs)


def _hg_bwd(p1, lb, states, do, pad, *, name, scatter=()):
    lp = p1.shape[0]
    n = lp // CHUNK
    nh = HG_HEADS
    ns = len(scatter)

    def body(q_ref, f_ref, i_ref, lb_ref, st_ref, do_ref, sums_ref, sums_t_ref, pairs_ref, *rest):
        s_ins, (dq_ref, df_ref, di_ref, dlb_ref) = rest[:ns], rest[ns:ns + 4]
        s_outs, ds_ref, s_sems = rest[ns + 4:2 * ns + 4], rest[2 * ns + 4], rest[2 * ns + 5:]
        step = pl.program_id(1)
        i = steps - 1 - step
        if ns:
            s_start, s_finish = _scatter_phases(s_ins, s_outs, *s_sems)
            pl.when((pl.program_id(0) == 0) & (step == 0))(s_start)

        @pl.when(step == 0)
        def _():
            ds_ref[...] = jnp.zeros_like(ds_ref)
            dlb_ref[...] = jnp.zeros_like(dlb_ref)

        q, f, iv, lbv, st, do, dst = q_ref[...], f_ref[...], i_ref[...], lb_ref[...], st_ref[...], do_ref[...], ds_ref[...]
        masks_v = dict(sums=sums_ref[...], sums_t=sums_t_ref[...], pairs=pairs_ref[...])
        vjps = []
        for c in range(HG_STEP):
            sl = slice(c * CHUNK, (c + 1) * CHUNK)
            valid = (i * rows + c * CHUNK + lax.broadcasted_iota(jnp.int32, (CHUNK, 1), 0)) >= pad
            fn = functools.partial(_hg_chunk, valid=valid, **masks_v)
            vjps.append(jax.vjp(fn, q[sl], f[sl], iv[sl], lbv, [st[h, c] for h in range(grp)])[1])
        ds = [dst[h] for h in range(grp)]
        grads = [None] * HG_STEP
        for c in reversed(range(HG_STEP)):
            grads[c] = vjps[c]((do[c * CHUNK:(c + 1) * CHUNK], ds))
            ds = grads[c][4]
        for j, ref in enumerate((dq_ref, df_ref, di_ref)):
            ref[...] = jnp.concatenate([gr[j] for gr in grads], axis=0)
        dlb_ref[...] += sum(gr[3] for gr in grads)
        for h in range(grp):
            ds_ref[h] = ds[h]
        if ns:
            pl.when((pl.program_id(0) == ngrp - 1) & (step == steps - 1))(s_finish)

    masks = _hg_masks()
    grp, ngrp, gw = HG_GROUP, nh // HG_GROUP, HG_GROUP * HEAD_W
    assert n % HG_STEP == 0
    steps, rows = n // HG_STEP, HG_STEP * CHUNK
    blk = lambda off: pl.BlockSpec((rows, gw), lambda h, s: (steps - 1 - s, off + h))
    const = lambda a: pl.BlockSpec(a.shape, lambda h, s: (0, 0))
    w = nh * HEAD_W
    return pl.pallas_call(
        body, name=name, grid=(ngrp, steps),
        in_specs=[blk(0), blk(ngrp), blk(2 * ngrp), pl.BlockSpec((1, gw), lambda h, s: (0, h)),
                  pl.BlockSpec((grp, HG_STEP, HEAD_W, HEAD_W), lambda h, s: (h, steps - 1 - s, 0, 0)), blk(0)]
        + [const(a) for a in masks] + [_ANY] * ns,
        out_specs=[blk(0), blk(0), blk(0), pl.BlockSpec((1, gw), lambda h, s: (0, h))] + [_ANY] * ns,
        out_shape=[jax.ShapeDtypeStruct((lp, w), F32)] * 3 + [jax.ShapeDtypeStruct((1, w), F32)]
        + [jax.ShapeDtypeStruct(s.shape, s.dtype) for s in scatter],
        scratch_shapes=[pltpu.VMEM((grp, HEAD_W, HEAD_W), F32)] + (_scatter_scratch(ns) if ns else []),
        compiler_params=_cp(("arbitrary", "arbitrary"), has_side_effects=bool(ns)),
    )(p1, p1, p1, lb, states, do, *masks, *scatter)


SB_GROUP = 4
SB_FAR = -110.0


def _sb_cat(kind, first_key=0):
    r = lax.broadcasted_iota(jnp.int32, (SB_BLOCK, 2 * SB_BLOCK), 0)
    c = lax.broadcasted_iota(jnp.int32, (SB_BLOCK, 2 * SB_BLOCK), 1)
    tri = {"after": c < r, "incl": r <= c, "before": r < c}[kind]
    m = ((c >= SB_BLOCK) | tri) & (r >= first_key)
    return jnp.where(m, 1.0, 0.0).astype(BF16)


def _sb_cumsum(x, cat):
    return _dot(x.astype(BF16), cat, NN)


def _sb_logsig(z):
    e = jnp.exp(-jnp.abs(z))
    lse = jnp.where(e < 1e-4, e, jnp.log(1.0 + e))
    lsz = jnp.minimum(z, 0.0) - lse
    return lsz, lsz - z, e


def _sb_stack(x, scale=None):
    lane = lax.broadcasted_iota(jnp.int32, (1, HEAD_W), 1)
    if scale is not None:
        x = x * scale
    return jnp.concatenate([jnp.where(lane < SB_DH, x, 0.0), jnp.where(lane >= SB_DH, x, 0.0)], axis=0).astype(BF16)


def _sb_unstack(x):
    lane = lax.broadcasted_iota(jnp.int32, (1, HEAD_W), 1)
    return jnp.where(lane < SB_DH, x[:SB_BLOCK], x[SB_BLOCK:])


def _sb_fwd(p0, pad, *, name, gather=None):
    lp = p0.shape[0]
    nb = lp // SB_BLOCK
    npair = SB_HEADS // 2
    blk0 = AB_SB // HEAD_W
    scale = SB_DH ** -0.5
    gw = SB_GROUP * SB_BLOCK
    assert pad < SB_BLOCK
    g_srcs, g_dtypes = gather if gather is not None else ([], [])
    ng_arr = len(g_srcs)

    def body(q_ref, k_ref, v_ref, *rest):
        g_ins, (o_ref, tot_ref, nproc_ref) = rest[:ng_arr], rest[ng_arr:ng_arr + 3]
        g_outs, g_scratch = rest[ng_arr + 3:2 * ng_arr + 3], rest[2 * ng_arr + 3:]
        first_step = (pl.program_id(0) == 0) & (pl.program_id(1) == 0)
        last_pair = pl.program_id(0) == npair - 1
        if ng_arr:
            g_start, g_forward, g_finish = _gather_phases(g_ins, g_outs, g_scratch[:ng_arr], *g_scratch[ng_arr:],
                                                          g_dtypes)
            pl.when(first_step)(g_start)
            pl.when(last_pair & (pl.program_id(1) == 0))(g_forward)
        i = pl.program_id(1)
        qs = _sb_stack(q_ref[...], scale)
        qpos = i * SB_BLOCK + lax.broadcasted_iota(jnp.int32, (SB_BLOCK, 1), 0)
        qpos = jnp.concatenate([qpos, qpos], axis=0)
        cat = _sb_cat("after")
        cat0 = _sb_cat("after", pad)
        ng = i // SB_GROUP

        def group(off, nblk, first_cat, allowed, carry):
            acc, run = carry
            kg = k_ref[pl.ds(off, nblk * SB_BLOCK), :].astype(BF16)
            vg = v_ref[pl.ds(off, nblk * SB_BLOCK), :].astype(BF16)
            lsz, l1m, _ = _sb_logsig(_dot(qs, kg, NT))
            if allowed is not None:
                l1m = jnp.where(allowed, l1m, 0.0)
            args = [None] * nblk
            for g in reversed(range(nblk)):
                sl = slice(g * SB_BLOCK, (g + 1) * SB_BLOCK)
                al = _sb_cumsum(l1m[:, sl], first_cat if g == 0 else cat)
                args[g] = lsz[:, sl] + al[:, :SB_BLOCK] + run
                run = run + al[:, SB_BLOCK:]
            wgt = jnp.exp(jnp.concatenate(args, axis=1))
            if allowed is not None:
                wgt = jnp.where(allowed, wgt, 0.0)
            return acc + _dot(wgt.astype(BF16), vg, NN), run

        def below(t, carry):
            gi = ng - 1 - t
            return group(pl.multiple_of(gi * gw, gw), SB_GROUP, jnp.where(gi == 0, cat0, cat), None, carry)

        top = ng * gw

        def top_group(nblk, carry):
            off = pl.multiple_of(jnp.minimum(top, lp - nblk * SB_BLOCK), SB_BLOCK)
            kpos = off + lax.broadcasted_iota(jnp.int32, (1, nblk * SB_BLOCK), 1)
            return group(off, nblk, cat, (kpos < qpos) & (kpos >= pad) & (kpos >= top), carry)

        zero = (jnp.zeros((2 * SB_BLOCK, HEAD_W), F32), jnp.zeros((2 * SB_BLOCK, HEAD_W), F32))
        carry = lax.cond(i - ng * SB_GROUP < SB_GROUP // 2, functools.partial(top_group, SB_GROUP // 2),
                         functools.partial(top_group, SB_GROUP), zero)
        used, acc, run = lax.while_loop(lambda s: (s[0] < ng) & (jnp.max(s[2]) > SB_FAR),
                                        lambda s: (s[0] + 1, *below(s[0], (s[1], s[2]))), (jnp.int32(0), *carry))
        o_ref[...] = _sb_unstack(acc)
        tot_ref[...] = _sb_unstack(run)
        nproc_ref[pl.program_id(0), i] = used.astype(F32)
        if ng_arr:
            pl.when(last_pair & (pl.program_id(1) == nb - 1))(g_finish)

    full = lambda c0: pl.BlockSpec((lp, HEAD_W), lambda p, i: (0, c0 + p))
    out = pl.BlockSpec((SB_BLOCK, HEAD_W), lambda p, i: (i, p))
    return pl.pallas_call(
        body, name=name, grid=(npair, nb),
        in_specs=[pl.BlockSpec((SB_BLOCK, HEAD_W), lambda p, i: (i, blk0 + p)), full(blk0 + npair), full(blk0 + 2 * npair)]
        + [pl.BlockSpec(memory_space=pltpu.VMEM)] * ng_arr,
        out_specs=[out, out, pl.BlockSpec(memory_space=pltpu.SMEM)] + [_ANY] * ng_arr,
        out_shape=[jax.ShapeDtypeStruct((lp, npair * HEAD_W), F32)] * 2 + [jax.ShapeDtypeStruct((npair, nb), F32)]
        + _gather_out_shapes(g_srcs, g_dtypes),
        scratch_shapes=_gather_scratch(g_srcs, g_dtypes) if ng_arr else [],
        compiler_params=_cp(("arbitrary", "arbitrary"), has_side_effects=bool(ng_arr)),
    )(p0, p0, p0, *g_srcs)


def _sb_bwd(p0, tot, nproc, dsrc, d_blk0, pad, *, name, scatter=()):
    lp = p0.shape[0]
    nb = lp // SB_BLOCK
    npair = SB_HEADS // 2
    blk0 = AB_SB // HEAD_W
    scale = SB_DH ** -0.5
    gw = SB_GROUP * SB_BLOCK
    assert pad < SB_BLOCK
    ns = len(scatter)

    def body(q_ref, k_ref, v_ref, tot_ref, nproc_ref, do_ref, *rest):
        s_ins, (dq_ref, dkt_ref, dvt_ref) = rest[:ns], rest[ns:ns + 3]
        s_outs, s_sems = rest[ns + 3:2 * ns + 3], rest[2 * ns + 3:]
        if ns:
            s_start, s_finish = _scatter_phases(s_ins, s_outs, *s_sems)
            pl.when((pl.program_id(0) == 0) & (pl.program_id(1) == 0))(s_start)
        i = pl.program_id(1)

        @pl.when(i == 0)
        def _():
            dkt_ref[...] = jnp.zeros_like(dkt_ref)
            dvt_ref[...] = jnp.zeros_like(dvt_ref)

        qs = _sb_stack(q_ref[...], scale)
        dos = _sb_stack(do_ref[...])
        qst, dost = qs.T, dos.T
        totv = tot_ref[...]
        ones = jnp.ones((1, HEAD_W), F32)
        tots = jnp.concatenate([totv[:, 0:1] * ones, totv[:, SB_DH:SB_DH + 1] * ones], axis=0)
        qpos = i * SB_BLOCK + lax.broadcasted_iota(jnp.int32, (SB_BLOCK, 1), 0)
        qpos = jnp.concatenate([qpos, qpos], axis=0)
        incl, incl0 = _sb_cat("incl"), _sb_cat("incl", pad)
        before = _sb_cat("before")
        ng = i // SB_GROUP
        used = jnp.clip(nproc_ref[pl.program_id(0), i].astype(jnp.int32), 0, ng)

        def dscore(z, e, ev, dl1m):
            r = 1.0 / (1.0 + e)
            sg = jnp.where(z >= 0, r, e * r)
            return ev * (1.0 - sg) - dl1m * sg

        def group(off, nblk, first_incl, allowed, carry):
            dq, prun, erun = carry
            width = nblk * SB_BLOCK
            kg = k_ref[pl.ds(off, width), :].astype(BF16)
            vg = v_ref[pl.ds(off, width), :].astype(BF16)
            z = _dot(qs, kg, NT)
            lsz, l1m, e = _sb_logsig(z)
            if allowed is not None:
                l1m = jnp.where(allowed, l1m, 0.0)
            dwgt = _dot(dos, vg, NT)
            dzs = [None] * nblk
            wgts = [None] * nblk
            for g in range(nblk):
                sl = slice(g * SB_BLOCK, (g + 1) * SB_BLOCK)
                al = _sb_cumsum(l1m[:, sl], first_incl if g == 0 else incl)
                wgt = jnp.exp(jnp.minimum(lsz[:, sl] + (tots - prun - al[:, :SB_BLOCK]), 0.0))
                if allowed is not None:
                    wgt = jnp.where(allowed[:, sl], wgt, 0.0)
                prun = prun + al[:, SB_BLOCK:]
                ev = wgt * dwgt[:, sl]
                el = _sb_cumsum(ev, before)
                dzs[g] = dscore(z[:, sl], e[:, sl], ev, erun + el[:, :SB_BLOCK])
                erun = erun + el[:, SB_BLOCK:]
                wgts[g] = wgt
            dz = jnp.concatenate(dzs, axis=1)
            if allowed is not None:
                dz = jnp.where(allowed, dz, 0.0)
            dz = dz.astype(BF16)
            wg = jnp.concatenate(wgts, axis=1).astype(BF16)
            dkt_ref[:, pl.ds(off, width)] += _dot(qst, dz, NN)
            dvt_ref[:, pl.ds(off, width)] += _dot(dost, wg, NN)
            return dq + _dot(dz, kg, NN), prun, erun

        def below(gi, carry):
            return group(pl.multiple_of(gi * gw, gw), SB_GROUP, jnp.where(gi == 0, incl0, incl), None, carry)

        zero = tuple(jnp.zeros((2 * SB_BLOCK, HEAD_W), F32) for _ in range(3))
        carry = lax.fori_loop(ng - used, ng, below, zero)
        top = ng * gw

        def top_group(nblk, carry):
            off = pl.multiple_of(jnp.minimum(top, lp - nblk * SB_BLOCK), SB_BLOCK)
            kpos = off + lax.broadcasted_iota(jnp.int32, (1, nblk * SB_BLOCK), 1)
            return group(off, nblk, incl, (kpos < qpos) & (kpos >= pad) & (kpos >= top), carry)

        dq, _, _ = lax.cond(i - ng * SB_GROUP < SB_GROUP // 2, functools.partial(top_group, SB_GROUP // 2),
                            functools.partial(top_group, SB_GROUP), carry)
        dq_ref[...] = _sb_unstack(dq) * scale
        if ns:
            pl.when((pl.program_id(0) == npair - 1) & (pl.program_id(1) == nb - 1))(s_finish)

    full = lambda c0: pl.BlockSpec((lp, HEAD_W), lambda p, i: (0, c0 + p))
    qb = lambda c0: pl.BlockSpec((SB_BLOCK, HEAD_W), lambda p, i: (i, c0 + p))
    tr = pl.BlockSpec((HEAD_W, lp), lambda p, i: (p, 0))
    return pl.pallas_call(
        body, name=name, grid=(npair, nb),
        in_specs=[qb(blk0), full(blk0 + npair), full(blk0 + 2 * npair), qb(0), pl.BlockSpec(memory_space=pltpu.SMEM),
                  qb(d_blk0)] + [_ANY] * ns,
        out_specs=[qb(0), tr, tr] + [_ANY] * ns,
        out_shape=[jax.ShapeDtypeStruct((lp, npair * HEAD_W), F32)]
        + [jax.ShapeDtypeStruct((npair * HEAD_W, lp), F32)] * 2
        + [jax.ShapeDtypeStruct(s.shape, s.dtype) for s in scatter],
        scratch_shapes=_scatter_scratch(ns) if ns else [],
        compiler_params=_cp(("arbitrary", "arbitrary"), has_side_effects=bool(ns)),
    )(p0, p0, p0, tot, nproc, dsrc, *scatter)


def _local_step(h0, target, pad, wts, hooks=None):
    lp = h0.shape[0]
    tm = _row_tile(lp, 1056)
    tkl = tm
    tml = _row_tile(lp, 528)
    d = D_MODEL
    mm = _mm
    mmw = functools.partial(_mm, out_dtype=BF16)
    g = {}

    h0_b = h0.astype(BF16)
    p0 = mm(h0_b, wts["w_ab"], "NN", tm=tm, tn=768, tk=d, name="l0_in_proj")
    ob, sb_tot, sb_used, *gathered = _sb_fwd(p0, pad, name="sb_fwd", gather=hooks["gather_a"] if hooks else None)
    if hooks:
        wts = {**wts, **hooks["weights_a"](gathered)}
    qkv = _gdn_pre_fwd(p0, wts["conv_w"], pad, name="gdn_pre_fwd")
    oa_raw, gdn_states, *gathered = _gdn_fwd(qkv, p0, wts["alog_v"], wts["dtb_v"], pad, name="gdn_fwd",
                                             gather=hooks["gather_b"] if hooks else None)
    if hooks:
        wts = {**wts, **hooks["weights_b"](gathered)}
    rows = lambda a, n: a.reshape(N_DEV, n // N_DEV, d)
    parts = g["parts"] = {}
    oab = _gate_fwd(oa_raw, p0, AB_Z // HEAD_W, wts["ab_gn"], ob, heads=GDN_HEADS, name="gdn_gate_fwd")
    ln = lambda kind, layer: (wts[f"ln_{kind}_g"][layer], wts[f"ln_{kind}_b"][layer])
    pre_mix0, h0a, h0a_b = mm(oab, wts["w_out0"], "NN", tm=tml, tn=d, tk=d, epi="ln", c=h0, scale=DN_ALPHA,
                              ln=ln("mix", 0), name="l0_out_proj")
    u0, act0 = mm(h0a_b, wts["w1"][0], "NN", tm=tm, tn=512, tk=d, b_dev=True, epi="relu2_copy", name="mlp0_up")
    pre_ffn0, h0b, h0b_b = mm(act0, wts["w2"][0], "NN", tm=tml, tn=d, tk=d, epi="ln", c=h0a, scale=DN_ALPHA,
                              ln=ln("ffn", 0), name="mlp0_down")
    p1 = mm(h0b_b, wts["w_c"], "NN", tm=tm, tn=512, tk=d, b_dev=True, name="l1_in_proj")
    oc_raw, hg_states, *gathered = _hg_fwd(p1, wts["lb"], pad, name="hg_fwd",
                                           gather=hooks["gather_c"] if hooks else None)
    if hooks:
        third = hooks["weights_c"](gathered)
        wts = {**wts, "w1": wts["w1"] + third["w1"], "w2": wts["w2"] + third["w2"]}
    oc = _gate_fwd(oc_raw, p1, 3 * HG_HEADS, wts["c_gn"], oc_raw, heads=HG_HEADS, name="hg_gate_fwd")
    pre_mix1, h1a, h1a_b = mm(oc, wts["w_out1"], "NN", tm=tml, tn=d, tk=d, epi="ln", c=h0b, scale=DN_ALPHA,
                              ln=ln("mix", 1), name="l1_out_proj")
    u1, act1 = mm(h1a_b, wts["w1"][1], "NN", tm=tm, tn=512, tk=d, b_dev=True, epi="relu2_copy", name="mlp1_up")
    pre_ffn1, h1b, _ = mm(act1, wts["w2"][1], "NN", tm=tml, tn=d, tk=d, epi="ln", c=h1a, scale=DN_ALPHA,
                          ln=ln("ffn", 1), name="mlp1_down")
    dy, loss_vec = _loss_head(h1b, target, name="loss_head")

    def mlp_bwd(layer, h_in_b, u, act, dpre, dpre_b, pre_mix):
        du = mm(dpre_b, wts["w2"][layer], "NT", tm=tm, tn=1024, tk=d, epi="relu2grad", c=u, out_dtype=BF16,
                name=f"mlp{layer}_d_hidden")
        dw2 = mmw(act, dpre_b, "TN", tm=1024, tn=1024, tk=tkl, name=f"mlp{layer}_dw2")
        dw1 = mmw(h_in_b, du, "TN", tm=1024, tn=512, tk=tkl, out_dev=True, name=f"mlp{layer}_dw1")
        return (*mm(du, k_major(wts["w1"][layer]), "NT", tm=tml, tn=1024, tk=2048, epi="ln_bwd", c=dpre, scale=DN_ALPHA,
                    ln=(pre_mix, wts["ln_mix_g"][layer]), name=f"mlp{layer}_d_in"), dw1, dw2)

    k_major = lambda wd: wd.transpose(1, 0, 2).reshape(wd.shape[1], -1)

    ln_ffn_dg, ln_ffn_db, ln_mix_dg, ln_mix_db, dw1s, dw2s = ([None, None] for _ in range(6))
    dpre, dpre_b, ln_ffn_dg[1], ln_ffn_db[1] = _ln_bwd(pre_ffn1, wts["ln_ffn_g"][1], dy, name="ln_ffn1_bwd")
    dpre, dpre_b, ln_mix_dg[1], ln_mix_db[1], dw1s[1], dw2s[1] = mlp_bwd(1, h1a_b, u1, act1, dpre, dpre_b, pre_mix1)
    g["c_w_out"] = mmw(oc, dpre_b, "TN", tm=1024, tn=1024, tk=tkl, name="l1_dw_out")
    doc = mm(dpre_b, wts["w_out1"], "NT", tm=tm, tn=1024, tk=d, name="l1_d_gate")
    doc_raw, dz1, g["c_gn"] = _gate_bwd(oc_raw, p1, 3 * HG_HEADS, wts["c_gn"], doc, heads=HG_HEADS, name="hg_gate_bwd")
    ready = [dw1s[1], rows(dw2s[1], D_FF), rows(g["c_w_out"], d)] if hooks else ()
    dq1, df1, di1, g["lb"], *got = _hg_bwd(p1, wts["lb"], hg_states, doc_raw, pad, name="hg_bwd", scatter=ready)
    parts.update(zip(("mlp_w1_1", "mlp_w2_1", "c_w_out"), got))
    dp1 = jnp.concatenate([dq1, df1, di1, dz1], axis=1).astype(BF16)
    g["c_w_in"] = mmw(h0b_b, dp1, "TN", tm=1024, tn=512, tk=tkl, out_dev=True, name="l1_dw_in")
    dpre, dpre_b, ln_ffn_dg[0], ln_ffn_db[0] = mm(
        dp1, k_major(wts["w_c"]), "NT", tm=tml, tn=1024, tk=2048, epi="ln_bwd", c=dpre, scale=DN_ALPHA,
        ln=(pre_ffn0, wts["ln_ffn_g"][0]), name="l1_d_in")
    dpre, dpre_b, ln_mix_dg[0], ln_mix_db[0], dw1s[0], dw2s[0] = mlp_bwd(0, h0a_b, u0, act0, dpre, dpre_b, pre_mix0)
    g["ab_w_out"] = mmw(oab, dpre_b, "TN", tm=1024, tn=1024, tk=tkl, name="l0_dw_out")
    doab = mm(dpre_b, wts["w_out0"], "NT", tm=tm, tn=1024, tk=d, name="l0_d_gate")
    doa_raw, dz0, g["ab_gn"] = _gate_bwd(oa_raw, p0, AB_Z // HEAD_W, wts["ab_gn"], doab, heads=GDN_HEADS,
                                         name="gdn_gate_bwd")
    ready = [g["c_w_in"]] if hooks else ()
    dqb, dkb_t, dvb_t, *got = _sb_bwd(p0, sb_tot, sb_used, doab, GDN_HEADS, pad, name="sb_bwd", scatter=ready)
    parts.update(zip(("c_w_in",), got))
    dkb, dvb = dkb_t.T, dvb_t.T
    ready = [dw1s[0], rows(dw2s[0], D_FF), rows(g["ab_w_out"], d)] if hooks else ()
    dqn, dkn, dvn, dba, g["alog_v"], g["dtb_v"], *got = _gdn_bwd(qkv, p0, wts["alog_v"], wts["dtb_v"], gdn_states,
                                                                 doa_raw, pad, name="gdn_bwd", scatter=ready)
    parts.update(zip(("mlp_w1_0", "mlp_w2_0", "ab_w_out"), got))
    dconv_in, g["conv_w"] = _gdn_pre_bwd(p0, wts["conv_w"], jnp.concatenate([dqn, dkn, dvn], axis=1), pad,
                                         name="gdn_pre_bwd")
    dp0 = jnp.concatenate([dconv_in, dz0, dqb, dkb, dvb, dba, jnp.zeros((lp, AB_CAT - AB_BA - HEAD_W), F32)],
                          axis=1).astype(BF16)
    g["w_ab"] = mmw(h0_b, dp0, "TN", tm=1024, tn=768, tk=tkl, name="l0_dw_in")
    last = ()
    if hooks:
        gab, ba0 = g["w_ab"], AB_Z + GDN_HEADS * HEAD_W
        gab = jnp.concatenate([gab[:, :ba0], gab[:, AB_BA:AB_BA + 2 * GDN_HEADS], gab[:, ba0:AB_BA]], axis=1)
        last = [gab.reshape(d, N_DEV, AB_IN // N_DEV).transpose(1, 0, 2)]
    res = mm(dp0, wts["w_ab"], "NT", tm=tm, tn=1024, tk=1920, epi="add", c=dpre, scale=DN_ALPHA, scatter=last,
             name="l0_d_in")
    dh0 = res[0] if last else res
    parts.update(zip(("ab_w_in",), res[1:] if last else ()))

    g["w1"], g["w2"] = dw1s, dw2s
    g["ln_mix_g"] = jnp.concatenate(ln_mix_dg, axis=0)
    g["ln_mix_b"] = jnp.concatenate(ln_mix_db, axis=0)
    g["ln_ffn_g"] = jnp.concatenate(ln_ffn_dg, axis=0)
    g["ln_ffn_b"] = jnp.concatenate(ln_ffn_db, axis=0)
    return loss_vec, dh0, g


N_CHIP = N_DEV // 2


def _place():
    x, y, c = lax.axis_index("x"), lax.axis_index("y"), lax.axis_index("c")
    return x, y, c, 2 * x + y


def _chip_dev(chip, core):
    return (chip // 2, chip % 2, core)


def _remote(src, dst, send_sem, recv_sem, dev):
    return pltpu.make_async_remote_copy(src_ref=src, dst_ref=dst, send_sem=send_sem, recv_sem=recv_sem,
                                        device_id=dev, device_id_type=pl.DeviceIdType.MESH)


_ANY = pl.BlockSpec(memory_space=pl.ANY)


def _gather(srcs, dtypes, *, name):
    n = len(srcs)

    def body(*refs):
        start, forward, finish = _gather_phases(refs[:n], refs[n:2 * n], refs[2 * n:3 * n], *refs[3 * n:], dtypes)
        start()
        forward()
        finish()

    return pl.pallas_call(
        body, name=name, in_specs=[pl.BlockSpec(memory_space=pltpu.VMEM)] * n, out_specs=[_ANY] * n,
        out_shape=_gather_out_shapes(srcs, dtypes), scratch_shapes=_gather_scratch(srcs, dtypes),
        compiler_params=_cp(has_side_effects=True),
    )(*srcs)


def _gather_out_shapes(srcs, dtypes):
    return [jax.ShapeDtypeStruct((N_DEV, *s.shape), dt) for s, dt in zip(srcs, dtypes)]


def _gather_scratch(srcs, dtypes):
    n = len(srcs)
    return [pltpu.VMEM(s.shape, dt) for s, dt in zip(srcs, dtypes)] + [
        pltpu.SemaphoreType.DMA((n, 2 * N_CHIP - 1)), pltpu.SemaphoreType.DMA((n, 2 * N_CHIP - 1)),
        pltpu.SemaphoreType.DMA((n,))]


def _gather_phases(ins, outs, stages, send_sems, recv_sems, local_sems, dtypes):
    n = len(ins)
    x, y, c, chip = _place()
    me = 2 * chip + c
    sibling = (x, y, 1 - c)

    def own(i):
        cps = [_remote(stages[i], outs[i].at[me], send_sems.at[i, 0], recv_sems.at[i, 0], sibling)]
        for j in range(1, N_CHIP):
            cps.append(_remote(stages[i], outs[i].at[me], send_sems.at[i, j], recv_sems.at[i, j],
                               _chip_dev(jnp.bitwise_xor(chip, j), c)))
        return cps

    def local(i):
        return pltpu.make_async_copy(stages[i], outs[i].at[me], local_sems.at[i])

    def passed_on(i, j):
        slot = outs[i].at[2 * jnp.bitwise_xor(chip, j) + c]
        return _remote(slot, slot, send_sems.at[i, N_CHIP - 1 + j], recv_sems.at[i, N_CHIP - 1 + j], sibling)

    def start():
        for i in range(n):
            stages[i][...] = ins[i][...].astype(dtypes[i])
            local(i).start()
            for cp in own(i):
                cp.start()

    def forward():
        for i in range(n):
            for j in range(1, N_CHIP):
                own(i)[j].wait_recv()
                passed_on(i, j).start()

    def finish():
        for i in range(n):
            own(i)[0].wait_recv()
            for j in range(1, N_CHIP):
                passed_on(i, j).wait_recv()
        for i in range(n):
            for cp in own(i):
                cp.wait_send()
            for j in range(1, N_CHIP):
                passed_on(i, j).wait_send()
            local(i).wait()

    return start, forward, finish


def _scatter_scratch(n):
    return [pltpu.SemaphoreType.DMA((n, N_DEV - 1)), pltpu.SemaphoreType.DMA((n, N_DEV - 1)),
            pltpu.SemaphoreType.DMA((n,))]


def _scatter_phases(ins, outs, send_sems, recv_sems, local_sems):
    n = len(ins)
    _, _, c, chip = _place()
    me = 2 * chip + c

    def copies():
        cps = []
        for i in range(n):
            cps.append(pltpu.make_async_copy(ins[i].at[me], outs[i].at[me], local_sems.at[i]))
            for k in range(1, N_DEV):
                peer = jnp.bitwise_xor(me, k)
                cps.append(_remote(ins[i].at[peer], outs[i].at[me], send_sems.at[i, k - 1], recv_sems.at[i, k - 1],
                                   _chip_dev(peer // 2, peer % 2)))
        return cps

    def start():
        for cp in copies():
            cp.start()

    def finish():
        for cp in copies():
            cp.wait()

    return start, finish


def _adamw(w, parts, m, v, *, name):
    r, c = w.shape
    s = parts.shape[0]
    tm = _row_tile(r, 128) if r % 8 == 0 else r
    c1 = 1.0 - ADAM_B1 ** ADAM_STEP
    c2 = 1.0 - ADAM_B2 ** ADAM_STEP

    def body(w_ref, p_ref, m_ref, v_ref, g_ref, d_ref, m2_ref, v2_ref):
        g = p_ref[0].astype(F32)
        for j in range(1, s):
            g = g + p_ref[j].astype(F32)
        m2 = ADAM_B1 * m_ref[...] + (1.0 - ADAM_B1) * g
        v2 = ADAM_B2 * v_ref[...] + (1.0 - ADAM_B2) * jnp.square(g)
        g_ref[...] = g
        m2_ref[...] = m2
        v2_ref[...] = v2
        d_ref[...] = -ADAM_LR * ((m2 / c1) / (jnp.sqrt(v2 / c2) + ADAM_EPS) + ADAM_WD * w_ref[...])

    blk = pl.BlockSpec((tm, c), lambda i: (i, 0))
    return pl.pallas_call(
        body, name=name, grid=(r // tm,),
        in_specs=[blk, pl.BlockSpec((s, tm, c), lambda i: (0, i, 0)), blk, blk], out_specs=[blk] * 4,
        out_shape=[jax.ShapeDtypeStruct((r, c), F32)] * 4, compiler_params=_cp(("parallel",)),
    )(w, parts, m, v)


_WEIGHTS = ("meta_tokens", "ab_w_in", "ab_conv_w", "ab_a_log", "ab_dt_bias", "ab_gnorm_g", "ab_w_out", "c_w_in",
            "c_lb_raw", "c_gnorm_g", "c_w_out", "ln_mix_g", "ln_mix_b", "mlp_w1", "mlp_w2", "ln_ffn_g", "ln_ffn_b")
_PACK_ROWS = (("ln_mix_g", 0), ("ln_mix_b", 2), ("ln_ffn_g", 4), ("ln_ffn_b", 6), ("c_lb_raw", 8))
_PACK_MISC_ROW = 10
_PACK_MISC = (("ab_gnorm_g", 0, 128), ("c_gnorm_g", 128, 128), ("ab_a_log", 256, GDN_HEADS), ("ab_dt_bias", 260, GDN_HEADS))
_PACK_N = 16
_SMALL_META = 16
_SMALL_CONV = 32
_SMALL_N = 40


def _pack_replicated(p):
    rows = jnp.zeros((_PACK_N, D_MODEL), F32)
    for name, r0 in _PACK_ROWS:
        rows = rows.at[r0:r0 + 2].set(p[name])
    for name, c0, width in _PACK_MISC:
        rows = rows.at[_PACK_MISC_ROW, c0:c0 + width].set(p[name].reshape(width))
    return rows


def _unpack_replicated(rows, like):
    out = {}
    for name, r0 in _PACK_ROWS:
        out[name] = rows[r0:r0 + 2]
    for name, c0, width in _PACK_MISC:
        out[name] = rows[_PACK_MISC_ROW, c0:c0 + width].reshape(like[name].shape)
    return out


def _lower_bound(c_lb_raw):
    lb_all = jnp.cumsum(jax.nn.softmax(c_lb_raw.astype(F32), axis=0), axis=0)
    return (lb_all - lb_all[0:1])[1].reshape(1, -1)


def kernel(x, meta_tokens, ab_w_in, ab_conv_w, ab_a_log, ab_dt_bias, ab_gnorm_g, ab_w_out, c_w_in, c_lb_raw, c_gnorm_g, c_w_out, ln_mix_g, ln_mix_b, mlp_w1, mlp_w2, ln_ffn_g, ln_ffn_b, loss_target, m_meta_tokens, m_ab_w_in, m_ab_conv_w, m_ab_a_log, m_ab_dt_bias, m_ab_gnorm_g, m_ab_w_out, m_c_w_in, m_c_lb_raw, m_c_gnorm_g, m_c_w_out, m_ln_mix_g, m_ln_mix_b, m_mlp_w1, m_mlp_w2, m_ln_ffn_g, m_ln_ffn_b, v_meta_tokens, v_ab_w_in, v_ab_conv_w, v_ab_a_log, v_ab_dt_bias, v_ab_gnorm_g, v_ab_w_out, v_c_w_in, v_c_lb_raw, v_c_gnorm_g, v_c_w_out, v_ln_mix_g, v_ln_mix_b, v_mlp_w1, v_mlp_w2, v_ln_ffn_g, v_ln_ffn_b):
    w = dict(zip(_WEIGHTS, (meta_tokens, ab_w_in, ab_conv_w, ab_a_log, ab_dt_bias, ab_gnorm_g, ab_w_out, c_w_in, c_lb_raw,
                            c_gnorm_g, c_w_out, ln_mix_g, ln_mix_b, mlp_w1, mlp_w2, ln_ffn_g, ln_ffn_b)))
    mom = dict(zip(_WEIGHTS, (m_meta_tokens, m_ab_w_in, m_ab_conv_w, m_ab_a_log, m_ab_dt_bias, m_ab_gnorm_g, m_ab_w_out,
                              m_c_w_in, m_c_lb_raw, m_c_gnorm_g, m_c_w_out, m_ln_mix_g, m_ln_mix_b, m_mlp_w1, m_mlp_w2,
                              m_ln_ffn_g, m_ln_ffn_b)))
    var = dict(zip(_WEIGHTS, (v_meta_tokens, v_ab_w_in, v_ab_conv_w, v_ab_a_log, v_ab_dt_bias, v_ab_gnorm_g, v_ab_w_out,
                              v_c_w_in, v_c_lb_raw, v_c_gnorm_g, v_c_w_out, v_ln_mix_g, v_ln_mix_b, v_mlp_w1, v_mlp_w2,
                              v_ln_ffn_g, v_ln_ffn_b)))
    me = 4 * lax.axis_index("x") + 2 * lax.axis_index("y") + lax.axis_index("c")
    seq = x.shape[1]
    pad = (-(N_META + seq)) % SB_BLOCK
    lp = pad + N_META + seq
    meta_w = D_MODEL // N_DEV
    conv_w_all = 2 * GDN_HEADS * HEAD_W + GDN_HEADS * HEAD_W
    conv_w_mine = conv_w_all // N_DEV

    g_meta, g_conv, g_ab_in = _gather([w["meta_tokens"], w["ab_conv_w"][0], w["ab_w_in"][0]], [F32, F32, BF16],
                                      name="gather_weights_first")
    meta_full = g_meta.transpose(1, 0, 2).reshape(N_META, D_MODEL)
    conv_full = g_conv.transpose(1, 0, 2).reshape(CONV_K, conv_w_all)
    ab_full = g_ab_in.transpose(1, 0, 2).reshape(D_MODEL, AB_IN)
    ba0 = AB_Z + 512
    w_ab = jnp.concatenate([ab_full[:, :ba0], ab_full[:, ba0 + 2 * GDN_HEADS:], ab_full[:, ba0:ba0 + 2 * GDN_HEADS],
                            jnp.zeros((D_MODEL, AB_CAT - AB_IN), BF16)], axis=1)
    vec128 = lambda p: jnp.zeros((1, HEAD_W), F32).at[0, :GDN_HEADS].set(p.reshape(GDN_HEADS))
    wts = dict(
        w_ab=w_ab, conv_w=conv_full, alog_v=vec128(w["ab_a_log"]), dtb_v=vec128(w["ab_dt_bias"]),
        ab_gn=w["ab_gnorm_g"][0], lb=_lower_bound(w["c_lb_raw"]), c_gn=w["c_gnorm_g"][0],
        ln_mix_g=w["ln_mix_g"], ln_mix_b=w["ln_mix_b"], ln_ffn_g=w["ln_ffn_g"], ln_ffn_b=w["ln_ffn_b"])

    def weights_a(gathered):
        g_ab_out, g_w1, g_w2 = gathered
        return dict(w_out0=g_ab_out.reshape(D_MODEL, D_MODEL), w1=[g_w1], w2=[g_w2.reshape(D_FF, D_MODEL)])

    def weights_b(gathered):
        g_c_in, g_c_out = gathered
        return dict(w_c=g_c_in, w_out1=g_c_out.reshape(D_MODEL, D_MODEL))

    def weights_c(gathered):
        g_w1, g_w2 = gathered
        return dict(w1=[g_w1], w2=[g_w2.reshape(D_FF, D_MODEL)])

    hooks = dict(
        gather_a=([w["ab_w_out"][0], w["mlp_w1"][0], w["mlp_w2"][0]], [BF16] * 3), weights_a=weights_a,
        gather_b=([w["c_w_in"][0], w["c_w_out"][0]], [BF16] * 2), weights_b=weights_b,
        gather_c=([w["mlp_w1"][1], w["mlp_w2"][1]], [BF16] * 2), weights_c=weights_c)

    h0 = jnp.concatenate([jnp.zeros((pad, D_MODEL), F32), meta_full, x[0]], axis=0)
    loss_vec, dh0, g = _local_step(h0, loss_target[0], pad, wts, hooks)
    loss = lax.psum(jnp.sum(loss_vec), ("x", "y", "c"))
    grad_x = dh0[lp - seq:][None]

    _, lb_vjp = jax.vjp(_lower_bound, w["c_lb_raw"])
    rep_part = _pack_replicated(dict(
        ln_mix_g=g["ln_mix_g"], ln_mix_b=g["ln_mix_b"], ln_ffn_g=g["ln_ffn_g"], ln_ffn_b=g["ln_ffn_b"],
        c_lb_raw=lb_vjp(g["lb"])[0], ab_gnorm_g=g["ab_gn"], c_gnorm_g=g["c_gn"],
        ab_a_log=g["alog_v"][0, :GDN_HEADS], ab_dt_bias=g["dtb_v"][0, :GDN_HEADS]))
    small = jnp.concatenate([rep_part, dh0[pad:pad + N_META], g["conv_w"].reshape(-1, D_MODEL),
                             jnp.zeros((_SMALL_N - _SMALL_CONV - CONV_K * conv_w_all // D_MODEL, D_MODEL), F32)], axis=0)
    (small_all,) = _gather([small], [F32], name="gather_small_grads")
    rep_out = _adamw(_pack_replicated(w), small_all[:, :_PACK_N], _pack_replicated(mom), _pack_replicated(var),
                     name="adamw_replicated")
    meta_parts = lax.dynamic_slice_in_dim(small_all[:, _SMALL_META:_SMALL_META + N_META], me * meta_w, meta_w, axis=2)
    meta_out = _adamw(w["meta_tokens"], meta_parts, mom["meta_tokens"], var["meta_tokens"], name="adamw_meta")
    conv_parts = small_all[:, _SMALL_CONV:_SMALL_CONV + CONV_K * conv_w_all // D_MODEL].reshape(N_DEV, CONV_K, conv_w_all)
    conv_parts = lax.dynamic_slice_in_dim(conv_parts, me * conv_w_mine, conv_w_mine, axis=2)
    conv_out = _adamw(w["ab_conv_w"][0], conv_parts, mom["ab_conv_w"][0], var["ab_conv_w"][0], name="adamw_conv")

    parts = g["parts"]
    big = [("ab_w_in", 0, parts["ab_w_in"]), ("ab_w_out", 0, parts["ab_w_out"]), ("mlp_w1", 0, parts["mlp_w1_0"]),
           ("mlp_w2", 0, parts["mlp_w2_0"]), ("c_w_in", 0, parts["c_w_in"]), ("c_w_out", 0, parts["c_w_out"]),
           ("mlp_w1", 1, parts["mlp_w1_1"]), ("mlp_w2", 1, parts["mlp_w2_1"])]
    big_out = {}
    for name, l, p in big:
        res = _adamw(w[name][l], p, mom[name][l], var[name][l], name=f"adamw_{name}{l}")
        big_out.setdefault(name, []).append(res)

    rep = [_unpack_replicated(r, w) for r in rep_out]
    outs = {}
    for name in _WEIGHTS:
        if name == "meta_tokens":
            outs[name] = list(meta_out)
        elif name == "ab_conv_w":
            outs[name] = [o[None] for o in conv_out]
        elif name in big_out:
            res = big_out[name]
            outs[name] = [o[None] for o in res[0]] if len(res) == 1 else [jnp.stack(pair) for pair in zip(*res)]
        else:
            outs[name] = [r[name] for r in rep]
    flat = [loss, grad_x]
    for kind in range(4):
        flat += [outs[name][kind] for name in _WEIGHTS]
    return tuple(flat)
```

```python
import functools

import jax
import jax.numpy as jnp
from jax import lax
from jax.experimental import pallas as pl
from jax.experimental.pallas import tpu as pltpu

F32 = jnp.float32
BF16 = jnp.bfloat16

N_DEV = 8
D_MODEL = 1024
N_META = 16
D_FF = 4096
DEPTH = 2
GDN_HEADS = 4
SB_HEADS = 8
SB_DH = 64
HG_HEADS = 8
HEAD_W = 128
CHUNK = 64
SB_BLOCK = 128
CONV_K = 4
DN_ALPHA = float((2 * DEPTH) ** 0.25)
LN_EPS = 1e-5
RMS_EPS = 1e-6
L2_EPS = 1e-6
ADAM_LR, ADAM_B1, ADAM_B2, ADAM_EPS, ADAM_WD, ADAM_STEP = 0.001, 0.9, 0.999, 1e-08, 0.01, 10

AB_Z = 1536
AB_SB = 2048
AB_BA = 3584
AB_CAT = 3840
AB_IN = 3592

VMEM_LIMIT = 56 * 1024 * 1024


def _cp(sem=None, **kw):
    if sem is not None:
        kw["dimension_semantics"] = sem
    return pltpu.CompilerParams(vmem_limit_bytes=VMEM_LIMIT, **kw)


def _row_tile(n, want):
    best = 8
    for t in range(8, min(n, want) + 1, 8):
        if n % t == 0:
            best = t
    return best


@jax.custom_vjp
def _sigmoid(x):
    e = jnp.exp(-jnp.abs(x))
    r = 1.0 / (1.0 + e)
    return jnp.where(x >= 0, r, e * r)


def _sigmoid_fwd(x):
    s = _sigmoid(x)
    return s, s


def _sigmoid_bwd(s, g):
    return (g * s * (1.0 - s),)


_sigmoid.defvjp(_sigmoid_fwd, _sigmoid_bwd)


def _log1p_exp_neg_abs(x):
    e = jnp.exp(-jnp.abs(x))
    return jnp.where(e < 1e-4, e - 0.5 * e * e, jnp.log(1.0 + e))


@jax.custom_vjp
def _softplus(x):
    return jnp.maximum(x, 0.0) + _log1p_exp_neg_abs(x)


def _softplus_fwd(x):
    return _softplus(x), x


def _softplus_bwd(x, g):
    return (g * _sigmoid(x),)


_softplus.defvjp(_softplus_fwd, _softplus_bwd)


def _silu(x):
    return x * _sigmoid(x)


def _silu_grad(x):
    s = _sigmoid(x)
    return s * (1.0 + x * (1.0 - s))


def _dot(a, b, dims, precision=None):
    return lax.dot_general(a, b, (dims, ((), ())), precision=precision, preferred_element_type=F32)


NN = ((1,), (0,))
NT = ((1,), (1,))
TN = ((0,), (0,))


def _bdot(a, b, dims):
    return _dot(a.astype(BF16), b.astype(BF16), dims)


def _layer_norm(pre, g, beta):
    mu = jnp.mean(pre, axis=-1, keepdims=True)
    xc = pre - mu
    var = jnp.mean(xc * xc, axis=-1, keepdims=True)
    return xc * lax.rsqrt(var + LN_EPS) * g + beta


def _layer_norm_bwd(pre, g, dy):
    mu = jnp.mean(pre, axis=-1, keepdims=True)
    xc = pre - mu
    rstd = lax.rsqrt(jnp.mean(xc * xc, axis=-1, keepdims=True) + LN_EPS)
    xhat = xc * rstd
    dxh = dy * g
    m1 = jnp.mean(dxh, axis=-1, keepdims=True)
    m2 = jnp.mean(dxh * xhat, axis=-1, keepdims=True)
    return (rstd * (dxh - m1 - xhat * m2), jnp.sum(dy * xhat, axis=0, keepdims=True),
            jnp.sum(dy, axis=0, keepdims=True))


def _mm(a, b, mode, *, tm, tn, tk, name, epi=None, c=None, scale=1.0, b_dev=False, out_dev=False, out_dtype=F32,
        ln=None, scatter=()):
    if mode == "NN":
        m, kk = a.shape
        n = b.shape[2] * N_DEV if b_dev else b.shape[1]
    elif mode == "NT":
        m, kk = a.shape
        n = b.shape[1] if b_dev else b.shape[0]
    else:
        kk, m = a.shape
        n = b.shape[1]
    assert m % tm == 0 and n % tn == 0 and kk % tk == 0, (name, m, n, kk, tm, tn, tk)
    nk = kk // tk
    dims = {"NN": NN, "NT": NT, "TN": TN}[mode]

    if mode == "TN":
        a_spec = pl.BlockSpec((tk, tm), lambda i, j, k: (k, i))
    else:
        a_spec = pl.BlockSpec((tm, tk), lambda i, j, k: (i, k))
    if mode == "NN":
        if b_dev:
            assert tn == b.shape[2]
            b_spec = pl.BlockSpec((None, tk, tn), lambda i, j, k: (j, k, 0))
        else:
            b_spec = pl.BlockSpec((tk, tn), lambda i, j, k: (k, j))
    elif mode == "NT":
        if b_dev:
            assert tk == b.shape[2]
            b_spec = pl.BlockSpec((None, tn, tk), lambda i, j, k: (k, j, 0))
        else:
            b_spec = pl.BlockSpec((tn, tk), lambda i, j, k: (j, k))
    else:
        b_spec = pl.BlockSpec((tk, tn), lambda i, j, k: (k, j))
    in_specs = [a_spec, b_spec]
    operands = [a, b]
    if c is not None:
        in_specs.append(pl.BlockSpec((tm, tn), lambda i, j, k: (i, j)))
        operands.append(c)
    if epi == "ln":
        assert tn == n and not out_dev
        in_specs += [pl.BlockSpec((1, n), lambda i, j, k: (0, 0))] * 2
        operands += [ln[0].reshape(1, n), ln[1].reshape(1, n)]
    elif epi == "ln_bwd":
        assert tn == n and not out_dev
        in_specs += [pl.BlockSpec((tm, tn), lambda i, j, k: (i, j)), pl.BlockSpec((1, n), lambda i, j, k: (0, 0))]
        operands += [ln[0], ln[1].reshape(1, n)]
    if out_dev:
        assert tn == n // N_DEV
        out_shape = jax.ShapeDtypeStruct((N_DEV, m, tn), out_dtype)
        out_spec = pl.BlockSpec((None, tm, tn), lambda i, j, k: (j, i, 0))
    else:
        out_shape = jax.ShapeDtypeStruct((m, n), out_dtype)
        out_spec = pl.BlockSpec((tm, tn), lambda i, j, k: (i, j))
    if epi == "ln":
        out_shape = [out_shape, out_shape, jax.ShapeDtypeStruct((m, n), BF16)]
        out_spec = [out_spec] * 3
    elif epi == "relu2_copy":
        assert not out_dev
        out_shape = [out_shape, jax.ShapeDtypeStruct((m, n), BF16)]
        out_spec = [out_spec] * 2
    elif epi == "ln_bwd":
        vec_shape, vec_spec = jax.ShapeDtypeStruct((1, n), F32), pl.BlockSpec((1, n), lambda i, j, k: (0, 0))
        out_shape = [out_shape, jax.ShapeDtypeStruct((m, n), BF16), vec_shape, vec_shape]
        out_spec = [out_spec, out_spec, vec_spec, vec_spec]
    n_out = {"ln": 3, "relu2_copy": 2, "ln_bwd": 4}.get(epi, 1)
    ns = len(scatter)
    if ns:
        in_specs += [_ANY] * ns
        operands += list(scatter)
        out_shape = (out_shape if n_out > 1 else [out_shape]) + [jax.ShapeDtypeStruct(s.shape, s.dtype) for s in scatter]
        out_spec = (out_spec if n_out > 1 else [out_spec]) + [_ANY] * ns
    n_in = len(operands)
    grid = (m // tm, n // tn, nk)

    def body(*refs):
        a_ref, b_ref = refs[0], refs[1]
        c_ref = refs[2] if c is not None else None
        o_ref = refs[n_in]
        scratch0 = n_in + n_out + ns
        acc_ref = refs[scratch0] if nk > 1 else None
        if ns:
            s_start, s_finish = _scatter_phases(refs[n_in - ns:n_in], refs[n_in + n_out:scratch0],
                                                *refs[scratch0 + (1 if nk > 1 else 0):])
            at = lambda step: functools.reduce(lambda x, y: x & y, [pl.program_id(ax) == step[ax] for ax in range(3)])
            pl.when(at((0, 0, 0)))(s_start)
        p = _dot(a_ref[...].astype(BF16), b_ref[...].astype(BF16), dims)
        first_rows = pl.program_id(0) == 0

        def finish(acc):
            if epi == "add":
                acc = acc + scale * c_ref[...]
            elif epi == "relu2grad":
                acc = acc * (2.0 * jnp.maximum(c_ref[...], 0.0))
            elif epi == "relu2_copy":
                refs[n_in + 1][...] = jnp.square(jnp.maximum(acc, 0.0)).astype(BF16)
            elif epi == "ln_bwd":
                acc, dg, db = _layer_norm_bwd(refs[3][...], refs[4][...], acc + scale * c_ref[...])
                dg_ref, db_ref = refs[n_in + 2], refs[n_in + 3]

                @pl.when(first_rows)
                def _():
                    dg_ref[...] = jnp.zeros_like(dg_ref)
                    db_ref[...] = jnp.zeros_like(db_ref)

                dg_ref[...] += dg
                db_ref[...] += db
                refs[n_in + 1][...] = acc.astype(BF16)
            elif epi == "ln":
                acc = acc + scale * c_ref[...]
                y = _layer_norm(acc, refs[3][...], refs[4][...])
                refs[n_in + 1][...] = y
                refs[n_in + 2][...] = y.astype(BF16)
            o_ref[...] = acc.astype(out_dtype)

        if nk == 1:
            finish(p)
        else:
            k = pl.program_id(2)

            @pl.when(k == 0)
            def _():
                acc_ref[...] = p

            @pl.when(k > 0)
            def _():
                acc_ref[...] += p

            @pl.when(k == nk - 1)
            def _():
                finish(acc_ref[...])

        if ns:
            pl.when(at(tuple(g - 1 for g in grid)))(s_finish)

    res = pl.pallas_call(
        body, name=name, grid=grid, in_specs=in_specs, out_specs=out_spec, out_shape=out_shape,
        scratch_shapes=([pltpu.VMEM((tm, tn), F32)] if nk > 1 else []) + (_scatter_scratch(ns) if ns else []),
        compiler_params=_cp(("arbitrary",) * 3 if ns or epi == "ln_bwd" else ("parallel", "parallel", "arbitrary"),
                            has_side_effects=bool(ns)),
    )(*operands)
    return res


def _ln_bwd(pre, g, dy, *, name):
    lp, d = pre.shape
    tm = _row_tile(lp, 512)

    def body(pre_ref, g_ref, dy_ref, dpre_ref, dpreb_ref, dg_ref, db_ref):
        dpre, dg, db = _layer_norm_bwd(pre_ref[...], g_ref[...], dy_ref[...])
        dpre_ref[...] = dpre
        dpreb_ref[...] = dpre.astype(BF16)

        @pl.when(pl.program_id(0) == 0)
        def _():
            dg_ref[...] = jnp.zeros_like(dg_ref)
            db_ref[...] = jnp.zeros_like(db_ref)

        dg_ref[...] += dg
        db_ref[...] += db

    row = pl.BlockSpec((tm, d), lambda i: (i, 0))
    vec = pl.BlockSpec((1, d), lambda i: (0, 0))
    return pl.pallas_call(
        body, name=name, grid=(lp // tm,), in_specs=[row, vec, row], out_specs=[row, row, vec, vec],
        out_shape=[jax.ShapeDtypeStruct((lp, d), F32), jax.ShapeDtypeStruct((lp, d), BF16),
                   jax.ShapeDtypeStruct((1, d), F32), jax.ShapeDtypeStruct((1, d), F32)],
        compiler_params=_cp(("arbitrary",)),
    )(pre, g.reshape(1, d), dy)


def _loss_head(y, target, *, name):
    lp, d = y.shape
    seq = target.shape[0]
    tm = SB_BLOCK
    first = (lp - seq) // tm
    assert (lp - seq) % tm == 0 and seq % tm == 0

    def body(y_ref, t_ref, dy_ref, loss_ref):
        i = pl.program_id(0)
        live = i >= first
        diff = jnp.where(live, y_ref[...] - t_ref[...], 0.0)
        dy_ref[...] = diff * (1.0 / d)

        @pl.when(i == 0)
        def _():
            loss_ref[...] = jnp.zeros_like(loss_ref)

        loss_ref[...] += jnp.sum(diff * diff, axis=0, keepdims=True) * (0.5 / d)

    return pl.pallas_call(
        body, name=name, grid=(lp // tm,),
        in_specs=[pl.BlockSpec((tm, d), lambda i: (i, 0)),
                  pl.BlockSpec((tm, d), lambda i: (jnp.maximum(i - first, 0), 0))],
        out_specs=[pl.BlockSpec((tm, d), lambda i: (i, 0)), pl.BlockSpec((1, d), lambda i: (0, 0))],
        out_shape=[jax.ShapeDtypeStruct((lp, d), F32), jax.ShapeDtypeStruct((1, d), F32)],
        compiler_params=_cp(("arbitrary",)),
    )(y, target)


def _gate_fwd(o, zsrc, z_blk0, g, other, *, heads, name):
    lp = o.shape[0]
    tm = _row_tile(lp, 512)
    w = heads * HEAD_W
    assert (z_blk0 * HEAD_W) % w == 0
    has_other = w < D_MODEL

    def body(o_ref, z_ref, g_ref, *rest):
        y_ref = rest[-1]
        gv = g_ref[...]
        for h in range(heads):
            cs = slice(h * HEAD_W, (h + 1) * HEAD_W)
            ov = o_ref[:, cs]
            r = lax.rsqrt(jnp.mean(ov * ov, axis=-1, keepdims=True) + RMS_EPS)
            y_ref[:, cs] = (ov * r * gv * _silu(z_ref[:, cs])).astype(BF16)
        if has_other:
            y_ref[:, w:] = rest[0][...].astype(BF16)

    row = lambda width, blk: pl.BlockSpec((tm, width), lambda i: (i, blk))
    return pl.pallas_call(
        body, name=name, grid=(lp // tm,),
        in_specs=[row(w, 0), row(w, z_blk0 * HEAD_W // w), pl.BlockSpec((1, HEAD_W), lambda i: (0, 0))]
        + ([row(D_MODEL - w, 0)] if has_other else []),
        out_specs=row(D_MODEL, 0), out_shape=jax.ShapeDtypeStruct((lp, D_MODEL), BF16),
        compiler_params=_cp(("parallel",)),
    )(o, zsrc, g.reshape(1, HEAD_W), *([other] if has_other else []))


def _gate_bwd(o, zsrc, z_blk0, g, dy, *, heads, name):
    lp = o.shape[0]
    tm = _row_tile(lp, 512)

    w = heads * HEAD_W
    assert (z_blk0 * HEAD_W) % w == 0

    def body(o_ref, z_ref, g_ref, dy_ref, do_ref, dz_ref, dg_ref):
        @pl.when(pl.program_id(0) == 0)
        def _():
            dg_ref[...] = jnp.zeros_like(dg_ref)

        gv = g_ref[...]
        dg = jnp.zeros((1, HEAD_W), F32)
        for h in range(heads):
            cs = slice(h * HEAD_W, (h + 1) * HEAD_W)
            ov, zv, dyv = o_ref[:, cs], z_ref[:, cs], dy_ref[:, cs]
            r = lax.rsqrt(jnp.mean(ov * ov, axis=-1, keepdims=True) + RMS_EPS)
            nrm = ov * r
            s = _silu(zv)
            dn = dyv * gv * s
            do_ref[:, cs] = r * (dn - nrm * jnp.mean(dn * nrm, axis=-1, keepdims=True))
            dz_ref[:, cs] = dyv * nrm * gv * _silu_grad(zv)
            dg = dg + jnp.sum(dyv * nrm * s, axis=0, keepdims=True)
        dg_ref[...] += dg

    row = lambda blk: pl.BlockSpec((tm, w), lambda i: (i, blk))
    vec = pl.BlockSpec((1, HEAD_W), lambda i: (0, 0))
    return pl.pallas_call(
        body, name=name, grid=(lp // tm,),
        in_specs=[row(0), row(z_blk0 * HEAD_W // w), vec, row(0)], out_specs=[row(0), row(0), vec],
        out_shape=[jax.ShapeDtypeStruct((lp, w), F32), jax.ShapeDtypeStruct((lp, w), F32),
                   jax.ShapeDtypeStruct((1, HEAD_W), F32)],
        compiler_params=_cp(("arbitrary",)),
    )(o, zsrc, g.reshape(1, HEAD_W), dy)


def _conv_taps(x, w):
    acc = w[CONV_K - 1:CONV_K, :] * x
    for k in range(CONV_K - 1):
        acc = acc + w[k:k + 1, :] * pltpu.roll(x, CONV_K - 1 - k, 0)
    return acc


def _gdn_pre_fwd(p0, conv_w, pad, *, name):
    lp = p0.shape[0]
    nq = GDN_HEADS
    qscale = HEAD_W ** -0.5

    def body(x_ref, w_ref, y_ref):
        j = pl.program_id(0)
        c = _conv_taps(x_ref[...], w_ref[...])
        s = _silu(c)
        r = lax.rsqrt(jnp.sum(s * s, axis=-1, keepdims=True) + L2_EPS)
        mult = jnp.where(j < nq, r * qscale, jnp.where(j < 2 * nq, r, 1.0))
        rows = lax.broadcasted_iota(jnp.int32, (lp, 1), 0)
        y_ref[...] = jnp.where(rows >= pad, s * mult, 0.0)

    return pl.pallas_call(
        body, name=name, grid=(3 * nq,),
        in_specs=[pl.BlockSpec((lp, HEAD_W), lambda j: (0, j)), pl.BlockSpec((CONV_K, HEAD_W), lambda j: (0, j))],
        out_specs=pl.BlockSpec((lp, HEAD_W), lambda j: (0, j)),
        out_shape=jax.ShapeDtypeStruct((lp, 3 * nq * HEAD_W), F32), compiler_params=_cp(("parallel",)),
    )(p0, conv_w)


def _gdn_pre_bwd(p0, conv_w, dqkv, pad, *, name):
    lp = p0.shape[0]
    nq = GDN_HEADS
    qscale = HEAD_W ** -0.5

    def body(x_ref, w_ref, dy_ref, dx_ref, dw_ref):
        j = pl.program_id(0)
        x, w = x_ref[...], w_ref[...]
        c = _conv_taps(x, w)
        s = _silu(c)
        r = lax.rsqrt(jnp.sum(s * s, axis=-1, keepdims=True) + L2_EPS)
        rows = lax.broadcasted_iota(jnp.int32, (lp, 1), 0)
        dy = jnp.where(rows >= pad, dy_ref[...], 0.0)
        nrm = s * r
        dn = dy * jnp.where(j < nq, qscale, 1.0)
        ds_norm = r * (dn - nrm * jnp.sum(nrm * dn, axis=-1, keepdims=True))
        ds = jnp.where(j < 2 * nq, ds_norm, dy)
        dc = ds * _silu_grad(c)
        dx = w[CONV_K - 1:CONV_K, :] * dc
        dws = [None] * CONV_K
        dws[CONV_K - 1] = jnp.sum(dc * x, axis=0, keepdims=True)
        for k in range(CONV_K - 1):
            sh = CONV_K - 1 - k
            dx = dx + w[k:k + 1, :] * pltpu.roll(dc, lp - sh, 0)
            dws[k] = jnp.sum(dc * pltpu.roll(x, sh, 0), axis=0, keepdims=True)
        dx_ref[...] = dx
        dw_ref[...] = jnp.concatenate(dws, axis=0)

    blk = pl.BlockSpec((lp, HEAD_W), lambda j: (0, j))
    wblk = pl.BlockSpec((CONV_K, HEAD_W), lambda j: (0, j))
    return pl.pallas_call(
        body, name=name, grid=(3 * nq,), in_specs=[blk, wblk, blk], out_specs=[blk, wblk],
        out_shape=[jax.ShapeDtypeStruct((lp, 3 * nq * HEAD_W), F32),
                   jax.ShapeDtypeStruct((CONV_K, 3 * nq * HEAD_W), F32)],
        compiler_params=_cp(("parallel",)),
    )(p0, conv_w, dqkv)


@jax.custom_vjp
def _inv_unit_lower(m):
    c = m.shape[0]
    eye = (lax.broadcasted_iota(jnp.int32, (c, c), 0) == lax.broadcasted_iota(jnp.int32, (c, c), 1)).astype(F32)
    x = eye - m
    p = m
    n = 2
    while n < CHUNK:
        p = _bdot(p, p, NN)
        x = x + _bdot(x, p, NN)
        n *= 2
    return x


def _inv_fwd(m):
    t = _inv_unit_lower(m)
    return t, t


def _inv_bwd(t, g):
    return (-_bdot(_bdot(t, g, TN), t, NT),)


_inv_unit_lower.defvjp(_inv_fwd, _inv_bwd)


GDN_STEP = 3


def _heads_to_rows(x, nh):
    return jnp.concatenate([x[:, h * HEAD_W:(h + 1) * HEAD_W] for h in range(nh)], axis=0)


def _rows_to_heads(x, nh):
    c = x.shape[0] // nh
    return jnp.concatenate([x[h * c:(h + 1) * c] for h in range(nh)], axis=1)


def _gdn_chunk(q, k, v, ba, alog, dtb, states, valid):
    nh = GDN_HEADS
    c = q.shape[0]
    r = nh * c
    lane = lax.broadcasted_iota(jnp.int32, (1, HEAD_W), 1)
    pick = lambda x, l: jnp.sum(jnp.where(lane == l, x, 0.0), axis=-1, keepdims=True)
    beta = jnp.concatenate([jnp.where(valid, _sigmoid(pick(ba, h)), 0.0) for h in range(nh)], axis=0)
    g = jnp.concatenate(
        [jnp.where(valid, -jnp.exp(pick(alog, h)) * _softplus(pick(ba, nh + h) + pick(dtb, h)), 0.0) for h in range(nh)],
        axis=0)
    qs, ks, vs = _heads_to_rows(q, nh), _heads_to_rows(k, nh), _heads_to_rows(v, nh)
    rr = lax.broadcasted_iota(jnp.int32, (r, r), 0)
    cc = lax.broadcasted_iota(jnp.int32, (r, r), 1)
    same = (rr // c) == (cc // c)
    causal, strict = same & (cc <= rr), same & (cc < rr)
    lower = jnp.where(causal, 1.0, 0.0).astype(BF16)
    upper = jnp.where(same & (cc >= rr), 1.0, 0.0).astype(BF16)
    gcb = _mask_mm(lower, upper, g * jnp.ones((1, HEAD_W), F32))
    gc_col = jnp.concatenate([gcb] * (r // HEAD_W), axis=1)
    decay = jnp.where(causal, jnp.exp(jnp.minimum(gc_col - gc_col.T, 0.0)), 0.0)
    egc = jnp.exp(gcb)
    kb = ks * beta
    m = jnp.where(strict, _dot3(kb, ks, NT) * decay, 0.0)
    t = _inv_unit_lower(m)
    u = _bdot(t, vs * beta, NN)
    w = _bdot(t, kb * egc, NN)
    a = _bdot(qs, ks, NT) * decay
    rows = lambda x, h: x[h * c:(h + 1) * c]
    qe = qs * egc
    v_new = u - jnp.concatenate([_bdot(rows(w, h), states[h], NN) for h in range(nh)], axis=0)
    o = jnp.concatenate([_bdot(rows(qe, h), states[h], NN) for h in range(nh)], axis=0) + _bdot(a, v_new, NN)
    new_states = []
    for h in range(nh):
        gl = gcb[(h + 1) * c - 1:(h + 1) * c, :]
        k_dec = rows(ks, h) * jnp.exp(gl - rows(gcb, h))
        new_states.append(states[h] * jnp.exp(gl) + _bdot(k_dec, rows(v_new, h), TN))
    return _rows_to_heads(o, nh), new_states


def _gdn_fwd(qkv, p0, alog_v, dtb_v, pad, *, name, gather=None):
    lp = qkv.shape[0]
    n = lp // CHUNK
    nh = GDN_HEADS
    assert n % GDN_STEP == 0
    steps, rows = n // GDN_STEP, GDN_STEP * CHUNK
    g_srcs, g_dtypes = gather if gather is not None else ([], [])
    ng_arr = len(g_srcs)

    def body(q_ref, k_ref, v_ref, ba_ref, al_ref, dt_ref, *rest):
        g_ins, (o_ref, st_ref) = rest[:ng_arr], rest[ng_arr:ng_arr + 2]
        g_outs, s_ref, g_scratch = rest[ng_arr + 2:2 * ng_arr + 2], rest[2 * ng_arr + 2], rest[2 * ng_arr + 3:]
        i = pl.program_id(0)
        if ng_arr:
            g_start, g_forward, g_finish = _gather_phases(g_ins, g_outs, g_scratch[:ng_arr], *g_scratch[ng_arr:],
                                                          g_dtypes)
            pl.when(i == 0)(g_start)
            pl.when(i == (3 * steps) // 4)(g_forward)

        @pl.when(i == 0)
        def _():
            s_ref[...] = jnp.zeros_like(s_ref)

        s = s_ref[...]
        s = [s[h] for h in range(nh)]
        q, k, v, ba, al, dt = q_ref[...], k_ref[...], v_ref[...], ba_ref[...], al_ref[...], dt_ref[...]
        outs = []
        for c in range(GDN_STEP):
            sl = slice(c * CHUNK, (c + 1) * CHUNK)
            valid = (i * rows + c * CHUNK + lax.broadcasted_iota(jnp.int32, (CHUNK, 1), 0)) >= pad
            for h in range(nh):
                st_ref[c, h] = s[h]
            o, s = _gdn_chunk(q[sl], k[sl], v[sl], ba[sl], al, dt, s, valid)
            outs.append(o)
        o_ref[...] = jnp.concatenate(outs, axis=0)
        for h in range(nh):
            s_ref[h] = s[h]
        if ng_arr:
            pl.when(i == steps - 1)(g_finish)

    w = nh * HEAD_W
    vec = pl.BlockSpec((1, HEAD_W), lambda i: (0, 0))
    return pl.pallas_call(
        body, name=name, grid=(steps,),
        in_specs=[pl.BlockSpec((rows, w), lambda i: (i, 0)), pl.BlockSpec((rows, w), lambda i: (i, 1)),
                  pl.BlockSpec((rows, w), lambda i: (i, 2)), pl.BlockSpec((rows, HEAD_W), lambda i: (i, AB_BA // HEAD_W)),
                  vec, vec] + [pl.BlockSpec(memory_space=pltpu.VMEM)] * ng_arr,
        out_specs=[pl.BlockSpec((rows, w), lambda i: (i, 0)),
                   pl.BlockSpec((GDN_STEP, nh, HEAD_W, HEAD_W), lambda i: (i, 0, 0, 0))] + [_ANY] * ng_arr,
        out_shape=[jax.ShapeDtypeStruct((lp, w), F32), jax.ShapeDtypeStruct((n, nh, HEAD_W, HEAD_W), F32)]
        + _gather_out_shapes(g_srcs, g_dtypes),
        scratch_shapes=[pltpu.VMEM((nh, HEAD_W, HEAD_W), F32)] + (_gather_scratch(g_srcs, g_dtypes) if ng_arr else []),
        compiler_params=_cp(("arbitrary",), has_side_effects=bool(ng_arr)),
    )(qkv, qkv, qkv, p0, alog_v, dtb_v, *g_srcs)


def _gdn_bwd(qkv, p0, alog_v, dtb_v, states, do, pad, *, name, scatter=()):
    lp = qkv.shape[0]
    n = lp // CHUNK
    nh = GDN_HEADS
    assert n % GDN_STEP == 0
    steps, rows = n // GDN_STEP, GDN_STEP * CHUNK
    ns = len(scatter)

    def body(q_ref, k_ref, v_ref, ba_ref, al_ref, dt_ref, st_ref, do_ref, *rest):
        s_ins, (dq_ref, dk_ref, dv_ref, dba_ref, dal_ref, ddt_ref) = rest[:ns], rest[ns:ns + 6]
        s_outs, ds_ref, s_sems = rest[ns + 6:2 * ns + 6], rest[2 * ns + 6], rest[2 * ns + 7:]
        step = pl.program_id(0)
        i = steps - 1 - step
        if ns:
            s_start, s_finish = _scatter_phases(s_ins, s_outs, *s_sems)
            pl.when(step == 0)(s_start)

        @pl.when(step == 0)
        def _():
            ds_ref[...] = jnp.zeros_like(ds_ref)
            dal_ref[...] = jnp.zeros_like(dal_ref)
            ddt_ref[...] = jnp.zeros_like(ddt_ref)

        q, k, v, ba, al, dt = q_ref[...], k_ref[...], v_ref[...], ba_ref[...], al_ref[...], dt_ref[...]
        st, do, dst = st_ref[...], do_ref[...], ds_ref[...]
        vjps = []
        for c in range(GDN_STEP):
            sl = slice(c * CHUNK, (c + 1) * CHUNK)
            valid = (i * rows + c * CHUNK + lax.broadcasted_iota(jnp.int32, (CHUNK, 1), 0)) >= pad
            fn = functools.partial(_gdn_chunk, valid=valid)
            vjps.append(jax.vjp(fn, q[sl], k[sl], v[sl], ba[sl], al, dt, [st[c, h] for h in range(nh)])[1])
        ds = [dst[h] for h in range(nh)]
        grads = [None] * GDN_STEP
        for c in reversed(range(GDN_STEP)):
            grads[c] = vjps[c]((do[c * CHUNK:(c + 1) * CHUNK], ds))
            ds = grads[c][6]
        for j, ref in enumerate((dq_ref, dk_ref, dv_ref, dba_ref)):
            ref[...] = jnp.concatenate([gr[j] for gr in grads], axis=0)
        dal_ref[...] += sum(gr[4] for gr in grads)
        ddt_ref[...] += sum(gr[5] for gr in grads)
        for h in range(nh):
            ds_ref[h] = ds[h]
        if ns:
            pl.when(step == steps - 1)(s_finish)

    w = nh * HEAD_W
    rev = lambda c: (lambda s: (steps - 1 - s, c))
    vec = pl.BlockSpec((1, HEAD_W), lambda s: (0, 0))
    return pl.pallas_call(
        body, name=name, grid=(steps,),
        in_specs=[pl.BlockSpec((rows, w), rev(0)), pl.BlockSpec((rows, w), rev(1)), pl.BlockSpec((rows, w), rev(2)),
                  pl.BlockSpec((rows, HEAD_W), rev(AB_BA // HEAD_W)), vec, vec,
                  pl.BlockSpec((GDN_STEP, nh, HEAD_W, HEAD_W), lambda s: (steps - 1 - s, 0, 0, 0)),
                  pl.BlockSpec((rows, w), rev(0))] + [_ANY] * ns,
        out_specs=[pl.BlockSpec((rows, w), rev(0)), pl.BlockSpec((rows, w), rev(0)), pl.BlockSpec((rows, w), rev(0)),
                   pl.BlockSpec((rows, HEAD_W), rev(0)), vec, vec] + [_ANY] * ns,
        out_shape=[jax.ShapeDtypeStruct((lp, w), F32)] * 3 + [jax.ShapeDtypeStruct((lp, HEAD_W), F32)]
        + [jax.ShapeDtypeStruct((1, HEAD_W), F32)] * 2 + [jax.ShapeDtypeStruct(s.shape, s.dtype) for s in scatter],
        scratch_shapes=[pltpu.VMEM((nh, HEAD_W, HEAD_W), F32)] + (_scatter_scratch(ns) if ns else []),
        compiler_params=_cp(("arbitrary",), has_side_effects=bool(ns)),
    )(qkv, qkv, qkv, p0, alog_v, dtb_v, states, do, *scatter)


HG_LEVELS = (32, 16, 8, 4, 2, 1)
HG_GROUP = 4
HG_STEP = 3


def _hg_masks():
    import numpy as np
    c = CHUNK
    t = np.arange(c)[:, None]
    j = np.arange(c)[None, :]
    sums = (j <= t).astype(np.float32)
    pairs = [j == t]
    for m in HG_LEVELS:
        p = (t // (2 * m)) * (2 * m)
        r = p + m
        pairs.append((t >= r) & (j < r) & (j >= p))
    pairs = np.concatenate([np.kron(np.eye(HG_GROUP), p) for p in pairs], axis=0).astype(np.float32)
    return jnp.asarray(sums, BF16), jnp.asarray(sums.T, BF16), jnp.asarray(pairs, F32)


def _hg_level_row(b, m):
    c, w = b.shape
    if m >= 8:
        return jnp.concatenate([jnp.broadcast_to(b[p + m:p + m + 1], (2 * m, w)) for p in range(0, c, 2 * m)], axis=0)
    tiles = b.reshape(c // 8, 8, w)
    sub = lax.broadcasted_iota(jnp.int32, (1, 8, 1), 1)
    out = None
    for r0 in range(m, 8, 2 * m):
        cand = jnp.broadcast_to(tiles[:, r0:r0 + 1, :], tiles.shape)
        out = cand if out is None else jnp.where(sub >= r0 - m, cand, out)
    return out.reshape(c, w)


def _split3(x):
    hi = x.astype(BF16)
    r1 = x - hi.astype(F32)
    mid = r1.astype(BF16)
    return hi, mid, (r1 - mid.astype(F32)).astype(BF16)


def _dot3_raw(a, b, dims):
    ah, am, _ = _split3(a)
    bh, bm, _ = _split3(b)
    return _dot(ah, bh, dims) + (_dot(ah, bm, dims) + _dot(am, bh, dims))


@functools.partial(jax.custom_vjp, nondiff_argnums=(2,))
def _dot3(a, b, dims):
    return _dot3_raw(a, b, dims)


def _dot3_fwd(a, b, dims):
    return _dot3_raw(a, b, dims), (a, b)


def _dot3_bwd(dims, res, g):
    a, b = res
    if dims == NN:
        return _dot3_raw(g, b, NT), _dot3_raw(a, g, TN)
    return _dot3_raw(g, b, NN), _dot3_raw(g, a, TN)


_dot3.defvjp(_dot3_fwd, _dot3_bwd)


def _mask_mm_raw(m, x):
    return sum(_dot(m, part, NN) for part in _split3(x))


@jax.custom_vjp
def _mask_mm(m, mt, x):
    return _mask_mm_raw(m, x)


def _mask_mm_fwd(m, mt, x):
    return _mask_mm_raw(m, x), (m, mt)


def _mask_mm_bwd(res, g):
    m, mt = res
    return jnp.zeros_like(m), jnp.zeros_like(mt), _mask_mm_raw(mt, g)


_mask_mm.defvjp(_mask_mm_fwd, _mask_mm_bwd)


def _hg_chunk(qr, fr, ir, lb, states, valid, sums, sums_t, pairs):
    nh = HG_GROUP
    c = qr.shape[0]
    r = nh * c
    fg = lb + (1.0 - lb) * _sigmoid(fr)
    logf = jnp.where(valid, jnp.log(fg), 0.0)
    k = jnp.where(valid, 1.0 - fg, 0.0)
    qs = jnp.where(valid, _silu(qr), 0.0)
    v = jnp.where(valid, ir, 0.0)
    b = _mask_mm(sums, sums_t, logf)
    mask = lambda n: pairs[n * r:(n + 1) * r]
    stack = lambda x: _heads_to_rows(x, nh)
    a = mask(0) * _bdot(stack(qs), stack(k), NT)
    for lvl, m in enumerate(HG_LEVELS):
        d = b - _hg_level_row(b, m)
        a = a + mask(1 + lvl) * _bdot(stack(qs * jnp.exp(jnp.minimum(d, 0.0))),
                                      stack(k * jnp.exp(jnp.minimum(-d, 0.0))), NT)
    av = _bdot(a, stack(v), NN)
    eb = jnp.exp(b)
    qe, kd = qs * eb, k * jnp.exp(b[c - 1:c] - b)
    outs, new_states = [], []
    for h in range(nh):
        cs = slice(h * HEAD_W, (h + 1) * HEAD_W)
        outs.append(_bdot(qe[:, cs], states[h], NT) + av[h * c:(h + 1) * c])
        new_states.append(states[h] * eb[c - 1:c, cs] + _bdot(v[:, cs], kd[:, cs], TN))
    return jnp.concatenate(outs, axis=1), new_states


def _hg_fwd(p1, lb, pad, *, name, gather=None):
    lp = p1.shape[0]
    n = lp // CHUNK
    nh = HG_HEADS
    g_srcs, g_dtypes = gather if gather is not None else ([], [])
    ng_arr = len(g_srcs)

    def body(q_ref, f_ref, i_ref, lb_ref, sums_ref, sums_t_ref, pairs_ref, *rest):
        g_ins, (o_ref, st_ref) = rest[:ng_arr], rest[ng_arr:ng_arr + 2]
        g_outs, s_ref, g_scratch = rest[ng_arr + 2:2 * ng_arr + 2], rest[2 * ng_arr + 2], rest[2 * ng_arr + 3:]
        i = pl.program_id(1)
        if ng_arr:
            g_start, g_forward, g_finish = _gather_phases(g_ins, g_outs, g_scratch[:ng_arr], *g_scratch[ng_arr:],
                                                          g_dtypes)
            last_group = pl.program_id(0) == ngrp - 1
            pl.when((pl.program_id(0) == 0) & (i == 0))(g_start)
            pl.when(last_group & (i == 0))(g_forward)

        @pl.when(i == 0)
        def _():
            s_ref[...] = jnp.zeros_like(s_ref)

        s = s_ref[...]
        s = [s[h] for h in range(grp)]
        q, f, iv, lbv = q_ref[...], f_ref[...], i_ref[...], lb_ref[...]
        masks_v = (sums_ref[...], sums_t_ref[...], pairs_ref[...])
        outs = []
        for c in range(HG_STEP):
            sl = slice(c * CHUNK, (c + 1) * CHUNK)
            valid = (i * rows + c * CHUNK + lax.broadcasted_iota(jnp.int32, (CHUNK, 1), 0)) >= pad
            for h in range(grp):
                st_ref[h, c] = s[h]
            o, s = _hg_chunk(q[sl], f[sl], iv[sl], lbv, s, valid, *masks_v)
            outs.append(o)
        o_ref[...] = jnp.concatenate(outs, axis=0)
        for h in range(grp):
            s_ref[h] = s[h]
        if ng_arr:
            pl.when(last_group & (i == steps - 1))(g_finish)

    masks = _hg_masks()
    grp, ngrp, gw = HG_GROUP, nh // HG_GROUP, HG_GROUP * HEAD_W
    assert n % HG_STEP == 0
    steps, rows = n // HG_STEP, HG_STEP * CHUNK
    blk = lambda off: pl.BlockSpec((rows, gw), lambda h, i: (i, off + h))
    const = lambda a: pl.BlockSpec(a.shape, lambda h, i: (0, 0))
    return pl.pallas_call(
        body, name=name, grid=(ngrp, steps),
        in_specs=[blk(0), blk(ngrp), blk(2 * ngrp), pl.BlockSpec((1, gw), lambda h, i: (0, h))]
        + [const(a) for a in masks] + [pl.BlockSpec(memory_space=pltpu.VMEM)] * ng_arr,
        out_specs=[blk(0), pl.BlockSpec((grp, HG_STEP, HEAD_W, HEAD_W), lambda h, i: (h, i, 0, 0))] + [_ANY] * ng_arr,
        out_shape=[jax.ShapeDtypeStruct((lp, nh * HEAD_W), F32), jax.ShapeDtypeStruct((nh, n, HEAD_W, HEAD_W), F32)]
        + _gather_out_shapes(g_srcs, g_dtypes),
        scratch_shapes=[pltpu.VMEM((grp, HEAD_W, HEAD_W), F32)] + (_gather_scratch(g_srcs, g_dtypes) if ng_arr else []),
        compiler_params=_cp(("arbitrary", "arbitrary"), has_side_effects=bool(ng_arr)),
    )(p1, p1, p1, lb, *masks, *g_srcs)


def _hg_bwd(p1, lb, states, do, pad, *, name, scatter=()):
    lp = p1.shape[0]
    n = lp // CHUNK
    nh = HG_HEADS
    ns = len(scatter)

    def body(q_ref, f_ref, i_ref, lb_ref, st_ref, do_ref, sums_ref, sums_t_ref, pairs_ref, *rest):
        s_ins, (dq_ref, df_ref, di_ref, dlb_ref) = rest[:ns], rest[ns:ns + 4]
        s_outs, ds_ref, s_sems = rest[ns + 4:2 * ns + 4], rest[2 * ns + 4], rest[2 * ns + 5:]
        step = pl.program_id(1)
        i = steps - 1 - step
        if ns:
            s_start, s_finish = _scatter_phases(s_ins, s_outs, *s_sems)
            pl.when((pl.program_id(0) == 0) & (step == 0))(s_start)

        @pl.when(step == 0)
        def _():
            ds_ref[...] = jnp.zeros_like(ds_ref)
            dlb_ref[...] = jnp.zeros_like(dlb_ref)

        q, f, iv, lbv, st, do, dst = q_ref[...], f_ref[...], i_ref[...], lb_ref[...], st_ref[...], do_ref[...], ds_ref[...]
        masks_v = dict(sums=sums_ref[...], sums_t=sums_t_ref[...], pairs=pairs_ref[...])
        vjps = []
        for c in range(HG_STEP):
            sl = slice(c * CHUNK, (c + 1) * CHUNK)
            valid = (i * rows + c * CHUNK + lax.broadcasted_iota(jnp.int32, (CHUNK, 1), 0)) >= pad
            fn = functools.partial(_hg_chunk, valid=valid, **masks_v)
            vjps.append(jax.vjp(fn, q[sl], f[sl], iv[sl], lbv, [st[h, c] for h in range(grp)])[1])
        ds = [dst[h] for h in range(grp)]
        grads = [None] * HG_STEP
        for c in reversed(range(HG_STEP)):
            grads[c] = vjps[c]((do[c * CHUNK:(c + 1) * CHUNK], ds))
            ds = grads[c][4]
        for j, ref in enumerate((dq_ref, df_ref, di_ref)):
            ref[...] = jnp.concatenate([gr[j] for gr in grads], axis=0)
        dlb_ref[...] += sum(gr[3] for gr in grads)
        for h in range(grp):
            ds_ref[h] = ds[h]
        if ns:
            pl.when((pl.program_id(0) == ngrp - 1) & (step == steps - 1))(s_finish)

    masks = _hg_masks()
    grp, ngrp, gw = HG_GROUP, nh // HG_GROUP, HG_GROUP * HEAD_W
    assert n % HG_STEP == 0
    steps, rows = n // HG_STEP, HG_STEP * CHUNK
    blk = lambda off: pl.BlockSpec((rows, gw), lambda h, s: (steps - 1 - s, off + h))
    const = lambda a: pl.BlockSpec(a.shape, lambda h, s: (0, 0))
    w = nh * HEAD_W
    return pl.pallas_call(
        body, name=name, grid=(ngrp, steps),
        in_specs=[blk(0), blk(ngrp), blk(2 * ngrp), pl.BlockSpec((1, gw), lambda h, s: (0, h)),
                  pl.BlockSpec((grp, HG_STEP, HEAD_W, HEAD_W), lambda h, s: (h, steps - 1 - s, 0, 0)), blk(0)]
        + [const(a) for a in masks] + [_ANY] * ns,
        out_specs=[blk(0), blk(0), blk(0), pl.BlockSpec((1, gw), lambda h, s: (0, h))] + [_ANY] * ns,
        out_shape=[jax.ShapeDtypeStruct((lp, w), F32)] * 3 + [jax.ShapeDtypeStruct((1, w), F32)]
        + [jax.ShapeDtypeStruct(s.shape, s.dtype) for s in scatter],
        scratch_shapes=[pltpu.VMEM((grp, HEAD_W, HEAD_W), F32)] + (_scatter_scratch(ns) if ns else []),
        compiler_params=_cp(("arbitrary", "arbitrary"), has_side_effects=bool(ns)),
    )(p1, p1, p1, lb, states, do, *masks, *scatter)


SB_GROUP = 4
SB_FAR = -110.0


def _sb_cat(kind, first_key=0):
    r = lax.broadcasted_iota(jnp.int32, (SB_BLOCK, 2 * SB_BLOCK), 0)
    c = lax.broadcasted_iota(jnp.int32, (SB_BLOCK, 2 * SB_BLOCK), 1)
    tri = {"after": c < r, "incl": r <= c, "before": r < c}[kind]
    m = ((c >= SB_BLOCK) | tri) & (r >= first_key)
    return jnp.where(m, 1.0, 0.0).astype(BF16)


def _sb_cumsum(x, cat):
    return _dot(x.astype(BF16), cat, NN)


def _sb_logsig(z):
    e = jnp.exp(-jnp.abs(z))
    lse = jnp.where(e < 1e-4, e, jnp.log(1.0 + e))
    lsz = jnp.minimum(z, 0.0) - lse
    return lsz, lsz - z, e


def _sb_stack(x, scale=None):
    lane = lax.broadcasted_iota(jnp.int32, (1, HEAD_W), 1)
    if scale is not None:
        x = x * scale
    return jnp.concatenate([jnp.where(lane < SB_DH, x, 0.0), jnp.where(lane >= SB_DH, x, 0.0)], axis=0).astype(BF16)


def _sb_unstack(x):
    lane = lax.broadcasted_iota(jnp.int32, (1, HEAD_W), 1)
    return jnp.where(lane < SB_DH, x[:SB_BLOCK], x[SB_BLOCK:])


def _sb_fwd(p0, pad, *, name, gather=None):
    lp = p0.shape[0]
    nb = lp // SB_BLOCK
    npair = SB_HEADS // 2
    blk0 = AB_SB // HEAD_W
    scale = SB_DH ** -0.5
    gw = SB_GROUP * SB_BLOCK
    assert pad < SB_BLOCK
    g_srcs, g_dtypes = gather if gather is not None else ([], [])
    ng_arr = len(g_srcs)

    def body(q_ref, k_ref, v_ref, *rest):
        g_ins, (o_ref, tot_ref, nproc_ref) = rest[:ng_arr], rest[ng_arr:ng_arr + 3]
        g_outs, g_scratch = rest[ng_arr + 3:2 * ng_arr + 3], rest[2 * ng_arr + 3:]
        first_step = (pl.program_id(0) == 0) & (pl.program_id(1) == 0)
        last_pair = pl.program_id(0) == npair - 1
        if ng_arr:
            g_start, g_forward, g_finish = _gather_phases(g_ins, g_outs, g_scratch[:ng_arr], *g_scratch[ng_arr:],
                                                          g_dtypes)
            pl.when(first_step)(g_start)
            pl.when(last_pair & (pl.program_id(1) == 0))(g_forward)
        i = pl.program_id(1)
        qs = _sb_stack(q_ref[...], scale)
        qpos = i * SB_BLOCK + lax.broadcasted_iota(jnp.int32, (SB_BLOCK, 1), 0)
        qpos = jnp.concatenate([qpos, qpos], axis=0)
        cat = _sb_cat("after")
        cat0 = _sb_cat("after", pad)
        ng = i // SB_GROUP

        def group(off, nblk, first_cat, allowed, carry):
            acc, run = carry
            kg = k_ref[pl.ds(off, nblk * SB_BLOCK), :].astype(BF16)
            vg = v_ref[pl.ds(off, nblk * SB_BLOCK), :].astype(BF16)
            lsz, l1m, _ = _sb_logsig(_dot(qs, kg, NT))
            if allowed is not None:
                l1m = jnp.where(allowed, l1m, 0.0)
            args = [None] * nblk
            for g in reversed(range(nblk)):
                sl = slice(g * SB_BLOCK, (g + 1) * SB_BLOCK)
                al = _sb_cumsum(l1m[:, sl], first_cat if g == 0 else cat)
                args[g] = lsz[:, sl] + al[:, :SB_BLOCK] + run
                run = run + al[:, SB_BLOCK:]
            wgt = jnp.exp(jnp.concatenate(args, axis=1))
            if allowed is not None:
                wgt = jnp.where(allowed, wgt, 0.0)
            return acc + _dot(wgt.astype(BF16), vg, NN), run

        def below(t, carry):
            gi = ng - 1 - t
            return group(pl.multiple_of(gi * gw, gw), SB_GROUP, jnp.where(gi == 0, cat0, cat), None, carry)

        top = ng * gw

        def top_group(nblk, carry):
            off = pl.multiple_of(jnp.minimum(top, lp - nblk * SB_BLOCK), SB_BLOCK)
            kpos = off + lax.broadcasted_iota(jnp.int32, (1, nblk * SB_BLOCK), 1)
            return group(off, nblk, cat, (kpos < qpos) & (kpos >= pad) & (kpos >= top), carry)

        zero = (jnp.zeros((2 * SB_BLOCK, HEAD_W), F32), jnp.zeros((2 * SB_BLOCK, HEAD_W), F32))
        carry = lax.cond(i - ng * SB_GROUP < SB_GROUP // 2, functools.partial(top_group, SB_GROUP // 2),
                         functools.partial(top_group, SB_GROUP), zero)
        used, acc, run = lax.while_loop(lambda s: (s[0] < ng) & (jnp.max(s[2]) > SB_FAR),
                                        lambda s: (s[0] + 1, *below(s[0], (s[1], s[2]))), (jnp.int32(0), *carry))
        o_ref[...] = _sb_unstack(acc)
        tot_ref[...] = _sb_unstack(run)
        nproc_ref[pl.program_id(0), i] = used.astype(F32)
        if ng_arr:
            pl.when(last_pair & (pl.program_id(1) == nb - 1))(g_finish)

    full = lambda c0: pl.BlockSpec((lp, HEAD_W), lambda p, i: (0, c0 + p))
    out = pl.BlockSpec((SB_BLOCK, HEAD_W), lambda p, i: (i, p))
    return pl.pallas_call(
        body, name=name, grid=(npair, nb),
        in_specs=[pl.BlockSpec((SB_BLOCK, HEAD_W), lambda p, i: (i, blk0 + p)), full(blk0 + npair), full(blk0 + 2 * npair)]
        + [pl.BlockSpec(memory_space=pltpu.VMEM)] * ng_arr,
        out_specs=[out, out, pl.BlockSpec(memory_space=pltpu.SMEM)] + [_ANY] * ng_arr,
        out_shape=[jax.ShapeDtypeStruct((lp, npair * HEAD_W), F32)] * 2 + [jax.ShapeDtypeStruct((npair, nb), F32)]
        + _gather_out_shapes(g_srcs, g_dtypes),
        scratch_shapes=_gather_scratch(g_srcs, g_dtypes) if ng_arr else [],
        compiler_params=_cp(("arbitrary", "arbitrary"), has_side_effects=bool(ng_arr)),
    )(p0, p0, p0, *g_srcs)


def _sb_bwd(p0, tot, nproc, dsrc, d_blk0, pad, *, name, scatter=()):
    lp = p0.shape[0]
    nb = lp // SB_BLOCK
    npair = SB_HEADS // 2
    blk0 = AB_SB // HEAD_W
    scale = SB_DH ** -0.5
    gw = SB_GROUP * SB_BLOCK
    assert pad < SB_BLOCK
    ns = len(scatter)

    def body(q_ref, k_ref, v_ref, tot_ref, nproc_ref, do_ref, *rest):
        s_ins, (dq_ref, dkt_ref, dvt_ref) = rest[:ns], rest[ns:ns + 3]
        s_outs, s_sems = rest[ns + 3:2 * ns + 3], rest[2 * ns + 3:]
        if ns:
            s_start, s_finish = _scatter_phases(s_ins, s_outs, *s_sems)
            pl.when((pl.program_id(0) == 0) & (pl.program_id(1) == 0))(s_start)
        i = pl.program_id(1)

        @pl.when(i == 0)
        def _():
            dkt_ref[...] = jnp.zeros_like(dkt_ref)
            dvt_ref[...] = jnp.zeros_like(dvt_ref)

        qs = _sb_stack(q_ref[...], scale)
        dos = _sb_stack(do_ref[...])
        qst, dost = qs.T, dos.T
        totv = tot_ref[...]
        ones = jnp.ones((1, HEAD_W), F32)
        tots = jnp.concatenate([totv[:, 0:1] * ones, totv[:, SB_DH:SB_DH + 1] * ones], axis=0)
        qpos = i * SB_BLOCK + lax.broadcasted_iota(jnp.int32, (SB_BLOCK, 1), 0)
        qpos = jnp.concatenate([qpos, qpos], axis=0)
        incl, incl0 = _sb_cat("incl"), _sb_cat("incl", pad)
        before = _sb_cat("before")
        ng = i // SB_GROUP
        used = jnp.clip(nproc_ref[pl.program_id(0), i].astype(jnp.int32), 0, ng)

        def dscore(z, e, ev, dl1m):
            r = 1.0 / (1.0 + e)
            sg = jnp.where(z >= 0, r, e * r)
            return ev * (1.0 - sg) - dl1m * sg

        def group(off, nblk, first_incl, allowed, carry):
            dq, prun, erun = carry
            width = nblk * SB_BLOCK
            kg = k_ref[pl.ds(off, width), :].astype(BF16)
            vg = v_ref[pl.ds(off, width), :].astype(BF16)
            z = _dot(qs, kg, NT)
            lsz, l1m, e = _sb_logsig(z)
            if allowed is not None:
                l1m = jnp.where(allowed, l1m, 0.0)
            dwgt = _dot(dos, vg, NT)
            dzs = [None] * nblk
            wgts = [None] * nblk
            for g in range(nblk):
                sl = slice(g * SB_BLOCK, (g + 1) * SB_BLOCK)
                al = _sb_cumsum(l1m[:, sl], first_incl if g == 0 else incl)
                wgt = jnp.exp(jnp.minimum(lsz[:, sl] + (tots - prun - al[:, :SB_BLOCK]), 0.0))
                if allowed is not None:
                    wgt = jnp.where(allowed[:, sl], wgt, 0.0)
                prun = prun + al[:, SB_BLOCK:]
                ev = wgt * dwgt[:, sl]
                el = _sb_cumsum(ev, before)
                dzs[g] = dscore(z[:, sl], e[:, sl], ev, erun + el[:, :SB_BLOCK])
                erun = erun + el[:, SB_BLOCK:]
                wgts[g] = wgt
            dz = jnp.concatenate(dzs, axis=1)
            if allowed is not None:
                dz = jnp.where(allowed, dz, 0.0)
            dz = dz.astype(BF16)
            wg = jnp.concatenate(wgts, axis=1).astype(BF16)
            dkt_ref[:, pl.ds(off, width)] += _dot(qst, dz, NN)
            dvt_ref[:, pl.ds(off, width)] += _dot(dost, wg, NN)
            return dq + _dot(dz, kg, NN), prun, erun

        def below(gi, carry):
            return group(pl.multiple_of(gi * gw, gw), SB_GROUP, jnp.where(gi == 0, incl0, incl), None, carry)

        zero = tuple(jnp.zeros((2 * SB_BLOCK, HEAD_W), F32) for _ in range(3))
        carry = lax.fori_loop(ng - used, ng, below, zero)
        top = ng * gw

        def top_group(nblk, carry):
            off = pl.multiple_of(jnp.minimum(top, lp - nblk * SB_BLOCK), SB_BLOCK)
            kpos = off + lax.broadcasted_iota(jnp.int32, (1, nblk * SB_BLOCK), 1)
            return group(off, nblk, incl, (kpos < qpos) & (kpos >= pad) & (kpos >= top), carry)

        dq, _, _ = lax.cond(i - ng * SB_GROUP < SB_GROUP // 2, functools.partial(top_group, SB_GROUP // 2),
                            functools.partial(top_group, SB_GROUP), carry)
        dq_ref[...] = _sb_unstack(dq) * scale
        if ns:
            pl.when((pl.program_id(0) == npair - 1) & (pl.program_id(1) == nb - 1))(s_finish)

    full = lambda c0: pl.BlockSpec((lp, HEAD_W), lambda p, i: (0, c0 + p))
    qb = lambda c0: pl.BlockSpec((SB_BLOCK, HEAD_W), lambda p, i: (i, c0 + p))
    tr = pl.BlockSpec((HEAD_W, lp), lambda p, i: (p, 0))
    return pl.pallas_call(
        body, name=name, grid=(npair, nb),
        in_specs=[qb(blk0), full(blk0 + npair), full(blk0 + 2 * npair), qb(0), pl.BlockSpec(memory_space=pltpu.SMEM),
                  qb(d_blk0)] + [_ANY] * ns,
        out_specs=[qb(0), tr, tr] + [_ANY] * ns,
        out_shape=[jax.ShapeDtypeStruct((lp, npair * HEAD_W), F32)]
        + [jax.ShapeDtypeStruct((npair * HEAD_W, lp), F32)] * 2
        + [jax.ShapeDtypeStruct(s.shape, s.dtype) for s in scatter],
        scratch_shapes=_scatter_scratch(ns) if ns else [],
        compiler_params=_cp(("arbitrary", "arbitrary"), has_side_effects=bool(ns)),
    )(p0, p0, p0, tot, nproc, dsrc, *scatter)


def _local_step(h0, target, pad, wts, hooks=None):
    lp = h0.shape[0]
    tm = _row_tile(lp, 1056)
    tkl = tm
    tml = _row_tile(lp, 528)
    d = D_MODEL
    mm = _mm
    mmw = functools.partial(_mm, out_dtype=BF16)
    g = {}

    h0_b = h0.astype(BF16)
    p0 = mm(h0_b, wts["w_ab"], "NN", tm=tm, tn=768, tk=d, name="l0_in_proj")
    ob, sb_tot, sb_used, *gathered = _sb_fwd(p0, pad, name="sb_fwd", gather=hooks["gather_a"] if hooks else None)
    if hooks:
        wts = {**wts, **hooks["weights_a"](gathered)}
    qkv = _gdn_pre_fwd(p0, wts["conv_w"], pad, name="gdn_pre_fwd")
    oa_raw, gdn_states, *gathered = _gdn_fwd(qkv, p0, wts["alog_v"], wts["dtb_v"], pad, name="gdn_fwd",
                                             gather=hooks["gather_b"] if hooks else None)
    if hooks:
        wts = {**wts, **hooks["weights_b"](gathered)}
    rows = lambda a, n: a.reshape(N_DEV, n // N_DEV, d)
    parts = g["parts"] = {}
    oab = _gate_fwd(oa_raw, p0, AB_Z // HEAD_W, wts["ab_gn"], ob, heads=GDN_HEADS, name="gdn_gate_fwd")
    ln = lambda kind, layer: (wts[f"ln_{kind}_g"][layer], wts[f"ln_{kind}_b"][layer])
    pre_mix0, h0a, h0a_b = mm(oab, wts["w_out0"], "NN", tm=tm, tn=d, tk=d, epi="ln", c=h0, scale=DN_ALPHA,
                              ln=ln("mix", 0), name="l0_out_proj")
    u0, act0 = mm(h0a_b, wts["w1"][0], "NN", tm=tm, tn=512, tk=d, b_dev=True, epi="relu2_copy", name="mlp0_up")
    pre_ffn0, h0b, h0b_b = mm(act0, wts["w2"][0], "NN", tm=tm, tn=d, tk=d, epi="ln", c=h0a, scale=DN_ALPHA,
                              ln=ln("ffn", 0), name="mlp0_down")
    p1 = mm(h0b_b, wts["w_c"], "NN", tm=tm, tn=512, tk=d, b_dev=True, name="l1_in_proj")
    oc_raw, hg_states, *gathered = _hg_fwd(p1, wts["lb"], pad, name="hg_fwd",
                                           gather=hooks["gather_c"] if hooks else None)
    if hooks:
        third = hooks["weights_c"](gathered)
        wts = {**wts, "w1": wts["w1"] + third["w1"], "w2": wts["w2"] + third["w2"]}
    oc = _gate_fwd(oc_raw, p1, 3 * HG_HEADS, wts["c_gn"], oc_raw, heads=HG_HEADS, name="hg_gate_fwd")
    pre_mix1, h1a, h1a_b = mm(oc, wts["w_out1"], "NN", tm=tm, tn=d, tk=d, epi="ln", c=h0b, scale=DN_ALPHA,
                              ln=ln("mix", 1), name="l1_out_proj")
    u1, act1 = mm(h1a_b, wts["w1"][1], "NN", tm=tm, tn=512, tk=d, b_dev=True, epi="relu2_copy", name="mlp1_up")
    pre_ffn1, h1b, _ = mm(act1, wts["w2"][1], "NN", tm=tm, tn=d, tk=d, epi="ln", c=h1a, scale=DN_ALPHA,
                          ln=ln("ffn", 1), name="mlp1_down")
    dy, loss_vec = _loss_head(h1b, target, name="loss_head")

    def mlp_bwd(layer, h_in_b, u, act, dpre, dpre_b, pre_mix):
        du = mm(dpre_b, wts["w2"][layer], "NT", tm=tm, tn=1024, tk=d, epi="relu2grad", c=u, out_dtype=BF16,
                name=f"mlp{layer}_d_hidden")
        dw2 = mmw(act, dpre_b, "TN", tm=1024, tn=1024, tk=tkl, name=f"mlp{layer}_dw2")
        dw1 = mmw(h_in_b, du, "TN", tm=1024, tn=512, tk=tkl, out_dev=True, name=f"mlp{layer}_dw1")
        return (*mm(du, k_major(wts["w1"][layer]), "NT", tm=tml, tn=1024, tk=2048, epi="ln_bwd", c=dpre, scale=DN_ALPHA,
                    ln=(pre_mix, wts["ln_mix_g"][layer]), name=f"mlp{layer}_d_in"), dw1, dw2)

    k_major = lambda wd: wd.transpose(1, 0, 2).reshape(wd.shape[1], -1)

    ln_ffn_dg, ln_ffn_db, ln_mix_dg, ln_mix_db, dw1s, dw2s = ([None, None] for _ in range(6))
    dpre, dpre_b, ln_ffn_dg[1], ln_ffn_db[1] = _ln_bwd(pre_ffn1, wts["ln_ffn_g"][1], dy, name="ln_ffn1_bwd")
    dpre, dpre_b, ln_mix_dg[1], ln_mix_db[1], dw1s[1], dw2s[1] = mlp_bwd(1, h1a_b, u1, act1, dpre, dpre_b, pre_mix1)
    g["c_w_out"] = mmw(oc, dpre_b, "TN", tm=1024, tn=1024, tk=tkl, name="l1_dw_out")
    doc = mm(dpre_b, wts["w_out1"], "NT", tm=tm, tn=1024, tk=d, name="l1_d_gate")
    doc_raw, dz1, g["c_gn"] = _gate_bwd(oc_raw, p1, 3 * HG_HEADS, wts["c_gn"], doc, heads=HG_HEADS, name="hg_gate_bwd")
    ready = [dw1s[1], rows(dw2s[1], D_FF), rows(g["c_w_out"], d)] if hooks else ()
    dq1, df1, di1, g["lb"], *got = _hg_bwd(p1, wts["lb"], hg_states, doc_raw, pad, name="hg_bwd", scatter=ready)
    parts.update(zip(("mlp_w1_1", "mlp_w2_1", "c_w_out"), got))
    dp1 = jnp.concatenate([dq1, df1, di1, dz1], axis=1).astype(BF16)
    g["c_w_in"] = mmw(h0b_b, dp1, "TN", tm=1024, tn=512, tk=tkl, out_dev=True, name="l1_dw_in")
    dpre, dpre_b, ln_ffn_dg[0], ln_ffn_db[0] = mm(
        dp1, k_major(wts["w_c"]), "NT", tm=tml, tn=1024, tk=2048, epi="ln_bwd", c=dpre, scale=DN_ALPHA,
        ln=(pre_ffn0, wts["ln_ffn_g"][0]), name="l1_d_in")
    dpre, dpre_b, ln_mix_dg[0], ln_mix_db[0], dw1s[0], dw2s[0] = mlp_bwd(0, h0a_b, u0, act0, dpre, dpre_b, pre_mix0)
    g["ab_w_out"] = mmw(oab, dpre_b, "TN", tm=1024, tn=1024, tk=tkl, name="l0_dw_out")
    doab = mm(dpre_b, wts["w_out0"], "NT", tm=tm, tn=1024, tk=d, name="l0_d_gate")
    doa_raw, dz0, g["ab_gn"] = _gate_bwd(oa_raw, p0, AB_Z // HEAD_W, wts["ab_gn"], doab, heads=GDN_HEADS,
                                         name="gdn_gate_bwd")
    ready = [g["c_w_in"]] if hooks else ()
    dqb, dkb_t, dvb_t, *got = _sb_bwd(p0, sb_tot, sb_used, doab, GDN_HEADS, pad, name="sb_bwd", scatter=ready)
    parts.update(zip(("c_w_in",), got))
    dkb, dvb = dkb_t.T, dvb_t.T
    ready = [dw1s[0], rows(dw2s[0], D_FF), rows(g["ab_w_out"], d)] if hooks else ()
    dqn, dkn, dvn, dba, g["alog_v"], g["dtb_v"], *got = _gdn_bwd(qkv, p0, wts["alog_v"], wts["dtb_v"], gdn_states,
                                                                 doa_raw, pad, name="gdn_bwd", scatter=ready)
    parts.update(zip(("mlp_w1_0", "mlp_w2_0", "ab_w_out"), got))
    dconv_in, g["conv_w"] = _gdn_pre_bwd(p0, wts["conv_w"], jnp.concatenate([dqn, dkn, dvn], axis=1), pad,
                                         name="gdn_pre_bwd")
    dp0 = jnp.concatenate([dconv_in, dz0, dqb, dkb, dvb, dba, jnp.zeros((lp, AB_CAT - AB_BA - HEAD_W), F32)],
                          axis=1).astype(BF16)
    g["w_ab"] = mmw(h0_b, dp0, "TN", tm=1024, tn=768, tk=tkl, name="l0_dw_in")
    last = ()
    if hooks:
        gab, ba0 = g["w_ab"], AB_Z + GDN_HEADS * HEAD_W
        gab = jnp.concatenate([gab[:, :ba0], gab[:, AB_BA:AB_BA + 2 * GDN_HEADS], gab[:, ba0:AB_BA]], axis=1)
        last = [gab.reshape(d, N_DEV, AB_IN // N_DEV).transpose(1, 0, 2)]
    res = mm(dp0, wts["w_ab"], "NT", tm=tm, tn=1024, tk=1920, epi="add", c=dpre, scale=DN_ALPHA, scatter=last,
             name="l0_d_in")
    dh0 = res[0] if last else res
    parts.update(zip(("ab_w_in",), res[1:] if last else ()))

    g["w1"], g["w2"] = dw1s, dw2s
    g["ln_mix_g"] = jnp.concatenate(ln_mix_dg, axis=0)
    g["ln_mix_b"] = jnp.concatenate(ln_mix_db, axis=0)
    g["ln_ffn_g"] = jnp.concatenate(ln_ffn_dg, axis=0)
    g["ln_ffn_b"] = jnp.concatenate(ln_ffn_db, axis=0)
    return loss_vec, dh0, g


N_CHIP = N_DEV // 2


def _place():
    x, y, c = lax.axis_index("x"), lax.axis_index("y"), lax.axis_index("c")
    return x, y, c, 2 * x + y


def _chip_dev(chip, core):
    return (chip // 2, chip % 2, core)


def _remote(src, dst, send_sem, recv_sem, dev):
    return pltpu.make_async_remote_copy(src_ref=src, dst_ref=dst, send_sem=send_sem, recv_sem=recv_sem,
                                        device_id=dev, device_id_type=pl.DeviceIdType.MESH)


_ANY = pl.BlockSpec(memory_space=pl.ANY)


def _gather(srcs, dtypes, *, name):
    n = len(srcs)

    def body(*refs):
        start, forward, finish = _gather_phases(refs[:n], refs[n:2 * n], refs[2 * n:3 * n], *refs[3 * n:], dtypes)
        start()
        forward()
        finish()

    return pl.pallas_call(
        body, name=name, in_specs=[pl.BlockSpec(memory_space=pltpu.VMEM)] * n, out_specs=[_ANY] * n,
        out_shape=_gather_out_shapes(srcs, dtypes), scratch_shapes=_gather_scratch(srcs, dtypes),
        compiler_params=_cp(has_side_effects=True),
    )(*srcs)


def _gather_out_shapes(srcs, dtypes):
    return [jax.ShapeDtypeStruct((N_DEV, *s.shape), dt) for s, dt in zip(srcs, dtypes)]


def _gather_scratch(srcs, dtypes):
    n = len(srcs)
    return [pltpu.VMEM(s.shape, dt) for s, dt in zip(srcs, dtypes)] + [
        pltpu.SemaphoreType.DMA((n, 2 * N_CHIP - 1)), pltpu.SemaphoreType.DMA((n, 2 * N_CHIP - 1)),
        pltpu.SemaphoreType.DMA((n,))]


def _gather_phases(ins, outs, stages, send_sems, recv_sems, local_sems, dtypes):
    n = len(ins)
    x, y, c, chip = _place()
    me = 2 * chip + c
    sibling = (x, y, 1 - c)

    def own(i):
        cps = [_remote(stages[i], outs[i].at[me], send_sems.at[i, 0], recv_sems.at[i, 0], sibling)]
        for j in range(1, N_CHIP):
            cps.append(_remote(stages[i], outs[i].at[me], send_sems.at[i, j], recv_sems.at[i, j],
                               _chip_dev(jnp.bitwise_xor(chip, j), c)))
        return cps

    def local(i):
        return pltpu.make_async_copy(stages[i], outs[i].at[me], local_sems.at[i])

    def passed_on(i, j):
        slot = outs[i].at[2 * jnp.bitwise_xor(chip, j) + c]
        return _remote(slot, slot, send_sems.at[i, N_CHIP - 1 + j], recv_sems.at[i, N_CHIP - 1 + j], sibling)

    def start():
        for i in range(n):
            stages[i][...] = ins[i][...].astype(dtypes[i])
            local(i).start()
            for cp in own(i):
                cp.start()

    def forward():
        for i in range(n):
            for j in range(1, N_CHIP):
                own(i)[j].wait_recv()
                passed_on(i, j).start()

    def finish():
        for i in range(n):
            own(i)[0].wait_recv()
            for j in range(1, N_CHIP):
                passed_on(i, j).wait_recv()
        for i in range(n):
            for cp in own(i):
                cp.wait_send()
            for j in range(1, N_CHIP):
                passed_on(i, j).wait_send()
            local(i).wait()

    return start, forward, finish


def _scatter_scratch(n):
    return [pltpu.SemaphoreType.DMA((n, N_DEV - 1)), pltpu.SemaphoreType.DMA((n, N_DEV - 1)),
            pltpu.SemaphoreType.DMA((n,))]


def _scatter_phases(ins, outs, send_sems, recv_sems, local_sems):
    n = len(ins)
    _, _, c, chip = _place()
    me = 2 * chip + c

    def copies():
        cps = []
        for i in range(n):
            cps.append(pltpu.make_async_copy(ins[i].at[me], outs[i].at[me], local_sems.at[i]))
            for k in range(1, N_DEV):
                peer = jnp.bitwise_xor(me, k)
                cps.append(_remote(ins[i].at[peer], outs[i].at[me], send_sems.at[i, k - 1], recv_sems.at[i, k - 1],
                                   _chip_dev(peer // 2, peer % 2)))
        return cps

    def start():
        for cp in copies():
            cp.start()

    def finish():
        for cp in copies():
            cp.wait()

    return start, finish


def _adamw(w, parts, m, v, *, name):
    r, c = w.shape
    s = parts.shape[0]
    tm = _row_tile(r, 128) if r % 8 == 0 else r
    c1 = 1.0 - ADAM_B1 ** ADAM_STEP
    c2 = 1.0 - ADAM_B2 ** ADAM_STEP

    def body(w_ref, p_ref, m_ref, v_ref, g_ref, d_ref, m2_ref, v2_ref):
        g = p_ref[0].astype(F32)
        for j in range(1, s):
            g = g + p_ref[j].astype(F32)
        m2 = ADAM_B1 * m_ref[...] + (1.0 - ADAM_B1) * g
        v2 = ADAM_B2 * v_ref[...] + (1.0 - ADAM_B2) * jnp.square(g)
        g_ref[...] = g
        m2_ref[...] = m2
        v2_ref[...] = v2
        d_ref[...] = -ADAM_LR * ((m2 / c1) / (jnp.sqrt(v2 / c2) + ADAM_EPS) + ADAM_WD * w_ref[...])

    blk = pl.BlockSpec((tm, c), lambda i: (i, 0))
    return pl.pallas_call(
        body, name=name, grid=(r // tm,),
        in_specs=[blk, pl.BlockSpec((s, tm, c), lambda i: (0, i, 0)), blk, blk], out_specs=[blk] * 4,
        out_shape=[jax.ShapeDtypeStruct((r, c), F32)] * 4, compiler_params=_cp(("parallel",)),
    )(w, parts, m, v)


_WEIGHTS = ("meta_tokens", "ab_w_in", "ab_conv_w", "ab_a_log", "ab_dt_bias", "ab_gnorm_g", "ab_w_out", "c_w_in",
            "c_lb_raw", "c_gnorm_g", "c_w_out", "ln_mix_g", "ln_mix_b", "mlp_w1", "mlp_w2", "ln_ffn_g", "ln_ffn_b")
_PACK_ROWS = (("ln_mix_g", 0), ("ln_mix_b", 2), ("ln_ffn_g", 4), ("ln_ffn_b", 6), ("c_lb_raw", 8))
_PACK_MISC_ROW = 10
_PACK_MISC = (("ab_gnorm_g", 0, 128), ("c_gnorm_g", 128, 128), ("ab_a_log", 256, GDN_HEADS), ("ab_dt_bias", 260, GDN_HEADS))
_PACK_N = 16
_SMALL_META = 16
_SMALL_CONV = 32
_SMALL_N = 40


def _pack_replicated(p):
    rows = jnp.zeros((_PACK_N, D_MODEL), F32)
    for name, r0 in _PACK_ROWS:
        rows = rows.at[r0:r0 + 2].set(p[name])
    for name, c0, width in _PACK_MISC:
        rows = rows.at[_PACK_MISC_ROW, c0:c0 + width].set(p[name].reshape(width))
    return rows


def _unpack_replicated(rows, like):
    out = {}
    for name, r0 in _PACK_ROWS:
        out[name] = rows[r0:r0 + 2]
    for name, c0, width in _PACK_MISC:
        out[name] = rows[_PACK_MISC_ROW, c0:c0 + width].reshape(like[name].shape)
    return out


def _lower_bound(c_lb_raw):
    lb_all = jnp.cumsum(jax.nn.softmax(c_lb_raw.astype(F32), axis=0), axis=0)
    return (lb_all - lb_all[0:1])[1].reshape(1, -1)


def kernel(x, meta_tokens, ab_w_in, ab_conv_w, ab_a_log, ab_dt_bias, ab_gnorm_g, ab_w_out, c_w_in, c_lb_raw, c_gnorm_g, c_w_out, ln_mix_g, ln_mix_b, mlp_w1, mlp_w2, ln_ffn_g, ln_ffn_b, loss_target, m_meta_tokens, m_ab_w_in, m_ab_conv_w, m_ab_a_log, m_ab_dt_bias, m_ab_gnorm_g, m_ab_w_out, m_c_w_in, m_c_lb_raw, m_c_gnorm_g, m_c_w_out, m_ln_mix_g, m_ln_mix_b, m_mlp_w1, m_mlp_w2, m_ln_ffn_g, m_ln_ffn_b, v_meta_tokens, v_ab_w_in, v_ab_conv_w, v_ab_a_log, v_ab_dt_bias, v_ab_gnorm_g, v_ab_w_out, v_c_w_in, v_c_lb_raw, v_c_gnorm_g, v_c_w_out, v_ln_mix_g, v_ln_mix_b, v_mlp_w1, v_mlp_w2, v_ln_ffn_g, v_ln_ffn_b):
    w = dict(zip(_WEIGHTS, (meta_tokens, ab_w_in, ab_conv_w, ab_a_log, ab_dt_bias, ab_gnorm_g, ab_w_out, c_w_in, c_lb_raw,
                            c_gnorm_g, c_w_out, ln_mix_g, ln_mix_b, mlp_w1, mlp_w2, ln_ffn_g, ln_ffn_b)))
    mom = dict(zip(_WEIGHTS, (m_meta_tokens, m_ab_w_in, m_ab_conv_w, m_ab_a_log, m_ab_dt_bias, m_ab_gnorm_g, m_ab_w_out,
                              m_c_w_in, m_c_lb_raw, m_c_gnorm_g, m_c_w_out, m_ln_mix_g, m_ln_mix_b, m_mlp_w1, m_mlp_w2,
                              m_ln_ffn_g, m_ln_ffn_b)))
    var = dict(zip(_WEIGHTS, (v_meta_tokens, v_ab_w_in, v_ab_conv_w, v_ab_a_log, v_ab_dt_bias, v_ab_gnorm_g, v_ab_w_out,
                              v_c_w_in, v_c_lb_raw, v_c_gnorm_g, v_c_w_out, v_ln_mix_g, v_ln_mix_b, v_mlp_w1, v_mlp_w2,
                              v_ln_ffn_g, v_ln_ffn_b)))
    me = 4 * lax.axis_index("x") + 2 * lax.axis_index("y") + lax.axis_index("c")
    seq = x.shape[1]
    pad = (-(N_META + seq)) % SB_BLOCK
    lp = pad + N_META + seq
    meta_w = D_MODEL // N_DEV
    conv_w_all = 2 * GDN_HEADS * HEAD_W + GDN_HEADS * HEAD_W
    conv_w_mine = conv_w_all // N_DEV

    g_meta, g_conv, g_ab_in = _gather([w["meta_tokens"], w["ab_conv_w"][0], w["ab_w_in"][0]], [F32, F32, BF16],
                                      name="gather_weights_first")
    meta_full = g_meta.transpose(1, 0, 2).reshape(N_META, D_MODEL)
    conv_full = g_conv.transpose(1, 0, 2).reshape(CONV_K, conv_w_all)
    ab_full = g_ab_in.transpose(1, 0, 2).reshape(D_MODEL, AB_IN)
    ba0 = AB_Z + 512
    w_ab = jnp.concatenate([ab_full[:, :ba0], ab_full[:, ba0 + 2 * GDN_HEADS:], ab_full[:, ba0:ba0 + 2 * GDN_HEADS],
                            jnp.zeros((D_MODEL, AB_CAT - AB_IN), BF16)], axis=1)
    vec128 = lambda p: jnp.zeros((1, HEAD_W), F32).at[0, :GDN_HEADS].set(p.reshape(GDN_HEADS))
    wts = dict(
        w_ab=w_ab, conv_w=conv_full, alog_v=vec128(w["ab_a_log"]), dtb_v=vec128(w["ab_dt_bias"]),
        ab_gn=w["ab_gnorm_g"][0], lb=_lower_bound(w["c_lb_raw"]), c_gn=w["c_gnorm_g"][0],
        ln_mix_g=w["ln_mix_g"], ln_mix_b=w["ln_mix_b"], ln_ffn_g=w["ln_ffn_g"], ln_ffn_b=w["ln_ffn_b"])

    def weights_a(gathered):
        g_ab_out, g_w1, g_w2 = gathered
        return dict(w_out0=g_ab_out.reshape(D_MODEL, D_MODEL), w1=[g_w1], w2=[g_w2.reshape(D_FF, D_MODEL)])

    def weights_b(gathered):
        g_c_in, g_c_out = gathered
        return dict(w_c=g_c_in, w_out1=g_c_out.reshape(D_MODEL, D_MODEL))

    def weights_c(gathered):
        g_w1, g_w2 = gathered
        return dict(w1=[g_w1], w2=[g_w2.reshape(D_FF, D_MODEL)])

    hooks = dict(
        gather_a=([w["ab_w_out"][0], w["mlp_w1"][0], w["mlp_w2"][0]], [BF16] * 3), weights_a=weights_a,
        gather_b=([w["c_w_in"][0], w["c_w_out"][0]], [BF16] * 2), weights_b=weights_b,
        gather_c=([w["mlp_w1"][1], w["mlp_w2"][1]], [BF16] * 2), weights_c=weights_c)

    h0 = jnp.concatenate([jnp.zeros((pad, D_MODEL), F32), meta_full, x[0]], axis=0)
    loss_vec, dh0, g = _local_step(h0, loss_target[0], pad, wts, hooks)
    loss = lax.psum(jnp.sum(loss_vec), ("x", "y", "c"))
    grad_x = dh0[lp - seq:][None]

    _, lb_vjp = jax.vjp(_lower_bound, w["c_lb_raw"])
    rep_part = _pack_replicated(dict(
        ln_mix_g=g["ln_mix_g"], ln_mix_b=g["ln_mix_b"], ln_ffn_g=g["ln_ffn_g"], ln_ffn_b=g["ln_ffn_b"],
        c_lb_raw=lb_vjp(g["lb"])[0], ab_gnorm_g=g["ab_gn"], c_gnorm_g=g["c_gn"],
        ab_a_log=g["alog_v"][0, :GDN_HEADS], ab_dt_bias=g["dtb_v"][0, :GDN_HEADS]))
    small = jnp.concatenate([rep_part, dh0[pad:pad + N_META], g["conv_w"].reshape(-1, D_MODEL),
                             jnp.zeros((_SMALL_N - _SMALL_CONV - CONV_K * conv_w_all // D_MODEL, D_MODEL), F32)], axis=0)
    (small_all,) = _gather([small], [F32], name="gather_small_grads")
    rep_out = _adamw(_pack_replicated(w), small_all[:, :_PACK_N], _pack_replicated(mom), _pack_replicated(var),
                     name="adamw_replicated")
    meta_parts = lax.dynamic_slice_in_dim(small_all[:, _SMALL_META:_SMALL_META + N_META], me * meta_w, meta_w, axis=2)
    meta_out = _adamw(w["meta_tokens"], meta_parts, mom["meta_tokens"], var["meta_tokens"], name="adamw_meta")
    conv_parts = small_all[:, _SMALL_CONV:_SMALL_CONV + CONV_K * conv_w_all // D_MODEL].reshape(N_DEV, CONV_K, conv_w_all)
    conv_parts = lax.dynamic_slice_in_dim(conv_parts, me * conv_w_mine, conv_w_mine, axis=2)
    conv_out = _adamw(w["ab_conv_w"][0], conv_parts, mom["ab_conv_w"][0], var["ab_conv_w"][0], name="adamw_conv")

    parts = g["parts"]
    big = [("ab_w_in", 0, parts["ab_w_in"]), ("ab_w_out", 0, parts["ab_w_out"]), ("mlp_w1", 0, parts["mlp_w1_0"]),
           ("mlp_w2", 0, parts["mlp_w2_0"]), ("c_w_in", 0, parts["c_w_in"]), ("c_w_out", 0, parts["c_w_out"]),
           ("mlp_w1", 1, parts["mlp_w1_1"]), ("mlp_w2", 1, parts["mlp_w2_1"])]
    big_out = {}
    for name, l, p in big:
        res = _adamw(w[name][l], p, mom[name][l], var[name][l], name=f"adamw_{name}{l}")
        big_out.setdefault(name, []).append(res)

    rep = [_unpack_replicated(r, w) for r in rep_out]
    outs = {}
    for name in _WEIGHTS:
        if name == "meta_tokens":
            outs[name] = list(meta_out)
        elif name == "ab_conv_w":
            outs[name] = [o[None] for o in conv_out]
        elif name in big_out:
            res = big_out[name]
            outs[name] = [o[None] for o in res[0]] if len(res) == 1 else [jnp.stack(pair) for pair in zip(*res)]
        else:
            outs[name] = [r[name] for r in rep]
    flat = [loss, grad_x]
    for kind in range(4):
        flat += [outs[name][kind] for name in _WEIGHTS]
    return tuple(flat)
```

```python
import functools

import jax
import jax.numpy as jnp
from jax import lax
from jax.experimental import pallas as pl
from jax.experimental.pallas import tpu as pltpu

F32 = jnp.float32
BF16 = jnp.bfloat16

N_DEV = 8
D_MODEL = 1024
N_META = 16
D_FF = 4096
DEPTH = 2
GDN_HEADS = 4
SB_HEADS = 8
SB_DH = 64
HG_HEADS = 8
HEAD_W = 128
CHUNK = 64
SB_BLOCK = 128
CONV_K = 4
DN_ALPHA = float((2 * DEPTH) ** 0.25)
LN_EPS = 1e-5
RMS_EPS = 1e-6
L2_EPS = 1e-6
ADAM_LR, ADAM_B1, ADAM_B2, ADAM_EPS, ADAM_WD, ADAM_STEP = 0.001, 0.9, 0.999, 1e-08, 0.01, 10

AB_Z = 1536
AB_SB = 2048
AB_BA = 3584
AB_CAT = 3840
AB_IN = 3592

VMEM_LIMIT = 56 * 1024 * 1024


def _cp(sem=None, **kw):
    if sem is not None:
        kw["dimension_semantics"] = sem
    return pltpu.CompilerParams(vmem_limit_bytes=VMEM_LIMIT, **kw)


def _row_tile(n, want):
    best = 8
    for t in range(8, min(n, want) + 1, 8):
        if n % t == 0:
            best = t
    return best


@jax.custom_vjp
def _sigmoid(x):
    e = jnp.exp(-jnp.abs(x))
    r = 1.0 / (1.0 + e)
    return jnp.where(x >= 0, r, e * r)


def _sigmoid_fwd(x):
    s = _sigmoid(x)
    return s, s


def _sigmoid_bwd(s, g):
    return (g * s * (1.0 - s),)


_sigmoid.defvjp(_sigmoid_fwd, _sigmoid_bwd)


def _log1p_exp_neg_abs(x):
    e = jnp.exp(-jnp.abs(x))
    return jnp.where(e < 1e-4, e - 0.5 * e * e, jnp.log(1.0 + e))


@jax.custom_vjp
def _softplus(x):
    return jnp.maximum(x, 0.0) + _log1p_exp_neg_abs(x)


def _softplus_fwd(x):
    return _softplus(x), x


def _softplus_bwd(x, g):
    return (g * _sigmoid(x),)


_softplus.defvjp(_softplus_fwd, _softplus_bwd)


def _silu(x):
    return x * _sigmoid(x)


def _silu_grad(x):
    s = _sigmoid(x)
    return s * (1.0 + x * (1.0 - s))


def _dot(a, b, dims, precision=None):
    return lax.dot_general(a, b, (dims, ((), ())), precision=precision, preferred_element_type=F32)


NN = ((1,), (0,))
NT = ((1,), (1,))
TN = ((0,), (0,))


def _bdot(a, b, dims):
    return _dot(a.astype(BF16), b.astype(BF16), dims)


def _layer_norm(pre, g, beta):
    mu = jnp.mean(pre, axis=-1, keepdims=True)
    xc = pre - mu
    var = jnp.mean(xc * xc, axis=-1, keepdims=True)
    return xc * lax.rsqrt(var + LN_EPS) * g + beta


def _layer_norm_bwd(pre, g, dy):
    mu = jnp.mean(pre, axis=-1, keepdims=True)
    xc = pre - mu
    rstd = lax.rsqrt(jnp.mean(xc * xc, axis=-1, keepdims=True) + LN_EPS)
    xhat = xc * rstd
    dxh = dy * g
    m1 = jnp.mean(dxh, axis=-1, keepdims=True)
    m2 = jnp.mean(dxh * xhat, axis=-1, keepdims=True)
    return (rstd * (dxh - m1 - xhat * m2), jnp.sum(dy * xhat, axis=0, keepdims=True),
            jnp.sum(dy, axis=0, keepdims=True))


def _mm(a, b, mode, *, tm, tn, tk, name, epi=None, c=None, scale=1.0, b_dev=False, out_dev=False, out_dtype=F32,
        ln=None, scatter=()):
    if mode == "NN":
        m, kk = a.shape
        n = b.shape[2] * N_DEV if b_dev else b.shape[1]
    elif mode == "NT":
        m, kk = a.shape
        n = b.shape[1] if b_dev else b.shape[0]
    else:
        kk, m = a.shape
        n = b.shape[1]
    assert m % tm == 0 and n % tn == 0 and kk % tk == 0, (name, m, n, kk, tm, tn, tk)
    nk = kk // tk
    dims = {"NN": NN, "NT": NT, "TN": TN}[mode]

    if mode == "TN":
        a_spec = pl.BlockSpec((tk, tm), lambda i, j, k: (k, i))
    else:
        a_spec = pl.BlockSpec((tm, tk), lambda i, j, k: (i, k))
    if mode == "NN":
        if b_dev:
            assert tn == b.shape[2]
            b_spec = pl.BlockSpec((None, tk, tn), lambda i, j, k: (j, k, 0))
        else:
            b_spec = pl.BlockSpec((tk, tn), lambda i, j, k: (k, j))
    elif mode == "NT":
        if b_dev:
            assert tk == b.shape[2]
            b_spec = pl.BlockSpec((None, tn, tk), lambda i, j, k: (k, j, 0))
        else:
            b_spec = pl.BlockSpec((tn, tk), lambda i, j, k: (j, k))
    else:
        b_spec = pl.BlockSpec((tk, tn), lambda i, j, k: (k, j))
    in_specs = [a_spec, b_spec]
    operands = [a, b]
    if c is not None:
        in_specs.append(pl.BlockSpec((tm, tn), lambda i, j, k: (i, j)))
        operands.append(c)
    if epi == "ln":
        assert tn == n and not out_dev
        in_specs += [pl.BlockSpec((1, n), lambda i, j, k: (0, 0))] * 2
        operands += [ln[0].reshape(1, n), ln[1].reshape(1, n)]
    elif epi == "ln_bwd":
        assert tn == n and not out_dev
        in_specs += [pl.BlockSpec((tm, tn), lambda i, j, k: (i, j)), pl.BlockSpec((1, n), lambda i, j, k: (0, 0))]
        operands += [ln[0], ln[1].reshape(1, n)]
    if out_dev:
        assert tn == n // N_DEV
        out_shape = jax.ShapeDtypeStruct((N_DEV, m, tn), out_dtype)
        out_spec = pl.BlockSpec((None, tm, tn), lambda i, j, k: (j, i, 0))
    else:
        out_shape = jax.ShapeDtypeStruct((m, n), out_dtype)
        out_spec = pl.BlockSpec((tm, tn), lambda i, j, k: (i, j))
    if epi == "ln":
        out_shape = [out_shape, out_shape, jax.ShapeDtypeStruct((m, n), BF16)]
        out_spec = [out_spec] * 3
    elif epi == "relu2_copy":
        assert not out_dev
        out_shape = [out_shape, jax.ShapeDtypeStruct((m, n), BF16)]
        out_spec = [out_spec] * 2
    elif epi == "ln_bwd":
        vec_shape, vec_spec = jax.ShapeDtypeStruct((1, n), F32), pl.BlockSpec((1, n), lambda i, j, k: (0, 0))
        out_shape = [out_shape, jax.ShapeDtypeStruct((m, n), BF16), vec_shape, vec_shape]
        out_spec = [out_spec, out_spec, vec_spec, vec_spec]
    n_out = {"ln": 3, "relu2_copy": 2, "ln_bwd": 4}.get(epi, 1)
    ns = len(scatter)
    if ns:
        in_specs += [_ANY] * ns
        operands += list(scatter)
        out_shape = (out_shape if n_out > 1 else [out_shape]) + [jax.ShapeDtypeStruct(s.shape, s.dtype) for s in scatter]
        out_spec = (out_spec if n_out > 1 else [out_spec]) + [_ANY] * ns
    n_in = len(operands)
    grid = (m // tm, n // tn, nk)

    def body(*refs):
        a_ref, b_ref = refs[0], refs[1]
        c_ref = refs[2] if c is not None else None
        o_ref = refs[n_in]
        scratch0 = n_in + n_out + ns
        acc_ref = refs[scratch0] if nk > 1 else None
        if ns:
            s_start, s_finish = _scatter_phases(refs[n_in - ns:n_in], refs[n_in + n_out:scratch0],
                                                *refs[scratch0 + (1 if nk > 1 else 0):])
            at = lambda step: functools.reduce(lambda x, y: x & y, [pl.program_id(ax) == step[ax] for ax in range(3)])
            pl.when(at((0, 0, 0)))(s_start)
        p = _dot(a_ref[...].astype(BF16), b_ref[...].astype(BF16), dims)
        first_rows = pl.program_id(0) == 0

        def finish(acc):
            if epi == "add":
                acc = acc + scale * c_ref[...]
            elif epi == "relu2grad":
                acc = acc * (2.0 * jnp.maximum(c_ref[...], 0.0))
            elif epi == "relu2_copy":
                refs[n_in + 1][...] = jnp.square(jnp.maximum(acc, 0.0)).astype(BF16)
            elif epi == "ln_bwd":
                acc, dg, db = _layer_norm_bwd(refs[3][...], refs[4][...], acc + scale * c_ref[...])
                dg_ref, db_ref = refs[n_in + 2], refs[n_in + 3]

                @pl.when(first_rows)
                def _():
                    dg_ref[...] = jnp.zeros_like(dg_ref)
                    db_ref[...] = jnp.zeros_like(db_ref)

                dg_ref[...] += dg
                db_ref[...] += db
                refs[n_in + 1][...] = acc.astype(BF16)
            elif epi == "ln":
                acc = acc + scale * c_ref[...]
                y = _layer_norm(acc, refs[3][...], refs[4][...])
                refs[n_in + 1][...] = y
                refs[n_in + 2][...] = y.astype(BF16)
            o_ref[...] = acc.astype(out_dtype)

        if nk == 1:
            finish(p)
        else:
            k = pl.program_id(2)

            @pl.when(k == 0)
            def _():
                acc_ref[...] = p

            @pl.when(k > 0)
            def _():
                acc_ref[...] += p

            @pl.when(k == nk - 1)
            def _():
                finish(acc_ref[...])

        if ns:
            pl.when(at(tuple(g - 1 for g in grid)))(s_finish)

    res = pl.pallas_call(
        body, name=name, grid=grid, in_specs=in_specs, out_specs=out_spec, out_shape=out_shape,
        scratch_shapes=([pltpu.VMEM((tm, tn), F32)] if nk > 1 else []) + (_scatter_scratch(ns) if ns else []),
        compiler_params=_cp(("arbitrary",) * 3 if ns or epi == "ln_bwd" else ("parallel", "parallel", "arbitrary"),
                            has_side_effects=bool(ns)),
    )(*operands)
    return res


def _ln_bwd(pre, g, dy, *, name):
    lp, d = pre.shape
    tm = _row_tile(lp, 512)

    def body(pre_ref, g_ref, dy_ref, dpre_ref, dpreb_ref, dg_ref, db_ref):
        dpre, dg, db = _layer_norm_bwd(pre_ref[...], g_ref[...], dy_ref[...])
        dpre_ref[...] = dpre
        dpreb_ref[...] = dpre.astype(BF16)

        @pl.when(pl.program_id(0) == 0)
        def _():
            dg_ref[...] = jnp.zeros_like(dg_ref)
            db_ref[...] = jnp.zeros_like(db_ref)

        dg_ref[...] += dg
        db_ref[...] += db

    row = pl.BlockSpec((tm, d), lambda i: (i, 0))
    vec = pl.BlockSpec((1, d), lambda i: (0, 0))
    return pl.pallas_call(
        body, name=name, grid=(lp // tm,), in_specs=[row, vec, row], out_specs=[row, row, vec, vec],
        out_shape=[jax.ShapeDtypeStruct((lp, d), F32), jax.ShapeDtypeStruct((lp, d), BF16),
                   jax.ShapeDtypeStruct((1, d), F32), jax.ShapeDtypeStruct((1, d), F32)],
        compiler_params=_cp(("arbitrary",)),
    )(pre, g.reshape(1, d), dy)


def _loss_head(y, target, *, name):
    lp, d = y.shape
    seq = target.shape[0]
    tm = SB_BLOCK
    first = (lp - seq) // tm
    assert (lp - seq) % tm == 0 and seq % tm == 0

    def body(y_ref, t_ref, dy_ref, loss_ref):
        i = pl.program_id(0)
        live = i >= first
        diff = jnp.where(live, y_ref[...] - t_ref[...], 0.0)
        dy_ref[...] = diff * (1.0 / d)

        @pl.when(i == 0)
        def _():
            loss_ref[...] = jnp.zeros_like(loss_ref)

        loss_ref[...] += jnp.sum(diff * diff, axis=0, keepdims=True) * (0.5 / d)

    return pl.pallas_call(
        body, name=name, grid=(lp // tm,),
        in_specs=[pl.BlockSpec((tm, d), lambda i: (i, 0)),
                  pl.BlockSpec((tm, d), lambda i: (jnp.maximum(i - first, 0), 0))],
        out_specs=[pl.BlockSpec((tm, d), lambda i: (i, 0)), pl.BlockSpec((1, d), lambda i: (0, 0))],
        out_shape=[jax.ShapeDtypeStruct((lp, d), F32), jax.ShapeDtypeStruct((1, d), F32)],
        compiler_params=_cp(("arbitrary",)),
    )(y, target)


def _gate_fwd(o, zsrc, z_blk0, g, other, *, heads, name):
    lp = o.shape[0]
    tm = _row_tile(lp, 512)
    w = heads * HEAD_W
    assert (z_blk0 * HEAD_W) % w == 0
    has_other = w < D_MODEL

    def body(o_ref, z_ref, g_ref, *rest):
        y_ref = rest[-1]
        gv = g_ref[...]
        for h in range(heads):
            cs = slice(h * HEAD_W, (h + 1) * HEAD_W)
            ov = o_ref[:, cs]
            r = lax.rsqrt(jnp.mean(ov * ov, axis=-1, keepdims=True) + RMS_EPS)
            y_ref[:, cs] = (ov * r * gv * _silu(z_ref[:, cs])).astype(BF16)
        if has_other:
            y_ref[:, w:] = rest[0][...].astype(BF16)

    row = lambda width, blk: pl.BlockSpec((tm, width), lambda i: (i, blk))
    return pl.pallas_call(
        body, name=name, grid=(lp // tm,),
        in_specs=[row(w, 0), row(w, z_blk0 * HEAD_W // w), pl.BlockSpec((1, HEAD_W), lambda i: (0, 0))]
        + ([row(D_MODEL - w, 0)] if has_other else []),
        out_specs=row(D_MODEL, 0), out_shape=jax.ShapeDtypeStruct((lp, D_MODEL), BF16),
        compiler_params=_cp(("parallel",)),
    )(o, zsrc, g.reshape(1, HEAD_W), *([other] if has_other else []))


def _gate_bwd(o, zsrc, z_blk0, g, dy, *, heads, name):
    lp = o.shape[0]
    tm = _row_tile(lp, 512)

    w = heads * HEAD_W
    assert (z_blk0 * HEAD_W) % w == 0

    def body(o_ref, z_ref, g_ref, dy_ref, do_ref, dz_ref, dg_ref):
        @pl.when(pl.program_id(0) == 0)
        def _():
            dg_ref[...] = jnp.zeros_like(dg_ref)

        gv = g_ref[...]
        dg = jnp.zeros((1, HEAD_W), F32)
        for h in range(heads):
            cs = slice(h * HEAD_W, (h + 1) * HEAD_W)
            ov, zv, dyv = o_ref[:, cs], z_ref[:, cs], dy_ref[:, cs]
            r = lax.rsqrt(jnp.mean(ov * ov, axis=-1, keepdims=True) + RMS_EPS)
            nrm = ov * r
            s = _silu(zv)
            dn = dyv * gv * s
            do_ref[:, cs] = r * (dn - nrm * jnp.mean(dn * nrm, axis=-1, keepdims=True))
            dz_ref[:, cs] = dyv * nrm * gv * _silu_grad(zv)
            dg = dg + jnp.sum(dyv * nrm * s, axis=0, keepdims=True)
        dg_ref[...] += dg

    row = lambda blk: pl.BlockSpec((tm, w), lambda i: (i, blk))
    vec = pl.BlockSpec((1, HEAD_W), lambda i: (0, 0))
    return pl.pallas_call(
        body, name=name, grid=(lp // tm,),
        in_specs=[row(0), row(z_blk0 * HEAD_W // w), vec, row(0)], out_specs=[row(0), row(0), vec],
        out_shape=[jax.ShapeDtypeStruct((lp, w), F32), jax.ShapeDtypeStruct((lp, w), F32),
                   jax.ShapeDtypeStruct((1, HEAD_W), F32)],
        compiler_params=_cp(("arbitrary",)),
    )(o, zsrc, g.reshape(1, HEAD_W), dy)


def _conv_taps(x, w):
    acc = w[CONV_K - 1:CONV_K, :] * x
    for k in range(CONV_K - 1):
        acc = acc + w[k:k + 1, :] * pltpu.roll(x, CONV_K - 1 - k, 0)
    return acc


def _gdn_pre_fwd(p0, conv_w, pad, *, name):
    lp = p0.shape[0]
    nq = GDN_HEADS
    qscale = HEAD_W ** -0.5

    def body(x_ref, w_ref, y_ref):
        j = pl.program_id(0)
        c = _conv_taps(x_ref[...], w_ref[...])
        s = _silu(c)
        r = lax.rsqrt(jnp.sum(s * s, axis=-1, keepdims=True) + L2_EPS)
        mult = jnp.where(j < nq, r * qscale, jnp.where(j < 2 * nq, r, 1.0))
        rows = lax.broadcasted_iota(jnp.int32, (lp, 1), 0)
        y_ref[...] = jnp.where(rows >= pad, s * mult, 0.0)

    return pl.pallas_call(
        body, name=name, grid=(3 * nq,),
        in_specs=[pl.BlockSpec((lp, HEAD_W), lambda j: (0, j)), pl.BlockSpec((CONV_K, HEAD_W), lambda j: (0, j))],
        out_specs=pl.BlockSpec((lp, HEAD_W), lambda j: (0, j)),
        out_shape=jax.ShapeDtypeStruct((lp, 3 * nq * HEAD_W), F32), compiler_params=_cp(("parallel",)),
    )(p0, conv_w)


def _gdn_pre_bwd(p0, conv_w, dqkv, pad, *, name):
    lp = p0.shape[0]
    nq = GDN_HEADS
    qscale = HEAD_W ** -0.5

    def body(x_ref, w_ref, dy_ref, dx_ref, dw_ref):
        j = pl.program_id(0)
        x, w = x_ref[...], w_ref[...]
        c = _conv_taps(x, w)
        s = _silu(c)
        r = lax.rsqrt(jnp.sum(s * s, axis=-1, keepdims=True) + L2_EPS)
        rows = lax.broadcasted_iota(jnp.int32, (lp, 1), 0)
        dy = jnp.where(rows >= pad, dy_ref[...], 0.0)
        nrm = s * r
        dn = dy * jnp.where(j < nq, qscale, 1.0)
        ds_norm = r * (dn - nrm * jnp.sum(nrm * dn, axis=-1, keepdims=True))
        ds = jnp.where(j < 2 * nq, ds_norm, dy)
        dc = ds * _silu_grad(c)
        dx = w[CONV_K - 1:CONV_K, :] * dc
        dws = [None] * CONV_K
        dws[CONV_K - 1] = jnp.sum(dc * x, axis=0, keepdims=True)
        for k in range(CONV_K - 1):
            sh = CONV_K - 1 - k
            dx = dx + w[k:k + 1, :] * pltpu.roll(dc, lp - sh, 0)
            dws[k] = jnp.sum(dc * pltpu.roll(x, sh, 0), axis=0, keepdims=True)
        dx_ref[...] = dx
        dw_ref[...] = jnp.concatenate(dws, axis=0)

    blk = pl.BlockSpec((lp, HEAD_W), lambda j: (0, j))
    wblk = pl.BlockSpec((CONV_K, HEAD_W), lambda j: (0, j))
    return pl.pallas_call(
        body, name=name, grid=(3 * nq,), in_specs=[blk, wblk, blk], out_specs=[blk, wblk],
        out_shape=[jax.ShapeDtypeStruct((lp, 3 * nq * HEAD_W), F32),
                   jax.ShapeDtypeStruct((CONV_K, 3 * nq * HEAD_W), F32)],
        compiler_params=_cp(("parallel",)),
    )(p0, conv_w, dqkv)


@jax.custom_vjp
def _inv_unit_lower(m):
    c = m.shape[0]
    eye = (lax.broadcasted_iota(jnp.int32, (c, c), 0) == lax.broadcasted_iota(jnp.int32, (c, c), 1)).astype(F32)
    x = eye - m
    p = m
    n = 2
    while n < CHUNK:
        p = _bdot(p, p, NN)
        x = x + _bdot(x, p, NN)
        n *= 2
    return x


def _inv_fwd(m):
    t = _inv_unit_lower(m)
    return t, t


def _inv_bwd(t, g):
    return (-_bdot(_bdot(t, g, TN), t, NT),)


_inv_unit_lower.defvjp(_inv_fwd, _inv_bwd)


GDN_STEP = 3


def _heads_to_rows(x, nh):
    return jnp.concatenate([x[:, h * HEAD_W:(h + 1) * HEAD_W] for h in range(nh)], axis=0)


def _rows_to_heads(x, nh):
    c = x.shape[0] // nh
    return jnp.concatenate([x[h * c:(h + 1) * c] for h in range(nh)], axis=1)


def _gdn_chunk(q, k, v, ba, alog, dtb, states, valid):
    nh = GDN_HEADS
    c = q.shape[0]
    r = nh * c
    lane = lax.broadcasted_iota(jnp.int32, (1, HEAD_W), 1)
    pick = lambda x, l: jnp.sum(jnp.where(lane == l, x, 0.0), axis=-1, keepdims=True)
    beta = jnp.concatenate([jnp.where(valid, _sigmoid(pick(ba, h)), 0.0) for h in range(nh)], axis=0)
    g = jnp.concatenate(
        [jnp.where(valid, -jnp.exp(pick(alog, h)) * _softplus(pick(ba, nh + h) + pick(dtb, h)), 0.0) for h in range(nh)],
        axis=0)
    qs, ks, vs = _heads_to_rows(q, nh), _heads_to_rows(k, nh), _heads_to_rows(v, nh)
    rr = lax.broadcasted_iota(jnp.int32, (r, r), 0)
    cc = lax.broadcasted_iota(jnp.int32, (r, r), 1)
    same = (rr // c) == (cc // c)
    causal, strict = same & (cc <= rr), same & (cc < rr)
    lower = jnp.where(causal, 1.0, 0.0).astype(BF16)
    upper = jnp.where(same & (cc >= rr), 1.0, 0.0).astype(BF16)
    gcb = _mask_mm(lower, upper, g * jnp.ones((1, HEAD_W), F32))
    gc_col = jnp.concatenate([gcb] * (r // HEAD_W), axis=1)
    decay = jnp.where(causal, jnp.exp(jnp.minimum(gc_col - gc_col.T, 0.0)), 0.0)
    egc = jnp.exp(gcb)
    kb = ks * beta
    m = jnp.where(strict, _dot3(kb, ks, NT) * decay, 0.0)
    t = _inv_unit_lower(m)
    u = _bdot(t, vs * beta, NN)
    w = _bdot(t, kb * egc, NN)
    a = _bdot(qs, ks, NT) * decay
    rows = lambda x, h: x[h * c:(h + 1) * c]
    qe = qs * egc
    v_new = u - jnp.concatenate([_bdot(rows(w, h), states[h], NN) for h in range(nh)], axis=0)
    o = jnp.concatenate([_bdot(rows(qe, h), states[h], NN) for h in range(nh)], axis=0) + _bdot(a, v_new, NN)
    new_states = []
    for h in range(nh):
        gl = gcb[(h + 1) * c - 1:(h + 1) * c, :]
        k_dec = rows(ks, h) * jnp.exp(gl - rows(gcb, h))
        new_states.append(states[h] * jnp.exp(gl) + _bdot(k_dec, rows(v_new, h), TN))
    return _rows_to_heads(o, nh), new_states


def _gdn_fwd(qkv, p0, alog_v, dtb_v, pad, *, name, gather=None):
    lp = qkv.shape[0]
    n = lp // CHUNK
    nh = GDN_HEADS
    assert n % GDN_STEP == 0
    steps, rows = n // GDN_STEP, GDN_STEP * CHUNK
    g_srcs, g_dtypes = gather if gather is not None else ([], [])
    ng_arr = len(g_srcs)

    def body(q_ref, k_ref, v_ref, ba_ref, al_ref, dt_ref, *rest):
        g_ins, (o_ref, st_ref) = rest[:ng_arr], rest[ng_arr:ng_arr + 2]
        g_outs, s_ref, g_scratch = rest[ng_arr + 2:2 * ng_arr + 2], rest[2 * ng_arr + 2], rest[2 * ng_arr + 3:]
        i = pl.program_id(0)
        if ng_arr:
            g_start, g_forward, g_finish = _gather_phases(g_ins, g_outs, g_scratch[:ng_arr], *g_scratch[ng_arr:],
                                                          g_dtypes)
            pl.when(i == 0)(g_start)
            pl.when(i == (3 * steps) // 4)(g_forward)

        @pl.when(i == 0)
        def _():
            s_ref[...] = jnp.zeros_like(s_ref)

        s = s_ref[...]
        s = [s[h] for h in range(nh)]
        q, k, v, ba, al, dt = q_ref[...], k_ref[...], v_ref[...], ba_ref[...], al_ref[...], dt_ref[...]
        outs = []
        for c in range(GDN_STEP):
            sl = slice(c * CHUNK, (c + 1) * CHUNK)
            valid = (i * rows + c * CHUNK + lax.broadcasted_iota(jnp.int32, (CHUNK, 1), 0)) >= pad
            for h in range(nh):
                st_ref[c, h] = s[h]
            o, s = _gdn_chunk(q[sl], k[sl], v[sl], ba[sl], al, dt, s, valid)
            outs.append(o)
        o_ref[...] = jnp.concatenate(outs, axis=0)
        for h in range(nh):
            s_ref[h] = s[h]
        if ng_arr:
            pl.when(i == steps - 1)(g_finish)

    w = nh * HEAD_W
    vec = pl.BlockSpec((1, HEAD_W), lambda i: (0, 0))
    return pl.pallas_call(
        body, name=name, grid=(steps,),
        in_specs=[pl.BlockSpec((rows, w), lambda i: (i, 0)), pl.BlockSpec((rows, w), lambda i: (i, 1)),
                  pl.BlockSpec((rows, w), lambda i: (i, 2)), pl.BlockSpec((rows, HEAD_W), lambda i: (i, AB_BA // HEAD_W)),
                  vec, vec] + [pl.BlockSpec(memory_space=pltpu.VMEM)] * ng_arr,
        out_specs=[pl.BlockSpec((rows, w), lambda i: (i, 0)),
                   pl.BlockSpec((GDN_STEP, nh, HEAD_W, HEAD_W), lambda i: (i, 0, 0, 0))] + [_ANY] * ng_arr,
        out_shape=[jax.ShapeDtypeStruct((lp, w), F32), jax.ShapeDtypeStruct((n, nh, HEAD_W, HEAD_W), F32)]
        + _gather_out_shapes(g_srcs, g_dtypes),
        scratch_shapes=[pltpu.VMEM((nh, HEAD_W, HEAD_W), F32)] + (_gather_scratch(g_srcs, g_dtypes) if ng_arr else []),
        compiler_params=_cp(("arbitrary",), has_side_effects=bool(ng_arr)),
    )(qkv, qkv, qkv, p0, alog_v, dtb_v, *g_srcs)


def _gdn_bwd(qkv, p0, alog_v, dtb_v, states, do, pad, *, name, scatter=()):
    lp = qkv.shape[0]
    n = lp // CHUNK
    nh = GDN_HEADS
    assert n % GDN_STEP == 0
    steps, rows = n // GDN_STEP, GDN_STEP * CHUNK
    ns = len(scatter)

    def body(q_ref, k_ref, v_ref, ba_ref, al_ref, dt_ref, st_ref, do_ref, *rest):
        s_ins, (dq_ref, dk_ref, dv_ref, dba_ref, dal_ref, ddt_ref) = rest[:ns], rest[ns:ns + 6]
        s_outs, ds_ref, s_sems = rest[ns + 6:2 * ns + 6], rest[2 * ns + 6], rest[2 * ns + 7:]
        step = pl.program_id(0)
        i = steps - 1 - step
        if ns:
            s_start, s_finish = _scatter_phases(s_ins, s_outs, *s_sems)
            pl.when(step == 0)(s_start)

        @pl.when(step == 0)
        def _():
            ds_ref[...] = jnp.zeros_like(ds_ref)
            dal_ref[...] = jnp.zeros_like(dal_ref)
            ddt_ref[...] = jnp.zeros_like(ddt_ref)

        q, k, v, ba, al, dt = q_ref[...], k_ref[...], v_ref[...], ba_ref[...], al_ref[...], dt_ref[...]
        st, do, dst = st_ref[...], do_ref[...], ds_ref[...]
        vjps = []
        for c in range(GDN_STEP):
            sl = slice(c * CHUNK, (c + 1) * CHUNK)
            valid = (i * rows + c * CHUNK + lax.broadcasted_iota(jnp.int32, (CHUNK, 1), 0)) >= pad
            fn = functools.partial(_gdn_chunk, valid=valid)
            vjps.append(jax.vjp(fn, q[sl], k[sl], v[sl], ba[sl], al, dt, [st[c, h] for h in range(nh)])[1])
        ds = [dst[h] for h in range(nh)]
        grads = [None] * GDN_STEP
        for c in reversed(range(GDN_STEP)):
            grads[c] = vjps[c]((do[c * CHUNK:(c + 1) * CHUNK], ds))
            ds = grads[c][6]
        for j, ref in enumerate((dq_ref, dk_ref, dv_ref, dba_ref)):
            ref[...] = jnp.concatenate([gr[j] for gr in grads], axis=0)
        dal_ref[...] += sum(gr[4] for gr in grads)
        ddt_ref[...] += sum(gr[5] for gr in grads)
        for h in range(nh):
            ds_ref[h] = ds[h]
        if ns:
            pl.when(step == steps - 1)(s_finish)

    w = nh * HEAD_W
    rev = lambda c: (lambda s: (steps - 1 - s, c))
    vec = pl.BlockSpec((1, HEAD_W), lambda s: (0, 0))
    return pl.pallas_call(
        body, name=name, grid=(steps,),
        in_specs=[pl.BlockSpec((rows, w), rev(0)), pl.BlockSpec((rows, w), rev(1)), pl.BlockSpec((rows, w), rev(2)),
                  pl.BlockSpec((rows, HEAD_W), rev(AB_BA // HEAD_W)), vec, vec,
                  pl.BlockSpec((GDN_STEP, nh, HEAD_W, HEAD_W), lambda s: (steps - 1 - s, 0, 0, 0)),
                  pl.BlockSpec((rows, w), rev(0))] + [_ANY] * ns,
        out_specs=[pl.BlockSpec((rows, w), rev(0)), pl.BlockSpec((rows, w), rev(0)), pl.BlockSpec((rows, w), rev(0)),
                   pl.BlockSpec((rows, HEAD_W), rev(0)), vec, vec] + [_ANY] * ns,
        out_shape=[jax.ShapeDtypeStruct((lp, w), F32)] * 3 + [jax.ShapeDtypeStruct((lp, HEAD_W), F32)]
        + [jax.ShapeDtypeStruct((1, HEAD_W), F32)] * 2 + [jax.ShapeDtypeStruct(s.shape, s.dtype) for s in scatter],
        scratch_shapes=[pltpu.VMEM((nh, HEAD_W, HEAD_W), F32)] + (_scatter_scratch(ns) if ns else []),
        compiler_params=_cp(("arbitrary",), has_side_effects=bool(ns)),
    )(qkv, qkv, qkv, p0, alog_v, dtb_v, states, do, *scatter)


HG_LEVELS = (32, 16, 8, 4, 2, 1)
HG_GROUP = 4
HG_STEP = 3


def _hg_masks():
    import numpy as np
    c = CHUNK
    t = np.arange(c)[:, None]
    j = np.arange(c)[None, :]
    sums = (j <= t).astype(np.float32)
    pairs = [j == t]
    for m in HG_LEVELS:
        p = (t // (2 * m)) * (2 * m)
        r = p + m
        pairs.append((t >= r) & (j < r) & (j >= p))
    pairs = np.concatenate([np.kron(np.eye(HG_GROUP), p) for p in pairs], axis=0).astype(np.float32)
    return jnp.asarray(sums, BF16), jnp.asarray(sums.T, BF16), jnp.asarray(pairs, F32)


def _hg_level_row(b, m):
    c, w = b.shape
    if m >= 8:
        return jnp.concatenate([jnp.broadcast_to(b[p + m:p + m + 1], (2 * m, w)) for p in range(0, c, 2 * m)], axis=0)
    tiles = b.reshape(c // 8, 8, w)
    sub = lax.broadcasted_iota(jnp.int32, (1, 8, 1), 1)
    out = None
    for r0 in range(m, 8, 2 * m):
        cand = jnp.broadcast_to(tiles[:, r0:r0 + 1, :], tiles.shape)
        out = cand if out is None else jnp.where(sub >= r0 - m, cand, out)
    return out.reshape(c, w)


def _split3(x):
    hi = x.astype(BF16)
    r1 = x - hi.astype(F32)
    mid = r1.astype(BF16)
    return hi, mid, (r1 - mid.astype(F32)).astype(BF16)


def _dot3_raw(a, b, dims):
    ah, am, _ = _split3(a)
    bh, bm, _ = _split3(b)
    return _dot(ah, bh, dims) + (_dot(ah, bm, dims) + _dot(am, bh, dims))


@functools.partial(jax.custom_vjp, nondiff_argnums=(2,))
def _dot3(a, b, dims):
    return _dot3_raw(a, b, dims)


def _dot3_fwd(a, b, dims):
    return _dot3_raw(a, b, dims), (a, b)


def _dot3_bwd(dims, res, g):
    a, b = res
    if dims == NN:
        return _dot3_raw(g, b, NT), _dot3_raw(a, g, TN)
    return _dot3_raw(g, b, NN), _dot3_raw(g, a, TN)


_dot3.defvjp(_dot3_fwd, _dot3_bwd)


def _mask_mm_raw(m, x):
    return sum(_dot(m, part, NN) for part in _split3(x))


@jax.custom_vjp
def _mask_mm(m, mt, x):
    return _mask_mm_raw(m, x)


def _mask_mm_fwd(m, mt, x):
    return _mask_mm_raw(m, x), (m, mt)


def _mask_mm_bwd(res, g):
    m, mt = res
    return jnp.zeros_like(m), jnp.zeros_like(mt), _mask_mm_raw(mt, g)


_mask_mm.defvjp(_mask_mm_fwd, _mask_mm_bwd)


def _hg_chunk(qr, fr, ir, lb, states, valid, sums, sums_t, pairs):
    nh = HG_GROUP
    c = qr.shape[0]
    r = nh * c
    fg = lb + (1.0 - lb) * _sigmoid(fr)
    logf = jnp.where(valid, jnp.log(fg), 0.0)
    k = jnp.where(valid, 1.0 - fg, 0.0)
    qs = jnp.where(valid, _silu(qr), 0.0)
    v = jnp.where(valid, ir, 0.0)
    b = _mask_mm(sums, sums_t, logf)
    mask = lambda n: pairs[n * r:(n + 1) * r]
    stack = lambda x: _heads_to_rows(x, nh)
    a = mask(0) * _bdot(stack(qs), stack(k), NT)
    for lvl, m in enumerate(HG_LEVELS):
        d = b - _hg_level_row(b, m)
        a = a + mask(1 + lvl) * _bdot(stack(qs * jnp.exp(jnp.minimum(d, 0.0))),
                                      stack(k * jnp.exp(jnp.minimum(-d, 0.0))), NT)
    av = _bdot(a, stack(v), NN)
    eb = jnp.exp(b)
    qe, kd = qs * eb, k * jnp.exp(b[c - 1:c] - b)
    outs, new_states = [], []
    for h in range(nh):
        cs = slice(h * HEAD_W, (h + 1) * HEAD_W)
        outs.append(_bdot(qe[:, cs], states[h], NT) + av[h * c:(h + 1) * c])
        new_states.append(states[h] * eb[c - 1:c, cs] + _bdot(v[:, cs], kd[:, cs], TN))
    return jnp.concatenate(outs, axis=1), new_states


def _hg_fwd(p1, lb, pad, *, name, gather=None):
    lp = p1.shape[0]
    n = lp // CHUNK
    nh = HG_HEADS
    g_srcs, g_dtypes = gather if gather is not None else ([], [])
    ng_arr = len(g_srcs)

    def body(q_ref, f_ref, i_ref, lb_ref, sums_ref, sums_t_ref, pairs_ref, *rest):
        g_ins, (o_ref, st_ref) = rest[:ng_arr], rest[ng_arr:ng_arr + 2]
        g_outs, s_ref, g_scratch = rest[ng_arr + 2:2 * ng_arr + 2], rest[2 * ng_arr + 2], rest[2 * ng_arr + 3:]
        i = pl.program_id(1)
        if ng_arr:
            g_start, g_forward, g_finish = _gather_phases(g_ins, g_outs, g_scratch[:ng_arr], *g_scratch[ng_arr:],
                                                          g_dtypes)
            last_group = pl.program_id(0) == ngrp - 1
            pl.when((pl.program_id(0) == 0) & (i == 0))(g_start)
            pl.when(last_group & (i == 0))(g_forward)

        @pl.when(i == 0)
        def _():
            s_ref[...] = jnp.zeros_like(s_ref)

        s = s_ref[...]
        s = [s[h] for h in range(grp)]
        q, f, iv, lbv = q_ref[...], f_ref[...], i_ref[...], lb_ref[...]
        masks_v = (sums_ref[...], sums_t_ref[...], pairs_ref[...])
        outs = []
        for c in range(HG_STEP):
            sl = slice(c * CHUNK, (c + 1) * CHUNK)
            valid = (i * rows + c * CHUNK + lax.broadcasted_iota(jnp.int32, (CHUNK, 1), 0)) >= pad
            for h in range(grp):
                st_ref[h, c] = s[h]
            o, s = _hg_chunk(q[sl], f[sl], iv[sl], lbv, s, valid, *masks_v)
            outs.append(o)
        o_ref[...] = jnp.concatenate(outs, axis=0)
        for h in range(grp):
            s_ref[h] = s[h]
        if ng_arr:
            pl.when(last_group & (i == steps - 1))(g_finish)

    masks = _hg_masks()
    grp, ngrp, gw = HG_GROUP, nh // HG_GROUP, HG_GROUP * HEAD_W
    assert n % HG_STEP == 0
    steps, rows = n // HG_STEP, HG_STEP * CHUNK
    blk = lambda off: pl.BlockSpec((rows, gw), lambda h, i: (i, off + h))
    const = lambda a: pl.BlockSpec(a.shape, lambda h, i: (0, 0))
    return pl.pallas_call(
        body, name=name, grid=(ngrp, steps),
        in_specs=[blk(0), blk(ngrp), blk(2 * ngrp), pl.BlockSpec((1, gw), lambda h, i: (0, h))]
        + [const(a) for a in masks] + [pl.BlockSpec(memory_space=pltpu.VMEM)] * ng_arr,
        out_specs=[blk(0), pl.BlockSpec((grp, HG_STEP, HEAD_W, HEAD_W), lambda h, i: (h, i, 0, 0))] + [_ANY] * ng_arr,
        out_shape=[jax.ShapeDtypeStruct((lp, nh * HEAD_W), F32), jax.ShapeDtypeStruct((nh, n, HEAD_W, HEAD_W), F32)]
        + _gather_out_shapes(g_srcs, g_dtypes),
        scratch_shapes=[pltpu.VMEM((grp, HEAD_W, HEAD_W), F32)] + (_gather_scratch(g_srcs, g_dtypes) if ng_arr else []),
        compiler_params=_cp(("arbitrary", "arbitrary"), has_side_effects=bool(ng_arr)),
    )(p1, p1, p1, lb, *masks, *g_srcs)


def _hg_bwd(p1, lb, states, do, pad, *, name, scatter=()):
    lp = p1.shape[0]
    n = lp // CHUNK
    nh = HG_HEADS
    ns = len(scatter)

    def body(q_ref, f_ref, i_ref, lb_ref, st_ref, do_ref, sums_ref, sums_t_ref, pairs_ref, *rest):
        s_ins, (dq_ref, df_ref, di_ref, dlb_ref) = rest[:ns], rest[ns:ns + 4]
        s_outs, ds_ref, s_sems = rest[ns + 4:2 * ns + 4], rest[2 * ns + 4], rest[2 * ns + 5:]
        step = pl.program_id(1)
        i = steps - 1 - step
        if ns:
            s_start, s_finish = _scatter_phases(s_ins, s_outs, *s_sems)
            pl.when((pl.program_id(0) == 0) & (step == 0))(s_start)

        @pl.when(step == 0)
        def _():
            ds_ref[...] = jnp.zeros_like(ds_ref)
            dlb_ref[...] = jnp.zeros_like(dlb_ref)

        q, f, iv, lbv, st, do, dst = q_ref[...], f_ref[...], i_ref[...], lb_ref[...], st_ref[...], do_ref[...], ds_ref[...]
        masks_v = dict(sums=sums_ref[...], sums_t=sums_t_ref[...], pairs=pairs_ref[...])
        vjps = []
        for c in range(HG_STEP):
            sl = slice(c * CHUNK, (c + 1) * CHUNK)
            valid = (i * rows + c * CHUNK + lax.broadcasted_iota(jnp.int32, (CHUNK, 1), 0)) >= pad
            fn = functools.partial(_hg_chunk, valid=valid, **masks_v)
            vjps.append(jax.vjp(fn, q[sl], f[sl], iv[sl], lbv, [st[h, c] for h in range(grp)])[1])
        ds = [dst[h] for h in range(grp)]
        grads = [None] * HG_STEP
        for c in reversed(range(HG_STEP)):
            grads[c] = vjps[c]((do[c * CHUNK:(c + 1) * CHUNK], ds))
            ds = grads[c][4]
        for j, ref in enumerate((dq_ref, df_ref, di_ref)):
            ref[...] = jnp.concatenate([gr[j] for gr in grads], axis=0)
        dlb_ref[...] += sum(gr[3] for gr in grads)
        for h in range(grp):
            ds_ref[h] = ds[h]
        if ns:
            pl.when((pl.program_id(0) == ngrp - 1) & (step == steps - 1))(s_finish)

    masks = _hg_masks()
    grp, ngrp, gw = HG_GROUP, nh // HG_GROUP, HG_GROUP * HEAD_W
    assert n % HG_STEP == 0
    steps, rows = n // HG_STEP, HG_STEP * CHUNK
    blk = lambda off: pl.BlockSpec((rows, gw), lambda h, s: (steps - 1 - s, off + h))
    const = lambda a: pl.BlockSpec(a.shape, lambda h, s: (0, 0))
    w = nh * HEAD_W
    return pl.pallas_call(
        body, name=name, grid=(ngrp, steps),
        in_specs=[blk(0), blk(ngrp), blk(2 * ngrp), pl.BlockSpec((1, gw), lambda h, s: (0, h)),
                  pl.BlockSpec((grp, HG_STEP, HEAD_W, HEAD_W), lambda h, s: (h, steps - 1 - s, 0, 0)), blk(0)]
        + [const(a) for a in masks] + [_ANY] * ns,
        out_specs=[blk(0), blk(0), blk(0), pl.BlockSpec((1, gw), lambda h, s: (0, h))] + [_ANY] * ns,
        out_shape=[jax.ShapeDtypeStruct((lp, w), F32)] * 3 + [jax.ShapeDtypeStruct((1, w), F32)]
        + [jax.ShapeDtypeStruct(s.shape, s.dtype) for s in scatter],
        scratch_shapes=[pltpu.VMEM((grp, HEAD_W, HEAD_W), F32)] + (_scatter_scratch(ns) if ns else []),
        compiler_params=_cp(("arbitrary", "arbitrary"), has_side_effects=bool(ns)),
    )(p1, p1, p1, lb, states, do, *masks, *scatter)


SB_GROUP = 4
SB_FAR = -110.0


def _sb_cat(kind, first_key=0):
    r = lax.broadcasted_iota(jnp.int32, (SB_BLOCK, 2 * SB_BLOCK), 0)
    c = lax.broadcasted_iota(jnp.int32, (SB_BLOCK, 2 * SB_BLOCK), 1)
    tri = {"after": c < r, "incl": r <= c, "before": r < c}[kind]
    m = ((c >= SB_BLOCK) | tri) & (r >= first_key)
    return jnp.where(m, 1.0, 0.0).astype(BF16)


def _sb_cumsum(x, cat):
    return _dot(x.astype(BF16), cat, NN)


def _sb_logsig(z):
    e = jnp.exp(-jnp.abs(z))
    lse = jnp.where(e < 1e-4, e, jnp.log(1.0 + e))
    lsz = jnp.minimum(z, 0.0) - lse
    return lsz, lsz - z, e


def _sb_stack(x, scale=None):
    lane = lax.broadcasted_iota(jnp.int32, (1, HEAD_W), 1)
    if scale is not None:
        x = x * scale
    return jnp.concatenate([jnp.where(lane < SB_DH, x, 0.0), jnp.where(lane >= SB_DH, x, 0.0)], axis=0).astype(BF16)


def _sb_unstack(x):
    lane = lax.broadcasted_iota(jnp.int32, (1, HEAD_W), 1)
    return jnp.where(lane < SB_DH, x[:SB_BLOCK], x[SB_BLOCK:])


def _sb_fwd(p0, pad, *, name, gather=None):
    lp = p0.shape[0]
    nb = lp // SB_BLOCK
    npair = SB_HEADS // 2
    blk0 = AB_SB // HEAD_W
    scale = SB_DH ** -0.5
    gw = SB_GROUP * SB_BLOCK
    assert pad < SB_BLOCK
    g_srcs, g_dtypes = gather if gather is not None else ([], [])
    ng_arr = len(g_srcs)

    def body(q_ref, k_ref, v_ref, *rest):
        g_ins, (o_ref, tot_ref, nproc_ref) = rest[:ng_arr], rest[ng_arr:ng_arr + 3]
        g_outs, g_scratch = rest[ng_arr + 3:2 * ng_arr + 3], rest[2 * ng_arr + 3:]
        first_step = (pl.program_id(0) == 0) & (pl.program_id(1) == 0)
        last_pair = pl.program_id(0) == npair - 1
        if ng_arr:
            g_start, g_forward, g_finish = _gather_phases(g_ins, g_outs, g_scratch[:ng_arr], *g_scratch[ng_arr:],
                                                          g_dtypes)
            pl.when(first_step)(g_start)
            pl.when(last_pair & (pl.program_id(1) == 0))(g_forward)
        i = pl.program_id(1)
        qs = _sb_stack(q_ref[...], scale)
        qpos = i * SB_BLOCK + lax.broadcasted_iota(jnp.int32, (SB_BLOCK, 1), 0)
        qpos = jnp.concatenate([qpos, qpos], axis=0)
        cat = _sb_cat("after")
        cat0 = _sb_cat("after", pad)
        ng = i // SB_GROUP

        def group(off, nblk, first_cat, allowed, carry):
            acc, run = carry
            kg = k_ref[pl.ds(off, nblk * SB_BLOCK), :].astype(BF16)
            vg = v_ref[pl.ds(off, nblk * SB_BLOCK), :].astype(BF16)
            lsz, l1m, _ = _sb_logsig(_dot(qs, kg, NT))
            if allowed is not None:
                l1m = jnp.where(allowed, l1m, 0.0)
            args = [None] * nblk
            for g in reversed(range(nblk)):
                sl = slice(g * SB_BLOCK, (g + 1) * SB_BLOCK)
                al = _sb_cumsum(l1m[:, sl], first_cat if g == 0 else cat)
                args[g] = lsz[:, sl] + al[:, :SB_BLOCK] + run
                run = run + al[:, SB_BLOCK:]
            wgt = jnp.exp(jnp.concatenate(args, axis=1))
            if allowed is not None:
                wgt = jnp.where(allowed, wgt, 0.0)
            return acc + _dot(wgt.astype(BF16), vg, NN), run

        def below(t, carry):
            gi = ng - 1 - t
            return group(pl.multiple_of(gi * gw, gw), SB_GROUP, jnp.where(gi == 0, cat0, cat), None, carry)

        top = ng * gw

        def top_group(nblk, carry):
            off = pl.multiple_of(jnp.minimum(top, lp - nblk * SB_BLOCK), SB_BLOCK)
            kpos = off + lax.broadcasted_iota(jnp.int32, (1, nblk * SB_BLOCK), 1)
            return group(off, nblk, cat, (kpos < qpos) & (kpos >= pad) & (kpos >= top), carry)

        zero = (jnp.zeros((2 * SB_BLOCK, HEAD_W), F32), jnp.zeros((2 * SB_BLOCK, HEAD_W), F32))
        carry = lax.cond(i - ng * SB_GROUP < SB_GROUP // 2, functools.partial(top_group, SB_GROUP // 2),
                         functools.partial(top_group, SB_GROUP), zero)
        used, acc, run = lax.while_loop(lambda s: (s[0] < ng) & (jnp.max(s[2]) > SB_FAR),
                                        lambda s: (s[0] + 1, *below(s[0], (s[1], s[2]))), (jnp.int32(0), *carry))
        o_ref[...] = _sb_unstack(acc)
        tot_ref[...] = _sb_unstack(run)
        nproc_ref[pl.program_id(0), i] = used.astype(F32)
        if ng_arr:
            pl.when(last_pair & (pl.program_id(1) == nb - 1))(g_finish)

    full = lambda c0: pl.BlockSpec((lp, HEAD_W), lambda p, i: (0, c0 + p))
    out = pl.BlockSpec((SB_BLOCK, HEAD_W), lambda p, i: (i, p))
    return pl.pallas_call(
        body, name=name, grid=(npair, nb),
        in_specs=[pl.BlockSpec((SB_BLOCK, HEAD_W), lambda p, i: (i, blk0 + p)), full(blk0 + npair), full(blk0 + 2 * npair)]
        + [pl.BlockSpec(memory_space=pltpu.VMEM)] * ng_arr,
        out_specs=[out, out, pl.BlockSpec(memory_space=pltpu.SMEM)] + [_ANY] * ng_arr,
        out_shape=[jax.ShapeDtypeStruct((lp, npair * HEAD_W), F32)] * 2 + [jax.ShapeDtypeStruct((npair, nb), F32)]
        + _gather_out_shapes(g_srcs, g_dtypes),
        scratch_shapes=_gather_scratch(g_srcs, g_dtypes) if ng_arr else [],
        compiler_params=_cp(("arbitrary", "arbitrary"), has_side_effects=bool(ng_arr)),
    )(p0, p0, p0, *g_srcs)


def _sb_bwd(p0, tot, nproc, dsrc, d_blk0, pad, *, name, scatter=()):
    lp = p0.shape[0]
    nb = lp // SB_BLOCK
    npair = SB_HEADS // 2
    blk0 = AB_SB // HEAD_W
    scale = SB_DH ** -0.5
    gw = SB_GROUP * SB_BLOCK
    assert pad < SB_BLOCK
    ns = len(scatter)

    def body(q_ref, k_ref, v_ref, tot_ref, nproc_ref, do_ref, *rest):
        s_ins, (dq_ref, dkt_ref, dvt_ref) = rest[:ns], rest[ns:ns + 3]
        s_outs, s_sems = rest[ns + 3:2 * ns + 3], rest[2 * ns + 3:]
        if ns:
            s_start, s_finish = _scatter_phases(s_ins, s_outs, *s_sems)
            pl.when((pl.program_id(0) == 0) & (pl.program_id(1) == 0))(s_start)
        i = pl.program_id(1)

        @pl.when(i == 0)
        def _():
            dkt_ref[...] = jnp.zeros_like(dkt_ref)
            dvt_ref[...] = jnp.zeros_like(dvt_ref)

        qs = _sb_stack(q_ref[...], scale)
        dos = _sb_stack(do_ref[...])
        qst, dost = qs.T, dos.T
        totv = tot_ref[...]
        ones = jnp.ones((1, HEAD_W), F32)
        tots = jnp.concatenate([totv[:, 0:1] * ones, totv[:, SB_DH:SB_DH + 1] * ones], axis=0)
        qpos = i * SB_BLOCK + lax.broadcasted_iota(jnp.int32, (SB_BLOCK, 1), 0)
        qpos = jnp.concatenate([qpos, qpos], axis=0)
        incl, incl0 = _sb_cat("incl"), _sb_cat("incl", pad)
        before = _sb_cat("before")
        ng = i // SB_GROUP
        used = jnp.clip(nproc_ref[pl.program_id(0), i].astype(jnp.int32), 0, ng)

        def dscore(z, e, ev, dl1m):
            r = 1.0 / (1.0 + e)
            sg = jnp.where(z >= 0, r, e * r)
            return ev * (1.0 - sg) - dl1m * sg

        def group(off, nblk, first_incl, allowed, carry):
            dq, prun, erun = carry
            width = nblk * SB_BLOCK
            kg = k_ref[pl.ds(off, width), :].astype(BF16)
            vg = v_ref[pl.ds(off, width), :].astype(BF16)
            z = _dot(qs, kg, NT)
            lsz, l1m, e = _sb_logsig(z)
            if allowed is not None:
                l1m = jnp.where(allowed, l1m, 0.0)
            dwgt = _dot(dos, vg, NT)
            dzs = [None] * nblk
            wgts = [None] * nblk
            for g in range(nblk):
                sl = slice(g * SB_BLOCK, (g + 1) * SB_BLOCK)
                al = _sb_cumsum(l1m[:, sl], first_incl if g == 0 else incl)
                wgt = jnp.exp(jnp.minimum(lsz[:, sl] + (tots - prun - al[:, :SB_BLOCK]), 0.0))
                if allowed is not None:
                    wgt = jnp.where(allowed[:, sl], wgt, 0.0)
                prun = prun + al[:, SB_BLOCK:]
                ev = wgt * dwgt[:, sl]
                el = _sb_cumsum(ev, before)
                dzs[g] = dscore(z[:, sl], e[:, sl], ev, erun + el[:, :SB_BLOCK])
                erun = erun + el[:, SB_BLOCK:]
                wgts[g] = wgt
            dz = jnp.concatenate(dzs, axis=1)
            if allowed is not None:
                dz = jnp.where(allowed, dz, 0.0)
            dz = dz.astype(BF16)
            wg = jnp.concatenate(wgts, axis=1).astype(BF16)
            dkt_ref[:, pl.ds(off, width)] += _dot(qst, dz, NN)
            dvt_ref[:, pl.ds(off, width)] += _dot(dost, wg, NN)
            return dq + _dot(dz, kg, NN), prun, erun

        def below(gi, carry):
            return group(pl.multiple_of(gi * gw, gw), SB_GROUP, jnp.where(gi == 0, incl0, incl), None, carry)

        zero = tuple(jnp.zeros((2 * SB_BLOCK, HEAD_W), F32) for _ in range(3))
        carry = lax.fori_loop(ng - used, ng, below, zero)
        top = ng * gw

        def top_group(nblk, carry):
            off = pl.multiple_of(jnp.minimum(top, lp - nblk * SB_BLOCK), SB_BLOCK)
            kpos = off + lax.broadcasted_iota(jnp.int32, (1, nblk * SB_BLOCK), 1)
            return group(off, nblk, incl, (kpos < qpos) & (kpos >= pad) & (kpos >= top), carry)

        dq, _, _ = lax.cond(i - ng * SB_GROUP < SB_GROUP // 2, functools.partial(top_group, SB_GROUP // 2),
                            functools.partial(top_group, SB_GROUP), carry)
        dq_ref[...] = _sb_unstack(dq) * scale
        if ns:
            pl.when((pl.program_id(0) == npair - 1) & (pl.program_id(1) == nb - 1))(s_finish)

    full = lambda c0: pl.BlockSpec((lp, HEAD_W), lambda p, i: (0, c0 + p))
    qb = lambda c0: pl.BlockSpec((SB_BLOCK, HEAD_W), lambda p, i: (i, c0 + p))
    tr = pl.BlockSpec((HEAD_W, lp), lambda p, i: (p, 0))
    return pl.pallas_call(
        body, name=name, grid=(npair, nb),
        in_specs=[qb(blk0), full(blk0 + npair), full(blk0 + 2 * npair), qb(0), pl.BlockSpec(memory_space=pltpu.SMEM),
                  qb(d_blk0)] + [_ANY] * ns,
        out_specs=[qb(0), tr, tr] + [_ANY] * ns,
        out_shape=[jax.ShapeDtypeStruct((lp, npair * HEAD_W), F32)]
        + [jax.ShapeDtypeStruct((npair * HEAD_W, lp), F32)] * 2
        + [jax.ShapeDtypeStruct(s.shape, s.dtype) for s in scatter],
        scratch_shapes=_scatter_scratch(ns) if ns else [],
        compiler_params=_cp(("arbitrary", "arbitrary"), has_side_effects=bool(ns)),
    )(p0, p0, p0, tot, nproc, dsrc, *scatter)


def _local_step(h0, target, pad, wts, hooks=None):
    lp = h0.shape[0]
    tm = _row_tile(lp, 1056)
    tkl = tm
    tml = _row_tile(lp, 528)
    d = D_MODEL
    mm = _mm
    mmw = functools.partial(_mm, out_dtype=BF16)
    k_major = lambda wd: wd.transpose(1, 0, 2).reshape(wd.shape[1], -1)
    g = {}

    h0_b = h0.astype(BF16)
    p0 = mm(h0_b, wts["w_ab"], "NN", tm=tm, tn=768, tk=d, name="l0_in_proj")
    ob, sb_tot, sb_used, *gathered = _sb_fwd(p0, pad, name="sb_fwd", gather=hooks["gather_a"] if hooks else None)
    if hooks:
        wts = {**wts, **hooks["weights_a"](gathered)}
    qkv = _gdn_pre_fwd(p0, wts["conv_w"], pad, name="gdn_pre_fwd")
    oa_raw, gdn_states, *gathered = _gdn_fwd(qkv, p0, wts["alog_v"], wts["dtb_v"], pad, name="gdn_fwd",
                                             gather=hooks["gather_b"] if hooks else None)
    if hooks:
        wts = {**wts, **hooks["weights_b"](gathered)}
    rows = lambda a, n: a.reshape(N_DEV, n // N_DEV, d)
    parts = g["parts"] = {}
    oab = _gate_fwd(oa_raw, p0, AB_Z // HEAD_W, wts["ab_gn"], ob, heads=GDN_HEADS, name="gdn_gate_fwd")
    ln = lambda kind, layer: (wts[f"ln_{kind}_g"][layer], wts[f"ln_{kind}_b"][layer])
    pre_mix0, h0a, h0a_b = mm(oab, wts["w_out0"], "NN", tm=tm, tn=d, tk=d, epi="ln", c=h0, scale=DN_ALPHA,
                              ln=ln("mix", 0), name="l0_out_proj")
    u0, act0 = mm(h0a_b, k_major(wts["w1"][0]), "NN", tm=tm, tn=1024, tk=d, epi="relu2_copy", name="mlp0_up")
    pre_ffn0, h0b, h0b_b = mm(act0, wts["w2"][0], "NN", tm=tm, tn=d, tk=d, epi="ln", c=h0a, scale=DN_ALPHA,
                              ln=ln("ffn", 0), name="mlp0_down")
    p1 = mm(h0b_b, k_major(wts["w_c"]), "NN", tm=tm, tn=1024, tk=d, name="l1_in_proj")
    oc_raw, hg_states, *gathered = _hg_fwd(p1, wts["lb"], pad, name="hg_fwd",
                                           gather=hooks["gather_c"] if hooks else None)
    if hooks:
        third = hooks["weights_c"](gathered)
        wts = {**wts, "w1": wts["w1"] + third["w1"], "w2": wts["w2"] + third["w2"]}
    oc = _gate_fwd(oc_raw, p1, 3 * HG_HEADS, wts["c_gn"], oc_raw, heads=HG_HEADS, name="hg_gate_fwd")
    pre_mix1, h1a, h1a_b = mm(oc, wts["w_out1"], "NN", tm=tm, tn=d, tk=d, epi="ln", c=h0b, scale=DN_ALPHA,
                              ln=ln("mix", 1), name="l1_out_proj")
    u1, act1 = mm(h1a_b, k_major(wts["w1"][1]), "NN", tm=tm, tn=1024, tk=d, epi="relu2_copy", name="mlp1_up")
    pre_ffn1, h1b, _ = mm(act1, wts["w2"][1], "NN", tm=tm, tn=d, tk=d, epi="ln", c=h1a, scale=DN_ALPHA,
                          ln=ln("ffn", 1), name="mlp1_down")
    dy, loss_vec = _loss_head(h1b, target, name="loss_head")

    def mlp_bwd(layer, h_in_b, u, act, dpre, dpre_b, pre_mix):
        du = mm(dpre_b, wts["w2"][layer], "NT", tm=tm, tn=1024, tk=d, epi="relu2grad", c=u, out_dtype=BF16,
                name=f"mlp{layer}_d_hidden")
        dw2 = mmw(act, dpre_b, "TN", tm=1024, tn=1024, tk=tkl, name=f"mlp{layer}_dw2")
        dw1 = mmw(h_in_b, du, "TN", tm=1024, tn=512, tk=tkl, out_dev=True, name=f"mlp{layer}_dw1")
        return (*mm(du, k_major(wts["w1"][layer]), "NT", tm=tml, tn=1024, tk=2048, epi="ln_bwd", c=dpre, scale=DN_ALPHA,
                    ln=(pre_mix, wts["ln_mix_g"][layer]), name=f"mlp{layer}_d_in"), dw1, dw2)

    ln_ffn_dg, ln_ffn_db, ln_mix_dg, ln_mix_db, dw1s, dw2s = ([None, None] for _ in range(6))
    dpre, dpre_b, ln_ffn_dg[1], ln_ffn_db[1] = _ln_bwd(pre_ffn1, wts["ln_ffn_g"][1], dy, name="ln_ffn1_bwd")
    dpre, dpre_b, ln_mix_dg[1], ln_mix_db[1], dw1s[1], dw2s[1] = mlp_bwd(1, h1a_b, u1, act1, dpre, dpre_b, pre_mix1)
    g["c_w_out"] = mmw(oc, dpre_b, "TN", tm=1024, tn=1024, tk=tkl, name="l1_dw_out")
    doc = mm(dpre_b, wts["w_out1"], "NT", tm=tm, tn=1024, tk=d, name="l1_d_gate")
    doc_raw, dz1, g["c_gn"] = _gate_bwd(oc_raw, p1, 3 * HG_HEADS, wts["c_gn"], doc, heads=HG_HEADS, name="hg_gate_bwd")
    ready = [dw1s[1], rows(dw2s[1], D_FF), rows(g["c_w_out"], d)] if hooks else ()
    dq1, df1, di1, g["lb"], *got = _hg_bwd(p1, wts["lb"], hg_states, doc_raw, pad, name="hg_bwd", scatter=ready)
    parts.update(zip(("mlp_w1_1", "mlp_w2_1", "c_w_out"), got))
    dp1 = jnp.concatenate([dq1, df1, di1, dz1], axis=1).astype(BF16)
    g["c_w_in"] = mmw(h0b_b, dp1, "TN", tm=1024, tn=512, tk=tkl, out_dev=True, name="l1_dw_in")
    dpre, dpre_b, ln_ffn_dg[0], ln_ffn_db[0] = mm(
        dp1, k_major(wts["w_c"]), "NT", tm=tml, tn=1024, tk=2048, epi="ln_bwd", c=dpre, scale=DN_ALPHA,
        ln=(pre_ffn0, wts["ln_ffn_g"][0]), name="l1_d_in")
    dpre, dpre_b, ln_mix_dg[0], ln_mix_db[0], dw1s[0], dw2s[0] = mlp_bwd(0, h0a_b, u0, act0, dpre, dpre_b, pre_mix0)
    g["ab_w_out"] = mmw(oab, dpre_b, "TN", tm=1024, tn=1024, tk=tkl, name="l0_dw_out")
    doab = mm(dpre_b, wts["w_out0"], "NT", tm=tm, tn=1024, tk=d, name="l0_d_gate")
    doa_raw, dz0, g["ab_gn"] = _gate_bwd(oa_raw, p0, AB_Z // HEAD_W, wts["ab_gn"], doab, heads=GDN_HEADS,
                                         name="gdn_gate_bwd")
    ready = [g["c_w_in"]] if hooks else ()
    dqb, dkb_t, dvb_t, *got = _sb_bwd(p0, sb_tot, sb_used, doab, GDN_HEADS, pad, name="sb_bwd", scatter=ready)
    parts.update(zip(("c_w_in",), got))
    dkb, dvb = dkb_t.T, dvb_t.T
    ready = [dw1s[0], rows(dw2s[0], D_FF), rows(g["ab_w_out"], d)] if hooks else ()
    dqn, dkn, dvn, dba, g["alog_v"], g["dtb_v"], *got = _gdn_bwd(qkv, p0, wts["alog_v"], wts["dtb_v"], gdn_states,
                                                                 doa_raw, pad, name="gdn_bwd", scatter=ready)
    parts.update(zip(("mlp_w1_0", "mlp_w2_0", "ab_w_out"), got))
    dconv_in, g["conv_w"] = _gdn_pre_bwd(p0, wts["conv_w"], jnp.concatenate([dqn, dkn, dvn], axis=1), pad,
                                         name="gdn_pre_bwd")
    dp0 = jnp.concatenate([dconv_in, dz0, dqb, dkb, dvb, dba, jnp.zeros((lp, AB_CAT - AB_BA - HEAD_W), F32)],
                          axis=1).astype(BF16)
    g["w_ab"] = mmw(h0_b, dp0, "TN", tm=1024, tn=768, tk=tkl, name="l0_dw_in")
    last = ()
    if hooks:
        gab, ba0 = g["w_ab"], AB_Z + GDN_HEADS * HEAD_W
        gab = jnp.concatenate([gab[:, :ba0], gab[:, AB_BA:AB_BA + 2 * GDN_HEADS], gab[:, ba0:AB_BA]], axis=1)
        last = [gab.reshape(d, N_DEV, AB_IN // N_DEV).transpose(1, 0, 2)]
    res = mm(dp0, wts["w_ab"], "NT", tm=tm, tn=1024, tk=1920, epi="add", c=dpre, scale=DN_ALPHA, scatter=last,
             name="l0_d_in")
    dh0 = res[0] if last else res
    parts.update(zip(("ab_w_in",), res[1:] if last else ()))

    g["w1"], g["w2"] = dw1s, dw2s
    g["ln_mix_g"] = jnp.concatenate(ln_mix_dg, axis=0)
    g["ln_mix_b"] = jnp.concatenate(ln_mix_db, axis=0)
    g["ln_ffn_g"] = jnp.concatenate(ln_ffn_dg, axis=0)
    g["ln_ffn_b"] = jnp.concatenate(ln_ffn_db, axis=0)
    return loss_vec, dh0, g


N_CHIP = N_DEV // 2


def _place():
    x, y, c = lax.axis_index("x"), lax.axis_index("y"), lax.axis_index("c")
    return x, y, c, 2 * x + y


def _chip_dev(chip, core):
    return (chip // 2, chip % 2, core)


def _remote(src, dst, send_sem, recv_sem, dev):
    return pltpu.make_async_remote_copy(src_ref=src, dst_ref=dst, send_sem=send_sem, recv_sem=recv_sem,
                                        device_id=dev, device_id_type=pl.DeviceIdType.MESH)


_ANY = pl.BlockSpec(memory_space=pl.ANY)


def _gather(srcs, dtypes, *, name):
    n = len(srcs)

    def body(*refs):
        start, forward, finish = _gather_phases(refs[:n], refs[n:2 * n], refs[2 * n:3 * n], *refs[3 * n:], dtypes)
        start()
        forward()
        finish()

    return pl.pallas_call(
        body, name=name, in_specs=[pl.BlockSpec(memory_space=pltpu.VMEM)] * n, out_specs=[_ANY] * n,
        out_shape=_gather_out_shapes(srcs, dtypes), scratch_shapes=_gather_scratch(srcs, dtypes),
        compiler_params=_cp(has_side_effects=True),
    )(*srcs)


def _gather_out_shapes(srcs, dtypes):
    return [jax.ShapeDtypeStruct((N_DEV, *s.shape), dt) for s, dt in zip(srcs, dtypes)]


def _gather_scratch(srcs, dtypes):
    n = len(srcs)
    return [pltpu.VMEM(s.shape, dt) for s, dt in zip(srcs, dtypes)] + [
        pltpu.SemaphoreType.DMA((n, 2 * N_CHIP - 1)), pltpu.SemaphoreType.DMA((n, 2 * N_CHIP - 1)),
        pltpu.SemaphoreType.DMA((n,))]


def _gather_phases(ins, outs, stages, send_sems, recv_sems, local_sems, dtypes):
    n = len(ins)
    x, y, c, chip = _place()
    me = 2 * chip + c
    sibling = (x, y, 1 - c)

    def own(i):
        cps = [_remote(stages[i], outs[i].at[me], send_sems.at[i, 0], recv_sems.at[i, 0], sibling)]
        for j in range(1, N_CHIP):
            cps.append(_remote(stages[i], outs[i].at[me], send_sems.at[i, j], recv_sems.at[i, j],
                               _chip_dev(jnp.bitwise_xor(chip, j), c)))
        return cps

    def local(i):
        return pltpu.make_async_copy(stages[i], outs[i].at[me], local_sems.at[i])

    def passed_on(i, j):
        slot = outs[i].at[2 * jnp.bitwise_xor(chip, j) + c]
        return _remote(slot, slot, send_sems.at[i, N_CHIP - 1 + j], recv_sems.at[i, N_CHIP - 1 + j], sibling)

    def start():
        for i in range(n):
            stages[i][...] = ins[i][...].astype(dtypes[i])
            local(i).start()
            for cp in own(i):
                cp.start()

    def forward():
        for i in range(n):
            for j in range(1, N_CHIP):
                own(i)[j].wait_recv()
                passed_on(i, j).start()

    def finish():
        for i in range(n):
            own(i)[0].wait_recv()
            for j in range(1, N_CHIP):
                passed_on(i, j).wait_recv()
        for i in range(n):
            for cp in own(i):
                cp.wait_send()
            for j in range(1, N_CHIP):
                passed_on(i, j).wait_send()
            local(i).wait()

    return start, forward, finish


def _scatter_scratch(n):
    return [pltpu.SemaphoreType.DMA((n, N_DEV - 1)), pltpu.SemaphoreType.DMA((n, N_DEV - 1)),
            pltpu.SemaphoreType.DMA((n,))]


def _scatter_phases(ins, outs, send_sems, recv_sems, local_sems):
    n = len(ins)
    _, _, c, chip = _place()
    me = 2 * chip + c

    def copies():
        cps = []
        for i in range(n):
            cps.append(pltpu.make_async_copy(ins[i].at[me], outs[i].at[me], local_sems.at[i]))
            for k in range(1, N_DEV):
                peer = jnp.bitwise_xor(me, k)
                cps.append(_remote(ins[i].at[peer], outs[i].at[me], send_sems.at[i, k - 1], recv_sems.at[i, k - 1],
                                   _chip_dev(peer // 2, peer % 2)))
        return cps

    def start():
        for cp in copies():
            cp.start()

    def finish():
        for cp in copies():
            cp.wait()

    return start, finish


def _adamw(w, parts, m, v, *, name):
    r, c = w.shape
    s = parts.shape[0]
    tm = _row_tile(r, 128) if r % 8 == 0 else r
    c1 = 1.0 - ADAM_B1 ** ADAM_STEP
    c2 = 1.0 - ADAM_B2 ** ADAM_STEP

    def body(w_ref, p_ref, m_ref, v_ref, g_ref, d_ref, m2_ref, v2_ref):
        g = p_ref[0].astype(F32)
        for j in range(1, s):
            g = g + p_ref[j].astype(F32)
        m2 = ADAM_B1 * m_ref[...] + (1.0 - ADAM_B1) * g
        v2 = ADAM_B2 * v_ref[...] + (1.0 - ADAM_B2) * jnp.square(g)
        g_ref[...] = g
        m2_ref[...] = m2
        v2_ref[...] = v2
        d_ref[...] = -ADAM_LR * ((m2 / c1) / (jnp.sqrt(v2 / c2) + ADAM_EPS) + ADAM_WD * w_ref[...])

    blk = pl.BlockSpec((tm, c), lambda i: (i, 0))
    return pl.pallas_call(
        body, name=name, grid=(r // tm,),
        in_specs=[blk, pl.BlockSpec((s, tm, c), lambda i: (0, i, 0)), blk, blk], out_specs=[blk] * 4,
        out_shape=[jax.ShapeDtypeStruct((r, c), F32)] * 4, compiler_params=_cp(("parallel",)),
    )(w, parts, m, v)


_WEIGHTS = ("meta_tokens", "ab_w_in", "ab_conv_w", "ab_a_log", "ab_dt_bias", "ab_gnorm_g", "ab_w_out", "c_w_in",
            "c_lb_raw", "c_gnorm_g", "c_w_out", "ln_mix_g", "ln_mix_b", "mlp_w1", "mlp_w2", "ln_ffn_g", "ln_ffn_b")
_PACK_ROWS = (("ln_mix_g", 0), ("ln_mix_b", 2), ("ln_ffn_g", 4), ("ln_ffn_b", 6), ("c_lb_raw", 8))
_PACK_MISC_ROW = 10
_PACK_MISC = (("ab_gnorm_g", 0, 128), ("c_gnorm_g", 128, 128), ("ab_a_log", 256, GDN_HEADS), ("ab_dt_bias", 260, GDN_HEADS))
_PACK_N = 16
_SMALL_META = 16
_SMALL_CONV = 32
_SMALL_N = 40


def _pack_replicated(p):
    rows = jnp.zeros((_PACK_N, D_MODEL), F32)
    for name, r0 in _PACK_ROWS:
        rows = rows.at[r0:r0 + 2].set(p[name])
    for name, c0, width in _PACK_MISC:
        rows = rows.at[_PACK_MISC_ROW, c0:c0 + width].set(p[name].reshape(width))
    return rows


def _unpack_replicated(rows, like):
    out = {}
    for name, r0 in _PACK_ROWS:
        out[name] = rows[r0:r0 + 2]
    for name, c0, width in _PACK_MISC:
        out[name] = rows[_PACK_MISC_ROW, c0:c0 + width].reshape(like[name].shape)
    return out


def _lower_bound(c_lb_raw):
    lb_all = jnp.cumsum(jax.nn.softmax(c_lb_raw.astype(F32), axis=0), axis=0)
    return (lb_all - lb_all[0:1])[1].reshape(1, -1)


def kernel(x, meta_tokens, ab_w_in, ab_conv_w, ab_a_log, ab_dt_bias, ab_gnorm_g, ab_w_out, c_w_in, c_lb_raw, c_gnorm_g, c_w_out, ln_mix_g, ln_mix_b, mlp_w1, mlp_w2, ln_ffn_g, ln_ffn_b, loss_target, m_meta_tokens, m_ab_w_in, m_ab_conv_w, m_ab_a_log, m_ab_dt_bias, m_ab_gnorm_g, m_ab_w_out, m_c_w_in, m_c_lb_raw, m_c_gnorm_g, m_c_w_out, m_ln_mix_g, m_ln_mix_b, m_mlp_w1, m_mlp_w2, m_ln_ffn_g, m_ln_ffn_b, v_meta_tokens, v_ab_w_in, v_ab_conv_w, v_ab_a_log, v_ab_dt_bias, v_ab_gnorm_g, v_ab_w_out, v_c_w_in, v_c_lb_raw, v_c_gnorm_g, v_c_w_out, v_ln_mix_g, v_ln_mix_b, v_mlp_w1, v_mlp_w2, v_ln_ffn_g, v_ln_ffn_b):
    w = dict(zip(_WEIGHTS, (meta_tokens, ab_w_in, ab_conv_w, ab_a_log, ab_dt_bias, ab_gnorm_g, ab_w_out, c_w_in, c_lb_raw,
                            c_gnorm_g, c_w_out, ln_mix_g, ln_mix_b, mlp_w1, mlp_w2, ln_ffn_g, ln_ffn_b)))
    mom = dict(zip(_WEIGHTS, (m_meta_tokens, m_ab_w_in, m_ab_conv_w, m_ab_a_log, m_ab_dt_bias, m_ab_gnorm_g, m_ab_w_out,
                              m_c_w_in, m_c_lb_raw, m_c_gnorm_g, m_c_w_out, m_ln_mix_g, m_ln_mix_b, m_mlp_w1, m_mlp_w2,
                              m_ln_ffn_g, m_ln_ffn_b)))
    var = dict(zip(_WEIGHTS, (v_meta_tokens, v_ab_w_in, v_ab_conv_w, v_ab_a_log, v_ab_dt_bias, v_ab_gnorm_g, v_ab_w_out,
                              v_c_w_in, v_c_lb_raw, v_c_gnorm_g, v_c_w_out, v_ln_mix_g, v_ln_mix_b, v_mlp_w1, v_mlp_w2,
                              v_ln_ffn_g, v_ln_ffn_b)))
    me = 4 * lax.axis_index("x") + 2 * lax.axis_index("y") + lax.axis_index("c")
    seq = x.shape[1]
    pad = (-(N_META + seq)) % SB_BLOCK
    lp = pad + N_META + seq
    meta_w = D_MODEL // N_DEV
    conv_w_all = 2 * GDN_HEADS * HEAD_W + GDN_HEADS * HEAD_W
    conv_w_mine = conv_w_all // N_DEV

    g_meta, g_conv, g_ab_in = _gather([w["meta_tokens"], w["ab_conv_w"][0], w["ab_w_in"][0]], [F32, F32, BF16],
                                      name="gather_weights_first")
    meta_full = g_meta.transpose(1, 0, 2).reshape(N_META, D_MODEL)
    conv_full = g_conv.transpose(1, 0, 2).reshape(CONV_K, conv_w_all)
    ab_full = g_ab_in.transpose(1, 0, 2).reshape(D_MODEL, AB_IN)
    ba0 = AB_Z + 512
    w_ab = jnp.concatenate([ab_full[:, :ba0], ab_full[:, ba0 + 2 * GDN_HEADS:], ab_full[:, ba0:ba0 + 2 * GDN_HEADS],
                            jnp.zeros((D_MODEL, AB_CAT - AB_IN), BF16)], axis=1)
    vec128 = lambda p: jnp.zeros((1, HEAD_W), F32).at[0, :GDN_HEADS].set(p.reshape(GDN_HEADS))
    wts = dict(
        w_ab=w_ab, conv_w=conv_full, alog_v=vec128(w["ab_a_log"]), dtb_v=vec128(w["ab_dt_bias"]),
        ab_gn=w["ab_gnorm_g"][0], lb=_lower_bound(w["c_lb_raw"]), c_gn=w["c_gnorm_g"][0],
        ln_mix_g=w["ln_mix_g"], ln_mix_b=w["ln_mix_b"], ln_ffn_g=w["ln_ffn_g"], ln_ffn_b=w["ln_ffn_b"])

    def weights_a(gathered):
        g_ab_out, g_w1, g_w2 = gathered
        return dict(w_out0=g_ab_out.reshape(D_MODEL, D_MODEL), w1=[g_w1], w2=[g_w2.reshape(D_FF, D_MODEL)])

    def weights_b(gathered):
        g_c_in, g_c_out = gathered
        return dict(w_c=g_c_in, w_out1=g_c_out.reshape(D_MODEL, D_MODEL))

    def weights_c(gathered):
        g_w1, g_w2 = gathered
        return dict(w1=[g_w1], w2=[g_w2.reshape(D_FF, D_MODEL)])

    hooks = dict(
        gather_a=([w["ab_w_out"][0], w["mlp_w1"][0], w["mlp_w2"][0]], [BF16] * 3), weights_a=weights_a,
        gather_b=([w["c_w_in"][0], w["c_w_out"][0]], [BF16] * 2), weights_b=weights_b,
        gather_c=([w["mlp_w1"][1], w["mlp_w2"][1]], [BF16] * 2), weights_c=weights_c)

    h0 = jnp.concatenate([jnp.zeros((pad, D_MODEL), F32), meta_full, x[0]], axis=0)
    loss_vec, dh0, g = _local_step(h0, loss_target[0], pad, wts, hooks)
    loss = lax.psum(jnp.sum(loss_vec), ("x", "y", "c"))
    grad_x = dh0[lp - seq:][None]

    _, lb_vjp = jax.vjp(_lower_bound, w["c_lb_raw"])
    rep_part = _pack_replicated(dict(
        ln_mix_g=g["ln_mix_g"], ln_mix_b=g["ln_mix_b"], ln_ffn_g=g["ln_ffn_g"], ln_ffn_b=g["ln_ffn_b"],
        c_lb_raw=lb_vjp(g["lb"])[0], ab_gnorm_g=g["ab_gn"], c_gnorm_g=g["c_gn"],
        ab_a_log=g["alog_v"][0, :GDN_HEADS], ab_dt_bias=g["dtb_v"][0, :GDN_HEADS]))
    small = jnp.concatenate([rep_part, dh0[pad:pad + N_META], g["conv_w"].reshape(-1, D_MODEL),
                             jnp.zeros((_SMALL_N - _SMALL_CONV - CONV_K * conv_w_all // D_MODEL, D_MODEL), F32)], axis=0)
    (small_all,) = _gather([small], [F32], name="gather_small_grads")
    rep_out = _adamw(_pack_replicated(w), small_all[:, :_PACK_N], _pack_replicated(mom), _pack_replicated(var),
                     name="adamw_replicated")
    meta_parts = lax.dynamic_slice_in_dim(small_all[:, _SMALL_META:_SMALL_META + N_META], me * meta_w, meta_w, axis=2)
    meta_out = _adamw(w["meta_tokens"], meta_parts, mom["meta_tokens"], var["meta_tokens"], name="adamw_meta")
    conv_parts = small_all[:, _SMALL_CONV:_SMALL_CONV + CONV_K * conv_w_all // D_MODEL].reshape(N_DEV, CONV_K, conv_w_all)
    conv_parts = lax.dynamic_slice_in_dim(conv_parts, me * conv_w_mine, conv_w_mine, axis=2)
    conv_out = _adamw(w["ab_conv_w"][0], conv_parts, mom["ab_conv_w"][0], var["ab_conv_w"][0], name="adamw_conv")

    parts = g["parts"]
    big = [("ab_w_in", 0, parts["ab_w_in"]), ("ab_w_out", 0, parts["ab_w_out"]), ("mlp_w1", 0, parts["mlp_w1_0"]),
           ("mlp_w2", 0, parts["mlp_w2_0"]), ("c_w_in", 0, parts["c_w_in"]), ("c_w_out", 0, parts["c_w_out"]),
           ("mlp_w1", 1, parts["mlp_w1_1"]), ("mlp_w2", 1, parts["mlp_w2_1"])]
    big_out = {}
    for name, l, p in big:
        res = _adamw(w[name][l], p, mom[name][l], var[name][l], name=f"adamw_{name}{l}")
        big_out.setdefault(name, []).append(res)

    rep = [_unpack_replicated(r, w) for r in rep_out]
    outs = {}
    for name in _WEIGHTS:
        if name == "meta_tokens":
            outs[name] = list(meta_out)
        elif name == "ab_conv_w":
            outs[name] = [o[None] for o in conv_out]
        elif name in big_out:
            res = big_out[name]
            outs[name] = [o[None] for o in res[0]] if len(res) == 1 else [jnp.stack(pair) for pair in zip(*res)]
        else:
            outs[name] = [r[name] for r in rep]
    flat = [loss, grad_x]
    for kind in range(4):
        flat += [outs[name][kind] for name in _WEIGHTS]
    return tuple(flat)
```

```python
import functools

import jax
import jax.numpy as jnp
from jax import lax
from jax.experimental import pallas as pl
from jax.experimental.pallas import tpu as pltpu

F32 = jnp.float32
BF16 = jnp.bfloat16

N_DEV = 8
D_MODEL = 1024
N_META = 16
D_FF = 4096
DEPTH = 2
GDN_HEADS = 4
SB_HEADS = 8
SB_DH = 64
HG_HEADS = 8
HEAD_W = 128
CHUNK = 64
SB_BLOCK = 128
CONV_K = 4
DN_ALPHA = float((2 * DEPTH) ** 0.25)
LN_EPS = 1e-5
RMS_EPS = 1e-6
L2_EPS = 1e-6
ADAM_LR, ADAM_B1, ADAM_B2, ADAM_EPS, ADAM_WD, ADAM_STEP = 0.001, 0.9, 0.999, 1e-08, 0.01, 10

AB_Z = 1536
AB_SB = 2048
AB_BA = 3584
AB_CAT = 3840
AB_IN = 3592

VMEM_LIMIT = 56 * 1024 * 1024


def _cp(sem=None, **kw):
    if sem is not None:
        kw["dimension_semantics"] = sem
    return pltpu.CompilerParams(vmem_limit_bytes=VMEM_LIMIT, **kw)


def _row_tile(n, want):
    best = 8
    for t in range(8, min(n, want) + 1, 8):
        if n % t == 0:
            best = t
    return best


@jax.custom_vjp
def _sigmoid(x):
    e = jnp.exp(-jnp.abs(x))
    r = 1.0 / (1.0 + e)
    return jnp.where(x >= 0, r, e * r)


def _sigmoid_fwd(x):
    s = _sigmoid(x)
    return s, s


def _sigmoid_bwd(s, g):
    return (g * s * (1.0 - s),)


_sigmoid.defvjp(_sigmoid_fwd, _sigmoid_bwd)


def _log1p_exp_neg_abs(x):
    e = jnp.exp(-jnp.abs(x))
    return jnp.where(e < 1e-4, e - 0.5 * e * e, jnp.log(1.0 + e))


@jax.custom_vjp
def _softplus(x):
    return jnp.maximum(x, 0.0) + _log1p_exp_neg_abs(x)


def _softplus_fwd(x):
    return _softplus(x), x


def _softplus_bwd(x, g):
    return (g * _sigmoid(x),)


_softplus.defvjp(_softplus_fwd, _softplus_bwd)


def _silu(x):
    return x * _sigmoid(x)


def _silu_grad(x):
    s = _sigmoid(x)
    return s * (1.0 + x * (1.0 - s))


def _dot(a, b, dims, precision=None):
    return lax.dot_general(a, b, (dims, ((), ())), precision=precision, preferred_element_type=F32)


NN = ((1,), (0,))
NT = ((1,), (1,))
TN = ((0,), (0,))


def _bdot(a, b, dims):
    return _dot(a.astype(BF16), b.astype(BF16), dims)


def _layer_norm(pre, g, beta):
    mu = jnp.mean(pre, axis=-1, keepdims=True)
    xc = pre - mu
    var = jnp.mean(xc * xc, axis=-1, keepdims=True)
    return xc * lax.rsqrt(var + LN_EPS) * g + beta


def _layer_norm_bwd(pre, g, dy):
    mu = jnp.mean(pre, axis=-1, keepdims=True)
    xc = pre - mu
    rstd = lax.rsqrt(jnp.mean(xc * xc, axis=-1, keepdims=True) + LN_EPS)
    xhat = xc * rstd
    dxh = dy * g
    m1 = jnp.mean(dxh, axis=-1, keepdims=True)
    m2 = jnp.mean(dxh * xhat, axis=-1, keepdims=True)
    return (rstd * (dxh - m1 - xhat * m2), jnp.sum(dy * xhat, axis=0, keepdims=True),
            jnp.sum(dy, axis=0, keepdims=True))


def _mm(a, b, mode, *, tm, tn, tk, name, epi=None, c=None, scale=1.0, b_dev=False, out_dev=False, out_dtype=F32,
        ln=None, scatter=()):
    if mode == "NN":
        m, kk = a.shape
        n = b.shape[2] * N_DEV if b_dev else b.shape[1]
    elif mode == "NT":
        m, kk = a.shape
        n = b.shape[1] if b_dev else b.shape[0]
    else:
        kk, m = a.shape
        n = b.shape[1]
    assert m % tm == 0 and n % tn == 0 and kk % tk == 0, (name, m, n, kk, tm, tn, tk)
    nk = kk // tk
    dims = {"NN": NN, "NT": NT, "TN": TN}[mode]

    if mode == "TN":
        a_spec = pl.BlockSpec((tk, tm), lambda i, j, k: (k, i))
    else:
        a_spec = pl.BlockSpec((tm, tk), lambda i, j, k: (i, k))
    if mode == "NN":
        if b_dev:
            assert tn == b.shape[2]
            b_spec = pl.BlockSpec((None, tk, tn), lambda i, j, k: (j, k, 0))
        else:
            b_spec = pl.BlockSpec((tk, tn), lambda i, j, k: (k, j))
    elif mode == "NT":
        if b_dev:
            assert tk == b.shape[2]
            b_spec = pl.BlockSpec((None, tn, tk), lambda i, j, k: (k, j, 0))
        else:
            b_spec = pl.BlockSpec((tn, tk), lambda i, j, k: (j, k))
    else:
        b_spec = pl.BlockSpec((tk, tn), lambda i, j, k: (k, j))
    in_specs = [a_spec, b_spec]
    operands = [a, b]
    if c is not None:
        in_specs.append(pl.BlockSpec((tm, tn), lambda i, j, k: (i, j)))
        operands.append(c)
    if epi == "ln":
        assert tn == n and not out_dev
        in_specs += [pl.BlockSpec((1, n), lambda i, j, k: (0, 0))] * 2
        operands += [ln[0].reshape(1, n), ln[1].reshape(1, n)]
    elif epi == "ln_bwd":
        assert tn == n and not out_dev
        in_specs += [pl.BlockSpec((tm, tn), lambda i, j, k: (i, j)), pl.BlockSpec((1, n), lambda i, j, k: (0, 0))]
        operands += [ln[0], ln[1].reshape(1, n)]
    if out_dev:
        assert tn == n // N_DEV
        out_shape = jax.ShapeDtypeStruct((N_DEV, m, tn), out_dtype)
        out_spec = pl.BlockSpec((None, tm, tn), lambda i, j, k: (j, i, 0))
    else:
        out_shape = jax.ShapeDtypeStruct((m, n), out_dtype)
        out_spec = pl.BlockSpec((tm, tn), lambda i, j, k: (i, j))
    if epi == "ln":
        out_shape = [out_shape, out_shape, jax.ShapeDtypeStruct((m, n), BF16)]
        out_spec = [out_spec] * 3
    elif epi == "relu2_copy":
        assert not out_dev
        out_shape = [out_shape, jax.ShapeDtypeStruct((m, n), BF16)]
        out_spec = [out_spec] * 2
    elif epi == "ln_bwd":
        vec_shape, vec_spec = jax.ShapeDtypeStruct((1, n), F32), pl.BlockSpec((1, n), lambda i, j, k: (0, 0))
        out_shape = [out_shape, jax.ShapeDtypeStruct((m, n), BF16), vec_shape, vec_shape]
        out_spec = [out_spec, out_spec, vec_spec, vec_spec]
    n_out = {"ln": 3, "relu2_copy": 2, "ln_bwd": 4}.get(epi, 1)
    ns = len(scatter)
    if ns:
        in_specs += [_ANY] * ns
        operands += list(scatter)
        out_shape = (out_shape if n_out > 1 else [out_shape]) + [jax.ShapeDtypeStruct(s.shape, s.dtype) for s in scatter]
        out_spec = (out_spec if n_out > 1 else [out_spec]) + [_ANY] * ns
    n_in = len(operands)
    grid = (m // tm, n // tn, nk)

    def body(*refs):
        a_ref, b_ref = refs[0], refs[1]
        c_ref = refs[2] if c is not None else None
        o_ref = refs[n_in]
        scratch0 = n_in + n_out + ns
        acc_ref = refs[scratch0] if nk > 1 else None
        if ns:
            s_start, s_finish = _scatter_phases(refs[n_in - ns:n_in], refs[n_in + n_out:scratch0],
                                                *refs[scratch0 + (1 if nk > 1 else 0):])
            at = lambda step: functools.reduce(lambda x, y: x & y, [pl.program_id(ax) == step[ax] for ax in range(3)])
            pl.when(at((0, 0, 0)))(s_start)
        p = _dot(a_ref[...].astype(BF16), b_ref[...].astype(BF16), dims)
        first_rows = pl.program_id(0) == 0

        def finish(acc):
            if epi == "add":
                acc = acc + scale * c_ref[...]
            elif epi == "relu2grad":
                acc = acc * (2.0 * jnp.maximum(c_ref[...], 0.0))
            elif epi == "relu2_copy":
                refs[n_in + 1][...] = jnp.square(jnp.maximum(acc, 0.0)).astype(BF16)
            elif epi == "ln_bwd":
                acc, dg, db = _layer_norm_bwd(refs[3][...], refs[4][...], acc + scale * c_ref[...])
                dg_ref, db_ref = refs[n_in + 2], refs[n_in + 3]

                @pl.when(first_rows)
                def _():
                    dg_ref[...] = jnp.zeros_like(dg_ref)
                    db_ref[...] = jnp.zeros_like(db_ref)

                dg_ref[...] += dg
                db_ref[...] += db
                refs[n_in + 1][...] = acc.astype(BF16)
            elif epi == "ln":
                acc = acc + scale * c_ref[...]
                y = _layer_norm(acc, refs[3][...], refs[4][...])
                refs[n_in + 1][...] = y
                refs[n_in + 2][...] = y.astype(BF16)
            o_ref[...] = acc.astype(out_dtype)

        if nk == 1:
            finish(p)
        else:
            k = pl.program_id(2)

            @pl.when(k == 0)
            def _():
                acc_ref[...] = p

            @pl.when(k > 0)
            def _():
                acc_ref[...] += p

            @pl.when(k == nk - 1)
            def _():
                finish(acc_ref[...])

        if ns:
            pl.when(at(tuple(g - 1 for g in grid)))(s_finish)

    res = pl.pallas_call(
        body, name=name, grid=grid, in_specs=in_specs, out_specs=out_spec, out_shape=out_shape,
        scratch_shapes=([pltpu.VMEM((tm, tn), F32)] if nk > 1 else []) + (_scatter_scratch(ns) if ns else []),
        compiler_params=_cp(("arbitrary",) * 3 if ns or epi == "ln_bwd" else ("parallel", "parallel", "arbitrary"),
                            has_side_effects=bool(ns)),
    )(*operands)
    return res


def _ln_bwd(pre, g, dy, *, name):
    lp, d = pre.shape
    tm = _row_tile(lp, 512)

    def body(pre_ref, g_ref, dy_ref, dpre_ref, dpreb_ref, dg_ref, db_ref):
        dpre, dg, db = _layer_norm_bwd(pre_ref[...], g_ref[...], dy_ref[...])
        dpre_ref[...] = dpre
        dpreb_ref[...] = dpre.astype(BF16)

        @pl.when(pl.program_id(0) == 0)
        def _():
            dg_ref[...] = jnp.zeros_like(dg_ref)
            db_ref[...] = jnp.zeros_like(db_ref)

        dg_ref[...] += dg
        db_ref[...] += db

    row = pl.BlockSpec((tm, d), lambda i: (i, 0))
    vec = pl.BlockSpec((1, d), lambda i: (0, 0))
    return pl.pallas_call(
        body, name=name, grid=(lp // tm,), in_specs=[row, vec, row], out_specs=[row, row, vec, vec],
        out_shape=[jax.ShapeDtypeStruct((lp, d), F32), jax.ShapeDtypeStruct((lp, d), BF16),
                   jax.ShapeDtypeStruct((1, d), F32), jax.ShapeDtypeStruct((1, d), F32)],
        compiler_params=_cp(("arbitrary",)),
    )(pre, g.reshape(1, d), dy)


def _loss_head(y, target, *, name):
    lp, d = y.shape
    seq = target.shape[0]
    tm = SB_BLOCK
    first = (lp - seq) // tm
    assert (lp - seq) % tm == 0 and seq % tm == 0

    def body(y_ref, t_ref, dy_ref, loss_ref):
        i = pl.program_id(0)
        live = i >= first
        diff = jnp.where(live, y_ref[...] - t_ref[...], 0.0)
        dy_ref[...] = diff * (1.0 / d)

        @pl.when(i == 0)
        def _():
            loss_ref[...] = jnp.zeros_like(loss_ref)

        loss_ref[...] += jnp.sum(diff * diff, axis=0, keepdims=True) * (0.5 / d)

    return pl.pallas_call(
        body, name=name, grid=(lp // tm,),
        in_specs=[pl.BlockSpec((tm, d), lambda i: (i, 0)),
                  pl.BlockSpec((tm, d), lambda i: (jnp.maximum(i - first, 0), 0))],
        out_specs=[pl.BlockSpec((tm, d), lambda i: (i, 0)), pl.BlockSpec((1, d), lambda i: (0, 0))],
        out_shape=[jax.ShapeDtypeStruct((lp, d), F32), jax.ShapeDtypeStruct((1, d), F32)],
        compiler_params=_cp(("arbitrary",)),
    )(y, target)


def _gate_fwd(o, zsrc, z_blk0, g, other, *, heads, name):
    lp = o.shape[0]
    tm = _row_tile(lp, 512)
    w = heads * HEAD_W
    assert (z_blk0 * HEAD_W) % w == 0
    has_other = w < D_MODEL

    def body(o_ref, z_ref, g_ref, *rest):
        y_ref = rest[-1]
        gv = g_ref[...]
        for h in range(heads):
            cs = slice(h * HEAD_W, (h + 1) * HEAD_W)
            ov = o_ref[:, cs]
            r = lax.rsqrt(jnp.mean(ov * ov, axis=-1, keepdims=True) + RMS_EPS)
            y_ref[:, cs] = (ov * r * gv * _silu(z_ref[:, cs])).astype(BF16)
        if has_other:
            y_ref[:, w:] = rest[0][...].astype(BF16)

    row = lambda width, blk: pl.BlockSpec((tm, width), lambda i: (i, blk))
    return pl.pallas_call(
        body, name=name, grid=(lp // tm,),
        in_specs=[row(w, 0), row(w, z_blk0 * HEAD_W // w), pl.BlockSpec((1, HEAD_W), lambda i: (0, 0))]
        + ([row(D_MODEL - w, 0)] if has_other else []),
        out_specs=row(D_MODEL, 0), out_shape=jax.ShapeDtypeStruct((lp, D_MODEL), BF16),
        compiler_params=_cp(("parallel",)),
    )(o, zsrc, g.reshape(1, HEAD_W), *([other] if has_other else []))


def _gate_bwd(o, zsrc, z_blk0, g, dy, *, heads, name):
    lp = o.shape[0]
    tm = _row_tile(lp, 512)

    w = heads * HEAD_W
    assert (z_blk0 * HEAD_W) % w == 0

    def body(o_ref, z_ref, g_ref, dy_ref, do_ref, dz_ref, dg_ref):
        @pl.when(pl.program_id(0) == 0)
        def _():
            dg_ref[...] = jnp.zeros_like(dg_ref)

        gv = g_ref[...]
        dg = jnp.zeros((1, HEAD_W), F32)
        for h in range(heads):
            cs = slice(h * HEAD_W, (h + 1) * HEAD_W)
            ov, zv, dyv = o_ref[:, cs], z_ref[:, cs], dy_ref[:, cs]
            r = lax.rsqrt(jnp.mean(ov * ov, axis=-1, keepdims=True) + RMS_EPS)
            nrm = ov * r
            s = _silu(zv)
            dn = dyv * gv * s
            do_ref[:, cs] = r * (dn - nrm * jnp.mean(dn * nrm, axis=-1, keepdims=True))
            dz_ref[:, cs] = dyv * nrm * gv * _silu_grad(zv)
            dg = dg + jnp.sum(dyv * nrm * s, axis=0, keepdims=True)
        dg_ref[...] += dg

    row = lambda blk: pl.BlockSpec((tm, w), lambda i: (i, blk))
    vec = pl.BlockSpec((1, HEAD_W), lambda i: (0, 0))
    return pl.pallas_call(
        body, name=name, grid=(lp // tm,),
        in_specs=[row(0), row(z_blk0 * HEAD_W // w), vec, row(0)], out_specs=[row(0), row(0), vec],
        out_shape=[jax.ShapeDtypeStruct((lp, w), F32), jax.ShapeDtypeStruct((lp, w), F32),
                   jax.ShapeDtypeStruct((1, HEAD_W), F32)],
        compiler_params=_cp(("arbitrary",)),
    )(o, zsrc, g.reshape(1, HEAD_W), dy)


def _conv_taps(x, w):
    acc = w[CONV_K - 1:CONV_K, :] * x
    for k in range(CONV_K - 1):
        acc = acc + w[k:k + 1, :] * pltpu.roll(x, CONV_K - 1 - k, 0)
    return acc


def _gdn_pre_fwd(p0, conv_w, pad, *, name):
    lp = p0.shape[0]
    nq = GDN_HEADS
    qscale = HEAD_W ** -0.5

    def body(x_ref, w_ref, y_ref):
        j = pl.program_id(0)
        c = _conv_taps(x_ref[...], w_ref[...])
        s = _silu(c)
        r = lax.rsqrt(jnp.sum(s * s, axis=-1, keepdims=True) + L2_EPS)
        mult = jnp.where(j < nq, r * qscale, jnp.where(j < 2 * nq, r, 1.0))
        rows = lax.broadcasted_iota(jnp.int32, (lp, 1), 0)
        y_ref[...] = jnp.where(rows >= pad, s * mult, 0.0)

    return pl.pallas_call(
        body, name=name, grid=(3 * nq,),
        in_specs=[pl.BlockSpec((lp, HEAD_W), lambda j: (0, j)), pl.BlockSpec((CONV_K, HEAD_W), lambda j: (0, j))],
        out_specs=pl.BlockSpec((lp, HEAD_W), lambda j: (0, j)),
        out_shape=jax.ShapeDtypeStruct((lp, 3 * nq * HEAD_W), F32), compiler_params=_cp(("parallel",)),
    )(p0, conv_w)


def _gdn_pre_bwd(p0, conv_w, dqkv, pad, *, name):
    lp = p0.shape[0]
    nq = GDN_HEADS
    qscale = HEAD_W ** -0.5

    def body(x_ref, w_ref, dy_ref, dx_ref, dw_ref):
        j = pl.program_id(0)
        x, w = x_ref[...], w_ref[...]
        c = _conv_taps(x, w)
        s = _silu(c)
        r = lax.rsqrt(jnp.sum(s * s, axis=-1, keepdims=True) + L2_EPS)
        rows = lax.broadcasted_iota(jnp.int32, (lp, 1), 0)
        dy = jnp.where(rows >= pad, dy_ref[...], 0.0)
        nrm = s * r
        dn = dy * jnp.where(j < nq, qscale, 1.0)
        ds_norm = r * (dn - nrm * jnp.sum(nrm * dn, axis=-1, keepdims=True))
        ds = jnp.where(j < 2 * nq, ds_norm, dy)
        dc = ds * _silu_grad(c)
        dx = w[CONV_K - 1:CONV_K, :] * dc
        dws = [None] * CONV_K
        dws[CONV_K - 1] = jnp.sum(dc * x, axis=0, keepdims=True)
        for k in range(CONV_K - 1):
            sh = CONV_K - 1 - k
            dx = dx + w[k:k + 1, :] * pltpu.roll(dc, lp - sh, 0)
            dws[k] = jnp.sum(dc * pltpu.roll(x, sh, 0), axis=0, keepdims=True)
        dx_ref[...] = dx
        dw_ref[...] = jnp.concatenate(dws, axis=0)

    blk = pl.BlockSpec((lp, HEAD_W), lambda j: (0, j))
    wblk = pl.BlockSpec((CONV_K, HEAD_W), lambda j: (0, j))
    return pl.pallas_call(
        body, name=name, grid=(3 * nq,), in_specs=[blk, wblk, blk], out_specs=[blk, wblk],
        out_shape=[jax.ShapeDtypeStruct((lp, 3 * nq * HEAD_W), F32),
                   jax.ShapeDtypeStruct((CONV_K, 3 * nq * HEAD_W), F32)],
        compiler_params=_cp(("parallel",)),
    )(p0, conv_w, dqkv)


@jax.custom_vjp
def _inv_unit_lower(m):
    c = m.shape[0]
    eye = (lax.broadcasted_iota(jnp.int32, (c, c), 0) == lax.broadcasted_iota(jnp.int32, (c, c), 1)).astype(F32)
    x = eye - m
    p = m
    n = 2
    while n < CHUNK:
        p = _bdot(p, p, NN)
        x = x + _bdot(x, p, NN)
        n *= 2
    return x


def _inv_fwd(m):
    t = _inv_unit_lower(m)
    return t, t


def _inv_bwd(t, g):
    return (-_bdot(_bdot(t, g, TN), t, NT),)


_inv_unit_lower.defvjp(_inv_fwd, _inv_bwd)


GDN_STEP = 3


def _heads_to_rows(x, nh):
    return jnp.concatenate([x[:, h * HEAD_W:(h + 1) * HEAD_W] for h in range(nh)], axis=0)


def _rows_to_heads(x, nh):
    c = x.shape[0] // nh
    return jnp.concatenate([x[h * c:(h + 1) * c] for h in range(nh)], axis=1)


def _gdn_chunk(q, k, v, ba, alog, dtb, states, valid):
    nh = GDN_HEADS
    c = q.shape[0]
    r = nh * c
    lane = lax.broadcasted_iota(jnp.int32, (1, HEAD_W), 1)
    pick = lambda x, l: jnp.sum(jnp.where(lane == l, x, 0.0), axis=-1, keepdims=True)
    beta = jnp.concatenate([jnp.where(valid, _sigmoid(pick(ba, h)), 0.0) for h in range(nh)], axis=0)
    g = jnp.concatenate(
        [jnp.where(valid, -jnp.exp(pick(alog, h)) * _softplus(pick(ba, nh + h) + pick(dtb, h)), 0.0) for h in range(nh)],
        axis=0)
    qs, ks, vs = _heads_to_rows(q, nh), _heads_to_rows(k, nh), _heads_to_rows(v, nh)
    rr = lax.broadcasted_iota(jnp.int32, (r, r), 0)
    cc = lax.broadcasted_iota(jnp.int32, (r, r), 1)
    same = (rr // c) == (cc // c)
    causal, strict = same & (cc <= rr), same & (cc < rr)
    lower = jnp.where(causal, 1.0, 0.0).astype(BF16)
    upper = jnp.where(same & (cc >= rr), 1.0, 0.0).astype(BF16)
    gcb = _mask_mm(lower, upper, g * jnp.ones((1, HEAD_W), F32))
    gc_col = jnp.concatenate([gcb] * (r // HEAD_W), axis=1)
    decay = jnp.where(causal, jnp.exp(jnp.minimum(gc_col - gc_col.T, 0.0)), 0.0)
    egc = jnp.exp(gcb)
    kb = ks * beta
    m = jnp.where(strict, _dot3(kb, ks, NT) * decay, 0.0)
    t = _inv_unit_lower(m)
    u = _bdot(t, vs * beta, NN)
    w = _bdot(t, kb * egc, NN)
    a = _bdot(qs, ks, NT) * decay
    rows = lambda x, h: x[h * c:(h + 1) * c]
    qe = qs * egc
    v_new = u - jnp.concatenate([_bdot(rows(w, h), states[h], NN) for h in range(nh)], axis=0)
    o = jnp.concatenate([_bdot(rows(qe, h), states[h], NN) for h in range(nh)], axis=0) + _bdot(a, v_new, NN)
    new_states = []
    for h in range(nh):
        gl = gcb[(h + 1) * c - 1:(h + 1) * c, :]
        k_dec = rows(ks, h) * jnp.exp(gl - rows(gcb, h))
        new_states.append(states[h] * jnp.exp(gl) + _bdot(k_dec, rows(v_new, h), TN))
    return _rows_to_heads(o, nh), new_states


def _gdn_fwd(qkv, p0, alog_v, dtb_v, pad, *, name, gather=None):
    lp = qkv.shape[0]
    n = lp // CHUNK
    nh = GDN_HEADS
    assert n % GDN_STEP == 0
    steps, rows = n // GDN_STEP, GDN_STEP * CHUNK
    g_srcs, g_dtypes = gather if gather is not None else ([], [])
    ng_arr = len(g_srcs)

    def body(q_ref, k_ref, v_ref, ba_ref, al_ref, dt_ref, *rest):
        g_ins, (o_ref, st_ref) = rest[:ng_arr], rest[ng_arr:ng_arr + 2]
        g_outs, s_ref, g_scratch = rest[ng_arr + 2:2 * ng_arr + 2], rest[2 * ng_arr + 2], rest[2 * ng_arr + 3:]
        i = pl.program_id(0)
        if ng_arr:
            g_start, g_forward, g_finish = _gather_phases(g_ins, g_outs, g_scratch[:ng_arr], *g_scratch[ng_arr:],
                                                          g_dtypes)
            pl.when(i == 0)(g_start)
            pl.when(i == (3 * steps) // 4)(g_forward)

        @pl.when(i == 0)
        def _():
            s_ref[...] = jnp.zeros_like(s_ref)

        s = s_ref[...]
        s = [s[h] for h in range(nh)]
        q, k, v, ba, al, dt = q_ref[...], k_ref[...], v_ref[...], ba_ref[...], al_ref[...], dt_ref[...]
        outs = []
        for c in range(GDN_STEP):
            sl = slice(c * CHUNK, (c + 1) * CHUNK)
            valid = (i * rows + c * CHUNK + lax.broadcasted_iota(jnp.int32, (CHUNK, 1), 0)) >= pad
            for h in range(nh):
                st_ref[c, h] = s[h]
            o, s = _gdn_chunk(q[sl], k[sl], v[sl], ba[sl], al, dt, s, valid)
            outs.append(o)
        o_ref[...] = jnp.concatenate(outs, axis=0)
        for h in range(nh):
            s_ref[h] = s[h]
        if ng_arr:
            pl.when(i == steps - 1)(g_finish)

    w = nh * HEAD_W
    vec = pl.BlockSpec((1, HEAD_W), lambda i: (0, 0))
    return pl.pallas_call(
        body, name=name, grid=(steps,),
        in_specs=[pl.BlockSpec((rows, w), lambda i: (i, 0)), pl.BlockSpec((rows, w), lambda i: (i, 1)),
                  pl.BlockSpec((rows, w), lambda i: (i, 2)), pl.BlockSpec((rows, HEAD_W), lambda i: (i, AB_BA // HEAD_W)),
                  vec, vec] + [pl.BlockSpec(memory_space=pltpu.VMEM)] * ng_arr,
        out_specs=[pl.BlockSpec((rows, w), lambda i: (i, 0)),
                   pl.BlockSpec((GDN_STEP, nh, HEAD_W, HEAD_W), lambda i: (i, 0, 0, 0))] + [_ANY] * ng_arr,
        out_shape=[jax.ShapeDtypeStruct((lp, w), F32), jax.ShapeDtypeStruct((n, nh, HEAD_W, HEAD_W), F32)]
        + _gather_out_shapes(g_srcs, g_dtypes),
        scratch_shapes=[pltpu.VMEM((nh, HEAD_W, HEAD_W), F32)] + (_gather_scratch(g_srcs, g_dtypes) if ng_arr else []),
        compiler_params=_cp(("arbitrary",), has_side_effects=bool(ng_arr)),
    )(qkv, qkv, qkv, p0, alog_v, dtb_v, *g_srcs)


def _gdn_bwd(qkv, p0, alog_v, dtb_v, states, do, pad, *, name, scatter=()):
    lp = qkv.shape[0]
    n = lp // CHUNK
    nh = GDN_HEADS
    assert n % GDN_STEP == 0
    steps, rows = n // GDN_STEP, GDN_STEP * CHUNK
    ns = len(scatter)

    def body(q_ref, k_ref, v_ref, ba_ref, al_ref, dt_ref, st_ref, do_ref, *rest):
        s_ins, (dq_ref, dk_ref, dv_ref, dba_ref, dal_ref, ddt_ref) = rest[:ns], rest[ns:ns + 6]
        s_outs, ds_ref, s_sems = rest[ns + 6:2 * ns + 6], rest[2 * ns + 6], rest[2 * ns + 7:]
        step = pl.program_id(0)
        i = steps - 1 - step
        if ns:
            s_start, s_finish = _scatter_phases(s_ins, s_outs, *s_sems)
            pl.when(step == 0)(s_start)

        @pl.when(step == 0)
        def _():
            ds_ref[...] = jnp.zeros_like(ds_ref)
            dal_ref[...] = jnp.zeros_like(dal_ref)
            ddt_ref[...] = jnp.zeros_like(ddt_ref)

        q, k, v, ba, al, dt = q_ref[...], k_ref[...], v_ref[...], ba_ref[...], al_ref[...], dt_ref[...]
        st, do, dst = st_ref[...], do_ref[...], ds_ref[...]
        vjps = []
        for c in range(GDN_STEP):
            sl = slice(c * CHUNK, (c + 1) * CHUNK)
            valid = (i * rows + c * CHUNK + lax.broadcasted_iota(jnp.int32, (CHUNK, 1), 0)) >= pad
            fn = functools.partial(_gdn_chunk, valid=valid)
            vjps.append(jax.vjp(fn, q[sl], k[sl], v[sl], ba[sl], al, dt, [st[c, h] for h in range(nh)])[1])
        ds = [dst[h] for h in range(nh)]
        grads = [None] * GDN_STEP
        for c in reversed(range(GDN_STEP)):
            grads[c] = vjps[c]((do[c * CHUNK:(c + 1) * CHUNK], ds))
            ds = grads[c][6]
        for j, ref in enumerate((dq_ref, dk_ref, dv_ref, dba_ref)):
            ref[...] = jnp.concatenate([gr[j] for gr in grads], axis=0)
        dal_ref[...] += sum(gr[4] for gr in grads)
        ddt_ref[...] += sum(gr[5] for gr in grads)
        for h in range(nh):
            ds_ref[h] = ds[h]
        if ns:
            pl.when(step == steps - 1)(s_finish)

    w = nh * HEAD_W
    rev = lambda c: (lambda s: (steps - 1 - s, c))
    vec = pl.BlockSpec((1, HEAD_W), lambda s: (0, 0))
    return pl.pallas_call(
        body, name=name, grid=(steps,),
        in_specs=[pl.BlockSpec((rows, w), rev(0)), pl.BlockSpec((rows, w), rev(1)), pl.BlockSpec((rows, w), rev(2)),
                  pl.BlockSpec((rows, HEAD_W), rev(AB_BA // HEAD_W)), vec, vec,
                  pl.BlockSpec((GDN_STEP, nh, HEAD_W, HEAD_W), lambda s: (steps - 1 - s, 0, 0, 0)),
                  pl.BlockSpec((rows, w), rev(0))] + [_ANY] * ns,
        out_specs=[pl.BlockSpec((rows, w), rev(0)), pl.BlockSpec((rows, w), rev(0)), pl.BlockSpec((rows, w), rev(0)),
                   pl.BlockSpec((rows, HEAD_W), rev(0)), vec, vec] + [_ANY] * ns,
        out_shape=[jax.ShapeDtypeStruct((lp, w), F32)] * 3 + [jax.ShapeDtypeStruct((lp, HEAD_W), F32)]
        + [jax.ShapeDtypeStruct((1, HEAD_W), F32)] * 2 + [jax.ShapeDtypeStruct(s.shape, s.dtype) for s in scatter],
        scratch_shapes=[pltpu.VMEM((nh, HEAD_W, HEAD_W), F32)] + (_scatter_scratch(ns) if ns else []),
        compiler_params=_cp(("arbitrary",), has_side_effects=bool(ns)),
    )(qkv, qkv, qkv, p0, alog_v, dtb_v, states, do, *scatter)


HG_LEVELS = (32, 16, 8, 4, 2, 1)
HG_GROUP = 4
HG_STEP = 3


def _hg_masks():
    import numpy as np
    c = CHUNK
    t = np.arange(c)[:, None]
    j = np.arange(c)[None, :]
    sums = (j <= t).astype(np.float32)
    pairs = [j == t]
    for m in HG_LEVELS:
        p = (t // (2 * m)) * (2 * m)
        r = p + m
        pairs.append((t >= r) & (j < r) & (j >= p))
    pairs = np.concatenate([np.kron(np.eye(HG_GROUP), p) for p in pairs], axis=0).astype(np.float32)
    return jnp.asarray(sums, BF16), jnp.asarray(sums.T, BF16), jnp.asarray(pairs, F32)


def _hg_level_row(b, m):
    c, w = b.shape
    if m >= 8:
        return jnp.concatenate([jnp.broadcast_to(b[p + m:p + m + 1], (2 * m, w)) for p in range(0, c, 2 * m)], axis=0)
    tiles = b.reshape(c // 8, 8, w)
    sub = lax.broadcasted_iota(jnp.int32, (1, 8, 1), 1)
    out = None
    for r0 in range(m, 8, 2 * m):
        cand = jnp.broadcast_to(tiles[:, r0:r0 + 1, :], tiles.shape)
        out = cand if out is None else jnp.where(sub >= r0 - m, cand, out)
    return out.reshape(c, w)


def _split3(x):
    hi = x.astype(BF16)
    r1 = x - hi.astype(F32)
    mid = r1.astype(BF16)
    return hi, mid, (r1 - mid.astype(F32)).astype(BF16)


def _dot3_raw(a, b, dims):
    ah, am, _ = _split3(a)
    bh, bm, _ = _split3(b)
    return _dot(ah, bh, dims) + (_dot(ah, bm, dims) + _dot(am, bh, dims))


@functools.partial(jax.custom_vjp, nondiff_argnums=(2,))
def _dot3(a, b, dims):
    return _dot3_raw(a, b, dims)


def _dot3_fwd(a, b, dims):
    return _dot3_raw(a, b, dims), (a, b)


def _dot3_bwd(dims, res, g):
    a, b = res
    if dims == NN:
        return _dot3_raw(g, b, NT), _dot3_raw(a, g, TN)
    return _dot3_raw(g, b, NN), _dot3_raw(g, a, TN)


_dot3.defvjp(_dot3_fwd, _dot3_bwd)


def _mask_mm_raw(m, x):
    return sum(_dot(m, part, NN) for part in _split3(x))


@jax.custom_vjp
def _mask_mm(m, mt, x):
    return _mask_mm_raw(m, x)


def _mask_mm_fwd(m, mt, x):
    return _mask_mm_raw(m, x), (m, mt)


def _mask_mm_bwd(res, g):
    m, mt = res
    return jnp.zeros_like(m), jnp.zeros_like(mt), _mask_mm_raw(mt, g)


_mask_mm.defvjp(_mask_mm_fwd, _mask_mm_bwd)


def _hg_chunk(qr, fr, ir, lb, states, valid, sums, sums_t, pairs):
    nh = HG_GROUP
    c = qr.shape[0]
    r = nh * c
    fg = lb + (1.0 - lb) * _sigmoid(fr)
    logf = jnp.where(valid, jnp.log(fg), 0.0)
    k = jnp.where(valid, 1.0 - fg, 0.0)
    qs = jnp.where(valid, _silu(qr), 0.0)
    v = jnp.where(valid, ir, 0.0)
    b = _mask_mm(sums, sums_t, logf)
    mask = lambda n: pairs[n * r:(n + 1) * r]
    stack = lambda x: _heads_to_rows(x, nh)
    a = mask(0) * _bdot(stack(qs), stack(k), NT)
    for lvl, m in enumerate(HG_LEVELS):
        d = b - _hg_level_row(b, m)
        a = a + mask(1 + lvl) * _bdot(stack(qs * jnp.exp(jnp.minimum(d, 0.0))),
                                      stack(k * jnp.exp(jnp.minimum(-d, 0.0))), NT)
    av = _bdot(a, stack(v), NN)
    eb = jnp.exp(b)
    qe, kd = qs * eb, k * jnp.exp(b[c - 1:c] - b)
    outs, new_states = [], []
    for h in range(nh):
        cs = slice(h * HEAD_W, (h + 1) * HEAD_W)
        outs.append(_bdot(qe[:, cs], states[h], NT) + av[h * c:(h + 1) * c])
        new_states.append(states[h] * eb[c - 1:c, cs] + _bdot(v[:, cs], kd[:, cs], TN))
    return jnp.concatenate(outs, axis=1), new_states


def _hg_fwd(p1, lb, pad, *, name, gather=None):
    lp = p1.shape[0]
    n = lp // CHUNK
    nh = HG_HEADS
    g_srcs, g_dtypes = gather if gather is not None else ([], [])
    ng_arr = len(g_srcs)

    def body(q_ref, f_ref, i_ref, lb_ref, sums_ref, sums_t_ref, pairs_ref, *rest):
        g_ins, (o_ref, st_ref) = rest[:ng_arr], rest[ng_arr:ng_arr + 2]
        g_outs, s_ref, g_scratch = rest[ng_arr + 2:2 * ng_arr + 2], rest[2 * ng_arr + 2], rest[2 * ng_arr + 3:]
        i = pl.program_id(1)
        if ng_arr:
            g_start, g_forward, g_finish = _gather_phases(g_ins, g_outs, g_scratch[:ng_arr], *g_scratch[ng_arr:],
                                                          g_dtypes)
            last_group = pl.program_id(0) == ngrp - 1
            pl.when((pl.program_id(0) == 0) & (i == 0))(g_start)
            pl.when(last_group & (i == 0))(g_forward)

        @pl.when(i == 0)
        def _():
            s_ref[...] = jnp.zeros_like(s_ref)

        s = s_ref[...]
        s = [s[h] for h in range(grp)]
        q, f, iv, lbv = q_ref[...], f_ref[...], i_ref[...], lb_ref[...]
        masks_v = (sums_ref[...], sums_t_ref[...], pairs_ref[...])
        outs = []
        for c in range(HG_STEP):
            sl = slice(c * CHUNK, (c + 1) * CHUNK)
            valid = (i * rows + c * CHUNK + lax.broadcasted_iota(jnp.int32, (CHUNK, 1), 0)) >= pad
            for h in range(grp):
                st_ref[h, c] = s[h]
            o, s = _hg_chunk(q[sl], f[sl], iv[sl], lbv, s, valid, *masks_v)
            outs.append(o)
        o_ref[...] = jnp.concatenate(outs, axis=0)
        for h in range(grp):
            s_ref[h] = s[h]
        if ng_arr:
            pl.when(last_group & (i == steps - 1))(g_finish)

    masks = _hg_masks()
    grp, ngrp, gw = HG_GROUP, nh // HG_GROUP, HG_GROUP * HEAD_W
    assert n % HG_STEP == 0
    steps, rows = n // HG_STEP, HG_STEP * CHUNK
    blk = lambda off: pl.BlockSpec((rows, gw), lambda h, i: (i, off + h))
    const = lambda a: pl.BlockSpec(a.shape, lambda h, i: (0, 0))
    return pl.pallas_call(
        body, name=name, grid=(ngrp, steps),
        in_specs=[blk(0), blk(ngrp), blk(2 * ngrp), pl.BlockSpec((1, gw), lambda h, i: (0, h))]
        + [const(a) for a in masks] + [pl.BlockSpec(memory_space=pltpu.VMEM)] * ng_arr,
        out_specs=[blk(0), pl.BlockSpec((grp, HG_STEP, HEAD_W, HEAD_W), lambda h, i: (h, i, 0, 0))] + [_ANY] * ng_arr,
        out_shape=[jax.ShapeDtypeStruct((lp, nh * HEAD_W), F32), jax.ShapeDtypeStruct((nh, n, HEAD_W, HEAD_W), F32)]
        + _gather_out_shapes(g_srcs, g_dtypes),
        scratch_shapes=[pltpu.VMEM((grp, HEAD_W, HEAD_W), F32)] + (_gather_scratch(g_srcs, g_dtypes) if ng_arr else []),
        compiler_params=_cp(("arbitrary", "arbitrary"), has_side_effects=bool(ng_arr)),
    )(p1, p1, p1, lb, *masks, *g_srcs)


def _hg_bwd(p1, lb, states, do, pad, *, name, scatter=()):
    lp = p1.shape[0]
    n = lp // CHUNK
    nh = HG_HEADS
    ns = len(scatter)

    def body(q_ref, f_ref, i_ref, lb_ref, st_ref, do_ref, sums_ref, sums_t_ref, pairs_ref, *rest):
        s_ins, (dq_ref, df_ref, di_ref, dlb_ref) = rest[:ns], rest[ns:ns + 4]
        s_outs, ds_ref, s_sems = rest[ns + 4:2 * ns + 4], rest[2 * ns + 4], rest[2 * ns + 5:]
        step = pl.program_id(1)
        i = steps - 1 - step
        if ns:
            s_start, s_finish = _scatter_phases(s_ins, s_outs, *s_sems)
            pl.when((pl.program_id(0) == 0) & (step == 0))(s_start)

        @pl.when(step == 0)
        def _():
            ds_ref[...] = jnp.zeros_like(ds_ref)
            dlb_ref[...] = jnp.zeros_like(dlb_ref)

        q, f, iv, lbv, st, do, dst = q_ref[...], f_ref[...], i_ref[...], lb_ref[...], st_ref[...], do_ref[...], ds_ref[...]
        masks_v = dict(sums=sums_ref[...], sums_t=sums_t_ref[...], pairs=pairs_ref[...])
        vjps = []
        for c in range(HG_STEP):
            sl = slice(c * CHUNK, (c + 1) * CHUNK)
            valid = (i * rows + c * CHUNK + lax.broadcasted_iota(jnp.int32, (CHUNK, 1), 0)) >= pad
            fn = functools.partial(_hg_chunk, valid=valid, **masks_v)
            vjps.append(jax.vjp(fn, q[sl], f[sl], iv[sl], lbv, [st[h, c] for h in range(grp)])[1])
        ds = [dst[h] for h in range(grp)]
        grads = [None] * HG_STEP
        for c in reversed(range(HG_STEP)):
            grads[c] = vjps[c]((do[c * CHUNK:(c + 1) * CHUNK], ds))
            ds = grads[c][4]
        for j, ref in enumerate((dq_ref, df_ref, di_ref)):
            ref[...] = jnp.concatenate([gr[j] for gr in grads], axis=0)
        dlb_ref[...] += sum(gr[3] for gr in grads)
        for h in range(grp):
            ds_ref[h] = ds[h]
        if ns:
            pl.when((pl.program_id(0) == ngrp - 1) & (step == steps - 1))(s_finish)

    masks = _hg_masks()
    grp, ngrp, gw = HG_GROUP, nh // HG_GROUP, HG_GROUP * HEAD_W
    assert n % HG_STEP == 0
    steps, rows = n // HG_STEP, HG_STEP * CHUNK
    blk = lambda off: pl.BlockSpec((rows, gw), lambda h, s: (steps - 1 - s, off + h))
    const = lambda a: pl.BlockSpec(a.shape, lambda h, s: (0, 0))
    w = nh * HEAD_W
    return pl.pallas_call(
        body, name=name, grid=(ngrp, steps),
        in_specs=[blk(0), blk(ngrp), blk(2 * ngrp), pl.BlockSpec((1, gw), lambda h, s: (0, h)),
                  pl.BlockSpec((grp, HG_STEP, HEAD_W, HEAD_W), lambda h, s: (h, steps - 1 - s, 0, 0)), blk(0)]
        + [const(a) for a in masks] + [_ANY] * ns,
        out_specs=[blk(0), blk(0), blk(0), pl.BlockSpec((1, gw), lambda h, s: (0, h))] + [_ANY] * ns,
        out_shape=[jax.ShapeDtypeStruct((lp, w), F32)] * 3 + [jax.ShapeDtypeStruct((1, w), F32)]
        + [jax.ShapeDtypeStruct(s.shape, s.dtype) for s in scatter],
        scratch_shapes=[pltpu.VMEM((grp, HEAD_W, HEAD_W), F32)] + (_scatter_scratch(ns) if ns else []),
        compiler_params=_cp(("arbitrary", "arbitrary"), has_side_effects=bool(ns)),
    )(p1, p1, p1, lb, states, do, *masks, *scatter)


SB_GROUP = 4
SB_FAR = -110.0


def _sb_cat(kind, first_key=0):
    r = lax.broadcasted_iota(jnp.int32, (SB_BLOCK, 2 * SB_BLOCK), 0)
    c = lax.broadcasted_iota(jnp.int32, (SB_BLOCK, 2 * SB_BLOCK), 1)
    tri = {"after": c < r, "incl": r <= c, "before": r < c}[kind]
    m = ((c >= SB_BLOCK) | tri) & (r >= first_key)
    return jnp.where(m, 1.0, 0.0).astype(BF16)


def _sb_cumsum(x, cat):
    return _dot(x.astype(BF16), cat, NN)


def _sb_logsig(z):
    e = jnp.exp(-jnp.abs(z))
    lse = jnp.where(e < 1e-4, e, jnp.log(1.0 + e))
    lsz = jnp.minimum(z, 0.0) - lse
    return lsz, lsz - z, e


def _sb_stack(x, scale=None):
    lane = lax.broadcasted_iota(jnp.int32, (1, HEAD_W), 1)
    if scale is not None:
        x = x * scale
    return jnp.concatenate([jnp.where(lane < SB_DH, x, 0.0), jnp.where(lane >= SB_DH, x, 0.0)], axis=0).astype(BF16)


def _sb_unstack(x):
    lane = lax.broadcasted_iota(jnp.int32, (1, HEAD_W), 1)
    return jnp.where(lane < SB_DH, x[:SB_BLOCK], x[SB_BLOCK:])


def _sb_fwd(p0, pad, *, name, gather=None):
    lp = p0.shape[0]
    nb = lp // SB_BLOCK
    npair = SB_HEADS // 2
    blk0 = AB_SB // HEAD_W
    scale = SB_DH ** -0.5
    gw = SB_GROUP * SB_BLOCK
    assert pad < SB_BLOCK
    g_srcs, g_dtypes = gather if gather is not None else ([], [])
    ng_arr = len(g_srcs)

    def body(q_ref, k_ref, v_ref, *rest):
        g_ins, (o_ref, tot_ref, nproc_ref) = rest[:ng_arr], rest[ng_arr:ng_arr + 3]
        g_outs, g_scratch = rest[ng_arr + 3:2 * ng_arr + 3], rest[2 * ng_arr + 3:]
        first_step = (pl.program_id(0) == 0) & (pl.program_id(1) == 0)
        last_pair = pl.program_id(0) == npair - 1
        if ng_arr:
            g_start, g_forward, g_finish = _gather_phases(g_ins, g_outs, g_scratch[:ng_arr], *g_scratch[ng_arr:],
                                                          g_dtypes)
            pl.when(first_step)(g_start)
            pl.when(last_pair & (pl.program_id(1) == 0))(g_forward)
        i = pl.program_id(1)
        qs = _sb_stack(q_ref[...], scale)
        qpos = i * SB_BLOCK + lax.broadcasted_iota(jnp.int32, (SB_BLOCK, 1), 0)
        qpos = jnp.concatenate([qpos, qpos], axis=0)
        cat = _sb_cat("after")
        cat0 = _sb_cat("after", pad)
        ng = i // SB_GROUP

        def group(off, nblk, first_cat, allowed, carry):
            acc, run = carry
            kg = k_ref[pl.ds(off, nblk * SB_BLOCK), :].astype(BF16)
            vg = v_ref[pl.ds(off, nblk * SB_BLOCK), :].astype(BF16)
            lsz, l1m, _ = _sb_logsig(_dot(qs, kg, NT))
            if allowed is not None:
                l1m = jnp.where(allowed, l1m, 0.0)
            args = [None] * nblk
            for g in reversed(range(nblk)):
                sl = slice(g * SB_BLOCK, (g + 1) * SB_BLOCK)
                al = _sb_cumsum(l1m[:, sl], first_cat if g == 0 else cat)
                args[g] = lsz[:, sl] + al[:, :SB_BLOCK] + run
                run = run + al[:, SB_BLOCK:]
            wgt = jnp.exp(jnp.concatenate(args, axis=1))
            if allowed is not None:
                wgt = jnp.where(allowed, wgt, 0.0)
            return acc + _dot(wgt.astype(BF16), vg, NN), run

        def below(t, carry):
            gi = ng - 1 - t
            return group(pl.multiple_of(gi * gw, gw), SB_GROUP, jnp.where(gi == 0, cat0, cat), None, carry)

        top = ng * gw

        def top_group(nblk, carry):
            off = pl.multiple_of(jnp.minimum(top, lp - nblk * SB_BLOCK), SB_BLOCK)
            kpos = off + lax.broadcasted_iota(jnp.int32, (1, nblk * SB_BLOCK), 1)
            return group(off, nblk, cat, (kpos < qpos) & (kpos >= pad) & (kpos >= top), carry)

        zero = (jnp.zeros((2 * SB_BLOCK, HEAD_W), F32), jnp.zeros((2 * SB_BLOCK, HEAD_W), F32))
        carry = lax.cond(i - ng * SB_GROUP < SB_GROUP // 2, functools.partial(top_group, SB_GROUP // 2),
                         functools.partial(top_group, SB_GROUP), zero)
        used, acc, run = lax.while_loop(lambda s: (s[0] < ng) & (jnp.max(s[2]) > SB_FAR),
                                        lambda s: (s[0] + 1, *below(s[0], (s[1], s[2]))), (jnp.int32(0), *carry))
        o_ref[...] = _sb_unstack(acc)
        tot_ref[...] = _sb_unstack(run)
        nproc_ref[pl.program_id(0), i] = used.astype(F32)
        if ng_arr:
            pl.when(last_pair & (pl.program_id(1) == nb - 1))(g_finish)

    full = lambda c0: pl.BlockSpec((lp, HEAD_W), lambda p, i: (0, c0 + p))
    out = pl.BlockSpec((SB_BLOCK, HEAD_W), lambda p, i: (i, p))
    return pl.pallas_call(
        body, name=name, grid=(npair, nb),
        in_specs=[pl.BlockSpec((SB_BLOCK, HEAD_W), lambda p, i: (i, blk0 + p)), full(blk0 + npair), full(blk0 + 2 * npair)]
        + [pl.BlockSpec(memory_space=pltpu.VMEM)] * ng_arr,
        out_specs=[out, out, pl.BlockSpec(memory_space=pltpu.SMEM)] + [_ANY] * ng_arr,
        out_shape=[jax.ShapeDtypeStruct((lp, npair * HEAD_W), F32)] * 2 + [jax.ShapeDtypeStruct((npair, nb), F32)]
        + _gather_out_shapes(g_srcs, g_dtypes),
        scratch_shapes=_gather_scratch(g_srcs, g_dtypes) if ng_arr else [],
        compiler_params=_cp(("arbitrary", "arbitrary"), has_side_effects=bool(ng_arr)),
    )(p0, p0, p0, *g_srcs)


def _sb_bwd(p0, tot, nproc, dsrc, d_blk0, pad, *, name, scatter=()):
    lp = p0.shape[0]
    nb = lp // SB_BLOCK
    npair = SB_HEADS // 2
    blk0 = AB_SB // HEAD_W
    scale = SB_DH ** -0.5
    gw = SB_GROUP * SB_BLOCK
    assert pad < SB_BLOCK
    ns = len(scatter)

    def body(q_ref, k_ref, v_ref, tot_ref, nproc_ref, do_ref, *rest):
        s_ins, (dq_ref, dkt_ref, dvt_ref) = rest[:ns], rest[ns:ns + 3]
        s_outs, s_sems = rest[ns + 3:2 * ns + 3], rest[2 * ns + 3:]
        if ns:
            s_start, s_finish = _scatter_phases(s_ins, s_outs, *s_sems)
            pl.when((pl.program_id(0) == 0) & (pl.program_id(1) == 0))(s_start)
        i = pl.program_id(1)

        @pl.when(i == 0)
        def _():
            dkt_ref[...] = jnp.zeros_like(dkt_ref)
            dvt_ref[...] = jnp.zeros_like(dvt_ref)

        qs = _sb_stack(q_ref[...], scale)
        dos = _sb_stack(do_ref[...])
        qst, dost = qs.T, dos.T
        totv = tot_ref[...]
        ones = jnp.ones((1, HEAD_W), F32)
        tots = jnp.concatenate([totv[:, 0:1] * ones, totv[:, SB_DH:SB_DH + 1] * ones], axis=0)
        qpos = i * SB_BLOCK + lax.broadcasted_iota(jnp.int32, (SB_BLOCK, 1), 0)
        qpos = jnp.concatenate([qpos, qpos], axis=0)
        incl, incl0 = _sb_cat("incl"), _sb_cat("incl", pad)
        before = _sb_cat("before")
        ng = i // SB_GROUP
        used = jnp.clip(nproc_ref[pl.program_id(0), i].astype(jnp.int32), 0, ng)

        def dscore(z, e, ev, dl1m):
            r = 1.0 / (1.0 + e)
            sg = jnp.where(z >= 0, r, e * r)
            return ev * (1.0 - sg) - dl1m * sg

        def group(off, nblk, first_incl, allowed, carry):
            dq, prun, erun = carry
            width = nblk * SB_BLOCK
            kg = k_ref[pl.ds(off, width), :].astype(BF16)
            vg = v_ref[pl.ds(off, width), :].astype(BF16)
            z = _dot(qs, kg, NT)
            lsz, l1m, e = _sb_logsig(z)
            if allowed is not None:
                l1m = jnp.where(allowed, l1m, 0.0)
            dwgt = _dot(dos, vg, NT)
            dzs = [None] * nblk
            wgts = [None] * nblk
            for g in range(nblk):
                sl = slice(g * SB_BLOCK, (g + 1) * SB_BLOCK)
                al = _sb_cumsum(l1m[:, sl], first_incl if g == 0 else incl)
                wgt = jnp.exp(jnp.minimum(lsz[:, sl] + (tots - prun - al[:, :SB_BLOCK]), 0.0))
                if allowed is not None:
                    wgt = jnp.where(allowed[:, sl], wgt, 0.0)
                prun = prun + al[:, SB_BLOCK:]
                ev = wgt * dwgt[:, sl]
                el = _sb_cumsum(ev, before)
                dzs[g] = dscore(z[:, sl], e[:, sl], ev, erun + el[:, :SB_BLOCK])
                erun = erun + el[:, SB_BLOCK:]
                wgts[g] = wgt
            dz = jnp.concatenate(dzs, axis=1)
            if allowed is not None:
                dz = jnp.where(allowed, dz, 0.0)
            dz = dz.astype(BF16)
            wg = jnp.concatenate(wgts, axis=1).astype(BF16)
            dkt_ref[:, pl.ds(off, width)] += _dot(qst, dz, NN)
            dvt_ref[:, pl.ds(off, width)] += _dot(dost, wg, NN)
            return dq + _dot(dz, kg, NN), prun, erun

        def below(gi, carry):
            return group(pl.multiple_of(gi * gw, gw), SB_GROUP, jnp.where(gi == 0, incl0, incl), None, carry)

        zero = tuple(jnp.zeros((2 * SB_BLOCK, HEAD_W), F32) for _ in range(3))
        carry = lax.fori_loop(ng - used, ng, below, zero)
        top = ng * gw

        def top_group(nblk, carry):
            off = pl.multiple_of(jnp.minimum(top, lp - nblk * SB_BLOCK), SB_BLOCK)
            kpos = off + lax.broadcasted_iota(jnp.int32, (1, nblk * SB_BLOCK), 1)
            return group(off, nblk, incl, (kpos < qpos) & (kpos >= pad) & (kpos >= top), carry)

        dq, _, _ = lax.cond(i - ng * SB_GROUP < SB_GROUP // 2, functools.partial(top_group, SB_GROUP // 2),
                            functools.partial(top_group, SB_GROUP), carry)
        dq_ref[...] = _sb_unstack(dq) * scale
        if ns:
            pl.when((pl.program_id(0) == npair - 1) & (pl.program_id(1) == nb - 1))(s_finish)

    full = lambda c0: pl.BlockSpec((lp, HEAD_W), lambda p, i: (0, c0 + p))
    qb = lambda c0: pl.BlockSpec((SB_BLOCK, HEAD_W), lambda p, i: (i, c0 + p))
    tr = pl.BlockSpec((HEAD_W, lp), lambda p, i: (p, 0))
    return pl.pallas_call(
        body, name=name, grid=(npair, nb),
        in_specs=[qb(blk0), full(blk0 + npair), full(blk0 + 2 * npair), qb(0), pl.BlockSpec(memory_space=pltpu.SMEM),
                  qb(d_blk0)] + [_ANY] * ns,
        out_specs=[qb(0), tr, tr] + [_ANY] * ns,
        out_shape=[jax.ShapeDtypeStruct((lp, npair * HEAD_W), F32)]
        + [jax.ShapeDtypeStruct((npair * HEAD_W, lp), F32)] * 2
        + [jax.ShapeDtypeStruct(s.shape, s.dtype) for s in scatter],
        scratch_shapes=_scatter_scratch(ns) if ns else [],
        compiler_params=_cp(("arbitrary", "arbitrary"), has_side_effects=bool(ns)),
    )(p0, p0, p0, tot, nproc, dsrc, *scatter)


def _local_step(h0, target, pad, wts, hooks=None):
    lp = h0.shape[0]
    tm = _row_tile(lp, 1056)
    tkl = tm
    tml = _row_tile(lp, 528)
    d = D_MODEL
    mm = _mm
    mmw = functools.partial(_mm, out_dtype=BF16)
    k_major = lambda wd: wd.transpose(1, 0, 2).reshape(wd.shape[1], -1)
    g = {}

    h0_b = h0.astype(BF16)
    p0 = mm(h0_b, wts["w_ab"], "NN", tm=tm, tn=1280, tk=d, name="l0_in_proj")
    ob, sb_tot, sb_used, *gathered = _sb_fwd(p0, pad, name="sb_fwd", gather=hooks["gather_a"] if hooks else None)
    if hooks:
        wts = {**wts, **hooks["weights_a"](gathered)}
    qkv = _gdn_pre_fwd(p0, wts["conv_w"], pad, name="gdn_pre_fwd")
    oa_raw, gdn_states, *gathered = _gdn_fwd(qkv, p0, wts["alog_v"], wts["dtb_v"], pad, name="gdn_fwd",
                                             gather=hooks["gather_b"] if hooks else None)
    if hooks:
        wts = {**wts, **hooks["weights_b"](gathered)}
    rows = lambda a, n: a.reshape(N_DEV, n // N_DEV, d)
    parts = g["parts"] = {}
    oab = _gate_fwd(oa_raw, p0, AB_Z // HEAD_W, wts["ab_gn"], ob, heads=GDN_HEADS, name="gdn_gate_fwd")
    ln = lambda kind, layer: (wts[f"ln_{kind}_g"][layer], wts[f"ln_{kind}_b"][layer])
    pre_mix0, h0a, h0a_b = mm(oab, wts["w_out0"], "NN", tm=tm, tn=d, tk=d, epi="ln", c=h0, scale=DN_ALPHA,
                              ln=ln("mix", 0), name="l0_out_proj")
    u0, act0 = mm(h0a_b, k_major(wts["w1"][0]), "NN", tm=tm, tn=1024, tk=d, epi="relu2_copy", name="mlp0_up")
    pre_ffn0, h0b, h0b_b = mm(act0, wts["w2"][0], "NN", tm=tm, tn=d, tk=d, epi="ln", c=h0a, scale=DN_ALPHA,
                              ln=ln("ffn", 0), name="mlp0_down")
    p1 = mm(h0b_b, k_major(wts["w_c"]), "NN", tm=tm, tn=1024, tk=d, name="l1_in_proj")
    oc_raw, hg_states, *gathered = _hg_fwd(p1, wts["lb"], pad, name="hg_fwd",
                                           gather=hooks["gather_c"] if hooks else None)
    if hooks:
        third = hooks["weights_c"](gathered)
        wts = {**wts, "w1": wts["w1"] + third["w1"], "w2": wts["w2"] + third["w2"]}
    oc = _gate_fwd(oc_raw, p1, 3 * HG_HEADS, wts["c_gn"], oc_raw, heads=HG_HEADS, name="hg_gate_fwd")
    pre_mix1, h1a, h1a_b = mm(oc, wts["w_out1"], "NN", tm=tm, tn=d, tk=d, epi="ln", c=h0b, scale=DN_ALPHA,
                              ln=ln("mix", 1), name="l1_out_proj")
    u1, act1 = mm(h1a_b, k_major(wts["w1"][1]), "NN", tm=tm, tn=1024, tk=d, epi="relu2_copy", name="mlp1_up")
    pre_ffn1, h1b, _ = mm(act1, wts["w2"][1], "NN", tm=tm, tn=d, tk=d, epi="ln", c=h1a, scale=DN_ALPHA,
                          ln=ln("ffn", 1), name="mlp1_down")
    dy, loss_vec = _loss_head(h1b, target, name="loss_head")

    def mlp_bwd(layer, h_in_b, u, act, dpre, dpre_b, pre_mix):
        du = mm(dpre_b, wts["w2"][layer], "NT", tm=tm, tn=1024, tk=d, epi="relu2grad", c=u, out_dtype=BF16,
                name=f"mlp{layer}_d_hidden")
        dw2 = mmw(act, dpre_b, "TN", tm=1024, tn=1024, tk=tkl, name=f"mlp{layer}_dw2")
        dw1 = mmw(h_in_b, du, "TN", tm=1024, tn=512, tk=tkl, out_dev=True, name=f"mlp{layer}_dw1")
        return (*mm(du, k_major(wts["w1"][layer]), "NT", tm=tml, tn=1024, tk=2048, epi="ln_bwd", c=dpre, scale=DN_ALPHA,
                    ln=(pre_mix, wts["ln_mix_g"][layer]), name=f"mlp{layer}_d_in"), dw1, dw2)

    ln_ffn_dg, ln_ffn_db, ln_mix_dg, ln_mix_db, dw1s, dw2s = ([None, None] for _ in range(6))
    dpre, dpre_b, ln_ffn_dg[1], ln_ffn_db[1] = _ln_bwd(pre_ffn1, wts["ln_ffn_g"][1], dy, name="ln_ffn1_bwd")
    dpre, dpre_b, ln_mix_dg[1], ln_mix_db[1], dw1s[1], dw2s[1] = mlp_bwd(1, h1a_b, u1, act1, dpre, dpre_b, pre_mix1)
    g["c_w_out"] = mmw(oc, dpre_b, "TN", tm=1024, tn=1024, tk=tkl, name="l1_dw_out")
    doc = mm(dpre_b, wts["w_out1"], "NT", tm=tm, tn=1024, tk=d, name="l1_d_gate")
    doc_raw, dz1, g["c_gn"] = _gate_bwd(oc_raw, p1, 3 * HG_HEADS, wts["c_gn"], doc, heads=HG_HEADS, name="hg_gate_bwd")
    ready = [dw1s[1], rows(dw2s[1], D_FF), rows(g["c_w_out"], d)] if hooks else ()
    dq1, df1, di1, g["lb"], *got = _hg_bwd(p1, wts["lb"], hg_states, doc_raw, pad, name="hg_bwd", scatter=ready)
    parts.update(zip(("mlp_w1_1", "mlp_w2_1", "c_w_out"), got))
    dp1 = jnp.concatenate([dq1, df1, di1, dz1], axis=1).astype(BF16)
    g["c_w_in"] = mmw(h0b_b, dp1, "TN", tm=1024, tn=512, tk=tkl, out_dev=True, name="l1_dw_in")
    dpre, dpre_b, ln_ffn_dg[0], ln_ffn_db[0] = mm(
        dp1, k_major(wts["w_c"]), "NT", tm=tml, tn=1024, tk=2048, epi="ln_bwd", c=dpre, scale=DN_ALPHA,
        ln=(pre_ffn0, wts["ln_ffn_g"][0]), name="l1_d_in")
    dpre, dpre_b, ln_mix_dg[0], ln_mix_db[0], dw1s[0], dw2s[0] = mlp_bwd(0, h0a_b, u0, act0, dpre, dpre_b, pre_mix0)
    g["ab_w_out"] = mmw(oab, dpre_b, "TN", tm=1024, tn=1024, tk=tkl, name="l0_dw_out")
    doab = mm(dpre_b, wts["w_out0"], "NT", tm=tm, tn=1024, tk=d, name="l0_d_gate")
    doa_raw, dz0, g["ab_gn"] = _gate_bwd(oa_raw, p0, AB_Z // HEAD_W, wts["ab_gn"], doab, heads=GDN_HEADS,
                                         name="gdn_gate_bwd")
    ready = [g["c_w_in"]] if hooks else ()
    dqb, dkb_t, dvb_t, *got = _sb_bwd(p0, sb_tot, sb_used, doab, GDN_HEADS, pad, name="sb_bwd", scatter=ready)
    parts.update(zip(("c_w_in",), got))
    dkb, dvb = dkb_t.T, dvb_t.T
    ready = [dw1s[0], rows(dw2s[0], D_FF), rows(g["ab_w_out"], d)] if hooks else ()
    dqn, dkn, dvn, dba, g["alog_v"], g["dtb_v"], *got = _gdn_bwd(qkv, p0, wts["alog_v"], wts["dtb_v"], gdn_states,
                                                                 doa_raw, pad, name="gdn_bwd", scatter=ready)
    parts.update(zip(("mlp_w1_0", "mlp_w2_0", "ab_w_out"), got))
    dconv_in, g["conv_w"] = _gdn_pre_bwd(p0, wts["conv_w"], jnp.concatenate([dqn, dkn, dvn], axis=1), pad,
                                         name="gdn_pre_bwd")
    dp0 = jnp.concatenate([dconv_in, dz0, dqb, dkb, dvb, dba, jnp.zeros((lp, AB_CAT - AB_BA - HEAD_W), F32)],
                          axis=1).astype(BF16)
    g["w_ab"] = mmw(h0_b, dp0, "TN", tm=1024, tn=1280, tk=tkl, name="l0_dw_in")
    last = ()
    if hooks:
        gab, ba0 = g["w_ab"], AB_Z + GDN_HEADS * HEAD_W
        gab = jnp.concatenate([gab[:, :ba0], gab[:, AB_BA:AB_BA + 2 * GDN_HEADS], gab[:, ba0:AB_BA]], axis=1)
        last = [gab.reshape(d, N_DEV, AB_IN // N_DEV).transpose(1, 0, 2)]
    res = mm(dp0, wts["w_ab"], "NT", tm=tm, tn=1024, tk=1920, epi="add", c=dpre, scale=DN_ALPHA, scatter=last,
             name="l0_d_in")
    dh0 = res[0] if last else res
    parts.update(zip(("ab_w_in",), res[1:] if last else ()))

    g["w1"], g["w2"] = dw1s, dw2s
    g["ln_mix_g"] = jnp.concatenate(ln_mix_dg, axis=0)
    g["ln_mix_b"] = jnp.concatenate(ln_mix_db, axis=0)
    g["ln_ffn_g"] = jnp.concatenate(ln_ffn_dg, axis=0)
    g["ln_ffn_b"] = jnp.concatenate(ln_ffn_db, axis=0)
    return loss_vec, dh0, g


N_CHIP = N_DEV // 2


def _place():
    x, y, c = lax.axis_index("x"), lax.axis_index("y"), lax.axis_index("c")
    return x, y, c, 2 * x + y


def _chip_dev(chip, core):
    return (chip // 2, chip % 2, core)


def _remote(src, dst, send_sem, recv_sem, dev):
    return pltpu.make_async_remote_copy(src_ref=src, dst_ref=dst, send_sem=send_sem, recv_sem=recv_sem,
                                        device_id=dev, device_id_type=pl.DeviceIdType.MESH)


_ANY = pl.BlockSpec(memory_space=pl.ANY)


def _gather(srcs, dtypes, *, name):
    n = len(srcs)

    def body(*refs):
        start, forward, finish = _gather_phases(refs[:n], refs[n:2 * n], refs[2 * n:3 * n], *refs[3 * n:], dtypes)
        start()
        forward()
        finish()

    return pl.pallas_call(
        body, name=name, in_specs=[pl.BlockSpec(memory_space=pltpu.VMEM)] * n, out_specs=[_ANY] * n,
        out_shape=_gather_out_shapes(srcs, dtypes), scratch_shapes=_gather_scratch(srcs, dtypes),
        compiler_params=_cp(has_side_effects=True),
    )(*srcs)


def _gather_out_shapes(srcs, dtypes):
    return [jax.ShapeDtypeStruct((N_DEV, *s.shape), dt) for s, dt in zip(srcs, dtypes)]


def _gather_scratch(srcs, dtypes):
    n = len(srcs)
    return [pltpu.VMEM(s.shape, dt) for s, dt in zip(srcs, dtypes)] + [
        pltpu.SemaphoreType.DMA((n, 2 * N_CHIP - 1)), pltpu.SemaphoreType.DMA((n, 2 * N_CHIP - 1)),
        pltpu.SemaphoreType.DMA((n,))]


def _gather_phases(ins, outs, stages, send_sems, recv_sems, local_sems, dtypes):
    n = len(ins)
    x, y, c, chip = _place()
    me = 2 * chip + c
    sibling = (x, y, 1 - c)

    def own(i):
        cps = [_remote(stages[i], outs[i].at[me], send_sems.at[i, 0], recv_sems.at[i, 0], sibling)]
        for j in range(1, N_CHIP):
            cps.append(_remote(stages[i], outs[i].at[me], send_sems.at[i, j], recv_sems.at[i, j],
                               _chip_dev(jnp.bitwise_xor(chip, j), c)))
        return cps

    def local(i):
        return pltpu.make_async_copy(stages[i], outs[i].at[me], local_sems.at[i])

    def passed_on(i, j):
        slot = outs[i].at[2 * jnp.bitwise_xor(chip, j) + c]
        return _remote(slot, slot, send_sems.at[i, N_CHIP - 1 + j], recv_sems.at[i, N_CHIP - 1 + j], sibling)

    def start():
        for i in range(n):
            stages[i][...] = ins[i][...].astype(dtypes[i])
            local(i).start()
            for cp in own(i):
                cp.start()

    def forward():
        for i in range(n):
            for j in range(1, N_CHIP):
                own(i)[j].wait_recv()
                passed_on(i, j).start()

    def finish():
        for i in range(n):
            own(i)[0].wait_recv()
            for j in range(1, N_CHIP):
                passed_on(i, j).wait_recv()
        for i in range(n):
            for cp in own(i):
                cp.wait_send()
            for j in range(1, N_CHIP):
                passed_on(i, j).wait_send()
            local(i).wait()

    return start, forward, finish


def _scatter_scratch(n):
    return [pltpu.SemaphoreType.DMA((n, N_DEV - 1)), pltpu.SemaphoreType.DMA((n, N_DEV - 1)),
            pltpu.SemaphoreType.DMA((n,))]


def _scatter_phases(ins, outs, send_sems, recv_sems, local_sems):
    n = len(ins)
    _, _, c, chip = _place()
    me = 2 * chip + c

    def copies():
        cps = []
        for i in range(n):
            cps.append(pltpu.make_async_copy(ins[i].at[me], outs[i].at[me], local_sems.at[i]))
            for k in range(1, N_DEV):
                peer = jnp.bitwise_xor(me, k)
                cps.append(_remote(ins[i].at[peer], outs[i].at[me], send_sems.at[i, k - 1], recv_sems.at[i, k - 1],
                                   _chip_dev(peer // 2, peer % 2)))
        return cps

    def start():
        for cp in copies():
            cp.start()

    def finish():
        for cp in copies():
            cp.wait()

    return start, finish


def _adamw(w, parts, m, v, *, name):
    r, c = w.shape
    s = parts.shape[0]
    tm = _row_tile(r, 128) if r % 8 == 0 else r
    c1 = 1.0 - ADAM_B1 ** ADAM_STEP
    c2 = 1.0 - ADAM_B2 ** ADAM_STEP

    def body(w_ref, p_ref, m_ref, v_ref, g_ref, d_ref, m2_ref, v2_ref):
        g = p_ref[0].astype(F32)
        for j in range(1, s):
            g = g + p_ref[j].astype(F32)
        m2 = ADAM_B1 * m_ref[...] + (1.0 - ADAM_B1) * g
        v2 = ADAM_B2 * v_ref[...] + (1.0 - ADAM_B2) * jnp.square(g)
        g_ref[...] = g
        m2_ref[...] = m2
        v2_ref[...] = v2
        d_ref[...] = -ADAM_LR * ((m2 / c1) / (jnp.sqrt(v2 / c2) + ADAM_EPS) + ADAM_WD * w_ref[...])

    blk = pl.BlockSpec((tm, c), lambda i: (i, 0))
    return pl.pallas_call(
        body, name=name, grid=(r // tm,),
        in_specs=[blk, pl.BlockSpec((s, tm, c), lambda i: (0, i, 0)), blk, blk], out_specs=[blk] * 4,
        out_shape=[jax.ShapeDtypeStruct((r, c), F32)] * 4, compiler_params=_cp(("parallel",)),
    )(w, parts, m, v)


_WEIGHTS = ("meta_tokens", "ab_w_in", "ab_conv_w", "ab_a_log", "ab_dt_bias", "ab_gnorm_g", "ab_w_out", "c_w_in",
            "c_lb_raw", "c_gnorm_g", "c_w_out", "ln_mix_g", "ln_mix_b", "mlp_w1", "mlp_w2", "ln_ffn_g", "ln_ffn_b")
_PACK_ROWS = (("ln_mix_g", 0), ("ln_mix_b", 2), ("ln_ffn_g", 4), ("ln_ffn_b", 6), ("c_lb_raw", 8))
_PACK_MISC_ROW = 10
_PACK_MISC = (("ab_gnorm_g", 0, 128), ("c_gnorm_g", 128, 128), ("ab_a_log", 256, GDN_HEADS), ("ab_dt_bias", 260, GDN_HEADS))
_PACK_N = 16
_SMALL_META = 16
_SMALL_CONV = 32
_SMALL_N = 40


def _pack_replicated(p):
    rows = jnp.zeros((_PACK_N, D_MODEL), F32)
    for name, r0 in _PACK_ROWS:
        rows = rows.at[r0:r0 + 2].set(p[name])
    for name, c0, width in _PACK_MISC:
        rows = rows.at[_PACK_MISC_ROW, c0:c0 + width].set(p[name].reshape(width))
    return rows


def _unpack_replicated(rows, like):
    out = {}
    for name, r0 in _PACK_ROWS:
        out[name] = rows[r0:r0 + 2]
    for name, c0, width in _PACK_MISC:
        out[name] = rows[_PACK_MISC_ROW, c0:c0 + width].reshape(like[name].shape)
    return out


def _lower_bound(c_lb_raw):
    lb_all = jnp.cumsum(jax.nn.softmax(c_lb_raw.astype(F32), axis=0), axis=0)
    return (lb_all - lb_all[0:1])[1].reshape(1, -1)


def kernel(x, meta_tokens, ab_w_in, ab_conv_w, ab_a_log, ab_dt_bias, ab_gnorm_g, ab_w_out, c_w_in, c_lb_raw, c_gnorm_g, c_w_out, ln_mix_g, ln_mix_b, mlp_w1, mlp_w2, ln_ffn_g, ln_ffn_b, loss_target, m_meta_tokens, m_ab_w_in, m_ab_conv_w, m_ab_a_log, m_ab_dt_bias, m_ab_gnorm_g, m_ab_w_out, m_c_w_in, m_c_lb_raw, m_c_gnorm_g, m_c_w_out, m_ln_mix_g, m_ln_mix_b, m_mlp_w1, m_mlp_w2, m_ln_ffn_g, m_ln_ffn_b, v_meta_tokens, v_ab_w_in, v_ab_conv_w, v_ab_a_log, v_ab_dt_bias, v_ab_gnorm_g, v_ab_w_out, v_c_w_in, v_c_lb_raw, v_c_gnorm_g, v_c_w_out, v_ln_mix_g, v_ln_mix_b, v_mlp_w1, v_mlp_w2, v_ln_ffn_g, v_ln_ffn_b):
    w = dict(zip(_WEIGHTS, (meta_tokens, ab_w_in, ab_conv_w, ab_a_log, ab_dt_bias, ab_gnorm_g, ab_w_out, c_w_in, c_lb_raw,
                            c_gnorm_g, c_w_out, ln_mix_g, ln_mix_b, mlp_w1, mlp_w2, ln_ffn_g, ln_ffn_b)))
    mom = dict(zip(_WEIGHTS, (m_meta_tokens, m_ab_w_in, m_ab_conv_w, m_ab_a_log, m_ab_dt_bias, m_ab_gnorm_g, m_ab_w_out,
                              m_c_w_in, m_c_lb_raw, m_c_gnorm_g, m_c_w_out, m_ln_mix_g, m_ln_mix_b, m_mlp_w1, m_mlp_w2,
                              m_ln_ffn_g, m_ln_ffn_b)))
    var = dict(zip(_WEIGHTS, (v_meta_tokens, v_ab_w_in, v_ab_conv_w, v_ab_a_log, v_ab_dt_bias, v_ab_gnorm_g, v_ab_w_out,
                              v_c_w_in, v_c_lb_raw, v_c_gnorm_g, v_c_w_out, v_ln_mix_g, v_ln_mix_b, v_mlp_w1, v_mlp_w2,
                              v_ln_ffn_g, v_ln_ffn_b)))
    me = 4 * lax.axis_index("x") + 2 * lax.axis_index("y") + lax.axis_index("c")
    seq = x.shape[1]
    pad = (-(N_META + seq)) % SB_BLOCK
    lp = pad + N_META + seq
    meta_w = D_MODEL // N_DEV
    conv_w_all = 2 * GDN_HEADS * HEAD_W + GDN_HEADS * HEAD_W
    conv_w_mine = conv_w_all // N_DEV

    g_meta, g_conv, g_ab_in = _gather([w["meta_tokens"], w["ab_conv_w"][0], w["ab_w_in"][0]], [F32, F32, BF16],
                                      name="gather_weights_first")
    meta_full = g_meta.transpose(1, 0, 2).reshape(N_META, D_MODEL)
    conv_full = g_conv.transpose(1, 0, 2).reshape(CONV_K, conv_w_all)
    ab_full = g_ab_in.transpose(1, 0, 2).reshape(D_MODEL, AB_IN)
    ba0 = AB_Z + 512
    w_ab = jnp.concatenate([ab_full[:, :ba0], ab_full[:, ba0 + 2 * GDN_HEADS:], ab_full[:, ba0:ba0 + 2 * GDN_HEADS],
                            jnp.zeros((D_MODEL, AB_CAT - AB_IN), BF16)], axis=1)
    vec128 = lambda p: jnp.zeros((1, HEAD_W), F32).at[0, :GDN_HEADS].set(p.reshape(GDN_HEADS))
    wts = dict(
        w_ab=w_ab, conv_w=conv_full, alog_v=vec128(w["ab_a_log"]), dtb_v=vec128(w["ab_dt_bias"]),
        ab_gn=w["ab_gnorm_g"][0], lb=_lower_bound(w["c_lb_raw"]), c_gn=w["c_gnorm_g"][0],
        ln_mix_g=w["ln_mix_g"], ln_mix_b=w["ln_mix_b"], ln_ffn_g=w["ln_ffn_g"], ln_ffn_b=w["ln_ffn_b"])

    def weights_a(gathered):
        g_ab_out, g_w1, g_w2 = gathered
        return dict(w_out0=g_ab_out.reshape(D_MODEL, D_MODEL), w1=[g_w1], w2=[g_w2.reshape(D_FF, D_MODEL)])

    def weights_b(gathered):
        g_c_in, g_c_out = gathered
        return dict(w_c=g_c_in, w_out1=g_c_out.reshape(D_MODEL, D_MODEL))

    def weights_c(gathered):
        g_w1, g_w2 = gathered
        return dict(w1=[g_w1], w2=[g_w2.reshape(D_FF, D_MODEL)])

    hooks = dict(
        gather_a=([w["ab_w_out"][0], w["mlp_w1"][0], w["mlp_w2"][0]], [BF16] * 3), weights_a=weights_a,
        gather_b=([w["c_w_in"][0], w["c_w_out"][0]], [BF16] * 2), weights_b=weights_b,
        gather_c=([w["mlp_w1"][1], w["mlp_w2"][1]], [BF16] * 2), weights_c=weights_c)

    h0 = jnp.concatenate([jnp.zeros((pad, D_MODEL), F32), meta_full, x[0]], axis=0)
    loss_vec, dh0, g = _local_step(h0, loss_target[0], pad, wts, hooks)
    loss = lax.psum(jnp.sum(loss_vec), ("x", "y", "c"))
    grad_x = dh0[lp - seq:][None]

    _, lb_vjp = jax.vjp(_lower_bound, w["c_lb_raw"])
    rep_part = _pack_replicated(dict(
        ln_mix_g=g["ln_mix_g"], ln_mix_b=g["ln_mix_b"], ln_ffn_g=g["ln_ffn_g"], ln_ffn_b=g["ln_ffn_b"],
        c_lb_raw=lb_vjp(g["lb"])[0], ab_gnorm_g=g["ab_gn"], c_gnorm_g=g["c_gn"],
        ab_a_log=g["alog_v"][0, :GDN_HEADS], ab_dt_bias=g["dtb_v"][0, :GDN_HEADS]))
    small = jnp.concatenate([rep_part, dh0[pad:pad + N_META], g["conv_w"].reshape(-1, D_MODEL),
                             jnp.zeros((_SMALL_N - _SMALL_CONV - CONV_K * conv_w_all // D_MODEL, D_MODEL), F32)], axis=0)
    (small_all,) = _gather([small], [F32], name="gather_small_grads")
    rep_out = _adamw(_pack_replicated(w), small_all[:, :_PACK_N], _pack_replicated(mom), _pack_replicated(var),
                     name="adamw_replicated")
    meta_parts = lax.dynamic_slice_in_dim(small_all[:, _SMALL_META:_SMALL_META + N_META], me * meta_w, meta_w, axis=2)
    meta_out = _adamw(w["meta_tokens"], meta_parts, mom["meta_tokens"], var["meta_tokens"], name="adamw_meta")
    conv_parts = small_all[:, _SMALL_CONV:_SMALL_CONV + CONV_K * conv_w_all // D_MODEL].reshape(N_DEV, CONV_K, conv_w_all)
    conv_parts = lax.dynamic_slice_in_dim(conv_parts, me * conv_w_mine, conv_w_mine, axis=2)
    conv_out = _adamw(w["ab_conv_w"][0], conv_parts, mom["ab_conv_w"][0], var["ab_conv_w"][0], name="adamw_conv")

    parts = g["parts"]
    big = [("ab_w_in", 0, parts["ab_w_in"]), ("ab_w_out", 0, parts["ab_w_out"]), ("mlp_w1", 0, parts["mlp_w1_0"]),
           ("mlp_w2", 0, parts["mlp_w2_0"]), ("c_w_in", 0, parts["c_w_in"]), ("c_w_out", 0, parts["c_w_out"]),
           ("mlp_w1", 1, parts["mlp_w1_1"]), ("mlp_w2", 1, parts["mlp_w2_1"])]
    big_out = {}
    for name, l, p in big:
        res = _adamw(w[name][l], p, mom[name][l], var[name][l], name=f"adamw_{name}{l}")
        big_out.setdefault(name, []).append(res)

    rep = [_unpack_replicated(r, w) for r in rep_out]
    outs = {}
    for name in _WEIGHTS:
        if name == "meta_tokens":
            outs[name] = list(meta_out)
        elif name == "ab_conv_w":
            outs[name] = [o[None] for o in conv_out]
        elif name in big_out:
            res = big_out[name]
            outs[name] = [o[None] for o in res[0]] if len(res) == 1 else [jnp.stack(pair) for pair in zip(*res)]
        else:
            outs[name] = [r[name] for r in rep]
    flat = [loss, grad_x]
    for kind in range(4):
        flat += [outs[name][kind] for name in _WEIGHTS]
    return tuple(flat)
```

```python
import functools

import jax
import jax.numpy as jnp
from jax import lax
from jax.experimental import pallas as pl
from jax.experimental.pallas import tpu as pltpu

F32 = jnp.float32
BF16 = jnp.bfloat16

N_DEV = 8
D_MODEL = 1024
N_META = 16
D_FF = 4096
DEPTH = 2
GDN_HEADS = 4
SB_HEADS = 8
SB_DH = 64
HG_HEADS = 8
HEAD_W = 128
CHUNK = 64
SB_BLOCK = 128
CONV_K = 4
DN_ALPHA = float((2 * DEPTH) ** 0.25)
LN_EPS = 1e-5
RMS_EPS = 1e-6
L2_EPS = 1e-6
ADAM_LR, ADAM_B1, ADAM_B2, ADAM_EPS, ADAM_WD, ADAM_STEP = 0.001, 0.9, 0.999, 1e-08, 0.01, 10

AB_Z = 1536
AB_SB = 2048
AB_BA = 3584
AB_CAT = 3840
AB_IN = 3592

VMEM_LIMIT = 56 * 1024 * 1024


def _cp(sem=None, **kw):
    if sem is not None:
        kw["dimension_semantics"] = sem
    return pltpu.CompilerParams(vmem_limit_bytes=VMEM_LIMIT, **kw)


def _row_tile(n, want):
    best = 8
    for t in range(8, min(n, want) + 1, 8):
        if n % t == 0:
            best = t
    return best


@jax.custom_vjp
def _sigmoid(x):
    e = jnp.exp(-jnp.abs(x))
    r = 1.0 / (1.0 + e)
    return jnp.where(x >= 0, r, e * r)


def _sigmoid_fwd(x):
    s = _sigmoid(x)
    return s, s


def _sigmoid_bwd(s, g):
    return (g * s * (1.0 - s),)


_sigmoid.defvjp(_sigmoid_fwd, _sigmoid_bwd)


def _log1p_exp_neg_abs(x):
    e = jnp.exp(-jnp.abs(x))
    return jnp.where(e < 1e-4, e - 0.5 * e * e, jnp.log(1.0 + e))


@jax.custom_vjp
def _softplus(x):
    return jnp.maximum(x, 0.0) + _log1p_exp_neg_abs(x)


def _softplus_fwd(x):
    return _softplus(x), x


def _softplus_bwd(x, g):
    return (g * _sigmoid(x),)


_softplus.defvjp(_softplus_fwd, _softplus_bwd)


def _silu(x):
    return x * _sigmoid(x)


def _silu_grad(x):
    s = _sigmoid(x)
    return s * (1.0 + x * (1.0 - s))


def _dot(a, b, dims, precision=None):
    return lax.dot_general(a, b, (dims, ((), ())), precision=precision, preferred_element_type=F32)


NN = ((1,), (0,))
NT = ((1,), (1,))
TN = ((0,), (0,))


def _bdot(a, b, dims):
    return _dot(a.astype(BF16), b.astype(BF16), dims)


def _layer_norm(pre, g, beta):
    mu = jnp.mean(pre, axis=-1, keepdims=True)
    xc = pre - mu
    var = jnp.mean(xc * xc, axis=-1, keepdims=True)
    return xc * lax.rsqrt(var + LN_EPS) * g + beta


def _layer_norm_bwd(pre, g, dy):
    mu = jnp.mean(pre, axis=-1, keepdims=True)
    xc = pre - mu
    rstd = lax.rsqrt(jnp.mean(xc * xc, axis=-1, keepdims=True) + LN_EPS)
    xhat = xc * rstd
    dxh = dy * g
    m1 = jnp.mean(dxh, axis=-1, keepdims=True)
    m2 = jnp.mean(dxh * xhat, axis=-1, keepdims=True)
    return (rstd * (dxh - m1 - xhat * m2), jnp.sum(dy * xhat, axis=0, keepdims=True),
            jnp.sum(dy, axis=0, keepdims=True))


def _mm(a, b, mode, *, tm, tn, tk, name, epi=None, c=None, scale=1.0, b_dev=False, out_dev=False, out_dtype=F32,
        ln=None, scatter=()):
    if mode == "NN":
        m, kk = a.shape
        n = b.shape[2] * N_DEV if b_dev else b.shape[1]
    elif mode == "NT":
        m, kk = a.shape
        n = b.shape[1] if b_dev else b.shape[0]
    else:
        kk, m = a.shape
        n = b.shape[1]
    assert m % tm == 0 and n % tn == 0 and kk % tk == 0, (name, m, n, kk, tm, tn, tk)
    nk = kk // tk
    dims = {"NN": NN, "NT": NT, "TN": TN}[mode]

    if mode == "TN":
        a_spec = pl.BlockSpec((tk, tm), lambda i, j, k: (k, i))
    else:
        a_spec = pl.BlockSpec((tm, tk), lambda i, j, k: (i, k))
    if mode == "NN":
        if b_dev:
            assert tn == b.shape[2]
            b_spec = pl.BlockSpec((None, tk, tn), lambda i, j, k: (j, k, 0))
        else:
            b_spec = pl.BlockSpec((tk, tn), lambda i, j, k: (k, j))
    elif mode == "NT":
        if b_dev:
            assert tk == b.shape[2]
            b_spec = pl.BlockSpec((None, tn, tk), lambda i, j, k: (k, j, 0))
        else:
            b_spec = pl.BlockSpec((tn, tk), lambda i, j, k: (j, k))
    else:
        b_spec = pl.BlockSpec((tk, tn), lambda i, j, k: (k, j))
    in_specs = [a_spec, b_spec]
    operands = [a, b]
    if c is not None:
        in_specs.append(pl.BlockSpec((tm, tn), lambda i, j, k: (i, j)))
        operands.append(c)
    if epi == "ln":
        assert tn == n and not out_dev
        in_specs += [pl.BlockSpec((1, n), lambda i, j, k: (0, 0))] * 2
        operands += [ln[0].reshape(1, n), ln[1].reshape(1, n)]
    elif epi == "ln_bwd":
        assert tn == n and not out_dev
        in_specs += [pl.BlockSpec((tm, tn), lambda i, j, k: (i, j)), pl.BlockSpec((1, n), lambda i, j, k: (0, 0))]
        operands += [ln[0], ln[1].reshape(1, n)]
    if out_dev:
        assert tn == n // N_DEV
        out_shape = jax.ShapeDtypeStruct((N_DEV, m, tn), out_dtype)
        out_spec = pl.BlockSpec((None, tm, tn), lambda i, j, k: (j, i, 0))
    else:
        out_shape = jax.ShapeDtypeStruct((m, n), out_dtype)
        out_spec = pl.BlockSpec((tm, tn), lambda i, j, k: (i, j))
    if epi == "ln":
        out_shape = [out_shape, out_shape, jax.ShapeDtypeStruct((m, n), BF16)]
        out_spec = [out_spec] * 3
    elif epi == "relu2_copy":
        assert not out_dev
        out_shape = [out_shape, jax.ShapeDtypeStruct((m, n), BF16)]
        out_spec = [out_spec] * 2
    elif epi == "ln_bwd":
        vec_shape, vec_spec = jax.ShapeDtypeStruct((1, n), F32), pl.BlockSpec((1, n), lambda i, j, k: (0, 0))
        out_shape = [out_shape, jax.ShapeDtypeStruct((m, n), BF16), vec_shape, vec_shape]
        out_spec = [out_spec, out_spec, vec_spec, vec_spec]
    n_out = {"ln": 3, "relu2_copy": 2, "ln_bwd": 4}.get(epi, 1)
    ns = len(scatter)
    if ns:
        in_specs += [_ANY] * ns
        operands += list(scatter)
        out_shape = (out_shape if n_out > 1 else [out_shape]) + [jax.ShapeDtypeStruct(s.shape, s.dtype) for s in scatter]
        out_spec = (out_spec if n_out > 1 else [out_spec]) + [_ANY] * ns
    n_in = len(operands)
    grid = (m // tm, n // tn, nk)

    def body(*refs):
        a_ref, b_ref = refs[0], refs[1]
        c_ref = refs[2] if c is not None else None
        o_ref = refs[n_in]
        scratch0 = n_in + n_out + ns
        acc_ref = refs[scratch0] if nk > 1 else None
        if ns:
            s_start, s_finish = _scatter_phases(refs[n_in - ns:n_in], refs[n_in + n_out:scratch0],
                                                *refs[scratch0 + (1 if nk > 1 else 0):])
            at = lambda step: functools.reduce(lambda x, y: x & y, [pl.program_id(ax) == step[ax] for ax in range(3)])
            pl.when(at((0, 0, 0)))(s_start)
        p = _dot(a_ref[...].astype(BF16), b_ref[...].astype(BF16), dims)
        first_rows = pl.program_id(0) == 0

        def finish(acc):
            if epi == "add":
                acc = acc + scale * c_ref[...]
            elif epi == "relu2grad":
                acc = acc * (2.0 * jnp.maximum(c_ref[...], 0.0))
            elif epi == "relu2_copy":
                refs[n_in + 1][...] = jnp.square(jnp.maximum(acc, 0.0)).astype(BF16)
            elif epi == "ln_bwd":
                acc, dg, db = _layer_norm_bwd(refs[3][...], refs[4][...], acc + scale * c_ref[...])
                dg_ref, db_ref = refs[n_in + 2], refs[n_in + 3]

                @pl.when(first_rows)
                def _():
                    dg_ref[...] = jnp.zeros_like(dg_ref)
                    db_ref[...] = jnp.zeros_like(db_ref)

                dg_ref[...] += dg
                db_ref[...] += db
                refs[n_in + 1][...] = acc.astype(BF16)
            elif epi == "ln":
                acc = acc + scale * c_ref[...]
                y = _layer_norm(acc, refs[3][...], refs[4][...])
                refs[n_in + 1][...] = y
                refs[n_in + 2][...] = y.astype(BF16)
            o_ref[...] = acc.astype(out_dtype)

        if nk == 1:
            finish(p)
        else:
            k = pl.program_id(2)

            @pl.when(k == 0)
            def _():
                acc_ref[...] = p

            @pl.when(k > 0)
            def _():
                acc_ref[...] += p

            @pl.when(k == nk - 1)
            def _():
                finish(acc_ref[...])

        if ns:
            pl.when(at(tuple(g - 1 for g in grid)))(s_finish)

    res = pl.pallas_call(
        body, name=name, grid=grid, in_specs=in_specs, out_specs=out_spec, out_shape=out_shape,
        scratch_shapes=([pltpu.VMEM((tm, tn), F32)] if nk > 1 else []) + (_scatter_scratch(ns) if ns else []),
        compiler_params=_cp(("arbitrary",) * 3 if ns or epi == "ln_bwd" else ("parallel", "parallel", "arbitrary"),
                            has_side_effects=bool(ns)),
    )(*operands)
    return res


def _ln_bwd(pre, g, dy, *, name):
    lp, d = pre.shape
    tm = _row_tile(lp, 512)

    def body(pre_ref, g_ref, dy_ref, dpre_ref, dpreb_ref, dg_ref, db_ref):
        dpre, dg, db = _layer_norm_bwd(pre_ref[...], g_ref[...], dy_ref[...])
        dpre_ref[...] = dpre
        dpreb_ref[...] = dpre.astype(BF16)

        @pl.when(pl.program_id(0) == 0)
        def _():
            dg_ref[...] = jnp.zeros_like(dg_ref)
            db_ref[...] = jnp.zeros_like(db_ref)

        dg_ref[...] += dg
        db_ref[...] += db

    row = pl.BlockSpec((tm, d), lambda i: (i, 0))
    vec = pl.BlockSpec((1, d), lambda i: (0, 0))
    return pl.pallas_call(
        body, name=name, grid=(lp // tm,), in_specs=[row, vec, row], out_specs=[row, row, vec, vec],
        out_shape=[jax.ShapeDtypeStruct((lp, d), F32), jax.ShapeDtypeStruct((lp, d), BF16),
                   jax.ShapeDtypeStruct((1, d), F32), jax.ShapeDtypeStruct((1, d), F32)],
        compiler_params=_cp(("arbitrary",)),
    )(pre, g.reshape(1, d), dy)


def _loss_head(y, target, *, name):
    lp, d = y.shape
    seq = target.shape[0]
    tm = SB_BLOCK
    first = (lp - seq) // tm
    assert (lp - seq) % tm == 0 and seq % tm == 0

    def body(y_ref, t_ref, dy_ref, loss_ref):
        i = pl.program_id(0)
        live = i >= first
        diff = jnp.where(live, y_ref[...] - t_ref[...], 0.0)
        dy_ref[...] = diff * (1.0 / d)

        @pl.when(i == 0)
        def _():
            loss_ref[...] = jnp.zeros_like(loss_ref)

        loss_ref[...] += jnp.sum(diff * diff, axis=0, keepdims=True) * (0.5 / d)

    return pl.pallas_call(
        body, name=name, grid=(lp // tm,),
        in_specs=[pl.BlockSpec((tm, d), lambda i: (i, 0)),
                  pl.BlockSpec((tm, d), lambda i: (jnp.maximum(i - first, 0), 0))],
        out_specs=[pl.BlockSpec((tm, d), lambda i: (i, 0)), pl.BlockSpec((1, d), lambda i: (0, 0))],
        out_shape=[jax.ShapeDtypeStruct((lp, d), F32), jax.ShapeDtypeStruct((1, d), F32)],
        compiler_params=_cp(("arbitrary",)),
    )(y, target)


def _gate_fwd(o, zsrc, z_blk0, g, other, *, heads, name):
    lp = o.shape[0]
    tm = _row_tile(lp, 512)
    w = heads * HEAD_W
    assert (z_blk0 * HEAD_W) % w == 0
    has_other = w < D_MODEL

    def body(o_ref, z_ref, g_ref, *rest):
        y_ref = rest[-1]
        gv = g_ref[...]
        for h in range(heads):
            cs = slice(h * HEAD_W, (h + 1) * HEAD_W)
            ov = o_ref[:, cs]
            r = lax.rsqrt(jnp.mean(ov * ov, axis=-1, keepdims=True) + RMS_EPS)
            y_ref[:, cs] = (ov * r * gv * _silu(z_ref[:, cs])).astype(BF16)
        if has_other:
            y_ref[:, w:] = rest[0][...].astype(BF16)

    row = lambda width, blk: pl.BlockSpec((tm, width), lambda i: (i, blk))
    return pl.pallas_call(
        body, name=name, grid=(lp // tm,),
        in_specs=[row(w, 0), row(w, z_blk0 * HEAD_W // w), pl.BlockSpec((1, HEAD_W), lambda i: (0, 0))]
        + ([row(D_MODEL - w, 0)] if has_other else []),
        out_specs=row(D_MODEL, 0), out_shape=jax.ShapeDtypeStruct((lp, D_MODEL), BF16),
        compiler_params=_cp(("parallel",)),
    )(o, zsrc, g.reshape(1, HEAD_W), *([other] if has_other else []))


def _gate_bwd(o, zsrc, z_blk0, g, dy, *, heads, name):
    lp = o.shape[0]
    tm = _row_tile(lp, 512)

    w = heads * HEAD_W
    assert (z_blk0 * HEAD_W) % w == 0

    def body(o_ref, z_ref, g_ref, dy_ref, do_ref, dz_ref, dg_ref):
        @pl.when(pl.program_id(0) == 0)
        def _():
            dg_ref[...] = jnp.zeros_like(dg_ref)

        gv = g_ref[...]
        dg = jnp.zeros((1, HEAD_W), F32)
        for h in range(heads):
            cs = slice(h * HEAD_W, (h + 1) * HEAD_W)
            ov, zv, dyv = o_ref[:, cs], z_ref[:, cs], dy_ref[:, cs]
            r = lax.rsqrt(jnp.mean(ov * ov, axis=-1, keepdims=True) + RMS_EPS)
            nrm = ov * r
            s = _silu(zv)
            dn = dyv * gv * s
            do_ref[:, cs] = r * (dn - nrm * jnp.mean(dn * nrm, axis=-1, keepdims=True))
            dz_ref[:, cs] = dyv * nrm * gv * _silu_grad(zv)
            dg = dg + jnp.sum(dyv * nrm * s, axis=0, keepdims=True)
        dg_ref[...] += dg

    row = lambda blk: pl.BlockSpec((tm, w), lambda i: (i, blk))
    vec = pl.BlockSpec((1, HEAD_W), lambda i: (0, 0))
    return pl.pallas_call(
        body, name=name, grid=(lp // tm,),
        in_specs=[row(0), row(z_blk0 * HEAD_W // w), vec, row(0)], out_specs=[row(0), row(0), vec],
        out_shape=[jax.ShapeDtypeStruct((lp, w), F32), jax.ShapeDtypeStruct((lp, w), F32),
                   jax.ShapeDtypeStruct((1, HEAD_W), F32)],
        compiler_params=_cp(("arbitrary",)),
    )(o, zsrc, g.reshape(1, HEAD_W), dy)


def _conv_taps(x, w):
    acc = w[CONV_K - 1:CONV_K, :] * x
    for k in range(CONV_K - 1):
        acc = acc + w[k:k + 1, :] * pltpu.roll(x, CONV_K - 1 - k, 0)
    return acc


def _gdn_pre_fwd(p0, conv_w, pad, *, name):
    lp = p0.shape[0]
    nq = GDN_HEADS
    qscale = HEAD_W ** -0.5

    def body(x_ref, w_ref, y_ref):
        j = pl.program_id(0)
        c = _conv_taps(x_ref[...], w_ref[...])
        s = _silu(c)
        r = lax.rsqrt(jnp.sum(s * s, axis=-1, keepdims=True) + L2_EPS)
        mult = jnp.where(j < nq, r * qscale, jnp.where(j < 2 * nq, r, 1.0))
        rows = lax.broadcasted_iota(jnp.int32, (lp, 1), 0)
        y_ref[...] = jnp.where(rows >= pad, s * mult, 0.0)

    return pl.pallas_call(
        body, name=name, grid=(3 * nq,),
        in_specs=[pl.BlockSpec((lp, HEAD_W), lambda j: (0, j)), pl.BlockSpec((CONV_K, HEAD_W), lambda j: (0, j))],
        out_specs=pl.BlockSpec((lp, HEAD_W), lambda j: (0, j)),
        out_shape=jax.ShapeDtypeStruct((lp, 3 * nq * HEAD_W), F32), compiler_params=_cp(("parallel",)),
    )(p0, conv_w)


def _gdn_pre_bwd(p0, conv_w, dqkv, pad, *, name):
    lp = p0.shape[0]
    nq = GDN_HEADS
    qscale = HEAD_W ** -0.5

    def body(x_ref, w_ref, dy_ref, dx_ref, dw_ref):
        j = pl.program_id(0)
        x, w = x_ref[...], w_ref[...]
        c = _conv_taps(x, w)
        s = _silu(c)
        r = lax.rsqrt(jnp.sum(s * s, axis=-1, keepdims=True) + L2_EPS)
        rows = lax.broadcasted_iota(jnp.int32, (lp, 1), 0)
        dy = jnp.where(rows >= pad, dy_ref[...], 0.0)
        nrm = s * r
        dn = dy * jnp.where(j < nq, qscale, 1.0)
        ds_norm = r * (dn - nrm * jnp.sum(nrm * dn, axis=-1, keepdims=True))
        ds = jnp.where(j < 2 * nq, ds_norm, dy)
        dc = ds * _silu_grad(c)
        dx = w[CONV_K - 1:CONV_K, :] * dc
        dws = [None] * CONV_K
        dws[CONV_K - 1] = jnp.sum(dc * x, axis=0, keepdims=True)
        for k in range(CONV_K - 1):
            sh = CONV_K - 1 - k
            dx = dx + w[k:k + 1, :] * pltpu.roll(dc, lp - sh, 0)
            dws[k] = jnp.sum(dc * pltpu.roll(x, sh, 0), axis=0, keepdims=True)
        dx_ref[...] = dx
        dw_ref[...] = jnp.concatenate(dws, axis=0)

    blk = pl.BlockSpec((lp, HEAD_W), lambda j: (0, j))
    wblk = pl.BlockSpec((CONV_K, HEAD_W), lambda j: (0, j))
    return pl.pallas_call(
        body, name=name, grid=(3 * nq,), in_specs=[blk, wblk, blk], out_specs=[blk, wblk],
        out_shape=[jax.ShapeDtypeStruct((lp, 3 * nq * HEAD_W), F32),
                   jax.ShapeDtypeStruct((CONV_K, 3 * nq * HEAD_W), F32)],
        compiler_params=_cp(("parallel",)),
    )(p0, conv_w, dqkv)


@jax.custom_vjp
def _inv_unit_lower(m):
    c = m.shape[0]
    eye = (lax.broadcasted_iota(jnp.int32, (c, c), 0) == lax.broadcasted_iota(jnp.int32, (c, c), 1)).astype(F32)
    x = eye - m
    p = m
    n = 2
    while n < CHUNK:
        p = _bdot(p, p, NN)
        x = x + _bdot(x, p, NN)
        n *= 2
    return x


def _inv_fwd(m):
    t = _inv_unit_lower(m)
    return t, t


def _inv_bwd(t, g):
    return (-_bdot(_bdot(t, g, TN), t, NT),)


_inv_unit_lower.defvjp(_inv_fwd, _inv_bwd)


GDN_STEP = 3


def _heads_to_rows(x, nh):
    return jnp.concatenate([x[:, h * HEAD_W:(h + 1) * HEAD_W] for h in range(nh)], axis=0)


def _rows_to_heads(x, nh):
    c = x.shape[0] // nh
    return jnp.concatenate([x[h * c:(h + 1) * c] for h in range(nh)], axis=1)


def _gdn_chunk(q, k, v, ba, alog, dtb, states, valid):
    nh = GDN_HEADS
    c = q.shape[0]
    r = nh * c
    lane = lax.broadcasted_iota(jnp.int32, (1, HEAD_W), 1)
    pick = lambda x, l: jnp.sum(jnp.where(lane == l, x, 0.0), axis=-1, keepdims=True)
    beta = jnp.concatenate([jnp.where(valid, _sigmoid(pick(ba, h)), 0.0) for h in range(nh)], axis=0)
    g = jnp.concatenate(
        [jnp.where(valid, -jnp.exp(pick(alog, h)) * _softplus(pick(ba, nh + h) + pick(dtb, h)), 0.0) for h in range(nh)],
        axis=0)
    qs, ks, vs = _heads_to_rows(q, nh), _heads_to_rows(k, nh), _heads_to_rows(v, nh)
    rr = lax.broadcasted_iota(jnp.int32, (r, r), 0)
    cc = lax.broadcasted_iota(jnp.int32, (r, r), 1)
    same = (rr // c) == (cc // c)
    causal, strict = same & (cc <= rr), same & (cc < rr)
    lower = jnp.where(causal, 1.0, 0.0).astype(BF16)
    upper = jnp.where(same & (cc >= rr), 1.0, 0.0).astype(BF16)
    gcb = _mask_mm(lower, upper, g * jnp.ones((1, HEAD_W), F32))
    gc_col = jnp.concatenate([gcb] * (r // HEAD_W), axis=1)
    decay = jnp.where(causal, jnp.exp(jnp.minimum(gc_col - gc_col.T, 0.0)), 0.0)
    egc = jnp.exp(gcb)
    kb = ks * beta
    m = jnp.where(strict, _dot3(kb, ks, NT) * decay, 0.0)
    t = _inv_unit_lower(m)
    u = _bdot(t, vs * beta, NN)
    w = _bdot(t, kb * egc, NN)
    a = _bdot(qs, ks, NT) * decay
    rows = lambda x, h: x[h * c:(h + 1) * c]
    qe = qs * egc
    v_new = u - jnp.concatenate([_bdot(rows(w, h), states[h], NN) for h in range(nh)], axis=0)
    o = jnp.concatenate([_bdot(rows(qe, h), states[h], NN) for h in range(nh)], axis=0) + _bdot(a, v_new, NN)
    new_states = []
    for h in range(nh):
        gl = gcb[(h + 1) * c - 1:(h + 1) * c, :]
        k_dec = rows(ks, h) * jnp.exp(gl - rows(gcb, h))
        new_states.append(states[h] * jnp.exp(gl) + _bdot(k_dec, rows(v_new, h), TN))
    return _rows_to_heads(o, nh), new_states


def _gdn_fwd(qkv, p0, alog_v, dtb_v, pad, *, name, gather=None):
    lp = qkv.shape[0]
    n = lp // CHUNK
    nh = GDN_HEADS
    assert n % GDN_STEP == 0
    steps, rows = n // GDN_STEP, GDN_STEP * CHUNK
    g_srcs, g_dtypes = gather if gather is not None else ([], [])
    ng_arr = len(g_srcs)

    def body(q_ref, k_ref, v_ref, ba_ref, al_ref, dt_ref, *rest):
        g_ins, (o_ref, st_ref) = rest[:ng_arr], rest[ng_arr:ng_arr + 2]
        g_outs, s_ref, g_scratch = rest[ng_arr + 2:2 * ng_arr + 2], rest[2 * ng_arr + 2], rest[2 * ng_arr + 3:]
        i = pl.program_id(0)
        if ng_arr:
            g_start, g_forward, g_finish = _gather_phases(g_ins, g_outs, g_scratch[:ng_arr], *g_scratch[ng_arr:],
                                                          g_dtypes)
            pl.when(i == 0)(g_start)
            pl.when(i == (3 * steps) // 4)(g_forward)

        @pl.when(i == 0)
        def _():
            s_ref[...] = jnp.zeros_like(s_ref)

        s = s_ref[...]
        s = [s[h] for h in range(nh)]
        q, k, v, ba, al, dt = q_ref[...], k_ref[...], v_ref[...], ba_ref[...], al_ref[...], dt_ref[...]
        outs = []
        for c in range(GDN_STEP):
            sl = slice(c * CHUNK, (c + 1) * CHUNK)
            valid = (i * rows + c * CHUNK + lax.broadcasted_iota(jnp.int32, (CHUNK, 1), 0)) >= pad
            for h in range(nh):
                st_ref[c, h] = s[h]
            o, s = _gdn_chunk(q[sl], k[sl], v[sl], ba[sl], al, dt, s, valid)
            outs.append(o)
        o_ref[...] = jnp.concatenate(outs, axis=0)
        for h in range(nh):
            s_ref[h] = s[h]
        if ng_arr:
            pl.when(i == steps - 1)(g_finish)

    w = nh * HEAD_W
    vec = pl.BlockSpec((1, HEAD_W), lambda i: (0, 0))
    return pl.pallas_call(
        body, name=name, grid=(steps,),
        in_specs=[pl.BlockSpec((rows, w), lambda i: (i, 0)), pl.BlockSpec((rows, w), lambda i: (i, 1)),
                  pl.BlockSpec((rows, w), lambda i: (i, 2)), pl.BlockSpec((rows, HEAD_W), lambda i: (i, AB_BA // HEAD_W)),
                  vec, vec] + [pl.BlockSpec(memory_space=pltpu.VMEM)] * ng_arr,
        out_specs=[pl.BlockSpec((rows, w), lambda i: (i, 0)),
                   pl.BlockSpec((GDN_STEP, nh, HEAD_W, HEAD_W), lambda i: (i, 0, 0, 0))] + [_ANY] * ng_arr,
        out_shape=[jax.ShapeDtypeStruct((lp, w), F32), jax.ShapeDtypeStruct((n, nh, HEAD_W, HEAD_W), F32)]
        + _gather_out_shapes(g_srcs, g_dtypes),
        scratch_shapes=[pltpu.VMEM((nh, HEAD_W, HEAD_W), F32)] + (_gather_scratch(g_srcs, g_dtypes) if ng_arr else []),
        compiler_params=_cp(("arbitrary",), has_side_effects=bool(ng_arr)),
    )(qkv, qkv, qkv, p0, alog_v, dtb_v, *g_srcs)


def _gdn_bwd(qkv, p0, alog_v, dtb_v, states, do, pad, *, name, scatter=()):
    lp = qkv.shape[0]
    n = lp // CHUNK
    nh = GDN_HEADS
    assert n % GDN_STEP == 0
    steps, rows = n // GDN_STEP, GDN_STEP * CHUNK
    ns = len(scatter)

    def body(q_ref, k_ref, v_ref, ba_ref, al_ref, dt_ref, st_ref, do_ref, *rest):
        s_ins, (dq_ref, dk_ref, dv_ref, dba_ref, dal_ref, ddt_ref) = rest[:ns], rest[ns:ns + 6]
        s_outs, ds_ref, s_sems = rest[ns + 6:2 * ns + 6], rest[2 * ns + 6], rest[2 * ns + 7:]
        step = pl.program_id(0)
        i = steps - 1 - step
        if ns:
            s_start, s_finish = _scatter_phases(s_ins, s_outs, *s_sems)
            pl.when(step == 0)(s_start)

        @pl.when(step == 0)
        def _():
            ds_ref[...] = jnp.zeros_like(ds_ref)
            dal_ref[...] = jnp.zeros_like(dal_ref)
            ddt_ref[...] = jnp.zeros_like(ddt_ref)

        q, k, v, ba, al, dt = q_ref[...], k_ref[...], v_ref[...], ba_ref[...], al_ref[...], dt_ref[...]
        st, do, dst = st_ref[...], do_ref[...], ds_ref[...]
        vjps = []
        for c in range(GDN_STEP):
            sl = slice(c * CHUNK, (c + 1) * CHUNK)
            valid = (i * rows + c * CHUNK + lax.broadcasted_iota(jnp.int32, (CHUNK, 1), 0)) >= pad
            fn = functools.partial(_gdn_chunk, valid=valid)
            vjps.append(jax.vjp(fn, q[sl], k[sl], v[sl], ba[sl], al, dt, [st[c, h] for h in range(nh)])[1])
        ds = [dst[h] for h in range(nh)]
        grads = [None] * GDN_STEP
        for c in reversed(range(GDN_STEP)):
            grads[c] = vjps[c]((do[c * CHUNK:(c + 1) * CHUNK], ds))
            ds = grads[c][6]
        for j, ref in enumerate((dq_ref, dk_ref, dv_ref, dba_ref)):
            ref[...] = jnp.concatenate([gr[j] for gr in grads], axis=0)
        dal_ref[...] += sum(gr[4] for gr in grads)
        ddt_ref[...] += sum(gr[5] for gr in grads)
        for h in range(nh):
            ds_ref[h] = ds[h]
        if ns:
            pl.when(step == steps - 1)(s_finish)

    w = nh * HEAD_W
    rev = lambda c: (lambda s: (steps - 1 - s, c))
    vec = pl.BlockSpec((1, HEAD_W), lambda s: (0, 0))
    return pl.pallas_call(
        body, name=name, grid=(steps,),
        in_specs=[pl.BlockSpec((rows, w), rev(0)), pl.BlockSpec((rows, w), rev(1)), pl.BlockSpec((rows, w), rev(2)),
                  pl.BlockSpec((rows, HEAD_W), rev(AB_BA // HEAD_W)), vec, vec,
                  pl.BlockSpec((GDN_STEP, nh, HEAD_W, HEAD_W), lambda s: (steps - 1 - s, 0, 0, 0)),
                  pl.BlockSpec((rows, w), rev(0))] + [_ANY] * ns,
        out_specs=[pl.BlockSpec((rows, w), rev(0)), pl.BlockSpec((rows, w), rev(0)), pl.BlockSpec((rows, w), rev(0)),
                   pl.BlockSpec((rows, HEAD_W), rev(0)), vec, vec] + [_ANY] * ns,
        out_shape=[jax.ShapeDtypeStruct((lp, w), F32)] * 3 + [jax.ShapeDtypeStruct((lp, HEAD_W), F32)]
        + [jax.ShapeDtypeStruct((1, HEAD_W), F32)] * 2 + [jax.ShapeDtypeStruct(s.shape, s.dtype) for s in scatter],
        scratch_shapes=[pltpu.VMEM((nh, HEAD_W, HEAD_W), F32)] + (_scatter_scratch(ns) if ns else []),
        compiler_params=_cp(("arbitrary",), has_side_effects=bool(ns)),
    )(qkv, qkv, qkv, p0, alog_v, dtb_v, states, do, *scatter)


HG_LEVELS = (32, 16, 8, 4, 2, 1)
HG_GROUP = 4
HG_STEP = 3


def _hg_masks():
    import numpy as np
    c = CHUNK
    t = np.arange(c)[:, None]
    j = np.arange(c)[None, :]
    sums = (j <= t).astype(np.float32)
    pairs = [j == t]
    for m in HG_LEVELS:
        p = (t // (2 * m)) * (2 * m)
        r = p + m
        pairs.append((t >= r) & (j < r) & (j >= p))
    pairs = np.concatenate([np.kron(np.eye(HG_GROUP), p) for p in pairs], axis=0).astype(np.float32)
    return jnp.asarray(sums, BF16), jnp.asarray(sums.T, BF16), jnp.asarray(pairs, F32)


def _hg_level_row(b, m):
    c, w = b.shape
    if m >= 8:
        return jnp.concatenate([jnp.broadcast_to(b[p + m:p + m + 1], (2 * m, w)) for p in range(0, c, 2 * m)], axis=0)
    tiles = b.reshape(c // 8, 8, w)
    sub = lax.broadcasted_iota(jnp.int32, (1, 8, 1), 1)
    out = None
    for r0 in range(m, 8, 2 * m):
        cand = jnp.broadcast_to(tiles[:, r0:r0 + 1, :], tiles.shape)
        out = cand if out is None else jnp.where(sub >= r0 - m, cand, out)
    return out.reshape(c, w)


def _split3(x):
    hi = x.astype(BF16)
    r1 = x - hi.astype(F32)
    mid = r1.astype(BF16)
    return hi, mid, (r1 - mid.astype(F32)).astype(BF16)


def _dot3_raw(a, b, dims):
    ah, am, _ = _split3(a)
    bh, bm, _ = _split3(b)
    return _dot(ah, bh, dims) + (_dot(ah, bm, dims) + _dot(am, bh, dims))


@functools.partial(jax.custom_vjp, nondiff_argnums=(2,))
def _dot3(a, b, dims):
    return _dot3_raw(a, b, dims)


def _dot3_fwd(a, b, dims):
    return _dot3_raw(a, b, dims), (a, b)


def _dot3_bwd(dims, res, g):
    a, b = res
    if dims == NN:
        return _dot3_raw(g, b, NT), _dot3_raw(a, g, TN)
    return _dot3_raw(g, b, NN), _dot3_raw(g, a, TN)


_dot3.defvjp(_dot3_fwd, _dot3_bwd)


def _mask_mm_raw(m, x):
    return sum(_dot(m, part, NN) for part in _split3(x))


@jax.custom_vjp
def _mask_mm(m, mt, x):
    return _mask_mm_raw(m, x)


def _mask_mm_fwd(m, mt, x):
    return _mask_mm_raw(m, x), (m, mt)


def _mask_mm_bwd(res, g):
    m, mt = res
    return jnp.zeros_like(m), jnp.zeros_like(mt), _mask_mm_raw(mt, g)


_mask_mm.defvjp(_mask_mm_fwd, _mask_mm_bwd)


def _hg_chunk(qr, fr, ir, lb, states, valid, sums, sums_t, pairs):
    nh = HG_GROUP
    c = qr.shape[0]
    r = nh * c
    fg = lb + (1.0 - lb) * _sigmoid(fr)
    logf = jnp.where(valid, jnp.log(fg), 0.0)
    k = jnp.where(valid, 1.0 - fg, 0.0)
    qs = jnp.where(valid, _silu(qr), 0.0)
    v = jnp.where(valid, ir, 0.0)
    b = _mask_mm(sums, sums_t, logf)
    mask = lambda n: pairs[n * r:(n + 1) * r]
    stack = lambda x: _heads_to_rows(x, nh)
    a = mask(0) * _bdot(stack(qs), stack(k), NT)
    for lvl, m in enumerate(HG_LEVELS):
        d = b - _hg_level_row(b, m)
        a = a + mask(1 + lvl) * _bdot(stack(qs * jnp.exp(jnp.minimum(d, 0.0))),
                                      stack(k * jnp.exp(jnp.minimum(-d, 0.0))), NT)
    av = _bdot(a, stack(v), NN)
    eb = jnp.exp(b)
    qe, kd = qs * eb, k * jnp.exp(b[c - 1:c] - b)
    outs, new_states = [], []
    for h in range(nh):
        cs = slice(h * HEAD_W, (h + 1) * HEAD_W)
        outs.append(_bdot(qe[:, cs], states[h], NT) + av[h * c:(h + 1) * c])
        new_states.append(states[h] * eb[c - 1:c, cs] + _bdot(v[:, cs], kd[:, cs], TN))
    return jnp.concatenate(outs, axis=1), new_states


def _hg_fwd(p1, lb, pad, *, name, gather=None):
    lp = p1.shape[0]
    n = lp // CHUNK
    nh = HG_HEADS
    g_srcs, g_dtypes = gather if gather is not None else ([], [])
    ng_arr = len(g_srcs)

    def body(q_ref, f_ref, i_ref, lb_ref, sums_ref, sums_t_ref, pairs_ref, *rest):
        g_ins, (o_ref, st_ref) = rest[:ng_arr], rest[ng_arr:ng_arr + 2]
        g_outs, s_ref, g_scratch = rest[ng_arr + 2:2 * ng_arr + 2], rest[2 * ng_arr + 2], rest[2 * ng_arr + 3:]
        i = pl.program_id(1)
        if ng_arr:
            g_start, g_forward, g_finish = _gather_phases(g_ins, g_outs, g_scratch[:ng_arr], *g_scratch[ng_arr:],
                                                          g_dtypes)
            last_group = pl.program_id(0) == ngrp - 1
            pl.when((pl.program_id(0) == 0) & (i == 0))(g_start)
            pl.when(last_group & (i == 0))(g_forward)

        @pl.when(i == 0)
        def _():
            s_ref[...] = jnp.zeros_like(s_ref)

        s = s_ref[...]
        s = [s[h] for h in range(grp)]
        q, f, iv, lbv = q_ref[...], f_ref[...], i_ref[...], lb_ref[...]
        masks_v = (sums_ref[...], sums_t_ref[...], pairs_ref[...])
        outs = []
        for c in range(HG_STEP):
            sl = slice(c * CHUNK, (c + 1) * CHUNK)
            valid = (i * rows + c * CHUNK + lax.broadcasted_iota(jnp.int32, (CHUNK, 1), 0)) >= pad
            for h in range(grp):
                st_ref[h, c] = s[h]
            o, s = _hg_chunk(q[sl], f[sl], iv[sl], lbv, s, valid, *masks_v)
            outs.append(o)
        o_ref[...] = jnp.concatenate(outs, axis=0)
        for h in range(grp):
            s_ref[h] = s[h]
        if ng_arr:
            pl.when(last_group & (i == steps - 1))(g_finish)

    masks = _hg_masks()
    grp, ngrp, gw = HG_GROUP, nh // HG_GROUP, HG_GROUP * HEAD_W
    assert n % HG_STEP == 0
    steps, rows = n // HG_STEP, HG_STEP * CHUNK
    blk = lambda off: pl.BlockSpec((rows, gw), lambda h, i: (i, off + h))
    const = lambda a: pl.BlockSpec(a.shape, lambda h, i: (0, 0))
    return pl.pallas_call(
        body, name=name, grid=(ngrp, steps),
        in_specs=[blk(0), blk(ngrp), blk(2 * ngrp), pl.BlockSpec((1, gw), lambda h, i: (0, h))]
        + [const(a) for a in masks] + [pl.BlockSpec(memory_space=pltpu.VMEM)] * ng_arr,
        out_specs=[blk(0), pl.BlockSpec((grp, HG_STEP, HEAD_W, HEAD_W), lambda h, i: (h, i, 0, 0))] + [_ANY] * ng_arr,
        out_shape=[jax.ShapeDtypeStruct((lp, nh * HEAD_W), F32), jax.ShapeDtypeStruct((nh, n, HEAD_W, HEAD_W), F32)]
        + _gather_out_shapes(g_srcs, g_dtypes),
        scratch_shapes=[pltpu.VMEM((grp, HEAD_W, HEAD_W), F32)] + (_gather_scratch(g_srcs, g_dtypes) if ng_arr else []),
        compiler_params=_cp(("arbitrary", "arbitrary"), has_side_effects=bool(ng_arr)),
    )(p1, p1, p1, lb, *masks, *g_srcs)


def _hg_bwd(p1, lb, states, do, pad, *, name, scatter=()):
    lp = p1.shape[0]
    n = lp // CHUNK
    nh = HG_HEADS
    ns = len(scatter)

    def body(q_ref, f_ref, i_ref, lb_ref, st_ref, do_ref, sums_ref, sums_t_ref, pairs_ref, *rest):
        s_ins, (dq_ref, df_ref, di_ref, dlb_ref) = rest[:ns], rest[ns:ns + 4]
        s_outs, ds_ref, s_sems = rest[ns + 4:2 * ns + 4], rest[2 * ns + 4], rest[2 * ns + 5:]
        step = pl.program_id(1)
        i = steps - 1 - step
        if ns:
            s_start, s_finish = _scatter_phases(s_ins, s_outs, *s_sems)
            pl.when((pl.program_id(0) == 0) & (step == 0))(s_start)

        @pl.when(step == 0)
        def _():
            ds_ref[...] = jnp.zeros_like(ds_ref)
            dlb_ref[...] = jnp.zeros_like(dlb_ref)

        q, f, iv, lbv, st, do, dst = q_ref[...], f_ref[...], i_ref[...], lb_ref[...], st_ref[...], do_ref[...], ds_ref[...]
        masks_v = dict(sums=sums_ref[...], sums_t=sums_t_ref[...], pairs=pairs_ref[...])
        vjps = []
        for c in range(HG_STEP):
            sl = slice(c * CHUNK, (c + 1) * CHUNK)
            valid = (i * rows + c * CHUNK + lax.broadcasted_iota(jnp.int32, (CHUNK, 1), 0)) >= pad
            fn = functools.partial(_hg_chunk, valid=valid, **masks_v)
            vjps.append(jax.vjp(fn, q[sl], f[sl], iv[sl], lbv, [st[h, c] for h in range(grp)])[1])
        ds = [dst[h] for h in range(grp)]
        grads = [None] * HG_STEP
        for c in reversed(range(HG_STEP)):
            grads[c] = vjps[c]((do[c * CHUNK:(c + 1) * CHUNK], ds))
            ds = grads[c][4]
        for j, ref in enumerate((dq_ref, df_ref, di_ref)):
            ref[...] = jnp.concatenate([gr[j] for gr in grads], axis=0)
        dlb_ref[...] += sum(gr[3] for gr in grads)
        for h in range(grp):
            ds_ref[h] = ds[h]
        if ns:
            pl.when((pl.program_id(0) == ngrp - 1) & (step == steps - 1))(s_finish)

    masks = _hg_masks()
    grp, ngrp, gw = HG_GROUP, nh // HG_GROUP, HG_GROUP * HEAD_W
    assert n % HG_STEP == 0
    steps, rows = n // HG_STEP, HG_STEP * CHUNK
    blk = lambda off: pl.BlockSpec((rows, gw), lambda h, s: (steps - 1 - s, off + h))
    const = lambda a: pl.BlockSpec(a.shape, lambda h, s: (0, 0))
    w = nh * HEAD_W
    return pl.pallas_call(
        body, name=name, grid=(ngrp, steps),
        in_specs=[blk(0), blk(ngrp), blk(2 * ngrp), pl.BlockSpec((1, gw), lambda h, s: (0, h)),
                  pl.BlockSpec((grp, HG_STEP, HEAD_W, HEAD_W), lambda h, s: (h, steps - 1 - s, 0, 0)), blk(0)]
        + [const(a) for a in masks] + [_ANY] * ns,
        out_specs=[blk(0), blk(0), blk(0), pl.BlockSpec((1, gw), lambda h, s: (0, h))] + [_ANY] * ns,
        out_shape=[jax.ShapeDtypeStruct((lp, w), F32)] * 3 + [jax.ShapeDtypeStruct((1, w), F32)]
        + [jax.ShapeDtypeStruct(s.shape, s.dtype) for s in scatter],
        scratch_shapes=[pltpu.VMEM((grp, HEAD_W, HEAD_W), F32)] + (_scatter_scratch(ns) if ns else []),
        compiler_params=_cp(("arbitrary", "arbitrary"), has_side_effects=bool(ns)),
    )(p1, p1, p1, lb, states, do, *masks, *scatter)


SB_GROUP = 4
SB_FAR = -110.0


def _sb_cat(kind, first_key=0):
    r = lax.broadcasted_iota(jnp.int32, (SB_BLOCK, 2 * SB_BLOCK), 0)
    c = lax.broadcasted_iota(jnp.int32, (SB_BLOCK, 2 * SB_BLOCK), 1)
    tri = {"after": c < r, "incl": r <= c, "before": r < c}[kind]
    m = ((c >= SB_BLOCK) | tri) & (r >= first_key)
    return jnp.where(m, 1.0, 0.0).astype(BF16)


def _sb_cumsum(x, cat):
    return _dot(x.astype(BF16), cat, NN)


def _sb_logsig(z):
    e = jnp.exp(-jnp.abs(z))
    lse = jnp.where(e < 1e-4, e, jnp.log(1.0 + e))
    lsz = jnp.minimum(z, 0.0) - lse
    return lsz, lsz - z, e


def _sb_stack(x, scale=None):
    lane = lax.broadcasted_iota(jnp.int32, (1, HEAD_W), 1)
    if scale is not None:
        x = x * scale
    return jnp.concatenate([jnp.where(lane < SB_DH, x, 0.0), jnp.where(lane >= SB_DH, x, 0.0)], axis=0).astype(BF16)


def _sb_unstack(x):
    lane = lax.broadcasted_iota(jnp.int32, (1, HEAD_W), 1)
    return jnp.where(lane < SB_DH, x[:SB_BLOCK], x[SB_BLOCK:])


def _sb_fwd(p0, pad, *, name, gather=None):
    lp = p0.shape[0]
    nb = lp // SB_BLOCK
    npair = SB_HEADS // 2
    blk0 = AB_SB // HEAD_W
    scale = SB_DH ** -0.5
    gw = SB_GROUP * SB_BLOCK
    assert pad < SB_BLOCK
    g_srcs, g_dtypes = gather if gather is not None else ([], [])
    ng_arr = len(g_srcs)

    def body(q_ref, k_ref, v_ref, *rest):
        g_ins, (o_ref, tot_ref, nproc_ref) = rest[:ng_arr], rest[ng_arr:ng_arr + 3]
        g_outs, g_scratch = rest[ng_arr + 3:2 * ng_arr + 3], rest[2 * ng_arr + 3:]
        first_step = (pl.program_id(0) == 0) & (pl.program_id(1) == 0)
        last_pair = pl.program_id(0) == npair - 1
        if ng_arr:
            g_start, g_forward, g_finish = _gather_phases(g_ins, g_outs, g_scratch[:ng_arr], *g_scratch[ng_arr:],
                                                          g_dtypes)
            pl.when(first_step)(g_start)
            pl.when(last_pair & (pl.program_id(1) == 0))(g_forward)
        i = pl.program_id(1)
        qs = _sb_stack(q_ref[...], scale)
        qpos = i * SB_BLOCK + lax.broadcasted_iota(jnp.int32, (SB_BLOCK, 1), 0)
        qpos = jnp.concatenate([qpos, qpos], axis=0)
        cat = _sb_cat("after")
        cat0 = _sb_cat("after", pad)
        ng = i // SB_GROUP

        def group(off, nblk, first_cat, allowed, carry):
            acc, run = carry
            kg = k_ref[pl.ds(off, nblk * SB_BLOCK), :].astype(BF16)
            vg = v_ref[pl.ds(off, nblk * SB_BLOCK), :].astype(BF16)
            lsz, l1m, _ = _sb_logsig(_dot(qs, kg, NT))
            if allowed is not None:
                l1m = jnp.where(allowed, l1m, 0.0)
            args = [None] * nblk
            for g in reversed(range(nblk)):
                sl = slice(g * SB_BLOCK, (g + 1) * SB_BLOCK)
                al = _sb_cumsum(l1m[:, sl], first_cat if g == 0 else cat)
                args[g] = lsz[:, sl] + al[:, :SB_BLOCK] + run
                run = run + al[:, SB_BLOCK:]
            wgt = jnp.exp(jnp.concatenate(args, axis=1))
            if allowed is not None:
                wgt = jnp.where(allowed, wgt, 0.0)
            return acc + _dot(wgt.astype(BF16), vg, NN), run

        def below(t, carry):
            gi = ng - 1 - t
            return group(pl.multiple_of(gi * gw, gw), SB_GROUP, jnp.where(gi == 0, cat0, cat), None, carry)

        top = ng * gw

        def top_group(nblk, carry):
            off = pl.multiple_of(jnp.minimum(top, lp - nblk * SB_BLOCK), SB_BLOCK)
            kpos = off + lax.broadcasted_iota(jnp.int32, (1, nblk * SB_BLOCK), 1)
            return group(off, nblk, cat, (kpos < qpos) & (kpos >= pad) & (kpos >= top), carry)

        zero = (jnp.zeros((2 * SB_BLOCK, HEAD_W), F32), jnp.zeros((2 * SB_BLOCK, HEAD_W), F32))
        carry = lax.cond(i - ng * SB_GROUP < SB_GROUP // 2, functools.partial(top_group, SB_GROUP // 2),
                         functools.partial(top_group, SB_GROUP), zero)
        used, acc, run = lax.while_loop(lambda s: (s[0] < ng) & (jnp.max(s[2]) > SB_FAR),
                                        lambda s: (s[0] + 1, *below(s[0], (s[1], s[2]))), (jnp.int32(0), *carry))
        o_ref[...] = _sb_unstack(acc)
        tot_ref[...] = _sb_unstack(run)
        nproc_ref[pl.program_id(0), i] = used.astype(F32)
        if ng_arr:
            pl.when(last_pair & (pl.program_id(1) == nb - 1))(g_finish)

    full = lambda c0: pl.BlockSpec((lp, HEAD_W), lambda p, i: (0, c0 + p))
    out = pl.BlockSpec((SB_BLOCK, HEAD_W), lambda p, i: (i, p))
    return pl.pallas_call(
        body, name=name, grid=(npair, nb),
        in_specs=[pl.BlockSpec((SB_BLOCK, HEAD_W), lambda p, i: (i, blk0 + p)), full(blk0 + npair), full(blk0 + 2 * npair)]
        + [pl.BlockSpec(memory_space=pltpu.VMEM)] * ng_arr,
        out_specs=[out, out, pl.BlockSpec(memory_space=pltpu.SMEM)] + [_ANY] * ng_arr,
        out_shape=[jax.ShapeDtypeStruct((lp, npair * HEAD_W), F32)] * 2 + [jax.ShapeDtypeStruct((npair, nb), F32)]
        + _gather_out_shapes(g_srcs, g_dtypes),
        scratch_shapes=_gather_scratch(g_srcs, g_dtypes) if ng_arr else [],
        compiler_params=_cp(("arbitrary", "arbitrary"), has_side_effects=bool(ng_arr)),
    )(p0, p0, p0, *g_srcs)


def _sb_bwd(p0, tot, nproc, dsrc, d_blk0, pad, *, name, scatter=()):
    lp = p0.shape[0]
    nb = lp // SB_BLOCK
    npair = SB_HEADS // 2
    blk0 = AB_SB // HEAD_W
    scale = SB_DH ** -0.5
    gw = SB_GROUP * SB_BLOCK
    assert pad < SB_BLOCK
    ns = len(scatter)

    def body(q_ref, k_ref, v_ref, tot_ref, nproc_ref, do_ref, *rest):
        s_ins, (dq_ref, dkt_ref, dvt_ref) = rest[:ns], rest[ns:ns + 3]
        s_outs, s_sems = rest[ns + 3:2 * ns + 3], rest[2 * ns + 3:]
        if ns:
            s_start, s_finish = _scatter_phases(s_ins, s_outs, *s_sems)
            pl.when((pl.program_id(0) == 0) & (pl.program_id(1) == 0))(s_start)
        i = pl.program_id(1)

        @pl.when(i == 0)
        def _():
            dkt_ref[...] = jnp.zeros_like(dkt_ref)
            dvt_ref[...] = jnp.zeros_like(dvt_ref)

        qs = _sb_stack(q_ref[...], scale)
        dos = _sb_stack(do_ref[...])
        qst, dost = qs.T, dos.T
        totv = tot_ref[...]
        ones = jnp.ones((1, HEAD_W), F32)
        tots = jnp.concatenate([totv[:, 0:1] * ones, totv[:, SB_DH:SB_DH + 1] * ones], axis=0)
        qpos = i * SB_BLOCK + lax.broadcasted_iota(jnp.int32, (SB_BLOCK, 1), 0)
        qpos = jnp.concatenate([qpos, qpos], axis=0)
        incl, incl0 = _sb_cat("incl"), _sb_cat("incl", pad)
        before = _sb_cat("before")
        ng = i // SB_GROUP
        used = jnp.clip(nproc_ref[pl.program_id(0), i].astype(jnp.int32), 0, ng)

        def dscore(z, e, ev, dl1m):
            r = 1.0 / (1.0 + e)
            sg = jnp.where(z >= 0, r, e * r)
            return ev * (1.0 - sg) - dl1m * sg

        def group(off, nblk, first_incl, allowed, carry):
            dq, prun, erun = carry
            width = nblk * SB_BLOCK
            kg = k_ref[pl.ds(off, width), :].astype(BF16)
            vg = v_ref[pl.ds(off, width), :].astype(BF16)
            z = _dot(qs, kg, NT)
            lsz, l1m, e = _sb_logsig(z)
            if allowed is not None:
                l1m = jnp.where(allowed, l1m, 0.0)
            dwgt = _dot(dos, vg, NT)
            dzs = [None] * nblk
            wgts = [None] * nblk
            for g in range(nblk):
                sl = slice(g * SB_BLOCK, (g + 1) * SB_BLOCK)
                al = _sb_cumsum(l1m[:, sl], first_incl if g == 0 else incl)
                wgt = jnp.exp(jnp.minimum(lsz[:, sl] + (tots - prun - al[:, :SB_BLOCK]), 0.0))
                if allowed is not None:
                    wgt = jnp.where(allowed[:, sl], wgt, 0.0)
                prun = prun + al[:, SB_BLOCK:]
                ev = wgt * dwgt[:, sl]
                el = _sb_cumsum(ev, before)
                dzs[g] = dscore(z[:, sl], e[:, sl], ev, erun + el[:, :SB_BLOCK])
                erun = erun + el[:, SB_BLOCK:]
                wgts[g] = wgt
            dz = jnp.concatenate(dzs, axis=1)
            if allowed is not None:
                dz = jnp.where(allowed, dz, 0.0)
            dz = dz.astype(BF16)
            wg = jnp.concatenate(wgts, axis=1).astype(BF16)
            dkt_ref[:, pl.ds(off, width)] += _dot(qst, dz, NN)
            dvt_ref[:, pl.ds(off, width)] += _dot(dost, wg, NN)
            return dq + _dot(dz, kg, NN), prun, erun

        def below(gi, carry):
            return group(pl.multiple_of(gi * gw, gw), SB_GROUP, jnp.where(gi == 0, incl0, incl), None, carry)

        zero = tuple(jnp.zeros((2 * SB_BLOCK, HEAD_W), F32) for _ in range(3))
        carry = lax.fori_loop(ng - used, ng, below, zero)
        top = ng * gw

        def top_group(nblk, carry):
            off = pl.multiple_of(jnp.minimum(top, lp - nblk * SB_BLOCK), SB_BLOCK)
            kpos = off + lax.broadcasted_iota(jnp.int32, (1, nblk * SB_BLOCK), 1)
            return group(off, nblk, incl, (kpos < qpos) & (kpos >= pad) & (kpos >= top), carry)

        dq, _, _ = lax.cond(i - ng * SB_GROUP < SB_GROUP // 2, functools.partial(top_group, SB_GROUP // 2),
                            functools.partial(top_group, SB_GROUP), carry)
        dq_ref[...] = _sb_unstack(dq) * scale
        if ns:
            pl.when((pl.program_id(0) == npair - 1) & (pl.program_id(1) == nb - 1))(s_finish)

    full = lambda c0: pl.BlockSpec((lp, HEAD_W), lambda p, i: (0, c0 + p))
    qb = lambda c0: pl.BlockSpec((SB_BLOCK, HEAD_W), lambda p, i: (i, c0 + p))
    tr = pl.BlockSpec((HEAD_W, lp), lambda p, i: (p, 0))
    return pl.pallas_call(
        body, name=name, grid=(npair, nb),
        in_specs=[qb(blk0), full(blk0 + npair), full(blk0 + 2 * npair), qb(0), pl.BlockSpec(memory_space=pltpu.SMEM),
                  qb(d_blk0)] + [_ANY] * ns,
        out_specs=[qb(0), tr, tr] + [_ANY] * ns,
        out_shape=[jax.ShapeDtypeStruct((lp, npair * HEAD_W), F32)]
        + [jax.ShapeDtypeStruct((npair * HEAD_W, lp), F32)] * 2
        + [jax.ShapeDtypeStruct(s.shape, s.dtype) for s in scatter],
        scratch_shapes=_scatter_scratch(ns) if ns else [],
        compiler_params=_cp(("arbitrary", "arbitrary"), has_side_effects=bool(ns)),
    )(p0, p0, p0, tot, nproc, dsrc, *scatter)


def _local_step(h0, target, pad, wts, hooks=None):
    lp = h0.shape[0]
    tm = _row_tile(lp, 1056)
    tkl = tm
    tml = _row_tile(lp, 528)
    d = D_MODEL
    mm = _mm
    mmw = functools.partial(_mm, out_dtype=BF16)
    k_major = lambda wd: wd.transpose(1, 0, 2).reshape(wd.shape[1], -1)
    g = {}

    h0_b = h0.astype(BF16)
    p0 = mm(h0_b, wts["w_ab"], "NN", tm=tm, tn=768, tk=d, name="l0_in_proj")
    ob, sb_tot, sb_used, *gathered = _sb_fwd(p0, pad, name="sb_fwd", gather=hooks["gather_a"] if hooks else None)
    if hooks:
        wts = {**wts, **hooks["weights_a"](gathered)}
    qkv = _gdn_pre_fwd(p0, wts["conv_w"], pad, name="gdn_pre_fwd")
    oa_raw, gdn_states, *gathered = _gdn_fwd(qkv, p0, wts["alog_v"], wts["dtb_v"], pad, name="gdn_fwd",
                                             gather=hooks["gather_b"] if hooks else None)
    if hooks:
        wts = {**wts, **hooks["weights_b"](gathered)}
    rows = lambda a, n: a.reshape(N_DEV, n // N_DEV, d)
    parts = g["parts"] = {}
    oab = _gate_fwd(oa_raw, p0, AB_Z // HEAD_W, wts["ab_gn"], ob, heads=GDN_HEADS, name="gdn_gate_fwd")
    ln = lambda kind, layer: (wts[f"ln_{kind}_g"][layer], wts[f"ln_{kind}_b"][layer])
    pre_mix0, h0a, h0a_b = mm(oab, wts["w_out0"], "NN", tm=tm, tn=d, tk=d, epi="ln", c=h0, scale=DN_ALPHA,
                              ln=ln("mix", 0), name="l0_out_proj")
    u0, act0 = mm(h0a_b, k_major(wts["w1"][0]), "NN", tm=tm, tn=1024, tk=d, epi="relu2_copy", name="mlp0_up")
    pre_ffn0, h0b, h0b_b = mm(act0, wts["w2"][0], "NN", tm=tm, tn=d, tk=d, epi="ln", c=h0a, scale=DN_ALPHA,
                              ln=ln("ffn", 0), name="mlp0_down")
    p1 = mm(h0b_b, k_major(wts["w_c"]), "NN", tm=tm, tn=1024, tk=d, name="l1_in_proj")
    oc_raw, hg_states, *gathered = _hg_fwd(p1, wts["lb"], pad, name="hg_fwd",
                                           gather=hooks["gather_c"] if hooks else None)
    if hooks:
        third = hooks["weights_c"](gathered)
        wts = {**wts, "w1": wts["w1"] + third["w1"], "w2": wts["w2"] + third["w2"]}
    oc = _gate_fwd(oc_raw, p1, 3 * HG_HEADS, wts["c_gn"], oc_raw, heads=HG_HEADS, name="hg_gate_fwd")
    pre_mix1, h1a, h1a_b = mm(oc, wts["w_out1"], "NN", tm=tm, tn=d, tk=d, epi="ln", c=h0b, scale=DN_ALPHA,
                              ln=ln("mix", 1), name="l1_out_proj")
    u1, act1 = mm(h1a_b, k_major(wts["w1"][1]), "NN", tm=tm, tn=1024, tk=d, epi="relu2_copy", name="mlp1_up")
    pre_ffn1, h1b, _ = mm(act1, wts["w2"][1], "NN", tm=tm, tn=d, tk=d, epi="ln", c=h1a, scale=DN_ALPHA,
                          ln=ln("ffn", 1), name="mlp1_down")
    dy, loss_vec = _loss_head(h1b, target, name="loss_head")

    def mlp_bwd(layer, h_in_b, u, act, dpre, dpre_b, pre_mix):
        du = mm(dpre_b, wts["w2"][layer], "NT", tm=tm, tn=1024, tk=d, epi="relu2grad", c=u, out_dtype=BF16,
                name=f"mlp{layer}_d_hidden")
        dw2 = mmw(act, dpre_b, "TN", tm=1024, tn=1024, tk=tkl, name=f"mlp{layer}_dw2")
        dw1 = mmw(h_in_b, du, "TN", tm=1024, tn=512, tk=tkl, out_dev=True, name=f"mlp{layer}_dw1")
        return (*mm(du, k_major(wts["w1"][layer]), "NT", tm=tml, tn=1024, tk=2048, epi="ln_bwd", c=dpre, scale=DN_ALPHA,
                    ln=(pre_mix, wts["ln_mix_g"][layer]), name=f"mlp{layer}_d_in"), dw1, dw2)

    ln_ffn_dg, ln_ffn_db, ln_mix_dg, ln_mix_db, dw1s, dw2s = ([None, None] for _ in range(6))
    dpre, dpre_b, ln_ffn_dg[1], ln_ffn_db[1] = _ln_bwd(pre_ffn1, wts["ln_ffn_g"][1], dy, name="ln_ffn1_bwd")
    dpre, dpre_b, ln_mix_dg[1], ln_mix_db[1], dw1s[1], dw2s[1] = mlp_bwd(1, h1a_b, u1, act1, dpre, dpre_b, pre_mix1)
    g["c_w_out"] = mmw(oc, dpre_b, "TN", tm=1024, tn=1024, tk=tkl, name="l1_dw_out")
    doc = mm(dpre_b, wts["w_out1"], "NT", tm=tm, tn=1024, tk=d, name="l1_d_gate")
    doc_raw, dz1, g["c_gn"] = _gate_bwd(oc_raw, p1, 3 * HG_HEADS, wts["c_gn"], doc, heads=HG_HEADS, name="hg_gate_bwd")
    ready = [dw1s[1], rows(dw2s[1], D_FF), rows(g["c_w_out"], d)] if hooks else ()
    dq1, df1, di1, g["lb"], *got = _hg_bwd(p1, wts["lb"], hg_states, doc_raw, pad, name="hg_bwd", scatter=ready)
    parts.update(zip(("mlp_w1_1", "mlp_w2_1", "c_w_out"), got))
    dp1 = jnp.concatenate([dq1, df1, di1, dz1], axis=1).astype(BF16)
    g["c_w_in"] = mmw(h0b_b, dp1, "TN", tm=1024, tn=512, tk=tkl, out_dev=True, name="l1_dw_in")
    dpre, dpre_b, ln_ffn_dg[0], ln_ffn_db[0] = mm(
        dp1, k_major(wts["w_c"]), "NT", tm=tml, tn=1024, tk=2048, epi="ln_bwd", c=dpre, scale=DN_ALPHA,
        ln=(pre_ffn0, wts["ln_ffn_g"][0]), name="l1_d_in")
    dpre, dpre_b, ln_mix_dg[0], ln_mix_db[0], dw1s[0], dw2s[0] = mlp_bwd(0, h0a_b, u0, act0, dpre, dpre_b, pre_mix0)
    g["ab_w_out"] = mmw(oab, dpre_b, "TN", tm=1024, tn=1024, tk=tkl, name="l0_dw_out")
    doab = mm(dpre_b, wts["w_out0"], "NT", tm=tm, tn=1024, tk=d, name="l0_d_gate")
    doa_raw, dz0, g["ab_gn"] = _gate_bwd(oa_raw, p0, AB_Z // HEAD_W, wts["ab_gn"], doab, heads=GDN_HEADS,
                                         name="gdn_gate_bwd")
    ready = [g["c_w_in"]] if hooks else ()
    dqb, dkb_t, dvb_t, *got = _sb_bwd(p0, sb_tot, sb_used, doab, GDN_HEADS, pad, name="sb_bwd", scatter=ready)
    parts.update(zip(("c_w_in",), got))
    dkb, dvb = dkb_t.T, dvb_t.T
    ready = [dw1s[0], rows(dw2s[0], D_FF), rows(g["ab_w_out"], d)] if hooks else ()
    dqn, dkn, dvn, dba, g["alog_v"], g["dtb_v"], *got = _gdn_bwd(qkv, p0, wts["alog_v"], wts["dtb_v"], gdn_states,
                                                                 doa_raw, pad, name="gdn_bwd", scatter=ready)
    parts.update(zip(("mlp_w1_0", "mlp_w2_0", "ab_w_out"), got))
    dconv_in, g["conv_w"] = _gdn_pre_bwd(p0, wts["conv_w"], jnp.concatenate([dqn, dkn, dvn], axis=1), pad,
                                         name="gdn_pre_bwd")
    dp0 = jnp.concatenate([dconv_in, dz0, dqb, dkb, dvb, dba, jnp.zeros((lp, AB_CAT - AB_BA - HEAD_W), F32)],
                          axis=1).astype(BF16)
    def slots(gab):
        ba0 = AB_Z + GDN_HEADS * HEAD_W
        gab = jnp.concatenate([gab[:, :ba0], gab[:, AB_BA:AB_BA + 2 * GDN_HEADS], gab[:, ba0:AB_BA]], axis=1)
        return [gab.reshape(gab.shape[0], N_DEV, AB_IN // N_DEV).transpose(1, 0, 2)]

    half = d // 2
    g_top = mmw(h0_b[:, :half], dp0, "TN", tm=half, tn=768, tk=tkl, name="l0_dw_in_top")
    res = mmw(h0_b[:, half:], dp0, "TN", tm=half, tn=768, tk=tkl, scatter=slots(g_top) if hooks else (),
              name="l0_dw_in_bottom")
    g_bottom, got_top = (res[0], res[1:]) if hooks else (res, [])
    g["w_ab"] = jnp.concatenate([g_top, g_bottom], axis=0)
    res = mm(dp0, wts["w_ab"], "NT", tm=tm, tn=1024, tk=1920, epi="add", c=dpre, scale=DN_ALPHA,
             scatter=slots(g_bottom) if hooks else (), name="l0_d_in")
    dh0, got_bottom = (res[0], res[1:]) if hooks else (res, [])
    if hooks:
        parts["ab_w_in"] = jnp.concatenate([got_top[0], got_bottom[0]], axis=1)

    g["w1"], g["w2"] = dw1s, dw2s
    g["ln_mix_g"] = jnp.concatenate(ln_mix_dg, axis=0)
    g["ln_mix_b"] = jnp.concatenate(ln_mix_db, axis=0)
    g["ln_ffn_g"] = jnp.concatenate(ln_ffn_dg, axis=0)
    g["ln_ffn_b"] = jnp.concatenate(ln_ffn_db, axis=0)
    return loss_vec, dh0, g


N_CHIP = N_DEV // 2


def _place():
    x, y, c = lax.axis_index("x"), lax.axis_index("y"), lax.axis_index("c")
    return x, y, c, 2 * x + y


def _chip_dev(chip, core):
    return (chip // 2, chip % 2, core)


def _remote(src, dst, send_sem, recv_sem, dev):
    return pltpu.make_async_remote_copy(src_ref=src, dst_ref=dst, send_sem=send_sem, recv_sem=recv_sem,
                                        device_id=dev, device_id_type=pl.DeviceIdType.MESH)


_ANY = pl.BlockSpec(memory_space=pl.ANY)


def _gather(srcs, dtypes, *, name):
    n = len(srcs)

    def body(*refs):
        start, forward, finish = _gather_phases(refs[:n], refs[n:2 * n], refs[2 * n:3 * n], *refs[3 * n:], dtypes)
        start()
        forward()
        finish()

    return pl.pallas_call(
        body, name=name, in_specs=[pl.BlockSpec(memory_space=pltpu.VMEM)] * n, out_specs=[_ANY] * n,
        out_shape=_gather_out_shapes(srcs, dtypes), scratch_shapes=_gather_scratch(srcs, dtypes),
        compiler_params=_cp(has_side_effects=True),
    )(*srcs)


def _gather_out_shapes(srcs, dtypes):
    return [jax.ShapeDtypeStruct((N_DEV, *s.shape), dt) for s, dt in zip(srcs, dtypes)]


def _gather_scratch(srcs, dtypes):
    n = len(srcs)
    return [pltpu.VMEM(s.shape, dt) for s, dt in zip(srcs, dtypes)] + [
        pltpu.SemaphoreType.DMA((n, 2 * N_CHIP - 1)), pltpu.SemaphoreType.DMA((n, 2 * N_CHIP - 1)),
        pltpu.SemaphoreType.DMA((n,))]


def _gather_phases(ins, outs, stages, send_sems, recv_sems, local_sems, dtypes):
    n = len(ins)
    x, y, c, chip = _place()
    me = 2 * chip + c
    sibling = (x, y, 1 - c)

    def own(i):
        cps = [_remote(stages[i], outs[i].at[me], send_sems.at[i, 0], recv_sems.at[i, 0], sibling)]
        for j in range(1, N_CHIP):
            cps.append(_remote(stages[i], outs[i].at[me], send_sems.at[i, j], recv_sems.at[i, j],
                               _chip_dev(jnp.bitwise_xor(chip, j), c)))
        return cps

    def local(i):
        return pltpu.make_async_copy(stages[i], outs[i].at[me], local_sems.at[i])

    def passed_on(i, j):
        slot = outs[i].at[2 * jnp.bitwise_xor(chip, j) + c]
        return _remote(slot, slot, send_sems.at[i, N_CHIP - 1 + j], recv_sems.at[i, N_CHIP - 1 + j], sibling)

    def start():
        for i in range(n):
            stages[i][...] = ins[i][...].astype(dtypes[i])
            local(i).start()
            for cp in own(i):
                cp.start()

    def forward():
        for i in range(n):
            for j in range(1, N_CHIP):
                own(i)[j].wait_recv()
                passed_on(i, j).start()

    def finish():
        for i in range(n):
            own(i)[0].wait_recv()
            for j in range(1, N_CHIP):
                passed_on(i, j).wait_recv()
        for i in range(n):
            for cp in own(i):
                cp.wait_send()
            for j in range(1, N_CHIP):
                passed_on(i, j).wait_send()
            local(i).wait()

    return start, forward, finish


def _scatter_scratch(n):
    return [pltpu.SemaphoreType.DMA((n, N_DEV - 1)), pltpu.SemaphoreType.DMA((n, N_DEV - 1)),
            pltpu.SemaphoreType.DMA((n,))]


def _scatter_phases(ins, outs, send_sems, recv_sems, local_sems):
    n = len(ins)
    _, _, c, chip = _place()
    me = 2 * chip + c

    def copies():
        cps = []
        for i in range(n):
            cps.append(pltpu.make_async_copy(ins[i].at[me], outs[i].at[me], local_sems.at[i]))
            for k in range(1, N_DEV):
                peer = jnp.bitwise_xor(me, k)
                cps.append(_remote(ins[i].at[peer], outs[i].at[me], send_sems.at[i, k - 1], recv_sems.at[i, k - 1],
                                   _chip_dev(peer // 2, peer % 2)))
        return cps

    def start():
        for cp in copies():
            cp.start()

    def finish():
        for cp in copies():
            cp.wait()

    return start, finish


def _adamw(w, parts, m, v, *, name):
    r, c = w.shape
    s = parts.shape[0]
    tm = _row_tile(r, 128) if r % 8 == 0 else r
    c1 = 1.0 - ADAM_B1 ** ADAM_STEP
    c2 = 1.0 - ADAM_B2 ** ADAM_STEP

    def body(w_ref, p_ref, m_ref, v_ref, g_ref, d_ref, m2_ref, v2_ref):
        g = p_ref[0].astype(F32)
        for j in range(1, s):
            g = g + p_ref[j].astype(F32)
        m2 = ADAM_B1 * m_ref[...] + (1.0 - ADAM_B1) * g
        v2 = ADAM_B2 * v_ref[...] + (1.0 - ADAM_B2) * jnp.square(g)
        g_ref[...] = g
        m2_ref[...] = m2
        v2_ref[...] = v2
        d_ref[...] = -ADAM_LR * ((m2 / c1) / (jnp.sqrt(v2 / c2) + ADAM_EPS) + ADAM_WD * w_ref[...])

    blk = pl.BlockSpec((tm, c), lambda i: (i, 0))
    return pl.pallas_call(
        body, name=name, grid=(r // tm,),
        in_specs=[blk, pl.BlockSpec((s, tm, c), lambda i: (0, i, 0)), blk, blk], out_specs=[blk] * 4,
        out_shape=[jax.ShapeDtypeStruct((r, c), F32)] * 4, compiler_params=_cp(("parallel",)),
    )(w, parts, m, v)


_WEIGHTS = ("meta_tokens", "ab_w_in", "ab_conv_w", "ab_a_log", "ab_dt_bias", "ab_gnorm_g", "ab_w_out", "c_w_in",
            "c_lb_raw", "c_gnorm_g", "c_w_out", "ln_mix_g", "ln_mix_b", "mlp_w1", "mlp_w2", "ln_ffn_g", "ln_ffn_b")
_PACK_ROWS = (("ln_mix_g", 0), ("ln_mix_b", 2), ("ln_ffn_g", 4), ("ln_ffn_b", 6), ("c_lb_raw", 8))
_PACK_MISC_ROW = 10
_PACK_MISC = (("ab_gnorm_g", 0, 128), ("c_gnorm_g", 128, 128), ("ab_a_log", 256, GDN_HEADS), ("ab_dt_bias", 260, GDN_HEADS))
_PACK_N = 16
_SMALL_META = 16
_SMALL_CONV = 32
_SMALL_N = 40


def _pack_replicated(p):
    rows = jnp.zeros((_PACK_N, D_MODEL), F32)
    for name, r0 in _PACK_ROWS:
        rows = rows.at[r0:r0 + 2].set(p[name])
    for name, c0, width in _PACK_MISC:
        rows = rows.at[_PACK_MISC_ROW, c0:c0 + width].set(p[name].reshape(width))
    return rows


def _unpack_replicated(rows, like):
    out = {}
    for name, r0 in _PACK_ROWS:
        out[name] = rows[r0:r0 + 2]
    for name, c0, width in _PACK_MISC:
        out[name] = rows[_PACK_MISC_ROW, c0:c0 + width].reshape(like[name].shape)
    return out


def _lower_bound(c_lb_raw):
    lb_all = jnp.cumsum(jax.nn.softmax(c_lb_raw.astype(F32), axis=0), axis=0)
    return (lb_all - lb_all[0:1])[1].reshape(1, -1)


def kernel(x, meta_tokens, ab_w_in, ab_conv_w, ab_a_log, ab_dt_bias, ab_gnorm_g, ab_w_out, c_w_in, c_lb_raw, c_gnorm_g, c_w_out, ln_mix_g, ln_mix_b, mlp_w1, mlp_w2, ln_ffn_g, ln_ffn_b, loss_target, m_meta_tokens, m_ab_w_in, m_ab_conv_w, m_ab_a_log, m_ab_dt_bias, m_ab_gnorm_g, m_ab_w_out, m_c_w_in, m_c_lb_raw, m_c_gnorm_g, m_c_w_out, m_ln_mix_g, m_ln_mix_b, m_mlp_w1, m_mlp_w2, m_ln_ffn_g, m_ln_ffn_b, v_meta_tokens, v_ab_w_in, v_ab_conv_w, v_ab_a_log, v_ab_dt_bias, v_ab_gnorm_g, v_ab_w_out, v_c_w_in, v_c_lb_raw, v_c_gnorm_g, v_c_w_out, v_ln_mix_g, v_ln_mix_b, v_mlp_w1, v_mlp_w2, v_ln_ffn_g, v_ln_ffn_b):
    w = dict(zip(_WEIGHTS, (meta_tokens, ab_w_in, ab_conv_w, ab_a_log, ab_dt_bias, ab_gnorm_g, ab_w_out, c_w_in, c_lb_raw,
                            c_gnorm_g, c_w_out, ln_mix_g, ln_mix_b, mlp_w1, mlp_w2, ln_ffn_g, ln_ffn_b)))
    mom = dict(zip(_WEIGHTS, (m_meta_tokens, m_ab_w_in, m_ab_conv_w, m_ab_a_log, m_ab_dt_bias, m_ab_gnorm_g, m_ab_w_out,
                              m_c_w_in, m_c_lb_raw, m_c_gnorm_g, m_c_w_out, m_ln_mix_g, m_ln_mix_b, m_mlp_w1, m_mlp_w2,
                              m_ln_ffn_g, m_ln_ffn_b)))
    var = dict(zip(_WEIGHTS, (v_meta_tokens, v_ab_w_in, v_ab_conv_w, v_ab_a_log, v_ab_dt_bias, v_ab_gnorm_g, v_ab_w_out,
                              v_c_w_in, v_c_lb_raw, v_c_gnorm_g, v_c_w_out, v_ln_mix_g, v_ln_mix_b, v_mlp_w1, v_mlp_w2,
                              v_ln_ffn_g, v_ln_ffn_b)))
    me = 4 * lax.axis_index("x") + 2 * lax.axis_index("y") + lax.axis_index("c")
    seq = x.shape[1]
    pad = (-(N_META + seq)) % SB_BLOCK
    lp = pad + N_META + seq
    meta_w = D_MODEL // N_DEV
    conv_w_all = 2 * GDN_HEADS * HEAD_W + GDN_HEADS * HEAD_W
    conv_w_mine = conv_w_all // N_DEV

    g_meta, g_conv, g_ab_in = _gather([w["meta_tokens"], w["ab_conv_w"][0], w["ab_w_in"][0]], [F32, F32, BF16],
                                      name="gather_weights_first")
    meta_full = g_meta.transpose(1, 0, 2).reshape(N_META, D_MODEL)
    conv_full = g_conv.transpose(1, 0, 2).reshape(CONV_K, conv_w_all)
    ab_full = g_ab_in.transpose(1, 0, 2).reshape(D_MODEL, AB_IN)
    ba0 = AB_Z + 512
    w_ab = jnp.concatenate([ab_full[:, :ba0], ab_full[:, ba0 + 2 * GDN_HEADS:], ab_full[:, ba0:ba0 + 2 * GDN_HEADS],
                            jnp.zeros((D_MODEL, AB_CAT - AB_IN), BF16)], axis=1)
    vec128 = lambda p: jnp.zeros((1, HEAD_W), F32).at[0, :GDN_HEADS].set(p.reshape(GDN_HEADS))
    wts = dict(
        w_ab=w_ab, conv_w=conv_full, alog_v=vec128(w["ab_a_log"]), dtb_v=vec128(w["ab_dt_bias"]),
        ab_gn=w["ab_gnorm_g"][0], lb=_lower_bound(w["c_lb_raw"]), c_gn=w["c_gnorm_g"][0],
        ln_mix_g=w["ln_mix_g"], ln_mix_b=w["ln_mix_b"], ln_ffn_g=w["ln_ffn_g"], ln_ffn_b=w["ln_ffn_b"])

    def weights_a(gathered):
        g_ab_out, g_w1, g_w2 = gathered
        return dict(w_out0=g_ab_out.reshape(D_MODEL, D_MODEL), w1=[g_w1], w2=[g_w2.reshape(D_FF, D_MODEL)])

    def weights_b(gathered):
        g_c_in, g_c_out = gathered
        return dict(w_c=g_c_in, w_out1=g_c_out.reshape(D_MODEL, D_MODEL))

    def weights_c(gathered):
        g_w1, g_w2 = gathered
        return dict(w1=[g_w1], w2=[g_w2.reshape(D_FF, D_MODEL)])

    hooks = dict(
        gather_a=([w["ab_w_out"][0], w["mlp_w1"][0], w["mlp_w2"][0]], [BF16] * 3), weights_a=weights_a,
        gather_b=([w["c_w_in"][0], w["c_w_out"][0]], [BF16] * 2), weights_b=weights_b,
        gather_c=([w["mlp_w1"][1], w["mlp_w2"][1]], [BF16] * 2), weights_c=weights_c)

    h0 = jnp.concatenate([jnp.zeros((pad, D_MODEL), F32), meta_full, x[0]], axis=0)
    loss_vec, dh0, g = _local_step(h0, loss_target[0], pad, wts, hooks)
    loss = lax.psum(jnp.sum(loss_vec), ("x", "y", "c"))
    grad_x = dh0[lp - seq:][None]

    _, lb_vjp = jax.vjp(_lower_bound, w["c_lb_raw"])
    rep_part = _pack_replicated(dict(
        ln_mix_g=g["ln_mix_g"], ln_mix_b=g["ln_mix_b"], ln_ffn_g=g["ln_ffn_g"], ln_ffn_b=g["ln_ffn_b"],
        c_lb_raw=lb_vjp(g["lb"])[0], ab_gnorm_g=g["ab_gn"], c_gnorm_g=g["c_gn"],
        ab_a_log=g["alog_v"][0, :GDN_HEADS], ab_dt_bias=g["dtb_v"][0, :GDN_HEADS]))
    small = jnp.concatenate([rep_part, dh0[pad:pad + N_META], g["conv_w"].reshape(-1, D_MODEL),
                             jnp.zeros((_SMALL_N - _SMALL_CONV - CONV_K * conv_w_all // D_MODEL, D_MODEL), F32)], axis=0)
    (small_all,) = _gather([small], [F32], name="gather_small_grads")
    rep_out = _adamw(_pack_replicated(w), small_all[:, :_PACK_N], _pack_replicated(mom), _pack_replicated(var),
                     name="adamw_replicated")
    meta_parts = lax.dynamic_slice_in_dim(small_all[:, _SMALL_META:_SMALL_META + N_META], me * meta_w, meta_w, axis=2)
    meta_out = _adamw(w["meta_tokens"], meta_parts, mom["meta_tokens"], var["meta_tokens"], name="adamw_meta")
    conv_parts = small_all[:, _SMALL_CONV:_SMALL_CONV + CONV_K * conv_w_all // D_MODEL].reshape(N_DEV, CONV_K, conv_w_all)
    conv_parts = lax.dynamic_slice_in_dim(conv_parts, me * conv_w_mine, conv_w_mine, axis=2)
    conv_out = _adamw(w["ab_conv_w"][0], conv_parts, mom["ab_conv_w"][0], var["ab_conv_w"][0], name="adamw_conv")

    parts = g["parts"]
    big = [("ab_w_in", 0, parts["ab_w_in"]), ("ab_w_out", 0, parts["ab_w_out"]), ("mlp_w1", 0, parts["mlp_w1_0"]),
           ("mlp_w2", 0, parts["mlp_w2_0"]), ("c_w_in", 0, parts["c_w_in"]), ("c_w_out", 0, parts["c_w_out"]),
           ("mlp_w1", 1, parts["mlp_w1_1"]), ("mlp_w2", 1, parts["mlp_w2_1"])]
    big_out = {}
    for name, l, p in big:
        res = _adamw(w[name][l], p, mom[name][l], var[name][l], name=f"adamw_{name}{l}")
        big_out.setdefault(name, []).append(res)

    rep = [_unpack_replicated(r, w) for r in rep_out]
    outs = {}
    for name in _WEIGHTS:
        if name == "meta_tokens":
            outs[name] = list(meta_out)
        elif name == "ab_conv_w":
            outs[name] = [o[None] for o in conv_out]
        elif name in big_out:
            res = big_out[name]
            outs[name] = [o[None] for o in res[0]] if len(res) == 1 else [jnp.stack(pair) for pair in zip(*res)]
        else:
            outs[name] = [r[name] for r in rep]
    flat = [loss, grad_x]
    for kind in range(4):
        flat += [outs[name][kind] for name in _WEIGHTS]
    return tuple(flat)
```

```python
import functools

import jax
import jax.numpy as jnp
from jax import lax
from jax.experimental import pallas as pl
from jax.experimental.pallas import tpu as pltpu

F32 = jnp.float32
BF16 = jnp.bfloat16

N_DEV = 8
D_MODEL = 1024
N_META = 16
D_FF = 4096
DEPTH = 2
GDN_HEADS = 4
SB_HEADS = 8
SB_DH = 64
HG_HEADS = 8
HEAD_W = 128
CHUNK = 64
SB_BLOCK = 128
CONV_K = 4
DN_ALPHA = float((2 * DEPTH) ** 0.25)
LN_EPS = 1e-5
RMS_EPS = 1e-6
L2_EPS = 1e-6
ADAM_LR, ADAM_B1, ADAM_B2, ADAM_EPS, ADAM_WD, ADAM_STEP = 0.001, 0.9, 0.999, 1e-08, 0.01, 10

AB_Z = 1536
AB_SB = 2048
AB_BA = 3584
AB_CAT = 3840
AB_IN = 3592

VMEM_LIMIT = 56 * 1024 * 1024


def _cp(sem=None, **kw):
    if sem is not None:
        kw["dimension_semantics"] = sem
    return pltpu.CompilerParams(vmem_limit_bytes=VMEM_LIMIT, **kw)


def _row_tile(n, want):
    best = 8
    for t in range(8, min(n, want) + 1, 8):
        if n % t == 0:
            best = t
    return best


@jax.custom_vjp
def _sigmoid(x):
    e = jnp.exp(-jnp.abs(x))
    r = 1.0 / (1.0 + e)
    return jnp.where(x >= 0, r, e * r)


def _sigmoid_fwd(x):
    s = _sigmoid(x)
    return s, s


def _sigmoid_bwd(s, g):
    return (g * s * (1.0 - s),)


_sigmoid.defvjp(_sigmoid_fwd, _sigmoid_bwd)


def _log1p_exp_neg_abs(x):
    e = jnp.exp(-jnp.abs(x))
    return jnp.where(e < 1e-4, e - 0.5 * e * e, jnp.log(1.0 + e))


@jax.custom_vjp
def _softplus(x):
    return jnp.maximum(x, 0.0) + _log1p_exp_neg_abs(x)


def _softplus_fwd(x):
    return _softplus(x), x


def _softplus_bwd(x, g):
    return (g * _sigmoid(x),)


_softplus.defvjp(_softplus_fwd, _softplus_bwd)


def _silu(x):
    return x * _sigmoid(x)


def _silu_grad(x):
    s = _sigmoid(x)
    return s * (1.0 + x * (1.0 - s))


def _dot(a, b, dims, precision=None):
    return lax.dot_general(a, b, (dims, ((), ())), precision=precision, preferred_element_type=F32)


NN = ((1,), (0,))
NT = ((1,), (1,))
TN = ((0,), (0,))


def _bdot(a, b, dims):
    return _dot(a.astype(BF16), b.astype(BF16), dims)


def _layer_norm(pre, g, beta):
    mu = jnp.mean(pre, axis=-1, keepdims=True)
    xc = pre - mu
    var = jnp.mean(xc * xc, axis=-1, keepdims=True)
    return xc * lax.rsqrt(var + LN_EPS) * g + beta


def _layer_norm_bwd(pre, g, dy):
    mu = jnp.mean(pre, axis=-1, keepdims=True)
    xc = pre - mu
    rstd = lax.rsqrt(jnp.mean(xc * xc, axis=-1, keepdims=True) + LN_EPS)
    xhat = xc * rstd
    dxh = dy * g
    m1 = jnp.mean(dxh, axis=-1, keepdims=True)
    m2 = jnp.mean(dxh * xhat, axis=-1, keepdims=True)
    return (rstd * (dxh - m1 - xhat * m2), jnp.sum(dy * xhat, axis=0, keepdims=True),
            jnp.sum(dy, axis=0, keepdims=True))


def _mm(a, b, mode, *, tm, tn, tk, name, epi=None, c=None, scale=1.0, b_dev=False, out_dev=False, out_dtype=F32,
        ln=None, scatter=()):
    if mode == "NN":
        m, kk = a.shape
        n = b.shape[2] * N_DEV if b_dev else b.shape[1]
    elif mode == "NT":
        m, kk = a.shape
        n = b.shape[1] if b_dev else b.shape[0]
    else:
        kk, m = a.shape
        n = b.shape[1]
    assert m % tm == 0 and n % tn == 0 and kk % tk == 0, (name, m, n, kk, tm, tn, tk)
    nk = kk // tk
    dims = {"NN": NN, "NT": NT, "TN": TN}[mode]

    if mode == "TN":
        a_spec = pl.BlockSpec((tk, tm), lambda i, j, k: (k, i))
    else:
        a_spec = pl.BlockSpec((tm, tk), lambda i, j, k: (i, k))
    if mode == "NN":
        if b_dev:
            assert tn == b.shape[2]
            b_spec = pl.BlockSpec((None, tk, tn), lambda i, j, k: (j, k, 0))
        else:
            b_spec = pl.BlockSpec((tk, tn), lambda i, j, k: (k, j))
    elif mode == "NT":
        if b_dev:
            assert tk == b.shape[2]
            b_spec = pl.BlockSpec((None, tn, tk), lambda i, j, k: (k, j, 0))
        else:
            b_spec = pl.BlockSpec((tn, tk), lambda i, j, k: (j, k))
    else:
        b_spec = pl.BlockSpec((tk, tn), lambda i, j, k: (k, j))
    in_specs = [a_spec, b_spec]
    operands = [a, b]
    if c is not None:
        in_specs.append(pl.BlockSpec((tm, tn), lambda i, j, k: (i, j)))
        operands.append(c)
    if epi == "ln":
        assert tn == n and not out_dev
        in_specs += [pl.BlockSpec((1, n), lambda i, j, k: (0, 0))] * 2
        operands += [ln[0].reshape(1, n), ln[1].reshape(1, n)]
    elif epi == "ln_bwd":
        assert tn == n and not out_dev
        in_specs += [pl.BlockSpec((tm, tn), lambda i, j, k: (i, j)), pl.BlockSpec((1, n), lambda i, j, k: (0, 0))]
        operands += [ln[0], ln[1].reshape(1, n)]
    if out_dev:
        assert tn == n // N_DEV
        out_shape = jax.ShapeDtypeStruct((N_DEV, m, tn), out_dtype)
        out_spec = pl.BlockSpec((None, tm, tn), lambda i, j, k: (j, i, 0))
    else:
        out_shape = jax.ShapeDtypeStruct((m, n), out_dtype)
        out_spec = pl.BlockSpec((tm, tn), lambda i, j, k: (i, j))
    if epi == "ln":
        out_shape = [out_shape, out_shape, jax.ShapeDtypeStruct((m, n), BF16)]
        out_spec = [out_spec] * 3
    elif epi == "relu2_copy":
        assert not out_dev
        out_shape = [out_shape, jax.ShapeDtypeStruct((m, n), BF16)]
        out_spec = [out_spec] * 2
    elif epi == "ln_bwd":
        vec_shape, vec_spec = jax.ShapeDtypeStruct((1, n), F32), pl.BlockSpec((1, n), lambda i, j, k: (0, 0))
        out_shape = [out_shape, jax.ShapeDtypeStruct((m, n), BF16), vec_shape, vec_shape]
        out_spec = [out_spec, out_spec, vec_spec, vec_spec]
    n_out = {"ln": 3, "relu2_copy": 2, "ln_bwd": 4}.get(epi, 1)
    ns = len(scatter)
    if ns:
        in_specs += [_ANY] * ns
        operands += list(scatter)
        out_shape = (out_shape if n_out > 1 else [out_shape]) + [jax.ShapeDtypeStruct(s.shape, s.dtype) for s in scatter]
        out_spec = (out_spec if n_out > 1 else [out_spec]) + [_ANY] * ns
    n_in = len(operands)
    grid = (m // tm, n // tn, nk)

    def body(*refs):
        a_ref, b_ref = refs[0], refs[1]
        c_ref = refs[2] if c is not None else None
        o_ref = refs[n_in]
        scratch0 = n_in + n_out + ns
        acc_ref = refs[scratch0] if nk > 1 else None
        if ns:
            s_start, s_finish = _scatter_phases(refs[n_in - ns:n_in], refs[n_in + n_out:scratch0],
                                                *refs[scratch0 + (1 if nk > 1 else 0):])
            at = lambda step: functools.reduce(lambda x, y: x & y, [pl.program_id(ax) == step[ax] for ax in range(3)])
            pl.when(at((0, 0, 0)))(s_start)
        p = _dot(a_ref[...].astype(BF16), b_ref[...].astype(BF16), dims)
        first_rows = pl.program_id(0) == 0

        def finish(acc):
            if epi == "add":
                acc = acc + scale * c_ref[...]
            elif epi == "relu2grad":
                acc = acc * (2.0 * jnp.maximum(c_ref[...], 0.0))
            elif epi == "relu2_copy":
                refs[n_in + 1][...] = jnp.square(jnp.maximum(acc, 0.0)).astype(BF16)
            elif epi == "ln_bwd":
                acc, dg, db = _layer_norm_bwd(refs[3][...], refs[4][...], acc + scale * c_ref[...])
                dg_ref, db_ref = refs[n_in + 2], refs[n_in + 3]

                @pl.when(first_rows)
                def _():
                    dg_ref[...] = jnp.zeros_like(dg_ref)
                    db_ref[...] = jnp.zeros_like(db_ref)

                dg_ref[...] += dg
                db_ref[...] += db
                refs[n_in + 1][...] = acc.astype(BF16)
            elif epi == "ln":
                acc = acc + scale * c_ref[...]
                y = _layer_norm(acc, refs[3][...], refs[4][...])
                refs[n_in + 1][...] = y
                refs[n_in + 2][...] = y.astype(BF16)
            o_ref[...] = acc.astype(out_dtype)

        if nk == 1:
            finish(p)
        else:
            k = pl.program_id(2)

            @pl.when(k == 0)
            def _():
                acc_ref[...] = p

            @pl.when(k > 0)
            def _():
                acc_ref[...] += p

            @pl.when(k == nk - 1)
            def _():
                finish(acc_ref[...])

        if ns:
            pl.when(at(tuple(g - 1 for g in grid)))(s_finish)

    res = pl.pallas_call(
        body, name=name, grid=grid, in_specs=in_specs, out_specs=out_spec, out_shape=out_shape,
        scratch_shapes=([pltpu.VMEM((tm, tn), F32)] if nk > 1 else []) + (_scatter_scratch(ns) if ns else []),
        compiler_params=_cp(("arbitrary",) * 3 if ns or epi == "ln_bwd" else ("parallel", "parallel", "arbitrary"),
                            has_side_effects=bool(ns)),
    )(*operands)
    return res


def _ln_bwd(pre, g, dy, *, name):
    lp, d = pre.shape
    tm = _row_tile(lp, 512)

    def body(pre_ref, g_ref, dy_ref, dpre_ref, dpreb_ref, dg_ref, db_ref):
        dpre, dg, db = _layer_norm_bwd(pre_ref[...], g_ref[...], dy_ref[...])
        dpre_ref[...] = dpre
        dpreb_ref[...] = dpre.astype(BF16)

        @pl.when(pl.program_id(0) == 0)
        def _():
            dg_ref[...] = jnp.zeros_like(dg_ref)
            db_ref[...] = jnp.zeros_like(db_ref)

        dg_ref[...] += dg
        db_ref[...] += db

    row = pl.BlockSpec((tm, d), lambda i: (i, 0))
    vec = pl.BlockSpec((1, d), lambda i: (0, 0))
    return pl.pallas_call(
        body, name=name, grid=(lp // tm,), in_specs=[row, vec, row], out_specs=[row, row, vec, vec],
        out_shape=[jax.ShapeDtypeStruct((lp, d), F32), jax.ShapeDtypeStruct((lp, d), BF16),
                   jax.ShapeDtypeStruct((1, d), F32), jax.ShapeDtypeStruct((1, d), F32)],
        compiler_params=_cp(("arbitrary",)),
    )(pre, g.reshape(1, d), dy)


def _loss_head(y, target, *, name):
    lp, d = y.shape
    seq = target.shape[0]
    tm = SB_BLOCK
    first = (lp - seq) // tm
    assert (lp - seq) % tm == 0 and seq % tm == 0

    def body(y_ref, t_ref, dy_ref, loss_ref):
        i = pl.program_id(0)
        live = i >= first
        diff = jnp.where(live, y_ref[...] - t_ref[...], 0.0)
        dy_ref[...] = diff * (1.0 / d)

        @pl.when(i == 0)
        def _():
            loss_ref[...] = jnp.zeros_like(loss_ref)

        loss_ref[...] += jnp.sum(diff * diff, axis=0, keepdims=True) * (0.5 / d)

    return pl.pallas_call(
        body, name=name, grid=(lp // tm,),
        in_specs=[pl.BlockSpec((tm, d), lambda i: (i, 0)),
                  pl.BlockSpec((tm, d), lambda i: (jnp.maximum(i - first, 0), 0))],
        out_specs=[pl.BlockSpec((tm, d), lambda i: (i, 0)), pl.BlockSpec((1, d), lambda i: (0, 0))],
        out_shape=[jax.ShapeDtypeStruct((lp, d), F32), jax.ShapeDtypeStruct((1, d), F32)],
        compiler_params=_cp(("arbitrary",)),
    )(y, target)


def _gate_fwd(o, zsrc, z_blk0, g, other, *, heads, name):
    lp = o.shape[0]
    tm = _row_tile(lp, 512)
    w = heads * HEAD_W
    assert (z_blk0 * HEAD_W) % w == 0
    has_other = w < D_MODEL

    def body(o_ref, z_ref, g_ref, *rest):
        y_ref = rest[-1]
        gv = g_ref[...]
        for h in range(heads):
            cs = slice(h * HEAD_W, (h + 1) * HEAD_W)
            ov = o_ref[:, cs]
            r = lax.rsqrt(jnp.mean(ov * ov, axis=-1, keepdims=True) + RMS_EPS)
            y_ref[:, cs] = (ov * r * gv * _silu(z_ref[:, cs])).astype(BF16)
        if has_other:
            y_ref[:, w:] = rest[0][...].astype(BF16)

    row = lambda width, blk: pl.BlockSpec((tm, width), lambda i: (i, blk))
    return pl.pallas_call(
        body, name=name, grid=(lp // tm,),
        in_specs=[row(w, 0), row(w, z_blk0 * HEAD_W // w), pl.BlockSpec((1, HEAD_W), lambda i: (0, 0))]
        + ([row(D_MODEL - w, 0)] if has_other else []),
        out_specs=row(D_MODEL, 0), out_shape=jax.ShapeDtypeStruct((lp, D_MODEL), BF16),
        compiler_params=_cp(("parallel",)),
    )(o, zsrc, g.reshape(1, HEAD_W), *([other] if has_other else []))


def _gate_bwd(o, zsrc, z_blk0, g, dy, *, heads, name):
    lp = o.shape[0]
    tm = _row_tile(lp, 512)

    w = heads * HEAD_W
    assert (z_blk0 * HEAD_W) % w == 0

    def body(o_ref, z_ref, g_ref, dy_ref, do_ref, dz_ref, dg_ref):
        @pl.when(pl.program_id(0) == 0)
        def _():
            dg_ref[...] = jnp.zeros_like(dg_ref)

        gv = g_ref[...]
        dg = jnp.zeros((1, HEAD_W), F32)
        for h in range(heads):
            cs = slice(h * HEAD_W, (h + 1) * HEAD_W)
            ov, zv, dyv = o_ref[:, cs], z_ref[:, cs], dy_ref[:, cs]
            r = lax.rsqrt(jnp.mean(ov * ov, axis=-1, keepdims=True) + RMS_EPS)
            nrm = ov * r
            s = _silu(zv)
            dn = dyv * gv * s
            do_ref[:, cs] = r * (dn - nrm * jnp.mean(dn * nrm, axis=-1, keepdims=True))
            dz_ref[:, cs] = dyv * nrm * gv * _silu_grad(zv)
            dg = dg + jnp.sum(dyv * nrm * s, axis=0, keepdims=True)
        dg_ref[...] += dg

    row = lambda blk: pl.BlockSpec((tm, w), lambda i: (i, blk))
    vec = pl.BlockSpec((1, HEAD_W), lambda i: (0, 0))
    return pl.pallas_call(
        body, name=name, grid=(lp // tm,),
        in_specs=[row(0), row(z_blk0 * HEAD_W // w), vec, row(0)], out_specs=[row(0), row(0), vec],
        out_shape=[jax.ShapeDtypeStruct((lp, w), F32), jax.ShapeDtypeStruct((lp, w), F32),
                   jax.ShapeDtypeStruct((1, HEAD_W), F32)],
        compiler_params=_cp(("arbitrary",)),
    )(o, zsrc, g.reshape(1, HEAD_W), dy)


def _conv_taps(x, w):
    acc = w[CONV_K - 1:CONV_K, :] * x
    for k in range(CONV_K - 1):
        acc = acc + w[k:k + 1, :] * pltpu.roll(x, CONV_K - 1 - k, 0)
    return acc


def _gdn_pre_fwd(p0, conv_w, pad, *, name):
    lp = p0.shape[0]
    nq = GDN_HEADS
    qscale = HEAD_W ** -0.5

    def body(x_ref, w_ref, y_ref):
        j = pl.program_id(0)
        c = _conv_taps(x_ref[...], w_ref[...])
        s = _silu(c)
        r = lax.rsqrt(jnp.sum(s * s, axis=-1, keepdims=True) + L2_EPS)
        mult = jnp.where(j < nq, r * qscale, jnp.where(j < 2 * nq, r, 1.0))
        rows = lax.broadcasted_iota(jnp.int32, (lp, 1), 0)
        y_ref[...] = jnp.where(rows >= pad, s * mult, 0.0)

    return pl.pallas_call(
        body, name=name, grid=(3 * nq,),
        in_specs=[pl.BlockSpec((lp, HEAD_W), lambda j: (0, j)), pl.BlockSpec((CONV_K, HEAD_W), lambda j: (0, j))],
        out_specs=pl.BlockSpec((lp, HEAD_W), lambda j: (0, j)),
        out_shape=jax.ShapeDtypeStruct((lp, 3 * nq * HEAD_W), F32), compiler_params=_cp(("parallel",)),
    )(p0, conv_w)


def _gdn_pre_bwd(p0, conv_w, dqkv, pad, *, name):
    lp = p0.shape[0]
    nq = GDN_HEADS
    qscale = HEAD_W ** -0.5

    def body(x_ref, w_ref, dy_ref, dx_ref, dw_ref):
        j = pl.program_id(0)
        x, w = x_ref[...], w_ref[...]
        c = _conv_taps(x, w)
        s = _silu(c)
        r = lax.rsqrt(jnp.sum(s * s, axis=-1, keepdims=True) + L2_EPS)
        rows = lax.broadcasted_iota(jnp.int32, (lp, 1), 0)
        dy = jnp.where(rows >= pad, dy_ref[...], 0.0)
        nrm = s * r
        dn = dy * jnp.where(j < nq, qscale, 1.0)
        ds_norm = r * (dn - nrm * jnp.sum(nrm * dn, axis=-1, keepdims=True))
        ds = jnp.where(j < 2 * nq, ds_norm, dy)
        dc = ds * _silu_grad(c)
        dx = w[CONV_K - 1:CONV_K, :] * dc
        dws = [None] * CONV_K
        dws[CONV_K - 1] = jnp.sum(dc * x, axis=0, keepdims=True)
        for k in range(CONV_K - 1):
            sh = CONV_K - 1 - k
            dx = dx + w[k:k + 1, :] * pltpu.roll(dc, lp - sh, 0)
            dws[k] = jnp.sum(dc * pltpu.roll(x, sh, 0), axis=0, keepdims=True)
        dx_ref[...] = dx
        dw_ref[...] = jnp.concatenate(dws, axis=0)

    blk = pl.BlockSpec((lp, HEAD_W), lambda j: (0, j))
    wblk = pl.BlockSpec((CONV_K, HEAD_W), lambda j: (0, j))
    return pl.pallas_call(
        body, name=name, grid=(3 * nq,), in_specs=[blk, wblk, blk], out_specs=[blk, wblk],
        out_shape=[jax.ShapeDtypeStruct((lp, 3 * nq * HEAD_W), F32),
                   jax.ShapeDtypeStruct((CONV_K, 3 * nq * HEAD_W), F32)],
        compiler_params=_cp(("parallel",)),
    )(p0, conv_w, dqkv)


@jax.custom_vjp
def _inv_unit_lower(m):
    c = m.shape[0]
    eye = (lax.broadcasted_iota(jnp.int32, (c, c), 0) == lax.broadcasted_iota(jnp.int32, (c, c), 1)).astype(F32)
    x = eye - m
    p = m
    n = 2
    while n < CHUNK:
        p = _bdot(p, p, NN)
        x = x + _bdot(x, p, NN)
        n *= 2
    return x


def _inv_fwd(m):
    t = _inv_unit_lower(m)
    return t, t


def _inv_bwd(t, g):
    return (-_bdot(_bdot(t, g, TN), t, NT),)


_inv_unit_lower.defvjp(_inv_fwd, _inv_bwd)


GDN_STEP = 6


def _heads_to_rows(x, nh):
    return jnp.concatenate([x[:, h * HEAD_W:(h + 1) * HEAD_W] for h in range(nh)], axis=0)


def _rows_to_heads(x, nh):
    c = x.shape[0] // nh
    return jnp.concatenate([x[h * c:(h + 1) * c] for h in range(nh)], axis=1)


def _gdn_chunk(q, k, v, ba, alog, dtb, states, valid):
    nh = GDN_HEADS
    c = q.shape[0]
    r = nh * c
    lane = lax.broadcasted_iota(jnp.int32, (1, HEAD_W), 1)
    pick = lambda x, l: jnp.sum(jnp.where(lane == l, x, 0.0), axis=-1, keepdims=True)
    beta = jnp.concatenate([jnp.where(valid, _sigmoid(pick(ba, h)), 0.0) for h in range(nh)], axis=0)
    g = jnp.concatenate(
        [jnp.where(valid, -jnp.exp(pick(alog, h)) * _softplus(pick(ba, nh + h) + pick(dtb, h)), 0.0) for h in range(nh)],
        axis=0)
    qs, ks, vs = _heads_to_rows(q, nh), _heads_to_rows(k, nh), _heads_to_rows(v, nh)
    rr = lax.broadcasted_iota(jnp.int32, (r, r), 0)
    cc = lax.broadcasted_iota(jnp.int32, (r, r), 1)
    same = (rr // c) == (cc // c)
    causal, strict = same & (cc <= rr), same & (cc < rr)
    lower = jnp.where(causal, 1.0, 0.0).astype(BF16)
    upper = jnp.where(same & (cc >= rr), 1.0, 0.0).astype(BF16)
    gcb = _mask_mm(lower, upper, g * jnp.ones((1, HEAD_W), F32))
    gc_col = jnp.concatenate([gcb] * (r // HEAD_W), axis=1)
    decay = jnp.where(causal, jnp.exp(jnp.minimum(gc_col - gc_col.T, 0.0)), 0.0)
    egc = jnp.exp(gcb)
    kb = ks * beta
    m = jnp.where(strict, _dot3(kb, ks, NT) * decay, 0.0)
    t = _inv_unit_lower(m)
    u = _bdot(t, vs * beta, NN)
    w = _bdot(t, kb * egc, NN)
    a = _bdot(qs, ks, NT) * decay
    rows = lambda x, h: x[h * c:(h + 1) * c]
    qe = qs * egc
    v_new = u - jnp.concatenate([_bdot(rows(w, h), states[h], NN) for h in range(nh)], axis=0)
    o = jnp.concatenate([_bdot(rows(qe, h), states[h], NN) for h in range(nh)], axis=0) + _bdot(a, v_new, NN)
    new_states = []
    for h in range(nh):
        gl = gcb[(h + 1) * c - 1:(h + 1) * c, :]
        k_dec = rows(ks, h) * jnp.exp(gl - rows(gcb, h))
        new_states.append(states[h] * jnp.exp(gl) + _bdot(k_dec, rows(v_new, h), TN))
    return _rows_to_heads(o, nh), new_states


def _gdn_fwd(qkv, p0, alog_v, dtb_v, pad, *, name, gather=None):
    lp = qkv.shape[0]
    n = lp // CHUNK
    nh = GDN_HEADS
    assert n % GDN_STEP == 0
    steps, rows = n // GDN_STEP, GDN_STEP * CHUNK
    g_srcs, g_dtypes = gather if gather is not None else ([], [])
    ng_arr = len(g_srcs)

    def body(q_ref, k_ref, v_ref, ba_ref, al_ref, dt_ref, *rest):
        g_ins, (o_ref, st_ref) = rest[:ng_arr], rest[ng_arr:ng_arr + 2]
        g_outs, s_ref, g_scratch = rest[ng_arr + 2:2 * ng_arr + 2], rest[2 * ng_arr + 2], rest[2 * ng_arr + 3:]
        i = pl.program_id(0)
        if ng_arr:
            g_start, g_forward, g_finish = _gather_phases(g_ins, g_outs, g_scratch[:ng_arr], *g_scratch[ng_arr:],
                                                          g_dtypes)
            pl.when(i == 0)(g_start)
            pl.when(i == (3 * steps) // 4)(g_forward)

        @pl.when(i == 0)
        def _():
            s_ref[...] = jnp.zeros_like(s_ref)

        s = s_ref[...]
        s = [s[h] for h in range(nh)]
        q, k, v, ba, al, dt = q_ref[...], k_ref[...], v_ref[...], ba_ref[...], al_ref[...], dt_ref[...]
        outs = []
        for c in range(GDN_STEP):
            sl = slice(c * CHUNK, (c + 1) * CHUNK)
            valid = (i * rows + c * CHUNK + lax.broadcasted_iota(jnp.int32, (CHUNK, 1), 0)) >= pad
            for h in range(nh):
                st_ref[c, h] = s[h]
            o, s = _gdn_chunk(q[sl], k[sl], v[sl], ba[sl], al, dt, s, valid)
            outs.append(o)
        o_ref[...] = jnp.concatenate(outs, axis=0)
        for h in range(nh):
            s_ref[h] = s[h]
        if ng_arr:
            pl.when(i == steps - 1)(g_finish)

    w = nh * HEAD_W
    vec = pl.BlockSpec((1, HEAD_W), lambda i: (0, 0))
    return pl.pallas_call(
        body, name=name, grid=(steps,),
        in_specs=[pl.BlockSpec((rows, w), lambda i: (i, 0)), pl.BlockSpec((rows, w), lambda i: (i, 1)),
                  pl.BlockSpec((rows, w), lambda i: (i, 2)), pl.BlockSpec((rows, HEAD_W), lambda i: (i, AB_BA // HEAD_W)),
                  vec, vec] + [pl.BlockSpec(memory_space=pltpu.VMEM)] * ng_arr,
        out_specs=[pl.BlockSpec((rows, w), lambda i: (i, 0)),
                   pl.BlockSpec((GDN_STEP, nh, HEAD_W, HEAD_W), lambda i: (i, 0, 0, 0))] + [_ANY] * ng_arr,
        out_shape=[jax.ShapeDtypeStruct((lp, w), F32), jax.ShapeDtypeStruct((n, nh, HEAD_W, HEAD_W), F32)]
        + _gather_out_shapes(g_srcs, g_dtypes),
        scratch_shapes=[pltpu.VMEM((nh, HEAD_W, HEAD_W), F32)] + (_gather_scratch(g_srcs, g_dtypes) if ng_arr else []),
        compiler_params=_cp(("arbitrary",), has_side_effects=bool(ng_arr)),
    )(qkv, qkv, qkv, p0, alog_v, dtb_v, *g_srcs)


def _gdn_bwd(qkv, p0, alog_v, dtb_v, states, do, pad, *, name, scatter=()):
    lp = qkv.shape[0]
    n = lp // CHUNK
    nh = GDN_HEADS
    assert n % GDN_STEP == 0
    steps, rows = n // GDN_STEP, GDN_STEP * CHUNK
    ns = len(scatter)

    def body(q_ref, k_ref, v_ref, ba_ref, al_ref, dt_ref, st_ref, do_ref, *rest):
        s_ins, (dq_ref, dk_ref, dv_ref, dba_ref, dal_ref, ddt_ref) = rest[:ns], rest[ns:ns + 6]
        s_outs, ds_ref, s_sems = rest[ns + 6:2 * ns + 6], rest[2 * ns + 6], rest[2 * ns + 7:]
        step = pl.program_id(0)
        i = steps - 1 - step
        if ns:
            s_start, s_finish = _scatter_phases(s_ins, s_outs, *s_sems)
            pl.when(step == 0)(s_start)

        @pl.when(step == 0)
        def _():
            ds_ref[...] = jnp.zeros_like(ds_ref)
            dal_ref[...] = jnp.zeros_like(dal_ref)
            ddt_ref[...] = jnp.zeros_like(ddt_ref)

        q, k, v, ba, al, dt = q_ref[...], k_ref[...], v_ref[...], ba_ref[...], al_ref[...], dt_ref[...]
        st, do, dst = st_ref[...], do_ref[...], ds_ref[...]
        vjps = []
        for c in range(GDN_STEP):
            sl = slice(c * CHUNK, (c + 1) * CHUNK)
            valid = (i * rows + c * CHUNK + lax.broadcasted_iota(jnp.int32, (CHUNK, 1), 0)) >= pad
            fn = functools.partial(_gdn_chunk, valid=valid)
            vjps.append(jax.vjp(fn, q[sl], k[sl], v[sl], ba[sl], al, dt, [st[c, h] for h in range(nh)])[1])
        ds = [dst[h] for h in range(nh)]
        grads = [None] * GDN_STEP
        for c in reversed(range(GDN_STEP)):
            grads[c] = vjps[c]((do[c * CHUNK:(c + 1) * CHUNK], ds))
            ds = grads[c][6]
        for j, ref in enumerate((dq_ref, dk_ref, dv_ref, dba_ref)):
            ref[...] = jnp.concatenate([gr[j] for gr in grads], axis=0)
        dal_ref[...] += sum(gr[4] for gr in grads)
        ddt_ref[...] += sum(gr[5] for gr in grads)
        for h in range(nh):
            ds_ref[h] = ds[h]
        if ns:
            pl.when(step == steps - 1)(s_finish)

    w = nh * HEAD_W
    rev = lambda c: (lambda s: (steps - 1 - s, c))
    vec = pl.BlockSpec((1, HEAD_W), lambda s: (0, 0))
    return pl.pallas_call(
        body, name=name, grid=(steps,),
        in_specs=[pl.BlockSpec((rows, w), rev(0)), pl.BlockSpec((rows, w), rev(1)), pl.BlockSpec((rows, w), rev(2)),
                  pl.BlockSpec((rows, HEAD_W), rev(AB_BA // HEAD_W)), vec, vec,
                  pl.BlockSpec((GDN_STEP, nh, HEAD_W, HEAD_W), lambda s: (steps - 1 - s, 0, 0, 0)),
                  pl.BlockSpec((rows, w), rev(0))] + [_ANY] * ns,
        out_specs=[pl.BlockSpec((rows, w), rev(0)), pl.BlockSpec((rows, w), rev(0)), pl.BlockSpec((rows, w), rev(0)),
                   pl.BlockSpec((rows, HEAD_W), rev(0)), vec, vec] + [_ANY] * ns,
        out_shape=[jax.ShapeDtypeStruct((lp, w), F32)] * 3 + [jax.ShapeDtypeStruct((lp, HEAD_W), F32)]
        + [jax.ShapeDtypeStruct((1, HEAD_W), F32)] * 2 + [jax.ShapeDtypeStruct(s.shape, s.dtype) for s in scatter],
        scratch_shapes=[pltpu.VMEM((nh, HEAD_W, HEAD_W), F32)] + (_scatter_scratch(ns) if ns else []),
        compiler_params=_cp(("arbitrary",), has_side_effects=bool(ns)),
    )(qkv, qkv, qkv, p0, alog_v, dtb_v, states, do, *scatter)


HG_LEVELS = (32, 16, 8, 4, 2, 1)
HG_GROUP = 4
HG_STEP = 3


def _hg_masks():
    import numpy as np
    c = CHUNK
    t = np.arange(c)[:, None]
    j = np.arange(c)[None, :]
    sums = (j <= t).astype(np.float32)
    pairs = [j == t]
    for m in HG_LEVELS:
        p = (t // (2 * m)) * (2 * m)
        r = p + m
        pairs.append((t >= r) & (j < r) & (j >= p))
    pairs = np.concatenate([np.kron(np.eye(HG_GROUP), p) for p in pairs], axis=0).astype(np.float32)
    return jnp.asarray(sums, BF16), jnp.asarray(sums.T, BF16), jnp.asarray(pairs, F32)


def _hg_level_row(b, m):
    c, w = b.shape
    if m >= 8:
        return jnp.concatenate([jnp.broadcast_to(b[p + m:p + m + 1], (2 * m, w)) for p in range(0, c, 2 * m)], axis=0)
    tiles = b.reshape(c // 8, 8, w)
    sub = lax.broadcasted_iota(jnp.int32, (1, 8, 1), 1)
    out = None
    for r0 in range(m, 8, 2 * m):
        cand = jnp.broadcast_to(tiles[:, r0:r0 + 1, :], tiles.shape)
        out = cand if out is None else jnp.where(sub >= r0 - m, cand, out)
    return out.reshape(c, w)


def _split3(x):
    hi = x.astype(BF16)
    r1 = x - hi.astype(F32)
    mid = r1.astype(BF16)
    return hi, mid, (r1 - mid.astype(F32)).astype(BF16)


def _dot3_raw(a, b, dims):
    ah, am, _ = _split3(a)
    bh, bm, _ = _split3(b)
    return _dot(ah, bh, dims) + (_dot(ah, bm, dims) + _dot(am, bh, dims))


@functools.partial(jax.custom_vjp, nondiff_argnums=(2,))
def _dot3(a, b, dims):
    return _dot3_raw(a, b, dims)


def _dot3_fwd(a, b, dims):
    return _dot3_raw(a, b, dims), (a, b)


def _dot3_bwd(dims, res, g):
    a, b = res
    if dims == NN:
        return _dot3_raw(g, b, NT), _dot3_raw(a, g, TN)
    return _dot3_raw(g, b, NN), _dot3_raw(g, a, TN)


_dot3.defvjp(_dot3_fwd, _dot3_bwd)


def _mask_mm_raw(m, x):
    return sum(_dot(m, part, NN) for part in _split3(x))


@jax.custom_vjp
def _mask_mm(m, mt, x):
    return _mask_mm_raw(m, x)


def _mask_mm_fwd(m, mt, x):
    return _mask_mm_raw(m, x), (m, mt)


def _mask_mm_bwd(res, g):
    m, mt = res
    return jnp.zeros_like(m), jnp.zeros_like(mt), _mask_mm_raw(mt, g)


_mask_mm.defvjp(_mask_mm_fwd, _mask_mm_bwd)


def _hg_chunk(qr, fr, ir, lb, states, valid, sums, sums_t, pairs):
    nh = HG_GROUP
    c = qr.shape[0]
    r = nh * c
    fg = lb + (1.0 - lb) * _sigmoid(fr)
    logf = jnp.where(valid, jnp.log(fg), 0.0)
    k = jnp.where(valid, 1.0 - fg, 0.0)
    qs = jnp.where(valid, _silu(qr), 0.0)
    v = jnp.where(valid, ir, 0.0)
    b = _mask_mm(sums, sums_t, logf)
    mask = lambda n: pairs[n * r:(n + 1) * r]
    stack = lambda x: _heads_to_rows(x, nh)
    a = mask(0) * _bdot(stack(qs), stack(k), NT)
    for lvl, m in enumerate(HG_LEVELS):
        d = b - _hg_level_row(b, m)
        a = a + mask(1 + lvl) * _bdot(stack(qs * jnp.exp(jnp.minimum(d, 0.0))),
                                      stack(k * jnp.exp(jnp.minimum(-d, 0.0))), NT)
    av = _bdot(a, stack(v), NN)
    eb = jnp.exp(b)
    qe, kd = qs * eb, k * jnp.exp(b[c - 1:c] - b)
    outs, new_states = [], []
    for h in range(nh):
        cs = slice(h * HEAD_W, (h + 1) * HEAD_W)
        outs.append(_bdot(qe[:, cs], states[h], NT) + av[h * c:(h + 1) * c])
        new_states.append(states[h] * eb[c - 1:c, cs] + _bdot(v[:, cs], kd[:, cs], TN))
    return jnp.concatenate(outs, axis=1), new_states


def _hg_fwd(p1, lb, pad, *, name, gather=None):
    lp = p1.shape[0]
    n = lp // CHUNK
    nh = HG_HEADS
    g_srcs, g_dtypes = gather if gather is not None else ([], [])
    ng_arr = len(g_srcs)

    def body(q_ref, f_ref, i_ref, lb_ref, sums_ref, sums_t_ref, pairs_ref, *rest):
        g_ins, (o_ref, st_ref) = rest[:ng_arr], rest[ng_arr:ng_arr + 2]
        g_outs, s_ref, g_scratch = rest[ng_arr + 2:2 * ng_arr + 2], rest[2 * ng_arr + 2], rest[2 * ng_arr + 3:]
        i = pl.program_id(1)
        if ng_arr:
            g_start, g_forward, g_finish = _gather_phases(g_ins, g_outs, g_scratch[:ng_arr], *g_scratch[ng_arr:],
                                                          g_dtypes)
            last_group = pl.program_id(0) == ngrp - 1
            pl.when((pl.program_id(0) == 0) & (i == 0))(g_start)
            pl.when(last_group & (i == 0))(g_forward)

        @pl.when(i == 0)
        def _():
            s_ref[...] = jnp.zeros_like(s_ref)

        s = s_ref[...]
        s = [s[h] for h in range(grp)]
        q, f, iv, lbv = q_ref[...], f_ref[...], i_ref[...], lb_ref[...]
        masks_v = (sums_ref[...], sums_t_ref[...], pairs_ref[...])
        outs = []
        for c in range(HG_STEP):
            sl = slice(c * CHUNK, (c + 1) * CHUNK)
            valid = (i * rows + c * CHUNK + lax.broadcasted_iota(jnp.int32, (CHUNK, 1), 0)) >= pad
            for h in range(grp):
                st_ref[h, c] = s[h]
            o, s = _hg_chunk(q[sl], f[sl], iv[sl], lbv, s, valid, *masks_v)
            outs.append(o)
        o_ref[...] = jnp.concatenate(outs, axis=0)
        for h in range(grp):
            s_ref[h] = s[h]
        if ng_arr:
            pl.when(last_group & (i == steps - 1))(g_finish)

    masks = _hg_masks()
    grp, ngrp, gw = HG_GROUP, nh // HG_GROUP, HG_GROUP * HEAD_W
    assert n % HG_STEP == 0
    steps, rows = n // HG_STEP, HG_STEP * CHUNK
    blk = lambda off: pl.BlockSpec((rows, gw), lambda h, i: (i, off + h))
    const = lambda a: pl.BlockSpec(a.shape, lambda h, i: (0, 0))
    return pl.pallas_call(
        body, name=name, grid=(ngrp, steps),
        in_specs=[blk(0), blk(ngrp), blk(2 * ngrp), pl.BlockSpec((1, gw), lambda h, i: (0, h))]
        + [const(a) for a in masks] + [pl.BlockSpec(memory_space=pltpu.VMEM)] * ng_arr,
        out_specs=[blk(0), pl.BlockSpec((grp, HG_STEP, HEAD_W, HEAD_W), lambda h, i: (h, i, 0, 0))] + [_ANY] * ng_arr,
        out_shape=[jax.ShapeDtypeStruct((lp, nh * HEAD_W), F32), jax.ShapeDtypeStruct((nh, n, HEAD_W, HEAD_W), F32)]
        + _gather_out_shapes(g_srcs, g_dtypes),
        scratch_shapes=[pltpu.VMEM((grp, HEAD_W, HEAD_W), F32)] + (_gather_scratch(g_srcs, g_dtypes) if ng_arr else []),
        compiler_params=_cp(("arbitrary", "arbitrary"), has_side_effects=bool(ng_arr)),
    )(p1, p1, p1, lb, *masks, *g_srcs)


def _hg_bwd(p1, lb, states, do, pad, *, name, scatter=()):
    lp = p1.shape[0]
    n = lp // CHUNK
    nh = HG_HEADS
    ns = len(scatter)

    def body(q_ref, f_ref, i_ref, lb_ref, st_ref, do_ref, sums_ref, sums_t_ref, pairs_ref, *rest):
        s_ins, (dq_ref, df_ref, di_ref, dlb_ref) = rest[:ns], rest[ns:ns + 4]
        s_outs, ds_ref, s_sems = rest[ns + 4:2 * ns + 4], rest[2 * ns + 4], rest[2 * ns + 5:]
        step = pl.program_id(1)
        i = steps - 1 - step
        if ns:
            s_start, s_finish = _scatter_phases(s_ins, s_outs, *s_sems)
            pl.when((pl.program_id(0) == 0) & (step == 0))(s_start)

        @pl.when(step == 0)
        def _():
            ds_ref[...] = jnp.zeros_like(ds_ref)
            dlb_ref[...] = jnp.zeros_like(dlb_ref)

        q, f, iv, lbv, st, do, dst = q_ref[...], f_ref[...], i_ref[...], lb_ref[...], st_ref[...], do_ref[...], ds_ref[...]
        masks_v = dict(sums=sums_ref[...], sums_t=sums_t_ref[...], pairs=pairs_ref[...])
        vjps = []
        for c in range(HG_STEP):
            sl = slice(c * CHUNK, (c + 1) * CHUNK)
            valid = (i * rows + c * CHUNK + lax.broadcasted_iota(jnp.int32, (CHUNK, 1), 0)) >= pad
            fn = functools.partial(_hg_chunk, valid=valid, **masks_v)
            vjps.append(jax.vjp(fn, q[sl], f[sl], iv[sl], lbv, [st[h, c] for h in range(grp)])[1])
        ds = [dst[h] for h in range(grp)]
        grads = [None] * HG_STEP
        for c in reversed(range(HG_STEP)):
            grads[c] = vjps[c]((do[c * CHUNK:(c + 1) * CHUNK], ds))
            ds = grads[c][4]
        for j, ref in enumerate((dq_ref, df_ref, di_ref)):
            ref[...] = jnp.concatenate([gr[j] for gr in grads], axis=0)
        dlb_ref[...] += sum(gr[3] for gr in grads)
        for h in range(grp):
            ds_ref[h] = ds[h]
        if ns:
            pl.when((pl.program_id(0) == ngrp - 1) & (step == steps - 1))(s_finish)

    masks = _hg_masks()
    grp, ngrp, gw = HG_GROUP, nh // HG_GROUP, HG_GROUP * HEAD_W
    assert n % HG_STEP == 0
    steps, rows = n // HG_STEP, HG_STEP * CHUNK
    blk = lambda off: pl.BlockSpec((rows, gw), lambda h, s: (steps - 1 - s, off + h))
    const = lambda a: pl.BlockSpec(a.shape, lambda h, s: (0, 0))
    w = nh * HEAD_W
    return pl.pallas_call(
        body, name=name, grid=(ngrp, steps),
        in_specs=[blk(0), blk(ngrp), blk(2 * ngrp), pl.BlockSpec((1, gw), lambda h, s: (0, h)),
                  pl.BlockSpec((grp, HG_STEP, HEAD_W, HEAD_W), lambda h, s: (h, steps - 1 - s, 0, 0)), blk(0)]
        + [const(a) for a in masks] + [_ANY] * ns,
        out_specs=[blk(0), blk(0), blk(0), pl.BlockSpec((1, gw), lambda h, s: (0, h))] + [_ANY] * ns,
        out_shape=[jax.ShapeDtypeStruct((lp, w), F32)] * 3 + [jax.ShapeDtypeStruct((1, w), F32)]
        + [jax.ShapeDtypeStruct(s.shape, s.dtype) for s in scatter],
        scratch_shapes=[pltpu.VMEM((grp, HEAD_W, HEAD_W), F32)] + (_scatter_scratch(ns) if ns else []),
        compiler_params=_cp(("arbitrary", "arbitrary"), has_side_effects=bool(ns)),
    )(p1, p1, p1, lb, states, do, *masks, *scatter)


SB_GROUP = 4
SB_FAR = -110.0


def _sb_cat(kind, first_key=0):
    r = lax.broadcasted_iota(jnp.int32, (SB_BLOCK, 2 * SB_BLOCK), 0)
    c = lax.broadcasted_iota(jnp.int32, (SB_BLOCK, 2 * SB_BLOCK), 1)
    tri = {"after": c < r, "incl": r <= c, "before": r < c}[kind]
    m = ((c >= SB_BLOCK) | tri) & (r >= first_key)
    return jnp.where(m, 1.0, 0.0).astype(BF16)


def _sb_cumsum(x, cat):
    return _dot(x.astype(BF16), cat, NN)


def _sb_logsig(z):
    e = jnp.exp(-jnp.abs(z))
    lse = jnp.where(e < 1e-4, e, jnp.log(1.0 + e))
    lsz = jnp.minimum(z, 0.0) - lse
    return lsz, lsz - z, e


def _sb_stack(x, scale=None):
    lane = lax.broadcasted_iota(jnp.int32, (1, HEAD_W), 1)
    if scale is not None:
        x = x * scale
    return jnp.concatenate([jnp.where(lane < SB_DH, x, 0.0), jnp.where(lane >= SB_DH, x, 0.0)], axis=0).astype(BF16)


def _sb_unstack(x):
    lane = lax.broadcasted_iota(jnp.int32, (1, HEAD_W), 1)
    return jnp.where(lane < SB_DH, x[:SB_BLOCK], x[SB_BLOCK:])


def _sb_fwd(p0, pad, *, name, gather=None):
    lp = p0.shape[0]
    nb = lp // SB_BLOCK
    npair = SB_HEADS // 2
    blk0 = AB_SB // HEAD_W
    scale = SB_DH ** -0.5
    gw = SB_GROUP * SB_BLOCK
    assert pad < SB_BLOCK
    g_srcs, g_dtypes = gather if gather is not None else ([], [])
    ng_arr = len(g_srcs)

    def body(q_ref, k_ref, v_ref, *rest):
        g_ins, (o_ref, tot_ref, nproc_ref) = rest[:ng_arr], rest[ng_arr:ng_arr + 3]
        g_outs, g_scratch = rest[ng_arr + 3:2 * ng_arr + 3], rest[2 * ng_arr + 3:]
        first_step = (pl.program_id(0) == 0) & (pl.program_id(1) == 0)
        last_pair = pl.program_id(0) == npair - 1
        if ng_arr:
            g_start, g_forward, g_finish = _gather_phases(g_ins, g_outs, g_scratch[:ng_arr], *g_scratch[ng_arr:],
                                                          g_dtypes)
            pl.when(first_step)(g_start)
            pl.when(last_pair & (pl.program_id(1) == 0))(g_forward)
        i = pl.program_id(1)
        qs = _sb_stack(q_ref[...], scale)
        qpos = i * SB_BLOCK + lax.broadcasted_iota(jnp.int32, (SB_BLOCK, 1), 0)
        qpos = jnp.concatenate([qpos, qpos], axis=0)
        cat = _sb_cat("after")
        cat0 = _sb_cat("after", pad)
        ng = i // SB_GROUP

        def group(off, nblk, first_cat, allowed, carry):
            acc, run = carry
            kg = k_ref[pl.ds(off, nblk * SB_BLOCK), :].astype(BF16)
            vg = v_ref[pl.ds(off, nblk * SB_BLOCK), :].astype(BF16)
            lsz, l1m, _ = _sb_logsig(_dot(qs, kg, NT))
            if allowed is not None:
                l1m = jnp.where(allowed, l1m, 0.0)
            args = [None] * nblk
            for g in reversed(range(nblk)):
                sl = slice(g * SB_BLOCK, (g + 1) * SB_BLOCK)
                al = _sb_cumsum(l1m[:, sl], first_cat if g == 0 else cat)
                args[g] = lsz[:, sl] + al[:, :SB_BLOCK] + run
                run = run + al[:, SB_BLOCK:]
            wgt = jnp.exp(jnp.concatenate(args, axis=1))
            if allowed is not None:
                wgt = jnp.where(allowed, wgt, 0.0)
            return acc + _dot(wgt.astype(BF16), vg, NN), run

        def below(t, carry):
            gi = ng - 1 - t
            return group(pl.multiple_of(gi * gw, gw), SB_GROUP, jnp.where(gi == 0, cat0, cat), None, carry)

        top = ng * gw

        def top_group(nblk, carry):
            off = pl.multiple_of(jnp.minimum(top, lp - nblk * SB_BLOCK), SB_BLOCK)
            kpos = off + lax.broadcasted_iota(jnp.int32, (1, nblk * SB_BLOCK), 1)
            return group(off, nblk, cat, (kpos < qpos) & (kpos >= pad) & (kpos >= top), carry)

        zero = (jnp.zeros((2 * SB_BLOCK, HEAD_W), F32), jnp.zeros((2 * SB_BLOCK, HEAD_W), F32))
        carry = lax.cond(i - ng * SB_GROUP < SB_GROUP // 2, functools.partial(top_group, SB_GROUP // 2),
                         functools.partial(top_group, SB_GROUP), zero)
        used, acc, run = lax.while_loop(lambda s: (s[0] < ng) & (jnp.max(s[2]) > SB_FAR),
                                        lambda s: (s[0] + 1, *below(s[0], (s[1], s[2]))), (jnp.int32(0), *carry))
        o_ref[...] = _sb_unstack(acc)
        tot_ref[...] = _sb_unstack(run)
        nproc_ref[pl.program_id(0), i] = used.astype(F32)
        if ng_arr:
            pl.when(last_pair & (pl.program_id(1) == nb - 1))(g_finish)

    full = lambda c0: pl.BlockSpec((lp, HEAD_W), lambda p, i: (0, c0 + p))
    out = pl.BlockSpec((SB_BLOCK, HEAD_W), lambda p, i: (i, p))
    return pl.pallas_call(
        body, name=name, grid=(npair, nb),
        in_specs=[pl.BlockSpec((SB_BLOCK, HEAD_W), lambda p, i: (i, blk0 + p)), full(blk0 + npair), full(blk0 + 2 * npair)]
        + [pl.BlockSpec(memory_space=pltpu.VMEM)] * ng_arr,
        out_specs=[out, out, pl.BlockSpec(memory_space=pltpu.SMEM)] + [_ANY] * ng_arr,
        out_shape=[jax.ShapeDtypeStruct((lp, npair * HEAD_W), F32)] * 2 + [jax.ShapeDtypeStruct((npair, nb), F32)]
        + _gather_out_shapes(g_srcs, g_dtypes),
        scratch_shapes=_gather_scratch(g_srcs, g_dtypes) if ng_arr else [],
        compiler_params=_cp(("arbitrary", "arbitrary"), has_side_effects=bool(ng_arr)),
    )(p0, p0, p0, *g_srcs)


def _sb_bwd(p0, tot, nproc, dsrc, d_blk0, pad, *, name, scatter=()):
    lp = p0.shape[0]
    nb = lp // SB_BLOCK
    npair = SB_HEADS // 2
    blk0 = AB_SB // HEAD_W
    scale = SB_DH ** -0.5
    gw = SB_GROUP * SB_BLOCK
    assert pad < SB_BLOCK
    ns = len(scatter)

    def body(q_ref, k_ref, v_ref, tot_ref, nproc_ref, do_ref, *rest):
        s_ins, (dq_ref, dkt_ref, dvt_ref) = rest[:ns], rest[ns:ns + 3]
        s_outs, s_sems = rest[ns + 3:2 * ns + 3], rest[2 * ns + 3:]
        if ns:
            s_start, s_finish = _scatter_phases(s_ins, s_outs, *s_sems)
            pl.when((pl.program_id(0) == 0) & (pl.program_id(1) == 0))(s_start)
        i = pl.program_id(1)

        @pl.when(i == 0)
        def _():
            dkt_ref[...] = jnp.zeros_like(dkt_ref)
            dvt_ref[...] = jnp.zeros_like(dvt_ref)

        qs = _sb_stack(q_ref[...], scale)
        dos = _sb_stack(do_ref[...])
        qst, dost = qs.T, dos.T
        totv = tot_ref[...]
        ones = jnp.ones((1, HEAD_W), F32)
        tots = jnp.concatenate([totv[:, 0:1] * ones, totv[:, SB_DH:SB_DH + 1] * ones], axis=0)
        qpos = i * SB_BLOCK + lax.broadcasted_iota(jnp.int32, (SB_BLOCK, 1), 0)
        qpos = jnp.concatenate([qpos, qpos], axis=0)
        incl, incl0 = _sb_cat("incl"), _sb_cat("incl", pad)
        before = _sb_cat("before")
        ng = i // SB_GROUP
        used = jnp.clip(nproc_ref[pl.program_id(0), i].astype(jnp.int32), 0, ng)

        def dscore(z, e, ev, dl1m):
            r = 1.0 / (1.0 + e)
            sg = jnp.where(z >= 0, r, e * r)
            return ev * (1.0 - sg) - dl1m * sg

        def group(off, nblk, first_incl, allowed, carry):
            dq, prun, erun = carry
            width = nblk * SB_BLOCK
            kg = k_ref[pl.ds(off, width), :].astype(BF16)
            vg = v_ref[pl.ds(off, width), :].astype(BF16)
            z = _dot(qs, kg, NT)
            lsz, l1m, e = _sb_logsig(z)
            if allowed is not None:
                l1m = jnp.where(allowed, l1m, 0.0)
            dwgt = _dot(dos, vg, NT)
            dzs = [None] * nblk
            wgts = [None] * nblk
            for g in range(nblk):
                sl = slice(g * SB_BLOCK, (g + 1) * SB_BLOCK)
                al = _sb_cumsum(l1m[:, sl], first_incl if g == 0 else incl)
                wgt = jnp.exp(jnp.minimum(lsz[:, sl] + (tots - prun - al[:, :SB_BLOCK]), 0.0))
                if allowed is not None:
                    wgt = jnp.where(allowed[:, sl], wgt, 0.0)
                prun = prun + al[:, SB_BLOCK:]
                ev = wgt * dwgt[:, sl]
                el = _sb_cumsum(ev, before)
                dzs[g] = dscore(z[:, sl], e[:, sl], ev, erun + el[:, :SB_BLOCK])
                erun = erun + el[:, SB_BLOCK:]
                wgts[g] = wgt
            dz = jnp.concatenate(dzs, axis=1)
            if allowed is not None:
                dz = jnp.where(allowed, dz, 0.0)
            dz = dz.astype(BF16)
            wg = jnp.concatenate(wgts, axis=1).astype(BF16)
            dkt_ref[:, pl.ds(off, width)] += _dot(qst, dz, NN)
            dvt_ref[:, pl.ds(off, width)] += _dot(dost, wg, NN)
            return dq + _dot(dz, kg, NN), prun, erun

        def below(gi, carry):
            return group(pl.multiple_of(gi * gw, gw), SB_GROUP, jnp.where(gi == 0, incl0, incl), None, carry)

        zero = tuple(jnp.zeros((2 * SB_BLOCK, HEAD_W), F32) for _ in range(3))
        carry = lax.fori_loop(ng - used, ng, below, zero)
        top = ng * gw

        def top_group(nblk, carry):
            off = pl.multiple_of(jnp.minimum(top, lp - nblk * SB_BLOCK), SB_BLOCK)
            kpos = off + lax.broadcasted_iota(jnp.int32, (1, nblk * SB_BLOCK), 1)
            return group(off, nblk, incl, (kpos < qpos) & (kpos >= pad) & (kpos >= top), carry)

        dq, _, _ = lax.cond(i - ng * SB_GROUP < SB_GROUP // 2, functools.partial(top_group, SB_GROUP // 2),
                            functools.partial(top_group, SB_GROUP), carry)
        dq_ref[...] = _sb_unstack(dq) * scale
        if ns:
            pl.when((pl.program_id(0) == npair - 1) & (pl.program_id(1) == nb - 1))(s_finish)

    full = lambda c0: pl.BlockSpec((lp, HEAD_W), lambda p, i: (0, c0 + p))
    qb = lambda c0: pl.BlockSpec((SB_BLOCK, HEAD_W), lambda p, i: (i, c0 + p))
    tr = pl.BlockSpec((HEAD_W, lp), lambda p, i: (p, 0))
    return pl.pallas_call(
        body, name=name, grid=(npair, nb),
        in_specs=[qb(blk0), full(blk0 + npair), full(blk0 + 2 * npair), qb(0), pl.BlockSpec(memory_space=pltpu.SMEM),
                  qb(d_blk0)] + [_ANY] * ns,
        out_specs=[qb(0), tr, tr] + [_ANY] * ns,
        out_shape=[jax.ShapeDtypeStruct((lp, npair * HEAD_W), F32)]
        + [jax.ShapeDtypeStruct((npair * HEAD_W, lp), F32)] * 2
        + [jax.ShapeDtypeStruct(s.shape, s.dtype) for s in scatter],
        scratch_shapes=_scatter_scratch(ns) if ns else [],
        compiler_params=_cp(("arbitrary", "arbitrary"), has_side_effects=bool(ns)),
    )(p0, p0, p0, tot, nproc, dsrc, *scatter)


def _local_step(h0, target, pad, wts, hooks=None):
    lp = h0.shape[0]
    tm = _row_tile(lp, 1056)
    tkl = tm
    tml = _row_tile(lp, 528)
    d = D_MODEL
    mm = _mm
    mmw = functools.partial(_mm, out_dtype=BF16)
    k_major = lambda wd: wd.transpose(1, 0, 2).reshape(wd.shape[1], -1)
    g = {}

    h0_b = h0.astype(BF16)
    p0 = mm(h0_b, wts["w_ab"], "NN", tm=tm, tn=768, tk=d, name="l0_in_proj")
    ob, sb_tot, sb_used, *gathered = _sb_fwd(p0, pad, name="sb_fwd", gather=hooks["gather_a"] if hooks else None)
    if hooks:
        wts = {**wts, **hooks["weights_a"](gathered)}
    qkv = _gdn_pre_fwd(p0, wts["conv_w"], pad, name="gdn_pre_fwd")
    oa_raw, gdn_states, *gathered = _gdn_fwd(qkv, p0, wts["alog_v"], wts["dtb_v"], pad, name="gdn_fwd",
                                             gather=hooks["gather_b"] if hooks else None)
    if hooks:
        wts = {**wts, **hooks["weights_b"](gathered)}
    rows = lambda a, n: a.reshape(N_DEV, n // N_DEV, d)
    parts = g["parts"] = {}
    oab = _gate_fwd(oa_raw, p0, AB_Z // HEAD_W, wts["ab_gn"], ob, heads=GDN_HEADS, name="gdn_gate_fwd")
    ln = lambda kind, layer: (wts[f"ln_{kind}_g"][layer], wts[f"ln_{kind}_b"][layer])
    pre_mix0, h0a, h0a_b = mm(oab, wts["w_out0"], "NN", tm=tm, tn=d, tk=d, epi="ln", c=h0, scale=DN_ALPHA,
                              ln=ln("mix", 0), name="l0_out_proj")
    u0, act0 = mm(h0a_b, k_major(wts["w1"][0]), "NN", tm=tm, tn=1024, tk=d, epi="relu2_copy", name="mlp0_up")
    pre_ffn0, h0b, h0b_b = mm(act0, wts["w2"][0], "NN", tm=tm, tn=d, tk=d, epi="ln", c=h0a, scale=DN_ALPHA,
                              ln=ln("ffn", 0), name="mlp0_down")
    p1 = mm(h0b_b, k_major(wts["w_c"]), "NN", tm=tm, tn=1024, tk=d, name="l1_in_proj")
    oc_raw, hg_states, *gathered = _hg_fwd(p1, wts["lb"], pad, name="hg_fwd",
                                           gather=hooks["gather_c"] if hooks else None)
    if hooks:
        third = hooks["weights_c"](gathered)
        wts = {**wts, "w1": wts["w1"] + third["w1"], "w2": wts["w2"] + third["w2"]}
    oc = _gate_fwd(oc_raw, p1, 3 * HG_HEADS, wts["c_gn"], oc_raw, heads=HG_HEADS, name="hg_gate_fwd")
    pre_mix1, h1a, h1a_b = mm(oc, wts["w_out1"], "NN", tm=tm, tn=d, tk=d, epi="ln", c=h0b, scale=DN_ALPHA,
                              ln=ln("mix", 1), name="l1_out_proj")
    u1, act1 = mm(h1a_b, k_major(wts["w1"][1]), "NN", tm=tm, tn=1024, tk=d, epi="relu2_copy", name="mlp1_up")
    pre_ffn1, h1b, _ = mm(act1, wts["w2"][1], "NN", tm=tm, tn=d, tk=d, epi="ln", c=h1a, scale=DN_ALPHA,
                          ln=ln("ffn", 1), name="mlp1_down")
    dy, loss_vec = _loss_head(h1b, target, name="loss_head")

    def mlp_bwd(layer, h_in_b, u, act, dpre, dpre_b, pre_mix):
        du = mm(dpre_b, wts["w2"][layer], "NT", tm=tm, tn=1024, tk=d, epi="relu2grad", c=u, out_dtype=BF16,
                name=f"mlp{layer}_d_hidden")
        dw2 = mmw(act, dpre_b, "TN", tm=1024, tn=1024, tk=tkl, name=f"mlp{layer}_dw2")
        dw1 = mmw(h_in_b, du, "TN", tm=1024, tn=512, tk=tkl, out_dev=True, name=f"mlp{layer}_dw1")
        return (*mm(du, k_major(wts["w1"][layer]), "NT", tm=tml, tn=1024, tk=2048, epi="ln_bwd", c=dpre, scale=DN_ALPHA,
                    ln=(pre_mix, wts["ln_mix_g"][layer]), name=f"mlp{layer}_d_in"), dw1, dw2)

    ln_ffn_dg, ln_ffn_db, ln_mix_dg, ln_mix_db, dw1s, dw2s = ([None, None] for _ in range(6))
    dpre, dpre_b, ln_ffn_dg[1], ln_ffn_db[1] = _ln_bwd(pre_ffn1, wts["ln_ffn_g"][1], dy, name="ln_ffn1_bwd")
    dpre, dpre_b, ln_mix_dg[1], ln_mix_db[1], dw1s[1], dw2s[1] = mlp_bwd(1, h1a_b, u1, act1, dpre, dpre_b, pre_mix1)
    g["c_w_out"] = mmw(oc, dpre_b, "TN", tm=1024, tn=1024, tk=tkl, name="l1_dw_out")
    doc = mm(dpre_b, wts["w_out1"], "NT", tm=tm, tn=1024, tk=d, name="l1_d_gate")
    doc_raw, dz1, g["c_gn"] = _gate_bwd(oc_raw, p1, 3 * HG_HEADS, wts["c_gn"], doc, heads=HG_HEADS, name="hg_gate_bwd")
    ready = [dw1s[1], rows(dw2s[1], D_FF), rows(g["c_w_out"], d)] if hooks else ()
    dq1, df1, di1, g["lb"], *got = _hg_bwd(p1, wts["lb"], hg_states, doc_raw, pad, name="hg_bwd", scatter=ready)
    parts.update(zip(("mlp_w1_1", "mlp_w2_1", "c_w_out"), got))
    dp1 = jnp.concatenate([dq1, df1, di1, dz1], axis=1).astype(BF16)
    g["c_w_in"] = mmw(h0b_b, dp1, "TN", tm=1024, tn=512, tk=tkl, out_dev=True, name="l1_dw_in")
    dpre, dpre_b, ln_ffn_dg[0], ln_ffn_db[0] = mm(
        dp1, k_major(wts["w_c"]), "NT", tm=tml, tn=1024, tk=2048, epi="ln_bwd", c=dpre, scale=DN_ALPHA,
        ln=(pre_ffn0, wts["ln_ffn_g"][0]), name="l1_d_in")
    dpre, dpre_b, ln_mix_dg[0], ln_mix_db[0], dw1s[0], dw2s[0] = mlp_bwd(0, h0a_b, u0, act0, dpre, dpre_b, pre_mix0)
    g["ab_w_out"] = mmw(oab, dpre_b, "TN", tm=1024, tn=1024, tk=tkl, name="l0_dw_out")
    doab = mm(dpre_b, wts["w_out0"], "NT", tm=tm, tn=1024, tk=d, name="l0_d_gate")
    doa_raw, dz0, g["ab_gn"] = _gate_bwd(oa_raw, p0, AB_Z // HEAD_W, wts["ab_gn"], doab, heads=GDN_HEADS,
                                         name="gdn_gate_bwd")
    ready = [g["c_w_in"]] if hooks else ()
    dqb, dkb_t, dvb_t, *got = _sb_bwd(p0, sb_tot, sb_used, doab, GDN_HEADS, pad, name="sb_bwd", scatter=ready)
    parts.update(zip(("c_w_in",), got))
    dkb, dvb = dkb_t.T, dvb_t.T
    ready = [dw1s[0], rows(dw2s[0], D_FF), rows(g["ab_w_out"], d)] if hooks else ()
    dqn, dkn, dvn, dba, g["alog_v"], g["dtb_v"], *got = _gdn_bwd(qkv, p0, wts["alog_v"], wts["dtb_v"], gdn_states,
                                                                 doa_raw, pad, name="gdn_bwd", scatter=ready)
    parts.update(zip(("mlp_w1_0", "mlp_w2_0", "ab_w_out"), got))
    dconv_in, g["conv_w"] = _gdn_pre_bwd(p0, wts["conv_w"], jnp.concatenate([dqn, dkn, dvn], axis=1), pad,
                                         name="gdn_pre_bwd")
    dp0 = jnp.concatenate([dconv_in, dz0, dqb, dkb, dvb, dba, jnp.zeros((lp, AB_CAT - AB_BA - HEAD_W), F32)],
                          axis=1).astype(BF16)
    g["w_ab"] = mmw(h0_b, dp0, "TN", tm=1024, tn=768, tk=tkl, name="l0_dw_in")
    last = ()
    if hooks:
        gab, ba0 = g["w_ab"], AB_Z + GDN_HEADS * HEAD_W
        gab = jnp.concatenate([gab[:, :ba0], gab[:, AB_BA:AB_BA + 2 * GDN_HEADS], gab[:, ba0:AB_BA]], axis=1)
        last = [gab.reshape(d, N_DEV, AB_IN // N_DEV).transpose(1, 0, 2)]
    res = mm(dp0, wts["w_ab"], "NT", tm=tm, tn=1024, tk=1920, epi="add", c=dpre, scale=DN_ALPHA, scatter=last,
             name="l0_d_in")
    dh0 = res[0] if last else res
    parts.update(zip(("ab_w_in",), res[1:] if last else ()))

    g["w1"], g["w2"] = dw1s, dw2s
    g["ln_mix_g"] = jnp.concatenate(ln_mix_dg, axis=0)
    g["ln_mix_b"] = jnp.concatenate(ln_mix_db, axis=0)
    g["ln_ffn_g"] = jnp.concatenate(ln_ffn_dg, axis=0)
    g["ln_ffn_b"] = jnp.concatenate(ln_ffn_db, axis=0)
    return loss_vec, dh0, g


N_CHIP = N_DEV // 2


def _place():
    x, y, c = lax.axis_index("x"), lax.axis_index("y"), lax.axis_index("c")
    return x, y, c, 2 * x + y


def _chip_dev(chip, core):
    return (chip // 2, chip % 2, core)


def _remote(src, dst, send_sem, recv_sem, dev):
    return pltpu.make_async_remote_copy(src_ref=src, dst_ref=dst, send_sem=send_sem, recv_sem=recv_sem,
                                        device_id=dev, device_id_type=pl.DeviceIdType.MESH)


_ANY = pl.BlockSpec(memory_space=pl.ANY)


def _gather(srcs, dtypes, *, name):
    n = len(srcs)

    def body(*refs):
        start, forward, finish = _gather_phases(refs[:n], refs[n:2 * n], refs[2 * n:3 * n], *refs[3 * n:], dtypes)
        start()
        forward()
        finish()

    return pl.pallas_call(
        body, name=name, in_specs=[pl.BlockSpec(memory_space=pltpu.VMEM)] * n, out_specs=[_ANY] * n,
        out_shape=_gather_out_shapes(srcs, dtypes), scratch_shapes=_gather_scratch(srcs, dtypes),
        compiler_params=_cp(has_side_effects=True),
    )(*srcs)


def _gather_out_shapes(srcs, dtypes):
    return [jax.ShapeDtypeStruct((N_DEV, *s.shape), dt) for s, dt in zip(srcs, dtypes)]


def _gather_scratch(srcs, dtypes):
    n = len(srcs)
    return [pltpu.VMEM(s.shape, dt) for s, dt in zip(srcs, dtypes)] + [
        pltpu.SemaphoreType.DMA((n, 2 * N_CHIP - 1)), pltpu.SemaphoreType.DMA((n, 2 * N_CHIP - 1)),
        pltpu.SemaphoreType.DMA((n,))]


def _gather_phases(ins, outs, stages, send_sems, recv_sems, local_sems, dtypes):
    n = len(ins)
    x, y, c, chip = _place()
    me = 2 * chip + c
    sibling = (x, y, 1 - c)

    def own(i):
        cps = [_remote(stages[i], outs[i].at[me], send_sems.at[i, 0], recv_sems.at[i, 0], sibling)]
        for j in range(1, N_CHIP):
            cps.append(_remote(stages[i], outs[i].at[me], send_sems.at[i, j], recv_sems.at[i, j],
                               _chip_dev(jnp.bitwise_xor(chip, j), c)))
        return cps

    def local(i):
        return pltpu.make_async_copy(stages[i], outs[i].at[me], local_sems.at[i])

    def passed_on(i, j):
        slot = outs[i].at[2 * jnp.bitwise_xor(chip, j) + c]
        return _remote(slot, slot, send_sems.at[i, N_CHIP - 1 + j], recv_sems.at[i, N_CHIP - 1 + j], sibling)

    def start():
        for i in range(n):
            stages[i][...] = ins[i][...].astype(dtypes[i])
            local(i).start()
            for cp in own(i):
                cp.start()

    def forward():
        for i in range(n):
            for j in range(1, N_CHIP):
                own(i)[j].wait_recv()
                passed_on(i, j).start()

    def finish():
        for i in range(n):
            own(i)[0].wait_recv()
            for j in range(1, N_CHIP):
                passed_on(i, j).wait_recv()
        for i in range(n):
            for cp in own(i):
                cp.wait_send()
            for j in range(1, N_CHIP):
                passed_on(i, j).wait_send()
            local(i).wait()

    return start, forward, finish


def _scatter_scratch(n):
    return [pltpu.SemaphoreType.DMA((n, N_DEV - 1)), pltpu.SemaphoreType.DMA((n, N_DEV - 1)),
            pltpu.SemaphoreType.DMA((n,))]


def _scatter_phases(ins, outs, send_sems, recv_sems, local_sems):
    n = len(ins)
    _, _, c, chip = _place()
    me = 2 * chip + c

    def copies():
        cps = []
        for i in range(n):
            cps.append(pltpu.make_async_copy(ins[i].at[me], outs[i].at[me], local_sems.at[i]))
            for k in range(1, N_DEV):
                peer = jnp.bitwise_xor(me, k)
                cps.append(_remote(ins[i].at[peer], outs[i].at[me], send_sems.at[i, k - 1], recv_sems.at[i, k - 1],
                                   _chip_dev(peer // 2, peer % 2)))
        return cps

    def start():
        for cp in copies():
            cp.start()

    def finish():
        for cp in copies():
            cp.wait()

    return start, finish


def _adamw(w, parts, m, v, *, name):
    r, c = w.shape
    s = parts.shape[0]
    tm = _row_tile(r, 128) if r % 8 == 0 else r
    c1 = 1.0 - ADAM_B1 ** ADAM_STEP
    c2 = 1.0 - ADAM_B2 ** ADAM_STEP

    def body(w_ref, p_ref, m_ref, v_ref, g_ref, d_ref, m2_ref, v2_ref):
        g = p_ref[0].astype(F32)
        for j in range(1, s):
            g = g + p_ref[j].astype(F32)
        m2 = ADAM_B1 * m_ref[...] + (1.0 - ADAM_B1) * g
        v2 = ADAM_B2 * v_ref[...] + (1.0 - ADAM_B2) * jnp.square(g)
        g_ref[...] = g
        m2_ref[...] = m2
        v2_ref[...] = v2
        d_ref[...] = -ADAM_LR * ((m2 / c1) / (jnp.sqrt(v2 / c2) + ADAM_EPS) + ADAM_WD * w_ref[...])

    blk = pl.BlockSpec((tm, c), lambda i: (i, 0))
    return pl.pallas_call(
        body, name=name, grid=(r // tm,),
        in_specs=[blk, pl.BlockSpec((s, tm, c), lambda i: (0, i, 0)), blk, blk], out_specs=[blk] * 4,
        out_shape=[jax.ShapeDtypeStruct((r, c), F32)] * 4, compiler_params=_cp(("parallel",)),
    )(w, parts, m, v)


_WEIGHTS = ("meta_tokens", "ab_w_in", "ab_conv_w", "ab_a_log", "ab_dt_bias", "ab_gnorm_g", "ab_w_out", "c_w_in",
            "c_lb_raw", "c_gnorm_g", "c_w_out", "ln_mix_g", "ln_mix_b", "mlp_w1", "mlp_w2", "ln_ffn_g", "ln_ffn_b")
_PACK_ROWS = (("ln_mix_g", 0), ("ln_mix_b", 2), ("ln_ffn_g", 4), ("ln_ffn_b", 6), ("c_lb_raw", 8))
_PACK_MISC_ROW = 10
_PACK_MISC = (("ab_gnorm_g", 0, 128), ("c_gnorm_g", 128, 128), ("ab_a_log", 256, GDN_HEADS), ("ab_dt_bias", 260, GDN_HEADS))
_PACK_N = 16
_SMALL_META = 16
_SMALL_CONV = 32
_SMALL_N = 40


def _pack_replicated(p):
    rows = jnp.zeros((_PACK_N, D_MODEL), F32)
    for name, r0 in _PACK_ROWS:
        rows = rows.at[r0:r0 + 2].set(p[name])
    for name, c0, width in _PACK_MISC:
        rows = rows.at[_PACK_MISC_ROW, c0:c0 + width].set(p[name].reshape(width))
    return rows


def _unpack_replicated(rows, like):
    out = {}
    for name, r0 in _PACK_ROWS:
        out[name] = rows[r0:r0 + 2]
    for name, c0, width in _PACK_MISC:
        out[name] = rows[_PACK_MISC_ROW, c0:c0 + width].reshape(like[name].shape)
    return out


def _lower_bound(c_lb_raw):
    lb_all = jnp.cumsum(jax.nn.softmax(c_lb_raw.astype(F32), axis=0), axis=0)
    return (lb_all - lb_all[0:1])[1].reshape(1, -1)


def kernel(x, meta_tokens, ab_w_in, ab_conv_w, ab_a_log, ab_dt_bias, ab_gnorm_g, ab_w_out, c_w_in, c_lb_raw, c_gnorm_g, c_w_out, ln_mix_g, ln_mix_b, mlp_w1, mlp_w2, ln_ffn_g, ln_ffn_b, loss_target, m_meta_tokens, m_ab_w_in, m_ab_conv_w, m_ab_a_log, m_ab_dt_bias, m_ab_gnorm_g, m_ab_w_out, m_c_w_in, m_c_lb_raw, m_c_gnorm_g, m_c_w_out, m_ln_mix_g, m_ln_mix_b, m_mlp_w1, m_mlp_w2, m_ln_ffn_g, m_ln_ffn_b, v_meta_tokens, v_ab_w_in, v_ab_conv_w, v_ab_a_log, v_ab_dt_bias, v_ab_gnorm_g, v_ab_w_out, v_c_w_in, v_c_lb_raw, v_c_gnorm_g, v_c_w_out, v_ln_mix_g, v_ln_mix_b, v_mlp_w1, v_mlp_w2, v_ln_ffn_g, v_ln_ffn_b):
    w = dict(zip(_WEIGHTS, (meta_tokens, ab_w_in, ab_conv_w, ab_a_log, ab_dt_bias, ab_gnorm_g, ab_w_out, c_w_in, c_lb_raw,
                            c_gnorm_g, c_w_out, ln_mix_g, ln_mix_b, mlp_w1, mlp_w2, ln_ffn_g, ln_ffn_b)))
    mom = dict(zip(_WEIGHTS, (m_meta_tokens, m_ab_w_in, m_ab_conv_w, m_ab_a_log, m_ab_dt_bias, m_ab_gnorm_g, m_ab_w_out,
                              m_c_w_in, m_c_lb_raw, m_c_gnorm_g, m_c_w_out, m_ln_mix_g, m_ln_mix_b, m_mlp_w1, m_mlp_w2,
                              m_ln_ffn_g, m_ln_ffn_b)))
    var = dict(zip(_WEIGHTS, (v_meta_tokens, v_ab_w_in, v_ab_conv_w, v_ab_a_log, v_ab_dt_bias, v_ab_gnorm_g, v_ab_w_out,
                              v_c_w_in, v_c_lb_raw, v_c_gnorm_g, v_c_w_out, v_ln_mix_g, v_ln_mix_b, v_mlp_w1, v_mlp_w2,
                              v_ln_ffn_g, v_ln_ffn_b)))
    me = 4 * lax.axis_index("x") + 2 * lax.axis_index("y") + lax.axis_index("c")
    seq = x.shape[1]
    pad = (-(N_META + seq)) % SB_BLOCK
    lp = pad + N_META + seq
    meta_w = D_MODEL // N_DEV
    conv_w_all = 2 * GDN_HEADS * HEAD_W + GDN_HEADS * HEAD_W
    conv_w_mine = conv_w_all // N_DEV

    g_meta, g_conv, g_ab_in = _gather([w["meta_tokens"], w["ab_conv_w"][0], w["ab_w_in"][0]], [F32, F32, BF16],
                                      name="gather_weights_first")
    meta_full = g_meta.transpose(1, 0, 2).reshape(N_META, D_MODEL)
    conv_full = g_conv.transpose(1, 0, 2).reshape(CONV_K, conv_w_all)
    ab_full = g_ab_in.transpose(1, 0, 2).reshape(D_MODEL, AB_IN)
    ba0 = AB_Z + 512
    w_ab = jnp.concatenate([ab_full[:, :ba0], ab_full[:, ba0 + 2 * GDN_HEADS:], ab_full[:, ba0:ba0 + 2 * GDN_HEADS],
                            jnp.zeros((D_MODEL, AB_CAT - AB_IN), BF16)], axis=1)
    vec128 = lambda p: jnp.zeros((1, HEAD_W), F32).at[0, :GDN_HEADS].set(p.reshape(GDN_HEADS))
    wts = dict(
        w_ab=w_ab, conv_w=conv_full, alog_v=vec128(w["ab_a_log"]), dtb_v=vec128(w["ab_dt_bias"]),
        ab_gn=w["ab_gnorm_g"][0], lb=_lower_bound(w["c_lb_raw"]), c_gn=w["c_gnorm_g"][0],
        ln_mix_g=w["ln_mix_g"], ln_mix_b=w["ln_mix_b"], ln_ffn_g=w["ln_ffn_g"], ln_ffn_b=w["ln_ffn_b"])

    def weights_a(gathered):
        g_ab_out, g_w1, g_w2 = gathered
        return dict(w_out0=g_ab_out.reshape(D_MODEL, D_MODEL), w1=[g_w1], w2=[g_w2.reshape(D_FF, D_MODEL)])

    def weights_b(gathered):
        g_c_in, g_c_out = gathered
        return dict(w_c=g_c_in, w_out1=g_c_out.reshape(D_MODEL, D_MODEL))

    def weights_c(gathered):
        g_w1, g_w2 = gathered
        return dict(w1=[g_w1], w2=[g_w2.reshape(D_FF, D_MODEL)])

    hooks = dict(
        gather_a=([w["ab_w_out"][0], w["mlp_w1"][0], w["mlp_w2"][0]], [BF16] * 3), weights_a=weights_a,
        gather_b=([w["c_w_in"][0], w["c_w_out"][0]], [BF16] * 2), weights_b=weights_b,
        gather_c=([w["mlp_w1"][1], w["mlp_w2"][1]], [BF16] * 2), weights_c=weights_c)

    h0 = jnp.concatenate([jnp.zeros((pad, D_MODEL), F32), meta_full, x[0]], axis=0)
    loss_vec, dh0, g = _local_step(h0, loss_target[0], pad, wts, hooks)
    loss = lax.psum(jnp.sum(loss_vec), ("x", "y", "c"))
    grad_x = dh0[lp - seq:][None]

    _, lb_vjp = jax.vjp(_lower_bound, w["c_lb_raw"])
    rep_part = _pack_replicated(dict(
        ln_mix_g=g["ln_mix_g"], ln_mix_b=g["ln_mix_b"], ln_ffn_g=g["ln_ffn_g"], ln_ffn_b=g["ln_ffn_b"],
        c_lb_raw=lb_vjp(g["lb"])[0], ab_gnorm_g=g["ab_gn"], c_gnorm_g=g["c_gn"],
        ab_a_log=g["alog_v"][0, :GDN_HEADS], ab_dt_bias=g["dtb_v"][0, :GDN_HEADS]))
    small = jnp.concatenate([rep_part, dh0[pad:pad + N_META], g["conv_w"].reshape(-1, D_MODEL),
                             jnp.zeros((_SMALL_N - _SMALL_CONV - CONV_K * conv_w_all // D_MODEL, D_MODEL), F32)], axis=0)
    (small_all,) = _gather([small], [F32], name="gather_small_grads")
    rep_out = _adamw(_pack_replicated(w), small_all[:, :_PACK_N], _pack_replicated(mom), _pack_replicated(var),
                     name="adamw_replicated")
    meta_parts = lax.dynamic_slice_in_dim(small_all[:, _SMALL_META:_SMALL_META + N_META], me * meta_w, meta_w, axis=2)
    meta_out = _adamw(w["meta_tokens"], meta_parts, mom["meta_tokens"], var["meta_tokens"], name="adamw_meta")
    conv_parts = small_all[:, _SMALL_CONV:_SMALL_CONV + CONV_K * conv_w_all // D_MODEL].reshape(N_DEV, CONV_K, conv_w_all)
    conv_parts = lax.dynamic_slice_in_dim(conv_parts, me * conv_w_mine, conv_w_mine, axis=2)
    conv_out = _adamw(w["ab_conv_w"][0], conv_parts, mom["ab_conv_w"][0], var["ab_conv_w"][0], name="adamw_conv")

    parts = g["parts"]
    big = [("ab_w_in", 0, parts["ab_w_in"]), ("ab_w_out", 0, parts["ab_w_out"]), ("mlp_w1", 0, parts["mlp_w1_0"]),
           ("mlp_w2", 0, parts["mlp_w2_0"]), ("c_w_in", 0, parts["c_w_in"]), ("c_w_out", 0, parts["c_w_out"]),
           ("mlp_w1", 1, parts["mlp_w1_1"]), ("mlp_w2", 1, parts["mlp_w2_1"])]
    big_out = {}
    for name, l, p in big:
        res = _adamw(w[name][l], p, mom[name][l], var[name][l], name=f"adamw_{name}{l}")
        big_out.setdefault(name, []).append(res)

    rep = [_unpack_replicated(r, w) for r in rep_out]
    outs = {}
    for name in _WEIGHTS:
        if name == "meta_tokens":
            outs[name] = list(meta_out)
        elif name == "ab_conv_w":
            outs[name] = [o[None] for o in conv_out]
        elif name in big_out:
            res = big_out[name]
            outs[name] = [o[None] for o in res[0]] if len(res) == 1 else [jnp.stack(pair) for pair in zip(*res)]
        else:
            outs[name] = [r[name] for r in rep]
    flat = [loss, grad_x]
    for kind in range(4):
        flat += [outs[name][kind] for name in _WEIGHTS]
    return tuple(flat)
```
